```python
import jax, jax.numpy as jnp
from jax import lax
import numpy as np

D_MODEL = 1024
BATCH = 8
SEQ = 4096
DEPTH = 2

GMLP_WIDTH = D_MODEL
GMLP_GROUPS = 8
GMLP_GROUP_DIM = GMLP_WIDTH // GMLP_GROUPS
CHUNK = 128
LRU_WIDTH = D_MODEL
LRU_HEADS = 8
LRU_HEAD_DIM = LRU_WIDTH // LRU_HEADS
CONV_WIDTH = 4
CONV_LEFT = 1
LRU_C = 8.0
N_DIRS = 2
D_FF = -(-8 * D_MODEL // (3 * 256)) * 256
N_IN = 2 * GMLP_WIDTH + 2 * LRU_WIDTH + 2 * D_MODEL
EPS = 1e-6

kernel_name = "hybrid_gmlp_rglru_encoder"


def _rmsnorm(x, g):
    x32 = x.astype(jnp.float32)
    y = x32 * lax.rsqrt(jnp.mean(x32 * x32, axis=-1, keepdims=True) + EPS)
    return (y * g.astype(jnp.float32)).astype(x.dtype)


def _layernorm(x, g, b):
    x32 = x.astype(jnp.float32)
    mu = jnp.mean(x32, axis=-1, keepdims=True)
    xc = x32 - mu
    y = xc * lax.rsqrt(jnp.mean(xc * xc, axis=-1, keepdims=True) + EPS)
    return (y * g.astype(jnp.float32) + b.astype(jnp.float32)).astype(x.dtype)


def _blockdiag(x, w, b):
    B, S, _ = x.shape
    xh = x.reshape(B, S, LRU_HEADS, LRU_HEAD_DIM)
    y = jnp.einsum('bshc,hcd->bshd', xh, w.astype(x.dtype))
    return y.reshape(B, S, LRU_WIDTH) + b.astype(x.dtype)


def _lin_combine(left, right):
    a1, b1 = left
    a2, b2 = right
    return a1 * a2, a2 * b1 + b2


def _rglru_scan(x32, w_r, b_r, w_i, b_i, lam, reverse):
    r = jax.nn.sigmoid(_blockdiag(x32, w_r, b_r))
    i = jax.nn.sigmoid(_blockdiag(x32, w_i, b_i))
    log_a = -LRU_C * r * jax.nn.softplus(-lam.astype(jnp.float32))
    a = jnp.exp(log_a)
    mult = jnp.sqrt(jnp.maximum(-jnp.expm1(2.0 * log_a), 0.0))
    bt = mult * (i * x32)
    _, h = lax.associative_scan(_lin_combine, (a, bt), axis=1, reverse=reverse)
    return h


def _gmlp_branch(zu, zv, ln_g, ln_b, w_s, b_s):
    B, S, _ = zu.shape
    u = jax.nn.gelu(zu)
    v = _layernorm(jax.nn.gelu(zv), ln_g, ln_b)
    vc = v.reshape(B, S // CHUNK, CHUNK, GMLP_GROUPS, GMLP_GROUP_DIM)
    mixed = jnp.einsum('gpq,bnqgc->bnpgc', w_s.astype(v.dtype), vc)
    mixed = mixed + b_s.T.astype(v.dtype)[None, None, :, :, None]
    return u * mixed.reshape(B, S, GMLP_WIDTH)


def _rglru_branch(zx, zg, conv_w, conv_b, w_r, b_r, w_i, b_i, lam):
    S = zx.shape[1]
    xp = jnp.pad(zx, ((0, 0), (CONV_LEFT, CONV_WIDTH - 1 - CONV_LEFT), (0, 0)))
    xc = conv_b.astype(zx.dtype) + sum(xp[:, k:k + S] * conv_w[k].astype(zx.dtype) for k in range(CONV_WIDTH))
    x32 = xc.astype(jnp.float32)
    h = (_rglru_scan(x32, w_r[0], b_r[0], w_i[0], b_i[0], lam[0], False)
         + _rglru_scan(x32, w_r[1], b_r[1], w_i[1], b_i[1], lam[1], True))
    return h.astype(zx.dtype) * jax.nn.gelu(zg)


def _fwd_setup_inputs(seed: int = 0) -> dict:
    key = jax.random.key(seed)
    ks = jax.random.split(key, 24)
    f32 = jnp.float32
    nrm = lambda k, shape, s: jax.random.normal(k, shape, f32) * s
    L = DEPTH
    x = jax.random.normal(ks[0], (BATCH, SEQ, D_MODEL), f32)
    norm1_g = 1.0 + nrm(ks[1], (L, D_MODEL), 0.05)
    w_in = nrm(ks[2], (L, D_MODEL, N_IN), D_MODEL ** -0.5)
    gmlp_ln_g = 1.0 + nrm(ks[3], (L, GMLP_WIDTH), 0.05)
    gmlp_ln_b = nrm(ks[4], (L, GMLP_WIDTH), 0.05)
    gmlp_w_s = nrm(ks[5], (L, GMLP_GROUPS, CHUNK, CHUNK), CHUNK ** -0.5)
    gmlp_b_s = 1.0 + nrm(ks[6], (L, GMLP_GROUPS, CHUNK), 0.1)
    conv_w = nrm(ks[7], (L, CONV_WIDTH, LRU_WIDTH), CONV_WIDTH ** -0.5)
    conv_b = nrm(ks[8], (L, LRU_WIDTH), 0.02)
    lru_w_r = nrm(ks[9], (L, N_DIRS, LRU_HEADS, LRU_HEAD_DIM, LRU_HEAD_DIM), LRU_HEAD_DIM ** -0.5)
    lru_b_r = nrm(ks[10], (L, N_DIRS, LRU_WIDTH), 0.02)
    lru_w_i = nrm(ks[11], (L, N_DIRS, LRU_HEADS, LRU_HEAD_DIM, LRU_HEAD_DIM), LRU_HEAD_DIM ** -0.5)
    lru_b_i = nrm(ks[12], (L, N_DIRS, LRU_WIDTH), 0.02)
    a_c = jax.random.uniform(ks[13], (L, N_DIRS, LRU_WIDTH), f32, 0.9, 0.999)
    a0 = a_c ** (1.0 / LRU_C)
    lru_lambda = jnp.log(a0) - jnp.log1p(-a0)
    w_out = nrm(ks[14], (L, D_MODEL, D_MODEL), D_MODEL ** -0.5)
    norm2_g = 1.0 + nrm(ks[15], (L, D_MODEL), 0.05)
    w_ffn_in = nrm(ks[16], (L, D_MODEL, 2 * D_FF), D_MODEL ** -0.5)
    w_ffn_out = nrm(ks[17], (L, D_FF, D_MODEL), D_FF ** -0.5)
    final_g = 1.0 + nrm(ks[18], (D_MODEL,), 0.05)
    return {"x": x, "norm1_g": norm1_g, "w_in": w_in, "gmlp_ln_g": gmlp_ln_g,
            "gmlp_ln_b": gmlp_ln_b, "gmlp_w_s": gmlp_w_s, "gmlp_b_s": gmlp_b_s,
            "conv_w": conv_w, "conv_b": conv_b, "lru_w_r": lru_w_r, "lru_b_r": lru_b_r,
            "lru_w_i": lru_w_i, "lru_b_i": lru_b_i, "lru_lambda": lru_lambda,
            "w_out": w_out, "norm2_g": norm2_g, "w_ffn_in": w_ffn_in,
            "w_ffn_out": w_ffn_out, "final_g": final_g}


def _fwd_reference(x, norm1_g, w_in, gmlp_ln_g, gmlp_ln_b, gmlp_w_s, gmlp_b_s, conv_w, conv_b,
              lru_w_r, lru_b_r, lru_w_i, lru_b_i, lru_lambda, w_out, norm2_g, w_ffn_in,
              w_ffn_out, final_g):
    c0 = GMLP_WIDTH
    c1 = 2 * GMLP_WIDTH
    c2 = c1 + LRU_WIDTH
    c3 = c2 + LRU_WIDTH
    c4 = c3 + D_MODEL
    for l in range(DEPTH):
        h = _rmsnorm(x, norm1_g[l])
        z = h @ w_in[l].astype(h.dtype)
        y_a = _gmlp_branch(z[..., :c0], z[..., c0:c1], gmlp_ln_g[l], gmlp_ln_b[l],
                           gmlp_w_s[l], gmlp_b_s[l])
        y_b = _rglru_branch(z[..., c1:c2], z[..., c2:c3], conv_w[l], conv_b[l],
                            lru_w_r[l], lru_b_r[l], lru_w_i[l], lru_b_i[l], lru_lambda[l])
        merged = jax.nn.sigmoid(z[..., c3:c4]) * y_a + jax.nn.sigmoid(z[..., c4:]) * y_b
        x = x + merged @ w_out[l].astype(merged.dtype)
        h = _rmsnorm(x, norm2_g[l])
        gu = h @ w_ffn_in[l].astype(h.dtype)
        ff = jax.nn.silu(gu[..., :D_FF]) * gu[..., D_FF:]
        x = x + ff @ w_ffn_out[l].astype(ff.dtype)
    return _rmsnorm(x, final_g)


import jax as _jax
import jax.numpy as _jnp

TWIN_FORMAT = 'train_step'
FWD_PARAMS = ['x', 'norm1_g', 'w_in', 'gmlp_ln_g', 'gmlp_ln_b', 'gmlp_w_s', 'gmlp_b_s', 'conv_w', 'conv_b', 'lru_w_r', 'lru_b_r', 'lru_w_i', 'lru_b_i', 'lru_lambda', 'w_out', 'norm2_g', 'w_ffn_in', 'w_ffn_out', 'final_g']
TWIN_WEIGHTS = ['norm1_g', 'w_in', 'gmlp_ln_g', 'gmlp_ln_b', 'gmlp_w_s', 'gmlp_b_s', 'conv_w', 'conv_b', 'lru_w_r', 'lru_b_r', 'lru_w_i', 'lru_b_i', 'lru_lambda', 'w_out', 'norm2_g', 'w_ffn_in', 'w_ffn_out', 'final_g']
TWIN_DIFF_INPUT = 'x'
TWIN_INPUTS = ['x', 'norm1_g', 'w_in', 'gmlp_ln_g', 'gmlp_ln_b', 'gmlp_w_s', 'gmlp_b_s', 'conv_w', 'conv_b', 'lru_w_r', 'lru_b_r', 'lru_w_i', 'lru_b_i', 'lru_lambda', 'w_out', 'norm2_g', 'w_ffn_in', 'w_ffn_out', 'final_g', 'loss_target', 'm_norm1_g', 'm_w_in', 'm_gmlp_ln_g', 'm_gmlp_ln_b', 'm_gmlp_w_s', 'm_gmlp_b_s', 'm_conv_w', 'm_conv_b', 'm_lru_w_r', 'm_lru_b_r', 'm_lru_w_i', 'm_lru_b_i', 'm_lru_lambda', 'm_w_out', 'm_norm2_g', 'm_w_ffn_in', 'm_w_ffn_out', 'm_final_g', 'v_norm1_g', 'v_w_in', 'v_gmlp_ln_g', 'v_gmlp_ln_b', 'v_gmlp_w_s', 'v_gmlp_b_s', 'v_conv_w', 'v_conv_b', 'v_lru_w_r', 'v_lru_b_r', 'v_lru_w_i', 'v_lru_b_i', 'v_lru_lambda', 'v_w_out', 'v_norm2_g', 'v_w_ffn_in', 'v_w_ffn_out', 'v_final_g']
TWIN_OUTPUTS = ['loss', 'grad_x', 'grad_norm1_g', 'grad_w_in', 'grad_gmlp_ln_g', 'grad_gmlp_ln_b', 'grad_gmlp_w_s', 'grad_gmlp_b_s', 'grad_conv_w', 'grad_conv_b', 'grad_lru_w_r', 'grad_lru_b_r', 'grad_lru_w_i', 'grad_lru_b_i', 'grad_lru_lambda', 'grad_w_out', 'grad_norm2_g', 'grad_w_ffn_in', 'grad_w_ffn_out', 'grad_final_g', 'delta_norm1_g', 'delta_w_in', 'delta_gmlp_ln_g', 'delta_gmlp_ln_b', 'delta_gmlp_w_s', 'delta_gmlp_b_s', 'delta_conv_w', 'delta_conv_b', 'delta_lru_w_r', 'delta_lru_b_r', 'delta_lru_w_i', 'delta_lru_b_i', 'delta_lru_lambda', 'delta_w_out', 'delta_norm2_g', 'delta_w_ffn_in', 'delta_w_ffn_out', 'delta_final_g', 'new_m_norm1_g', 'new_m_w_in', 'new_m_gmlp_ln_g', 'new_m_gmlp_ln_b', 'new_m_gmlp_w_s', 'new_m_gmlp_b_s', 'new_m_conv_w', 'new_m_conv_b', 'new_m_lru_w_r', 'new_m_lru_b_r', 'new_m_lru_w_i', 'new_m_lru_b_i', 'new_m_lru_lambda', 'new_m_w_out', 'new_m_norm2_g', 'new_m_w_ffn_in', 'new_m_w_ffn_out', 'new_m_final_g', 'new_v_norm1_g', 'new_v_w_in', 'new_v_gmlp_ln_g', 'new_v_gmlp_ln_b', 'new_v_gmlp_w_s', 'new_v_gmlp_b_s', 'new_v_conv_w', 'new_v_conv_b', 'new_v_lru_w_r', 'new_v_lru_b_r', 'new_v_lru_w_i', 'new_v_lru_b_i', 'new_v_lru_lambda', 'new_v_w_out', 'new_v_norm2_g', 'new_v_w_ffn_in', 'new_v_w_ffn_out', 'new_v_final_g']
TWIN_LEAF_KINDS = {'loss': 'loss', 'grad_x': 'grad_x', 'grad_norm1_g': 'grad_w', 'grad_w_in': 'grad_w', 'grad_gmlp_ln_g': 'grad_w', 'grad_gmlp_ln_b': 'grad_w', 'grad_gmlp_w_s': 'grad_w', 'grad_gmlp_b_s': 'grad_w', 'grad_conv_w': 'grad_w', 'grad_conv_b': 'grad_w', 'grad_lru_w_r': 'grad_w', 'grad_lru_b_r': 'grad_w', 'grad_lru_w_i': 'grad_w', 'grad_lru_b_i': 'grad_w', 'grad_lru_lambda': 'grad_w', 'grad_w_out': 'grad_w', 'grad_norm2_g': 'grad_w', 'grad_w_ffn_in': 'grad_w', 'grad_w_ffn_out': 'grad_w', 'grad_final_g': 'grad_w', 'delta_norm1_g': 'delta_w', 'delta_w_in': 'delta_w', 'delta_gmlp_ln_g': 'delta_w', 'delta_gmlp_ln_b': 'delta_w', 'delta_gmlp_w_s': 'delta_w', 'delta_gmlp_b_s': 'delta_w', 'delta_conv_w': 'delta_w', 'delta_conv_b': 'delta_w', 'delta_lru_w_r': 'delta_w', 'delta_lru_b_r': 'delta_w', 'delta_lru_w_i': 'delta_w', 'delta_lru_b_i': 'delta_w', 'delta_lru_lambda': 'delta_w', 'delta_w_out': 'delta_w', 'delta_norm2_g': 'delta_w', 'delta_w_ffn_in': 'delta_w', 'delta_w_ffn_out': 'delta_w', 'delta_final_g': 'delta_w', 'new_m_norm1_g': 'new_m', 'new_m_w_in': 'new_m', 'new_m_gmlp_ln_g': 'new_m', 'new_m_gmlp_ln_b': 'new_m', 'new_m_gmlp_w_s': 'new_m', 'new_m_gmlp_b_s': 'new_m', 'new_m_conv_w': 'new_m', 'new_m_conv_b': 'new_m', 'new_m_lru_w_r': 'new_m', 'new_m_lru_b_r': 'new_m', 'new_m_lru_w_i': 'new_m', 'new_m_lru_b_i': 'new_m', 'new_m_lru_lambda': 'new_m', 'new_m_w_out': 'new_m', 'new_m_norm2_g': 'new_m', 'new_m_w_ffn_in': 'new_m', 'new_m_w_ffn_out': 'new_m', 'new_m_final_g': 'new_m', 'new_v_norm1_g': 'new_v', 'new_v_w_in': 'new_v', 'new_v_gmlp_ln_g': 'new_v', 'new_v_gmlp_ln_b': 'new_v', 'new_v_gmlp_w_s': 'new_v', 'new_v_gmlp_b_s': 'new_v', 'new_v_conv_w': 'new_v', 'new_v_conv_b': 'new_v', 'new_v_lru_w_r': 'new_v', 'new_v_lru_b_r': 'new_v', 'new_v_lru_w_i': 'new_v', 'new_v_lru_b_i': 'new_v', 'new_v_lru_lambda': 'new_v', 'new_v_w_out': 'new_v', 'new_v_norm2_g': 'new_v', 'new_v_w_ffn_in': 'new_v', 'new_v_w_ffn_out': 'new_v', 'new_v_final_g': 'new_v'}


def _forward(args):
    return _fwd_reference(*[args[k] for k in FWD_PARAMS])


def _output_shape():
    out = _jax.eval_shape(lambda: _forward(_fwd_setup_inputs(0)))
    return out.shape, out.dtype

N_MICROBATCH = 1
ADAM_LR = 0.001
ADAM_B1 = 0.9
ADAM_B2 = 0.999
ADAM_EPS = 1e-08
ADAM_WD = 0.01
ADAM_STEP = 10
PER_EXAMPLE_BATCH_AXIS = {'x': 0, 'loss_target': 0}
SHARED_INPUTS = []
_WEIGHT_DTYPES = {'norm1_g': _jnp.float32, 'w_in': _jnp.float32, 'gmlp_ln_g': _jnp.float32, 'gmlp_ln_b': _jnp.float32, 'gmlp_w_s': _jnp.float32, 'gmlp_b_s': _jnp.float32, 'conv_w': _jnp.float32, 'conv_b': _jnp.float32, 'lru_w_r': _jnp.float32, 'lru_b_r': _jnp.float32, 'lru_w_i': _jnp.float32, 'lru_b_i': _jnp.float32, 'lru_lambda': _jnp.float32, 'w_out': _jnp.float32, 'norm2_g': _jnp.float32, 'w_ffn_in': _jnp.float32, 'w_ffn_out': _jnp.float32, 'final_g': _jnp.float32}
MOMENT_SCALE = {'norm1_g': 1.703977e-01, 'w_in': 6.809453e-02, 'gmlp_ln_g': 5.984364e-02, 'gmlp_ln_b': 5.771352e-02, 'gmlp_w_s': 5.939944e-02, 'gmlp_b_s': 5.807000e-02, 'conv_w': 9.473347e-02, 'conv_b': 1.304925e+00, 'lru_w_r': 1.744977e-02, 'lru_b_r': 1.504813e-02, 'lru_w_i': 3.164618e-02, 'lru_b_i': 1.994414e-02, 'lru_lambda': 3.040608e-02, 'w_out': 1.354692e-01, 'norm2_g': 1.227172e-01, 'w_ffn_in': 5.056061e-02, 'w_ffn_out': 8.247961e-02, 'final_g': 3.210363e+01}


def _to_microbatches(a, axis):
    t = _jnp.moveaxis(a, axis, 0)
    t = t.reshape((N_MICROBATCH, t.shape[0] // N_MICROBATCH) + t.shape[1:])
    return _jnp.moveaxis(t, 1, axis + 1)


def setup_inputs(seed: int = 0) -> dict:
    inp = _fwd_setup_inputs(seed)
    key = _jax.random.fold_in(_jax.random.key(seed), 7919)
    shape, _ = _output_shape()
    out = dict(inp)
    out["loss_target"] = _jax.random.normal(_jax.random.fold_in(key, 0), shape, _jnp.float32)
    for i, name in enumerate(TWIN_WEIGHTS):
        w = inp[name].astype(_jnp.float32)
        if MOMENT_SCALE is None:
            s = _jnp.sqrt(_jnp.mean(_jnp.square(w)) + 1e-30)
        else:
            s = MOMENT_SCALE[name]
        km, kv = _jax.random.split(_jax.random.fold_in(key, i + 1))
        out[name] = w
        out["m_" + name] = s * _jax.random.normal(km, w.shape, _jnp.float32)
        out["v_" + name] = (s * s) * _jax.random.uniform(kv, w.shape, _jnp.float32, 0.5, 1.5)
    if N_MICROBATCH > 1:
        for name, axis in PER_EXAMPLE_BATCH_AXIS.items():
            out[name] = _to_microbatches(out[name], axis)
    return {'x': out['x'], 'norm1_g': out['norm1_g'], 'w_in': out['w_in'], 'gmlp_ln_g': out['gmlp_ln_g'], 'gmlp_ln_b': out['gmlp_ln_b'], 'gmlp_w_s': out['gmlp_w_s'], 'gmlp_b_s': out['gmlp_b_s'], 'conv_w': out['conv_w'], 'conv_b': out['conv_b'], 'lru_w_r': out['lru_w_r'], 'lru_b_r': out['lru_b_r'], 'lru_w_i': out['lru_w_i'], 'lru_b_i': out['lru_b_i'], 'lru_lambda': out['lru_lambda'], 'w_out': out['w_out'], 'norm2_g': out['norm2_g'], 'w_ffn_in': out['w_ffn_in'], 'w_ffn_out': out['w_ffn_out'], 'final_g': out['final_g'], 'loss_target': out['loss_target'], 'm_norm1_g': out['m_norm1_g'], 'm_w_in': out['m_w_in'], 'm_gmlp_ln_g': out['m_gmlp_ln_g'], 'm_gmlp_ln_b': out['m_gmlp_ln_b'], 'm_gmlp_w_s': out['m_gmlp_w_s'], 'm_gmlp_b_s': out['m_gmlp_b_s'], 'm_conv_w': out['m_conv_w'], 'm_conv_b': out['m_conv_b'], 'm_lru_w_r': out['m_lru_w_r'], 'm_lru_b_r': out['m_lru_b_r'], 'm_lru_w_i': out['m_lru_w_i'], 'm_lru_b_i': out['m_lru_b_i'], 'm_lru_lambda': out['m_lru_lambda'], 'm_w_out': out['m_w_out'], 'm_norm2_g': out['m_norm2_g'], 'm_w_ffn_in': out['m_w_ffn_in'], 'm_w_ffn_out': out['m_w_ffn_out'], 'm_final_g': out['m_final_g'], 'v_norm1_g': out['v_norm1_g'], 'v_w_in': out['v_w_in'], 'v_gmlp_ln_g': out['v_gmlp_ln_g'], 'v_gmlp_ln_b': out['v_gmlp_ln_b'], 'v_gmlp_w_s': out['v_gmlp_w_s'], 'v_gmlp_b_s': out['v_gmlp_b_s'], 'v_conv_w': out['v_conv_w'], 'v_conv_b': out['v_conv_b'], 'v_lru_w_r': out['v_lru_w_r'], 'v_lru_b_r': out['v_lru_b_r'], 'v_lru_w_i': out['v_lru_w_i'], 'v_lru_b_i': out['v_lru_b_i'], 'v_lru_lambda': out['v_lru_lambda'], 'v_w_out': out['v_w_out'], 'v_norm2_g': out['v_norm2_g'], 'v_w_ffn_in': out['v_w_ffn_in'], 'v_w_ffn_out': out['v_w_ffn_out'], 'v_final_g': out['v_final_g']}


def _loss(weights, diff, rest, loss_target):
    with _jax.named_scope("forward"):
        args = {**rest, TWIN_DIFF_INPUT: diff, **{k: w.astype(_WEIGHT_DTYPES[k]) for k, w in weights.items()}}
        y = _forward(args)
    with _jax.named_scope("loss_head"):
        err = _jnp.square(y.astype(_jnp.float32) - loss_target)
        return 0.5 * _jnp.sum(_jnp.mean(err, axis=-1)) if err.ndim else 0.5 * err


def _adamw(w, g, m, v):
    m = ADAM_B1 * m + (1.0 - ADAM_B1) * g
    v = ADAM_B2 * v + (1.0 - ADAM_B2) * _jnp.square(g)
    m_hat = m / (1.0 - ADAM_B1 ** ADAM_STEP)
    v_hat = v / (1.0 - ADAM_B2 ** ADAM_STEP)
    delta = -ADAM_LR * (m_hat / (_jnp.sqrt(v_hat) + ADAM_EPS) + ADAM_WD * w)
    return delta, m, v


def reference(x, norm1_g, w_in, gmlp_ln_g, gmlp_ln_b, gmlp_w_s, gmlp_b_s, conv_w, conv_b, lru_w_r, lru_b_r, lru_w_i, lru_b_i, lru_lambda, w_out, norm2_g, w_ffn_in, w_ffn_out, final_g, loss_target, m_norm1_g, m_w_in, m_gmlp_ln_g, m_gmlp_ln_b, m_gmlp_w_s, m_gmlp_b_s, m_conv_w, m_conv_b, m_lru_w_r, m_lru_b_r, m_lru_w_i, m_lru_b_i, m_lru_lambda, m_w_out, m_norm2_g, m_w_ffn_in, m_w_ffn_out, m_final_g, v_norm1_g, v_w_in, v_gmlp_ln_g, v_gmlp_ln_b, v_gmlp_w_s, v_gmlp_b_s, v_conv_w, v_conv_b, v_lru_w_r, v_lru_b_r, v_lru_w_i, v_lru_b_i, v_lru_lambda, v_w_out, v_norm2_g, v_w_ffn_in, v_w_ffn_out, v_final_g):
    given = dict(x=x, norm1_g=norm1_g, w_in=w_in, gmlp_ln_g=gmlp_ln_g, gmlp_ln_b=gmlp_ln_b, gmlp_w_s=gmlp_w_s, gmlp_b_s=gmlp_b_s, conv_w=conv_w, conv_b=conv_b, lru_w_r=lru_w_r, lru_b_r=lru_b_r, lru_w_i=lru_w_i, lru_b_i=lru_b_i, lru_lambda=lru_lambda, w_out=w_out, norm2_g=norm2_g, w_ffn_in=w_ffn_in, w_ffn_out=w_ffn_out, final_g=final_g, loss_target=loss_target, m_norm1_g=m_norm1_g, m_w_in=m_w_in, m_gmlp_ln_g=m_gmlp_ln_g, m_gmlp_ln_b=m_gmlp_ln_b, m_gmlp_w_s=m_gmlp_w_s, m_gmlp_b_s=m_gmlp_b_s, m_conv_w=m_conv_w, m_conv_b=m_conv_b, m_lru_w_r=m_lru_w_r, m_lru_b_r=m_lru_b_r, m_lru_w_i=m_lru_w_i, m_lru_b_i=m_lru_b_i, m_lru_lambda=m_lru_lambda, m_w_out=m_w_out, m_norm2_g=m_norm2_g, m_w_ffn_in=m_w_ffn_in, m_w_ffn_out=m_w_ffn_out, m_final_g=m_final_g, v_norm1_g=v_norm1_g, v_w_in=v_w_in, v_gmlp_ln_g=v_gmlp_ln_g, v_gmlp_ln_b=v_gmlp_ln_b, v_gmlp_w_s=v_gmlp_w_s, v_gmlp_b_s=v_gmlp_b_s, v_conv_w=v_conv_w, v_conv_b=v_conv_b, v_lru_w_r=v_lru_w_r, v_lru_b_r=v_lru_b_r, v_lru_w_i=v_lru_w_i, v_lru_b_i=v_lru_b_i, v_lru_lambda=v_lru_lambda, v_w_out=v_w_out, v_norm2_g=v_norm2_g, v_w_ffn_in=v_w_ffn_in, v_w_ffn_out=v_w_ffn_out, v_final_g=v_final_g)
    weights = {n: given[n] for n in TWIN_WEIGHTS}
    shared = {n: given[n] for n in SHARED_INPUTS}
    per_example = {n: given[n] for n in ['x']}
    grad_fn = _jax.value_and_grad(_loss, argnums=(0, 1))

    def one_microbatch(ex, loss_target):
        ex = dict(ex)
        diff = ex.pop(TWIN_DIFF_INPUT)
        return grad_fn(weights, diff, {**shared, **ex}, loss_target)

    if N_MICROBATCH == 1:
        loss, (grad_w, grad_x) = one_microbatch(per_example, given["loss_target"])
    else:
        def body(carry, xs):
            loss_sum, grad_sum = carry
            l_k, (gw_k, gx_k) = one_microbatch(xs[0], xs[1])
            with _jax.named_scope("update"):
                return (loss_sum + l_k, _jax.tree.map(_jnp.add, grad_sum, gw_k)), gx_k

        init = (_jnp.zeros((), _jnp.float32), _jax.tree.map(_jnp.zeros_like, weights))
        (loss, grad_w), grad_x = _jax.lax.scan(body, init, (per_example, given["loss_target"]))
    with _jax.named_scope("update"):
        delta_w, new_m, new_v = {}, {}, {}
        for n in TWIN_WEIGHTS:
            delta_w[n], new_m[n], new_v[n] = _adamw(weights[n], grad_w[n], given["m_" + n], given["v_" + n])
    return (loss, grad_x, *[grad_w[n] for n in TWIN_WEIGHTS], *[delta_w[n] for n in TWIN_WEIGHTS],
            *[new_m[n] for n in TWIN_WEIGHTS], *[new_v[n] for n in TWIN_WEIGHTS])
```

```python
import jax
import jax.numpy as jnp
from jax import lax
from jax.experimental import pallas as pl
from jax.experimental.pallas import tpu as pltpu

F32 = jnp.float32
BF16 = jnp.bfloat16
SDS = jax.ShapeDtypeStruct

D = 1024
N_IN = 6 * D
D_FF = 2816
N_DEV = 8
IN_BLK = N_IN // N_DEV
FF_BLK = 2 * D_FF // N_DEV
HEADS = 8
HD = 128
EPS = 1e-6
LRU_C = 8.0
MESH_AXES = ("x", "y", "c")

ADAM_LR = 0.001
ADAM_B1 = 0.9
ADAM_B2 = 0.999
ADAM_EPS = 1e-08
ADAM_WD = 0.01
ADAM_STEP = 10

VMEM_LIMIT = 48 * 2**20


def _cp(*sem):
    return pltpu.CompilerParams(dimension_semantics=sem, vmem_limit_bytes=VMEM_LIMIT)


def _row_tile(s):
    return 512 if s >= 1024 else s // 2


_GELU_C = 0.7978845608028654


def _gelu(x):
    t = jnp.tanh(_GELU_C * (x + 0.044715 * (x * x * x)))
    return 0.5 * x * (1.0 + t), t


def _gelu_grad(x, t):
    return 0.5 * (1.0 + t) + 0.5 * x * (1.0 - t * t) * (_GELU_C * (1.0 + 0.134145 * (x * x)))


def _sigmoid(x):
    return jax.nn.sigmoid(x)


def _expm1(x):
    u = jnp.exp(x)
    um1 = u - 1.0
    lu = jnp.log(u)
    return jnp.where(um1 == 0.0, x, jnp.where(um1 == -1.0, -1.0, um1 * x / jnp.where(lu == 0.0, 1.0, lu)))


def _softplus(x):
    e = jnp.exp(-jnp.abs(x))
    w = 1.0 + e
    l1p = jnp.where(w == 1.0, e, jnp.log(w) * e / jnp.where(w == 1.0, 1.0, w - 1.0))
    return jnp.maximum(x, 0.0) + l1p


def _rms_fwd(x, g):
    r = lax.rsqrt(jnp.mean(x * x, axis=-1, keepdims=True) + EPS)
    return x * r * g


def _rms_bwd(x, g, dh):
    r = lax.rsqrt(jnp.mean(x * x, axis=-1, keepdims=True) + EPS)
    xh = x * r
    dxh = dh * g
    dx = r * (dxh - xh * jnp.mean(dxh * xh, axis=-1, keepdims=True))
    dg = jnp.sum(dh * xh, axis=0, keepdims=True)
    return dx, dg


def _dot(a, b):
    return jnp.dot(a, b, preferred_element_type=F32)


def _dot_nt(a, b):
    return lax.dot_general(a, b, (((1,), (1,)), ((), ())), preferred_element_type=F32)


def _dot_tn(a, b):
    return lax.dot_general(a, b, (((0,), (0,)), ((), ())), preferred_element_type=F32)


def _taps(prev, cur, nxt, tm):
    ext = jnp.concatenate([prev, cur, nxt], axis=0)
    n = tm + 16
    sl = slice(8, 8 + tm)
    return (pltpu.roll(ext, 2, 0)[sl], pltpu.roll(ext, 1, 0)[sl], cur,
            pltpu.roll(ext, n - 1, 0)[sl], pltpu.roll(ext, n - 2, 0)[sl])


def _halo_specs(tm, s, col):
    nb8 = s // 8
    r8 = tm // 8
    return (pl.BlockSpec((8, D), lambda i: (jnp.maximum(i * r8 - 1, 0), col)),
            pl.BlockSpec((tm, D), lambda i: (i, col)),
            pl.BlockSpec((8, D), lambda i: (jnp.minimum((i + 1) * r8, nb8 - 1), col)))


def _halo_flags(nt):
    i = pl.program_id(0)
    return (i > 0).astype(F32), (i < nt - 1).astype(F32)


def _full(shape):
    nd = len(shape)
    return pl.BlockSpec(shape, lambda *_: (0,) * nd)


def _norm_inproj(x, g, w, layer, tm):
    s = x.shape[0]

    def body(x_ref, g_ref, w_ref, z_ref, h_ref, hs):
        @pl.when(pl.program_id(1) == 0)
        def _():
            h = _rms_fwd(x_ref[...], g_ref[...]).astype(BF16)
            hs[...] = h
            h_ref[...] = h
        z_ref[...] = _dot(hs[...], w_ref[...])

    return pl.pallas_call(
        body, name=f"norm_inproj_l{layer}", grid=(s // tm, N_DEV),
        in_specs=[pl.BlockSpec((tm, D), lambda i, j: (i, 0)), _full((1, D)),
                  pl.BlockSpec((None, None, D, IN_BLK), lambda i, j: (j, layer, 0, 0))],
        out_specs=[pl.BlockSpec((tm, IN_BLK), lambda i, j: (i, j)), pl.BlockSpec((tm, D), lambda i, j: (i, 0))],
        out_shape=[SDS((s, N_IN), F32), SDS((s, D), BF16)],
        scratch_shapes=[pltpu.VMEM((tm, D), BF16)],
        compiler_params=_cp("parallel", "arbitrary"))(x, g, w)


def _gmlp_fwd(z, lng, lnb, ws, bsb, layer, tm):
    s = z.shape[0]

    def body(zu_ref, zv_ref, lng_ref, lnb_ref, ws_ref, bsb_ref, ya_ref):
        u, _ = _gelu(zu_ref[...])
        gv, _ = _gelu(zv_ref[...])
        xc = gv - jnp.mean(gv, axis=-1, keepdims=True)
        rstd = lax.rsqrt(jnp.mean(xc * xc, axis=-1, keepdims=True) + EPS)
        vb = (xc * rstd * lng_ref[...] + lnb_ref[...]).astype(BF16)
        for c in range(tm // HD):
            rs = slice(c * HD, (c + 1) * HD)
            for g in range(HEADS):
                cs = slice(g * HD, (g + 1) * HD)
                mixed = _dot(ws_ref[g], vb[rs, cs]) + bsb_ref[g]
                ya_ref[rs, cs] = u[rs, cs] * mixed

    return pl.pallas_call(
        body, name=f"gmlp_fwd_l{layer}", grid=(s // tm,),
        in_specs=[pl.BlockSpec((tm, D), lambda i: (i, 0)), pl.BlockSpec((tm, D), lambda i: (i, 1)),
                  _full((1, D)), _full((1, D)), _full((HEADS, HD, HD)), _full((HEADS, HD, HD))],
        out_specs=pl.BlockSpec((tm, D), lambda i: (i, 0)),
        out_shape=SDS((s, D), F32),
        compiler_params=_cp("parallel"))(z, z, lng, lnb, ws, bsb)


def _conv(taps, cw_ref, cb_ref):
    _, m1, c0, p1, p2 = taps
    return cb_ref[...] + m1 * cw_ref[0:1, :] + c0 * cw_ref[1:2, :] + p1 * cw_ref[2:3, :] + p2 * cw_ref[3:4, :]


def _heads_dot(xb, w_ref, d):
    return jnp.concatenate([_dot(xb[:, h * HD:(h + 1) * HD], w_ref[d, h]) for h in range(HEADS)], axis=1)


def _lru_gates(xc, xb, wr_ref, wi_ref, br_ref, bi_ref, lam_ref, d):
    sp = _softplus(-lam_ref[d:d + 1, :])
    r = _sigmoid(_heads_dot(xb, wr_ref, d) + br_ref[d:d + 1, :])
    ig = _sigmoid(_heads_dot(xb, wi_ref, d) + bi_ref[d:d + 1, :])
    la = (-LRU_C) * r * sp
    a = jnp.exp(la)
    mult = jnp.sqrt(jnp.maximum(-_expm1(2.0 * la), 0.0))
    return sp, r, ig, a, mult


def _lru_gates_fwd(z, cw, cb, wr, wi, br, bi, lam, layer, tm):
    s = z.shape[0]
    nt = s // tm

    def body(zp_ref, zc_ref, zn_ref, cw_ref, cb_ref, wr_ref, wi_ref, br_ref, bi_ref, lam_ref,
             a0_ref, b0_ref, a1_ref, b1_ref):
        fp, fn = _halo_flags(nt)
        xc = _conv(_taps(zp_ref[...] * fp, zc_ref[...], zn_ref[...] * fn, tm), cw_ref, cb_ref)
        xb = xc.astype(BF16)
        for d, (a_ref, b_ref) in enumerate(((a0_ref, b0_ref), (a1_ref, b1_ref))):
            _, _, ig, a, mult = _lru_gates(xc, xb, wr_ref, wi_ref, br_ref, bi_ref, lam_ref, d)
            a_ref[...] = a
            b_ref[...] = mult * (ig * xc)

    tile = pl.BlockSpec((tm, D), lambda i: (i, 0))
    return pl.pallas_call(
        body, name=f"lru_gates_fwd_l{layer}", grid=(nt,),
        in_specs=[*_halo_specs(tm, s, 2), _full((4, D)), _full((1, D)),
                  _full((2, HEADS, HD, HD)), _full((2, HEADS, HD, HD)), _full((2, D)), _full((2, D)), _full((2, D))],
        out_specs=[tile] * 4, out_shape=[SDS((s, D), F32)] * 4,
        compiler_params=_cp("parallel"))(z, z, z, cw, cb, wr, wi, br, bi, lam)


def _scan_group(a, x, c, reverse, bwd):
    row = lax.broadcasted_iota(jnp.int32, a.shape, 0)
    b = a * x if bwd else x
    for d in (1, 2, 4):
        keep = (row < 8 - d) if reverse else (row >= d)
        sh = 8 - d if reverse else d
        a_s = jnp.where(keep, pltpu.roll(a, sh, 0), 1.0)
        b_s = jnp.where(keep, pltpu.roll(b, sh, 0), 0.0)
        b = a * b_s + b
        a = a * a_s
    h = b + a * c
    new_c = h[0:1, :] if reverse else h[7:8, :]
    if not bwd:
        return h, new_c
    if reverse:
        prev = jnp.where(row < 7, pltpu.roll(h, 7, 0), c)
    else:
        prev = jnp.where(row >= 1, pltpu.roll(h, 1, 0), c)
    return x + prev, new_c


def _lru_scan(a_f, x_f, a_r, x_r, bwd, layer):
    s = a_f.shape[0]
    ts = min(1024, s // 2)
    cb = 512
    nt = s // ts
    ng = ts // 8

    def body(af_ref, xf_ref, ar_ref, xr_ref, of_ref, or_ref, cf, cr):
        @pl.when(pl.program_id(1) == 0)
        def _():
            cf[...] = jnp.zeros_like(cf)
            cr[...] = jnp.zeros_like(cr)

        def step(j, carry):
            c_f, c_r = carry
            rf = pl.multiple_of(j * 8, 8)
            rr = pl.multiple_of((ng - 1 - j) * 8, 8)
            o, c_f = _scan_group(af_ref[pl.ds(rf, 8), :], xf_ref[pl.ds(rf, 8), :], c_f, False, bwd)
            of_ref[pl.ds(rf, 8), :] = o
            o, c_r = _scan_group(ar_ref[pl.ds(rr, 8), :], xr_ref[pl.ds(rr, 8), :], c_r, True, bwd)
            or_ref[pl.ds(rr, 8), :] = o
            return c_f, c_r

        c_f, c_r = lax.fori_loop(0, ng, step, (cf[0:1, :], cr[0:1, :]), unroll=2)
        cf[...] = jnp.broadcast_to(c_f, cf.shape)
        cr[...] = jnp.broadcast_to(c_r, cr.shape)

    fwd = pl.BlockSpec((ts, cb), lambda c, t: (t, c))
    rev = pl.BlockSpec((ts, cb), lambda c, t: (nt - 1 - t, c))
    return pl.pallas_call(
        body, name=f"lru_scan_{'bwd' if bwd else 'fwd'}_l{layer}", grid=(D // cb, nt),
        in_specs=[fwd, fwd, rev, rev], out_specs=[fwd, rev],
        out_shape=[SDS((s, D), F32)] * 2,
        scratch_shapes=[pltpu.VMEM((8, cb), F32), pltpu.VMEM((8, cb), F32)],
        compiler_params=_cp("parallel", "arbitrary"))(a_f, x_f, a_r, x_r)


def _merge_outproj(x, ya, h0, h1, z, wo, layer, tm):
    s = x.shape[0]

    def body(x_ref, ya_ref, h0_ref, h1_ref, zg_ref, za_ref, zb_ref, wo_ref, x1_ref, mg_ref):
        gg, _ = _gelu(zg_ref[...])
        yb = (h0_ref[...] + h1_ref[...]) * gg
        mb = (_sigmoid(za_ref[...]) * ya_ref[...] + _sigmoid(zb_ref[...]) * yb).astype(BF16)
        mg_ref[...] = mb
        x1_ref[...] = x_ref[...] + _dot(mb, wo_ref[...])

    tile = pl.BlockSpec((tm, D), lambda i: (i, 0))
    return pl.pallas_call(
        body, name=f"merge_outproj_l{layer}", grid=(s // tm,),
        in_specs=[tile, tile, tile, tile] + [pl.BlockSpec((tm, D), lambda i, c=c: (i, c)) for c in (3, 4, 5)]
        + [pl.BlockSpec((None, D, D), lambda i: (layer, 0, 0))],
        out_specs=[tile, tile], out_shape=[SDS((s, D), F32), SDS((s, D), BF16)],
        compiler_params=_cp("parallel"))(x, ya, h0, h1, z, z, z, wo)


def _ffn_fwd(x1, g, wfi, wfo, layer, tm):
    s = x1.shape[0]

    def body(x_ref, g_ref, wi_ref, wo_ref, x2_ref, gu_ref, h_ref, hs, acc):
        k = pl.program_id(1)

        @pl.when(k == 0)
        def _():
            h = _rms_fwd(x_ref[...], g_ref[...]).astype(BF16)
            hs[...] = h
            h_ref[...] = h
            acc[...] = jnp.zeros_like(acc)

        gate = _dot(hs[...], wi_ref[0])
        up = _dot(hs[...], wi_ref[1])
        gu_ref[0] = gate
        gu_ref[1] = up
        acc[...] += _dot((gate * _sigmoid(gate) * up).astype(BF16), wo_ref[...])

        @pl.when(k == 3)
        def _():
            x2_ref[...] = x_ref[...] + acc[...]

    return pl.pallas_call(
        body, name=f"ffn_fwd_l{layer}", grid=(s // tm, 4),
        in_specs=[pl.BlockSpec((tm, D), lambda i, k: (i, 0)), _full((1, D)),
                  pl.BlockSpec((2, None, None, D, FF_BLK), lambda i, k: (0, k, layer, 0, 0)),
                  pl.BlockSpec((None, None, FF_BLK, D), lambda i, k: (k, layer, 0, 0))],
        out_specs=[pl.BlockSpec((tm, D), lambda i, k: (i, 0)),
                   pl.BlockSpec((2, None, tm, FF_BLK), lambda i, k: (0, k, i, 0)),
                   pl.BlockSpec((tm, D), lambda i, k: (i, 0))],
        out_shape=[SDS((s, D), F32), SDS((2, 4, s, FF_BLK), F32), SDS((s, D), BF16)],
        scratch_shapes=[pltpu.VMEM((tm, D), BF16), pltpu.VMEM((tm, D), F32)],
        compiler_params=_cp("parallel", "arbitrary"))(x1, g, wfi, wfo)


def _loss_head(x, g, tgt, tm):
    s = x.shape[0]

    def body(x_ref, g_ref, t_ref, dx_ref, loss_ref, dg_ref):
        @pl.when(pl.program_id(0) == 0)
        def _():
            loss_ref[...] = jnp.zeros_like(loss_ref)
            dg_ref[...] = jnp.zeros_like(dg_ref)

        x = x_ref[...]
        gv = g_ref[...]
        e = _rms_fwd(x, gv) - t_ref[...]
        rows = jnp.sum(e * e, axis=-1, keepdims=True)
        loss_ref[...] += (0.5 / D) * jnp.sum(rows, axis=0, keepdims=True)
        dx, dg = _rms_bwd(x, gv, e * (1.0 / D))
        dx_ref[...] = dx
        dg_ref[...] += dg

    tile = pl.BlockSpec((tm, D), lambda i: (i, 0))
    return pl.pallas_call(
        body, name="loss_head", grid=(s // tm,),
        in_specs=[tile, _full((1, D)), tile],
        out_specs=[tile, _full((1, 1)), _full((1, D))],
        out_shape=[SDS((s, D), F32), SDS((1, 1), F32), SDS((1, D), F32)],
        compiler_params=_cp("arbitrary"))(x, g, tgt)


def _ffn_bwd_act(dx2, wfo, gu, layer, tm):
    s = dx2.shape[0]

    def body(dx_ref, wo_ref, gu_ref, ff_ref, dgu_ref):
        dff = _dot_nt(dx_ref[...].astype(BF16), wo_ref[...])
        gate = gu_ref[0]
        up = gu_ref[1]
        sg = _sigmoid(gate)
        sl = gate * sg
        ff_ref[...] = (sl * up).astype(BF16)
        dgu_ref[0] = (dff * up * (sg * (1.0 + gate * (1.0 - sg)))).astype(BF16)
        dgu_ref[1] = (dff * sl).astype(BF16)

    blk = pl.BlockSpec((2, None, tm, FF_BLK), lambda i, k: (0, k, i, 0))
    return pl.pallas_call(
        body, name=f"ffn_bwd_act_l{layer}", grid=(s // tm, 4),
        in_specs=[pl.BlockSpec((tm, D), lambda i, k: (i, 0)),
                  pl.BlockSpec((None, None, FF_BLK, D), lambda i, k: (k, layer, 0, 0)), blk],
        out_specs=[pl.BlockSpec((None, tm, FF_BLK), lambda i, k: (k, i, 0)), blk],
        out_shape=[SDS((4, s, FF_BLK), BF16), SDS((2, 4, s, FF_BLK), BF16)],
        compiler_params=_cp("parallel", "arbitrary"))(dx2, wfo, gu)


def _mm_nt_rms_bwd(a, a_spec, w, w_spec, nk, x, g, dres, name, tm):
    s = x.shape[0]

    def body(a_ref, w_ref, x_ref, g_ref, dres_ref, dx_ref, dg_ref, acc):
        i = pl.program_id(0)
        k = pl.program_id(1)

        @pl.when(k == 0)
        def _():
            acc[...] = jnp.zeros_like(acc)

        @pl.when(jnp.logical_and(i == 0, k == 0))
        def _():
            dg_ref[...] = jnp.zeros_like(dg_ref)

        acc[...] += _dot_nt(a_ref[...], w_ref[...])

        @pl.when(k == nk - 1)
        def _():
            dx, dg = _rms_bwd(x_ref[...], g_ref[...], acc[...])
            dx_ref[...] = dres_ref[...] + dx
            dg_ref[...] += dg

    tile = pl.BlockSpec((tm, D), lambda i, k: (i, 0))
    return pl.pallas_call(
        body, name=name, grid=(s // tm, nk),
        in_specs=[a_spec, w_spec, tile, _full((1, D)), tile],
        out_specs=[tile, _full((1, D))], out_shape=[SDS((s, D), F32), SDS((1, D), F32)],
        scratch_shapes=[pltpu.VMEM((tm, D), F32)],
        compiler_params=_cp("arbitrary", "arbitrary"))(a, w, x, g, dres)


def _mm_tn(a, a_spec, b, b_spec, nb, kb, bn, s, name):
    ts = _row_tile(s)
    ns = s // ts

    def body(a_ref, b_ref, o_ref, acc):
        t = pl.program_id(1)

        @pl.when(t == 0)
        def _():
            acc[...] = jnp.zeros_like(acc)

        acc[...] += _dot_tn(a_ref[...].astype(BF16), b_ref[...].astype(BF16))

        @pl.when(t == ns - 1)
        def _():
            o_ref[...] = acc[...].astype(BF16)

    return pl.pallas_call(
        body, name=name, grid=(nb, ns), in_specs=[a_spec(ts), b_spec(ts)],
        out_specs=pl.BlockSpec((None, kb, bn), lambda j, t: (j, 0, 0)),
        out_shape=SDS((nb, kb, bn), BF16),
        scratch_shapes=[pltpu.VMEM((kb, bn), F32)],
        compiler_params=_cp("parallel", "arbitrary"))(a, b)


def _outproj_bwd_merge(dx1, wo, ya, h0, h1, z, layer, tm):
    s = dx1.shape[0]

    def body(dx_ref, wo_ref, ya_ref, h0_ref, h1_ref, zg_ref, za_ref, zb_ref, dz_ref, dya_ref, dh_ref):
        dm = _dot_nt(dx_ref[...].astype(BF16), wo_ref[...])
        sa = _sigmoid(za_ref[...])
        sb = _sigmoid(zb_ref[...])
        zg = zg_ref[...]
        gg, tg = _gelu(zg)
        hs = h0_ref[...] + h1_ref[...]
        dyb = dm * sb
        dya_ref[...] = dm * sa
        dh_ref[...] = dyb * gg
        dz_ref[:, 0:D] = (dyb * hs * _gelu_grad(zg, tg)).astype(BF16)
        dz_ref[:, D:2 * D] = (dm * ya_ref[...] * (sa * (1.0 - sa))).astype(BF16)
        dz_ref[:, 2 * D:3 * D] = (dm * (hs * gg) * (sb * (1.0 - sb))).astype(BF16)

    tile = pl.BlockSpec((tm, D), lambda i: (i, 0))
    return pl.pallas_call(
        body, name=f"outproj_bwd_merge_l{layer}", grid=(s // tm,),
        in_specs=[tile, pl.BlockSpec((None, D, D), lambda i: (layer, 0, 0)), tile, tile, tile]
        + [pl.BlockSpec((tm, D), lambda i, c=c: (i, c)) for c in (3, 4, 5)],
        out_specs=[pl.BlockSpec((tm, 3 * D), lambda i: (i, 1)), tile, tile],
        out_shape=[SDS((s, N_IN), BF16), SDS((s, D), F32), SDS((s, D), F32)],
        compiler_params=_cp("parallel"))(dx1, wo, ya, h0, h1, z, z, z)


def _lru_gates_bwd(z, h0, h1, g0, g1, cw, cb, wr, wi, br, bi, lam, layer, tm):
    s = z.shape[0]
    nt = s // tm

    def body(zp_ref, zc_ref, zn_ref, h0p_ref, h0_ref, h1_ref, h1n_ref, g0_ref, g1_ref,
             cw_ref, cb_ref, wr_ref, wi_ref, br_ref, bi_ref, lam_ref,
             dxc_ref, dwr_ref, dwi_ref, dbr_ref, dbi_ref, dlam_ref):
        i = pl.program_id(0)
        fp, fn = _halo_flags(nt)

        @pl.when(i == 0)
        def _():
            for r in (dwr_ref, dwi_ref, dbr_ref, dbi_ref, dlam_ref):
                r[...] = jnp.zeros_like(r)

        xc = _conv(_taps(zp_ref[...] * fp, zc_ref[...], zn_ref[...] * fn, tm), cw_ref, cb_ref)
        xb = xc.astype(BF16)
        zeros8 = jnp.zeros((8, D), F32)
        h_prev = _taps(h0p_ref[...] * fp, h0_ref[...], zeros8, tm)[1]
        h_next = _taps(zeros8, h1_ref[...], h1n_ref[...] * fn, tm)[3]
        dxc = jnp.zeros((tm, D), F32)
        for d, (g_ref, hsh) in enumerate(((g0_ref, h_prev), (g1_ref, h_next))):
            sp, r, ig, a, mult = _lru_gates(xc, xb, wr_ref, wi_ref, br_ref, bi_ref, lam_ref, d)
            db = g_ref[...]
            da = db * hsh
            dmult = db * (ig * xc)
            di = db * (mult * xc)
            dxc = dxc + db * (mult * ig)
            dla = da * a - dmult * jnp.where(mult > 0.0, a * a / jnp.where(mult > 0.0, mult, 1.0), 0.0)
            dlam_ref[d:d + 1, :] += jnp.sum(dla * r, axis=0, keepdims=True) * (-LRU_C)
            dpr = dla * sp * (-LRU_C) * (r * (1.0 - r))
            dpi = di * (ig * (1.0 - ig))
            dbr_ref[d:d + 1, :] += jnp.sum(dpr, axis=0, keepdims=True)
            dbi_ref[d:d + 1, :] += jnp.sum(dpi, axis=0, keepdims=True)
            dprb = dpr.astype(BF16)
            dpib = dpi.astype(BF16)
            parts = []
            for h in range(HEADS):
                cs = slice(h * HD, (h + 1) * HD)
                dwr_ref[d, h] += _dot_tn(xb[:, cs], dprb[:, cs])
                dwi_ref[d, h] += _dot_tn(xb[:, cs], dpib[:, cs])
                parts.append(_dot_nt(dprb[:, cs], wr_ref[d, h]) + _dot_nt(dpib[:, cs], wi_ref[d, h]))
            dxc = dxc + jnp.concatenate(parts, axis=1)
        dxc_ref[...] = dxc

        @pl.when(i == nt - 1)
        def _():
            dlam_ref[...] = dlam_ref[...] * (-_sigmoid(-lam_ref[...]))

    tile = pl.BlockSpec((tm, D), lambda i: (i, 0))
    zp, zc, zn = _halo_specs(tm, s, 2)
    hp, hc, hn = _halo_specs(tm, s, 0)
    wspec = _full((2, HEADS, HD, HD))
    return pl.pallas_call(
        body, name=f"lru_gates_bwd_l{layer}", grid=(nt,),
        in_specs=[zp, zc, zn, hp, hc, hc, hn, tile, tile, _full((4, D)), _full((1, D)),
                  wspec, wspec, _full((2, D)), _full((2, D)), _full((2, D))],
        out_specs=[tile, wspec, wspec, _full((2, D)), _full((2, D)), _full((2, D))],
        out_shape=[SDS((s, D), F32), SDS((2, HEADS, HD, HD), F32), SDS((2, HEADS, HD, HD), F32),
                   SDS((2, D), F32), SDS((2, D), F32), SDS((2, D), F32)],
        compiler_params=_cp("arbitrary"))(z, z, z, h0, h0, h1, h1, g0, g1, cw, cb, wr, wi, br, bi, lam)


def _conv_bwd(dz, dxc, z, cw, layer, tm):
    s = z.shape[0]
    nt = s // tm

    def body(dz_in, dp_ref, dc_ref, dn_ref, zp_ref, zc_ref, zn_ref, cw_ref, dz_ref, dcw_ref, dcb_ref):
        del dz_in
        fp, fn = _halo_flags(nt)

        @pl.when(pl.program_id(0) == 0)
        def _():
            dcw_ref[...] = jnp.zeros_like(dcw_ref)
            dcb_ref[...] = jnp.zeros_like(dcb_ref)

        dxc = dc_ref[...]
        dm2, dm1, _, dp1, _ = _taps(dp_ref[...] * fp, dxc, dn_ref[...] * fn, tm)
        dz_ref[...] = (cw_ref[0:1, :] * dp1 + cw_ref[1:2, :] * dxc + cw_ref[2:3, :] * dm1
                       + cw_ref[3:4, :] * dm2).astype(BF16)
        _, zm1, z0, zp1, zp2 = _taps(zp_ref[...] * fp, zc_ref[...], zn_ref[...] * fn, tm)
        for k, zt in enumerate((zm1, z0, zp1, zp2)):
            dcw_ref[k:k + 1, :] += jnp.sum(dxc * zt, axis=0, keepdims=True)
        dcb_ref[...] += jnp.sum(dxc, axis=0, keepdims=True)

    return pl.pallas_call(
        body, name=f"conv_bwd_l{layer}", grid=(nt,),
        in_specs=[pl.BlockSpec(memory_space=pl.ANY), *_halo_specs(tm, s, 0), *_halo_specs(tm, s, 2), _full((4, D))],
        out_specs=[pl.BlockSpec((tm, D), lambda i: (i, 2)), _full((4, D)), _full((1, D))],
        out_shape=[SDS((s, N_IN), BF16), SDS((4, D), F32), SDS((1, D), F32)],
        input_output_aliases={0: 0},
        compiler_params=_cp("arbitrary"))(dz, dxc, dxc, dxc, z, z, z, cw)


def _gmlp_bwd(dz, z, dya, lng, lnb, ws, wst, bsb, layer, tm):
    s = z.shape[0]
    nt = s // tm

    def body(dz_in, zu_ref, zv_ref, dya_ref, lng_ref, lnb_ref, ws_ref, wst_ref, bsb_ref,
             dz_ref, dws_ref, dbs_ref, dlng_ref, dlnb_ref, du_s, dv_s, dbs_acc):
        del dz_in
        i = pl.program_id(0)

        @pl.when(i == 0)
        def _():
            for r in (dws_ref, dlng_ref, dlnb_ref, dbs_acc):
                r[...] = jnp.zeros_like(r)

        zu = zu_ref[...]
        zv = zv_ref[...]
        u, tu = _gelu(zu)
        gv, tv = _gelu(zv)
        xc = gv - jnp.mean(gv, axis=-1, keepdims=True)
        rstd = lax.rsqrt(jnp.mean(xc * xc, axis=-1, keepdims=True) + EPS)
        xh = xc * rstd
        lng_v = lng_ref[...]
        vb = (xh * lng_v + lnb_ref[...]).astype(BF16)
        dya = dya_ref[...]
        for c in range(tm // HD):
            rs = slice(c * HD, (c + 1) * HD)
            for g in range(HEADS):
                cs = slice(g * HD, (g + 1) * HD)
                vblk = vb[rs, cs]
                mixed = _dot(ws_ref[g], vblk) + bsb_ref[g]
                du_s[rs, cs] = dya[rs, cs] * mixed
                dmx = dya[rs, cs] * u[rs, cs]
                dbs_acc[g] += dmx
                dmxb = dmx.astype(BF16)
                dws_ref[g] += _dot_nt(dmxb, vblk)
                dv_s[rs, cs] = _dot(wst_ref[g], dmxb)
        dv = dv_s[...]
        dlng_ref[...] += jnp.sum(dv * xh, axis=0, keepdims=True)
        dlnb_ref[...] += jnp.sum(dv, axis=0, keepdims=True)
        dxh = dv * lng_v
        dgv = rstd * (dxh - jnp.mean(dxh, axis=-1, keepdims=True)
                      - xh * jnp.mean(dxh * xh, axis=-1, keepdims=True))
        dz_ref[:, 0:D] = (du_s[...] * _gelu_grad(zu, tu)).astype(BF16)
        dz_ref[:, D:2 * D] = (dgv * _gelu_grad(zv, tv)).astype(BF16)

        @pl.when(i == nt - 1)
        def _():
            for g in range(HEADS):
                dbs_ref[g:g + 1, :] = jnp.sum(dbs_acc[g].T, axis=0, keepdims=True)

    tile = pl.BlockSpec((tm, D), lambda i: (i, 0))
    wspec = _full((HEADS, HD, HD))
    return pl.pallas_call(
        body, name=f"gmlp_bwd_l{layer}", grid=(nt,),
        in_specs=[pl.BlockSpec(memory_space=pl.ANY), tile, pl.BlockSpec((tm, D), lambda i: (i, 1)), tile,
                  _full((1, D)), _full((1, D)), wspec, wspec, wspec],
        out_specs=[pl.BlockSpec((tm, 2 * D), lambda i: (i, 0)), wspec, _full((HEADS, HD)), _full((1, D)), _full((1, D))],
        out_shape=[SDS((s, N_IN), BF16), SDS((HEADS, HD, HD), F32), SDS((HEADS, HD), F32),
                   SDS((1, D), F32), SDS((1, D), F32)],
        scratch_shapes=[pltpu.VMEM((tm, D), F32), pltpu.VMEM((tm, D), F32), pltpu.VMEM((HEADS, HD, HD), F32)],
        input_output_aliases={0: 0},
        compiler_params=_cp("arbitrary"))(dz, z, z, dya, lng, lnb, ws, wst, bsb)


def _me():
    return lax.axis_index("x"), lax.axis_index("y"), lax.axis_index("c")


def _peer(m):
    x, y, c = _me()
    px = 1 - x if m & 4 else x
    py = 1 - y if m & 2 else y
    pc = 1 - c if m & 1 else c
    return (px, py, pc), 4 * px + 2 * py + pc


_ANY = pl.BlockSpec(memory_space=pl.ANY)
_EXCHANGE_SEMS = [pltpu.SemaphoreType.DMA((N_DEV - 1,)), pltpu.SemaphoreType.DMA((N_DEV - 1,)), pltpu.SemaphoreType.DMA(())]


def _all_gather(v, name):
    def body(v_ref, o_ref, send_sems, recv_sems, local_sem):
        x, y, c = _me()
        me = 4 * x + 2 * y + c
        local = pltpu.make_async_copy(v_ref, o_ref.at[me], local_sem)
        local.start()
        sends = []
        for m in range(1, N_DEV):
            dev, _ = _peer(m)
            cp = pltpu.make_async_remote_copy(v_ref, o_ref.at[me], send_sems.at[m - 1], recv_sems.at[m - 1],
                                              device_id=dev, device_id_type=pl.DeviceIdType.MESH)
            cp.start()
            sends.append(cp)
        for m in range(1, N_DEV):
            dev, blk = _peer(m)
            pltpu.make_async_remote_copy(v_ref, o_ref.at[blk], send_sems.at[m - 1], recv_sems.at[m - 1],
                                         device_id=dev, device_id_type=pl.DeviceIdType.MESH).wait_recv()
        for cp in sends:
            cp.wait_send()
        local.wait()

    return pl.pallas_call(
        body, name=name, in_specs=[_ANY], out_specs=_ANY,
        out_shape=SDS((N_DEV,) + v.shape, v.dtype), scratch_shapes=_EXCHANGE_SEMS)(v)


def _exchange_partials(p, name):
    def body(p_ref, o_ref, send_sems, recv_sems, local_sem):
        x, y, c = _me()
        me = 4 * x + 2 * y + c
        local = pltpu.make_async_copy(p_ref.at[me], o_ref.at[me], local_sem)
        local.start()
        sends = []
        for m in range(1, N_DEV):
            dev, blk = _peer(m)
            cp = pltpu.make_async_remote_copy(p_ref.at[blk], o_ref.at[me], send_sems.at[m - 1], recv_sems.at[m - 1],
                                              device_id=dev, device_id_type=pl.DeviceIdType.MESH)
            cp.start()
            sends.append(cp)
        for m in range(1, N_DEV):
            dev, blk = _peer(m)
            pltpu.make_async_remote_copy(p_ref.at[blk], o_ref.at[blk], send_sems.at[m - 1], recv_sems.at[m - 1],
                                         device_id=dev, device_id_type=pl.DeviceIdType.MESH).wait_recv()
        for cp in sends:
            cp.wait_send()
        local.wait()

    return pl.pallas_call(
        body, name=name, in_specs=[_ANY], out_specs=_ANY,
        out_shape=SDS(p.shape, p.dtype), scratch_shapes=_EXCHANGE_SEMS)(p)


def _cast_bf16(w, name):
    nl, r, c = w.shape
    tr = 256 if r % 256 == 0 else r

    def body(w_ref, o_ref):
        o_ref[...] = w_ref[...].astype(BF16)

    blk = pl.BlockSpec((None, tr, c), lambda l, i: (l, i, 0))
    return pl.pallas_call(body, name=name, grid=(nl, r // tr), in_specs=[blk], out_specs=blk,
                          out_shape=SDS(w.shape, BF16), compiler_params=_cp("parallel", "parallel"))(w)


def _sum8(p, name):
    _, r, c = p.shape

    def body(p_ref, o_ref):
        acc = p_ref[0].astype(F32)
        for k in range(1, N_DEV):
            acc = acc + p_ref[k].astype(F32)
        o_ref[...] = acc

    return pl.pallas_call(body, name=name, grid=(1,), in_specs=[_full(p.shape)], out_specs=_full((r, c)),
                          out_shape=SDS((r, c), F32), compiler_params=_cp("arbitrary"))(p)


def _adamw(w, g, m, v):
    m = ADAM_B1 * m + (1.0 - ADAM_B1) * g
    v = ADAM_B2 * v + (1.0 - ADAM_B2) * (g * g)
    m_hat = m / (1.0 - ADAM_B1 ** ADAM_STEP)
    v_hat = v / (1.0 - ADAM_B2 ** ADAM_STEP)
    delta = -ADAM_LR * (m_hat / (jnp.sqrt(v_hat) + ADAM_EPS) + ADAM_WD * w)
    return delta, m, v


def _adam_shard(parts, w, m, v, layer, prev, name):
    _, r, c = parts.shape
    tr = 256 if r % 256 == 0 else r
    n_prev = 0 if prev is None else 4

    def body(*refs):
        p_ref, w_ref, m_ref, v_ref = refs[:4]
        g_ref, d_ref, nm_ref, nv_ref = refs[4 + n_prev:]
        g = p_ref[0].astype(F32)
        for k in range(1, N_DEV):
            g = g + p_ref[k].astype(F32)
        delta, nm, nv = _adamw(w_ref[...], g, m_ref[...], v_ref[...])
        g_ref[...] = g
        d_ref[...] = delta
        nm_ref[...] = nm
        nv_ref[...] = nv

    blk = pl.BlockSpec((None, tr, c), lambda i: (layer, i, 0))
    return pl.pallas_call(
        body, name=name, grid=(r // tr,),
        in_specs=[pl.BlockSpec((N_DEV, tr, c), lambda i: (0, i, 0)), blk, blk, blk] + [_ANY] * n_prev,
        out_specs=[blk] * 4, out_shape=[SDS(w.shape, F32)] * 4,
        input_output_aliases={4 + k: k for k in range(n_prev)},
        compiler_params=_cp("parallel"))(parts, w, m, v, *(prev or ()))


def _adam_flat(w, g, m, v):
    r = w.shape[0]
    tr = r // 2 if r % 16 == 0 else r

    def body(w_ref, g_ref, m_ref, v_ref, d_ref, nm_ref, nv_ref):
        delta, nm, nv = _adamw(w_ref[...], g_ref[...], m_ref[...], v_ref[...])
        d_ref[...] = delta
        nm_ref[...] = nm
        nv_ref[...] = nv

    blk = pl.BlockSpec((tr, D), lambda i: (i, 0))
    return pl.pallas_call(body, name="adam_small", grid=(r // tr,), in_specs=[blk] * 4, out_specs=[blk] * 3,
                          out_shape=[SDS(w.shape, F32)] * 3, compiler_params=_cp("parallel"))(w, g, m, v)


def _local_step(x, tgt, p, wts):
    s = x.shape[0]
    tm = _row_tile(s)
    win, wout, wfi, wfo = wts["win"], wts["wout"], wts["wfi"], wts["wfo"]
    wsb = p["gmlp_w_s"].astype(BF16)
    wstb = jnp.swapaxes(p["gmlp_w_s"], -1, -2).astype(BF16)
    bsb = jnp.broadcast_to(p["gmlp_b_s"][..., None], p["gmlp_w_s"].shape)
    wrb = p["lru_w_r"].astype(BF16)
    wib = p["lru_w_i"].astype(BF16)
    saved = []
    for l in range(2):
        z, h1 = _norm_inproj(x, p["norm1_g"][l][None], win, l, tm)
        ya = _gmlp_fwd(z, p["gmlp_ln_g"][l][None], p["gmlp_ln_b"][l][None], wsb[l], bsb[l], l, tm)
        a0, b0, a1, b1 = _lru_gates_fwd(z, p["conv_w"][l], p["conv_b"][l][None], wrb[l], wib[l],
                                        p["lru_b_r"][l], p["lru_b_i"][l], p["lru_lambda"][l], l, tm)
        h0, hr = _lru_scan(a0, b0, a1, b1, False, l)
        x1, mg = _merge_outproj(x, ya, h0, hr, z, wout, l, tm)
        x2, gu, h2 = _ffn_fwd(x1, p["norm2_g"][l][None], wfi, wfo, l, tm)
        saved.append((x, z, h1, ya, a0, a1, h0, hr, x1, mg, gu, h2))
        x = x2
    dx, loss, dfg = _loss_head(x, p["final_g"][None], tgt, tm)
    big = [None, None]
    small = [None, None]
    for l in (1, 0):
        x0, z, h1, ya, a0, a1, h0, hr, x1, mg, gu, h2 = saved[l]
        ff, dgu = _ffn_bwd_act(dx, wfo, gu, l, tm)
        d_wfo = _mm_tn(ff, lambda ts: pl.BlockSpec((None, ts, FF_BLK), lambda j, t: (j, t, 0)),
                       dx, lambda ts: pl.BlockSpec((ts, D), lambda j, t: (t, 0)),
                       4, FF_BLK, D, s, f"dw_ffn_out_l{l}")
        dgu8 = dgu.reshape(N_DEV, s, FF_BLK)
        d_wfi = _mm_tn(h2, lambda ts: pl.BlockSpec((ts, D), lambda j, t: (t, 0)),
                       dgu8, lambda ts: pl.BlockSpec((None, ts, FF_BLK), lambda j, t: (j, t, 0)),
                       N_DEV, D, FF_BLK, s, f"dw_ffn_in_l{l}")
        dx1, dg2 = _mm_nt_rms_bwd(
            dgu8, pl.BlockSpec((None, tm, FF_BLK), lambda i, k: (k, i, 0)),
            wfi, pl.BlockSpec((None, None, None, D, FF_BLK), lambda i, k: (k // 4, k % 4, l, 0, 0)),
            N_DEV, x1, p["norm2_g"][l][None], dx, f"ffn_bwd_dx_l{l}", tm)
        dz, dya, dh = _outproj_bwd_merge(dx1, wout, ya, h0, hr, z, l, tm)
        d_wout = _mm_tn(mg, lambda ts: pl.BlockSpec((ts, D), lambda j, t: (t, 0)),
                        dx1, lambda ts: pl.BlockSpec((ts, D), lambda j, t: (t, 0)),
                        1, D, D, s, f"dw_out_l{l}")
        g1, g0 = _lru_scan(a1, dh, a0, dh, True, l)
        dxc, dwr, dwi, dbr, dbi, dlam = _lru_gates_bwd(
            z, h0, hr, g0, g1, p["conv_w"][l], p["conv_b"][l][None], wrb[l], wib[l],
            p["lru_b_r"][l], p["lru_b_i"][l], p["lru_lambda"][l], l, tm)
        dz, dcw, dcb = _conv_bwd(dz, dxc, z, p["conv_w"][l], l, tm)
        dz, dws, dbs, dlng, dlnb = _gmlp_bwd(dz, z, dya, p["gmlp_ln_g"][l][None], p["gmlp_ln_b"][l][None],
                                             wsb[l], wstb[l], bsb[l], l, tm)
        d_win = _mm_tn(h1, lambda ts: pl.BlockSpec((ts, D), lambda j, t: (t, 0)),
                       dz, lambda ts: pl.BlockSpec((ts, IN_BLK), lambda j, t: (t, j)),
                       N_DEV, D, IN_BLK, s, f"dw_in_l{l}")
        dx, dg1 = _mm_nt_rms_bwd(
            dz, pl.BlockSpec((tm, IN_BLK), lambda i, k: (i, k)),
            win, pl.BlockSpec((None, None, D, IN_BLK), lambda i, k: (k, l, 0, 0)),
            N_DEV, x0, p["norm1_g"][l][None], dx1, f"inproj_bwd_dx_l{l}", tm)
        big[l] = dict(w_in=d_win, w_out=d_wout.reshape(N_DEV, D // N_DEV, D), w_ffn_in=d_wfi,
                      w_ffn_out=d_wfo.reshape(N_DEV, D_FF // N_DEV, D))
        small[l] = dict(norm1_g=dg1[0], gmlp_ln_g=dlng[0], gmlp_ln_b=dlnb[0], gmlp_w_s=dws, gmlp_b_s=dbs,
                        conv_w=dcw, conv_b=dcb[0], lru_w_r=dwr, lru_b_r=dbr, lru_w_i=dwi, lru_b_i=dbi,
                        lru_lambda=dlam, norm2_g=dg2[0])
    small_g = {k: jnp.stack([small[0][k], small[1][k]]) for k in small[0]}
    small_g["final_g"] = dfg[0]
    return loss, dx, big, small_g


_REPL = ["norm1_g", "gmlp_ln_g", "gmlp_ln_b", "gmlp_w_s", "gmlp_b_s", "conv_b", "lru_w_r", "lru_w_i", "norm2_g", "final_g"]
_LANE_SHARDED = ["conv_w", "lru_b_r", "lru_b_i", "lru_lambda"]
_BIG = ["w_in", "w_out", "w_ffn_in", "w_ffn_out"]
_ORDER = ["norm1_g", "w_in", "gmlp_ln_g", "gmlp_ln_b", "gmlp_w_s", "gmlp_b_s", "conv_w", "conv_b", "lru_w_r", "lru_b_r",
          "lru_w_i", "lru_b_i", "lru_lambda", "w_out", "norm2_g", "w_ffn_in", "w_ffn_out", "final_g"]


def _pack(parts, rows):
    flat = jnp.concatenate([a.reshape(-1) for a in parts])
    return jnp.pad(flat, (0, rows * D - flat.shape[0])).reshape(rows, D)


def _unpack(flat, shapes):
    out, off = [], 0
    flat = flat.reshape(-1)
    for shp in shapes:
        n = 1
        for k in shp:
            n *= k
        out.append(flat[off:off + n].reshape(shp))
        off += n
    return out


def kernel(x, norm1_g, w_in, gmlp_ln_g, gmlp_ln_b, gmlp_w_s, gmlp_b_s, conv_w, conv_b, lru_w_r, lru_b_r, lru_w_i, lru_b_i, lru_lambda, w_out, norm2_g, w_ffn_in, w_ffn_out, final_g, loss_target, m_norm1_g, m_w_in, m_gmlp_ln_g, m_gmlp_ln_b, m_gmlp_w_s, m_gmlp_b_s, m_conv_w, m_conv_b, m_lru_w_r, m_lru_b_r, m_lru_w_i, m_lru_b_i, m_lru_lambda, m_w_out, m_norm2_g, m_w_ffn_in, m_w_ffn_out, m_final_g, v_norm1_g, v_w_in, v_gmlp_ln_g, v_gmlp_ln_b, v_gmlp_w_s, v_gmlp_b_s, v_conv_w, v_conv_b, v_lru_w_r, v_lru_b_r, v_lru_w_i, v_lru_b_i, v_lru_lambda, v_w_out, v_norm2_g, v_w_ffn_in, v_w_ffn_out, v_final_g):
    w = dict(norm1_g=norm1_g, w_in=w_in, gmlp_ln_g=gmlp_ln_g, gmlp_ln_b=gmlp_ln_b, gmlp_w_s=gmlp_w_s, gmlp_b_s=gmlp_b_s,
             conv_w=conv_w, conv_b=conv_b, lru_w_r=lru_w_r, lru_b_r=lru_b_r, lru_w_i=lru_w_i, lru_b_i=lru_b_i,
             lru_lambda=lru_lambda, w_out=w_out, norm2_g=norm2_g, w_ffn_in=w_ffn_in, w_ffn_out=w_ffn_out, final_g=final_g)
    mom = dict(norm1_g=m_norm1_g, w_in=m_w_in, gmlp_ln_g=m_gmlp_ln_g, gmlp_ln_b=m_gmlp_ln_b, gmlp_w_s=m_gmlp_w_s,
               gmlp_b_s=m_gmlp_b_s, conv_w=m_conv_w, conv_b=m_conv_b, lru_w_r=m_lru_w_r, lru_b_r=m_lru_b_r,
               lru_w_i=m_lru_w_i, lru_b_i=m_lru_b_i, lru_lambda=m_lru_lambda, w_out=m_w_out, norm2_g=m_norm2_g,
               w_ffn_in=m_w_ffn_in, w_ffn_out=m_w_ffn_out, final_g=m_final_g)
    var = dict(norm1_g=v_norm1_g, w_in=v_w_in, gmlp_ln_g=v_gmlp_ln_g, gmlp_ln_b=v_gmlp_ln_b, gmlp_w_s=v_gmlp_w_s,
               gmlp_b_s=v_gmlp_b_s, conv_w=v_conv_w, conv_b=v_conv_b, lru_w_r=v_lru_w_r, lru_b_r=v_lru_b_r,
               lru_w_i=v_lru_w_i, lru_b_i=v_lru_b_i, lru_lambda=v_lru_lambda, w_out=v_w_out, norm2_g=v_norm2_g,
               w_ffn_in=v_w_ffn_in, w_ffn_out=v_w_ffn_out, final_g=v_final_g)
    xi, yi, ci = _me()
    me = 4 * xi + 2 * yi + ci
    lane0 = me * HD

    gathered = {k: _all_gather(_cast_bf16(w[k], f"cast_{k}"), f"gather_{k}") for k in _BIG}
    wts = dict(win=gathered["w_in"],
               wout=jnp.swapaxes(gathered["w_out"], 0, 1).reshape(2, D, D),
               wfi=gathered["w_ffn_in"].reshape(2, 4, 2, D, FF_BLK),
               wfo=gathered["w_ffn_out"].reshape(4, 2, 2, D_FF // N_DEV, D).swapaxes(1, 2).reshape(4, 2, FF_BLK, D))
    lane_shapes = [w[k].shape for k in _LANE_SHARDED]
    lane_rows = sum(a[0] * a[1] for a in lane_shapes)
    packed = jnp.concatenate([w[k].reshape(-1, HD) for k in _LANE_SHARDED])
    packed = jnp.pad(packed, ((0, -lane_rows % 8), (0, 0)))
    lanes = _all_gather(packed, "gather_small")
    params = {k: w[k] for k in _REPL}
    off = 0
    for k, shp in zip(_LANE_SHARDED, lane_shapes):
        n = shp[0] * shp[1]
        params[k] = jnp.swapaxes(lanes[:, off:off + n], 0, 1).reshape(shp[0], shp[1], D)
        off += n

    loss, dx, big, small_g = _local_step(x[0], loss_target[0], params, wts)

    out = {}
    for k in _BIG:
        res = None
        for l in (1, 0):
            got = _exchange_partials(big[l][k], f"exchange_{k}_l{l}")
            res = _adam_shard(got, w[k], mom[k], var[k], l, res, f"adam_{k}_l{l}")
        out[k] = res

    names = _REPL + _LANE_SHARDED
    full_shapes = [small_g[k].shape for k in names]
    n_rows = sum(a.size for a in small_g.values()) // D
    shard_rows = -(-n_rows // (8 * N_DEV)) * 8
    g_flat = _pack([small_g[k] for k in names], shard_rows * N_DEV).reshape(N_DEV, shard_rows, D)
    mine = _sum8(_exchange_partials(g_flat, "exchange_small"), "sum_small")
    g_all = _all_gather(mine, "gather_small_grads")
    grads = dict(zip(names, _unpack(g_all, full_shapes)))
    for k in _LANE_SHARDED:
        grads[k] = lax.dynamic_slice_in_dim(grads[k], lane0, HD, axis=2)
    dev_shapes = [w[k].shape for k in names]
    rows = -(-sum(w[k].size for k in names) // (8 * D)) * 8
    flats = [_pack([src[k] for k in names], rows) for src in (w, grads, mom, var)]
    for nm, flat in zip(("delta", "m", "v"), _adam_flat(*flats)):
        for k, a in zip(names, _unpack(flat, dev_shapes)):
            out.setdefault(k, [grads[k], None, None, None])[("delta", "m", "v").index(nm) + 1] = a

    loss = lax.psum(loss[0, 0], MESH_AXES)
    return (loss, dx[None], *[out[k][0] for k in _ORDER], *[out[k][1] for k in _ORDER],
            *[out[k][2] for k in _ORDER], *[out[k][3] for k in _ORDER])
```

```python
import jax
import jax.numpy as jnp
from jax import lax
from jax.experimental import pallas as pl
from jax.experimental.pallas import tpu as pltpu

F32 = jnp.float32
BF16 = jnp.bfloat16
SDS = jax.ShapeDtypeStruct

D = 1024
N_IN = 6 * D
D_FF = 2816
N_DEV = 8
IN_BLK = N_IN // N_DEV
FF_BLK = 2 * D_FF // N_DEV
HEADS = 8
HD = 128
EPS = 1e-6
LRU_C = 8.0
MESH_AXES = ("x", "y", "c")

ADAM_LR = 0.001
ADAM_B1 = 0.9
ADAM_B2 = 0.999
ADAM_EPS = 1e-08
ADAM_WD = 0.01
ADAM_STEP = 10

VMEM_LIMIT = 48 * 2**20


def _cp(*sem):
    return pltpu.CompilerParams(dimension_semantics=sem, vmem_limit_bytes=VMEM_LIMIT)


def _row_tile(s):
    return 512 if s >= 1024 else s // 2


_GELU_C = 0.7978845608028654


def _gelu(x):
    t = jnp.tanh(_GELU_C * (x + 0.044715 * (x * x * x)))
    return 0.5 * x * (1.0 + t), t


def _gelu_grad(x, t):
    return 0.5 * (1.0 + t) + 0.5 * x * (1.0 - t * t) * (_GELU_C * (1.0 + 0.134145 * (x * x)))


def _sigmoid(x):
    return jax.nn.sigmoid(x)


def _expm1(x):
    u = jnp.exp(x)
    um1 = u - 1.0
    lu = jnp.log(u)
    return jnp.where(um1 == 0.0, x, jnp.where(um1 == -1.0, -1.0, um1 * x / jnp.where(lu == 0.0, 1.0, lu)))


def _softplus(x):
    e = jnp.exp(-jnp.abs(x))
    w = 1.0 + e
    l1p = jnp.where(w == 1.0, e, jnp.log(w) * e / jnp.where(w == 1.0, 1.0, w - 1.0))
    return jnp.maximum(x, 0.0) + l1p


def _rms_fwd(x, g):
    r = lax.rsqrt(jnp.mean(x * x, axis=-1, keepdims=True) + EPS)
    return x * r * g


def _rms_bwd(x, g, dh):
    r = lax.rsqrt(jnp.mean(x * x, axis=-1, keepdims=True) + EPS)
    xh = x * r
    dxh = dh * g
    dx = r * (dxh - xh * jnp.mean(dxh * xh, axis=-1, keepdims=True))
    dg = jnp.sum(dh * xh, axis=0, keepdims=True)
    return dx, dg


def _dot(a, b):
    return jnp.dot(a, b, preferred_element_type=F32)


def _dot_nt(a, b):
    return lax.dot_general(a, b, (((1,), (1,)), ((), ())), preferred_element_type=F32)


def _dot_tn(a, b):
    return lax.dot_general(a, b, (((0,), (0,)), ((), ())), preferred_element_type=F32)


def _taps(prev, cur, nxt, tm):
    ext = jnp.concatenate([prev, cur, nxt], axis=0)
    n = tm + 16
    sl = slice(8, 8 + tm)
    return (pltpu.roll(ext, 2, 0)[sl], pltpu.roll(ext, 1, 0)[sl], cur,
            pltpu.roll(ext, n - 1, 0)[sl], pltpu.roll(ext, n - 2, 0)[sl])


def _halo_specs(tm, s, col):
    nb8 = s // 8
    r8 = tm // 8
    return (pl.BlockSpec((8, D), lambda i: (jnp.maximum(i * r8 - 1, 0), col)),
            pl.BlockSpec((tm, D), lambda i: (i, col)),
            pl.BlockSpec((8, D), lambda i: (jnp.minimum((i + 1) * r8, nb8 - 1), col)))


def _halo_flags(nt):
    i = pl.program_id(0)
    return (i > 0).astype(F32), (i < nt - 1).astype(F32)


def _full(shape):
    nd = len(shape)
    return pl.BlockSpec(shape, lambda *_: (0,) * nd)


def _norm_inproj(x, g, w, layer, tm):
    s = x.shape[0]

    def body(x_ref, g_ref, w_ref, z_ref, h_ref, hs):
        @pl.when(pl.program_id(1) == 0)
        def _():
            h = _rms_fwd(x_ref[...], g_ref[...]).astype(BF16)
            hs[...] = h
            h_ref[...] = h
        z_ref[...] = _dot(hs[...], w_ref[...])

    return pl.pallas_call(
        body, name=f"norm_inproj_l{layer}", grid=(s // tm, N_DEV),
        in_specs=[pl.BlockSpec((tm, D), lambda i, j: (i, 0)), _full((1, D)),
                  pl.BlockSpec((None, D, IN_BLK), lambda i, j: (j, 0, 0))],
        out_specs=[pl.BlockSpec((tm, IN_BLK), lambda i, j: (i, j)), pl.BlockSpec((tm, D), lambda i, j: (i, 0))],
        out_shape=[SDS((s, N_IN), F32), SDS((s, D), BF16)],
        scratch_shapes=[pltpu.VMEM((tm, D), BF16)],
        compiler_params=_cp("parallel", "arbitrary"))(x, g, w)


def _gmlp_fwd(z, lng, lnb, ws, bsb, layer, tm):
    s = z.shape[0]

    def body(zu_ref, zv_ref, lng_ref, lnb_ref, ws_ref, bsb_ref, ya_ref):
        u, _ = _gelu(zu_ref[...])
        gv, _ = _gelu(zv_ref[...])
        xc = gv - jnp.mean(gv, axis=-1, keepdims=True)
        rstd = lax.rsqrt(jnp.mean(xc * xc, axis=-1, keepdims=True) + EPS)
        vb = (xc * rstd * lng_ref[...] + lnb_ref[...]).astype(BF16)
        for c in range(tm // HD):
            rs = slice(c * HD, (c + 1) * HD)
            for g in range(HEADS):
                cs = slice(g * HD, (g + 1) * HD)
                mixed = _dot(ws_ref[g], vb[rs, cs]) + bsb_ref[g]
                ya_ref[rs, cs] = u[rs, cs] * mixed

    return pl.pallas_call(
        body, name=f"gmlp_fwd_l{layer}", grid=(s // tm,),
        in_specs=[pl.BlockSpec((tm, D), lambda i: (i, 0)), pl.BlockSpec((tm, D), lambda i: (i, 1)),
                  _full((1, D)), _full((1, D)), _full((HEADS, HD, HD)), _full((HEADS, HD, HD))],
        out_specs=pl.BlockSpec((tm, D), lambda i: (i, 0)),
        out_shape=SDS((s, D), F32),
        compiler_params=_cp("parallel"))(z, z, lng, lnb, ws, bsb)


def _conv(taps, cw_ref, cb_ref):
    _, m1, c0, p1, p2 = taps
    return cb_ref[...] + m1 * cw_ref[0:1, :] + c0 * cw_ref[1:2, :] + p1 * cw_ref[2:3, :] + p2 * cw_ref[3:4, :]


def _heads_dot(xb, w_ref, d):
    return jnp.concatenate([_dot(xb[:, h * HD:(h + 1) * HD], w_ref[d, h]) for h in range(HEADS)], axis=1)


def _lru_gates(xc, xb, wr_ref, wi_ref, br_ref, bi_ref, lam_ref, d):
    sp = _softplus(-lam_ref[d:d + 1, :])
    r = _sigmoid(_heads_dot(xb, wr_ref, d) + br_ref[d:d + 1, :])
    ig = _sigmoid(_heads_dot(xb, wi_ref, d) + bi_ref[d:d + 1, :])
    la = (-LRU_C) * r * sp
    a = jnp.exp(la)
    mult = jnp.sqrt(jnp.maximum(-_expm1(2.0 * la), 0.0))
    return sp, r, ig, a, mult


def _lru_gates_fwd(z, cw, cb, wr, wi, br, bi, lam, layer, tm):
    s = z.shape[0]
    nt = s // tm

    def body(zp_ref, zc_ref, zn_ref, cw_ref, cb_ref, wr_ref, wi_ref, br_ref, bi_ref, lam_ref,
             a0_ref, b0_ref, a1_ref, b1_ref):
        fp, fn = _halo_flags(nt)
        xc = _conv(_taps(zp_ref[...] * fp, zc_ref[...], zn_ref[...] * fn, tm), cw_ref, cb_ref)
        xb = xc.astype(BF16)
        for d, (a_ref, b_ref) in enumerate(((a0_ref, b0_ref), (a1_ref, b1_ref))):
            _, _, ig, a, mult = _lru_gates(xc, xb, wr_ref, wi_ref, br_ref, bi_ref, lam_ref, d)
            a_ref[...] = a
            b_ref[...] = mult * (ig * xc)

    tile = pl.BlockSpec((tm, D), lambda i: (i, 0))
    return pl.pallas_call(
        body, name=f"lru_gates_fwd_l{layer}", grid=(nt,),
        in_specs=[*_halo_specs(tm, s, 2), _full((4, D)), _full((1, D)),
                  _full((2, HEADS, HD, HD)), _full((2, HEADS, HD, HD)), _full((2, D)), _full((2, D)), _full((2, D))],
        out_specs=[tile] * 4, out_shape=[SDS((s, D), F32)] * 4,
        compiler_params=_cp("parallel"))(z, z, z, cw, cb, wr, wi, br, bi, lam)


def _scan_group(a, x, c, reverse, bwd):
    row = lax.broadcasted_iota(jnp.int32, a.shape, 0)
    b = a * x if bwd else x
    for d in (1, 2, 4):
        keep = (row < 8 - d) if reverse else (row >= d)
        sh = 8 - d if reverse else d
        a_s = jnp.where(keep, pltpu.roll(a, sh, 0), 1.0)
        b_s = jnp.where(keep, pltpu.roll(b, sh, 0), 0.0)
        b = a * b_s + b
        a = a * a_s
    h = b + a * c
    new_c = h[0:1, :] if reverse else h[7:8, :]
    if not bwd:
        return h, new_c
    if reverse:
        prev = jnp.where(row < 7, pltpu.roll(h, 7, 0), c)
    else:
        prev = jnp.where(row >= 1, pltpu.roll(h, 1, 0), c)
    return x + prev, new_c


def _lru_scan(a_f, x_f, a_r, x_r, bwd, layer):
    s = a_f.shape[0]
    ts = min(1024, s // 2)
    cb = 512
    nt = s // ts
    ng = ts // 8

    def body(af_ref, xf_ref, ar_ref, xr_ref, of_ref, or_ref, cf, cr):
        @pl.when(pl.program_id(1) == 0)
        def _():
            cf[...] = jnp.zeros_like(cf)
            cr[...] = jnp.zeros_like(cr)

        def step(j, carry):
            c_f, c_r = carry
            rf = pl.multiple_of(j * 8, 8)
            rr = pl.multiple_of((ng - 1 - j) * 8, 8)
            o, c_f = _scan_group(af_ref[pl.ds(rf, 8), :], xf_ref[pl.ds(rf, 8), :], c_f, False, bwd)
            of_ref[pl.ds(rf, 8), :] = o
            o, c_r = _scan_group(ar_ref[pl.ds(rr, 8), :], xr_ref[pl.ds(rr, 8), :], c_r, True, bwd)
            or_ref[pl.ds(rr, 8), :] = o
            return c_f, c_r

        c_f, c_r = lax.fori_loop(0, ng, step, (cf[0:1, :], cr[0:1, :]), unroll=2)
        cf[...] = jnp.broadcast_to(c_f, cf.shape)
        cr[...] = jnp.broadcast_to(c_r, cr.shape)

    fwd = pl.BlockSpec((ts, cb), lambda c, t: (t, c))
    rev = pl.BlockSpec((ts, cb), lambda c, t: (nt - 1 - t, c))
    return pl.pallas_call(
        body, name=f"lru_scan_{'bwd' if bwd else 'fwd'}_l{layer}", grid=(D // cb, nt),
        in_specs=[fwd, fwd, rev, rev], out_specs=[fwd, rev],
        out_shape=[SDS((s, D), F32)] * 2,
        scratch_shapes=[pltpu.VMEM((8, cb), F32), pltpu.VMEM((8, cb), F32)],
        compiler_params=_cp("parallel", "arbitrary"))(a_f, x_f, a_r, x_r)


def _merge_outproj(x, ya, h0, h1, z, wo, layer, tm):
    s = x.shape[0]

    def body(x_ref, ya_ref, h0_ref, h1_ref, zg_ref, za_ref, zb_ref, wo_ref, x1_ref, mg_ref):
        gg, _ = _gelu(zg_ref[...])
        yb = (h0_ref[...] + h1_ref[...]) * gg
        mb = (_sigmoid(za_ref[...]) * ya_ref[...] + _sigmoid(zb_ref[...]) * yb).astype(BF16)
        mg_ref[...] = mb
        x1_ref[...] = x_ref[...] + _dot(mb, wo_ref[...])

    tile = pl.BlockSpec((tm, D), lambda i: (i, 0))
    return pl.pallas_call(
        body, name=f"merge_outproj_l{layer}", grid=(s // tm,),
        in_specs=[tile, tile, tile, tile] + [pl.BlockSpec((tm, D), lambda i, c=c: (i, c)) for c in (3, 4, 5)]
        + [_full((D, D))],
        out_specs=[tile, tile], out_shape=[SDS((s, D), F32), SDS((s, D), BF16)],
        compiler_params=_cp("parallel"))(x, ya, h0, h1, z, z, z, wo)


def _ffn_fwd(x1, g, wfi, wfo, layer, tm):
    s = x1.shape[0]

    def body(x_ref, g_ref, wi_ref, wo_ref, x2_ref, gu_ref, h_ref, hs, acc):
        k = pl.program_id(1)

        @pl.when(k == 0)
        def _():
            h = _rms_fwd(x_ref[...], g_ref[...]).astype(BF16)
            hs[...] = h
            h_ref[...] = h
            acc[...] = jnp.zeros_like(acc)

        gate = _dot(hs[...], wi_ref[0])
        up = _dot(hs[...], wi_ref[1])
        gu_ref[0] = gate
        gu_ref[1] = up
        acc[...] += _dot((gate * _sigmoid(gate) * up).astype(BF16), wo_ref[...])

        @pl.when(k == 3)
        def _():
            x2_ref[...] = x_ref[...] + acc[...]

    return pl.pallas_call(
        body, name=f"ffn_fwd_l{layer}", grid=(s // tm, 4),
        in_specs=[pl.BlockSpec((tm, D), lambda i, k: (i, 0)), _full((1, D)),
                  pl.BlockSpec((2, None, D, FF_BLK), lambda i, k: (0, k, 0, 0)),
                  pl.BlockSpec((None, FF_BLK, D), lambda i, k: (k, 0, 0))],
        out_specs=[pl.BlockSpec((tm, D), lambda i, k: (i, 0)),
                   pl.BlockSpec((2, None, tm, FF_BLK), lambda i, k: (0, k, i, 0)),
                   pl.BlockSpec((tm, D), lambda i, k: (i, 0))],
        out_shape=[SDS((s, D), F32), SDS((2, 4, s, FF_BLK), F32), SDS((s, D), BF16)],
        scratch_shapes=[pltpu.VMEM((tm, D), BF16), pltpu.VMEM((tm, D), F32)],
        compiler_params=_cp("parallel", "arbitrary"))(x1, g, wfi, wfo)


def _loss_head(x, g, tgt, tm):
    s = x.shape[0]

    def body(x_ref, g_ref, t_ref, dx_ref, loss_ref, dg_ref):
        @pl.when(pl.program_id(0) == 0)
        def _():
            loss_ref[...] = jnp.zeros_like(loss_ref)
            dg_ref[...] = jnp.zeros_like(dg_ref)

        x = x_ref[...]
        gv = g_ref[...]
        e = _rms_fwd(x, gv) - t_ref[...]
        rows = jnp.sum(e * e, axis=-1, keepdims=True)
        loss_ref[...] += (0.5 / D) * jnp.sum(rows, axis=0, keepdims=True)
        dx, dg = _rms_bwd(x, gv, e * (1.0 / D))
        dx_ref[...] = dx
        dg_ref[...] += dg

    tile = pl.BlockSpec((tm, D), lambda i: (i, 0))
    return pl.pallas_call(
        body, name="loss_head", grid=(s // tm,),
        in_specs=[tile, _full((1, D)), tile],
        out_specs=[tile, _full((1, 1)), _full((1, D))],
        out_shape=[SDS((s, D), F32), SDS((1, 1), F32), SDS((1, D), F32)],
        compiler_params=_cp("arbitrary"))(x, g, tgt)


def _ffn_bwd_act(dx2, wfo, gu, layer, tm):
    s = dx2.shape[0]

    def body(dx_ref, wo_ref, gu_ref, ff_ref, dgu_ref):
        dff = _dot_nt(dx_ref[...].astype(BF16), wo_ref[...])
        gate = gu_ref[0]
        up = gu_ref[1]
        sg = _sigmoid(gate)
        sl = gate * sg
        ff_ref[...] = (sl * up).astype(BF16)
        dgu_ref[0] = (dff * up * (sg * (1.0 + gate * (1.0 - sg)))).astype(BF16)
        dgu_ref[1] = (dff * sl).astype(BF16)

    blk = pl.BlockSpec((2, None, tm, FF_BLK), lambda i, k: (0, k, i, 0))
    return pl.pallas_call(
        body, name=f"ffn_bwd_act_l{layer}", grid=(s // tm, 4),
        in_specs=[pl.BlockSpec((tm, D), lambda i, k: (i, 0)),
                  pl.BlockSpec((None, FF_BLK, D), lambda i, k: (k, 0, 0)), blk],
        out_specs=[pl.BlockSpec((None, tm, FF_BLK), lambda i, k: (k, i, 0)), blk],
        out_shape=[SDS((4, s, FF_BLK), BF16), SDS((2, 4, s, FF_BLK), BF16)],
        compiler_params=_cp("parallel", "arbitrary"))(dx2, wfo, gu)


def _mm_nt_rms_bwd(a, a_spec, w, w_spec, nk, x, g, dres, name, tm):
    s = x.shape[0]

    def body(a_ref, w_ref, x_ref, g_ref, dres_ref, dx_ref, dg_ref, acc):
        i = pl.program_id(0)
        k = pl.program_id(1)

        @pl.when(k == 0)
        def _():
            acc[...] = jnp.zeros_like(acc)

        @pl.when(jnp.logical_and(i == 0, k == 0))
        def _():
            dg_ref[...] = jnp.zeros_like(dg_ref)

        acc[...] += _dot_nt(a_ref[...], w_ref[...])

        @pl.when(k == nk - 1)
        def _():
            dx, dg = _rms_bwd(x_ref[...], g_ref[...], acc[...])
            dx_ref[...] = dres_ref[...] + dx
            dg_ref[...] += dg

    tile = pl.BlockSpec((tm, D), lambda i, k: (i, 0))
    return pl.pallas_call(
        body, name=name, grid=(s // tm, nk),
        in_specs=[a_spec, w_spec, tile, _full((1, D)), tile],
        out_specs=[tile, _full((1, D))], out_shape=[SDS((s, D), F32), SDS((1, D), F32)],
        scratch_shapes=[pltpu.VMEM((tm, D), F32)],
        compiler_params=_cp("arbitrary", "arbitrary"))(a, w, x, g, dres)


def _mm_tn(a, a_spec, b, b_spec, nb, kb, bn, s, name):
    ts = _row_tile(s)
    ns = s // ts

    def body(a_ref, b_ref, o_ref, acc):
        t = pl.program_id(1)

        @pl.when(t == 0)
        def _():
            acc[...] = jnp.zeros_like(acc)

        acc[...] += _dot_tn(a_ref[...].astype(BF16), b_ref[...].astype(BF16))

        @pl.when(t == ns - 1)
        def _():
            o_ref[...] = acc[...].astype(BF16)

    return pl.pallas_call(
        body, name=name, grid=(nb, ns), in_specs=[a_spec(ts), b_spec(ts)],
        out_specs=pl.BlockSpec((None, kb, bn), lambda j, t: (j, 0, 0)),
        out_shape=SDS((nb, kb, bn), BF16),
        scratch_shapes=[pltpu.VMEM((kb, bn), F32)],
        compiler_params=_cp("parallel", "arbitrary"))(a, b)


def _outproj_bwd_merge(dx1, wo, ya, h0, h1, z, layer, tm):
    s = dx1.shape[0]

    def body(dx_ref, wo_ref, ya_ref, h0_ref, h1_ref, zg_ref, za_ref, zb_ref, dz_ref, dya_ref, dh_ref):
        dm = _dot_nt(dx_ref[...].astype(BF16), wo_ref[...])
        sa = _sigmoid(za_ref[...])
        sb = _sigmoid(zb_ref[...])
        zg = zg_ref[...]
        gg, tg = _gelu(zg)
        hs = h0_ref[...] + h1_ref[...]
        dyb = dm * sb
        dya_ref[...] = dm * sa
        dh_ref[...] = dyb * gg
        dz_ref[:, 0:D] = (dyb * hs * _gelu_grad(zg, tg)).astype(BF16)
        dz_ref[:, D:2 * D] = (dm * ya_ref[...] * (sa * (1.0 - sa))).astype(BF16)
        dz_ref[:, 2 * D:3 * D] = (dm * (hs * gg) * (sb * (1.0 - sb))).astype(BF16)

    tile = pl.BlockSpec((tm, D), lambda i: (i, 0))
    return pl.pallas_call(
        body, name=f"outproj_bwd_merge_l{layer}", grid=(s // tm,),
        in_specs=[tile, _full((D, D)), tile, tile, tile]
        + [pl.BlockSpec((tm, D), lambda i, c=c: (i, c)) for c in (3, 4, 5)],
        out_specs=[pl.BlockSpec((tm, 3 * D), lambda i: (i, 1)), tile, tile],
        out_shape=[SDS((s, N_IN), BF16), SDS((s, D), F32), SDS((s, D), F32)],
        compiler_params=_cp("parallel"))(dx1, wo, ya, h0, h1, z, z, z)


def _lru_gates_bwd(z, h0, h1, g0, g1, cw, cb, wr, wi, br, bi, lam, layer, tm):
    s = z.shape[0]
    nt = s // tm

    def body(zp_ref, zc_ref, zn_ref, h0p_ref, h0_ref, h1_ref, h1n_ref, g0_ref, g1_ref,
             cw_ref, cb_ref, wr_ref, wi_ref, br_ref, bi_ref, lam_ref,
             dxc_ref, dwr_ref, dwi_ref, dbr_ref, dbi_ref, dlam_ref):
        i = pl.program_id(0)
        fp, fn = _halo_flags(nt)

        @pl.when(i == 0)
        def _():
            for r in (dwr_ref, dwi_ref, dbr_ref, dbi_ref, dlam_ref):
                r[...] = jnp.zeros_like(r)

        xc = _conv(_taps(zp_ref[...] * fp, zc_ref[...], zn_ref[...] * fn, tm), cw_ref, cb_ref)
        xb = xc.astype(BF16)
        zeros8 = jnp.zeros((8, D), F32)
        h_prev = _taps(h0p_ref[...] * fp, h0_ref[...], zeros8, tm)[1]
        h_next = _taps(zeros8, h1_ref[...], h1n_ref[...] * fn, tm)[3]
        dxc = jnp.zeros((tm, D), F32)
        for d, (g_ref, hsh) in enumerate(((g0_ref, h_prev), (g1_ref, h_next))):
            sp, r, ig, a, mult = _lru_gates(xc, xb, wr_ref, wi_ref, br_ref, bi_ref, lam_ref, d)
            db = g_ref[...]
            da = db * hsh
            dmult = db * (ig * xc)
            di = db * (mult * xc)
            dxc = dxc + db * (mult * ig)
            dla = da * a - dmult * jnp.where(mult > 0.0, a * a / jnp.where(mult > 0.0, mult, 1.0), 0.0)
            dlam_ref[d:d + 1, :] += jnp.sum(dla * r, axis=0, keepdims=True) * (-LRU_C)
            dpr = dla * sp * (-LRU_C) * (r * (1.0 - r))
            dpi = di * (ig * (1.0 - ig))
            dbr_ref[d:d + 1, :] += jnp.sum(dpr, axis=0, keepdims=True)
            dbi_ref[d:d + 1, :] += jnp.sum(dpi, axis=0, keepdims=True)
            dprb = dpr.astype(BF16)
            dpib = dpi.astype(BF16)
            parts = []
            for h in range(HEADS):
                cs = slice(h * HD, (h + 1) * HD)
                dwr_ref[d, h] += _dot_tn(xb[:, cs], dprb[:, cs])
                dwi_ref[d, h] += _dot_tn(xb[:, cs], dpib[:, cs])
                parts.append(_dot_nt(dprb[:, cs], wr_ref[d, h]) + _dot_nt(dpib[:, cs], wi_ref[d, h]))
            dxc = dxc + jnp.concatenate(parts, axis=1)
        dxc_ref[...] = dxc

        @pl.when(i == nt - 1)
        def _():
            dlam_ref[...] = dlam_ref[...] * (-_sigmoid(-lam_ref[...]))

    tile = pl.BlockSpec((tm, D), lambda i: (i, 0))
    zp, zc, zn = _halo_specs(tm, s, 2)
    hp, hc, hn = _halo_specs(tm, s, 0)
    wspec = _full((2, HEADS, HD, HD))
    return pl.pallas_call(
        body, name=f"lru_gates_bwd_l{layer}", grid=(nt,),
        in_specs=[zp, zc, zn, hp, hc, hc, hn, tile, tile, _full((4, D)), _full((1, D)),
                  wspec, wspec, _full((2, D)), _full((2, D)), _full((2, D))],
        out_specs=[tile, wspec, wspec, _full((2, D)), _full((2, D)), _full((2, D))],
        out_shape=[SDS((s, D), F32), SDS((2, HEADS, HD, HD), F32), SDS((2, HEADS, HD, HD), F32),
                   SDS((2, D), F32), SDS((2, D), F32), SDS((2, D), F32)],
        compiler_params=_cp("arbitrary"))(z, z, z, h0, h0, h1, h1, g0, g1, cw, cb, wr, wi, br, bi, lam)


def _conv_bwd(dz, dxc, z, cw, layer, tm):
    s = z.shape[0]
    nt = s // tm

    def body(dz_in, dp_ref, dc_ref, dn_ref, zp_ref, zc_ref, zn_ref, cw_ref, dz_ref, dcw_ref, dcb_ref):
        del dz_in
        fp, fn = _halo_flags(nt)

        @pl.when(pl.program_id(0) == 0)
        def _():
            dcw_ref[...] = jnp.zeros_like(dcw_ref)
            dcb_ref[...] = jnp.zeros_like(dcb_ref)

        dxc = dc_ref[...]
        dm2, dm1, _, dp1, _ = _taps(dp_ref[...] * fp, dxc, dn_ref[...] * fn, tm)
        dz_ref[...] = (cw_ref[0:1, :] * dp1 + cw_ref[1:2, :] * dxc + cw_ref[2:3, :] * dm1
                       + cw_ref[3:4, :] * dm2).astype(BF16)
        _, zm1, z0, zp1, zp2 = _taps(zp_ref[...] * fp, zc_ref[...], zn_ref[...] * fn, tm)
        for k, zt in enumerate((zm1, z0, zp1, zp2)):
            dcw_ref[k:k + 1, :] += jnp.sum(dxc * zt, axis=0, keepdims=True)
        dcb_ref[...] += jnp.sum(dxc, axis=0, keepdims=True)

    return pl.pallas_call(
        body, name=f"conv_bwd_l{layer}", grid=(nt,),
        in_specs=[pl.BlockSpec(memory_space=pl.ANY), *_halo_specs(tm, s, 0), *_halo_specs(tm, s, 2), _full((4, D))],
        out_specs=[pl.BlockSpec((tm, D), lambda i: (i, 2)), _full((4, D)), _full((1, D))],
        out_shape=[SDS((s, N_IN), BF16), SDS((4, D), F32), SDS((1, D), F32)],
        input_output_aliases={0: 0},
        compiler_params=_cp("arbitrary"))(dz, dxc, dxc, dxc, z, z, z, cw)


def _gmlp_bwd(dz, z, dya, lng, lnb, ws, wst, bsb, layer, tm):
    s = z.shape[0]
    nt = s // tm

    def body(dz_in, zu_ref, zv_ref, dya_ref, lng_ref, lnb_ref, ws_ref, wst_ref, bsb_ref,
             dz_ref, dws_ref, dbs_ref, dlng_ref, dlnb_ref, du_s, dv_s, dbs_acc):
        del dz_in
        i = pl.program_id(0)

        @pl.when(i == 0)
        def _():
            for r in (dws_ref, dlng_ref, dlnb_ref, dbs_acc):
                r[...] = jnp.zeros_like(r)

        zu = zu_ref[...]
        zv = zv_ref[...]
        u, tu = _gelu(zu)
        gv, tv = _gelu(zv)
        xc = gv - jnp.mean(gv, axis=-1, keepdims=True)
        rstd = lax.rsqrt(jnp.mean(xc * xc, axis=-1, keepdims=True) + EPS)
        xh = xc * rstd
        lng_v = lng_ref[...]
        vb = (xh * lng_v + lnb_ref[...]).astype(BF16)
        dya = dya_ref[...]
        for c in range(tm // HD):
            rs = slice(c * HD, (c + 1) * HD)
            for g in range(HEADS):
                cs = slice(g * HD, (g + 1) * HD)
                vblk = vb[rs, cs]
                mixed = _dot(ws_ref[g], vblk) + bsb_ref[g]
                du_s[rs, cs] = dya[rs, cs] * mixed
                dmx = dya[rs, cs] * u[rs, cs]
                dbs_acc[g] += dmx
                dmxb = dmx.astype(BF16)
                dws_ref[g] += _dot_nt(dmxb, vblk)
                dv_s[rs, cs] = _dot(wst_ref[g], dmxb)
        dv = dv_s[...]
        dlng_ref[...] += jnp.sum(dv * xh, axis=0, keepdims=True)
        dlnb_ref[...] += jnp.sum(dv, axis=0, keepdims=True)
        dxh = dv * lng_v
        dgv = rstd * (dxh - jnp.mean(dxh, axis=-1, keepdims=True)
                      - xh * jnp.mean(dxh * xh, axis=-1, keepdims=True))
        dz_ref[:, 0:D] = (du_s[...] * _gelu_grad(zu, tu)).astype(BF16)
        dz_ref[:, D:2 * D] = (dgv * _gelu_grad(zv, tv)).astype(BF16)

        @pl.when(i == nt - 1)
        def _():
            for g in range(HEADS):
                dbs_ref[g:g + 1, :] = jnp.sum(dbs_acc[g].T, axis=0, keepdims=True)

    tile = pl.BlockSpec((tm, D), lambda i: (i, 0))
    wspec = _full((HEADS, HD, HD))
    return pl.pallas_call(
        body, name=f"gmlp_bwd_l{layer}", grid=(nt,),
        in_specs=[pl.BlockSpec(memory_space=pl.ANY), tile, pl.BlockSpec((tm, D), lambda i: (i, 1)), tile,
                  _full((1, D)), _full((1, D)), wspec, wspec, wspec],
        out_specs=[pl.BlockSpec((tm, 2 * D), lambda i: (i, 0)), wspec, _full((HEADS, HD)), _full((1, D)), _full((1, D))],
        out_shape=[SDS((s, N_IN), BF16), SDS((HEADS, HD, HD), F32), SDS((HEADS, HD), F32),
                   SDS((1, D), F32), SDS((1, D), F32)],
        scratch_shapes=[pltpu.VMEM((tm, D), F32), pltpu.VMEM((tm, D), F32), pltpu.VMEM((HEADS, HD, HD), F32)],
        input_output_aliases={0: 0},
        compiler_params=_cp("arbitrary"))(dz, z, z, dya, lng, lnb, ws, wst, bsb)


def _me():
    return lax.axis_index("x"), lax.axis_index("y"), lax.axis_index("c")


def _peer(m):
    x, y, c = _me()
    px = 1 - x if m & 4 else x
    py = 1 - y if m & 2 else y
    pc = 1 - c if m & 1 else c
    return (px, py, pc), 4 * px + 2 * py + pc


_ANY = pl.BlockSpec(memory_space=pl.ANY)
_EXCHANGE_SEMS = [pltpu.SemaphoreType.DMA((N_DEV - 1,)), pltpu.SemaphoreType.DMA((N_DEV - 1,)), pltpu.SemaphoreType.DMA(())]


def _all_gather(v, name):
    def body(v_ref, o_ref, send_sems, recv_sems, local_sem):
        x, y, c = _me()
        me = 4 * x + 2 * y + c
        local = pltpu.make_async_copy(v_ref, o_ref.at[me], local_sem)
        local.start()
        sends = []
        for m in range(1, N_DEV):
            dev, _ = _peer(m)
            cp = pltpu.make_async_remote_copy(v_ref, o_ref.at[me], send_sems.at[m - 1], recv_sems.at[m - 1],
                                              device_id=dev, device_id_type=pl.DeviceIdType.MESH)
            cp.start()
            sends.append(cp)
        for m in range(1, N_DEV):
            dev, blk = _peer(m)
            pltpu.make_async_remote_copy(v_ref, o_ref.at[blk], send_sems.at[m - 1], recv_sems.at[m - 1],
                                         device_id=dev, device_id_type=pl.DeviceIdType.MESH).wait_recv()
        for cp in sends:
            cp.wait_send()
        local.wait()

    return pl.pallas_call(
        body, name=name, in_specs=[_ANY], out_specs=_ANY,
        out_shape=SDS((N_DEV,) + v.shape, v.dtype), scratch_shapes=_EXCHANGE_SEMS)(v)


def _exchange_partials(p, name):
    def body(p_ref, o_ref, send_sems, recv_sems, local_sem):
        x, y, c = _me()
        me = 4 * x + 2 * y + c
        local = pltpu.make_async_copy(p_ref.at[me], o_ref.at[me], local_sem)
        local.start()
        sends = []
        for m in range(1, N_DEV):
            dev, blk = _peer(m)
            cp = pltpu.make_async_remote_copy(p_ref.at[blk], o_ref.at[me], send_sems.at[m - 1], recv_sems.at[m - 1],
                                              device_id=dev, device_id_type=pl.DeviceIdType.MESH)
            cp.start()
            sends.append(cp)
        for m in range(1, N_DEV):
            dev, blk = _peer(m)
            pltpu.make_async_remote_copy(p_ref.at[blk], o_ref.at[blk], send_sems.at[m - 1], recv_sems.at[m - 1],
                                         device_id=dev, device_id_type=pl.DeviceIdType.MESH).wait_recv()
        for cp in sends:
            cp.wait_send()
        local.wait()

    return pl.pallas_call(
        body, name=name, in_specs=[_ANY], out_specs=_ANY,
        out_shape=SDS(p.shape, p.dtype), scratch_shapes=_EXCHANGE_SEMS)(p)


_HBM = pl.BlockSpec(memory_space=pltpu.HBM)
_SEM = pl.BlockSpec(memory_space=pltpu.SEMAPHORE)
_EFFECT = pltpu.CompilerParams(has_side_effects=pltpu.SideEffectType.DATAFLOW_SIDE_EFFECTING)
_PEER_SEMS = pltpu.SemaphoreType.DMA((N_DEV - 1,))


def _in_hbm(a):
    return pltpu.with_memory_space_constraint(a, pltpu.HBM)


def _tie(a, token):
    return a if token is None else lax.optimization_barrier((a, token))[0]


def _remote(src, dst, send_sems, recv_sems, m):
    dev, _ = _peer(m)
    return pltpu.make_async_remote_copy(src, dst, send_sems.at[m - 1], recv_sems.at[m - 1],
                                        device_id=dev, device_id_type=pl.DeviceIdType.MESH)


def _gather_start(lands, name):
    n = len(lands)

    def body(*refs):
        land = refs[:n]
        sems = refs[n:3 * n]
        token = refs[-1]
        x, y, c = _me()
        me = 4 * x + 2 * y + c
        for t in range(n):
            for m in range(1, N_DEV):
                _remote(land[t].at[me], land[t].at[me], sems[2 * t], sems[2 * t + 1], m).start()
        token[...] = jnp.zeros_like(token)

    res = pl.pallas_call(
        body, name=name, in_specs=[_HBM] * n,
        out_specs=[_SEM] * (2 * n) + [_HBM] * n + [pl.BlockSpec(memory_space=pltpu.VMEM)],
        out_shape=[_PEER_SEMS] * (2 * n) + [pltpu.HBM(a.shape, a.dtype) for a in lands] + [SDS((8, 128), F32)],
        input_output_aliases={t: 2 * n + t for t in range(n)},
        compiler_params=_EFFECT)(*[_in_hbm(a) for a in lands])
    return [(res[2 * t], res[2 * t + 1], res[2 * n + t]) for t in range(n)], res[-1]


def _gather_wait(handle, after, name):
    send_sems, recv_sems, land = handle

    def body(land_ref, ssem, rsem, after_ref, out_ref):
        del after_ref, out_ref
        x, y, c = _me()
        me = 4 * x + 2 * y + c
        for m in range(1, N_DEV):
            _, blk = _peer(m)
            cp = _remote(land_ref.at[me], land_ref.at[blk], ssem, rsem, m)
            cp.wait_send()
            cp.wait_recv()

    return pl.pallas_call(
        body, name=name, in_specs=[_HBM, _SEM, _SEM, _ANY], out_specs=_HBM,
        out_shape=pltpu.HBM(land.shape, land.dtype), input_output_aliases={0: 0},
        compiler_params=_EFFECT)(land, send_sems, recv_sems, after)


def _exchange_start(ps, name):
    n = len(ps)

    def body(*refs):
        p = refs[:n]
        got = refs[n:2 * n]
        sems = refs[2 * n:5 * n]
        token = refs[-1]
        x, y, c = _me()
        me = 4 * x + 2 * y + c
        for t in range(n):
            pltpu.make_async_copy(p[t].at[me], got[t].at[me], sems[3 * t + 2]).start()
            for m in range(1, N_DEV):
                _, blk = _peer(m)
                _remote(p[t].at[blk], got[t].at[me], sems[3 * t], sems[3 * t + 1], m).start()
        token[...] = jnp.zeros_like(token)

    res = pl.pallas_call(
        body, name=name, in_specs=[_HBM] * (2 * n),
        out_specs=[_SEM] * (3 * n) + [_HBM] * (2 * n) + [pl.BlockSpec(memory_space=pltpu.VMEM)],
        out_shape=[_PEER_SEMS, _PEER_SEMS, pltpu.SemaphoreType.DMA(())] * n
        + [pltpu.HBM(a.shape, a.dtype) for a in ps] * 2 + [SDS((8, 128), F32)],
        input_output_aliases={t: 3 * n + t for t in range(2 * n)},
        compiler_params=_EFFECT)(*[_in_hbm(a) for a in ps], *[_in_hbm(lax.empty(a.shape, a.dtype)) for a in ps])
    return [(res[3 * t], res[3 * t + 1], res[3 * t + 2], res[3 * n + t], res[4 * n + t]) for t in range(n)], res[-1]


def _exchange_wait(handle, after, name):
    send_sems, recv_sems, local_sem, p, got = handle

    def body(p_ref, got_ref, ssem, rsem, lsem, after_ref, p_out, got_out):
        del after_ref, p_out, got_out
        x, y, c = _me()
        me = 4 * x + 2 * y + c
        pltpu.make_async_copy(p_ref.at[me], got_ref.at[me], lsem).wait()
        for m in range(1, N_DEV):
            _, blk = _peer(m)
            cp = _remote(p_ref.at[blk], got_ref.at[blk], ssem, rsem, m)
            cp.wait_send()
            cp.wait_recv()

    return pl.pallas_call(
        body, name=name, in_specs=[_HBM, _HBM, _SEM, _SEM, _SEM, _ANY], out_specs=[_HBM, _HBM],
        out_shape=[pltpu.HBM(p.shape, p.dtype), pltpu.HBM(got.shape, got.dtype)],
        input_output_aliases={0: 0, 1: 1}, compiler_params=_EFFECT)(p, got, send_sems, recv_sems, local_sem, after)[1]


def _cast_into_slot(w, layer, me1, name):
    _, r, c = w.shape
    tr = 256 if r % 256 == 0 else r

    def body(me_ref, w_ref, o_ref):
        del me_ref
        o_ref[...] = w_ref[...].astype(BF16)

    return pl.pallas_call(
        body, name=name,
        grid_spec=pltpu.PrefetchScalarGridSpec(
            num_scalar_prefetch=1, grid=(r // tr,),
            in_specs=[pl.BlockSpec((None, tr, c), lambda i, me: (layer, i, 0))],
            out_specs=pl.BlockSpec((None, tr, c), lambda i, me: (me[0], i, 0))),
        out_shape=SDS((N_DEV, r, c), BF16), compiler_params=_cp("arbitrary"))(me1, w)


def _sum8(p, name):
    _, r, c = p.shape

    def body(p_ref, o_ref):
        acc = p_ref[0].astype(F32)
        for k in range(1, N_DEV):
            acc = acc + p_ref[k].astype(F32)
        o_ref[...] = acc

    return pl.pallas_call(body, name=name, grid=(1,), in_specs=[_full(p.shape)], out_specs=_full((r, c)),
                          out_shape=SDS((r, c), F32), compiler_params=_cp("arbitrary"))(p)


def _adamw(w, g, m, v):
    m = ADAM_B1 * m + (1.0 - ADAM_B1) * g
    v = ADAM_B2 * v + (1.0 - ADAM_B2) * (g * g)
    m_hat = m / (1.0 - ADAM_B1 ** ADAM_STEP)
    v_hat = v / (1.0 - ADAM_B2 ** ADAM_STEP)
    delta = -ADAM_LR * (m_hat / (jnp.sqrt(v_hat) + ADAM_EPS) + ADAM_WD * w)
    return delta, m, v


def _adam_shard(parts, w, m, v, layer, prev, name):
    _, r, c = parts.shape
    tr = 256 if r % 256 == 0 else r
    n_prev = 0 if prev is None else 4

    def body(*refs):
        p_ref, w_ref, m_ref, v_ref = refs[:4]
        g_ref, d_ref, nm_ref, nv_ref = refs[4 + n_prev:]
        g = p_ref[0].astype(F32)
        for k in range(1, N_DEV):
            g = g + p_ref[k].astype(F32)
        delta, nm, nv = _adamw(w_ref[...], g, m_ref[...], v_ref[...])
        g_ref[...] = g
        d_ref[...] = delta
        nm_ref[...] = nm
        nv_ref[...] = nv

    blk = pl.BlockSpec((None, tr, c), lambda i: (layer, i, 0))
    return pl.pallas_call(
        body, name=name, grid=(r // tr,),
        in_specs=[pl.BlockSpec((N_DEV, tr, c), lambda i: (0, i, 0)), blk, blk, blk] + [_ANY] * n_prev,
        out_specs=[blk] * 4, out_shape=[SDS(w.shape, F32)] * 4,
        input_output_aliases={4 + k: k for k in range(n_prev)},
        compiler_params=_cp("parallel"))(parts, w, m, v, *(prev or ()))


def _adam_flat(w, g, m, v):
    r = w.shape[0]
    tr = r // 2 if r % 16 == 0 else r

    def body(w_ref, g_ref, m_ref, v_ref, d_ref, nm_ref, nv_ref):
        delta, nm, nv = _adamw(w_ref[...], g_ref[...], m_ref[...], v_ref[...])
        d_ref[...] = delta
        nm_ref[...] = nm
        nv_ref[...] = nv

    blk = pl.BlockSpec((tr, D), lambda i: (i, 0))
    return pl.pallas_call(body, name="adam_small", grid=(r // tr,), in_specs=[blk] * 4, out_specs=[blk] * 3,
                          out_shape=[SDS(w.shape, F32)] * 3, compiler_params=_cp("parallel"))(w, g, m, v)


def _local_step(x, tgt, p, get_w, on_partials=None):
    s = x.shape[0]
    tm = _row_tile(s)
    wsb = p["gmlp_w_s"].astype(BF16)
    wstb = jnp.swapaxes(p["gmlp_w_s"], -1, -2).astype(BF16)
    bsb = jnp.broadcast_to(p["gmlp_b_s"][..., None], p["gmlp_w_s"].shape)
    wrb = p["lru_w_r"].astype(BF16)
    wib = p["lru_w_i"].astype(BF16)
    saved = []
    for l in range(2):
        win = get_w("w_in", l, x)
        z, h1 = _norm_inproj(x, p["norm1_g"][l][None], win, l, tm)
        ya = _gmlp_fwd(z, p["gmlp_ln_g"][l][None], p["gmlp_ln_b"][l][None], wsb[l], bsb[l], l, tm)
        a0, b0, a1, b1 = _lru_gates_fwd(z, p["conv_w"][l], p["conv_b"][l][None], wrb[l], wib[l],
                                        p["lru_b_r"][l], p["lru_b_i"][l], p["lru_lambda"][l], l, tm)
        h0, hr = _lru_scan(a0, b0, a1, b1, False, l)
        wout = get_w("w_out", l, h0)
        x1, mg = _merge_outproj(x, ya, h0, hr, z, wout, l, tm)
        wfi = get_w("w_ffn_in", l, x1)
        wfo = get_w("w_ffn_out", l, x1)
        x2, gu, h2 = _ffn_fwd(x1, p["norm2_g"][l][None], wfi, wfo, l, tm)
        saved.append((x, z, h1, ya, a0, a1, h0, hr, x1, mg, gu, h2, win, wout, wfi, wfo))
        x = x2
    dx, loss, dfg = _loss_head(x, p["final_g"][None], tgt, tm)
    small = [None, None]
    for l in (1, 0):
        x0, z, h1, ya, a0, a1, h0, hr, x1, mg, gu, h2, win, wout, wfi, wfo = saved[l]
        ff, dgu = _ffn_bwd_act(dx, wfo, gu, l, tm)
        d_wfo = _mm_tn(ff, lambda ts: pl.BlockSpec((None, ts, FF_BLK), lambda j, t: (j, t, 0)),
                       dx, lambda ts: pl.BlockSpec((ts, D), lambda j, t: (t, 0)),
                       4, FF_BLK, D, s, f"dw_ffn_out_l{l}")
        dgu8 = dgu.reshape(N_DEV, s, FF_BLK)
        d_wfi = _mm_tn(h2, lambda ts: pl.BlockSpec((ts, D), lambda j, t: (t, 0)),
                       dgu8, lambda ts: pl.BlockSpec((None, ts, FF_BLK), lambda j, t: (j, t, 0)),
                       N_DEV, D, FF_BLK, s, f"dw_ffn_in_l{l}")
        token = None
        if on_partials is not None:
            token = on_partials(l, dict(w_ffn_out=d_wfo.reshape(N_DEV, D_FF // N_DEV, D), w_ffn_in=d_wfi))
        dx1, dg2 = _mm_nt_rms_bwd(
            _tie(dgu8, token), pl.BlockSpec((None, tm, FF_BLK), lambda i, k: (k, i, 0)),
            wfi.reshape(N_DEV, D, FF_BLK), pl.BlockSpec((None, D, FF_BLK), lambda i, k: (k, 0, 0)),
            N_DEV, x1, p["norm2_g"][l][None], dx, f"ffn_bwd_dx_l{l}", tm)
        dz, dya, dh = _outproj_bwd_merge(dx1, wout, ya, h0, hr, z, l, tm)
        d_wout = _mm_tn(mg, lambda ts: pl.BlockSpec((ts, D), lambda j, t: (t, 0)),
                        dx1, lambda ts: pl.BlockSpec((ts, D), lambda j, t: (t, 0)),
                        1, D, D, s, f"dw_out_l{l}")
        g1, g0 = _lru_scan(a1, dh, a0, dh, True, l)
        dxc, dwr, dwi, dbr, dbi, dlam = _lru_gates_bwd(
            z, h0, hr, g0, g1, p["conv_w"][l], p["conv_b"][l][None], wrb[l], wib[l],
            p["lru_b_r"][l], p["lru_b_i"][l], p["lru_lambda"][l], l, tm)
        dz, dcw, dcb = _conv_bwd(dz, dxc, z, p["conv_w"][l], l, tm)
        dz, dws, dbs, dlng, dlnb = _gmlp_bwd(dz, z, dya, p["gmlp_ln_g"][l][None], p["gmlp_ln_b"][l][None],
                                             wsb[l], wstb[l], bsb[l], l, tm)
        d_win = _mm_tn(h1, lambda ts: pl.BlockSpec((ts, D), lambda j, t: (t, 0)),
                       dz, lambda ts: pl.BlockSpec((ts, IN_BLK), lambda j, t: (t, j)),
                       N_DEV, D, IN_BLK, s, f"dw_in_l{l}")
        token = None
        if on_partials is not None:
            token = on_partials(l, dict(w_out=d_wout.reshape(N_DEV, D // N_DEV, D), w_in=d_win))
        dx, dg1 = _mm_nt_rms_bwd(
            _tie(dz, token), pl.BlockSpec((tm, IN_BLK), lambda i, k: (i, k)),
            win, pl.BlockSpec((None, D, IN_BLK), lambda i, k: (k, 0, 0)),
            N_DEV, x0, p["norm1_g"][l][None], dx1, f"inproj_bwd_dx_l{l}", tm)
        small[l] = dict(norm1_g=dg1[0], gmlp_ln_g=dlng[0], gmlp_ln_b=dlnb[0], gmlp_w_s=dws, gmlp_b_s=dbs,
                        conv_w=dcw, conv_b=dcb[0], lru_w_r=dwr, lru_b_r=dbr, lru_w_i=dwi, lru_b_i=dbi,
                        lru_lambda=dlam, norm2_g=dg2[0])
    small_g = {k: jnp.stack([small[0][k], small[1][k]]) for k in small[0]}
    small_g["final_g"] = dfg[0]
    return loss, dx, small_g


_REPL = ["norm1_g", "gmlp_ln_g", "gmlp_ln_b", "gmlp_w_s", "gmlp_b_s", "conv_b", "lru_w_r", "lru_w_i", "norm2_g", "final_g"]
_LANE_SHARDED = ["conv_w", "lru_b_r", "lru_b_i", "lru_lambda"]
_BIG = ["w_in", "w_out", "w_ffn_in", "w_ffn_out"]
_ORDER = ["norm1_g", "w_in", "gmlp_ln_g", "gmlp_ln_b", "gmlp_w_s", "gmlp_b_s", "conv_w", "conv_b", "lru_w_r", "lru_b_r",
          "lru_w_i", "lru_b_i", "lru_lambda", "w_out", "norm2_g", "w_ffn_in", "w_ffn_out", "final_g"]


def _pack(parts, rows):
    flat = jnp.concatenate([a.reshape(-1) for a in parts])
    return jnp.pad(flat, (0, rows * D - flat.shape[0])).reshape(rows, D)


def _unpack(flat, shapes):
    out, off = [], 0
    flat = flat.reshape(-1)
    for shp in shapes:
        n = 1
        for k in shp:
            n *= k
        out.append(flat[off:off + n].reshape(shp))
        off += n
    return out


def kernel(x, norm1_g, w_in, gmlp_ln_g, gmlp_ln_b, gmlp_w_s, gmlp_b_s, conv_w, conv_b, lru_w_r, lru_b_r, lru_w_i, lru_b_i, lru_lambda, w_out, norm2_g, w_ffn_in, w_ffn_out, final_g, loss_target, m_norm1_g, m_w_in, m_gmlp_ln_g, m_gmlp_ln_b, m_gmlp_w_s, m_gmlp_b_s, m_conv_w, m_conv_b, m_lru_w_r, m_lru_b_r, m_lru_w_i, m_lru_b_i, m_lru_lambda, m_w_out, m_norm2_g, m_w_ffn_in, m_w_ffn_out, m_final_g, v_norm1_g, v_w_in, v_gmlp_ln_g, v_gmlp_ln_b, v_gmlp_w_s, v_gmlp_b_s, v_conv_w, v_conv_b, v_lru_w_r, v_lru_b_r, v_lru_w_i, v_lru_b_i, v_lru_lambda, v_w_out, v_norm2_g, v_w_ffn_in, v_w_ffn_out, v_final_g):
    w = dict(norm1_g=norm1_g, w_in=w_in, gmlp_ln_g=gmlp_ln_g, gmlp_ln_b=gmlp_ln_b, gmlp_w_s=gmlp_w_s, gmlp_b_s=gmlp_b_s,
             conv_w=conv_w, conv_b=conv_b, lru_w_r=lru_w_r, lru_b_r=lru_b_r, lru_w_i=lru_w_i, lru_b_i=lru_b_i,
             lru_lambda=lru_lambda, w_out=w_out, norm2_g=norm2_g, w_ffn_in=w_ffn_in, w_ffn_out=w_ffn_out, final_g=final_g)
    mom = dict(norm1_g=m_norm1_g, w_in=m_w_in, gmlp_ln_g=m_gmlp_ln_g, gmlp_ln_b=m_gmlp_ln_b, gmlp_w_s=m_gmlp_w_s,
               gmlp_b_s=m_gmlp_b_s, conv_w=m_conv_w, conv_b=m_conv_b, lru_w_r=m_lru_w_r, lru_b_r=m_lru_b_r,
               lru_w_i=m_lru_w_i, lru_b_i=m_lru_b_i, lru_lambda=m_lru_lambda, w_out=m_w_out, norm2_g=m_norm2_g,
               w_ffn_in=m_w_ffn_in, w_ffn_out=m_w_ffn_out, final_g=m_final_g)
    var = dict(norm1_g=v_norm1_g, w_in=v_w_in, gmlp_ln_g=v_gmlp_ln_g, gmlp_ln_b=v_gmlp_ln_b, gmlp_w_s=v_gmlp_w_s,
               gmlp_b_s=v_gmlp_b_s, conv_w=v_conv_w, conv_b=v_conv_b, lru_w_r=v_lru_w_r, lru_b_r=v_lru_b_r,
               lru_w_i=v_lru_w_i, lru_b_i=v_lru_b_i, lru_lambda=v_lru_lambda, w_out=v_w_out, norm2_g=v_norm2_g,
               w_ffn_in=v_w_ffn_in, w_ffn_out=v_w_ffn_out, final_g=v_final_g)
    xi, yi, ci = _me()
    me = 4 * xi + 2 * yi + ci
    lane0 = me * HD

    me1 = jnp.reshape(me, (1,)).astype(jnp.int32)
    handles = {}
    token = None
    for l in range(2):
        lands = [_cast_into_slot(w[k], l, me1, f"cast_{k}_l{l}") for k in _BIG]
        lands[0] = _tie(lands[0], token)
        started, token = _gather_start(lands, f"gather_start_l{l}")
        handles.update({(k, l): h for k, h in zip(_BIG, started)})
    views = dict(w_in=(N_DEV, D, IN_BLK), w_out=(D, D), w_ffn_in=(2, 4, D, FF_BLK), w_ffn_out=(4, FF_BLK, D))

    def get_w(k, l, after):
        return _gather_wait(handles[(k, l)], after, f"gather_wait_{k}_l{l}").reshape(views[k])

    lane_shapes = [w[k].shape for k in _LANE_SHARDED]
    lane_rows = sum(a[0] * a[1] for a in lane_shapes)
    packed = jnp.concatenate([w[k].reshape(-1, HD) for k in _LANE_SHARDED])
    packed = jnp.pad(packed, ((0, -lane_rows % 8), (0, 0)))
    lanes = _all_gather(packed, "gather_small")
    params = {k: w[k] for k in _REPL}
    off = 0
    for k, shp in zip(_LANE_SHARDED, lane_shapes):
        n = shp[0] * shp[1]
        params[k] = jnp.swapaxes(lanes[:, off:off + n], 0, 1).reshape(shp[0], shp[1], D)
        off += n

    exchanges = {}

    def on_partials(l, parts):
        started, tok = _exchange_start(list(parts.values()), f"exchange_start_{'_'.join(parts)}_l{l}")
        exchanges.update({(k, l): h for k, h in zip(parts, started)})
        return tok

    loss, dx, small_g = _local_step(_tie(x[0], token), loss_target[0], params, get_w, on_partials)

    out = {}
    for k in _BIG:
        res = None
        for l in (1, 0):
            got = _exchange_wait(exchanges[(k, l)], dx, f"exchange_wait_{k}_l{l}")
            res = _adam_shard(got, w[k], mom[k], var[k], l, res, f"adam_{k}_l{l}")
        out[k] = res

    names = _REPL + _LANE_SHARDED
    full_shapes = [small_g[k].shape for k in names]
    n_rows = sum(a.size for a in small_g.values()) // D
    shard_rows = -(-n_rows // (8 * N_DEV)) * 8
    g_flat = _pack([small_g[k] for k in names], shard_rows * N_DEV).reshape(N_DEV, shard_rows, D)
    mine = _sum8(_exchange_partials(g_flat, "exchange_small"), "sum_small")
    g_all = _all_gather(mine, "gather_small_grads")
    grads = dict(zip(names, _unpack(g_all, full_shapes)))
    for k in _LANE_SHARDED:
        grads[k] = lax.dynamic_slice_in_dim(grads[k], lane0, HD, axis=2)
    dev_shapes = [w[k].shape for k in names]
    rows = -(-sum(w[k].size for k in names) // (8 * D)) * 8
    flats = [_pack([src[k] for k in names], rows) for src in (w, grads, mom, var)]
    for nm, flat in zip(("delta", "m", "v"), _adam_flat(*flats)):
        for k, a in zip(names, _unpack(flat, dev_shapes)):
            out.setdefault(k, [grads[k], None, None, None])[("delta", "m", "v").index(nm) + 1] = a

    loss = lax.psum(loss[0, 0], MESH_AXES)
    return (loss, dx[None], *[out[k][0] for k in _ORDER], *[out[k][1] for k in _ORDER],
            *[out[k][2] for k in _ORDER], *[out[k][3] for k in _ORDER])
```

```python
import jax
import jax.numpy as jnp
from jax import lax
from jax.experimental import pallas as pl
from jax.experimental.pallas import tpu as pltpu

F32 = jnp.float32
BF16 = jnp.bfloat16
SDS = jax.ShapeDtypeStruct

D = 1024
N_IN = 6 * D
D_FF = 2816
N_DEV = 8
IN_BLK = N_IN // N_DEV
FF_BLK = 2 * D_FF // N_DEV
HEADS = 8
HD = 128
EPS = 1e-6
LRU_C = 8.0
MESH_AXES = ("x", "y", "c")

ADAM_LR = 0.001
ADAM_B1 = 0.9
ADAM_B2 = 0.999
ADAM_EPS = 1e-08
ADAM_WD = 0.01
ADAM_STEP = 10

VMEM_LIMIT = 48 * 2**20


def _cp(*sem):
    return pltpu.CompilerParams(dimension_semantics=sem, vmem_limit_bytes=VMEM_LIMIT)


def _row_tile(s):
    return 512 if s >= 1024 else s // 2


_GELU_C = 0.7978845608028654


def _gelu(x):
    t = jnp.tanh(_GELU_C * (x + 0.044715 * (x * x * x)))
    return 0.5 * x * (1.0 + t), t


def _gelu_grad(x, t):
    return 0.5 * (1.0 + t) + 0.5 * x * (1.0 - t * t) * (_GELU_C * (1.0 + 0.134145 * (x * x)))


def _sigmoid(x):
    return jax.nn.sigmoid(x)


def _expm1(x):
    u = jnp.exp(x)
    um1 = u - 1.0
    lu = jnp.log(u)
    return jnp.where(um1 == 0.0, x, jnp.where(um1 == -1.0, -1.0, um1 * x / jnp.where(lu == 0.0, 1.0, lu)))


def _softplus(x):
    e = jnp.exp(-jnp.abs(x))
    w = 1.0 + e
    l1p = jnp.where(w == 1.0, e, jnp.log(w) * e / jnp.where(w == 1.0, 1.0, w - 1.0))
    return jnp.maximum(x, 0.0) + l1p


def _rms_fwd(x, g):
    r = lax.rsqrt(jnp.mean(x * x, axis=-1, keepdims=True) + EPS)
    return x * r * g


def _rms_bwd(x, g, dh):
    r = lax.rsqrt(jnp.mean(x * x, axis=-1, keepdims=True) + EPS)
    xh = x * r
    dxh = dh * g
    dx = r * (dxh - xh * jnp.mean(dxh * xh, axis=-1, keepdims=True))
    dg = jnp.sum(dh * xh, axis=0, keepdims=True)
    return dx, dg


def _dot(a, b):
    return jnp.dot(a, b, preferred_element_type=F32)


def _dot_nt(a, b):
    return lax.dot_general(a, b, (((1,), (1,)), ((), ())), preferred_element_type=F32)


def _dot_tn(a, b):
    return lax.dot_general(a, b, (((0,), (0,)), ((), ())), preferred_element_type=F32)


def _taps(prev, cur, nxt, tm):
    ext = jnp.concatenate([prev, cur, nxt], axis=0)
    n = tm + 16
    sl = slice(8, 8 + tm)
    return (pltpu.roll(ext, 2, 0)[sl], pltpu.roll(ext, 1, 0)[sl], cur,
            pltpu.roll(ext, n - 1, 0)[sl], pltpu.roll(ext, n - 2, 0)[sl])


def _halo_specs(tm, s, col):
    nb8 = s // 8
    r8 = tm // 8
    return (pl.BlockSpec((8, D), lambda i: (jnp.maximum(i * r8 - 1, 0), col)),
            pl.BlockSpec((tm, D), lambda i: (i, col)),
            pl.BlockSpec((8, D), lambda i: (jnp.minimum((i + 1) * r8, nb8 - 1), col)))


def _halo_flags(nt):
    i = pl.program_id(0)
    return (i > 0).astype(F32), (i < nt - 1).astype(F32)


def _full(shape):
    nd = len(shape)
    return pl.BlockSpec(shape, lambda *_: (0,) * nd)


def _norm_inproj(x, g, w, layer, tm):
    s = x.shape[0]

    def body(x_ref, g_ref, w_ref, z_ref, h_ref, hs):
        @pl.when(pl.program_id(1) == 0)
        def _():
            h = _rms_fwd(x_ref[...], g_ref[...]).astype(BF16)
            hs[...] = h
            h_ref[...] = h
        z_ref[...] = _dot(hs[...], w_ref[...])

    return pl.pallas_call(
        body, name=f"norm_inproj_l{layer}", grid=(s // tm, N_DEV),
        in_specs=[pl.BlockSpec((tm, D), lambda i, j: (i, 0)), _full((1, D)),
                  pl.BlockSpec((None, D, IN_BLK), lambda i, j: (j, 0, 0))],
        out_specs=[pl.BlockSpec((tm, IN_BLK), lambda i, j: (i, j)), pl.BlockSpec((tm, D), lambda i, j: (i, 0))],
        out_shape=[SDS((s, N_IN), F32), SDS((s, D), BF16)],
        scratch_shapes=[pltpu.VMEM((tm, D), BF16)],
        compiler_params=_cp("parallel", "arbitrary"))(x, g, w)


def _gmlp_fwd(z, lng, lnb, ws, bsb, layer, tm):
    s = z.shape[0]

    def body(zu_ref, zv_ref, lng_ref, lnb_ref, ws_ref, bsb_ref, ya_ref):
        u, _ = _gelu(zu_ref[...])
        gv, _ = _gelu(zv_ref[...])
        xc = gv - jnp.mean(gv, axis=-1, keepdims=True)
        rstd = lax.rsqrt(jnp.mean(xc * xc, axis=-1, keepdims=True) + EPS)
        vb = (xc * rstd * lng_ref[...] + lnb_ref[...]).astype(BF16)
        for c in range(tm // HD):
            rs = slice(c * HD, (c + 1) * HD)
            for g in range(HEADS):
                cs = slice(g * HD, (g + 1) * HD)
                mixed = _dot(ws_ref[g], vb[rs, cs]) + bsb_ref[g]
                ya_ref[rs, cs] = u[rs, cs] * mixed

    return pl.pallas_call(
        body, name=f"gmlp_fwd_l{layer}", grid=(s // tm,),
        in_specs=[pl.BlockSpec((tm, D), lambda i: (i, 0)), pl.BlockSpec((tm, D), lambda i: (i, 1)),
                  _full((1, D)), _full((1, D)), _full((HEADS, HD, HD)), _full((HEADS, HD, HD))],
        out_specs=pl.BlockSpec((tm, D), lambda i: (i, 0)),
        out_shape=SDS((s, D), F32),
        compiler_params=_cp("parallel"))(z, z, lng, lnb, ws, bsb)


def _conv(taps, cw_ref, cb_ref):
    _, m1, c0, p1, p2 = taps
    return cb_ref[...] + m1 * cw_ref[0:1, :] + c0 * cw_ref[1:2, :] + p1 * cw_ref[2:3, :] + p2 * cw_ref[3:4, :]


def _heads_dot(xb, w_ref, d):
    return jnp.concatenate([_dot(xb[:, h * HD:(h + 1) * HD], w_ref[d, h]) for h in range(HEADS)], axis=1)


def _lru_gates(xc, xb, wr_ref, wi_ref, br_ref, bi_ref, lam_ref, d):
    sp = _softplus(-lam_ref[d:d + 1, :])
    r = _sigmoid(_heads_dot(xb, wr_ref, d) + br_ref[d:d + 1, :])
    ig = _sigmoid(_heads_dot(xb, wi_ref, d) + bi_ref[d:d + 1, :])
    la = (-LRU_C) * r * sp
    a = jnp.exp(la)
    mult = jnp.sqrt(jnp.maximum(-_expm1(2.0 * la), 0.0))
    return sp, r, ig, a, mult


def _lru_gates_fwd(z, cw, cb, wr, wi, br, bi, lam, layer, tm):
    s = z.shape[0]
    nt = s // tm

    def body(zp_ref, zc_ref, zn_ref, cw_ref, cb_ref, wr_ref, wi_ref, br_ref, bi_ref, lam_ref,
             a0_ref, b0_ref, a1_ref, b1_ref):
        fp, fn = _halo_flags(nt)
        xc = _conv(_taps(zp_ref[...] * fp, zc_ref[...], zn_ref[...] * fn, tm), cw_ref, cb_ref)
        xb = xc.astype(BF16)
        for d, (a_ref, b_ref) in enumerate(((a0_ref, b0_ref), (a1_ref, b1_ref))):
            _, _, ig, a, mult = _lru_gates(xc, xb, wr_ref, wi_ref, br_ref, bi_ref, lam_ref, d)
            a_ref[...] = a
            b_ref[...] = mult * (ig * xc)

    tile = pl.BlockSpec((tm, D), lambda i: (i, 0))
    return pl.pallas_call(
        body, name=f"lru_gates_fwd_l{layer}", grid=(nt,),
        in_specs=[*_halo_specs(tm, s, 2), _full((4, D)), _full((1, D)),
                  _full((2, HEADS, HD, HD)), _full((2, HEADS, HD, HD)), _full((2, D)), _full((2, D)), _full((2, D))],
        out_specs=[tile] * 4, out_shape=[SDS((s, D), F32)] * 4,
        compiler_params=_cp("parallel"))(z, z, z, cw, cb, wr, wi, br, bi, lam)


def _scan_group(a, x, c, reverse, bwd):
    row = lax.broadcasted_iota(jnp.int32, a.shape, 0)
    b = a * x if bwd else x
    for d in (1, 2, 4):
        keep = (row < 8 - d) if reverse else (row >= d)
        sh = 8 - d if reverse else d
        a_s = jnp.where(keep, pltpu.roll(a, sh, 0), 1.0)
        b_s = jnp.where(keep, pltpu.roll(b, sh, 0), 0.0)
        b = a * b_s + b
        a = a * a_s
    h = b + a * c
    new_c = h[0:1, :] if reverse else h[7:8, :]
    if not bwd:
        return h, new_c
    if reverse:
        prev = jnp.where(row < 7, pltpu.roll(h, 7, 0), c)
    else:
        prev = jnp.where(row >= 1, pltpu.roll(h, 1, 0), c)
    return x + prev, new_c


def _lru_scan(a_f, x_f, a_r, x_r, bwd, layer):
    s = a_f.shape[0]
    ts = min(1024, s // 2)
    cb = 512
    nt = s // ts
    ng = ts // 8

    def body(af_ref, xf_ref, ar_ref, xr_ref, of_ref, or_ref, cf, cr):
        @pl.when(pl.program_id(1) == 0)
        def _():
            cf[...] = jnp.zeros_like(cf)
            cr[...] = jnp.zeros_like(cr)

        def step(j, carry):
            c_f, c_r = carry
            rf = pl.multiple_of(j * 8, 8)
            rr = pl.multiple_of((ng - 1 - j) * 8, 8)
            o, c_f = _scan_group(af_ref[pl.ds(rf, 8), :], xf_ref[pl.ds(rf, 8), :], c_f, False, bwd)
            of_ref[pl.ds(rf, 8), :] = o
            o, c_r = _scan_group(ar_ref[pl.ds(rr, 8), :], xr_ref[pl.ds(rr, 8), :], c_r, True, bwd)
            or_ref[pl.ds(rr, 8), :] = o
            return c_f, c_r

        c_f, c_r = lax.fori_loop(0, ng, step, (cf[0:1, :], cr[0:1, :]), unroll=2)
        cf[...] = jnp.broadcast_to(c_f, cf.shape)
        cr[...] = jnp.broadcast_to(c_r, cr.shape)

    fwd = pl.BlockSpec((ts, cb), lambda c, t: (t, c))
    rev = pl.BlockSpec((ts, cb), lambda c, t: (nt - 1 - t, c))
    return pl.pallas_call(
        body, name=f"lru_scan_{'bwd' if bwd else 'fwd'}_l{layer}", grid=(D // cb, nt),
        in_specs=[fwd, fwd, rev, rev], out_specs=[fwd, rev],
        out_shape=[SDS((s, D), F32)] * 2,
        scratch_shapes=[pltpu.VMEM((8, cb), F32), pltpu.VMEM((8, cb), F32)],
        compiler_params=_cp("parallel", "arbitrary"))(a_f, x_f, a_r, x_r)


def _merge_outproj(x, ya, h0, h1, z, wo, layer, tm):
    s = x.shape[0]

    def body(x_ref, ya_ref, h0_ref, h1_ref, zg_ref, za_ref, zb_ref, wo_ref, x1_ref, mg_ref):
        gg, _ = _gelu(zg_ref[...])
        yb = (h0_ref[...] + h1_ref[...]) * gg
        mb = (_sigmoid(za_ref[...]) * ya_ref[...] + _sigmoid(zb_ref[...]) * yb).astype(BF16)
        mg_ref[...] = mb
        x1_ref[...] = x_ref[...] + _dot(mb, wo_ref[...])

    tile = pl.BlockSpec((tm, D), lambda i: (i, 0))
    return pl.pallas_call(
        body, name=f"merge_outproj_l{layer}", grid=(s // tm,),
        in_specs=[tile, tile, tile, tile] + [pl.BlockSpec((tm, D), lambda i, c=c: (i, c)) for c in (3, 4, 5)]
        + [_full((D, D))],
        out_specs=[tile, tile], out_shape=[SDS((s, D), F32), SDS((s, D), BF16)],
        compiler_params=_cp("parallel"))(x, ya, h0, h1, z, z, z, wo)


def _ffn_fwd(x1, g, wfi, wfo, layer, tm):
    s = x1.shape[0]

    def body(x_ref, g_ref, wi_ref, wo_ref, x2_ref, gu_ref, h_ref, hs, acc):
        k = pl.program_id(1)

        @pl.when(k == 0)
        def _():
            h = _rms_fwd(x_ref[...], g_ref[...]).astype(BF16)
            hs[...] = h
            h_ref[...] = h
            acc[...] = jnp.zeros_like(acc)

        gate = _dot(hs[...], wi_ref[0])
        up = _dot(hs[...], wi_ref[1])
        gu_ref[0] = gate
        gu_ref[1] = up
        acc[...] += _dot((gate * _sigmoid(gate) * up).astype(BF16), wo_ref[...])

        @pl.when(k == 3)
        def _():
            x2_ref[...] = x_ref[...] + acc[...]

    return pl.pallas_call(
        body, name=f"ffn_fwd_l{layer}", grid=(s // tm, 4),
        in_specs=[pl.BlockSpec((tm, D), lambda i, k: (i, 0)), _full((1, D)),
                  pl.BlockSpec((2, None, D, FF_BLK), lambda i, k: (0, k, 0, 0)),
                  pl.BlockSpec((None, FF_BLK, D), lambda i, k: (k, 0, 0))],
        out_specs=[pl.BlockSpec((tm, D), lambda i, k: (i, 0)),
                   pl.BlockSpec((2, None, tm, FF_BLK), lambda i, k: (0, k, i, 0)),
                   pl.BlockSpec((tm, D), lambda i, k: (i, 0))],
        out_shape=[SDS((s, D), F32), SDS((2, 4, s, FF_BLK), F32), SDS((s, D), BF16)],
        scratch_shapes=[pltpu.VMEM((tm, D), BF16), pltpu.VMEM((tm, D), F32)],
        compiler_params=_cp("parallel", "arbitrary"))(x1, g, wfi, wfo)


def _loss_head(x, g, tgt, tm):
    s = x.shape[0]

    def body(x_ref, g_ref, t_ref, dx_ref, loss_ref, dg_ref):
        @pl.when(pl.program_id(0) == 0)
        def _():
            loss_ref[...] = jnp.zeros_like(loss_ref)
            dg_ref[...] = jnp.zeros_like(dg_ref)

        x = x_ref[...]
        gv = g_ref[...]
        e = _rms_fwd(x, gv) - t_ref[...]
        rows = jnp.sum(e * e, axis=-1, keepdims=True)
        loss_ref[...] += (0.5 / D) * jnp.sum(rows, axis=0, keepdims=True)
        dx, dg = _rms_bwd(x, gv, e * (1.0 / D))
        dx_ref[...] = dx
        dg_ref[...] += dg

    tile = pl.BlockSpec((tm, D), lambda i: (i, 0))
    return pl.pallas_call(
        body, name="loss_head", grid=(s // tm,),
        in_specs=[tile, _full((1, D)), tile],
        out_specs=[tile, _full((1, 1)), _full((1, D))],
        out_shape=[SDS((s, D), F32), SDS((1, 1), F32), SDS((1, D), F32)],
        compiler_params=_cp("arbitrary"))(x, g, tgt)


def _ffn_bwd_act(dx2, wfo, gu, layer, tm):
    s = dx2.shape[0]

    def body(dx_ref, wo_ref, gu_ref, ff_ref, dgu_ref):
        dff = _dot_nt(dx_ref[...].astype(BF16), wo_ref[...])
        gate = gu_ref[0]
        up = gu_ref[1]
        sg = _sigmoid(gate)
        sl = gate * sg
        ff_ref[...] = (sl * up).astype(BF16)
        dgu_ref[0] = (dff * up * (sg * (1.0 + gate * (1.0 - sg)))).astype(BF16)
        dgu_ref[1] = (dff * sl).astype(BF16)

    blk = pl.BlockSpec((2, None, tm, FF_BLK), lambda i, k: (0, k, i, 0))
    return pl.pallas_call(
        body, name=f"ffn_bwd_act_l{layer}", grid=(s // tm, 4),
        in_specs=[pl.BlockSpec((tm, D), lambda i, k: (i, 0)),
                  pl.BlockSpec((None, FF_BLK, D), lambda i, k: (k, 0, 0)), blk],
        out_specs=[pl.BlockSpec((None, tm, FF_BLK), lambda i, k: (k, i, 0)), blk],
        out_shape=[SDS((4, s, FF_BLK), BF16), SDS((2, 4, s, FF_BLK), BF16)],
        compiler_params=_cp("parallel", "arbitrary"))(dx2, wfo, gu)


def _mm_nt_rms_bwd(a, a_spec, w, w_spec, nk, x, g, dres, name, tm):
    s = x.shape[0]

    def body(a_ref, w_ref, x_ref, g_ref, dres_ref, dx_ref, dg_ref, acc):
        i = pl.program_id(0)
        k = pl.program_id(1)

        @pl.when(k == 0)
        def _():
            acc[...] = jnp.zeros_like(acc)

        @pl.when(jnp.logical_and(i == 0, k == 0))
        def _():
            dg_ref[...] = jnp.zeros_like(dg_ref)

        acc[...] += _dot_nt(a_ref[...], w_ref[...])

        @pl.when(k == nk - 1)
        def _():
            dx, dg = _rms_bwd(x_ref[...], g_ref[...], acc[...])
            dx_ref[...] = dres_ref[...] + dx
            dg_ref[...] += dg

    tile = pl.BlockSpec((tm, D), lambda i, k: (i, 0))
    return pl.pallas_call(
        body, name=name, grid=(s // tm, nk),
        in_specs=[a_spec, w_spec, tile, _full((1, D)), tile],
        out_specs=[tile, _full((1, D))], out_shape=[SDS((s, D), F32), SDS((1, D), F32)],
        scratch_shapes=[pltpu.VMEM((tm, D), F32)],
        compiler_params=_cp("arbitrary", "arbitrary"))(a, w, x, g, dres)


def _mm_tn(a, a_spec, b, b_spec, nb, kb, bn, s, name):
    ts = _row_tile(s)
    ns = s // ts

    def body(a_ref, b_ref, o_ref, acc):
        t = pl.program_id(1)

        @pl.when(t == 0)
        def _():
            acc[...] = jnp.zeros_like(acc)

        acc[...] += _dot_tn(a_ref[...].astype(BF16), b_ref[...].astype(BF16))

        @pl.when(t == ns - 1)
        def _():
            o_ref[...] = acc[...].astype(BF16)

    return pl.pallas_call(
        body, name=name, grid=(nb, ns), in_specs=[a_spec(ts), b_spec(ts)],
        out_specs=pl.BlockSpec((None, kb, bn), lambda j, t: (j, 0, 0)),
        out_shape=SDS((nb, kb, bn), BF16),
        scratch_shapes=[pltpu.VMEM((kb, bn), F32)],
        compiler_params=_cp("parallel", "arbitrary"))(a, b)


def _outproj_bwd_merge(dx1, wo, ya, h0, h1, z, layer, tm):
    s = dx1.shape[0]

    def body(dx_ref, wo_ref, ya_ref, h0_ref, h1_ref, zg_ref, za_ref, zb_ref, dz_ref, dya_ref, dh_ref):
        dm = _dot_nt(dx_ref[...].astype(BF16), wo_ref[...])
        sa = _sigmoid(za_ref[...])
        sb = _sigmoid(zb_ref[...])
        zg = zg_ref[...]
        gg, tg = _gelu(zg)
        hs = h0_ref[...] + h1_ref[...]
        dyb = dm * sb
        dya_ref[...] = dm * sa
        dh_ref[...] = dyb * gg
        dz_ref[:, 0:D] = (dyb * hs * _gelu_grad(zg, tg)).astype(BF16)
        dz_ref[:, D:2 * D] = (dm * ya_ref[...] * (sa * (1.0 - sa))).astype(BF16)
        dz_ref[:, 2 * D:3 * D] = (dm * (hs * gg) * (sb * (1.0 - sb))).astype(BF16)

    tile = pl.BlockSpec((tm, D), lambda i: (i, 0))
    return pl.pallas_call(
        body, name=f"outproj_bwd_merge_l{layer}", grid=(s // tm,),
        in_specs=[tile, _full((D, D)), tile, tile, tile]
        + [pl.BlockSpec((tm, D), lambda i, c=c: (i, c)) for c in (3, 4, 5)],
        out_specs=[pl.BlockSpec((tm, 3 * D), lambda i: (i, 1)), tile, tile],
        out_shape=[SDS((s, N_IN), BF16), SDS((s, D), F32), SDS((s, D), F32)],
        compiler_params=_cp("parallel"))(dx1, wo, ya, h0, h1, z, z, z)


def _lru_gates_bwd(z, h0, h1, g0, g1, cw, cb, wr, wi, br, bi, lam, layer, tm):
    s = z.shape[0]
    nt = s // tm

    def body(zp_ref, zc_ref, zn_ref, h0p_ref, h0_ref, h1_ref, h1n_ref, g0_ref, g1_ref,
             cw_ref, cb_ref, wr_ref, wi_ref, br_ref, bi_ref, lam_ref,
             dxc_ref, dwr_ref, dwi_ref, dbr_ref, dbi_ref, dlam_ref):
        i = pl.program_id(0)
        fp, fn = _halo_flags(nt)

        @pl.when(i == 0)
        def _():
            for r in (dwr_ref, dwi_ref, dbr_ref, dbi_ref, dlam_ref):
                r[...] = jnp.zeros_like(r)

        xc = _conv(_taps(zp_ref[...] * fp, zc_ref[...], zn_ref[...] * fn, tm), cw_ref, cb_ref)
        xb = xc.astype(BF16)
        zeros8 = jnp.zeros((8, D), F32)
        h_prev = _taps(h0p_ref[...] * fp, h0_ref[...], zeros8, tm)[1]
        h_next = _taps(zeros8, h1_ref[...], h1n_ref[...] * fn, tm)[3]
        dxc = jnp.zeros((tm, D), F32)
        for d, (g_ref, hsh) in enumerate(((g0_ref, h_prev), (g1_ref, h_next))):
            sp, r, ig, a, mult = _lru_gates(xc, xb, wr_ref, wi_ref, br_ref, bi_ref, lam_ref, d)
            db = g_ref[...]
            da = db * hsh
            dmult = db * (ig * xc)
            di = db * (mult * xc)
            dxc = dxc + db * (mult * ig)
            dla = da * a - dmult * jnp.where(mult > 0.0, a * a / jnp.where(mult > 0.0, mult, 1.0), 0.0)
            dlam_ref[d:d + 1, :] += jnp.sum(dla * r, axis=0, keepdims=True) * (-LRU_C)
            dpr = dla * sp * (-LRU_C) * (r * (1.0 - r))
            dpi = di * (ig * (1.0 - ig))
            dbr_ref[d:d + 1, :] += jnp.sum(dpr, axis=0, keepdims=True)
            dbi_ref[d:d + 1, :] += jnp.sum(dpi, axis=0, keepdims=True)
            dprb = dpr.astype(BF16)
            dpib = dpi.astype(BF16)
            parts = []
            for h in range(HEADS):
                cs = slice(h * HD, (h + 1) * HD)
                dwr_ref[d, h] += _dot_tn(xb[:, cs], dprb[:, cs])
                dwi_ref[d, h] += _dot_tn(xb[:, cs], dpib[:, cs])
                parts.append(_dot_nt(dprb[:, cs], wr_ref[d, h]) + _dot_nt(dpib[:, cs], wi_ref[d, h]))
            dxc = dxc + jnp.concatenate(parts, axis=1)
        dxc_ref[...] = dxc

        @pl.when(i == nt - 1)
        def _():
            dlam_ref[...] = dlam_ref[...] * (-_sigmoid(-lam_ref[...]))

    tile = pl.BlockSpec((tm, D), lambda i: (i, 0))
    zp, zc, zn = _halo_specs(tm, s, 2)
    hp, hc, hn = _halo_specs(tm, s, 0)
    wspec = _full((2, HEADS, HD, HD))
    return pl.pallas_call(
        body, name=f"lru_gates_bwd_l{layer}", grid=(nt,),
        in_specs=[zp, zc, zn, hp, hc, hc, hn, tile, tile, _full((4, D)), _full((1, D)),
                  wspec, wspec, _full((2, D)), _full((2, D)), _full((2, D))],
        out_specs=[tile, wspec, wspec, _full((2, D)), _full((2, D)), _full((2, D))],
        out_shape=[SDS((s, D), F32), SDS((2, HEADS, HD, HD), F32), SDS((2, HEADS, HD, HD), F32),
                   SDS((2, D), F32), SDS((2, D), F32), SDS((2, D), F32)],
        compiler_params=_cp("arbitrary"))(z, z, z, h0, h0, h1, h1, g0, g1, cw, cb, wr, wi, br, bi, lam)


def _conv_bwd(dz, dxc, z, cw, layer, tm):
    s = z.shape[0]
    nt = s // tm

    def body(dz_in, dp_ref, dc_ref, dn_ref, zp_ref, zc_ref, zn_ref, cw_ref, dz_ref, dcw_ref, dcb_ref):
        del dz_in
        fp, fn = _halo_flags(nt)

        @pl.when(pl.program_id(0) == 0)
        def _():
            dcw_ref[...] = jnp.zeros_like(dcw_ref)
            dcb_ref[...] = jnp.zeros_like(dcb_ref)

        dxc = dc_ref[...]
        dm2, dm1, _, dp1, _ = _taps(dp_ref[...] * fp, dxc, dn_ref[...] * fn, tm)
        dz_ref[...] = (cw_ref[0:1, :] * dp1 + cw_ref[1:2, :] * dxc + cw_ref[2:3, :] * dm1
                       + cw_ref[3:4, :] * dm2).astype(BF16)
        _, zm1, z0, zp1, zp2 = _taps(zp_ref[...] * fp, zc_ref[...], zn_ref[...] * fn, tm)
        for k, zt in enumerate((zm1, z0, zp1, zp2)):
            dcw_ref[k:k + 1, :] += jnp.sum(dxc * zt, axis=0, keepdims=True)
        dcb_ref[...] += jnp.sum(dxc, axis=0, keepdims=True)

    return pl.pallas_call(
        body, name=f"conv_bwd_l{layer}", grid=(nt,),
        in_specs=[pl.BlockSpec(memory_space=pl.ANY), *_halo_specs(tm, s, 0), *_halo_specs(tm, s, 2), _full((4, D))],
        out_specs=[pl.BlockSpec((tm, D), lambda i: (i, 2)), _full((4, D)), _full((1, D))],
        out_shape=[SDS((s, N_IN), BF16), SDS((4, D), F32), SDS((1, D), F32)],
        input_output_aliases={0: 0},
        compiler_params=_cp("arbitrary"))(dz, dxc, dxc, dxc, z, z, z, cw)


def _gmlp_bwd(dz, z, dya, lng, lnb, ws, wst, bsb, layer, tm):
    s = z.shape[0]
    nt = s // tm

    def body(dz_in, zu_ref, zv_ref, dya_ref, lng_ref, lnb_ref, ws_ref, wst_ref, bsb_ref,
             dz_ref, dws_ref, dbs_ref, dlng_ref, dlnb_ref, du_s, dv_s, dbs_acc):
        del dz_in
        i = pl.program_id(0)

        @pl.when(i == 0)
        def _():
            for r in (dws_ref, dlng_ref, dlnb_ref, dbs_acc):
                r[...] = jnp.zeros_like(r)

        zu = zu_ref[...]
        zv = zv_ref[...]
        u, tu = _gelu(zu)
        gv, tv = _gelu(zv)
        xc = gv - jnp.mean(gv, axis=-1, keepdims=True)
        rstd = lax.rsqrt(jnp.mean(xc * xc, axis=-1, keepdims=True) + EPS)
        xh = xc * rstd
        lng_v = lng_ref[...]
        vb = (xh * lng_v + lnb_ref[...]).astype(BF16)
        dya = dya_ref[...]
        for c in range(tm // HD):
            rs = slice(c * HD, (c + 1) * HD)
            for g in range(HEADS):
                cs = slice(g * HD, (g + 1) * HD)
                vblk = vb[rs, cs]
                mixed = _dot(ws_ref[g], vblk) + bsb_ref[g]
                du_s[rs, cs] = dya[rs, cs] * mixed
                dmx = dya[rs, cs] * u[rs, cs]
                dbs_acc[g] += dmx
                dmxb = dmx.astype(BF16)
                dws_ref[g] += _dot_nt(dmxb, vblk)
                dv_s[rs, cs] = _dot(wst_ref[g], dmxb)
        dv = dv_s[...]
        dlng_ref[...] += jnp.sum(dv * xh, axis=0, keepdims=True)
        dlnb_ref[...] += jnp.sum(dv, axis=0, keepdims=True)
        dxh = dv * lng_v
        dgv = rstd * (dxh - jnp.mean(dxh, axis=-1, keepdims=True)
                      - xh * jnp.mean(dxh * xh, axis=-1, keepdims=True))
        dz_ref[:, 0:D] = (du_s[...] * _gelu_grad(zu, tu)).astype(BF16)
        dz_ref[:, D:2 * D] = (dgv * _gelu_grad(zv, tv)).astype(BF16)

        @pl.when(i == nt - 1)
        def _():
            for g in range(HEADS):
                dbs_ref[g:g + 1, :] = jnp.sum(dbs_acc[g].T, axis=0, keepdims=True)

    tile = pl.BlockSpec((tm, D), lambda i: (i, 0))
    wspec = _full((HEADS, HD, HD))
    return pl.pallas_call(
        body, name=f"gmlp_bwd_l{layer}", grid=(nt,),
        in_specs=[pl.BlockSpec(memory_space=pl.ANY), tile, pl.BlockSpec((tm, D), lambda i: (i, 1)), tile,
                  _full((1, D)), _full((1, D)), wspec, wspec, wspec],
        out_specs=[pl.BlockSpec((tm, 2 * D), lambda i: (i, 0)), wspec, _full((HEADS, HD)), _full((1, D)), _full((1, D))],
        out_shape=[SDS((s, N_IN), BF16), SDS((HEADS, HD, HD), F32), SDS((HEADS, HD), F32),
                   SDS((1, D), F32), SDS((1, D), F32)],
        scratch_shapes=[pltpu.VMEM((tm, D), F32), pltpu.VMEM((tm, D), F32), pltpu.VMEM((HEADS, HD, HD), F32)],
        input_output_aliases={0: 0},
        compiler_params=_cp("arbitrary"))(dz, z, z, dya, lng, lnb, ws, wst, bsb)


def _me():
    return lax.axis_index("x"), lax.axis_index("y"), lax.axis_index("c")


def _peer(m):
    x, y, c = _me()
    px = 1 - x if m & 4 else x
    py = 1 - y if m & 2 else y
    pc = 1 - c if m & 1 else c
    return (px, py, pc), 4 * px + 2 * py + pc


_ANY = pl.BlockSpec(memory_space=pl.ANY)
_EXCHANGE_SEMS = [pltpu.SemaphoreType.DMA((N_DEV - 1,)), pltpu.SemaphoreType.DMA((N_DEV - 1,)), pltpu.SemaphoreType.DMA(())]


def _all_gather(v, name):
    def body(v_ref, o_ref, send_sems, recv_sems, local_sem):
        x, y, c = _me()
        me = 4 * x + 2 * y + c
        local = pltpu.make_async_copy(v_ref, o_ref.at[me], local_sem)
        local.start()
        sends = []
        for m in range(1, N_DEV):
            dev, _ = _peer(m)
            cp = pltpu.make_async_remote_copy(v_ref, o_ref.at[me], send_sems.at[m - 1], recv_sems.at[m - 1],
                                              device_id=dev, device_id_type=pl.DeviceIdType.MESH)
            cp.start()
            sends.append(cp)
        for m in range(1, N_DEV):
            dev, blk = _peer(m)
            pltpu.make_async_remote_copy(v_ref, o_ref.at[blk], send_sems.at[m - 1], recv_sems.at[m - 1],
                                         device_id=dev, device_id_type=pl.DeviceIdType.MESH).wait_recv()
        for cp in sends:
            cp.wait_send()
        local.wait()

    return pl.pallas_call(
        body, name=name, in_specs=[_ANY], out_specs=_ANY,
        out_shape=SDS((N_DEV,) + v.shape, v.dtype), scratch_shapes=_EXCHANGE_SEMS)(v)


def _exchange_partials(p, name):
    def body(p_ref, o_ref, send_sems, recv_sems, local_sem):
        x, y, c = _me()
        me = 4 * x + 2 * y + c
        local = pltpu.make_async_copy(p_ref.at[me], o_ref.at[me], local_sem)
        local.start()
        sends = []
        for m in range(1, N_DEV):
            dev, blk = _peer(m)
            cp = pltpu.make_async_remote_copy(p_ref.at[blk], o_ref.at[me], send_sems.at[m - 1], recv_sems.at[m - 1],
                                              device_id=dev, device_id_type=pl.DeviceIdType.MESH)
            cp.start()
            sends.append(cp)
        for m in range(1, N_DEV):
            dev, blk = _peer(m)
            pltpu.make_async_remote_copy(p_ref.at[blk], o_ref.at[blk], send_sems.at[m - 1], recv_sems.at[m - 1],
                                         device_id=dev, device_id_type=pl.DeviceIdType.MESH).wait_recv()
        for cp in sends:
            cp.wait_send()
        local.wait()

    return pl.pallas_call(
        body, name=name, in_specs=[_ANY], out_specs=_ANY,
        out_shape=SDS(p.shape, p.dtype), scratch_shapes=_EXCHANGE_SEMS)(p)


_HBM = pl.BlockSpec(memory_space=pltpu.HBM)
_SEM = pl.BlockSpec(memory_space=pltpu.SEMAPHORE)
_EFFECT = pltpu.CompilerParams(has_side_effects=pltpu.SideEffectType.DATAFLOW_SIDE_EFFECTING)
_PEER_SEMS = pltpu.SemaphoreType.DMA((N_DEV - 1,))


def _in_hbm(a):
    return pltpu.with_memory_space_constraint(a, pltpu.HBM)


def _remote(src, dst, send_sems, recv_sems, m):
    dev, _ = _peer(m)
    return pltpu.make_async_remote_copy(src, dst, send_sems.at[m - 1], recv_sems.at[m - 1],
                                        device_id=dev, device_id_type=pl.DeviceIdType.MESH)


def _gather_start(lands, after, name):
    n = len(lands)

    def body(*refs):
        land = refs[:n]
        sems = refs[n + 1:3 * n + 1]
        token = refs[-1]
        x, y, c = _me()
        me = 4 * x + 2 * y + c
        for t in range(n):
            for m in range(1, N_DEV):
                _remote(land[t].at[me], land[t].at[me], sems[2 * t], sems[2 * t + 1], m).start()
        token[...] = jnp.zeros_like(token)

    res = pl.pallas_call(
        body, name=name, in_specs=[_HBM] * n + [_ANY],
        out_specs=[_SEM] * (2 * n) + [_HBM] * n + [pl.BlockSpec(memory_space=pltpu.VMEM)],
        out_shape=[_PEER_SEMS] * (2 * n) + [pltpu.HBM(a.shape, a.dtype) for a in lands] + [SDS((8, 128), F32)],
        input_output_aliases={t: 2 * n + t for t in range(n)},
        compiler_params=_EFFECT)(*[_in_hbm(a) for a in lands], after)
    return [(res[2 * t], res[2 * t + 1], res[2 * n + t]) for t in range(n)], res[-1]


def _gather_wait(handle, after, name):
    send_sems, recv_sems, land = handle

    def body(land_ref, ssem, rsem, after_ref, out_ref):
        del after_ref, out_ref
        x, y, c = _me()
        me = 4 * x + 2 * y + c
        for m in range(1, N_DEV):
            _, blk = _peer(m)
            cp = _remote(land_ref.at[me], land_ref.at[blk], ssem, rsem, m)
            cp.wait_send()
            cp.wait_recv()

    return pl.pallas_call(
        body, name=name, in_specs=[_HBM, _SEM, _SEM, _ANY], out_specs=_HBM,
        out_shape=pltpu.HBM(land.shape, land.dtype), input_output_aliases={0: 0},
        compiler_params=_EFFECT)(land, send_sems, recv_sems, after)


def _exchange_start(ps, name):
    n = len(ps)

    def body(*refs):
        p = refs[:n]
        got = refs[n:2 * n]
        sems = refs[2 * n:5 * n]
        token = refs[-1]
        x, y, c = _me()
        me = 4 * x + 2 * y + c
        for t in range(n):
            pltpu.make_async_copy(p[t].at[me], got[t].at[me], sems[3 * t + 2]).start()
            for m in range(1, N_DEV):
                _, blk = _peer(m)
                _remote(p[t].at[blk], got[t].at[me], sems[3 * t], sems[3 * t + 1], m).start()
        token[...] = jnp.zeros_like(token)

    res = pl.pallas_call(
        body, name=name, in_specs=[_HBM] * (2 * n),
        out_specs=[_SEM] * (3 * n) + [_HBM] * (2 * n) + [pl.BlockSpec(memory_space=pltpu.VMEM)],
        out_shape=[_PEER_SEMS, _PEER_SEMS, pltpu.SemaphoreType.DMA(())] * n
        + [pltpu.HBM(a.shape, a.dtype) for a in ps] * 2 + [SDS((8, 128), F32)],
        input_output_aliases={t: 3 * n + t for t in range(2 * n)},
        compiler_params=_EFFECT)(*[_in_hbm(a) for a in ps], *[_in_hbm(lax.empty(a.shape, a.dtype)) for a in ps])
    return [(res[3 * t], res[3 * t + 1], res[3 * t + 2], res[3 * n + t], res[4 * n + t]) for t in range(n)], res[-1]


def _exchange_wait(handle, after, name):
    send_sems, recv_sems, local_sem, p, got = handle

    def body(p_ref, got_ref, ssem, rsem, lsem, after_ref, p_out, got_out):
        del after_ref, p_out, got_out
        x, y, c = _me()
        me = 4 * x + 2 * y + c
        pltpu.make_async_copy(p_ref.at[me], got_ref.at[me], lsem).wait()
        for m in range(1, N_DEV):
            _, blk = _peer(m)
            cp = _remote(p_ref.at[blk], got_ref.at[blk], ssem, rsem, m)
            cp.wait_send()
            cp.wait_recv()

    return pl.pallas_call(
        body, name=name, in_specs=[_HBM, _HBM, _SEM, _SEM, _SEM, _ANY], out_specs=[_HBM, _HBM],
        out_shape=[pltpu.HBM(p.shape, p.dtype), pltpu.HBM(got.shape, got.dtype)],
        input_output_aliases={0: 0, 1: 1}, compiler_params=_EFFECT)(p, got, send_sems, recv_sems, local_sem, after)[1]


def _cast_into_slot(w, layer, me1, name):
    _, r, c = w.shape
    tr = 256 if r % 256 == 0 else r

    def body(me_ref, w_ref, o_ref):
        del me_ref
        o_ref[...] = w_ref[...].astype(BF16)

    return pl.pallas_call(
        body, name=name,
        grid_spec=pltpu.PrefetchScalarGridSpec(
            num_scalar_prefetch=1, grid=(r // tr,),
            in_specs=[pl.BlockSpec((None, tr, c), lambda i, me: (layer, i, 0))],
            out_specs=pl.BlockSpec((None, tr, c), lambda i, me: (me[0], i, 0))),
        out_shape=SDS((N_DEV, r, c), BF16), compiler_params=_cp("arbitrary"))(me1, w)


def _sum8(p, name):
    _, r, c = p.shape

    def body(p_ref, o_ref):
        acc = p_ref[0].astype(F32)
        for k in range(1, N_DEV):
            acc = acc + p_ref[k].astype(F32)
        o_ref[...] = acc

    return pl.pallas_call(body, name=name, grid=(1,), in_specs=[_full(p.shape)], out_specs=_full((r, c)),
                          out_shape=SDS((r, c), F32), compiler_params=_cp("arbitrary"))(p)


def _adamw(w, g, m, v):
    m = ADAM_B1 * m + (1.0 - ADAM_B1) * g
    v = ADAM_B2 * v + (1.0 - ADAM_B2) * (g * g)
    m_hat = m / (1.0 - ADAM_B1 ** ADAM_STEP)
    v_hat = v / (1.0 - ADAM_B2 ** ADAM_STEP)
    delta = -ADAM_LR * (m_hat / (jnp.sqrt(v_hat) + ADAM_EPS) + ADAM_WD * w)
    return delta, m, v


def _adam_shard(parts, w, m, v, layer, prev, name):
    _, r, c = parts.shape
    tr = 256 if r % 256 == 0 else r
    n_prev = 0 if prev is None else 4

    def body(*refs):
        p_ref, w_ref, m_ref, v_ref = refs[:4]
        g_ref, d_ref, nm_ref, nv_ref = refs[4 + n_prev:]
        g = p_ref[0].astype(F32)
        for k in range(1, N_DEV):
            g = g + p_ref[k].astype(F32)
        delta, nm, nv = _adamw(w_ref[...], g, m_ref[...], v_ref[...])
        g_ref[...] = g
        d_ref[...] = delta
        nm_ref[...] = nm
        nv_ref[...] = nv

    blk = pl.BlockSpec((None, tr, c), lambda i: (layer, i, 0))
    return pl.pallas_call(
        body, name=name, grid=(r // tr,),
        in_specs=[pl.BlockSpec((N_DEV, tr, c), lambda i: (0, i, 0)), blk, blk, blk] + [_ANY] * n_prev,
        out_specs=[blk] * 4, out_shape=[SDS(w.shape, F32)] * 4,
        input_output_aliases={4 + k: k for k in range(n_prev)},
        compiler_params=_cp("parallel"))(parts, w, m, v, *(prev or ()))


def _adam_flat(w, g, m, v):
    r = w.shape[0]
    tr = r // 2 if r % 16 == 0 else r

    def body(w_ref, g_ref, m_ref, v_ref, d_ref, nm_ref, nv_ref):
        delta, nm, nv = _adamw(w_ref[...], g_ref[...], m_ref[...], v_ref[...])
        d_ref[...] = delta
        nm_ref[...] = nm
        nv_ref[...] = nv

    blk = pl.BlockSpec((tr, D), lambda i: (i, 0))
    return pl.pallas_call(body, name="adam_small", grid=(r // tr,), in_specs=[blk] * 4, out_specs=[blk] * 3,
                          out_shape=[SDS(w.shape, F32)] * 3, compiler_params=_cp("parallel"))(w, g, m, v)


def _after(a, token):
    return a if token is None else a + token[0:1, 0:1]


def _local_step(x, tgt, p, get_w, hook=lambda stage, layer, payload: None):
    s = x.shape[0]
    tm = _row_tile(s)
    wsb = p["gmlp_w_s"].astype(BF16)
    wstb = jnp.swapaxes(p["gmlp_w_s"], -1, -2).astype(BF16)
    bsb = jnp.broadcast_to(p["gmlp_b_s"][..., None], p["gmlp_w_s"].shape)
    wrb = p["lru_w_r"].astype(BF16)
    wib = p["lru_w_i"].astype(BF16)
    saved = []
    for l in range(2):
        win = get_w("w_in", l, x)
        z, h1 = _norm_inproj(x, _after(p["norm1_g"][l][None], hook("pre_inproj", l, win)), win, l, tm)
        ya = _gmlp_fwd(z, p["gmlp_ln_g"][l][None], p["gmlp_ln_b"][l][None], wsb[l], bsb[l], l, tm)
        a0, b0, a1, b1 = _lru_gates_fwd(z, p["conv_w"][l], p["conv_b"][l][None], wrb[l], wib[l],
                                        p["lru_b_r"][l], p["lru_b_i"][l], p["lru_lambda"][l], l, tm)
        h0, hr = _lru_scan(a0, b0, a1, b1, False, l)
        wout = get_w("w_out", l, h0)
        x1, mg = _merge_outproj(x, ya, h0, hr, z, wout, l, tm)
        wfi = get_w("w_ffn_in", l, x1)
        wfo = get_w("w_ffn_out", l, x1)
        x2, gu, h2 = _ffn_fwd(x1, _after(p["norm2_g"][l][None], hook("pre_ffn", l, wfi)), wfi, wfo, l, tm)
        saved.append((x, z, h1, ya, a0, a1, h0, hr, x1, mg, gu, h2, win, wout, wfi, wfo))
        x = x2
    dx, loss, dfg = _loss_head(x, p["final_g"][None], tgt, tm)
    small = [None, None]
    for l in (1, 0):
        x0, z, h1, ya, a0, a1, h0, hr, x1, mg, gu, h2, win, wout, wfi, wfo = saved[l]
        ff, dgu = _ffn_bwd_act(dx, wfo, gu, l, tm)
        d_wfo = _mm_tn(ff, lambda ts: pl.BlockSpec((None, ts, FF_BLK), lambda j, t: (j, t, 0)),
                       dx, lambda ts: pl.BlockSpec((ts, D), lambda j, t: (t, 0)),
                       4, FF_BLK, D, s, f"dw_ffn_out_l{l}")
        dgu8 = dgu.reshape(N_DEV, s, FF_BLK)
        d_wfi = _mm_tn(h2, lambda ts: pl.BlockSpec((ts, D), lambda j, t: (t, 0)),
                       dgu8, lambda ts: pl.BlockSpec((None, ts, FF_BLK), lambda j, t: (j, t, 0)),
                       N_DEV, D, FF_BLK, s, f"dw_ffn_in_l{l}")
        token = hook("ffn_partials", l, dict(w_ffn_out=d_wfo.reshape(N_DEV, D_FF // N_DEV, D), w_ffn_in=d_wfi))
        dx1, dg2 = _mm_nt_rms_bwd(
            dgu8, pl.BlockSpec((None, tm, FF_BLK), lambda i, k: (k, i, 0)),
            wfi.reshape(N_DEV, D, FF_BLK), pl.BlockSpec((None, D, FF_BLK), lambda i, k: (k, 0, 0)),
            N_DEV, x1, _after(p["norm2_g"][l][None], token), dx, f"ffn_bwd_dx_l{l}", tm)
        dz, dya, dh = _outproj_bwd_merge(dx1, wout, ya, h0, hr, z, l, tm)
        d_wout = _mm_tn(mg, lambda ts: pl.BlockSpec((ts, D), lambda j, t: (t, 0)),
                        dx1, lambda ts: pl.BlockSpec((ts, D), lambda j, t: (t, 0)),
                        1, D, D, s, f"dw_out_l{l}")
        g1, g0 = _lru_scan(a1, dh, a0, dh, True, l)
        dxc, dwr, dwi, dbr, dbi, dlam = _lru_gates_bwd(
            z, h0, hr, g0, g1, p["conv_w"][l], p["conv_b"][l][None], wrb[l], wib[l],
            p["lru_b_r"][l], p["lru_b_i"][l], p["lru_lambda"][l], l, tm)
        dz, dcw, dcb = _conv_bwd(dz, dxc, z, p["conv_w"][l], l, tm)
        dz, dws, dbs, dlng, dlnb = _gmlp_bwd(dz, z, dya, p["gmlp_ln_g"][l][None], p["gmlp_ln_b"][l][None],
                                             wsb[l], wstb[l], bsb[l], l, tm)
        d_win = _mm_tn(h1, lambda ts: pl.BlockSpec((ts, D), lambda j, t: (t, 0)),
                       dz, lambda ts: pl.BlockSpec((ts, IN_BLK), lambda j, t: (t, j)),
                       N_DEV, D, IN_BLK, s, f"dw_in_l{l}")
        token = hook("mixer_partials", l, dict(w_out=d_wout.reshape(N_DEV, D // N_DEV, D), w_in=d_win))
        dx, dg1 = _mm_nt_rms_bwd(
            dz, pl.BlockSpec((tm, IN_BLK), lambda i, k: (i, k)),
            win, pl.BlockSpec((None, D, IN_BLK), lambda i, k: (k, 0, 0)),
            N_DEV, x0, _after(p["norm1_g"][l][None], token), dx1, f"inproj_bwd_dx_l{l}", tm)
        small[l] = dict(norm1_g=dg1[0], gmlp_ln_g=dlng[0], gmlp_ln_b=dlnb[0], gmlp_w_s=dws, gmlp_b_s=dbs,
                        conv_w=dcw, conv_b=dcb[0], lru_w_r=dwr, lru_b_r=dbr, lru_w_i=dwi, lru_b_i=dbi,
                        lru_lambda=dlam, norm2_g=dg2[0])
    small_g = {k: jnp.stack([small[0][k], small[1][k]]) for k in small[0]}
    small_g["final_g"] = dfg[0]
    return loss, dx, small_g


_REPL = ["norm1_g", "gmlp_ln_g", "gmlp_ln_b", "gmlp_w_s", "gmlp_b_s", "conv_b", "lru_w_r", "lru_w_i", "norm2_g", "final_g"]
_LANE_SHARDED = ["conv_w", "lru_b_r", "lru_b_i", "lru_lambda"]
_BIG = ["w_in", "w_out", "w_ffn_in", "w_ffn_out"]
_ORDER = ["norm1_g", "w_in", "gmlp_ln_g", "gmlp_ln_b", "gmlp_w_s", "gmlp_b_s", "conv_w", "conv_b", "lru_w_r", "lru_b_r",
          "lru_w_i", "lru_b_i", "lru_lambda", "w_out", "norm2_g", "w_ffn_in", "w_ffn_out", "final_g"]


def _pack(parts, rows):
    flat = jnp.concatenate([a.reshape(-1) for a in parts])
    return jnp.pad(flat, (0, rows * D - flat.shape[0])).reshape(rows, D)


def _unpack(flat, shapes):
    out, off = [], 0
    flat = flat.reshape(-1)
    for shp in shapes:
        n = 1
        for k in shp:
            n *= k
        out.append(flat[off:off + n].reshape(shp))
        off += n
    return out


def kernel(x, norm1_g, w_in, gmlp_ln_g, gmlp_ln_b, gmlp_w_s, gmlp_b_s, conv_w, conv_b, lru_w_r, lru_b_r, lru_w_i, lru_b_i, lru_lambda, w_out, norm2_g, w_ffn_in, w_ffn_out, final_g, loss_target, m_norm1_g, m_w_in, m_gmlp_ln_g, m_gmlp_ln_b, m_gmlp_w_s, m_gmlp_b_s, m_conv_w, m_conv_b, m_lru_w_r, m_lru_b_r, m_lru_w_i, m_lru_b_i, m_lru_lambda, m_w_out, m_norm2_g, m_w_ffn_in, m_w_ffn_out, m_final_g, v_norm1_g, v_w_in, v_gmlp_ln_g, v_gmlp_ln_b, v_gmlp_w_s, v_gmlp_b_s, v_conv_w, v_conv_b, v_lru_w_r, v_lru_b_r, v_lru_w_i, v_lru_b_i, v_lru_lambda, v_w_out, v_norm2_g, v_w_ffn_in, v_w_ffn_out, v_final_g):
    w = dict(norm1_g=norm1_g, w_in=w_in, gmlp_ln_g=gmlp_ln_g, gmlp_ln_b=gmlp_ln_b, gmlp_w_s=gmlp_w_s, gmlp_b_s=gmlp_b_s,
             conv_w=conv_w, conv_b=conv_b, lru_w_r=lru_w_r, lru_b_r=lru_b_r, lru_w_i=lru_w_i, lru_b_i=lru_b_i,
             lru_lambda=lru_lambda, w_out=w_out, norm2_g=norm2_g, w_ffn_in=w_ffn_in, w_ffn_out=w_ffn_out, final_g=final_g)
    mom = dict(norm1_g=m_norm1_g, w_in=m_w_in, gmlp_ln_g=m_gmlp_ln_g, gmlp_ln_b=m_gmlp_ln_b, gmlp_w_s=m_gmlp_w_s,
               gmlp_b_s=m_gmlp_b_s, conv_w=m_conv_w, conv_b=m_conv_b, lru_w_r=m_lru_w_r, lru_b_r=m_lru_b_r,
               lru_w_i=m_lru_w_i, lru_b_i=m_lru_b_i, lru_lambda=m_lru_lambda, w_out=m_w_out, norm2_g=m_norm2_g,
               w_ffn_in=m_w_ffn_in, w_ffn_out=m_w_ffn_out, final_g=m_final_g)
    var = dict(norm1_g=v_norm1_g, w_in=v_w_in, gmlp_ln_g=v_gmlp_ln_g, gmlp_ln_b=v_gmlp_ln_b, gmlp_w_s=v_gmlp_w_s,
               gmlp_b_s=v_gmlp_b_s, conv_w=v_conv_w, conv_b=v_conv_b, lru_w_r=v_lru_w_r, lru_b_r=v_lru_b_r,
               lru_w_i=v_lru_w_i, lru_b_i=v_lru_b_i, lru_lambda=v_lru_lambda, w_out=v_w_out, norm2_g=v_norm2_g,
               w_ffn_in=v_w_ffn_in, w_ffn_out=v_w_ffn_out, final_g=v_final_g)
    xi, yi, ci = _me()
    me = 4 * xi + 2 * yi + ci
    lane0 = me * HD

    lane_shapes = [w[k].shape for k in _LANE_SHARDED]
    lane_rows = sum(a[0] * a[1] for a in lane_shapes)
    packed = jnp.concatenate([w[k].reshape(-1, HD) for k in _LANE_SHARDED])
    packed = jnp.pad(packed, ((0, -lane_rows % 8), (0, 0)))
    lanes = _all_gather(packed, "gather_small")
    params = {k: w[k] for k in _REPL}
    off = 0
    for k, shp in zip(_LANE_SHARDED, lane_shapes):
        n = shp[0] * shp[1]
        params[k] = jnp.swapaxes(lanes[:, off:off + n], 0, 1).reshape(shp[0], shp[1], D)
        off += n

    me1 = jnp.reshape(me, (1,)).astype(jnp.int32)
    gathers = {}
    exchanges = {}
    views = dict(w_in=(N_DEV, D, IN_BLK), w_out=(D, D), w_ffn_in=(2, 4, D, FF_BLK), w_ffn_out=(4, FF_BLK, D))

    def start_gather(names, l, after):
        lands = [_cast_into_slot(w[k], l, me1, f"cast_{k}_l{l}") for k in names]
        started, tok = _gather_start(lands, after, f"gather_start_{'_'.join(names)}_l{l}")
        gathers.update({(k, l): h for k, h in zip(names, started)})
        return tok

    def get_w(k, l, after):
        return _gather_wait(gathers[(k, l)], after, f"gather_wait_{k}_l{l}").reshape(views[k])

    def hook(stage, l, payload):
        if stage == "pre_inproj":
            return start_gather(_BIG[1:], l, payload)
        if stage == "pre_ffn":
            return start_gather(_BIG[:1], l + 1, payload) if l == 0 else None
        started, tok = _exchange_start(list(payload.values()), f"exchange_start_{'_'.join(payload)}_l{l}")
        exchanges.update({(k, l): h for k, h in zip(payload, started)})
        return tok

    start_gather(_BIG[:1], 0, lanes)
    loss, dx, small_g = _local_step(x[0], loss_target[0], params, get_w, hook)

    out = {}
    for k in _BIG:
        res = None
        for l in (1, 0):
            got = _exchange_wait(exchanges[(k, l)], dx, f"exchange_wait_{k}_l{l}")
            res = _adam_shard(got, w[k], mom[k], var[k], l, res, f"adam_{k}_l{l}")
        out[k] = res

    names = _REPL + _LANE_SHARDED
    full_shapes = [small_g[k].shape for k in names]
    n_rows = sum(a.size for a in small_g.values()) // D
    shard_rows = -(-n_rows // (8 * N_DEV)) * 8
    g_flat = _pack([small_g[k] for k in names], shard_rows * N_DEV).reshape(N_DEV, shard_rows, D)
    mine = _sum8(_exchange_partials(g_flat, "exchange_small"), "sum_small")
    g_all = _all_gather(mine, "gather_small_grads")
    grads = dict(zip(names, _unpack(g_all, full_shapes)))
    for k in _LANE_SHARDED:
        grads[k] = lax.dynamic_slice_in_dim(grads[k], lane0, HD, axis=2)
    dev_shapes = [w[k].shape for k in names]
    rows = -(-sum(w[k].size for k in names) // (8 * D)) * 8
    flats = [_pack([src[k] for k in names], rows) for src in (w, grads, mom, var)]
    for nm, flat in zip(("delta", "m", "v"), _adam_flat(*flats)):
        for k, a in zip(names, _unpack(flat, dev_shapes)):
            out.setdefault(k, [grads[k], None, None, None])[("delta", "m", "v").index(nm) + 1] = a

    loss = lax.psum(loss[0, 0], MESH_AXES)
    return (loss, dx[None], *[out[k][0] for k in _ORDER], *[out[k][1] for k in _ORDER],
            *[out[k][2] for k in _ORDER], *[out[k][3] for k in _ORDER])
```

```python
import jax
import jax.numpy as jnp
from jax import lax
from jax.experimental import pallas as pl
from jax.experimental.pallas import tpu as pltpu

F32 = jnp.float32
BF16 = jnp.bfloat16
SDS = jax.ShapeDtypeStruct

D = 1024
N_IN = 6 * D
D_FF = 2816
N_DEV = 8
IN_BLK = N_IN // N_DEV
FF_BLK = 2 * D_FF // N_DEV
HEADS = 8
HD = 128
EPS = 1e-6
LRU_C = 8.0
MESH_AXES = ("x", "y", "c")

ADAM_LR = 0.001
ADAM_B1 = 0.9
ADAM_B2 = 0.999
ADAM_EPS = 1e-08
ADAM_WD = 0.01
ADAM_STEP = 10

VMEM_LIMIT = 56 * 2**20


def _cp(*sem, **kw):
    return pltpu.CompilerParams(dimension_semantics=sem, vmem_limit_bytes=VMEM_LIMIT, **kw)


def _row_tile(s):
    return 512 if s >= 1024 else s // 2


_GELU_C = 0.7978845608028654


def _gelu(x):
    t = jnp.tanh(_GELU_C * (x + 0.044715 * (x * x * x)))
    return 0.5 * x * (1.0 + t), t


def _gelu_grad(x, t):
    return 0.5 * (1.0 + t) + 0.5 * x * (1.0 - t * t) * (_GELU_C * (1.0 + 0.134145 * (x * x)))


def _sigmoid(x):
    return 0.5 + 0.5 * jnp.tanh(0.5 * x)


def _softplus(x):
    e = jnp.exp(-jnp.abs(x))
    w = 1.0 + e
    l1p = jnp.where(w == 1.0, e, jnp.log(w) * e / jnp.where(w == 1.0, 1.0, w - 1.0))
    return jnp.maximum(x, 0.0) + l1p


def _rms_fwd(x, g):
    r = lax.rsqrt(jnp.mean(x * x, axis=-1, keepdims=True) + EPS)
    return x * r * g


def _rms_bwd(x, g, dh):
    r = lax.rsqrt(jnp.mean(x * x, axis=-1, keepdims=True) + EPS)
    xh = x * r
    dxh = dh * g
    dx = r * (dxh - xh * jnp.mean(dxh * xh, axis=-1, keepdims=True))
    dg = jnp.sum(dh * xh, axis=0, keepdims=True)
    return dx, dg


def _dot(a, b):
    return jnp.dot(a, b, preferred_element_type=F32)


def _dot_nt(a, b):
    return lax.dot_general(a, b, (((1,), (1,)), ((), ())), preferred_element_type=F32)


def _dot_tn(a, b):
    return lax.dot_general(a, b, (((0,), (0,)), ((), ())), preferred_element_type=F32)


def _taps(prev, cur, nxt, tm):
    ext = jnp.concatenate([prev, cur, nxt], axis=0)
    n = tm + 16
    sl = slice(8, 8 + tm)
    return (pltpu.roll(ext, 2, 0)[sl], pltpu.roll(ext, 1, 0)[sl], cur,
            pltpu.roll(ext, n - 1, 0)[sl], pltpu.roll(ext, n - 2, 0)[sl])


def _halo_specs(tm, s, col):
    nb8 = s // 8
    r8 = tm // 8
    return (pl.BlockSpec((8, D), lambda i: (jnp.maximum(i * r8 - 1, 0), col)),
            pl.BlockSpec((tm, D), lambda i: (i, col)),
            pl.BlockSpec((8, D), lambda i: (jnp.minimum((i + 1) * r8, nb8 - 1), col)))


def _halo_flags(nt):
    i = pl.program_id(0)
    return (i > 0).astype(F32), (i < nt - 1).astype(F32)


def _full(shape):
    nd = len(shape)
    return pl.BlockSpec(shape, lambda *_: (0,) * nd)


def _resident(shape):
    nd = len(shape)
    return pl.BlockSpec(shape, lambda *_: (0,) * nd, pipeline_mode=pl.Buffered(1))


def _norm_inproj(x, g, w, layer, tm):
    s = x.shape[0]

    def body(x_ref, g_ref, w_ref, z_ref, ht_ref):
        h32 = _rms_fwd(x_ref[...], g_ref[...])
        ht_ref[...] = h32.T.astype(BF16)
        h = h32.astype(BF16)
        for j in range(N_DEV):
            z_ref[:, j * IN_BLK:(j + 1) * IN_BLK] = _dot(h, w_ref[j])

    return pl.pallas_call(
        body, name=f"norm_inproj_l{layer}", grid=(s // tm,),
        in_specs=[pl.BlockSpec((tm, D), lambda i: (i, 0)), _full((1, D)), _resident((N_DEV, D, IN_BLK))],
        out_specs=[pl.BlockSpec((tm, N_IN), lambda i: (i, 0)), pl.BlockSpec((D, tm), lambda i: (0, i))],
        out_shape=[SDS((s, N_IN), F32), SDS((D, s), BF16)],
        compiler_params=_cp("parallel"))(x, g, w)


def _gmlp_fwd(z, lng, lnb, ws, bsb, layer, tm):
    s = z.shape[0]

    def body(zu_ref, zv_ref, lng_ref, lnb_ref, ws_ref, bsb_ref, ya_ref):
        u, _ = _gelu(zu_ref[...])
        gv, _ = _gelu(zv_ref[...])
        xc = gv - jnp.mean(gv, axis=-1, keepdims=True)
        rstd = lax.rsqrt(jnp.mean(xc * xc, axis=-1, keepdims=True) + EPS)
        vb = (xc * rstd * lng_ref[...] + lnb_ref[...]).astype(BF16)
        for c in range(tm // HD):
            rs = slice(c * HD, (c + 1) * HD)
            for g in range(HEADS):
                cs = slice(g * HD, (g + 1) * HD)
                mixed = _dot(ws_ref[g], vb[rs, cs]) + bsb_ref[g]
                ya_ref[rs, cs] = u[rs, cs] * mixed

    return pl.pallas_call(
        body, name=f"gmlp_fwd_l{layer}", grid=(s // tm,),
        in_specs=[pl.BlockSpec((tm, D), lambda i: (i, 0)), pl.BlockSpec((tm, D), lambda i: (i, 1)),
                  _full((1, D)), _full((1, D)), _full((HEADS, HD, HD)), _full((HEADS, HD, HD))],
        out_specs=pl.BlockSpec((tm, D), lambda i: (i, 0)),
        out_shape=SDS((s, D), F32),
        compiler_params=_cp("parallel"))(z, z, lng, lnb, ws, bsb)


def _conv(taps, cw_ref, cb_ref):
    _, m1, c0, p1, p2 = taps
    return cb_ref[...] + m1 * cw_ref[0:1, :] + c0 * cw_ref[1:2, :] + p1 * cw_ref[2:3, :] + p2 * cw_ref[3:4, :]


def _heads_dot(xb, w_ref, d):
    return jnp.concatenate([_dot(xb[:, h * HD:(h + 1) * HD], w_ref[d, h]) for h in range(HEADS)], axis=1)


def _lru_gates(xc, xb, wr_ref, wi_ref, br_ref, bi_ref, lam_ref, d):
    sp = _softplus(-lam_ref[d:d + 1, :])
    r = _sigmoid(_heads_dot(xb, wr_ref, d) + br_ref[d:d + 1, :])
    ig = _sigmoid(_heads_dot(xb, wi_ref, d) + bi_ref[d:d + 1, :])
    la = (-LRU_C) * r * sp
    a = jnp.exp(la)
    q = jnp.maximum(jnp.tanh(-la) * (a * a + 1.0), 0.0)
    rq = jnp.where(q > 0.0, lax.rsqrt(jnp.where(q > 0.0, q, 1.0)), 0.0)
    return sp, r, ig, a, q * rq, rq


def _lru_gates_fwd(z, cw, cb, wr, wi, br, bi, lam, layer, tm):
    s = z.shape[0]
    nt = s // tm

    def body(zp_ref, zc_ref, zn_ref, cw_ref, cb_ref, wr_ref, wi_ref, br_ref, bi_ref, lam_ref,
             a0_ref, b0_ref, a1_ref, b1_ref):
        fp, fn = _halo_flags(nt)
        xc = _conv(_taps(zp_ref[...] * fp, zc_ref[...], zn_ref[...] * fn, tm), cw_ref, cb_ref)
        xb = xc.astype(BF16)
        for d, (a_ref, b_ref) in enumerate(((a0_ref, b0_ref), (a1_ref, b1_ref))):
            _, _, ig, a, mult, _ = _lru_gates(xc, xb, wr_ref, wi_ref, br_ref, bi_ref, lam_ref, d)
            a_ref[...] = a
            b_ref[...] = mult * (ig * xc)

    tile = pl.BlockSpec((tm, D), lambda i: (i, 0))
    return pl.pallas_call(
        body, name=f"lru_gates_fwd_l{layer}", grid=(nt,),
        in_specs=[*_halo_specs(tm, s, 2), _full((4, D)), _full((1, D)),
                  _full((2, HEADS, HD, HD)), _full((2, HEADS, HD, HD)), _full((2, D)), _full((2, D)), _full((2, D))],
        out_specs=[tile] * 4, out_shape=[SDS((s, D), F32)] * 4,
        compiler_params=_cp("parallel"))(z, z, z, cw, cb, wr, wi, br, bi, lam)


def _scan_group(a, x, c, reverse, bwd):
    row = lax.broadcasted_iota(jnp.int32, a.shape, 0)
    b = a * x if bwd else x
    for d in (1, 2, 4):
        keep = (row < 8 - d) if reverse else (row >= d)
        sh = 8 - d if reverse else d
        a_s = jnp.where(keep, pltpu.roll(a, sh, 0), 1.0)
        b_s = jnp.where(keep, pltpu.roll(b, sh, 0), 0.0)
        b = a * b_s + b
        a = a * a_s
    h = b + a * c
    new_c = h[0:1, :] if reverse else h[7:8, :]
    if not bwd:
        return h, new_c
    if reverse:
        prev = jnp.where(row < 7, pltpu.roll(h, 7, 0), c)
    else:
        prev = jnp.where(row >= 1, pltpu.roll(h, 1, 0), c)
    return x + prev, new_c


def _lru_scan(a_f, x_f, a_r, x_r, bwd, layer):
    s = a_f.shape[0]
    ts = min(1024, s // 2)
    cb = 512
    nt = s // ts
    ng = ts // 8

    def body(af_ref, xf_ref, ar_ref, xr_ref, of_ref, or_ref, cf, cr):
        @pl.when(pl.program_id(1) == 0)
        def _():
            cf[...] = jnp.zeros_like(cf)
            cr[...] = jnp.zeros_like(cr)

        def step(j, carry):
            c_f, c_r = carry
            rf = pl.multiple_of(j * 8, 8)
            rr = pl.multiple_of((ng - 1 - j) * 8, 8)
            o, c_f = _scan_group(af_ref[pl.ds(rf, 8), :], xf_ref[pl.ds(rf, 8), :], c_f, False, bwd)
            of_ref[pl.ds(rf, 8), :] = o
            o, c_r = _scan_group(ar_ref[pl.ds(rr, 8), :], xr_ref[pl.ds(rr, 8), :], c_r, True, bwd)
            or_ref[pl.ds(rr, 8), :] = o
            return c_f, c_r

        c_f, c_r = lax.fori_loop(0, ng, step, (cf[0:1, :], cr[0:1, :]), unroll=2)
        cf[...] = jnp.broadcast_to(c_f, cf.shape)
        cr[...] = jnp.broadcast_to(c_r, cr.shape)

    fwd = pl.BlockSpec((ts, cb), lambda c, t: (t, c))
    rev = pl.BlockSpec((ts, cb), lambda c, t: (nt - 1 - t, c))
    return pl.pallas_call(
        body, name=f"lru_scan_{'bwd' if bwd else 'fwd'}_l{layer}", grid=(D // cb, nt),
        in_specs=[fwd, fwd, rev, rev], out_specs=[fwd, rev],
        out_shape=[SDS((s, D), F32)] * 2,
        scratch_shapes=[pltpu.VMEM((8, cb), F32), pltpu.VMEM((8, cb), F32)],
        compiler_params=_cp("parallel", "arbitrary"))(a_f, x_f, a_r, x_r)


def _merge_outproj(x, ya, h0, h1, z, wo, layer, tm):
    s = x.shape[0]

    def body(x_ref, ya_ref, h0_ref, h1_ref, zg_ref, za_ref, zb_ref, wo_ref, x1_ref, mg_ref):
        gg, _ = _gelu(zg_ref[...])
        yb = (h0_ref[...] + h1_ref[...]) * gg
        m32 = _sigmoid(za_ref[...]) * ya_ref[...] + _sigmoid(zb_ref[...]) * yb
        mg_ref[...] = m32.T.astype(BF16)
        x1_ref[...] = x_ref[...] + _dot(m32.astype(BF16), wo_ref[...])

    tile = pl.BlockSpec((tm, D), lambda i: (i, 0))
    return pl.pallas_call(
        body, name=f"merge_outproj_l{layer}", grid=(s // tm,),
        in_specs=[tile, tile, tile, tile] + [pl.BlockSpec((tm, D), lambda i, c=c: (i, c)) for c in (3, 4, 5)]
        + [_full((D, D))],
        out_specs=[tile, pl.BlockSpec((D, tm), lambda i: (0, i))], out_shape=[SDS((s, D), F32), SDS((D, s), BF16)],
        compiler_params=_cp("parallel"))(x, ya, h0, h1, z, z, z, wo)


def _ffn_fwd(x1, g, wfi, wfo, layer, tm):
    s = x1.shape[0]

    def body(x_ref, g_ref, wi_ref, wo_ref, x2_ref, gu_ref, ht_ref):
        x = x_ref[...]
        h32 = _rms_fwd(x, g_ref[...])
        ht_ref[...] = h32.T.astype(BF16)
        h = h32.astype(BF16)
        acc = x
        for k in range(4):
            gate = _dot(h, wi_ref[0, k])
            up = _dot(h, wi_ref[1, k])
            gu_ref[0, k] = gate.astype(BF16)
            gu_ref[1, k] = up.astype(BF16)
            acc = acc + _dot((gate * _sigmoid(gate) * up).astype(BF16), wo_ref[k])
        x2_ref[...] = acc

    tile = pl.BlockSpec((tm, D), lambda i: (i, 0))
    return pl.pallas_call(
        body, name=f"ffn_fwd_l{layer}", grid=(s // tm,),
        in_specs=[tile, _full((1, D)), _resident((2, 4, D, FF_BLK)), _resident((4, FF_BLK, D))],
        out_specs=[tile, pl.BlockSpec((2, 4, tm, FF_BLK), lambda i: (0, 0, i, 0)),
                   pl.BlockSpec((D, tm), lambda i: (0, i))],
        out_shape=[SDS((s, D), F32), SDS((2, 4, s, FF_BLK), BF16), SDS((D, s), BF16)],
        compiler_params=_cp("parallel"))(x1, g, wfi, wfo)


def _loss_head(x, g, tgt, tm):
    s = x.shape[0]

    def body(x_ref, g_ref, t_ref, dx_ref, loss_ref, dg_ref):
        @pl.when(pl.program_id(0) == 0)
        def _():
            loss_ref[...] = jnp.zeros_like(loss_ref)
            dg_ref[...] = jnp.zeros_like(dg_ref)

        x = x_ref[...]
        gv = g_ref[...]
        e = _rms_fwd(x, gv) - t_ref[...]
        rows = jnp.sum(e * e, axis=-1, keepdims=True)
        loss_ref[...] += (0.5 / D) * jnp.sum(rows, axis=0, keepdims=True)
        dx, dg = _rms_bwd(x, gv, e * (1.0 / D))
        dx_ref[...] = dx
        dg_ref[...] += dg

    tile = pl.BlockSpec((tm, D), lambda i: (i, 0))
    return pl.pallas_call(
        body, name="loss_head", grid=(s // tm,),
        in_specs=[tile, _full((1, D)), tile],
        out_specs=[tile, _full((1, 1)), _full((1, D))],
        out_shape=[SDS((s, D), F32), SDS((1, 1), F32), SDS((1, D), F32)],
        compiler_params=_cp("arbitrary"))(x, g, tgt)


def _ffn_bwd_act(dx2, wfo, gu, layer, tm):
    s = dx2.shape[0]

    def body(dx_ref, wo_ref, gu_ref, ff_ref, dgu_ref):
        dxb = dx_ref[...].astype(BF16)
        for k in range(4):
            dff = _dot_nt(dxb, wo_ref[k])
            gate = gu_ref[0, k].astype(F32)
            up = gu_ref[1, k].astype(F32)
            sg = _sigmoid(gate)
            sl = gate * sg
            ff_ref[k] = (sl * up).astype(BF16)
            dgu_ref[0, k] = (dff * up * (sg * (1.0 + gate * (1.0 - sg)))).astype(BF16)
            dgu_ref[1, k] = (dff * sl).astype(BF16)

    blk = pl.BlockSpec((2, 4, tm, FF_BLK), lambda i: (0, 0, i, 0))
    return pl.pallas_call(
        body, name=f"ffn_bwd_act_l{layer}", grid=(s // tm,),
        in_specs=[pl.BlockSpec((tm, D), lambda i: (i, 0)), _resident((4, FF_BLK, D)), blk],
        out_specs=[pl.BlockSpec((4, tm, FF_BLK), lambda i: (0, i, 0)), blk],
        out_shape=[SDS((4, s, FF_BLK), BF16), SDS((2, 4, s, FF_BLK), BF16)],
        compiler_params=_cp("parallel"))(dx2, wfo, gu)


def _mm_nt_rms_bwd(a, a_spec, a_blocks, w, x, g, dres, name, tm):
    s = x.shape[0]

    def body(a_ref, w_ref, x_ref, g_ref, dres_ref, dx_ref, dg_ref):
        @pl.when(pl.program_id(0) == 0)
        def _():
            dg_ref[...] = jnp.zeros_like(dg_ref)

        dh = None
        for k, blk in enumerate(a_blocks(a_ref)):
            part = _dot_nt(blk, w_ref[k])
            dh = part if dh is None else dh + part
        dx, dg = _rms_bwd(x_ref[...], g_ref[...], dh)
        dx_ref[...] = dres_ref[...] + dx
        dg_ref[...] += dg

    tile = pl.BlockSpec((tm, D), lambda i: (i, 0))
    return pl.pallas_call(
        body, name=name, grid=(s // tm,),
        in_specs=[a_spec, _resident(w.shape), tile, _full((1, D)), tile],
        out_specs=[tile, _full((1, D))], out_shape=[SDS((s, D), F32), SDS((1, D), F32)],
        compiler_params=_cp("arbitrary"))(a, w, x, g, dres)


def _mm_tn(a, a_spec, b, b_spec, nb, out_shape, out_spec, name, a_is_transposed=True):
    def body(a_ref, b_ref, o_ref):
        bb = b_ref[...].astype(BF16)
        o_ref[...] = (_dot(a_ref[...], bb) if a_is_transposed else _dot_tn(a_ref[...], bb)).astype(BF16)

    return pl.pallas_call(
        body, name=name, grid=(nb,), in_specs=[a_spec, b_spec], out_specs=out_spec,
        out_shape=SDS(out_shape, BF16), compiler_params=_cp("parallel"))(a, b)


def _outproj_bwd_merge(dx1, wo, ya, h0, h1, z, layer, tm):
    s = dx1.shape[0]

    def body(dx_ref, wo_ref, ya_ref, h0_ref, h1_ref, zg_ref, za_ref, zb_ref, dz_ref, dya_ref, dh_ref):
        dm = _dot_nt(dx_ref[...].astype(BF16), wo_ref[...])
        sa = _sigmoid(za_ref[...])
        sb = _sigmoid(zb_ref[...])
        zg = zg_ref[...]
        gg, tg = _gelu(zg)
        hs = h0_ref[...] + h1_ref[...]
        dyb = dm * sb
        dya_ref[...] = dm * sa
        dh_ref[...] = dyb * gg
        dz_ref[:, 0:D] = (dyb * hs * _gelu_grad(zg, tg)).astype(BF16)
        dz_ref[:, D:2 * D] = (dm * ya_ref[...] * (sa * (1.0 - sa))).astype(BF16)
        dz_ref[:, 2 * D:3 * D] = (dm * (hs * gg) * (sb * (1.0 - sb))).astype(BF16)

    tile = pl.BlockSpec((tm, D), lambda i: (i, 0))
    return pl.pallas_call(
        body, name=f"outproj_bwd_merge_l{layer}", grid=(s // tm,),
        in_specs=[tile, _full((D, D)), tile, tile, tile]
        + [pl.BlockSpec((tm, D), lambda i, c=c: (i, c)) for c in (3, 4, 5)],
        out_specs=[pl.BlockSpec((tm, 3 * D), lambda i: (i, 1)), tile, tile],
        out_shape=[SDS((s, N_IN), BF16), SDS((s, D), F32), SDS((s, D), F32)],
        compiler_params=_cp("parallel"))(dx1, wo, ya, h0, h1, z, z, z)


def _lru_gates_bwd(z, h0, h1, g0, g1, cw, cb, wr, wi, br, bi, lam, layer, tm):
    s = z.shape[0]
    nt = s // tm

    def body(zp_ref, zc_ref, zn_ref, h0p_ref, h0_ref, h1_ref, h1n_ref, g0_ref, g1_ref,
             cw_ref, cb_ref, wr_ref, wi_ref, br_ref, bi_ref, lam_ref,
             dxc_ref, dwr_ref, dwi_ref, dbr_ref, dbi_ref, dlam_ref):
        i = pl.program_id(0)
        fp, fn = _halo_flags(nt)

        @pl.when(i == 0)
        def _():
            for r in (dwr_ref, dwi_ref, dbr_ref, dbi_ref, dlam_ref):
                r[...] = jnp.zeros_like(r)

        xc = _conv(_taps(zp_ref[...] * fp, zc_ref[...], zn_ref[...] * fn, tm), cw_ref, cb_ref)
        xb = xc.astype(BF16)
        zeros8 = jnp.zeros((8, D), F32)
        h_prev = _taps(h0p_ref[...] * fp, h0_ref[...], zeros8, tm)[1]
        h_next = _taps(zeros8, h1_ref[...], h1n_ref[...] * fn, tm)[3]
        dxc = jnp.zeros((tm, D), F32)
        for d, (g_ref, hsh) in enumerate(((g0_ref, h_prev), (g1_ref, h_next))):
            sp, r, ig, a, mult, rmult = _lru_gates(xc, xb, wr_ref, wi_ref, br_ref, bi_ref, lam_ref, d)
            db = g_ref[...]
            da = db * hsh
            dmult = db * (ig * xc)
            di = db * (mult * xc)
            dxc = dxc + db * (mult * ig)
            dla = da * a - dmult * (a * a * rmult)
            dlam_ref[d:d + 1, :] += jnp.sum(dla * r, axis=0, keepdims=True) * (-LRU_C)
            dpr = dla * sp * (-LRU_C) * (r * (1.0 - r))
            dpi = di * (ig * (1.0 - ig))
            dbr_ref[d:d + 1, :] += jnp.sum(dpr, axis=0, keepdims=True)
            dbi_ref[d:d + 1, :] += jnp.sum(dpi, axis=0, keepdims=True)
            dprb = dpr.astype(BF16)
            dpib = dpi.astype(BF16)
            parts = []
            for h in range(HEADS):
                cs = slice(h * HD, (h + 1) * HD)
                dwr_ref[d, h] += _dot_tn(xb[:, cs], dprb[:, cs])
                dwi_ref[d, h] += _dot_tn(xb[:, cs], dpib[:, cs])
                parts.append(_dot_nt(dprb[:, cs], wr_ref[d, h]) + _dot_nt(dpib[:, cs], wi_ref[d, h]))
            dxc = dxc + jnp.concatenate(parts, axis=1)
        dxc_ref[...] = dxc

        @pl.when(i == nt - 1)
        def _():
            dlam_ref[...] = dlam_ref[...] * (-_sigmoid(-lam_ref[...]))

    tile = pl.BlockSpec((tm, D), lambda i: (i, 0))
    zp, zc, zn = _halo_specs(tm, s, 2)
    hp, hc, hn = _halo_specs(tm, s, 0)
    wspec = _full((2, HEADS, HD, HD))
    return pl.pallas_call(
        body, name=f"lru_gates_bwd_l{layer}", grid=(nt,),
        in_specs=[zp, zc, zn, hp, hc, hc, hn, tile, tile, _full((4, D)), _full((1, D)),
                  wspec, wspec, _full((2, D)), _full((2, D)), _full((2, D))],
        out_specs=[tile, wspec, wspec, _full((2, D)), _full((2, D)), _full((2, D))],
        out_shape=[SDS((s, D), F32), SDS((2, HEADS, HD, HD), F32), SDS((2, HEADS, HD, HD), F32),
                   SDS((2, D), F32), SDS((2, D), F32), SDS((2, D), F32)],
        compiler_params=_cp("arbitrary"))(z, z, z, h0, h0, h1, h1, g0, g1, cw, cb, wr, wi, br, bi, lam)


def _conv_bwd(dz, dxc, z, cw, layer, tm):
    s = z.shape[0]
    nt = s // tm

    def body(dz_in, dp_ref, dc_ref, dn_ref, zp_ref, zc_ref, zn_ref, cw_ref, dz_ref, dcw_ref, dcb_ref):
        del dz_in
        fp, fn = _halo_flags(nt)

        @pl.when(pl.program_id(0) == 0)
        def _():
            dcw_ref[...] = jnp.zeros_like(dcw_ref)
            dcb_ref[...] = jnp.zeros_like(dcb_ref)

        dxc = dc_ref[...]
        dm2, dm1, _, dp1, _ = _taps(dp_ref[...] * fp, dxc, dn_ref[...] * fn, tm)
        dz_ref[...] = (cw_ref[0:1, :] * dp1 + cw_ref[1:2, :] * dxc + cw_ref[2:3, :] * dm1
                       + cw_ref[3:4, :] * dm2).astype(BF16)
        _, zm1, z0, zp1, zp2 = _taps(zp_ref[...] * fp, zc_ref[...], zn_ref[...] * fn, tm)
        for k, zt in enumerate((zm1, z0, zp1, zp2)):
            dcw_ref[k:k + 1, :] += jnp.sum(dxc * zt, axis=0, keepdims=True)
        dcb_ref[...] += jnp.sum(dxc, axis=0, keepdims=True)

    return pl.pallas_call(
        body, name=f"conv_bwd_l{layer}", grid=(nt,),
        in_specs=[pl.BlockSpec(memory_space=pl.ANY), *_halo_specs(tm, s, 0), *_halo_specs(tm, s, 2), _full((4, D))],
        out_specs=[pl.BlockSpec((tm, D), lambda i: (i, 2)), _full((4, D)), _full((1, D))],
        out_shape=[SDS((s, N_IN), BF16), SDS((4, D), F32), SDS((1, D), F32)],
        input_output_aliases={0: 0},
        compiler_params=_cp("arbitrary"))(dz, dxc, dxc, dxc, z, z, z, cw)


def _gmlp_bwd(dz, z, dya, lng, lnb, ws, wst, bsb, layer, tm):
    s = z.shape[0]
    nt = s // tm

    def body(dz_in, zu_ref, zv_ref, dya_ref, lng_ref, lnb_ref, ws_ref, wst_ref, bsb_ref,
             dz_ref, dws_ref, dbs_ref, dlng_ref, dlnb_ref, du_s, dv_s, dbs_acc):
        del dz_in
        i = pl.program_id(0)

        @pl.when(i == 0)
        def _():
            for r in (dws_ref, dlng_ref, dlnb_ref, dbs_acc):
                r[...] = jnp.zeros_like(r)

        zu = zu_ref[...]
        zv = zv_ref[...]
        u, tu = _gelu(zu)
        gv, tv = _gelu(zv)
        xc = gv - jnp.mean(gv, axis=-1, keepdims=True)
        rstd = lax.rsqrt(jnp.mean(xc * xc, axis=-1, keepdims=True) + EPS)
        xh = xc * rstd
        lng_v = lng_ref[...]
        vb = (xh * lng_v + lnb_ref[...]).astype(BF16)
        dya = dya_ref[...]
        for c in range(tm // HD):
            rs = slice(c * HD, (c + 1) * HD)
            for g in range(HEADS):
                cs = slice(g * HD, (g + 1) * HD)
                vblk = vb[rs, cs]
                mixed = _dot(ws_ref[g], vblk) + bsb_ref[g]
                du_s[rs, cs] = dya[rs, cs] * mixed
                dmx = dya[rs, cs] * u[rs, cs]
                dbs_acc[g] += dmx
                dmxb = dmx.astype(BF16)
                dws_ref[g] += _dot_nt(dmxb, vblk)
                dv_s[rs, cs] = _dot(wst_ref[g], dmxb)
        dv = dv_s[...]
        dlng_ref[...] += jnp.sum(dv * xh, axis=0, keepdims=True)
        dlnb_ref[...] += jnp.sum(dv, axis=0, keepdims=True)
        dxh = dv * lng_v
        dgv = rstd * (dxh - jnp.mean(dxh, axis=-1, keepdims=True)
                      - xh * jnp.mean(dxh * xh, axis=-1, keepdims=True))
        dz_ref[:, 0:D] = (du_s[...] * _gelu_grad(zu, tu)).astype(BF16)
        dz_ref[:, D:2 * D] = (dgv * _gelu_grad(zv, tv)).astype(BF16)

        @pl.when(i == nt - 1)
        def _():
            for g in range(HEADS):
                dbs_ref[g:g + 1, :] = jnp.sum(dbs_acc[g].T, axis=0, keepdims=True)

    tile = pl.BlockSpec((tm, D), lambda i: (i, 0))
    wspec = _full((HEADS, HD, HD))
    return pl.pallas_call(
        body, name=f"gmlp_bwd_l{layer}", grid=(nt,),
        in_specs=[pl.BlockSpec(memory_space=pl.ANY), tile, pl.BlockSpec((tm, D), lambda i: (i, 1)), tile,
                  _full((1, D)), _full((1, D)), wspec, wspec, wspec],
        out_specs=[pl.BlockSpec((tm, 2 * D), lambda i: (i, 0)), wspec, _full((HEADS, HD)), _full((1, D)), _full((1, D))],
        out_shape=[SDS((s, N_IN), BF16), SDS((HEADS, HD, HD), F32), SDS((HEADS, HD), F32),
                   SDS((1, D), F32), SDS((1, D), F32)],
        scratch_shapes=[pltpu.VMEM((tm, D), F32), pltpu.VMEM((tm, D), F32), pltpu.VMEM((HEADS, HD, HD), F32)],
        input_output_aliases={0: 0},
        compiler_params=_cp("arbitrary"))(dz, z, z, dya, lng, lnb, ws, wst, bsb)


def _me():
    return lax.axis_index("x"), lax.axis_index("y"), lax.axis_index("c")


def _peer(m):
    x, y, c = _me()
    px = 1 - x if m & 4 else x
    py = 1 - y if m & 2 else y
    pc = 1 - c if m & 1 else c
    return (px, py, pc), 4 * px + 2 * py + pc


_ANY = pl.BlockSpec(memory_space=pl.ANY)
_EXCHANGE_SEMS = [pltpu.SemaphoreType.DMA((N_DEV - 1,)), pltpu.SemaphoreType.DMA((N_DEV - 1,)), pltpu.SemaphoreType.DMA(())]


def _all_gather(v, name):
    def body(v_ref, o_ref, send_sems, recv_sems, local_sem):
        x, y, c = _me()
        me = 4 * x + 2 * y + c
        local = pltpu.make_async_copy(v_ref, o_ref.at[me], local_sem)
        local.start()
        sends = []
        for m in range(1, N_DEV):
            dev, _ = _peer(m)
            cp = pltpu.make_async_remote_copy(v_ref, o_ref.at[me], send_sems.at[m - 1], recv_sems.at[m - 1],
                                              device_id=dev, device_id_type=pl.DeviceIdType.MESH)
            cp.start()
            sends.append(cp)
        for m in range(1, N_DEV):
            dev, blk = _peer(m)
            pltpu.make_async_remote_copy(v_ref, o_ref.at[blk], send_sems.at[m - 1], recv_sems.at[m - 1],
                                         device_id=dev, device_id_type=pl.DeviceIdType.MESH).wait_recv()
        for cp in sends:
            cp.wait_send()
        local.wait()

    return pl.pallas_call(
        body, name=name, in_specs=[_ANY], out_specs=_ANY,
        out_shape=SDS((N_DEV,) + v.shape, v.dtype), scratch_shapes=_EXCHANGE_SEMS)(v)


def _exchange_partials(p, name):
    def body(p_ref, o_ref, send_sems, recv_sems, local_sem):
        x, y, c = _me()
        me = 4 * x + 2 * y + c
        local = pltpu.make_async_copy(p_ref.at[me], o_ref.at[me], local_sem)
        local.start()
        sends = []
        for m in range(1, N_DEV):
            dev, blk = _peer(m)
            cp = pltpu.make_async_remote_copy(p_ref.at[blk], o_ref.at[me], send_sems.at[m - 1], recv_sems.at[m - 1],
                                              device_id=dev, device_id_type=pl.DeviceIdType.MESH)
            cp.start()
            sends.append(cp)
        for m in range(1, N_DEV):
            dev, blk = _peer(m)
            pltpu.make_async_remote_copy(p_ref.at[blk], o_ref.at[blk], send_sems.at[m - 1], recv_sems.at[m - 1],
                                         device_id=dev, device_id_type=pl.DeviceIdType.MESH).wait_recv()
        for cp in sends:
            cp.wait_send()
        local.wait()

    return pl.pallas_call(
        body, name=name, in_specs=[_ANY], out_specs=_ANY,
        out_shape=SDS(p.shape, p.dtype), scratch_shapes=_EXCHANGE_SEMS)(p)


_HBM = pl.BlockSpec(memory_space=pltpu.HBM)
_SEM = pl.BlockSpec(memory_space=pltpu.SEMAPHORE)
_EFFECT = pltpu.CompilerParams(has_side_effects=pltpu.SideEffectType.DATAFLOW_SIDE_EFFECTING)
_PEER_SEMS = pltpu.SemaphoreType.DMA((N_DEV - 1,))


def _in_hbm(a):
    return pltpu.with_memory_space_constraint(a, pltpu.HBM)


def _remote(src, dst, send_sems, recv_sems, m):
    dev, _ = _peer(m)
    return pltpu.make_async_remote_copy(src, dst, send_sems.at[m - 1], recv_sems.at[m - 1],
                                        device_id=dev, device_id_type=pl.DeviceIdType.MESH)


def _gather_start(lands, after, name):
    n = len(lands)

    def body(*refs):
        land = refs[:n]
        sems = refs[n + 1:3 * n + 1]
        token = refs[-1]
        x, y, c = _me()
        me = 4 * x + 2 * y + c
        for t in range(n):
            for m in range(1, N_DEV):
                _remote(land[t].at[me], land[t].at[me], sems[2 * t], sems[2 * t + 1], m).start()
        token[...] = jnp.zeros_like(token)

    res = pl.pallas_call(
        body, name=name, in_specs=[_HBM] * n + [_ANY],
        out_specs=[_SEM] * (2 * n) + [_HBM] * n + [pl.BlockSpec(memory_space=pltpu.VMEM)],
        out_shape=[_PEER_SEMS] * (2 * n) + [pltpu.HBM(a.shape, a.dtype) for a in lands] + [SDS((8, 128), F32)],
        input_output_aliases={t: 2 * n + t for t in range(n)},
        compiler_params=_EFFECT)(*[_in_hbm(a) for a in lands], after)
    return [(res[2 * t], res[2 * t + 1], res[2 * n + t]) for t in range(n)], res[-1]


def _gather_wait(handle, after, name):
    send_sems, recv_sems, land = handle

    def body(land_ref, ssem, rsem, after_ref, out_ref):
        del after_ref, out_ref
        x, y, c = _me()
        me = 4 * x + 2 * y + c
        for m in range(1, N_DEV):
            _, blk = _peer(m)
            cp = _remote(land_ref.at[me], land_ref.at[blk], ssem, rsem, m)
            cp.wait_send()
            cp.wait_recv()

    return pl.pallas_call(
        body, name=name, in_specs=[_HBM, _SEM, _SEM, _ANY], out_specs=_HBM,
        out_shape=pltpu.HBM(land.shape, land.dtype), input_output_aliases={0: 0},
        compiler_params=_EFFECT)(land, send_sems, recv_sems, after)


def _exchange_start(ps, name):
    n = len(ps)

    def body(*refs):
        p = refs[:n]
        got = refs[n:2 * n]
        sems = refs[2 * n:5 * n]
        token = refs[-1]
        x, y, c = _me()
        me = 4 * x + 2 * y + c
        for t in range(n):
            pltpu.make_async_copy(p[t].at[me], got[t].at[me], sems[3 * t + 2]).start()
            for m in range(1, N_DEV):
                _, blk = _peer(m)
                _remote(p[t].at[blk], got[t].at[me], sems[3 * t], sems[3 * t + 1], m).start()
        token[...] = jnp.zeros_like(token)

    res = pl.pallas_call(
        body, name=name, in_specs=[_HBM] * (2 * n),
        out_specs=[_SEM] * (3 * n) + [_HBM] * (2 * n) + [pl.BlockSpec(memory_space=pltpu.VMEM)],
        out_shape=[_PEER_SEMS, _PEER_SEMS, pltpu.SemaphoreType.DMA(())] * n
        + [pltpu.HBM(a.shape, a.dtype) for a in ps] * 2 + [SDS((8, 128), F32)],
        input_output_aliases={t: 3 * n + t for t in range(2 * n)},
        compiler_params=_EFFECT)(*[_in_hbm(a) for a in ps], *[_in_hbm(lax.empty(a.shape, a.dtype)) for a in ps])
    return [(res[3 * t], res[3 * t + 1], res[3 * t + 2], res[3 * n + t], res[4 * n + t]) for t in range(n)], res[-1]


def _exchange_wait(handle, after, name):
    send_sems, recv_sems, local_sem, p, got = handle

    def body(p_ref, got_ref, ssem, rsem, lsem, after_ref, p_out, got_out):
        del after_ref, p_out, got_out
        x, y, c = _me()
        me = 4 * x + 2 * y + c
        pltpu.make_async_copy(p_ref.at[me], got_ref.at[me], lsem).wait()
        for m in range(1, N_DEV):
            _, blk = _peer(m)
            cp = _remote(p_ref.at[blk], got_ref.at[blk], ssem, rsem, m)
            cp.wait_send()
            cp.wait_recv()

    return pl.pallas_call(
        body, name=name, in_specs=[_HBM, _HBM, _SEM, _SEM, _SEM, _ANY], out_specs=[_HBM, _HBM],
        out_shape=[pltpu.HBM(p.shape, p.dtype), pltpu.HBM(got.shape, got.dtype)],
        input_output_aliases={0: 0, 1: 1}, compiler_params=_EFFECT)(p, got, send_sems, recv_sems, local_sem, after)[1]


def _cast_into_slot(w, layer, me1, name):
    _, r, c = w.shape
    tr = 256 if r % 256 == 0 else r

    def body(me_ref, w_ref, o_ref):
        del me_ref
        o_ref[...] = w_ref[...].astype(BF16)

    return pl.pallas_call(
        body, name=name,
        grid_spec=pltpu.PrefetchScalarGridSpec(
            num_scalar_prefetch=1, grid=(r // tr,),
            in_specs=[pl.BlockSpec((None, tr, c), lambda i, me: (layer, i, 0))],
            out_specs=pl.BlockSpec((None, tr, c), lambda i, me: (me[0], i, 0))),
        out_shape=SDS((N_DEV, r, c), BF16), compiler_params=_cp("arbitrary"))(me1, w)


def _sum8(p, name):
    _, r, c = p.shape

    def body(p_ref, o_ref):
        acc = p_ref[0].astype(F32)
        for k in range(1, N_DEV):
            acc = acc + p_ref[k].astype(F32)
        o_ref[...] = acc

    return pl.pallas_call(body, name=name, grid=(1,), in_specs=[_full(p.shape)], out_specs=_full((r, c)),
                          out_shape=SDS((r, c), F32), compiler_params=_cp("arbitrary"))(p)


def _adamw(w, g, m, v):
    m = ADAM_B1 * m + (1.0 - ADAM_B1) * g
    v = ADAM_B2 * v + (1.0 - ADAM_B2) * (g * g)
    m_hat = m / (1.0 - ADAM_B1 ** ADAM_STEP)
    v_hat = v / (1.0 - ADAM_B2 ** ADAM_STEP)
    delta = -ADAM_LR * (m_hat / (jnp.sqrt(v_hat) + ADAM_EPS) + ADAM_WD * w)
    return delta, m, v


def _adam_shard(parts, w, m, v, layer, prev, name):
    _, r, c = parts.shape
    tr = 256 if r % 256 == 0 else r
    n_prev = 0 if prev is None else 4

    def body(*refs):
        p_ref, w_ref, m_ref, v_ref = refs[:4]
        g_ref, d_ref, nm_ref, nv_ref = refs[4 + n_prev:]
        g = p_ref[0].astype(F32)
        for k in range(1, N_DEV):
            g = g + p_ref[k].astype(F32)
        delta, nm, nv = _adamw(w_ref[...], g, m_ref[...], v_ref[...])
        g_ref[...] = g
        d_ref[...] = delta
        nm_ref[...] = nm
        nv_ref[...] = nv

    blk = pl.BlockSpec((None, tr, c), lambda i: (layer, i, 0))
    return pl.pallas_call(
        body, name=name, grid=(r // tr,),
        in_specs=[pl.BlockSpec((N_DEV, tr, c), lambda i: (0, i, 0)), blk, blk, blk] + [_ANY] * n_prev,
        out_specs=[blk] * 4, out_shape=[SDS(w.shape, F32)] * 4,
        input_output_aliases={4 + k: k for k in range(n_prev)},
        compiler_params=_cp("parallel"))(parts, w, m, v, *(prev or ()))


def _adam_flat(w, g, m, v):
    r = w.shape[0]
    tr = r // 2 if r % 16 == 0 else r

    def body(w_ref, g_ref, m_ref, v_ref, d_ref, nm_ref, nv_ref):
        delta, nm, nv = _adamw(w_ref[...], g_ref[...], m_ref[...], v_ref[...])
        d_ref[...] = delta
        nm_ref[...] = nm
        nv_ref[...] = nv

    blk = pl.BlockSpec((tr, D), lambda i: (i, 0))
    return pl.pallas_call(body, name="adam_small", grid=(r // tr,), in_specs=[blk] * 4, out_specs=[blk] * 3,
                          out_shape=[SDS(w.shape, F32)] * 3, compiler_params=_cp("parallel"))(w, g, m, v)


def _after(a, token):
    return a if token is None else a + token[0:1, 0:1]


def _local_step(x, tgt, p, get_w, hook=lambda stage, layer, payload: None):
    s = x.shape[0]
    tm = _row_tile(s)
    wsb = p["gmlp_w_s"].astype(BF16)
    wstb = jnp.swapaxes(p["gmlp_w_s"], -1, -2).astype(BF16)
    bsb = jnp.broadcast_to(p["gmlp_b_s"][..., None], p["gmlp_w_s"].shape)
    wrb = p["lru_w_r"].astype(BF16)
    wib = p["lru_w_i"].astype(BF16)
    saved = []
    for l in range(2):
        win = get_w("w_in", l, x)
        z, h1 = _norm_inproj(x, _after(p["norm1_g"][l][None], hook("pre_inproj", l, win)), win, l, tm)
        ya = _gmlp_fwd(z, p["gmlp_ln_g"][l][None], p["gmlp_ln_b"][l][None], wsb[l], bsb[l], l, tm)
        a0, b0, a1, b1 = _lru_gates_fwd(z, p["conv_w"][l], p["conv_b"][l][None], wrb[l], wib[l],
                                        p["lru_b_r"][l], p["lru_b_i"][l], p["lru_lambda"][l], l, tm)
        h0, hr = _lru_scan(a0, b0, a1, b1, False, l)
        wout = get_w("w_out", l, h0)
        x1, mg = _merge_outproj(x, ya, h0, hr, z, wout, l, tm)
        wfi = get_w("w_ffn_in", l, x1)
        wfo = get_w("w_ffn_out", l, x1)
        x2, gu, h2 = _ffn_fwd(x1, _after(p["norm2_g"][l][None], hook("pre_ffn", l, wfi)), wfi, wfo, l, tm)
        saved.append((x, z, h1, ya, a0, a1, h0, hr, x1, mg, gu, h2, win, wout, wfi, wfo))
        x = x2
    dx, loss, dfg = _loss_head(x, p["final_g"][None], tgt, tm)
    small = [None, None]
    for l in (1, 0):
        x0, z, h1, ya, a0, a1, h0, hr, x1, mg, gu, h2, win, wout, wfi, wfo = saved[l]
        ff, dgu = _ffn_bwd_act(dx, wfo, gu, l, tm)
        d_wfo = _mm_tn(ff, pl.BlockSpec((None, s, FF_BLK), lambda j: (j, 0, 0)), dx, _resident((s, D)),
                       4, (4, FF_BLK, D), pl.BlockSpec((None, FF_BLK, D), lambda j: (j, 0, 0)),
                       f"dw_ffn_out_l{l}", a_is_transposed=False)
        dgu8 = dgu.reshape(N_DEV, s, FF_BLK)
        d_wfi = _mm_tn(h2, _resident((D, s)), dgu8, pl.BlockSpec((None, s, FF_BLK), lambda j: (j, 0, 0)),
                       N_DEV, (N_DEV, D, FF_BLK), pl.BlockSpec((None, D, FF_BLK), lambda j: (j, 0, 0)),
                       f"dw_ffn_in_l{l}")
        token = hook("ffn_partials", l, dict(w_ffn_out=d_wfo.reshape(N_DEV, D_FF // N_DEV, D), w_ffn_in=d_wfi))
        dx1, dg2 = _mm_nt_rms_bwd(
            dgu8, pl.BlockSpec((N_DEV, tm, FF_BLK), lambda i: (0, i, 0)), lambda r: [r[k] for k in range(N_DEV)],
            wfi.reshape(N_DEV, D, FF_BLK), x1, _after(p["norm2_g"][l][None], token), dx, f"ffn_bwd_dx_l{l}", tm)
        dz, dya, dh = _outproj_bwd_merge(dx1, wout, ya, h0, hr, z, l, tm)
        d_wout = _mm_tn(mg, _resident((D, s)), dx1, pl.BlockSpec((s, D // 2), lambda j: (0, j)),
                        2, (D, D), pl.BlockSpec((D, D // 2), lambda j: (0, j)), f"dw_out_l{l}")
        g1, g0 = _lru_scan(a1, dh, a0, dh, True, l)
        dxc, dwr, dwi, dbr, dbi, dlam = _lru_gates_bwd(
            z, h0, hr, g0, g1, p["conv_w"][l], p["conv_b"][l][None], wrb[l], wib[l],
            p["lru_b_r"][l], p["lru_b_i"][l], p["lru_lambda"][l], l, tm)
        dz, dcw, dcb = _conv_bwd(dz, dxc, z, p["conv_w"][l], l, tm)
        dz, dws, dbs, dlng, dlnb = _gmlp_bwd(dz, z, dya, p["gmlp_ln_g"][l][None], p["gmlp_ln_b"][l][None],
                                             wsb[l], wstb[l], bsb[l], l, tm)
        d_win = _mm_tn(h1, _resident((D, s)), dz, pl.BlockSpec((s, IN_BLK), lambda j: (0, j)),
                       N_DEV, (N_DEV, D, IN_BLK), pl.BlockSpec((None, D, IN_BLK), lambda j: (j, 0, 0)),
                       f"dw_in_l{l}")
        token = hook("mixer_partials", l, dict(w_out=d_wout.reshape(N_DEV, D // N_DEV, D), w_in=d_win))
        dx, dg1 = _mm_nt_rms_bwd(
            dz, pl.BlockSpec((tm, N_IN), lambda i: (i, 0)),
            lambda r: [r[:, k * IN_BLK:(k + 1) * IN_BLK] for k in range(N_DEV)],
            win, x0, _after(p["norm1_g"][l][None], token), dx1, f"inproj_bwd_dx_l{l}", tm)
        small[l] = dict(norm1_g=dg1[0], gmlp_ln_g=dlng[0], gmlp_ln_b=dlnb[0], gmlp_w_s=dws, gmlp_b_s=dbs,
                        conv_w=dcw, conv_b=dcb[0], lru_w_r=dwr, lru_b_r=dbr, lru_w_i=dwi, lru_b_i=dbi,
                        lru_lambda=dlam, norm2_g=dg2[0])
    small_g = {k: jnp.stack([small[0][k], small[1][k]]) for k in small[0]}
    small_g["final_g"] = dfg[0]
    return loss, dx, small_g


_REPL = ["norm1_g", "gmlp_ln_g", "gmlp_ln_b", "gmlp_w_s", "gmlp_b_s", "conv_b", "lru_w_r", "lru_w_i", "norm2_g", "final_g"]
_LANE_SHARDED = ["conv_w", "lru_b_r", "lru_b_i", "lru_lambda"]
_BIG = ["w_in", "w_out", "w_ffn_in", "w_ffn_out"]
_ORDER = ["norm1_g", "w_in", "gmlp_ln_g", "gmlp_ln_b", "gmlp_w_s", "gmlp_b_s", "conv_w", "conv_b", "lru_w_r", "lru_b_r",
          "lru_w_i", "lru_b_i", "lru_lambda", "w_out", "norm2_g", "w_ffn_in", "w_ffn_out", "final_g"]


def _pack(parts, rows):
    flat = jnp.concatenate([a.reshape(-1) for a in parts])
    return jnp.pad(flat, (0, rows * D - flat.shape[0])).reshape(rows, D)


def _unpack(flat, shapes):
    out, off = [], 0
    flat = flat.reshape(-1)
    for shp in shapes:
        n = 1
        for k in shp:
            n *= k
        out.append(flat[off:off + n].reshape(shp))
        off += n
    return out


def kernel(x, norm1_g, w_in, gmlp_ln_g, gmlp_ln_b, gmlp_w_s, gmlp_b_s, conv_w, conv_b, lru_w_r, lru_b_r, lru_w_i, lru_b_i, lru_lambda, w_out, norm2_g, w_ffn_in, w_ffn_out, final_g, loss_target, m_norm1_g, m_w_in, m_gmlp_ln_g, m_gmlp_ln_b, m_gmlp_w_s, m_gmlp_b_s, m_conv_w, m_conv_b, m_lru_w_r, m_lru_b_r, m_lru_w_i, m_lru_b_i, m_lru_lambda, m_w_out, m_norm2_g, m_w_ffn_in, m_w_ffn_out, m_final_g, v_norm1_g, v_w_in, v_gmlp_ln_g, v_gmlp_ln_b, v_gmlp_w_s, v_gmlp_b_s, v_conv_w, v_conv_b, v_lru_w_r, v_lru_b_r, v_lru_w_i, v_lru_b_i, v_lru_lambda, v_w_out, v_norm2_g, v_w_ffn_in, v_w_ffn_out, v_final_g):
    w = dict(norm1_g=norm1_g, w_in=w_in, gmlp_ln_g=gmlp_ln_g, gmlp_ln_b=gmlp_ln_b, gmlp_w_s=gmlp_w_s, gmlp_b_s=gmlp_b_s,
             conv_w=conv_w, conv_b=conv_b, lru_w_r=lru_w_r, lru_b_r=lru_b_r, lru_w_i=lru_w_i, lru_b_i=lru_b_i,
             lru_lambda=lru_lambda, w_out=w_out, norm2_g=norm2_g, w_ffn_in=w_ffn_in, w_ffn_out=w_ffn_out, final_g=final_g)
    mom = dict(norm1_g=m_norm1_g, w_in=m_w_in, gmlp_ln_g=m_gmlp_ln_g, gmlp_ln_b=m_gmlp_ln_b, gmlp_w_s=m_gmlp_w_s,
               gmlp_b_s=m_gmlp_b_s, conv_w=m_conv_w, conv_b=m_conv_b, lru_w_r=m_lru_w_r, lru_b_r=m_lru_b_r,
               lru_w_i=m_lru_w_i, lru_b_i=m_lru_b_i, lru_lambda=m_lru_lambda, w_out=m_w_out, norm2_g=m_norm2_g,
               w_ffn_in=m_w_ffn_in, w_ffn_out=m_w_ffn_out, final_g=m_final_g)
    var = dict(norm1_g=v_norm1_g, w_in=v_w_in, gmlp_ln_g=v_gmlp_ln_g, gmlp_ln_b=v_gmlp_ln_b, gmlp_w_s=v_gmlp_w_s,
               gmlp_b_s=v_gmlp_b_s, conv_w=v_conv_w, conv_b=v_conv_b, lru_w_r=v_lru_w_r, lru_b_r=v_lru_b_r,
               lru_w_i=v_lru_w_i, lru_b_i=v_lru_b_i, lru_lambda=v_lru_lambda, w_out=v_w_out, norm2_g=v_norm2_g,
               w_ffn_in=v_w_ffn_in, w_ffn_out=v_w_ffn_out, final_g=v_final_g)
    xi, yi, ci = _me()
    me = 4 * xi + 2 * yi + ci
    lane0 = me * HD

    lane_shapes = [w[k].shape for k in _LANE_SHARDED]
    lane_rows = sum(a[0] * a[1] for a in lane_shapes)
    packed = jnp.concatenate([w[k].reshape(-1, HD) for k in _LANE_SHARDED])
    packed = jnp.pad(packed, ((0, -lane_rows % 8), (0, 0)))
    lanes = _all_gather(packed, "gather_small")
    params = {k: w[k] for k in _REPL}
    off = 0
    for k, shp in zip(_LANE_SHARDED, lane_shapes):
        n = shp[0] * shp[1]
        params[k] = jnp.swapaxes(lanes[:, off:off + n], 0, 1).reshape(shp[0], shp[1], D)
        off += n

    me1 = jnp.reshape(me, (1,)).astype(jnp.int32)
    gathers = {}
    exchanges = {}
    views = dict(w_in=(N_DEV, D, IN_BLK), w_out=(D, D), w_ffn_in=(2, 4, D, FF_BLK), w_ffn_out=(4, FF_BLK, D))

    def start_gather(names, l, after):
        lands = [_cast_into_slot(w[k], l, me1, f"cast_{k}_l{l}") for k in names]
        started, tok = _gather_start(lands, after, f"gather_start_{'_'.join(names)}_l{l}")
        gathers.update({(k, l): h for k, h in zip(names, started)})
        return tok

    def get_w(k, l, after):
        return _gather_wait(gathers[(k, l)], after, f"gather_wait_{k}_l{l}").reshape(views[k])

    def hook(stage, l, payload):
        if stage == "pre_inproj":
            return start_gather(_BIG[1:], l, payload)
        if stage == "pre_ffn":
            return start_gather(_BIG[:1], l + 1, payload) if l == 0 else None
        started, tok = _exchange_start(list(payload.values()), f"exchange_start_{'_'.join(payload)}_l{l}")
        exchanges.update({(k, l): h for k, h in zip(payload, started)})
        return tok

    start_gather(_BIG[:1], 0, lanes)
    loss, dx, small_g = _local_step(x[0], loss_target[0], params, get_w, hook)

    out = {}
    for k in _BIG:
        res = None
        for l in (1, 0):
            got = _exchange_wait(exchanges[(k, l)], dx, f"exchange_wait_{k}_l{l}")
            res = _adam_shard(got, w[k], mom[k], var[k], l, res, f"adam_{k}_l{l}")
        out[k] = res

    names = _REPL + _LANE_SHARDED
    full_shapes = [small_g[k].shape for k in names]
    n_rows = sum(a.size for a in small_g.values()) // D
    shard_rows = -(-n_rows // (8 * N_DEV)) * 8
    g_flat = _pack([small_g[k] for k in names], shard_rows * N_DEV).reshape(N_DEV, shard_rows, D)
    mine = _sum8(_exchange_partials(g_flat, "exchange_small"), "sum_small")
    g_all = _all_gather(mine, "gather_small_grads")
    grads = dict(zip(names, _unpack(g_all, full_shapes)))
    for k in _LANE_SHARDED:
        grads[k] = lax.dynamic_slice_in_dim(grads[k], lane0, HD, axis=2)
    dev_shapes = [w[k].shape for k in names]
    rows = -(-sum(w[k].size for k in names) // (8 * D)) * 8
    flats = [_pack([src[k] for k in names], rows) for src in (w, grads, mom, var)]
    for nm, flat in zip(("delta", "m", "v"), _adam_flat(*flats)):
        for k, a in zip(names, _unpack(flat, dev_shapes)):
            out.setdefault(k, [grads[k], None, None, None])[("delta", "m", "v").index(nm) + 1] = a

    loss = lax.psum(loss[0, 0], MESH_AXES)
    return (loss, dx[None], *[out[k][0] for k in _ORDER], *[out[k][1] for k in _ORDER],
            *[out[k][2] for k in _ORDER], *[out[k][3] for k in _ORDER])
```

```python
import jax
import jax.numpy as jnp
from jax import lax
from jax.experimental import pallas as pl
from jax.experimental.pallas import tpu as pltpu

F32 = jnp.float32
BF16 = jnp.bfloat16
SDS = jax.ShapeDtypeStruct

D = 1024
N_IN = 6 * D
D_FF = 2816
N_DEV = 8
IN_BLK = N_IN // N_DEV
FF_BLK = 2 * D_FF // N_DEV
HEADS = 8
HD = 128
EPS = 1e-6
LRU_C = 8.0
MESH_AXES = ("x", "y", "c")

ADAM_LR = 0.001
ADAM_B1 = 0.9
ADAM_B2 = 0.999
ADAM_EPS = 1e-08
ADAM_WD = 0.01
ADAM_STEP = 10

VMEM_LIMIT = 56 * 2**20


def _cp(*sem, **kw):
    return pltpu.CompilerParams(dimension_semantics=sem, vmem_limit_bytes=VMEM_LIMIT, **kw)


def _row_tile(s):
    return 512 if s >= 1024 else s // 2


_GELU_C = 0.7978845608028654


def _gelu(x):
    t = jnp.tanh(_GELU_C * (x + 0.044715 * (x * x * x)))
    return 0.5 * x * (1.0 + t), t


def _gelu_grad(x, t):
    return 0.5 * (1.0 + t) + 0.5 * x * (1.0 - t * t) * (_GELU_C * (1.0 + 0.134145 * (x * x)))


def _sigmoid(x):
    return 0.5 + 0.5 * jnp.tanh(0.5 * x)


def _softplus(x):
    e = jnp.exp(-jnp.abs(x))
    w = 1.0 + e
    l1p = jnp.where(w == 1.0, e, jnp.log(w) * e / jnp.where(w == 1.0, 1.0, w - 1.0))
    return jnp.maximum(x, 0.0) + l1p


def _rms_fwd(x, g):
    r = lax.rsqrt(jnp.mean(x * x, axis=-1, keepdims=True) + EPS)
    return x * r * g


def _rms_bwd(x, g, dh):
    r = lax.rsqrt(jnp.mean(x * x, axis=-1, keepdims=True) + EPS)
    xh = x * r
    dxh = dh * g
    dx = r * (dxh - xh * jnp.mean(dxh * xh, axis=-1, keepdims=True))
    dg = jnp.sum(dh * xh, axis=0, keepdims=True)
    return dx, dg


def _dot(a, b):
    return jnp.dot(a, b, preferred_element_type=F32)


def _dot_nt(a, b):
    return lax.dot_general(a, b, (((1,), (1,)), ((), ())), preferred_element_type=F32)


def _dot_tn(a, b):
    return lax.dot_general(a, b, (((0,), (0,)), ((), ())), preferred_element_type=F32)


def _taps(prev, cur, nxt, tm):
    ext = jnp.concatenate([prev, cur, nxt], axis=0)
    n = tm + 16
    sl = slice(8, 8 + tm)
    return (pltpu.roll(ext, 2, 0)[sl], pltpu.roll(ext, 1, 0)[sl], cur,
            pltpu.roll(ext, n - 1, 0)[sl], pltpu.roll(ext, n - 2, 0)[sl])


def _halo_specs(tm, s, col):
    nb8 = s // 8
    r8 = tm // 8
    return (pl.BlockSpec((8, D), lambda i: (jnp.maximum(i * r8 - 1, 0), col)),
            pl.BlockSpec((tm, D), lambda i: (i, col)),
            pl.BlockSpec((8, D), lambda i: (jnp.minimum((i + 1) * r8, nb8 - 1), col)))


def _halo_flags(nt):
    i = pl.program_id(0)
    return (i > 0).astype(F32), (i < nt - 1).astype(F32)


def _full(shape):
    nd = len(shape)
    return pl.BlockSpec(shape, lambda *_: (0,) * nd)


def _resident(shape):
    nd = len(shape)
    return pl.BlockSpec(shape, lambda *_: (0,) * nd, pipeline_mode=pl.Buffered(1))


def _norm_inproj(x, g, w, layer, tm):
    s = x.shape[0]

    def body(x_ref, g_ref, w_ref, z_ref, ht_ref):
        h32 = _rms_fwd(x_ref[...], g_ref[...])
        ht_ref[...] = h32.T.astype(BF16)
        h = h32.astype(BF16)
        for j in range(N_DEV):
            z_ref[:, j * IN_BLK:(j + 1) * IN_BLK] = _dot(h, w_ref[j])

    return pl.pallas_call(
        body, name=f"norm_inproj_l{layer}", grid=(s // tm,),
        in_specs=[pl.BlockSpec((tm, D), lambda i: (i, 0)), _full((1, D)), _resident((N_DEV, D, IN_BLK))],
        out_specs=[pl.BlockSpec((tm, N_IN), lambda i: (i, 0)), pl.BlockSpec((D, tm), lambda i: (0, i))],
        out_shape=[SDS((s, N_IN), F32), SDS((D, s), BF16)],
        compiler_params=_cp("parallel"))(x, g, w)


def _gmlp_fwd(z, lng, lnb, ws, bsb, layer, tm):
    s = z.shape[0]

    def body(zu_ref, zv_ref, lng_ref, lnb_ref, ws_ref, bsb_ref, ya_ref):
        u, _ = _gelu(zu_ref[...])
        gv, _ = _gelu(zv_ref[...])
        xc = gv - jnp.mean(gv, axis=-1, keepdims=True)
        rstd = lax.rsqrt(jnp.mean(xc * xc, axis=-1, keepdims=True) + EPS)
        vb = (xc * rstd * lng_ref[...] + lnb_ref[...]).astype(BF16)
        for c in range(tm // HD):
            rs = slice(c * HD, (c + 1) * HD)
            for g in range(HEADS):
                cs = slice(g * HD, (g + 1) * HD)
                mixed = _dot(ws_ref[g], vb[rs, cs]) + bsb_ref[g]
                ya_ref[rs, cs] = u[rs, cs] * mixed

    return pl.pallas_call(
        body, name=f"gmlp_fwd_l{layer}", grid=(s // tm,),
        in_specs=[pl.BlockSpec((tm, D), lambda i: (i, 0)), pl.BlockSpec((tm, D), lambda i: (i, 1)),
                  _full((1, D)), _full((1, D)), _full((HEADS, HD, HD)), _full((HEADS, HD, HD))],
        out_specs=pl.BlockSpec((tm, D), lambda i: (i, 0)),
        out_shape=SDS((s, D), F32),
        compiler_params=_cp("parallel"))(z, z, lng, lnb, ws, bsb)


def _conv(taps, cw_ref, cb_ref):
    _, m1, c0, p1, p2 = taps
    return cb_ref[...] + m1 * cw_ref[0:1, :] + c0 * cw_ref[1:2, :] + p1 * cw_ref[2:3, :] + p2 * cw_ref[3:4, :]


def _heads_dot(xb, w_ref, d):
    return jnp.concatenate([_dot(xb[:, h * HD:(h + 1) * HD], w_ref[d, h]) for h in range(HEADS)], axis=1)


def _lru_gates(xc, xb, wr_ref, wi_ref, br_ref, bi_ref, lam_ref, d):
    sp = _softplus(-lam_ref[d:d + 1, :])
    r = _sigmoid(_heads_dot(xb, wr_ref, d) + br_ref[d:d + 1, :])
    ig = _sigmoid(_heads_dot(xb, wi_ref, d) + bi_ref[d:d + 1, :])
    la = (-LRU_C) * r * sp
    a = jnp.exp(la)
    q = jnp.maximum(jnp.tanh(-la) * (a * a + 1.0), 0.0)
    rq = jnp.where(q > 0.0, lax.rsqrt(jnp.where(q > 0.0, q, 1.0)), 0.0)
    return sp, r, ig, a, q * rq, rq


def _lru_gates_fwd(z, cw, cb, wr, wi, br, bi, lam, layer, tm):
    s = z.shape[0]
    nt = s // tm

    def body(zp_ref, zc_ref, zn_ref, cw_ref, cb_ref, wr_ref, wi_ref, br_ref, bi_ref, lam_ref,
             a0_ref, b0_ref, a1_ref, b1_ref):
        fp, fn = _halo_flags(nt)
        xc = _conv(_taps(zp_ref[...] * fp, zc_ref[...], zn_ref[...] * fn, tm), cw_ref, cb_ref)
        xb = xc.astype(BF16)
        for d, (a_ref, b_ref) in enumerate(((a0_ref, b0_ref), (a1_ref, b1_ref))):
            _, _, ig, a, mult, _ = _lru_gates(xc, xb, wr_ref, wi_ref, br_ref, bi_ref, lam_ref, d)
            a_ref[...] = a
            b_ref[...] = mult * (ig * xc)

    tile = pl.BlockSpec((tm, D), lambda i: (i, 0))
    return pl.pallas_call(
        body, name=f"lru_gates_fwd_l{layer}", grid=(nt,),
        in_specs=[*_halo_specs(tm, s, 2), _full((4, D)), _full((1, D)),
                  _full((2, HEADS, HD, HD)), _full((2, HEADS, HD, HD)), _full((2, D)), _full((2, D)), _full((2, D))],
        out_specs=[tile] * 4, out_shape=[SDS((s, D), F32)] * 4,
        compiler_params=_cp("parallel"))(z, z, z, cw, cb, wr, wi, br, bi, lam)


def _scan_group(a, x, c, reverse, bwd):
    row = lax.broadcasted_iota(jnp.int32, a.shape, 0)
    b = a * x if bwd else x
    for d in (1, 2, 4):
        keep = (row < 8 - d) if reverse else (row >= d)
        sh = 8 - d if reverse else d
        a_s = jnp.where(keep, pltpu.roll(a, sh, 0), 1.0)
        b_s = jnp.where(keep, pltpu.roll(b, sh, 0), 0.0)
        b = a * b_s + b
        a = a * a_s
    h = b + a * c
    new_c = h[0:1, :] if reverse else h[7:8, :]
    if not bwd:
        return h, new_c
    if reverse:
        prev = jnp.where(row < 7, pltpu.roll(h, 7, 0), c)
    else:
        prev = jnp.where(row >= 1, pltpu.roll(h, 1, 0), c)
    return x + prev, new_c


def _lru_scan(a_f, x_f, a_r, x_r, bwd, layer):
    s = a_f.shape[0]
    ts = min(1024, s // 2)
    cb = 512
    nt = s // ts
    ng = ts // 8

    def body(af_ref, xf_ref, ar_ref, xr_ref, of_ref, or_ref, cf, cr):
        @pl.when(pl.program_id(1) == 0)
        def _():
            cf[...] = jnp.zeros_like(cf)
            cr[...] = jnp.zeros_like(cr)

        def step(j, carry):
            c_f, c_r = carry
            rf = pl.multiple_of(j * 8, 8)
            rr = pl.multiple_of((ng - 1 - j) * 8, 8)
            o, c_f = _scan_group(af_ref[pl.ds(rf, 8), :], xf_ref[pl.ds(rf, 8), :], c_f, False, bwd)
            of_ref[pl.ds(rf, 8), :] = o
            o, c_r = _scan_group(ar_ref[pl.ds(rr, 8), :], xr_ref[pl.ds(rr, 8), :], c_r, True, bwd)
            or_ref[pl.ds(rr, 8), :] = o
            return c_f, c_r

        c_f, c_r = lax.fori_loop(0, ng, step, (cf[0:1, :], cr[0:1, :]), unroll=2)
        cf[...] = jnp.broadcast_to(c_f, cf.shape)
        cr[...] = jnp.broadcast_to(c_r, cr.shape)

    fwd = pl.BlockSpec((ts, cb), lambda c, t: (t, c))
    rev = pl.BlockSpec((ts, cb), lambda c, t: (nt - 1 - t, c))
    return pl.pallas_call(
        body, name=f"lru_scan_{'bwd' if bwd else 'fwd'}_l{layer}", grid=(D // cb, nt),
        in_specs=[fwd, fwd, rev, rev], out_specs=[fwd, rev],
        out_shape=[SDS((s, D), F32)] * 2,
        scratch_shapes=[pltpu.VMEM((8, cb), F32), pltpu.VMEM((8, cb), F32)],
        compiler_params=_cp("parallel", "arbitrary"))(a_f, x_f, a_r, x_r)


def _merge_outproj(x, ya, h0, h1, z, wo, layer, tm):
    s = x.shape[0]

    def body(x_ref, ya_ref, h0_ref, h1_ref, zg_ref, za_ref, zb_ref, wo_ref, x1_ref, mg_ref):
        gg, _ = _gelu(zg_ref[...])
        yb = (h0_ref[...] + h1_ref[...]) * gg
        m32 = _sigmoid(za_ref[...]) * ya_ref[...] + _sigmoid(zb_ref[...]) * yb
        mg_ref[...] = m32.T.astype(BF16)
        x1_ref[...] = x_ref[...] + _dot(m32.astype(BF16), wo_ref[...])

    tile = pl.BlockSpec((tm, D), lambda i: (i, 0))
    return pl.pallas_call(
        body, name=f"merge_outproj_l{layer}", grid=(s // tm,),
        in_specs=[tile, tile, tile, tile] + [pl.BlockSpec((tm, D), lambda i, c=c: (i, c)) for c in (3, 4, 5)]
        + [_full((D, D))],
        out_specs=[tile, pl.BlockSpec((D, tm), lambda i: (0, i))], out_shape=[SDS((s, D), F32), SDS((D, s), BF16)],
        compiler_params=_cp("parallel"))(x, ya, h0, h1, z, z, z, wo)


def _ffn_fwd(x1, g, wfi, wfo, layer, tm):
    s = x1.shape[0]

    def body(x_ref, g_ref, wi_ref, wo_ref, x2_ref, gu_ref, h_ref):
        x = x_ref[...]
        h = _rms_fwd(x, g_ref[...]).astype(BF16)
        h_ref[...] = h
        acc = x
        for k in range(4):
            gate = _dot_nt(h, wi_ref[0, k])
            up = _dot_nt(h, wi_ref[1, k])
            gu_ref[0, k] = gate.astype(BF16)
            gu_ref[1, k] = up.astype(BF16)
            acc = acc + _dot((gate * _sigmoid(gate) * up).astype(BF16), wo_ref[k])
        x2_ref[...] = acc

    tile = pl.BlockSpec((tm, D), lambda i: (i, 0))
    return pl.pallas_call(
        body, name=f"ffn_fwd_l{layer}", grid=(s // tm,),
        in_specs=[tile, _full((1, D)), _resident((2, 4, FF_BLK, D)), _resident((4, FF_BLK, D))],
        out_specs=[tile, pl.BlockSpec((2, 4, tm, FF_BLK), lambda i: (0, 0, i, 0)), tile],
        out_shape=[SDS((s, D), F32), SDS((2, 4, s, FF_BLK), BF16), SDS((s, D), BF16)],
        compiler_params=_cp("parallel"))(x1, g, wfi, wfo)


def _loss_head(x, g, tgt, tm):
    s = x.shape[0]

    def body(x_ref, g_ref, t_ref, dx_ref, loss_ref, dg_ref):
        @pl.when(pl.program_id(0) == 0)
        def _():
            loss_ref[...] = jnp.zeros_like(loss_ref)
            dg_ref[...] = jnp.zeros_like(dg_ref)

        x = x_ref[...]
        gv = g_ref[...]
        e = _rms_fwd(x, gv) - t_ref[...]
        rows = jnp.sum(e * e, axis=-1, keepdims=True)
        loss_ref[...] += (0.5 / D) * jnp.sum(rows, axis=0, keepdims=True)
        dx, dg = _rms_bwd(x, gv, e * (1.0 / D))
        dx_ref[...] = dx
        dg_ref[...] += dg

    tile = pl.BlockSpec((tm, D), lambda i: (i, 0))
    return pl.pallas_call(
        body, name="loss_head", grid=(s // tm,),
        in_specs=[tile, _full((1, D)), tile],
        out_specs=[tile, _full((1, 1)), _full((1, D))],
        out_shape=[SDS((s, D), F32), SDS((1, 1), F32), SDS((1, D), F32)],
        compiler_params=_cp("arbitrary"))(x, g, tgt)


def _ffn_bwd_act(dx2, wfo, gu, layer, tm):
    s = dx2.shape[0]

    def body(dx_ref, wo_ref, gu_ref, ff_ref, dgu_ref):
        dxb = dx_ref[...].astype(BF16)
        for k in range(4):
            dff = _dot_nt(dxb, wo_ref[k])
            gate = gu_ref[0, k].astype(F32)
            up = gu_ref[1, k].astype(F32)
            sg = _sigmoid(gate)
            sl = gate * sg
            ff_ref[k] = (sl * up).astype(BF16)
            dgu_ref[0, k] = (dff * up * (sg * (1.0 + gate * (1.0 - sg)))).astype(BF16)
            dgu_ref[1, k] = (dff * sl).astype(BF16)

    blk = pl.BlockSpec((2, 4, tm, FF_BLK), lambda i: (0, 0, i, 0))
    return pl.pallas_call(
        body, name=f"ffn_bwd_act_l{layer}", grid=(s // tm,),
        in_specs=[pl.BlockSpec((tm, D), lambda i: (i, 0)), _resident((4, FF_BLK, D)), blk],
        out_specs=[pl.BlockSpec((4, tm, FF_BLK), lambda i: (0, i, 0)), blk],
        out_shape=[SDS((4, s, FF_BLK), BF16), SDS((2, 4, s, FF_BLK), BF16)],
        compiler_params=_cp("parallel"))(dx2, wfo, gu)


def _mm_nt_rms_bwd(a, a_spec, a_blocks, w, w_is_transposed, x, g, dres, name, tm):
    s = x.shape[0]

    def body(a_ref, w_ref, x_ref, g_ref, dres_ref, dx_ref, dg_ref):
        @pl.when(pl.program_id(0) == 0)
        def _():
            dg_ref[...] = jnp.zeros_like(dg_ref)

        dh = None
        for k, blk in enumerate(a_blocks(a_ref)):
            part = _dot(blk, w_ref[k]) if w_is_transposed else _dot_nt(blk, w_ref[k])
            dh = part if dh is None else dh + part
        dx, dg = _rms_bwd(x_ref[...], g_ref[...], dh)
        dx_ref[...] = dres_ref[...] + dx
        dg_ref[...] += dg

    tile = pl.BlockSpec((tm, D), lambda i: (i, 0))
    return pl.pallas_call(
        body, name=name, grid=(s // tm,),
        in_specs=[a_spec, _resident(w.shape), tile, _full((1, D)), tile],
        out_specs=[tile, _full((1, D))], out_shape=[SDS((s, D), F32), SDS((1, D), F32)],
        compiler_params=_cp("arbitrary"))(a, w, x, g, dres)


def _mm_tn(a, a_spec, b, b_spec, nb, out_shape, out_spec, name, a_is_transposed=True):
    def body(a_ref, b_ref, o_ref):
        bb = b_ref[...].astype(BF16)
        o_ref[...] = (_dot(a_ref[...], bb) if a_is_transposed else _dot_tn(a_ref[...], bb)).astype(BF16)

    return pl.pallas_call(
        body, name=name, grid=(nb,), in_specs=[a_spec, b_spec], out_specs=out_spec,
        out_shape=SDS(out_shape, BF16), compiler_params=_cp("parallel"))(a, b)


def _outproj_bwd_merge(dx1, wo, ya, h0, h1, z, layer, tm):
    s = dx1.shape[0]

    def body(dx_ref, wo_ref, ya_ref, h0_ref, h1_ref, zg_ref, za_ref, zb_ref, dz_ref, dya_ref, dh_ref):
        dm = _dot_nt(dx_ref[...].astype(BF16), wo_ref[...])
        sa = _sigmoid(za_ref[...])
        sb = _sigmoid(zb_ref[...])
        zg = zg_ref[...]
        gg, tg = _gelu(zg)
        hs = h0_ref[...] + h1_ref[...]
        dyb = dm * sb
        dya_ref[...] = dm * sa
        dh_ref[...] = dyb * gg
        dz_ref[:, 0:D] = (dyb * hs * _gelu_grad(zg, tg)).astype(BF16)
        dz_ref[:, D:2 * D] = (dm * ya_ref[...] * (sa * (1.0 - sa))).astype(BF16)
        dz_ref[:, 2 * D:3 * D] = (dm * (hs * gg) * (sb * (1.0 - sb))).astype(BF16)

    tile = pl.BlockSpec((tm, D), lambda i: (i, 0))
    return pl.pallas_call(
        body, name=f"outproj_bwd_merge_l{layer}", grid=(s // tm,),
        in_specs=[tile, _full((D, D)), tile, tile, tile]
        + [pl.BlockSpec((tm, D), lambda i, c=c: (i, c)) for c in (3, 4, 5)],
        out_specs=[pl.BlockSpec((tm, 3 * D), lambda i: (i, 1)), tile, tile],
        out_shape=[SDS((s, N_IN), BF16), SDS((s, D), F32), SDS((s, D), F32)],
        compiler_params=_cp("parallel"))(dx1, wo, ya, h0, h1, z, z, z)


def _lru_gates_bwd(z, h0, h1, g0, g1, cw, cb, wr, wi, br, bi, lam, layer, tm):
    s = z.shape[0]
    nt = s // tm

    def body(zp_ref, zc_ref, zn_ref, h0p_ref, h0_ref, h1_ref, h1n_ref, g0_ref, g1_ref,
             cw_ref, cb_ref, wr_ref, wi_ref, br_ref, bi_ref, lam_ref,
             dxc_ref, dwr_ref, dwi_ref, dbr_ref, dbi_ref, dlam_ref):
        i = pl.program_id(0)
        fp, fn = _halo_flags(nt)

        @pl.when(i == 0)
        def _():
            for r in (dwr_ref, dwi_ref, dbr_ref, dbi_ref, dlam_ref):
                r[...] = jnp.zeros_like(r)

        xc = _conv(_taps(zp_ref[...] * fp, zc_ref[...], zn_ref[...] * fn, tm), cw_ref, cb_ref)
        xb = xc.astype(BF16)
        zeros8 = jnp.zeros((8, D), F32)
        h_prev = _taps(h0p_ref[...] * fp, h0_ref[...], zeros8, tm)[1]
        h_next = _taps(zeros8, h1_ref[...], h1n_ref[...] * fn, tm)[3]
        dxc = jnp.zeros((tm, D), F32)
        for d, (g_ref, hsh) in enumerate(((g0_ref, h_prev), (g1_ref, h_next))):
            sp, r, ig, a, mult, rmult = _lru_gates(xc, xb, wr_ref, wi_ref, br_ref, bi_ref, lam_ref, d)
            db = g_ref[...]
            da = db * hsh
            dmult = db * (ig * xc)
            di = db * (mult * xc)
            dxc = dxc + db * (mult * ig)
            dla = da * a - dmult * (a * a * rmult)
            dlam_ref[d:d + 1, :] += jnp.sum(dla * r, axis=0, keepdims=True) * (-LRU_C)
            dpr = dla * sp * (-LRU_C) * (r * (1.0 - r))
            dpi = di * (ig * (1.0 - ig))
            dbr_ref[d:d + 1, :] += jnp.sum(dpr, axis=0, keepdims=True)
            dbi_ref[d:d + 1, :] += jnp.sum(dpi, axis=0, keepdims=True)
            dprb = dpr.astype(BF16)
            dpib = dpi.astype(BF16)
            parts = []
            for h in range(HEADS):
                cs = slice(h * HD, (h + 1) * HD)
                dwr_ref[d, h] += _dot_tn(xb[:, cs], dprb[:, cs])
                dwi_ref[d, h] += _dot_tn(xb[:, cs], dpib[:, cs])
                parts.append(_dot_nt(dprb[:, cs], wr_ref[d, h]) + _dot_nt(dpib[:, cs], wi_ref[d, h]))
            dxc = dxc + jnp.concatenate(parts, axis=1)
        dxc_ref[...] = dxc

        @pl.when(i == nt - 1)
        def _():
            dlam_ref[...] = dlam_ref[...] * (-_sigmoid(-lam_ref[...]))

    tile = pl.BlockSpec((tm, D), lambda i: (i, 0))
    zp, zc, zn = _halo_specs(tm, s, 2)
    hp, hc, hn = _halo_specs(tm, s, 0)
    wspec = _full((2, HEADS, HD, HD))
    return pl.pallas_call(
        body, name=f"lru_gates_bwd_l{layer}", grid=(nt,),
        in_specs=[zp, zc, zn, hp, hc, hc, hn, tile, tile, _full((4, D)), _full((1, D)),
                  wspec, wspec, _full((2, D)), _full((2, D)), _full((2, D))],
        out_specs=[tile, wspec, wspec, _full((2, D)), _full((2, D)), _full((2, D))],
        out_shape=[SDS((s, D), F32), SDS((2, HEADS, HD, HD), F32), SDS((2, HEADS, HD, HD), F32),
                   SDS((2, D), F32), SDS((2, D), F32), SDS((2, D), F32)],
        compiler_params=_cp("arbitrary"))(z, z, z, h0, h0, h1, h1, g0, g1, cw, cb, wr, wi, br, bi, lam)


def _conv_bwd(dz, dxc, z, cw, layer, tm):
    s = z.shape[0]
    nt = s // tm

    def body(dz_in, dp_ref, dc_ref, dn_ref, zp_ref, zc_ref, zn_ref, cw_ref, dz_ref, dcw_ref, dcb_ref):
        del dz_in
        fp, fn = _halo_flags(nt)

        @pl.when(pl.program_id(0) == 0)
        def _():
            dcw_ref[...] = jnp.zeros_like(dcw_ref)
            dcb_ref[...] = jnp.zeros_like(dcb_ref)

        dxc = dc_ref[...]
        dm2, dm1, _, dp1, _ = _taps(dp_ref[...] * fp, dxc, dn_ref[...] * fn, tm)
        dz_ref[...] = (cw_ref[0:1, :] * dp1 + cw_ref[1:2, :] * dxc + cw_ref[2:3, :] * dm1
                       + cw_ref[3:4, :] * dm2).astype(BF16)
        _, zm1, z0, zp1, zp2 = _taps(zp_ref[...] * fp, zc_ref[...], zn_ref[...] * fn, tm)
        for k, zt in enumerate((zm1, z0, zp1, zp2)):
            dcw_ref[k:k + 1, :] += jnp.sum(dxc * zt, axis=0, keepdims=True)
        dcb_ref[...] += jnp.sum(dxc, axis=0, keepdims=True)

    return pl.pallas_call(
        body, name=f"conv_bwd_l{layer}", grid=(nt,),
        in_specs=[pl.BlockSpec(memory_space=pl.ANY), *_halo_specs(tm, s, 0), *_halo_specs(tm, s, 2), _full((4, D))],
        out_specs=[pl.BlockSpec((tm, D), lambda i: (i, 2)), _full((4, D)), _full((1, D))],
        out_shape=[SDS((s, N_IN), BF16), SDS((4, D), F32), SDS((1, D), F32)],
        input_output_aliases={0: 0},
        compiler_params=_cp("arbitrary"))(dz, dxc, dxc, dxc, z, z, z, cw)


def _gmlp_bwd(dz, z, dya, lng, lnb, ws, wst, bsb, layer, tm):
    s = z.shape[0]
    nt = s // tm

    def body(dz_in, zu_ref, zv_ref, dya_ref, lng_ref, lnb_ref, ws_ref, wst_ref, bsb_ref,
             dz_ref, dws_ref, dbs_ref, dlng_ref, dlnb_ref, du_s, dv_s, dbs_acc):
        del dz_in
        i = pl.program_id(0)

        @pl.when(i == 0)
        def _():
            for r in (dws_ref, dlng_ref, dlnb_ref, dbs_acc):
                r[...] = jnp.zeros_like(r)

        zu = zu_ref[...]
        zv = zv_ref[...]
        u, tu = _gelu(zu)
        gv, tv = _gelu(zv)
        xc = gv - jnp.mean(gv, axis=-1, keepdims=True)
        rstd = lax.rsqrt(jnp.mean(xc * xc, axis=-1, keepdims=True) + EPS)
        xh = xc * rstd
        lng_v = lng_ref[...]
        vb = (xh * lng_v + lnb_ref[...]).astype(BF16)
        dya = dya_ref[...]
        for c in range(tm // HD):
            rs = slice(c * HD, (c + 1) * HD)
            for g in range(HEADS):
                cs = slice(g * HD, (g + 1) * HD)
                vblk = vb[rs, cs]
                mixed = _dot(ws_ref[g], vblk) + bsb_ref[g]
                du_s[rs, cs] = dya[rs, cs] * mixed
                dmx = dya[rs, cs] * u[rs, cs]
                dbs_acc[g] += dmx
                dmxb = dmx.astype(BF16)
                dws_ref[g] += _dot_nt(dmxb, vblk)
                dv_s[rs, cs] = _dot(wst_ref[g], dmxb)
        dv = dv_s[...]
        dlng_ref[...] += jnp.sum(dv * xh, axis=0, keepdims=True)
        dlnb_ref[...] += jnp.sum(dv, axis=0, keepdims=True)
        dxh = dv * lng_v
        dgv = rstd * (dxh - jnp.mean(dxh, axis=-1, keepdims=True)
                      - xh * jnp.mean(dxh * xh, axis=-1, keepdims=True))
        dz_ref[:, 0:D] = (du_s[...] * _gelu_grad(zu, tu)).astype(BF16)
        dz_ref[:, D:2 * D] = (dgv * _gelu_grad(zv, tv)).astype(BF16)

        @pl.when(i == nt - 1)
        def _():
            for g in range(HEADS):
                dbs_ref[g:g + 1, :] = jnp.sum(dbs_acc[g].T, axis=0, keepdims=True)

    tile = pl.BlockSpec((tm, D), lambda i: (i, 0))
    wspec = _full((HEADS, HD, HD))
    return pl.pallas_call(
        body, name=f"gmlp_bwd_l{layer}", grid=(nt,),
        in_specs=[pl.BlockSpec(memory_space=pl.ANY), tile, pl.BlockSpec((tm, D), lambda i: (i, 1)), tile,
                  _full((1, D)), _full((1, D)), wspec, wspec, wspec],
        out_specs=[pl.BlockSpec((tm, 2 * D), lambda i: (i, 0)), wspec, _full((HEADS, HD)), _full((1, D)), _full((1, D))],
        out_shape=[SDS((s, N_IN), BF16), SDS((HEADS, HD, HD), F32), SDS((HEADS, HD), F32),
                   SDS((1, D), F32), SDS((1, D), F32)],
        scratch_shapes=[pltpu.VMEM((tm, D), F32), pltpu.VMEM((tm, D), F32), pltpu.VMEM((HEADS, HD, HD), F32)],
        input_output_aliases={0: 0},
        compiler_params=_cp("arbitrary"))(dz, z, z, dya, lng, lnb, ws, wst, bsb)


def _me():
    return lax.axis_index("x"), lax.axis_index("y"), lax.axis_index("c")


def _peer(m):
    x, y, c = _me()
    px = 1 - x if m & 4 else x
    py = 1 - y if m & 2 else y
    pc = 1 - c if m & 1 else c
    return (px, py, pc), 4 * px + 2 * py + pc


_ANY = pl.BlockSpec(memory_space=pl.ANY)
_EXCHANGE_SEMS = [pltpu.SemaphoreType.DMA((N_DEV - 1,)), pltpu.SemaphoreType.DMA((N_DEV - 1,)), pltpu.SemaphoreType.DMA(())]


def _all_gather(v, name):
    def body(v_ref, o_ref, send_sems, recv_sems, local_sem):
        x, y, c = _me()
        me = 4 * x + 2 * y + c
        local = pltpu.make_async_copy(v_ref, o_ref.at[me], local_sem)
        local.start()
        sends = []
        for m in range(1, N_DEV):
            dev, _ = _peer(m)
            cp = pltpu.make_async_remote_copy(v_ref, o_ref.at[me], send_sems.at[m - 1], recv_sems.at[m - 1],
                                              device_id=dev, device_id_type=pl.DeviceIdType.MESH)
            cp.start()
            sends.append(cp)
        for m in range(1, N_DEV):
            dev, blk = _peer(m)
            pltpu.make_async_remote_copy(v_ref, o_ref.at[blk], send_sems.at[m - 1], recv_sems.at[m - 1],
                                         device_id=dev, device_id_type=pl.DeviceIdType.MESH).wait_recv()
        for cp in sends:
            cp.wait_send()
        local.wait()

    return pl.pallas_call(
        body, name=name, in_specs=[_ANY], out_specs=_ANY,
        out_shape=SDS((N_DEV,) + v.shape, v.dtype), scratch_shapes=_EXCHANGE_SEMS)(v)


_HBM = pl.BlockSpec(memory_space=pltpu.HBM)
_SEM = pl.BlockSpec(memory_space=pltpu.SEMAPHORE)
_EFFECT = pltpu.CompilerParams(has_side_effects=pltpu.SideEffectType.DATAFLOW_SIDE_EFFECTING)
_PEER_SEMS = pltpu.SemaphoreType.DMA((N_DEV - 1,))


def _in_hbm(a):
    return pltpu.with_memory_space_constraint(a, pltpu.HBM)


def _remote(src, dst, send_sems, recv_sems, m):
    dev, _ = _peer(m)
    return pltpu.make_async_remote_copy(src, dst, send_sems.at[m - 1], recv_sems.at[m - 1],
                                        device_id=dev, device_id_type=pl.DeviceIdType.MESH)


def _gather_start(lands, after, name):
    n = len(lands)

    def body(*refs):
        land = refs[:n]
        sems = refs[n + 1:3 * n + 1]
        token = refs[-1]
        x, y, c = _me()
        me = 4 * x + 2 * y + c
        for t in range(n):
            for m in range(1, N_DEV):
                _remote(land[t].at[me], land[t].at[me], sems[2 * t], sems[2 * t + 1], m).start()
        token[...] = jnp.zeros_like(token)

    res = pl.pallas_call(
        body, name=name, in_specs=[_HBM] * n + [_ANY],
        out_specs=[_SEM] * (2 * n) + [_HBM] * n + [pl.BlockSpec(memory_space=pltpu.VMEM)],
        out_shape=[_PEER_SEMS] * (2 * n) + [pltpu.HBM(a.shape, a.dtype) for a in lands] + [SDS((8, 128), F32)],
        input_output_aliases={t: 2 * n + t for t in range(n)},
        compiler_params=_EFFECT)(*[_in_hbm(a) for a in lands], after)
    return [(res[2 * t], res[2 * t + 1], res[2 * n + t]) for t in range(n)], res[-1]


def _gather_wait(handle, after, name):
    send_sems, recv_sems, land = handle

    def body(land_ref, ssem, rsem, after_ref, out_ref):
        del after_ref, out_ref
        x, y, c = _me()
        me = 4 * x + 2 * y + c
        for m in range(1, N_DEV):
            _, blk = _peer(m)
            cp = _remote(land_ref.at[me], land_ref.at[blk], ssem, rsem, m)
            cp.wait_send()
            cp.wait_recv()

    return pl.pallas_call(
        body, name=name, in_specs=[_HBM, _SEM, _SEM, _ANY], out_specs=_HBM,
        out_shape=pltpu.HBM(land.shape, land.dtype), input_output_aliases={0: 0},
        compiler_params=_EFFECT)(land, send_sems, recv_sems, after)


def _exchange_start(ps, name):
    n = len(ps)

    def body(*refs):
        p = refs[:n]
        got = refs[n:2 * n]
        sems = refs[2 * n:5 * n]
        token = refs[-1]
        x, y, c = _me()
        me = 4 * x + 2 * y + c
        for t in range(n):
            pltpu.make_async_copy(p[t].at[me], got[t].at[me], sems[3 * t + 2]).start()
            for m in range(1, N_DEV):
                _, blk = _peer(m)
                _remote(p[t].at[blk], got[t].at[me], sems[3 * t], sems[3 * t + 1], m).start()
        token[...] = jnp.zeros_like(token)

    res = pl.pallas_call(
        body, name=name, in_specs=[_HBM] * (2 * n),
        out_specs=[_SEM] * (3 * n) + [_HBM] * (2 * n) + [pl.BlockSpec(memory_space=pltpu.VMEM)],
        out_shape=[_PEER_SEMS, _PEER_SEMS, pltpu.SemaphoreType.DMA(())] * n
        + [pltpu.HBM(a.shape, a.dtype) for a in ps] * 2 + [SDS((8, 128), F32)],
        input_output_aliases={t: 3 * n + t for t in range(2 * n)},
        compiler_params=_EFFECT)(*[_in_hbm(a) for a in ps], *[_in_hbm(lax.empty(a.shape, a.dtype)) for a in ps])
    return [(res[3 * t], res[3 * t + 1], res[3 * t + 2], res[3 * n + t], res[4 * n + t]) for t in range(n)], res[-1]


def _exchange_wait(handle, after, name):
    send_sems, recv_sems, local_sem, p, got = handle

    def body(p_ref, got_ref, ssem, rsem, lsem, after_ref, p_out, got_out):
        del after_ref, p_out, got_out
        x, y, c = _me()
        me = 4 * x + 2 * y + c
        pltpu.make_async_copy(p_ref.at[me], got_ref.at[me], lsem).wait()
        for m in range(1, N_DEV):
            _, blk = _peer(m)
            cp = _remote(p_ref.at[blk], got_ref.at[blk], ssem, rsem, m)
            cp.wait_send()
            cp.wait_recv()

    return pl.pallas_call(
        body, name=name, in_specs=[_HBM, _HBM, _SEM, _SEM, _SEM, _ANY], out_specs=[_HBM, _HBM],
        out_shape=[pltpu.HBM(p.shape, p.dtype), pltpu.HBM(got.shape, got.dtype)],
        input_output_aliases={0: 0, 1: 1}, compiler_params=_EFFECT)(p, got, send_sems, recv_sems, local_sem, after)[1]


def _cast_into_slot(w, layer, me1, name):
    _, r, c = w.shape
    tr = next(t for t in (256, 352, r) if r % t == 0)

    def body(me_ref, w_ref, o_ref):
        del me_ref
        o_ref[...] = w_ref[...].astype(BF16)

    return pl.pallas_call(
        body, name=name,
        grid_spec=pltpu.PrefetchScalarGridSpec(
            num_scalar_prefetch=1, grid=(r // tr,),
            in_specs=[pl.BlockSpec((None, tr, c), lambda i, me: (layer, i, 0))],
            out_specs=pl.BlockSpec((None, tr, c), lambda i, me: (me[0], i, 0))),
        out_shape=SDS((N_DEV, r, c), BF16), compiler_params=_cp("arbitrary"))(me1, w)


def _sum8_into_slot(p, me1, name):
    _, r, c = p.shape

    def body(me_ref, p_ref, o_ref):
        del me_ref
        acc = p_ref[0]
        for k in range(1, N_DEV):
            acc = acc + p_ref[k]
        o_ref[...] = acc

    return pl.pallas_call(
        body, name=name,
        grid_spec=pltpu.PrefetchScalarGridSpec(
            num_scalar_prefetch=1, grid=(1,),
            in_specs=[pl.BlockSpec(p.shape, lambda i, me: (0, 0, 0))],
            out_specs=pl.BlockSpec((None, r, c), lambda i, me: (me[0], 0, 0))),
        out_shape=SDS(p.shape, F32), compiler_params=_cp("arbitrary"))(me1, p)


def _adamw(w, g, m, v):
    m = ADAM_B1 * m + (1.0 - ADAM_B1) * g
    v = ADAM_B2 * v + (1.0 - ADAM_B2) * (g * g)
    m_hat = m / (1.0 - ADAM_B1 ** ADAM_STEP)
    v_hat = v / (1.0 - ADAM_B2 ** ADAM_STEP)
    delta = -ADAM_LR * (m_hat / (jnp.sqrt(v_hat) + ADAM_EPS) + ADAM_WD * w)
    return delta, m, v


def _adam_shard(parts, w, m, v, layer, prev, name):
    _, r, c = parts.shape
    tr = next(t for t in (256, 352, r) if r % t == 0)
    n_prev = 0 if prev is None else 4

    def body(*refs):
        p_ref, w_ref, m_ref, v_ref = refs[:4]
        g_ref, d_ref, nm_ref, nv_ref = refs[4 + n_prev:]
        g = p_ref[0].astype(F32)
        for k in range(1, N_DEV):
            g = g + p_ref[k].astype(F32)
        delta, nm, nv = _adamw(w_ref[...], g, m_ref[...], v_ref[...])
        g_ref[...] = g
        d_ref[...] = delta
        nm_ref[...] = nm
        nv_ref[...] = nv

    blk = pl.BlockSpec((None, tr, c), lambda i: (layer, i, 0))
    return pl.pallas_call(
        body, name=name, grid=(r // tr,),
        in_specs=[pl.BlockSpec((N_DEV, tr, c), lambda i: (0, i, 0)), blk, blk, blk] + [_ANY] * n_prev,
        out_specs=[blk] * 4, out_shape=[SDS(w.shape, F32)] * 4,
        input_output_aliases={4 + k: k for k in range(n_prev)},
        compiler_params=_cp("parallel"))(parts, w, m, v, *(prev or ()))


def _adam_flat(w, g, m, v):
    r = w.shape[0]
    tr = r // 2 if r % 16 == 0 else r

    def body(w_ref, g_ref, m_ref, v_ref, d_ref, nm_ref, nv_ref):
        delta, nm, nv = _adamw(w_ref[...], g_ref[...], m_ref[...], v_ref[...])
        d_ref[...] = delta
        nm_ref[...] = nm
        nv_ref[...] = nv

    blk = pl.BlockSpec((tr, D), lambda i: (i, 0))
    return pl.pallas_call(body, name="adam_small", grid=(r // tr,), in_specs=[blk] * 4, out_specs=[blk] * 3,
                          out_shape=[SDS(w.shape, F32)] * 3, compiler_params=_cp("parallel"))(w, g, m, v)


def _after(a, *tokens):
    for token in tokens:
        if token is not None:
            a = a + token[0:1, 0:1]
    return a


def _local_step(x, tgt, p, get_w, hook=lambda stage, layer, payload: None):
    s = x.shape[0]
    tm = _row_tile(s)
    wsb = p["gmlp_w_s"].astype(BF16)
    wstb = jnp.swapaxes(p["gmlp_w_s"], -1, -2).astype(BF16)
    bsb = jnp.broadcast_to(p["gmlp_b_s"][..., None], p["gmlp_w_s"].shape)
    wrb = p["lru_w_r"].astype(BF16)
    wib = p["lru_w_i"].astype(BF16)
    saved = []
    for l in range(2):
        win = get_w("w_in", l, x)
        z, h1 = _norm_inproj(x, _after(p["norm1_g"][l][None], hook("pre_inproj", l, win)), win, l, tm)
        ya = _gmlp_fwd(z, p["gmlp_ln_g"][l][None], p["gmlp_ln_b"][l][None], wsb[l], bsb[l], l, tm)
        a0, b0, a1, b1 = _lru_gates_fwd(z, p["conv_w"][l], p["conv_b"][l][None], wrb[l], wib[l],
                                        p["lru_b_r"][l], p["lru_b_i"][l], p["lru_lambda"][l], l, tm)
        h0, hr = _lru_scan(a0, b0, a1, b1, False, l)
        wout = get_w("w_out", l, h0)
        x1, mg = _merge_outproj(x, ya, h0, hr, z, wout, l, tm)
        wfi = get_w("w_ffn_in", l, x1)
        wfo = get_w("w_ffn_out", l, x1)
        x2, gu, h2 = _ffn_fwd(x1, _after(p["norm2_g"][l][None], hook("pre_ffn", l, wfi)), wfi, wfo, l, tm)
        saved.append((x, z, h1, ya, a0, a1, h0, hr, x1, mg, gu, h2, win, wout, wfi, wfo))
        x = x2
    dx, loss, dfg = _loss_head(x, p["final_g"][None], tgt, tm)
    pending = None
    for l in (1, 0):
        x0, z, h1, ya, a0, a1, h0, hr, x1, mg, gu, h2, win, wout, wfi, wfo = saved[l]
        ff, dgu = _ffn_bwd_act(dx, wfo, gu, l, tm)
        d_wfo = _mm_tn(ff, pl.BlockSpec((None, s, FF_BLK), lambda j: (j, 0, 0)), dx, _resident((s, D)),
                       4, (4, FF_BLK, D), pl.BlockSpec((None, FF_BLK, D), lambda j: (j, 0, 0)),
                       f"dw_ffn_out_l{l}", a_is_transposed=False)
        dgu8 = dgu.reshape(N_DEV, s, FF_BLK)
        d_wfi = _mm_tn(dgu8, pl.BlockSpec((None, s, FF_BLK), lambda j: (j, 0, 0)), h2, _resident((s, D)),
                       N_DEV, (N_DEV, FF_BLK, D), pl.BlockSpec((None, FF_BLK, D), lambda j: (j, 0, 0)),
                       f"dw_ffn_in_l{l}", a_is_transposed=False)
        token = hook("ffn_partials", l, dict(w_ffn_out=d_wfo.reshape(N_DEV, D_FF // N_DEV, D), w_ffn_in=d_wfi))
        dx1, dg2 = _mm_nt_rms_bwd(
            dgu8, pl.BlockSpec((N_DEV, tm, FF_BLK), lambda i: (0, i, 0)), lambda r: [r[k] for k in range(N_DEV)],
            wfi.reshape(N_DEV, FF_BLK, D), True, x1, _after(p["norm2_g"][l][None], token, pending), dx,
            f"ffn_bwd_dx_l{l}", tm)
        pending = hook("mid_backward", l, dx1)
        dz, dya, dh = _outproj_bwd_merge(dx1, wout, ya, h0, hr, z, l, tm)
        d_wout = _mm_tn(mg, _resident((D, s)), dx1, pl.BlockSpec((s, D // 2), lambda j: (0, j)),
                        2, (D, D), pl.BlockSpec((D, D // 2), lambda j: (0, j)), f"dw_out_l{l}")
        g1, g0 = _lru_scan(a1, dh, a0, dh, True, l)
        dxc, dwr, dwi, dbr, dbi, dlam = _lru_gates_bwd(
            z, h0, hr, g0, g1, p["conv_w"][l], p["conv_b"][l][None], wrb[l], wib[l],
            p["lru_b_r"][l], p["lru_b_i"][l], p["lru_lambda"][l], l, tm)
        dz, dcw, dcb = _conv_bwd(dz, dxc, z, p["conv_w"][l], l, tm)
        dz, dws, dbs, dlng, dlnb = _gmlp_bwd(dz, z, dya, p["gmlp_ln_g"][l][None], p["gmlp_ln_b"][l][None],
                                             wsb[l], wstb[l], bsb[l], l, tm)
        d_win = _mm_tn(h1, _resident((D, s)), dz, pl.BlockSpec((s, IN_BLK), lambda j: (0, j)),
                       N_DEV, (N_DEV, D, IN_BLK), pl.BlockSpec((None, D, IN_BLK), lambda j: (j, 0, 0)),
                       f"dw_in_l{l}")
        token = hook("mixer_partials", l, dict(w_out=d_wout.reshape(N_DEV, D // N_DEV, D), w_in=d_win))
        dx, dg1 = _mm_nt_rms_bwd(
            dz, pl.BlockSpec((tm, N_IN), lambda i: (i, 0)),
            lambda r: [r[:, k * IN_BLK:(k + 1) * IN_BLK] for k in range(N_DEV)],
            win, False, x0, _after(p["norm1_g"][l][None], token, pending), dx1, f"inproj_bwd_dx_l{l}", tm)
        small = dict(norm1_g=dg1[0], gmlp_ln_g=dlng[0], gmlp_ln_b=dlnb[0], gmlp_w_s=dws, gmlp_b_s=dbs,
                     conv_w=dcw, conv_b=dcb[0], lru_w_r=dwr, lru_b_r=dbr, lru_w_i=dwi, lru_b_i=dbi,
                     lru_lambda=dlam, norm2_g=dg2[0])
        if l == 1:
            small["final_g"] = dfg[0]
        pending = hook("small_grads", l, small)
    return loss, dx


_REPL = ["norm1_g", "gmlp_ln_g", "gmlp_ln_b", "gmlp_w_s", "gmlp_b_s", "conv_b", "lru_w_r", "lru_w_i", "norm2_g", "final_g"]
_LANE_SHARDED = ["conv_w", "lru_b_r", "lru_b_i", "lru_lambda"]
_BIG = ["w_in", "w_out", "w_ffn_in", "w_ffn_out"]
_ORDER = ["norm1_g", "w_in", "gmlp_ln_g", "gmlp_ln_b", "gmlp_w_s", "gmlp_b_s", "conv_w", "conv_b", "lru_w_r", "lru_b_r",
          "lru_w_i", "lru_b_i", "lru_lambda", "w_out", "norm2_g", "w_ffn_in", "w_ffn_out", "final_g"]


def _pack(parts, rows):
    flat = jnp.concatenate([a.reshape(-1) for a in parts])
    return jnp.pad(flat, (0, rows * D - flat.shape[0])).reshape(rows, D)


def _unpack(flat, shapes):
    out, off = [], 0
    flat = flat.reshape(-1)
    for shp in shapes:
        n = 1
        for k in shp:
            n *= k
        out.append(flat[off:off + n].reshape(shp))
        off += n
    return out


def kernel(x, norm1_g, w_in, gmlp_ln_g, gmlp_ln_b, gmlp_w_s, gmlp_b_s, conv_w, conv_b, lru_w_r, lru_b_r, lru_w_i, lru_b_i, lru_lambda, w_out, norm2_g, w_ffn_in, w_ffn_out, final_g, loss_target, m_norm1_g, m_w_in, m_gmlp_ln_g, m_gmlp_ln_b, m_gmlp_w_s, m_gmlp_b_s, m_conv_w, m_conv_b, m_lru_w_r, m_lru_b_r, m_lru_w_i, m_lru_b_i, m_lru_lambda, m_w_out, m_norm2_g, m_w_ffn_in, m_w_ffn_out, m_final_g, v_norm1_g, v_w_in, v_gmlp_ln_g, v_gmlp_ln_b, v_gmlp_w_s, v_gmlp_b_s, v_conv_w, v_conv_b, v_lru_w_r, v_lru_b_r, v_lru_w_i, v_lru_b_i, v_lru_lambda, v_w_out, v_norm2_g, v_w_ffn_in, v_w_ffn_out, v_final_g):
    w = dict(norm1_g=norm1_g, w_in=w_in, gmlp_ln_g=gmlp_ln_g, gmlp_ln_b=gmlp_ln_b, gmlp_w_s=gmlp_w_s, gmlp_b_s=gmlp_b_s,
             conv_w=conv_w, conv_b=conv_b, lru_w_r=lru_w_r, lru_b_r=lru_b_r, lru_w_i=lru_w_i, lru_b_i=lru_b_i,
             lru_lambda=lru_lambda, w_out=w_out, norm2_g=norm2_g, w_ffn_in=w_ffn_in, w_ffn_out=w_ffn_out, final_g=final_g)
    mom = dict(norm1_g=m_norm1_g, w_in=m_w_in, gmlp_ln_g=m_gmlp_ln_g, gmlp_ln_b=m_gmlp_ln_b, gmlp_w_s=m_gmlp_w_s,
               gmlp_b_s=m_gmlp_b_s, conv_w=m_conv_w, conv_b=m_conv_b, lru_w_r=m_lru_w_r, lru_b_r=m_lru_b_r,
               lru_w_i=m_lru_w_i, lru_b_i=m_lru_b_i, lru_lambda=m_lru_lambda, w_out=m_w_out, norm2_g=m_norm2_g,
               w_ffn_in=m_w_ffn_in, w_ffn_out=m_w_ffn_out, final_g=m_final_g)
    var = dict(norm1_g=v_norm1_g, w_in=v_w_in, gmlp_ln_g=v_gmlp_ln_g, gmlp_ln_b=v_gmlp_ln_b, gmlp_w_s=v_gmlp_w_s,
               gmlp_b_s=v_gmlp_b_s, conv_w=v_conv_w, conv_b=v_conv_b, lru_w_r=v_lru_w_r, lru_b_r=v_lru_b_r,
               lru_w_i=v_lru_w_i, lru_b_i=v_lru_b_i, lru_lambda=v_lru_lambda, w_out=v_w_out, norm2_g=v_norm2_g,
               w_ffn_in=v_w_ffn_in, w_ffn_out=v_w_ffn_out, final_g=v_final_g)
    for src in (w, mom, var):
        src["w_ffn_in"] = jnp.swapaxes(src["w_ffn_in"], 1, 2)
    xi, yi, ci = _me()
    me = 4 * xi + 2 * yi + ci
    lane0 = me * HD

    lane_shapes = [w[k].shape for k in _LANE_SHARDED]
    lane_rows = sum(a[0] * a[1] for a in lane_shapes)
    packed = jnp.concatenate([w[k].reshape(-1, HD) for k in _LANE_SHARDED])
    packed = jnp.pad(packed, ((0, -lane_rows % 8), (0, 0)))
    lanes = _all_gather(packed, "gather_small")
    params = {k: w[k] for k in _REPL}
    off = 0
    for k, shp in zip(_LANE_SHARDED, lane_shapes):
        n = shp[0] * shp[1]
        params[k] = jnp.swapaxes(lanes[:, off:off + n], 0, 1).reshape(shp[0], shp[1], D)
        off += n

    me1 = jnp.reshape(me, (1,)).astype(jnp.int32)
    gathers = {}
    exchanges = {}
    views = dict(w_in=(N_DEV, D, IN_BLK), w_out=(D, D), w_ffn_in=(2, 4, FF_BLK, D), w_ffn_out=(4, FF_BLK, D))
    small_ex = {}
    small_tok = {}
    small_ag = {}
    layer_names = [k for k in _REPL + _LANE_SHARDED if k != "final_g"]
    flat_rows = 8 * N_DEV * 11

    def start_gather(names, l, after):
        lands = [_cast_into_slot(w[k], l, me1, f"cast_{k}_l{l}") for k in names]
        started, tok = _gather_start(lands, after, f"gather_start_{'_'.join(names)}_l{l}")
        gathers.update({(k, l): h for k, h in zip(names, started)})
        return tok

    def get_w(k, l, after):
        return _gather_wait(gathers[(k, l)], after, f"gather_wait_{k}_l{l}").reshape(views[k])

    def hook(stage, l, payload):
        if stage == "pre_inproj":
            return start_gather(_BIG[1:], l, payload)
        if stage == "pre_ffn":
            return start_gather(_BIG[:1], l + 1, payload) if l == 0 else None
        if stage == "small_grads":
            order = layer_names + (["final_g"] if "final_g" in payload else [])
            flat = _pack([payload[k] for k in order], flat_rows).reshape(N_DEV, flat_rows // N_DEV, D)
            (small_ex[l],), small_tok[l] = _exchange_start([flat], f"exchange_start_small_l{l}")
            return small_tok[l]
        if stage == "mid_backward":
            return reduce_small(l + 1, payload) if l == 0 else None
        started, tok = _exchange_start(list(payload.values()), f"exchange_start_{'_'.join(payload)}_l{l}")
        exchanges.update({(k, l): h for k, h in zip(payload, started)})
        return tok

    def reduce_small(l, after):
        got = _exchange_wait(small_ex[l], after, f"exchange_wait_small_l{l}")
        mine = _sum8_into_slot(got, me1, f"sum_small_l{l}")
        (small_ag[l],), tok = _gather_start([mine], got, f"gather_start_small_l{l}")
        return tok

    start_gather(_BIG[:1], 0, lanes)
    loss, dx = _local_step(x[0], loss_target[0], params, get_w, hook)

    out = {}
    last = {}
    after = small_tok[0]
    for k, l in [(k, l) for k in ("w_ffn_out", "w_ffn_in") for l in (1, 0)] + [("w_out", 1), ("w_in", 1)]:
        got = _exchange_wait(exchanges[(k, l)], after, f"exchange_wait_{k}_l{l}")
        last[k] = _adam_shard(got, w[k], mom[k], var[k], l, last.get(k), f"adam_{k}_l{l}")
        after = last[k][3]
    after = reduce_small(0, after)
    for k in ("w_out", "w_in"):
        got = _exchange_wait(exchanges[(k, 0)], after, f"exchange_wait_{k}_l0")
        last[k] = _adam_shard(got, w[k], mom[k], var[k], 0, last[k], f"adam_{k}_l0")
        after = last[k][3]
    out.update(last)
    out["w_ffn_in"] = [jnp.swapaxes(a, 1, 2) for a in out["w_ffn_in"]]

    grads = {}
    for l in (1, 0):
        g_all = _gather_wait(small_ag[l], after, f"gather_wait_small_l{l}")
        order = layer_names + (["final_g"] if l == 1 else [])
        shapes = [(D,) if k == "final_g" else params[k].shape[1:] for k in order]
        grads[l] = dict(zip(order, _unpack(g_all, shapes)))
    names = _REPL + _LANE_SHARDED
    grads = {k: grads[1][k] if k == "final_g" else jnp.stack([grads[0][k], grads[1][k]]) for k in names}
    for k in _LANE_SHARDED:
        grads[k] = lax.dynamic_slice_in_dim(grads[k], lane0, HD, axis=2)
    dev_shapes = [w[k].shape for k in names]
    rows = -(-sum(w[k].size for k in names) // (8 * D)) * 8
    flats = [_pack([src[k] for k in names], rows) for src in (w, grads, mom, var)]
    for nm, flat in zip(("delta", "m", "v"), _adam_flat(*flats)):
        for k, a in zip(names, _unpack(flat, dev_shapes)):
            out.setdefault(k, [grads[k], None, None, None])[("delta", "m", "v").index(nm) + 1] = a

    loss = lax.psum(loss[0, 0], MESH_AXES)
    return (loss, dx[None], *[out[k][0] for k in _ORDER], *[out[k][1] for k in _ORDER],
            *[out[k][2] for k in _ORDER], *[out[k][3] for k in _ORDER])
```

```python
import jax
import jax.numpy as jnp
from jax import lax
from jax.experimental import pallas as pl
from jax.experimental.pallas import tpu as pltpu

F32 = jnp.float32
BF16 = jnp.bfloat16
SDS = jax.ShapeDtypeStruct

D = 1024
N_IN = 6 * D
D_FF = 2816
N_DEV = 8
IN_BLK = N_IN // N_DEV
FF_BLK = 2 * D_FF // N_DEV
HEADS = 8
HD = 128
EPS = 1e-6
LRU_C = 8.0
MESH_AXES = ("x", "y", "c")

ADAM_LR = 0.001
ADAM_B1 = 0.9
ADAM_B2 = 0.999
ADAM_EPS = 1e-08
ADAM_WD = 0.01
ADAM_STEP = 10

VMEM_LIMIT = 56 * 2**20


def _cp(*sem, **kw):
    return pltpu.CompilerParams(dimension_semantics=sem, vmem_limit_bytes=VMEM_LIMIT, **kw)


def _row_tile(s):
    return 512 if s >= 1024 else s // 2


_GELU_C = 0.7978845608028654


def _gelu(x):
    t = jnp.tanh(_GELU_C * (x + 0.044715 * (x * x * x)))
    return 0.5 * x * (1.0 + t), t


def _gelu_grad(x, t):
    return 0.5 * (1.0 + t) + 0.5 * x * (1.0 - t * t) * (_GELU_C * (1.0 + 0.134145 * (x * x)))


def _sigmoid(x):
    return 0.5 + 0.5 * jnp.tanh(0.5 * x)


def _softplus(x):
    e = jnp.exp(-jnp.abs(x))
    w = 1.0 + e
    l1p = jnp.where(w == 1.0, e, jnp.log(w) * e / jnp.where(w == 1.0, 1.0, w - 1.0))
    return jnp.maximum(x, 0.0) + l1p


def _rms_fwd(x, g):
    r = lax.rsqrt(jnp.mean(x * x, axis=-1, keepdims=True) + EPS)
    return x * r * g


def _rms_bwd(x, g, dh):
    r = lax.rsqrt(jnp.mean(x * x, axis=-1, keepdims=True) + EPS)
    xh = x * r
    dxh = dh * g
    dx = r * (dxh - xh * jnp.mean(dxh * xh, axis=-1, keepdims=True))
    dg = jnp.sum(dh * xh, axis=0, keepdims=True)
    return dx, dg


LANE_ROWS = D // HD


def _add_rows128(ref, vec, row0=0):
    for i in range(vec.shape[0]):
        for k in range(LANE_ROWS):
            j = row0 + i * LANE_ROWS + k
            ref[j:j + 1, :] += vec[i:i + 1, k * HD:(k + 1) * HD]


def _dot(a, b):
    return jnp.dot(a, b, preferred_element_type=F32)


def _dot_nt(a, b):
    return lax.dot_general(a, b, (((1,), (1,)), ((), ())), preferred_element_type=F32)


def _dot_tn(a, b):
    return lax.dot_general(a, b, (((0,), (0,)), ((), ())), preferred_element_type=F32)


def _taps(prev, cur, nxt, tm):
    ext = jnp.concatenate([prev, cur, nxt], axis=0)
    n = tm + 16
    sl = slice(8, 8 + tm)
    return (pltpu.roll(ext, 2, 0)[sl], pltpu.roll(ext, 1, 0)[sl], cur,
            pltpu.roll(ext, n - 1, 0)[sl], pltpu.roll(ext, n - 2, 0)[sl])


def _halo_specs(tm, s, col):
    nb8 = s // 8
    r8 = tm // 8
    return (pl.BlockSpec((8, D), lambda i: (jnp.maximum(i * r8 - 1, 0), col)),
            pl.BlockSpec((tm, D), lambda i: (i, col)),
            pl.BlockSpec((8, D), lambda i: (jnp.minimum((i + 1) * r8, nb8 - 1), col)))


def _halo_flags(nt):
    i = pl.program_id(0)
    return (i > 0).astype(F32), (i < nt - 1).astype(F32)


def _full(shape):
    nd = len(shape)
    return pl.BlockSpec(shape, lambda *_: (0,) * nd)


def _resident(shape):
    nd = len(shape)
    return pl.BlockSpec(shape, lambda *_: (0,) * nd, pipeline_mode=pl.Buffered(1))


def _norm_inproj(x, g, w, layer, tm):
    s = x.shape[0]

    def body(x_ref, g_ref, w_ref, z_ref, ht_ref):
        h32 = _rms_fwd(x_ref[...], g_ref[...])
        ht_ref[...] = h32.T.astype(BF16)
        h = h32.astype(BF16)
        for j in range(N_DEV):
            z_ref[:, j * IN_BLK:(j + 1) * IN_BLK] = _dot(h, w_ref[j])

    return pl.pallas_call(
        body, name=f"norm_inproj_l{layer}", grid=(s // tm,),
        in_specs=[pl.BlockSpec((tm, D), lambda i: (i, 0)), _full((1, D)), _resident((N_DEV, D, IN_BLK))],
        out_specs=[pl.BlockSpec((tm, N_IN), lambda i: (i, 0)), pl.BlockSpec((D, tm), lambda i: (0, i))],
        out_shape=[SDS((s, N_IN), F32), SDS((D, s), BF16)],
        compiler_params=_cp("parallel"))(x, g, w)


def _gmlp_fwd(z, lng, lnb, ws, bsb, layer, tm):
    s = z.shape[0]

    def body(zu_ref, zv_ref, lng_ref, lnb_ref, ws_ref, bsb_ref, ya_ref):
        u, _ = _gelu(zu_ref[...])
        gv, _ = _gelu(zv_ref[...])
        xc = gv - jnp.mean(gv, axis=-1, keepdims=True)
        rstd = lax.rsqrt(jnp.mean(xc * xc, axis=-1, keepdims=True) + EPS)
        vb = (xc * rstd * lng_ref[...] + lnb_ref[...]).astype(BF16)
        for c in range(tm // HD):
            rs = slice(c * HD, (c + 1) * HD)
            for g in range(HEADS):
                cs = slice(g * HD, (g + 1) * HD)
                mixed = _dot(ws_ref[g], vb[rs, cs]) + bsb_ref[g]
                ya_ref[rs, cs] = u[rs, cs] * mixed

    return pl.pallas_call(
        body, name=f"gmlp_fwd_l{layer}", grid=(s // tm,),
        in_specs=[pl.BlockSpec((tm, D), lambda i: (i, 0)), pl.BlockSpec((tm, D), lambda i: (i, 1)),
                  _full((1, D)), _full((1, D)), _full((HEADS, HD, HD)), _full((HEADS, HD, HD))],
        out_specs=pl.BlockSpec((tm, D), lambda i: (i, 0)),
        out_shape=SDS((s, D), F32),
        compiler_params=_cp("parallel"))(z, z, lng, lnb, ws, bsb)


def _conv(taps, cw_ref, cb_ref):
    _, m1, c0, p1, p2 = taps
    return cb_ref[...] + m1 * cw_ref[0:1, :] + c0 * cw_ref[1:2, :] + p1 * cw_ref[2:3, :] + p2 * cw_ref[3:4, :]


def _heads_dot(xb, w_ref, d):
    return jnp.concatenate([_dot(xb[:, h * HD:(h + 1) * HD], w_ref[d, h]) for h in range(HEADS)], axis=1)


def _lru_gates(xc, xb, wr_ref, wi_ref, br_ref, bi_ref, lam_ref, d):
    sp = _softplus(-lam_ref[d:d + 1, :])
    r = _sigmoid(_heads_dot(xb, wr_ref, d) + br_ref[d:d + 1, :])
    ig = _sigmoid(_heads_dot(xb, wi_ref, d) + bi_ref[d:d + 1, :])
    la = (-LRU_C) * r * sp
    a = jnp.exp(la)
    q = jnp.maximum(jnp.tanh(-la) * (a * a + 1.0), 0.0)
    rq = jnp.where(q > 0.0, lax.rsqrt(jnp.where(q > 0.0, q, 1.0)), 0.0)
    return sp, r, ig, a, q * rq, rq


def _lru_gates_fwd(z, cw, cb, wr, wi, br, bi, lam, layer, tm):
    s = z.shape[0]
    nt = s // tm

    def body(zp_ref, zc_ref, zn_ref, cw_ref, cb_ref, wr_ref, wi_ref, br_ref, bi_ref, lam_ref,
             a0_ref, b0_ref, a1_ref, b1_ref):
        fp, fn = _halo_flags(nt)
        xc = _conv(_taps(zp_ref[...] * fp, zc_ref[...], zn_ref[...] * fn, tm), cw_ref, cb_ref)
        xb = xc.astype(BF16)
        for d, (a_ref, b_ref) in enumerate(((a0_ref, b0_ref), (a1_ref, b1_ref))):
            _, _, ig, a, mult, _ = _lru_gates(xc, xb, wr_ref, wi_ref, br_ref, bi_ref, lam_ref, d)
            a_ref[...] = a
            b_ref[...] = mult * (ig * xc)

    tile = pl.BlockSpec((tm, D), lambda i: (i, 0))
    return pl.pallas_call(
        body, name=f"lru_gates_fwd_l{layer}", grid=(nt,),
        in_specs=[*_halo_specs(tm, s, 2), _full((4, D)), _full((1, D)),
                  _full((2, HEADS, HD, HD)), _full((2, HEADS, HD, HD)), _full((2, D)), _full((2, D)), _full((2, D))],
        out_specs=[tile] * 4, out_shape=[SDS((s, D), F32)] * 4,
        compiler_params=_cp("parallel"))(z, z, z, cw, cb, wr, wi, br, bi, lam)


def _scan_group(a, x, c, reverse, bwd):
    row = lax.broadcasted_iota(jnp.int32, a.shape, 0)
    b = a * x if bwd else x
    for d in (1, 2, 4):
        keep = (row < 8 - d) if reverse else (row >= d)
        sh = 8 - d if reverse else d
        a_s = jnp.where(keep, pltpu.roll(a, sh, 0), 1.0)
        b_s = jnp.where(keep, pltpu.roll(b, sh, 0), 0.0)
        b = a * b_s + b
        a = a * a_s
    h = b + a * c
    new_c = h[0:1, :] if reverse else h[7:8, :]
    if not bwd:
        return h, new_c
    if reverse:
        prev = jnp.where(row < 7, pltpu.roll(h, 7, 0), c)
    else:
        prev = jnp.where(row >= 1, pltpu.roll(h, 1, 0), c)
    return x + prev, new_c


def _lru_scan(a_f, x_f, a_r, x_r, bwd, layer):
    s = a_f.shape[0]
    ts = min(1024, s // 2)
    cb = 512
    nt = s // ts
    ng = ts // 8

    def body(af_ref, xf_ref, ar_ref, xr_ref, of_ref, or_ref, cf, cr):
        @pl.when(pl.program_id(1) == 0)
        def _():
            cf[...] = jnp.zeros_like(cf)
            cr[...] = jnp.zeros_like(cr)

        def step(j, carry):
            c_f, c_r = carry
            rf = pl.multiple_of(j * 8, 8)
            rr = pl.multiple_of((ng - 1 - j) * 8, 8)
            o, c_f = _scan_group(af_ref[pl.ds(rf, 8), :], xf_ref[pl.ds(rf, 8), :], c_f, False, bwd)
            of_ref[pl.ds(rf, 8), :] = o
            o, c_r = _scan_group(ar_ref[pl.ds(rr, 8), :], xr_ref[pl.ds(rr, 8), :], c_r, True, bwd)
            or_ref[pl.ds(rr, 8), :] = o
            return c_f, c_r

        c_f, c_r = lax.fori_loop(0, ng, step, (cf[0:1, :], cr[0:1, :]), unroll=2)
        cf[...] = jnp.broadcast_to(c_f, cf.shape)
        cr[...] = jnp.broadcast_to(c_r, cr.shape)

    fwd = pl.BlockSpec((ts, cb), lambda c, t: (t, c))
    rev = pl.BlockSpec((ts, cb), lambda c, t: (nt - 1 - t, c))
    return pl.pallas_call(
        body, name=f"lru_scan_{'bwd' if bwd else 'fwd'}_l{layer}", grid=(D // cb, nt),
        in_specs=[fwd, fwd, rev, rev], out_specs=[fwd, rev],
        out_shape=[SDS((s, D), F32)] * 2,
        scratch_shapes=[pltpu.VMEM((8, cb), F32), pltpu.VMEM((8, cb), F32)],
        compiler_params=_cp("parallel", "arbitrary"))(a_f, x_f, a_r, x_r)


def _merge_outproj(x, ya, h0, h1, z, wo, layer, tm):
    s = x.shape[0]

    def body(x_ref, ya_ref, h0_ref, h1_ref, zg_ref, za_ref, zb_ref, wo_ref, x1_ref, mg_ref):
        gg, _ = _gelu(zg_ref[...])
        yb = (h0_ref[...] + h1_ref[...]) * gg
        m32 = _sigmoid(za_ref[...]) * ya_ref[...] + _sigmoid(zb_ref[...]) * yb
        mg_ref[...] = m32.T.astype(BF16)
        x1_ref[...] = x_ref[...] + _dot(m32.astype(BF16), wo_ref[...])

    tile = pl.BlockSpec((tm, D), lambda i: (i, 0))
    return pl.pallas_call(
        body, name=f"merge_outproj_l{layer}", grid=(s // tm,),
        in_specs=[tile, tile, tile, tile] + [pl.BlockSpec((tm, D), lambda i, c=c: (i, c)) for c in (3, 4, 5)]
        + [_full((D, D))],
        out_specs=[tile, pl.BlockSpec((D, tm), lambda i: (0, i))], out_shape=[SDS((s, D), F32), SDS((D, s), BF16)],
        compiler_params=_cp("parallel"))(x, ya, h0, h1, z, z, z, wo)


def _ffn_fwd(x1, g, wfi, wfo, layer, tm):
    s = x1.shape[0]

    def body(x_ref, g_ref, wi_ref, wo_ref, x2_ref, gu_ref, h_ref):
        x = x_ref[...]
        h = _rms_fwd(x, g_ref[...]).astype(BF16)
        h_ref[...] = h
        acc = x
        for k in range(4):
            gate = _dot_nt(h, wi_ref[0, k])
            up = _dot_nt(h, wi_ref[1, k])
            gu_ref[0, k] = gate.astype(BF16)
            gu_ref[1, k] = up.astype(BF16)
            acc = acc + _dot((gate * _sigmoid(gate) * up).astype(BF16), wo_ref[k])
        x2_ref[...] = acc

    tile = pl.BlockSpec((tm, D), lambda i: (i, 0))
    return pl.pallas_call(
        body, name=f"ffn_fwd_l{layer}", grid=(s // tm,),
        in_specs=[tile, _full((1, D)), _resident((2, 4, FF_BLK, D)), _resident((4, FF_BLK, D))],
        out_specs=[tile, pl.BlockSpec((2, 4, tm, FF_BLK), lambda i: (0, 0, i, 0)), tile],
        out_shape=[SDS((s, D), F32), SDS((2, 4, s, FF_BLK), BF16), SDS((s, D), BF16)],
        compiler_params=_cp("parallel"))(x1, g, wfi, wfo)


def _loss_head(x, g, tgt, tm):
    s = x.shape[0]

    def body(x_ref, g_ref, t_ref, dx_ref, loss_ref, dg_ref):
        @pl.when(pl.program_id(0) == 0)
        def _():
            loss_ref[...] = jnp.zeros_like(loss_ref)
            dg_ref[...] = jnp.zeros_like(dg_ref)

        x = x_ref[...]
        gv = g_ref[...]
        e = _rms_fwd(x, gv) - t_ref[...]
        rows = jnp.sum(e * e, axis=-1, keepdims=True)
        loss_ref[...] += (0.5 / D) * jnp.sum(rows, axis=0, keepdims=True)
        dx, dg = _rms_bwd(x, gv, e * (1.0 / D))
        dx_ref[...] = dx
        _add_rows128(dg_ref, dg)

    tile = pl.BlockSpec((tm, D), lambda i: (i, 0))
    return pl.pallas_call(
        body, name="loss_head", grid=(s // tm,),
        in_specs=[tile, _full((1, D)), tile],
        out_specs=[tile, _full((1, 1)), _full((LANE_ROWS, HD))],
        out_shape=[SDS((s, D), F32), SDS((1, 1), F32), SDS((LANE_ROWS, HD), F32)],
        compiler_params=_cp("arbitrary"))(x, g, tgt)


def _ffn_bwd_act(dx2, wfo, gu, layer, tm):
    s = dx2.shape[0]

    def body(dx_ref, wo_ref, gu_ref, ff_ref, dgu_ref):
        dxb = dx_ref[...].astype(BF16)
        for k in range(4):
            dff = _dot_nt(dxb, wo_ref[k])
            gate = gu_ref[0, k].astype(F32)
            up = gu_ref[1, k].astype(F32)
            sg = _sigmoid(gate)
            sl = gate * sg
            ff_ref[k] = (sl * up).astype(BF16)
            dgu_ref[0, k] = (dff * up * (sg * (1.0 + gate * (1.0 - sg)))).astype(BF16)
            dgu_ref[1, k] = (dff * sl).astype(BF16)

    blk = pl.BlockSpec((2, 4, tm, FF_BLK), lambda i: (0, 0, i, 0))
    return pl.pallas_call(
        body, name=f"ffn_bwd_act_l{layer}", grid=(s // tm,),
        in_specs=[pl.BlockSpec((tm, D), lambda i: (i, 0)), _resident((4, FF_BLK, D)), blk],
        out_specs=[pl.BlockSpec((4, tm, FF_BLK), lambda i: (0, i, 0)), blk],
        out_shape=[SDS((4, s, FF_BLK), BF16), SDS((2, 4, s, FF_BLK), BF16)],
        compiler_params=_cp("parallel"))(dx2, wfo, gu)


def _mm_nt_rms_bwd(a, a_spec, a_blocks, w, w_is_transposed, x, g, dres, name, tm):
    s = x.shape[0]

    def body(a_ref, w_ref, x_ref, g_ref, dres_ref, dx_ref, dg_ref):
        @pl.when(pl.program_id(0) == 0)
        def _():
            dg_ref[...] = jnp.zeros_like(dg_ref)

        dh = None
        for k, blk in enumerate(a_blocks(a_ref)):
            part = _dot(blk, w_ref[k]) if w_is_transposed else _dot_nt(blk, w_ref[k])
            dh = part if dh is None else dh + part
        dx, dg = _rms_bwd(x_ref[...], g_ref[...], dh)
        dx_ref[...] = dres_ref[...] + dx
        _add_rows128(dg_ref, dg)

    tile = pl.BlockSpec((tm, D), lambda i: (i, 0))
    return pl.pallas_call(
        body, name=name, grid=(s // tm,),
        in_specs=[a_spec, _resident(w.shape), tile, _full((1, D)), tile],
        out_specs=[tile, _full((LANE_ROWS, HD))], out_shape=[SDS((s, D), F32), SDS((LANE_ROWS, HD), F32)],
        compiler_params=_cp("arbitrary"))(a, w, x, g, dres)


def _mm_tn(a, a_spec, b, b_spec, nb, out_shape, out_spec, name, a_is_transposed=True):
    def body(a_ref, b_ref, o_ref):
        bb = b_ref[...].astype(BF16)
        o_ref[...] = (_dot(a_ref[...], bb) if a_is_transposed else _dot_tn(a_ref[...], bb)).astype(BF16)

    return pl.pallas_call(
        body, name=name, grid=(nb,), in_specs=[a_spec, b_spec], out_specs=out_spec,
        out_shape=SDS(out_shape, BF16), compiler_params=_cp("parallel"))(a, b)


def _outproj_bwd_merge(dx1, wo, ya, h0, h1, z, layer, tm):
    s = dx1.shape[0]

    def body(dx_ref, wo_ref, ya_ref, h0_ref, h1_ref, zg_ref, za_ref, zb_ref, dz_ref, dya_ref, dh_ref):
        dm = _dot_nt(dx_ref[...].astype(BF16), wo_ref[...])
        sa = _sigmoid(za_ref[...])
        sb = _sigmoid(zb_ref[...])
        zg = zg_ref[...]
        gg, tg = _gelu(zg)
        hs = h0_ref[...] + h1_ref[...]
        dyb = dm * sb
        dya_ref[...] = dm * sa
        dh_ref[...] = dyb * gg
        dz_ref[:, 0:D] = (dyb * hs * _gelu_grad(zg, tg)).astype(BF16)
        dz_ref[:, D:2 * D] = (dm * ya_ref[...] * (sa * (1.0 - sa))).astype(BF16)
        dz_ref[:, 2 * D:3 * D] = (dm * (hs * gg) * (sb * (1.0 - sb))).astype(BF16)

    tile = pl.BlockSpec((tm, D), lambda i: (i, 0))
    return pl.pallas_call(
        body, name=f"outproj_bwd_merge_l{layer}", grid=(s // tm,),
        in_specs=[tile, _full((D, D)), tile, tile, tile]
        + [pl.BlockSpec((tm, D), lambda i, c=c: (i, c)) for c in (3, 4, 5)],
        out_specs=[pl.BlockSpec((tm, 3 * D), lambda i: (i, 1)), tile, tile],
        out_shape=[SDS((s, N_IN), BF16), SDS((s, D), F32), SDS((s, D), F32)],
        compiler_params=_cp("parallel"))(dx1, wo, ya, h0, h1, z, z, z)


def _lru_gates_bwd(z, h0, h1, g0, g1, cw, cb, wr, wi, br, bi, lam, layer, tm):
    s = z.shape[0]
    nt = s // tm

    def body(zp_ref, zc_ref, zn_ref, h0p_ref, h0_ref, h1_ref, h1n_ref, g0_ref, g1_ref,
             cw_ref, cb_ref, wr_ref, wi_ref, br_ref, bi_ref, lam_ref,
             dxc_ref, dwr_ref, dwi_ref, dbr_ref, dbi_ref, dlam_ref):
        i = pl.program_id(0)
        fp, fn = _halo_flags(nt)

        @pl.when(i == 0)
        def _():
            for r in (dwr_ref, dwi_ref, dbr_ref, dbi_ref, dlam_ref):
                r[...] = jnp.zeros_like(r)

        xc = _conv(_taps(zp_ref[...] * fp, zc_ref[...], zn_ref[...] * fn, tm), cw_ref, cb_ref)
        xb = xc.astype(BF16)
        zeros8 = jnp.zeros((8, D), F32)
        h_prev = _taps(h0p_ref[...] * fp, h0_ref[...], zeros8, tm)[1]
        h_next = _taps(zeros8, h1_ref[...], h1n_ref[...] * fn, tm)[3]
        dxc = jnp.zeros((tm, D), F32)
        for d, (g_ref, hsh) in enumerate(((g0_ref, h_prev), (g1_ref, h_next))):
            sp, r, ig, a, mult, rmult = _lru_gates(xc, xb, wr_ref, wi_ref, br_ref, bi_ref, lam_ref, d)
            db = g_ref[...]
            da = db * hsh
            dmult = db * (ig * xc)
            di = db * (mult * xc)
            dxc = dxc + db * (mult * ig)
            dla = da * a - dmult * (a * a * rmult)
            dsp_dlam = -_sigmoid(-lam_ref[d:d + 1, :])
            _add_rows128(dlam_ref, jnp.sum(dla * r, axis=0, keepdims=True) * ((-LRU_C) * dsp_dlam), d * LANE_ROWS)
            dpr = dla * sp * (-LRU_C) * (r * (1.0 - r))
            dpi = di * (ig * (1.0 - ig))
            _add_rows128(dbr_ref, jnp.sum(dpr, axis=0, keepdims=True), d * LANE_ROWS)
            _add_rows128(dbi_ref, jnp.sum(dpi, axis=0, keepdims=True), d * LANE_ROWS)
            dprb = dpr.astype(BF16)
            dpib = dpi.astype(BF16)
            parts = []
            for h in range(HEADS):
                cs = slice(h * HD, (h + 1) * HD)
                dwr_ref[d, h] += _dot_tn(xb[:, cs], dprb[:, cs])
                dwi_ref[d, h] += _dot_tn(xb[:, cs], dpib[:, cs])
                parts.append(_dot_nt(dprb[:, cs], wr_ref[d, h]) + _dot_nt(dpib[:, cs], wi_ref[d, h]))
            dxc = dxc + jnp.concatenate(parts, axis=1)
        dxc_ref[...] = dxc

    tile = pl.BlockSpec((tm, D), lambda i: (i, 0))
    zp, zc, zn = _halo_specs(tm, s, 2)
    hp, hc, hn = _halo_specs(tm, s, 0)
    wspec = _full((2, HEADS, HD, HD))
    vspec = _full((2 * LANE_ROWS, HD))
    return pl.pallas_call(
        body, name=f"lru_gates_bwd_l{layer}", grid=(nt,),
        in_specs=[zp, zc, zn, hp, hc, hc, hn, tile, tile, _full((4, D)), _full((1, D)),
                  wspec, wspec, _full((2, D)), _full((2, D)), _full((2, D))],
        out_specs=[tile, wspec, wspec, vspec, vspec, vspec],
        out_shape=[SDS((s, D), F32), SDS((2, HEADS, HD, HD), F32), SDS((2, HEADS, HD, HD), F32)]
        + [SDS((2 * LANE_ROWS, HD), F32)] * 3,
        compiler_params=_cp("arbitrary"))(z, z, z, h0, h0, h1, h1, g0, g1, cw, cb, wr, wi, br, bi, lam)


def _conv_bwd(dz, dxc, z, cw, layer, tm):
    s = z.shape[0]
    nt = s // tm

    def body(dz_in, dp_ref, dc_ref, dn_ref, zp_ref, zc_ref, zn_ref, cw_ref, dz_ref, dcw_ref, dcb_ref):
        del dz_in
        fp, fn = _halo_flags(nt)

        @pl.when(pl.program_id(0) == 0)
        def _():
            dcw_ref[...] = jnp.zeros_like(dcw_ref)
            dcb_ref[...] = jnp.zeros_like(dcb_ref)

        dxc = dc_ref[...]
        dm2, dm1, _, dp1, _ = _taps(dp_ref[...] * fp, dxc, dn_ref[...] * fn, tm)
        dz_ref[...] = (cw_ref[0:1, :] * dp1 + cw_ref[1:2, :] * dxc + cw_ref[2:3, :] * dm1
                       + cw_ref[3:4, :] * dm2).astype(BF16)
        _, zm1, z0, zp1, zp2 = _taps(zp_ref[...] * fp, zc_ref[...], zn_ref[...] * fn, tm)
        for k, zt in enumerate((zm1, z0, zp1, zp2)):
            _add_rows128(dcw_ref, jnp.sum(dxc * zt, axis=0, keepdims=True), k * LANE_ROWS)
        _add_rows128(dcb_ref, jnp.sum(dxc, axis=0, keepdims=True))

    return pl.pallas_call(
        body, name=f"conv_bwd_l{layer}", grid=(nt,),
        in_specs=[pl.BlockSpec(memory_space=pl.ANY), *_halo_specs(tm, s, 0), *_halo_specs(tm, s, 2), _full((4, D))],
        out_specs=[pl.BlockSpec((tm, D), lambda i: (i, 2)), _full((4 * LANE_ROWS, HD)), _full((LANE_ROWS, HD))],
        out_shape=[SDS((s, N_IN), BF16), SDS((4 * LANE_ROWS, HD), F32), SDS((LANE_ROWS, HD), F32)],
        input_output_aliases={0: 0},
        compiler_params=_cp("arbitrary"))(dz, dxc, dxc, dxc, z, z, z, cw)


def _gmlp_bwd(dz, z, dya, lng, lnb, ws, wst, bsb, layer, tm):
    s = z.shape[0]
    nt = s // tm

    def body(dz_in, zu_ref, zv_ref, dya_ref, lng_ref, lnb_ref, ws_ref, wst_ref, bsb_ref,
             dz_ref, dws_ref, dbs_ref, dlng_ref, dlnb_ref, du_s, dv_s, dbs_acc):
        del dz_in
        i = pl.program_id(0)

        @pl.when(i == 0)
        def _():
            for r in (dws_ref, dlng_ref, dlnb_ref, dbs_acc):
                r[...] = jnp.zeros_like(r)

        zu = zu_ref[...]
        zv = zv_ref[...]
        u, tu = _gelu(zu)
        gv, tv = _gelu(zv)
        xc = gv - jnp.mean(gv, axis=-1, keepdims=True)
        rstd = lax.rsqrt(jnp.mean(xc * xc, axis=-1, keepdims=True) + EPS)
        xh = xc * rstd
        lng_v = lng_ref[...]
        vb = (xh * lng_v + lnb_ref[...]).astype(BF16)
        dya = dya_ref[...]
        for c in range(tm // HD):
            rs = slice(c * HD, (c + 1) * HD)
            for g in range(HEADS):
                cs = slice(g * HD, (g + 1) * HD)
                vblk = vb[rs, cs]
                mixed = _dot(ws_ref[g], vblk) + bsb_ref[g]
                du_s[rs, cs] = dya[rs, cs] * mixed
                dmx = dya[rs, cs] * u[rs, cs]
                dbs_acc[g] += dmx
                dmxb = dmx.astype(BF16)
                dws_ref[g] += _dot_nt(dmxb, vblk)
                dv_s[rs, cs] = _dot(wst_ref[g], dmxb)
        dv = dv_s[...]
        _add_rows128(dlng_ref, jnp.sum(dv * xh, axis=0, keepdims=True))
        _add_rows128(dlnb_ref, jnp.sum(dv, axis=0, keepdims=True))
        dxh = dv * lng_v
        dgv = rstd * (dxh - jnp.mean(dxh, axis=-1, keepdims=True)
                      - xh * jnp.mean(dxh * xh, axis=-1, keepdims=True))
        dz_ref[:, 0:D] = (du_s[...] * _gelu_grad(zu, tu)).astype(BF16)
        dz_ref[:, D:2 * D] = (dgv * _gelu_grad(zv, tv)).astype(BF16)

        @pl.when(i == nt - 1)
        def _():
            for g in range(HEADS):
                dbs_ref[g:g + 1, :] = jnp.sum(dbs_acc[g].T, axis=0, keepdims=True)

    tile = pl.BlockSpec((tm, D), lambda i: (i, 0))
    wspec = _full((HEADS, HD, HD))
    return pl.pallas_call(
        body, name=f"gmlp_bwd_l{layer}", grid=(nt,),
        in_specs=[pl.BlockSpec(memory_space=pl.ANY), tile, pl.BlockSpec((tm, D), lambda i: (i, 1)), tile,
                  _full((1, D)), _full((1, D)), wspec, wspec, wspec],
        out_specs=[pl.BlockSpec((tm, 2 * D), lambda i: (i, 0)), wspec, _full((HEADS, HD)),
                   _full((LANE_ROWS, HD)), _full((LANE_ROWS, HD))],
        out_shape=[SDS((s, N_IN), BF16), SDS((HEADS, HD, HD), F32), SDS((HEADS, HD), F32),
                   SDS((LANE_ROWS, HD), F32), SDS((LANE_ROWS, HD), F32)],
        scratch_shapes=[pltpu.VMEM((tm, D), F32), pltpu.VMEM((tm, D), F32), pltpu.VMEM((HEADS, HD, HD), F32)],
        input_output_aliases={0: 0},
        compiler_params=_cp("arbitrary"))(dz, z, z, dya, lng, lnb, ws, wst, bsb)


def _me():
    return lax.axis_index("x"), lax.axis_index("y"), lax.axis_index("c")


def _peer(m):
    x, y, c = _me()
    px = 1 - x if m & 4 else x
    py = 1 - y if m & 2 else y
    pc = 1 - c if m & 1 else c
    return (px, py, pc), 4 * px + 2 * py + pc


_ANY = pl.BlockSpec(memory_space=pl.ANY)
_EXCHANGE_SEMS = [pltpu.SemaphoreType.DMA((N_DEV - 1,)), pltpu.SemaphoreType.DMA((N_DEV - 1,)), pltpu.SemaphoreType.DMA(())]


def _all_gather(v, after, name):
    def body(v_ref, after_ref, o_ref, send_sems, recv_sems, local_sem):
        del after_ref
        x, y, c = _me()
        me = 4 * x + 2 * y + c
        local = pltpu.make_async_copy(v_ref, o_ref.at[me], local_sem)
        local.start()
        sends = []
        for m in range(1, N_DEV):
            dev, _ = _peer(m)
            cp = pltpu.make_async_remote_copy(v_ref, o_ref.at[me], send_sems.at[m - 1], recv_sems.at[m - 1],
                                              device_id=dev, device_id_type=pl.DeviceIdType.MESH)
            cp.start()
            sends.append(cp)
        for m in range(1, N_DEV):
            dev, blk = _peer(m)
            pltpu.make_async_remote_copy(v_ref, o_ref.at[blk], send_sems.at[m - 1], recv_sems.at[m - 1],
                                         device_id=dev, device_id_type=pl.DeviceIdType.MESH).wait_recv()
        for cp in sends:
            cp.wait_send()
        local.wait()

    return pl.pallas_call(
        body, name=name, in_specs=[_ANY, _ANY], out_specs=_ANY,
        out_shape=SDS((N_DEV,) + v.shape, v.dtype), scratch_shapes=_EXCHANGE_SEMS)(v, after)


_HBM = pl.BlockSpec(memory_space=pltpu.HBM)
_SEM = pl.BlockSpec(memory_space=pltpu.SEMAPHORE)
_EFFECT = pltpu.CompilerParams(has_side_effects=pltpu.SideEffectType.DATAFLOW_SIDE_EFFECTING)
_PEER_SEMS = pltpu.SemaphoreType.DMA((N_DEV - 1,))


def _in_hbm(a):
    return pltpu.with_memory_space_constraint(a, pltpu.HBM)


def _remote(src, dst, send_sems, recv_sems, m):
    dev, _ = _peer(m)
    return pltpu.make_async_remote_copy(src, dst, send_sems.at[m - 1], recv_sems.at[m - 1],
                                        device_id=dev, device_id_type=pl.DeviceIdType.MESH)


def _gather_start(lands, after, name):
    n = len(lands)

    def body(*refs):
        land = refs[:n]
        sems = refs[n + 1:3 * n + 1]
        token = refs[-1]
        x, y, c = _me()
        me = 4 * x + 2 * y + c
        for t in range(n):
            for m in range(1, N_DEV):
                _remote(land[t].at[me], land[t].at[me], sems[2 * t], sems[2 * t + 1], m).start()
        token[...] = jnp.zeros_like(token)

    res = pl.pallas_call(
        body, name=name, in_specs=[_HBM] * n + [_ANY],
        out_specs=[_SEM] * (2 * n) + [_HBM] * n + [pl.BlockSpec(memory_space=pltpu.VMEM)],
        out_shape=[_PEER_SEMS] * (2 * n) + [pltpu.HBM(a.shape, a.dtype) for a in lands] + [SDS((8, 128), F32)],
        input_output_aliases={t: 2 * n + t for t in range(n)},
        compiler_params=_EFFECT)(*[_in_hbm(a) for a in lands], after)
    return [(res[2 * t], res[2 * t + 1], res[2 * n + t]) for t in range(n)], res[-1]


def _gather_wait(handle, after, name):
    send_sems, recv_sems, land = handle

    def body(land_ref, ssem, rsem, after_ref, out_ref):
        del after_ref, out_ref
        x, y, c = _me()
        me = 4 * x + 2 * y + c
        for m in range(1, N_DEV):
            _, blk = _peer(m)
            cp = _remote(land_ref.at[me], land_ref.at[blk], ssem, rsem, m)
            cp.wait_send()
            cp.wait_recv()

    return pl.pallas_call(
        body, name=name, in_specs=[_HBM, _SEM, _SEM, _ANY], out_specs=_HBM,
        out_shape=pltpu.HBM(land.shape, land.dtype), input_output_aliases={0: 0},
        compiler_params=_EFFECT)(land, send_sems, recv_sems, after)


def _exchange_start(ps, name):
    n = len(ps)

    def body(*refs):
        p = refs[:n]
        got = refs[n:2 * n]
        sems = refs[2 * n:5 * n]
        token = refs[-1]
        x, y, c = _me()
        me = 4 * x + 2 * y + c
        for t in range(n):
            pltpu.make_async_copy(p[t].at[me], got[t].at[me], sems[3 * t + 2]).start()
            for m in range(1, N_DEV):
                _, blk = _peer(m)
                _remote(p[t].at[blk], got[t].at[me], sems[3 * t], sems[3 * t + 1], m).start()
        token[...] = jnp.zeros_like(token)

    res = pl.pallas_call(
        body, name=name, in_specs=[_HBM] * (2 * n),
        out_specs=[_SEM] * (3 * n) + [_HBM] * (2 * n) + [pl.BlockSpec(memory_space=pltpu.VMEM)],
        out_shape=[_PEER_SEMS, _PEER_SEMS, pltpu.SemaphoreType.DMA(())] * n
        + [pltpu.HBM(a.shape, a.dtype) for a in ps] * 2 + [SDS((8, 128), F32)],
        input_output_aliases={t: 3 * n + t for t in range(2 * n)},
        compiler_params=_EFFECT)(*[_in_hbm(a) for a in ps], *[_in_hbm(lax.empty(a.shape, a.dtype)) for a in ps])
    return [(res[3 * t], res[3 * t + 1], res[3 * t + 2], res[3 * n + t], res[4 * n + t]) for t in range(n)], res[-1]


def _exchange_wait(handle, after, name):
    send_sems, recv_sems, local_sem, p, got = handle

    def body(p_ref, got_ref, ssem, rsem, lsem, after_ref, p_out, got_out):
        del after_ref, p_out, got_out
        x, y, c = _me()
        me = 4 * x + 2 * y + c
        pltpu.make_async_copy(p_ref.at[me], got_ref.at[me], lsem).wait()
        for m in range(1, N_DEV):
            _, blk = _peer(m)
            cp = _remote(p_ref.at[blk], got_ref.at[blk], ssem, rsem, m)
            cp.wait_send()
            cp.wait_recv()

    return pl.pallas_call(
        body, name=name, in_specs=[_HBM, _HBM, _SEM, _SEM, _SEM, _ANY], out_specs=[_HBM, _HBM],
        out_shape=[pltpu.HBM(p.shape, p.dtype), pltpu.HBM(got.shape, got.dtype)],
        input_output_aliases={0: 0, 1: 1}, compiler_params=_EFFECT)(p, got, send_sems, recv_sems, local_sem, after)[1]


def _cast_into_slot(w, layer, me1, name):
    _, r, c = w.shape
    tr = next(t for t in (256, 352, r) if r % t == 0)

    def body(me_ref, w_ref, o_ref):
        del me_ref
        o_ref[...] = w_ref[...].astype(BF16)

    return pl.pallas_call(
        body, name=name,
        grid_spec=pltpu.PrefetchScalarGridSpec(
            num_scalar_prefetch=1, grid=(r // tr,),
            in_specs=[pl.BlockSpec((None, tr, c), lambda i, me: (layer, i, 0))],
            out_specs=pl.BlockSpec((None, tr, c), lambda i, me: (me[0], i, 0))),
        out_shape=SDS((N_DEV, r, c), BF16), compiler_params=_cp("arbitrary"))(me1, w)


def _sum8_into_slot(p, me1, name):
    _, r, c = p.shape

    def body(me_ref, p_ref, o_ref):
        del me_ref
        acc = p_ref[0]
        for k in range(1, N_DEV):
            acc = acc + p_ref[k]
        o_ref[...] = acc

    return pl.pallas_call(
        body, name=name,
        grid_spec=pltpu.PrefetchScalarGridSpec(
            num_scalar_prefetch=1, grid=(1,),
            in_specs=[pl.BlockSpec(p.shape, lambda i, me: (0, 0, 0))],
            out_specs=pl.BlockSpec((None, r, c), lambda i, me: (me[0], 0, 0))),
        out_shape=SDS(p.shape, F32), compiler_params=_cp("arbitrary"))(me1, p)


def _adamw(w, g, m, v):
    m = ADAM_B1 * m + (1.0 - ADAM_B1) * g
    v = ADAM_B2 * v + (1.0 - ADAM_B2) * (g * g)
    m_hat = m / (1.0 - ADAM_B1 ** ADAM_STEP)
    v_hat = v / (1.0 - ADAM_B2 ** ADAM_STEP)
    delta = -ADAM_LR * (m_hat / (jnp.sqrt(v_hat) + ADAM_EPS) + ADAM_WD * w)
    return delta, m, v


def _adam_shard(parts, w, m, v, layer, prev, name):
    _, r, c = parts.shape
    tr = next(t for t in (256, 352, r) if r % t == 0)
    n_prev = 0 if prev is None else 4

    def body(*refs):
        p_ref, w_ref, m_ref, v_ref = refs[:4]
        g_ref, d_ref, nm_ref, nv_ref = refs[4 + n_prev:]
        g = p_ref[0].astype(F32)
        for k in range(1, N_DEV):
            g = g + p_ref[k].astype(F32)
        delta, nm, nv = _adamw(w_ref[...], g, m_ref[...], v_ref[...])
        g_ref[...] = g
        d_ref[...] = delta
        nm_ref[...] = nm
        nv_ref[...] = nv

    blk = pl.BlockSpec((None, tr, c), lambda i: (layer, i, 0))
    return pl.pallas_call(
        body, name=name, grid=(r // tr,),
        in_specs=[pl.BlockSpec((N_DEV, tr, c), lambda i: (0, i, 0)), blk, blk, blk] + [_ANY] * n_prev,
        out_specs=[blk] * 4, out_shape=[SDS(w.shape, F32)] * 4,
        input_output_aliases={4 + k: k for k in range(n_prev)},
        compiler_params=_cp("parallel"))(parts, w, m, v, *(prev or ()))


SMALL_MATRICES = [("lru_w_r", 2048), ("lru_w_i", 2048), ("gmlp_w_s", 1024)]
SMALL_VECTORS = [("norm1_g", 8), ("gmlp_ln_g", 8), ("gmlp_ln_b", 8), ("gmlp_b_s", 8), ("conv_w", 32), ("conv_b", 8),
                 ("lru_b_r", 16), ("lru_b_i", 16), ("lru_lambda", 16), ("norm2_g", 8), ("final_g", 8)]
SMALL_VECTOR_ROW0 = sum(n for _, n in SMALL_MATRICES)
SMALL_VECTOR_BLOCK = 256
SMALL_ROWS = SMALL_VECTOR_ROW0 + SMALL_VECTOR_BLOCK


def _pack_small(small):
    parts = [small[k] for k, _ in SMALL_MATRICES]
    parts += [small[k] if k in small else jnp.zeros((n, HD), F32) for k, n in SMALL_VECTORS]
    flat = jnp.concatenate(parts)
    return jnp.pad(flat, ((0, SMALL_ROWS - flat.shape[0]), (0, 0))).reshape(N_DEV, SMALL_ROWS // N_DEV, HD)


def _adam_matrix(g0, g1, w, m, v, row0, name):
    _, rows, _ = w.shape

    def body(g0_ref, g1_ref, w_ref, m_ref, v_ref, g_ref, d_ref, nm_ref, nv_ref):
        for l, src in enumerate((g0_ref, g1_ref)):
            g = src[...]
            delta, nm, nv = _adamw(w_ref[l], g, m_ref[l], v_ref[l])
            g_ref[l] = g
            d_ref[l] = delta
            nm_ref[l] = nm
            nv_ref[l] = nv

    gspec = pl.BlockSpec((rows, HD), lambda i: (row0 // rows, 0))
    return pl.pallas_call(body, name=name, grid=(1,), in_specs=[gspec, gspec] + [_full(w.shape)] * 3,
                          out_specs=[_full(w.shape)] * 4, out_shape=[SDS(w.shape, F32)] * 4,
                          compiler_params=_cp("arbitrary"))(g0, g1, w, m, v)


def _adam_vectors(g0, g1, dg1_parts, me1, ws, ms, vs):
    names = [k for k, _ in SMALL_VECTORS]
    n = len(names)

    def lanes(rows8):
        return jnp.concatenate([rows8[k:k + 1, :] for k in range(LANE_ROWS)], axis=1)

    def body(me_ref, g0_ref, g1_ref, dg1_ref, *refs):
        w_refs, m_refs, v_refs = refs[:n], refs[n:2 * n], refs[2 * n:3 * n]
        outs = refs[3 * n:]
        me = me_ref[0]
        g_refs = (g0_ref, g1_ref)

        def emit(i, idx, g):
            delta, nm, nv = _adamw(w_refs[i][idx], g, m_refs[i][idx], v_refs[i][idx])
            for j, val in enumerate((g, delta, nm, nv)):
                outs[4 * i + j][idx] = val

        off = 0
        for i, (name, rows) in enumerate(SMALL_VECTORS):
            for l in range(2):
                row = (slice(l, l + 1), slice(None))
                if name == "final_g":
                    if l == 1:
                        emit(i, (slice(0, 1), slice(None)), lanes(g1_ref[off:off + rows, :]))
                elif name == "norm1_g":
                    if l == 1:
                        emit(i, row, lanes(g0_ref[off:off + rows, :]))
                    else:
                        total = dg1_ref[0]
                        for k in range(1, N_DEV):
                            total = total + dg1_ref[k]
                        emit(i, row, lanes(total))
                elif name == "gmlp_b_s":
                    emit(i, (l,), g_refs[l][off:off + rows, :])
                elif rows == LANE_ROWS:
                    emit(i, row, lanes(g_refs[l][off:off + rows, :]))
                else:
                    for r in range(rows // LANE_ROWS):
                        emit(i, (l, slice(r, r + 1), slice(None)), g_refs[l][pl.ds(off + r * LANE_ROWS + me, 1), :])
            off += rows

    args = [ws[k] for k in names] + [ms[k] for k in names] + [vs[k] for k in names]
    gspec = pl.BlockSpec((SMALL_VECTOR_BLOCK, HD), lambda i, me: (SMALL_VECTOR_ROW0 // SMALL_VECTOR_BLOCK, 0))
    res = pl.pallas_call(
        body, name="adam_vectors",
        grid_spec=pltpu.PrefetchScalarGridSpec(
            num_scalar_prefetch=1, grid=(1,),
            in_specs=[gspec, gspec, _full(dg1_parts.shape)] + [_full(a.shape) for a in args],
            out_specs=[_full(ws[k].shape) for k in names for _ in range(4)]),
        out_shape=[SDS(ws[k].shape, F32) for k in names for _ in range(4)],
        compiler_params=_cp("arbitrary"))(me1, g0, g1, dg1_parts, *args)
    return {k: list(res[4 * i:4 * i + 4]) for i, k in enumerate(names)}


def _after(a, *tokens):
    for token in tokens:
        if token is not None:
            a = a + token[0:1, 0:1]
    return a


def _local_step(x, tgt, p, get_w, hook=lambda stage, layer, payload: None):
    s = x.shape[0]
    tm = _row_tile(s)
    wsb = p["gmlp_w_s"].astype(BF16)
    wstb = jnp.swapaxes(p["gmlp_w_s"], -1, -2).astype(BF16)
    bsb = jnp.broadcast_to(p["gmlp_b_s"][..., None], p["gmlp_w_s"].shape)
    wrb = p["lru_w_r"].astype(BF16)
    wib = p["lru_w_i"].astype(BF16)
    saved = []
    for l in range(2):
        win = get_w("w_in", l, x)
        z, h1 = _norm_inproj(x, _after(p["norm1_g"][l][None], hook("pre_inproj", l, win)), win, l, tm)
        ya = _gmlp_fwd(z, p["gmlp_ln_g"][l][None], p["gmlp_ln_b"][l][None], wsb[l], bsb[l], l, tm)
        a0, b0, a1, b1 = _lru_gates_fwd(z, p["conv_w"][l], p["conv_b"][l][None], wrb[l], wib[l],
                                        p["lru_b_r"][l], p["lru_b_i"][l], p["lru_lambda"][l], l, tm)
        h0, hr = _lru_scan(a0, b0, a1, b1, False, l)
        wout = get_w("w_out", l, h0)
        x1, mg = _merge_outproj(x, ya, h0, hr, z, wout, l, tm)
        wfi = get_w("w_ffn_in", l, x1)
        wfo = get_w("w_ffn_out", l, x1)
        x2, gu, h2 = _ffn_fwd(x1, _after(p["norm2_g"][l][None], hook("pre_ffn", l, wfi)), wfi, wfo, l, tm)
        saved.append((x, z, h1, ya, a0, a1, h0, hr, x1, mg, gu, h2, win, wout, wfi, wfo))
        x = x2
    dx, loss, dfg = _loss_head(x, p["final_g"][None], tgt, tm)
    pending = None
    for l in (1, 0):
        x0, z, h1, ya, a0, a1, h0, hr, x1, mg, gu, h2, win, wout, wfi, wfo = saved[l]
        ff, dgu = _ffn_bwd_act(dx, wfo, gu, l, tm)
        d_wfo = _mm_tn(ff, pl.BlockSpec((None, s, FF_BLK), lambda j: (j, 0, 0)), dx, _resident((s, D)),
                       4, (4, FF_BLK, D), pl.BlockSpec((None, FF_BLK, D), lambda j: (j, 0, 0)),
                       f"dw_ffn_out_l{l}", a_is_transposed=False)
        dgu8 = dgu.reshape(N_DEV, s, FF_BLK)
        d_wfi = _mm_tn(dgu8, pl.BlockSpec((None, s, FF_BLK), lambda j: (j, 0, 0)), h2, _resident((s, D)),
                       N_DEV, (N_DEV, FF_BLK, D), pl.BlockSpec((None, FF_BLK, D), lambda j: (j, 0, 0)),
                       f"dw_ffn_in_l{l}", a_is_transposed=False)
        token = hook("ffn_partials", l, dict(w_ffn_out=d_wfo.reshape(N_DEV, D_FF // N_DEV, D), w_ffn_in=d_wfi))
        dx1, dg2 = _mm_nt_rms_bwd(
            dgu8, pl.BlockSpec((N_DEV, tm, FF_BLK), lambda i: (0, i, 0)), lambda r: [r[k] for k in range(N_DEV)],
            wfi.reshape(N_DEV, FF_BLK, D), True, x1, _after(p["norm2_g"][l][None], token, pending), dx,
            f"ffn_bwd_dx_l{l}", tm)
        pending = hook("mid_backward", l, dx1)
        dz, dya, dh = _outproj_bwd_merge(dx1, wout, ya, h0, hr, z, l, tm)
        d_wout = _mm_tn(mg, _resident((D, s)), dx1, pl.BlockSpec((s, D // 2), lambda j: (0, j)),
                        2, (D, D), pl.BlockSpec((D, D // 2), lambda j: (0, j)), f"dw_out_l{l}")
        g1, g0 = _lru_scan(a1, dh, a0, dh, True, l)
        dxc, dwr, dwi, dbr, dbi, dlam = _lru_gates_bwd(
            z, h0, hr, g0, g1, p["conv_w"][l], p["conv_b"][l][None], wrb[l], wib[l],
            p["lru_b_r"][l], p["lru_b_i"][l], p["lru_lambda"][l], l, tm)
        dz, dcw, dcb = _conv_bwd(dz, dxc, z, p["conv_w"][l], l, tm)
        dz, dws, dbs, dlng, dlnb = _gmlp_bwd(dz, z, dya, p["gmlp_ln_g"][l][None], p["gmlp_ln_b"][l][None],
                                             wsb[l], wstb[l], bsb[l], l, tm)
        small = dict(lru_w_r=dwr.reshape(-1, HD), lru_w_i=dwi.reshape(-1, HD), gmlp_w_s=dws.reshape(-1, HD),
                     gmlp_ln_g=dlng, gmlp_ln_b=dlnb, gmlp_b_s=dbs, conv_w=dcw, conv_b=dcb, lru_b_r=dbr,
                     lru_b_i=dbi, lru_lambda=dlam, norm2_g=dg2)
        if l == 1:
            small["final_g"] = dfg
        else:
            small["norm1_g"] = dg1
        started = hook("small_grads", l, small)
        d_win = _mm_tn(h1, _resident((D, s)), dz, pl.BlockSpec((s, IN_BLK), lambda j: (0, j)),
                       N_DEV, (N_DEV, D, IN_BLK), pl.BlockSpec((None, D, IN_BLK), lambda j: (j, 0, 0)),
                       f"dw_in_l{l}")
        token = hook("mixer_partials", l, dict(w_out=d_wout.reshape(N_DEV, D // N_DEV, D), w_in=d_win))
        dx, dg1 = _mm_nt_rms_bwd(
            dz, pl.BlockSpec((tm, N_IN), lambda i: (i, 0)),
            lambda r: [r[:, k * IN_BLK:(k + 1) * IN_BLK] for k in range(N_DEV)],
            win, False, x0, _after(p["norm1_g"][l][None], token, started, pending), dx1, f"inproj_bwd_dx_l{l}", tm)
        pending = None
    return loss, dx, dg1


_REPL = ["norm1_g", "gmlp_ln_g", "gmlp_ln_b", "gmlp_w_s", "gmlp_b_s", "conv_b", "lru_w_r", "lru_w_i", "norm2_g", "final_g"]
_LANE_SHARDED = ["conv_w", "lru_b_r", "lru_b_i", "lru_lambda"]
_BIG = ["w_in", "w_out", "w_ffn_in", "w_ffn_out"]
_ORDER = ["norm1_g", "w_in", "gmlp_ln_g", "gmlp_ln_b", "gmlp_w_s", "gmlp_b_s", "conv_w", "conv_b", "lru_w_r", "lru_b_r",
          "lru_w_i", "lru_b_i", "lru_lambda", "w_out", "norm2_g", "w_ffn_in", "w_ffn_out", "final_g"]


def kernel(x, norm1_g, w_in, gmlp_ln_g, gmlp_ln_b, gmlp_w_s, gmlp_b_s, conv_w, conv_b, lru_w_r, lru_b_r, lru_w_i, lru_b_i, lru_lambda, w_out, norm2_g, w_ffn_in, w_ffn_out, final_g, loss_target, m_norm1_g, m_w_in, m_gmlp_ln_g, m_gmlp_ln_b, m_gmlp_w_s, m_gmlp_b_s, m_conv_w, m_conv_b, m_lru_w_r, m_lru_b_r, m_lru_w_i, m_lru_b_i, m_lru_lambda, m_w_out, m_norm2_g, m_w_ffn_in, m_w_ffn_out, m_final_g, v_norm1_g, v_w_in, v_gmlp_ln_g, v_gmlp_ln_b, v_gmlp_w_s, v_gmlp_b_s, v_conv_w, v_conv_b, v_lru_w_r, v_lru_b_r, v_lru_w_i, v_lru_b_i, v_lru_lambda, v_w_out, v_norm2_g, v_w_ffn_in, v_w_ffn_out, v_final_g):
    w = dict(norm1_g=norm1_g, w_in=w_in, gmlp_ln_g=gmlp_ln_g, gmlp_ln_b=gmlp_ln_b, gmlp_w_s=gmlp_w_s, gmlp_b_s=gmlp_b_s,
             conv_w=conv_w, conv_b=conv_b, lru_w_r=lru_w_r, lru_b_r=lru_b_r, lru_w_i=lru_w_i, lru_b_i=lru_b_i,
             lru_lambda=lru_lambda, w_out=w_out, norm2_g=norm2_g, w_ffn_in=w_ffn_in, w_ffn_out=w_ffn_out, final_g=final_g)
    mom = dict(norm1_g=m_norm1_g, w_in=m_w_in, gmlp_ln_g=m_gmlp_ln_g, gmlp_ln_b=m_gmlp_ln_b, gmlp_w_s=m_gmlp_w_s,
               gmlp_b_s=m_gmlp_b_s, conv_w=m_conv_w, conv_b=m_conv_b, lru_w_r=m_lru_w_r, lru_b_r=m_lru_b_r,
               lru_w_i=m_lru_w_i, lru_b_i=m_lru_b_i, lru_lambda=m_lru_lambda, w_out=m_w_out, norm2_g=m_norm2_g,
               w_ffn_in=m_w_ffn_in, w_ffn_out=m_w_ffn_out, final_g=m_final_g)
    var = dict(norm1_g=v_norm1_g, w_in=v_w_in, gmlp_ln_g=v_gmlp_ln_g, gmlp_ln_b=v_gmlp_ln_b, gmlp_w_s=v_gmlp_w_s,
               gmlp_b_s=v_gmlp_b_s, conv_w=v_conv_w, conv_b=v_conv_b, lru_w_r=v_lru_w_r, lru_b_r=v_lru_b_r,
               lru_w_i=v_lru_w_i, lru_b_i=v_lru_b_i, lru_lambda=v_lru_lambda, w_out=v_w_out, norm2_g=v_norm2_g,
               w_ffn_in=v_w_ffn_in, w_ffn_out=v_w_ffn_out, final_g=v_final_g)
    for src in (w, mom, var):
        src["w_ffn_in"] = jnp.swapaxes(src["w_ffn_in"], 1, 2)
    xi, yi, ci = _me()
    me = 4 * xi + 2 * yi + ci

    lane_shapes = [w[k].shape for k in _LANE_SHARDED]
    lane_rows = sum(a[0] * a[1] for a in lane_shapes)
    packed = jnp.concatenate([w[k].reshape(-1, HD) for k in _LANE_SHARDED])
    packed = jnp.pad(packed, ((0, -lane_rows % 8), (0, 0)))
    lanes = _all_gather(packed, packed, "gather_small")
    params = {k: w[k] for k in _REPL}
    off = 0
    for k, shp in zip(_LANE_SHARDED, lane_shapes):
        n = shp[0] * shp[1]
        params[k] = jnp.swapaxes(lanes[:, off:off + n], 0, 1).reshape(shp[0], shp[1], D)
        off += n

    me1 = jnp.reshape(me, (1,)).astype(jnp.int32)
    gathers = {}
    exchanges = {}
    views = dict(w_in=(N_DEV, D, IN_BLK), w_out=(D, D), w_ffn_in=(2, 4, FF_BLK, D), w_ffn_out=(4, FF_BLK, D))
    small_ex = {}
    small_ag = {}

    def start_gather(names, l, after):
        lands = [_cast_into_slot(w[k], l, me1, f"cast_{k}_l{l}") for k in names]
        started, tok = _gather_start(lands, after, f"gather_start_{'_'.join(names)}_l{l}")
        gathers.update({(k, l): h for k, h in zip(names, started)})
        return tok

    def get_w(k, l, after):
        return _gather_wait(gathers[(k, l)], after, f"gather_wait_{k}_l{l}").reshape(views[k])

    def hook(stage, l, payload):
        if stage == "pre_inproj":
            return start_gather(_BIG[1:], l, payload)
        if stage == "pre_ffn":
            return start_gather(_BIG[:1], l + 1, payload) if l == 0 else None
        if stage == "small_grads":
            (small_ex[l],), tok = _exchange_start([_pack_small(payload)], f"exchange_start_small_l{l}")
            return tok
        if stage == "mid_backward":
            return reduce_small(l + 1, payload) if l == 0 else None
        extra = reduce_small(0, payload["w_in"]) if (stage, l) == ("mixer_partials", 0) else None
        started, tok = _exchange_start(list(payload.values()), f"exchange_start_{'_'.join(payload)}_l{l}")
        exchanges.update({(k, l): h for k, h in zip(payload, started)})
        return tok if extra is None else tok + extra

    def reduce_small(l, after):
        got = _exchange_wait(small_ex[l], after, f"exchange_wait_small_l{l}")
        mine = _sum8_into_slot(got, me1, f"sum_small_l{l}")
        (small_ag[l],), tok = _gather_start([mine], got, f"gather_start_small_l{l}")
        return tok

    start_gather(_BIG[:1], 0, lanes)
    loss, dx, dg1 = _local_step(x[0], loss_target[0], params, get_w, hook)

    out = {}
    after = dx
    for k, l in [(k, l) for k in ("w_ffn_out", "w_ffn_in") for l in (1, 0)] + [("w_out", 1), ("w_in", 1)]:
        got = _exchange_wait(exchanges[(k, l)], after, f"exchange_wait_{k}_l{l}")
        out[k] = _adam_shard(got, w[k], mom[k], var[k], l, out.get(k), f"adam_{k}_l{l}")
        after = out[k][3]
    g_small = [_gather_wait(small_ag[l], after, f"gather_wait_small_l{l}").reshape(SMALL_ROWS, HD) for l in (0, 1)]
    row0 = 0
    for k, rows in SMALL_MATRICES:
        res = _adam_matrix(*g_small, *[src[k].reshape(2, rows, HD) for src in (w, mom, var)], row0, f"adam_{k}")
        out[k] = [a.reshape(w[k].shape) for a in res]
        after = res[3]
        row0 += rows
    for k in ("w_out", "w_in"):
        got = _exchange_wait(exchanges[(k, 0)], after, f"exchange_wait_{k}_l0")
        out[k] = _adam_shard(got, w[k], mom[k], var[k], 0, out[k], f"adam_{k}_l0")
    out["w_ffn_in"] = [jnp.swapaxes(a, 1, 2) for a in out["w_ffn_in"]]
    as_rows = lambda a: a.reshape(1, D) if a.ndim == 1 else a
    vec = _adam_vectors(*g_small, _all_gather(dg1, out["w_in"][3], "gather_norm1_grad"), me1,
                        *[{k: as_rows(src[k]) for k, _ in SMALL_VECTORS} for src in (w, mom, var)])
    out.update({k: [a.reshape(w[k].shape) for a in res] for k, res in vec.items()})

    loss = lax.psum(loss[0, 0], MESH_AXES)
    return (loss, dx[None], *[out[k][0] for k in _ORDER], *[out[k][1] for k in _ORDER],
            *[out[k][2] for k in _ORDER], *[out[k][3] for k in _ORDER])
```

```python
import jax
import jax.numpy as jnp
from jax import lax
from jax.experimental import pallas as pl
from jax.experimental.pallas import tpu as pltpu

F32 = jnp.float32
BF16 = jnp.bfloat16
SDS = jax.ShapeDtypeStruct

D = 1024
N_IN = 6 * D
D_FF = 2816
N_DEV = 8
IN_BLK = N_IN // N_DEV
FF_BLK = 2 * D_FF // N_DEV
HEADS = 8
HD = 128
EPS = 1e-6
LRU_C = 8.0
MESH_AXES = ("x", "y", "c")

ADAM_LR = 0.001
ADAM_B1 = 0.9
ADAM_B2 = 0.999
ADAM_EPS = 1e-08
ADAM_WD = 0.01
ADAM_STEP = 10

VMEM_LIMIT = 56 * 2**20


def _cp(*sem, **kw):
    return pltpu.CompilerParams(dimension_semantics=sem, vmem_limit_bytes=VMEM_LIMIT, **kw)


def _row_tile(s):
    return 512 if s >= 1024 else s // 2


_GELU_C = 0.7978845608028654


def _gelu(x):
    t = jnp.tanh(_GELU_C * (x + 0.044715 * (x * x * x)))
    return 0.5 * x * (1.0 + t), t


def _gelu_grad(x, t):
    return 0.5 * (1.0 + t) + 0.5 * x * (1.0 - t * t) * (_GELU_C * (1.0 + 0.134145 * (x * x)))


def _sigmoid(x):
    return 0.5 + 0.5 * jnp.tanh(0.5 * x)


def _softplus(x):
    e = jnp.exp(-jnp.abs(x))
    w = 1.0 + e
    l1p = jnp.where(w == 1.0, e, jnp.log(w) * e / jnp.where(w == 1.0, 1.0, w - 1.0))
    return jnp.maximum(x, 0.0) + l1p


def _rms_fwd(x, g):
    r = lax.rsqrt(jnp.mean(x * x, axis=-1, keepdims=True) + EPS)
    return x * r * g


def _rms_bwd(x, g, dh):
    r = lax.rsqrt(jnp.mean(x * x, axis=-1, keepdims=True) + EPS)
    xh = x * r
    dxh = dh * g
    dx = r * (dxh - xh * jnp.mean(dxh * xh, axis=-1, keepdims=True))
    dg = jnp.sum(dh * xh, axis=0, keepdims=True)
    return dx, dg


LANE_ROWS = D // HD


def _add_rows128(ref, vec, row0=0):
    for i in range(vec.shape[0]):
        for k in range(LANE_ROWS):
            j = row0 + i * LANE_ROWS + k
            ref[j:j + 1, :] += vec[i:i + 1, k * HD:(k + 1) * HD]


def _dot(a, b):
    return jnp.dot(a, b, preferred_element_type=F32)


def _dot_nt(a, b):
    return lax.dot_general(a, b, (((1,), (1,)), ((), ())), preferred_element_type=F32)


def _dot_tn(a, b):
    return lax.dot_general(a, b, (((0,), (0,)), ((), ())), preferred_element_type=F32)


def _taps(prev, cur, nxt, tm):
    hr = prev.shape[0]
    ext = jnp.concatenate([prev, cur, nxt], axis=0)
    n = tm + 2 * hr
    sl = slice(hr, hr + tm)
    return (pltpu.roll(ext, 2, 0)[sl], pltpu.roll(ext, 1, 0)[sl], cur,
            pltpu.roll(ext, n - 1, 0)[sl], pltpu.roll(ext, n - 2, 0)[sl])


def _halo_specs(tm, s, col, rows=8):
    nb = s // rows
    r = tm // rows
    return (pl.BlockSpec((rows, D), lambda i: (jnp.maximum(i * r - 1, 0), col)),
            pl.BlockSpec((tm, D), lambda i: (i, col)),
            pl.BlockSpec((rows, D), lambda i: (jnp.minimum((i + 1) * r, nb - 1), col)))


def _halo_load(prev_ref, cur_ref, next_ref, fp, fn):
    return prev_ref[...].astype(F32) * fp, cur_ref[...].astype(F32), next_ref[...].astype(F32) * fn


def _halo_flags(nt):
    i = pl.program_id(0)
    return (i > 0).astype(F32), (i < nt - 1).astype(F32)


def _full(shape):
    nd = len(shape)
    return pl.BlockSpec(shape, lambda *_: (0,) * nd)


def _resident(shape):
    nd = len(shape)
    return pl.BlockSpec(shape, lambda *_: (0,) * nd, pipeline_mode=pl.Buffered(1))


def _norm_inproj(x, g, w, layer, tm):
    s = x.shape[0]

    def body(x_ref, g_ref, w_ref, z_ref, ht_ref):
        h32 = _rms_fwd(x_ref[...], g_ref[...])
        ht_ref[...] = h32.T.astype(BF16)
        h = h32.astype(BF16)
        for j in range(N_DEV):
            z_ref[:, j * IN_BLK:(j + 1) * IN_BLK] = _dot(h, w_ref[j]).astype(BF16)

    return pl.pallas_call(
        body, name=f"norm_inproj_l{layer}", grid=(s // tm,),
        in_specs=[pl.BlockSpec((tm, D), lambda i: (i, 0)), _full((1, D)), _resident((N_DEV, D, IN_BLK))],
        out_specs=[pl.BlockSpec((tm, N_IN), lambda i: (i, 0)), pl.BlockSpec((D, tm), lambda i: (0, i))],
        out_shape=[SDS((s, N_IN), BF16), SDS((D, s), BF16)],
        compiler_params=_cp("parallel"))(x, g, w)


def _gmlp_fwd(z, lng, lnb, ws, bsb, layer, tm):
    s = z.shape[0]

    def body(zu_ref, zv_ref, lng_ref, lnb_ref, ws_ref, bsb_ref, ya_ref):
        u, _ = _gelu(zu_ref[...].astype(F32))
        gv, _ = _gelu(zv_ref[...].astype(F32))
        xc = gv - jnp.mean(gv, axis=-1, keepdims=True)
        rstd = lax.rsqrt(jnp.mean(xc * xc, axis=-1, keepdims=True) + EPS)
        vb = (xc * rstd * lng_ref[...] + lnb_ref[...]).astype(BF16)
        for c in range(tm // HD):
            rs = slice(c * HD, (c + 1) * HD)
            for g in range(HEADS):
                cs = slice(g * HD, (g + 1) * HD)
                mixed = _dot(ws_ref[g], vb[rs, cs]) + bsb_ref[g]
                ya_ref[rs, cs] = (u[rs, cs] * mixed).astype(BF16)

    return pl.pallas_call(
        body, name=f"gmlp_fwd_l{layer}", grid=(s // tm,),
        in_specs=[pl.BlockSpec((tm, D), lambda i: (i, 0)), pl.BlockSpec((tm, D), lambda i: (i, 1)),
                  _full((1, D)), _full((1, D)), _full((HEADS, HD, HD)), _full((HEADS, HD, HD))],
        out_specs=pl.BlockSpec((tm, D), lambda i: (i, 0)),
        out_shape=SDS((s, D), BF16),
        compiler_params=_cp("parallel"))(z, z, lng, lnb, ws, bsb)


def _conv(taps, cw_ref, cb_ref):
    _, m1, c0, p1, p2 = taps
    return cb_ref[...] + m1 * cw_ref[0:1, :] + c0 * cw_ref[1:2, :] + p1 * cw_ref[2:3, :] + p2 * cw_ref[3:4, :]


def _heads_dot(xb, w_ref, d):
    return jnp.concatenate([_dot(xb[:, h * HD:(h + 1) * HD], w_ref[d, h]) for h in range(HEADS)], axis=1)


def _lru_gates(xc, xb, wr_ref, wi_ref, br_ref, bi_ref, lam_ref, d):
    sp = _softplus(-lam_ref[d:d + 1, :])
    r = _sigmoid(_heads_dot(xb, wr_ref, d) + br_ref[d:d + 1, :])
    ig = _sigmoid(_heads_dot(xb, wi_ref, d) + bi_ref[d:d + 1, :])
    la = (-LRU_C) * r * sp
    a = jnp.exp(la)
    q = jnp.maximum(jnp.tanh(-la) * (a * a + 1.0), 0.0)
    rq = jnp.where(q > 0.0, lax.rsqrt(jnp.where(q > 0.0, q, 1.0)), 0.0)
    return sp, r, ig, a, q * rq, rq


def _lru_gates_fwd(z, cw, cb, wr, wi, br, bi, lam, layer, tm):
    s = z.shape[0]
    nt = s // tm

    def body(zp_ref, zc_ref, zn_ref, cw_ref, cb_ref, wr_ref, wi_ref, br_ref, bi_ref, lam_ref,
             a0_ref, b0_ref, a1_ref, b1_ref):
        fp, fn = _halo_flags(nt)
        xc = _conv(_taps(*_halo_load(zp_ref, zc_ref, zn_ref, fp, fn), tm), cw_ref, cb_ref)
        xb = xc.astype(BF16)
        for d, (a_ref, b_ref) in enumerate(((a0_ref, b0_ref), (a1_ref, b1_ref))):
            _, _, ig, a, mult, _ = _lru_gates(xc, xb, wr_ref, wi_ref, br_ref, bi_ref, lam_ref, d)
            a_ref[...] = a
            b_ref[...] = mult * (ig * xc)

    tile = pl.BlockSpec((tm, D), lambda i: (i, 0))
    return pl.pallas_call(
        body, name=f"lru_gates_fwd_l{layer}", grid=(nt,),
        in_specs=[*_halo_specs(tm, s, 2, 16), _full((4, D)), _full((1, D)),
                  _full((2, HEADS, HD, HD)), _full((2, HEADS, HD, HD)), _full((2, D)), _full((2, D)), _full((2, D))],
        out_specs=[tile] * 4, out_shape=[SDS((s, D), F32)] * 4,
        compiler_params=_cp("parallel"))(z, z, z, cw, cb, wr, wi, br, bi, lam)


def _scan_group(a, x, c, reverse, bwd):
    row = lax.broadcasted_iota(jnp.int32, a.shape, 0)
    b = a * x if bwd else x
    for d in (1, 2, 4):
        keep = (row < 8 - d) if reverse else (row >= d)
        sh = 8 - d if reverse else d
        a_s = jnp.where(keep, pltpu.roll(a, sh, 0), 1.0)
        b_s = jnp.where(keep, pltpu.roll(b, sh, 0), 0.0)
        b = a * b_s + b
        a = a * a_s
    h = b + a * c
    new_c = h[0:1, :] if reverse else h[7:8, :]
    if not bwd:
        return h, new_c
    if reverse:
        prev = jnp.where(row < 7, pltpu.roll(h, 7, 0), c)
    else:
        prev = jnp.where(row >= 1, pltpu.roll(h, 1, 0), c)
    return x + prev, new_c


def _lru_scan(a_f, x_f, a_r, x_r, bwd, layer):
    s = a_f.shape[0]
    ts = min(1024, s // 2)
    cb = 512
    nt = s // ts
    ng = ts // 8

    def body(af_ref, xf_ref, ar_ref, xr_ref, of_ref, or_ref, cf, cr):
        @pl.when(pl.program_id(1) == 0)
        def _():
            cf[...] = jnp.zeros_like(cf)
            cr[...] = jnp.zeros_like(cr)

        def step(j, carry):
            c_f, c_r = carry
            rf = pl.multiple_of(j * 8, 8)
            rr = pl.multiple_of((ng - 1 - j) * 8, 8)
            o, c_f = _scan_group(af_ref[pl.ds(rf, 8), :], xf_ref[pl.ds(rf, 8), :], c_f, False, bwd)
            of_ref[pl.ds(rf, 8), :] = o
            o, c_r = _scan_group(ar_ref[pl.ds(rr, 8), :], xr_ref[pl.ds(rr, 8), :], c_r, True, bwd)
            or_ref[pl.ds(rr, 8), :] = o
            return c_f, c_r

        c_f, c_r = lax.fori_loop(0, ng, step, (cf[0:1, :], cr[0:1, :]), unroll=2)
        cf[...] = jnp.broadcast_to(c_f, cf.shape)
        cr[...] = jnp.broadcast_to(c_r, cr.shape)

    fwd = pl.BlockSpec((ts, cb), lambda c, t: (t, c))
    rev = pl.BlockSpec((ts, cb), lambda c, t: (nt - 1 - t, c))
    return pl.pallas_call(
        body, name=f"lru_scan_{'bwd' if bwd else 'fwd'}_l{layer}", grid=(D // cb, nt),
        in_specs=[fwd, fwd, rev, rev], out_specs=[fwd, rev],
        out_shape=[SDS((s, D), F32)] * 2,
        scratch_shapes=[pltpu.VMEM((8, cb), F32), pltpu.VMEM((8, cb), F32)],
        compiler_params=_cp("parallel", "arbitrary"))(a_f, x_f, a_r, x_r)


def _merge_outproj(x, ya, h0, h1, z, wo, layer, tm):
    s = x.shape[0]

    def body(x_ref, ya_ref, h0_ref, h1_ref, zg_ref, za_ref, zb_ref, wo_ref, x1_ref, mg_ref):
        gg, _ = _gelu(zg_ref[...].astype(F32))
        yb = (h0_ref[...] + h1_ref[...]) * gg
        m32 = (_sigmoid(za_ref[...].astype(F32)) * ya_ref[...].astype(F32)
               + _sigmoid(zb_ref[...].astype(F32)) * yb)
        mg_ref[...] = m32.T.astype(BF16)
        x1_ref[...] = x_ref[...] + _dot(m32.astype(BF16), wo_ref[...])

    tile = pl.BlockSpec((tm, D), lambda i: (i, 0))
    return pl.pallas_call(
        body, name=f"merge_outproj_l{layer}", grid=(s // tm,),
        in_specs=[tile, tile, tile, tile]
        + [pl.BlockSpec((tm, D), lambda i, c=c: (i, c)) for c in (3, 4, 5)] + [_full((D, D))],
        out_specs=[tile, pl.BlockSpec((D, tm), lambda i: (0, i))], out_shape=[SDS((s, D), F32), SDS((D, s), BF16)],
        compiler_params=_cp("parallel"))(x, ya, h0, h1, z, z, z, wo)


def _ffn_fwd(x1, g, wfi, wfo, layer, tm):
    s = x1.shape[0]

    def body(x_ref, g_ref, wi_ref, wo_ref, x2_ref, gu_ref, h_ref):
        x = x_ref[...]
        h = _rms_fwd(x, g_ref[...]).astype(BF16)
        h_ref[...] = h
        acc = x
        for k in range(4):
            gate = _dot_nt(h, wi_ref[0, k])
            up = _dot_nt(h, wi_ref[1, k])
            gu_ref[0, k] = gate.astype(BF16)
            gu_ref[1, k] = up.astype(BF16)
            acc = acc + _dot((gate * _sigmoid(gate) * up).astype(BF16), wo_ref[k])
        x2_ref[...] = acc

    tile = pl.BlockSpec((tm, D), lambda i: (i, 0))
    return pl.pallas_call(
        body, name=f"ffn_fwd_l{layer}", grid=(s // tm,),
        in_specs=[tile, _full((1, D)), _resident((2, 4, FF_BLK, D)), _resident((4, FF_BLK, D))],
        out_specs=[tile, pl.BlockSpec((2, 4, tm, FF_BLK), lambda i: (0, 0, i, 0)), tile],
        out_shape=[SDS((s, D), F32), SDS((2, 4, s, FF_BLK), BF16), SDS((s, D), BF16)],
        compiler_params=_cp("parallel"))(x1, g, wfi, wfo)


def _loss_head(x, g, tgt, tm):
    s = x.shape[0]

    def body(x_ref, g_ref, t_ref, dx_ref, loss_ref, dg_ref):
        @pl.when(pl.program_id(0) == 0)
        def _():
            loss_ref[...] = jnp.zeros_like(loss_ref)
            dg_ref[...] = jnp.zeros_like(dg_ref)

        x = x_ref[...]
        gv = g_ref[...]
        e = _rms_fwd(x, gv) - t_ref[...]
        rows = jnp.sum(e * e, axis=-1, keepdims=True)
        loss_ref[...] += (0.5 / D) * jnp.sum(rows, axis=0, keepdims=True)
        dx, dg = _rms_bwd(x, gv, e * (1.0 / D))
        dx_ref[...] = dx
        _add_rows128(dg_ref, dg)

    tile = pl.BlockSpec((tm, D), lambda i: (i, 0))
    return pl.pallas_call(
        body, name="loss_head", grid=(s // tm,),
        in_specs=[tile, _full((1, D)), tile],
        out_specs=[tile, _full((1, 1)), _full((LANE_ROWS, HD))],
        out_shape=[SDS((s, D), F32), SDS((1, 1), F32), SDS((LANE_ROWS, HD), F32)],
        compiler_params=_cp("arbitrary"))(x, g, tgt)


def _ffn_bwd_act(dx2, wfo, gu, layer, tm):
    s = dx2.shape[0]

    def body(dx_ref, wo_ref, gu_ref, ff_ref, dgu_ref):
        dxb = dx_ref[...].astype(BF16)
        for k in range(4):
            dff = _dot_nt(dxb, wo_ref[k])
            gate = gu_ref[0, k].astype(F32)
            up = gu_ref[1, k].astype(F32)
            sg = _sigmoid(gate)
            sl = gate * sg
            ff_ref[k] = (sl * up).astype(BF16)
            dgu_ref[0, k] = (dff * up * (sg * (1.0 + gate * (1.0 - sg)))).astype(BF16)
            dgu_ref[1, k] = (dff * sl).astype(BF16)

    blk = pl.BlockSpec((2, 4, tm, FF_BLK), lambda i: (0, 0, i, 0))
    return pl.pallas_call(
        body, name=f"ffn_bwd_act_l{layer}", grid=(s // tm,),
        in_specs=[pl.BlockSpec((tm, D), lambda i: (i, 0)), _resident((4, FF_BLK, D)), blk],
        out_specs=[pl.BlockSpec((4, tm, FF_BLK), lambda i: (0, i, 0)), blk],
        out_shape=[SDS((4, s, FF_BLK), BF16), SDS((2, 4, s, FF_BLK), BF16)],
        compiler_params=_cp("parallel"))(dx2, wfo, gu)


def _mm_nt_rms_bwd(a, a_spec, a_blocks, w, w_is_transposed, x, g, dres, name, tm):
    s = x.shape[0]

    def body(a_ref, w_ref, x_ref, g_ref, dres_ref, dx_ref, dg_ref):
        @pl.when(pl.program_id(0) == 0)
        def _():
            dg_ref[...] = jnp.zeros_like(dg_ref)

        dh = None
        for k, blk in enumerate(a_blocks(a_ref)):
            part = _dot(blk, w_ref[k]) if w_is_transposed else _dot_nt(blk, w_ref[k])
            dh = part if dh is None else dh + part
        dx, dg = _rms_bwd(x_ref[...], g_ref[...], dh)
        dx_ref[...] = dres_ref[...] + dx
        _add_rows128(dg_ref, dg)

    tile = pl.BlockSpec((tm, D), lambda i: (i, 0))
    return pl.pallas_call(
        body, name=name, grid=(s // tm,),
        in_specs=[a_spec, _resident(w.shape), tile, _full((1, D)), tile],
        out_specs=[tile, _full((LANE_ROWS, HD))], out_shape=[SDS((s, D), F32), SDS((LANE_ROWS, HD), F32)],
        compiler_params=_cp("arbitrary"))(a, w, x, g, dres)


def _mm_tn(a, a_spec, b, b_spec, nb, out_shape, out_spec, name, a_is_transposed=True, after=None):
    def body(a_ref, b_ref, *rest):
        o_ref = rest[-1]
        bb = b_ref[...].astype(BF16)
        o_ref[...] = (_dot(a_ref[...], bb) if a_is_transposed else _dot_tn(a_ref[...], bb)).astype(BF16)

    deps = [] if after is None else [after]
    return pl.pallas_call(
        body, name=name, grid=(nb,), in_specs=[a_spec, b_spec] + [_ANY] * len(deps), out_specs=out_spec,
        out_shape=SDS(out_shape, BF16), compiler_params=_cp("parallel"))(a, b, *deps)


def _outproj_bwd_merge(dx1, wo, ya, h0, h1, z, layer, tm):
    s = dx1.shape[0]

    def body(dx_ref, wo_ref, ya_ref, h0_ref, h1_ref, zg_ref, za_ref, zb_ref, dz_ref, dya_ref, dh_ref):
        dm = _dot_nt(dx_ref[...].astype(BF16), wo_ref[...])
        sa = _sigmoid(za_ref[...].astype(F32))
        sb = _sigmoid(zb_ref[...].astype(F32))
        zg = zg_ref[...].astype(F32)
        gg, tg = _gelu(zg)
        hs = h0_ref[...] + h1_ref[...]
        dyb = dm * sb
        dya_ref[...] = (dm * sa).astype(BF16)
        dh_ref[...] = dyb * gg
        dz_ref[:, 0:D] = (dyb * hs * _gelu_grad(zg, tg)).astype(BF16)
        dz_ref[:, D:2 * D] = (dm * ya_ref[...].astype(F32) * (sa * (1.0 - sa))).astype(BF16)
        dz_ref[:, 2 * D:3 * D] = (dm * (hs * gg) * (sb * (1.0 - sb))).astype(BF16)

    tile = pl.BlockSpec((tm, D), lambda i: (i, 0))
    return pl.pallas_call(
        body, name=f"outproj_bwd_merge_l{layer}", grid=(s // tm,),
        in_specs=[tile, _full((D, D)), tile, tile, tile]
        + [pl.BlockSpec((tm, D), lambda i, c=c: (i, c)) for c in (3, 4, 5)],
        out_specs=[pl.BlockSpec((tm, 3 * D), lambda i: (i, 1)), tile, tile],
        out_shape=[SDS((s, N_IN), BF16), SDS((s, D), BF16), SDS((s, D), F32)],
        compiler_params=_cp("parallel"))(dx1, wo, ya, h0, h1, z, z, z)


def _lru_gates_bwd(z, h0, h1, g0, g1, cw, cb, wr, wi, br, bi, lam, layer, tm):
    s = z.shape[0]
    nt = s // tm

    def body(zp_ref, zc_ref, zn_ref, h0p_ref, h0_ref, h1_ref, h1n_ref, g0_ref, g1_ref,
             cw_ref, cb_ref, wr_ref, wi_ref, br_ref, bi_ref, lam_ref,
             dxc_ref, dwr_ref, dwi_ref, dbr_ref, dbi_ref, dlam_ref):
        i = pl.program_id(0)
        fp, fn = _halo_flags(nt)

        @pl.when(i == 0)
        def _():
            for r in (dwr_ref, dwi_ref, dbr_ref, dbi_ref, dlam_ref):
                r[...] = jnp.zeros_like(r)

        xc = _conv(_taps(*_halo_load(zp_ref, zc_ref, zn_ref, fp, fn), tm), cw_ref, cb_ref)
        xb = xc.astype(BF16)
        zeros8 = jnp.zeros((8, D), F32)
        h_prev = _taps(h0p_ref[...] * fp, h0_ref[...], zeros8, tm)[1]
        h_next = _taps(zeros8, h1_ref[...], h1n_ref[...] * fn, tm)[3]
        dxc = jnp.zeros((tm, D), F32)
        for d, (g_ref, hsh) in enumerate(((g0_ref, h_prev), (g1_ref, h_next))):
            sp, r, ig, a, mult, rmult = _lru_gates(xc, xb, wr_ref, wi_ref, br_ref, bi_ref, lam_ref, d)
            db = g_ref[...]
            da = db * hsh
            dmult = db * (ig * xc)
            di = db * (mult * xc)
            dxc = dxc + db * (mult * ig)
            dla = da * a - dmult * (a * a * rmult)
            dsp_dlam = -_sigmoid(-lam_ref[d:d + 1, :])
            _add_rows128(dlam_ref, jnp.sum(dla * r, axis=0, keepdims=True) * ((-LRU_C) * dsp_dlam), d * LANE_ROWS)
            dpr = dla * sp * (-LRU_C) * (r * (1.0 - r))
            dpi = di * (ig * (1.0 - ig))
            _add_rows128(dbr_ref, jnp.sum(dpr, axis=0, keepdims=True), d * LANE_ROWS)
            _add_rows128(dbi_ref, jnp.sum(dpi, axis=0, keepdims=True), d * LANE_ROWS)
            dprb = dpr.astype(BF16)
            dpib = dpi.astype(BF16)
            parts = []
            for h in range(HEADS):
                cs = slice(h * HD, (h + 1) * HD)
                dwr_ref[d, h] += _dot_tn(xb[:, cs], dprb[:, cs])
                dwi_ref[d, h] += _dot_tn(xb[:, cs], dpib[:, cs])
                parts.append(_dot_nt(dprb[:, cs], wr_ref[d, h]) + _dot_nt(dpib[:, cs], wi_ref[d, h]))
            dxc = dxc + jnp.concatenate(parts, axis=1)
        dxc_ref[...] = dxc.astype(BF16)

    tile = pl.BlockSpec((tm, D), lambda i: (i, 0))
    zp, zc, zn = _halo_specs(tm, s, 2, 16)
    hp, hc, hn = _halo_specs(tm, s, 0)
    wspec = _full((2, HEADS, HD, HD))
    vspec = _full((2 * LANE_ROWS, HD))
    return pl.pallas_call(
        body, name=f"lru_gates_bwd_l{layer}", grid=(nt,),
        in_specs=[zp, zc, zn, hp, hc, hc, hn, tile, tile, _full((4, D)), _full((1, D)),
                  wspec, wspec, _full((2, D)), _full((2, D)), _full((2, D))],
        out_specs=[tile, wspec, wspec, vspec, vspec, vspec],
        out_shape=[SDS((s, D), BF16), SDS((2, HEADS, HD, HD), F32), SDS((2, HEADS, HD, HD), F32)]
        + [SDS((2 * LANE_ROWS, HD), F32)] * 3,
        compiler_params=_cp("arbitrary"))(z, z, z, h0, h0, h1, h1, g0, g1, cw, cb, wr, wi, br, bi, lam)


def _conv_bwd(dz, dxc, z, cw, layer, tm):
    s = z.shape[0]
    nt = s // tm

    def body(dz_in, dp_ref, dc_ref, dn_ref, zp_ref, zc_ref, zn_ref, cw_ref, dz_ref, dcw_ref, dcb_ref):
        del dz_in
        fp, fn = _halo_flags(nt)

        @pl.when(pl.program_id(0) == 0)
        def _():
            dcw_ref[...] = jnp.zeros_like(dcw_ref)
            dcb_ref[...] = jnp.zeros_like(dcb_ref)

        dxc_halo = _halo_load(dp_ref, dc_ref, dn_ref, fp, fn)
        dxc = dxc_halo[1]
        dm2, dm1, _, dp1, _ = _taps(*dxc_halo, tm)
        dz_ref[...] = (cw_ref[0:1, :] * dp1 + cw_ref[1:2, :] * dxc + cw_ref[2:3, :] * dm1
                       + cw_ref[3:4, :] * dm2).astype(BF16)
        _, zm1, z0, zp1, zp2 = _taps(*_halo_load(zp_ref, zc_ref, zn_ref, fp, fn), tm)
        for k, zt in enumerate((zm1, z0, zp1, zp2)):
            _add_rows128(dcw_ref, jnp.sum(dxc * zt, axis=0, keepdims=True), k * LANE_ROWS)
        _add_rows128(dcb_ref, jnp.sum(dxc, axis=0, keepdims=True))

    return pl.pallas_call(
        body, name=f"conv_bwd_l{layer}", grid=(nt,),
        in_specs=[pl.BlockSpec(memory_space=pl.ANY), *_halo_specs(tm, s, 0, 16), *_halo_specs(tm, s, 2, 16),
                  _full((4, D))],
        out_specs=[pl.BlockSpec((tm, D), lambda i: (i, 2)), _full((4 * LANE_ROWS, HD)), _full((LANE_ROWS, HD))],
        out_shape=[SDS((s, N_IN), BF16), SDS((4 * LANE_ROWS, HD), F32), SDS((LANE_ROWS, HD), F32)],
        input_output_aliases={0: 0},
        compiler_params=_cp("arbitrary"))(dz, dxc, dxc, dxc, z, z, z, cw)


def _gmlp_bwd(dz, z, dya, lng, lnb, ws, wst, bsb, layer, tm):
    s = z.shape[0]
    nt = s // tm

    def body(dz_in, zu_ref, zv_ref, dya_ref, lng_ref, lnb_ref, ws_ref, wst_ref, bsb_ref,
             dz_ref, dws_ref, dbs_ref, dlng_ref, dlnb_ref, du_s, dv_s, dbs_acc):
        del dz_in
        i = pl.program_id(0)

        @pl.when(i == 0)
        def _():
            for r in (dws_ref, dlng_ref, dlnb_ref, dbs_acc):
                r[...] = jnp.zeros_like(r)

        zu = zu_ref[...].astype(F32)
        zv = zv_ref[...].astype(F32)
        u, tu = _gelu(zu)
        gv, tv = _gelu(zv)
        xc = gv - jnp.mean(gv, axis=-1, keepdims=True)
        rstd = lax.rsqrt(jnp.mean(xc * xc, axis=-1, keepdims=True) + EPS)
        xh = xc * rstd
        lng_v = lng_ref[...]
        vb = (xh * lng_v + lnb_ref[...]).astype(BF16)
        dya = dya_ref[...].astype(F32)
        for c in range(tm // HD):
            rs = slice(c * HD, (c + 1) * HD)
            for g in range(HEADS):
                cs = slice(g * HD, (g + 1) * HD)
                vblk = vb[rs, cs]
                mixed = _dot(ws_ref[g], vblk) + bsb_ref[g]
                du_s[rs, cs] = dya[rs, cs] * mixed
                dmx = dya[rs, cs] * u[rs, cs]
                dbs_acc[g] += dmx
                dmxb = dmx.astype(BF16)
                dws_ref[g] += _dot_nt(dmxb, vblk)
                dv_s[rs, cs] = _dot(wst_ref[g], dmxb)
        dv = dv_s[...]
        _add_rows128(dlng_ref, jnp.sum(dv * xh, axis=0, keepdims=True))
        _add_rows128(dlnb_ref, jnp.sum(dv, axis=0, keepdims=True))
        dxh = dv * lng_v
        dgv = rstd * (dxh - jnp.mean(dxh, axis=-1, keepdims=True)
                      - xh * jnp.mean(dxh * xh, axis=-1, keepdims=True))
        dz_ref[:, 0:D] = (du_s[...] * _gelu_grad(zu, tu)).astype(BF16)
        dz_ref[:, D:2 * D] = (dgv * _gelu_grad(zv, tv)).astype(BF16)

        @pl.when(i == nt - 1)
        def _():
            for g in range(HEADS):
                dbs_ref[g:g + 1, :] = jnp.sum(dbs_acc[g].T, axis=0, keepdims=True)

    tile = pl.BlockSpec((tm, D), lambda i: (i, 0))
    wspec = _full((HEADS, HD, HD))
    return pl.pallas_call(
        body, name=f"gmlp_bwd_l{layer}", grid=(nt,),
        in_specs=[pl.BlockSpec(memory_space=pl.ANY), tile, pl.BlockSpec((tm, D), lambda i: (i, 1)), tile,
                  _full((1, D)), _full((1, D)), wspec, wspec, wspec],
        out_specs=[pl.BlockSpec((tm, 2 * D), lambda i: (i, 0)), wspec, _full((HEADS, HD)),
                   _full((LANE_ROWS, HD)), _full((LANE_ROWS, HD))],
        out_shape=[SDS((s, N_IN), BF16), SDS((HEADS, HD, HD), F32), SDS((HEADS, HD), F32),
                   SDS((LANE_ROWS, HD), F32), SDS((LANE_ROWS, HD), F32)],
        scratch_shapes=[pltpu.VMEM((tm, D), F32), pltpu.VMEM((tm, D), F32), pltpu.VMEM((HEADS, HD, HD), F32)],
        input_output_aliases={0: 0},
        compiler_params=_cp("arbitrary"))(dz, z, z, dya, lng, lnb, ws, wst, bsb)


def _me():
    return lax.axis_index("x"), lax.axis_index("y"), lax.axis_index("c")


def _peer(m):
    x, y, c = _me()
    px = 1 - x if m & 4 else x
    py = 1 - y if m & 2 else y
    pc = 1 - c if m & 1 else c
    return (px, py, pc), 4 * px + 2 * py + pc


_ANY = pl.BlockSpec(memory_space=pl.ANY)
_EXCHANGE_SEMS = [pltpu.SemaphoreType.DMA((N_DEV - 1,)), pltpu.SemaphoreType.DMA((N_DEV - 1,)), pltpu.SemaphoreType.DMA(())]


def _all_gather(v, after, name):
    def body(v_ref, after_ref, o_ref, send_sems, recv_sems, local_sem):
        del after_ref
        x, y, c = _me()
        me = 4 * x + 2 * y + c
        local = pltpu.make_async_copy(v_ref, o_ref.at[me], local_sem)
        local.start()
        sends = []
        for m in range(1, N_DEV):
            dev, _ = _peer(m)
            cp = pltpu.make_async_remote_copy(v_ref, o_ref.at[me], send_sems.at[m - 1], recv_sems.at[m - 1],
                                              device_id=dev, device_id_type=pl.DeviceIdType.MESH)
            cp.start()
            sends.append(cp)
        for m in range(1, N_DEV):
            dev, blk = _peer(m)
            pltpu.make_async_remote_copy(v_ref, o_ref.at[blk], send_sems.at[m - 1], recv_sems.at[m - 1],
                                         device_id=dev, device_id_type=pl.DeviceIdType.MESH).wait_recv()
        for cp in sends:
            cp.wait_send()
        local.wait()

    return pl.pallas_call(
        body, name=name, in_specs=[_ANY, _ANY], out_specs=_ANY,
        out_shape=SDS((N_DEV,) + v.shape, v.dtype), scratch_shapes=_EXCHANGE_SEMS)(v, after)


_HBM = pl.BlockSpec(memory_space=pltpu.HBM)
_SEM = pl.BlockSpec(memory_space=pltpu.SEMAPHORE)
_EFFECT = pltpu.CompilerParams(has_side_effects=pltpu.SideEffectType.DATAFLOW_SIDE_EFFECTING)
_PEER_SEMS = pltpu.SemaphoreType.DMA((N_DEV - 1,))


def _in_hbm(a):
    return pltpu.with_memory_space_constraint(a, pltpu.HBM)


def _remote(src, dst, send_sems, recv_sems, m):
    dev, _ = _peer(m)
    return pltpu.make_async_remote_copy(src, dst, send_sems.at[m - 1], recv_sems.at[m - 1],
                                        device_id=dev, device_id_type=pl.DeviceIdType.MESH)


def _gather_start(lands, after, name):
    n = len(lands)

    def body(*refs):
        land = refs[:n]
        sems = refs[n + 1:3 * n + 1]
        token = refs[-1]
        x, y, c = _me()
        me = 4 * x + 2 * y + c
        for t in range(n):
            for m in range(1, N_DEV):
                _remote(land[t].at[me], land[t].at[me], sems[2 * t], sems[2 * t + 1], m).start()
        token[...] = jnp.zeros_like(token)

    res = pl.pallas_call(
        body, name=name, in_specs=[_HBM] * n + [_ANY],
        out_specs=[_SEM] * (2 * n) + [_HBM] * n + [pl.BlockSpec(memory_space=pltpu.VMEM)],
        out_shape=[_PEER_SEMS] * (2 * n) + [pltpu.HBM(a.shape, a.dtype) for a in lands] + [SDS((8, 128), F32)],
        input_output_aliases={t: 2 * n + t for t in range(n)},
        compiler_params=_EFFECT)(*[_in_hbm(a) for a in lands], after)
    return [(res[2 * t], res[2 * t + 1], res[2 * n + t]) for t in range(n)], res[-1]


def _gather_wait(handle, after, name):
    send_sems, recv_sems, land = handle

    def body(land_ref, ssem, rsem, after_ref, out_ref):
        del after_ref, out_ref
        x, y, c = _me()
        me = 4 * x + 2 * y + c
        for m in range(1, N_DEV):
            _, blk = _peer(m)
            cp = _remote(land_ref.at[me], land_ref.at[blk], ssem, rsem, m)
            cp.wait_send()
            cp.wait_recv()

    return pl.pallas_call(
        body, name=name, in_specs=[_HBM, _SEM, _SEM, _ANY], out_specs=_HBM,
        out_shape=pltpu.HBM(land.shape, land.dtype), input_output_aliases={0: 0},
        compiler_params=_EFFECT)(land, send_sems, recv_sems, after)


def _exchange_start(ps, name):
    n = len(ps)

    def body(*refs):
        p = refs[:n]
        got = refs[n:2 * n]
        sems = refs[2 * n:5 * n]
        token = refs[-1]
        x, y, c = _me()
        me = 4 * x + 2 * y + c
        for t in range(n):
            pltpu.make_async_copy(p[t].at[me], got[t].at[me], sems[3 * t + 2]).start()
            for m in range(1, N_DEV):
                _, blk = _peer(m)
                _remote(p[t].at[blk], got[t].at[me], sems[3 * t], sems[3 * t + 1], m).start()
        token[...] = jnp.zeros_like(token)

    res = pl.pallas_call(
        body, name=name, in_specs=[_HBM] * (2 * n),
        out_specs=[_SEM] * (3 * n) + [_HBM] * (2 * n) + [pl.BlockSpec(memory_space=pltpu.VMEM)],
        out_shape=[_PEER_SEMS, _PEER_SEMS, pltpu.SemaphoreType.DMA(())] * n
        + [pltpu.HBM(a.shape, a.dtype) for a in ps] * 2 + [SDS((8, 128), F32)],
        input_output_aliases={t: 3 * n + t for t in range(2 * n)},
        compiler_params=_EFFECT)(*[_in_hbm(a) for a in ps], *[_in_hbm(lax.empty(a.shape, a.dtype)) for a in ps])
    return [(res[3 * t], res[3 * t + 1], res[3 * t + 2], res[3 * n + t], res[4 * n + t]) for t in range(n)], res[-1]


def _exchange_wait(handle, after, name):
    send_sems, recv_sems, local_sem, p, got = handle

    def body(p_ref, got_ref, ssem, rsem, lsem, after_ref, p_out, got_out):
        del after_ref, p_out, got_out
        x, y, c = _me()
        me = 4 * x + 2 * y + c
        pltpu.make_async_copy(p_ref.at[me], got_ref.at[me], lsem).wait()
        for m in range(1, N_DEV):
            _, blk = _peer(m)
            cp = _remote(p_ref.at[blk], got_ref.at[blk], ssem, rsem, m)
            cp.wait_send()
            cp.wait_recv()

    return pl.pallas_call(
        body, name=name, in_specs=[_HBM, _HBM, _SEM, _SEM, _SEM, _ANY], out_specs=[_HBM, _HBM],
        out_shape=[pltpu.HBM(p.shape, p.dtype), pltpu.HBM(got.shape, got.dtype)],
        input_output_aliases={0: 0, 1: 1}, compiler_params=_EFFECT)(p, got, send_sems, recv_sems, local_sem, after)[1]


def _cast_into_slot(w, layer, me1, name):
    _, r, c = w.shape
    tr = next(t for t in (256, 352, r) if r % t == 0)

    def body(me_ref, w_ref, o_ref):
        del me_ref
        o_ref[...] = w_ref[...].astype(BF16)

    return pl.pallas_call(
        body, name=name,
        grid_spec=pltpu.PrefetchScalarGridSpec(
            num_scalar_prefetch=1, grid=(r // tr,),
            in_specs=[pl.BlockSpec((None, tr, c), lambda i, me: (layer, i, 0))],
            out_specs=pl.BlockSpec((None, tr, c), lambda i, me: (me[0], i, 0))),
        out_shape=SDS((N_DEV, r, c), BF16), compiler_params=_cp("arbitrary"))(me1, w)


def _sum8_into_slot(p, me1, name):
    _, r, c = p.shape

    def body(me_ref, p_ref, o_ref):
        del me_ref
        acc = p_ref[0]
        for k in range(1, N_DEV):
            acc = acc + p_ref[k]
        o_ref[...] = acc

    return pl.pallas_call(
        body, name=name,
        grid_spec=pltpu.PrefetchScalarGridSpec(
            num_scalar_prefetch=1, grid=(1,),
            in_specs=[pl.BlockSpec(p.shape, lambda i, me: (0, 0, 0))],
            out_specs=pl.BlockSpec((None, r, c), lambda i, me: (me[0], 0, 0))),
        out_shape=SDS(p.shape, F32), compiler_params=_cp("arbitrary"))(me1, p)


def _adamw(w, g, m, v):
    m = ADAM_B1 * m + (1.0 - ADAM_B1) * g
    v = ADAM_B2 * v + (1.0 - ADAM_B2) * (g * g)
    m_hat = m / (1.0 - ADAM_B1 ** ADAM_STEP)
    v_hat = v / (1.0 - ADAM_B2 ** ADAM_STEP)
    delta = -ADAM_LR * (m_hat / (jnp.sqrt(v_hat) + ADAM_EPS) + ADAM_WD * w)
    return delta, m, v


def _adam_shard(parts, w, m, v, layer, prev, name):
    _, r, c = parts.shape
    tr = next(t for t in (256, 352, r) if r % t == 0)
    n_prev = 0 if prev is None else 4

    def body(*refs):
        p_ref, w_ref, m_ref, v_ref = refs[:4]
        g_ref, d_ref, nm_ref, nv_ref = refs[4 + n_prev:]
        g = p_ref[0].astype(F32)
        for k in range(1, N_DEV):
            g = g + p_ref[k].astype(F32)
        delta, nm, nv = _adamw(w_ref[...], g, m_ref[...], v_ref[...])
        g_ref[...] = g
        d_ref[...] = delta
        nm_ref[...] = nm
        nv_ref[...] = nv

    blk = pl.BlockSpec((None, tr, c), lambda i: (layer, i, 0))
    return pl.pallas_call(
        body, name=name, grid=(r // tr,),
        in_specs=[pl.BlockSpec((N_DEV, tr, c), lambda i: (0, i, 0)), blk, blk, blk] + [_ANY] * n_prev,
        out_specs=[blk] * 4, out_shape=[SDS(w.shape, F32)] * 4,
        input_output_aliases={4 + k: k for k in range(n_prev)},
        compiler_params=_cp("parallel"))(parts, w, m, v, *(prev or ()))


SMALL_MATRICES = [("lru_w_r", 2048), ("lru_w_i", 2048), ("gmlp_w_s", 1024)]
SMALL_VECTORS = [("norm1_g", 8), ("gmlp_ln_g", 8), ("gmlp_ln_b", 8), ("gmlp_b_s", 8), ("conv_w", 32), ("conv_b", 8),
                 ("lru_b_r", 16), ("lru_b_i", 16), ("lru_lambda", 16), ("norm2_g", 8), ("final_g", 8)]
SMALL_VECTOR_ROW0 = sum(n for _, n in SMALL_MATRICES)
SMALL_VECTOR_BLOCK = 256
SMALL_ROWS = SMALL_VECTOR_ROW0 + SMALL_VECTOR_BLOCK


def _pack_small(small):
    parts = [small[k] for k, _ in SMALL_MATRICES]
    parts += [small[k] if k in small else jnp.zeros((n, HD), F32) for k, n in SMALL_VECTORS]
    flat = jnp.concatenate(parts)
    return jnp.pad(flat, ((0, SMALL_ROWS - flat.shape[0]), (0, 0))).reshape(N_DEV, SMALL_ROWS // N_DEV, HD)


def _adam_matrix(g0, g1, w, m, v, row0, name):
    _, rows, _ = w.shape

    def body(g0_ref, g1_ref, w_ref, m_ref, v_ref, g_ref, d_ref, nm_ref, nv_ref):
        for l, src in enumerate((g0_ref, g1_ref)):
            g = src[...]
            delta, nm, nv = _adamw(w_ref[l], g, m_ref[l], v_ref[l])
            g_ref[l] = g
            d_ref[l] = delta
            nm_ref[l] = nm
            nv_ref[l] = nv

    gspec = pl.BlockSpec((rows, HD), lambda i: (row0 // rows, 0))
    return pl.pallas_call(body, name=name, grid=(1,), in_specs=[gspec, gspec] + [_full(w.shape)] * 3,
                          out_specs=[_full(w.shape)] * 4, out_shape=[SDS(w.shape, F32)] * 4,
                          compiler_params=_cp("arbitrary"))(g0, g1, w, m, v)


def _adam_vectors(g0, g1, dg1_parts, me1, ws, ms, vs):
    names = [k for k, _ in SMALL_VECTORS]
    n = len(names)

    def lanes(rows8):
        return jnp.concatenate([rows8[k:k + 1, :] for k in range(LANE_ROWS)], axis=1)

    def body(me_ref, g0_ref, g1_ref, dg1_ref, *refs):
        w_refs, m_refs, v_refs = refs[:n], refs[n:2 * n], refs[2 * n:3 * n]
        outs = refs[3 * n:]
        me = me_ref[0]
        g_refs = (g0_ref, g1_ref)

        def emit(i, idx, g):
            delta, nm, nv = _adamw(w_refs[i][idx], g, m_refs[i][idx], v_refs[i][idx])
            for j, val in enumerate((g, delta, nm, nv)):
                outs[4 * i + j][idx] = val

        off = 0
        for i, (name, rows) in enumerate(SMALL_VECTORS):
            for l in range(2):
                row = (slice(l, l + 1), slice(None))
                if name == "final_g":
                    if l == 1:
                        emit(i, (slice(0, 1), slice(None)), lanes(g1_ref[off:off + rows, :]))
                elif name == "norm1_g":
                    if l == 1:
                        emit(i, row, lanes(g0_ref[off:off + rows, :]))
                    else:
                        total = dg1_ref[0]
                        for k in range(1, N_DEV):
                            total = total + dg1_ref[k]
                        emit(i, row, lanes(total))
                elif name == "gmlp_b_s":
                    emit(i, (l,), g_refs[l][off:off + rows, :])
                elif rows == LANE_ROWS:
                    emit(i, row, lanes(g_refs[l][off:off + rows, :]))
                else:
                    for r in range(rows // LANE_ROWS):
                        emit(i, (l, slice(r, r + 1), slice(None)), g_refs[l][pl.ds(off + r * LANE_ROWS + me, 1), :])
            off += rows

    args = [ws[k] for k in names] + [ms[k] for k in names] + [vs[k] for k in names]
    gspec = pl.BlockSpec((SMALL_VECTOR_BLOCK, HD), lambda i, me: (SMALL_VECTOR_ROW0 // SMALL_VECTOR_BLOCK, 0))
    res = pl.pallas_call(
        body, name="adam_vectors",
        grid_spec=pltpu.PrefetchScalarGridSpec(
            num_scalar_prefetch=1, grid=(1,),
            in_specs=[gspec, gspec, _full(dg1_parts.shape)] + [_full(a.shape) for a in args],
            out_specs=[_full(ws[k].shape) for k in names for _ in range(4)]),
        out_shape=[SDS(ws[k].shape, F32) for k in names for _ in range(4)],
        compiler_params=_cp("arbitrary"))(me1, g0, g1, dg1_parts, *args)
    return {k: list(res[4 * i:4 * i + 4]) for i, k in enumerate(names)}


def _after(a, *tokens):
    for token in tokens:
        if token is not None:
            a = a + token[0:1, 0:1]
    return a


def _local_step(x, tgt, p, get_w, hook=lambda stage, layer, payload: None):
    s = x.shape[0]
    tm = _row_tile(s)
    wsb = p["gmlp_w_s"].astype(BF16)
    wstb = jnp.swapaxes(p["gmlp_w_s"], -1, -2).astype(BF16)
    bsb = jnp.broadcast_to(p["gmlp_b_s"][..., None], p["gmlp_w_s"].shape)
    wrb = p["lru_w_r"].astype(BF16)
    wib = p["lru_w_i"].astype(BF16)
    saved = []
    for l in range(2):
        win = get_w("w_in", l, x)
        z, h1 = _norm_inproj(x, _after(p["norm1_g"][l][None], hook("pre_inproj", l, win)), win, l, tm)
        ya = _gmlp_fwd(z, p["gmlp_ln_g"][l][None], p["gmlp_ln_b"][l][None], wsb[l], bsb[l], l, tm)
        a0, b0, a1, b1 = _lru_gates_fwd(z, p["conv_w"][l], p["conv_b"][l][None], wrb[l], wib[l],
                                        p["lru_b_r"][l], p["lru_b_i"][l], p["lru_lambda"][l], l, tm)
        h0, hr = _lru_scan(a0, b0, a1, b1, False, l)
        wout = get_w("w_out", l, h0)
        x1, mg = _merge_outproj(x, ya, h0, hr, z, wout, l, tm)
        wfi = get_w("w_ffn_in", l, x1)
        wfo = get_w("w_ffn_out", l, x1)
        x2, gu, h2 = _ffn_fwd(x1, _after(p["norm2_g"][l][None], hook("pre_ffn", l, wfi)), wfi, wfo, l, tm)
        saved.append((x, z, h1, ya, a0, a1, h0, hr, x1, mg, gu, h2, win, wout, wfi, wfo))
        x = x2
    dx, loss, dfg = _loss_head(x, p["final_g"][None], tgt, tm)
    pending = None
    for l in (1, 0):
        x0, z, h1, ya, a0, a1, h0, hr, x1, mg, gu, h2, win, wout, wfi, wfo = saved[l]
        ff, dgu = _ffn_bwd_act(dx, wfo, gu, l, tm)
        d_wfo = _mm_tn(ff, pl.BlockSpec((None, s, FF_BLK), lambda j: (j, 0, 0)), dx, _resident((s, D)),
                       4, (4, FF_BLK, D), pl.BlockSpec((None, FF_BLK, D), lambda j: (j, 0, 0)),
                       f"dw_ffn_out_l{l}", a_is_transposed=False)
        dgu8 = dgu.reshape(N_DEV, s, FF_BLK)
        d_wfi = _mm_tn(dgu8, pl.BlockSpec((None, s, FF_BLK), lambda j: (j, 0, 0)), h2, _resident((s, D)),
                       N_DEV, (N_DEV, FF_BLK, D), pl.BlockSpec((None, FF_BLK, D), lambda j: (j, 0, 0)),
                       f"dw_ffn_in_l{l}", a_is_transposed=False)
        token = hook("ffn_partials", l, dict(w_ffn_out=d_wfo.reshape(N_DEV, D_FF // N_DEV, D), w_ffn_in=d_wfi))
        dx1, dg2 = _mm_nt_rms_bwd(
            dgu8, pl.BlockSpec((N_DEV, tm, FF_BLK), lambda i: (0, i, 0)), lambda r: [r[k] for k in range(N_DEV)],
            wfi.reshape(N_DEV, FF_BLK, D), True, x1, _after(p["norm2_g"][l][None], token, pending), dx,
            f"ffn_bwd_dx_l{l}", tm)
        pending = hook("mid_backward", l, dx1)
        dz, dya, dh = _outproj_bwd_merge(dx1, wout, ya, h0, hr, z, l, tm)
        d_wout = _mm_tn(mg, _resident((D, s)), dx1, pl.BlockSpec((s, D // 2), lambda j: (0, j)),
                        2, (D, D), pl.BlockSpec((D, D // 2), lambda j: (0, j)), f"dw_out_l{l}")
        g1, g0 = _lru_scan(a1, dh, a0, dh, True, l)
        dxc, dwr, dwi, dbr, dbi, dlam = _lru_gates_bwd(
            z, h0, hr, g0, g1, p["conv_w"][l], _after(p["conv_b"][l][None], pending), wrb[l], wib[l],
            p["lru_b_r"][l], p["lru_b_i"][l], p["lru_lambda"][l], l, tm)
        dz, dcw, dcb = _conv_bwd(dz, dxc, z, p["conv_w"][l], l, tm)
        dz, dws, dbs, dlng, dlnb = _gmlp_bwd(dz, z, dya, p["gmlp_ln_g"][l][None], p["gmlp_ln_b"][l][None],
                                             wsb[l], wstb[l], bsb[l], l, tm)
        small = dict(lru_w_r=dwr.reshape(-1, HD), lru_w_i=dwi.reshape(-1, HD), gmlp_w_s=dws.reshape(-1, HD),
                     gmlp_ln_g=dlng, gmlp_ln_b=dlnb, gmlp_b_s=dbs, conv_w=dcw, conv_b=dcb, lru_b_r=dbr,
                     lru_b_i=dbi, lru_lambda=dlam, norm2_g=dg2)
        if l == 1:
            small["final_g"] = dfg
        else:
            small["norm1_g"] = dg1
        started = hook("small_grads", l, small)
        d_win = _mm_tn(h1, _resident((D, s)), dz, pl.BlockSpec((s, IN_BLK), lambda j: (0, j)),
                       N_DEV, (N_DEV, D, IN_BLK), pl.BlockSpec((None, D, IN_BLK), lambda j: (j, 0, 0)),
                       f"dw_in_l{l}", after=started)
        token = hook("mixer_partials", l, dict(w_out=d_wout.reshape(N_DEV, D // N_DEV, D), w_in=d_win))
        dx, dg1 = _mm_nt_rms_bwd(
            dz, pl.BlockSpec((tm, N_IN), lambda i: (i, 0)),
            lambda r: [r[:, k * IN_BLK:(k + 1) * IN_BLK] for k in range(N_DEV)],
            win, False, x0, _after(p["norm1_g"][l][None], token, started, pending), dx1, f"inproj_bwd_dx_l{l}", tm)
        pending = None
    return loss, dx, dg1


_REPL = ["norm1_g", "gmlp_ln_g", "gmlp_ln_b", "gmlp_w_s", "gmlp_b_s", "conv_b", "lru_w_r", "lru_w_i", "norm2_g", "final_g"]
_LANE_SHARDED = ["conv_w", "lru_b_r", "lru_b_i", "lru_lambda"]
_BIG = ["w_in", "w_out", "w_ffn_in", "w_ffn_out"]
_ORDER = ["norm1_g", "w_in", "gmlp_ln_g", "gmlp_ln_b", "gmlp_w_s", "gmlp_b_s", "conv_w", "conv_b", "lru_w_r", "lru_b_r",
          "lru_w_i", "lru_b_i", "lru_lambda", "w_out", "norm2_g", "w_ffn_in", "w_ffn_out", "final_g"]


def kernel(x, norm1_g, w_in, gmlp_ln_g, gmlp_ln_b, gmlp_w_s, gmlp_b_s, conv_w, conv_b, lru_w_r, lru_b_r, lru_w_i, lru_b_i, lru_lambda, w_out, norm2_g, w_ffn_in, w_ffn_out, final_g, loss_target, m_norm1_g, m_w_in, m_gmlp_ln_g, m_gmlp_ln_b, m_gmlp_w_s, m_gmlp_b_s, m_conv_w, m_conv_b, m_lru_w_r, m_lru_b_r, m_lru_w_i, m_lru_b_i, m_lru_lambda, m_w_out, m_norm2_g, m_w_ffn_in, m_w_ffn_out, m_final_g, v_norm1_g, v_w_in, v_gmlp_ln_g, v_gmlp_ln_b, v_gmlp_w_s, v_gmlp_b_s, v_conv_w, v_conv_b, v_lru_w_r, v_lru_b_r, v_lru_w_i, v_lru_b_i, v_lru_lambda, v_w_out, v_norm2_g, v_w_ffn_in, v_w_ffn_out, v_final_g):
    w = dict(norm1_g=norm1_g, w_in=w_in, gmlp_ln_g=gmlp_ln_g, gmlp_ln_b=gmlp_ln_b, gmlp_w_s=gmlp_w_s, gmlp_b_s=gmlp_b_s,
             conv_w=conv_w, conv_b=conv_b, lru_w_r=lru_w_r, lru_b_r=lru_b_r, lru_w_i=lru_w_i, lru_b_i=lru_b_i,
             lru_lambda=lru_lambda, w_out=w_out, norm2_g=norm2_g, w_ffn_in=w_ffn_in, w_ffn_out=w_ffn_out, final_g=final_g)
    mom = dict(norm1_g=m_norm1_g, w_in=m_w_in, gmlp_ln_g=m_gmlp_ln_g, gmlp_ln_b=m_gmlp_ln_b, gmlp_w_s=m_gmlp_w_s,
               gmlp_b_s=m_gmlp_b_s, conv_w=m_conv_w, conv_b=m_conv_b, lru_w_r=m_lru_w_r, lru_b_r=m_lru_b_r,
               lru_w_i=m_lru_w_i, lru_b_i=m_lru_b_i, lru_lambda=m_lru_lambda, w_out=m_w_out, norm2_g=m_norm2_g,
               w_ffn_in=m_w_ffn_in, w_ffn_out=m_w_ffn_out, final_g=m_final_g)
    var = dict(norm1_g=v_norm1_g, w_in=v_w_in, gmlp_ln_g=v_gmlp_ln_g, gmlp_ln_b=v_gmlp_ln_b, gmlp_w_s=v_gmlp_w_s,
               gmlp_b_s=v_gmlp_b_s, conv_w=v_conv_w, conv_b=v_conv_b, lru_w_r=v_lru_w_r, lru_b_r=v_lru_b_r,
               lru_w_i=v_lru_w_i, lru_b_i=v_lru_b_i, lru_lambda=v_lru_lambda, w_out=v_w_out, norm2_g=v_norm2_g,
               w_ffn_in=v_w_ffn_in, w_ffn_out=v_w_ffn_out, final_g=v_final_g)
    for src in (w, mom, var):
        src["w_ffn_in"] = jnp.swapaxes(src["w_ffn_in"], 1, 2)
    xi, yi, ci = _me()
    me = 4 * xi + 2 * yi + ci

    lane_shapes = [w[k].shape for k in _LANE_SHARDED]
    lane_rows = sum(a[0] * a[1] for a in lane_shapes)
    packed = jnp.concatenate([w[k].reshape(-1, HD) for k in _LANE_SHARDED])
    packed = jnp.pad(packed, ((0, -lane_rows % 8), (0, 0)))
    lanes = _all_gather(packed, packed, "gather_small")
    params = {k: w[k] for k in _REPL}
    off = 0
    for k, shp in zip(_LANE_SHARDED, lane_shapes):
        n = shp[0] * shp[1]
        params[k] = jnp.swapaxes(lanes[:, off:off + n], 0, 1).reshape(shp[0], shp[1], D)
        off += n

    me1 = jnp.reshape(me, (1,)).astype(jnp.int32)
    gathers = {}
    exchanges = {}
    views = dict(w_in=(N_DEV, D, IN_BLK), w_out=(D, D), w_ffn_in=(2, 4, FF_BLK, D), w_ffn_out=(4, FF_BLK, D))
    small_ex = {}
    small_ag = {}

    def start_gather(names, l, after):
        lands = [_cast_into_slot(w[k], l, me1, f"cast_{k}_l{l}") for k in names]
        started, tok = _gather_start(lands, after, f"gather_start_{'_'.join(names)}_l{l}")
        gathers.update({(k, l): h for k, h in zip(names, started)})
        return tok

    def get_w(k, l, after):
        return _gather_wait(gathers[(k, l)], after, f"gather_wait_{k}_l{l}").reshape(views[k])

    def hook(stage, l, payload):
        if stage == "pre_inproj":
            return start_gather(_BIG[1:], l, payload)
        if stage == "pre_ffn":
            return start_gather(_BIG[:1], l + 1, payload) if l == 0 else None
        if stage == "small_grads":
            (small_ex[l],), tok = _exchange_start([_pack_small(payload)], f"exchange_start_small_l{l}")
            return tok
        if stage == "mid_backward":
            return reduce_small(l + 1, payload) if l == 0 else None
        extra = reduce_small(0, payload["w_in"]) if (stage, l) == ("mixer_partials", 0) else None
        started, tok = _exchange_start(list(payload.values()), f"exchange_start_{'_'.join(payload)}_l{l}")
        exchanges.update({(k, l): h for k, h in zip(payload, started)})
        return tok if extra is None else tok + extra

    def reduce_small(l, after):
        got = _exchange_wait(small_ex[l], after, f"exchange_wait_small_l{l}")
        mine = _sum8_into_slot(got, me1, f"sum_small_l{l}")
        (small_ag[l],), tok = _gather_start([mine], got, f"gather_start_small_l{l}")
        return tok

    start_gather(_BIG[:1], 0, lanes)
    loss, dx, dg1 = _local_step(x[0], loss_target[0], params, get_w, hook)

    out = {}
    after = dx
    for k, l in [(k, l) for k in ("w_ffn_out", "w_ffn_in") for l in (1, 0)] + [("w_out", 1), ("w_in", 1)]:
        got = _exchange_wait(exchanges[(k, l)], after, f"exchange_wait_{k}_l{l}")
        out[k] = _adam_shard(got, w[k], mom[k], var[k], l, out.get(k), f"adam_{k}_l{l}")
        after = out[k][3]
    g_small = [_gather_wait(small_ag[l], after, f"gather_wait_small_l{l}").reshape(SMALL_ROWS, HD) for l in (0, 1)]
    row0 = 0
    for k, rows in SMALL_MATRICES:
        res = _adam_matrix(*g_small, *[src[k].reshape(2, rows, HD) for src in (w, mom, var)], row0, f"adam_{k}")
        out[k] = [a.reshape(w[k].shape) for a in res]
        after = res[3]
        row0 += rows
    for k in ("w_out", "w_in"):
        got = _exchange_wait(exchanges[(k, 0)], after, f"exchange_wait_{k}_l0")
        out[k] = _adam_shard(got, w[k], mom[k], var[k], 0, out[k], f"adam_{k}_l0")
    out["w_ffn_in"] = [jnp.swapaxes(a, 1, 2) for a in out["w_ffn_in"]]
    as_rows = lambda a: a.reshape(1, D) if a.ndim == 1 else a
    vec = _adam_vectors(*g_small, _all_gather(dg1, out["w_in"][3], "gather_norm1_grad"), me1,
                        *[{k: as_rows(src[k]) for k, _ in SMALL_VECTORS} for src in (w, mom, var)])
    out.update({k: [a.reshape(w[k].shape) for a in res] for k, res in vec.items()})

    loss = lax.psum(loss[0, 0], MESH_AXES)
    return (loss, dx[None], *[out[k][0] for k in _ORDER], *[out[k][1] for k in _ORDER],
            *[out[k][2] for k in _ORDER], *[out[k][3] for k in _ORDER])
```

```python
import jax
import jax.numpy as jnp
from jax import lax
from jax.experimental import pallas as pl
from jax.experimental.pallas import tpu as pltpu

F32 = jnp.float32
BF16 = jnp.bfloat16
SDS = jax.ShapeDtypeStruct

D = 1024
N_IN = 6 * D
D_FF = 2816
N_DEV = 8
IN_BLK = N_IN // N_DEV
FF_BLK = 2 * D_FF // N_DEV
HEADS = 8
HD = 128
EPS = 1e-6
LRU_C = 8.0
MESH_AXES = ("x", "y", "c")

ADAM_LR = 0.001
ADAM_B1 = 0.9
ADAM_B2 = 0.999
ADAM_EPS = 1e-08
ADAM_WD = 0.01
ADAM_STEP = 10

VMEM_LIMIT = 56 * 2**20


def _cp(*sem, **kw):
    return pltpu.CompilerParams(dimension_semantics=sem, vmem_limit_bytes=VMEM_LIMIT, **kw)


def _row_tile(s):
    return 512 if s >= 1024 else s // 2


_GELU_C = 0.7978845608028654


def _gelu(x):
    t = jnp.tanh(_GELU_C * (x + 0.044715 * (x * x * x)))
    return 0.5 * x * (1.0 + t), t


def _gelu_grad(x, t):
    return 0.5 * (1.0 + t) + 0.5 * x * (1.0 - t * t) * (_GELU_C * (1.0 + 0.134145 * (x * x)))


def _sigmoid(x):
    return 0.5 + 0.5 * jnp.tanh(0.5 * x)


def _softplus(x):
    e = jnp.exp(-jnp.abs(x))
    w = 1.0 + e
    l1p = jnp.where(w == 1.0, e, jnp.log(w) * e / jnp.where(w == 1.0, 1.0, w - 1.0))
    return jnp.maximum(x, 0.0) + l1p


def _rms_fwd(x, g):
    r = lax.rsqrt(jnp.mean(x * x, axis=-1, keepdims=True) + EPS)
    return x * r * g


def _rms_bwd(x, g, dh):
    r = lax.rsqrt(jnp.mean(x * x, axis=-1, keepdims=True) + EPS)
    xh = x * r
    dxh = dh * g
    dx = r * (dxh - xh * jnp.mean(dxh * xh, axis=-1, keepdims=True))
    dg = jnp.sum(dh * xh, axis=0, keepdims=True)
    return dx, dg


LANE_ROWS = D // HD


def _add_rows128(ref, vec, row0=0):
    for i in range(vec.shape[0]):
        for k in range(LANE_ROWS):
            j = row0 + i * LANE_ROWS + k
            ref[j:j + 1, :] += vec[i:i + 1, k * HD:(k + 1) * HD]


def _dot(a, b):
    return jnp.dot(a, b, preferred_element_type=F32)


def _dot_nt(a, b):
    return lax.dot_general(a, b, (((1,), (1,)), ((), ())), preferred_element_type=F32)


def _dot_tn(a, b):
    return lax.dot_general(a, b, (((0,), (0,)), ((), ())), preferred_element_type=F32)


def _taps(prev, cur, nxt, tm):
    hr = prev.shape[0]
    ext = jnp.concatenate([prev, cur, nxt], axis=0)
    n = tm + 2 * hr
    sl = slice(hr, hr + tm)
    return (pltpu.roll(ext, 2, 0)[sl], pltpu.roll(ext, 1, 0)[sl], cur,
            pltpu.roll(ext, n - 1, 0)[sl], pltpu.roll(ext, n - 2, 0)[sl])


def _halo_specs(tm, s, col, rows=8):
    nb = s // rows
    r = tm // rows
    return (pl.BlockSpec((rows, D), lambda i: (jnp.maximum(i * r - 1, 0), col)),
            pl.BlockSpec((tm, D), lambda i: (i, col)),
            pl.BlockSpec((rows, D), lambda i: (jnp.minimum((i + 1) * r, nb - 1), col)))


def _halo_load(prev_ref, cur_ref, next_ref, fp, fn):
    return prev_ref[...].astype(F32) * fp, cur_ref[...].astype(F32), next_ref[...].astype(F32) * fn


def _halo_flags(nt):
    i = pl.program_id(0)
    return (i > 0).astype(F32), (i < nt - 1).astype(F32)


def _full(shape):
    nd = len(shape)
    return pl.BlockSpec(shape, lambda *_: (0,) * nd)


def _resident(shape):
    nd = len(shape)
    return pl.BlockSpec(shape, lambda *_: (0,) * nd, pipeline_mode=pl.Buffered(1))


def _norm_inproj(x, g, w, layer, tm):
    s = x.shape[0]

    def body(x_ref, g_ref, w_ref, z_ref, ht_ref):
        h32 = _rms_fwd(x_ref[...], g_ref[...])
        ht_ref[...] = h32.T.astype(BF16)
        h = h32.astype(BF16)
        for j in range(N_DEV):
            z_ref[:, j * IN_BLK:(j + 1) * IN_BLK] = _dot(h, w_ref[j]).astype(BF16)

    return pl.pallas_call(
        body, name=f"norm_inproj_l{layer}", grid=(s // tm,),
        in_specs=[pl.BlockSpec((tm, D), lambda i: (i, 0)), _full((1, D)), _resident((N_DEV, D, IN_BLK))],
        out_specs=[pl.BlockSpec((tm, N_IN), lambda i: (i, 0)), pl.BlockSpec((D, tm), lambda i: (0, i))],
        out_shape=[SDS((s, N_IN), BF16), SDS((D, s), BF16)],
        compiler_params=_cp("parallel"))(x, g, w)


def _gmlp_fwd(z, lng, lnb, ws, bsb, layer, tm):
    s = z.shape[0]

    def body(zu_ref, zv_ref, lng_ref, lnb_ref, ws_ref, bsb_ref, ya_ref):
        u, _ = _gelu(zu_ref[...].astype(F32))
        gv, _ = _gelu(zv_ref[...].astype(F32))
        xc = gv - jnp.mean(gv, axis=-1, keepdims=True)
        rstd = lax.rsqrt(jnp.mean(xc * xc, axis=-1, keepdims=True) + EPS)
        vb = (xc * rstd * lng_ref[...] + lnb_ref[...]).astype(BF16)
        for c in range(tm // HD):
            rs = slice(c * HD, (c + 1) * HD)
            for g in range(HEADS):
                cs = slice(g * HD, (g + 1) * HD)
                mixed = _dot(ws_ref[g], vb[rs, cs]) + bsb_ref[g]
                ya_ref[rs, cs] = (u[rs, cs] * mixed).astype(BF16)

    return pl.pallas_call(
        body, name=f"gmlp_fwd_l{layer}", grid=(s // tm,),
        in_specs=[pl.BlockSpec((tm, D), lambda i: (i, 0)), pl.BlockSpec((tm, D), lambda i: (i, 1)),
                  _full((1, D)), _full((1, D)), _full((HEADS, HD, HD)), _full((HEADS, HD, HD))],
        out_specs=pl.BlockSpec((tm, D), lambda i: (i, 0)),
        out_shape=SDS((s, D), BF16),
        compiler_params=_cp("parallel"))(z, z, lng, lnb, ws, bsb)


def _conv(taps, cw_ref, cb_ref):
    _, m1, c0, p1, p2 = taps
    return cb_ref[...] + m1 * cw_ref[0:1, :] + c0 * cw_ref[1:2, :] + p1 * cw_ref[2:3, :] + p2 * cw_ref[3:4, :]


def _heads_dot(xb, w_ref, d):
    return jnp.concatenate([_dot(xb[:, h * HD:(h + 1) * HD], w_ref[d, h]) for h in range(HEADS)], axis=1)


def _lru_gates(xc, xb, wr_ref, wi_ref, br_ref, bi_ref, lam_ref, d):
    sp = _softplus(-lam_ref[d:d + 1, :])
    r = _sigmoid(_heads_dot(xb, wr_ref, d) + br_ref[d:d + 1, :])
    ig = _sigmoid(_heads_dot(xb, wi_ref, d) + bi_ref[d:d + 1, :])
    la = (-LRU_C) * r * sp
    a = jnp.exp(la)
    q = jnp.maximum(jnp.tanh(-la) * (a * a + 1.0), 0.0)
    rq = jnp.where(q > 0.0, lax.rsqrt(jnp.where(q > 0.0, q, 1.0)), 0.0)
    return sp, r, ig, a, q * rq, rq


def _lru_gates_fwd(z, cw, cb, wr, wi, br, bi, lam, layer, tm):
    s = z.shape[0]
    nt = s // tm

    def body(zp_ref, zc_ref, zn_ref, cw_ref, cb_ref, wr_ref, wi_ref, br_ref, bi_ref, lam_ref,
             a0_ref, b0_ref, a1_ref, b1_ref):
        fp, fn = _halo_flags(nt)
        xc = _conv(_taps(*_halo_load(zp_ref, zc_ref, zn_ref, fp, fn), tm), cw_ref, cb_ref)
        xb = xc.astype(BF16)
        for d, (a_ref, b_ref) in enumerate(((a0_ref, b0_ref), (a1_ref, b1_ref))):
            _, _, ig, a, mult, _ = _lru_gates(xc, xb, wr_ref, wi_ref, br_ref, bi_ref, lam_ref, d)
            a_ref[...] = a
            b_ref[...] = mult * (ig * xc)

    tile = pl.BlockSpec((tm, D), lambda i: (i, 0))
    return pl.pallas_call(
        body, name=f"lru_gates_fwd_l{layer}", grid=(nt,),
        in_specs=[*_halo_specs(tm, s, 2, 16), _full((4, D)), _full((1, D)),
                  _full((2, HEADS, HD, HD)), _full((2, HEADS, HD, HD)), _full((2, D)), _full((2, D)), _full((2, D))],
        out_specs=[tile] * 4, out_shape=[SDS((s, D), F32)] * 4,
        compiler_params=_cp("parallel"))(z, z, z, cw, cb, wr, wi, br, bi, lam)


def _scan_group(a, x, c, reverse, bwd):
    row = lax.broadcasted_iota(jnp.int32, a.shape, 0)
    b = a * x if bwd else x
    for d in (1, 2, 4):
        keep = (row < 8 - d) if reverse else (row >= d)
        sh = 8 - d if reverse else d
        a_s = jnp.where(keep, pltpu.roll(a, sh, 0), 1.0)
        b_s = jnp.where(keep, pltpu.roll(b, sh, 0), 0.0)
        b = a * b_s + b
        a = a * a_s
    h = b + a * c
    new_c = h[0:1, :] if reverse else h[7:8, :]
    if not bwd:
        return h, new_c
    if reverse:
        prev = jnp.where(row < 7, pltpu.roll(h, 7, 0), c)
    else:
        prev = jnp.where(row >= 1, pltpu.roll(h, 1, 0), c)
    return x + prev, new_c


def _lru_scan(a_f, x_f, a_r, x_r, bwd, layer):
    s = a_f.shape[0]
    ts = min(1024, s // 2)
    cb = 512
    nt = s // ts
    ng = ts // 8

    def body(af_ref, xf_ref, ar_ref, xr_ref, of_ref, or_ref, cf, cr):
        @pl.when(pl.program_id(1) == 0)
        def _():
            cf[...] = jnp.zeros_like(cf)
            cr[...] = jnp.zeros_like(cr)

        def step(j, carry):
            c_f, c_r = carry
            rf = pl.multiple_of(j * 8, 8)
            rr = pl.multiple_of((ng - 1 - j) * 8, 8)
            o, c_f = _scan_group(af_ref[pl.ds(rf, 8), :], xf_ref[pl.ds(rf, 8), :], c_f, False, bwd)
            of_ref[pl.ds(rf, 8), :] = o
            o, c_r = _scan_group(ar_ref[pl.ds(rr, 8), :], xr_ref[pl.ds(rr, 8), :], c_r, True, bwd)
            or_ref[pl.ds(rr, 8), :] = o
            return c_f, c_r

        c_f, c_r = lax.fori_loop(0, ng, step, (cf[0:1, :], cr[0:1, :]), unroll=2)
        cf[...] = jnp.broadcast_to(c_f, cf.shape)
        cr[...] = jnp.broadcast_to(c_r, cr.shape)

    fwd = pl.BlockSpec((ts, cb), lambda c, t: (t, c))
    rev = pl.BlockSpec((ts, cb), lambda c, t: (nt - 1 - t, c))
    return pl.pallas_call(
        body, name=f"lru_scan_{'bwd' if bwd else 'fwd'}_l{layer}", grid=(D // cb, nt),
        in_specs=[fwd, fwd, rev, rev], out_specs=[fwd, rev],
        out_shape=[SDS((s, D), F32)] * 2,
        scratch_shapes=[pltpu.VMEM((8, cb), F32), pltpu.VMEM((8, cb), F32)],
        compiler_params=_cp("parallel", "arbitrary"))(a_f, x_f, a_r, x_r)


def _merge_outproj(x, ya, h0, h1, z, wo, layer, tm):
    s = x.shape[0]

    def body(x_ref, ya_ref, h0_ref, h1_ref, zg_ref, za_ref, zb_ref, wo_ref, x1_ref, mg_ref):
        gg, _ = _gelu(zg_ref[...].astype(F32))
        yb = (h0_ref[...] + h1_ref[...]) * gg
        m32 = (_sigmoid(za_ref[...].astype(F32)) * ya_ref[...].astype(F32)
               + _sigmoid(zb_ref[...].astype(F32)) * yb)
        mg_ref[...] = m32.T.astype(BF16)
        x1_ref[...] = x_ref[...] + _dot(m32.astype(BF16), wo_ref[...])

    tile = pl.BlockSpec((tm, D), lambda i: (i, 0))
    return pl.pallas_call(
        body, name=f"merge_outproj_l{layer}", grid=(s // tm,),
        in_specs=[tile, tile, tile, tile]
        + [pl.BlockSpec((tm, D), lambda i, c=c: (i, c)) for c in (3, 4, 5)] + [_full((D, D))],
        out_specs=[tile, pl.BlockSpec((D, tm), lambda i: (0, i))], out_shape=[SDS((s, D), F32), SDS((D, s), BF16)],
        compiler_params=_cp("parallel"))(x, ya, h0, h1, z, z, z, wo)


def _ffn_fwd(x1, g, wfi, wfo, layer, tm):
    s = x1.shape[0]

    def body(x_ref, g_ref, wi_ref, wo_ref, x2_ref, gu_ref, h_ref):
        x = x_ref[...]
        h = _rms_fwd(x, g_ref[...]).astype(BF16)
        h_ref[...] = h
        acc = x
        for k in range(4):
            gate = _dot_nt(h, wi_ref[0, k])
            up = _dot_nt(h, wi_ref[1, k])
            gu_ref[0, k] = gate.astype(BF16)
            gu_ref[1, k] = up.astype(BF16)
            acc = acc + _dot((gate * _sigmoid(gate) * up).astype(BF16), wo_ref[k])
        x2_ref[...] = acc

    tile = pl.BlockSpec((tm, D), lambda i: (i, 0))
    return pl.pallas_call(
        body, name=f"ffn_fwd_l{layer}", grid=(s // tm,),
        in_specs=[tile, _full((1, D)), _resident((2, 4, FF_BLK, D)), _resident((4, FF_BLK, D))],
        out_specs=[tile, pl.BlockSpec((2, 4, tm, FF_BLK), lambda i: (0, 0, i, 0)), tile],
        out_shape=[SDS((s, D), F32), SDS((2, 4, s, FF_BLK), BF16), SDS((s, D), BF16)],
        compiler_params=_cp("parallel"))(x1, g, wfi, wfo)


def _loss_head(x, g, tgt, tm):
    s = x.shape[0]

    def body(x_ref, g_ref, t_ref, dx_ref, loss_ref, dg_ref):
        @pl.when(pl.program_id(0) == 0)
        def _():
            loss_ref[...] = jnp.zeros_like(loss_ref)
            dg_ref[...] = jnp.zeros_like(dg_ref)

        x = x_ref[...]
        gv = g_ref[...]
        e = _rms_fwd(x, gv) - t_ref[...]
        rows = jnp.sum(e * e, axis=-1, keepdims=True)
        loss_ref[...] += (0.5 / D) * jnp.sum(rows, axis=0, keepdims=True)
        dx, dg = _rms_bwd(x, gv, e * (1.0 / D))
        dx_ref[...] = dx
        _add_rows128(dg_ref, dg)

    tile = pl.BlockSpec((tm, D), lambda i: (i, 0))
    return pl.pallas_call(
        body, name="loss_head", grid=(s // tm,),
        in_specs=[tile, _full((1, D)), tile],
        out_specs=[tile, _full((1, 1)), _full((LANE_ROWS, HD))],
        out_shape=[SDS((s, D), F32), SDS((1, 1), F32), SDS((LANE_ROWS, HD), F32)],
        compiler_params=_cp("arbitrary"))(x, g, tgt)


def _ffn_bwd_act(dx2, wfo, gu, layer, tm):
    s = dx2.shape[0]

    def body(dx_ref, wo_ref, gu_ref, ff_ref, dgu_ref):
        dxb = dx_ref[...].astype(BF16)
        for k in range(4):
            dff = _dot_nt(dxb, wo_ref[k])
            gate = gu_ref[0, k].astype(F32)
            up = gu_ref[1, k].astype(F32)
            sg = _sigmoid(gate)
            sl = gate * sg
            ff_ref[k] = (sl * up).astype(BF16)
            dgu_ref[0, k] = (dff * up * (sg * (1.0 + gate * (1.0 - sg)))).astype(BF16)
            dgu_ref[1, k] = (dff * sl).astype(BF16)

    blk = pl.BlockSpec((2, 4, tm, FF_BLK), lambda i: (0, 0, i, 0))
    return pl.pallas_call(
        body, name=f"ffn_bwd_act_l{layer}", grid=(s // tm,),
        in_specs=[pl.BlockSpec((tm, D), lambda i: (i, 0)), _resident((4, FF_BLK, D)), blk],
        out_specs=[pl.BlockSpec((4, tm, FF_BLK), lambda i: (0, i, 0)), blk],
        out_shape=[SDS((4, s, FF_BLK), BF16), SDS((2, 4, s, FF_BLK), BF16)],
        compiler_params=_cp("parallel"))(dx2, wfo, gu)


def _mm_nt_rms_bwd(a, a_spec, a_blocks, w, w_is_transposed, x, g, dres, name, tm):
    s = x.shape[0]

    def body(a_ref, w_ref, x_ref, g_ref, dres_ref, dx_ref, dg_ref):
        @pl.when(pl.program_id(0) == 0)
        def _():
            dg_ref[...] = jnp.zeros_like(dg_ref)

        dh = None
        for k, blk in enumerate(a_blocks(a_ref)):
            part = _dot(blk, w_ref[k]) if w_is_transposed else _dot_nt(blk, w_ref[k])
            dh = part if dh is None else dh + part
        dx, dg = _rms_bwd(x_ref[...], g_ref[...], dh)
        dx_ref[...] = dres_ref[...] + dx
        _add_rows128(dg_ref, dg)

    tile = pl.BlockSpec((tm, D), lambda i: (i, 0))
    return pl.pallas_call(
        body, name=name, grid=(s // tm,),
        in_specs=[a_spec, _resident(w.shape), tile, _full((1, D)), tile],
        out_specs=[tile, _full((LANE_ROWS, HD))], out_shape=[SDS((s, D), F32), SDS((LANE_ROWS, HD), F32)],
        compiler_params=_cp("arbitrary"))(a, w, x, g, dres)


def _mm_tn(a, a_spec, b, b_spec, nb, out_shape, out_spec, name, a_is_transposed=True, after=None):
    def body(a_ref, b_ref, *rest):
        o_ref = rest[-1]
        bb = b_ref[...].astype(BF16)
        o_ref[...] = (_dot(a_ref[...], bb) if a_is_transposed else _dot_tn(a_ref[...], bb)).astype(BF16)

    deps = [] if after is None else [after]
    return pl.pallas_call(
        body, name=name, grid=(nb,), in_specs=[a_spec, b_spec] + [_ANY] * len(deps), out_specs=out_spec,
        out_shape=SDS(out_shape, BF16), compiler_params=_cp("parallel"))(a, b, *deps)


def _outproj_bwd_merge(dx1, wo, ya, h0, h1, z, layer, tm):
    s = dx1.shape[0]

    def body(dx_ref, wo_ref, ya_ref, h0_ref, h1_ref, zg_ref, za_ref, zb_ref, dz_ref, dya_ref, dh_ref):
        dm = _dot_nt(dx_ref[...].astype(BF16), wo_ref[...])
        sa = _sigmoid(za_ref[...].astype(F32))
        sb = _sigmoid(zb_ref[...].astype(F32))
        zg = zg_ref[...].astype(F32)
        gg, tg = _gelu(zg)
        hs = h0_ref[...] + h1_ref[...]
        dyb = dm * sb
        dya_ref[...] = (dm * sa).astype(BF16)
        dh_ref[...] = dyb * gg
        dz_ref[:, 0:D] = (dyb * hs * _gelu_grad(zg, tg)).astype(BF16)
        dz_ref[:, D:2 * D] = (dm * ya_ref[...].astype(F32) * (sa * (1.0 - sa))).astype(BF16)
        dz_ref[:, 2 * D:3 * D] = (dm * (hs * gg) * (sb * (1.0 - sb))).astype(BF16)

    tile = pl.BlockSpec((tm, D), lambda i: (i, 0))
    return pl.pallas_call(
        body, name=f"outproj_bwd_merge_l{layer}", grid=(s // tm,),
        in_specs=[tile, _full((D, D)), tile, tile, tile]
        + [pl.BlockSpec((tm, D), lambda i, c=c: (i, c)) for c in (3, 4, 5)],
        out_specs=[pl.BlockSpec((tm, 3 * D), lambda i: (i, 1)), tile, tile],
        out_shape=[SDS((s, N_IN), BF16), SDS((s, D), BF16), SDS((s, D), F32)],
        compiler_params=_cp("parallel"))(dx1, wo, ya, h0, h1, z, z, z)


def _lru_gates_bwd(z, h0, h1, g0, g1, cw, cb, wr, wi, br, bi, lam, layer, tm):
    s = z.shape[0]
    nt = s // tm

    def body(zp_ref, zc_ref, zn_ref, h0p_ref, h0_ref, h1_ref, h1n_ref, g0_ref, g1_ref,
             cw_ref, cb_ref, wr_ref, wi_ref, br_ref, bi_ref, lam_ref,
             dxc_ref, dwr_ref, dwi_ref, dbr_ref, dbi_ref, dlam_ref):
        i = pl.program_id(0)
        fp, fn = _halo_flags(nt)

        @pl.when(i == 0)
        def _():
            for r in (dwr_ref, dwi_ref, dbr_ref, dbi_ref, dlam_ref):
                r[...] = jnp.zeros_like(r)

        xc = _conv(_taps(*_halo_load(zp_ref, zc_ref, zn_ref, fp, fn), tm), cw_ref, cb_ref)
        xb = xc.astype(BF16)
        zeros8 = jnp.zeros((8, D), F32)
        h_prev = _taps(h0p_ref[...] * fp, h0_ref[...], zeros8, tm)[1]
        h_next = _taps(zeros8, h1_ref[...], h1n_ref[...] * fn, tm)[3]
        dxc = jnp.zeros((tm, D), F32)
        for d, (g_ref, hsh) in enumerate(((g0_ref, h_prev), (g1_ref, h_next))):
            sp, r, ig, a, mult, rmult = _lru_gates(xc, xb, wr_ref, wi_ref, br_ref, bi_ref, lam_ref, d)
            db = g_ref[...]
            da = db * hsh
            dmult = db * (ig * xc)
            di = db * (mult * xc)
            dxc = dxc + db * (mult * ig)
            dla = da * a - dmult * (a * a * rmult)
            dsp_dlam = -_sigmoid(-lam_ref[d:d + 1, :])
            _add_rows128(dlam_ref, jnp.sum(dla * r, axis=0, keepdims=True) * ((-LRU_C) * dsp_dlam), d * LANE_ROWS)
            dpr = dla * sp * (-LRU_C) * (r * (1.0 - r))
            dpi = di * (ig * (1.0 - ig))
            _add_rows128(dbr_ref, jnp.sum(dpr, axis=0, keepdims=True), d * LANE_ROWS)
            _add_rows128(dbi_ref, jnp.sum(dpi, axis=0, keepdims=True), d * LANE_ROWS)
            dprb = dpr.astype(BF16)
            dpib = dpi.astype(BF16)
            parts = []
            for h in range(HEADS):
                cs = slice(h * HD, (h + 1) * HD)
                dwr_ref[d, h] += _dot_tn(xb[:, cs], dprb[:, cs])
                dwi_ref[d, h] += _dot_tn(xb[:, cs], dpib[:, cs])
                parts.append(_dot_nt(dprb[:, cs], wr_ref[d, h]) + _dot_nt(dpib[:, cs], wi_ref[d, h]))
            dxc = dxc + jnp.concatenate(parts, axis=1)
        dxc_ref[...] = dxc.astype(BF16)

    tile = pl.BlockSpec((tm, D), lambda i: (i, 0))
    zp, zc, zn = _halo_specs(tm, s, 2, 16)
    hp, hc, hn = _halo_specs(tm, s, 0)
    wspec = _full((2, HEADS, HD, HD))
    vspec = _full((2 * LANE_ROWS, HD))
    return pl.pallas_call(
        body, name=f"lru_gates_bwd_l{layer}", grid=(nt,),
        in_specs=[zp, zc, zn, hp, hc, hc, hn, tile, tile, _full((4, D)), _full((1, D)),
                  wspec, wspec, _full((2, D)), _full((2, D)), _full((2, D))],
        out_specs=[tile, wspec, wspec, vspec, vspec, vspec],
        out_shape=[SDS((s, D), BF16), SDS((2, HEADS, HD, HD), F32), SDS((2, HEADS, HD, HD), F32)]
        + [SDS((2 * LANE_ROWS, HD), F32)] * 3,
        compiler_params=_cp("arbitrary"))(z, z, z, h0, h0, h1, h1, g0, g1, cw, cb, wr, wi, br, bi, lam)


def _conv_bwd(dz, dxc, z, cw, layer, tm):
    s = z.shape[0]
    nt = s // tm

    def body(dz_in, dp_ref, dc_ref, dn_ref, zp_ref, zc_ref, zn_ref, cw_ref, dz_ref, dcw_ref, dcb_ref):
        del dz_in
        fp, fn = _halo_flags(nt)

        @pl.when(pl.program_id(0) == 0)
        def _():
            dcw_ref[...] = jnp.zeros_like(dcw_ref)
            dcb_ref[...] = jnp.zeros_like(dcb_ref)

        dxc_halo = _halo_load(dp_ref, dc_ref, dn_ref, fp, fn)
        dxc = dxc_halo[1]
        dm2, dm1, _, dp1, _ = _taps(*dxc_halo, tm)
        dz_ref[...] = (cw_ref[0:1, :] * dp1 + cw_ref[1:2, :] * dxc + cw_ref[2:3, :] * dm1
                       + cw_ref[3:4, :] * dm2).astype(BF16)
        _, zm1, z0, zp1, zp2 = _taps(*_halo_load(zp_ref, zc_ref, zn_ref, fp, fn), tm)
        for k, zt in enumerate((zm1, z0, zp1, zp2)):
            _add_rows128(dcw_ref, jnp.sum(dxc * zt, axis=0, keepdims=True), k * LANE_ROWS)
        _add_rows128(dcb_ref, jnp.sum(dxc, axis=0, keepdims=True))

    return pl.pallas_call(
        body, name=f"conv_bwd_l{layer}", grid=(nt,),
        in_specs=[pl.BlockSpec(memory_space=pl.ANY), *_halo_specs(tm, s, 0, 16), *_halo_specs(tm, s, 2, 16),
                  _full((4, D))],
        out_specs=[pl.BlockSpec((tm, D), lambda i: (i, 2)), _full((4 * LANE_ROWS, HD)), _full((LANE_ROWS, HD))],
        out_shape=[SDS((s, N_IN), BF16), SDS((4 * LANE_ROWS, HD), F32), SDS((LANE_ROWS, HD), F32)],
        input_output_aliases={0: 0},
        compiler_params=_cp("arbitrary"))(dz, dxc, dxc, dxc, z, z, z, cw)


def _gmlp_bwd(dz, z, dya, lng, lnb, ws, wst, bsb, layer, tm):
    s = z.shape[0]
    nt = s // tm

    def body(dz_in, zu_ref, zv_ref, dya_ref, lng_ref, lnb_ref, ws_ref, wst_ref, bsb_ref,
             dz_ref, dws_ref, dbs_ref, dlng_ref, dlnb_ref, du_s, dv_s, dbs_acc):
        del dz_in
        i = pl.program_id(0)

        @pl.when(i == 0)
        def _():
            for r in (dws_ref, dlng_ref, dlnb_ref, dbs_acc):
                r[...] = jnp.zeros_like(r)

        zu = zu_ref[...].astype(F32)
        zv = zv_ref[...].astype(F32)
        u, tu = _gelu(zu)
        gv, tv = _gelu(zv)
        xc = gv - jnp.mean(gv, axis=-1, keepdims=True)
        rstd = lax.rsqrt(jnp.mean(xc * xc, axis=-1, keepdims=True) + EPS)
        xh = xc * rstd
        lng_v = lng_ref[...]
        vb = (xh * lng_v + lnb_ref[...]).astype(BF16)
        dya = dya_ref[...].astype(F32)
        for c in range(tm // HD):
            rs = slice(c * HD, (c + 1) * HD)
            for g in range(HEADS):
                cs = slice(g * HD, (g + 1) * HD)
                vblk = vb[rs, cs]
                mixed = _dot(ws_ref[g], vblk) + bsb_ref[g]
                du_s[rs, cs] = dya[rs, cs] * mixed
                dmx = dya[rs, cs] * u[rs, cs]
                dbs_acc[g] += dmx
                dmxb = dmx.astype(BF16)
                dws_ref[g] += _dot_nt(dmxb, vblk)
                dv_s[rs, cs] = _dot(wst_ref[g], dmxb)
        dv = dv_s[...]
        _add_rows128(dlng_ref, jnp.sum(dv * xh, axis=0, keepdims=True))
        _add_rows128(dlnb_ref, jnp.sum(dv, axis=0, keepdims=True))
        dxh = dv * lng_v
        dgv = rstd * (dxh - jnp.mean(dxh, axis=-1, keepdims=True)
                      - xh * jnp.mean(dxh * xh, axis=-1, keepdims=True))
        dz_ref[:, 0:D] = (du_s[...] * _gelu_grad(zu, tu)).astype(BF16)
        dz_ref[:, D:2 * D] = (dgv * _gelu_grad(zv, tv)).astype(BF16)

        @pl.when(i == nt - 1)
        def _():
            for g in range(HEADS):
                dbs_ref[g:g + 1, :] = jnp.sum(dbs_acc[g].T, axis=0, keepdims=True)

    tile = pl.BlockSpec((tm, D), lambda i: (i, 0))
    wspec = _full((HEADS, HD, HD))
    return pl.pallas_call(
        body, name=f"gmlp_bwd_l{layer}", grid=(nt,),
        in_specs=[pl.BlockSpec(memory_space=pl.ANY), tile, pl.BlockSpec((tm, D), lambda i: (i, 1)), tile,
                  _full((1, D)), _full((1, D)), wspec, wspec, wspec],
        out_specs=[pl.BlockSpec((tm, 2 * D), lambda i: (i, 0)), wspec, _full((HEADS, HD)),
                   _full((LANE_ROWS, HD)), _full((LANE_ROWS, HD))],
        out_shape=[SDS((s, N_IN), BF16), SDS((HEADS, HD, HD), F32), SDS((HEADS, HD), F32),
                   SDS((LANE_ROWS, HD), F32), SDS((LANE_ROWS, HD), F32)],
        scratch_shapes=[pltpu.VMEM((tm, D), F32), pltpu.VMEM((tm, D), F32), pltpu.VMEM((HEADS, HD, HD), F32)],
        input_output_aliases={0: 0},
        compiler_params=_cp("arbitrary"))(dz, z, z, dya, lng, lnb, ws, wst, bsb)


def _me():
    return lax.axis_index("x"), lax.axis_index("y"), lax.axis_index("c")


def _peer(m):
    x, y, c = _me()
    px = 1 - x if m & 4 else x
    py = 1 - y if m & 2 else y
    pc = 1 - c if m & 1 else c
    return (px, py, pc), 4 * px + 2 * py + pc


_ANY = pl.BlockSpec(memory_space=pl.ANY)
_EXCHANGE_SEMS = [pltpu.SemaphoreType.DMA((N_DEV - 1,)), pltpu.SemaphoreType.DMA((N_DEV - 1,)), pltpu.SemaphoreType.DMA(())]


def _all_gather(v, after, name):
    def body(v_ref, after_ref, o_ref, send_sems, recv_sems, local_sem):
        del after_ref
        x, y, c = _me()
        me = 4 * x + 2 * y + c
        local = pltpu.make_async_copy(v_ref, o_ref.at[me], local_sem)
        local.start()
        sends = []
        for m in range(1, N_DEV):
            dev, _ = _peer(m)
            cp = pltpu.make_async_remote_copy(v_ref, o_ref.at[me], send_sems.at[m - 1], recv_sems.at[m - 1],
                                              device_id=dev, device_id_type=pl.DeviceIdType.MESH)
            cp.start()
            sends.append(cp)
        for m in range(1, N_DEV):
            dev, blk = _peer(m)
            pltpu.make_async_remote_copy(v_ref, o_ref.at[blk], send_sems.at[m - 1], recv_sems.at[m - 1],
                                         device_id=dev, device_id_type=pl.DeviceIdType.MESH).wait_recv()
        for cp in sends:
            cp.wait_send()
        local.wait()

    return pl.pallas_call(
        body, name=name, in_specs=[_ANY, _ANY], out_specs=_ANY,
        out_shape=SDS((N_DEV,) + v.shape, v.dtype), scratch_shapes=_EXCHANGE_SEMS)(v, after)


_HBM = pl.BlockSpec(memory_space=pltpu.HBM)
_SEM = pl.BlockSpec(memory_space=pltpu.SEMAPHORE)
_EFFECT = pltpu.CompilerParams(has_side_effects=pltpu.SideEffectType.DATAFLOW_SIDE_EFFECTING)
_PEER_SEMS = pltpu.SemaphoreType.DMA((N_DEV - 1,))


def _in_hbm(a):
    return pltpu.with_memory_space_constraint(a, pltpu.HBM)


def _remote(src, dst, send_sems, recv_sems, m):
    dev, _ = _peer(m)
    return pltpu.make_async_remote_copy(src, dst, send_sems.at[m - 1], recv_sems.at[m - 1],
                                        device_id=dev, device_id_type=pl.DeviceIdType.MESH)


def _gather_start(lands, after, name):
    n = len(lands)

    def body(*refs):
        land = refs[:n]
        sems = refs[n + 1:3 * n + 1]
        token = refs[-1]
        x, y, c = _me()
        me = 4 * x + 2 * y + c
        for t in range(n):
            for m in range(1, N_DEV):
                _remote(land[t].at[me], land[t].at[me], sems[2 * t], sems[2 * t + 1], m).start()
        token[...] = jnp.zeros_like(token)

    res = pl.pallas_call(
        body, name=name, in_specs=[_HBM] * n + [_ANY],
        out_specs=[_SEM] * (2 * n) + [_HBM] * n + [pl.BlockSpec(memory_space=pltpu.VMEM)],
        out_shape=[_PEER_SEMS] * (2 * n) + [pltpu.HBM(a.shape, a.dtype) for a in lands] + [SDS((8, 128), F32)],
        input_output_aliases={t: 2 * n + t for t in range(n)},
        compiler_params=_EFFECT)(*[_in_hbm(a) for a in lands], after)
    return [(res[2 * t], res[2 * t + 1], res[2 * n + t]) for t in range(n)], res[-1]


def _gather_wait(handle, after, name):
    send_sems, recv_sems, land = handle

    def body(land_ref, ssem, rsem, after_ref, out_ref):
        del after_ref, out_ref
        x, y, c = _me()
        me = 4 * x + 2 * y + c
        for m in range(1, N_DEV):
            _, blk = _peer(m)
            cp = _remote(land_ref.at[me], land_ref.at[blk], ssem, rsem, m)
            cp.wait_send()
            cp.wait_recv()

    return pl.pallas_call(
        body, name=name, in_specs=[_HBM, _SEM, _SEM, _ANY], out_specs=_HBM,
        out_shape=pltpu.HBM(land.shape, land.dtype), input_output_aliases={0: 0},
        compiler_params=_EFFECT)(land, send_sems, recv_sems, after)


FIRST_STAGE = (1, 2, 4, 6)
RELAYED = (2, 4, 6)
OTHER_CORE = 1


def _stage_copy(src, dst, send_sems, recv_sems, k, m):
    dev, _ = _peer(m)
    return pltpu.make_async_remote_copy(src, dst, send_sems.at[k], recv_sems.at[k],
                                        device_id=dev, device_id_type=pl.DeviceIdType.MESH)


def _gather2_start(lands, after, name):
    n = len(lands)

    def body(*refs):
        land = refs[:n]
        sems = refs[n + 1:3 * n + 1]
        token = refs[-1]
        x, y, c = _me()
        me = 4 * x + 2 * y + c
        for t in range(n):
            for k, m in enumerate(FIRST_STAGE):
                _stage_copy(land[t].at[me], land[t].at[me], sems[2 * t], sems[2 * t + 1], k, m).start()
        token[...] = jnp.zeros_like(token)

    stage_sems = pltpu.SemaphoreType.DMA((len(FIRST_STAGE),))
    res = pl.pallas_call(
        body, name=name, in_specs=[_HBM] * n + [_ANY],
        out_specs=[_SEM] * (2 * n) + [_HBM] * n + [pl.BlockSpec(memory_space=pltpu.VMEM)],
        out_shape=[stage_sems] * (2 * n) + [pltpu.HBM(a.shape, a.dtype) for a in lands] + [SDS((8, 128), F32)],
        input_output_aliases={t: 2 * n + t for t in range(n)},
        compiler_params=_EFFECT)(*[_in_hbm(a) for a in lands], after)
    return [(res[2 * t], res[2 * t + 1], res[2 * n + t]) for t in range(n)], res[-1]


def _gather2_relay(handles, after, name):
    n = len(handles)

    def body(*refs):
        land, send1, recv1 = refs[:n], refs[n:2 * n], refs[2 * n:3 * n]
        sems = refs[3 * n + 1:5 * n + 1]
        token = refs[-1]
        x, y, c = _me()
        me = 4 * x + 2 * y + c
        for t in range(n):
            for j, m in enumerate(RELAYED):
                _, blk = _peer(m)
                _stage_copy(land[t].at[me], land[t].at[blk], send1[t], recv1[t], 1 + j, m).wait_recv()
                _stage_copy(land[t].at[blk], land[t].at[blk], sems[2 * t], sems[2 * t + 1], j, OTHER_CORE).start()
        token[...] = jnp.zeros_like(token)

    relay_sems = pltpu.SemaphoreType.DMA((len(RELAYED),))
    lands = [h[2] for h in handles]
    res = pl.pallas_call(
        body, name=name, in_specs=[_HBM] * n + [_SEM] * (2 * n) + [_ANY],
        out_specs=[_SEM] * (2 * n) + [_HBM] * n + [pl.BlockSpec(memory_space=pltpu.VMEM)],
        out_shape=[relay_sems] * (2 * n) + [pltpu.HBM(a.shape, a.dtype) for a in lands] + [SDS((8, 128), F32)],
        input_output_aliases={t: 2 * n + t for t in range(n)},
        compiler_params=_EFFECT)(*lands, *[h[0] for h in handles], *[h[1] for h in handles], after)
    return [(h[0], h[1], res[2 * t], res[2 * t + 1], res[2 * n + t]) for t, h in enumerate(handles)], res[-1]


def _gather2_wait(handle, after, name):
    send1, recv1, send2, recv2, land = handle

    def body(land_ref, s1, r1, s2, r2, after_ref, out_ref):
        del after_ref, out_ref
        x, y, c = _me()
        me = 4 * x + 2 * y + c
        _, other = _peer(OTHER_CORE)
        _stage_copy(land_ref.at[me], land_ref.at[other], s1, r1, 0, OTHER_CORE).wait_recv()
        for k, m in enumerate(FIRST_STAGE):
            _stage_copy(land_ref.at[me], land_ref.at[me], s1, r1, k, m).wait_send()
        for j, m in enumerate(RELAYED):
            _, mine = _peer(m)
            _, theirs = _peer(m ^ OTHER_CORE)
            _stage_copy(land_ref.at[mine], land_ref.at[mine], s2, r2, j, OTHER_CORE).wait_send()
            _stage_copy(land_ref.at[mine], land_ref.at[theirs], s2, r2, j, OTHER_CORE).wait_recv()

    return pl.pallas_call(
        body, name=name, in_specs=[_HBM] + [_SEM] * 4 + [_ANY], out_specs=_HBM,
        out_shape=pltpu.HBM(land.shape, land.dtype), input_output_aliases={0: 0},
        compiler_params=_EFFECT)(land, send1, recv1, send2, recv2, after)


def _exchange_start(ps, name):
    n = len(ps)

    def body(*refs):
        p = refs[:n]
        got = refs[n:2 * n]
        sems = refs[2 * n:5 * n]
        token = refs[-1]
        x, y, c = _me()
        me = 4 * x + 2 * y + c
        for t in range(n):
            pltpu.make_async_copy(p[t].at[me], got[t].at[me], sems[3 * t + 2]).start()
            for m in range(1, N_DEV):
                _, blk = _peer(m)
                _remote(p[t].at[blk], got[t].at[me], sems[3 * t], sems[3 * t + 1], m).start()
        token[...] = jnp.zeros_like(token)

    res = pl.pallas_call(
        body, name=name, in_specs=[_HBM] * (2 * n),
        out_specs=[_SEM] * (3 * n) + [_HBM] * (2 * n) + [pl.BlockSpec(memory_space=pltpu.VMEM)],
        out_shape=[_PEER_SEMS, _PEER_SEMS, pltpu.SemaphoreType.DMA(())] * n
        + [pltpu.HBM(a.shape, a.dtype) for a in ps] * 2 + [SDS((8, 128), F32)],
        input_output_aliases={t: 3 * n + t for t in range(2 * n)},
        compiler_params=_EFFECT)(*[_in_hbm(a) for a in ps], *[_in_hbm(lax.empty(a.shape, a.dtype)) for a in ps])
    return [(res[3 * t], res[3 * t + 1], res[3 * t + 2], res[3 * n + t], res[4 * n + t]) for t in range(n)], res[-1]


def _exchange_wait(handle, after, name):
    send_sems, recv_sems, local_sem, p, got = handle

    def body(p_ref, got_ref, ssem, rsem, lsem, after_ref, p_out, got_out):
        del after_ref, p_out, got_out
        x, y, c = _me()
        me = 4 * x + 2 * y + c
        pltpu.make_async_copy(p_ref.at[me], got_ref.at[me], lsem).wait()
        for m in range(1, N_DEV):
            _, blk = _peer(m)
            cp = _remote(p_ref.at[blk], got_ref.at[blk], ssem, rsem, m)
            cp.wait_send()
            cp.wait_recv()

    return pl.pallas_call(
        body, name=name, in_specs=[_HBM, _HBM, _SEM, _SEM, _SEM, _ANY], out_specs=[_HBM, _HBM],
        out_shape=[pltpu.HBM(p.shape, p.dtype), pltpu.HBM(got.shape, got.dtype)],
        input_output_aliases={0: 0, 1: 1}, compiler_params=_EFFECT)(p, got, send_sems, recv_sems, local_sem, after)[1]


def _cast_into_slot(w, layer, me1, name):
    _, r, c = w.shape
    tr = next(t for t in (256, 352, r) if r % t == 0)

    def body(me_ref, w_ref, o_ref):
        del me_ref
        o_ref[...] = w_ref[...].astype(BF16)

    return pl.pallas_call(
        body, name=name,
        grid_spec=pltpu.PrefetchScalarGridSpec(
            num_scalar_prefetch=1, grid=(r // tr,),
            in_specs=[pl.BlockSpec((None, tr, c), lambda i, me: (layer, i, 0))],
            out_specs=pl.BlockSpec((None, tr, c), lambda i, me: (me[0], i, 0))),
        out_shape=SDS((N_DEV, r, c), BF16), compiler_params=_cp("arbitrary"))(me1, w)


def _sum8_into_slot(p, me1, name):
    _, r, c = p.shape

    def body(me_ref, p_ref, o_ref):
        del me_ref
        acc = p_ref[0]
        for k in range(1, N_DEV):
            acc = acc + p_ref[k]
        o_ref[...] = acc

    return pl.pallas_call(
        body, name=name,
        grid_spec=pltpu.PrefetchScalarGridSpec(
            num_scalar_prefetch=1, grid=(1,),
            in_specs=[pl.BlockSpec(p.shape, lambda i, me: (0, 0, 0))],
            out_specs=pl.BlockSpec((None, r, c), lambda i, me: (me[0], 0, 0))),
        out_shape=SDS(p.shape, F32), compiler_params=_cp("arbitrary"))(me1, p)


def _adamw(w, g, m, v):
    m = ADAM_B1 * m + (1.0 - ADAM_B1) * g
    v = ADAM_B2 * v + (1.0 - ADAM_B2) * (g * g)
    m_hat = m / (1.0 - ADAM_B1 ** ADAM_STEP)
    v_hat = v / (1.0 - ADAM_B2 ** ADAM_STEP)
    delta = -ADAM_LR * (m_hat / (jnp.sqrt(v_hat) + ADAM_EPS) + ADAM_WD * w)
    return delta, m, v


def _adam_shard(parts, w, m, v, layer, prev, name):
    _, r, c = parts.shape
    tr = next(t for t in (256, 352, r) if r % t == 0)
    n_prev = 0 if prev is None else 4

    def body(*refs):
        p_ref, w_ref, m_ref, v_ref = refs[:4]
        g_ref, d_ref, nm_ref, nv_ref = refs[4 + n_prev:]
        g = p_ref[0].astype(F32)
        for k in range(1, N_DEV):
            g = g + p_ref[k].astype(F32)
        delta, nm, nv = _adamw(w_ref[...], g, m_ref[...], v_ref[...])
        g_ref[...] = g
        d_ref[...] = delta
        nm_ref[...] = nm
        nv_ref[...] = nv

    blk = pl.BlockSpec((None, tr, c), lambda i: (layer, i, 0))
    return pl.pallas_call(
        body, name=name, grid=(r // tr,),
        in_specs=[pl.BlockSpec((N_DEV, tr, c), lambda i: (0, i, 0)), blk, blk, blk] + [_ANY] * n_prev,
        out_specs=[blk] * 4, out_shape=[SDS(w.shape, F32)] * 4,
        input_output_aliases={4 + k: k for k in range(n_prev)},
        compiler_params=_cp("parallel"))(parts, w, m, v, *(prev or ()))


SMALL_MATRICES = [("lru_w_r", 2048), ("lru_w_i", 2048), ("gmlp_w_s", 1024)]
SMALL_VECTORS = [("norm1_g", 8), ("gmlp_ln_g", 8), ("gmlp_ln_b", 8), ("gmlp_b_s", 8), ("conv_w", 32), ("conv_b", 8),
                 ("lru_b_r", 16), ("lru_b_i", 16), ("lru_lambda", 16), ("norm2_g", 8), ("final_g", 8)]
SMALL_VECTOR_ROW0 = sum(n for _, n in SMALL_MATRICES)
SMALL_VECTOR_BLOCK = 256
SMALL_ROWS = SMALL_VECTOR_ROW0 + SMALL_VECTOR_BLOCK


def _pack_small(small):
    parts = [small[k] for k, _ in SMALL_MATRICES]
    parts += [small[k] if k in small else jnp.zeros((n, HD), F32) for k, n in SMALL_VECTORS]
    flat = jnp.concatenate(parts)
    return jnp.pad(flat, ((0, SMALL_ROWS - flat.shape[0]), (0, 0))).reshape(N_DEV, SMALL_ROWS // N_DEV, HD)


def _adam_matrix(g0, g1, w, m, v, row0, name):
    _, rows, _ = w.shape

    def body(g0_ref, g1_ref, w_ref, m_ref, v_ref, g_ref, d_ref, nm_ref, nv_ref):
        for l, src in enumerate((g0_ref, g1_ref)):
            g = src[...]
            delta, nm, nv = _adamw(w_ref[l], g, m_ref[l], v_ref[l])
            g_ref[l] = g
            d_ref[l] = delta
            nm_ref[l] = nm
            nv_ref[l] = nv

    gspec = pl.BlockSpec((rows, HD), lambda i: (row0 // rows, 0))
    return pl.pallas_call(body, name=name, grid=(1,), in_specs=[gspec, gspec] + [_full(w.shape)] * 3,
                          out_specs=[_full(w.shape)] * 4, out_shape=[SDS(w.shape, F32)] * 4,
                          compiler_params=_cp("arbitrary"))(g0, g1, w, m, v)


def _adam_vectors(g0, g1, dg1_parts, me1, ws, ms, vs):
    names = [k for k, _ in SMALL_VECTORS]
    n = len(names)

    def lanes(rows8):
        return jnp.concatenate([rows8[k:k + 1, :] for k in range(LANE_ROWS)], axis=1)

    def body(me_ref, g0_ref, g1_ref, dg1_ref, *refs):
        w_refs, m_refs, v_refs = refs[:n], refs[n:2 * n], refs[2 * n:3 * n]
        outs = refs[3 * n:]
        me = me_ref[0]
        g_refs = (g0_ref, g1_ref)

        def emit(i, idx, g):
            delta, nm, nv = _adamw(w_refs[i][idx], g, m_refs[i][idx], v_refs[i][idx])
            for j, val in enumerate((g, delta, nm, nv)):
                outs[4 * i + j][idx] = val

        off = 0
        for i, (name, rows) in enumerate(SMALL_VECTORS):
            for l in range(2):
                row = (slice(l, l + 1), slice(None))
                if name == "final_g":
                    if l == 1:
                        emit(i, (slice(0, 1), slice(None)), lanes(g1_ref[off:off + rows, :]))
                elif name == "norm1_g":
                    if l == 1:
                        emit(i, row, lanes(g0_ref[off:off + rows, :]))
                    else:
                        total = dg1_ref[0]
                        for k in range(1, N_DEV):
                            total = total + dg1_ref[k]
                        emit(i, row, lanes(total))
                elif name == "gmlp_b_s":
                    emit(i, (l,), g_refs[l][off:off + rows, :])
                elif rows == LANE_ROWS:
                    emit(i, row, lanes(g_refs[l][off:off + rows, :]))
                else:
                    for r in range(rows // LANE_ROWS):
                        emit(i, (l, slice(r, r + 1), slice(None)), g_refs[l][pl.ds(off + r * LANE_ROWS + me, 1), :])
            off += rows

    args = [ws[k] for k in names] + [ms[k] for k in names] + [vs[k] for k in names]
    gspec = pl.BlockSpec((SMALL_VECTOR_BLOCK, HD), lambda i, me: (SMALL_VECTOR_ROW0 // SMALL_VECTOR_BLOCK, 0))
    res = pl.pallas_call(
        body, name="adam_vectors",
        grid_spec=pltpu.PrefetchScalarGridSpec(
            num_scalar_prefetch=1, grid=(1,),
            in_specs=[gspec, gspec, _full(dg1_parts.shape)] + [_full(a.shape) for a in args],
            out_specs=[_full(ws[k].shape) for k in names for _ in range(4)]),
        out_shape=[SDS(ws[k].shape, F32) for k in names for _ in range(4)],
        compiler_params=_cp("arbitrary"))(me1, g0, g1, dg1_parts, *args)
    return {k: list(res[4 * i:4 * i + 4]) for i, k in enumerate(names)}


def _after(a, *tokens):
    for token in tokens:
        if token is not None:
            a = a + token[0:1, 0:1]
    return a


def _local_step(x, tgt, p, get_w, hook=lambda stage, layer, payload: None):
    s = x.shape[0]
    tm = _row_tile(s)
    wsb = p["gmlp_w_s"].astype(BF16)
    wstb = jnp.swapaxes(p["gmlp_w_s"], -1, -2).astype(BF16)
    bsb = jnp.broadcast_to(p["gmlp_b_s"][..., None], p["gmlp_w_s"].shape)
    wrb = p["lru_w_r"].astype(BF16)
    wib = p["lru_w_i"].astype(BF16)
    saved = []
    for l in range(2):
        win = get_w("w_in", l, x)
        z, h1 = _norm_inproj(x, _after(p["norm1_g"][l][None], hook("pre_inproj", l, win)), win, l, tm)
        a0, b0, a1, b1 = _lru_gates_fwd(z, p["conv_w"][l], p["conv_b"][l][None], wrb[l], wib[l],
                                        p["lru_b_r"][l], p["lru_b_i"][l], p["lru_lambda"][l], l, tm)
        h0, hr = _lru_scan(a0, b0, a1, b1, False, l)
        ya = _gmlp_fwd(z, _after(p["gmlp_ln_g"][l][None], hook("pre_gmlp", l, h0)), p["gmlp_ln_b"][l][None],
                       wsb[l], bsb[l], l, tm)
        wout = get_w("w_out", l, ya)
        x1, mg = _merge_outproj(x, ya, h0, hr, z, wout, l, tm)
        wfi = get_w("w_ffn_in", l, x1)
        wfo = get_w("w_ffn_out", l, x1)
        x2, gu, h2 = _ffn_fwd(x1, p["norm2_g"][l][None], wfi, wfo, l, tm)
        saved.append((x, z, h1, ya, a0, a1, h0, hr, x1, mg, gu, h2, win, wout, wfi, wfo))
        x = x2
    dx, loss, dfg = _loss_head(x, p["final_g"][None], tgt, tm)
    pending = None
    for l in (1, 0):
        x0, z, h1, ya, a0, a1, h0, hr, x1, mg, gu, h2, win, wout, wfi, wfo = saved[l]
        ff, dgu = _ffn_bwd_act(dx, wfo, gu, l, tm)
        d_wfo = _mm_tn(ff, pl.BlockSpec((None, s, FF_BLK), lambda j: (j, 0, 0)), dx, _resident((s, D)),
                       4, (4, FF_BLK, D), pl.BlockSpec((None, FF_BLK, D), lambda j: (j, 0, 0)),
                       f"dw_ffn_out_l{l}", a_is_transposed=False)
        dgu8 = dgu.reshape(N_DEV, s, FF_BLK)
        d_wfi = _mm_tn(dgu8, pl.BlockSpec((None, s, FF_BLK), lambda j: (j, 0, 0)), h2, _resident((s, D)),
                       N_DEV, (N_DEV, FF_BLK, D), pl.BlockSpec((None, FF_BLK, D), lambda j: (j, 0, 0)),
                       f"dw_ffn_in_l{l}", a_is_transposed=False)
        token = hook("ffn_partials", l, dict(w_ffn_out=d_wfo.reshape(N_DEV, D_FF // N_DEV, D), w_ffn_in=d_wfi))
        dx1, dg2 = _mm_nt_rms_bwd(
            dgu8, pl.BlockSpec((N_DEV, tm, FF_BLK), lambda i: (0, i, 0)), lambda r: [r[k] for k in range(N_DEV)],
            wfi.reshape(N_DEV, FF_BLK, D), True, x1, _after(p["norm2_g"][l][None], token, pending), dx,
            f"ffn_bwd_dx_l{l}", tm)
        pending = hook("mid_backward", l, dx1)
        dz, dya, dh = _outproj_bwd_merge(dx1, wout, ya, h0, hr, z, l, tm)
        d_wout = _mm_tn(mg, _resident((D, s)), dx1, pl.BlockSpec((s, D // 2), lambda j: (0, j)),
                        2, (D, D), pl.BlockSpec((D, D // 2), lambda j: (0, j)), f"dw_out_l{l}")
        g1, g0 = _lru_scan(a1, dh, a0, dh, True, l)
        dxc, dwr, dwi, dbr, dbi, dlam = _lru_gates_bwd(
            z, h0, hr, g0, g1, p["conv_w"][l], _after(p["conv_b"][l][None], pending), wrb[l], wib[l],
            p["lru_b_r"][l], p["lru_b_i"][l], p["lru_lambda"][l], l, tm)
        dz, dcw, dcb = _conv_bwd(dz, dxc, z, p["conv_w"][l], l, tm)
        dz, dws, dbs, dlng, dlnb = _gmlp_bwd(dz, z, dya, p["gmlp_ln_g"][l][None], p["gmlp_ln_b"][l][None],
                                             wsb[l], wstb[l], bsb[l], l, tm)
        small = dict(lru_w_r=dwr.reshape(-1, HD), lru_w_i=dwi.reshape(-1, HD), gmlp_w_s=dws.reshape(-1, HD),
                     gmlp_ln_g=dlng, gmlp_ln_b=dlnb, gmlp_b_s=dbs, conv_w=dcw, conv_b=dcb, lru_b_r=dbr,
                     lru_b_i=dbi, lru_lambda=dlam, norm2_g=dg2)
        if l == 1:
            small["final_g"] = dfg
        else:
            small["norm1_g"] = dg1
        started = hook("small_grads", l, small)
        d_win = _mm_tn(h1, _resident((D, s)), dz, pl.BlockSpec((s, IN_BLK), lambda j: (0, j)),
                       N_DEV, (N_DEV, D, IN_BLK), pl.BlockSpec((None, D, IN_BLK), lambda j: (j, 0, 0)),
                       f"dw_in_l{l}", after=started)
        token = hook("mixer_partials", l, dict(w_out=d_wout.reshape(N_DEV, D // N_DEV, D), w_in=d_win))
        dx, dg1 = _mm_nt_rms_bwd(
            dz, pl.BlockSpec((tm, N_IN), lambda i: (i, 0)),
            lambda r: [r[:, k * IN_BLK:(k + 1) * IN_BLK] for k in range(N_DEV)],
            win, False, x0, _after(p["norm1_g"][l][None], token, started, pending), dx1, f"inproj_bwd_dx_l{l}", tm)
        pending = None
    return loss, dx, dg1


_REPL = ["norm1_g", "gmlp_ln_g", "gmlp_ln_b", "gmlp_w_s", "gmlp_b_s", "conv_b", "lru_w_r", "lru_w_i", "norm2_g", "final_g"]
_LANE_SHARDED = ["conv_w", "lru_b_r", "lru_b_i", "lru_lambda"]
_BIG = ["w_in", "w_out", "w_ffn_in", "w_ffn_out"]
_ORDER = ["norm1_g", "w_in", "gmlp_ln_g", "gmlp_ln_b", "gmlp_w_s", "gmlp_b_s", "conv_w", "conv_b", "lru_w_r", "lru_b_r",
          "lru_w_i", "lru_b_i", "lru_lambda", "w_out", "norm2_g", "w_ffn_in", "w_ffn_out", "final_g"]


def kernel(x, norm1_g, w_in, gmlp_ln_g, gmlp_ln_b, gmlp_w_s, gmlp_b_s, conv_w, conv_b, lru_w_r, lru_b_r, lru_w_i, lru_b_i, lru_lambda, w_out, norm2_g, w_ffn_in, w_ffn_out, final_g, loss_target, m_norm1_g, m_w_in, m_gmlp_ln_g, m_gmlp_ln_b, m_gmlp_w_s, m_gmlp_b_s, m_conv_w, m_conv_b, m_lru_w_r, m_lru_b_r, m_lru_w_i, m_lru_b_i, m_lru_lambda, m_w_out, m_norm2_g, m_w_ffn_in, m_w_ffn_out, m_final_g, v_norm1_g, v_w_in, v_gmlp_ln_g, v_gmlp_ln_b, v_gmlp_w_s, v_gmlp_b_s, v_conv_w, v_conv_b, v_lru_w_r, v_lru_b_r, v_lru_w_i, v_lru_b_i, v_lru_lambda, v_w_out, v_norm2_g, v_w_ffn_in, v_w_ffn_out, v_final_g):
    w = dict(norm1_g=norm1_g, w_in=w_in, gmlp_ln_g=gmlp_ln_g, gmlp_ln_b=gmlp_ln_b, gmlp_w_s=gmlp_w_s, gmlp_b_s=gmlp_b_s,
             conv_w=conv_w, conv_b=conv_b, lru_w_r=lru_w_r, lru_b_r=lru_b_r, lru_w_i=lru_w_i, lru_b_i=lru_b_i,
             lru_lambda=lru_lambda, w_out=w_out, norm2_g=norm2_g, w_ffn_in=w_ffn_in, w_ffn_out=w_ffn_out, final_g=final_g)
    mom = dict(norm1_g=m_norm1_g, w_in=m_w_in, gmlp_ln_g=m_gmlp_ln_g, gmlp_ln_b=m_gmlp_ln_b, gmlp_w_s=m_gmlp_w_s,
               gmlp_b_s=m_gmlp_b_s, conv_w=m_conv_w, conv_b=m_conv_b, lru_w_r=m_lru_w_r, lru_b_r=m_lru_b_r,
               lru_w_i=m_lru_w_i, lru_b_i=m_lru_b_i, lru_lambda=m_lru_lambda, w_out=m_w_out, norm2_g=m_norm2_g,
               w_ffn_in=m_w_ffn_in, w_ffn_out=m_w_ffn_out, final_g=m_final_g)
    var = dict(norm1_g=v_norm1_g, w_in=v_w_in, gmlp_ln_g=v_gmlp_ln_g, gmlp_ln_b=v_gmlp_ln_b, gmlp_w_s=v_gmlp_w_s,
               gmlp_b_s=v_gmlp_b_s, conv_w=v_conv_w, conv_b=v_conv_b, lru_w_r=v_lru_w_r, lru_b_r=v_lru_b_r,
               lru_w_i=v_lru_w_i, lru_b_i=v_lru_b_i, lru_lambda=v_lru_lambda, w_out=v_w_out, norm2_g=v_norm2_g,
               w_ffn_in=v_w_ffn_in, w_ffn_out=v_w_ffn_out, final_g=v_final_g)
    for src in (w, mom, var):
        src["w_ffn_in"] = jnp.swapaxes(src["w_ffn_in"], 1, 2)
    xi, yi, ci = _me()
    me = 4 * xi + 2 * yi + ci

    lane_shapes = [w[k].shape for k in _LANE_SHARDED]
    lane_rows = sum(a[0] * a[1] for a in lane_shapes)
    packed = jnp.concatenate([w[k].reshape(-1, HD) for k in _LANE_SHARDED])
    packed = jnp.pad(packed, ((0, -lane_rows % 8), (0, 0)))
    lanes = _all_gather(packed, packed, "gather_small")
    params = {k: w[k] for k in _REPL}
    off = 0
    for k, shp in zip(_LANE_SHARDED, lane_shapes):
        n = shp[0] * shp[1]
        params[k] = jnp.swapaxes(lanes[:, off:off + n], 0, 1).reshape(shp[0], shp[1], D)
        off += n

    me1 = jnp.reshape(me, (1,)).astype(jnp.int32)
    gathers = {}
    exchanges = {}
    views = dict(w_in=(N_DEV, D, IN_BLK), w_out=(D, D), w_ffn_in=(2, 4, FF_BLK, D), w_ffn_out=(4, FF_BLK, D))
    small_ex = {}
    small_ag = {}

    def start_gather(names, l, after):
        lands = [_cast_into_slot(w[k], l, me1, f"cast_{k}_l{l}") for k in names]
        started, tok = _gather2_start(lands, after, f"gather_start_{'_'.join(names)}_l{l}")
        gathers.update({(k, l): h for k, h in zip(names, started)})
        return tok

    def relay_gather(names, l, after):
        relayed, tok = _gather2_relay([gathers[(k, l)] for k in names], after, f"gather_relay_{'_'.join(names)}_l{l}")
        gathers.update({(k, l): h for k, h in zip(names, relayed)})
        return tok

    def get_w(k, l, after):
        if k == "w_in":
            after = relay_gather(_BIG[:1], l, after)
        return _gather2_wait(gathers[(k, l)], after, f"gather_wait_{k}_l{l}").reshape(views[k])

    def hook(stage, l, payload):
        if stage == "pre_inproj":
            return start_gather(_BIG[1:], l, payload)
        if stage == "pre_gmlp":
            tok = relay_gather(_BIG[1:], l, payload)
            return tok + start_gather(_BIG[:1], l + 1, tok) if l == 0 else tok
        if stage == "small_grads":
            (small_ex[l],), tok = _exchange_start([_pack_small(payload)], f"exchange_start_small_l{l}")
            return tok
        if stage == "mid_backward":
            return reduce_small(l + 1, payload) if l == 0 else None
        extra = reduce_small(0, payload["w_in"]) if (stage, l) == ("mixer_partials", 0) else None
        started, tok = _exchange_start(list(payload.values()), f"exchange_start_{'_'.join(payload)}_l{l}")
        exchanges.update({(k, l): h for k, h in zip(payload, started)})
        return tok if extra is None else tok + extra

    def reduce_small(l, after):
        got = _exchange_wait(small_ex[l], after, f"exchange_wait_small_l{l}")
        mine = _sum8_into_slot(got, me1, f"sum_small_l{l}")
        (small_ag[l],), tok = _gather_start([mine], got, f"gather_start_small_l{l}")
        return tok

    start_gather(_BIG[:1], 0, lanes)
    loss, dx, dg1 = _local_step(x[0], loss_target[0], params, get_w, hook)

    out = {}
    after = dx
    for k, l in [(k, l) for k in ("w_ffn_out", "w_ffn_in") for l in (1, 0)] + [("w_out", 1), ("w_in", 1)]:
        got = _exchange_wait(exchanges[(k, l)], after, f"exchange_wait_{k}_l{l}")
        out[k] = _adam_shard(got, w[k], mom[k], var[k], l, out.get(k), f"adam_{k}_l{l}")
        after = out[k][3]
    g_small = [_gather_wait(small_ag[l], after, f"gather_wait_small_l{l}").reshape(SMALL_ROWS, HD) for l in (0, 1)]
    row0 = 0
    for k, rows in SMALL_MATRICES:
        res = _adam_matrix(*g_small, *[src[k].reshape(2, rows, HD) for src in (w, mom, var)], row0, f"adam_{k}")
        out[k] = [a.reshape(w[k].shape) for a in res]
        after = res[3]
        row0 += rows
    for k in ("w_out", "w_in"):
        got = _exchange_wait(exchanges[(k, 0)], after, f"exchange_wait_{k}_l0")
        out[k] = _adam_shard(got, w[k], mom[k], var[k], 0, out[k], f"adam_{k}_l0")
    out["w_ffn_in"] = [jnp.swapaxes(a, 1, 2) for a in out["w_ffn_in"]]
    as_rows = lambda a: a.reshape(1, D) if a.ndim == 1 else a
    vec = _adam_vectors(*g_small, _all_gather(dg1, out["w_in"][3], "gather_norm1_grad"), me1,
                        *[{k: as_rows(src[k]) for k, _ in SMALL_VECTORS} for src in (w, mom, var)])
    out.update({k: [a.reshape(w[k].shape) for a in res] for k, res in vec.items()})

    loss = lax.psum(loss[0, 0], MESH_AXES)
    return (loss, dx[None], *[out[k][0] for k in _ORDER], *[out[k][1] for k in _ORDER],
            *[out[k][2] for k in _ORDER], *[out[k][3] for k in _ORDER])
```

```python
import jax
import jax.numpy as jnp
from jax import lax
from jax.experimental import pallas as pl
from jax.experimental.pallas import tpu as pltpu

F32 = jnp.float32
BF16 = jnp.bfloat16
SDS = jax.ShapeDtypeStruct

D = 1024
N_IN = 6 * D
D_FF = 2816
N_DEV = 8
IN_BLK = N_IN // N_DEV
FF_BLK = 2 * D_FF // N_DEV
HEADS = 8
HD = 128
EPS = 1e-6
LRU_C = 8.0
MESH_AXES = ("x", "y", "c")

ADAM_LR = 0.001
ADAM_B1 = 0.9
ADAM_B2 = 0.999
ADAM_EPS = 1e-08
ADAM_WD = 0.01
ADAM_STEP = 10

VMEM_LIMIT = 56 * 2**20


def _cp(*sem, **kw):
    return pltpu.CompilerParams(dimension_semantics=sem, vmem_limit_bytes=VMEM_LIMIT, **kw)


def _row_tile(s):
    return 512 if s >= 1024 else s // 2


_GELU_C = 0.7978845608028654


def _gelu(x):
    t = jnp.tanh(_GELU_C * (x + 0.044715 * (x * x * x)))
    return 0.5 * x * (1.0 + t), t


def _gelu_grad(x, t):
    return 0.5 * (1.0 + t) + 0.5 * x * (1.0 - t * t) * (_GELU_C * (1.0 + 0.134145 * (x * x)))


def _sigmoid(x):
    return 0.5 + 0.5 * jnp.tanh(0.5 * x)


def _softplus(x):
    e = jnp.exp(-jnp.abs(x))
    w = 1.0 + e
    l1p = jnp.where(w == 1.0, e, jnp.log(w) * e / jnp.where(w == 1.0, 1.0, w - 1.0))
    return jnp.maximum(x, 0.0) + l1p


def _rms_fwd(x, g):
    r = lax.rsqrt(jnp.mean(x * x, axis=-1, keepdims=True) + EPS)
    return x * r * g


def _rms_bwd(x, g, dh):
    r = lax.rsqrt(jnp.mean(x * x, axis=-1, keepdims=True) + EPS)
    xh = x * r
    dxh = dh * g
    dx = r * (dxh - xh * jnp.mean(dxh * xh, axis=-1, keepdims=True))
    dg = jnp.sum(dh * xh, axis=0, keepdims=True)
    return dx, dg


LANE_ROWS = D // HD


def _add_rows128(ref, vec, row0=0):
    for i in range(vec.shape[0]):
        for k in range(LANE_ROWS):
            j = row0 + i * LANE_ROWS + k
            ref[j:j + 1, :] += vec[i:i + 1, k * HD:(k + 1) * HD]


def _dot(a, b):
    return jnp.dot(a, b, preferred_element_type=F32)


def _dot_nt(a, b):
    return lax.dot_general(a, b, (((1,), (1,)), ((), ())), preferred_element_type=F32)


def _dot_tn(a, b):
    return lax.dot_general(a, b, (((0,), (0,)), ((), ())), preferred_element_type=F32)


def _taps(prev, cur, nxt, tm):
    hr = prev.shape[0]
    ext = jnp.concatenate([prev, cur, nxt], axis=0)
    n = tm + 2 * hr
    sl = slice(hr, hr + tm)
    return (pltpu.roll(ext, 2, 0)[sl], pltpu.roll(ext, 1, 0)[sl], cur,
            pltpu.roll(ext, n - 1, 0)[sl], pltpu.roll(ext, n - 2, 0)[sl])


def _halo_specs(tm, s, col, rows=8):
    nb = s // rows
    r = tm // rows
    return (pl.BlockSpec((rows, D), lambda i: (jnp.maximum(i * r - 1, 0), col)),
            pl.BlockSpec((tm, D), lambda i: (i, col)),
            pl.BlockSpec((rows, D), lambda i: (jnp.minimum((i + 1) * r, nb - 1), col)))


def _halo_load(prev_ref, cur_ref, next_ref, fp, fn):
    return prev_ref[...].astype(F32) * fp, cur_ref[...].astype(F32), next_ref[...].astype(F32) * fn


def _halo_flags(nt):
    i = pl.program_id(0)
    return (i > 0).astype(F32), (i < nt - 1).astype(F32)


def _full(shape):
    nd = len(shape)
    return pl.BlockSpec(shape, lambda *_: (0,) * nd)


def _resident(shape):
    nd = len(shape)
    return pl.BlockSpec(shape, lambda *_: (0,) * nd, pipeline_mode=pl.Buffered(1))


def _norm_inproj(x, g, w, layer, tm):
    s = x.shape[0]

    def body(x_ref, g_ref, w_ref, z_ref, ht_ref):
        h32 = _rms_fwd(x_ref[...], g_ref[...])
        ht_ref[...] = h32.T.astype(BF16)
        h = h32.astype(BF16)
        for j in range(N_DEV):
            z_ref[:, j * IN_BLK:(j + 1) * IN_BLK] = _dot(h, w_ref[j]).astype(BF16)

    return pl.pallas_call(
        body, name=f"norm_inproj_l{layer}", grid=(s // tm,),
        in_specs=[pl.BlockSpec((tm, D), lambda i: (i, 0)), _full((1, D)), _resident((N_DEV, D, IN_BLK))],
        out_specs=[pl.BlockSpec((tm, N_IN), lambda i: (i, 0)), pl.BlockSpec((D, tm), lambda i: (0, i))],
        out_shape=[SDS((s, N_IN), BF16), SDS((D, s), BF16)],
        compiler_params=_cp("parallel"))(x, g, w)


def _gmlp_fwd(z, lng, lnb, ws, bsb, layer, tm):
    s = z.shape[0]

    def body(zu_ref, zv_ref, lng_ref, lnb_ref, ws_ref, bsb_ref, ya_ref):
        u, _ = _gelu(zu_ref[...].astype(F32))
        gv, _ = _gelu(zv_ref[...].astype(F32))
        xc = gv - jnp.mean(gv, axis=-1, keepdims=True)
        rstd = lax.rsqrt(jnp.mean(xc * xc, axis=-1, keepdims=True) + EPS)
        vb = (xc * rstd * lng_ref[...] + lnb_ref[...]).astype(BF16)
        for c in range(tm // HD):
            rs = slice(c * HD, (c + 1) * HD)
            for g in range(HEADS):
                cs = slice(g * HD, (g + 1) * HD)
                mixed = _dot(ws_ref[g], vb[rs, cs]) + bsb_ref[g]
                ya_ref[rs, cs] = (u[rs, cs] * mixed).astype(BF16)

    return pl.pallas_call(
        body, name=f"gmlp_fwd_l{layer}", grid=(s // tm,),
        in_specs=[pl.BlockSpec((tm, D), lambda i: (i, 0)), pl.BlockSpec((tm, D), lambda i: (i, 1)),
                  _full((1, D)), _full((1, D)), _full((HEADS, HD, HD)), _full((HEADS, HD, HD))],
        out_specs=pl.BlockSpec((tm, D), lambda i: (i, 0)),
        out_shape=SDS((s, D), BF16),
        compiler_params=_cp("parallel"))(z, z, lng, lnb, ws, bsb)


def _conv(taps, cw_ref, cb_ref):
    _, m1, c0, p1, p2 = taps
    return cb_ref[...] + m1 * cw_ref[0:1, :] + c0 * cw_ref[1:2, :] + p1 * cw_ref[2:3, :] + p2 * cw_ref[3:4, :]


def _heads_dot(xb, w_ref, d):
    return jnp.concatenate([_dot(xb[:, h * HD:(h + 1) * HD], w_ref[d, h]) for h in range(HEADS)], axis=1)


def _lru_decay(r, sp):
    la = (-LRU_C) * r * sp
    a = jnp.exp(la)
    return a, jnp.tanh(-la) * (a * a + 1.0)


def _lru_gates_fwd(z, cw, cb, wr, wi, br, bi, lam, layer, tm):
    s = z.shape[0]
    nt = s // tm

    def body(zp_ref, zc_ref, zn_ref, cw_ref, cb_ref, wr_ref, wi_ref, br_ref, bi_ref, lam_ref,
             a0_ref, b0_ref, a1_ref, b1_ref, xc_ref, r0_ref, i0_ref, r1_ref, i1_ref):
        fp, fn = _halo_flags(nt)
        xc = _conv(_taps(*_halo_load(zp_ref, zc_ref, zn_ref, fp, fn), tm), cw_ref, cb_ref)
        xb = xc.astype(BF16)
        xc_ref[...] = xb
        for d, (a_ref, b_ref, r_ref, i_ref) in enumerate(((a0_ref, b0_ref, r0_ref, i0_ref),
                                                          (a1_ref, b1_ref, r1_ref, i1_ref))):
            r = _sigmoid(_heads_dot(xb, wr_ref, d) + br_ref[d:d + 1, :])
            ig = _sigmoid(_heads_dot(xb, wi_ref, d) + bi_ref[d:d + 1, :])
            a, q = _lru_decay(r, _softplus(-lam_ref[d:d + 1, :]))
            a_ref[...] = a
            b_ref[...] = jnp.sqrt(q) * (ig * xc)
            r_ref[...] = r.astype(BF16)
            i_ref[...] = ig.astype(BF16)

    tile = pl.BlockSpec((tm, D), lambda i: (i, 0))
    return pl.pallas_call(
        body, name=f"lru_gates_fwd_l{layer}", grid=(nt,),
        in_specs=[*_halo_specs(tm, s, 2, 16), _full((4, D)), _full((1, D)),
                  _full((2, HEADS, HD, HD)), _full((2, HEADS, HD, HD)), _full((2, D)), _full((2, D)), _full((2, D))],
        out_specs=[tile] * 9, out_shape=[SDS((s, D), F32)] * 4 + [SDS((s, D), BF16)] * 5,
        compiler_params=_cp("parallel"))(z, z, z, cw, cb, wr, wi, br, bi, lam)


def _scan_group(a, x, c, reverse, bwd):
    row = lax.broadcasted_iota(jnp.int32, a.shape, 0)
    b = a * x if bwd else x
    for d in (1, 2, 4):
        keep = (row < 8 - d) if reverse else (row >= d)
        sh = 8 - d if reverse else d
        a_s = jnp.where(keep, pltpu.roll(a, sh, 0), 1.0)
        b_s = jnp.where(keep, pltpu.roll(b, sh, 0), 0.0)
        b = a * b_s + b
        a = a * a_s
    h = b + a * c
    new_c = h[0:1, :] if reverse else h[7:8, :]
    if not bwd:
        return h, new_c
    if reverse:
        prev = jnp.where(row < 7, pltpu.roll(h, 7, 0), c)
    else:
        prev = jnp.where(row >= 1, pltpu.roll(h, 1, 0), c)
    return x + prev, new_c


def _lru_scan(a_f, x_f, a_r, x_r, bwd, layer):
    s = a_f.shape[0]
    ts = min(1024, s // 2)
    cb = 512
    nt = s // ts
    ng = ts // 8

    def body(af_ref, xf_ref, ar_ref, xr_ref, of_ref, or_ref, cf, cr):
        @pl.when(pl.program_id(1) == 0)
        def _():
            cf[...] = jnp.zeros_like(cf)
            cr[...] = jnp.zeros_like(cr)

        def step(j, carry):
            c_f, c_r = carry
            rf = pl.multiple_of(j * 8, 8)
            rr = pl.multiple_of((ng - 1 - j) * 8, 8)
            o, c_f = _scan_group(af_ref[pl.ds(rf, 8), :], xf_ref[pl.ds(rf, 8), :], c_f, False, bwd)
            of_ref[pl.ds(rf, 8), :] = o
            o, c_r = _scan_group(ar_ref[pl.ds(rr, 8), :], xr_ref[pl.ds(rr, 8), :], c_r, True, bwd)
            or_ref[pl.ds(rr, 8), :] = o
            return c_f, c_r

        c_f, c_r = lax.fori_loop(0, ng, step, (cf[0:1, :], cr[0:1, :]), unroll=2)
        cf[...] = jnp.broadcast_to(c_f, cf.shape)
        cr[...] = jnp.broadcast_to(c_r, cr.shape)

    fwd = pl.BlockSpec((ts, cb), lambda c, t: (t, c))
    rev = pl.BlockSpec((ts, cb), lambda c, t: (nt - 1 - t, c))
    return pl.pallas_call(
        body, name=f"lru_scan_{'bwd' if bwd else 'fwd'}_l{layer}", grid=(D // cb, nt),
        in_specs=[fwd, fwd, rev, rev], out_specs=[fwd, rev],
        out_shape=[SDS((s, D), F32)] * 2,
        scratch_shapes=[pltpu.VMEM((8, cb), F32), pltpu.VMEM((8, cb), F32)],
        compiler_params=_cp("parallel", "arbitrary"))(a_f, x_f, a_r, x_r)


def _merge_outproj(x, ya, h0, h1, z, wo, layer, tm):
    s = x.shape[0]

    def body(x_ref, ya_ref, h0_ref, h1_ref, zg_ref, za_ref, zb_ref, wo_ref, x1_ref, mg_ref):
        gg, _ = _gelu(zg_ref[...].astype(F32))
        yb = (h0_ref[...] + h1_ref[...]) * gg
        m32 = (_sigmoid(za_ref[...].astype(F32)) * ya_ref[...].astype(F32)
               + _sigmoid(zb_ref[...].astype(F32)) * yb)
        mg_ref[...] = m32.T.astype(BF16)
        x1_ref[...] = x_ref[...] + _dot(m32.astype(BF16), wo_ref[...])

    tile = pl.BlockSpec((tm, D), lambda i: (i, 0))
    return pl.pallas_call(
        body, name=f"merge_outproj_l{layer}", grid=(s // tm,),
        in_specs=[tile, tile, tile, tile]
        + [pl.BlockSpec((tm, D), lambda i, c=c: (i, c)) for c in (3, 4, 5)] + [_full((D, D))],
        out_specs=[tile, pl.BlockSpec((D, tm), lambda i: (0, i))], out_shape=[SDS((s, D), F32), SDS((D, s), BF16)],
        compiler_params=_cp("parallel"))(x, ya, h0, h1, z, z, z, wo)


def _ffn_fwd(x1, g, wfi, wfo, layer, tm):
    s = x1.shape[0]

    def body(x_ref, g_ref, wi_ref, wo_ref, x2_ref, gu_ref, h_ref):
        x = x_ref[...]
        h = _rms_fwd(x, g_ref[...]).astype(BF16)
        h_ref[...] = h
        acc = x
        for k in range(4):
            gate = _dot_nt(h, wi_ref[0, k])
            up = _dot_nt(h, wi_ref[1, k])
            gu_ref[0, k] = gate.astype(BF16)
            gu_ref[1, k] = up.astype(BF16)
            acc = acc + _dot((gate * _sigmoid(gate) * up).astype(BF16), wo_ref[k])
        x2_ref[...] = acc

    tile = pl.BlockSpec((tm, D), lambda i: (i, 0))
    return pl.pallas_call(
        body, name=f"ffn_fwd_l{layer}", grid=(s // tm,),
        in_specs=[tile, _full((1, D)), _resident((2, 4, FF_BLK, D)), _resident((4, FF_BLK, D))],
        out_specs=[tile, pl.BlockSpec((2, 4, tm, FF_BLK), lambda i: (0, 0, i, 0)), tile],
        out_shape=[SDS((s, D), F32), SDS((2, 4, s, FF_BLK), BF16), SDS((s, D), BF16)],
        compiler_params=_cp("parallel"))(x1, g, wfi, wfo)


def _loss_head(x, g, tgt, tm):
    s = x.shape[0]

    def body(x_ref, g_ref, t_ref, dx_ref, loss_ref, dg_ref):
        @pl.when(pl.program_id(0) == 0)
        def _():
            loss_ref[...] = jnp.zeros_like(loss_ref)
            dg_ref[...] = jnp.zeros_like(dg_ref)

        x = x_ref[...]
        gv = g_ref[...]
        e = _rms_fwd(x, gv) - t_ref[...]
        rows = jnp.sum(e * e, axis=-1, keepdims=True)
        loss_ref[...] += (0.5 / D) * jnp.sum(rows, axis=0, keepdims=True)
        dx, dg = _rms_bwd(x, gv, e * (1.0 / D))
        dx_ref[...] = dx
        _add_rows128(dg_ref, dg)

    tile = pl.BlockSpec((tm, D), lambda i: (i, 0))
    return pl.pallas_call(
        body, name="loss_head", grid=(s // tm,),
        in_specs=[tile, _full((1, D)), tile],
        out_specs=[tile, _full((1, 1)), _full((LANE_ROWS, HD))],
        out_shape=[SDS((s, D), F32), SDS((1, 1), F32), SDS((LANE_ROWS, HD), F32)],
        compiler_params=_cp("arbitrary"))(x, g, tgt)


def _ffn_bwd_act(dx2, wfo, gu, layer, tm):
    s = dx2.shape[0]

    def body(dx_ref, wo_ref, gu_ref, ff_ref, dgu_ref):
        dxb = dx_ref[...].astype(BF16)
        for k in range(4):
            dff = _dot_nt(dxb, wo_ref[k])
            gate = gu_ref[0, k].astype(F32)
            up = gu_ref[1, k].astype(F32)
            sg = _sigmoid(gate)
            sl = gate * sg
            ff_ref[k] = (sl * up).astype(BF16)
            dgu_ref[0, k] = (dff * up * (sg * (1.0 + gate * (1.0 - sg)))).astype(BF16)
            dgu_ref[1, k] = (dff * sl).astype(BF16)

    blk = pl.BlockSpec((2, 4, tm, FF_BLK), lambda i: (0, 0, i, 0))
    return pl.pallas_call(
        body, name=f"ffn_bwd_act_l{layer}", grid=(s // tm,),
        in_specs=[pl.BlockSpec((tm, D), lambda i: (i, 0)), _resident((4, FF_BLK, D)), blk],
        out_specs=[pl.BlockSpec((4, tm, FF_BLK), lambda i: (0, i, 0)), blk],
        out_shape=[SDS((4, s, FF_BLK), BF16), SDS((2, 4, s, FF_BLK), BF16)],
        compiler_params=_cp("parallel"))(dx2, wfo, gu)


def _mm_nt_rms_bwd(a, a_spec, a_blocks, w, w_is_transposed, x, g, dres, name, tm):
    s = x.shape[0]

    def body(a_ref, w_ref, x_ref, g_ref, dres_ref, dx_ref, dg_ref):
        @pl.when(pl.program_id(0) == 0)
        def _():
            dg_ref[...] = jnp.zeros_like(dg_ref)

        dh = None
        for k, blk in enumerate(a_blocks(a_ref)):
            part = _dot(blk, w_ref[k]) if w_is_transposed else _dot_nt(blk, w_ref[k])
            dh = part if dh is None else dh + part
        dx, dg = _rms_bwd(x_ref[...], g_ref[...], dh)
        dx_ref[...] = dres_ref[...] + dx
        _add_rows128(dg_ref, dg)

    tile = pl.BlockSpec((tm, D), lambda i: (i, 0))
    return pl.pallas_call(
        body, name=name, grid=(s // tm,),
        in_specs=[a_spec, _resident(w.shape), tile, _full((1, D)), tile],
        out_specs=[tile, _full((LANE_ROWS, HD))], out_shape=[SDS((s, D), F32), SDS((LANE_ROWS, HD), F32)],
        compiler_params=_cp("arbitrary"))(a, w, x, g, dres)


def _mm_tn(a, a_spec, b, b_spec, nb, out_shape, out_spec, name, a_is_transposed=True, after=None):
    def body(a_ref, b_ref, *rest):
        o_ref = rest[-1]
        bb = b_ref[...].astype(BF16)
        o_ref[...] = (_dot(a_ref[...], bb) if a_is_transposed else _dot_tn(a_ref[...], bb)).astype(BF16)

    deps = [] if after is None else [after]
    return pl.pallas_call(
        body, name=name, grid=(nb,), in_specs=[a_spec, b_spec] + [_ANY] * len(deps), out_specs=out_spec,
        out_shape=SDS(out_shape, BF16), compiler_params=_cp("parallel"))(a, b, *deps)


def _outproj_bwd_merge(dx1, wo, ya, h0, h1, z, layer, tm):
    s = dx1.shape[0]

    def body(dx_ref, wo_ref, ya_ref, h0_ref, h1_ref, zg_ref, za_ref, zb_ref, dz_ref, dya_ref, dh_ref):
        dm = _dot_nt(dx_ref[...].astype(BF16), wo_ref[...])
        sa = _sigmoid(za_ref[...].astype(F32))
        sb = _sigmoid(zb_ref[...].astype(F32))
        zg = zg_ref[...].astype(F32)
        gg, tg = _gelu(zg)
        hs = h0_ref[...] + h1_ref[...]
        dyb = dm * sb
        dya_ref[...] = (dm * sa).astype(BF16)
        dh_ref[...] = dyb * gg
        dz_ref[:, 0:D] = (dyb * hs * _gelu_grad(zg, tg)).astype(BF16)
        dz_ref[:, D:2 * D] = (dm * ya_ref[...].astype(F32) * (sa * (1.0 - sa))).astype(BF16)
        dz_ref[:, 2 * D:3 * D] = (dm * (hs * gg) * (sb * (1.0 - sb))).astype(BF16)

    tile = pl.BlockSpec((tm, D), lambda i: (i, 0))
    return pl.pallas_call(
        body, name=f"outproj_bwd_merge_l{layer}", grid=(s // tm,),
        in_specs=[tile, _full((D, D)), tile, tile, tile]
        + [pl.BlockSpec((tm, D), lambda i, c=c: (i, c)) for c in (3, 4, 5)],
        out_specs=[pl.BlockSpec((tm, 3 * D), lambda i: (i, 1)), tile, tile],
        out_shape=[SDS((s, N_IN), BF16), SDS((s, D), BF16), SDS((s, D), F32)],
        compiler_params=_cp("parallel"))(dx1, wo, ya, h0, h1, z, z, z)


def _lru_gates_bwd(xcb, gates, h0, h1, g0, g1, wr, wi, lam, layer, tm):
    s = xcb.shape[0]
    nt = s // tm

    def body(xc_ref, r0_ref, i0_ref, r1_ref, i1_ref, h0p_ref, h0_ref, h1_ref, h1n_ref, g0_ref, g1_ref,
             wr_ref, wi_ref, lam_ref, dxc_ref, dwr_ref, dwi_ref, dbr_ref, dbi_ref, dlam_ref):
        i = pl.program_id(0)
        fp, fn = _halo_flags(nt)

        @pl.when(i == 0)
        def _():
            for r in (dwr_ref, dwi_ref, dbr_ref, dbi_ref, dlam_ref):
                r[...] = jnp.zeros_like(r)

        xb = xc_ref[...]
        xc = xb.astype(F32)
        zeros8 = jnp.zeros((8, D), F32)
        h_prev = _taps(h0p_ref[...] * fp, h0_ref[...], zeros8, tm)[1]
        h_next = _taps(zeros8, h1_ref[...], h1n_ref[...] * fn, tm)[3]
        dxc = jnp.zeros((tm, D), F32)
        for d, (g_ref, hsh, r_ref, i_ref) in enumerate(((g0_ref, h_prev, r0_ref, i0_ref),
                                                        (g1_ref, h_next, r1_ref, i1_ref))):
            sp = _softplus(-lam_ref[d:d + 1, :])
            r = r_ref[...].astype(F32)
            ig = i_ref[...].astype(F32)
            a, q = _lru_decay(r, sp)
            rmult = jnp.where(q > 0.0, lax.rsqrt(jnp.where(q > 0.0, q, 1.0)), 0.0)
            mult = q * rmult
            db = g_ref[...]
            da = db * hsh
            dmult = db * (ig * xc)
            di = db * (mult * xc)
            dxc = dxc + db * (mult * ig)
            dla = da * a - dmult * (a * a * rmult)
            dsp_dlam = -_sigmoid(-lam_ref[d:d + 1, :])
            _add_rows128(dlam_ref, jnp.sum(dla * r, axis=0, keepdims=True) * ((-LRU_C) * dsp_dlam), d * LANE_ROWS)
            dpr = dla * sp * (-LRU_C) * (r * (1.0 - r))
            dpi = di * (ig * (1.0 - ig))
            _add_rows128(dbr_ref, jnp.sum(dpr, axis=0, keepdims=True), d * LANE_ROWS)
            _add_rows128(dbi_ref, jnp.sum(dpi, axis=0, keepdims=True), d * LANE_ROWS)
            dprb = dpr.astype(BF16)
            dpib = dpi.astype(BF16)
            parts = []
            for h in range(HEADS):
                cs = slice(h * HD, (h + 1) * HD)
                dwr_ref[d, h] += _dot_tn(xb[:, cs], dprb[:, cs])
                dwi_ref[d, h] += _dot_tn(xb[:, cs], dpib[:, cs])
                parts.append(_dot_nt(dprb[:, cs], wr_ref[d, h]) + _dot_nt(dpib[:, cs], wi_ref[d, h]))
            dxc = dxc + jnp.concatenate(parts, axis=1)
        dxc_ref[...] = dxc.astype(BF16)

    tile = pl.BlockSpec((tm, D), lambda i: (i, 0))
    hp, hc, hn = _halo_specs(tm, s, 0)
    wspec = _full((2, HEADS, HD, HD))
    vspec = _full((2 * LANE_ROWS, HD))
    return pl.pallas_call(
        body, name=f"lru_gates_bwd_l{layer}", grid=(nt,),
        in_specs=[tile] * 5 + [hp, hc, hc, hn, tile, tile, wspec, wspec, _full((2, D))],
        out_specs=[tile, wspec, wspec, vspec, vspec, vspec],
        out_shape=[SDS((s, D), BF16), SDS((2, HEADS, HD, HD), F32), SDS((2, HEADS, HD, HD), F32)]
        + [SDS((2 * LANE_ROWS, HD), F32)] * 3,
        compiler_params=_cp("arbitrary"))(xcb, *gates, h0, h0, h1, h1, g0, g1, wr, wi, lam)


def _conv_bwd(dz, dxc, z, cw, layer, tm):
    s = z.shape[0]
    nt = s // tm

    def body(dz_in, dp_ref, dc_ref, dn_ref, zp_ref, zc_ref, zn_ref, cw_ref, dz_ref, dcw_ref, dcb_ref):
        del dz_in
        fp, fn = _halo_flags(nt)

        @pl.when(pl.program_id(0) == 0)
        def _():
            dcw_ref[...] = jnp.zeros_like(dcw_ref)
            dcb_ref[...] = jnp.zeros_like(dcb_ref)

        dxc_halo = _halo_load(dp_ref, dc_ref, dn_ref, fp, fn)
        dxc = dxc_halo[1]
        dm2, dm1, _, dp1, _ = _taps(*dxc_halo, tm)
        dz_ref[...] = (cw_ref[0:1, :] * dp1 + cw_ref[1:2, :] * dxc + cw_ref[2:3, :] * dm1
                       + cw_ref[3:4, :] * dm2).astype(BF16)
        _, zm1, z0, zp1, zp2 = _taps(*_halo_load(zp_ref, zc_ref, zn_ref, fp, fn), tm)
        for k, zt in enumerate((zm1, z0, zp1, zp2)):
            _add_rows128(dcw_ref, jnp.sum(dxc * zt, axis=0, keepdims=True), k * LANE_ROWS)
        _add_rows128(dcb_ref, jnp.sum(dxc, axis=0, keepdims=True))

    return pl.pallas_call(
        body, name=f"conv_bwd_l{layer}", grid=(nt,),
        in_specs=[pl.BlockSpec(memory_space=pl.ANY), *_halo_specs(tm, s, 0, 16), *_halo_specs(tm, s, 2, 16),
                  _full((4, D))],
        out_specs=[pl.BlockSpec((tm, D), lambda i: (i, 2)), _full((4 * LANE_ROWS, HD)), _full((LANE_ROWS, HD))],
        out_shape=[SDS((s, N_IN), BF16), SDS((4 * LANE_ROWS, HD), F32), SDS((LANE_ROWS, HD), F32)],
        input_output_aliases={0: 0},
        compiler_params=_cp("arbitrary"))(dz, dxc, dxc, dxc, z, z, z, cw)


def _gmlp_bwd(dz, z, dya, lng, lnb, ws, wst, bsb, layer, tm):
    s = z.shape[0]
    nt = s // tm

    def body(dz_in, zu_ref, zv_ref, dya_ref, lng_ref, lnb_ref, ws_ref, wst_ref, bsb_ref,
             dz_ref, dws_ref, dbs_ref, dlng_ref, dlnb_ref, du_s, dv_s, dbs_acc):
        del dz_in
        i = pl.program_id(0)

        @pl.when(i == 0)
        def _():
            for r in (dws_ref, dlng_ref, dlnb_ref, dbs_acc):
                r[...] = jnp.zeros_like(r)

        zu = zu_ref[...].astype(F32)
        zv = zv_ref[...].astype(F32)
        u, tu = _gelu(zu)
        gv, tv = _gelu(zv)
        xc = gv - jnp.mean(gv, axis=-1, keepdims=True)
        rstd = lax.rsqrt(jnp.mean(xc * xc, axis=-1, keepdims=True) + EPS)
        xh = xc * rstd
        lng_v = lng_ref[...]
        vb = (xh * lng_v + lnb_ref[...]).astype(BF16)
        dya = dya_ref[...].astype(F32)
        for c in range(tm // HD):
            rs = slice(c * HD, (c + 1) * HD)
            for g in range(HEADS):
                cs = slice(g * HD, (g + 1) * HD)
                vblk = vb[rs, cs]
                mixed = _dot(ws_ref[g], vblk) + bsb_ref[g]
                du_s[rs, cs] = dya[rs, cs] * mixed
                dmx = dya[rs, cs] * u[rs, cs]
                dbs_acc[g] += dmx
                dmxb = dmx.astype(BF16)
                dws_ref[g] += _dot_nt(dmxb, vblk)
                dv_s[rs, cs] = _dot(wst_ref[g], dmxb)
        dv = dv_s[...]
        _add_rows128(dlng_ref, jnp.sum(dv * xh, axis=0, keepdims=True))
        _add_rows128(dlnb_ref, jnp.sum(dv, axis=0, keepdims=True))
        dxh = dv * lng_v
        dgv = rstd * (dxh - jnp.mean(dxh, axis=-1, keepdims=True)
                      - xh * jnp.mean(dxh * xh, axis=-1, keepdims=True))
        dz_ref[:, 0:D] = (du_s[...] * _gelu_grad(zu, tu)).astype(BF16)
        dz_ref[:, D:2 * D] = (dgv * _gelu_grad(zv, tv)).astype(BF16)

        @pl.when(i == nt - 1)
        def _():
            for g in range(HEADS):
                dbs_ref[g:g + 1, :] = jnp.sum(dbs_acc[g].T, axis=0, keepdims=True)

    tile = pl.BlockSpec((tm, D), lambda i: (i, 0))
    wspec = _full((HEADS, HD, HD))
    return pl.pallas_call(
        body, name=f"gmlp_bwd_l{layer}", grid=(nt,),
        in_specs=[pl.BlockSpec(memory_space=pl.ANY), tile, pl.BlockSpec((tm, D), lambda i: (i, 1)), tile,
                  _full((1, D)), _full((1, D)), wspec, wspec, wspec],
        out_specs=[pl.BlockSpec((tm, 2 * D), lambda i: (i, 0)), wspec, _full((HEADS, HD)),
                   _full((LANE_ROWS, HD)), _full((LANE_ROWS, HD))],
        out_shape=[SDS((s, N_IN), BF16), SDS((HEADS, HD, HD), F32), SDS((HEADS, HD), F32),
                   SDS((LANE_ROWS, HD), F32), SDS((LANE_ROWS, HD), F32)],
        scratch_shapes=[pltpu.VMEM((tm, D), F32), pltpu.VMEM((tm, D), F32), pltpu.VMEM((HEADS, HD, HD), F32)],
        input_output_aliases={0: 0},
        compiler_params=_cp("arbitrary"))(dz, z, z, dya, lng, lnb, ws, wst, bsb)


def _me():
    return lax.axis_index("x"), lax.axis_index("y"), lax.axis_index("c")


def _peer(m):
    x, y, c = _me()
    px = 1 - x if m & 4 else x
    py = 1 - y if m & 2 else y
    pc = 1 - c if m & 1 else c
    return (px, py, pc), 4 * px + 2 * py + pc


_ANY = pl.BlockSpec(memory_space=pl.ANY)
_EXCHANGE_SEMS = [pltpu.SemaphoreType.DMA((N_DEV - 1,)), pltpu.SemaphoreType.DMA((N_DEV - 1,)), pltpu.SemaphoreType.DMA(())]


def _all_gather(v, after, name):
    def body(v_ref, after_ref, o_ref, send_sems, recv_sems, local_sem):
        del after_ref
        x, y, c = _me()
        me = 4 * x + 2 * y + c
        local = pltpu.make_async_copy(v_ref, o_ref.at[me], local_sem)
        local.start()
        sends = []
        for m in range(1, N_DEV):
            dev, _ = _peer(m)
            cp = pltpu.make_async_remote_copy(v_ref, o_ref.at[me], send_sems.at[m - 1], recv_sems.at[m - 1],
                                              device_id=dev, device_id_type=pl.DeviceIdType.MESH)
            cp.start()
            sends.append(cp)
        for m in range(1, N_DEV):
            dev, blk = _peer(m)
            pltpu.make_async_remote_copy(v_ref, o_ref.at[blk], send_sems.at[m - 1], recv_sems.at[m - 1],
                                         device_id=dev, device_id_type=pl.DeviceIdType.MESH).wait_recv()
        for cp in sends:
            cp.wait_send()
        local.wait()

    return pl.pallas_call(
        body, name=name, in_specs=[_ANY, _ANY], out_specs=_ANY,
        out_shape=SDS((N_DEV,) + v.shape, v.dtype), scratch_shapes=_EXCHANGE_SEMS)(v, after)


_HBM = pl.BlockSpec(memory_space=pltpu.HBM)
_SEM = pl.BlockSpec(memory_space=pltpu.SEMAPHORE)
_EFFECT = pltpu.CompilerParams(has_side_effects=pltpu.SideEffectType.DATAFLOW_SIDE_EFFECTING)
_PEER_SEMS = pltpu.SemaphoreType.DMA((N_DEV - 1,))


def _in_hbm(a):
    return pltpu.with_memory_space_constraint(a, pltpu.HBM)


def _remote(src, dst, send_sems, recv_sems, m):
    dev, _ = _peer(m)
    return pltpu.make_async_remote_copy(src, dst, send_sems.at[m - 1], recv_sems.at[m - 1],
                                        device_id=dev, device_id_type=pl.DeviceIdType.MESH)


def _gather_start(lands, after, name):
    n = len(lands)

    def body(*refs):
        land = refs[:n]
        sems = refs[n + 1:3 * n + 1]
        token = refs[-1]
        x, y, c = _me()
        me = 4 * x + 2 * y + c
        for t in range(n):
            for m in range(1, N_DEV):
                _remote(land[t].at[me], land[t].at[me], sems[2 * t], sems[2 * t + 1], m).start()
        token[...] = jnp.zeros_like(token)

    res = pl.pallas_call(
        body, name=name, in_specs=[_HBM] * n + [_ANY],
        out_specs=[_SEM] * (2 * n) + [_HBM] * n + [pl.BlockSpec(memory_space=pltpu.VMEM)],
        out_shape=[_PEER_SEMS] * (2 * n) + [pltpu.HBM(a.shape, a.dtype) for a in lands] + [SDS((8, 128), F32)],
        input_output_aliases={t: 2 * n + t for t in range(n)},
        compiler_params=_EFFECT)(*[_in_hbm(a) for a in lands], after)
    return [(res[2 * t], res[2 * t + 1], res[2 * n + t]) for t in range(n)], res[-1]


def _gather_wait(handle, after, name):
    send_sems, recv_sems, land = handle

    def body(land_ref, ssem, rsem, after_ref, out_ref):
        del after_ref, out_ref
        x, y, c = _me()
        me = 4 * x + 2 * y + c
        for m in range(1, N_DEV):
            _, blk = _peer(m)
            cp = _remote(land_ref.at[me], land_ref.at[blk], ssem, rsem, m)
            cp.wait_send()
            cp.wait_recv()

    return pl.pallas_call(
        body, name=name, in_specs=[_HBM, _SEM, _SEM, _ANY], out_specs=_HBM,
        out_shape=pltpu.HBM(land.shape, land.dtype), input_output_aliases={0: 0},
        compiler_params=_EFFECT)(land, send_sems, recv_sems, after)


FIRST_STAGE = (1, 2, 4, 6)
RELAYED = (2, 4, 6)
OTHER_CORE = 1


def _stage_copy(src, dst, send_sems, recv_sems, k, m):
    dev, _ = _peer(m)
    return pltpu.make_async_remote_copy(src, dst, send_sems.at[k], recv_sems.at[k],
                                        device_id=dev, device_id_type=pl.DeviceIdType.MESH)


def _gather2_start(lands, after, name):
    n = len(lands)

    def body(*refs):
        land = refs[:n]
        sems = refs[n + 1:3 * n + 1]
        token = refs[-1]
        x, y, c = _me()
        me = 4 * x + 2 * y + c
        for t in range(n):
            for k, m in enumerate(FIRST_STAGE):
                _stage_copy(land[t].at[me], land[t].at[me], sems[2 * t], sems[2 * t + 1], k, m).start()
        token[...] = jnp.zeros_like(token)

    stage_sems = pltpu.SemaphoreType.DMA((len(FIRST_STAGE),))
    res = pl.pallas_call(
        body, name=name, in_specs=[_HBM] * n + [_ANY],
        out_specs=[_SEM] * (2 * n) + [_HBM] * n + [pl.BlockSpec(memory_space=pltpu.VMEM)],
        out_shape=[stage_sems] * (2 * n) + [pltpu.HBM(a.shape, a.dtype) for a in lands] + [SDS((8, 128), F32)],
        input_output_aliases={t: 2 * n + t for t in range(n)},
        compiler_params=_EFFECT)(*[_in_hbm(a) for a in lands], after)
    return [(res[2 * t], res[2 * t + 1], res[2 * n + t]) for t in range(n)], res[-1]


def _gather2_relay(handles, after, name):
    n = len(handles)

    def body(*refs):
        land, send1, recv1 = refs[:n], refs[n:2 * n], refs[2 * n:3 * n]
        sems = refs[3 * n + 1:5 * n + 1]
        token = refs[-1]
        x, y, c = _me()
        me = 4 * x + 2 * y + c
        for t in range(n):
            for j, m in enumerate(RELAYED):
                _, blk = _peer(m)
                _stage_copy(land[t].at[me], land[t].at[blk], send1[t], recv1[t], 1 + j, m).wait_recv()
                _stage_copy(land[t].at[blk], land[t].at[blk], sems[2 * t], sems[2 * t + 1], j, OTHER_CORE).start()
        token[...] = jnp.zeros_like(token)

    relay_sems = pltpu.SemaphoreType.DMA((len(RELAYED),))
    lands = [h[2] for h in handles]
    res = pl.pallas_call(
        body, name=name, in_specs=[_HBM] * n + [_SEM] * (2 * n) + [_ANY],
        out_specs=[_SEM] * (2 * n) + [_HBM] * n + [pl.BlockSpec(memory_space=pltpu.VMEM)],
        out_shape=[relay_sems] * (2 * n) + [pltpu.HBM(a.shape, a.dtype) for a in lands] + [SDS((8, 128), F32)],
        input_output_aliases={t: 2 * n + t for t in range(n)},
        compiler_params=_EFFECT)(*lands, *[h[0] for h in handles], *[h[1] for h in handles], after)
    return [(h[0], h[1], res[2 * t], res[2 * t + 1], res[2 * n + t]) for t, h in enumerate(handles)], res[-1]


def _gather2_wait(handle, after, name):
    send1, recv1, send2, recv2, land = handle

    def body(land_ref, s1, r1, s2, r2, after_ref, out_ref):
        del after_ref, out_ref
        x, y, c = _me()
        me = 4 * x + 2 * y + c
        _, other = _peer(OTHER_CORE)
        _stage_copy(land_ref.at[me], land_ref.at[other], s1, r1, 0, OTHER_CORE).wait_recv()
        for k, m in enumerate(FIRST_STAGE):
            _stage_copy(land_ref.at[me], land_ref.at[me], s1, r1, k, m).wait_send()
        for j, m in enumerate(RELAYED):
            _, mine = _peer(m)
            _, theirs = _peer(m ^ OTHER_CORE)
            _stage_copy(land_ref.at[mine], land_ref.at[mine], s2, r2, j, OTHER_CORE).wait_send()
            _stage_copy(land_ref.at[mine], land_ref.at[theirs], s2, r2, j, OTHER_CORE).wait_recv()

    return pl.pallas_call(
        body, name=name, in_specs=[_HBM] + [_SEM] * 4 + [_ANY], out_specs=_HBM,
        out_shape=pltpu.HBM(land.shape, land.dtype), input_output_aliases={0: 0},
        compiler_params=_EFFECT)(land, send1, recv1, send2, recv2, after)


def _exchange_start(ps, name):
    n = len(ps)

    def body(*refs):
        p = refs[:n]
        got = refs[n:2 * n]
        sems = refs[2 * n:5 * n]
        token = refs[-1]
        x, y, c = _me()
        me = 4 * x + 2 * y + c
        for t in range(n):
            pltpu.make_async_copy(p[t].at[me], got[t].at[me], sems[3 * t + 2]).start()
            for m in range(1, N_DEV):
                _, blk = _peer(m)
                _remote(p[t].at[blk], got[t].at[me], sems[3 * t], sems[3 * t + 1], m).start()
        token[...] = jnp.zeros_like(token)

    res = pl.pallas_call(
        body, name=name, in_specs=[_HBM] * (2 * n),
        out_specs=[_SEM] * (3 * n) + [_HBM] * (2 * n) + [pl.BlockSpec(memory_space=pltpu.VMEM)],
        out_shape=[_PEER_SEMS, _PEER_SEMS, pltpu.SemaphoreType.DMA(())] * n
        + [pltpu.HBM(a.shape, a.dtype) for a in ps] * 2 + [SDS((8, 128), F32)],
        input_output_aliases={t: 3 * n + t for t in range(2 * n)},
        compiler_params=_EFFECT)(*[_in_hbm(a) for a in ps], *[_in_hbm(lax.empty(a.shape, a.dtype)) for a in ps])
    return [(res[3 * t], res[3 * t + 1], res[3 * t + 2], res[3 * n + t], res[4 * n + t]) for t in range(n)], res[-1]


def _exchange_wait(handle, after, name):
    send_sems, recv_sems, local_sem, p, got = handle

    def body(p_ref, got_ref, ssem, rsem, lsem, after_ref, p_out, got_out):
        del after_ref, p_out, got_out
        x, y, c = _me()
        me = 4 * x + 2 * y + c
        pltpu.make_async_copy(p_ref.at[me], got_ref.at[me], lsem).wait()
        for m in range(1, N_DEV):
            _, blk = _peer(m)
            cp = _remote(p_ref.at[blk], got_ref.at[blk], ssem, rsem, m)
            cp.wait_send()
            cp.wait_recv()

    return pl.pallas_call(
        body, name=name, in_specs=[_HBM, _HBM, _SEM, _SEM, _SEM, _ANY], out_specs=[_HBM, _HBM],
        out_shape=[pltpu.HBM(p.shape, p.dtype), pltpu.HBM(got.shape, got.dtype)],
        input_output_aliases={0: 0, 1: 1}, compiler_params=_EFFECT)(p, got, send_sems, recv_sems, local_sem, after)[1]


def _cast_into_slot(w, layer, me1, name):
    _, r, c = w.shape
    tr = next(t for t in (256, 352, r) if r % t == 0)

    def body(me_ref, w_ref, o_ref):
        del me_ref
        o_ref[...] = w_ref[...].astype(BF16)

    return pl.pallas_call(
        body, name=name,
        grid_spec=pltpu.PrefetchScalarGridSpec(
            num_scalar_prefetch=1, grid=(r // tr,),
            in_specs=[pl.BlockSpec((None, tr, c), lambda i, me: (layer, i, 0))],
            out_specs=pl.BlockSpec((None, tr, c), lambda i, me: (me[0], i, 0))),
        out_shape=SDS((N_DEV, r, c), BF16), compiler_params=_cp("arbitrary"))(me1, w)


def _sum8_into_slot(p, me1, name):
    _, r, c = p.shape

    def body(me_ref, p_ref, o_ref):
        del me_ref
        acc = p_ref[0]
        for k in range(1, N_DEV):
            acc = acc + p_ref[k]
        o_ref[...] = acc

    return pl.pallas_call(
        body, name=name,
        grid_spec=pltpu.PrefetchScalarGridSpec(
            num_scalar_prefetch=1, grid=(1,),
            in_specs=[pl.BlockSpec(p.shape, lambda i, me: (0, 0, 0))],
            out_specs=pl.BlockSpec((None, r, c), lambda i, me: (me[0], 0, 0))),
        out_shape=SDS(p.shape, F32), compiler_params=_cp("arbitrary"))(me1, p)


def _adamw(w, g, m, v):
    m = ADAM_B1 * m + (1.0 - ADAM_B1) * g
    v = ADAM_B2 * v + (1.0 - ADAM_B2) * (g * g)
    m_hat = m / (1.0 - ADAM_B1 ** ADAM_STEP)
    v_hat = v / (1.0 - ADAM_B2 ** ADAM_STEP)
    delta = -ADAM_LR * (m_hat / (jnp.sqrt(v_hat) + ADAM_EPS) + ADAM_WD * w)
    return delta, m, v


def _adam_shard(parts, w, m, v, layer, prev, name):
    _, r, c = parts.shape
    tr = next(t for t in (256, 352, r) if r % t == 0)
    n_prev = 0 if prev is None else 4

    def body(*refs):
        p_ref, w_ref, m_ref, v_ref = refs[:4]
        g_ref, d_ref, nm_ref, nv_ref = refs[4 + n_prev:]
        g = p_ref[0].astype(F32)
        for k in range(1, N_DEV):
            g = g + p_ref[k].astype(F32)
        delta, nm, nv = _adamw(w_ref[...], g, m_ref[...], v_ref[...])
        g_ref[...] = g
        d_ref[...] = delta
        nm_ref[...] = nm
        nv_ref[...] = nv

    blk = pl.BlockSpec((None, tr, c), lambda i: (layer, i, 0))
    return pl.pallas_call(
        body, name=name, grid=(r // tr,),
        in_specs=[pl.BlockSpec((N_DEV, tr, c), lambda i: (0, i, 0)), blk, blk, blk] + [_ANY] * n_prev,
        out_specs=[blk] * 4, out_shape=[SDS(w.shape, F32)] * 4,
        input_output_aliases={4 + k: k for k in range(n_prev)},
        compiler_params=_cp("parallel"))(parts, w, m, v, *(prev or ()))


SMALL_MATRICES = [("lru_w_r", 2048), ("lru_w_i", 2048), ("gmlp_w_s", 1024)]
SMALL_VECTORS = [("norm1_g", 8), ("gmlp_ln_g", 8), ("gmlp_ln_b", 8), ("gmlp_b_s", 8), ("conv_w", 32), ("conv_b", 8),
                 ("lru_b_r", 16), ("lru_b_i", 16), ("lru_lambda", 16), ("norm2_g", 8), ("final_g", 8)]
SMALL_VECTOR_ROW0 = sum(n for _, n in SMALL_MATRICES)
SMALL_VECTOR_BLOCK = 256
SMALL_ROWS = SMALL_VECTOR_ROW0 + SMALL_VECTOR_BLOCK


def _pack_small(small):
    parts = [small[k] for k, _ in SMALL_MATRICES]
    parts += [small[k] if k in small else jnp.zeros((n, HD), F32) for k, n in SMALL_VECTORS]
    flat = jnp.concatenate(parts)
    return jnp.pad(flat, ((0, SMALL_ROWS - flat.shape[0]), (0, 0))).reshape(N_DEV, SMALL_ROWS // N_DEV, HD)


def _adam_matrix(g0, g1, w, m, v, row0, name):
    _, rows, _ = w.shape

    def body(g0_ref, g1_ref, w_ref, m_ref, v_ref, g_ref, d_ref, nm_ref, nv_ref):
        for l, src in enumerate((g0_ref, g1_ref)):
            g = src[...]
            delta, nm, nv = _adamw(w_ref[l], g, m_ref[l], v_ref[l])
            g_ref[l] = g
            d_ref[l] = delta
            nm_ref[l] = nm
            nv_ref[l] = nv

    gspec = pl.BlockSpec((rows, HD), lambda i: (row0 // rows, 0))
    return pl.pallas_call(body, name=name, grid=(1,), in_specs=[gspec, gspec] + [_full(w.shape)] * 3,
                          out_specs=[_full(w.shape)] * 4, out_shape=[SDS(w.shape, F32)] * 4,
                          compiler_params=_cp("arbitrary"))(g0, g1, w, m, v)


def _adam_vectors(g0, g1, dg1_parts, me1, ws, ms, vs):
    names = [k for k, _ in SMALL_VECTORS]
    n = len(names)

    def lanes(rows8):
        return jnp.concatenate([rows8[k:k + 1, :] for k in range(LANE_ROWS)], axis=1)

    def body(me_ref, g0_ref, g1_ref, dg1_ref, *refs):
        w_refs, m_refs, v_refs = refs[:n], refs[n:2 * n], refs[2 * n:3 * n]
        outs = refs[3 * n:]
        me = me_ref[0]
        g_refs = (g0_ref, g1_ref)

        def emit(i, idx, g):
            delta, nm, nv = _adamw(w_refs[i][idx], g, m_refs[i][idx], v_refs[i][idx])
            for j, val in enumerate((g, delta, nm, nv)):
                outs[4 * i + j][idx] = val

        off = 0
        for i, (name, rows) in enumerate(SMALL_VECTORS):
            for l in range(2):
                row = (slice(l, l + 1), slice(None))
                if name == "final_g":
                    if l == 1:
                        emit(i, (slice(0, 1), slice(None)), lanes(g1_ref[off:off + rows, :]))
                elif name == "norm1_g":
                    if l == 1:
                        emit(i, row, lanes(g0_ref[off:off + rows, :]))
                    else:
                        total = dg1_ref[0]
                        for k in range(1, N_DEV):
                            total = total + dg1_ref[k]
                        emit(i, row, lanes(total))
                elif name == "gmlp_b_s":
                    emit(i, (l,), g_refs[l][off:off + rows, :])
                elif rows == LANE_ROWS:
                    emit(i, row, lanes(g_refs[l][off:off + rows, :]))
                else:
                    for r in range(rows // LANE_ROWS):
                        emit(i, (l, slice(r, r + 1), slice(None)), g_refs[l][pl.ds(off + r * LANE_ROWS + me, 1), :])
            off += rows

    args = [ws[k] for k in names] + [ms[k] for k in names] + [vs[k] for k in names]
    gspec = pl.BlockSpec((SMALL_VECTOR_BLOCK, HD), lambda i, me: (SMALL_VECTOR_ROW0 // SMALL_VECTOR_BLOCK, 0))
    res = pl.pallas_call(
        body, name="adam_vectors",
        grid_spec=pltpu.PrefetchScalarGridSpec(
            num_scalar_prefetch=1, grid=(1,),
            in_specs=[gspec, gspec, _full(dg1_parts.shape)] + [_full(a.shape) for a in args],
            out_specs=[_full(ws[k].shape) for k in names for _ in range(4)]),
        out_shape=[SDS(ws[k].shape, F32) for k in names for _ in range(4)],
        compiler_params=_cp("arbitrary"))(me1, g0, g1, dg1_parts, *args)
    return {k: list(res[4 * i:4 * i + 4]) for i, k in enumerate(names)}


def _after(a, *tokens):
    for token in tokens:
        if token is not None:
            a = a + token[0:1, 0:1]
    return a


def _local_step(x, tgt, p, get_w, hook=lambda stage, layer, payload: None):
    s = x.shape[0]
    tm = _row_tile(s)
    wsb = p["gmlp_w_s"].astype(BF16)
    wstb = jnp.swapaxes(p["gmlp_w_s"], -1, -2).astype(BF16)
    bsb = jnp.broadcast_to(p["gmlp_b_s"][..., None], p["gmlp_w_s"].shape)
    wrb = p["lru_w_r"].astype(BF16)
    wib = p["lru_w_i"].astype(BF16)
    saved = []
    for l in range(2):
        win = get_w("w_in", l, x)
        z, h1 = _norm_inproj(x, _after(p["norm1_g"][l][None], hook("pre_inproj", l, win)), win, l, tm)
        a0, b0, a1, b1, xcb, *gates = _lru_gates_fwd(z, p["conv_w"][l], p["conv_b"][l][None], wrb[l], wib[l],
                                                     p["lru_b_r"][l], p["lru_b_i"][l], p["lru_lambda"][l], l, tm)
        h0, hr = _lru_scan(a0, b0, a1, b1, False, l)
        ya = _gmlp_fwd(z, _after(p["gmlp_ln_g"][l][None], hook("pre_gmlp", l, h0)), p["gmlp_ln_b"][l][None],
                       wsb[l], bsb[l], l, tm)
        wout = get_w("w_out", l, ya)
        x1, mg = _merge_outproj(x, ya, h0, hr, z, wout, l, tm)
        wfi = get_w("w_ffn_in", l, x1)
        wfo = get_w("w_ffn_out", l, x1)
        x2, gu, h2 = _ffn_fwd(x1, p["norm2_g"][l][None], wfi, wfo, l, tm)
        saved.append((x, z, h1, ya, a0, a1, h0, hr, x1, mg, gu, h2, win, wout, wfi, wfo, xcb, gates))
        x = x2
    dx, loss, dfg = _loss_head(x, p["final_g"][None], tgt, tm)
    pending = None
    for l in (1, 0):
        x0, z, h1, ya, a0, a1, h0, hr, x1, mg, gu, h2, win, wout, wfi, wfo, xcb, gates = saved[l]
        ff, dgu = _ffn_bwd_act(dx, wfo, gu, l, tm)
        d_wfo = _mm_tn(ff, pl.BlockSpec((None, s, FF_BLK), lambda j: (j, 0, 0)), dx, _resident((s, D)),
                       4, (4, FF_BLK, D), pl.BlockSpec((None, FF_BLK, D), lambda j: (j, 0, 0)),
                       f"dw_ffn_out_l{l}", a_is_transposed=False)
        dgu8 = dgu.reshape(N_DEV, s, FF_BLK)
        d_wfi = _mm_tn(dgu8, pl.BlockSpec((None, s, FF_BLK), lambda j: (j, 0, 0)), h2, _resident((s, D)),
                       N_DEV, (N_DEV, FF_BLK, D), pl.BlockSpec((None, FF_BLK, D), lambda j: (j, 0, 0)),
                       f"dw_ffn_in_l{l}", a_is_transposed=False)
        token = hook("ffn_partials", l, dict(w_ffn_out=d_wfo.reshape(N_DEV, D_FF // N_DEV, D), w_ffn_in=d_wfi))
        dx1, dg2 = _mm_nt_rms_bwd(
            dgu8, pl.BlockSpec((N_DEV, tm, FF_BLK), lambda i: (0, i, 0)), lambda r: [r[k] for k in range(N_DEV)],
            wfi.reshape(N_DEV, FF_BLK, D), True, x1, _after(p["norm2_g"][l][None], token, pending), dx,
            f"ffn_bwd_dx_l{l}", tm)
        pending = hook("mid_backward", l, dx1)
        dz, dya, dh = _outproj_bwd_merge(dx1, wout, ya, h0, hr, z, l, tm)
        d_wout = _mm_tn(mg, _resident((D, s)), dx1, pl.BlockSpec((s, D // 2), lambda j: (0, j)),
                        2, (D, D), pl.BlockSpec((D, D // 2), lambda j: (0, j)), f"dw_out_l{l}")
        g1, g0 = _lru_scan(a1, dh, a0, dh, True, l)
        dxc, dwr, dwi, dbr, dbi, dlam = _lru_gates_bwd(
            xcb, gates, h0, hr, g0, g1, wrb[l], wib[l], _after(p["lru_lambda"][l], pending), l, tm)
        dz, dcw, dcb = _conv_bwd(dz, dxc, z, p["conv_w"][l], l, tm)
        dz, dws, dbs, dlng, dlnb = _gmlp_bwd(dz, z, dya, p["gmlp_ln_g"][l][None], p["gmlp_ln_b"][l][None],
                                             wsb[l], wstb[l], bsb[l], l, tm)
        small = dict(lru_w_r=dwr.reshape(-1, HD), lru_w_i=dwi.reshape(-1, HD), gmlp_w_s=dws.reshape(-1, HD),
                     gmlp_ln_g=dlng, gmlp_ln_b=dlnb, gmlp_b_s=dbs, conv_w=dcw, conv_b=dcb, lru_b_r=dbr,
                     lru_b_i=dbi, lru_lambda=dlam, norm2_g=dg2)
        if l == 1:
            small["final_g"] = dfg
        else:
            small["norm1_g"] = dg1
        started = hook("small_grads", l, small)
        d_win = _mm_tn(h1, _resident((D, s)), dz, pl.BlockSpec((s, IN_BLK), lambda j: (0, j)),
                       N_DEV, (N_DEV, D, IN_BLK), pl.BlockSpec((None, D, IN_BLK), lambda j: (j, 0, 0)),
                       f"dw_in_l{l}", after=started)
        token = hook("mixer_partials", l, dict(w_out=d_wout.reshape(N_DEV, D // N_DEV, D), w_in=d_win))
        dx, dg1 = _mm_nt_rms_bwd(
            dz, pl.BlockSpec((tm, N_IN), lambda i: (i, 0)),
            lambda r: [r[:, k * IN_BLK:(k + 1) * IN_BLK] for k in range(N_DEV)],
            win, False, x0, _after(p["norm1_g"][l][None], token, started, pending), dx1, f"inproj_bwd_dx_l{l}", tm)
        pending = None
    return loss, dx, dg1


_REPL = ["norm1_g", "gmlp_ln_g", "gmlp_ln_b", "gmlp_w_s", "gmlp_b_s", "conv_b", "lru_w_r", "lru_w_i", "norm2_g", "final_g"]
_LANE_SHARDED = ["conv_w", "lru_b_r", "lru_b_i", "lru_lambda"]
_BIG = ["w_in", "w_out", "w_ffn_in", "w_ffn_out"]
_ORDER = ["norm1_g", "w_in", "gmlp_ln_g", "gmlp_ln_b", "gmlp_w_s", "gmlp_b_s", "conv_w", "conv_b", "lru_w_r", "lru_b_r",
          "lru_w_i", "lru_b_i", "lru_lambda", "w_out", "norm2_g", "w_ffn_in", "w_ffn_out", "final_g"]


def kernel(x, norm1_g, w_in, gmlp_ln_g, gmlp_ln_b, gmlp_w_s, gmlp_b_s, conv_w, conv_b, lru_w_r, lru_b_r, lru_w_i, lru_b_i, lru_lambda, w_out, norm2_g, w_ffn_in, w_ffn_out, final_g, loss_target, m_norm1_g, m_w_in, m_gmlp_ln_g, m_gmlp_ln_b, m_gmlp_w_s, m_gmlp_b_s, m_conv_w, m_conv_b, m_lru_w_r, m_lru_b_r, m_lru_w_i, m_lru_b_i, m_lru_lambda, m_w_out, m_norm2_g, m_w_ffn_in, m_w_ffn_out, m_final_g, v_norm1_g, v_w_in, v_gmlp_ln_g, v_gmlp_ln_b, v_gmlp_w_s, v_gmlp_b_s, v_conv_w, v_conv_b, v_lru_w_r, v_lru_b_r, v_lru_w_i, v_lru_b_i, v_lru_lambda, v_w_out, v_norm2_g, v_w_ffn_in, v_w_ffn_out, v_final_g):
    w = dict(norm1_g=norm1_g, w_in=w_in, gmlp_ln_g=gmlp_ln_g, gmlp_ln_b=gmlp_ln_b, gmlp_w_s=gmlp_w_s, gmlp_b_s=gmlp_b_s,
             conv_w=conv_w, conv_b=conv_b, lru_w_r=lru_w_r, lru_b_r=lru_b_r, lru_w_i=lru_w_i, lru_b_i=lru_b_i,
             lru_lambda=lru_lambda, w_out=w_out, norm2_g=norm2_g, w_ffn_in=w_ffn_in, w_ffn_out=w_ffn_out, final_g=final_g)
    mom = dict(norm1_g=m_norm1_g, w_in=m_w_in, gmlp_ln_g=m_gmlp_ln_g, gmlp_ln_b=m_gmlp_ln_b, gmlp_w_s=m_gmlp_w_s,
               gmlp_b_s=m_gmlp_b_s, conv_w=m_conv_w, conv_b=m_conv_b, lru_w_r=m_lru_w_r, lru_b_r=m_lru_b_r,
               lru_w_i=m_lru_w_i, lru_b_i=m_lru_b_i, lru_lambda=m_lru_lambda, w_out=m_w_out, norm2_g=m_norm2_g,
               w_ffn_in=m_w_ffn_in, w_ffn_out=m_w_ffn_out, final_g=m_final_g)
    var = dict(norm1_g=v_norm1_g, w_in=v_w_in, gmlp_ln_g=v_gmlp_ln_g, gmlp_ln_b=v_gmlp_ln_b, gmlp_w_s=v_gmlp_w_s,
               gmlp_b_s=v_gmlp_b_s, conv_w=v_conv_w, conv_b=v_conv_b, lru_w_r=v_lru_w_r, lru_b_r=v_lru_b_r,
               lru_w_i=v_lru_w_i, lru_b_i=v_lru_b_i, lru_lambda=v_lru_lambda, w_out=v_w_out, norm2_g=v_norm2_g,
               w_ffn_in=v_w_ffn_in, w_ffn_out=v_w_ffn_out, final_g=v_final_g)
    for src in (w, mom, var):
        src["w_ffn_in"] = jnp.swapaxes(src["w_ffn_in"], 1, 2)
    xi, yi, ci = _me()
    me = 4 * xi + 2 * yi + ci

    lane_shapes = [w[k].shape for k in _LANE_SHARDED]
    lane_rows = sum(a[0] * a[1] for a in lane_shapes)
    packed = jnp.concatenate([w[k].reshape(-1, HD) for k in _LANE_SHARDED])
    packed = jnp.pad(packed, ((0, -lane_rows % 8), (0, 0)))

    me1 = jnp.reshape(me, (1,)).astype(jnp.int32)
    gathers = {}
    exchanges = {}
    views = dict(w_in=(N_DEV, D, IN_BLK), w_out=(D, D), w_ffn_in=(2, 4, FF_BLK, D), w_ffn_out=(4, FF_BLK, D))
    small_ex = {}
    small_ag = {}

    def start_gather(names, l, after):
        lands = [_cast_into_slot(w[k], l, me1, f"cast_{k}_l{l}") for k in names]
        started, tok = _gather2_start(lands, after, f"gather_start_{'_'.join(names)}_l{l}")
        gathers.update({(k, l): h for k, h in zip(names, started)})
        return tok

    def relay_gather(names, l, after):
        relayed, tok = _gather2_relay([gathers[(k, l)] for k in names], after, f"gather_relay_{'_'.join(names)}_l{l}")
        gathers.update({(k, l): h for k, h in zip(names, relayed)})
        return tok

    def get_w(k, l, after):
        if (k, l) == ("w_in", 1):
            after = relay_gather(_BIG[:1], l, after)
        return _gather2_wait(gathers[(k, l)], after, f"gather_wait_{k}_l{l}").reshape(views[k])

    def hook(stage, l, payload):
        if stage == "pre_inproj":
            return start_gather(_BIG[1:], l, payload)
        if stage == "pre_gmlp":
            tok = relay_gather(_BIG[1:], l, payload)
            return tok + start_gather(_BIG[:1], l + 1, tok) if l == 0 else tok
        if stage == "small_grads":
            (small_ex[l],), tok = _exchange_start([_pack_small(payload)], f"exchange_start_small_l{l}")
            return tok
        if stage == "mid_backward":
            return reduce_small(l + 1, payload) if l == 0 else None
        extra = reduce_small(0, payload["w_in"]) if (stage, l) == ("mixer_partials", 0) else None
        started, tok = _exchange_start(list(payload.values()), f"exchange_start_{'_'.join(payload)}_l{l}")
        exchanges.update({(k, l): h for k, h in zip(payload, started)})
        return tok if extra is None else tok + extra

    def reduce_small(l, after):
        got = _exchange_wait(small_ex[l], after, f"exchange_wait_small_l{l}")
        mine = _sum8_into_slot(got, me1, f"sum_small_l{l}")
        (small_ag[l],), tok = _gather_start([mine], got, f"gather_start_small_l{l}")
        return tok

    land = lax.dynamic_update_slice(jnp.zeros((N_DEV,) + packed.shape, F32), packed[None], (me, 0, 0))
    (lanes_handle,), token = _gather_start([land], packed, "gather_start_lanes")
    token = relay_gather(_BIG[:1], 0, start_gather(_BIG[:1], 0, token))
    lanes = _gather_wait(lanes_handle, token, "gather_wait_lanes")
    params = {k: w[k] for k in _REPL}
    off = 0
    for k, shp in zip(_LANE_SHARDED, lane_shapes):
        n = shp[0] * shp[1]
        params[k] = jnp.swapaxes(lanes[:, off:off + n], 0, 1).reshape(shp[0], shp[1], D)
        off += n
    loss, dx, dg1 = _local_step(x[0], loss_target[0], params, get_w, hook)

    out = {}
    after = dx
    for k, l in [(k, l) for k in ("w_ffn_out", "w_ffn_in") for l in (1, 0)] + [("w_out", 1), ("w_in", 1)]:
        got = _exchange_wait(exchanges[(k, l)], after, f"exchange_wait_{k}_l{l}")
        out[k] = _adam_shard(got, w[k], mom[k], var[k], l, out.get(k), f"adam_{k}_l{l}")
        after = out[k][3]
    g_small = [_gather_wait(small_ag[l], after, f"gather_wait_small_l{l}").reshape(SMALL_ROWS, HD) for l in (0, 1)]
    row0 = 0
    for k, rows in SMALL_MATRICES:
        res = _adam_matrix(*g_small, *[src[k].reshape(2, rows, HD) for src in (w, mom, var)], row0, f"adam_{k}")
        out[k] = [a.reshape(w[k].shape) for a in res]
        after = res[3]
        row0 += rows
    for k in ("w_out", "w_in"):
        got = _exchange_wait(exchanges[(k, 0)], after, f"exchange_wait_{k}_l0")
        out[k] = _adam_shard(got, w[k], mom[k], var[k], 0, out[k], f"adam_{k}_l0")
    out["w_ffn_in"] = [jnp.swapaxes(a, 1, 2) for a in out["w_ffn_in"]]
    as_rows = lambda a: a.reshape(1, D) if a.ndim == 1 else a
    vec = _adam_vectors(*g_small, _all_gather(dg1, out["w_in"][3], "gather_norm1_grad"), me1,
                        *[{k: as_rows(src[k]) for k, _ in SMALL_VECTORS} for src in (w, mom, var)])
    out.update({k: [a.reshape(w[k].shape) for a in res] for k, res in vec.items()})

    loss = lax.psum(loss[0, 0], MESH_AXES)
    return (loss, dx[None], *[out[k][0] for k in _ORDER], *[out[k][1] for k in _ORDER],
            *[out[k][2] for k in _ORDER], *[out[k][3] for k in _ORDER])
```

```python
import jax
import jax.numpy as jnp
from jax import lax
from jax.experimental import pallas as pl
from jax.experimental.pallas import tpu as pltpu

F32 = jnp.float32
BF16 = jnp.bfloat16
SDS = jax.ShapeDtypeStruct

D = 1024
N_IN = 6 * D
D_FF = 2816
N_DEV = 8
IN_BLK = N_IN // N_DEV
FF_BLK = 2 * D_FF // N_DEV
HEADS = 8
HD = 128
EPS = 1e-6
LRU_C = 8.0
MESH_AXES = ("x", "y", "c")

ADAM_LR = 0.001
ADAM_B1 = 0.9
ADAM_B2 = 0.999
ADAM_EPS = 1e-08
ADAM_WD = 0.01
ADAM_STEP = 10

VMEM_LIMIT = 56 * 2**20


def _cp(*sem, **kw):
    return pltpu.CompilerParams(dimension_semantics=sem, vmem_limit_bytes=VMEM_LIMIT, **kw)


def _row_tile(s):
    return 512 if s >= 1024 else s // 2


_GELU_C = 0.7978845608028654


def _gelu(x):
    t = jnp.tanh(_GELU_C * (x + 0.044715 * (x * x * x)))
    return 0.5 * x * (1.0 + t), t


def _gelu_grad(x, t):
    return 0.5 * (1.0 + t) + 0.5 * x * (1.0 - t * t) * (_GELU_C * (1.0 + 0.134145 * (x * x)))


def _sigmoid(x):
    return 0.5 + 0.5 * jnp.tanh(0.5 * x)


def _softplus(x):
    e = jnp.exp(-jnp.abs(x))
    w = 1.0 + e
    l1p = jnp.where(w == 1.0, e, jnp.log(w) * e / jnp.where(w == 1.0, 1.0, w - 1.0))
    return jnp.maximum(x, 0.0) + l1p


def _rms_fwd(x, g):
    r = lax.rsqrt(jnp.mean(x * x, axis=-1, keepdims=True) + EPS)
    return x * r * g


def _rms_bwd(x, g, dh):
    r = lax.rsqrt(jnp.mean(x * x, axis=-1, keepdims=True) + EPS)
    xh = x * r
    dxh = dh * g
    dx = r * (dxh - xh * jnp.mean(dxh * xh, axis=-1, keepdims=True))
    dg = jnp.sum(dh * xh, axis=0, keepdims=True)
    return dx, dg


LANE_ROWS = D // HD


def _add_rows128(ref, vec, row0=0):
    for i in range(vec.shape[0]):
        for k in range(LANE_ROWS):
            j = row0 + i * LANE_ROWS + k
            ref[j:j + 1, :] += vec[i:i + 1, k * HD:(k + 1) * HD]


def _dot(a, b):
    return jnp.dot(a, b, preferred_element_type=F32)


def _dot_nt(a, b):
    return lax.dot_general(a, b, (((1,), (1,)), ((), ())), preferred_element_type=F32)


def _dot_tn(a, b):
    return lax.dot_general(a, b, (((0,), (0,)), ((), ())), preferred_element_type=F32)


def _taps(prev, cur, nxt, tm):
    hr = prev.shape[0]
    ext = jnp.concatenate([prev, cur, nxt], axis=0)
    n = tm + 2 * hr
    sl = slice(hr, hr + tm)
    return (pltpu.roll(ext, 2, 0)[sl], pltpu.roll(ext, 1, 0)[sl], cur,
            pltpu.roll(ext, n - 1, 0)[sl], pltpu.roll(ext, n - 2, 0)[sl])


def _halo_specs(tm, s, col, rows=8):
    nb = s // rows
    r = tm // rows
    return (pl.BlockSpec((rows, D), lambda i: (jnp.maximum(i * r - 1, 0), col)),
            pl.BlockSpec((tm, D), lambda i: (i, col)),
            pl.BlockSpec((rows, D), lambda i: (jnp.minimum((i + 1) * r, nb - 1), col)))


def _halo_load(prev_ref, cur_ref, next_ref, fp, fn):
    return prev_ref[...].astype(F32) * fp, cur_ref[...].astype(F32), next_ref[...].astype(F32) * fn


def _halo_flags(nt):
    i = pl.program_id(0)
    return (i > 0).astype(F32), (i < nt - 1).astype(F32)


def _full(shape):
    nd = len(shape)
    return pl.BlockSpec(shape, lambda *_: (0,) * nd)


def _resident(shape):
    nd = len(shape)
    return pl.BlockSpec(shape, lambda *_: (0,) * nd, pipeline_mode=pl.Buffered(1))


def _norm_inproj(x, g, w, layer, tm):
    s = x.shape[0]

    def body(x_ref, g_ref, w_ref, z_ref, ht_ref):
        h32 = _rms_fwd(x_ref[...], g_ref[...])
        ht_ref[...] = h32.T.astype(BF16)
        h = h32.astype(BF16)
        for j in range(N_DEV):
            z_ref[:, j * IN_BLK:(j + 1) * IN_BLK] = _dot(h, w_ref[j]).astype(BF16)

    return pl.pallas_call(
        body, name=f"norm_inproj_l{layer}", grid=(s // tm,),
        in_specs=[pl.BlockSpec((tm, D), lambda i: (i, 0)), _full((1, D)), _resident((N_DEV, D, IN_BLK))],
        out_specs=[pl.BlockSpec((tm, N_IN), lambda i: (i, 0)), pl.BlockSpec((D, tm), lambda i: (0, i))],
        out_shape=[SDS((s, N_IN), BF16), SDS((D, s), BF16)],
        compiler_params=_cp("parallel"))(x, g, w)


def _gmlp_values(zu_ref, zv_ref, lng_ref, lnb_ref):
    zu = zu_ref[...].astype(F32)
    zv = zv_ref[...].astype(F32)
    u, tu = _gelu(zu)
    gv, tv = _gelu(zv)
    xc = gv - jnp.mean(gv, axis=-1, keepdims=True)
    rstd = lax.rsqrt(jnp.mean(xc * xc, axis=-1, keepdims=True) + EPS)
    xh = xc * rstd
    vb = (xh * lng_ref[...] + lnb_ref[...]).astype(BF16)
    return zu, zv, u, tu, tv, xh, rstd, vb


def _mixer_fwd(x, h0, h1, z, lng, lnb, ws, bsb, wo, layer, tm):
    s = x.shape[0]

    def body(x_ref, h0_ref, h1_ref, zu_ref, zv_ref, zg_ref, za_ref, zb_ref, lng_ref, lnb_ref, ws_ref, bsb_ref,
             wo_ref, x1_ref, mg_ref, ya_s):
        _, _, u, _, _, _, _, vb = _gmlp_values(zu_ref, zv_ref, lng_ref, lnb_ref)
        for c in range(tm // HD):
            rs = slice(c * HD, (c + 1) * HD)
            for g in range(HEADS):
                cs = slice(g * HD, (g + 1) * HD)
                ya_s[rs, cs] = u[rs, cs] * (_dot(ws_ref[g], vb[rs, cs]) + bsb_ref[g])
        gg, _ = _gelu(zg_ref[...].astype(F32))
        yb = (h0_ref[...] + h1_ref[...]) * gg
        m32 = _sigmoid(za_ref[...].astype(F32)) * ya_s[...] + _sigmoid(zb_ref[...].astype(F32)) * yb
        mg_ref[...] = m32.T.astype(BF16)
        x1_ref[...] = x_ref[...] + _dot(m32.astype(BF16), wo_ref[...])

    tile = pl.BlockSpec((tm, D), lambda i: (i, 0))
    wspec = _full((HEADS, HD, HD))
    return pl.pallas_call(
        body, name=f"mixer_fwd_l{layer}", grid=(s // tm,),
        in_specs=[tile, tile, tile] + [pl.BlockSpec((tm, D), lambda i, c=c: (i, c)) for c in (0, 1, 3, 4, 5)]
        + [_full((1, D)), _full((1, D)), wspec, wspec, _full((D, D))],
        out_specs=[tile, pl.BlockSpec((D, tm), lambda i: (0, i))], out_shape=[SDS((s, D), F32), SDS((D, s), BF16)],
        scratch_shapes=[pltpu.VMEM((tm, D), F32)],
        compiler_params=_cp("parallel"))(x, h0, h1, z, z, z, z, z, lng, lnb, ws, bsb, wo)


def _conv(taps, cw_ref, cb_ref):
    _, m1, c0, p1, p2 = taps
    return cb_ref[...] + m1 * cw_ref[0:1, :] + c0 * cw_ref[1:2, :] + p1 * cw_ref[2:3, :] + p2 * cw_ref[3:4, :]


def _heads_dot(xb, w_ref, d):
    return jnp.concatenate([_dot(xb[:, h * HD:(h + 1) * HD], w_ref[d, h]) for h in range(HEADS)], axis=1)


def _lru_decay(r, sp):
    la = (-LRU_C) * r * sp
    a = jnp.exp(la)
    return a, jnp.tanh(-la) * (a * a + 1.0)


def _lru_gates_fwd(z, cw, cb, wr, wi, br, bi, lam, layer, tm):
    s = z.shape[0]
    nt = s // tm

    def body(zp_ref, zc_ref, zn_ref, cw_ref, cb_ref, wr_ref, wi_ref, br_ref, bi_ref, lam_ref,
             a0_ref, b0_ref, a1_ref, b1_ref, xc_ref, r0_ref, i0_ref, r1_ref, i1_ref):
        fp, fn = _halo_flags(nt)
        xc = _conv(_taps(*_halo_load(zp_ref, zc_ref, zn_ref, fp, fn), tm), cw_ref, cb_ref)
        xb = xc.astype(BF16)
        xc_ref[...] = xb
        for d, (a_ref, b_ref, r_ref, i_ref) in enumerate(((a0_ref, b0_ref, r0_ref, i0_ref),
                                                          (a1_ref, b1_ref, r1_ref, i1_ref))):
            r = _sigmoid(_heads_dot(xb, wr_ref, d) + br_ref[d:d + 1, :])
            ig = _sigmoid(_heads_dot(xb, wi_ref, d) + bi_ref[d:d + 1, :])
            a, q = _lru_decay(r, _softplus(-lam_ref[d:d + 1, :]))
            a_ref[...] = a
            b_ref[...] = jnp.sqrt(q) * (ig * xc)
            r_ref[...] = r.astype(BF16)
            i_ref[...] = ig.astype(BF16)

    tile = pl.BlockSpec((tm, D), lambda i: (i, 0))
    return pl.pallas_call(
        body, name=f"lru_gates_fwd_l{layer}", grid=(nt,),
        in_specs=[*_halo_specs(tm, s, 2, 16), _full((4, D)), _full((1, D)),
                  _full((2, HEADS, HD, HD)), _full((2, HEADS, HD, HD)), _full((2, D)), _full((2, D)), _full((2, D))],
        out_specs=[tile] * 9, out_shape=[SDS((s, D), F32)] * 4 + [SDS((s, D), BF16)] * 5,
        compiler_params=_cp("parallel"))(z, z, z, cw, cb, wr, wi, br, bi, lam)


def _scan_group(a, x, c, reverse, bwd):
    row = lax.broadcasted_iota(jnp.int32, a.shape, 0)
    b = a * x if bwd else x
    for d in (1, 2, 4):
        keep = (row < 8 - d) if reverse else (row >= d)
        sh = 8 - d if reverse else d
        a_s = jnp.where(keep, pltpu.roll(a, sh, 0), 1.0)
        b_s = jnp.where(keep, pltpu.roll(b, sh, 0), 0.0)
        b = a * b_s + b
        a = a * a_s
    h = b + a * c
    new_c = h[0:1, :] if reverse else h[7:8, :]
    if not bwd:
        return h, new_c
    if reverse:
        prev = jnp.where(row < 7, pltpu.roll(h, 7, 0), c)
    else:
        prev = jnp.where(row >= 1, pltpu.roll(h, 1, 0), c)
    return x + prev, new_c


def _lru_scan(a_f, x_f, a_r, x_r, bwd, layer):
    s = a_f.shape[0]
    ts = min(1024, s // 2)
    cb = 512
    nt = s // ts
    ng = ts // 8

    def body(af_ref, xf_ref, ar_ref, xr_ref, of_ref, or_ref, cf, cr):
        @pl.when(pl.program_id(1) == 0)
        def _():
            cf[...] = jnp.zeros_like(cf)
            cr[...] = jnp.zeros_like(cr)

        def step(j, carry):
            c_f, c_r = carry
            rf = pl.multiple_of(j * 8, 8)
            rr = pl.multiple_of((ng - 1 - j) * 8, 8)
            o, c_f = _scan_group(af_ref[pl.ds(rf, 8), :], xf_ref[pl.ds(rf, 8), :], c_f, False, bwd)
            of_ref[pl.ds(rf, 8), :] = o
            o, c_r = _scan_group(ar_ref[pl.ds(rr, 8), :], xr_ref[pl.ds(rr, 8), :], c_r, True, bwd)
            or_ref[pl.ds(rr, 8), :] = o
            return c_f, c_r

        c_f, c_r = lax.fori_loop(0, ng, step, (cf[0:1, :], cr[0:1, :]), unroll=2)
        cf[...] = jnp.broadcast_to(c_f, cf.shape)
        cr[...] = jnp.broadcast_to(c_r, cr.shape)

    fwd = pl.BlockSpec((ts, cb), lambda c, t: (t, c))
    rev = pl.BlockSpec((ts, cb), lambda c, t: (nt - 1 - t, c))
    return pl.pallas_call(
        body, name=f"lru_scan_{'bwd' if bwd else 'fwd'}_l{layer}", grid=(D // cb, nt),
        in_specs=[fwd, fwd, rev, rev], out_specs=[fwd, rev],
        out_shape=[SDS((s, D), F32)] * 2,
        scratch_shapes=[pltpu.VMEM((8, cb), F32), pltpu.VMEM((8, cb), F32)],
        compiler_params=_cp("parallel", "arbitrary"))(a_f, x_f, a_r, x_r)


def _ffn_fwd(x1, g, wfi, wfo, layer, tm):
    s = x1.shape[0]

    def body(x_ref, g_ref, wi_ref, wo_ref, x2_ref, gu_ref, h_ref):
        x = x_ref[...]
        h = _rms_fwd(x, g_ref[...]).astype(BF16)
        h_ref[...] = h
        acc = x
        for k in range(4):
            gate = _dot_nt(h, wi_ref[0, k])
            up = _dot_nt(h, wi_ref[1, k])
            gu_ref[0, k] = gate.astype(BF16)
            gu_ref[1, k] = up.astype(BF16)
            acc = acc + _dot((gate * _sigmoid(gate) * up).astype(BF16), wo_ref[k])
        x2_ref[...] = acc

    tile = pl.BlockSpec((tm, D), lambda i: (i, 0))
    return pl.pallas_call(
        body, name=f"ffn_fwd_l{layer}", grid=(s // tm,),
        in_specs=[tile, _full((1, D)), _resident((2, 4, FF_BLK, D)), _resident((4, FF_BLK, D))],
        out_specs=[tile, pl.BlockSpec((2, 4, tm, FF_BLK), lambda i: (0, 0, i, 0)), tile],
        out_shape=[SDS((s, D), F32), SDS((2, 4, s, FF_BLK), BF16), SDS((s, D), BF16)],
        compiler_params=_cp("parallel"))(x1, g, wfi, wfo)


def _loss_head(x, g, tgt, tm):
    s = x.shape[0]

    def body(x_ref, g_ref, t_ref, dx_ref, loss_ref, dg_ref):
        @pl.when(pl.program_id(0) == 0)
        def _():
            loss_ref[...] = jnp.zeros_like(loss_ref)
            dg_ref[...] = jnp.zeros_like(dg_ref)

        x = x_ref[...]
        gv = g_ref[...]
        e = _rms_fwd(x, gv) - t_ref[...]
        rows = jnp.sum(e * e, axis=-1, keepdims=True)
        loss_ref[...] += (0.5 / D) * jnp.sum(rows, axis=0, keepdims=True)
        dx, dg = _rms_bwd(x, gv, e * (1.0 / D))
        dx_ref[...] = dx
        _add_rows128(dg_ref, dg)

    tile = pl.BlockSpec((tm, D), lambda i: (i, 0))
    return pl.pallas_call(
        body, name="loss_head", grid=(s // tm,),
        in_specs=[tile, _full((1, D)), tile],
        out_specs=[tile, _full((1, 1)), _full((LANE_ROWS, HD))],
        out_shape=[SDS((s, D), F32), SDS((1, 1), F32), SDS((LANE_ROWS, HD), F32)],
        compiler_params=_cp("arbitrary"))(x, g, tgt)


def _ffn_bwd_act(dx2, wfo, gu, layer, tm):
    s = dx2.shape[0]

    def body(dx_ref, wo_ref, gu_ref, ff_ref, dgu_ref):
        dxb = dx_ref[...].astype(BF16)
        for k in range(4):
            dff = _dot_nt(dxb, wo_ref[k])
            gate = gu_ref[0, k].astype(F32)
            up = gu_ref[1, k].astype(F32)
            sg = _sigmoid(gate)
            sl = gate * sg
            ff_ref[k] = (sl * up).astype(BF16)
            dgu_ref[0, k] = (dff * up * (sg * (1.0 + gate * (1.0 - sg)))).astype(BF16)
            dgu_ref[1, k] = (dff * sl).astype(BF16)

    blk = pl.BlockSpec((2, 4, tm, FF_BLK), lambda i: (0, 0, i, 0))
    return pl.pallas_call(
        body, name=f"ffn_bwd_act_l{layer}", grid=(s // tm,),
        in_specs=[pl.BlockSpec((tm, D), lambda i: (i, 0)), _resident((4, FF_BLK, D)), blk],
        out_specs=[pl.BlockSpec((4, tm, FF_BLK), lambda i: (0, i, 0)), blk],
        out_shape=[SDS((4, s, FF_BLK), BF16), SDS((2, 4, s, FF_BLK), BF16)],
        compiler_params=_cp("parallel"))(dx2, wfo, gu)


def _mm_nt_rms_bwd(a, a_spec, a_blocks, w, w_is_transposed, x, g, dres, name, tm):
    s = x.shape[0]

    def body(a_ref, w_ref, x_ref, g_ref, dres_ref, dx_ref, dg_ref):
        @pl.when(pl.program_id(0) == 0)
        def _():
            dg_ref[...] = jnp.zeros_like(dg_ref)

        dh = None
        for k, blk in enumerate(a_blocks(a_ref)):
            part = _dot(blk, w_ref[k]) if w_is_transposed else _dot_nt(blk, w_ref[k])
            dh = part if dh is None else dh + part
        dx, dg = _rms_bwd(x_ref[...], g_ref[...], dh)
        dx_ref[...] = dres_ref[...] + dx
        _add_rows128(dg_ref, dg)

    tile = pl.BlockSpec((tm, D), lambda i: (i, 0))
    return pl.pallas_call(
        body, name=name, grid=(s // tm,),
        in_specs=[a_spec, _resident(w.shape), tile, _full((1, D)), tile],
        out_specs=[tile, _full((LANE_ROWS, HD))], out_shape=[SDS((s, D), F32), SDS((LANE_ROWS, HD), F32)],
        compiler_params=_cp("arbitrary"))(a, w, x, g, dres)


def _mm_tn(a, a_spec, b, b_spec, nb, out_shape, out_spec, name, a_is_transposed=True, after=None):
    def body(a_ref, b_ref, *rest):
        o_ref = rest[-1]
        bb = b_ref[...].astype(BF16)
        o_ref[...] = (_dot(a_ref[...], bb) if a_is_transposed else _dot_tn(a_ref[...], bb)).astype(BF16)

    deps = [] if after is None else [after]
    return pl.pallas_call(
        body, name=name, grid=(nb,), in_specs=[a_spec, b_spec] + [_ANY] * len(deps), out_specs=out_spec,
        out_shape=SDS(out_shape, BF16), compiler_params=_cp("parallel"))(a, b, *deps)


def _mixer_bwd(dx1, wo, h0, h1, z, lng, lnb, ws, wst, bsb, layer, tm):
    s = dx1.shape[0]
    nt = s // tm

    def body(dx_ref, wo_ref, h0_ref, h1_ref, zu_ref, zv_ref, zg_ref, za_ref, zb_ref, lng_ref, lnb_ref,
             ws_ref, wst_ref, bsb_ref, dz_ref, dh_ref, dws_ref, dbs_ref, dlng_ref, dlnb_ref,
             du_s, dv_s, ya_s, dbs_acc):
        i = pl.program_id(0)

        @pl.when(i == 0)
        def _():
            for r in (dws_ref, dlng_ref, dlnb_ref, dbs_acc):
                r[...] = jnp.zeros_like(r)

        dm = _dot_nt(dx_ref[...].astype(BF16), wo_ref[...])
        sa = _sigmoid(za_ref[...].astype(F32))
        sb = _sigmoid(zb_ref[...].astype(F32))
        zg = zg_ref[...].astype(F32)
        gg, tg = _gelu(zg)
        hs = h0_ref[...] + h1_ref[...]
        dyb = dm * sb
        dya = dm * sa
        dh_ref[...] = dyb * gg
        dz_ref[:, 2 * D:3 * D] = jnp.zeros((tm, D), BF16)
        dz_ref[:, 3 * D:4 * D] = (dyb * hs * _gelu_grad(zg, tg)).astype(BF16)
        dz_ref[:, 5 * D:6 * D] = (dm * (hs * gg) * (sb * (1.0 - sb))).astype(BF16)

        zu, zv, u, tu, tv, xh, rstd, vb = _gmlp_values(zu_ref, zv_ref, lng_ref, lnb_ref)
        for c in range(tm // HD):
            rs = slice(c * HD, (c + 1) * HD)
            for g in range(HEADS):
                cs = slice(g * HD, (g + 1) * HD)
                vblk = vb[rs, cs]
                mixed = _dot(ws_ref[g], vblk) + bsb_ref[g]
                ya_s[rs, cs] = u[rs, cs] * mixed
                du_s[rs, cs] = dya[rs, cs] * mixed
                dmx = dya[rs, cs] * u[rs, cs]
                dbs_acc[g] += dmx
                dmxb = dmx.astype(BF16)
                dws_ref[g] += _dot_nt(dmxb, vblk)
                dv_s[rs, cs] = _dot(wst_ref[g], dmxb)
        dz_ref[:, 4 * D:5 * D] = (dm * ya_s[...] * (sa * (1.0 - sa))).astype(BF16)
        dv = dv_s[...]
        _add_rows128(dlng_ref, jnp.sum(dv * xh, axis=0, keepdims=True))
        _add_rows128(dlnb_ref, jnp.sum(dv, axis=0, keepdims=True))
        dxh = dv * lng_ref[...]
        dgv = rstd * (dxh - jnp.mean(dxh, axis=-1, keepdims=True)
                      - xh * jnp.mean(dxh * xh, axis=-1, keepdims=True))
        dz_ref[:, 0:D] = (du_s[...] * _gelu_grad(zu, tu)).astype(BF16)
        dz_ref[:, D:2 * D] = (dgv * _gelu_grad(zv, tv)).astype(BF16)

        @pl.when(i == nt - 1)
        def _():
            for g in range(HEADS):
                dbs_ref[g:g + 1, :] = jnp.sum(dbs_acc[g].T, axis=0, keepdims=True)

    tile = pl.BlockSpec((tm, D), lambda i: (i, 0))
    wspec = _full((HEADS, HD, HD))
    return pl.pallas_call(
        body, name=f"mixer_bwd_l{layer}", grid=(nt,),
        in_specs=[tile, _full((D, D)), tile, tile]
        + [pl.BlockSpec((tm, D), lambda i, c=c: (i, c)) for c in (0, 1, 3, 4, 5)]
        + [_full((1, D)), _full((1, D)), wspec, wspec, wspec],
        out_specs=[pl.BlockSpec((tm, N_IN), lambda i: (i, 0)), tile, wspec, _full((HEADS, HD)),
                   _full((LANE_ROWS, HD)), _full((LANE_ROWS, HD))],
        out_shape=[SDS((s, N_IN), BF16), SDS((s, D), F32), SDS((HEADS, HD, HD), F32), SDS((HEADS, HD), F32),
                   SDS((LANE_ROWS, HD), F32), SDS((LANE_ROWS, HD), F32)],
        scratch_shapes=[pltpu.VMEM((tm, D), F32)] * 3 + [pltpu.VMEM((HEADS, HD, HD), F32)],
        compiler_params=_cp("arbitrary"))(dx1, wo, h0, h1, z, z, z, z, z, lng, lnb, ws, wst, bsb)


def _lru_gates_bwd(xcb, gates, h0, h1, g0, g1, wr, wi, lam, layer, tm):
    s = xcb.shape[0]
    nt = s // tm

    def body(xc_ref, r0_ref, i0_ref, r1_ref, i1_ref, h0p_ref, h0_ref, h1_ref, h1n_ref, g0_ref, g1_ref,
             wr_ref, wi_ref, lam_ref, dxc_ref, dwr_ref, dwi_ref, dbr_ref, dbi_ref, dlam_ref):
        i = pl.program_id(0)
        fp, fn = _halo_flags(nt)

        @pl.when(i == 0)
        def _():
            for r in (dwr_ref, dwi_ref, dbr_ref, dbi_ref, dlam_ref):
                r[...] = jnp.zeros_like(r)

        xb = xc_ref[...]
        xc = xb.astype(F32)
        zeros8 = jnp.zeros((8, D), F32)
        h_prev = _taps(h0p_ref[...] * fp, h0_ref[...], zeros8, tm)[1]
        h_next = _taps(zeros8, h1_ref[...], h1n_ref[...] * fn, tm)[3]
        dxc = jnp.zeros((tm, D), F32)
        for d, (g_ref, hsh, r_ref, i_ref) in enumerate(((g0_ref, h_prev, r0_ref, i0_ref),
                                                        (g1_ref, h_next, r1_ref, i1_ref))):
            sp = _softplus(-lam_ref[d:d + 1, :])
            r = r_ref[...].astype(F32)
            ig = i_ref[...].astype(F32)
            a, q = _lru_decay(r, sp)
            rmult = jnp.where(q > 0.0, lax.rsqrt(jnp.where(q > 0.0, q, 1.0)), 0.0)
            mult = q * rmult
            db = g_ref[...]
            da = db * hsh
            dmult = db * (ig * xc)
            di = db * (mult * xc)
            dxc = dxc + db * (mult * ig)
            dla = da * a - dmult * (a * a * rmult)
            dsp_dlam = -_sigmoid(-lam_ref[d:d + 1, :])
            _add_rows128(dlam_ref, jnp.sum(dla * r, axis=0, keepdims=True) * ((-LRU_C) * dsp_dlam), d * LANE_ROWS)
            dpr = dla * sp * (-LRU_C) * (r * (1.0 - r))
            dpi = di * (ig * (1.0 - ig))
            _add_rows128(dbr_ref, jnp.sum(dpr, axis=0, keepdims=True), d * LANE_ROWS)
            _add_rows128(dbi_ref, jnp.sum(dpi, axis=0, keepdims=True), d * LANE_ROWS)
            dprb = dpr.astype(BF16)
            dpib = dpi.astype(BF16)
            parts = []
            for h in range(HEADS):
                cs = slice(h * HD, (h + 1) * HD)
                dwr_ref[d, h] += _dot_tn(xb[:, cs], dprb[:, cs])
                dwi_ref[d, h] += _dot_tn(xb[:, cs], dpib[:, cs])
                parts.append(_dot_nt(dprb[:, cs], wr_ref[d, h]) + _dot_nt(dpib[:, cs], wi_ref[d, h]))
            dxc = dxc + jnp.concatenate(parts, axis=1)
        dxc_ref[...] = dxc.astype(BF16)

    tile = pl.BlockSpec((tm, D), lambda i: (i, 0))
    hp, hc, hn = _halo_specs(tm, s, 0)
    wspec = _full((2, HEADS, HD, HD))
    vspec = _full((2 * LANE_ROWS, HD))
    return pl.pallas_call(
        body, name=f"lru_gates_bwd_l{layer}", grid=(nt,),
        in_specs=[tile] * 5 + [hp, hc, hc, hn, tile, tile, wspec, wspec, _full((2, D))],
        out_specs=[tile, wspec, wspec, vspec, vspec, vspec],
        out_shape=[SDS((s, D), BF16), SDS((2, HEADS, HD, HD), F32), SDS((2, HEADS, HD, HD), F32)]
        + [SDS((2 * LANE_ROWS, HD), F32)] * 3,
        compiler_params=_cp("arbitrary"))(xcb, *gates, h0, h0, h1, h1, g0, g1, wr, wi, lam)


def _conv_bwd(dz, dxc, z, cw, layer, tm):
    s = z.shape[0]
    nt = s // tm

    def body(dz_in, dp_ref, dc_ref, dn_ref, zp_ref, zc_ref, zn_ref, cw_ref, dz_ref, dcw_ref, dcb_ref):
        del dz_in
        fp, fn = _halo_flags(nt)

        @pl.when(pl.program_id(0) == 0)
        def _():
            dcw_ref[...] = jnp.zeros_like(dcw_ref)
            dcb_ref[...] = jnp.zeros_like(dcb_ref)

        dxc_halo = _halo_load(dp_ref, dc_ref, dn_ref, fp, fn)
        dxc = dxc_halo[1]
        dm2, dm1, _, dp1, _ = _taps(*dxc_halo, tm)
        dz_ref[...] = (cw_ref[0:1, :] * dp1 + cw_ref[1:2, :] * dxc + cw_ref[2:3, :] * dm1
                       + cw_ref[3:4, :] * dm2).astype(BF16)
        _, zm1, z0, zp1, zp2 = _taps(*_halo_load(zp_ref, zc_ref, zn_ref, fp, fn), tm)
        for k, zt in enumerate((zm1, z0, zp1, zp2)):
            _add_rows128(dcw_ref, jnp.sum(dxc * zt, axis=0, keepdims=True), k * LANE_ROWS)
        _add_rows128(dcb_ref, jnp.sum(dxc, axis=0, keepdims=True))

    return pl.pallas_call(
        body, name=f"conv_bwd_l{layer}", grid=(nt,),
        in_specs=[pl.BlockSpec(memory_space=pl.ANY), *_halo_specs(tm, s, 0, 16), *_halo_specs(tm, s, 2, 16),
                  _full((4, D))],
        out_specs=[pl.BlockSpec((tm, D), lambda i: (i, 2)), _full((4 * LANE_ROWS, HD)), _full((LANE_ROWS, HD))],
        out_shape=[SDS((s, N_IN), BF16), SDS((4 * LANE_ROWS, HD), F32), SDS((LANE_ROWS, HD), F32)],
        input_output_aliases={0: 0},
        compiler_params=_cp("arbitrary"))(dz, dxc, dxc, dxc, z, z, z, cw)


def _me():
    return lax.axis_index("x"), lax.axis_index("y"), lax.axis_index("c")


def _peer(m):
    x, y, c = _me()
    px = 1 - x if m & 4 else x
    py = 1 - y if m & 2 else y
    pc = 1 - c if m & 1 else c
    return (px, py, pc), 4 * px + 2 * py + pc


_ANY = pl.BlockSpec(memory_space=pl.ANY)
_EXCHANGE_SEMS = [pltpu.SemaphoreType.DMA((N_DEV - 1,)), pltpu.SemaphoreType.DMA((N_DEV - 1,)), pltpu.SemaphoreType.DMA(())]


def _all_gather(v, after, name):
    def body(v_ref, after_ref, o_ref, send_sems, recv_sems, local_sem):
        del after_ref
        x, y, c = _me()
        me = 4 * x + 2 * y + c
        local = pltpu.make_async_copy(v_ref, o_ref.at[me], local_sem)
        local.start()
        sends = []
        for m in range(1, N_DEV):
            dev, _ = _peer(m)
            cp = pltpu.make_async_remote_copy(v_ref, o_ref.at[me], send_sems.at[m - 1], recv_sems.at[m - 1],
                                              device_id=dev, device_id_type=pl.DeviceIdType.MESH)
            cp.start()
            sends.append(cp)
        for m in range(1, N_DEV):
            dev, blk = _peer(m)
            pltpu.make_async_remote_copy(v_ref, o_ref.at[blk], send_sems.at[m - 1], recv_sems.at[m - 1],
                                         device_id=dev, device_id_type=pl.DeviceIdType.MESH).wait_recv()
        for cp in sends:
            cp.wait_send()
        local.wait()

    return pl.pallas_call(
        body, name=name, in_specs=[_ANY, _ANY], out_specs=_ANY,
        out_shape=SDS((N_DEV,) + v.shape, v.dtype), scratch_shapes=_EXCHANGE_SEMS)(v, after)


_HBM = pl.BlockSpec(memory_space=pltpu.HBM)
_SEM = pl.BlockSpec(memory_space=pltpu.SEMAPHORE)
_EFFECT = pltpu.CompilerParams(has_side_effects=pltpu.SideEffectType.DATAFLOW_SIDE_EFFECTING)
_PEER_SEMS = pltpu.SemaphoreType.DMA((N_DEV - 1,))


def _in_hbm(a):
    return pltpu.with_memory_space_constraint(a, pltpu.HBM)


def _remote(src, dst, send_sems, recv_sems, m):
    dev, _ = _peer(m)
    return pltpu.make_async_remote_copy(src, dst, send_sems.at[m - 1], recv_sems.at[m - 1],
                                        device_id=dev, device_id_type=pl.DeviceIdType.MESH)


def _gather_start(lands, after, name):
    n = len(lands)

    def body(*refs):
        land = refs[:n]
        sems = refs[n + 1:3 * n + 1]
        token = refs[-1]
        x, y, c = _me()
        me = 4 * x + 2 * y + c
        for t in range(n):
            for m in range(1, N_DEV):
                _remote(land[t].at[me], land[t].at[me], sems[2 * t], sems[2 * t + 1], m).start()
        token[...] = jnp.zeros_like(token)

    res = pl.pallas_call(
        body, name=name, in_specs=[_HBM] * n + [_ANY],
        out_specs=[_SEM] * (2 * n) + [_HBM] * n + [pl.BlockSpec(memory_space=pltpu.VMEM)],
        out_shape=[_PEER_SEMS] * (2 * n) + [pltpu.HBM(a.shape, a.dtype) for a in lands] + [SDS((8, 128), F32)],
        input_output_aliases={t: 2 * n + t for t in range(n)},
        compiler_params=_EFFECT)(*[_in_hbm(a) for a in lands], after)
    return [(res[2 * t], res[2 * t + 1], res[2 * n + t]) for t in range(n)], res[-1]


def _gather_wait(handle, after, name):
    send_sems, recv_sems, land = handle

    def body(land_ref, ssem, rsem, after_ref, out_ref):
        del after_ref, out_ref
        x, y, c = _me()
        me = 4 * x + 2 * y + c
        for m in range(1, N_DEV):
            _, blk = _peer(m)
            cp = _remote(land_ref.at[me], land_ref.at[blk], ssem, rsem, m)
            cp.wait_send()
            cp.wait_recv()

    return pl.pallas_call(
        body, name=name, in_specs=[_HBM, _SEM, _SEM, _ANY], out_specs=_HBM,
        out_shape=pltpu.HBM(land.shape, land.dtype), input_output_aliases={0: 0},
        compiler_params=_EFFECT)(land, send_sems, recv_sems, after)


FIRST_STAGE = (1, 2, 4, 6)
RELAYED = (2, 4, 6)
OTHER_CORE = 1


def _stage_copy(src, dst, send_sems, recv_sems, k, m):
    dev, _ = _peer(m)
    return pltpu.make_async_remote_copy(src, dst, send_sems.at[k], recv_sems.at[k],
                                        device_id=dev, device_id_type=pl.DeviceIdType.MESH)


def _gather2_start(lands, after, name):
    n = len(lands)

    def body(*refs):
        land = refs[:n]
        sems = refs[n + 1:3 * n + 1]
        token = refs[-1]
        x, y, c = _me()
        me = 4 * x + 2 * y + c
        for t in range(n):
            for k, m in enumerate(FIRST_STAGE):
                _stage_copy(land[t].at[me], land[t].at[me], sems[2 * t], sems[2 * t + 1], k, m).start()
        token[...] = jnp.zeros_like(token)

    stage_sems = pltpu.SemaphoreType.DMA((len(FIRST_STAGE),))
    res = pl.pallas_call(
        body, name=name, in_specs=[_HBM] * n + [_ANY],
        out_specs=[_SEM] * (2 * n) + [_HBM] * n + [pl.BlockSpec(memory_space=pltpu.VMEM)],
        out_shape=[stage_sems] * (2 * n) + [pltpu.HBM(a.shape, a.dtype) for a in lands] + [SDS((8, 128), F32)],
        input_output_aliases={t: 2 * n + t for t in range(n)},
        compiler_params=_EFFECT)(*[_in_hbm(a) for a in lands], after)
    return [(res[2 * t], res[2 * t + 1], res[2 * n + t]) for t in range(n)], res[-1]


def _gather2_relay(handles, after, name):
    n = len(handles)

    def body(*refs):
        land, send1, recv1 = refs[:n], refs[n:2 * n], refs[2 * n:3 * n]
        sems = refs[3 * n + 1:5 * n + 1]
        token = refs[-1]
        x, y, c = _me()
        me = 4 * x + 2 * y + c
        for t in range(n):
            for j, m in enumerate(RELAYED):
                _, blk = _peer(m)
                _stage_copy(land[t].at[me], land[t].at[blk], send1[t], recv1[t], 1 + j, m).wait_recv()
                _stage_copy(land[t].at[blk], land[t].at[blk], sems[2 * t], sems[2 * t + 1], j, OTHER_CORE).start()
        token[...] = jnp.zeros_like(token)

    relay_sems = pltpu.SemaphoreType.DMA((len(RELAYED),))
    lands = [h[2] for h in handles]
    res = pl.pallas_call(
        body, name=name, in_specs=[_HBM] * n + [_SEM] * (2 * n) + [_ANY],
        out_specs=[_SEM] * (2 * n) + [_HBM] * n + [pl.BlockSpec(memory_space=pltpu.VMEM)],
        out_shape=[relay_sems] * (2 * n) + [pltpu.HBM(a.shape, a.dtype) for a in lands] + [SDS((8, 128), F32)],
        input_output_aliases={t: 2 * n + t for t in range(n)},
        compiler_params=_EFFECT)(*lands, *[h[0] for h in handles], *[h[1] for h in handles], after)
    return [(h[0], h[1], res[2 * t], res[2 * t + 1], res[2 * n + t]) for t, h in enumerate(handles)], res[-1]


def _gather2_wait(handle, after, name):
    send1, recv1, send2, recv2, land = handle

    def body(land_ref, s1, r1, s2, r2, after_ref, out_ref):
        del after_ref, out_ref
        x, y, c = _me()
        me = 4 * x + 2 * y + c
        _, other = _peer(OTHER_CORE)
        _stage_copy(land_ref.at[me], land_ref.at[other], s1, r1, 0, OTHER_CORE).wait_recv()
        for k, m in enumerate(FIRST_STAGE):
            _stage_copy(land_ref.at[me], land_ref.at[me], s1, r1, k, m).wait_send()
        for j, m in enumerate(RELAYED):
            _, mine = _peer(m)
            _, theirs = _peer(m ^ OTHER_CORE)
            _stage_copy(land_ref.at[mine], land_ref.at[mine], s2, r2, j, OTHER_CORE).wait_send()
            _stage_copy(land_ref.at[mine], land_ref.at[theirs], s2, r2, j, OTHER_CORE).wait_recv()

    return pl.pallas_call(
        body, name=name, in_specs=[_HBM] + [_SEM] * 4 + [_ANY], out_specs=_HBM,
        out_shape=pltpu.HBM(land.shape, land.dtype), input_output_aliases={0: 0},
        compiler_params=_EFFECT)(land, send1, recv1, send2, recv2, after)


def _exchange_start(ps, name):
    n = len(ps)

    def body(*refs):
        p = refs[:n]
        got = refs[n:2 * n]
        sems = refs[2 * n:5 * n]
        token = refs[-1]
        x, y, c = _me()
        me = 4 * x + 2 * y + c
        for t in range(n):
            pltpu.make_async_copy(p[t].at[me], got[t].at[me], sems[3 * t + 2]).start()
            for m in range(1, N_DEV):
                _, blk = _peer(m)
                _remote(p[t].at[blk], got[t].at[me], sems[3 * t], sems[3 * t + 1], m).start()
        token[...] = jnp.zeros_like(token)

    res = pl.pallas_call(
        body, name=name, in_specs=[_HBM] * (2 * n),
        out_specs=[_SEM] * (3 * n) + [_HBM] * (2 * n) + [pl.BlockSpec(memory_space=pltpu.VMEM)],
        out_shape=[_PEER_SEMS, _PEER_SEMS, pltpu.SemaphoreType.DMA(())] * n
        + [pltpu.HBM(a.shape, a.dtype) for a in ps] * 2 + [SDS((8, 128), F32)],
        input_output_aliases={t: 3 * n + t for t in range(2 * n)},
        compiler_params=_EFFECT)(*[_in_hbm(a) for a in ps], *[_in_hbm(lax.empty(a.shape, a.dtype)) for a in ps])
    return [(res[3 * t], res[3 * t + 1], res[3 * t + 2], res[3 * n + t], res[4 * n + t]) for t in range(n)], res[-1]


def _exchange_wait(handle, after, name):
    send_sems, recv_sems, local_sem, p, got = handle

    def body(p_ref, got_ref, ssem, rsem, lsem, after_ref, p_out, got_out):
        del after_ref, p_out, got_out
        x, y, c = _me()
        me = 4 * x + 2 * y + c
        pltpu.make_async_copy(p_ref.at[me], got_ref.at[me], lsem).wait()
        for m in range(1, N_DEV):
            _, blk = _peer(m)
            cp = _remote(p_ref.at[blk], got_ref.at[blk], ssem, rsem, m)
            cp.wait_send()
            cp.wait_recv()

    return pl.pallas_call(
        body, name=name, in_specs=[_HBM, _HBM, _SEM, _SEM, _SEM, _ANY], out_specs=[_HBM, _HBM],
        out_shape=[pltpu.HBM(p.shape, p.dtype), pltpu.HBM(got.shape, got.dtype)],
        input_output_aliases={0: 0, 1: 1}, compiler_params=_EFFECT)(p, got, send_sems, recv_sems, local_sem, after)[1]


def _cast_into_slot(w, layer, me1, name):
    _, r, c = w.shape
    tr = next(t for t in (256, 352, r) if r % t == 0)

    def body(me_ref, w_ref, o_ref):
        del me_ref
        o_ref[...] = w_ref[...].astype(BF16)

    return pl.pallas_call(
        body, name=name,
        grid_spec=pltpu.PrefetchScalarGridSpec(
            num_scalar_prefetch=1, grid=(r // tr,),
            in_specs=[pl.BlockSpec((None, tr, c), lambda i, me: (layer, i, 0))],
            out_specs=pl.BlockSpec((None, tr, c), lambda i, me: (me[0], i, 0))),
        out_shape=SDS((N_DEV, r, c), BF16), compiler_params=_cp("arbitrary"))(me1, w)


def _sum8_into_slot(p, me1, name):
    _, r, c = p.shape

    def body(me_ref, p_ref, o_ref):
        del me_ref
        acc = p_ref[0]
        for k in range(1, N_DEV):
            acc = acc + p_ref[k]
        o_ref[...] = acc

    return pl.pallas_call(
        body, name=name,
        grid_spec=pltpu.PrefetchScalarGridSpec(
            num_scalar_prefetch=1, grid=(1,),
            in_specs=[pl.BlockSpec(p.shape, lambda i, me: (0, 0, 0))],
            out_specs=pl.BlockSpec((None, r, c), lambda i, me: (me[0], 0, 0))),
        out_shape=SDS(p.shape, F32), compiler_params=_cp("arbitrary"))(me1, p)


def _adamw(w, g, m, v):
    m = ADAM_B1 * m + (1.0 - ADAM_B1) * g
    v = ADAM_B2 * v + (1.0 - ADAM_B2) * (g * g)
    m_hat = m / (1.0 - ADAM_B1 ** ADAM_STEP)
    v_hat = v / (1.0 - ADAM_B2 ** ADAM_STEP)
    delta = -ADAM_LR * (m_hat / (jnp.sqrt(v_hat) + ADAM_EPS) + ADAM_WD * w)
    return delta, m, v


def _adam_shard(parts, w, m, v, layer, prev, name):
    _, r, c = parts.shape
    tr = next(t for t in (256, 352, r) if r % t == 0)
    n_prev = 0 if prev is None else 4

    def body(*refs):
        p_ref, w_ref, m_ref, v_ref = refs[:4]
        g_ref, d_ref, nm_ref, nv_ref = refs[4 + n_prev:]
        g = p_ref[0].astype(F32)
        for k in range(1, N_DEV):
            g = g + p_ref[k].astype(F32)
        delta, nm, nv = _adamw(w_ref[...], g, m_ref[...], v_ref[...])
        g_ref[...] = g
        d_ref[...] = delta
        nm_ref[...] = nm
        nv_ref[...] = nv

    blk = pl.BlockSpec((None, tr, c), lambda i: (layer, i, 0))
    return pl.pallas_call(
        body, name=name, grid=(r // tr,),
        in_specs=[pl.BlockSpec((N_DEV, tr, c), lambda i: (0, i, 0)), blk, blk, blk] + [_ANY] * n_prev,
        out_specs=[blk] * 4, out_shape=[SDS(w.shape, F32)] * 4,
        input_output_aliases={4 + k: k for k in range(n_prev)},
        compiler_params=_cp("parallel"))(parts, w, m, v, *(prev or ()))


SMALL_MATRICES = [("lru_w_r", 2048), ("lru_w_i", 2048), ("gmlp_w_s", 1024)]
SMALL_VECTORS = [("norm1_g", 8), ("gmlp_ln_g", 8), ("gmlp_ln_b", 8), ("gmlp_b_s", 8), ("conv_w", 32), ("conv_b", 8),
                 ("lru_b_r", 16), ("lru_b_i", 16), ("lru_lambda", 16), ("norm2_g", 8), ("final_g", 8)]
SMALL_VECTOR_ROW0 = sum(n for _, n in SMALL_MATRICES)
SMALL_VECTOR_BLOCK = 256
SMALL_ROWS = SMALL_VECTOR_ROW0 + SMALL_VECTOR_BLOCK


def _pack_small(small):
    parts = [small[k] for k, _ in SMALL_MATRICES]
    parts += [small[k] if k in small else jnp.zeros((n, HD), F32) for k, n in SMALL_VECTORS]
    flat = jnp.concatenate(parts)
    return jnp.pad(flat, ((0, SMALL_ROWS - flat.shape[0]), (0, 0))).reshape(N_DEV, SMALL_ROWS // N_DEV, HD)


def _adam_matrix(g0, g1, w, m, v, row0, name):
    _, rows, _ = w.shape

    def body(g0_ref, g1_ref, w_ref, m_ref, v_ref, g_ref, d_ref, nm_ref, nv_ref):
        for l, src in enumerate((g0_ref, g1_ref)):
            g = src[...]
            delta, nm, nv = _adamw(w_ref[l], g, m_ref[l], v_ref[l])
            g_ref[l] = g
            d_ref[l] = delta
            nm_ref[l] = nm
            nv_ref[l] = nv

    gspec = pl.BlockSpec((rows, HD), lambda i: (row0 // rows, 0))
    return pl.pallas_call(body, name=name, grid=(1,), in_specs=[gspec, gspec] + [_full(w.shape)] * 3,
                          out_specs=[_full(w.shape)] * 4, out_shape=[SDS(w.shape, F32)] * 4,
                          compiler_params=_cp("arbitrary"))(g0, g1, w, m, v)


def _adam_vectors(g0, g1, dg1_parts, me1, ws, ms, vs):
    names = [k for k, _ in SMALL_VECTORS]
    n = len(names)

    def lanes(rows8):
        return jnp.concatenate([rows8[k:k + 1, :] for k in range(LANE_ROWS)], axis=1)

    def body(me_ref, g0_ref, g1_ref, dg1_ref, *refs):
        w_refs, m_refs, v_refs = refs[:n], refs[n:2 * n], refs[2 * n:3 * n]
        outs = refs[3 * n:]
        me = me_ref[0]
        g_refs = (g0_ref, g1_ref)

        def emit(i, idx, g):
            delta, nm, nv = _adamw(w_refs[i][idx], g, m_refs[i][idx], v_refs[i][idx])
            for j, val in enumerate((g, delta, nm, nv)):
                outs[4 * i + j][idx] = val

        off = 0
        for i, (name, rows) in enumerate(SMALL_VECTORS):
            for l in range(2):
                row = (slice(l, l + 1), slice(None))
                if name == "final_g":
                    if l == 1:
                        emit(i, (slice(0, 1), slice(None)), lanes(g1_ref[off:off + rows, :]))
                elif name == "norm1_g":
                    if l == 1:
                        emit(i, row, lanes(g0_ref[off:off + rows, :]))
                    else:
                        total = dg1_ref[0]
                        for k in range(1, N_DEV):
                            total = total + dg1_ref[k]
                        emit(i, row, lanes(total))
                elif name == "gmlp_b_s":
                    emit(i, (l,), g_refs[l][off:off + rows, :])
                elif rows == LANE_ROWS:
                    emit(i, row, lanes(g_refs[l][off:off + rows, :]))
                else:
                    for r in range(rows // LANE_ROWS):
                        emit(i, (l, slice(r, r + 1), slice(None)), g_refs[l][pl.ds(off + r * LANE_ROWS + me, 1), :])
            off += rows

    args = [ws[k] for k in names] + [ms[k] for k in names] + [vs[k] for k in names]
    gspec = pl.BlockSpec((SMALL_VECTOR_BLOCK, HD), lambda i, me: (SMALL_VECTOR_ROW0 // SMALL_VECTOR_BLOCK, 0))
    res = pl.pallas_call(
        body, name="adam_vectors",
        grid_spec=pltpu.PrefetchScalarGridSpec(
            num_scalar_prefetch=1, grid=(1,),
            in_specs=[gspec, gspec, _full(dg1_parts.shape)] + [_full(a.shape) for a in args],
            out_specs=[_full(ws[k].shape) for k in names for _ in range(4)]),
        out_shape=[SDS(ws[k].shape, F32) for k in names for _ in range(4)],
        compiler_params=_cp("arbitrary"))(me1, g0, g1, dg1_parts, *args)
    return {k: list(res[4 * i:4 * i + 4]) for i, k in enumerate(names)}


def _after(a, *tokens):
    for token in tokens:
        if token is not None:
            a = a + token[0:1, 0:1]
    return a


def _local_step(x, tgt, p, get_w, hook=lambda stage, layer, payload: None):
    s = x.shape[0]
    tm = _row_tile(s)
    wsb = p["gmlp_w_s"].astype(BF16)
    wstb = jnp.swapaxes(p["gmlp_w_s"], -1, -2).astype(BF16)
    bsb = jnp.broadcast_to(p["gmlp_b_s"][..., None], p["gmlp_w_s"].shape)
    wrb = p["lru_w_r"].astype(BF16)
    wib = p["lru_w_i"].astype(BF16)
    saved = []
    for l in range(2):
        win = get_w("w_in", l, x)
        z, h1 = _norm_inproj(x, _after(p["norm1_g"][l][None], hook("pre_inproj", l, win)), win, l, tm)
        a0, b0, a1, b1, xcb, *gates = _lru_gates_fwd(z, p["conv_w"][l], p["conv_b"][l][None], wrb[l], wib[l],
                                                     p["lru_b_r"][l], p["lru_b_i"][l], p["lru_lambda"][l], l, tm)
        h0, hr = _lru_scan(a0, b0, a1, b1, False, l)
        lng = _after(p["gmlp_ln_g"][l][None], hook("pre_gmlp", l, h0))
        wout = get_w("w_out", l, lng)
        x1, mg = _mixer_fwd(x, h0, hr, z, lng, p["gmlp_ln_b"][l][None], wsb[l], bsb[l], wout, l, tm)
        wfi = get_w("w_ffn_in", l, x1)
        wfo = get_w("w_ffn_out", l, x1)
        x2, gu, h2 = _ffn_fwd(x1, p["norm2_g"][l][None], wfi, wfo, l, tm)
        saved.append((x, z, h1, a0, a1, h0, hr, x1, mg, gu, h2, win, wout, wfi, wfo, xcb, gates))
        x = x2
    dx, loss, dfg = _loss_head(x, p["final_g"][None], tgt, tm)
    pending = None
    for l in (1, 0):
        x0, z, h1, a0, a1, h0, hr, x1, mg, gu, h2, win, wout, wfi, wfo, xcb, gates = saved[l]
        ff, dgu = _ffn_bwd_act(dx, wfo, gu, l, tm)
        d_wfo = _mm_tn(ff, pl.BlockSpec((None, s, FF_BLK), lambda j: (j, 0, 0)), dx, _resident((s, D)),
                       4, (4, FF_BLK, D), pl.BlockSpec((None, FF_BLK, D), lambda j: (j, 0, 0)),
                       f"dw_ffn_out_l{l}", a_is_transposed=False)
        dgu8 = dgu.reshape(N_DEV, s, FF_BLK)
        d_wfi = _mm_tn(dgu8, pl.BlockSpec((None, s, FF_BLK), lambda j: (j, 0, 0)), h2, _resident((s, D)),
                       N_DEV, (N_DEV, FF_BLK, D), pl.BlockSpec((None, FF_BLK, D), lambda j: (j, 0, 0)),
                       f"dw_ffn_in_l{l}", a_is_transposed=False)
        token = hook("ffn_partials", l, dict(w_ffn_out=d_wfo.reshape(N_DEV, D_FF // N_DEV, D), w_ffn_in=d_wfi))
        dx1, dg2 = _mm_nt_rms_bwd(
            dgu8, pl.BlockSpec((N_DEV, tm, FF_BLK), lambda i: (0, i, 0)), lambda r: [r[k] for k in range(N_DEV)],
            wfi.reshape(N_DEV, FF_BLK, D), True, x1, _after(p["norm2_g"][l][None], token, pending), dx,
            f"ffn_bwd_dx_l{l}", tm)
        pending = hook("mid_backward", l, dx1)
        dz, dh, dws, dbs, dlng, dlnb = _mixer_bwd(dx1, wout, h0, hr, z, p["gmlp_ln_g"][l][None],
                                                  p["gmlp_ln_b"][l][None], wsb[l], wstb[l], bsb[l], l, tm)
        d_wout = _mm_tn(mg, _resident((D, s)), dx1, pl.BlockSpec((s, D // 2), lambda j: (0, j)),
                        2, (D, D), pl.BlockSpec((D, D // 2), lambda j: (0, j)), f"dw_out_l{l}")
        g1, g0 = _lru_scan(a1, dh, a0, dh, True, l)
        dxc, dwr, dwi, dbr, dbi, dlam = _lru_gates_bwd(
            xcb, gates, h0, hr, g0, g1, wrb[l], wib[l], _after(p["lru_lambda"][l], pending), l, tm)
        dz, dcw, dcb = _conv_bwd(dz, dxc, z, p["conv_w"][l], l, tm)
        small = dict(lru_w_r=dwr.reshape(-1, HD), lru_w_i=dwi.reshape(-1, HD), gmlp_w_s=dws.reshape(-1, HD),
                     gmlp_ln_g=dlng, gmlp_ln_b=dlnb, gmlp_b_s=dbs, conv_w=dcw, conv_b=dcb, lru_b_r=dbr,
                     lru_b_i=dbi, lru_lambda=dlam, norm2_g=dg2)
        if l == 1:
            small["final_g"] = dfg
        else:
            small["norm1_g"] = dg1
        started = hook("small_grads", l, small)
        d_win = _mm_tn(h1, _resident((D, s)), dz, pl.BlockSpec((s, IN_BLK), lambda j: (0, j)),
                       N_DEV, (N_DEV, D, IN_BLK), pl.BlockSpec((None, D, IN_BLK), lambda j: (j, 0, 0)),
                       f"dw_in_l{l}", after=started)
        token = hook("mixer_partials", l, dict(w_out=d_wout.reshape(N_DEV, D // N_DEV, D), w_in=d_win))
        dx, dg1 = _mm_nt_rms_bwd(
            dz, pl.BlockSpec((tm, N_IN), lambda i: (i, 0)),
            lambda r: [r[:, k * IN_BLK:(k + 1) * IN_BLK] for k in range(N_DEV)],
            win, False, x0, _after(p["norm1_g"][l][None], token, started, pending), dx1, f"inproj_bwd_dx_l{l}", tm)
        pending = None
    return loss, dx, dg1


_REPL = ["norm1_g", "gmlp_ln_g", "gmlp_ln_b", "gmlp_w_s", "gmlp_b_s", "conv_b", "lru_w_r", "lru_w_i", "norm2_g", "final_g"]
_LANE_SHARDED = ["conv_w", "lru_b_r", "lru_b_i", "lru_lambda"]
_BIG = ["w_in", "w_out", "w_ffn_in", "w_ffn_out"]
_ORDER = ["norm1_g", "w_in", "gmlp_ln_g", "gmlp_ln_b", "gmlp_w_s", "gmlp_b_s", "conv_w", "conv_b", "lru_w_r", "lru_b_r",
          "lru_w_i", "lru_b_i", "lru_lambda", "w_out", "norm2_g", "w_ffn_in", "w_ffn_out", "final_g"]


def kernel(x, norm1_g, w_in, gmlp_ln_g, gmlp_ln_b, gmlp_w_s, gmlp_b_s, conv_w, conv_b, lru_w_r, lru_b_r, lru_w_i, lru_b_i, lru_lambda, w_out, norm2_g, w_ffn_in, w_ffn_out, final_g, loss_target, m_norm1_g, m_w_in, m_gmlp_ln_g, m_gmlp_ln_b, m_gmlp_w_s, m_gmlp_b_s, m_conv_w, m_conv_b, m_lru_w_r, m_lru_b_r, m_lru_w_i, m_lru_b_i, m_lru_lambda, m_w_out, m_norm2_g, m_w_ffn_in, m_w_ffn_out, m_final_g, v_norm1_g, v_w_in, v_gmlp_ln_g, v_gmlp_ln_b, v_gmlp_w_s, v_gmlp_b_s, v_conv_w, v_conv_b, v_lru_w_r, v_lru_b_r, v_lru_w_i, v_lru_b_i, v_lru_lambda, v_w_out, v_norm2_g, v_w_ffn_in, v_w_ffn_out, v_final_g):
    w = dict(norm1_g=norm1_g, w_in=w_in, gmlp_ln_g=gmlp_ln_g, gmlp_ln_b=gmlp_ln_b, gmlp_w_s=gmlp_w_s, gmlp_b_s=gmlp_b_s,
             conv_w=conv_w, conv_b=conv_b, lru_w_r=lru_w_r, lru_b_r=lru_b_r, lru_w_i=lru_w_i, lru_b_i=lru_b_i,
             lru_lambda=lru_lambda, w_out=w_out, norm2_g=norm2_g, w_ffn_in=w_ffn_in, w_ffn_out=w_ffn_out, final_g=final_g)
    mom = dict(norm1_g=m_norm1_g, w_in=m_w_in, gmlp_ln_g=m_gmlp_ln_g, gmlp_ln_b=m_gmlp_ln_b, gmlp_w_s=m_gmlp_w_s,
               gmlp_b_s=m_gmlp_b_s, conv_w=m_conv_w, conv_b=m_conv_b, lru_w_r=m_lru_w_r, lru_b_r=m_lru_b_r,
               lru_w_i=m_lru_w_i, lru_b_i=m_lru_b_i, lru_lambda=m_lru_lambda, w_out=m_w_out, norm2_g=m_norm2_g,
               w_ffn_in=m_w_ffn_in, w_ffn_out=m_w_ffn_out, final_g=m_final_g)
    var = dict(norm1_g=v_norm1_g, w_in=v_w_in, gmlp_ln_g=v_gmlp_ln_g, gmlp_ln_b=v_gmlp_ln_b, gmlp_w_s=v_gmlp_w_s,
               gmlp_b_s=v_gmlp_b_s, conv_w=v_conv_w, conv_b=v_conv_b, lru_w_r=v_lru_w_r, lru_b_r=v_lru_b_r,
               lru_w_i=v_lru_w_i, lru_b_i=v_lru_b_i, lru_lambda=v_lru_lambda, w_out=v_w_out, norm2_g=v_norm2_g,
               w_ffn_in=v_w_ffn_in, w_ffn_out=v_w_ffn_out, final_g=v_final_g)
    for src in (w, mom, var):
        src["w_ffn_in"] = jnp.swapaxes(src["w_ffn_in"], 1, 2)
    xi, yi, ci = _me()
    me = 4 * xi + 2 * yi + ci

    lane_shapes = [w[k].shape for k in _LANE_SHARDED]
    lane_rows = sum(a[0] * a[1] for a in lane_shapes)
    packed = jnp.concatenate([w[k].reshape(-1, HD) for k in _LANE_SHARDED])
    packed = jnp.pad(packed, ((0, -lane_rows % 8), (0, 0)))

    me1 = jnp.reshape(me, (1,)).astype(jnp.int32)
    gathers = {}
    exchanges = {}
    views = dict(w_in=(N_DEV, D, IN_BLK), w_out=(D, D), w_ffn_in=(2, 4, FF_BLK, D), w_ffn_out=(4, FF_BLK, D))
    small_ex = {}
    small_ag = {}

    def start_gather(names, l, after):
        lands = [_cast_into_slot(w[k], l, me1, f"cast_{k}_l{l}") for k in names]
        started, tok = _gather2_start(lands, after, f"gather_start_{'_'.join(names)}_l{l}")
        gathers.update({(k, l): h for k, h in zip(names, started)})
        return tok

    def relay_gather(names, l, after):
        relayed, tok = _gather2_relay([gathers[(k, l)] for k in names], after, f"gather_relay_{'_'.join(names)}_l{l}")
        gathers.update({(k, l): h for k, h in zip(names, relayed)})
        return tok

    def get_w(k, l, after):
        if (k, l) == ("w_in", 1):
            after = relay_gather(_BIG[:1], l, after)
        return _gather2_wait(gathers[(k, l)], after, f"gather_wait_{k}_l{l}").reshape(views[k])

    def hook(stage, l, payload):
        if stage == "pre_inproj":
            return start_gather(_BIG[1:], l, payload)
        if stage == "pre_gmlp":
            tok = relay_gather(_BIG[1:], l, payload)
            return tok + start_gather(_BIG[:1], l + 1, tok) if l == 0 else tok
        if stage == "small_grads":
            (small_ex[l],), tok = _exchange_start([_pack_small(payload)], f"exchange_start_small_l{l}")
            return tok
        if stage == "mid_backward":
            return reduce_small(l + 1, payload) if l == 0 else None
        extra = reduce_small(0, payload["w_in"]) if (stage, l) == ("mixer_partials", 0) else None
        started, tok = _exchange_start(list(payload.values()), f"exchange_start_{'_'.join(payload)}_l{l}")
        exchanges.update({(k, l): h for k, h in zip(payload, started)})
        return tok if extra is None else tok + extra

    def reduce_small(l, after):
        got = _exchange_wait(small_ex[l], after, f"exchange_wait_small_l{l}")
        mine = _sum8_into_slot(got, me1, f"sum_small_l{l}")
        (small_ag[l],), tok = _gather_start([mine], got, f"gather_start_small_l{l}")
        return tok

    land = lax.dynamic_update_slice(jnp.zeros((N_DEV,) + packed.shape, F32), packed[None], (me, 0, 0))
    (lanes_handle,), token = _gather_start([land], packed, "gather_start_lanes")
    token = relay_gather(_BIG[:1], 0, start_gather(_BIG[:1], 0, token))
    lanes = _gather_wait(lanes_handle, token, "gather_wait_lanes")
    params = {k: w[k] for k in _REPL}
    off = 0
    for k, shp in zip(_LANE_SHARDED, lane_shapes):
        n = shp[0] * shp[1]
        params[k] = jnp.swapaxes(lanes[:, off:off + n], 0, 1).reshape(shp[0], shp[1], D)
        off += n
    loss, dx, dg1 = _local_step(x[0], loss_target[0], params, get_w, hook)

    out = {}
    after = dx
    for k, l in [(k, l) for k in ("w_ffn_out", "w_ffn_in") for l in (1, 0)] + [("w_out", 1), ("w_in", 1)]:
        got = _exchange_wait(exchanges[(k, l)], after, f"exchange_wait_{k}_l{l}")
        out[k] = _adam_shard(got, w[k], mom[k], var[k], l, out.get(k), f"adam_{k}_l{l}")
        after = out[k][3]
    g_small = [_gather_wait(small_ag[l], after, f"gather_wait_small_l{l}").reshape(SMALL_ROWS, HD) for l in (0, 1)]
    row0 = 0
    for k, rows in SMALL_MATRICES:
        res = _adam_matrix(*g_small, *[src[k].reshape(2, rows, HD) for src in (w, mom, var)], row0, f"adam_{k}")
        out[k] = [a.reshape(w[k].shape) for a in res]
        after = res[3]
        row0 += rows
    for k in ("w_out", "w_in"):
        got = _exchange_wait(exchanges[(k, 0)], after, f"exchange_wait_{k}_l0")
        out[k] = _adam_shard(got, w[k], mom[k], var[k], 0, out[k], f"adam_{k}_l0")
    out["w_ffn_in"] = [jnp.swapaxes(a, 1, 2) for a in out["w_ffn_in"]]
    as_rows = lambda a: a.reshape(1, D) if a.ndim == 1 else a
    vec = _adam_vectors(*g_small, _all_gather(dg1, out["w_in"][3], "gather_norm1_grad"), me1,
                        *[{k: as_rows(src[k]) for k, _ in SMALL_VECTORS} for src in (w, mom, var)])
    out.update({k: [a.reshape(w[k].shape) for a in res] for k, res in vec.items()})

    loss = lax.psum(loss[0, 0], MESH_AXES)
    return (loss, dx[None], *[out[k][0] for k in _ORDER], *[out[k][1] for k in _ORDER],
            *[out[k][2] for k in _ORDER], *[out[k][3] for k in _ORDER])
```

```python
import jax
import jax.numpy as jnp
from jax import lax
from jax.experimental import pallas as pl
from jax.experimental.pallas import tpu as pltpu

F32 = jnp.float32
BF16 = jnp.bfloat16
SDS = jax.ShapeDtypeStruct

D = 1024
N_IN = 6 * D
D_FF = 2816
N_DEV = 8
IN_BLK = N_IN // N_DEV
FF_BLK = 2 * D_FF // N_DEV
HEADS = 8
HD = 128
EPS = 1e-6
LRU_C = 8.0
MESH_AXES = ("x", "y", "c")

ADAM_LR = 0.001
ADAM_B1 = 0.9
ADAM_B2 = 0.999
ADAM_EPS = 1e-08
ADAM_WD = 0.01
ADAM_STEP = 10

VMEM_LIMIT = 56 * 2**20


def _cp(*sem, **kw):
    return pltpu.CompilerParams(dimension_semantics=sem, vmem_limit_bytes=VMEM_LIMIT, **kw)


def _row_tile(s):
    return 512 if s >= 1024 else s // 2


_GELU_C = 0.7978845608028654


def _gelu(x):
    t = jnp.tanh(_GELU_C * (x + 0.044715 * (x * x * x)))
    return 0.5 * x * (1.0 + t), t


def _gelu_grad(x, t):
    return 0.5 * (1.0 + t) + 0.5 * x * (1.0 - t * t) * (_GELU_C * (1.0 + 0.134145 * (x * x)))


def _sigmoid(x):
    return 0.5 + 0.5 * jnp.tanh(0.5 * x)


def _softplus(x):
    e = jnp.exp(-jnp.abs(x))
    w = 1.0 + e
    l1p = jnp.where(w == 1.0, e, jnp.log(w) * e / jnp.where(w == 1.0, 1.0, w - 1.0))
    return jnp.maximum(x, 0.0) + l1p


def _rms_fwd(x, g):
    r = lax.rsqrt(jnp.mean(x * x, axis=-1, keepdims=True) + EPS)
    return x * r * g


def _rms_bwd(x, g, dh):
    r = lax.rsqrt(jnp.mean(x * x, axis=-1, keepdims=True) + EPS)
    xh = x * r
    dxh = dh * g
    dx = r * (dxh - xh * jnp.mean(dxh * xh, axis=-1, keepdims=True))
    dg = jnp.sum(dh * xh, axis=0, keepdims=True)
    return dx, dg


LANE_ROWS = D // HD


def _add_rows128(ref, vec, row0=0):
    for i in range(vec.shape[0]):
        for k in range(LANE_ROWS):
            j = row0 + i * LANE_ROWS + k
            ref[j:j + 1, :] += vec[i:i + 1, k * HD:(k + 1) * HD]


def _dot(a, b):
    return jnp.dot(a, b, preferred_element_type=F32)


def _dot_nt(a, b):
    return lax.dot_general(a, b, (((1,), (1,)), ((), ())), preferred_element_type=F32)


def _dot_tn(a, b):
    return lax.dot_general(a, b, (((0,), (0,)), ((), ())), preferred_element_type=F32)


def _taps(prev, cur, nxt, tm):
    hr = prev.shape[0]
    ext = jnp.concatenate([prev, cur, nxt], axis=0)
    n = tm + 2 * hr
    sl = slice(hr, hr + tm)
    return (pltpu.roll(ext, 2, 0)[sl], pltpu.roll(ext, 1, 0)[sl], cur,
            pltpu.roll(ext, n - 1, 0)[sl], pltpu.roll(ext, n - 2, 0)[sl])


def _halo_specs(tm, s, col, rows=8):
    nb = s // rows
    r = tm // rows
    return (pl.BlockSpec((rows, D), lambda i: (jnp.maximum(i * r - 1, 0), col)),
            pl.BlockSpec((tm, D), lambda i: (i, col)),
            pl.BlockSpec((rows, D), lambda i: (jnp.minimum((i + 1) * r, nb - 1), col)))


def _halo_load(prev_ref, cur_ref, next_ref, fp, fn):
    return prev_ref[...].astype(F32) * fp, cur_ref[...].astype(F32), next_ref[...].astype(F32) * fn


def _halo_flags(nt):
    i = pl.program_id(0)
    return (i > 0).astype(F32), (i < nt - 1).astype(F32)


def _full(shape):
    nd = len(shape)
    return pl.BlockSpec(shape, lambda *_: (0,) * nd)


def _resident(shape):
    nd = len(shape)
    return pl.BlockSpec(shape, lambda *_: (0,) * nd, pipeline_mode=pl.Buffered(1))


def _norm_inproj(x, g, w, layer, tm):
    s = x.shape[0]

    def body(x_ref, g_ref, w_ref, z_ref, ht_ref):
        h32 = _rms_fwd(x_ref[...], g_ref[...])
        ht_ref[...] = h32.T.astype(BF16)
        h = h32.astype(BF16)
        for j in range(N_DEV):
            z_ref[:, j * IN_BLK:(j + 1) * IN_BLK] = _dot(h, w_ref[j]).astype(BF16)

    return pl.pallas_call(
        body, name=f"norm_inproj_l{layer}", grid=(s // tm,),
        in_specs=[pl.BlockSpec((tm, D), lambda i: (i, 0)), _full((1, D)), _resident((N_DEV, D, IN_BLK))],
        out_specs=[pl.BlockSpec((tm, N_IN), lambda i: (i, 0)), pl.BlockSpec((D, tm), lambda i: (0, i))],
        out_shape=[SDS((s, N_IN), BF16), SDS((D, s), BF16)],
        compiler_params=_cp("parallel"))(x, g, w)


def _gmlp_values(zu_ref, zv_ref, lng_ref, lnb_ref):
    zu = zu_ref[...].astype(F32)
    zv = zv_ref[...].astype(F32)
    u, tu = _gelu(zu)
    gv, tv = _gelu(zv)
    xc = gv - jnp.mean(gv, axis=-1, keepdims=True)
    rstd = lax.rsqrt(jnp.mean(xc * xc, axis=-1, keepdims=True) + EPS)
    xh = xc * rstd
    vb = (xh * lng_ref[...] + lnb_ref[...]).astype(BF16)
    return zu, zv, u, tu, tv, xh, rstd, vb


def _mixer_fwd(x, h0, h1, z, lng, lnb, ws, bsb, wo, layer, tm):
    s = x.shape[0]

    def body(x_ref, h0_ref, h1_ref, zu_ref, zv_ref, zg_ref, za_ref, zb_ref, lng_ref, lnb_ref, ws_ref, bsb_ref,
             wo_ref, x1_ref, mg_ref, ya_s):
        _, _, u, _, _, _, _, vb = _gmlp_values(zu_ref, zv_ref, lng_ref, lnb_ref)
        for c in range(tm // HD):
            rs = slice(c * HD, (c + 1) * HD)
            for g in range(HEADS):
                cs = slice(g * HD, (g + 1) * HD)
                ya_s[rs, cs] = u[rs, cs] * (_dot(ws_ref[g], vb[rs, cs]) + bsb_ref[g])
        gg, _ = _gelu(zg_ref[...].astype(F32))
        yb = (h0_ref[...] + h1_ref[...]) * gg
        m32 = _sigmoid(za_ref[...].astype(F32)) * ya_s[...] + _sigmoid(zb_ref[...].astype(F32)) * yb
        mg_ref[...] = m32.T.astype(BF16)
        x1_ref[...] = x_ref[...] + _dot(m32.astype(BF16), wo_ref[...])

    tile = pl.BlockSpec((tm, D), lambda i: (i, 0))
    wspec = _full((HEADS, HD, HD))
    return pl.pallas_call(
        body, name=f"mixer_fwd_l{layer}", grid=(s // tm,),
        in_specs=[tile, tile, tile] + [pl.BlockSpec((tm, D), lambda i, c=c: (i, c)) for c in (0, 1, 3, 4, 5)]
        + [_full((1, D)), _full((1, D)), wspec, wspec, _full((D, D))],
        out_specs=[tile, pl.BlockSpec((D, tm), lambda i: (0, i))], out_shape=[SDS((s, D), F32), SDS((D, s), BF16)],
        scratch_shapes=[pltpu.VMEM((tm, D), F32)],
        compiler_params=_cp("parallel"))(x, h0, h1, z, z, z, z, z, lng, lnb, ws, bsb, wo)


def _conv(taps, cw_ref, cb_ref):
    _, m1, c0, p1, p2 = taps
    return cb_ref[...] + m1 * cw_ref[0:1, :] + c0 * cw_ref[1:2, :] + p1 * cw_ref[2:3, :] + p2 * cw_ref[3:4, :]


def _heads_dot(xb, w_ref, d):
    return jnp.concatenate([_dot(xb[:, h * HD:(h + 1) * HD], w_ref[d, h]) for h in range(HEADS)], axis=1)


def _lru_decay(r, sp):
    la = (-LRU_C) * r * sp
    a = jnp.exp(la)
    return a, jnp.tanh(-la) * (a * a + 1.0)


def _lru_gates_fwd(z, cw, cb, wr, wi, br, bi, lam, layer, tm):
    s = z.shape[0]
    nt = s // tm

    def body(zp_ref, zc_ref, zn_ref, cw_ref, cb_ref, wr_ref, wi_ref, br_ref, bi_ref, lam_ref,
             a0_ref, b0_ref, a1_ref, b1_ref, xc_ref, r0_ref, i0_ref, r1_ref, i1_ref):
        fp, fn = _halo_flags(nt)
        xc = _conv(_taps(*_halo_load(zp_ref, zc_ref, zn_ref, fp, fn), tm), cw_ref, cb_ref)
        xb = xc.astype(BF16)
        xc_ref[...] = xb
        for d, (a_ref, b_ref, r_ref, i_ref) in enumerate(((a0_ref, b0_ref, r0_ref, i0_ref),
                                                          (a1_ref, b1_ref, r1_ref, i1_ref))):
            r = _sigmoid(_heads_dot(xb, wr_ref, d) + br_ref[d:d + 1, :])
            ig = _sigmoid(_heads_dot(xb, wi_ref, d) + bi_ref[d:d + 1, :])
            a, q = _lru_decay(r, _softplus(-lam_ref[d:d + 1, :]))
            a_ref[...] = a
            b_ref[...] = jnp.sqrt(q) * (ig * xc)
            r_ref[...] = r.astype(BF16)
            i_ref[...] = ig.astype(BF16)

    tile = pl.BlockSpec((tm, D), lambda i: (i, 0))
    return pl.pallas_call(
        body, name=f"lru_gates_fwd_l{layer}", grid=(nt,),
        in_specs=[*_halo_specs(tm, s, 2, 16), _full((4, D)), _full((1, D)),
                  _full((2, HEADS, HD, HD)), _full((2, HEADS, HD, HD)), _full((2, D)), _full((2, D)), _full((2, D))],
        out_specs=[tile] * 9, out_shape=[SDS((s, D), F32)] * 4 + [SDS((s, D), BF16)] * 5,
        compiler_params=_cp("parallel"))(z, z, z, cw, cb, wr, wi, br, bi, lam)


def _scan_group(a, x, c, reverse, bwd):
    row = lax.broadcasted_iota(jnp.int32, a.shape, 0)
    b = a * x if bwd else x
    for d in (1, 2, 4):
        keep = (row < 8 - d) if reverse else (row >= d)
        sh = 8 - d if reverse else d
        a_s = jnp.where(keep, pltpu.roll(a, sh, 0), 1.0)
        b_s = jnp.where(keep, pltpu.roll(b, sh, 0), 0.0)
        b = a * b_s + b
        a = a * a_s
    h = b + a * c
    new_c = h[0:1, :] if reverse else h[7:8, :]
    if not bwd:
        return h, new_c
    if reverse:
        prev = jnp.where(row < 7, pltpu.roll(h, 7, 0), c)
    else:
        prev = jnp.where(row >= 1, pltpu.roll(h, 1, 0), c)
    return x + prev, new_c


def _lru_scan(a_f, x_f, a_r, x_r, bwd, layer):
    s = a_f.shape[0]
    ts = min(1024, s // 2)
    cb = 512
    nt = s // ts
    ng = ts // 8

    def body(af_ref, xf_ref, ar_ref, xr_ref, of_ref, or_ref, cf, cr):
        @pl.when(pl.program_id(1) == 0)
        def _():
            cf[...] = jnp.zeros_like(cf)
            cr[...] = jnp.zeros_like(cr)

        def step(j, carry):
            c_f, c_r = carry
            rf = pl.multiple_of(j * 8, 8)
            rr = pl.multiple_of((ng - 1 - j) * 8, 8)
            o, c_f = _scan_group(af_ref[pl.ds(rf, 8), :], xf_ref[pl.ds(rf, 8), :], c_f, False, bwd)
            of_ref[pl.ds(rf, 8), :] = o
            o, c_r = _scan_group(ar_ref[pl.ds(rr, 8), :], xr_ref[pl.ds(rr, 8), :], c_r, True, bwd)
            or_ref[pl.ds(rr, 8), :] = o
            return c_f, c_r

        c_f, c_r = lax.fori_loop(0, ng, step, (cf[0:1, :], cr[0:1, :]), unroll=2)
        cf[...] = jnp.broadcast_to(c_f, cf.shape)
        cr[...] = jnp.broadcast_to(c_r, cr.shape)

    fwd = pl.BlockSpec((ts, cb), lambda c, t: (t, c))
    rev = pl.BlockSpec((ts, cb), lambda c, t: (nt - 1 - t, c))
    return pl.pallas_call(
        body, name=f"lru_scan_{'bwd' if bwd else 'fwd'}_l{layer}", grid=(D // cb, nt),
        in_specs=[fwd, fwd, rev, rev], out_specs=[fwd, rev],
        out_shape=[SDS((s, D), F32)] * 2,
        scratch_shapes=[pltpu.VMEM((8, cb), F32), pltpu.VMEM((8, cb), F32)],
        compiler_params=_cp("parallel", "arbitrary"))(a_f, x_f, a_r, x_r)


def _ffn_fwd(x1, g, wfi, wfo, layer, tm):
    s = x1.shape[0]

    def body(x_ref, g_ref, wi_ref, wo_ref, x2_ref, gu_ref, h_ref):
        x = x_ref[...]
        h = _rms_fwd(x, g_ref[...]).astype(BF16)
        h_ref[...] = h
        acc = x
        for k in range(4):
            gate = _dot_nt(h, wi_ref[0, k])
            up = _dot_nt(h, wi_ref[1, k])
            gu_ref[0, k] = gate.astype(BF16)
            gu_ref[1, k] = up.astype(BF16)
            acc = acc + _dot((gate * _sigmoid(gate) * up).astype(BF16), wo_ref[k])
        x2_ref[...] = acc

    tile = pl.BlockSpec((tm, D), lambda i: (i, 0))
    return pl.pallas_call(
        body, name=f"ffn_fwd_l{layer}", grid=(s // tm,),
        in_specs=[tile, _full((1, D)), _resident((2, 4, FF_BLK, D)), _resident((4, FF_BLK, D))],
        out_specs=[tile, pl.BlockSpec((2, 4, tm, FF_BLK), lambda i: (0, 0, i, 0)), tile],
        out_shape=[SDS((s, D), F32), SDS((2, 4, s, FF_BLK), BF16), SDS((s, D), BF16)],
        compiler_params=_cp("parallel"))(x1, g, wfi, wfo)


def _loss_head(x, g, tgt, tm):
    s = x.shape[0]

    def body(x_ref, g_ref, t_ref, dx_ref, loss_ref, dg_ref):
        @pl.when(pl.program_id(0) == 0)
        def _():
            loss_ref[...] = jnp.zeros_like(loss_ref)
            dg_ref[...] = jnp.zeros_like(dg_ref)

        x = x_ref[...]
        gv = g_ref[...]
        e = _rms_fwd(x, gv) - t_ref[...]
        rows = jnp.sum(e * e, axis=-1, keepdims=True)
        loss_ref[...] += (0.5 / D) * jnp.sum(rows, axis=0, keepdims=True)
        dx, dg = _rms_bwd(x, gv, e * (1.0 / D))
        dx_ref[...] = dx
        _add_rows128(dg_ref, dg)

    tile = pl.BlockSpec((tm, D), lambda i: (i, 0))
    return pl.pallas_call(
        body, name="loss_head", grid=(s // tm,),
        in_specs=[tile, _full((1, D)), tile],
        out_specs=[tile, _full((1, 1)), _full((LANE_ROWS, HD))],
        out_shape=[SDS((s, D), F32), SDS((1, 1), F32), SDS((LANE_ROWS, HD), F32)],
        compiler_params=_cp("arbitrary"))(x, g, tgt)


def _ffn_bwd_act(dx2, wfo, gu, layer, tm):
    s = dx2.shape[0]

    def body(dx_ref, wo_ref, gu_ref, ff_ref, dgu_ref):
        dxb = dx_ref[...].astype(BF16)
        for k in range(4):
            dff = _dot_nt(dxb, wo_ref[k])
            gate = gu_ref[0, k].astype(F32)
            up = gu_ref[1, k].astype(F32)
            sg = _sigmoid(gate)
            sl = gate * sg
            ff_ref[k] = (sl * up).astype(BF16)
            dgu_ref[0, k] = (dff * up * (sg * (1.0 + gate * (1.0 - sg)))).astype(BF16)
            dgu_ref[1, k] = (dff * sl).astype(BF16)

    blk = pl.BlockSpec((2, 4, tm, FF_BLK), lambda i: (0, 0, i, 0))
    return pl.pallas_call(
        body, name=f"ffn_bwd_act_l{layer}", grid=(s // tm,),
        in_specs=[pl.BlockSpec((tm, D), lambda i: (i, 0)), _resident((4, FF_BLK, D)), blk],
        out_specs=[pl.BlockSpec((4, tm, FF_BLK), lambda i: (0, i, 0)), blk],
        out_shape=[SDS((4, s, FF_BLK), BF16), SDS((2, 4, s, FF_BLK), BF16)],
        compiler_params=_cp("parallel"))(dx2, wfo, gu)


def _mm_nt_rms_bwd(a, a_spec, a_blocks, w, w_is_transposed, x, g, dres, name, tm):
    s = x.shape[0]

    def body(a_ref, w_ref, x_ref, g_ref, dres_ref, dx_ref, dg_ref):
        @pl.when(pl.program_id(0) == 0)
        def _():
            dg_ref[...] = jnp.zeros_like(dg_ref)

        dh = None
        for k, blk in enumerate(a_blocks(a_ref)):
            part = _dot(blk, w_ref[k]) if w_is_transposed else _dot_nt(blk, w_ref[k])
            dh = part if dh is None else dh + part
        dx, dg = _rms_bwd(x_ref[...], g_ref[...], dh)
        dx_ref[...] = dres_ref[...] + dx
        _add_rows128(dg_ref, dg)

    tile = pl.BlockSpec((tm, D), lambda i: (i, 0))
    return pl.pallas_call(
        body, name=name, grid=(s // tm,),
        in_specs=[a_spec, _resident(w.shape), tile, _full((1, D)), tile],
        out_specs=[tile, _full((LANE_ROWS, HD))], out_shape=[SDS((s, D), F32), SDS((LANE_ROWS, HD), F32)],
        compiler_params=_cp("arbitrary"))(a, w, x, g, dres)


def _mm_tn(a, a_spec, b, b_spec, nb, out_shape, out_spec, name, a_is_transposed=True, after=None):
    def body(a_ref, b_ref, *rest):
        o_ref = rest[-1]
        bb = b_ref[...].astype(BF16)
        o_ref[...] = (_dot(a_ref[...], bb) if a_is_transposed else _dot_tn(a_ref[...], bb)).astype(BF16)

    deps = [] if after is None else [after]
    return pl.pallas_call(
        body, name=name, grid=(nb,), in_specs=[a_spec, b_spec] + [_ANY] * len(deps), out_specs=out_spec,
        out_shape=SDS(out_shape, BF16), compiler_params=_cp("parallel"))(a, b, *deps)


def _mixer_bwd(dx1, wo, h0, h1, z, lng, lnb, ws, wst, bsb, layer, tm):
    s = dx1.shape[0]
    nt = s // tm

    def body(dx_ref, wo_ref, h0_ref, h1_ref, zu_ref, zv_ref, zg_ref, za_ref, zb_ref, lng_ref, lnb_ref,
             ws_ref, wst_ref, bsb_ref, dz_ref, dh_ref, dws_ref, dbs_ref, dlng_ref, dlnb_ref,
             du_s, dv_s, ya_s, dbs_acc):
        i = pl.program_id(0)

        @pl.when(i == 0)
        def _():
            for r in (dws_ref, dlng_ref, dlnb_ref, dbs_acc):
                r[...] = jnp.zeros_like(r)

        dm = _dot_nt(dx_ref[...].astype(BF16), wo_ref[...])
        sa = _sigmoid(za_ref[...].astype(F32))
        sb = _sigmoid(zb_ref[...].astype(F32))
        zg = zg_ref[...].astype(F32)
        gg, tg = _gelu(zg)
        hs = h0_ref[...] + h1_ref[...]
        dyb = dm * sb
        dya = dm * sa
        dh_ref[...] = dyb * gg
        dz_ref[:, 2 * D:3 * D] = jnp.zeros((tm, D), BF16)
        dz_ref[:, 3 * D:4 * D] = (dyb * hs * _gelu_grad(zg, tg)).astype(BF16)
        dz_ref[:, 5 * D:6 * D] = (dm * (hs * gg) * (sb * (1.0 - sb))).astype(BF16)

        zu, zv, u, tu, tv, xh, rstd, vb = _gmlp_values(zu_ref, zv_ref, lng_ref, lnb_ref)
        for c in range(tm // HD):
            rs = slice(c * HD, (c + 1) * HD)
            for g in range(HEADS):
                cs = slice(g * HD, (g + 1) * HD)
                vblk = vb[rs, cs]
                mixed = _dot(ws_ref[g], vblk) + bsb_ref[g]
                ya_s[rs, cs] = u[rs, cs] * mixed
                du_s[rs, cs] = dya[rs, cs] * mixed
                dmx = dya[rs, cs] * u[rs, cs]
                dbs_acc[g] += dmx
                dmxb = dmx.astype(BF16)
                dws_ref[g] += _dot_nt(dmxb, vblk)
                dv_s[rs, cs] = _dot(wst_ref[g], dmxb)
        dz_ref[:, 4 * D:5 * D] = (dm * ya_s[...] * (sa * (1.0 - sa))).astype(BF16)
        dv = dv_s[...]
        _add_rows128(dlng_ref, jnp.sum(dv * xh, axis=0, keepdims=True))
        _add_rows128(dlnb_ref, jnp.sum(dv, axis=0, keepdims=True))
        dxh = dv * lng_ref[...]
        dgv = rstd * (dxh - jnp.mean(dxh, axis=-1, keepdims=True)
                      - xh * jnp.mean(dxh * xh, axis=-1, keepdims=True))
        dz_ref[:, 0:D] = (du_s[...] * _gelu_grad(zu, tu)).astype(BF16)
        dz_ref[:, D:2 * D] = (dgv * _gelu_grad(zv, tv)).astype(BF16)

        @pl.when(i == nt - 1)
        def _():
            for g in range(HEADS):
                dbs_ref[g:g + 1, :] = jnp.sum(dbs_acc[g].T, axis=0, keepdims=True)

    tile = pl.BlockSpec((tm, D), lambda i: (i, 0))
    wspec = _full((HEADS, HD, HD))
    return pl.pallas_call(
        body, name=f"mixer_bwd_l{layer}", grid=(nt,),
        in_specs=[tile, _full((D, D)), tile, tile]
        + [pl.BlockSpec((tm, D), lambda i, c=c: (i, c)) for c in (0, 1, 3, 4, 5)]
        + [_full((1, D)), _full((1, D)), wspec, wspec, wspec],
        out_specs=[pl.BlockSpec((tm, N_IN), lambda i: (i, 0)), tile, wspec, _full((HEADS, HD)),
                   _full((LANE_ROWS, HD)), _full((LANE_ROWS, HD))],
        out_shape=[SDS((s, N_IN), BF16), SDS((s, D), F32), SDS((HEADS, HD, HD), F32), SDS((HEADS, HD), F32),
                   SDS((LANE_ROWS, HD), F32), SDS((LANE_ROWS, HD), F32)],
        scratch_shapes=[pltpu.VMEM((tm, D), F32)] * 3 + [pltpu.VMEM((HEADS, HD, HD), F32)],
        compiler_params=_cp("arbitrary"))(dx1, wo, h0, h1, z, z, z, z, z, lng, lnb, ws, wst, bsb)


def _lru_gates_bwd(xcb, gates, h0, h1, g0, g1, wr, wi, lam, layer, tm):
    s = xcb.shape[0]
    nt = s // tm

    def body(xc_ref, r0_ref, i0_ref, r1_ref, i1_ref, h0p_ref, h0_ref, h1_ref, h1n_ref, g0_ref, g1_ref,
             wr_ref, wi_ref, lam_ref, dxc_ref, dwr_ref, dwi_ref, dbr_ref, dbi_ref, dlam_ref):
        i = pl.program_id(0)
        fp, fn = _halo_flags(nt)

        @pl.when(i == 0)
        def _():
            for r in (dwr_ref, dwi_ref, dbr_ref, dbi_ref, dlam_ref):
                r[...] = jnp.zeros_like(r)

        xb = xc_ref[...]
        xc = xb.astype(F32)
        zeros8 = jnp.zeros((8, D), F32)
        h_prev = _taps(h0p_ref[...] * fp, h0_ref[...], zeros8, tm)[1]
        h_next = _taps(zeros8, h1_ref[...], h1n_ref[...] * fn, tm)[3]
        dxc = jnp.zeros((tm, D), F32)
        for d, (g_ref, hsh, r_ref, i_ref) in enumerate(((g0_ref, h_prev, r0_ref, i0_ref),
                                                        (g1_ref, h_next, r1_ref, i1_ref))):
            sp = _softplus(-lam_ref[d:d + 1, :])
            r = r_ref[...].astype(F32)
            ig = i_ref[...].astype(F32)
            a, q = _lru_decay(r, sp)
            rmult = jnp.where(q > 0.0, lax.rsqrt(jnp.where(q > 0.0, q, 1.0)), 0.0)
            mult = q * rmult
            db = g_ref[...]
            da = db * hsh
            dmult = db * (ig * xc)
            di = db * (mult * xc)
            dxc = dxc + db * (mult * ig)
            dla = da * a - dmult * (a * a * rmult)
            dsp_dlam = -_sigmoid(-lam_ref[d:d + 1, :])
            _add_rows128(dlam_ref, jnp.sum(dla * r, axis=0, keepdims=True) * ((-LRU_C) * dsp_dlam), d * LANE_ROWS)
            dpr = dla * sp * (-LRU_C) * (r * (1.0 - r))
            dpi = di * (ig * (1.0 - ig))
            _add_rows128(dbr_ref, jnp.sum(dpr, axis=0, keepdims=True), d * LANE_ROWS)
            _add_rows128(dbi_ref, jnp.sum(dpi, axis=0, keepdims=True), d * LANE_ROWS)
            dprb = dpr.astype(BF16)
            dpib = dpi.astype(BF16)
            parts = []
            for h in range(HEADS):
                cs = slice(h * HD, (h + 1) * HD)
                dwr_ref[d, h] += _dot_tn(xb[:, cs], dprb[:, cs])
                dwi_ref[d, h] += _dot_tn(xb[:, cs], dpib[:, cs])
                parts.append(_dot_nt(dprb[:, cs], wr_ref[d, h]) + _dot_nt(dpib[:, cs], wi_ref[d, h]))
            dxc = dxc + jnp.concatenate(parts, axis=1)
        dxc_ref[...] = dxc.astype(BF16)

    tile = pl.BlockSpec((tm, D), lambda i: (i, 0))
    hp, hc, hn = _halo_specs(tm, s, 0)
    wspec = _full((2, HEADS, HD, HD))
    vspec = _full((2 * LANE_ROWS, HD))
    return pl.pallas_call(
        body, name=f"lru_gates_bwd_l{layer}", grid=(nt,),
        in_specs=[tile] * 5 + [hp, hc, hc, hn, tile, tile, wspec, wspec, _full((2, D))],
        out_specs=[tile, wspec, wspec, vspec, vspec, vspec],
        out_shape=[SDS((s, D), BF16), SDS((2, HEADS, HD, HD), F32), SDS((2, HEADS, HD, HD), F32)]
        + [SDS((2 * LANE_ROWS, HD), F32)] * 3,
        compiler_params=_cp("arbitrary"))(xcb, *gates, h0, h0, h1, h1, g0, g1, wr, wi, lam)


def _conv_bwd(dz, dxc, z, cw, layer, tm):
    s = z.shape[0]
    nt = s // tm

    def body(dz_in, dp_ref, dc_ref, dn_ref, zp_ref, zc_ref, zn_ref, cw_ref, dz_ref, dcw_ref, dcb_ref):
        del dz_in
        fp, fn = _halo_flags(nt)

        @pl.when(pl.program_id(0) == 0)
        def _():
            dcw_ref[...] = jnp.zeros_like(dcw_ref)
            dcb_ref[...] = jnp.zeros_like(dcb_ref)

        dxc_halo = _halo_load(dp_ref, dc_ref, dn_ref, fp, fn)
        dxc = dxc_halo[1]
        dm2, dm1, _, dp1, _ = _taps(*dxc_halo, tm)
        dz_ref[...] = (cw_ref[0:1, :] * dp1 + cw_ref[1:2, :] * dxc + cw_ref[2:3, :] * dm1
                       + cw_ref[3:4, :] * dm2).astype(BF16)
        _, zm1, z0, zp1, zp2 = _taps(*_halo_load(zp_ref, zc_ref, zn_ref, fp, fn), tm)
        for k, zt in enumerate((zm1, z0, zp1, zp2)):
            _add_rows128(dcw_ref, jnp.sum(dxc * zt, axis=0, keepdims=True), k * LANE_ROWS)
        _add_rows128(dcb_ref, jnp.sum(dxc, axis=0, keepdims=True))

    return pl.pallas_call(
        body, name=f"conv_bwd_l{layer}", grid=(nt,),
        in_specs=[pl.BlockSpec(memory_space=pl.ANY), *_halo_specs(tm, s, 0, 16), *_halo_specs(tm, s, 2, 16),
                  _full((4, D))],
        out_specs=[pl.BlockSpec((tm, D), lambda i: (i, 2)), _full((4 * LANE_ROWS, HD)), _full((LANE_ROWS, HD))],
        out_shape=[SDS((s, N_IN), BF16), SDS((4 * LANE_ROWS, HD), F32), SDS((LANE_ROWS, HD), F32)],
        input_output_aliases={0: 0},
        compiler_params=_cp("arbitrary"))(dz, dxc, dxc, dxc, z, z, z, cw)


def _me():
    return lax.axis_index("x"), lax.axis_index("y"), lax.axis_index("c")


def _peer(m):
    x, y, c = _me()
    px = 1 - x if m & 4 else x
    py = 1 - y if m & 2 else y
    pc = 1 - c if m & 1 else c
    return (px, py, pc), 4 * px + 2 * py + pc


_ANY = pl.BlockSpec(memory_space=pl.ANY)
_EXCHANGE_SEMS = [pltpu.SemaphoreType.DMA((N_DEV - 1,)), pltpu.SemaphoreType.DMA((N_DEV - 1,)), pltpu.SemaphoreType.DMA(())]


def _all_gather(v, after, name):
    def body(v_ref, after_ref, o_ref, send_sems, recv_sems, local_sem):
        del after_ref
        x, y, c = _me()
        me = 4 * x + 2 * y + c
        local = pltpu.make_async_copy(v_ref, o_ref.at[me], local_sem)
        local.start()
        sends = []
        for m in range(1, N_DEV):
            dev, _ = _peer(m)
            cp = pltpu.make_async_remote_copy(v_ref, o_ref.at[me], send_sems.at[m - 1], recv_sems.at[m - 1],
                                              device_id=dev, device_id_type=pl.DeviceIdType.MESH)
            cp.start()
            sends.append(cp)
        for m in range(1, N_DEV):
            dev, blk = _peer(m)
            pltpu.make_async_remote_copy(v_ref, o_ref.at[blk], send_sems.at[m - 1], recv_sems.at[m - 1],
                                         device_id=dev, device_id_type=pl.DeviceIdType.MESH).wait_recv()
        for cp in sends:
            cp.wait_send()
        local.wait()

    return pl.pallas_call(
        body, name=name, in_specs=[_ANY, _ANY], out_specs=_ANY,
        out_shape=SDS((N_DEV,) + v.shape, v.dtype), scratch_shapes=_EXCHANGE_SEMS)(v, after)


_HBM = pl.BlockSpec(memory_space=pltpu.HBM)
_SEM = pl.BlockSpec(memory_space=pltpu.SEMAPHORE)
_EFFECT = pltpu.CompilerParams(has_side_effects=pltpu.SideEffectType.DATAFLOW_SIDE_EFFECTING)
_PEER_SEMS = pltpu.SemaphoreType.DMA((N_DEV - 1,))


def _in_hbm(a):
    return pltpu.with_memory_space_constraint(a, pltpu.HBM)


def _remote(src, dst, send_sems, recv_sems, m):
    dev, _ = _peer(m)
    return pltpu.make_async_remote_copy(src, dst, send_sems.at[m - 1], recv_sems.at[m - 1],
                                        device_id=dev, device_id_type=pl.DeviceIdType.MESH)


def _gather_start(lands, after, name):
    n = len(lands)

    def body(*refs):
        land = refs[:n]
        sems = refs[n + 1:3 * n + 1]
        token = refs[-1]
        x, y, c = _me()
        me = 4 * x + 2 * y + c
        for t in range(n):
            for m in range(1, N_DEV):
                _remote(land[t].at[me], land[t].at[me], sems[2 * t], sems[2 * t + 1], m).start()
        token[...] = jnp.zeros_like(token)

    res = pl.pallas_call(
        body, name=name, in_specs=[_HBM] * n + [_ANY],
        out_specs=[_SEM] * (2 * n) + [_HBM] * n + [pl.BlockSpec(memory_space=pltpu.VMEM)],
        out_shape=[_PEER_SEMS] * (2 * n) + [pltpu.HBM(a.shape, a.dtype) for a in lands] + [SDS((8, 128), F32)],
        input_output_aliases={t: 2 * n + t for t in range(n)},
        compiler_params=_EFFECT)(*[_in_hbm(a) for a in lands], after)
    return [(res[2 * t], res[2 * t + 1], res[2 * n + t]) for t in range(n)], res[-1]


def _gather_wait(handle, after, name):
    send_sems, recv_sems, land = handle

    def body(land_ref, ssem, rsem, after_ref, out_ref):
        del after_ref, out_ref
        x, y, c = _me()
        me = 4 * x + 2 * y + c
        for m in range(1, N_DEV):
            _, blk = _peer(m)
            cp = _remote(land_ref.at[me], land_ref.at[blk], ssem, rsem, m)
            cp.wait_send()
            cp.wait_recv()

    return pl.pallas_call(
        body, name=name, in_specs=[_HBM, _SEM, _SEM, _ANY], out_specs=_HBM,
        out_shape=pltpu.HBM(land.shape, land.dtype), input_output_aliases={0: 0},
        compiler_params=_EFFECT)(land, send_sems, recv_sems, after)


FIRST_STAGE = (1, 2, 4, 6)
RELAYED = (2, 4, 6)
OTHER_CORE = 1


def _stage_copy(src, dst, send_sems, recv_sems, k, m):
    dev, _ = _peer(m)
    return pltpu.make_async_remote_copy(src, dst, send_sems.at[k], recv_sems.at[k],
                                        device_id=dev, device_id_type=pl.DeviceIdType.MESH)


def _gather2_start(lands, after, name):
    n = len(lands)

    def body(*refs):
        land = refs[:n]
        sems = refs[n + 1:3 * n + 1]
        token = refs[-1]
        x, y, c = _me()
        me = 4 * x + 2 * y + c
        for t in range(n):
            for k, m in enumerate(FIRST_STAGE):
                _stage_copy(land[t].at[me], land[t].at[me], sems[2 * t], sems[2 * t + 1], k, m).start()
        token[...] = jnp.zeros_like(token)

    stage_sems = pltpu.SemaphoreType.DMA((len(FIRST_STAGE),))
    res = pl.pallas_call(
        body, name=name, in_specs=[_HBM] * n + [_ANY],
        out_specs=[_SEM] * (2 * n) + [_HBM] * n + [pl.BlockSpec(memory_space=pltpu.VMEM)],
        out_shape=[stage_sems] * (2 * n) + [pltpu.HBM(a.shape, a.dtype) for a in lands] + [SDS((8, 128), F32)],
        input_output_aliases={t: 2 * n + t for t in range(n)},
        compiler_params=_EFFECT)(*[_in_hbm(a) for a in lands], after)
    return [(res[2 * t], res[2 * t + 1], res[2 * n + t]) for t in range(n)], res[-1]


def _gather2_relay(handles, after, name):
    n = len(handles)

    def body(*refs):
        land, send1, recv1 = refs[:n], refs[n:2 * n], refs[2 * n:3 * n]
        sems = refs[3 * n + 1:5 * n + 1]
        token = refs[-1]
        x, y, c = _me()
        me = 4 * x + 2 * y + c
        for t in range(n):
            for j, m in enumerate(RELAYED):
                _, blk = _peer(m)
                _stage_copy(land[t].at[me], land[t].at[blk], send1[t], recv1[t], 1 + j, m).wait_recv()
                _stage_copy(land[t].at[blk], land[t].at[blk], sems[2 * t], sems[2 * t + 1], j, OTHER_CORE).start()
        token[...] = jnp.zeros_like(token)

    relay_sems = pltpu.SemaphoreType.DMA((len(RELAYED),))
    lands = [h[2] for h in handles]
    res = pl.pallas_call(
        body, name=name, in_specs=[_HBM] * n + [_SEM] * (2 * n) + [_ANY],
        out_specs=[_SEM] * (2 * n) + [_HBM] * n + [pl.BlockSpec(memory_space=pltpu.VMEM)],
        out_shape=[relay_sems] * (2 * n) + [pltpu.HBM(a.shape, a.dtype) for a in lands] + [SDS((8, 128), F32)],
        input_output_aliases={t: 2 * n + t for t in range(n)},
        compiler_params=_EFFECT)(*lands, *[h[0] for h in handles], *[h[1] for h in handles], after)
    return [(h[0], h[1], res[2 * t], res[2 * t + 1], res[2 * n + t]) for t, h in enumerate(handles)], res[-1]


def _gather2_wait(handle, after, name):
    send1, recv1, send2, recv2, land = handle

    def body(land_ref, s1, r1, s2, r2, after_ref, out_ref):
        del after_ref, out_ref
        x, y, c = _me()
        me = 4 * x + 2 * y + c
        _, other = _peer(OTHER_CORE)
        _stage_copy(land_ref.at[me], land_ref.at[other], s1, r1, 0, OTHER_CORE).wait_recv()
        for k, m in enumerate(FIRST_STAGE):
            _stage_copy(land_ref.at[me], land_ref.at[me], s1, r1, k, m).wait_send()
        for j, m in enumerate(RELAYED):
            _, mine = _peer(m)
            _, theirs = _peer(m ^ OTHER_CORE)
            _stage_copy(land_ref.at[mine], land_ref.at[mine], s2, r2, j, OTHER_CORE).wait_send()
            _stage_copy(land_ref.at[mine], land_ref.at[theirs], s2, r2, j, OTHER_CORE).wait_recv()

    return pl.pallas_call(
        body, name=name, in_specs=[_HBM] + [_SEM] * 4 + [_ANY], out_specs=_HBM,
        out_shape=pltpu.HBM(land.shape, land.dtype), input_output_aliases={0: 0},
        compiler_params=_EFFECT)(land, send1, recv1, send2, recv2, after)


def _exchange_start(ps, name):
    n = len(ps)

    def body(*refs):
        p = refs[:n]
        got = refs[n:2 * n]
        sems = refs[2 * n:5 * n]
        token = refs[-1]
        x, y, c = _me()
        me = 4 * x + 2 * y + c
        for t in range(n):
            pltpu.make_async_copy(p[t].at[me], got[t].at[me], sems[3 * t + 2]).start()
            for m in range(1, N_DEV):
                _, blk = _peer(m)
                _remote(p[t].at[blk], got[t].at[me], sems[3 * t], sems[3 * t + 1], m).start()
        token[...] = jnp.zeros_like(token)

    res = pl.pallas_call(
        body, name=name, in_specs=[_HBM] * (2 * n),
        out_specs=[_SEM] * (3 * n) + [_HBM] * (2 * n) + [pl.BlockSpec(memory_space=pltpu.VMEM)],
        out_shape=[_PEER_SEMS, _PEER_SEMS, pltpu.SemaphoreType.DMA(())] * n
        + [pltpu.HBM(a.shape, a.dtype) for a in ps] * 2 + [SDS((8, 128), F32)],
        input_output_aliases={t: 3 * n + t for t in range(2 * n)},
        compiler_params=_EFFECT)(*[_in_hbm(a) for a in ps], *[_in_hbm(lax.empty(a.shape, a.dtype)) for a in ps])
    return [(res[3 * t], res[3 * t + 1], res[3 * t + 2], res[3 * n + t], res[4 * n + t]) for t in range(n)], res[-1]


def _exchange_wait(handle, after, name):
    send_sems, recv_sems, local_sem, p, got = handle

    def body(p_ref, got_ref, ssem, rsem, lsem, after_ref, p_out, got_out):
        del after_ref, p_out, got_out
        x, y, c = _me()
        me = 4 * x + 2 * y + c
        pltpu.make_async_copy(p_ref.at[me], got_ref.at[me], lsem).wait()
        for m in range(1, N_DEV):
            _, blk = _peer(m)
            cp = _remote(p_ref.at[blk], got_ref.at[blk], ssem, rsem, m)
            cp.wait_send()
            cp.wait_recv()

    return pl.pallas_call(
        body, name=name, in_specs=[_HBM, _HBM, _SEM, _SEM, _SEM, _ANY], out_specs=[_HBM, _HBM],
        out_shape=[pltpu.HBM(p.shape, p.dtype), pltpu.HBM(got.shape, got.dtype)],
        input_output_aliases={0: 0, 1: 1}, compiler_params=_EFFECT)(p, got, send_sems, recv_sems, local_sem, after)[1]


def _cast_into_slot(w, layer, me1, name):
    _, r, c = w.shape
    tr = next(t for t in (256, 352, r) if r % t == 0)

    def body(me_ref, w_ref, o_ref):
        del me_ref
        o_ref[...] = w_ref[...].astype(BF16)

    return pl.pallas_call(
        body, name=name,
        grid_spec=pltpu.PrefetchScalarGridSpec(
            num_scalar_prefetch=1, grid=(r // tr,),
            in_specs=[pl.BlockSpec((None, tr, c), lambda i, me: (layer, i, 0))],
            out_specs=pl.BlockSpec((None, tr, c), lambda i, me: (me[0], i, 0))),
        out_shape=SDS((N_DEV, r, c), BF16), compiler_params=_cp("arbitrary"))(me1, w)


def _cast_all_into_slots(ws, layers, me1, after, name):
    n = len(ws)

    def body(me_ref, *refs):
        del me_ref
        for w_ref, o_ref in zip(refs[:n], refs[n + 1:]):
            o_ref[...] = w_ref[...].astype(BF16)

    return pl.pallas_call(
        body, name=name,
        grid_spec=pltpu.PrefetchScalarGridSpec(
            num_scalar_prefetch=1, grid=(1,),
            in_specs=[pl.BlockSpec((None,) + a.shape[1:], lambda i, me, l=l: (l, 0, 0)) for a, l in zip(ws, layers)]
            + [_ANY],
            out_specs=[pl.BlockSpec((None,) + a.shape[1:], lambda i, me: (me[0], 0, 0)) for a in ws]),
        out_shape=[SDS((N_DEV,) + a.shape[1:], BF16) for a in ws],
        compiler_params=_cp("arbitrary"))(me1, *ws, after)


def _sum8_into_slot(p, me1, name):
    _, r, c = p.shape

    def body(me_ref, p_ref, o_ref):
        del me_ref
        acc = p_ref[0]
        for k in range(1, N_DEV):
            acc = acc + p_ref[k]
        o_ref[...] = acc

    return pl.pallas_call(
        body, name=name,
        grid_spec=pltpu.PrefetchScalarGridSpec(
            num_scalar_prefetch=1, grid=(1,),
            in_specs=[pl.BlockSpec(p.shape, lambda i, me: (0, 0, 0))],
            out_specs=pl.BlockSpec((None, r, c), lambda i, me: (me[0], 0, 0))),
        out_shape=SDS(p.shape, F32), compiler_params=_cp("arbitrary"))(me1, p)


def _adamw(w, g, m, v):
    m = ADAM_B1 * m + (1.0 - ADAM_B1) * g
    v = ADAM_B2 * v + (1.0 - ADAM_B2) * (g * g)
    m_hat = m / (1.0 - ADAM_B1 ** ADAM_STEP)
    v_hat = v / (1.0 - ADAM_B2 ** ADAM_STEP)
    delta = -ADAM_LR * (m_hat / (jnp.sqrt(v_hat) + ADAM_EPS) + ADAM_WD * w)
    return delta, m, v


def _adam_shard(parts, w, m, v, layer, prev, name):
    _, r, c = parts.shape
    tr = next(t for t in (256, 352, r) if r % t == 0)
    n_prev = 0 if prev is None else 4

    def body(*refs):
        p_ref, w_ref, m_ref, v_ref = refs[:4]
        g_ref, d_ref, nm_ref, nv_ref = refs[4 + n_prev:]
        g = p_ref[0].astype(F32)
        for k in range(1, N_DEV):
            g = g + p_ref[k].astype(F32)
        delta, nm, nv = _adamw(w_ref[...], g, m_ref[...], v_ref[...])
        g_ref[...] = g
        d_ref[...] = delta
        nm_ref[...] = nm
        nv_ref[...] = nv

    blk = pl.BlockSpec((None, tr, c), lambda i: (layer, i, 0))
    return pl.pallas_call(
        body, name=name, grid=(r // tr,),
        in_specs=[pl.BlockSpec((N_DEV, tr, c), lambda i: (0, i, 0)), blk, blk, blk] + [_ANY] * n_prev,
        out_specs=[blk] * 4, out_shape=[SDS(w.shape, F32)] * 4,
        input_output_aliases={4 + k: k for k in range(n_prev)},
        compiler_params=_cp("parallel"))(parts, w, m, v, *(prev or ()))


SMALL_MATRICES = [("lru_w_r", 2048), ("lru_w_i", 2048), ("gmlp_w_s", 1024)]
SMALL_VECTORS = [("norm1_g", 8), ("gmlp_ln_g", 8), ("gmlp_ln_b", 8), ("gmlp_b_s", 8), ("conv_w", 32), ("conv_b", 8),
                 ("lru_b_r", 16), ("lru_b_i", 16), ("lru_lambda", 16), ("norm2_g", 8), ("final_g", 8)]
SMALL_VECTOR_ROW0 = sum(n for _, n in SMALL_MATRICES)
SMALL_VECTOR_BLOCK = 256
SMALL_ROWS = SMALL_VECTOR_ROW0 + SMALL_VECTOR_BLOCK


def _pack_small(small):
    parts = [small[k] for k, _ in SMALL_MATRICES]
    parts += [small[k] if k in small else jnp.zeros((n, HD), F32) for k, n in SMALL_VECTORS]
    flat = jnp.concatenate(parts)
    return jnp.pad(flat, ((0, SMALL_ROWS - flat.shape[0]), (0, 0))).reshape(N_DEV, SMALL_ROWS // N_DEV, HD)


def _adam_matrix(g0, g1, w, m, v, row0, name):
    _, rows, _ = w.shape

    def body(g0_ref, g1_ref, w_ref, m_ref, v_ref, g_ref, d_ref, nm_ref, nv_ref):
        for l, src in enumerate((g0_ref, g1_ref)):
            g = src[...]
            delta, nm, nv = _adamw(w_ref[l], g, m_ref[l], v_ref[l])
            g_ref[l] = g
            d_ref[l] = delta
            nm_ref[l] = nm
            nv_ref[l] = nv

    gspec = pl.BlockSpec((rows, HD), lambda i: (row0 // rows, 0))
    return pl.pallas_call(body, name=name, grid=(1,), in_specs=[gspec, gspec] + [_full(w.shape)] * 3,
                          out_specs=[_full(w.shape)] * 4, out_shape=[SDS(w.shape, F32)] * 4,
                          compiler_params=_cp("arbitrary"))(g0, g1, w, m, v)


def _adam_vectors(g0, g1, dg1_parts, me1, ws, ms, vs):
    names = [k for k, _ in SMALL_VECTORS]
    n = len(names)

    def lanes(rows8):
        return jnp.concatenate([rows8[k:k + 1, :] for k in range(LANE_ROWS)], axis=1)

    def body(me_ref, g0_ref, g1_ref, dg1_ref, *refs):
        w_refs, m_refs, v_refs = refs[:n], refs[n:2 * n], refs[2 * n:3 * n]
        outs = refs[3 * n:]
        me = me_ref[0]
        g_refs = (g0_ref, g1_ref)

        def emit(i, idx, g):
            delta, nm, nv = _adamw(w_refs[i][idx], g, m_refs[i][idx], v_refs[i][idx])
            for j, val in enumerate((g, delta, nm, nv)):
                outs[4 * i + j][idx] = val

        off = 0
        for i, (name, rows) in enumerate(SMALL_VECTORS):
            for l in range(2):
                row = (slice(l, l + 1), slice(None))
                if name == "final_g":
                    if l == 1:
                        emit(i, (slice(0, 1), slice(None)), lanes(g1_ref[off:off + rows, :]))
                elif name == "norm1_g":
                    if l == 1:
                        emit(i, row, lanes(g0_ref[off:off + rows, :]))
                    else:
                        total = dg1_ref[0]
                        for k in range(1, N_DEV):
                            total = total + dg1_ref[k]
                        emit(i, row, lanes(total))
                elif name == "gmlp_b_s":
                    emit(i, (l,), g_refs[l][off:off + rows, :])
                elif rows == LANE_ROWS:
                    emit(i, row, lanes(g_refs[l][off:off + rows, :]))
                else:
                    for r in range(rows // LANE_ROWS):
                        emit(i, (l, slice(r, r + 1), slice(None)), g_refs[l][pl.ds(off + r * LANE_ROWS + me, 1), :])
            off += rows

    args = [ws[k] for k in names] + [ms[k] for k in names] + [vs[k] for k in names]
    gspec = pl.BlockSpec((SMALL_VECTOR_BLOCK, HD), lambda i, me: (SMALL_VECTOR_ROW0 // SMALL_VECTOR_BLOCK, 0))
    res = pl.pallas_call(
        body, name="adam_vectors",
        grid_spec=pltpu.PrefetchScalarGridSpec(
            num_scalar_prefetch=1, grid=(1,),
            in_specs=[gspec, gspec, _full(dg1_parts.shape)] + [_full(a.shape) for a in args],
            out_specs=[_full(ws[k].shape) for k in names for _ in range(4)]),
        out_shape=[SDS(ws[k].shape, F32) for k in names for _ in range(4)],
        compiler_params=_cp("arbitrary"))(me1, g0, g1, dg1_parts, *args)
    return {k: list(res[4 * i:4 * i + 4]) for i, k in enumerate(names)}


def _after(a, *tokens):
    for token in tokens:
        if token is not None:
            a = a + token[0:1, 0:1]
    return a


def _local_step(x, tgt, p, get_w, hook=lambda stage, layer, payload: None):
    s = x.shape[0]
    tm = _row_tile(s)
    wsb = p["gmlp_w_s"].astype(BF16)
    wstb = jnp.swapaxes(p["gmlp_w_s"], -1, -2).astype(BF16)
    bsb = jnp.broadcast_to(p["gmlp_b_s"][..., None], p["gmlp_w_s"].shape)
    wrb = p["lru_w_r"].astype(BF16)
    wib = p["lru_w_i"].astype(BF16)
    saved = []
    for l in range(2):
        win = get_w("w_in", l, x)
        z, h1 = _norm_inproj(x, _after(p["norm1_g"][l][None], hook("pre_inproj", l, win)), win, l, tm)
        a0, b0, a1, b1, xcb, *gates = _lru_gates_fwd(z, p["conv_w"][l], p["conv_b"][l][None], wrb[l], wib[l],
                                                     p["lru_b_r"][l], p["lru_b_i"][l], p["lru_lambda"][l], l, tm)
        h0, hr = _lru_scan(a0, b0, a1, b1, False, l)
        lng = _after(p["gmlp_ln_g"][l][None], hook("pre_gmlp", l, h0))
        wout = get_w("w_out", l, lng)
        x1, mg = _mixer_fwd(x, h0, hr, z, lng, p["gmlp_ln_b"][l][None], wsb[l], bsb[l], wout, l, tm)
        wfi = get_w("w_ffn_in", l, x1)
        wfo = get_w("w_ffn_out", l, x1)
        x2, gu, h2 = _ffn_fwd(x1, p["norm2_g"][l][None], wfi, wfo, l, tm)
        saved.append((x, z, h1, a0, a1, h0, hr, x1, mg, gu, h2, win, wout, wfi, wfo, xcb, gates))
        x = x2
    dx, loss, dfg = _loss_head(x, p["final_g"][None], tgt, tm)
    pending = hook("loss", 1, loss)
    for l in (1, 0):
        x0, z, h1, a0, a1, h0, hr, x1, mg, gu, h2, win, wout, wfi, wfo, xcb, gates = saved[l]
        ff, dgu = _ffn_bwd_act(dx, wfo, gu, l, tm)
        d_wfo = _mm_tn(ff, pl.BlockSpec((None, s, FF_BLK), lambda j: (j, 0, 0)), dx, _resident((s, D)),
                       4, (4, FF_BLK, D), pl.BlockSpec((None, FF_BLK, D), lambda j: (j, 0, 0)),
                       f"dw_ffn_out_l{l}", a_is_transposed=False)
        dgu8 = dgu.reshape(N_DEV, s, FF_BLK)
        d_wfi = _mm_tn(dgu8, pl.BlockSpec((None, s, FF_BLK), lambda j: (j, 0, 0)), h2, _resident((s, D)),
                       N_DEV, (N_DEV, FF_BLK, D), pl.BlockSpec((None, FF_BLK, D), lambda j: (j, 0, 0)),
                       f"dw_ffn_in_l{l}", a_is_transposed=False)
        token = hook("ffn_partials", l, dict(w_ffn_out=d_wfo.reshape(N_DEV, D_FF // N_DEV, D), w_ffn_in=d_wfi))
        dx1, dg2 = _mm_nt_rms_bwd(
            dgu8, pl.BlockSpec((N_DEV, tm, FF_BLK), lambda i: (0, i, 0)), lambda r: [r[k] for k in range(N_DEV)],
            wfi.reshape(N_DEV, FF_BLK, D), True, x1, _after(p["norm2_g"][l][None], token, pending), dx,
            f"ffn_bwd_dx_l{l}", tm)
        pending = hook("mid_backward", l, dx1)
        dz, dh, dws, dbs, dlng, dlnb = _mixer_bwd(dx1, wout, h0, hr, z, p["gmlp_ln_g"][l][None],
                                                  p["gmlp_ln_b"][l][None], wsb[l], wstb[l], bsb[l], l, tm)
        d_wout = _mm_tn(mg, _resident((D, s)), dx1, pl.BlockSpec((s, D // 2), lambda j: (0, j)),
                        2, (D, D), pl.BlockSpec((D, D // 2), lambda j: (0, j)), f"dw_out_l{l}")
        g1, g0 = _lru_scan(a1, dh, a0, dh, True, l)
        dxc, dwr, dwi, dbr, dbi, dlam = _lru_gates_bwd(
            xcb, gates, h0, hr, g0, g1, wrb[l], wib[l], _after(p["lru_lambda"][l], pending), l, tm)
        dz, dcw, dcb = _conv_bwd(dz, dxc, z, p["conv_w"][l], l, tm)
        small = dict(lru_w_r=dwr.reshape(-1, HD), lru_w_i=dwi.reshape(-1, HD), gmlp_w_s=dws.reshape(-1, HD),
                     gmlp_ln_g=dlng, gmlp_ln_b=dlnb, gmlp_b_s=dbs, conv_w=dcw, conv_b=dcb, lru_b_r=dbr,
                     lru_b_i=dbi, lru_lambda=dlam, norm2_g=dg2)
        if l == 1:
            small["final_g"] = dfg
        else:
            small["norm1_g"] = dg1
        started = hook("small_grads", l, small)
        d_win = _mm_tn(h1, _resident((D, s)), dz, pl.BlockSpec((s, IN_BLK), lambda j: (0, j)),
                       N_DEV, (N_DEV, D, IN_BLK), pl.BlockSpec((None, D, IN_BLK), lambda j: (j, 0, 0)),
                       f"dw_in_l{l}", after=started)
        token = hook("mixer_partials", l, dict(w_out=d_wout.reshape(N_DEV, D // N_DEV, D), w_in=d_win))
        dx, dg1 = _mm_nt_rms_bwd(
            dz, pl.BlockSpec((tm, N_IN), lambda i: (i, 0)),
            lambda r: [r[:, k * IN_BLK:(k + 1) * IN_BLK] for k in range(N_DEV)],
            win, False, x0, _after(p["norm1_g"][l][None], token, started, pending), dx1, f"inproj_bwd_dx_l{l}", tm)
        pending = None
    return loss, dx, dg1


_REPL = ["norm1_g", "gmlp_ln_g", "gmlp_ln_b", "gmlp_w_s", "gmlp_b_s", "conv_b", "lru_w_r", "lru_w_i", "norm2_g", "final_g"]
_LANE_SHARDED = ["conv_w", "lru_b_r", "lru_b_i", "lru_lambda"]
_BIG = ["w_in", "w_out", "w_ffn_in", "w_ffn_out"]
_ORDER = ["norm1_g", "w_in", "gmlp_ln_g", "gmlp_ln_b", "gmlp_w_s", "gmlp_b_s", "conv_w", "conv_b", "lru_w_r", "lru_b_r",
          "lru_w_i", "lru_b_i", "lru_lambda", "w_out", "norm2_g", "w_ffn_in", "w_ffn_out", "final_g"]


def kernel(x, norm1_g, w_in, gmlp_ln_g, gmlp_ln_b, gmlp_w_s, gmlp_b_s, conv_w, conv_b, lru_w_r, lru_b_r, lru_w_i, lru_b_i, lru_lambda, w_out, norm2_g, w_ffn_in, w_ffn_out, final_g, loss_target, m_norm1_g, m_w_in, m_gmlp_ln_g, m_gmlp_ln_b, m_gmlp_w_s, m_gmlp_b_s, m_conv_w, m_conv_b, m_lru_w_r, m_lru_b_r, m_lru_w_i, m_lru_b_i, m_lru_lambda, m_w_out, m_norm2_g, m_w_ffn_in, m_w_ffn_out, m_final_g, v_norm1_g, v_w_in, v_gmlp_ln_g, v_gmlp_ln_b, v_gmlp_w_s, v_gmlp_b_s, v_conv_w, v_conv_b, v_lru_w_r, v_lru_b_r, v_lru_w_i, v_lru_b_i, v_lru_lambda, v_w_out, v_norm2_g, v_w_ffn_in, v_w_ffn_out, v_final_g):
    w = dict(norm1_g=norm1_g, w_in=w_in, gmlp_ln_g=gmlp_ln_g, gmlp_ln_b=gmlp_ln_b, gmlp_w_s=gmlp_w_s, gmlp_b_s=gmlp_b_s,
             conv_w=conv_w, conv_b=conv_b, lru_w_r=lru_w_r, lru_b_r=lru_b_r, lru_w_i=lru_w_i, lru_b_i=lru_b_i,
             lru_lambda=lru_lambda, w_out=w_out, norm2_g=norm2_g, w_ffn_in=w_ffn_in, w_ffn_out=w_ffn_out, final_g=final_g)
    mom = dict(norm1_g=m_norm1_g, w_in=m_w_in, gmlp_ln_g=m_gmlp_ln_g, gmlp_ln_b=m_gmlp_ln_b, gmlp_w_s=m_gmlp_w_s,
               gmlp_b_s=m_gmlp_b_s, conv_w=m_conv_w, conv_b=m_conv_b, lru_w_r=m_lru_w_r, lru_b_r=m_lru_b_r,
               lru_w_i=m_lru_w_i, lru_b_i=m_lru_b_i, lru_lambda=m_lru_lambda, w_out=m_w_out, norm2_g=m_norm2_g,
               w_ffn_in=m_w_ffn_in, w_ffn_out=m_w_ffn_out, final_g=m_final_g)
    var = dict(norm1_g=v_norm1_g, w_in=v_w_in, gmlp_ln_g=v_gmlp_ln_g, gmlp_ln_b=v_gmlp_ln_b, gmlp_w_s=v_gmlp_w_s,
               gmlp_b_s=v_gmlp_b_s, conv_w=v_conv_w, conv_b=v_conv_b, lru_w_r=v_lru_w_r, lru_b_r=v_lru_b_r,
               lru_w_i=v_lru_w_i, lru_b_i=v_lru_b_i, lru_lambda=v_lru_lambda, w_out=v_w_out, norm2_g=v_norm2_g,
               w_ffn_in=v_w_ffn_in, w_ffn_out=v_w_ffn_out, final_g=v_final_g)
    for src in (w, mom, var):
        src["w_ffn_in"] = jnp.swapaxes(src["w_ffn_in"], 1, 2)
    xi, yi, ci = _me()
    me = 4 * xi + 2 * yi + ci

    lane_shapes = [w[k].shape for k in _LANE_SHARDED]
    lane_rows = sum(a[0] * a[1] for a in lane_shapes)
    packed = jnp.concatenate([w[k].reshape(-1, HD) for k in _LANE_SHARDED])
    packed = jnp.pad(packed, ((0, -lane_rows % 8), (0, 0)))

    me1 = jnp.reshape(me, (1,)).astype(jnp.int32)
    gathers = {}
    exchanges = {}
    views = dict(w_in=(N_DEV, D, IN_BLK), w_out=(D, D), w_ffn_in=(2, 4, FF_BLK, D), w_ffn_out=(4, FF_BLK, D))
    small_ex = {}
    small_ag = {}

    casts = {}

    def start_gather(names, l, after):
        lands = [casts[(k, l)] if (k, l) in casts else _cast_into_slot(w[k], l, me1, f"cast_{k}_l{l}") for k in names]
        started, tok = _gather2_start(lands, after, f"gather_start_{'_'.join(names)}_l{l}")
        gathers.update({(k, l): h for k, h in zip(names, started)})
        return tok

    def relay_gather(names, l, after):
        relayed, tok = _gather2_relay([gathers[(k, l)] for k in names], after, f"gather_relay_{'_'.join(names)}_l{l}")
        gathers.update({(k, l): h for k, h in zip(names, relayed)})
        return tok

    def get_w(k, l, after):
        if (k, l) == ("w_in", 1):
            after = relay_gather(_BIG[:1], l, after)
        return _gather2_wait(gathers[(k, l)], after, f"gather_wait_{k}_l{l}").reshape(views[k])

    total_loss = []

    def hook(stage, l, payload):
        if stage == "loss":
            total_loss.append(lax.psum(payload[0, 0], MESH_AXES))
            return jnp.reshape(total_loss[0] * 0.0, (1, 1))
        if stage == "pre_inproj":
            return start_gather(_BIG[1:], l, payload)
        if stage == "pre_gmlp":
            tok = relay_gather(_BIG[1:], l, payload)
            return tok + start_gather(_BIG[:1], l + 1, tok) if l == 0 else tok
        if stage == "small_grads":
            (small_ex[l],), tok = _exchange_start([_pack_small(payload)], f"exchange_start_small_l{l}")
            return tok
        if stage == "mid_backward":
            return reduce_small(l + 1, payload) if l == 0 else None
        extra = reduce_small(0, payload["w_in"]) if (stage, l) == ("mixer_partials", 0) else None
        started, tok = _exchange_start(list(payload.values()), f"exchange_start_{'_'.join(payload)}_l{l}")
        exchanges.update({(k, l): h for k, h in zip(payload, started)})
        return tok if extra is None else tok + extra

    def reduce_small(l, after):
        got = _exchange_wait(small_ex[l], after, f"exchange_wait_small_l{l}")
        mine = _sum8_into_slot(got, me1, f"sum_small_l{l}")
        (small_ag[l],), tok = _gather_start([mine], got, f"gather_start_small_l{l}")
        return tok

    land = lax.dynamic_update_slice(jnp.zeros((N_DEV,) + packed.shape, F32), packed[None], (me, 0, 0))
    (lanes_handle,), token = _gather_start([land], packed, "gather_start_lanes")
    token = start_gather(_BIG[:1], 0, token)
    later = [(k, l) for l in range(2) for k in _BIG if (k, l) != ("w_in", 0)]
    casts.update(zip(later, _cast_all_into_slots([w[k] for k, _ in later], [l for _, l in later], me1, token,
                                                 "cast_later_weights")))
    token = relay_gather(_BIG[:1], 0, casts[later[0]])
    lanes = _gather_wait(lanes_handle, token, "gather_wait_lanes")
    params = {k: w[k] for k in _REPL}
    off = 0
    for k, shp in zip(_LANE_SHARDED, lane_shapes):
        n = shp[0] * shp[1]
        params[k] = jnp.swapaxes(lanes[:, off:off + n], 0, 1).reshape(shp[0], shp[1], D)
        off += n
    loss, dx, dg1 = _local_step(x[0], loss_target[0], params, get_w, hook)

    out = {}
    after = dx
    for k, l in [(k, l) for k in ("w_ffn_out", "w_ffn_in") for l in (1, 0)] + [("w_out", 1), ("w_in", 1)]:
        got = _exchange_wait(exchanges[(k, l)], after, f"exchange_wait_{k}_l{l}")
        out[k] = _adam_shard(got, w[k], mom[k], var[k], l, out.get(k), f"adam_{k}_l{l}")
        after = out[k][3]
    g_small = [_gather_wait(small_ag[l], after, f"gather_wait_small_l{l}").reshape(SMALL_ROWS, HD) for l in (0, 1)]
    row0 = 0
    for k, rows in SMALL_MATRICES:
        res = _adam_matrix(*g_small, *[src[k].reshape(2, rows, HD) for src in (w, mom, var)], row0, f"adam_{k}")
        out[k] = [a.reshape(w[k].shape) for a in res]
        after = res[3]
        row0 += rows
    for k in ("w_out", "w_in"):
        got = _exchange_wait(exchanges[(k, 0)], after, f"exchange_wait_{k}_l0")
        out[k] = _adam_shard(got, w[k], mom[k], var[k], 0, out[k], f"adam_{k}_l0")
    out["w_ffn_in"] = [jnp.swapaxes(a, 1, 2) for a in out["w_ffn_in"]]
    as_rows = lambda a: a.reshape(1, D) if a.ndim == 1 else a
    vec = _adam_vectors(*g_small, _all_gather(dg1, out["w_in"][3], "gather_norm1_grad"), me1,
                        *[{k: as_rows(src[k]) for k, _ in SMALL_VECTORS} for src in (w, mom, var)])
    out.update({k: [a.reshape(w[k].shape) for a in res] for k, res in vec.items()})

    return (total_loss[0], dx[None], *[out[k][0] for k in _ORDER], *[out[k][1] for k in _ORDER],
            *[out[k][2] for k in _ORDER], *[out[k][3] for k in _ORDER])
```

```python
import jax
import jax.numpy as jnp
from jax import lax
from jax.experimental import pallas as pl
from jax.experimental.pallas import tpu as pltpu

F32 = jnp.float32
BF16 = jnp.bfloat16
SDS = jax.ShapeDtypeStruct

D = 1024
N_IN = 6 * D
D_FF = 2816
N_DEV = 8
IN_BLK = N_IN // N_DEV
FF_BLK = 2 * D_FF // N_DEV
HEADS = 8
HD = 128
EPS = 1e-6
LRU_C = 8.0
MESH_AXES = ("x", "y", "c")

ADAM_LR = 0.001
ADAM_B1 = 0.9
ADAM_B2 = 0.999
ADAM_EPS = 1e-08
ADAM_WD = 0.01
ADAM_STEP = 10

VMEM_LIMIT = 56 * 2**20


def _cp(*sem, **kw):
    return pltpu.CompilerParams(dimension_semantics=sem, vmem_limit_bytes=VMEM_LIMIT, **kw)


def _row_tile(s):
    return 512 if s >= 1024 else s // 2


_GELU_C = 0.7978845608028654


def _gelu(x):
    t = jnp.tanh(_GELU_C * (x + 0.044715 * (x * x * x)))
    return 0.5 * x * (1.0 + t), t


def _gelu_grad(x, t):
    return 0.5 * (1.0 + t) + 0.5 * x * (1.0 - t * t) * (_GELU_C * (1.0 + 0.134145 * (x * x)))


def _sigmoid(x):
    return 0.5 + 0.5 * jnp.tanh(0.5 * x)


def _softplus(x):
    e = jnp.exp(-jnp.abs(x))
    w = 1.0 + e
    l1p = jnp.where(w == 1.0, e, jnp.log(w) * e / jnp.where(w == 1.0, 1.0, w - 1.0))
    return jnp.maximum(x, 0.0) + l1p


def _rms_fwd(x, g):
    r = lax.rsqrt(jnp.mean(x * x, axis=-1, keepdims=True) + EPS)
    return x * r * g


def _rms_bwd(x, g, dh):
    r = lax.rsqrt(jnp.mean(x * x, axis=-1, keepdims=True) + EPS)
    xh = x * r
    dxh = dh * g
    dx = r * (dxh - xh * jnp.mean(dxh * xh, axis=-1, keepdims=True))
    dg = jnp.sum(dh * xh, axis=0, keepdims=True)
    return dx, dg


LANE_ROWS = D // HD


def _add_rows128(ref, vec, row0=0):
    for i in range(vec.shape[0]):
        for k in range(LANE_ROWS):
            j = row0 + i * LANE_ROWS + k
            ref[j:j + 1, :] += vec[i:i + 1, k * HD:(k + 1) * HD]


def _dot(a, b):
    return jnp.dot(a, b, preferred_element_type=F32)


def _dot_nt(a, b):
    return lax.dot_general(a, b, (((1,), (1,)), ((), ())), preferred_element_type=F32)


def _dot_tn(a, b):
    return lax.dot_general(a, b, (((0,), (0,)), ((), ())), preferred_element_type=F32)


def _taps(prev, cur, nxt, tm):
    hr = prev.shape[0]
    ext = jnp.concatenate([prev, cur, nxt], axis=0)
    n = tm + 2 * hr
    sl = slice(hr, hr + tm)
    return (pltpu.roll(ext, 2, 0)[sl], pltpu.roll(ext, 1, 0)[sl], cur,
            pltpu.roll(ext, n - 1, 0)[sl], pltpu.roll(ext, n - 2, 0)[sl])


def _halo_specs(tm, s, col, rows=8):
    nb = s // rows
    r = tm // rows
    return (pl.BlockSpec((rows, D), lambda i: (jnp.maximum(i * r - 1, 0), col)),
            pl.BlockSpec((tm, D), lambda i: (i, col)),
            pl.BlockSpec((rows, D), lambda i: (jnp.minimum((i + 1) * r, nb - 1), col)))


def _halo_load(prev_ref, cur_ref, next_ref, fp, fn):
    return prev_ref[...].astype(F32) * fp, cur_ref[...].astype(F32), next_ref[...].astype(F32) * fn


def _halo_flags(nt):
    i = pl.program_id(0)
    return (i > 0).astype(F32), (i < nt - 1).astype(F32)


def _full(shape):
    nd = len(shape)
    return pl.BlockSpec(shape, lambda *_: (0,) * nd)


def _resident(shape):
    nd = len(shape)
    return pl.BlockSpec(shape, lambda *_: (0,) * nd, pipeline_mode=pl.Buffered(1))


def _norm_inproj(x, g, w, layer, tm):
    s = x.shape[0]

    def body(x_ref, g_ref, w_ref, z_ref, ht_ref):
        h32 = _rms_fwd(x_ref[...], g_ref[...])
        ht_ref[...] = h32.T.astype(BF16)
        h = h32.astype(BF16)
        for j in range(N_DEV):
            z_ref[:, j * IN_BLK:(j + 1) * IN_BLK] = _dot(h, w_ref[j]).astype(BF16)

    return pl.pallas_call(
        body, name=f"norm_inproj_l{layer}", grid=(s // tm,),
        in_specs=[pl.BlockSpec((tm, D), lambda i: (i, 0)), _full((1, D)), _resident((N_DEV, D, IN_BLK))],
        out_specs=[pl.BlockSpec((tm, N_IN), lambda i: (i, 0)), pl.BlockSpec((D, tm), lambda i: (0, i))],
        out_shape=[SDS((s, N_IN), BF16), SDS((D, s), BF16)],
        compiler_params=_cp("parallel"))(x, g, w)


def _gmlp_values(zu_ref, zv_ref, lng_ref, lnb_ref):
    zu = zu_ref[...].astype(F32)
    zv = zv_ref[...].astype(F32)
    u, tu = _gelu(zu)
    gv, tv = _gelu(zv)
    xc = gv - jnp.mean(gv, axis=-1, keepdims=True)
    rstd = lax.rsqrt(jnp.mean(xc * xc, axis=-1, keepdims=True) + EPS)
    xh = xc * rstd
    vb = (xh * lng_ref[...] + lnb_ref[...]).astype(BF16)
    return zu, zv, u, tu, tv, xh, rstd, vb


def _mixer_fwd(x, h0, h1, z, lng, lnb, ws, bsb, wo, layer, tm):
    s = x.shape[0]

    def body(x_ref, h0_ref, h1_ref, zu_ref, zv_ref, zg_ref, za_ref, zb_ref, lng_ref, lnb_ref, ws_ref, bsb_ref,
             wo_ref, x1_ref, mg_ref, ya_s):
        _, _, u, _, _, _, _, vb = _gmlp_values(zu_ref, zv_ref, lng_ref, lnb_ref)
        for c in range(tm // HD):
            rs = slice(c * HD, (c + 1) * HD)
            for g in range(HEADS):
                cs = slice(g * HD, (g + 1) * HD)
                ya_s[rs, cs] = u[rs, cs] * (_dot(ws_ref[g], vb[rs, cs]) + bsb_ref[g])
        gg, _ = _gelu(zg_ref[...].astype(F32))
        yb = (h0_ref[...] + h1_ref[...]) * gg
        m32 = _sigmoid(za_ref[...].astype(F32)) * ya_s[...] + _sigmoid(zb_ref[...].astype(F32)) * yb
        mg_ref[...] = m32.T.astype(BF16)
        x1_ref[...] = x_ref[...] + _dot(m32.astype(BF16), wo_ref[...])

    tile = pl.BlockSpec((tm, D), lambda i: (i, 0))
    wspec = _full((HEADS, HD, HD))
    return pl.pallas_call(
        body, name=f"mixer_fwd_l{layer}", grid=(s // tm,),
        in_specs=[tile, tile, tile] + [pl.BlockSpec((tm, D), lambda i, c=c: (i, c)) for c in (0, 1, 3, 4, 5)]
        + [_full((1, D)), _full((1, D)), wspec, wspec, _full((D, D))],
        out_specs=[tile, pl.BlockSpec((D, tm), lambda i: (0, i))], out_shape=[SDS((s, D), F32), SDS((D, s), BF16)],
        scratch_shapes=[pltpu.VMEM((tm, D), F32)],
        compiler_params=_cp("parallel"))(x, h0, h1, z, z, z, z, z, lng, lnb, ws, bsb, wo)


def _conv(taps, cw_ref, cb_ref):
    _, m1, c0, p1, p2 = taps
    return cb_ref[...] + m1 * cw_ref[0:1, :] + c0 * cw_ref[1:2, :] + p1 * cw_ref[2:3, :] + p2 * cw_ref[3:4, :]


def _heads_dot(xb, w_ref, d):
    return jnp.concatenate([_dot(xb[:, h * HD:(h + 1) * HD], w_ref[d, h]) for h in range(HEADS)], axis=1)


def _lru_decay(r, sp):
    la = (-LRU_C) * r * sp
    a = jnp.exp(la)
    return a, jnp.tanh(-la) * (a * a + 1.0)


def _lru_gates_fwd(z, cw, cb, wr, wi, br, bi, lam, layer, tm):
    s = z.shape[0]
    nt = s // tm

    def body(zp_ref, zc_ref, zn_ref, cw_ref, cb_ref, wr_ref, wi_ref, br_ref, bi_ref, lam_ref,
             a0_ref, b0_ref, a1_ref, b1_ref, xc_ref, r0_ref, i0_ref, r1_ref, i1_ref):
        fp, fn = _halo_flags(nt)
        xc = _conv(_taps(*_halo_load(zp_ref, zc_ref, zn_ref, fp, fn), tm), cw_ref, cb_ref)
        xb = xc.astype(BF16)
        xc_ref[...] = xb
        for d, (a_ref, b_ref, r_ref, i_ref) in enumerate(((a0_ref, b0_ref, r0_ref, i0_ref),
                                                          (a1_ref, b1_ref, r1_ref, i1_ref))):
            r = _sigmoid(_heads_dot(xb, wr_ref, d) + br_ref[d:d + 1, :])
            ig = _sigmoid(_heads_dot(xb, wi_ref, d) + bi_ref[d:d + 1, :])
            a, q = _lru_decay(r, _softplus(-lam_ref[d:d + 1, :]))
            a_ref[...] = a
            b_ref[...] = jnp.sqrt(q) * (ig * xc)
            r_ref[...] = r.astype(BF16)
            i_ref[...] = ig.astype(BF16)

    tile = pl.BlockSpec((tm, D), lambda i: (i, 0))
    return pl.pallas_call(
        body, name=f"lru_gates_fwd_l{layer}", grid=(nt,),
        in_specs=[*_halo_specs(tm, s, 2, 16), _full((4, D)), _full((1, D)),
                  _full((2, HEADS, HD, HD)), _full((2, HEADS, HD, HD)), _full((2, D)), _full((2, D)), _full((2, D))],
        out_specs=[tile] * 9, out_shape=[SDS((s, D), F32)] * 4 + [SDS((s, D), BF16)] * 5,
        compiler_params=_cp("parallel"))(z, z, z, cw, cb, wr, wi, br, bi, lam)


def _scan_group(a, x, c, reverse, bwd):
    row = lax.broadcasted_iota(jnp.int32, a.shape, 0)
    b = a * x if bwd else x
    for d in (1, 2, 4):
        keep = (row < 8 - d) if reverse else (row >= d)
        sh = 8 - d if reverse else d
        a_s = jnp.where(keep, pltpu.roll(a, sh, 0), 1.0)
        b_s = jnp.where(keep, pltpu.roll(b, sh, 0), 0.0)
        b = a * b_s + b
        a = a * a_s
    h = b + a * c
    new_c = h[0:1, :] if reverse else h[7:8, :]
    if not bwd:
        return h, new_c
    if reverse:
        prev = jnp.where(row < 7, pltpu.roll(h, 7, 0), c)
    else:
        prev = jnp.where(row >= 1, pltpu.roll(h, 1, 0), c)
    return x + prev, new_c


def _lru_scan(a_f, x_f, a_r, x_r, bwd, layer):
    s = a_f.shape[0]
    ts = min(1024, s // 2)
    cb = 512
    nt = s // ts
    ng = ts // 8

    def body(af_ref, xf_ref, ar_ref, xr_ref, of_ref, or_ref, cf, cr):
        @pl.when(pl.program_id(1) == 0)
        def _():
            cf[...] = jnp.zeros_like(cf)
            cr[...] = jnp.zeros_like(cr)

        def step(j, carry):
            c_f, c_r = carry
            rf = pl.multiple_of(j * 8, 8)
            rr = pl.multiple_of((ng - 1 - j) * 8, 8)
            o, c_f = _scan_group(af_ref[pl.ds(rf, 8), :], xf_ref[pl.ds(rf, 8), :], c_f, False, bwd)
            of_ref[pl.ds(rf, 8), :] = o
            o, c_r = _scan_group(ar_ref[pl.ds(rr, 8), :], xr_ref[pl.ds(rr, 8), :], c_r, True, bwd)
            or_ref[pl.ds(rr, 8), :] = o
            return c_f, c_r

        c_f, c_r = lax.fori_loop(0, ng, step, (cf[0:1, :], cr[0:1, :]), unroll=2)
        cf[...] = jnp.broadcast_to(c_f, cf.shape)
        cr[...] = jnp.broadcast_to(c_r, cr.shape)

    fwd = pl.BlockSpec((ts, cb), lambda c, t: (t, c))
    rev = pl.BlockSpec((ts, cb), lambda c, t: (nt - 1 - t, c))
    return pl.pallas_call(
        body, name=f"lru_scan_{'bwd' if bwd else 'fwd'}_l{layer}", grid=(D // cb, nt),
        in_specs=[fwd, fwd, rev, rev], out_specs=[fwd, rev],
        out_shape=[SDS((s, D), F32)] * 2,
        scratch_shapes=[pltpu.VMEM((8, cb), F32), pltpu.VMEM((8, cb), F32)],
        compiler_params=_cp("parallel", "arbitrary"))(a_f, x_f, a_r, x_r)


def _ffn_fwd(x1, g, wfi, wfo, layer, tm):
    s = x1.shape[0]

    def body(x_ref, g_ref, wi_ref, wo_ref, x2_ref, gu_ref, h_ref):
        x = x_ref[...]
        h = _rms_fwd(x, g_ref[...]).astype(BF16)
        h_ref[...] = h
        acc = x
        for k in range(4):
            gate = _dot_nt(h, wi_ref[0, k])
            up = _dot_nt(h, wi_ref[1, k])
            gu_ref[0, k] = gate.astype(BF16)
            gu_ref[1, k] = up.astype(BF16)
            acc = acc + _dot((gate * _sigmoid(gate) * up).astype(BF16), wo_ref[k])
        x2_ref[...] = acc

    tile = pl.BlockSpec((tm, D), lambda i: (i, 0))
    return pl.pallas_call(
        body, name=f"ffn_fwd_l{layer}", grid=(s // tm,),
        in_specs=[tile, _full((1, D)), _resident((2, 4, FF_BLK, D)), _resident((4, FF_BLK, D))],
        out_specs=[tile, pl.BlockSpec((2, 4, tm, FF_BLK), lambda i: (0, 0, i, 0)), tile],
        out_shape=[SDS((s, D), F32), SDS((2, 4, s, FF_BLK), BF16), SDS((s, D), BF16)],
        compiler_params=_cp("parallel"))(x1, g, wfi, wfo)


def _loss_head(x, g, tgt, tm):
    s = x.shape[0]

    def body(x_ref, g_ref, t_ref, dx_ref, loss_ref, dg_ref):
        @pl.when(pl.program_id(0) == 0)
        def _():
            loss_ref[...] = jnp.zeros_like(loss_ref)
            dg_ref[...] = jnp.zeros_like(dg_ref)

        x = x_ref[...]
        gv = g_ref[...]
        e = _rms_fwd(x, gv) - t_ref[...]
        rows = jnp.sum(e * e, axis=-1, keepdims=True)
        loss_ref[...] += (0.5 / D) * jnp.sum(rows, axis=0, keepdims=True)
        dx, dg = _rms_bwd(x, gv, e * (1.0 / D))
        dx_ref[...] = dx
        _add_rows128(dg_ref, dg)

    tile = pl.BlockSpec((tm, D), lambda i: (i, 0))
    return pl.pallas_call(
        body, name="loss_head", grid=(s // tm,),
        in_specs=[tile, _full((1, D)), tile],
        out_specs=[tile, _full((1, 1)), _full((LANE_ROWS, HD))],
        out_shape=[SDS((s, D), F32), SDS((1, 1), F32), SDS((LANE_ROWS, HD), F32)],
        compiler_params=_cp("arbitrary"))(x, g, tgt)


def _ffn_bwd_act(dx2, wfo, gu, layer, tm):
    s = dx2.shape[0]

    def body(dx_ref, wo_ref, gu_ref, ff_ref, dgu_ref):
        dxb = dx_ref[...].astype(BF16)
        for k in range(4):
            dff = _dot_nt(dxb, wo_ref[k])
            gate = gu_ref[0, k].astype(F32)
            up = gu_ref[1, k].astype(F32)
            sg = _sigmoid(gate)
            sl = gate * sg
            ff_ref[k] = (sl * up).astype(BF16)
            dgu_ref[0, k] = (dff * up * (sg * (1.0 + gate * (1.0 - sg)))).astype(BF16)
            dgu_ref[1, k] = (dff * sl).astype(BF16)

    blk = pl.BlockSpec((2, 4, tm, FF_BLK), lambda i: (0, 0, i, 0))
    return pl.pallas_call(
        body, name=f"ffn_bwd_act_l{layer}", grid=(s // tm,),
        in_specs=[pl.BlockSpec((tm, D), lambda i: (i, 0)), _resident((4, FF_BLK, D)), blk],
        out_specs=[pl.BlockSpec((4, tm, FF_BLK), lambda i: (0, i, 0)), blk],
        out_shape=[SDS((4, s, FF_BLK), BF16), SDS((2, 4, s, FF_BLK), BF16)],
        compiler_params=_cp("parallel"))(dx2, wfo, gu)


def _mm_nt_rms_bwd(a, a_spec, a_blocks, w, w_is_transposed, x, g, dres, name, tm):
    s = x.shape[0]

    def body(a_ref, w_ref, x_ref, g_ref, dres_ref, dx_ref, dg_ref):
        @pl.when(pl.program_id(0) == 0)
        def _():
            dg_ref[...] = jnp.zeros_like(dg_ref)

        dh = None
        for k, blk in enumerate(a_blocks(a_ref)):
            part = _dot(blk, w_ref[k]) if w_is_transposed else _dot_nt(blk, w_ref[k])
            dh = part if dh is None else dh + part
        dx, dg = _rms_bwd(x_ref[...], g_ref[...], dh)
        dx_ref[...] = dres_ref[...] + dx
        _add_rows128(dg_ref, dg)

    tile = pl.BlockSpec((tm, D), lambda i: (i, 0))
    return pl.pallas_call(
        body, name=name, grid=(s // tm,),
        in_specs=[a_spec, _resident(w.shape), tile, _full((1, D)), tile],
        out_specs=[tile, _full((LANE_ROWS, HD))], out_shape=[SDS((s, D), F32), SDS((LANE_ROWS, HD), F32)],
        compiler_params=_cp("arbitrary"))(a, w, x, g, dres)


def _mm_tn(a, a_spec, b, b_spec, nb, out_shape, out_spec, name, a_is_transposed=True, after=None):
    def body(a_ref, b_ref, *rest):
        o_ref = rest[-1]
        bb = b_ref[...].astype(BF16)
        o_ref[...] = (_dot(a_ref[...], bb) if a_is_transposed else _dot_tn(a_ref[...], bb)).astype(BF16)

    deps = [] if after is None else [after]
    return pl.pallas_call(
        body, name=name, grid=(nb,), in_specs=[a_spec, b_spec] + [_ANY] * len(deps), out_specs=out_spec,
        out_shape=SDS(out_shape, BF16), compiler_params=_cp("parallel"))(a, b, *deps)


def _mixer_bwd(dx1, wo, h0, h1, z, lng, lnb, ws, wst, bsb, layer, tm):
    s = dx1.shape[0]
    nt = s // tm

    def body(dx_ref, wo_ref, h0_ref, h1_ref, zu_ref, zv_ref, zg_ref, za_ref, zb_ref, lng_ref, lnb_ref,
             ws_ref, wst_ref, bsb_ref, dz_ref, dh_ref, dws_ref, dbs_ref, dlng_ref, dlnb_ref,
             du_s, dv_s, ya_s, dbs_acc):
        i = pl.program_id(0)

        @pl.when(i == 0)
        def _():
            for r in (dws_ref, dlng_ref, dlnb_ref, dbs_acc):
                r[...] = jnp.zeros_like(r)

        dm = _dot_nt(dx_ref[...].astype(BF16), wo_ref[...])
        sa = _sigmoid(za_ref[...].astype(F32))
        sb = _sigmoid(zb_ref[...].astype(F32))
        zg = zg_ref[...].astype(F32)
        gg, tg = _gelu(zg)
        hs = h0_ref[...] + h1_ref[...]
        dyb = dm * sb
        dya = dm * sa
        dh_ref[...] = dyb * gg
        dz_ref[:, 2 * D:3 * D] = jnp.zeros((tm, D), BF16)
        dz_ref[:, 3 * D:4 * D] = (dyb * hs * _gelu_grad(zg, tg)).astype(BF16)
        dz_ref[:, 5 * D:6 * D] = (dm * (hs * gg) * (sb * (1.0 - sb))).astype(BF16)

        zu, zv, u, tu, tv, xh, rstd, vb = _gmlp_values(zu_ref, zv_ref, lng_ref, lnb_ref)
        for c in range(tm // HD):
            rs = slice(c * HD, (c + 1) * HD)
            for g in range(HEADS):
                cs = slice(g * HD, (g + 1) * HD)
                vblk = vb[rs, cs]
                mixed = _dot(ws_ref[g], vblk) + bsb_ref[g]
                ya_s[rs, cs] = u[rs, cs] * mixed
                du_s[rs, cs] = dya[rs, cs] * mixed
                dmx = dya[rs, cs] * u[rs, cs]
                dbs_acc[g] += dmx
                dmxb = dmx.astype(BF16)
                dws_ref[g] += _dot_nt(dmxb, vblk)
                dv_s[rs, cs] = _dot(wst_ref[g], dmxb)
        dz_ref[:, 4 * D:5 * D] = (dm * ya_s[...] * (sa * (1.0 - sa))).astype(BF16)
        dv = dv_s[...]
        _add_rows128(dlng_ref, jnp.sum(dv * xh, axis=0, keepdims=True))
        _add_rows128(dlnb_ref, jnp.sum(dv, axis=0, keepdims=True))
        dxh = dv * lng_ref[...]
        dgv = rstd * (dxh - jnp.mean(dxh, axis=-1, keepdims=True)
                      - xh * jnp.mean(dxh * xh, axis=-1, keepdims=True))
        dz_ref[:, 0:D] = (du_s[...] * _gelu_grad(zu, tu)).astype(BF16)
        dz_ref[:, D:2 * D] = (dgv * _gelu_grad(zv, tv)).astype(BF16)

        @pl.when(i == nt - 1)
        def _():
            for g in range(HEADS):
                dbs_ref[g:g + 1, :] = jnp.sum(dbs_acc[g].T, axis=0, keepdims=True)

    tile = pl.BlockSpec((tm, D), lambda i: (i, 0))
    wspec = _full((HEADS, HD, HD))
    return pl.pallas_call(
        body, name=f"mixer_bwd_l{layer}", grid=(nt,),
        in_specs=[tile, _full((D, D)), tile, tile]
        + [pl.BlockSpec((tm, D), lambda i, c=c: (i, c)) for c in (0, 1, 3, 4, 5)]
        + [_full((1, D)), _full((1, D)), wspec, wspec, wspec],
        out_specs=[pl.BlockSpec((tm, N_IN), lambda i: (i, 0)), tile, wspec, _full((HEADS, HD)),
                   _full((LANE_ROWS, HD)), _full((LANE_ROWS, HD))],
        out_shape=[SDS((s, N_IN), BF16), SDS((s, D), F32), SDS((HEADS, HD, HD), F32), SDS((HEADS, HD), F32),
                   SDS((LANE_ROWS, HD), F32), SDS((LANE_ROWS, HD), F32)],
        scratch_shapes=[pltpu.VMEM((tm, D), F32)] * 3 + [pltpu.VMEM((HEADS, HD, HD), F32)],
        compiler_params=_cp("arbitrary"))(dx1, wo, h0, h1, z, z, z, z, z, lng, lnb, ws, wst, bsb)


def _lru_gates_bwd(xcb, gates, h0, h1, g0, g1, wr, wi, lam, layer, tm):
    s = xcb.shape[0]
    nt = s // tm

    def body(xc_ref, r0_ref, i0_ref, r1_ref, i1_ref, h0p_ref, h0_ref, h1_ref, h1n_ref, g0_ref, g1_ref,
             wr_ref, wi_ref, lam_ref, dxc_ref, dwr_ref, dwi_ref, dbr_ref, dbi_ref, dlam_ref):
        i = pl.program_id(0)
        fp, fn = _halo_flags(nt)

        @pl.when(i == 0)
        def _():
            for r in (dwr_ref, dwi_ref, dbr_ref, dbi_ref, dlam_ref):
                r[...] = jnp.zeros_like(r)

        xb = xc_ref[...]
        xc = xb.astype(F32)
        zeros8 = jnp.zeros((8, D), F32)
        h_prev = _taps(h0p_ref[...] * fp, h0_ref[...], zeros8, tm)[1]
        h_next = _taps(zeros8, h1_ref[...], h1n_ref[...] * fn, tm)[3]
        dxc = jnp.zeros((tm, D), F32)
        for d, (g_ref, hsh, r_ref, i_ref) in enumerate(((g0_ref, h_prev, r0_ref, i0_ref),
                                                        (g1_ref, h_next, r1_ref, i1_ref))):
            sp = _softplus(-lam_ref[d:d + 1, :])
            r = r_ref[...].astype(F32)
            ig = i_ref[...].astype(F32)
            a, q = _lru_decay(r, sp)
            rmult = jnp.where(q > 0.0, lax.rsqrt(jnp.where(q > 0.0, q, 1.0)), 0.0)
            mult = q * rmult
            db = g_ref[...]
            da = db * hsh
            dmult = db * (ig * xc)
            di = db * (mult * xc)
            dxc = dxc + db * (mult * ig)
            dla = da * a - dmult * (a * a * rmult)
            dsp_dlam = -_sigmoid(-lam_ref[d:d + 1, :])
            _add_rows128(dlam_ref, jnp.sum(dla * r, axis=0, keepdims=True) * ((-LRU_C) * dsp_dlam), d * LANE_ROWS)
            dpr = dla * sp * (-LRU_C) * (r * (1.0 - r))
            dpi = di * (ig * (1.0 - ig))
            _add_rows128(dbr_ref, jnp.sum(dpr, axis=0, keepdims=True), d * LANE_ROWS)
            _add_rows128(dbi_ref, jnp.sum(dpi, axis=0, keepdims=True), d * LANE_ROWS)
            dprb = dpr.astype(BF16)
            dpib = dpi.astype(BF16)
            parts = []
            for h in range(HEADS):
                cs = slice(h * HD, (h + 1) * HD)
                dwr_ref[d, h] += _dot_tn(xb[:, cs], dprb[:, cs])
                dwi_ref[d, h] += _dot_tn(xb[:, cs], dpib[:, cs])
                parts.append(_dot_nt(dprb[:, cs], wr_ref[d, h]) + _dot_nt(dpib[:, cs], wi_ref[d, h]))
            dxc = dxc + jnp.concatenate(parts, axis=1)
        dxc_ref[...] = dxc.astype(BF16)

    tile = pl.BlockSpec((tm, D), lambda i: (i, 0))
    hp, hc, hn = _halo_specs(tm, s, 0)
    wspec = _full((2, HEADS, HD, HD))
    vspec = _full((2 * LANE_ROWS, HD))
    return pl.pallas_call(
        body, name=f"lru_gates_bwd_l{layer}", grid=(nt,),
        in_specs=[tile] * 5 + [hp, hc, hc, hn, tile, tile, wspec, wspec, _full((2, D))],
        out_specs=[tile, wspec, wspec, vspec, vspec, vspec],
        out_shape=[SDS((s, D), BF16), SDS((2, HEADS, HD, HD), F32), SDS((2, HEADS, HD, HD), F32)]
        + [SDS((2 * LANE_ROWS, HD), F32)] * 3,
        compiler_params=_cp("arbitrary"))(xcb, *gates, h0, h0, h1, h1, g0, g1, wr, wi, lam)


def _conv_bwd(dz, dxc, z, cw, layer, tm):
    s = z.shape[0]
    nt = s // tm

    def body(dz_in, dp_ref, dc_ref, dn_ref, zp_ref, zc_ref, zn_ref, cw_ref, dz_ref, dcw_ref, dcb_ref):
        del dz_in
        fp, fn = _halo_flags(nt)

        @pl.when(pl.program_id(0) == 0)
        def _():
            dcw_ref[...] = jnp.zeros_like(dcw_ref)
            dcb_ref[...] = jnp.zeros_like(dcb_ref)

        dxc_halo = _halo_load(dp_ref, dc_ref, dn_ref, fp, fn)
        dxc = dxc_halo[1]
        dm2, dm1, _, dp1, _ = _taps(*dxc_halo, tm)
        dz_ref[...] = (cw_ref[0:1, :] * dp1 + cw_ref[1:2, :] * dxc + cw_ref[2:3, :] * dm1
                       + cw_ref[3:4, :] * dm2).astype(BF16)
        _, zm1, z0, zp1, zp2 = _taps(*_halo_load(zp_ref, zc_ref, zn_ref, fp, fn), tm)
        for k, zt in enumerate((zm1, z0, zp1, zp2)):
            _add_rows128(dcw_ref, jnp.sum(dxc * zt, axis=0, keepdims=True), k * LANE_ROWS)
        _add_rows128(dcb_ref, jnp.sum(dxc, axis=0, keepdims=True))

    return pl.pallas_call(
        body, name=f"conv_bwd_l{layer}", grid=(nt,),
        in_specs=[pl.BlockSpec(memory_space=pl.ANY), *_halo_specs(tm, s, 0, 16), *_halo_specs(tm, s, 2, 16),
                  _full((4, D))],
        out_specs=[pl.BlockSpec((tm, D), lambda i: (i, 2)), _full((4 * LANE_ROWS, HD)), _full((LANE_ROWS, HD))],
        out_shape=[SDS((s, N_IN), BF16), SDS((4 * LANE_ROWS, HD), F32), SDS((LANE_ROWS, HD), F32)],
        input_output_aliases={0: 0},
        compiler_params=_cp("arbitrary"))(dz, dxc, dxc, dxc, z, z, z, cw)


def _me():
    return lax.axis_index("x"), lax.axis_index("y"), lax.axis_index("c")


def _peer(m):
    x, y, c = _me()
    px = 1 - x if m & 4 else x
    py = 1 - y if m & 2 else y
    pc = 1 - c if m & 1 else c
    return (px, py, pc), 4 * px + 2 * py + pc


_ANY = pl.BlockSpec(memory_space=pl.ANY)
_EXCHANGE_SEMS = [pltpu.SemaphoreType.DMA((N_DEV - 1,)), pltpu.SemaphoreType.DMA((N_DEV - 1,)), pltpu.SemaphoreType.DMA(())]


def _all_gather(v, after, name):
    def body(v_ref, after_ref, o_ref, send_sems, recv_sems, local_sem):
        del after_ref
        x, y, c = _me()
        me = 4 * x + 2 * y + c
        local = pltpu.make_async_copy(v_ref, o_ref.at[me], local_sem)
        local.start()
        sends = []
        for m in range(1, N_DEV):
            dev, _ = _peer(m)
            cp = pltpu.make_async_remote_copy(v_ref, o_ref.at[me], send_sems.at[m - 1], recv_sems.at[m - 1],
                                              device_id=dev, device_id_type=pl.DeviceIdType.MESH)
            cp.start()
            sends.append(cp)
        for m in range(1, N_DEV):
            dev, blk = _peer(m)
            pltpu.make_async_remote_copy(v_ref, o_ref.at[blk], send_sems.at[m - 1], recv_sems.at[m - 1],
                                         device_id=dev, device_id_type=pl.DeviceIdType.MESH).wait_recv()
        for cp in sends:
            cp.wait_send()
        local.wait()

    return pl.pallas_call(
        body, name=name, in_specs=[_ANY, _ANY], out_specs=_ANY,
        out_shape=SDS((N_DEV,) + v.shape, v.dtype), scratch_shapes=_EXCHANGE_SEMS)(v, after)


_HBM = pl.BlockSpec(memory_space=pltpu.HBM)
_SEM = pl.BlockSpec(memory_space=pltpu.SEMAPHORE)
_EFFECT = pltpu.CompilerParams(has_side_effects=pltpu.SideEffectType.DATAFLOW_SIDE_EFFECTING)
_PEER_SEMS = pltpu.SemaphoreType.DMA((N_DEV - 1,))


def _in_hbm(a):
    return pltpu.with_memory_space_constraint(a, pltpu.HBM)


def _remote(src, dst, send_sems, recv_sems, m):
    dev, _ = _peer(m)
    return pltpu.make_async_remote_copy(src, dst, send_sems.at[m - 1], recv_sems.at[m - 1],
                                        device_id=dev, device_id_type=pl.DeviceIdType.MESH)


def _gather_start(lands, after, name):
    n = len(lands)

    def body(*refs):
        land = refs[:n]
        sems = refs[n + 1:3 * n + 1]
        token = refs[-1]
        x, y, c = _me()
        me = 4 * x + 2 * y + c
        for t in range(n):
            for m in range(1, N_DEV):
                _remote(land[t].at[me], land[t].at[me], sems[2 * t], sems[2 * t + 1], m).start()
        token[...] = jnp.zeros_like(token)

    res = pl.pallas_call(
        body, name=name, in_specs=[_HBM] * n + [_ANY],
        out_specs=[_SEM] * (2 * n) + [_HBM] * n + [pl.BlockSpec(memory_space=pltpu.VMEM)],
        out_shape=[_PEER_SEMS] * (2 * n) + [pltpu.HBM(a.shape, a.dtype) for a in lands] + [SDS((8, 128), F32)],
        input_output_aliases={t: 2 * n + t for t in range(n)},
        compiler_params=_EFFECT)(*[_in_hbm(a) for a in lands], after)
    return [(res[2 * t], res[2 * t + 1], res[2 * n + t]) for t in range(n)], res[-1]


def _gather_wait(handle, after, name):
    send_sems, recv_sems, land = handle

    def body(land_ref, ssem, rsem, after_ref, out_ref):
        del after_ref, out_ref
        x, y, c = _me()
        me = 4 * x + 2 * y + c
        for m in range(1, N_DEV):
            _, blk = _peer(m)
            cp = _remote(land_ref.at[me], land_ref.at[blk], ssem, rsem, m)
            cp.wait_send()
            cp.wait_recv()

    return pl.pallas_call(
        body, name=name, in_specs=[_HBM, _SEM, _SEM, _ANY], out_specs=_HBM,
        out_shape=pltpu.HBM(land.shape, land.dtype), input_output_aliases={0: 0},
        compiler_params=_EFFECT)(land, send_sems, recv_sems, after)


FIRST_STAGE = (1, 2, 4, 6)
RELAYED = (2, 4, 6)
OTHER_CORE = 1


def _stage_copy(src, dst, send_sems, recv_sems, k, m):
    dev, _ = _peer(m)
    return pltpu.make_async_remote_copy(src, dst, send_sems.at[k], recv_sems.at[k],
                                        device_id=dev, device_id_type=pl.DeviceIdType.MESH)


def _gather2_start(lands, after, name):
    n = len(lands)

    def body(*refs):
        land = refs[:n]
        sems = refs[n + 1:3 * n + 1]
        token = refs[-1]
        x, y, c = _me()
        me = 4 * x + 2 * y + c
        for t in range(n):
            for k, m in enumerate(FIRST_STAGE):
                _stage_copy(land[t].at[me], land[t].at[me], sems[2 * t], sems[2 * t + 1], k, m).start()
        token[...] = jnp.zeros_like(token)

    stage_sems = pltpu.SemaphoreType.DMA((len(FIRST_STAGE),))
    res = pl.pallas_call(
        body, name=name, in_specs=[_HBM] * n + [_ANY],
        out_specs=[_SEM] * (2 * n) + [_HBM] * n + [pl.BlockSpec(memory_space=pltpu.VMEM)],
        out_shape=[stage_sems] * (2 * n) + [pltpu.HBM(a.shape, a.dtype) for a in lands] + [SDS((8, 128), F32)],
        input_output_aliases={t: 2 * n + t for t in range(n)},
        compiler_params=_EFFECT)(*[_in_hbm(a) for a in lands], after)
    return [(res[2 * t], res[2 * t + 1], res[2 * n + t]) for t in range(n)], res[-1]


def _gather2_relay(handles, after, name):
    n = len(handles)

    def body(*refs):
        land, send1, recv1 = refs[:n], refs[n:2 * n], refs[2 * n:3 * n]
        sems = refs[3 * n + 1:5 * n + 1]
        token = refs[-1]
        x, y, c = _me()
        me = 4 * x + 2 * y + c
        for t in range(n):
            for j, m in enumerate(RELAYED):
                _, blk = _peer(m)
                _stage_copy(land[t].at[me], land[t].at[blk], send1[t], recv1[t], 1 + j, m).wait_recv()
                _stage_copy(land[t].at[blk], land[t].at[blk], sems[2 * t], sems[2 * t + 1], j, OTHER_CORE).start()
        token[...] = jnp.zeros_like(token)

    relay_sems = pltpu.SemaphoreType.DMA((len(RELAYED),))
    lands = [h[2] for h in handles]
    res = pl.pallas_call(
        body, name=name, in_specs=[_HBM] * n + [_SEM] * (2 * n) + [_ANY],
        out_specs=[_SEM] * (2 * n) + [_HBM] * n + [pl.BlockSpec(memory_space=pltpu.VMEM)],
        out_shape=[relay_sems] * (2 * n) + [pltpu.HBM(a.shape, a.dtype) for a in lands] + [SDS((8, 128), F32)],
        input_output_aliases={t: 2 * n + t for t in range(n)},
        compiler_params=_EFFECT)(*lands, *[h[0] for h in handles], *[h[1] for h in handles], after)
    return [(h[0], h[1], res[2 * t], res[2 * t + 1], res[2 * n + t]) for t, h in enumerate(handles)], res[-1]


def _gather2_wait(handle, after, name):
    send1, recv1, send2, recv2, land = handle

    def body(land_ref, s1, r1, s2, r2, after_ref, out_ref):
        del after_ref, out_ref
        x, y, c = _me()
        me = 4 * x + 2 * y + c
        _, other = _peer(OTHER_CORE)
        _stage_copy(land_ref.at[me], land_ref.at[other], s1, r1, 0, OTHER_CORE).wait_recv()
        for k, m in enumerate(FIRST_STAGE):
            _stage_copy(land_ref.at[me], land_ref.at[me], s1, r1, k, m).wait_send()
        for j, m in enumerate(RELAYED):
            _, mine = _peer(m)
            _, theirs = _peer(m ^ OTHER_CORE)
            _stage_copy(land_ref.at[mine], land_ref.at[mine], s2, r2, j, OTHER_CORE).wait_send()
            _stage_copy(land_ref.at[mine], land_ref.at[theirs], s2, r2, j, OTHER_CORE).wait_recv()

    return pl.pallas_call(
        body, name=name, in_specs=[_HBM] + [_SEM] * 4 + [_ANY], out_specs=_HBM,
        out_shape=pltpu.HBM(land.shape, land.dtype), input_output_aliases={0: 0},
        compiler_params=_EFFECT)(land, send1, recv1, send2, recv2, after)


def _exchange_start(ps, name):
    n = len(ps)

    def body(*refs):
        p = refs[:n]
        got = refs[n:2 * n]
        sems = refs[2 * n:5 * n]
        token = refs[-1]
        x, y, c = _me()
        me = 4 * x + 2 * y + c
        for t in range(n):
            pltpu.make_async_copy(p[t].at[me], got[t].at[me], sems[3 * t + 2]).start()
            for m in range(1, N_DEV):
                _, blk = _peer(m)
                _remote(p[t].at[blk], got[t].at[me], sems[3 * t], sems[3 * t + 1], m).start()
        token[...] = jnp.zeros_like(token)

    res = pl.pallas_call(
        body, name=name, in_specs=[_HBM] * (2 * n),
        out_specs=[_SEM] * (3 * n) + [_HBM] * (2 * n) + [pl.BlockSpec(memory_space=pltpu.VMEM)],
        out_shape=[_PEER_SEMS, _PEER_SEMS, pltpu.SemaphoreType.DMA(())] * n
        + [pltpu.HBM(a.shape, a.dtype) for a in ps] * 2 + [SDS((8, 128), F32)],
        input_output_aliases={t: 3 * n + t for t in range(2 * n)},
        compiler_params=_EFFECT)(*[_in_hbm(a) for a in ps], *[_in_hbm(lax.empty(a.shape, a.dtype)) for a in ps])
    return [(res[3 * t], res[3 * t + 1], res[3 * t + 2], res[3 * n + t], res[4 * n + t]) for t in range(n)], res[-1]


def _exchange_wait(handle, after, name):
    send_sems, recv_sems, local_sem, p, got = handle

    def body(p_ref, got_ref, ssem, rsem, lsem, after_ref, p_out, got_out):
        del after_ref, p_out, got_out
        x, y, c = _me()
        me = 4 * x + 2 * y + c
        pltpu.make_async_copy(p_ref.at[me], got_ref.at[me], lsem).wait()
        for m in range(1, N_DEV):
            _, blk = _peer(m)
            cp = _remote(p_ref.at[blk], got_ref.at[blk], ssem, rsem, m)
            cp.wait_send()
            cp.wait_recv()

    return pl.pallas_call(
        body, name=name, in_specs=[_HBM, _HBM, _SEM, _SEM, _SEM, _ANY], out_specs=[_HBM, _HBM],
        out_shape=[pltpu.HBM(p.shape, p.dtype), pltpu.HBM(got.shape, got.dtype)],
        input_output_aliases={0: 0, 1: 1}, compiler_params=_EFFECT)(p, got, send_sems, recv_sems, local_sem, after)[1]


def _cast_into_slot(w, layer, me1, name):
    _, r, c = w.shape
    tr = next(t for t in (256, 352, r) if r % t == 0)

    def body(me_ref, w_ref, o_ref):
        del me_ref
        o_ref[...] = w_ref[...].astype(BF16)

    return pl.pallas_call(
        body, name=name,
        grid_spec=pltpu.PrefetchScalarGridSpec(
            num_scalar_prefetch=1, grid=(r // tr,),
            in_specs=[pl.BlockSpec((None, tr, c), lambda i, me: (layer, i, 0))],
            out_specs=pl.BlockSpec((None, tr, c), lambda i, me: (me[0], i, 0))),
        out_shape=SDS((N_DEV, r, c), BF16), compiler_params=_cp("arbitrary"))(me1, w)


def _cast_all_into_slots(ws, layers, me1, after, name):
    n = len(ws)

    def body(me_ref, *refs):
        del me_ref
        for w_ref, o_ref in zip(refs[:n], refs[n + 1:]):
            o_ref[...] = w_ref[...].astype(BF16)

    return pl.pallas_call(
        body, name=name,
        grid_spec=pltpu.PrefetchScalarGridSpec(
            num_scalar_prefetch=1, grid=(1,),
            in_specs=[pl.BlockSpec((None,) + a.shape[1:], lambda i, me, l=l: (l, 0, 0)) for a, l in zip(ws, layers)]
            + [_ANY],
            out_specs=[pl.BlockSpec((None,) + a.shape[1:], lambda i, me: (me[0], 0, 0)) for a in ws]),
        out_shape=[SDS((N_DEV,) + a.shape[1:], BF16) for a in ws],
        compiler_params=_cp("arbitrary"))(me1, *ws, after)


def _sum8_into_slot(p, me1, name):
    _, r, c = p.shape

    def body(me_ref, p_ref, o_ref):
        del me_ref
        acc = p_ref[0]
        for k in range(1, N_DEV):
            acc = acc + p_ref[k]
        o_ref[...] = acc

    return pl.pallas_call(
        body, name=name,
        grid_spec=pltpu.PrefetchScalarGridSpec(
            num_scalar_prefetch=1, grid=(1,),
            in_specs=[pl.BlockSpec(p.shape, lambda i, me: (0, 0, 0))],
            out_specs=pl.BlockSpec((None, r, c), lambda i, me: (me[0], 0, 0))),
        out_shape=SDS(p.shape, F32), compiler_params=_cp("arbitrary"))(me1, p)


def _adamw(w, g, m, v):
    m = ADAM_B1 * m + (1.0 - ADAM_B1) * g
    v = ADAM_B2 * v + (1.0 - ADAM_B2) * (g * g)
    m_hat = m / (1.0 - ADAM_B1 ** ADAM_STEP)
    v_hat = v / (1.0 - ADAM_B2 ** ADAM_STEP)
    delta = -ADAM_LR * (m_hat / (jnp.sqrt(v_hat) + ADAM_EPS) + ADAM_WD * w)
    return delta, m, v


def _adam_shard(parts, w, m, v, layer, prev, name):
    _, r, c = parts.shape
    tr = next(t for t in (256, 352, r) if r % t == 0)
    n_prev = 0 if prev is None else 4

    def body(*refs):
        p_ref, w_ref, m_ref, v_ref = refs[:4]
        g_ref, d_ref, nm_ref, nv_ref = refs[4 + n_prev:]
        g = p_ref[0].astype(F32)
        for k in range(1, N_DEV):
            g = g + p_ref[k].astype(F32)
        delta, nm, nv = _adamw(w_ref[...], g, m_ref[...], v_ref[...])
        g_ref[...] = g
        d_ref[...] = delta
        nm_ref[...] = nm
        nv_ref[...] = nv

    blk = pl.BlockSpec((None, tr, c), lambda i: (layer, i, 0))
    return pl.pallas_call(
        body, name=name, grid=(r // tr,),
        in_specs=[pl.BlockSpec((N_DEV, tr, c), lambda i: (0, i, 0)), blk, blk, blk] + [_ANY] * n_prev,
        out_specs=[blk] * 4, out_shape=[SDS(w.shape, F32)] * 4,
        input_output_aliases={4 + k: k for k in range(n_prev)},
        compiler_params=_cp("parallel"))(parts, w, m, v, *(prev or ()))


SMALL_MATRICES = [("lru_w_r", 2048), ("lru_w_i", 2048), ("gmlp_w_s", 1024)]
SMALL_VECTORS = [("norm1_g", 8), ("gmlp_ln_g", 8), ("gmlp_ln_b", 8), ("gmlp_b_s", 8), ("conv_w", 32), ("conv_b", 8),
                 ("lru_b_r", 16), ("lru_b_i", 16), ("lru_lambda", 16), ("norm2_g", 8), ("final_g", 8)]
SMALL_VECTOR_ROW0 = sum(n for _, n in SMALL_MATRICES)
SMALL_VECTOR_BLOCK = 256
SMALL_ROWS = SMALL_VECTOR_ROW0 + SMALL_VECTOR_BLOCK


def _pack_small(small):
    parts = [small[k] for k, _ in SMALL_MATRICES]
    parts += [small[k] if k in small else jnp.zeros((n, HD), F32) for k, n in SMALL_VECTORS]
    flat = jnp.concatenate(parts)
    return jnp.pad(flat, ((0, SMALL_ROWS - flat.shape[0]), (0, 0))).reshape(N_DEV, SMALL_ROWS // N_DEV, HD)


def _adam_matrix(g0, g1, w, m, v, row0, name):
    _, rows, _ = w.shape

    def body(g0_ref, g1_ref, w_ref, m_ref, v_ref, g_ref, d_ref, nm_ref, nv_ref):
        for l, src in enumerate((g0_ref, g1_ref)):
            g = src[...]
            delta, nm, nv = _adamw(w_ref[l], g, m_ref[l], v_ref[l])
            g_ref[l] = g
            d_ref[l] = delta
            nm_ref[l] = nm
            nv_ref[l] = nv

    gspec = pl.BlockSpec((rows, HD), lambda i: (row0 // rows, 0))
    return pl.pallas_call(body, name=name, grid=(1,), in_specs=[gspec, gspec] + [_full(w.shape)] * 3,
                          out_specs=[_full(w.shape)] * 4, out_shape=[SDS(w.shape, F32)] * 4,
                          compiler_params=_cp("arbitrary"))(g0, g1, w, m, v)


def _adam_vectors(g0, g1, dg1_parts, me1, ws, ms, vs):
    names = [k for k, _ in SMALL_VECTORS]
    n = len(names)

    def lanes(rows8):
        return jnp.concatenate([rows8[k:k + 1, :] for k in range(LANE_ROWS)], axis=1)

    def body(me_ref, g0_ref, g1_ref, dg1_ref, *refs):
        w_refs, m_refs, v_refs = refs[:n], refs[n:2 * n], refs[2 * n:3 * n]
        outs = refs[3 * n:]
        me = me_ref[0]
        g_refs = (g0_ref, g1_ref)

        def emit(i, idx, g):
            delta, nm, nv = _adamw(w_refs[i][idx], g, m_refs[i][idx], v_refs[i][idx])
            for j, val in enumerate((g, delta, nm, nv)):
                outs[4 * i + j][idx] = val

        off = 0
        for i, (name, rows) in enumerate(SMALL_VECTORS):
            for l in range(2):
                row = (slice(l, l + 1), slice(None))
                if name == "final_g":
                    if l == 1:
                        emit(i, (slice(0, 1), slice(None)), lanes(g1_ref[off:off + rows, :]))
                elif name == "norm1_g":
                    if l == 1:
                        emit(i, row, lanes(g0_ref[off:off + rows, :]))
                    else:
                        total = dg1_ref[0]
                        for k in range(1, N_DEV):
                            total = total + dg1_ref[k]
                        emit(i, row, lanes(total))
                elif name == "gmlp_b_s":
                    emit(i, (l,), g_refs[l][off:off + rows, :])
                elif rows == LANE_ROWS:
                    emit(i, row, lanes(g_refs[l][off:off + rows, :]))
                else:
                    for r in range(rows // LANE_ROWS):
                        emit(i, (l, slice(r, r + 1), slice(None)), g_refs[l][pl.ds(off + r * LANE_ROWS + me, 1), :])
            off += rows

    args = [ws[k] for k in names] + [ms[k] for k in names] + [vs[k] for k in names]
    gspec = pl.BlockSpec((SMALL_VECTOR_BLOCK, HD), lambda i, me: (SMALL_VECTOR_ROW0 // SMALL_VECTOR_BLOCK, 0))
    res = pl.pallas_call(
        body, name="adam_vectors",
        grid_spec=pltpu.PrefetchScalarGridSpec(
            num_scalar_prefetch=1, grid=(1,),
            in_specs=[gspec, gspec, _full(dg1_parts.shape)] + [_full(a.shape) for a in args],
            out_specs=[_full(ws[k].shape) for k in names for _ in range(4)]),
        out_shape=[SDS(ws[k].shape, F32) for k in names for _ in range(4)],
        compiler_params=_cp("arbitrary"))(me1, g0, g1, dg1_parts, *args)
    return {k: list(res[4 * i:4 * i + 4]) for i, k in enumerate(names)}


def _after(a, *tokens):
    for token in tokens:
        if token is not None:
            a = a + token[0:1, 0:1]
    return a


def _local_step(x, tgt, p, get_w, hook=lambda stage, layer, payload: None):
    s = x.shape[0]
    tm = _row_tile(s)
    wsb = p["gmlp_w_s"].astype(BF16)
    wstb = jnp.swapaxes(p["gmlp_w_s"], -1, -2).astype(BF16)
    bsb = jnp.broadcast_to(p["gmlp_b_s"][..., None], p["gmlp_w_s"].shape)
    wrb = p["lru_w_r"].astype(BF16)
    wib = p["lru_w_i"].astype(BF16)
    saved = []
    for l in range(2):
        win = get_w("w_in", l, x)
        z, h1 = _norm_inproj(x, _after(p["norm1_g"][l][None], hook("pre_inproj", l, win)), win, l, tm)
        a0, b0, a1, b1, xcb, *gates = _lru_gates_fwd(z, p["conv_w"][l], p["conv_b"][l][None], wrb[l], wib[l],
                                                     p["lru_b_r"][l], p["lru_b_i"][l], p["lru_lambda"][l], l, tm)
        h0, hr = _lru_scan(a0, b0, a1, b1, False, l)
        lng = _after(p["gmlp_ln_g"][l][None], hook("pre_gmlp", l, h0))
        wout = get_w("w_out", l, lng)
        x1, mg = _mixer_fwd(x, h0, hr, z, lng, p["gmlp_ln_b"][l][None], wsb[l], bsb[l], wout, l, tm)
        wfi = get_w("w_ffn_in", l, x1)
        wfo = get_w("w_ffn_out", l, x1)
        x2, gu, h2 = _ffn_fwd(x1, p["norm2_g"][l][None], wfi, wfo, l, tm)
        saved.append((x, z, h1, a0, a1, h0, hr, x1, mg, gu, h2, win, wout, wfi, wfo, xcb, gates))
        x = x2
    dx, loss, dfg = _loss_head(x, p["final_g"][None], tgt, tm)
    pending = None
    for l in (1, 0):
        x0, z, h1, a0, a1, h0, hr, x1, mg, gu, h2, win, wout, wfi, wfo, xcb, gates = saved[l]
        ff, dgu = _ffn_bwd_act(dx, wfo, gu, l, tm)
        d_wfo = _mm_tn(ff, pl.BlockSpec((None, s, FF_BLK), lambda j: (j, 0, 0)), dx, _resident((s, D)),
                       4, (4, FF_BLK, D), pl.BlockSpec((None, FF_BLK, D), lambda j: (j, 0, 0)),
                       f"dw_ffn_out_l{l}", a_is_transposed=False)
        dgu8 = dgu.reshape(N_DEV, s, FF_BLK)
        d_wfi = _mm_tn(dgu8, pl.BlockSpec((None, s, FF_BLK), lambda j: (j, 0, 0)), h2, _resident((s, D)),
                       N_DEV, (N_DEV, FF_BLK, D), pl.BlockSpec((None, FF_BLK, D), lambda j: (j, 0, 0)),
                       f"dw_ffn_in_l{l}", a_is_transposed=False)
        token = hook("ffn_partials", l, dict(w_ffn_out=d_wfo.reshape(N_DEV, D_FF // N_DEV, D), w_ffn_in=d_wfi))
        dx1, dg2 = _mm_nt_rms_bwd(
            dgu8, pl.BlockSpec((N_DEV, tm, FF_BLK), lambda i: (0, i, 0)), lambda r: [r[k] for k in range(N_DEV)],
            wfi.reshape(N_DEV, FF_BLK, D), True, x1, _after(p["norm2_g"][l][None], token, pending), dx,
            f"ffn_bwd_dx_l{l}", tm)
        pending = hook("mid_backward", l, dx1)
        dz, dh, dws, dbs, dlng, dlnb = _mixer_bwd(dx1, wout, h0, hr, z, p["gmlp_ln_g"][l][None],
                                                  p["gmlp_ln_b"][l][None], wsb[l], wstb[l], bsb[l], l, tm)
        d_wout = _mm_tn(mg, _resident((D, s)), dx1, pl.BlockSpec((s, D // 2), lambda j: (0, j)),
                        2, (D, D), pl.BlockSpec((D, D // 2), lambda j: (0, j)), f"dw_out_l{l}")
        g1, g0 = _lru_scan(a1, dh, a0, dh, True, l)
        dxc, dwr, dwi, dbr, dbi, dlam = _lru_gates_bwd(
            xcb, gates, h0, hr, g0, g1, wrb[l], wib[l], _after(p["lru_lambda"][l], pending), l, tm)
        dz, dcw, dcb = _conv_bwd(dz, dxc, z, p["conv_w"][l], l, tm)
        small = dict(lru_w_r=dwr.reshape(-1, HD), lru_w_i=dwi.reshape(-1, HD), gmlp_w_s=dws.reshape(-1, HD),
                     gmlp_ln_g=dlng, gmlp_ln_b=dlnb, gmlp_b_s=dbs, conv_w=dcw, conv_b=dcb, lru_b_r=dbr,
                     lru_b_i=dbi, lru_lambda=dlam, norm2_g=dg2)
        if l == 1:
            small["final_g"] = dfg
        else:
            small["norm1_g"] = dg1
        started = hook("small_grads", l, small)
        d_win = _mm_tn(h1, _resident((D, s)), dz, pl.BlockSpec((s, IN_BLK), lambda j: (0, j)),
                       N_DEV, (N_DEV, D, IN_BLK), pl.BlockSpec((None, D, IN_BLK), lambda j: (j, 0, 0)),
                       f"dw_in_l{l}", after=started)
        token = hook("mixer_partials", l, dict(w_out=d_wout.reshape(N_DEV, D // N_DEV, D), w_in=d_win))
        dx, dg1 = _mm_nt_rms_bwd(
            dz, pl.BlockSpec((tm, N_IN), lambda i: (i, 0)),
            lambda r: [r[:, k * IN_BLK:(k + 1) * IN_BLK] for k in range(N_DEV)],
            win, False, x0, _after(p["norm1_g"][l][None], token, started, pending), dx1, f"inproj_bwd_dx_l{l}", tm)
        pending = None
    return loss, dx, dg1


_REPL = ["norm1_g", "gmlp_ln_g", "gmlp_ln_b", "gmlp_w_s", "gmlp_b_s", "conv_b", "lru_w_r", "lru_w_i", "norm2_g", "final_g"]
_LANE_SHARDED = ["conv_w", "lru_b_r", "lru_b_i", "lru_lambda"]
_BIG = ["w_in", "w_out", "w_ffn_in", "w_ffn_out"]
_ORDER = ["norm1_g", "w_in", "gmlp_ln_g", "gmlp_ln_b", "gmlp_w_s", "gmlp_b_s", "conv_w", "conv_b", "lru_w_r", "lru_b_r",
          "lru_w_i", "lru_b_i", "lru_lambda", "w_out", "norm2_g", "w_ffn_in", "w_ffn_out", "final_g"]


def kernel(x, norm1_g, w_in, gmlp_ln_g, gmlp_ln_b, gmlp_w_s, gmlp_b_s, conv_w, conv_b, lru_w_r, lru_b_r, lru_w_i, lru_b_i, lru_lambda, w_out, norm2_g, w_ffn_in, w_ffn_out, final_g, loss_target, m_norm1_g, m_w_in, m_gmlp_ln_g, m_gmlp_ln_b, m_gmlp_w_s, m_gmlp_b_s, m_conv_w, m_conv_b, m_lru_w_r, m_lru_b_r, m_lru_w_i, m_lru_b_i, m_lru_lambda, m_w_out, m_norm2_g, m_w_ffn_in, m_w_ffn_out, m_final_g, v_norm1_g, v_w_in, v_gmlp_ln_g, v_gmlp_ln_b, v_gmlp_w_s, v_gmlp_b_s, v_conv_w, v_conv_b, v_lru_w_r, v_lru_b_r, v_lru_w_i, v_lru_b_i, v_lru_lambda, v_w_out, v_norm2_g, v_w_ffn_in, v_w_ffn_out, v_final_g):
    w = dict(norm1_g=norm1_g, w_in=w_in, gmlp_ln_g=gmlp_ln_g, gmlp_ln_b=gmlp_ln_b, gmlp_w_s=gmlp_w_s, gmlp_b_s=gmlp_b_s,
             conv_w=conv_w, conv_b=conv_b, lru_w_r=lru_w_r, lru_b_r=lru_b_r, lru_w_i=lru_w_i, lru_b_i=lru_b_i,
             lru_lambda=lru_lambda, w_out=w_out, norm2_g=norm2_g, w_ffn_in=w_ffn_in, w_ffn_out=w_ffn_out, final_g=final_g)
    mom = dict(norm1_g=m_norm1_g, w_in=m_w_in, gmlp_ln_g=m_gmlp_ln_g, gmlp_ln_b=m_gmlp_ln_b, gmlp_w_s=m_gmlp_w_s,
               gmlp_b_s=m_gmlp_b_s, conv_w=m_conv_w, conv_b=m_conv_b, lru_w_r=m_lru_w_r, lru_b_r=m_lru_b_r,
               lru_w_i=m_lru_w_i, lru_b_i=m_lru_b_i, lru_lambda=m_lru_lambda, w_out=m_w_out, norm2_g=m_norm2_g,
               w_ffn_in=m_w_ffn_in, w_ffn_out=m_w_ffn_out, final_g=m_final_g)
    var = dict(norm1_g=v_norm1_g, w_in=v_w_in, gmlp_ln_g=v_gmlp_ln_g, gmlp_ln_b=v_gmlp_ln_b, gmlp_w_s=v_gmlp_w_s,
               gmlp_b_s=v_gmlp_b_s, conv_w=v_conv_w, conv_b=v_conv_b, lru_w_r=v_lru_w_r, lru_b_r=v_lru_b_r,
               lru_w_i=v_lru_w_i, lru_b_i=v_lru_b_i, lru_lambda=v_lru_lambda, w_out=v_w_out, norm2_g=v_norm2_g,
               w_ffn_in=v_w_ffn_in, w_ffn_out=v_w_ffn_out, final_g=v_final_g)
    for src in (w, mom, var):
        src["w_ffn_in"] = jnp.swapaxes(src["w_ffn_in"], 1, 2)
    xi, yi, ci = _me()
    me = 4 * xi + 2 * yi + ci

    lane_shapes = [w[k].shape for k in _LANE_SHARDED]
    lane_rows = sum(a[0] * a[1] for a in lane_shapes)
    packed = jnp.concatenate([w[k].reshape(-1, HD) for k in _LANE_SHARDED])
    packed = jnp.pad(packed, ((0, -lane_rows % 8), (0, 0)))

    me1 = jnp.reshape(me, (1,)).astype(jnp.int32)
    gathers = {}
    exchanges = {}
    views = dict(w_in=(N_DEV, D, IN_BLK), w_out=(D, D), w_ffn_in=(2, 4, FF_BLK, D), w_ffn_out=(4, FF_BLK, D))
    small_ex = {}
    small_ag = {}

    casts = {}

    def start_gather(names, l, after):
        lands = [casts[(k, l)] if (k, l) in casts else _cast_into_slot(w[k], l, me1, f"cast_{k}_l{l}") for k in names]
        started, tok = _gather2_start(lands, after, f"gather_start_{'_'.join(names)}_l{l}")
        gathers.update({(k, l): h for k, h in zip(names, started)})
        return tok

    def relay_gather(names, l, after):
        relayed, tok = _gather2_relay([gathers[(k, l)] for k in names], after, f"gather_relay_{'_'.join(names)}_l{l}")
        gathers.update({(k, l): h for k, h in zip(names, relayed)})
        return tok

    def get_w(k, l, after):
        if (k, l) == ("w_in", 1):
            after = relay_gather(_BIG[:1], l, after)
        return _gather2_wait(gathers[(k, l)], after, f"gather_wait_{k}_l{l}").reshape(views[k])

    def hook(stage, l, payload):
        if stage == "pre_inproj":
            return start_gather(_BIG[1:], l, payload)
        if stage == "pre_gmlp":
            tok = relay_gather(_BIG[1:], l, payload)
            return tok + start_gather(_BIG[:1], l + 1, tok) if l == 0 else tok
        if stage == "small_grads":
            (small_ex[l],), tok = _exchange_start([_pack_small(payload)], f"exchange_start_small_l{l}")
            return tok
        if stage == "mid_backward":
            return reduce_small(l + 1, payload) if l == 0 else None
        extra = reduce_small(0, payload["w_in"]) if (stage, l) == ("mixer_partials", 0) else None
        started, tok = _exchange_start(list(payload.values()), f"exchange_start_{'_'.join(payload)}_l{l}")
        exchanges.update({(k, l): h for k, h in zip(payload, started)})
        return tok if extra is None else tok + extra

    def reduce_small(l, after):
        got = _exchange_wait(small_ex[l], after, f"exchange_wait_small_l{l}")
        mine = _sum8_into_slot(got, me1, f"sum_small_l{l}")
        (small_ag[l],), tok = _gather_start([mine], got, f"gather_start_small_l{l}")
        return tok

    land = lax.dynamic_update_slice(jnp.zeros((N_DEV,) + packed.shape, F32), packed[None], (me, 0, 0))
    (lanes_handle,), token = _gather_start([land], packed, "gather_start_lanes")
    token = start_gather(_BIG[:1], 0, token)
    later = [(k, l) for l in range(2) for k in _BIG if (k, l) != ("w_in", 0)]
    casts.update(zip(later, _cast_all_into_slots([w[k] for k, _ in later], [l for _, l in later], me1, token,
                                                 "cast_later_weights")))
    token = relay_gather(_BIG[:1], 0, casts[later[0]])
    lanes = _gather_wait(lanes_handle, token, "gather_wait_lanes")
    params = {k: w[k] for k in _REPL}
    off = 0
    for k, shp in zip(_LANE_SHARDED, lane_shapes):
        n = shp[0] * shp[1]
        params[k] = jnp.swapaxes(lanes[:, off:off + n], 0, 1).reshape(shp[0], shp[1], D)
        off += n
    loss, dx, dg1 = _local_step(x[0], loss_target[0], params, get_w, hook)

    out = {}
    after = dx
    for k, l in [(k, l) for k in ("w_ffn_out", "w_ffn_in") for l in (1, 0)] + [("w_out", 1), ("w_in", 1)]:
        got = _exchange_wait(exchanges[(k, l)], after, f"exchange_wait_{k}_l{l}")
        out[k] = _adam_shard(got, w[k], mom[k], var[k], l, out.get(k), f"adam_{k}_l{l}")
        after = out[k][3]
    g_small = [_gather_wait(small_ag[l], after, f"gather_wait_small_l{l}").reshape(SMALL_ROWS, HD) for l in (0, 1)]
    row0 = 0
    for k, rows in SMALL_MATRICES:
        res = _adam_matrix(*g_small, *[src[k].reshape(2, rows, HD) for src in (w, mom, var)], row0, f"adam_{k}")
        out[k] = [a.reshape(w[k].shape) for a in res]
        after = res[3]
        row0 += rows
    for k in ("w_out", "w_in"):
        got = _exchange_wait(exchanges[(k, 0)], after, f"exchange_wait_{k}_l0")
        out[k] = _adam_shard(got, w[k], mom[k], var[k], 0, out[k], f"adam_{k}_l0")
    out["w_ffn_in"] = [jnp.swapaxes(a, 1, 2) for a in out["w_ffn_in"]]
    as_rows = lambda a: a.reshape(1, D) if a.ndim == 1 else a
    vec = _adam_vectors(*g_small, _all_gather(dg1, out["w_in"][3], "gather_norm1_grad"), me1,
                        *[{k: as_rows(src[k]) for k, _ in SMALL_VECTORS} for src in (w, mom, var)])
    out.update({k: [a.reshape(w[k].shape) for a in res] for k, res in vec.items()})

    loss = lax.psum(loss[0, 0], MESH_AXES)
    return (loss, dx[None], *[out[k][0] for k in _ORDER], *[out[k][1] for k in _ORDER],
            *[out[k][2] for k in _ORDER], *[out[k][3] for k in _ORDER])
```

```python
import jax
import jax.numpy as jnp
from jax import lax
from jax.experimental import pallas as pl
from jax.experimental.pallas import tpu as pltpu

F32 = jnp.float32
BF16 = jnp.bfloat16
SDS = jax.ShapeDtypeStruct

D = 1024
N_IN = 6 * D
D_FF = 2816
N_DEV = 8
IN_BLK = N_IN // N_DEV
FF_BLK = 2 * D_FF // N_DEV
HEADS = 8
HD = 128
EPS = 1e-6
LRU_C = 8.0
MESH_AXES = ("x", "y", "c")

ADAM_LR = 0.001
ADAM_B1 = 0.9
ADAM_B2 = 0.999
ADAM_EPS = 1e-08
ADAM_WD = 0.01
ADAM_STEP = 10

VMEM_LIMIT = 56 * 2**20


def _cp(*sem, **kw):
    return pltpu.CompilerParams(dimension_semantics=sem, vmem_limit_bytes=VMEM_LIMIT, **kw)


def _row_tile(s):
    return 512 if s >= 1024 else s // 2


_GELU_C = 0.7978845608028654


def _gelu(x):
    t = jnp.tanh(_GELU_C * (x + 0.044715 * (x * x * x)))
    return 0.5 * x * (1.0 + t), t


def _gelu_grad(x, t):
    return 0.5 * (1.0 + t) + 0.5 * x * (1.0 - t * t) * (_GELU_C * (1.0 + 0.134145 * (x * x)))


def _sigmoid(x):
    return 0.5 + 0.5 * jnp.tanh(0.5 * x)


def _softplus(x):
    e = jnp.exp(-jnp.abs(x))
    w = 1.0 + e
    l1p = jnp.where(w == 1.0, e, jnp.log(w) * e / jnp.where(w == 1.0, 1.0, w - 1.0))
    return jnp.maximum(x, 0.0) + l1p


def _rms_fwd(x, g):
    r = lax.rsqrt(jnp.mean(x * x, axis=-1, keepdims=True) + EPS)
    return x * r * g


def _rms_bwd(x, g, dh):
    r = lax.rsqrt(jnp.mean(x * x, axis=-1, keepdims=True) + EPS)
    xh = x * r
    dxh = dh * g
    dx = r * (dxh - xh * jnp.mean(dxh * xh, axis=-1, keepdims=True))
    dg = jnp.sum(dh * xh, axis=0, keepdims=True)
    return dx, dg


LANE_ROWS = D // HD


def _add_rows128(ref, vec, row0=0):
    for i in range(vec.shape[0]):
        for k in range(LANE_ROWS):
            j = row0 + i * LANE_ROWS + k
            ref[j:j + 1, :] += vec[i:i + 1, k * HD:(k + 1) * HD]


def _dot(a, b):
    return jnp.dot(a, b, preferred_element_type=F32)


def _dot_nt(a, b):
    return lax.dot_general(a, b, (((1,), (1,)), ((), ())), preferred_element_type=F32)


def _dot_tn(a, b):
    return lax.dot_general(a, b, (((0,), (0,)), ((), ())), preferred_element_type=F32)


def _taps(prev, cur, nxt, tm):
    hr = prev.shape[0]
    ext = jnp.concatenate([prev, cur, nxt], axis=0)
    n = tm + 2 * hr
    sl = slice(hr, hr + tm)
    return (pltpu.roll(ext, 2, 0)[sl], pltpu.roll(ext, 1, 0)[sl], cur,
            pltpu.roll(ext, n - 1, 0)[sl], pltpu.roll(ext, n - 2, 0)[sl])


def _halo_specs(tm, s, col, rows=8):
    nb = s // rows
    r = tm // rows
    return (pl.BlockSpec((rows, D), lambda i: (jnp.maximum(i * r - 1, 0), col)),
            pl.BlockSpec((tm, D), lambda i: (i, col)),
            pl.BlockSpec((rows, D), lambda i: (jnp.minimum((i + 1) * r, nb - 1), col)))


def _halo_load(prev_ref, cur_ref, next_ref, fp, fn):
    return prev_ref[...].astype(F32) * fp, cur_ref[...].astype(F32), next_ref[...].astype(F32) * fn


def _halo_flags(nt):
    i = pl.program_id(0)
    return (i > 0).astype(F32), (i < nt - 1).astype(F32)


def _full(shape):
    nd = len(shape)
    return pl.BlockSpec(shape, lambda *_: (0,) * nd)


def _resident(shape):
    nd = len(shape)
    return pl.BlockSpec(shape, lambda *_: (0,) * nd, pipeline_mode=pl.Buffered(1))


def _norm_inproj(x, g, w, layer, tm):
    s = x.shape[0]

    def body(x_ref, g_ref, w_ref, z_ref, ht_ref):
        h32 = _rms_fwd(x_ref[...], g_ref[...])
        ht_ref[...] = h32.T.astype(BF16)
        h = h32.astype(BF16)
        for j in range(N_DEV):
            z_ref[:, j * IN_BLK:(j + 1) * IN_BLK] = _dot(h, w_ref[j]).astype(BF16)

    return pl.pallas_call(
        body, name=f"norm_inproj_l{layer}", grid=(s // tm,),
        in_specs=[pl.BlockSpec((tm, D), lambda i: (i, 0)), _full((1, D)), _resident((N_DEV, D, IN_BLK))],
        out_specs=[pl.BlockSpec((tm, N_IN), lambda i: (i, 0)), pl.BlockSpec((D, tm), lambda i: (0, i))],
        out_shape=[SDS((s, N_IN), BF16), SDS((D, s), BF16)],
        compiler_params=_cp("parallel"))(x, g, w)


def _gmlp_values(zu_ref, zv_ref, lng_ref, lnb_ref):
    zu = zu_ref[...].astype(F32)
    zv = zv_ref[...].astype(F32)
    u, tu = _gelu(zu)
    gv, tv = _gelu(zv)
    xc = gv - jnp.mean(gv, axis=-1, keepdims=True)
    rstd = lax.rsqrt(jnp.mean(xc * xc, axis=-1, keepdims=True) + EPS)
    xh = xc * rstd
    vb = (xh * lng_ref[...] + lnb_ref[...]).astype(BF16)
    return zu, zv, u, tu, tv, xh, rstd, vb


def _mixer_fwd(x, h0, h1, z, lng, lnb, ws, bsb, wo, layer, tm):
    s = x.shape[0]

    def body(x_ref, h0_ref, h1_ref, zu_ref, zv_ref, zg_ref, za_ref, zb_ref, lng_ref, lnb_ref, ws_ref, bsb_ref,
             wo_ref, x1_ref, mg_ref, ya_s):
        _, _, u, _, _, _, _, vb = _gmlp_values(zu_ref, zv_ref, lng_ref, lnb_ref)
        for c in range(tm // HD):
            rs = slice(c * HD, (c + 1) * HD)
            for g in range(HEADS):
                cs = slice(g * HD, (g + 1) * HD)
                ya_s[rs, cs] = u[rs, cs] * (_dot(ws_ref[g], vb[rs, cs]) + bsb_ref[g])
        gg, _ = _gelu(zg_ref[...].astype(F32))
        yb = (h0_ref[...] + h1_ref[...]) * gg
        m32 = _sigmoid(za_ref[...].astype(F32)) * ya_s[...] + _sigmoid(zb_ref[...].astype(F32)) * yb
        mg_ref[...] = m32.T.astype(BF16)
        x1_ref[...] = x_ref[...] + _dot(m32.astype(BF16), wo_ref[...])

    tile = pl.BlockSpec((tm, D), lambda i: (i, 0))
    wspec = _full((HEADS, HD, HD))
    return pl.pallas_call(
        body, name=f"mixer_fwd_l{layer}", grid=(s // tm,),
        in_specs=[tile, tile, tile] + [pl.BlockSpec((tm, D), lambda i, c=c: (i, c)) for c in (0, 1, 3, 4, 5)]
        + [_full((1, D)), _full((1, D)), wspec, wspec, _full((D, D))],
        out_specs=[tile, pl.BlockSpec((D, tm), lambda i: (0, i))], out_shape=[SDS((s, D), F32), SDS((D, s), BF16)],
        scratch_shapes=[pltpu.VMEM((tm, D), F32)],
        compiler_params=_cp("parallel"))(x, h0, h1, z, z, z, z, z, lng, lnb, ws, bsb, wo)


def _conv(taps, cw_ref, cb_ref):
    _, m1, c0, p1, p2 = taps
    return cb_ref[...] + m1 * cw_ref[0:1, :] + c0 * cw_ref[1:2, :] + p1 * cw_ref[2:3, :] + p2 * cw_ref[3:4, :]


def _heads_dot(xb, w_ref, d):
    return jnp.concatenate([_dot(xb[:, h * HD:(h + 1) * HD], w_ref[d, h]) for h in range(HEADS)], axis=1)


def _lru_decay(r, sp):
    la = (-LRU_C) * r * sp
    a = jnp.exp(la)
    return a, jnp.tanh(-la) * (a * a + 1.0)


def _lru_gates_fwd(z, cw, cb, wr, wi, br, bi, lam, layer, tm):
    s = z.shape[0]
    nt = s // tm

    def body(zp_ref, zc_ref, zn_ref, cw_ref, cb_ref, wr_ref, wi_ref, br_ref, bi_ref, lam_ref,
             a0_ref, b0_ref, a1_ref, b1_ref, xc_ref, r0_ref, i0_ref, r1_ref, i1_ref):
        fp, fn = _halo_flags(nt)
        xc = _conv(_taps(*_halo_load(zp_ref, zc_ref, zn_ref, fp, fn), tm), cw_ref, cb_ref)
        xb = xc.astype(BF16)
        xc_ref[...] = xb
        for d, (a_ref, b_ref, r_ref, i_ref) in enumerate(((a0_ref, b0_ref, r0_ref, i0_ref),
                                                          (a1_ref, b1_ref, r1_ref, i1_ref))):
            r = _sigmoid(_heads_dot(xb, wr_ref, d) + br_ref[d:d + 1, :])
            ig = _sigmoid(_heads_dot(xb, wi_ref, d) + bi_ref[d:d + 1, :])
            a, q = _lru_decay(r, _softplus(-lam_ref[d:d + 1, :]))
            a_ref[...] = a
            b_ref[...] = jnp.sqrt(q) * (ig * xc)
            r_ref[...] = r.astype(BF16)
            i_ref[...] = ig.astype(BF16)

    tile = pl.BlockSpec((tm, D), lambda i: (i, 0))
    return pl.pallas_call(
        body, name=f"lru_gates_fwd_l{layer}", grid=(nt,),
        in_specs=[*_halo_specs(tm, s, 2, 16), _full((4, D)), _full((1, D)),
                  _full((2, HEADS, HD, HD)), _full((2, HEADS, HD, HD)), _full((2, D)), _full((2, D)), _full((2, D))],
        out_specs=[tile] * 9, out_shape=[SDS((s, D), F32)] * 4 + [SDS((s, D), BF16)] * 5,
        compiler_params=_cp("parallel"))(z, z, z, cw, cb, wr, wi, br, bi, lam)


def _scan_group(a, x, c, reverse, bwd):
    row = lax.broadcasted_iota(jnp.int32, a.shape, 0)
    b = a * x if bwd else x
    for d in (1, 2, 4):
        keep = (row < 8 - d) if reverse else (row >= d)
        sh = 8 - d if reverse else d
        a_s = jnp.where(keep, pltpu.roll(a, sh, 0), 1.0)
        b_s = jnp.where(keep, pltpu.roll(b, sh, 0), 0.0)
        b = a * b_s + b
        a = a * a_s
    h = b + a * c
    new_c = h[0:1, :] if reverse else h[7:8, :]
    if not bwd:
        return h, new_c
    if reverse:
        prev = jnp.where(row < 7, pltpu.roll(h, 7, 0), c)
    else:
        prev = jnp.where(row >= 1, pltpu.roll(h, 1, 0), c)
    return x + prev, new_c


def _lru_scan(a_f, x_f, a_r, x_r, bwd, layer):
    s = a_f.shape[0]
    ts = min(1024, s // 2)
    cb = 512
    nt = s // ts
    ng = ts // 8

    def body(af_ref, xf_ref, ar_ref, xr_ref, of_ref, or_ref, cf, cr):
        @pl.when(pl.program_id(1) == 0)
        def _():
            cf[...] = jnp.zeros_like(cf)
            cr[...] = jnp.zeros_like(cr)

        def step(j, carry):
            c_f, c_r = carry
            rf = pl.multiple_of(j * 8, 8)
            rr = pl.multiple_of((ng - 1 - j) * 8, 8)
            o, c_f = _scan_group(af_ref[pl.ds(rf, 8), :], xf_ref[pl.ds(rf, 8), :], c_f, False, bwd)
            of_ref[pl.ds(rf, 8), :] = o
            o, c_r = _scan_group(ar_ref[pl.ds(rr, 8), :], xr_ref[pl.ds(rr, 8), :], c_r, True, bwd)
            or_ref[pl.ds(rr, 8), :] = o
            return c_f, c_r

        c_f, c_r = lax.fori_loop(0, ng, step, (cf[0:1, :], cr[0:1, :]), unroll=2)
        cf[...] = jnp.broadcast_to(c_f, cf.shape)
        cr[...] = jnp.broadcast_to(c_r, cr.shape)

    fwd = pl.BlockSpec((ts, cb), lambda c, t: (t, c))
    rev = pl.BlockSpec((ts, cb), lambda c, t: (nt - 1 - t, c))
    return pl.pallas_call(
        body, name=f"lru_scan_{'bwd' if bwd else 'fwd'}_l{layer}", grid=(D // cb, nt),
        in_specs=[fwd, fwd, rev, rev], out_specs=[fwd, rev],
        out_shape=[SDS((s, D), F32)] * 2,
        scratch_shapes=[pltpu.VMEM((8, cb), F32), pltpu.VMEM((8, cb), F32)],
        compiler_params=_cp("parallel", "arbitrary"))(a_f, x_f, a_r, x_r)


def _ffn_fwd(x1, g, wfi, wfo, layer, tm):
    s = x1.shape[0]

    def body(x_ref, g_ref, wi_ref, wo_ref, x2_ref, ff_ref, dff_ref, h_ref):
        x = x_ref[...]
        h = _rms_fwd(x, g_ref[...]).astype(BF16)
        h_ref[...] = h
        acc = x
        for k in range(4):
            gate = _dot_nt(h, wi_ref[0, k])
            up = _dot_nt(h, wi_ref[1, k])
            sg = _sigmoid(gate)
            silu = gate * sg
            ff = (silu * up).astype(BF16)
            ff_ref[k] = ff
            dff_ref[0, k] = (up * (sg * (1.0 + gate * (1.0 - sg)))).astype(BF16)
            dff_ref[1, k] = silu.astype(BF16)
            acc = acc + _dot(ff, wo_ref[k])
        x2_ref[...] = acc

    tile = pl.BlockSpec((tm, D), lambda i: (i, 0))
    return pl.pallas_call(
        body, name=f"ffn_fwd_l{layer}", grid=(s // tm,),
        in_specs=[tile, _full((1, D)), _resident((2, 4, FF_BLK, D)), _resident((4, FF_BLK, D))],
        out_specs=[tile, pl.BlockSpec((4, tm, FF_BLK), lambda i: (0, i, 0)),
                   pl.BlockSpec((2, 4, tm, FF_BLK), lambda i: (0, 0, i, 0)), tile],
        out_shape=[SDS((s, D), F32), SDS((4, s, FF_BLK), BF16), SDS((2, 4, s, FF_BLK), BF16), SDS((s, D), BF16)],
        compiler_params=_cp("parallel"))(x1, g, wfi, wfo)


def _loss_head(x, g, tgt, tm):
    s = x.shape[0]

    def body(x_ref, g_ref, t_ref, dx_ref, loss_ref, dg_ref):
        @pl.when(pl.program_id(0) == 0)
        def _():
            loss_ref[...] = jnp.zeros_like(loss_ref)
            dg_ref[...] = jnp.zeros_like(dg_ref)

        x = x_ref[...]
        gv = g_ref[...]
        e = _rms_fwd(x, gv) - t_ref[...]
        rows = jnp.sum(e * e, axis=-1, keepdims=True)
        loss_ref[...] += (0.5 / D) * jnp.sum(rows, axis=0, keepdims=True)
        dx, dg = _rms_bwd(x, gv, e * (1.0 / D))
        dx_ref[...] = dx
        _add_rows128(dg_ref, dg)

    tile = pl.BlockSpec((tm, D), lambda i: (i, 0))
    return pl.pallas_call(
        body, name="loss_head", grid=(s // tm,),
        in_specs=[tile, _full((1, D)), tile],
        out_specs=[tile, _full((1, 1)), _full((LANE_ROWS, HD))],
        out_shape=[SDS((s, D), F32), SDS((1, 1), F32), SDS((LANE_ROWS, HD), F32)],
        compiler_params=_cp("arbitrary"))(x, g, tgt)


def _ffn_bwd_act(dx2, wfo, factors, layer, tm):
    s = dx2.shape[0]

    def body(dx_ref, wo_ref, f_ref, dgu_ref):
        dxb = dx_ref[...].astype(BF16)
        for k in range(4):
            dff = _dot_nt(dxb, wo_ref[k])
            dgu_ref[0, k] = (dff * f_ref[0, k].astype(F32)).astype(BF16)
            dgu_ref[1, k] = (dff * f_ref[1, k].astype(F32)).astype(BF16)

    blk = pl.BlockSpec((2, 4, tm, FF_BLK), lambda i: (0, 0, i, 0))
    return pl.pallas_call(
        body, name=f"ffn_bwd_act_l{layer}", grid=(s // tm,),
        in_specs=[pl.BlockSpec((tm, D), lambda i: (i, 0)), _resident((4, FF_BLK, D)), blk],
        out_specs=blk, out_shape=SDS((2, 4, s, FF_BLK), BF16),
        compiler_params=_cp("parallel"))(dx2, wfo, factors)


def _mm_nt_rms_bwd(a, a_spec, a_blocks, w, w_is_transposed, x, g, dres, name, tm):
    s = x.shape[0]

    def body(a_ref, w_ref, x_ref, g_ref, dres_ref, dx_ref, dg_ref):
        @pl.when(pl.program_id(0) == 0)
        def _():
            dg_ref[...] = jnp.zeros_like(dg_ref)

        dh = None
        for k, blk in enumerate(a_blocks(a_ref)):
            part = _dot(blk, w_ref[k]) if w_is_transposed else _dot_nt(blk, w_ref[k])
            dh = part if dh is None else dh + part
        dx, dg = _rms_bwd(x_ref[...], g_ref[...], dh)
        dx_ref[...] = dres_ref[...] + dx
        _add_rows128(dg_ref, dg)

    tile = pl.BlockSpec((tm, D), lambda i: (i, 0))
    return pl.pallas_call(
        body, name=name, grid=(s // tm,),
        in_specs=[a_spec, _resident(w.shape), tile, _full((1, D)), tile],
        out_specs=[tile, _full((LANE_ROWS, HD))], out_shape=[SDS((s, D), F32), SDS((LANE_ROWS, HD), F32)],
        compiler_params=_cp("arbitrary"))(a, w, x, g, dres)


def _mm_tn(a, a_spec, b, b_spec, nb, out_shape, out_spec, name, a_is_transposed=True, after=None):
    def body(a_ref, b_ref, *rest):
        o_ref = rest[-1]
        bb = b_ref[...].astype(BF16)
        o_ref[...] = (_dot(a_ref[...], bb) if a_is_transposed else _dot_tn(a_ref[...], bb)).astype(BF16)

    deps = [] if after is None else [after]
    return pl.pallas_call(
        body, name=name, grid=(nb,), in_specs=[a_spec, b_spec] + [_ANY] * len(deps), out_specs=out_spec,
        out_shape=SDS(out_shape, BF16), compiler_params=_cp("parallel"))(a, b, *deps)


def _mixer_bwd(dx1, wo, h0, h1, z, lng, lnb, ws, wst, bsb, layer, tm):
    s = dx1.shape[0]
    nt = s // tm

    def body(dx_ref, wo_ref, h0_ref, h1_ref, zu_ref, zv_ref, zg_ref, za_ref, zb_ref, lng_ref, lnb_ref,
             ws_ref, wst_ref, bsb_ref, dz_ref, dh_ref, dws_ref, dbs_ref, dlng_ref, dlnb_ref,
             du_s, dv_s, ya_s, dbs_acc):
        i = pl.program_id(0)

        @pl.when(i == 0)
        def _():
            for r in (dws_ref, dlng_ref, dlnb_ref, dbs_acc):
                r[...] = jnp.zeros_like(r)

        dm = _dot_nt(dx_ref[...].astype(BF16), wo_ref[...])
        sa = _sigmoid(za_ref[...].astype(F32))
        sb = _sigmoid(zb_ref[...].astype(F32))
        zg = zg_ref[...].astype(F32)
        gg, tg = _gelu(zg)
        hs = h0_ref[...] + h1_ref[...]
        dyb = dm * sb
        dya = dm * sa
        dh_ref[...] = dyb * gg
        dz_ref[:, 2 * D:3 * D] = jnp.zeros((tm, D), BF16)
        dz_ref[:, 3 * D:4 * D] = (dyb * hs * _gelu_grad(zg, tg)).astype(BF16)
        dz_ref[:, 5 * D:6 * D] = (dm * (hs * gg) * (sb * (1.0 - sb))).astype(BF16)

        zu, zv, u, tu, tv, xh, rstd, vb = _gmlp_values(zu_ref, zv_ref, lng_ref, lnb_ref)
        for c in range(tm // HD):
            rs = slice(c * HD, (c + 1) * HD)
            for g in range(HEADS):
                cs = slice(g * HD, (g + 1) * HD)
                vblk = vb[rs, cs]
                mixed = _dot(ws_ref[g], vblk) + bsb_ref[g]
                ya_s[rs, cs] = u[rs, cs] * mixed
                du_s[rs, cs] = dya[rs, cs] * mixed
                dmx = dya[rs, cs] * u[rs, cs]
                dbs_acc[g] += dmx
                dmxb = dmx.astype(BF16)
                dws_ref[g] += _dot_nt(dmxb, vblk)
                dv_s[rs, cs] = _dot(wst_ref[g], dmxb)
        dz_ref[:, 4 * D:5 * D] = (dm * ya_s[...] * (sa * (1.0 - sa))).astype(BF16)
        dv = dv_s[...]
        _add_rows128(dlng_ref, jnp.sum(dv * xh, axis=0, keepdims=True))
        _add_rows128(dlnb_ref, jnp.sum(dv, axis=0, keepdims=True))
        dxh = dv * lng_ref[...]
        dgv = rstd * (dxh - jnp.mean(dxh, axis=-1, keepdims=True)
                      - xh * jnp.mean(dxh * xh, axis=-1, keepdims=True))
        dz_ref[:, 0:D] = (du_s[...] * _gelu_grad(zu, tu)).astype(BF16)
        dz_ref[:, D:2 * D] = (dgv * _gelu_grad(zv, tv)).astype(BF16)

        @pl.when(i == nt - 1)
        def _():
            for g in range(HEADS):
                dbs_ref[g:g + 1, :] = jnp.sum(dbs_acc[g].T, axis=0, keepdims=True)

    tile = pl.BlockSpec((tm, D), lambda i: (i, 0))
    wspec = _full((HEADS, HD, HD))
    return pl.pallas_call(
        body, name=f"mixer_bwd_l{layer}", grid=(nt,),
        in_specs=[tile, _full((D, D)), tile, tile]
        + [pl.BlockSpec((tm, D), lambda i, c=c: (i, c)) for c in (0, 1, 3, 4, 5)]
        + [_full((1, D)), _full((1, D)), wspec, wspec, wspec],
        out_specs=[pl.BlockSpec((tm, N_IN), lambda i: (i, 0)), tile, wspec, _full((HEADS, HD)),
                   _full((LANE_ROWS, HD)), _full((LANE_ROWS, HD))],
        out_shape=[SDS((s, N_IN), BF16), SDS((s, D), F32), SDS((HEADS, HD, HD), F32), SDS((HEADS, HD), F32),
                   SDS((LANE_ROWS, HD), F32), SDS((LANE_ROWS, HD), F32)],
        scratch_shapes=[pltpu.VMEM((tm, D), F32)] * 3 + [pltpu.VMEM((HEADS, HD, HD), F32)],
        compiler_params=_cp("arbitrary"))(dx1, wo, h0, h1, z, z, z, z, z, lng, lnb, ws, wst, bsb)


def _lru_gates_bwd(xcb, gates, h0, h1, g0, g1, wr, wi, lam, layer, tm):
    s = xcb.shape[0]
    nt = s // tm

    def body(xc_ref, r0_ref, i0_ref, r1_ref, i1_ref, h0p_ref, h0_ref, h1_ref, h1n_ref, g0_ref, g1_ref,
             wr_ref, wi_ref, lam_ref, dxc_ref, dwr_ref, dwi_ref, dbr_ref, dbi_ref, dlam_ref):
        i = pl.program_id(0)
        fp, fn = _halo_flags(nt)

        @pl.when(i == 0)
        def _():
            for r in (dwr_ref, dwi_ref, dbr_ref, dbi_ref, dlam_ref):
                r[...] = jnp.zeros_like(r)

        xb = xc_ref[...]
        xc = xb.astype(F32)
        zeros8 = jnp.zeros((8, D), F32)
        h_prev = _taps(h0p_ref[...] * fp, h0_ref[...], zeros8, tm)[1]
        h_next = _taps(zeros8, h1_ref[...], h1n_ref[...] * fn, tm)[3]
        dxc = jnp.zeros((tm, D), F32)
        for d, (g_ref, hsh, r_ref, i_ref) in enumerate(((g0_ref, h_prev, r0_ref, i0_ref),
                                                        (g1_ref, h_next, r1_ref, i1_ref))):
            sp = _softplus(-lam_ref[d:d + 1, :])
            r = r_ref[...].astype(F32)
            ig = i_ref[...].astype(F32)
            a, q = _lru_decay(r, sp)
            rmult = jnp.where(q > 0.0, lax.rsqrt(jnp.where(q > 0.0, q, 1.0)), 0.0)
            mult = q * rmult
            db = g_ref[...]
            da = db * hsh
            dmult = db * (ig * xc)
            di = db * (mult * xc)
            dxc = dxc + db * (mult * ig)
            dla = da * a - dmult * (a * a * rmult)
            dsp_dlam = -_sigmoid(-lam_ref[d:d + 1, :])
            _add_rows128(dlam_ref, jnp.sum(dla * r, axis=0, keepdims=True) * ((-LRU_C) * dsp_dlam), d * LANE_ROWS)
            dpr = dla * sp * (-LRU_C) * (r * (1.0 - r))
            dpi = di * (ig * (1.0 - ig))
            _add_rows128(dbr_ref, jnp.sum(dpr, axis=0, keepdims=True), d * LANE_ROWS)
            _add_rows128(dbi_ref, jnp.sum(dpi, axis=0, keepdims=True), d * LANE_ROWS)
            dprb = dpr.astype(BF16)
            dpib = dpi.astype(BF16)
            parts = []
            for h in range(HEADS):
                cs = slice(h * HD, (h + 1) * HD)
                dwr_ref[d, h] += _dot_tn(xb[:, cs], dprb[:, cs])
                dwi_ref[d, h] += _dot_tn(xb[:, cs], dpib[:, cs])
                parts.append(_dot_nt(dprb[:, cs], wr_ref[d, h]) + _dot_nt(dpib[:, cs], wi_ref[d, h]))
            dxc = dxc + jnp.concatenate(parts, axis=1)
        dxc_ref[...] = dxc.astype(BF16)

    tile = pl.BlockSpec((tm, D), lambda i: (i, 0))
    hp, hc, hn = _halo_specs(tm, s, 0)
    wspec = _full((2, HEADS, HD, HD))
    vspec = _full((2 * LANE_ROWS, HD))
    return pl.pallas_call(
        body, name=f"lru_gates_bwd_l{layer}", grid=(nt,),
        in_specs=[tile] * 5 + [hp, hc, hc, hn, tile, tile, wspec, wspec, _full((2, D))],
        out_specs=[tile, wspec, wspec, vspec, vspec, vspec],
        out_shape=[SDS((s, D), BF16), SDS((2, HEADS, HD, HD), F32), SDS((2, HEADS, HD, HD), F32)]
        + [SDS((2 * LANE_ROWS, HD), F32)] * 3,
        compiler_params=_cp("arbitrary"))(xcb, *gates, h0, h0, h1, h1, g0, g1, wr, wi, lam)


def _conv_bwd(dz, dxc, z, cw, layer, tm):
    s = z.shape[0]
    nt = s // tm

    def body(dz_in, dp_ref, dc_ref, dn_ref, zp_ref, zc_ref, zn_ref, cw_ref, dz_ref, dcw_ref, dcb_ref):
        del dz_in
        fp, fn = _halo_flags(nt)

        @pl.when(pl.program_id(0) == 0)
        def _():
            dcw_ref[...] = jnp.zeros_like(dcw_ref)
            dcb_ref[...] = jnp.zeros_like(dcb_ref)

        dxc_halo = _halo_load(dp_ref, dc_ref, dn_ref, fp, fn)
        dxc = dxc_halo[1]
        dm2, dm1, _, dp1, _ = _taps(*dxc_halo, tm)
        dz_ref[...] = (cw_ref[0:1, :] * dp1 + cw_ref[1:2, :] * dxc + cw_ref[2:3, :] * dm1
                       + cw_ref[3:4, :] * dm2).astype(BF16)
        _, zm1, z0, zp1, zp2 = _taps(*_halo_load(zp_ref, zc_ref, zn_ref, fp, fn), tm)
        for k, zt in enumerate((zm1, z0, zp1, zp2)):
            _add_rows128(dcw_ref, jnp.sum(dxc * zt, axis=0, keepdims=True), k * LANE_ROWS)
        _add_rows128(dcb_ref, jnp.sum(dxc, axis=0, keepdims=True))

    return pl.pallas_call(
        body, name=f"conv_bwd_l{layer}", grid=(nt,),
        in_specs=[pl.BlockSpec(memory_space=pl.ANY), *_halo_specs(tm, s, 0, 16), *_halo_specs(tm, s, 2, 16),
                  _full((4, D))],
        out_specs=[pl.BlockSpec((tm, D), lambda i: (i, 2)), _full((4 * LANE_ROWS, HD)), _full((LANE_ROWS, HD))],
        out_shape=[SDS((s, N_IN), BF16), SDS((4 * LANE_ROWS, HD), F32), SDS((LANE_ROWS, HD), F32)],
        input_output_aliases={0: 0},
        compiler_params=_cp("arbitrary"))(dz, dxc, dxc, dxc, z, z, z, cw)


def _me():
    return lax.axis_index("x"), lax.axis_index("y"), lax.axis_index("c")


def _peer(m):
    x, y, c = _me()
    px = 1 - x if m & 4 else x
    py = 1 - y if m & 2 else y
    pc = 1 - c if m & 1 else c
    return (px, py, pc), 4 * px + 2 * py + pc


_ANY = pl.BlockSpec(memory_space=pl.ANY)
_EXCHANGE_SEMS = [pltpu.SemaphoreType.DMA((N_DEV - 1,)), pltpu.SemaphoreType.DMA((N_DEV - 1,)), pltpu.SemaphoreType.DMA(())]


def _all_gather(v, after, name):
    def body(v_ref, after_ref, o_ref, send_sems, recv_sems, local_sem):
        del after_ref
        x, y, c = _me()
        me = 4 * x + 2 * y + c
        local = pltpu.make_async_copy(v_ref, o_ref.at[me], local_sem)
        local.start()
        sends = []
        for m in range(1, N_DEV):
            dev, _ = _peer(m)
            cp = pltpu.make_async_remote_copy(v_ref, o_ref.at[me], send_sems.at[m - 1], recv_sems.at[m - 1],
                                              device_id=dev, device_id_type=pl.DeviceIdType.MESH)
            cp.start()
            sends.append(cp)
        for m in range(1, N_DEV):
            dev, blk = _peer(m)
            pltpu.make_async_remote_copy(v_ref, o_ref.at[blk], send_sems.at[m - 1], recv_sems.at[m - 1],
                                         device_id=dev, device_id_type=pl.DeviceIdType.MESH).wait_recv()
        for cp in sends:
            cp.wait_send()
        local.wait()

    return pl.pallas_call(
        body, name=name, in_specs=[_ANY, _ANY], out_specs=_ANY,
        out_shape=SDS((N_DEV,) + v.shape, v.dtype), scratch_shapes=_EXCHANGE_SEMS)(v, after)


_HBM = pl.BlockSpec(memory_space=pltpu.HBM)
_SEM = pl.BlockSpec(memory_space=pltpu.SEMAPHORE)
_EFFECT = pltpu.CompilerParams(has_side_effects=pltpu.SideEffectType.DATAFLOW_SIDE_EFFECTING)
_PEER_SEMS = pltpu.SemaphoreType.DMA((N_DEV - 1,))


def _in_hbm(a):
    return pltpu.with_memory_space_constraint(a, pltpu.HBM)


def _remote(src, dst, send_sems, recv_sems, m):
    dev, _ = _peer(m)
    return pltpu.make_async_remote_copy(src, dst, send_sems.at[m - 1], recv_sems.at[m - 1],
                                        device_id=dev, device_id_type=pl.DeviceIdType.MESH)


def _gather_start(lands, after, name):
    n = len(lands)

    def body(*refs):
        land = refs[:n]
        sems = refs[n + 1:3 * n + 1]
        token = refs[-1]
        x, y, c = _me()
        me = 4 * x + 2 * y + c
        for t in range(n):
            for m in range(1, N_DEV):
                _remote(land[t].at[me], land[t].at[me], sems[2 * t], sems[2 * t + 1], m).start()
        token[...] = jnp.zeros_like(token)

    res = pl.pallas_call(
        body, name=name, in_specs=[_HBM] * n + [_ANY],
        out_specs=[_SEM] * (2 * n) + [_HBM] * n + [pl.BlockSpec(memory_space=pltpu.VMEM)],
        out_shape=[_PEER_SEMS] * (2 * n) + [pltpu.HBM(a.shape, a.dtype) for a in lands] + [SDS((8, 128), F32)],
        input_output_aliases={t: 2 * n + t for t in range(n)},
        compiler_params=_EFFECT)(*[_in_hbm(a) for a in lands], after)
    return [(res[2 * t], res[2 * t + 1], res[2 * n + t]) for t in range(n)], res[-1]


def _gather_wait(handle, after, name):
    send_sems, recv_sems, land = handle

    def body(land_ref, ssem, rsem, after_ref, out_ref):
        del after_ref, out_ref
        x, y, c = _me()
        me = 4 * x + 2 * y + c
        for m in range(1, N_DEV):
            _, blk = _peer(m)
            cp = _remote(land_ref.at[me], land_ref.at[blk], ssem, rsem, m)
            cp.wait_send()
            cp.wait_recv()

    return pl.pallas_call(
        body, name=name, in_specs=[_HBM, _SEM, _SEM, _ANY], out_specs=_HBM,
        out_shape=pltpu.HBM(land.shape, land.dtype), input_output_aliases={0: 0},
        compiler_params=_EFFECT)(land, send_sems, recv_sems, after)


FIRST_STAGE = (1, 2, 4, 6)
RELAYED = (2, 4, 6)
OTHER_CORE = 1


def _stage_copy(src, dst, send_sems, recv_sems, k, m):
    dev, _ = _peer(m)
    return pltpu.make_async_remote_copy(src, dst, send_sems.at[k], recv_sems.at[k],
                                        device_id=dev, device_id_type=pl.DeviceIdType.MESH)


def _gather2_start(lands, after, name):
    n = len(lands)

    def body(*refs):
        land = refs[:n]
        sems = refs[n + 1:3 * n + 1]
        token = refs[-1]
        x, y, c = _me()
        me = 4 * x + 2 * y + c
        for t in range(n):
            for k, m in enumerate(FIRST_STAGE):
                _stage_copy(land[t].at[me], land[t].at[me], sems[2 * t], sems[2 * t + 1], k, m).start()
        token[...] = jnp.zeros_like(token)

    stage_sems = pltpu.SemaphoreType.DMA((len(FIRST_STAGE),))
    res = pl.pallas_call(
        body, name=name, in_specs=[_HBM] * n + [_ANY],
        out_specs=[_SEM] * (2 * n) + [_HBM] * n + [pl.BlockSpec(memory_space=pltpu.VMEM)],
        out_shape=[stage_sems] * (2 * n) + [pltpu.HBM(a.shape, a.dtype) for a in lands] + [SDS((8, 128), F32)],
        input_output_aliases={t: 2 * n + t for t in range(n)},
        compiler_params=_EFFECT)(*[_in_hbm(a) for a in lands], after)
    return [(res[2 * t], res[2 * t + 1], res[2 * n + t]) for t in range(n)], res[-1]


def _gather2_relay(handles, after, name):
    n = len(handles)

    def body(*refs):
        land, send1, recv1 = refs[:n], refs[n:2 * n], refs[2 * n:3 * n]
        sems = refs[3 * n + 1:5 * n + 1]
        token = refs[-1]
        x, y, c = _me()
        me = 4 * x + 2 * y + c
        for t in range(n):
            for j, m in enumerate(RELAYED):
                _, blk = _peer(m)
                _stage_copy(land[t].at[me], land[t].at[blk], send1[t], recv1[t], 1 + j, m).wait_recv()
                _stage_copy(land[t].at[blk], land[t].at[blk], sems[2 * t], sems[2 * t + 1], j, OTHER_CORE).start()
        token[...] = jnp.zeros_like(token)

    relay_sems = pltpu.SemaphoreType.DMA((len(RELAYED),))
    lands = [h[2] for h in handles]
    res = pl.pallas_call(
        body, name=name, in_specs=[_HBM] * n + [_SEM] * (2 * n) + [_ANY],
        out_specs=[_SEM] * (2 * n) + [_HBM] * n + [pl.BlockSpec(memory_space=pltpu.VMEM)],
        out_shape=[relay_sems] * (2 * n) + [pltpu.HBM(a.shape, a.dtype) for a in lands] + [SDS((8, 128), F32)],
        input_output_aliases={t: 2 * n + t for t in range(n)},
        compiler_params=_EFFECT)(*lands, *[h[0] for h in handles], *[h[1] for h in handles], after)
    return [(h[0], h[1], res[2 * t], res[2 * t + 1], res[2 * n + t]) for t, h in enumerate(handles)], res[-1]


def _gather2_wait(handle, after, name):
    send1, recv1, send2, recv2, land = handle

    def body(land_ref, s1, r1, s2, r2, after_ref, out_ref):
        del after_ref, out_ref
        x, y, c = _me()
        me = 4 * x + 2 * y + c
        _, other = _peer(OTHER_CORE)
        _stage_copy(land_ref.at[me], land_ref.at[other], s1, r1, 0, OTHER_CORE).wait_recv()
        for k, m in enumerate(FIRST_STAGE):
            _stage_copy(land_ref.at[me], land_ref.at[me], s1, r1, k, m).wait_send()
        for j, m in enumerate(RELAYED):
            _, mine = _peer(m)
            _, theirs = _peer(m ^ OTHER_CORE)
            _stage_copy(land_ref.at[mine], land_ref.at[mine], s2, r2, j, OTHER_CORE).wait_send()
            _stage_copy(land_ref.at[mine], land_ref.at[theirs], s2, r2, j, OTHER_CORE).wait_recv()

    return pl.pallas_call(
        body, name=name, in_specs=[_HBM] + [_SEM] * 4 + [_ANY], out_specs=_HBM,
        out_shape=pltpu.HBM(land.shape, land.dtype), input_output_aliases={0: 0},
        compiler_params=_EFFECT)(land, send1, recv1, send2, recv2, after)


def _exchange_start(ps, name):
    n = len(ps)

    def body(*refs):
        p = refs[:n]
        got = refs[n:2 * n]
        sems = refs[2 * n:5 * n]
        token = refs[-1]
        x, y, c = _me()
        me = 4 * x + 2 * y + c
        for t in range(n):
            pltpu.make_async_copy(p[t].at[me], got[t].at[me], sems[3 * t + 2]).start()
            for m in range(1, N_DEV):
                _, blk = _peer(m)
                _remote(p[t].at[blk], got[t].at[me], sems[3 * t], sems[3 * t + 1], m).start()
        token[...] = jnp.zeros_like(token)

    res = pl.pallas_call(
        body, name=name, in_specs=[_HBM] * (2 * n),
        out_specs=[_SEM] * (3 * n) + [_HBM] * (2 * n) + [pl.BlockSpec(memory_space=pltpu.VMEM)],
        out_shape=[_PEER_SEMS, _PEER_SEMS, pltpu.SemaphoreType.DMA(())] * n
        + [pltpu.HBM(a.shape, a.dtype) for a in ps] * 2 + [SDS((8, 128), F32)],
        input_output_aliases={t: 3 * n + t for t in range(2 * n)},
        compiler_params=_EFFECT)(*[_in_hbm(a) for a in ps], *[_in_hbm(lax.empty(a.shape, a.dtype)) for a in ps])
    return [(res[3 * t], res[3 * t + 1], res[3 * t + 2], res[3 * n + t], res[4 * n + t]) for t in range(n)], res[-1]


def _exchange_wait(handle, after, name):
    send_sems, recv_sems, local_sem, p, got = handle

    def body(p_ref, got_ref, ssem, rsem, lsem, after_ref, p_out, got_out):
        del after_ref, p_out, got_out
        x, y, c = _me()
        me = 4 * x + 2 * y + c
        pltpu.make_async_copy(p_ref.at[me], got_ref.at[me], lsem).wait()
        for m in range(1, N_DEV):
            _, blk = _peer(m)
            cp = _remote(p_ref.at[blk], got_ref.at[blk], ssem, rsem, m)
            cp.wait_send()
            cp.wait_recv()

    return pl.pallas_call(
        body, name=name, in_specs=[_HBM, _HBM, _SEM, _SEM, _SEM, _ANY], out_specs=[_HBM, _HBM],
        out_shape=[pltpu.HBM(p.shape, p.dtype), pltpu.HBM(got.shape, got.dtype)],
        input_output_aliases={0: 0, 1: 1}, compiler_params=_EFFECT)(p, got, send_sems, recv_sems, local_sem, after)[1]


def _cast_into_slot(w, layer, me1, name):
    _, r, c = w.shape
    tr = next(t for t in (256, 352, r) if r % t == 0)

    def body(me_ref, w_ref, o_ref):
        del me_ref
        o_ref[...] = w_ref[...].astype(BF16)

    return pl.pallas_call(
        body, name=name,
        grid_spec=pltpu.PrefetchScalarGridSpec(
            num_scalar_prefetch=1, grid=(r // tr,),
            in_specs=[pl.BlockSpec((None, tr, c), lambda i, me: (layer, i, 0))],
            out_specs=pl.BlockSpec((None, tr, c), lambda i, me: (me[0], i, 0))),
        out_shape=SDS((N_DEV, r, c), BF16), compiler_params=_cp("arbitrary"))(me1, w)


def _cast_all_into_slots(ws, layers, me1, after, name):
    n = len(ws)

    def body(me_ref, *refs):
        del me_ref
        for w_ref, o_ref in zip(refs[:n], refs[n + 1:]):
            o_ref[...] = w_ref[...].astype(BF16)

    return pl.pallas_call(
        body, name=name,
        grid_spec=pltpu.PrefetchScalarGridSpec(
            num_scalar_prefetch=1, grid=(1,),
            in_specs=[pl.BlockSpec((None,) + a.shape[1:], lambda i, me, l=l: (l, 0, 0)) for a, l in zip(ws, layers)]
            + [_ANY],
            out_specs=[pl.BlockSpec((None,) + a.shape[1:], lambda i, me: (me[0], 0, 0)) for a in ws]),
        out_shape=[SDS((N_DEV,) + a.shape[1:], BF16) for a in ws],
        compiler_params=_cp("arbitrary"))(me1, *ws, after)


def _sum8_into_slot(p, me1, name):
    _, r, c = p.shape

    def body(me_ref, p_ref, o_ref):
        del me_ref
        acc = p_ref[0]
        for k in range(1, N_DEV):
            acc = acc + p_ref[k]
        o_ref[...] = acc

    return pl.pallas_call(
        body, name=name,
        grid_spec=pltpu.PrefetchScalarGridSpec(
            num_scalar_prefetch=1, grid=(1,),
            in_specs=[pl.BlockSpec(p.shape, lambda i, me: (0, 0, 0))],
            out_specs=pl.BlockSpec((None, r, c), lambda i, me: (me[0], 0, 0))),
        out_shape=SDS(p.shape, F32), compiler_params=_cp("arbitrary"))(me1, p)


def _adamw(w, g, m, v):
    m = ADAM_B1 * m + (1.0 - ADAM_B1) * g
    v = ADAM_B2 * v + (1.0 - ADAM_B2) * (g * g)
    m_hat = m / (1.0 - ADAM_B1 ** ADAM_STEP)
    v_hat = v / (1.0 - ADAM_B2 ** ADAM_STEP)
    delta = -ADAM_LR * (m_hat / (jnp.sqrt(v_hat) + ADAM_EPS) + ADAM_WD * w)
    return delta, m, v


def _adam_shard(parts, w, m, v, layer, prev, name):
    _, r, c = parts.shape
    tr = next(t for t in (256, 352, r) if r % t == 0)
    n_prev = 0 if prev is None else 4

    def body(*refs):
        p_ref, w_ref, m_ref, v_ref = refs[:4]
        g_ref, d_ref, nm_ref, nv_ref = refs[4 + n_prev:]
        g = p_ref[0].astype(F32)
        for k in range(1, N_DEV):
            g = g + p_ref[k].astype(F32)
        delta, nm, nv = _adamw(w_ref[...], g, m_ref[...], v_ref[...])
        g_ref[...] = g
        d_ref[...] = delta
        nm_ref[...] = nm
        nv_ref[...] = nv

    blk = pl.BlockSpec((None, tr, c), lambda i: (layer, i, 0))
    return pl.pallas_call(
        body, name=name, grid=(r // tr,),
        in_specs=[pl.BlockSpec((N_DEV, tr, c), lambda i: (0, i, 0)), blk, blk, blk] + [_ANY] * n_prev,
        out_specs=[blk] * 4, out_shape=[SDS(w.shape, F32)] * 4,
        input_output_aliases={4 + k: k for k in range(n_prev)},
        compiler_params=_cp("parallel"))(parts, w, m, v, *(prev or ()))


SMALL_MATRICES = [("lru_w_r", 2048), ("lru_w_i", 2048), ("gmlp_w_s", 1024)]
SMALL_VECTORS = [("norm1_g", 8), ("gmlp_ln_g", 8), ("gmlp_ln_b", 8), ("gmlp_b_s", 8), ("conv_w", 32), ("conv_b", 8),
                 ("lru_b_r", 16), ("lru_b_i", 16), ("lru_lambda", 16), ("norm2_g", 8), ("final_g", 8)]
SMALL_VECTOR_ROW0 = sum(n for _, n in SMALL_MATRICES)
SMALL_VECTOR_BLOCK = 256
SMALL_ROWS = SMALL_VECTOR_ROW0 + SMALL_VECTOR_BLOCK


def _pack_small(small):
    parts = [small[k] for k, _ in SMALL_MATRICES]
    parts += [small[k] if k in small else jnp.zeros((n, HD), F32) for k, n in SMALL_VECTORS]
    flat = jnp.concatenate(parts)
    return jnp.pad(flat, ((0, SMALL_ROWS - flat.shape[0]), (0, 0))).reshape(N_DEV, SMALL_ROWS // N_DEV, HD)


def _adam_matrix(g0, g1, w, m, v, row0, name):
    _, rows, _ = w.shape

    def body(g0_ref, g1_ref, w_ref, m_ref, v_ref, g_ref, d_ref, nm_ref, nv_ref):
        for l, src in enumerate((g0_ref, g1_ref)):
            g = src[...]
            delta, nm, nv = _adamw(w_ref[l], g, m_ref[l], v_ref[l])
            g_ref[l] = g
            d_ref[l] = delta
            nm_ref[l] = nm
            nv_ref[l] = nv

    gspec = pl.BlockSpec((rows, HD), lambda i: (row0 // rows, 0))
    return pl.pallas_call(body, name=name, grid=(1,), in_specs=[gspec, gspec] + [_full(w.shape)] * 3,
                          out_specs=[_full(w.shape)] * 4, out_shape=[SDS(w.shape, F32)] * 4,
                          compiler_params=_cp("arbitrary"))(g0, g1, w, m, v)


def _adam_vectors(g0, g1, dg1_parts, me1, ws, ms, vs):
    names = [k for k, _ in SMALL_VECTORS]
    n = len(names)

    def lanes(rows8):
        return jnp.concatenate([rows8[k:k + 1, :] for k in range(LANE_ROWS)], axis=1)

    def body(me_ref, g0_ref, g1_ref, dg1_ref, *refs):
        w_refs, m_refs, v_refs = refs[:n], refs[n:2 * n], refs[2 * n:3 * n]
        outs = refs[3 * n:]
        me = me_ref[0]
        g_refs = (g0_ref, g1_ref)

        def emit(i, idx, g):
            delta, nm, nv = _adamw(w_refs[i][idx], g, m_refs[i][idx], v_refs[i][idx])
            for j, val in enumerate((g, delta, nm, nv)):
                outs[4 * i + j][idx] = val

        off = 0
        for i, (name, rows) in enumerate(SMALL_VECTORS):
            for l in range(2):
                row = (slice(l, l + 1), slice(None))
                if name == "final_g":
                    if l == 1:
                        emit(i, (slice(0, 1), slice(None)), lanes(g1_ref[off:off + rows, :]))
                elif name == "norm1_g":
                    if l == 1:
                        emit(i, row, lanes(g0_ref[off:off + rows, :]))
                    else:
                        total = dg1_ref[0]
                        for k in range(1, N_DEV):
                            total = total + dg1_ref[k]
                        emit(i, row, lanes(total))
                elif name == "gmlp_b_s":
                    emit(i, (l,), g_refs[l][off:off + rows, :])
                elif rows == LANE_ROWS:
                    emit(i, row, lanes(g_refs[l][off:off + rows, :]))
                else:
                    for r in range(rows // LANE_ROWS):
                        emit(i, (l, slice(r, r + 1), slice(None)), g_refs[l][pl.ds(off + r * LANE_ROWS + me, 1), :])
            off += rows

    args = [ws[k] for k in names] + [ms[k] for k in names] + [vs[k] for k in names]
    gspec = pl.BlockSpec((SMALL_VECTOR_BLOCK, HD), lambda i, me: (SMALL_VECTOR_ROW0 // SMALL_VECTOR_BLOCK, 0))
    res = pl.pallas_call(
        body, name="adam_vectors",
        grid_spec=pltpu.PrefetchScalarGridSpec(
            num_scalar_prefetch=1, grid=(1,),
            in_specs=[gspec, gspec, _full(dg1_parts.shape)] + [_full(a.shape) for a in args],
            out_specs=[_full(ws[k].shape) for k in names for _ in range(4)]),
        out_shape=[SDS(ws[k].shape, F32) for k in names for _ in range(4)],
        compiler_params=_cp("arbitrary"))(me1, g0, g1, dg1_parts, *args)
    return {k: list(res[4 * i:4 * i + 4]) for i, k in enumerate(names)}


def _after(a, *tokens):
    for token in tokens:
        if token is not None:
            a = a + token[0:1, 0:1]
    return a


def _local_step(x, tgt, p, get_w, hook=lambda stage, layer, payload: None):
    s = x.shape[0]
    tm = _row_tile(s)
    wsb = p["gmlp_w_s"].astype(BF16)
    wstb = jnp.swapaxes(p["gmlp_w_s"], -1, -2).astype(BF16)
    bsb = jnp.broadcast_to(p["gmlp_b_s"][..., None], p["gmlp_w_s"].shape)
    wrb = p["lru_w_r"].astype(BF16)
    wib = p["lru_w_i"].astype(BF16)
    saved = []
    for l in range(2):
        win = get_w("w_in", l, x)
        z, h1 = _norm_inproj(x, _after(p["norm1_g"][l][None], hook("pre_inproj", l, win)), win, l, tm)
        a0, b0, a1, b1, xcb, *gates = _lru_gates_fwd(z, p["conv_w"][l], p["conv_b"][l][None], wrb[l], wib[l],
                                                     p["lru_b_r"][l], p["lru_b_i"][l], p["lru_lambda"][l], l, tm)
        h0, hr = _lru_scan(a0, b0, a1, b1, False, l)
        lng = _after(p["gmlp_ln_g"][l][None], hook("pre_gmlp", l, h0))
        wout = get_w("w_out", l, lng)
        x1, mg = _mixer_fwd(x, h0, hr, z, lng, p["gmlp_ln_b"][l][None], wsb[l], bsb[l], wout, l, tm)
        wfi = get_w("w_ffn_in", l, x1)
        wfo = get_w("w_ffn_out", l, x1)
        x2, ff, dff, h2 = _ffn_fwd(x1, p["norm2_g"][l][None], wfi, wfo, l, tm)
        saved.append((x, z, h1, a0, a1, h0, hr, x1, mg, ff, dff, h2, win, wout, wfi, wfo, xcb, gates))
        x = x2
    dx, loss, dfg = _loss_head(x, p["final_g"][None], tgt, tm)
    pending = None
    for l in (1, 0):
        x0, z, h1, a0, a1, h0, hr, x1, mg, ff, dff, h2, win, wout, wfi, wfo, xcb, gates = saved[l]
        dgu = _ffn_bwd_act(dx, wfo, dff, l, tm)
        d_wfo = _mm_tn(ff, pl.BlockSpec((None, s, FF_BLK), lambda j: (j, 0, 0)), dx, _resident((s, D)),
                       4, (4, FF_BLK, D), pl.BlockSpec((None, FF_BLK, D), lambda j: (j, 0, 0)),
                       f"dw_ffn_out_l{l}", a_is_transposed=False)
        dgu8 = dgu.reshape(N_DEV, s, FF_BLK)
        d_wfi = _mm_tn(dgu8, pl.BlockSpec((None, s, FF_BLK), lambda j: (j, 0, 0)), h2, _resident((s, D)),
                       N_DEV, (N_DEV, FF_BLK, D), pl.BlockSpec((None, FF_BLK, D), lambda j: (j, 0, 0)),
                       f"dw_ffn_in_l{l}", a_is_transposed=False)
        token = hook("ffn_partials", l, dict(w_ffn_out=d_wfo.reshape(N_DEV, D_FF // N_DEV, D), w_ffn_in=d_wfi))
        dx1, dg2 = _mm_nt_rms_bwd(
            dgu8, pl.BlockSpec((N_DEV, tm, FF_BLK), lambda i: (0, i, 0)), lambda r: [r[k] for k in range(N_DEV)],
            wfi.reshape(N_DEV, FF_BLK, D), True, x1, _after(p["norm2_g"][l][None], token, pending), dx,
            f"ffn_bwd_dx_l{l}", tm)
        pending = hook("mid_backward", l, dx1)
        dz, dh, dws, dbs, dlng, dlnb = _mixer_bwd(dx1, wout, h0, hr, z, p["gmlp_ln_g"][l][None],
                                                  p["gmlp_ln_b"][l][None], wsb[l], wstb[l], bsb[l], l, tm)
        d_wout = _mm_tn(mg, _resident((D, s)), dx1, pl.BlockSpec((s, D // 2), lambda j: (0, j)),
                        2, (D, D), pl.BlockSpec((D, D // 2), lambda j: (0, j)), f"dw_out_l{l}")
        g1, g0 = _lru_scan(a1, dh, a0, dh, True, l)
        dxc, dwr, dwi, dbr, dbi, dlam = _lru_gates_bwd(
            xcb, gates, h0, hr, g0, g1, wrb[l], wib[l], _after(p["lru_lambda"][l], pending), l, tm)
        dz, dcw, dcb = _conv_bwd(dz, dxc, z, p["conv_w"][l], l, tm)
        small = dict(lru_w_r=dwr.reshape(-1, HD), lru_w_i=dwi.reshape(-1, HD), gmlp_w_s=dws.reshape(-1, HD),
                     gmlp_ln_g=dlng, gmlp_ln_b=dlnb, gmlp_b_s=dbs, conv_w=dcw, conv_b=dcb, lru_b_r=dbr,
                     lru_b_i=dbi, lru_lambda=dlam, norm2_g=dg2)
        if l == 1:
            small["final_g"] = dfg
        else:
            small["norm1_g"] = dg1
        started = hook("small_grads", l, small)
        d_win = _mm_tn(h1, _resident((D, s)), dz, pl.BlockSpec((s, IN_BLK), lambda j: (0, j)),
                       N_DEV, (N_DEV, D, IN_BLK), pl.BlockSpec((None, D, IN_BLK), lambda j: (j, 0, 0)),
                       f"dw_in_l{l}", after=started)
        token = hook("mixer_partials", l, dict(w_out=d_wout.reshape(N_DEV, D // N_DEV, D), w_in=d_win))
        dx, dg1 = _mm_nt_rms_bwd(
            dz, pl.BlockSpec((tm, N_IN), lambda i: (i, 0)),
            lambda r: [r[:, k * IN_BLK:(k + 1) * IN_BLK] for k in range(N_DEV)],
            win, False, x0, _after(p["norm1_g"][l][None], token, started, pending), dx1, f"inproj_bwd_dx_l{l}", tm)
        pending = None
    return loss, dx, dg1


_REPL = ["norm1_g", "gmlp_ln_g", "gmlp_ln_b", "gmlp_w_s", "gmlp_b_s", "conv_b", "lru_w_r", "lru_w_i", "norm2_g", "final_g"]
_LANE_SHARDED = ["conv_w", "lru_b_r", "lru_b_i", "lru_lambda"]
_BIG = ["w_in", "w_out", "w_ffn_in", "w_ffn_out"]
_ORDER = ["norm1_g", "w_in", "gmlp_ln_g", "gmlp_ln_b", "gmlp_w_s", "gmlp_b_s", "conv_w", "conv_b", "lru_w_r", "lru_b_r",
          "lru_w_i", "lru_b_i", "lru_lambda", "w_out", "norm2_g", "w_ffn_in", "w_ffn_out", "final_g"]


def kernel(x, norm1_g, w_in, gmlp_ln_g, gmlp_ln_b, gmlp_w_s, gmlp_b_s, conv_w, conv_b, lru_w_r, lru_b_r, lru_w_i, lru_b_i, lru_lambda, w_out, norm2_g, w_ffn_in, w_ffn_out, final_g, loss_target, m_norm1_g, m_w_in, m_gmlp_ln_g, m_gmlp_ln_b, m_gmlp_w_s, m_gmlp_b_s, m_conv_w, m_conv_b, m_lru_w_r, m_lru_b_r, m_lru_w_i, m_lru_b_i, m_lru_lambda, m_w_out, m_norm2_g, m_w_ffn_in, m_w_ffn_out, m_final_g, v_norm1_g, v_w_in, v_gmlp_ln_g, v_gmlp_ln_b, v_gmlp_w_s, v_gmlp_b_s, v_conv_w, v_conv_b, v_lru_w_r, v_lru_b_r, v_lru_w_i, v_lru_b_i, v_lru_lambda, v_w_out, v_norm2_g, v_w_ffn_in, v_w_ffn_out, v_final_g):
    w = dict(norm1_g=norm1_g, w_in=w_in, gmlp_ln_g=gmlp_ln_g, gmlp_ln_b=gmlp_ln_b, gmlp_w_s=gmlp_w_s, gmlp_b_s=gmlp_b_s,
             conv_w=conv_w, conv_b=conv_b, lru_w_r=lru_w_r, lru_b_r=lru_b_r, lru_w_i=lru_w_i, lru_b_i=lru_b_i,
             lru_lambda=lru_lambda, w_out=w_out, norm2_g=norm2_g, w_ffn_in=w_ffn_in, w_ffn_out=w_ffn_out, final_g=final_g)
    mom = dict(norm1_g=m_norm1_g, w_in=m_w_in, gmlp_ln_g=m_gmlp_ln_g, gmlp_ln_b=m_gmlp_ln_b, gmlp_w_s=m_gmlp_w_s,
               gmlp_b_s=m_gmlp_b_s, conv_w=m_conv_w, conv_b=m_conv_b, lru_w_r=m_lru_w_r, lru_b_r=m_lru_b_r,
               lru_w_i=m_lru_w_i, lru_b_i=m_lru_b_i, lru_lambda=m_lru_lambda, w_out=m_w_out, norm2_g=m_norm2_g,
               w_ffn_in=m_w_ffn_in, w_ffn_out=m_w_ffn_out, final_g=m_final_g)
    var = dict(norm1_g=v_norm1_g, w_in=v_w_in, gmlp_ln_g=v_gmlp_ln_g, gmlp_ln_b=v_gmlp_ln_b, gmlp_w_s=v_gmlp_w_s,
               gmlp_b_s=v_gmlp_b_s, conv_w=v_conv_w, conv_b=v_conv_b, lru_w_r=v_lru_w_r, lru_b_r=v_lru_b_r,
               lru_w_i=v_lru_w_i, lru_b_i=v_lru_b_i, lru_lambda=v_lru_lambda, w_out=v_w_out, norm2_g=v_norm2_g,
               w_ffn_in=v_w_ffn_in, w_ffn_out=v_w_ffn_out, final_g=v_final_g)
    for src in (w, mom, var):
        src["w_ffn_in"] = jnp.swapaxes(src["w_ffn_in"], 1, 2)
    xi, yi, ci = _me()
    me = 4 * xi + 2 * yi + ci

    lane_shapes = [w[k].shape for k in _LANE_SHARDED]
    lane_rows = sum(a[0] * a[1] for a in lane_shapes)
    packed = jnp.concatenate([w[k].reshape(-1, HD) for k in _LANE_SHARDED])
    packed = jnp.pad(packed, ((0, -lane_rows % 8), (0, 0)))

    me1 = jnp.reshape(me, (1,)).astype(jnp.int32)
    gathers = {}
    exchanges = {}
    views = dict(w_in=(N_DEV, D, IN_BLK), w_out=(D, D), w_ffn_in=(2, 4, FF_BLK, D), w_ffn_out=(4, FF_BLK, D))
    small_ex = {}
    small_ag = {}

    casts = {}

    def start_gather(names, l, after):
        lands = [casts[(k, l)] if (k, l) in casts else _cast_into_slot(w[k], l, me1, f"cast_{k}_l{l}") for k in names]
        started, tok = _gather2_start(lands, after, f"gather_start_{'_'.join(names)}_l{l}")
        gathers.update({(k, l): h for k, h in zip(names, started)})
        return tok

    def relay_gather(names, l, after):
        relayed, tok = _gather2_relay([gathers[(k, l)] for k in names], after, f"gather_relay_{'_'.join(names)}_l{l}")
        gathers.update({(k, l): h for k, h in zip(names, relayed)})
        return tok

    def get_w(k, l, after):
        if (k, l) == ("w_in", 1):
            after = relay_gather(_BIG[:1], l, after)
        return _gather2_wait(gathers[(k, l)], after, f"gather_wait_{k}_l{l}").reshape(views[k])

    def hook(stage, l, payload):
        if stage == "pre_inproj":
            return start_gather(_BIG[1:], l, payload)
        if stage == "pre_gmlp":
            tok = relay_gather(_BIG[1:], l, payload)
            return tok + start_gather(_BIG[:1], l + 1, tok) if l == 0 else tok
        if stage == "small_grads":
            (small_ex[l],), tok = _exchange_start([_pack_small(payload)], f"exchange_start_small_l{l}")
            return tok
        if stage == "mid_backward":
            return reduce_small(l + 1, payload) if l == 0 else None
        extra = reduce_small(0, payload["w_in"]) if (stage, l) == ("mixer_partials", 0) else None
        started, tok = _exchange_start(list(payload.values()), f"exchange_start_{'_'.join(payload)}_l{l}")
        exchanges.update({(k, l): h for k, h in zip(payload, started)})
        return tok if extra is None else tok + extra

    def reduce_small(l, after):
        got = _exchange_wait(small_ex[l], after, f"exchange_wait_small_l{l}")
        mine = _sum8_into_slot(got, me1, f"sum_small_l{l}")
        (small_ag[l],), tok = _gather_start([mine], got, f"gather_start_small_l{l}")
        return tok

    land = lax.dynamic_update_slice(jnp.zeros((N_DEV,) + packed.shape, F32), packed[None], (me, 0, 0))
    (lanes_handle,), token = _gather_start([land], packed, "gather_start_lanes")
    token = start_gather(_BIG[:1], 0, token)
    later = [(k, l) for l in range(2) for k in _BIG if (k, l) != ("w_in", 0)]
    casts.update(zip(later, _cast_all_into_slots([w[k] for k, _ in later], [l for _, l in later], me1, token,
                                                 "cast_later_weights")))
    token = relay_gather(_BIG[:1], 0, casts[later[0]])
    lanes = _gather_wait(lanes_handle, token, "gather_wait_lanes")
    params = {k: w[k] for k in _REPL}
    off = 0
    for k, shp in zip(_LANE_SHARDED, lane_shapes):
        n = shp[0] * shp[1]
        params[k] = jnp.swapaxes(lanes[:, off:off + n], 0, 1).reshape(shp[0], shp[1], D)
        off += n
    loss, dx, dg1 = _local_step(x[0], loss_target[0], params, get_w, hook)

    out = {}
    after = dx
    for k, l in [(k, l) for k in ("w_ffn_out", "w_ffn_in") for l in (1, 0)] + [("w_out", 1), ("w_in", 1)]:
        got = _exchange_wait(exchanges[(k, l)], after, f"exchange_wait_{k}_l{l}")
        out[k] = _adam_shard(got, w[k], mom[k], var[k], l, out.get(k), f"adam_{k}_l{l}")
        after = out[k][3]
    g_small = [_gather_wait(small_ag[l], after, f"gather_wait_small_l{l}").reshape(SMALL_ROWS, HD) for l in (0, 1)]
    row0 = 0
    for k, rows in SMALL_MATRICES:
        res = _adam_matrix(*g_small, *[src[k].reshape(2, rows, HD) for src in (w, mom, var)], row0, f"adam_{k}")
        out[k] = [a.reshape(w[k].shape) for a in res]
        after = res[3]
        row0 += rows
    for k in ("w_out", "w_in"):
        got = _exchange_wait(exchanges[(k, 0)], after, f"exchange_wait_{k}_l0")
        out[k] = _adam_shard(got, w[k], mom[k], var[k], 0, out[k], f"adam_{k}_l0")
    out["w_ffn_in"] = [jnp.swapaxes(a, 1, 2) for a in out["w_ffn_in"]]
    as_rows = lambda a: a.reshape(1, D) if a.ndim == 1 else a
    vec = _adam_vectors(*g_small, _all_gather(dg1, out["w_in"][3], "gather_norm1_grad"), me1,
                        *[{k: as_rows(src[k]) for k, _ in SMALL_VECTORS} for src in (w, mom, var)])
    out.update({k: [a.reshape(w[k].shape) for a in res] for k, res in vec.items()})

    loss = lax.psum(loss[0, 0], MESH_AXES)
    return (loss, dx[None], *[out[k][0] for k in _ORDER], *[out[k][1] for k in _ORDER],
            *[out[k][2] for k in _ORDER], *[out[k][3] for k in _ORDER])
```

```python
import jax
import jax.numpy as jnp
from jax import lax
from jax.experimental import pallas as pl
from jax.experimental.pallas import tpu as pltpu

F32 = jnp.float32
BF16 = jnp.bfloat16
SDS = jax.ShapeDtypeStruct

D = 1024
N_IN = 6 * D
D_FF = 2816
N_DEV = 8
IN_BLK = N_IN // N_DEV
FF_BLK = 2 * D_FF // N_DEV
HEADS = 8
HD = 128
EPS = 1e-6
LRU_C = 8.0
MESH_AXES = ("x", "y", "c")

ADAM_LR = 0.001
ADAM_B1 = 0.9
ADAM_B2 = 0.999
ADAM_EPS = 1e-08
ADAM_WD = 0.01
ADAM_STEP = 10

VMEM_LIMIT = 56 * 2**20


def _cp(*sem, **kw):
    return pltpu.CompilerParams(dimension_semantics=sem, vmem_limit_bytes=VMEM_LIMIT, **kw)


def _row_tile(s):
    return 512 if s >= 1024 else s // 2


_GELU_C = 0.7978845608028654


def _gelu(x):
    t = jnp.tanh(_GELU_C * (x + 0.044715 * (x * x * x)))
    return 0.5 * x * (1.0 + t), t


def _gelu_grad(x, t):
    return 0.5 * (1.0 + t) + 0.5 * x * (1.0 - t * t) * (_GELU_C * (1.0 + 0.134145 * (x * x)))


def _sigmoid(x):
    return 0.5 + 0.5 * jnp.tanh(0.5 * x)


def _softplus(x):
    e = jnp.exp(-jnp.abs(x))
    w = 1.0 + e
    l1p = jnp.where(w == 1.0, e, jnp.log(w) * e / jnp.where(w == 1.0, 1.0, w - 1.0))
    return jnp.maximum(x, 0.0) + l1p


def _rms_fwd(x, g):
    r = lax.rsqrt(jnp.mean(x * x, axis=-1, keepdims=True) + EPS)
    return x * r * g


def _rms_bwd(x, g, dh):
    r = lax.rsqrt(jnp.mean(x * x, axis=-1, keepdims=True) + EPS)
    xh = x * r
    dxh = dh * g
    dx = r * (dxh - xh * jnp.mean(dxh * xh, axis=-1, keepdims=True))
    dg = jnp.sum(dh * xh, axis=0, keepdims=True)
    return dx, dg


LANE_ROWS = D // HD


def _add_rows128(ref, vec, row0=0):
    for i in range(vec.shape[0]):
        for k in range(LANE_ROWS):
            j = row0 + i * LANE_ROWS + k
            ref[j:j + 1, :] += vec[i:i + 1, k * HD:(k + 1) * HD]


def _dot(a, b):
    return jnp.dot(a, b, preferred_element_type=F32)


def _dot_nt(a, b):
    return lax.dot_general(a, b, (((1,), (1,)), ((), ())), preferred_element_type=F32)


def _dot_tn(a, b):
    return lax.dot_general(a, b, (((0,), (0,)), ((), ())), preferred_element_type=F32)


def _taps(prev, cur, nxt, tm):
    hr = prev.shape[0]
    ext = jnp.concatenate([prev, cur, nxt], axis=0)
    n = tm + 2 * hr
    sl = slice(hr, hr + tm)
    return (pltpu.roll(ext, 2, 0)[sl], pltpu.roll(ext, 1, 0)[sl], cur,
            pltpu.roll(ext, n - 1, 0)[sl], pltpu.roll(ext, n - 2, 0)[sl])


def _halo_specs(tm, s, col, rows=8):
    nb = s // rows
    r = tm // rows
    return (pl.BlockSpec((rows, D), lambda i: (jnp.maximum(i * r - 1, 0), col)),
            pl.BlockSpec((tm, D), lambda i: (i, col)),
            pl.BlockSpec((rows, D), lambda i: (jnp.minimum((i + 1) * r, nb - 1), col)))


def _halo_load(prev_ref, cur_ref, next_ref, fp, fn):
    return prev_ref[...].astype(F32) * fp, cur_ref[...].astype(F32), next_ref[...].astype(F32) * fn


def _halo_flags(nt):
    i = pl.program_id(0)
    return (i > 0).astype(F32), (i < nt - 1).astype(F32)


def _full(shape):
    nd = len(shape)
    return pl.BlockSpec(shape, lambda *_: (0,) * nd)


def _resident(shape):
    nd = len(shape)
    return pl.BlockSpec(shape, lambda *_: (0,) * nd, pipeline_mode=pl.Buffered(1))


def _norm_inproj(x, g, w, layer, tm):
    s = x.shape[0]

    def body(x_ref, g_ref, w_ref, z_ref, ht_ref):
        h32 = _rms_fwd(x_ref[...], g_ref[...])
        ht_ref[...] = h32.T.astype(BF16)
        h = h32.astype(BF16)
        for j in range(N_DEV):
            z_ref[:, j * IN_BLK:(j + 1) * IN_BLK] = _dot(h, w_ref[j]).astype(BF16)

    return pl.pallas_call(
        body, name=f"norm_inproj_l{layer}", grid=(s // tm,),
        in_specs=[pl.BlockSpec((tm, D), lambda i: (i, 0)), _full((1, D)), _resident((N_DEV, D, IN_BLK))],
        out_specs=[pl.BlockSpec((tm, N_IN), lambda i: (i, 0)), pl.BlockSpec((D, tm), lambda i: (0, i))],
        out_shape=[SDS((s, N_IN), BF16), SDS((D, s), BF16)],
        compiler_params=_cp("parallel"))(x, g, w)


def _gmlp_values(zu_ref, zv_ref, lng_ref, lnb_ref):
    zu = zu_ref[...].astype(F32)
    zv = zv_ref[...].astype(F32)
    u, tu = _gelu(zu)
    gv, tv = _gelu(zv)
    xc = gv - jnp.mean(gv, axis=-1, keepdims=True)
    rstd = lax.rsqrt(jnp.mean(xc * xc, axis=-1, keepdims=True) + EPS)
    xh = xc * rstd
    vb = (xh * lng_ref[...] + lnb_ref[...]).astype(BF16)
    return zu, zv, u, tu, tv, xh, rstd, vb


def _mixer_fwd(x, h0, h1, z, lng, lnb, ws, bsb, wo, layer, tm):
    s = x.shape[0]

    def body(x_ref, h0_ref, h1_ref, zu_ref, zv_ref, zg_ref, za_ref, zb_ref, lng_ref, lnb_ref, ws_ref, bsb_ref,
             wo_ref, x1_ref, mg_ref, ya_s):
        _, _, u, _, _, _, _, vb = _gmlp_values(zu_ref, zv_ref, lng_ref, lnb_ref)
        for c in range(tm // HD):
            rs = slice(c * HD, (c + 1) * HD)
            for g in range(HEADS):
                cs = slice(g * HD, (g + 1) * HD)
                ya_s[rs, cs] = u[rs, cs] * (_dot(ws_ref[g], vb[rs, cs]) + bsb_ref[g])
        gg, _ = _gelu(zg_ref[...].astype(F32))
        yb = (h0_ref[...] + h1_ref[...]) * gg
        m32 = _sigmoid(za_ref[...].astype(F32)) * ya_s[...] + _sigmoid(zb_ref[...].astype(F32)) * yb
        mg_ref[...] = m32.T.astype(BF16)
        x1_ref[...] = x_ref[...] + _dot(m32.astype(BF16), wo_ref[...])

    tile = pl.BlockSpec((tm, D), lambda i: (i, 0))
    wspec = _full((HEADS, HD, HD))
    return pl.pallas_call(
        body, name=f"mixer_fwd_l{layer}", grid=(s // tm,),
        in_specs=[tile, tile, tile] + [pl.BlockSpec((tm, D), lambda i, c=c: (i, c)) for c in (0, 1, 3, 4, 5)]
        + [_full((1, D)), _full((1, D)), wspec, wspec, _full((D, D))],
        out_specs=[tile, pl.BlockSpec((D, tm), lambda i: (0, i))], out_shape=[SDS((s, D), F32), SDS((D, s), BF16)],
        scratch_shapes=[pltpu.VMEM((tm, D), F32)],
        compiler_params=_cp("parallel"))(x, h0, h1, z, z, z, z, z, lng, lnb, ws, bsb, wo)


def _conv(taps, cw_ref, cb_ref):
    _, m1, c0, p1, p2 = taps
    return cb_ref[...] + m1 * cw_ref[0:1, :] + c0 * cw_ref[1:2, :] + p1 * cw_ref[2:3, :] + p2 * cw_ref[3:4, :]


def _heads_dot(xb, w_ref, d):
    return jnp.concatenate([_dot(xb[:, h * HD:(h + 1) * HD], w_ref[d, h]) for h in range(HEADS)], axis=1)


def _lru_decay(r, sp):
    la = (-LRU_C) * r * sp
    a = jnp.exp(la)
    return a, jnp.tanh(-la) * (a * a + 1.0)


def _lru_gates_fwd(z, cw, cb, wr, wi, br, bi, lam, layer, tm):
    s = z.shape[0]
    nt = s // tm

    def body(zp_ref, zc_ref, zn_ref, cw_ref, cb_ref, wr_ref, wi_ref, br_ref, bi_ref, lam_ref,
             a0_ref, b0_ref, a1_ref, b1_ref, xc_ref, r0_ref, i0_ref, r1_ref, i1_ref):
        fp, fn = _halo_flags(nt)
        xc = _conv(_taps(*_halo_load(zp_ref, zc_ref, zn_ref, fp, fn), tm), cw_ref, cb_ref)
        xb = xc.astype(BF16)
        xc_ref[...] = xb
        for d, (a_ref, b_ref, r_ref, i_ref) in enumerate(((a0_ref, b0_ref, r0_ref, i0_ref),
                                                          (a1_ref, b1_ref, r1_ref, i1_ref))):
            r = _sigmoid(_heads_dot(xb, wr_ref, d) + br_ref[d:d + 1, :])
            ig = _sigmoid(_heads_dot(xb, wi_ref, d) + bi_ref[d:d + 1, :])
            a, q = _lru_decay(r, _softplus(-lam_ref[d:d + 1, :]))
            a_ref[...] = a
            b_ref[...] = jnp.sqrt(q) * (ig * xc)
            r_ref[...] = r.astype(BF16)
            i_ref[...] = ig.astype(BF16)

    tile = pl.BlockSpec((tm, D), lambda i: (i, 0))
    return pl.pallas_call(
        body, name=f"lru_gates_fwd_l{layer}", grid=(nt,),
        in_specs=[*_halo_specs(tm, s, 2, 16), _full((4, D)), _full((1, D)),
                  _full((2, HEADS, HD, HD)), _full((2, HEADS, HD, HD)), _full((2, D)), _full((2, D)), _full((2, D))],
        out_specs=[tile] * 9, out_shape=[SDS((s, D), F32)] * 4 + [SDS((s, D), BF16)] * 5,
        compiler_params=_cp("parallel"))(z, z, z, cw, cb, wr, wi, br, bi, lam)


def _scan_group(a, x, c, reverse, bwd):
    row = lax.broadcasted_iota(jnp.int32, a.shape, 0)
    b = a * x if bwd else x
    for d in (1, 2, 4):
        keep = (row < 8 - d) if reverse else (row >= d)
        sh = 8 - d if reverse else d
        a_s = jnp.where(keep, pltpu.roll(a, sh, 0), 1.0)
        b_s = jnp.where(keep, pltpu.roll(b, sh, 0), 0.0)
        b = a * b_s + b
        a = a * a_s
    h = b + a * c
    new_c = h[0:1, :] if reverse else h[7:8, :]
    if not bwd:
        return h, new_c
    if reverse:
        prev = jnp.where(row < 7, pltpu.roll(h, 7, 0), c)
    else:
        prev = jnp.where(row >= 1, pltpu.roll(h, 1, 0), c)
    return x + prev, new_c


def _lru_scan(a_f, x_f, a_r, x_r, bwd, layer):
    s = a_f.shape[0]
    ts = min(1024, s // 2)
    cb = 512
    nt = s // ts
    ng = ts // 8

    def body(af_ref, xf_ref, ar_ref, xr_ref, of_ref, or_ref, cf, cr):
        @pl.when(pl.program_id(1) == 0)
        def _():
            cf[...] = jnp.zeros_like(cf)
            cr[...] = jnp.zeros_like(cr)

        def step(j, carry):
            c_f, c_r = carry
            rf = pl.multiple_of(j * 8, 8)
            rr = pl.multiple_of((ng - 1 - j) * 8, 8)
            o, c_f = _scan_group(af_ref[pl.ds(rf, 8), :], xf_ref[pl.ds(rf, 8), :], c_f, False, bwd)
            of_ref[pl.ds(rf, 8), :] = o
            o, c_r = _scan_group(ar_ref[pl.ds(rr, 8), :], xr_ref[pl.ds(rr, 8), :], c_r, True, bwd)
            or_ref[pl.ds(rr, 8), :] = o
            return c_f, c_r

        c_f, c_r = lax.fori_loop(0, ng, step, (cf[0:1, :], cr[0:1, :]), unroll=2)
        cf[...] = jnp.broadcast_to(c_f, cf.shape)
        cr[...] = jnp.broadcast_to(c_r, cr.shape)

    fwd = pl.BlockSpec((ts, cb), lambda c, t: (t, c))
    rev = pl.BlockSpec((ts, cb), lambda c, t: (nt - 1 - t, c))
    return pl.pallas_call(
        body, name=f"lru_scan_{'bwd' if bwd else 'fwd'}_l{layer}", grid=(D // cb, nt),
        in_specs=[fwd, fwd, rev, rev], out_specs=[fwd, rev],
        out_shape=[SDS((s, D), F32)] * 2,
        scratch_shapes=[pltpu.VMEM((8, cb), F32), pltpu.VMEM((8, cb), F32)],
        compiler_params=_cp("parallel", "arbitrary"))(a_f, x_f, a_r, x_r)


def _ffn_fwd(x1, g, wfi, wfo, layer, tm, head=None):
    s = x1.shape[0]

    def ffn(x_ref, g_ref, wi_ref, wo_ref, ff_ref, dff_ref, h_ref):
        x = x_ref[...]
        h = _rms_fwd(x, g_ref[...]).astype(BF16)
        h_ref[...] = h
        acc = x
        for k in range(4):
            gate = _dot_nt(h, wi_ref[0, k])
            up = _dot_nt(h, wi_ref[1, k])
            sg = _sigmoid(gate)
            silu = gate * sg
            ff = (silu * up).astype(BF16)
            ff_ref[k] = ff
            dff_ref[0, k] = (up * (sg * (1.0 + gate * (1.0 - sg)))).astype(BF16)
            dff_ref[1, k] = silu.astype(BF16)
            acc = acc + _dot(ff, wo_ref[k])
        return acc

    def body(x_ref, g_ref, wi_ref, wo_ref, x2_ref, ff_ref, dff_ref, h_ref):
        x2_ref[...] = ffn(x_ref, g_ref, wi_ref, wo_ref, ff_ref, dff_ref, h_ref)

    def body_with_head(x_ref, g_ref, wi_ref, wo_ref, fg_ref, t_ref, dx_ref, loss_ref, dfg_ref, ff_ref, dff_ref, h_ref):
        @pl.when(pl.program_id(0) == 0)
        def _():
            loss_ref[...] = jnp.zeros_like(loss_ref)
            dfg_ref[...] = jnp.zeros_like(dfg_ref)

        x2 = ffn(x_ref, g_ref, wi_ref, wo_ref, ff_ref, dff_ref, h_ref)
        fg = fg_ref[...]
        e = _rms_fwd(x2, fg) - t_ref[...]
        rows = jnp.sum(e * e, axis=-1, keepdims=True)
        loss_ref[...] += (0.5 / D) * jnp.sum(rows, axis=0, keepdims=True)
        dx, dg = _rms_bwd(x2, fg, e * (1.0 / D))
        dx_ref[...] = dx
        _add_rows128(dfg_ref, dg)

    tile = pl.BlockSpec((tm, D), lambda i: (i, 0))
    weights = [_resident((2, 4, FF_BLK, D)), _resident((4, FF_BLK, D))]
    kept_specs = [pl.BlockSpec((4, tm, FF_BLK), lambda i: (0, i, 0)),
                  pl.BlockSpec((2, 4, tm, FF_BLK), lambda i: (0, 0, i, 0)), tile]
    kept_shapes = [SDS((4, s, FF_BLK), BF16), SDS((2, 4, s, FF_BLK), BF16), SDS((s, D), BF16)]
    if head is None:
        return pl.pallas_call(
            body, name=f"ffn_fwd_l{layer}", grid=(s // tm,),
            in_specs=[tile, _full((1, D))] + weights, out_specs=[tile] + kept_specs,
            out_shape=[SDS((s, D), F32)] + kept_shapes, compiler_params=_cp("parallel"))(x1, g, wfi, wfo)
    final_g, tgt = head
    return pl.pallas_call(
        body_with_head, name=f"ffn_fwd_loss_l{layer}", grid=(s // tm,),
        in_specs=[tile, _full((1, D))] + weights + [_full((1, D)), tile],
        out_specs=[tile, _full((1, 1)), _full((LANE_ROWS, HD))] + kept_specs,
        out_shape=[SDS((s, D), F32), SDS((1, 1), F32), SDS((LANE_ROWS, HD), F32)] + kept_shapes,
        compiler_params=_cp("arbitrary"))(x1, g, wfi, wfo, final_g, tgt)


def _ffn_bwd(dx2, wfo, factors, wfi, x1, g, layer, tm):
    s = dx2.shape[0]

    def body(dx_ref, wo_ref, f_ref, wi_ref, x_ref, g_ref, dgu_ref, dx1_ref, dg_ref):
        @pl.when(pl.program_id(0) == 0)
        def _():
            dg_ref[...] = jnp.zeros_like(dg_ref)

        dx = dx_ref[...]
        dxb = dx.astype(BF16)
        dh = None
        for k in range(4):
            dff = _dot_nt(dxb, wo_ref[k])
            d_gate = (dff * f_ref[0, k].astype(F32)).astype(BF16)
            d_up = (dff * f_ref[1, k].astype(F32)).astype(BF16)
            dgu_ref[0, k] = d_gate
            dgu_ref[1, k] = d_up
            part = _dot(d_gate, wi_ref[k]) + _dot(d_up, wi_ref[4 + k])
            dh = part if dh is None else dh + part
        dxn, dg = _rms_bwd(x_ref[...], g_ref[...], dh)
        dx1_ref[...] = dx + dxn
        _add_rows128(dg_ref, dg)

    tile = pl.BlockSpec((tm, D), lambda i: (i, 0))
    blk = pl.BlockSpec((2, 4, tm, FF_BLK), lambda i: (0, 0, i, 0))
    return pl.pallas_call(
        body, name=f"ffn_bwd_l{layer}", grid=(s // tm,),
        in_specs=[tile, _resident((4, FF_BLK, D)), blk, _resident((N_DEV, FF_BLK, D)), tile, _full((1, D))],
        out_specs=[blk, tile, _full((LANE_ROWS, HD))],
        out_shape=[SDS((2, 4, s, FF_BLK), BF16), SDS((s, D), F32), SDS((LANE_ROWS, HD), F32)],
        compiler_params=_cp("arbitrary"))(dx2, wfo, factors, wfi, x1, g)


def _mm_nt_rms_bwd(a, a_spec, a_blocks, w, w_is_transposed, x, g, dres, name, tm):
    s = x.shape[0]

    def body(a_ref, w_ref, x_ref, g_ref, dres_ref, dx_ref, dg_ref):
        @pl.when(pl.program_id(0) == 0)
        def _():
            dg_ref[...] = jnp.zeros_like(dg_ref)

        dh = None
        for k, blk in enumerate(a_blocks(a_ref)):
            part = _dot(blk, w_ref[k]) if w_is_transposed else _dot_nt(blk, w_ref[k])
            dh = part if dh is None else dh + part
        dx, dg = _rms_bwd(x_ref[...], g_ref[...], dh)
        dx_ref[...] = dres_ref[...] + dx
        _add_rows128(dg_ref, dg)

    tile = pl.BlockSpec((tm, D), lambda i: (i, 0))
    return pl.pallas_call(
        body, name=name, grid=(s // tm,),
        in_specs=[a_spec, _resident(w.shape), tile, _full((1, D)), tile],
        out_specs=[tile, _full((LANE_ROWS, HD))], out_shape=[SDS((s, D), F32), SDS((LANE_ROWS, HD), F32)],
        compiler_params=_cp("arbitrary"))(a, w, x, g, dres)


def _mm_tn(a, a_spec, b, b_spec, nb, out_shape, out_spec, name, a_is_transposed=True, after=None):
    def body(a_ref, b_ref, *rest):
        o_ref = rest[-1]
        bb = b_ref[...].astype(BF16)
        o_ref[...] = (_dot(a_ref[...], bb) if a_is_transposed else _dot_tn(a_ref[...], bb)).astype(BF16)

    deps = [] if after is None else [after]
    return pl.pallas_call(
        body, name=name, grid=(nb,), in_specs=[a_spec, b_spec] + [_ANY] * len(deps), out_specs=out_spec,
        out_shape=SDS(out_shape, BF16), compiler_params=_cp("parallel"))(a, b, *deps)


def _mixer_bwd(dx1, wo, h0, h1, z, lng, lnb, ws, wst, bsb, layer, tm):
    s = dx1.shape[0]
    nt = s // tm

    def body(dx_ref, wo_ref, h0_ref, h1_ref, zu_ref, zv_ref, zg_ref, za_ref, zb_ref, lng_ref, lnb_ref,
             ws_ref, wst_ref, bsb_ref, dz_ref, dh_ref, dws_ref, dbs_ref, dlng_ref, dlnb_ref,
             du_s, dv_s, ya_s, dbs_acc):
        i = pl.program_id(0)

        @pl.when(i == 0)
        def _():
            for r in (dws_ref, dlng_ref, dlnb_ref, dbs_acc):
                r[...] = jnp.zeros_like(r)

        dm = _dot_nt(dx_ref[...].astype(BF16), wo_ref[...])
        sa = _sigmoid(za_ref[...].astype(F32))
        sb = _sigmoid(zb_ref[...].astype(F32))
        zg = zg_ref[...].astype(F32)
        gg, tg = _gelu(zg)
        hs = h0_ref[...] + h1_ref[...]
        dyb = dm * sb
        dya = dm * sa
        dh_ref[...] = dyb * gg
        dz_ref[:, 2 * D:3 * D] = jnp.zeros((tm, D), BF16)
        dz_ref[:, 3 * D:4 * D] = (dyb * hs * _gelu_grad(zg, tg)).astype(BF16)
        dz_ref[:, 5 * D:6 * D] = (dm * (hs * gg) * (sb * (1.0 - sb))).astype(BF16)

        zu, zv, u, tu, tv, xh, rstd, vb = _gmlp_values(zu_ref, zv_ref, lng_ref, lnb_ref)
        for c in range(tm // HD):
            rs = slice(c * HD, (c + 1) * HD)
            for g in range(HEADS):
                cs = slice(g * HD, (g + 1) * HD)
                vblk = vb[rs, cs]
                mixed = _dot(ws_ref[g], vblk) + bsb_ref[g]
                ya_s[rs, cs] = u[rs, cs] * mixed
                du_s[rs, cs] = dya[rs, cs] * mixed
                dmx = dya[rs, cs] * u[rs, cs]
                dbs_acc[g] += dmx
                dmxb = dmx.astype(BF16)
                dws_ref[g] += _dot_nt(dmxb, vblk)
                dv_s[rs, cs] = _dot(wst_ref[g], dmxb)
        dz_ref[:, 4 * D:5 * D] = (dm * ya_s[...] * (sa * (1.0 - sa))).astype(BF16)
        dv = dv_s[...]
        _add_rows128(dlng_ref, jnp.sum(dv * xh, axis=0, keepdims=True))
        _add_rows128(dlnb_ref, jnp.sum(dv, axis=0, keepdims=True))
        dxh = dv * lng_ref[...]
        dgv = rstd * (dxh - jnp.mean(dxh, axis=-1, keepdims=True)
                      - xh * jnp.mean(dxh * xh, axis=-1, keepdims=True))
        dz_ref[:, 0:D] = (du_s[...] * _gelu_grad(zu, tu)).astype(BF16)
        dz_ref[:, D:2 * D] = (dgv * _gelu_grad(zv, tv)).astype(BF16)

        @pl.when(i == nt - 1)
        def _():
            for g in range(HEADS):
                dbs_ref[g:g + 1, :] = jnp.sum(dbs_acc[g].T, axis=0, keepdims=True)

    tile = pl.BlockSpec((tm, D), lambda i: (i, 0))
    wspec = _full((HEADS, HD, HD))
    return pl.pallas_call(
        body, name=f"mixer_bwd_l{layer}", grid=(nt,),
        in_specs=[tile, _full((D, D)), tile, tile]
        + [pl.BlockSpec((tm, D), lambda i, c=c: (i, c)) for c in (0, 1, 3, 4, 5)]
        + [_full((1, D)), _full((1, D)), wspec, wspec, wspec],
        out_specs=[pl.BlockSpec((tm, N_IN), lambda i: (i, 0)), tile, wspec, _full((HEADS, HD)),
                   _full((LANE_ROWS, HD)), _full((LANE_ROWS, HD))],
        out_shape=[SDS((s, N_IN), BF16), SDS((s, D), F32), SDS((HEADS, HD, HD), F32), SDS((HEADS, HD), F32),
                   SDS((LANE_ROWS, HD), F32), SDS((LANE_ROWS, HD), F32)],
        scratch_shapes=[pltpu.VMEM((tm, D), F32)] * 3 + [pltpu.VMEM((HEADS, HD, HD), F32)],
        compiler_params=_cp("arbitrary"))(dx1, wo, h0, h1, z, z, z, z, z, lng, lnb, ws, wst, bsb)


def _lru_gates_bwd(xcb, gates, h0, h1, g0, g1, wr, wi, lam, layer, tm):
    s = xcb.shape[0]
    nt = s // tm

    def body(xc_ref, r0_ref, i0_ref, r1_ref, i1_ref, h0p_ref, h0_ref, h1_ref, h1n_ref, g0_ref, g1_ref,
             wr_ref, wi_ref, lam_ref, dxc_ref, dwr_ref, dwi_ref, dbr_ref, dbi_ref, dlam_ref):
        i = pl.program_id(0)
        fp, fn = _halo_flags(nt)

        @pl.when(i == 0)
        def _():
            for r in (dwr_ref, dwi_ref, dbr_ref, dbi_ref, dlam_ref):
                r[...] = jnp.zeros_like(r)

        xb = xc_ref[...]
        xc = xb.astype(F32)
        zeros8 = jnp.zeros((8, D), F32)
        h_prev = _taps(h0p_ref[...] * fp, h0_ref[...], zeros8, tm)[1]
        h_next = _taps(zeros8, h1_ref[...], h1n_ref[...] * fn, tm)[3]
        dxc = jnp.zeros((tm, D), F32)
        for d, (g_ref, hsh, r_ref, i_ref) in enumerate(((g0_ref, h_prev, r0_ref, i0_ref),
                                                        (g1_ref, h_next, r1_ref, i1_ref))):
            sp = _softplus(-lam_ref[d:d + 1, :])
            r = r_ref[...].astype(F32)
            ig = i_ref[...].astype(F32)
            a, q = _lru_decay(r, sp)
            rmult = jnp.where(q > 0.0, lax.rsqrt(jnp.where(q > 0.0, q, 1.0)), 0.0)
            mult = q * rmult
            db = g_ref[...]
            da = db * hsh
            dmult = db * (ig * xc)
            di = db * (mult * xc)
            dxc = dxc + db * (mult * ig)
            dla = da * a - dmult * (a * a * rmult)
            dsp_dlam = -_sigmoid(-lam_ref[d:d + 1, :])
            _add_rows128(dlam_ref, jnp.sum(dla * r, axis=0, keepdims=True) * ((-LRU_C) * dsp_dlam), d * LANE_ROWS)
            dpr = dla * sp * (-LRU_C) * (r * (1.0 - r))
            dpi = di * (ig * (1.0 - ig))
            _add_rows128(dbr_ref, jnp.sum(dpr, axis=0, keepdims=True), d * LANE_ROWS)
            _add_rows128(dbi_ref, jnp.sum(dpi, axis=0, keepdims=True), d * LANE_ROWS)
            dprb = dpr.astype(BF16)
            dpib = dpi.astype(BF16)
            parts = []
            for h in range(HEADS):
                cs = slice(h * HD, (h + 1) * HD)
                dwr_ref[d, h] += _dot_tn(xb[:, cs], dprb[:, cs])
                dwi_ref[d, h] += _dot_tn(xb[:, cs], dpib[:, cs])
                parts.append(_dot_nt(dprb[:, cs], wr_ref[d, h]) + _dot_nt(dpib[:, cs], wi_ref[d, h]))
            dxc = dxc + jnp.concatenate(parts, axis=1)
        dxc_ref[...] = dxc.astype(BF16)

    tile = pl.BlockSpec((tm, D), lambda i: (i, 0))
    hp, hc, hn = _halo_specs(tm, s, 0)
    wspec = _full((2, HEADS, HD, HD))
    vspec = _full((2 * LANE_ROWS, HD))
    return pl.pallas_call(
        body, name=f"lru_gates_bwd_l{layer}", grid=(nt,),
        in_specs=[tile] * 5 + [hp, hc, hc, hn, tile, tile, wspec, wspec, _full((2, D))],
        out_specs=[tile, wspec, wspec, vspec, vspec, vspec],
        out_shape=[SDS((s, D), BF16), SDS((2, HEADS, HD, HD), F32), SDS((2, HEADS, HD, HD), F32)]
        + [SDS((2 * LANE_ROWS, HD), F32)] * 3,
        compiler_params=_cp("arbitrary"))(xcb, *gates, h0, h0, h1, h1, g0, g1, wr, wi, lam)


def _conv_bwd(dz, dxc, z, cw, layer, tm):
    s = z.shape[0]
    nt = s // tm

    def body(dz_in, dp_ref, dc_ref, dn_ref, zp_ref, zc_ref, zn_ref, cw_ref, dz_ref, dcw_ref, dcb_ref):
        del dz_in
        fp, fn = _halo_flags(nt)

        @pl.when(pl.program_id(0) == 0)
        def _():
            dcw_ref[...] = jnp.zeros_like(dcw_ref)
            dcb_ref[...] = jnp.zeros_like(dcb_ref)

        dxc_halo = _halo_load(dp_ref, dc_ref, dn_ref, fp, fn)
        dxc = dxc_halo[1]
        dm2, dm1, _, dp1, _ = _taps(*dxc_halo, tm)
        dz_ref[...] = (cw_ref[0:1, :] * dp1 + cw_ref[1:2, :] * dxc + cw_ref[2:3, :] * dm1
                       + cw_ref[3:4, :] * dm2).astype(BF16)
        _, zm1, z0, zp1, zp2 = _taps(*_halo_load(zp_ref, zc_ref, zn_ref, fp, fn), tm)
        for k, zt in enumerate((zm1, z0, zp1, zp2)):
            _add_rows128(dcw_ref, jnp.sum(dxc * zt, axis=0, keepdims=True), k * LANE_ROWS)
        _add_rows128(dcb_ref, jnp.sum(dxc, axis=0, keepdims=True))

    return pl.pallas_call(
        body, name=f"conv_bwd_l{layer}", grid=(nt,),
        in_specs=[pl.BlockSpec(memory_space=pl.ANY), *_halo_specs(tm, s, 0, 16), *_halo_specs(tm, s, 2, 16),
                  _full((4, D))],
        out_specs=[pl.BlockSpec((tm, D), lambda i: (i, 2)), _full((4 * LANE_ROWS, HD)), _full((LANE_ROWS, HD))],
        out_shape=[SDS((s, N_IN), BF16), SDS((4 * LANE_ROWS, HD), F32), SDS((LANE_ROWS, HD), F32)],
        input_output_aliases={0: 0},
        compiler_params=_cp("arbitrary"))(dz, dxc, dxc, dxc, z, z, z, cw)


def _me():
    return lax.axis_index("x"), lax.axis_index("y"), lax.axis_index("c")


def _peer(m):
    x, y, c = _me()
    px = 1 - x if m & 4 else x
    py = 1 - y if m & 2 else y
    pc = 1 - c if m & 1 else c
    return (px, py, pc), 4 * px + 2 * py + pc


_ANY = pl.BlockSpec(memory_space=pl.ANY)
_EXCHANGE_SEMS = [pltpu.SemaphoreType.DMA((N_DEV - 1,)), pltpu.SemaphoreType.DMA((N_DEV - 1,)), pltpu.SemaphoreType.DMA(())]


def _all_gather(v, after, name):
    def body(v_ref, after_ref, o_ref, send_sems, recv_sems, local_sem):
        del after_ref
        x, y, c = _me()
        me = 4 * x + 2 * y + c
        local = pltpu.make_async_copy(v_ref, o_ref.at[me], local_sem)
        local.start()
        sends = []
        for m in range(1, N_DEV):
            dev, _ = _peer(m)
            cp = pltpu.make_async_remote_copy(v_ref, o_ref.at[me], send_sems.at[m - 1], recv_sems.at[m - 1],
                                              device_id=dev, device_id_type=pl.DeviceIdType.MESH)
            cp.start()
            sends.append(cp)
        for m in range(1, N_DEV):
            dev, blk = _peer(m)
            pltpu.make_async_remote_copy(v_ref, o_ref.at[blk], send_sems.at[m - 1], recv_sems.at[m - 1],
                                         device_id=dev, device_id_type=pl.DeviceIdType.MESH).wait_recv()
        for cp in sends:
            cp.wait_send()
        local.wait()

    return pl.pallas_call(
        body, name=name, in_specs=[_ANY, _ANY], out_specs=_ANY,
        out_shape=SDS((N_DEV,) + v.shape, v.dtype), scratch_shapes=_EXCHANGE_SEMS)(v, after)


_HBM = pl.BlockSpec(memory_space=pltpu.HBM)
_SEM = pl.BlockSpec(memory_space=pltpu.SEMAPHORE)
_EFFECT = pltpu.CompilerParams(has_side_effects=pltpu.SideEffectType.DATAFLOW_SIDE_EFFECTING)
_PEER_SEMS = pltpu.SemaphoreType.DMA((N_DEV - 1,))


def _in_hbm(a):
    return pltpu.with_memory_space_constraint(a, pltpu.HBM)


def _remote(src, dst, send_sems, recv_sems, m):
    dev, _ = _peer(m)
    return pltpu.make_async_remote_copy(src, dst, send_sems.at[m - 1], recv_sems.at[m - 1],
                                        device_id=dev, device_id_type=pl.DeviceIdType.MESH)


def _gather_start(lands, after, name):
    n = len(lands)

    def body(*refs):
        land = refs[:n]
        sems = refs[n + 1:3 * n + 1]
        token = refs[-1]
        x, y, c = _me()
        me = 4 * x + 2 * y + c
        for t in range(n):
            for m in range(1, N_DEV):
                _remote(land[t].at[me], land[t].at[me], sems[2 * t], sems[2 * t + 1], m).start()
        token[...] = jnp.zeros_like(token)

    res = pl.pallas_call(
        body, name=name, in_specs=[_HBM] * n + [_ANY],
        out_specs=[_SEM] * (2 * n) + [_HBM] * n + [pl.BlockSpec(memory_space=pltpu.VMEM)],
        out_shape=[_PEER_SEMS] * (2 * n) + [pltpu.HBM(a.shape, a.dtype) for a in lands] + [SDS((8, 128), F32)],
        input_output_aliases={t: 2 * n + t for t in range(n)},
        compiler_params=_EFFECT)(*[_in_hbm(a) for a in lands], after)
    return [(res[2 * t], res[2 * t + 1], res[2 * n + t]) for t in range(n)], res[-1]


def _gather_wait(handle, after, name):
    send_sems, recv_sems, land = handle

    def body(land_ref, ssem, rsem, after_ref, out_ref):
        del after_ref, out_ref
        x, y, c = _me()
        me = 4 * x + 2 * y + c
        for m in range(1, N_DEV):
            _, blk = _peer(m)
            cp = _remote(land_ref.at[me], land_ref.at[blk], ssem, rsem, m)
            cp.wait_send()
            cp.wait_recv()

    return pl.pallas_call(
        body, name=name, in_specs=[_HBM, _SEM, _SEM, _ANY], out_specs=_HBM,
        out_shape=pltpu.HBM(land.shape, land.dtype), input_output_aliases={0: 0},
        compiler_params=_EFFECT)(land, send_sems, recv_sems, after)


FIRST_STAGE = (1, 2, 4, 6)
RELAYED = (2, 4, 6)
OTHER_CORE = 1


def _stage_copy(src, dst, send_sems, recv_sems, k, m):
    dev, _ = _peer(m)
    return pltpu.make_async_remote_copy(src, dst, send_sems.at[k], recv_sems.at[k],
                                        device_id=dev, device_id_type=pl.DeviceIdType.MESH)


def _gather2_start(lands, after, name):
    n = len(lands)

    def body(*refs):
        land = refs[:n]
        sems = refs[n + 1:3 * n + 1]
        token = refs[-1]
        x, y, c = _me()
        me = 4 * x + 2 * y + c
        for t in range(n):
            for k, m in enumerate(FIRST_STAGE):
                _stage_copy(land[t].at[me], land[t].at[me], sems[2 * t], sems[2 * t + 1], k, m).start()
        token[...] = jnp.zeros_like(token)

    stage_sems = pltpu.SemaphoreType.DMA((len(FIRST_STAGE),))
    res = pl.pallas_call(
        body, name=name, in_specs=[_HBM] * n + [_ANY],
        out_specs=[_SEM] * (2 * n) + [_HBM] * n + [pl.BlockSpec(memory_space=pltpu.VMEM)],
        out_shape=[stage_sems] * (2 * n) + [pltpu.HBM(a.shape, a.dtype) for a in lands] + [SDS((8, 128), F32)],
        input_output_aliases={t: 2 * n + t for t in range(n)},
        compiler_params=_EFFECT)(*[_in_hbm(a) for a in lands], after)
    return [(res[2 * t], res[2 * t + 1], res[2 * n + t]) for t in range(n)], res[-1]


def _gather2_relay(handles, after, name):
    n = len(handles)

    def body(*refs):
        land, send1, recv1 = refs[:n], refs[n:2 * n], refs[2 * n:3 * n]
        sems = refs[3 * n + 1:5 * n + 1]
        token = refs[-1]
        x, y, c = _me()
        me = 4 * x + 2 * y + c
        for t in range(n):
            for j, m in enumerate(RELAYED):
                _, blk = _peer(m)
                _stage_copy(land[t].at[me], land[t].at[blk], send1[t], recv1[t], 1 + j, m).wait_recv()
                _stage_copy(land[t].at[blk], land[t].at[blk], sems[2 * t], sems[2 * t + 1], j, OTHER_CORE).start()
        token[...] = jnp.zeros_like(token)

    relay_sems = pltpu.SemaphoreType.DMA((len(RELAYED),))
    lands = [h[2] for h in handles]
    res = pl.pallas_call(
        body, name=name, in_specs=[_HBM] * n + [_SEM] * (2 * n) + [_ANY],
        out_specs=[_SEM] * (2 * n) + [_HBM] * n + [pl.BlockSpec(memory_space=pltpu.VMEM)],
        out_shape=[relay_sems] * (2 * n) + [pltpu.HBM(a.shape, a.dtype) for a in lands] + [SDS((8, 128), F32)],
        input_output_aliases={t: 2 * n + t for t in range(n)},
        compiler_params=_EFFECT)(*lands, *[h[0] for h in handles], *[h[1] for h in handles], after)
    return [(h[0], h[1], res[2 * t], res[2 * t + 1], res[2 * n + t]) for t, h in enumerate(handles)], res[-1]


def _gather2_wait(handle, after, name):
    send1, recv1, send2, recv2, land = handle

    def body(land_ref, s1, r1, s2, r2, after_ref, out_ref):
        del after_ref, out_ref
        x, y, c = _me()
        me = 4 * x + 2 * y + c
        _, other = _peer(OTHER_CORE)
        _stage_copy(land_ref.at[me], land_ref.at[other], s1, r1, 0, OTHER_CORE).wait_recv()
        for k, m in enumerate(FIRST_STAGE):
            _stage_copy(land_ref.at[me], land_ref.at[me], s1, r1, k, m).wait_send()
        for j, m in enumerate(RELAYED):
            _, mine = _peer(m)
            _, theirs = _peer(m ^ OTHER_CORE)
            _stage_copy(land_ref.at[mine], land_ref.at[mine], s2, r2, j, OTHER_CORE).wait_send()
            _stage_copy(land_ref.at[mine], land_ref.at[theirs], s2, r2, j, OTHER_CORE).wait_recv()

    return pl.pallas_call(
        body, name=name, in_specs=[_HBM] + [_SEM] * 4 + [_ANY], out_specs=_HBM,
        out_shape=pltpu.HBM(land.shape, land.dtype), input_output_aliases={0: 0},
        compiler_params=_EFFECT)(land, send1, recv1, send2, recv2, after)


def _exchange_start(ps, name):
    n = len(ps)

    def body(*refs):
        p = refs[:n]
        got = refs[n:2 * n]
        sems = refs[2 * n:5 * n]
        token = refs[-1]
        x, y, c = _me()
        me = 4 * x + 2 * y + c
        for t in range(n):
            pltpu.make_async_copy(p[t].at[me], got[t].at[me], sems[3 * t + 2]).start()
            for m in range(1, N_DEV):
                _, blk = _peer(m)
                _remote(p[t].at[blk], got[t].at[me], sems[3 * t], sems[3 * t + 1], m).start()
        token[...] = jnp.zeros_like(token)

    res = pl.pallas_call(
        body, name=name, in_specs=[_HBM] * (2 * n),
        out_specs=[_SEM] * (3 * n) + [_HBM] * (2 * n) + [pl.BlockSpec(memory_space=pltpu.VMEM)],
        out_shape=[_PEER_SEMS, _PEER_SEMS, pltpu.SemaphoreType.DMA(())] * n
        + [pltpu.HBM(a.shape, a.dtype) for a in ps] * 2 + [SDS((8, 128), F32)],
        input_output_aliases={t: 3 * n + t for t in range(2 * n)},
        compiler_params=_EFFECT)(*[_in_hbm(a) for a in ps], *[_in_hbm(lax.empty(a.shape, a.dtype)) for a in ps])
    return [(res[3 * t], res[3 * t + 1], res[3 * t + 2], res[3 * n + t], res[4 * n + t]) for t in range(n)], res[-1]


def _exchange_wait(handle, after, name):
    send_sems, recv_sems, local_sem, p, got = handle

    def body(p_ref, got_ref, ssem, rsem, lsem, after_ref, p_out, got_out):
        del after_ref, p_out, got_out
        x, y, c = _me()
        me = 4 * x + 2 * y + c
        pltpu.make_async_copy(p_ref.at[me], got_ref.at[me], lsem).wait()
        for m in range(1, N_DEV):
            _, blk = _peer(m)
            cp = _remote(p_ref.at[blk], got_ref.at[blk], ssem, rsem, m)
            cp.wait_send()
            cp.wait_recv()

    return pl.pallas_call(
        body, name=name, in_specs=[_HBM, _HBM, _SEM, _SEM, _SEM, _ANY], out_specs=[_HBM, _HBM],
        out_shape=[pltpu.HBM(p.shape, p.dtype), pltpu.HBM(got.shape, got.dtype)],
        input_output_aliases={0: 0, 1: 1}, compiler_params=_EFFECT)(p, got, send_sems, recv_sems, local_sem, after)[1]


def _cast_into_slot(w, layer, me1, name):
    _, r, c = w.shape
    tr = next(t for t in (256, 352, r) if r % t == 0)

    def body(me_ref, w_ref, o_ref):
        del me_ref
        o_ref[...] = w_ref[...].astype(BF16)

    return pl.pallas_call(
        body, name=name,
        grid_spec=pltpu.PrefetchScalarGridSpec(
            num_scalar_prefetch=1, grid=(r // tr,),
            in_specs=[pl.BlockSpec((None, tr, c), lambda i, me: (layer, i, 0))],
            out_specs=pl.BlockSpec((None, tr, c), lambda i, me: (me[0], i, 0))),
        out_shape=SDS((N_DEV, r, c), BF16), compiler_params=_cp("arbitrary"))(me1, w)


def _cast_all_into_slots(ws, layers, me1, after, name):
    n = len(ws)

    def body(me_ref, *refs):
        del me_ref
        for w_ref, o_ref in zip(refs[:n], refs[n + 1:]):
            o_ref[...] = w_ref[...].astype(BF16)

    return pl.pallas_call(
        body, name=name,
        grid_spec=pltpu.PrefetchScalarGridSpec(
            num_scalar_prefetch=1, grid=(1,),
            in_specs=[pl.BlockSpec((None,) + a.shape[1:], lambda i, me, l=l: (l, 0, 0)) for a, l in zip(ws, layers)]
            + [_ANY],
            out_specs=[pl.BlockSpec((None,) + a.shape[1:], lambda i, me: (me[0], 0, 0)) for a in ws]),
        out_shape=[SDS((N_DEV,) + a.shape[1:], BF16) for a in ws],
        compiler_params=_cp("arbitrary"))(me1, *ws, after)


def _sum8_into_slot(p, me1, name):
    _, r, c = p.shape

    def body(me_ref, p_ref, o_ref):
        del me_ref
        acc = p_ref[0]
        for k in range(1, N_DEV):
            acc = acc + p_ref[k]
        o_ref[...] = acc

    return pl.pallas_call(
        body, name=name,
        grid_spec=pltpu.PrefetchScalarGridSpec(
            num_scalar_prefetch=1, grid=(1,),
            in_specs=[pl.BlockSpec(p.shape, lambda i, me: (0, 0, 0))],
            out_specs=pl.BlockSpec((None, r, c), lambda i, me: (me[0], 0, 0))),
        out_shape=SDS(p.shape, F32), compiler_params=_cp("arbitrary"))(me1, p)


def _adamw(w, g, m, v):
    m = ADAM_B1 * m + (1.0 - ADAM_B1) * g
    v = ADAM_B2 * v + (1.0 - ADAM_B2) * (g * g)
    m_hat = m / (1.0 - ADAM_B1 ** ADAM_STEP)
    v_hat = v / (1.0 - ADAM_B2 ** ADAM_STEP)
    delta = -ADAM_LR * (m_hat / (jnp.sqrt(v_hat) + ADAM_EPS) + ADAM_WD * w)
    return delta, m, v


def _adam_shard(parts, w, m, v, layer, prev, name):
    _, r, c = parts.shape
    tr = next(t for t in (256, 352, r) if r % t == 0)
    n_prev = 0 if prev is None else 4

    def body(*refs):
        p_ref, w_ref, m_ref, v_ref = refs[:4]
        g_ref, d_ref, nm_ref, nv_ref = refs[4 + n_prev:]
        g = p_ref[0].astype(F32)
        for k in range(1, N_DEV):
            g = g + p_ref[k].astype(F32)
        delta, nm, nv = _adamw(w_ref[...], g, m_ref[...], v_ref[...])
        g_ref[...] = g
        d_ref[...] = delta
        nm_ref[...] = nm
        nv_ref[...] = nv

    blk = pl.BlockSpec((None, tr, c), lambda i: (layer, i, 0))
    return pl.pallas_call(
        body, name=name, grid=(r // tr,),
        in_specs=[pl.BlockSpec((N_DEV, tr, c), lambda i: (0, i, 0)), blk, blk, blk] + [_ANY] * n_prev,
        out_specs=[blk] * 4, out_shape=[SDS(w.shape, F32)] * 4,
        input_output_aliases={4 + k: k for k in range(n_prev)},
        compiler_params=_cp("parallel"))(parts, w, m, v, *(prev or ()))


SMALL_MATRICES = [("lru_w_r", 2048), ("lru_w_i", 2048), ("gmlp_w_s", 1024)]
SMALL_VECTORS = [("norm1_g", 8), ("gmlp_ln_g", 8), ("gmlp_ln_b", 8), ("gmlp_b_s", 8), ("conv_w", 32), ("conv_b", 8),
                 ("lru_b_r", 16), ("lru_b_i", 16), ("lru_lambda", 16), ("norm2_g", 8), ("final_g", 8)]
SMALL_VECTOR_ROW0 = sum(n for _, n in SMALL_MATRICES)
SMALL_VECTOR_BLOCK = 256
SMALL_ROWS = SMALL_VECTOR_ROW0 + SMALL_VECTOR_BLOCK


def _pack_small(small):
    parts = [small[k] for k, _ in SMALL_MATRICES]
    parts += [small[k] if k in small else jnp.zeros((n, HD), F32) for k, n in SMALL_VECTORS]
    flat = jnp.concatenate(parts)
    return jnp.pad(flat, ((0, SMALL_ROWS - flat.shape[0]), (0, 0))).reshape(N_DEV, SMALL_ROWS // N_DEV, HD)


def _adam_matrix(g0, g1, w, m, v, row0, name):
    _, rows, _ = w.shape

    def body(g0_ref, g1_ref, w_ref, m_ref, v_ref, g_ref, d_ref, nm_ref, nv_ref):
        for l, src in enumerate((g0_ref, g1_ref)):
            g = src[...]
            delta, nm, nv = _adamw(w_ref[l], g, m_ref[l], v_ref[l])
            g_ref[l] = g
            d_ref[l] = delta
            nm_ref[l] = nm
            nv_ref[l] = nv

    gspec = pl.BlockSpec((rows, HD), lambda i: (row0 // rows, 0))
    return pl.pallas_call(body, name=name, grid=(1,), in_specs=[gspec, gspec] + [_full(w.shape)] * 3,
                          out_specs=[_full(w.shape)] * 4, out_shape=[SDS(w.shape, F32)] * 4,
                          compiler_params=_cp("arbitrary"))(g0, g1, w, m, v)


def _adam_vectors(g0, g1, dg1_parts, me1, ws, ms, vs):
    names = [k for k, _ in SMALL_VECTORS]
    n = len(names)

    def lanes(rows8):
        return jnp.concatenate([rows8[k:k + 1, :] for k in range(LANE_ROWS)], axis=1)

    def body(me_ref, g0_ref, g1_ref, dg1_ref, *refs):
        w_refs, m_refs, v_refs = refs[:n], refs[n:2 * n], refs[2 * n:3 * n]
        outs = refs[3 * n:]
        me = me_ref[0]
        g_refs = (g0_ref, g1_ref)

        def emit(i, idx, g):
            delta, nm, nv = _adamw(w_refs[i][idx], g, m_refs[i][idx], v_refs[i][idx])
            for j, val in enumerate((g, delta, nm, nv)):
                outs[4 * i + j][idx] = val

        off = 0
        for i, (name, rows) in enumerate(SMALL_VECTORS):
            for l in range(2):
                row = (slice(l, l + 1), slice(None))
                if name == "final_g":
                    if l == 1:
                        emit(i, (slice(0, 1), slice(None)), lanes(g1_ref[off:off + rows, :]))
                elif name == "norm1_g":
                    if l == 1:
                        emit(i, row, lanes(g0_ref[off:off + rows, :]))
                    else:
                        total = dg1_ref[0]
                        for k in range(1, N_DEV):
                            total = total + dg1_ref[k]
                        emit(i, row, lanes(total))
                elif name == "gmlp_b_s":
                    emit(i, (l,), g_refs[l][off:off + rows, :])
                elif rows == LANE_ROWS:
                    emit(i, row, lanes(g_refs[l][off:off + rows, :]))
                else:
                    for r in range(rows // LANE_ROWS):
                        emit(i, (l, slice(r, r + 1), slice(None)), g_refs[l][pl.ds(off + r * LANE_ROWS + me, 1), :])
            off += rows

    args = [ws[k] for k in names] + [ms[k] for k in names] + [vs[k] for k in names]
    gspec = pl.BlockSpec((SMALL_VECTOR_BLOCK, HD), lambda i, me: (SMALL_VECTOR_ROW0 // SMALL_VECTOR_BLOCK, 0))
    res = pl.pallas_call(
        body, name="adam_vectors",
        grid_spec=pltpu.PrefetchScalarGridSpec(
            num_scalar_prefetch=1, grid=(1,),
            in_specs=[gspec, gspec, _full(dg1_parts.shape)] + [_full(a.shape) for a in args],
            out_specs=[_full(ws[k].shape) for k in names for _ in range(4)]),
        out_shape=[SDS(ws[k].shape, F32) for k in names for _ in range(4)],
        compiler_params=_cp("arbitrary"))(me1, g0, g1, dg1_parts, *args)
    return {k: list(res[4 * i:4 * i + 4]) for i, k in enumerate(names)}


def _after(a, *tokens):
    for token in tokens:
        if token is not None:
            a = a + token[0:1, 0:1]
    return a


def _local_step(x, tgt, p, get_w, hook=lambda stage, layer, payload: None):
    s = x.shape[0]
    tm = _row_tile(s)
    wsb = p["gmlp_w_s"].astype(BF16)
    wstb = jnp.swapaxes(p["gmlp_w_s"], -1, -2).astype(BF16)
    bsb = jnp.broadcast_to(p["gmlp_b_s"][..., None], p["gmlp_w_s"].shape)
    wrb = p["lru_w_r"].astype(BF16)
    wib = p["lru_w_i"].astype(BF16)
    saved = []
    for l in range(2):
        win = get_w("w_in", l, x)
        z, h1 = _norm_inproj(x, _after(p["norm1_g"][l][None], hook("pre_inproj", l, win)), win, l, tm)
        a0, b0, a1, b1, xcb, *gates = _lru_gates_fwd(z, p["conv_w"][l], p["conv_b"][l][None], wrb[l], wib[l],
                                                     p["lru_b_r"][l], p["lru_b_i"][l], p["lru_lambda"][l], l, tm)
        h0, hr = _lru_scan(a0, b0, a1, b1, False, l)
        lng = _after(p["gmlp_ln_g"][l][None], hook("pre_gmlp", l, h0))
        wout = get_w("w_out", l, lng)
        x1, mg = _mixer_fwd(x, h0, hr, z, lng, p["gmlp_ln_b"][l][None], wsb[l], bsb[l], wout, l, tm)
        wfi = get_w("w_ffn_in", l, x1)
        wfo = get_w("w_ffn_out", l, x1)
        if l == 0:
            x2, ff, dff, h2 = _ffn_fwd(x1, p["norm2_g"][l][None], wfi, wfo, l, tm)
        else:
            dx, loss, dfg, ff, dff, h2 = _ffn_fwd(x1, p["norm2_g"][l][None], wfi, wfo, l, tm,
                                                  head=(p["final_g"][None], tgt))
        saved.append((x, z, h1, a0, a1, h0, hr, x1, mg, ff, dff, h2, win, wout, wfi, wfo, xcb, gates))
        x = x2
    pending = None
    for l in (1, 0):
        x0, z, h1, a0, a1, h0, hr, x1, mg, ff, dff, h2, win, wout, wfi, wfo, xcb, gates = saved[l]
        dgu, dx1, dg2 = _ffn_bwd(dx, wfo, dff, wfi.reshape(N_DEV, FF_BLK, D), x1, p["norm2_g"][l][None], l, tm // 2)
        d_wfo = _mm_tn(ff, pl.BlockSpec((None, s, FF_BLK), lambda j: (j, 0, 0)), dx, _resident((s, D)),
                       4, (4, FF_BLK, D), pl.BlockSpec((None, FF_BLK, D), lambda j: (j, 0, 0)),
                       f"dw_ffn_out_l{l}", a_is_transposed=False)
        dgu8 = dgu.reshape(N_DEV, s, FF_BLK)
        d_wfi = _mm_tn(dgu8, pl.BlockSpec((None, s, FF_BLK), lambda j: (j, 0, 0)), h2, _resident((s, D)),
                       N_DEV, (N_DEV, FF_BLK, D), pl.BlockSpec((None, FF_BLK, D), lambda j: (j, 0, 0)),
                       f"dw_ffn_in_l{l}", a_is_transposed=False)
        token = hook("ffn_partials", l, dict(w_ffn_out=d_wfo.reshape(N_DEV, D_FF // N_DEV, D), w_ffn_in=d_wfi))
        pending = hook("mid_backward", l, dx1)
        dz, dh, dws, dbs, dlng, dlnb = _mixer_bwd(dx1, wout, h0, hr, z, _after(p["gmlp_ln_g"][l][None], token),
                                                  p["gmlp_ln_b"][l][None], wsb[l], wstb[l], bsb[l], l, tm)
        d_wout = _mm_tn(mg, _resident((D, s)), dx1, pl.BlockSpec((s, D // 2), lambda j: (0, j)),
                        2, (D, D), pl.BlockSpec((D, D // 2), lambda j: (0, j)), f"dw_out_l{l}")
        g1, g0 = _lru_scan(a1, dh, a0, dh, True, l)
        dxc, dwr, dwi, dbr, dbi, dlam = _lru_gates_bwd(
            xcb, gates, h0, hr, g0, g1, wrb[l], wib[l], _after(p["lru_lambda"][l], pending), l, tm)
        dz, dcw, dcb = _conv_bwd(dz, dxc, z, p["conv_w"][l], l, tm)
        small = dict(lru_w_r=dwr.reshape(-1, HD), lru_w_i=dwi.reshape(-1, HD), gmlp_w_s=dws.reshape(-1, HD),
                     gmlp_ln_g=dlng, gmlp_ln_b=dlnb, gmlp_b_s=dbs, conv_w=dcw, conv_b=dcb, lru_b_r=dbr,
                     lru_b_i=dbi, lru_lambda=dlam, norm2_g=dg2)
        if l == 1:
            small["final_g"] = dfg
        else:
            small["norm1_g"] = dg1
        started = hook("small_grads", l, small)
        d_win = _mm_tn(h1, _resident((D, s)), dz, pl.BlockSpec((s, IN_BLK), lambda j: (0, j)),
                       N_DEV, (N_DEV, D, IN_BLK), pl.BlockSpec((None, D, IN_BLK), lambda j: (j, 0, 0)),
                       f"dw_in_l{l}", after=started)
        token = hook("mixer_partials", l, dict(w_out=d_wout.reshape(N_DEV, D // N_DEV, D), w_in=d_win))
        dx, dg1 = _mm_nt_rms_bwd(
            dz, pl.BlockSpec((tm, N_IN), lambda i: (i, 0)),
            lambda r: [r[:, k * IN_BLK:(k + 1) * IN_BLK] for k in range(N_DEV)],
            win, False, x0, _after(p["norm1_g"][l][None], token, started, pending), dx1, f"inproj_bwd_dx_l{l}", tm)
        pending = None
    return loss, dx, dg1


_REPL = ["norm1_g", "gmlp_ln_g", "gmlp_ln_b", "gmlp_w_s", "gmlp_b_s", "conv_b", "lru_w_r", "lru_w_i", "norm2_g", "final_g"]
_LANE_SHARDED = ["conv_w", "lru_b_r", "lru_b_i", "lru_lambda"]
_BIG = ["w_in", "w_out", "w_ffn_in", "w_ffn_out"]
_ORDER = ["norm1_g", "w_in", "gmlp_ln_g", "gmlp_ln_b", "gmlp_w_s", "gmlp_b_s", "conv_w", "conv_b", "lru_w_r", "lru_b_r",
          "lru_w_i", "lru_b_i", "lru_lambda", "w_out", "norm2_g", "w_ffn_in", "w_ffn_out", "final_g"]


def kernel(x, norm1_g, w_in, gmlp_ln_g, gmlp_ln_b, gmlp_w_s, gmlp_b_s, conv_w, conv_b, lru_w_r, lru_b_r, lru_w_i, lru_b_i, lru_lambda, w_out, norm2_g, w_ffn_in, w_ffn_out, final_g, loss_target, m_norm1_g, m_w_in, m_gmlp_ln_g, m_gmlp_ln_b, m_gmlp_w_s, m_gmlp_b_s, m_conv_w, m_conv_b, m_lru_w_r, m_lru_b_r, m_lru_w_i, m_lru_b_i, m_lru_lambda, m_w_out, m_norm2_g, m_w_ffn_in, m_w_ffn_out, m_final_g, v_norm1_g, v_w_in, v_gmlp_ln_g, v_gmlp_ln_b, v_gmlp_w_s, v_gmlp_b_s, v_conv_w, v_conv_b, v_lru_w_r, v_lru_b_r, v_lru_w_i, v_lru_b_i, v_lru_lambda, v_w_out, v_norm2_g, v_w_ffn_in, v_w_ffn_out, v_final_g):
    w = dict(norm1_g=norm1_g, w_in=w_in, gmlp_ln_g=gmlp_ln_g, gmlp_ln_b=gmlp_ln_b, gmlp_w_s=gmlp_w_s, gmlp_b_s=gmlp_b_s,
             conv_w=conv_w, conv_b=conv_b, lru_w_r=lru_w_r, lru_b_r=lru_b_r, lru_w_i=lru_w_i, lru_b_i=lru_b_i,
             lru_lambda=lru_lambda, w_out=w_out, norm2_g=norm2_g, w_ffn_in=w_ffn_in, w_ffn_out=w_ffn_out, final_g=final_g)
    mom = dict(norm1_g=m_norm1_g, w_in=m_w_in, gmlp_ln_g=m_gmlp_ln_g, gmlp_ln_b=m_gmlp_ln_b, gmlp_w_s=m_gmlp_w_s,
               gmlp_b_s=m_gmlp_b_s, conv_w=m_conv_w, conv_b=m_conv_b, lru_w_r=m_lru_w_r, lru_b_r=m_lru_b_r,
               lru_w_i=m_lru_w_i, lru_b_i=m_lru_b_i, lru_lambda=m_lru_lambda, w_out=m_w_out, norm2_g=m_norm2_g,
               w_ffn_in=m_w_ffn_in, w_ffn_out=m_w_ffn_out, final_g=m_final_g)
    var = dict(norm1_g=v_norm1_g, w_in=v_w_in, gmlp_ln_g=v_gmlp_ln_g, gmlp_ln_b=v_gmlp_ln_b, gmlp_w_s=v_gmlp_w_s,
               gmlp_b_s=v_gmlp_b_s, conv_w=v_conv_w, conv_b=v_conv_b, lru_w_r=v_lru_w_r, lru_b_r=v_lru_b_r,
               lru_w_i=v_lru_w_i, lru_b_i=v_lru_b_i, lru_lambda=v_lru_lambda, w_out=v_w_out, norm2_g=v_norm2_g,
               w_ffn_in=v_w_ffn_in, w_ffn_out=v_w_ffn_out, final_g=v_final_g)
    for src in (w, mom, var):
        src["w_ffn_in"] = jnp.swapaxes(src["w_ffn_in"], 1, 2)
    xi, yi, ci = _me()
    me = 4 * xi + 2 * yi + ci

    lane_shapes = [w[k].shape for k in _LANE_SHARDED]
    lane_rows = sum(a[0] * a[1] for a in lane_shapes)
    packed = jnp.concatenate([w[k].reshape(-1, HD) for k in _LANE_SHARDED])
    packed = jnp.pad(packed, ((0, -lane_rows % 8), (0, 0)))

    me1 = jnp.reshape(me, (1,)).astype(jnp.int32)
    gathers = {}
    exchanges = {}
    views = dict(w_in=(N_DEV, D, IN_BLK), w_out=(D, D), w_ffn_in=(2, 4, FF_BLK, D), w_ffn_out=(4, FF_BLK, D))
    small_ex = {}
    small_ag = {}

    casts = {}

    def start_gather(names, l, after):
        lands = [casts[(k, l)] if (k, l) in casts else _cast_into_slot(w[k], l, me1, f"cast_{k}_l{l}") for k in names]
        started, tok = _gather2_start(lands, after, f"gather_start_{'_'.join(names)}_l{l}")
        gathers.update({(k, l): h for k, h in zip(names, started)})
        return tok

    def relay_gather(names, l, after):
        relayed, tok = _gather2_relay([gathers[(k, l)] for k in names], after, f"gather_relay_{'_'.join(names)}_l{l}")
        gathers.update({(k, l): h for k, h in zip(names, relayed)})
        return tok

    def get_w(k, l, after):
        if (k, l) == ("w_in", 1):
            after = relay_gather(_BIG[:1], l, after)
        return _gather2_wait(gathers[(k, l)], after, f"gather_wait_{k}_l{l}").reshape(views[k])

    def hook(stage, l, payload):
        if stage == "pre_inproj":
            return start_gather(_BIG[1:], l, payload)
        if stage == "pre_gmlp":
            tok = relay_gather(_BIG[1:], l, payload)
            return tok + start_gather(_BIG[:1], l + 1, tok) if l == 0 else tok
        if stage == "small_grads":
            (small_ex[l],), tok = _exchange_start([_pack_small(payload)], f"exchange_start_small_l{l}")
            return tok
        if stage == "mid_backward":
            return reduce_small(l + 1, payload) if l == 0 else None
        extra = reduce_small(0, payload["w_in"]) if (stage, l) == ("mixer_partials", 0) else None
        started, tok = _exchange_start(list(payload.values()), f"exchange_start_{'_'.join(payload)}_l{l}")
        exchanges.update({(k, l): h for k, h in zip(payload, started)})
        return tok if extra is None else tok + extra

    def reduce_small(l, after):
        got = _exchange_wait(small_ex[l], after, f"exchange_wait_small_l{l}")
        mine = _sum8_into_slot(got, me1, f"sum_small_l{l}")
        (small_ag[l],), tok = _gather_start([mine], got, f"gather_start_small_l{l}")
        return tok

    land = lax.dynamic_update_slice(jnp.zeros((N_DEV,) + packed.shape, F32), packed[None], (me, 0, 0))
    (lanes_handle,), token = _gather_start([land], packed, "gather_start_lanes")
    token = start_gather(_BIG[:1], 0, token)
    later = [(k, l) for l in range(2) for k in _BIG if (k, l) != ("w_in", 0)]
    casts.update(zip(later, _cast_all_into_slots([w[k] for k, _ in later], [l for _, l in later], me1, token,
                                                 "cast_later_weights")))
    token = relay_gather(_BIG[:1], 0, casts[later[0]])
    lanes = _gather_wait(lanes_handle, token, "gather_wait_lanes")
    params = {k: w[k] for k in _REPL}
    off = 0
    for k, shp in zip(_LANE_SHARDED, lane_shapes):
        n = shp[0] * shp[1]
        params[k] = jnp.swapaxes(lanes[:, off:off + n], 0, 1).reshape(shp[0], shp[1], D)
        off += n
    loss, dx, dg1 = _local_step(x[0], loss_target[0], params, get_w, hook)

    out = {}
    after = dx
    for k, l in [(k, l) for k in ("w_ffn_out", "w_ffn_in") for l in (1, 0)] + [("w_out", 1), ("w_in", 1)]:
        got = _exchange_wait(exchanges[(k, l)], after, f"exchange_wait_{k}_l{l}")
        out[k] = _adam_shard(got, w[k], mom[k], var[k], l, out.get(k), f"adam_{k}_l{l}")
        after = out[k][3]
    g_small = [_gather_wait(small_ag[l], after, f"gather_wait_small_l{l}").reshape(SMALL_ROWS, HD) for l in (0, 1)]
    row0 = 0
    for k, rows in SMALL_MATRICES:
        res = _adam_matrix(*g_small, *[src[k].reshape(2, rows, HD) for src in (w, mom, var)], row0, f"adam_{k}")
        out[k] = [a.reshape(w[k].shape) for a in res]
        after = res[3]
        row0 += rows
    for k in ("w_out", "w_in"):
        got = _exchange_wait(exchanges[(k, 0)], after, f"exchange_wait_{k}_l0")
        out[k] = _adam_shard(got, w[k], mom[k], var[k], 0, out[k], f"adam_{k}_l0")
    out["w_ffn_in"] = [jnp.swapaxes(a, 1, 2) for a in out["w_ffn_in"]]
    as_rows = lambda a: a.reshape(1, D) if a.ndim == 1 else a
    vec = _adam_vectors(*g_small, _all_gather(dg1, out["w_in"][3], "gather_norm1_grad"), me1,
                        *[{k: as_rows(src[k]) for k, _ in SMALL_VECTORS} for src in (w, mom, var)])
    out.update({k: [a.reshape(w[k].shape) for a in res] for k, res in vec.items()})

    loss = lax.psum(loss[0, 0], MESH_AXES)
    return (loss, dx[None], *[out[k][0] for k in _ORDER], *[out[k][1] for k in _ORDER],
            *[out[k][2] for k in _ORDER], *[out[k][3] for k in _ORDER])
```

```python
import jax
import jax.numpy as jnp
from jax import lax
from jax.experimental import pallas as pl
from jax.experimental.pallas import tpu as pltpu

F32 = jnp.float32
BF16 = jnp.bfloat16
SDS = jax.ShapeDtypeStruct

D = 1024
N_IN = 6 * D
D_FF = 2816
N_DEV = 8
IN_BLK = N_IN // N_DEV
FF_BLK = 2 * D_FF // N_DEV
HEADS = 8
HD = 128
EPS = 1e-6
LRU_C = 8.0
MESH_AXES = ("x", "y", "c")

ADAM_LR = 0.001
ADAM_B1 = 0.9
ADAM_B2 = 0.999
ADAM_EPS = 1e-08
ADAM_WD = 0.01
ADAM_STEP = 10

VMEM_LIMIT = 60 * 2**20


def _cp(*sem, **kw):
    return pltpu.CompilerParams(dimension_semantics=sem, vmem_limit_bytes=VMEM_LIMIT, **kw)


def _row_tile(s):
    return 512 if s >= 1024 else s // 2


_GELU_C = 0.7978845608028654


def _gelu(x):
    t = jnp.tanh(_GELU_C * (x + 0.044715 * (x * x * x)))
    return 0.5 * x * (1.0 + t), t


def _gelu_grad(x, t):
    return 0.5 * (1.0 + t) + 0.5 * x * (1.0 - t * t) * (_GELU_C * (1.0 + 0.134145 * (x * x)))


def _sigmoid(x):
    return 0.5 + 0.5 * jnp.tanh(0.5 * x)


def _softplus(x):
    e = jnp.exp(-jnp.abs(x))
    w = 1.0 + e
    l1p = jnp.where(w == 1.0, e, jnp.log(w) * e / jnp.where(w == 1.0, 1.0, w - 1.0))
    return jnp.maximum(x, 0.0) + l1p


def _rms_fwd(x, g):
    r = lax.rsqrt(jnp.mean(x * x, axis=-1, keepdims=True) + EPS)
    return x * r * g


def _rms_bwd(x, g, dh):
    r = lax.rsqrt(jnp.mean(x * x, axis=-1, keepdims=True) + EPS)
    xh = x * r
    dxh = dh * g
    dx = r * (dxh - xh * jnp.mean(dxh * xh, axis=-1, keepdims=True))
    dg = jnp.sum(dh * xh, axis=0, keepdims=True)
    return dx, dg


LANE_ROWS = D // HD


def _add_rows128(ref, vec, row0=0):
    for i in range(vec.shape[0]):
        for k in range(LANE_ROWS):
            j = row0 + i * LANE_ROWS + k
            ref[j:j + 1, :] += vec[i:i + 1, k * HD:(k + 1) * HD]


def _dot(a, b):
    return jnp.dot(a, b, preferred_element_type=F32)


def _dot_nt(a, b):
    return lax.dot_general(a, b, (((1,), (1,)), ((), ())), preferred_element_type=F32)


def _dot_tn(a, b):
    return lax.dot_general(a, b, (((0,), (0,)), ((), ())), preferred_element_type=F32)


def _taps(prev, cur, nxt, tm):
    hr = prev.shape[0]
    ext = jnp.concatenate([prev, cur, nxt], axis=0)
    n = tm + 2 * hr
    sl = slice(hr, hr + tm)
    return (pltpu.roll(ext, 2, 0)[sl], pltpu.roll(ext, 1, 0)[sl], cur,
            pltpu.roll(ext, n - 1, 0)[sl], pltpu.roll(ext, n - 2, 0)[sl])


def _halo_specs(tm, s, col, rows=8):
    nb = s // rows
    r = tm // rows
    return (pl.BlockSpec((rows, D), lambda i: (jnp.maximum(i * r - 1, 0), col)),
            pl.BlockSpec((tm, D), lambda i: (i, col)),
            pl.BlockSpec((rows, D), lambda i: (jnp.minimum((i + 1) * r, nb - 1), col)))


def _halo_load(prev_ref, cur_ref, next_ref, fp, fn):
    return prev_ref[...].astype(F32) * fp, cur_ref[...].astype(F32), next_ref[...].astype(F32) * fn


def _halo_flags(nt):
    i = pl.program_id(0)
    return (i > 0).astype(F32), (i < nt - 1).astype(F32)


def _full(shape):
    nd = len(shape)
    return pl.BlockSpec(shape, lambda *_: (0,) * nd)


def _resident(shape):
    nd = len(shape)
    return pl.BlockSpec(shape, lambda *_: (0,) * nd, pipeline_mode=pl.Buffered(1))


def _norm_inproj(x, g, w, layer, tm):
    s = x.shape[0]

    def body(x_ref, g_ref, w_ref, z_ref, ht_ref):
        h32 = _rms_fwd(x_ref[...], g_ref[...])
        ht_ref[...] = h32.T.astype(BF16)
        h = h32.astype(BF16)
        for j in range(N_DEV):
            z_ref[:, j * IN_BLK:(j + 1) * IN_BLK] = _dot(h, w_ref[j]).astype(BF16)

    return pl.pallas_call(
        body, name=f"norm_inproj_l{layer}", grid=(s // tm,),
        in_specs=[pl.BlockSpec((tm, D), lambda i: (i, 0)), _full((1, D)), _resident((N_DEV, D, IN_BLK))],
        out_specs=[pl.BlockSpec((tm, N_IN), lambda i: (i, 0)), pl.BlockSpec((D, tm), lambda i: (0, i))],
        out_shape=[SDS((s, N_IN), BF16), SDS((D, s), BF16)],
        compiler_params=_cp("parallel"))(x, g, w)


def _gmlp_values(zu_ref, zv_ref, lng_ref, lnb_ref):
    zu = zu_ref[...].astype(F32)
    zv = zv_ref[...].astype(F32)
    u, tu = _gelu(zu)
    gv, tv = _gelu(zv)
    xc = gv - jnp.mean(gv, axis=-1, keepdims=True)
    rstd = lax.rsqrt(jnp.mean(xc * xc, axis=-1, keepdims=True) + EPS)
    xh = xc * rstd
    vb = (xh * lng_ref[...] + lnb_ref[...]).astype(BF16)
    return zu, zv, u, tu, tv, xh, rstd, vb


def _mixer_fwd(x, h0, h1, z, lng, lnb, ws, bsb, wo, layer, tm):
    s = x.shape[0]

    def body(x_ref, h0_ref, h1_ref, zu_ref, zv_ref, zg_ref, za_ref, zb_ref, lng_ref, lnb_ref, ws_ref, bsb_ref,
             wo_ref, x1_ref, mg_ref, ya_s):
        _, _, u, _, _, _, _, vb = _gmlp_values(zu_ref, zv_ref, lng_ref, lnb_ref)
        for c in range(tm // HD):
            rs = slice(c * HD, (c + 1) * HD)
            for g in range(HEADS):
                cs = slice(g * HD, (g + 1) * HD)
                ya_s[rs, cs] = u[rs, cs] * (_dot(ws_ref[g], vb[rs, cs]) + bsb_ref[g])
        gg, _ = _gelu(zg_ref[...].astype(F32))
        yb = (h0_ref[...] + h1_ref[...]) * gg
        m32 = _sigmoid(za_ref[...].astype(F32)) * ya_s[...] + _sigmoid(zb_ref[...].astype(F32)) * yb
        mg_ref[...] = m32.T.astype(BF16)
        x1_ref[...] = x_ref[...] + _dot(m32.astype(BF16), wo_ref[...])

    tile = pl.BlockSpec((tm, D), lambda i: (i, 0))
    wspec = _full((HEADS, HD, HD))
    return pl.pallas_call(
        body, name=f"mixer_fwd_l{layer}", grid=(s // tm,),
        in_specs=[tile, tile, tile] + [pl.BlockSpec((tm, D), lambda i, c=c: (i, c)) for c in (0, 1, 3, 4, 5)]
        + [_full((1, D)), _full((1, D)), wspec, wspec, _full((D, D))],
        out_specs=[tile, pl.BlockSpec((D, tm), lambda i: (0, i))], out_shape=[SDS((s, D), F32), SDS((D, s), BF16)],
        scratch_shapes=[pltpu.VMEM((tm, D), F32)],
        compiler_params=_cp("parallel"))(x, h0, h1, z, z, z, z, z, lng, lnb, ws, bsb, wo)


def _conv(taps, cw_ref, cb_ref):
    _, m1, c0, p1, p2 = taps
    return cb_ref[...] + m1 * cw_ref[0:1, :] + c0 * cw_ref[1:2, :] + p1 * cw_ref[2:3, :] + p2 * cw_ref[3:4, :]


def _heads_dot(xb, w_ref, d):
    return jnp.concatenate([_dot(xb[:, h * HD:(h + 1) * HD], w_ref[d, h]) for h in range(HEADS)], axis=1)


def _lru_decay(r, sp):
    la = (-LRU_C) * r * sp
    a = jnp.exp(la)
    return a, jnp.tanh(-la) * (a * a + 1.0)


def _lru_gates_fwd(z, cw, cb, wr, wi, br, bi, lam, layer, tm):
    s = z.shape[0]
    nt = s // tm

    def body(zp_ref, zc_ref, zn_ref, cw_ref, cb_ref, wr_ref, wi_ref, br_ref, bi_ref, lam_ref,
             a0_ref, b0_ref, a1_ref, b1_ref, xc_ref, r0_ref, i0_ref, r1_ref, i1_ref):
        fp, fn = _halo_flags(nt)
        xc = _conv(_taps(*_halo_load(zp_ref, zc_ref, zn_ref, fp, fn), tm), cw_ref, cb_ref)
        xb = xc.astype(BF16)
        xc_ref[...] = xb
        for d, (a_ref, b_ref, r_ref, i_ref) in enumerate(((a0_ref, b0_ref, r0_ref, i0_ref),
                                                          (a1_ref, b1_ref, r1_ref, i1_ref))):
            r = _sigmoid(_heads_dot(xb, wr_ref, d) + br_ref[d:d + 1, :])
            ig = _sigmoid(_heads_dot(xb, wi_ref, d) + bi_ref[d:d + 1, :])
            a, q = _lru_decay(r, _softplus(-lam_ref[d:d + 1, :]))
            a_ref[...] = a
            b_ref[...] = jnp.sqrt(q) * (ig * xc)
            r_ref[...] = r.astype(BF16)
            i_ref[...] = ig.astype(BF16)

    tile = pl.BlockSpec((tm, D), lambda i: (i, 0))
    return pl.pallas_call(
        body, name=f"lru_gates_fwd_l{layer}", grid=(nt,),
        in_specs=[*_halo_specs(tm, s, 2, 16), _full((4, D)), _full((1, D)),
                  _full((2, HEADS, HD, HD)), _full((2, HEADS, HD, HD)), _full((2, D)), _full((2, D)), _full((2, D))],
        out_specs=[tile] * 9, out_shape=[SDS((s, D), F32)] * 4 + [SDS((s, D), BF16)] * 5,
        compiler_params=_cp("parallel"))(z, z, z, cw, cb, wr, wi, br, bi, lam)


def _scan_group(a, x, c, reverse, bwd):
    row = lax.broadcasted_iota(jnp.int32, a.shape, 0)
    b = a * x if bwd else x
    for d in (1, 2, 4):
        keep = (row < 8 - d) if reverse else (row >= d)
        sh = 8 - d if reverse else d
        a_s = jnp.where(keep, pltpu.roll(a, sh, 0), 1.0)
        b_s = jnp.where(keep, pltpu.roll(b, sh, 0), 0.0)
        b = a * b_s + b
        a = a * a_s
    h = b + a * c
    new_c = h[0:1, :] if reverse else h[7:8, :]
    if not bwd:
        return h, new_c
    if reverse:
        prev = jnp.where(row < 7, pltpu.roll(h, 7, 0), c)
    else:
        prev = jnp.where(row >= 1, pltpu.roll(h, 1, 0), c)
    return x + prev, new_c


def _lru_scan(a_f, x_f, a_r, x_r, bwd, layer):
    s = a_f.shape[0]
    ts = min(1024, s // 2)
    cb = 512
    nt = s // ts
    ng = ts // 8

    def body(af_ref, xf_ref, ar_ref, xr_ref, of_ref, or_ref, cf, cr):
        @pl.when(pl.program_id(1) == 0)
        def _():
            cf[...] = jnp.zeros_like(cf)
            cr[...] = jnp.zeros_like(cr)

        def step(j, carry):
            c_f, c_r = carry
            rf = pl.multiple_of(j * 8, 8)
            rr = pl.multiple_of((ng - 1 - j) * 8, 8)
            o, c_f = _scan_group(af_ref[pl.ds(rf, 8), :], xf_ref[pl.ds(rf, 8), :], c_f, False, bwd)
            of_ref[pl.ds(rf, 8), :] = o
            o, c_r = _scan_group(ar_ref[pl.ds(rr, 8), :], xr_ref[pl.ds(rr, 8), :], c_r, True, bwd)
            or_ref[pl.ds(rr, 8), :] = o
            return c_f, c_r

        c_f, c_r = lax.fori_loop(0, ng, step, (cf[0:1, :], cr[0:1, :]), unroll=2)
        cf[...] = jnp.broadcast_to(c_f, cf.shape)
        cr[...] = jnp.broadcast_to(c_r, cr.shape)

    fwd = pl.BlockSpec((ts, cb), lambda c, t: (t, c))
    rev = pl.BlockSpec((ts, cb), lambda c, t: (nt - 1 - t, c))
    return pl.pallas_call(
        body, name=f"lru_scan_{'bwd' if bwd else 'fwd'}_l{layer}", grid=(D // cb, nt),
        in_specs=[fwd, fwd, rev, rev], out_specs=[fwd, rev],
        out_shape=[SDS((s, D), F32)] * 2,
        scratch_shapes=[pltpu.VMEM((8, cb), F32), pltpu.VMEM((8, cb), F32)],
        compiler_params=_cp("parallel", "arbitrary"))(a_f, x_f, a_r, x_r)


def _ffn_fwd(x1, g, wfi, wfo, layer, tm, head=None):
    s = x1.shape[0]

    def ffn(x_ref, g_ref, wi_ref, wo_ref, ff_ref, dff_ref, h_ref):
        x = x_ref[...]
        h = _rms_fwd(x, g_ref[...]).astype(BF16)
        h_ref[...] = h
        acc = x
        for k in range(4):
            gate = _dot_nt(h, wi_ref[0, k])
            up = _dot_nt(h, wi_ref[1, k])
            sg = _sigmoid(gate)
            silu = gate * sg
            ff = (silu * up).astype(BF16)
            ff_ref[k] = ff
            dff_ref[0, k] = (up * (sg * (1.0 + gate * (1.0 - sg)))).astype(BF16)
            dff_ref[1, k] = silu.astype(BF16)
            acc = acc + _dot(ff, wo_ref[k])
        return acc

    def body(x_ref, g_ref, wi_ref, wo_ref, x2_ref, ff_ref, dff_ref, h_ref):
        x2_ref[...] = ffn(x_ref, g_ref, wi_ref, wo_ref, ff_ref, dff_ref, h_ref)

    def body_with_head(x_ref, g_ref, wi_ref, wo_ref, fg_ref, t_ref, dx_ref, loss_ref, dfg_ref, ff_ref, dff_ref, h_ref):
        @pl.when(pl.program_id(0) == 0)
        def _():
            loss_ref[...] = jnp.zeros_like(loss_ref)
            dfg_ref[...] = jnp.zeros_like(dfg_ref)

        x2 = ffn(x_ref, g_ref, wi_ref, wo_ref, ff_ref, dff_ref, h_ref)
        fg = fg_ref[...]
        e = _rms_fwd(x2, fg) - t_ref[...]
        rows = jnp.sum(e * e, axis=-1, keepdims=True)
        loss_ref[...] += (0.5 / D) * jnp.sum(rows, axis=0, keepdims=True)
        dx, dg = _rms_bwd(x2, fg, e * (1.0 / D))
        dx_ref[...] = dx
        _add_rows128(dfg_ref, dg)

    tile = pl.BlockSpec((tm, D), lambda i: (i, 0))
    weights = [_resident((2, 4, FF_BLK, D)), _resident((4, FF_BLK, D))]
    kept_specs = [pl.BlockSpec((4, tm, FF_BLK), lambda i: (0, i, 0)),
                  pl.BlockSpec((2, 4, tm, FF_BLK), lambda i: (0, 0, i, 0)), tile]
    kept_shapes = [SDS((4, s, FF_BLK), BF16), SDS((2, 4, s, FF_BLK), BF16), SDS((s, D), BF16)]
    if head is None:
        return pl.pallas_call(
            body, name=f"ffn_fwd_l{layer}", grid=(s // tm,),
            in_specs=[tile, _full((1, D))] + weights, out_specs=[tile] + kept_specs,
            out_shape=[SDS((s, D), F32)] + kept_shapes, compiler_params=_cp("parallel"))(x1, g, wfi, wfo)
    final_g, tgt = head
    return pl.pallas_call(
        body_with_head, name=f"ffn_fwd_loss_l{layer}", grid=(s // tm,),
        in_specs=[tile, _full((1, D))] + weights + [_full((1, D)), tile],
        out_specs=[tile, _full((1, 1)), _full((LANE_ROWS, HD))] + kept_specs,
        out_shape=[SDS((s, D), F32), SDS((1, 1), F32), SDS((LANE_ROWS, HD), F32)] + kept_shapes,
        compiler_params=_cp("arbitrary"))(x1, g, wfi, wfo, final_g, tgt)


def _ffn_bwd(dx2, wfo, factors, wfi, x1, g, layer, tm):
    s = dx2.shape[0]

    def body(dx_ref, wo_ref, f_ref, wi_ref, x_ref, g_ref, dgu_ref, dx1_ref, dg_ref):
        @pl.when(pl.program_id(0) == 0)
        def _():
            dg_ref[...] = jnp.zeros_like(dg_ref)

        dx = dx_ref[...]
        dxb = dx.astype(BF16)
        dh = None
        for k in range(4):
            dff = _dot_nt(dxb, wo_ref[k])
            d_gate = (dff * f_ref[0, k].astype(F32)).astype(BF16)
            d_up = (dff * f_ref[1, k].astype(F32)).astype(BF16)
            dgu_ref[0, k] = d_gate
            dgu_ref[1, k] = d_up
            part = _dot(d_gate, wi_ref[k]) + _dot(d_up, wi_ref[4 + k])
            dh = part if dh is None else dh + part
        dxn, dg = _rms_bwd(x_ref[...], g_ref[...], dh)
        dx1_ref[...] = dx + dxn
        _add_rows128(dg_ref, dg)

    tile = pl.BlockSpec((tm, D), lambda i: (i, 0))
    blk = pl.BlockSpec((2, 4, tm, FF_BLK), lambda i: (0, 0, i, 0))
    return pl.pallas_call(
        body, name=f"ffn_bwd_l{layer}", grid=(s // tm,),
        in_specs=[tile, _resident((4, FF_BLK, D)), blk, _resident((N_DEV, FF_BLK, D)), tile, _full((1, D))],
        out_specs=[blk, tile, _full((LANE_ROWS, HD))],
        out_shape=[SDS((2, 4, s, FF_BLK), BF16), SDS((s, D), F32), SDS((LANE_ROWS, HD), F32)],
        compiler_params=_cp("arbitrary"))(dx2, wfo, factors, wfi, x1, g)


def _mm_nt_rms_bwd(a, a_spec, a_blocks, w, w_is_transposed, x, g, dres, name, tm):
    s = x.shape[0]

    def body(a_ref, w_ref, x_ref, g_ref, dres_ref, dx_ref, dg_ref):
        @pl.when(pl.program_id(0) == 0)
        def _():
            dg_ref[...] = jnp.zeros_like(dg_ref)

        dh = None
        for k, blk in enumerate(a_blocks(a_ref)):
            part = _dot(blk, w_ref[k]) if w_is_transposed else _dot_nt(blk, w_ref[k])
            dh = part if dh is None else dh + part
        dx, dg = _rms_bwd(x_ref[...], g_ref[...], dh)
        dx_ref[...] = dres_ref[...] + dx
        _add_rows128(dg_ref, dg)

    tile = pl.BlockSpec((tm, D), lambda i: (i, 0))
    return pl.pallas_call(
        body, name=name, grid=(s // tm,),
        in_specs=[a_spec, _resident(w.shape), tile, _full((1, D)), tile],
        out_specs=[tile, _full((LANE_ROWS, HD))], out_shape=[SDS((s, D), F32), SDS((LANE_ROWS, HD), F32)],
        compiler_params=_cp("arbitrary"))(a, w, x, g, dres)


def _mm_tn(a, a_spec, b, b_spec, nb, out_shape, out_spec, name, a_is_transposed=True, after=None):
    def body(a_ref, b_ref, *rest):
        o_ref = rest[-1]
        bb = b_ref[...].astype(BF16)
        o_ref[...] = (_dot(a_ref[...], bb) if a_is_transposed else _dot_tn(a_ref[...], bb)).astype(BF16)

    deps = [] if after is None else [after]
    return pl.pallas_call(
        body, name=name, grid=(nb,), in_specs=[a_spec, b_spec] + [_ANY] * len(deps), out_specs=out_spec,
        out_shape=SDS(out_shape, BF16), compiler_params=_cp("parallel"))(a, b, *deps)


def _mixer_bwd(dx1, wo, h0, h1, z, lng, lnb, ws, wst, bsb, layer, tm):
    s = dx1.shape[0]
    nt = s // tm

    def body(dx_ref, wo_ref, h0_ref, h1_ref, zu_ref, zv_ref, zg_ref, za_ref, zb_ref, lng_ref, lnb_ref,
             ws_ref, wst_ref, bsb_ref, dz_ref, dh_ref, dws_ref, dbs_ref, dlng_ref, dlnb_ref,
             du_s, dv_s, ya_s, dbs_acc):
        i = pl.program_id(0)

        @pl.when(i == 0)
        def _():
            for r in (dws_ref, dlng_ref, dlnb_ref, dbs_acc):
                r[...] = jnp.zeros_like(r)

        dm = _dot_nt(dx_ref[...].astype(BF16), wo_ref[...])
        sa = _sigmoid(za_ref[...].astype(F32))
        sb = _sigmoid(zb_ref[...].astype(F32))
        zg = zg_ref[...].astype(F32)
        gg, tg = _gelu(zg)
        hs = h0_ref[...] + h1_ref[...]
        dyb = dm * sb
        dya = dm * sa
        dh_ref[...] = dyb * gg
        dz_ref[:, 2 * D:3 * D] = jnp.zeros((tm, D), BF16)
        dz_ref[:, 3 * D:4 * D] = (dyb * hs * _gelu_grad(zg, tg)).astype(BF16)
        dz_ref[:, 5 * D:6 * D] = (dm * (hs * gg) * (sb * (1.0 - sb))).astype(BF16)

        zu, zv, u, tu, tv, xh, rstd, vb = _gmlp_values(zu_ref, zv_ref, lng_ref, lnb_ref)
        for c in range(tm // HD):
            rs = slice(c * HD, (c + 1) * HD)
            for g in range(HEADS):
                cs = slice(g * HD, (g + 1) * HD)
                vblk = vb[rs, cs]
                mixed = _dot(ws_ref[g], vblk) + bsb_ref[g]
                ya_s[rs, cs] = u[rs, cs] * mixed
                du_s[rs, cs] = dya[rs, cs] * mixed
                dmx = dya[rs, cs] * u[rs, cs]
                dbs_acc[g] += dmx
                dmxb = dmx.astype(BF16)
                dws_ref[g] += _dot_nt(dmxb, vblk)
                dv_s[rs, cs] = _dot(wst_ref[g], dmxb)
        dz_ref[:, 4 * D:5 * D] = (dm * ya_s[...] * (sa * (1.0 - sa))).astype(BF16)
        dv = dv_s[...]
        _add_rows128(dlng_ref, jnp.sum(dv * xh, axis=0, keepdims=True))
        _add_rows128(dlnb_ref, jnp.sum(dv, axis=0, keepdims=True))
        dxh = dv * lng_ref[...]
        dgv = rstd * (dxh - jnp.mean(dxh, axis=-1, keepdims=True)
                      - xh * jnp.mean(dxh * xh, axis=-1, keepdims=True))
        dz_ref[:, 0:D] = (du_s[...] * _gelu_grad(zu, tu)).astype(BF16)
        dz_ref[:, D:2 * D] = (dgv * _gelu_grad(zv, tv)).astype(BF16)

        @pl.when(i == nt - 1)
        def _():
            for g in range(HEADS):
                dbs_ref[g:g + 1, :] = jnp.sum(dbs_acc[g].T, axis=0, keepdims=True)

    tile = pl.BlockSpec((tm, D), lambda i: (i, 0))
    wspec = _full((HEADS, HD, HD))
    return pl.pallas_call(
        body, name=f"mixer_bwd_l{layer}", grid=(nt,),
        in_specs=[tile, _full((D, D)), tile, tile]
        + [pl.BlockSpec((tm, D), lambda i, c=c: (i, c)) for c in (0, 1, 3, 4, 5)]
        + [_full((1, D)), _full((1, D)), wspec, wspec, wspec],
        out_specs=[pl.BlockSpec((tm, N_IN), lambda i: (i, 0)), tile, wspec, _full((HEADS, HD)),
                   _full((LANE_ROWS, HD)), _full((LANE_ROWS, HD))],
        out_shape=[SDS((s, N_IN), BF16), SDS((s, D), F32), SDS((HEADS, HD, HD), F32), SDS((HEADS, HD), F32),
                   SDS((LANE_ROWS, HD), F32), SDS((LANE_ROWS, HD), F32)],
        scratch_shapes=[pltpu.VMEM((tm, D), F32)] * 3 + [pltpu.VMEM((HEADS, HD, HD), F32)],
        compiler_params=_cp("arbitrary"))(dx1, wo, h0, h1, z, z, z, z, z, lng, lnb, ws, wst, bsb)


def _lru_gates_bwd(xcb, gates, h0, h1, g0, g1, wr, wi, lam, layer, tm):
    s = xcb.shape[0]
    nt = s // tm

    def body(xc_ref, r0_ref, i0_ref, r1_ref, i1_ref, h0p_ref, h0_ref, h1_ref, h1n_ref, g0_ref, g1_ref,
             wr_ref, wi_ref, lam_ref, dxc_ref, dwr_ref, dwi_ref, dbr_ref, dbi_ref, dlam_ref):
        i = pl.program_id(0)
        fp, fn = _halo_flags(nt)

        @pl.when(i == 0)
        def _():
            for r in (dwr_ref, dwi_ref, dbr_ref, dbi_ref, dlam_ref):
                r[...] = jnp.zeros_like(r)

        xb = xc_ref[...]
        xc = xb.astype(F32)
        zeros8 = jnp.zeros((8, D), F32)
        h_prev = _taps(h0p_ref[...] * fp, h0_ref[...], zeros8, tm)[1]
        h_next = _taps(zeros8, h1_ref[...], h1n_ref[...] * fn, tm)[3]
        dxc = jnp.zeros((tm, D), F32)
        for d, (g_ref, hsh, r_ref, i_ref) in enumerate(((g0_ref, h_prev, r0_ref, i0_ref),
                                                        (g1_ref, h_next, r1_ref, i1_ref))):
            sp = _softplus(-lam_ref[d:d + 1, :])
            r = r_ref[...].astype(F32)
            ig = i_ref[...].astype(F32)
            a, q = _lru_decay(r, sp)
            rmult = jnp.where(q > 0.0, lax.rsqrt(jnp.where(q > 0.0, q, 1.0)), 0.0)
            mult = q * rmult
            db = g_ref[...]
            da = db * hsh
            dmult = db * (ig * xc)
            di = db * (mult * xc)
            dxc = dxc + db * (mult * ig)
            dla = da * a - dmult * (a * a * rmult)
            dsp_dlam = -_sigmoid(-lam_ref[d:d + 1, :])
            _add_rows128(dlam_ref, jnp.sum(dla * r, axis=0, keepdims=True) * ((-LRU_C) * dsp_dlam), d * LANE_ROWS)
            dpr = dla * sp * (-LRU_C) * (r * (1.0 - r))
            dpi = di * (ig * (1.0 - ig))
            _add_rows128(dbr_ref, jnp.sum(dpr, axis=0, keepdims=True), d * LANE_ROWS)
            _add_rows128(dbi_ref, jnp.sum(dpi, axis=0, keepdims=True), d * LANE_ROWS)
            dprb = dpr.astype(BF16)
            dpib = dpi.astype(BF16)
            parts = []
            for h in range(HEADS):
                cs = slice(h * HD, (h + 1) * HD)
                dwr_ref[d, h] += _dot_tn(xb[:, cs], dprb[:, cs])
                dwi_ref[d, h] += _dot_tn(xb[:, cs], dpib[:, cs])
                parts.append(_dot_nt(dprb[:, cs], wr_ref[d, h]) + _dot_nt(dpib[:, cs], wi_ref[d, h]))
            dxc = dxc + jnp.concatenate(parts, axis=1)
        dxc_ref[...] = dxc.astype(BF16)

    tile = pl.BlockSpec((tm, D), lambda i: (i, 0))
    hp, hc, hn = _halo_specs(tm, s, 0)
    wspec = _full((2, HEADS, HD, HD))
    vspec = _full((2 * LANE_ROWS, HD))
    return pl.pallas_call(
        body, name=f"lru_gates_bwd_l{layer}", grid=(nt,),
        in_specs=[tile] * 5 + [hp, hc, hc, hn, tile, tile, wspec, wspec, _full((2, D))],
        out_specs=[tile, wspec, wspec, vspec, vspec, vspec],
        out_shape=[SDS((s, D), BF16), SDS((2, HEADS, HD, HD), F32), SDS((2, HEADS, HD, HD), F32)]
        + [SDS((2 * LANE_ROWS, HD), F32)] * 3,
        compiler_params=_cp("arbitrary"))(xcb, *gates, h0, h0, h1, h1, g0, g1, wr, wi, lam)


def _conv_bwd(dz, dxc, z, cw, layer, tm):
    s = z.shape[0]
    nt = s // tm

    def body(dz_in, dp_ref, dc_ref, dn_ref, zp_ref, zc_ref, zn_ref, cw_ref, dz_ref, dcw_ref, dcb_ref):
        del dz_in
        fp, fn = _halo_flags(nt)

        @pl.when(pl.program_id(0) == 0)
        def _():
            dcw_ref[...] = jnp.zeros_like(dcw_ref)
            dcb_ref[...] = jnp.zeros_like(dcb_ref)

        dxc_halo = _halo_load(dp_ref, dc_ref, dn_ref, fp, fn)
        dxc = dxc_halo[1]
        dm2, dm1, _, dp1, _ = _taps(*dxc_halo, tm)
        dz_ref[...] = (cw_ref[0:1, :] * dp1 + cw_ref[1:2, :] * dxc + cw_ref[2:3, :] * dm1
                       + cw_ref[3:4, :] * dm2).astype(BF16)
        _, zm1, z0, zp1, zp2 = _taps(*_halo_load(zp_ref, zc_ref, zn_ref, fp, fn), tm)
        for k, zt in enumerate((zm1, z0, zp1, zp2)):
            _add_rows128(dcw_ref, jnp.sum(dxc * zt, axis=0, keepdims=True), k * LANE_ROWS)
        _add_rows128(dcb_ref, jnp.sum(dxc, axis=0, keepdims=True))

    return pl.pallas_call(
        body, name=f"conv_bwd_l{layer}", grid=(nt,),
        in_specs=[pl.BlockSpec(memory_space=pl.ANY), *_halo_specs(tm, s, 0, 16), *_halo_specs(tm, s, 2, 16),
                  _full((4, D))],
        out_specs=[pl.BlockSpec((tm, D), lambda i: (i, 2)), _full((4 * LANE_ROWS, HD)), _full((LANE_ROWS, HD))],
        out_shape=[SDS((s, N_IN), BF16), SDS((4 * LANE_ROWS, HD), F32), SDS((LANE_ROWS, HD), F32)],
        input_output_aliases={0: 0},
        compiler_params=_cp("arbitrary"))(dz, dxc, dxc, dxc, z, z, z, cw)


def _me():
    return lax.axis_index("x"), lax.axis_index("y"), lax.axis_index("c")


def _peer(m):
    x, y, c = _me()
    px = 1 - x if m & 4 else x
    py = 1 - y if m & 2 else y
    pc = 1 - c if m & 1 else c
    return (px, py, pc), 4 * px + 2 * py + pc


_ANY = pl.BlockSpec(memory_space=pl.ANY)
_EXCHANGE_SEMS = [pltpu.SemaphoreType.DMA((N_DEV - 1,)), pltpu.SemaphoreType.DMA((N_DEV - 1,)), pltpu.SemaphoreType.DMA(())]


def _all_gather(v, after, name):
    def body(v_ref, after_ref, o_ref, send_sems, recv_sems, local_sem):
        del after_ref
        x, y, c = _me()
        me = 4 * x + 2 * y + c
        local = pltpu.make_async_copy(v_ref, o_ref.at[me], local_sem)
        local.start()
        sends = []
        for m in range(1, N_DEV):
            dev, _ = _peer(m)
            cp = pltpu.make_async_remote_copy(v_ref, o_ref.at[me], send_sems.at[m - 1], recv_sems.at[m - 1],
                                              device_id=dev, device_id_type=pl.DeviceIdType.MESH)
            cp.start()
            sends.append(cp)
        for m in range(1, N_DEV):
            dev, blk = _peer(m)
            pltpu.make_async_remote_copy(v_ref, o_ref.at[blk], send_sems.at[m - 1], recv_sems.at[m - 1],
                                         device_id=dev, device_id_type=pl.DeviceIdType.MESH).wait_recv()
        for cp in sends:
            cp.wait_send()
        local.wait()

    return pl.pallas_call(
        body, name=name, in_specs=[_ANY, _ANY], out_specs=_ANY,
        out_shape=SDS((N_DEV,) + v.shape, v.dtype), scratch_shapes=_EXCHANGE_SEMS)(v, after)


_HBM = pl.BlockSpec(memory_space=pltpu.HBM)
_SEM = pl.BlockSpec(memory_space=pltpu.SEMAPHORE)
_EFFECT = pltpu.CompilerParams(has_side_effects=pltpu.SideEffectType.DATAFLOW_SIDE_EFFECTING)
_PEER_SEMS = pltpu.SemaphoreType.DMA((N_DEV - 1,))


def _in_hbm(a):
    return pltpu.with_memory_space_constraint(a, pltpu.HBM)


def _remote(src, dst, send_sems, recv_sems, m):
    dev, _ = _peer(m)
    return pltpu.make_async_remote_copy(src, dst, send_sems.at[m - 1], recv_sems.at[m - 1],
                                        device_id=dev, device_id_type=pl.DeviceIdType.MESH)


def _gather_start(lands, after, name):
    n = len(lands)

    def body(*refs):
        land = refs[:n]
        sems = refs[n + 1:3 * n + 1]
        token = refs[-1]
        x, y, c = _me()
        me = 4 * x + 2 * y + c
        for t in range(n):
            for m in range(1, N_DEV):
                _remote(land[t].at[me], land[t].at[me], sems[2 * t], sems[2 * t + 1], m).start()
        token[...] = jnp.zeros_like(token)

    res = pl.pallas_call(
        body, name=name, in_specs=[_HBM] * n + [_ANY],
        out_specs=[_SEM] * (2 * n) + [_HBM] * n + [pl.BlockSpec(memory_space=pltpu.VMEM)],
        out_shape=[_PEER_SEMS] * (2 * n) + [pltpu.HBM(a.shape, a.dtype) for a in lands] + [SDS((8, 128), F32)],
        input_output_aliases={t: 2 * n + t for t in range(n)},
        compiler_params=_EFFECT)(*[_in_hbm(a) for a in lands], after)
    return [(res[2 * t], res[2 * t + 1], res[2 * n + t]) for t in range(n)], res[-1]


def _gather_wait(handle, after, name):
    send_sems, recv_sems, land = handle

    def body(land_ref, ssem, rsem, after_ref, out_ref):
        del after_ref, out_ref
        x, y, c = _me()
        me = 4 * x + 2 * y + c
        for m in range(1, N_DEV):
            _, blk = _peer(m)
            cp = _remote(land_ref.at[me], land_ref.at[blk], ssem, rsem, m)
            cp.wait_send()
            cp.wait_recv()

    return pl.pallas_call(
        body, name=name, in_specs=[_HBM, _SEM, _SEM, _ANY], out_specs=_HBM,
        out_shape=pltpu.HBM(land.shape, land.dtype), input_output_aliases={0: 0},
        compiler_params=_EFFECT)(land, send_sems, recv_sems, after)


FIRST_STAGE = (1, 2, 4, 6)
RELAYED = (2, 4, 6)
OTHER_CORE = 1


def _stage_copy(src, dst, send_sems, recv_sems, k, m):
    dev, _ = _peer(m)
    return pltpu.make_async_remote_copy(src, dst, send_sems.at[k], recv_sems.at[k],
                                        device_id=dev, device_id_type=pl.DeviceIdType.MESH)


def _gather2_start(lands, after, name):
    n = len(lands)

    def body(*refs):
        land = refs[:n]
        sems = refs[n + 1:3 * n + 1]
        token = refs[-1]
        x, y, c = _me()
        me = 4 * x + 2 * y + c
        for t in range(n):
            for k, m in enumerate(FIRST_STAGE):
                _stage_copy(land[t].at[me], land[t].at[me], sems[2 * t], sems[2 * t + 1], k, m).start()
        token[...] = jnp.zeros_like(token)

    stage_sems = pltpu.SemaphoreType.DMA((len(FIRST_STAGE),))
    res = pl.pallas_call(
        body, name=name, in_specs=[_HBM] * n + [_ANY],
        out_specs=[_SEM] * (2 * n) + [_HBM] * n + [pl.BlockSpec(memory_space=pltpu.VMEM)],
        out_shape=[stage_sems] * (2 * n) + [pltpu.HBM(a.shape, a.dtype) for a in lands] + [SDS((8, 128), F32)],
        input_output_aliases={t: 2 * n + t for t in range(n)},
        compiler_params=_EFFECT)(*[_in_hbm(a) for a in lands], after)
    return [(res[2 * t], res[2 * t + 1], res[2 * n + t]) for t in range(n)], res[-1]


def _gather2_relay(handles, after, name):
    n = len(handles)

    def body(*refs):
        land, send1, recv1 = refs[:n], refs[n:2 * n], refs[2 * n:3 * n]
        sems = refs[3 * n + 1:5 * n + 1]
        token = refs[-1]
        x, y, c = _me()
        me = 4 * x + 2 * y + c
        for t in range(n):
            for j, m in enumerate(RELAYED):
                _, blk = _peer(m)
                _stage_copy(land[t].at[me], land[t].at[blk], send1[t], recv1[t], 1 + j, m).wait_recv()
                _stage_copy(land[t].at[blk], land[t].at[blk], sems[2 * t], sems[2 * t + 1], j, OTHER_CORE).start()
        token[...] = jnp.zeros_like(token)

    relay_sems = pltpu.SemaphoreType.DMA((len(RELAYED),))
    lands = [h[2] for h in handles]
    res = pl.pallas_call(
        body, name=name, in_specs=[_HBM] * n + [_SEM] * (2 * n) + [_ANY],
        out_specs=[_SEM] * (2 * n) + [_HBM] * n + [pl.BlockSpec(memory_space=pltpu.VMEM)],
        out_shape=[relay_sems] * (2 * n) + [pltpu.HBM(a.shape, a.dtype) for a in lands] + [SDS((8, 128), F32)],
        input_output_aliases={t: 2 * n + t for t in range(n)},
        compiler_params=_EFFECT)(*lands, *[h[0] for h in handles], *[h[1] for h in handles], after)
    return [(h[0], h[1], res[2 * t], res[2 * t + 1], res[2 * n + t]) for t, h in enumerate(handles)], res[-1]


def _gather2_wait(handle, after, name):
    send1, recv1, send2, recv2, land = handle

    def body(land_ref, s1, r1, s2, r2, after_ref, out_ref):
        del after_ref, out_ref
        x, y, c = _me()
        me = 4 * x + 2 * y + c
        _, other = _peer(OTHER_CORE)
        _stage_copy(land_ref.at[me], land_ref.at[other], s1, r1, 0, OTHER_CORE).wait_recv()
        for k, m in enumerate(FIRST_STAGE):
            _stage_copy(land_ref.at[me], land_ref.at[me], s1, r1, k, m).wait_send()
        for j, m in enumerate(RELAYED):
            _, mine = _peer(m)
            _, theirs = _peer(m ^ OTHER_CORE)
            _stage_copy(land_ref.at[mine], land_ref.at[mine], s2, r2, j, OTHER_CORE).wait_send()
            _stage_copy(land_ref.at[mine], land_ref.at[theirs], s2, r2, j, OTHER_CORE).wait_recv()

    return pl.pallas_call(
        body, name=name, in_specs=[_HBM] + [_SEM] * 4 + [_ANY], out_specs=_HBM,
        out_shape=pltpu.HBM(land.shape, land.dtype), input_output_aliases={0: 0},
        compiler_params=_EFFECT)(land, send1, recv1, send2, recv2, after)


def _exchange_start(ps, name):
    n = len(ps)

    def body(*refs):
        p = refs[:n]
        got = refs[n:2 * n]
        sems = refs[2 * n:5 * n]
        token = refs[-1]
        x, y, c = _me()
        me = 4 * x + 2 * y + c
        for t in range(n):
            pltpu.make_async_copy(p[t].at[me], got[t].at[me], sems[3 * t + 2]).start()
            for m in range(1, N_DEV):
                _, blk = _peer(m)
                _remote(p[t].at[blk], got[t].at[me], sems[3 * t], sems[3 * t + 1], m).start()
        token[...] = jnp.zeros_like(token)

    res = pl.pallas_call(
        body, name=name, in_specs=[_HBM] * (2 * n),
        out_specs=[_SEM] * (3 * n) + [_HBM] * (2 * n) + [pl.BlockSpec(memory_space=pltpu.VMEM)],
        out_shape=[_PEER_SEMS, _PEER_SEMS, pltpu.SemaphoreType.DMA(())] * n
        + [pltpu.HBM(a.shape, a.dtype) for a in ps] * 2 + [SDS((8, 128), F32)],
        input_output_aliases={t: 3 * n + t for t in range(2 * n)},
        compiler_params=_EFFECT)(*[_in_hbm(a) for a in ps], *[_in_hbm(lax.empty(a.shape, a.dtype)) for a in ps])
    return [(res[3 * t], res[3 * t + 1], res[3 * t + 2], res[3 * n + t], res[4 * n + t]) for t in range(n)], res[-1]


def _exchange_wait(handle, after, name):
    send_sems, recv_sems, local_sem, p, got = handle

    def body(p_ref, got_ref, ssem, rsem, lsem, after_ref, p_out, got_out):
        del after_ref, p_out, got_out
        x, y, c = _me()
        me = 4 * x + 2 * y + c
        pltpu.make_async_copy(p_ref.at[me], got_ref.at[me], lsem).wait()
        for m in range(1, N_DEV):
            _, blk = _peer(m)
            cp = _remote(p_ref.at[blk], got_ref.at[blk], ssem, rsem, m)
            cp.wait_send()
            cp.wait_recv()

    return pl.pallas_call(
        body, name=name, in_specs=[_HBM, _HBM, _SEM, _SEM, _SEM, _ANY], out_specs=[_HBM, _HBM],
        out_shape=[pltpu.HBM(p.shape, p.dtype), pltpu.HBM(got.shape, got.dtype)],
        input_output_aliases={0: 0, 1: 1}, compiler_params=_EFFECT)(p, got, send_sems, recv_sems, local_sem, after)[1]


def _cast_into_slot(w, layer, me1, name):
    _, r, c = w.shape
    tr = next(t for t in (256, 352, r) if r % t == 0)

    def body(me_ref, w_ref, o_ref):
        del me_ref
        o_ref[...] = w_ref[...].astype(BF16)

    return pl.pallas_call(
        body, name=name,
        grid_spec=pltpu.PrefetchScalarGridSpec(
            num_scalar_prefetch=1, grid=(r // tr,),
            in_specs=[pl.BlockSpec((None, tr, c), lambda i, me: (layer, i, 0))],
            out_specs=pl.BlockSpec((None, tr, c), lambda i, me: (me[0], i, 0))),
        out_shape=SDS((N_DEV, r, c), BF16), compiler_params=_cp("arbitrary"))(me1, w)


def _cast_all_into_slots(ws, layers, me1, after, name):
    n = len(ws)

    def body(me_ref, *refs):
        del me_ref
        for w_ref, o_ref in zip(refs[:n], refs[n + 1:]):
            o_ref[...] = w_ref[...].astype(BF16)

    return pl.pallas_call(
        body, name=name,
        grid_spec=pltpu.PrefetchScalarGridSpec(
            num_scalar_prefetch=1, grid=(1,),
            in_specs=[pl.BlockSpec((None,) + a.shape[1:], lambda i, me, l=l: (l, 0, 0)) for a, l in zip(ws, layers)]
            + [_ANY],
            out_specs=[pl.BlockSpec((None,) + a.shape[1:], lambda i, me: (me[0], 0, 0)) for a in ws]),
        out_shape=[SDS((N_DEV,) + a.shape[1:], BF16) for a in ws],
        compiler_params=_cp("arbitrary"))(me1, *ws, after)


def _sum8_into_slot(p, me1, name):
    _, r, c = p.shape

    def body(me_ref, p_ref, o_ref):
        del me_ref
        acc = p_ref[0]
        for k in range(1, N_DEV):
            acc = acc + p_ref[k]
        o_ref[...] = acc

    return pl.pallas_call(
        body, name=name,
        grid_spec=pltpu.PrefetchScalarGridSpec(
            num_scalar_prefetch=1, grid=(1,),
            in_specs=[pl.BlockSpec(p.shape, lambda i, me: (0, 0, 0))],
            out_specs=pl.BlockSpec((None, r, c), lambda i, me: (me[0], 0, 0))),
        out_shape=SDS(p.shape, F32), compiler_params=_cp("arbitrary"))(me1, p)


def _adamw(w, g, m, v):
    m = ADAM_B1 * m + (1.0 - ADAM_B1) * g
    v = ADAM_B2 * v + (1.0 - ADAM_B2) * (g * g)
    m_hat = m / (1.0 - ADAM_B1 ** ADAM_STEP)
    v_hat = v / (1.0 - ADAM_B2 ** ADAM_STEP)
    delta = -ADAM_LR * (m_hat / (jnp.sqrt(v_hat) + ADAM_EPS) + ADAM_WD * w)
    return delta, m, v


def _adam_shard(parts, w, m, v, layer, prev, name):
    _, r, c = parts.shape
    tr = next(t for t in (256, 352, r) if r % t == 0)
    n_prev = 0 if prev is None else 4

    def body(*refs):
        p_ref, w_ref, m_ref, v_ref = refs[:4]
        g_ref, d_ref, nm_ref, nv_ref = refs[4 + n_prev:]
        g = p_ref[0].astype(F32)
        for k in range(1, N_DEV):
            g = g + p_ref[k].astype(F32)
        delta, nm, nv = _adamw(w_ref[...], g, m_ref[...], v_ref[...])
        g_ref[...] = g
        d_ref[...] = delta
        nm_ref[...] = nm
        nv_ref[...] = nv

    blk = pl.BlockSpec((None, tr, c), lambda i: (layer, i, 0))
    return pl.pallas_call(
        body, name=name, grid=(r // tr,),
        in_specs=[pl.BlockSpec((N_DEV, tr, c), lambda i: (0, i, 0)), blk, blk, blk] + [_ANY] * n_prev,
        out_specs=[blk] * 4, out_shape=[SDS(w.shape, F32)] * 4,
        input_output_aliases={4 + k: k for k in range(n_prev)},
        compiler_params=_cp("parallel"))(parts, w, m, v, *(prev or ()))


SMALL_MATRICES = [("lru_w_r", 2048), ("lru_w_i", 2048), ("gmlp_w_s", 1024)]
SMALL_VECTORS = [("norm1_g", 8), ("gmlp_ln_g", 8), ("gmlp_ln_b", 8), ("gmlp_b_s", 8), ("conv_w", 32), ("conv_b", 8),
                 ("lru_b_r", 16), ("lru_b_i", 16), ("lru_lambda", 16), ("norm2_g", 8), ("final_g", 8)]
SMALL_VECTOR_ROW0 = sum(n for _, n in SMALL_MATRICES)
SMALL_VECTOR_BLOCK = 256
SMALL_ROWS = SMALL_VECTOR_ROW0 + SMALL_VECTOR_BLOCK


def _pack_small(small):
    parts = [small[k] for k, _ in SMALL_MATRICES]
    parts += [small[k] if k in small else jnp.zeros((n, HD), F32) for k, n in SMALL_VECTORS]
    flat = jnp.concatenate(parts)
    return jnp.pad(flat, ((0, SMALL_ROWS - flat.shape[0]), (0, 0))).reshape(N_DEV, SMALL_ROWS // N_DEV, HD)


def _adam_matrix(g0, g1, w, m, v, row0, name):
    _, rows, _ = w.shape

    def body(g0_ref, g1_ref, w_ref, m_ref, v_ref, g_ref, d_ref, nm_ref, nv_ref):
        for l, src in enumerate((g0_ref, g1_ref)):
            g = src[...]
            delta, nm, nv = _adamw(w_ref[l], g, m_ref[l], v_ref[l])
            g_ref[l] = g
            d_ref[l] = delta
            nm_ref[l] = nm
            nv_ref[l] = nv

    gspec = pl.BlockSpec((rows, HD), lambda i: (row0 // rows, 0))
    return pl.pallas_call(body, name=name, grid=(1,), in_specs=[gspec, gspec] + [_full(w.shape)] * 3,
                          out_specs=[_full(w.shape)] * 4, out_shape=[SDS(w.shape, F32)] * 4,
                          compiler_params=_cp("arbitrary"))(g0, g1, w, m, v)


def _adam_vectors(g0, g1, dg1_parts, me1, ws, ms, vs):
    names = [k for k, _ in SMALL_VECTORS]
    n = len(names)

    def lanes(rows8):
        return jnp.concatenate([rows8[k:k + 1, :] for k in range(LANE_ROWS)], axis=1)

    def body(me_ref, g0_ref, g1_ref, dg1_ref, *refs):
        w_refs, m_refs, v_refs = refs[:n], refs[n:2 * n], refs[2 * n:3 * n]
        outs = refs[3 * n:]
        me = me_ref[0]
        g_refs = (g0_ref, g1_ref)

        def emit(i, idx, g):
            delta, nm, nv = _adamw(w_refs[i][idx], g, m_refs[i][idx], v_refs[i][idx])
            for j, val in enumerate((g, delta, nm, nv)):
                outs[4 * i + j][idx] = val

        off = 0
        for i, (name, rows) in enumerate(SMALL_VECTORS):
            for l in range(2):
                row = (slice(l, l + 1), slice(None))
                if name == "final_g":
                    if l == 1:
                        emit(i, (slice(0, 1), slice(None)), lanes(g1_ref[off:off + rows, :]))
                elif name == "norm1_g":
                    if l == 1:
                        emit(i, row, lanes(g0_ref[off:off + rows, :]))
                    else:
                        total = dg1_ref[0]
                        for k in range(1, N_DEV):
                            total = total + dg1_ref[k]
                        emit(i, row, lanes(total))
                elif name == "gmlp_b_s":
                    emit(i, (l,), g_refs[l][off:off + rows, :])
                elif rows == LANE_ROWS:
                    emit(i, row, lanes(g_refs[l][off:off + rows, :]))
                else:
                    for r in range(rows // LANE_ROWS):
                        emit(i, (l, slice(r, r + 1), slice(None)), g_refs[l][pl.ds(off + r * LANE_ROWS + me, 1), :])
            off += rows

    args = [ws[k] for k in names] + [ms[k] for k in names] + [vs[k] for k in names]
    gspec = pl.BlockSpec((SMALL_VECTOR_BLOCK, HD), lambda i, me: (SMALL_VECTOR_ROW0 // SMALL_VECTOR_BLOCK, 0))
    res = pl.pallas_call(
        body, name="adam_vectors",
        grid_spec=pltpu.PrefetchScalarGridSpec(
            num_scalar_prefetch=1, grid=(1,),
            in_specs=[gspec, gspec, _full(dg1_parts.shape)] + [_full(a.shape) for a in args],
            out_specs=[_full(ws[k].shape) for k in names for _ in range(4)]),
        out_shape=[SDS(ws[k].shape, F32) for k in names for _ in range(4)],
        compiler_params=_cp("arbitrary"))(me1, g0, g1, dg1_parts, *args)
    return {k: list(res[4 * i:4 * i + 4]) for i, k in enumerate(names)}


def _after(a, *tokens):
    for token in tokens:
        if token is not None:
            a = a + token[0:1, 0:1]
    return a


def _local_step(x, tgt, p, get_w, hook=lambda stage, layer, payload: None):
    s = x.shape[0]
    tm = _row_tile(s)
    wsb = p["gmlp_w_s"].astype(BF16)
    wstb = jnp.swapaxes(p["gmlp_w_s"], -1, -2).astype(BF16)
    bsb = jnp.broadcast_to(p["gmlp_b_s"][..., None], p["gmlp_w_s"].shape)
    wrb = p["lru_w_r"].astype(BF16)
    wib = p["lru_w_i"].astype(BF16)
    saved = []
    for l in range(2):
        win = get_w("w_in", l, x)
        z, h1 = _norm_inproj(x, _after(p["norm1_g"][l][None], hook("pre_inproj", l, win)), win, l, tm)
        a0, b0, a1, b1, xcb, *gates = _lru_gates_fwd(z, p["conv_w"][l], p["conv_b"][l][None], wrb[l], wib[l],
                                                     p["lru_b_r"][l], p["lru_b_i"][l], p["lru_lambda"][l], l, tm)
        h0, hr = _lru_scan(a0, b0, a1, b1, False, l)
        lng = _after(p["gmlp_ln_g"][l][None], hook("pre_gmlp", l, h0))
        wout = get_w("w_out", l, lng)
        x1, mg = _mixer_fwd(x, h0, hr, z, lng, p["gmlp_ln_b"][l][None], wsb[l], bsb[l], wout, l, tm)
        wfi = get_w("w_ffn_in", l, x1)
        wfo = get_w("w_ffn_out", l, x1)
        if l == 0:
            x2, ff, dff, h2 = _ffn_fwd(x1, p["norm2_g"][l][None], wfi, wfo, l, tm)
        else:
            dx, loss, dfg, ff, dff, h2 = _ffn_fwd(x1, p["norm2_g"][l][None], wfi, wfo, l, tm,
                                                  head=(p["final_g"][None], tgt))
        saved.append((x, z, h1, a0, a1, h0, hr, x1, mg, ff, dff, h2, win, wout, wfi, wfo, xcb, gates))
        x = x2
    pending = None
    for l in (1, 0):
        x0, z, h1, a0, a1, h0, hr, x1, mg, ff, dff, h2, win, wout, wfi, wfo, xcb, gates = saved[l]
        dgu, dx1, dg2 = _ffn_bwd(dx, wfo, dff, wfi.reshape(N_DEV, FF_BLK, D), x1, p["norm2_g"][l][None], l, tm)
        d_wfo = _mm_tn(ff, pl.BlockSpec((None, s, FF_BLK), lambda j: (j, 0, 0)), dx, _resident((s, D)),
                       4, (4, FF_BLK, D), pl.BlockSpec((None, FF_BLK, D), lambda j: (j, 0, 0)),
                       f"dw_ffn_out_l{l}", a_is_transposed=False)
        dgu8 = dgu.reshape(N_DEV, s, FF_BLK)
        d_wfi = _mm_tn(dgu8, pl.BlockSpec((None, s, FF_BLK), lambda j: (j, 0, 0)), h2, _resident((s, D)),
                       N_DEV, (N_DEV, FF_BLK, D), pl.BlockSpec((None, FF_BLK, D), lambda j: (j, 0, 0)),
                       f"dw_ffn_in_l{l}", a_is_transposed=False)
        token = hook("ffn_partials", l, dict(w_ffn_out=d_wfo.reshape(N_DEV, D_FF // N_DEV, D), w_ffn_in=d_wfi))
        pending = hook("mid_backward", l, dx1)
        dz, dh, dws, dbs, dlng, dlnb = _mixer_bwd(dx1, wout, h0, hr, z, _after(p["gmlp_ln_g"][l][None], token),
                                                  p["gmlp_ln_b"][l][None], wsb[l], wstb[l], bsb[l], l, tm)
        d_wout = _mm_tn(mg, _resident((D, s)), dx1, pl.BlockSpec((s, D // 2), lambda j: (0, j)),
                        2, (D, D), pl.BlockSpec((D, D // 2), lambda j: (0, j)), f"dw_out_l{l}")
        g1, g0 = _lru_scan(a1, dh, a0, dh, True, l)
        dxc, dwr, dwi, dbr, dbi, dlam = _lru_gates_bwd(
            xcb, gates, h0, hr, g0, g1, wrb[l], wib[l], _after(p["lru_lambda"][l], pending), l, tm)
        dz, dcw, dcb = _conv_bwd(dz, dxc, z, p["conv_w"][l], l, tm)
        small = dict(lru_w_r=dwr.reshape(-1, HD), lru_w_i=dwi.reshape(-1, HD), gmlp_w_s=dws.reshape(-1, HD),
                     gmlp_ln_g=dlng, gmlp_ln_b=dlnb, gmlp_b_s=dbs, conv_w=dcw, conv_b=dcb, lru_b_r=dbr,
                     lru_b_i=dbi, lru_lambda=dlam, norm2_g=dg2)
        if l == 1:
            small["final_g"] = dfg
        else:
            small["norm1_g"] = dg1
        started = hook("small_grads", l, small)
        d_win = _mm_tn(h1, _resident((D, s)), dz, pl.BlockSpec((s, IN_BLK), lambda j: (0, j)),
                       N_DEV, (N_DEV, D, IN_BLK), pl.BlockSpec((None, D, IN_BLK), lambda j: (j, 0, 0)),
                       f"dw_in_l{l}", after=started)
        token = hook("mixer_partials", l, dict(w_out=d_wout.reshape(N_DEV, D // N_DEV, D), w_in=d_win))
        dx, dg1 = _mm_nt_rms_bwd(
            dz, pl.BlockSpec((tm, N_IN), lambda i: (i, 0)),
            lambda r: [r[:, k * IN_BLK:(k + 1) * IN_BLK] for k in range(N_DEV)],
            win, False, x0, _after(p["norm1_g"][l][None], token, started, pending), dx1, f"inproj_bwd_dx_l{l}", tm)
        pending = None
    return loss, dx, dg1


_REPL = ["norm1_g", "gmlp_ln_g", "gmlp_ln_b", "gmlp_w_s", "gmlp_b_s", "conv_b", "lru_w_r", "lru_w_i", "norm2_g", "final_g"]
_LANE_SHARDED = ["conv_w", "lru_b_r", "lru_b_i", "lru_lambda"]
_BIG = ["w_in", "w_out", "w_ffn_in", "w_ffn_out"]
_ORDER = ["norm1_g", "w_in", "gmlp_ln_g", "gmlp_ln_b", "gmlp_w_s", "gmlp_b_s", "conv_w", "conv_b", "lru_w_r", "lru_b_r",
          "lru_w_i", "lru_b_i", "lru_lambda", "w_out", "norm2_g", "w_ffn_in", "w_ffn_out", "final_g"]


def kernel(x, norm1_g, w_in, gmlp_ln_g, gmlp_ln_b, gmlp_w_s, gmlp_b_s, conv_w, conv_b, lru_w_r, lru_b_r, lru_w_i, lru_b_i, lru_lambda, w_out, norm2_g, w_ffn_in, w_ffn_out, final_g, loss_target, m_norm1_g, m_w_in, m_gmlp_ln_g, m_gmlp_ln_b, m_gmlp_w_s, m_gmlp_b_s, m_conv_w, m_conv_b, m_lru_w_r, m_lru_b_r, m_lru_w_i, m_lru_b_i, m_lru_lambda, m_w_out, m_norm2_g, m_w_ffn_in, m_w_ffn_out, m_final_g, v_norm1_g, v_w_in, v_gmlp_ln_g, v_gmlp_ln_b, v_gmlp_w_s, v_gmlp_b_s, v_conv_w, v_conv_b, v_lru_w_r, v_lru_b_r, v_lru_w_i, v_lru_b_i, v_lru_lambda, v_w_out, v_norm2_g, v_w_ffn_in, v_w_ffn_out, v_final_g):
    w = dict(norm1_g=norm1_g, w_in=w_in, gmlp_ln_g=gmlp_ln_g, gmlp_ln_b=gmlp_ln_b, gmlp_w_s=gmlp_w_s, gmlp_b_s=gmlp_b_s,
             conv_w=conv_w, conv_b=conv_b, lru_w_r=lru_w_r, lru_b_r=lru_b_r, lru_w_i=lru_w_i, lru_b_i=lru_b_i,
             lru_lambda=lru_lambda, w_out=w_out, norm2_g=norm2_g, w_ffn_in=w_ffn_in, w_ffn_out=w_ffn_out, final_g=final_g)
    mom = dict(norm1_g=m_norm1_g, w_in=m_w_in, gmlp_ln_g=m_gmlp_ln_g, gmlp_ln_b=m_gmlp_ln_b, gmlp_w_s=m_gmlp_w_s,
               gmlp_b_s=m_gmlp_b_s, conv_w=m_conv_w, conv_b=m_conv_b, lru_w_r=m_lru_w_r, lru_b_r=m_lru_b_r,
               lru_w_i=m_lru_w_i, lru_b_i=m_lru_b_i, lru_lambda=m_lru_lambda, w_out=m_w_out, norm2_g=m_norm2_g,
               w_ffn_in=m_w_ffn_in, w_ffn_out=m_w_ffn_out, final_g=m_final_g)
    var = dict(norm1_g=v_norm1_g, w_in=v_w_in, gmlp_ln_g=v_gmlp_ln_g, gmlp_ln_b=v_gmlp_ln_b, gmlp_w_s=v_gmlp_w_s,
               gmlp_b_s=v_gmlp_b_s, conv_w=v_conv_w, conv_b=v_conv_b, lru_w_r=v_lru_w_r, lru_b_r=v_lru_b_r,
               lru_w_i=v_lru_w_i, lru_b_i=v_lru_b_i, lru_lambda=v_lru_lambda, w_out=v_w_out, norm2_g=v_norm2_g,
               w_ffn_in=v_w_ffn_in, w_ffn_out=v_w_ffn_out, final_g=v_final_g)
    for src in (w, mom, var):
        src["w_ffn_in"] = jnp.swapaxes(src["w_ffn_in"], 1, 2)
    xi, yi, ci = _me()
    me = 4 * xi + 2 * yi + ci

    lane_shapes = [w[k].shape for k in _LANE_SHARDED]
    lane_rows = sum(a[0] * a[1] for a in lane_shapes)
    packed = jnp.concatenate([w[k].reshape(-1, HD) for k in _LANE_SHARDED])
    packed = jnp.pad(packed, ((0, -lane_rows % 8), (0, 0)))

    me1 = jnp.reshape(me, (1,)).astype(jnp.int32)
    gathers = {}
    exchanges = {}
    views = dict(w_in=(N_DEV, D, IN_BLK), w_out=(D, D), w_ffn_in=(2, 4, FF_BLK, D), w_ffn_out=(4, FF_BLK, D))
    small_ex = {}
    small_ag = {}

    casts = {}

    def start_gather(names, l, after):
        lands = [casts[(k, l)] if (k, l) in casts else _cast_into_slot(w[k], l, me1, f"cast_{k}_l{l}") for k in names]
        started, tok = _gather2_start(lands, after, f"gather_start_{'_'.join(names)}_l{l}")
        gathers.update({(k, l): h for k, h in zip(names, started)})
        return tok

    def relay_gather(names, l, after):
        relayed, tok = _gather2_relay([gathers[(k, l)] for k in names], after, f"gather_relay_{'_'.join(names)}_l{l}")
        gathers.update({(k, l): h for k, h in zip(names, relayed)})
        return tok

    def get_w(k, l, after):
        if (k, l) == ("w_in", 1):
            after = relay_gather(_BIG[:1], l, after)
        return _gather2_wait(gathers[(k, l)], after, f"gather_wait_{k}_l{l}").reshape(views[k])

    def hook(stage, l, payload):
        if stage == "pre_inproj":
            return start_gather(_BIG[1:], l, payload)
        if stage == "pre_gmlp":
            tok = relay_gather(_BIG[1:], l, payload)
            return tok + start_gather(_BIG[:1], l + 1, tok) if l == 0 else tok
        if stage == "small_grads":
            (small_ex[l],), tok = _exchange_start([_pack_small(payload)], f"exchange_start_small_l{l}")
            return tok
        if stage == "mid_backward":
            return reduce_small(l + 1, payload) if l == 0 else None
        extra = reduce_small(0, payload["w_in"]) if (stage, l) == ("mixer_partials", 0) else None
        started, tok = _exchange_start(list(payload.values()), f"exchange_start_{'_'.join(payload)}_l{l}")
        exchanges.update({(k, l): h for k, h in zip(payload, started)})
        return tok if extra is None else tok + extra

    def reduce_small(l, after):
        got = _exchange_wait(small_ex[l], after, f"exchange_wait_small_l{l}")
        mine = _sum8_into_slot(got, me1, f"sum_small_l{l}")
        (small_ag[l],), tok = _gather_start([mine], got, f"gather_start_small_l{l}")
        return tok

    land = lax.dynamic_update_slice(jnp.zeros((N_DEV,) + packed.shape, F32), packed[None], (me, 0, 0))
    (lanes_handle,), token = _gather_start([land], packed, "gather_start_lanes")
    token = start_gather(_BIG[:1], 0, token)
    later = [(k, l) for l in range(2) for k in _BIG if (k, l) != ("w_in", 0)]
    casts.update(zip(later, _cast_all_into_slots([w[k] for k, _ in later], [l for _, l in later], me1, token,
                                                 "cast_later_weights")))
    token = relay_gather(_BIG[:1], 0, casts[later[0]])
    lanes = _gather_wait(lanes_handle, token, "gather_wait_lanes")
    params = {k: w[k] for k in _REPL}
    off = 0
    for k, shp in zip(_LANE_SHARDED, lane_shapes):
        n = shp[0] * shp[1]
        params[k] = jnp.swapaxes(lanes[:, off:off + n], 0, 1).reshape(shp[0], shp[1], D)
        off += n
    loss, dx, dg1 = _local_step(x[0], loss_target[0], params, get_w, hook)

    out = {}
    after = dx
    for k, l in [(k, l) for k in ("w_ffn_out", "w_ffn_in") for l in (1, 0)] + [("w_out", 1), ("w_in", 1)]:
        got = _exchange_wait(exchanges[(k, l)], after, f"exchange_wait_{k}_l{l}")
        out[k] = _adam_shard(got, w[k], mom[k], var[k], l, out.get(k), f"adam_{k}_l{l}")
        after = out[k][3]
    g_small = [_gather_wait(small_ag[l], after, f"gather_wait_small_l{l}").reshape(SMALL_ROWS, HD) for l in (0, 1)]
    row0 = 0
    for k, rows in SMALL_MATRICES:
        res = _adam_matrix(*g_small, *[src[k].reshape(2, rows, HD) for src in (w, mom, var)], row0, f"adam_{k}")
        out[k] = [a.reshape(w[k].shape) for a in res]
        after = res[3]
        row0 += rows
    for k in ("w_out", "w_in"):
        got = _exchange_wait(exchanges[(k, 0)], after, f"exchange_wait_{k}_l0")
        out[k] = _adam_shard(got, w[k], mom[k], var[k], 0, out[k], f"adam_{k}_l0")
    out["w_ffn_in"] = [jnp.swapaxes(a, 1, 2) for a in out["w_ffn_in"]]
    as_rows = lambda a: a.reshape(1, D) if a.ndim == 1 else a
    vec = _adam_vectors(*g_small, _all_gather(dg1, out["w_in"][3], "gather_norm1_grad"), me1,
                        *[{k: as_rows(src[k]) for k, _ in SMALL_VECTORS} for src in (w, mom, var)])
    out.update({k: [a.reshape(w[k].shape) for a in res] for k, res in vec.items()})

    loss = lax.psum(loss[0, 0], MESH_AXES)
    return (loss, dx[None], *[out[k][0] for k in _ORDER], *[out[k][1] for k in _ORDER],
            *[out[k][2] for k in _ORDER], *[out[k][3] for k in _ORDER])
```

```python
import jax
import jax.numpy as jnp
from jax import lax
from jax.experimental import pallas as pl
from jax.experimental.pallas import tpu as pltpu

F32 = jnp.float32
BF16 = jnp.bfloat16
SDS = jax.ShapeDtypeStruct

D = 1024
N_IN = 6 * D
D_FF = 2816
N_DEV = 8
IN_BLK = N_IN // N_DEV
FF_BLK = 2 * D_FF // N_DEV
HEADS = 8
HD = 128
EPS = 1e-6
LRU_C = 8.0
MESH_AXES = ("x", "y", "c")

ADAM_LR = 0.001
ADAM_B1 = 0.9
ADAM_B2 = 0.999
ADAM_EPS = 1e-08
ADAM_WD = 0.01
ADAM_STEP = 10

VMEM_LIMIT = 60 * 2**20


def _cp(*sem, **kw):
    return pltpu.CompilerParams(dimension_semantics=sem, vmem_limit_bytes=VMEM_LIMIT, **kw)


def _row_tile(s):
    return 512 if s >= 1024 else s // 2


_GELU_C = 0.7978845608028654


def _gelu(x):
    t = jnp.tanh(_GELU_C * (x + 0.044715 * (x * x * x)))
    return 0.5 * x * (1.0 + t), t


def _gelu_grad(x, t):
    return 0.5 * (1.0 + t) + 0.5 * x * (1.0 - t * t) * (_GELU_C * (1.0 + 0.134145 * (x * x)))


def _sigmoid(x):
    return 0.5 + 0.5 * jnp.tanh(0.5 * x)


def _softplus(x):
    e = jnp.exp(-jnp.abs(x))
    w = 1.0 + e
    l1p = jnp.where(w == 1.0, e, jnp.log(w) * e / jnp.where(w == 1.0, 1.0, w - 1.0))
    return jnp.maximum(x, 0.0) + l1p


def _rms_fwd(x, g):
    r = lax.rsqrt(jnp.mean(x * x, axis=-1, keepdims=True) + EPS)
    return x * r * g


def _rms_bwd(x, g, dh):
    r = lax.rsqrt(jnp.mean(x * x, axis=-1, keepdims=True) + EPS)
    xh = x * r
    dxh = dh * g
    dx = r * (dxh - xh * jnp.mean(dxh * xh, axis=-1, keepdims=True))
    dg = jnp.sum(dh * xh, axis=0, keepdims=True)
    return dx, dg


LANE_ROWS = D // HD


def _add_rows128(ref, vec, row0=0):
    for i in range(vec.shape[0]):
        for k in range(LANE_ROWS):
            j = row0 + i * LANE_ROWS + k
            ref[j:j + 1, :] += vec[i:i + 1, k * HD:(k + 1) * HD]


def _dot(a, b):
    return jnp.dot(a, b, preferred_element_type=F32)


def _dot_nt(a, b):
    return lax.dot_general(a, b, (((1,), (1,)), ((), ())), preferred_element_type=F32)


def _dot_tn(a, b):
    return lax.dot_general(a, b, (((0,), (0,)), ((), ())), preferred_element_type=F32)


def _taps(prev, cur, nxt, tm):
    hr = prev.shape[0]
    ext = jnp.concatenate([prev, cur, nxt], axis=0)
    n = tm + 2 * hr
    sl = slice(hr, hr + tm)
    return (pltpu.roll(ext, 2, 0)[sl], pltpu.roll(ext, 1, 0)[sl], cur,
            pltpu.roll(ext, n - 1, 0)[sl], pltpu.roll(ext, n - 2, 0)[sl])


def _halo_specs(tm, s, col, rows=8):
    nb = s // rows
    r = tm // rows
    return (pl.BlockSpec((rows, D), lambda i: (jnp.maximum(i * r - 1, 0), col)),
            pl.BlockSpec((tm, D), lambda i: (i, col)),
            pl.BlockSpec((rows, D), lambda i: (jnp.minimum((i + 1) * r, nb - 1), col)))


def _halo_load(prev_ref, cur_ref, next_ref, fp, fn):
    return prev_ref[...].astype(F32) * fp, cur_ref[...].astype(F32), next_ref[...].astype(F32) * fn


def _halo_flags(nt):
    i = pl.program_id(0)
    return (i > 0).astype(F32), (i < nt - 1).astype(F32)


def _full(shape):
    nd = len(shape)
    return pl.BlockSpec(shape, lambda *_: (0,) * nd)


def _resident(shape):
    nd = len(shape)
    return pl.BlockSpec(shape, lambda *_: (0,) * nd, pipeline_mode=pl.Buffered(1))


def _norm_inproj(x, g, w, layer, tm):
    s = x.shape[0]

    def body(x_ref, g_ref, w_ref, z_ref, ht_ref):
        h32 = _rms_fwd(x_ref[...], g_ref[...])
        ht_ref[...] = h32.T.astype(BF16)
        h = h32.astype(BF16)
        for j in range(N_DEV):
            z_ref[:, j * IN_BLK:(j + 1) * IN_BLK] = _dot(h, w_ref[j]).astype(BF16)

    return pl.pallas_call(
        body, name=f"norm_inproj_l{layer}", grid=(s // tm,),
        in_specs=[pl.BlockSpec((tm, D), lambda i: (i, 0)), _full((1, D)), _resident((N_DEV, D, IN_BLK))],
        out_specs=[pl.BlockSpec((tm, N_IN), lambda i: (i, 0)), pl.BlockSpec((D, tm), lambda i: (0, i))],
        out_shape=[SDS((s, N_IN), BF16), SDS((D, s), BF16)],
        compiler_params=_cp("parallel"))(x, g, w)


def _gmlp_values(zu_ref, zv_ref, lng_ref, lnb_ref):
    zu = zu_ref[...].astype(F32)
    zv = zv_ref[...].astype(F32)
    u, tu = _gelu(zu)
    gv, tv = _gelu(zv)
    xc = gv - jnp.mean(gv, axis=-1, keepdims=True)
    rstd = lax.rsqrt(jnp.mean(xc * xc, axis=-1, keepdims=True) + EPS)
    xh = xc * rstd
    vb = (xh * lng_ref[...] + lnb_ref[...]).astype(BF16)
    return zu, zv, u, tu, tv, xh, rstd, vb


def _mixer_fwd(x, h0, h1, z, lng, lnb, ws, bsb, wo, layer, tm):
    s = x.shape[0]

    def body(x_ref, h0_ref, h1_ref, zu_ref, zv_ref, zg_ref, za_ref, zb_ref, lng_ref, lnb_ref, ws_ref, bsb_ref,
             wo_ref, x1_ref, mg_ref, ya_s):
        _, _, u, _, _, _, _, vb = _gmlp_values(zu_ref, zv_ref, lng_ref, lnb_ref)
        for c in range(tm // HD):
            rs = slice(c * HD, (c + 1) * HD)
            for g in range(HEADS):
                cs = slice(g * HD, (g + 1) * HD)
                ya_s[rs, cs] = u[rs, cs] * (_dot(ws_ref[g], vb[rs, cs]) + bsb_ref[g])
        gg, _ = _gelu(zg_ref[...].astype(F32))
        yb = (h0_ref[...] + h1_ref[...]) * gg
        m32 = _sigmoid(za_ref[...].astype(F32)) * ya_s[...] + _sigmoid(zb_ref[...].astype(F32)) * yb
        mg_ref[...] = m32.T.astype(BF16)
        x1_ref[...] = x_ref[...] + _dot(m32.astype(BF16), wo_ref[...])

    tile = pl.BlockSpec((tm, D), lambda i: (i, 0))
    wspec = _full((HEADS, HD, HD))
    return pl.pallas_call(
        body, name=f"mixer_fwd_l{layer}", grid=(s // tm,),
        in_specs=[tile, tile, tile] + [pl.BlockSpec((tm, D), lambda i, c=c: (i, c)) for c in (0, 1, 3, 4, 5)]
        + [_full((1, D)), _full((1, D)), wspec, wspec, _full((D, D))],
        out_specs=[tile, pl.BlockSpec((D, tm), lambda i: (0, i))], out_shape=[SDS((s, D), F32), SDS((D, s), BF16)],
        scratch_shapes=[pltpu.VMEM((tm, D), F32)],
        compiler_params=_cp("parallel"))(x, h0, h1, z, z, z, z, z, lng, lnb, ws, bsb, wo)


def _conv(taps, cw_ref, cb_ref):
    _, m1, c0, p1, p2 = taps
    return cb_ref[...] + m1 * cw_ref[0:1, :] + c0 * cw_ref[1:2, :] + p1 * cw_ref[2:3, :] + p2 * cw_ref[3:4, :]


def _heads_dot(xb, w_ref, d):
    return jnp.concatenate([_dot(xb[:, h * HD:(h + 1) * HD], w_ref[d, h]) for h in range(HEADS)], axis=1)


def _lru_decay(r, sp):
    la = (-LRU_C) * r * sp
    a = jnp.exp(la)
    return a, jnp.tanh(-la) * (a * a + 1.0)


def _lru_gates_fwd(z, cw, cb, wr, wi, br, bi, lam, layer, tm):
    s = z.shape[0]
    nt = s // tm

    def body(zp_ref, zc_ref, zn_ref, cw_ref, cb_ref, wr_ref, wi_ref, br_ref, bi_ref, lam_ref,
             a0_ref, b0_ref, a1_ref, b1_ref, xc_ref, r0_ref, i0_ref, r1_ref, i1_ref):
        fp, fn = _halo_flags(nt)
        xc = _conv(_taps(*_halo_load(zp_ref, zc_ref, zn_ref, fp, fn), tm), cw_ref, cb_ref)
        xb = xc.astype(BF16)
        xc_ref[...] = xb
        for d, (a_ref, b_ref, r_ref, i_ref) in enumerate(((a0_ref, b0_ref, r0_ref, i0_ref),
                                                          (a1_ref, b1_ref, r1_ref, i1_ref))):
            r = _sigmoid(_heads_dot(xb, wr_ref, d) + br_ref[d:d + 1, :])
            ig = _sigmoid(_heads_dot(xb, wi_ref, d) + bi_ref[d:d + 1, :])
            a, q = _lru_decay(r, _softplus(-lam_ref[d:d + 1, :]))
            a_ref[...] = a
            b_ref[...] = jnp.sqrt(q) * (ig * xc)
            r_ref[...] = r.astype(BF16)
            i_ref[...] = ig.astype(BF16)

    tile = pl.BlockSpec((tm, D), lambda i: (i, 0))
    return pl.pallas_call(
        body, name=f"lru_gates_fwd_l{layer}", grid=(nt,),
        in_specs=[*_halo_specs(tm, s, 2, 16), _full((4, D)), _full((1, D)),
                  _full((2, HEADS, HD, HD)), _full((2, HEADS, HD, HD)), _full((2, D)), _full((2, D)), _full((2, D))],
        out_specs=[tile] * 9, out_shape=[SDS((s, D), F32)] * 4 + [SDS((s, D), BF16)] * 5,
        compiler_params=_cp("parallel"))(z, z, z, cw, cb, wr, wi, br, bi, lam)


ZX0 = 2 * D
ZX_SPLIT = 3 * IN_BLK - ZX0


def _inproj_gates_fwd(x, g, w, cw, cb, wr, wi, br, bi, lam, layer, tm):
    s = x.shape[0]
    nt = s // tm

    def body(xp_ref, xc_ref, xn_ref, g_ref, w_ref, cw_ref, cb_ref, wr_ref, wi_ref, br_ref, bi_ref, lam_ref,
             z_ref, ht_ref, a0_ref, b0_ref, a1_ref, b1_ref, xcb_ref, r0_ref, i0_ref, r1_ref, i1_ref):
        fp, fn = _halo_flags(nt)
        he32 = _rms_fwd(jnp.concatenate([xp_ref[...], xc_ref[...], xn_ref[...]], axis=0), g_ref[...])
        h32 = he32[8:8 + tm]
        ht_ref[...] = h32.T.astype(BF16)
        h = h32.astype(BF16)
        he = he32.astype(BF16)
        zx = jnp.concatenate([_dot(he, w_ref[2, :, IN_BLK - ZX_SPLIT:IN_BLK]), _dot(he, w_ref[3])], axis=1)
        z_ref[:, ZX0:ZX0 + D] = zx[8:8 + tm].astype(BF16)
        zx = zx.astype(BF16).astype(F32)
        def z_blocks(blocks):
            for j in blocks:
                z_ref[:, j * IN_BLK:(j + 1) * IN_BLK] = _dot(h, w_ref[j]).astype(BF16)

        z_blocks((0, 1))
        xconv = _conv(_taps(zx[0:8] * fp, zx[8:8 + tm], zx[8 + tm:] * fn, tm), cw_ref, cb_ref)
        xb = xconv.astype(BF16)
        xcb_ref[...] = xb
        for d, (a_ref, b_ref, r_ref, i_ref) in enumerate(((a0_ref, b0_ref, r0_ref, i0_ref),
                                                          (a1_ref, b1_ref, r1_ref, i1_ref))):
            pre_r = _heads_dot(xb, wr_ref, d)
            pre_i = _heads_dot(xb, wi_ref, d)
            z_blocks((4, 5) if d == 0 else (6, 7))
            r = _sigmoid(pre_r + br_ref[d:d + 1, :])
            ig = _sigmoid(pre_i + bi_ref[d:d + 1, :])
            a, q = _lru_decay(r, _softplus(-lam_ref[d:d + 1, :]))
            a_ref[...] = a
            b_ref[...] = jnp.sqrt(q) * (ig * xconv)
            r_ref[...] = r.astype(BF16)
            i_ref[...] = ig.astype(BF16)
        z_ref[:, 2 * IN_BLK:ZX0] = _dot(h, w_ref[2, :, 0:IN_BLK - ZX_SPLIT]).astype(BF16)

    tile = pl.BlockSpec((tm, D), lambda i: (i, 0))
    return pl.pallas_call(
        body, name=f"inproj_gates_fwd_l{layer}", grid=(nt,),
        in_specs=[*_halo_specs(tm, s, 0), _full((1, D)), _resident((N_DEV, D, IN_BLK)), _full((4, D)), _full((1, D)),
                  _full((2, HEADS, HD, HD)), _full((2, HEADS, HD, HD)), _full((2, D)), _full((2, D)), _full((2, D))],
        out_specs=[pl.BlockSpec((tm, N_IN), lambda i: (i, 0)), pl.BlockSpec((D, tm), lambda i: (0, i))] + [tile] * 9,
        out_shape=[SDS((s, N_IN), BF16), SDS((D, s), BF16)] + [SDS((s, D), F32)] * 4 + [SDS((s, D), BF16)] * 5,
        compiler_params=_cp("parallel"))(x, x, x, g, w, cw, cb, wr, wi, br, bi, lam)


def _scan_group(a, x, c, reverse, bwd):
    row = lax.broadcasted_iota(jnp.int32, a.shape, 0)
    b = a * x if bwd else x
    for d in (1, 2, 4):
        keep = (row < 8 - d) if reverse else (row >= d)
        sh = 8 - d if reverse else d
        a_s = jnp.where(keep, pltpu.roll(a, sh, 0), 1.0)
        b_s = jnp.where(keep, pltpu.roll(b, sh, 0), 0.0)
        b = a * b_s + b
        a = a * a_s
    h = b + a * c
    new_c = h[0:1, :] if reverse else h[7:8, :]
    if not bwd:
        return h, new_c
    if reverse:
        prev = jnp.where(row < 7, pltpu.roll(h, 7, 0), c)
    else:
        prev = jnp.where(row >= 1, pltpu.roll(h, 1, 0), c)
    return x + prev, new_c


def _lru_scan(a_f, x_f, a_r, x_r, bwd, layer):
    s = a_f.shape[0]
    ts = min(1024, s // 2)
    cb = 512
    nt = s // ts
    ng = ts // 8

    def body(af_ref, xf_ref, ar_ref, xr_ref, of_ref, or_ref, cf, cr):
        @pl.when(pl.program_id(1) == 0)
        def _():
            cf[...] = jnp.zeros_like(cf)
            cr[...] = jnp.zeros_like(cr)

        def step(j, carry):
            c_f, c_r = carry
            rf = pl.multiple_of(j * 8, 8)
            rr = pl.multiple_of((ng - 1 - j) * 8, 8)
            o, c_f = _scan_group(af_ref[pl.ds(rf, 8), :], xf_ref[pl.ds(rf, 8), :], c_f, False, bwd)
            of_ref[pl.ds(rf, 8), :] = o
            o, c_r = _scan_group(ar_ref[pl.ds(rr, 8), :], xr_ref[pl.ds(rr, 8), :], c_r, True, bwd)
            or_ref[pl.ds(rr, 8), :] = o
            return c_f, c_r

        c_f, c_r = lax.fori_loop(0, ng, step, (cf[0:1, :], cr[0:1, :]), unroll=2)
        cf[...] = jnp.broadcast_to(c_f, cf.shape)
        cr[...] = jnp.broadcast_to(c_r, cr.shape)

    fwd = pl.BlockSpec((ts, cb), lambda c, t: (t, c))
    rev = pl.BlockSpec((ts, cb), lambda c, t: (nt - 1 - t, c))
    return pl.pallas_call(
        body, name=f"lru_scan_{'bwd' if bwd else 'fwd'}_l{layer}", grid=(D // cb, nt),
        in_specs=[fwd, fwd, rev, rev], out_specs=[fwd, rev],
        out_shape=[SDS((s, D), F32)] * 2,
        scratch_shapes=[pltpu.VMEM((8, cb), F32), pltpu.VMEM((8, cb), F32)],
        compiler_params=_cp("parallel", "arbitrary"))(a_f, x_f, a_r, x_r)


def _ffn_fwd(x1, g, wfi, wfo, layer, tm, head=None):
    s = x1.shape[0]

    def ffn(x_ref, g_ref, wi_ref, wo_ref, ff_ref, dff_ref, h_ref):
        x = x_ref[...]
        h = _rms_fwd(x, g_ref[...]).astype(BF16)
        h_ref[...] = h
        acc = x
        for k in range(4):
            gate = _dot_nt(h, wi_ref[0, k])
            up = _dot_nt(h, wi_ref[1, k])
            sg = _sigmoid(gate)
            silu = gate * sg
            ff = (silu * up).astype(BF16)
            ff_ref[k] = ff
            dff_ref[0, k] = (up * (sg * (1.0 + gate * (1.0 - sg)))).astype(BF16)
            dff_ref[1, k] = silu.astype(BF16)
            acc = acc + _dot(ff, wo_ref[k])
        return acc

    def body(x_ref, g_ref, wi_ref, wo_ref, x2_ref, ff_ref, dff_ref, h_ref):
        x2_ref[...] = ffn(x_ref, g_ref, wi_ref, wo_ref, ff_ref, dff_ref, h_ref)

    def body_with_head(x_ref, g_ref, wi_ref, wo_ref, fg_ref, t_ref, dx_ref, loss_ref, dfg_ref, ff_ref, dff_ref, h_ref):
        @pl.when(pl.program_id(0) == 0)
        def _():
            loss_ref[...] = jnp.zeros_like(loss_ref)
            dfg_ref[...] = jnp.zeros_like(dfg_ref)

        x2 = ffn(x_ref, g_ref, wi_ref, wo_ref, ff_ref, dff_ref, h_ref)
        fg = fg_ref[...]
        e = _rms_fwd(x2, fg) - t_ref[...]
        rows = jnp.sum(e * e, axis=-1, keepdims=True)
        loss_ref[...] += (0.5 / D) * jnp.sum(rows, axis=0, keepdims=True)
        dx, dg = _rms_bwd(x2, fg, e * (1.0 / D))
        dx_ref[...] = dx
        _add_rows128(dfg_ref, dg)

    tile = pl.BlockSpec((tm, D), lambda i: (i, 0))
    weights = [_resident((2, 4, FF_BLK, D)), _resident((4, FF_BLK, D))]
    kept_specs = [pl.BlockSpec((4, tm, FF_BLK), lambda i: (0, i, 0)),
                  pl.BlockSpec((2, 4, tm, FF_BLK), lambda i: (0, 0, i, 0)), tile]
    kept_shapes = [SDS((4, s, FF_BLK), BF16), SDS((2, 4, s, FF_BLK), BF16), SDS((s, D), BF16)]
    if head is None:
        return pl.pallas_call(
            body, name=f"ffn_fwd_l{layer}", grid=(s // tm,),
            in_specs=[tile, _full((1, D))] + weights, out_specs=[tile] + kept_specs,
            out_shape=[SDS((s, D), F32)] + kept_shapes, compiler_params=_cp("parallel"))(x1, g, wfi, wfo)
    final_g, tgt = head
    return pl.pallas_call(
        body_with_head, name=f"ffn_fwd_loss_l{layer}", grid=(s // tm,),
        in_specs=[tile, _full((1, D))] + weights + [_full((1, D)), tile],
        out_specs=[tile, _full((1, 1)), _full((LANE_ROWS, HD))] + kept_specs,
        out_shape=[SDS((s, D), F32), SDS((1, 1), F32), SDS((LANE_ROWS, HD), F32)] + kept_shapes,
        compiler_params=_cp("arbitrary"))(x1, g, wfi, wfo, final_g, tgt)


def _ffn_bwd(dx2, wfo, factors, wfi, x1, g, layer, tm):
    s = dx2.shape[0]

    def body(dx_ref, wo_ref, f_ref, wi_ref, x_ref, g_ref, dgu_ref, dx1_ref, dg_ref):
        @pl.when(pl.program_id(0) == 0)
        def _():
            dg_ref[...] = jnp.zeros_like(dg_ref)

        dx = dx_ref[...]
        dxb = dx.astype(BF16)
        dh = None
        for k in range(4):
            dff = _dot_nt(dxb, wo_ref[k])
            d_gate = (dff * f_ref[0, k].astype(F32)).astype(BF16)
            d_up = (dff * f_ref[1, k].astype(F32)).astype(BF16)
            dgu_ref[0, k] = d_gate
            dgu_ref[1, k] = d_up
            part = _dot(d_gate, wi_ref[k]) + _dot(d_up, wi_ref[4 + k])
            dh = part if dh is None else dh + part
        dxn, dg = _rms_bwd(x_ref[...], g_ref[...], dh)
        dx1_ref[...] = dx + dxn
        _add_rows128(dg_ref, dg)

    tile = pl.BlockSpec((tm, D), lambda i: (i, 0))
    blk = pl.BlockSpec((2, 4, tm, FF_BLK), lambda i: (0, 0, i, 0))
    return pl.pallas_call(
        body, name=f"ffn_bwd_l{layer}", grid=(s // tm,),
        in_specs=[tile, _resident((4, FF_BLK, D)), blk, _resident((N_DEV, FF_BLK, D)), tile, _full((1, D))],
        out_specs=[blk, tile, _full((LANE_ROWS, HD))],
        out_shape=[SDS((2, 4, s, FF_BLK), BF16), SDS((s, D), F32), SDS((LANE_ROWS, HD), F32)],
        compiler_params=_cp("arbitrary"))(dx2, wfo, factors, wfi, x1, g)


def _mm_nt_rms_bwd(a, a_spec, a_blocks, w, w_is_transposed, x, g, dres, name, tm):
    s = x.shape[0]

    def body(a_ref, w_ref, x_ref, g_ref, dres_ref, dx_ref, dg_ref):
        @pl.when(pl.program_id(0) == 0)
        def _():
            dg_ref[...] = jnp.zeros_like(dg_ref)

        dh = None
        for k, blk in enumerate(a_blocks(a_ref)):
            part = _dot(blk, w_ref[k]) if w_is_transposed else _dot_nt(blk, w_ref[k])
            dh = part if dh is None else dh + part
        dx, dg = _rms_bwd(x_ref[...], g_ref[...], dh)
        dx_ref[...] = dres_ref[...] + dx
        _add_rows128(dg_ref, dg)

    tile = pl.BlockSpec((tm, D), lambda i: (i, 0))
    return pl.pallas_call(
        body, name=name, grid=(s // tm,),
        in_specs=[a_spec, _resident(w.shape), tile, _full((1, D)), tile],
        out_specs=[tile, _full((LANE_ROWS, HD))], out_shape=[SDS((s, D), F32), SDS((LANE_ROWS, HD), F32)],
        compiler_params=_cp("arbitrary"))(a, w, x, g, dres)


def _mm_tn(a, a_spec, b, b_spec, nb, out_shape, out_spec, name, a_is_transposed=True, after=None):
    def body(a_ref, b_ref, *rest):
        o_ref = rest[-1]
        bb = b_ref[...].astype(BF16)
        o_ref[...] = (_dot(a_ref[...], bb) if a_is_transposed else _dot_tn(a_ref[...], bb)).astype(BF16)

    deps = [] if after is None else [after]
    return pl.pallas_call(
        body, name=name, grid=(nb,), in_specs=[a_spec, b_spec] + [_ANY] * len(deps), out_specs=out_spec,
        out_shape=SDS(out_shape, BF16), compiler_params=_cp("parallel"))(a, b, *deps)


def _mixer_bwd(dx1, wo, h0, h1, z, lng, lnb, ws, wst, bsb, layer, tm):
    s = dx1.shape[0]
    nt = s // tm

    def body(dx_ref, wo_ref, h0_ref, h1_ref, zu_ref, zv_ref, zg_ref, za_ref, zb_ref, lng_ref, lnb_ref,
             ws_ref, wst_ref, bsb_ref, dz_ref, dh_ref, dws_ref, dbs_ref, dlng_ref, dlnb_ref,
             du_s, dv_s, ya_s, dbs_acc):
        i = pl.program_id(0)

        @pl.when(i == 0)
        def _():
            for r in (dws_ref, dlng_ref, dlnb_ref, dbs_acc):
                r[...] = jnp.zeros_like(r)

        dm = _dot_nt(dx_ref[...].astype(BF16), wo_ref[...])
        sa = _sigmoid(za_ref[...].astype(F32))
        sb = _sigmoid(zb_ref[...].astype(F32))
        zg = zg_ref[...].astype(F32)
        gg, tg = _gelu(zg)
        hs = h0_ref[...] + h1_ref[...]
        dyb = dm * sb
        dya = dm * sa
        dh_ref[...] = dyb * gg
        dz_ref[:, 2 * D:3 * D] = jnp.zeros((tm, D), BF16)
        dz_ref[:, 3 * D:4 * D] = (dyb * hs * _gelu_grad(zg, tg)).astype(BF16)
        dz_ref[:, 5 * D:6 * D] = (dm * (hs * gg) * (sb * (1.0 - sb))).astype(BF16)

        zu, zv, u, tu, tv, xh, rstd, vb = _gmlp_values(zu_ref, zv_ref, lng_ref, lnb_ref)
        for c in range(tm // HD):
            rs = slice(c * HD, (c + 1) * HD)
            for g in range(HEADS):
                cs = slice(g * HD, (g + 1) * HD)
                vblk = vb[rs, cs]
                mixed = _dot(ws_ref[g], vblk) + bsb_ref[g]
                ya_s[rs, cs] = u[rs, cs] * mixed
                du_s[rs, cs] = dya[rs, cs] * mixed
                dmx = dya[rs, cs] * u[rs, cs]
                dbs_acc[g] += dmx
                dmxb = dmx.astype(BF16)
                dws_ref[g] += _dot_nt(dmxb, vblk)
                dv_s[rs, cs] = _dot(wst_ref[g], dmxb)
        dz_ref[:, 4 * D:5 * D] = (dm * ya_s[...] * (sa * (1.0 - sa))).astype(BF16)
        dv = dv_s[...]
        _add_rows128(dlng_ref, jnp.sum(dv * xh, axis=0, keepdims=True))
        _add_rows128(dlnb_ref, jnp.sum(dv, axis=0, keepdims=True))
        dxh = dv * lng_ref[...]
        dgv = rstd * (dxh - jnp.mean(dxh, axis=-1, keepdims=True)
                      - xh * jnp.mean(dxh * xh, axis=-1, keepdims=True))
        dz_ref[:, 0:D] = (du_s[...] * _gelu_grad(zu, tu)).astype(BF16)
        dz_ref[:, D:2 * D] = (dgv * _gelu_grad(zv, tv)).astype(BF16)

        @pl.when(i == nt - 1)
        def _():
            for g in range(HEADS):
                dbs_ref[g:g + 1, :] = jnp.sum(dbs_acc[g].T, axis=0, keepdims=True)

    tile = pl.BlockSpec((tm, D), lambda i: (i, 0))
    wspec = _full((HEADS, HD, HD))
    return pl.pallas_call(
        body, name=f"mixer_bwd_l{layer}", grid=(nt,),
        in_specs=[tile, _full((D, D)), tile, tile]
        + [pl.BlockSpec((tm, D), lambda i, c=c: (i, c)) for c in (0, 1, 3, 4, 5)]
        + [_full((1, D)), _full((1, D)), wspec, wspec, wspec],
        out_specs=[pl.BlockSpec((tm, N_IN), lambda i: (i, 0)), tile, wspec, _full((HEADS, HD)),
                   _full((LANE_ROWS, HD)), _full((LANE_ROWS, HD))],
        out_shape=[SDS((s, N_IN), BF16), SDS((s, D), F32), SDS((HEADS, HD, HD), F32), SDS((HEADS, HD), F32),
                   SDS((LANE_ROWS, HD), F32), SDS((LANE_ROWS, HD), F32)],
        scratch_shapes=[pltpu.VMEM((tm, D), F32)] * 3 + [pltpu.VMEM((HEADS, HD, HD), F32)],
        compiler_params=_cp("arbitrary"))(dx1, wo, h0, h1, z, z, z, z, z, lng, lnb, ws, wst, bsb)


def _lru_gates_bwd(xcb, gates, h0, h1, g0, g1, wr, wi, lam, layer, tm):
    s = xcb.shape[0]
    nt = s // tm

    def body(xc_ref, r0_ref, i0_ref, r1_ref, i1_ref, h0p_ref, h0_ref, h1_ref, h1n_ref, g0_ref, g1_ref,
             wr_ref, wi_ref, lam_ref, dxc_ref, dwr_ref, dwi_ref, dbr_ref, dbi_ref, dlam_ref):
        i = pl.program_id(0)
        fp, fn = _halo_flags(nt)

        @pl.when(i == 0)
        def _():
            for r in (dwr_ref, dwi_ref, dbr_ref, dbi_ref, dlam_ref):
                r[...] = jnp.zeros_like(r)

        xb = xc_ref[...]
        xc = xb.astype(F32)
        zeros8 = jnp.zeros((8, D), F32)
        h_prev = _taps(h0p_ref[...] * fp, h0_ref[...], zeros8, tm)[1]
        h_next = _taps(zeros8, h1_ref[...], h1n_ref[...] * fn, tm)[3]
        dxc = jnp.zeros((tm, D), F32)
        for d, (g_ref, hsh, r_ref, i_ref) in enumerate(((g0_ref, h_prev, r0_ref, i0_ref),
                                                        (g1_ref, h_next, r1_ref, i1_ref))):
            sp = _softplus(-lam_ref[d:d + 1, :])
            r = r_ref[...].astype(F32)
            ig = i_ref[...].astype(F32)
            a, q = _lru_decay(r, sp)
            rmult = jnp.where(q > 0.0, lax.rsqrt(jnp.where(q > 0.0, q, 1.0)), 0.0)
            mult = q * rmult
            db = g_ref[...]
            da = db * hsh
            dmult = db * (ig * xc)
            di = db * (mult * xc)
            dxc = dxc + db * (mult * ig)
            dla = da * a - dmult * (a * a * rmult)
            dsp_dlam = -_sigmoid(-lam_ref[d:d + 1, :])
            _add_rows128(dlam_ref, jnp.sum(dla * r, axis=0, keepdims=True) * ((-LRU_C) * dsp_dlam), d * LANE_ROWS)
            dpr = dla * sp * (-LRU_C) * (r * (1.0 - r))
            dpi = di * (ig * (1.0 - ig))
            _add_rows128(dbr_ref, jnp.sum(dpr, axis=0, keepdims=True), d * LANE_ROWS)
            _add_rows128(dbi_ref, jnp.sum(dpi, axis=0, keepdims=True), d * LANE_ROWS)
            dprb = dpr.astype(BF16)
            dpib = dpi.astype(BF16)
            parts = []
            for h in range(HEADS):
                cs = slice(h * HD, (h + 1) * HD)
                dwr_ref[d, h] += _dot_tn(xb[:, cs], dprb[:, cs])
                dwi_ref[d, h] += _dot_tn(xb[:, cs], dpib[:, cs])
                parts.append(_dot_nt(dprb[:, cs], wr_ref[d, h]) + _dot_nt(dpib[:, cs], wi_ref[d, h]))
            dxc = dxc + jnp.concatenate(parts, axis=1)
        dxc_ref[...] = dxc.astype(BF16)

    tile = pl.BlockSpec((tm, D), lambda i: (i, 0))
    hp, hc, hn = _halo_specs(tm, s, 0)
    wspec = _full((2, HEADS, HD, HD))
    vspec = _full((2 * LANE_ROWS, HD))
    return pl.pallas_call(
        body, name=f"lru_gates_bwd_l{layer}", grid=(nt,),
        in_specs=[tile] * 5 + [hp, hc, hc, hn, tile, tile, wspec, wspec, _full((2, D))],
        out_specs=[tile, wspec, wspec, vspec, vspec, vspec],
        out_shape=[SDS((s, D), BF16), SDS((2, HEADS, HD, HD), F32), SDS((2, HEADS, HD, HD), F32)]
        + [SDS((2 * LANE_ROWS, HD), F32)] * 3,
        compiler_params=_cp("arbitrary"))(xcb, *gates, h0, h0, h1, h1, g0, g1, wr, wi, lam)


def _conv_bwd(dz, dxc, z, cw, layer, tm):
    s = z.shape[0]
    nt = s // tm

    def body(dz_in, dp_ref, dc_ref, dn_ref, zp_ref, zc_ref, zn_ref, cw_ref, dz_ref, dcw_ref, dcb_ref):
        del dz_in
        fp, fn = _halo_flags(nt)

        @pl.when(pl.program_id(0) == 0)
        def _():
            dcw_ref[...] = jnp.zeros_like(dcw_ref)
            dcb_ref[...] = jnp.zeros_like(dcb_ref)

        dxc_halo = _halo_load(dp_ref, dc_ref, dn_ref, fp, fn)
        dxc = dxc_halo[1]
        dm2, dm1, _, dp1, _ = _taps(*dxc_halo, tm)
        dz_ref[...] = (cw_ref[0:1, :] * dp1 + cw_ref[1:2, :] * dxc + cw_ref[2:3, :] * dm1
                       + cw_ref[3:4, :] * dm2).astype(BF16)
        _, zm1, z0, zp1, zp2 = _taps(*_halo_load(zp_ref, zc_ref, zn_ref, fp, fn), tm)
        for k, zt in enumerate((zm1, z0, zp1, zp2)):
            _add_rows128(dcw_ref, jnp.sum(dxc * zt, axis=0, keepdims=True), k * LANE_ROWS)
        _add_rows128(dcb_ref, jnp.sum(dxc, axis=0, keepdims=True))

    return pl.pallas_call(
        body, name=f"conv_bwd_l{layer}", grid=(nt,),
        in_specs=[pl.BlockSpec(memory_space=pl.ANY), *_halo_specs(tm, s, 0, 16), *_halo_specs(tm, s, 2, 16),
                  _full((4, D))],
        out_specs=[pl.BlockSpec((tm, D), lambda i: (i, 2)), _full((4 * LANE_ROWS, HD)), _full((LANE_ROWS, HD))],
        out_shape=[SDS((s, N_IN), BF16), SDS((4 * LANE_ROWS, HD), F32), SDS((LANE_ROWS, HD), F32)],
        input_output_aliases={0: 0},
        compiler_params=_cp("arbitrary"))(dz, dxc, dxc, dxc, z, z, z, cw)


def _me():
    return lax.axis_index("x"), lax.axis_index("y"), lax.axis_index("c")


def _peer(m):
    x, y, c = _me()
    px = 1 - x if m & 4 else x
    py = 1 - y if m & 2 else y
    pc = 1 - c if m & 1 else c
    return (px, py, pc), 4 * px + 2 * py + pc


_ANY = pl.BlockSpec(memory_space=pl.ANY)
_EXCHANGE_SEMS = [pltpu.SemaphoreType.DMA((N_DEV - 1,)), pltpu.SemaphoreType.DMA((N_DEV - 1,)), pltpu.SemaphoreType.DMA(())]


def _all_gather(v, after, name):
    def body(v_ref, after_ref, o_ref, send_sems, recv_sems, local_sem):
        del after_ref
        x, y, c = _me()
        me = 4 * x + 2 * y + c
        local = pltpu.make_async_copy(v_ref, o_ref.at[me], local_sem)
        local.start()
        sends = []
        for m in range(1, N_DEV):
            dev, _ = _peer(m)
            cp = pltpu.make_async_remote_copy(v_ref, o_ref.at[me], send_sems.at[m - 1], recv_sems.at[m - 1],
                                              device_id=dev, device_id_type=pl.DeviceIdType.MESH)
            cp.start()
            sends.append(cp)
        for m in range(1, N_DEV):
            dev, blk = _peer(m)
            pltpu.make_async_remote_copy(v_ref, o_ref.at[blk], send_sems.at[m - 1], recv_sems.at[m - 1],
                                         device_id=dev, device_id_type=pl.DeviceIdType.MESH).wait_recv()
        for cp in sends:
            cp.wait_send()
        local.wait()

    return pl.pallas_call(
        body, name=name, in_specs=[_ANY, _ANY], out_specs=_ANY,
        out_shape=SDS((N_DEV,) + v.shape, v.dtype), scratch_shapes=_EXCHANGE_SEMS)(v, after)


_HBM = pl.BlockSpec(memory_space=pltpu.HBM)
_SEM = pl.BlockSpec(memory_space=pltpu.SEMAPHORE)
_EFFECT = pltpu.CompilerParams(has_side_effects=pltpu.SideEffectType.DATAFLOW_SIDE_EFFECTING)
_PEER_SEMS = pltpu.SemaphoreType.DMA((N_DEV - 1,))


def _in_hbm(a):
    return pltpu.with_memory_space_constraint(a, pltpu.HBM)


def _remote(src, dst, send_sems, recv_sems, m):
    dev, _ = _peer(m)
    return pltpu.make_async_remote_copy(src, dst, send_sems.at[m - 1], recv_sems.at[m - 1],
                                        device_id=dev, device_id_type=pl.DeviceIdType.MESH)


def _gather_start(lands, after, name):
    n = len(lands)

    def body(*refs):
        land = refs[:n]
        sems = refs[n + 1:3 * n + 1]
        token = refs[-1]
        x, y, c = _me()
        me = 4 * x + 2 * y + c
        for t in range(n):
            for m in range(1, N_DEV):
                _remote(land[t].at[me], land[t].at[me], sems[2 * t], sems[2 * t + 1], m).start()
        token[...] = jnp.zeros_like(token)

    res = pl.pallas_call(
        body, name=name, in_specs=[_HBM] * n + [_ANY],
        out_specs=[_SEM] * (2 * n) + [_HBM] * n + [pl.BlockSpec(memory_space=pltpu.VMEM)],
        out_shape=[_PEER_SEMS] * (2 * n) + [pltpu.HBM(a.shape, a.dtype) for a in lands] + [SDS((8, 128), F32)],
        input_output_aliases={t: 2 * n + t for t in range(n)},
        compiler_params=_EFFECT)(*[_in_hbm(a) for a in lands], after)
    return [(res[2 * t], res[2 * t + 1], res[2 * n + t]) for t in range(n)], res[-1]


def _gather_wait(handle, after, name):
    send_sems, recv_sems, land = handle

    def body(land_ref, ssem, rsem, after_ref, out_ref):
        del after_ref, out_ref
        x, y, c = _me()
        me = 4 * x + 2 * y + c
        for m in range(1, N_DEV):
            _, blk = _peer(m)
            cp = _remote(land_ref.at[me], land_ref.at[blk], ssem, rsem, m)
            cp.wait_send()
            cp.wait_recv()

    return pl.pallas_call(
        body, name=name, in_specs=[_HBM, _SEM, _SEM, _ANY], out_specs=_HBM,
        out_shape=pltpu.HBM(land.shape, land.dtype), input_output_aliases={0: 0},
        compiler_params=_EFFECT)(land, send_sems, recv_sems, after)


FIRST_STAGE = (1, 2, 4, 6)
RELAYED = (2, 4, 6)
OTHER_CORE = 1


def _stage_copy(src, dst, send_sems, recv_sems, k, m):
    dev, _ = _peer(m)
    return pltpu.make_async_remote_copy(src, dst, send_sems.at[k], recv_sems.at[k],
                                        device_id=dev, device_id_type=pl.DeviceIdType.MESH)


def _gather2_start(lands, after, name):
    n = len(lands)

    def body(*refs):
        land = refs[:n]
        sems = refs[n + 1:3 * n + 1]
        token = refs[-1]
        x, y, c = _me()
        me = 4 * x + 2 * y + c
        for t in range(n):
            for k, m in enumerate(FIRST_STAGE):
                _stage_copy(land[t].at[me], land[t].at[me], sems[2 * t], sems[2 * t + 1], k, m).start()
        token[...] = jnp.zeros_like(token)

    stage_sems = pltpu.SemaphoreType.DMA((len(FIRST_STAGE),))
    res = pl.pallas_call(
        body, name=name, in_specs=[_HBM] * n + [_ANY],
        out_specs=[_SEM] * (2 * n) + [_HBM] * n + [pl.BlockSpec(memory_space=pltpu.VMEM)],
        out_shape=[stage_sems] * (2 * n) + [pltpu.HBM(a.shape, a.dtype) for a in lands] + [SDS((8, 128), F32)],
        input_output_aliases={t: 2 * n + t for t in range(n)},
        compiler_params=_EFFECT)(*[_in_hbm(a) for a in lands], after)
    return [(res[2 * t], res[2 * t + 1], res[2 * n + t]) for t in range(n)], res[-1]


def _gather2_relay(handles, after, name):
    n = len(handles)

    def body(*refs):
        land, send1, recv1 = refs[:n], refs[n:2 * n], refs[2 * n:3 * n]
        sems = refs[3 * n + 1:5 * n + 1]
        token = refs[-1]
        x, y, c = _me()
        me = 4 * x + 2 * y + c
        for t in range(n):
            for j, m in enumerate(RELAYED):
                _, blk = _peer(m)
                _stage_copy(land[t].at[me], land[t].at[blk], send1[t], recv1[t], 1 + j, m).wait_recv()
                _stage_copy(land[t].at[blk], land[t].at[blk], sems[2 * t], sems[2 * t + 1], j, OTHER_CORE).start()
        token[...] = jnp.zeros_like(token)

    relay_sems = pltpu.SemaphoreType.DMA((len(RELAYED),))
    lands = [h[2] for h in handles]
    res = pl.pallas_call(
        body, name=name, in_specs=[_HBM] * n + [_SEM] * (2 * n) + [_ANY],
        out_specs=[_SEM] * (2 * n) + [_HBM] * n + [pl.BlockSpec(memory_space=pltpu.VMEM)],
        out_shape=[relay_sems] * (2 * n) + [pltpu.HBM(a.shape, a.dtype) for a in lands] + [SDS((8, 128), F32)],
        input_output_aliases={t: 2 * n + t for t in range(n)},
        compiler_params=_EFFECT)(*lands, *[h[0] for h in handles], *[h[1] for h in handles], after)
    return [(h[0], h[1], res[2 * t], res[2 * t + 1], res[2 * n + t]) for t, h in enumerate(handles)], res[-1]


def _gather2_wait(handle, after, name):
    send1, recv1, send2, recv2, land = handle

    def body(land_ref, s1, r1, s2, r2, after_ref, out_ref):
        del after_ref, out_ref
        x, y, c = _me()
        me = 4 * x + 2 * y + c
        _, other = _peer(OTHER_CORE)
        _stage_copy(land_ref.at[me], land_ref.at[other], s1, r1, 0, OTHER_CORE).wait_recv()
        for k, m in enumerate(FIRST_STAGE):
            _stage_copy(land_ref.at[me], land_ref.at[me], s1, r1, k, m).wait_send()
        for j, m in enumerate(RELAYED):
            _, mine = _peer(m)
            _, theirs = _peer(m ^ OTHER_CORE)
            _stage_copy(land_ref.at[mine], land_ref.at[mine], s2, r2, j, OTHER_CORE).wait_send()
            _stage_copy(land_ref.at[mine], land_ref.at[theirs], s2, r2, j, OTHER_CORE).wait_recv()

    return pl.pallas_call(
        body, name=name, in_specs=[_HBM] + [_SEM] * 4 + [_ANY], out_specs=_HBM,
        out_shape=pltpu.HBM(land.shape, land.dtype), input_output_aliases={0: 0},
        compiler_params=_EFFECT)(land, send1, recv1, send2, recv2, after)


def _exchange_start(ps, name):
    n = len(ps)

    def body(*refs):
        p = refs[:n]
        got = refs[n:2 * n]
        sems = refs[2 * n:5 * n]
        token = refs[-1]
        x, y, c = _me()
        me = 4 * x + 2 * y + c
        for t in range(n):
            pltpu.make_async_copy(p[t].at[me], got[t].at[me], sems[3 * t + 2]).start()
            for m in range(1, N_DEV):
                _, blk = _peer(m)
                _remote(p[t].at[blk], got[t].at[me], sems[3 * t], sems[3 * t + 1], m).start()
        token[...] = jnp.zeros_like(token)

    res = pl.pallas_call(
        body, name=name, in_specs=[_HBM] * (2 * n),
        out_specs=[_SEM] * (3 * n) + [_HBM] * (2 * n) + [pl.BlockSpec(memory_space=pltpu.VMEM)],
        out_shape=[_PEER_SEMS, _PEER_SEMS, pltpu.SemaphoreType.DMA(())] * n
        + [pltpu.HBM(a.shape, a.dtype) for a in ps] * 2 + [SDS((8, 128), F32)],
        input_output_aliases={t: 3 * n + t for t in range(2 * n)},
        compiler_params=_EFFECT)(*[_in_hbm(a) for a in ps], *[_in_hbm(lax.empty(a.shape, a.dtype)) for a in ps])
    return [(res[3 * t], res[3 * t + 1], res[3 * t + 2], res[3 * n + t], res[4 * n + t]) for t in range(n)], res[-1]


def _exchange_wait(handle, after, name):
    send_sems, recv_sems, local_sem, p, got = handle

    def body(p_ref, got_ref, ssem, rsem, lsem, after_ref, p_out, got_out):
        del after_ref, p_out, got_out
        x, y, c = _me()
        me = 4 * x + 2 * y + c
        pltpu.make_async_copy(p_ref.at[me], got_ref.at[me], lsem).wait()
        for m in range(1, N_DEV):
            _, blk = _peer(m)
            cp = _remote(p_ref.at[blk], got_ref.at[blk], ssem, rsem, m)
            cp.wait_send()
            cp.wait_recv()

    return pl.pallas_call(
        body, name=name, in_specs=[_HBM, _HBM, _SEM, _SEM, _SEM, _ANY], out_specs=[_HBM, _HBM],
        out_shape=[pltpu.HBM(p.shape, p.dtype), pltpu.HBM(got.shape, got.dtype)],
        input_output_aliases={0: 0, 1: 1}, compiler_params=_EFFECT)(p, got, send_sems, recv_sems, local_sem, after)[1]


def _cast_into_slot(w, layer, me1, name):
    _, r, c = w.shape
    tr = next(t for t in (256, 352, r) if r % t == 0)

    def body(me_ref, w_ref, o_ref):
        del me_ref
        o_ref[...] = w_ref[...].astype(BF16)

    return pl.pallas_call(
        body, name=name,
        grid_spec=pltpu.PrefetchScalarGridSpec(
            num_scalar_prefetch=1, grid=(r // tr,),
            in_specs=[pl.BlockSpec((None, tr, c), lambda i, me: (layer, i, 0))],
            out_specs=pl.BlockSpec((None, tr, c), lambda i, me: (me[0], i, 0))),
        out_shape=SDS((N_DEV, r, c), BF16), compiler_params=_cp("arbitrary"))(me1, w)


def _cast_all_into_slots(ws, layers, me1, after, name):
    n = len(ws)

    def body(me_ref, *refs):
        del me_ref
        for w_ref, o_ref in zip(refs[:n], refs[n + 1:]):
            o_ref[...] = w_ref[...].astype(BF16)

    return pl.pallas_call(
        body, name=name,
        grid_spec=pltpu.PrefetchScalarGridSpec(
            num_scalar_prefetch=1, grid=(1,),
            in_specs=[pl.BlockSpec((None,) + a.shape[1:], lambda i, me, l=l: (l, 0, 0)) for a, l in zip(ws, layers)]
            + [_ANY],
            out_specs=[pl.BlockSpec((None,) + a.shape[1:], lambda i, me: (me[0], 0, 0)) for a in ws]),
        out_shape=[SDS((N_DEV,) + a.shape[1:], BF16) for a in ws],
        compiler_params=_cp("arbitrary"))(me1, *ws, after)


def _sum8_into_slot(p, me1, name):
    _, r, c = p.shape

    def body(me_ref, p_ref, o_ref):
        del me_ref
        acc = p_ref[0]
        for k in range(1, N_DEV):
            acc = acc + p_ref[k]
        o_ref[...] = acc

    return pl.pallas_call(
        body, name=name,
        grid_spec=pltpu.PrefetchScalarGridSpec(
            num_scalar_prefetch=1, grid=(1,),
            in_specs=[pl.BlockSpec(p.shape, lambda i, me: (0, 0, 0))],
            out_specs=pl.BlockSpec((None, r, c), lambda i, me: (me[0], 0, 0))),
        out_shape=SDS(p.shape, F32), compiler_params=_cp("arbitrary"))(me1, p)


def _adamw(w, g, m, v):
    m = ADAM_B1 * m + (1.0 - ADAM_B1) * g
    v = ADAM_B2 * v + (1.0 - ADAM_B2) * (g * g)
    m_hat = m / (1.0 - ADAM_B1 ** ADAM_STEP)
    v_hat = v / (1.0 - ADAM_B2 ** ADAM_STEP)
    delta = -ADAM_LR * (m_hat / (jnp.sqrt(v_hat) + ADAM_EPS) + ADAM_WD * w)
    return delta, m, v


def _adam_shard(parts, w, m, v, layer, prev, name):
    _, r, c = parts.shape
    tr = next(t for t in (256, 352, r) if r % t == 0)
    n_prev = 0 if prev is None else 4

    def body(*refs):
        p_ref, w_ref, m_ref, v_ref = refs[:4]
        g_ref, d_ref, nm_ref, nv_ref = refs[4 + n_prev:]
        g = p_ref[0].astype(F32)
        for k in range(1, N_DEV):
            g = g + p_ref[k].astype(F32)
        delta, nm, nv = _adamw(w_ref[...], g, m_ref[...], v_ref[...])
        g_ref[...] = g
        d_ref[...] = delta
        nm_ref[...] = nm
        nv_ref[...] = nv

    blk = pl.BlockSpec((None, tr, c), lambda i: (layer, i, 0))
    return pl.pallas_call(
        body, name=name, grid=(r // tr,),
        in_specs=[pl.BlockSpec((N_DEV, tr, c), lambda i: (0, i, 0)), blk, blk, blk] + [_ANY] * n_prev,
        out_specs=[blk] * 4, out_shape=[SDS(w.shape, F32)] * 4,
        input_output_aliases={4 + k: k for k in range(n_prev)},
        compiler_params=_cp("parallel"))(parts, w, m, v, *(prev or ()))


SMALL_MATRICES = [("lru_w_r", 2048), ("lru_w_i", 2048), ("gmlp_w_s", 1024)]
SMALL_VECTORS = [("norm1_g", 8), ("gmlp_ln_g", 8), ("gmlp_ln_b", 8), ("gmlp_b_s", 8), ("conv_w", 32), ("conv_b", 8),
                 ("lru_b_r", 16), ("lru_b_i", 16), ("lru_lambda", 16), ("norm2_g", 8), ("final_g", 8)]
SMALL_VECTOR_ROW0 = sum(n for _, n in SMALL_MATRICES)
SMALL_VECTOR_BLOCK = 256
SMALL_ROWS = SMALL_VECTOR_ROW0 + SMALL_VECTOR_BLOCK


def _pack_small(small):
    parts = [small[k] for k, _ in SMALL_MATRICES]
    parts += [small[k] if k in small else jnp.zeros((n, HD), F32) for k, n in SMALL_VECTORS]
    flat = jnp.concatenate(parts)
    return jnp.pad(flat, ((0, SMALL_ROWS - flat.shape[0]), (0, 0))).reshape(N_DEV, SMALL_ROWS // N_DEV, HD)


def _adam_matrix(g0, g1, w, m, v, row0, name):
    _, rows, _ = w.shape

    def body(g0_ref, g1_ref, w_ref, m_ref, v_ref, g_ref, d_ref, nm_ref, nv_ref):
        for l, src in enumerate((g0_ref, g1_ref)):
            g = src[...]
            delta, nm, nv = _adamw(w_ref[l], g, m_ref[l], v_ref[l])
            g_ref[l] = g
            d_ref[l] = delta
            nm_ref[l] = nm
            nv_ref[l] = nv

    gspec = pl.BlockSpec((rows, HD), lambda i: (row0 // rows, 0))
    return pl.pallas_call(body, name=name, grid=(1,), in_specs=[gspec, gspec] + [_full(w.shape)] * 3,
                          out_specs=[_full(w.shape)] * 4, out_shape=[SDS(w.shape, F32)] * 4,
                          compiler_params=_cp("arbitrary"))(g0, g1, w, m, v)


def _adam_vectors(g0, g1, dg1_parts, me1, ws, ms, vs):
    names = [k for k, _ in SMALL_VECTORS]
    n = len(names)

    def lanes(rows8):
        return jnp.concatenate([rows8[k:k + 1, :] for k in range(LANE_ROWS)], axis=1)

    def body(me_ref, g0_ref, g1_ref, dg1_ref, *refs):
        w_refs, m_refs, v_refs = refs[:n], refs[n:2 * n], refs[2 * n:3 * n]
        outs = refs[3 * n:]
        me = me_ref[0]
        g_refs = (g0_ref, g1_ref)

        def emit(i, idx, g):
            delta, nm, nv = _adamw(w_refs[i][idx], g, m_refs[i][idx], v_refs[i][idx])
            for j, val in enumerate((g, delta, nm, nv)):
                outs[4 * i + j][idx] = val

        off = 0
        for i, (name, rows) in enumerate(SMALL_VECTORS):
            for l in range(2):
                row = (slice(l, l + 1), slice(None))
                if name == "final_g":
                    if l == 1:
                        emit(i, (slice(0, 1), slice(None)), lanes(g1_ref[off:off + rows, :]))
                elif name == "norm1_g":
                    if l == 1:
                        emit(i, row, lanes(g0_ref[off:off + rows, :]))
                    else:
                        total = dg1_ref[0]
                        for k in range(1, N_DEV):
                            total = total + dg1_ref[k]
                        emit(i, row, lanes(total))
                elif name == "gmlp_b_s":
                    emit(i, (l,), g_refs[l][off:off + rows, :])
                elif rows == LANE_ROWS:
                    emit(i, row, lanes(g_refs[l][off:off + rows, :]))
                else:
                    for r in range(rows // LANE_ROWS):
                        emit(i, (l, slice(r, r + 1), slice(None)), g_refs[l][pl.ds(off + r * LANE_ROWS + me, 1), :])
            off += rows

    args = [ws[k] for k in names] + [ms[k] for k in names] + [vs[k] for k in names]
    gspec = pl.BlockSpec((SMALL_VECTOR_BLOCK, HD), lambda i, me: (SMALL_VECTOR_ROW0 // SMALL_VECTOR_BLOCK, 0))
    res = pl.pallas_call(
        body, name="adam_vectors",
        grid_spec=pltpu.PrefetchScalarGridSpec(
            num_scalar_prefetch=1, grid=(1,),
            in_specs=[gspec, gspec, _full(dg1_parts.shape)] + [_full(a.shape) for a in args],
            out_specs=[_full(ws[k].shape) for k in names for _ in range(4)]),
        out_shape=[SDS(ws[k].shape, F32) for k in names for _ in range(4)],
        compiler_params=_cp("arbitrary"))(me1, g0, g1, dg1_parts, *args)
    return {k: list(res[4 * i:4 * i + 4]) for i, k in enumerate(names)}


def _after(a, *tokens):
    for token in tokens:
        if token is not None:
            a = a + token[0:1, 0:1]
    return a


def _local_step(x, tgt, p, get_w, hook=lambda stage, layer, payload: None):
    s = x.shape[0]
    tm = _row_tile(s)
    wsb = p["gmlp_w_s"].astype(BF16)
    wstb = jnp.swapaxes(p["gmlp_w_s"], -1, -2).astype(BF16)
    bsb = jnp.broadcast_to(p["gmlp_b_s"][..., None], p["gmlp_w_s"].shape)
    wrb = p["lru_w_r"].astype(BF16)
    wib = p["lru_w_i"].astype(BF16)
    saved = []
    for l in range(2):
        win = get_w("w_in", l, x)
        z, h1, a0, b0, a1, b1, xcb, *gates = _inproj_gates_fwd(
            x, _after(p["norm1_g"][l][None], hook("pre_inproj", l, win)), win, p["conv_w"][l], p["conv_b"][l][None],
            wrb[l], wib[l], p["lru_b_r"][l], p["lru_b_i"][l], p["lru_lambda"][l], l, tm // 2)
        h0, hr = _lru_scan(a0, b0, a1, b1, False, l)
        lng = _after(p["gmlp_ln_g"][l][None], hook("pre_gmlp", l, h0))
        wout = get_w("w_out", l, lng)
        x1, mg = _mixer_fwd(x, h0, hr, z, lng, p["gmlp_ln_b"][l][None], wsb[l], bsb[l], wout, l, tm)
        wfi = get_w("w_ffn_in", l, x1)
        wfo = get_w("w_ffn_out", l, x1)
        if l == 0:
            x2, ff, dff, h2 = _ffn_fwd(x1, p["norm2_g"][l][None], wfi, wfo, l, tm)
        else:
            dx, loss, dfg, ff, dff, h2 = _ffn_fwd(x1, p["norm2_g"][l][None], wfi, wfo, l, tm,
                                                  head=(p["final_g"][None], tgt))
        saved.append((x, z, h1, a0, a1, h0, hr, x1, mg, ff, dff, h2, win, wout, wfi, wfo, xcb, gates))
        x = x2
    pending = None
    for l in (1, 0):
        x0, z, h1, a0, a1, h0, hr, x1, mg, ff, dff, h2, win, wout, wfi, wfo, xcb, gates = saved[l]
        dgu, dx1, dg2 = _ffn_bwd(dx, wfo, dff, wfi.reshape(N_DEV, FF_BLK, D), x1, p["norm2_g"][l][None], l, tm)
        d_wfo = _mm_tn(ff, pl.BlockSpec((None, s, FF_BLK), lambda j: (j, 0, 0)), dx, _resident((s, D)),
                       4, (4, FF_BLK, D), pl.BlockSpec((None, FF_BLK, D), lambda j: (j, 0, 0)),
                       f"dw_ffn_out_l{l}", a_is_transposed=False)
        dgu8 = dgu.reshape(N_DEV, s, FF_BLK)
        d_wfi = _mm_tn(dgu8, pl.BlockSpec((None, s, FF_BLK), lambda j: (j, 0, 0)), h2, _resident((s, D)),
                       N_DEV, (N_DEV, FF_BLK, D), pl.BlockSpec((None, FF_BLK, D), lambda j: (j, 0, 0)),
                       f"dw_ffn_in_l{l}", a_is_transposed=False)
        token = hook("ffn_partials", l, dict(w_ffn_out=d_wfo.reshape(N_DEV, D_FF // N_DEV, D), w_ffn_in=d_wfi))
        pending = hook("mid_backward", l, dx1)
        dz, dh, dws, dbs, dlng, dlnb = _mixer_bwd(dx1, wout, h0, hr, z, _after(p["gmlp_ln_g"][l][None], token),
                                                  p["gmlp_ln_b"][l][None], wsb[l], wstb[l], bsb[l], l, tm)
        d_wout = _mm_tn(mg, _resident((D, s)), dx1, pl.BlockSpec((s, D // 2), lambda j: (0, j)),
                        2, (D, D), pl.BlockSpec((D, D // 2), lambda j: (0, j)), f"dw_out_l{l}")
        g1, g0 = _lru_scan(a1, dh, a0, dh, True, l)
        dxc, dwr, dwi, dbr, dbi, dlam = _lru_gates_bwd(
            xcb, gates, h0, hr, g0, g1, wrb[l], wib[l], _after(p["lru_lambda"][l], pending), l, tm)
        dz, dcw, dcb = _conv_bwd(dz, dxc, z, p["conv_w"][l], l, tm)
        small = dict(lru_w_r=dwr.reshape(-1, HD), lru_w_i=dwi.reshape(-1, HD), gmlp_w_s=dws.reshape(-1, HD),
                     gmlp_ln_g=dlng, gmlp_ln_b=dlnb, gmlp_b_s=dbs, conv_w=dcw, conv_b=dcb, lru_b_r=dbr,
                     lru_b_i=dbi, lru_lambda=dlam, norm2_g=dg2)
        if l == 1:
            small["final_g"] = dfg
        else:
            small["norm1_g"] = dg1
        started = hook("small_grads", l, small)
        d_win = _mm_tn(h1, _resident((D, s)), dz, pl.BlockSpec((s, IN_BLK), lambda j: (0, j)),
                       N_DEV, (N_DEV, D, IN_BLK), pl.BlockSpec((None, D, IN_BLK), lambda j: (j, 0, 0)),
                       f"dw_in_l{l}", after=started)
        token = hook("mixer_partials", l, dict(w_out=d_wout.reshape(N_DEV, D // N_DEV, D), w_in=d_win))
        dx, dg1 = _mm_nt_rms_bwd(
            dz, pl.BlockSpec((tm, N_IN), lambda i: (i, 0)),
            lambda r: [r[:, k * IN_BLK:(k + 1) * IN_BLK] for k in range(N_DEV)],
            win, False, x0, _after(p["norm1_g"][l][None], token, started, pending), dx1, f"inproj_bwd_dx_l{l}", tm)
        pending = None
    return loss, dx, dg1


_REPL = ["norm1_g", "gmlp_ln_g", "gmlp_ln_b", "gmlp_w_s", "gmlp_b_s", "conv_b", "lru_w_r", "lru_w_i", "norm2_g", "final_g"]
_LANE_SHARDED = ["conv_w", "lru_b_r", "lru_b_i", "lru_lambda"]
_BIG = ["w_in", "w_out", "w_ffn_in", "w_ffn_out"]
_ORDER = ["norm1_g", "w_in", "gmlp_ln_g", "gmlp_ln_b", "gmlp_w_s", "gmlp_b_s", "conv_w", "conv_b", "lru_w_r", "lru_b_r",
          "lru_w_i", "lru_b_i", "lru_lambda", "w_out", "norm2_g", "w_ffn_in", "w_ffn_out", "final_g"]


def kernel(x, norm1_g, w_in, gmlp_ln_g, gmlp_ln_b, gmlp_w_s, gmlp_b_s, conv_w, conv_b, lru_w_r, lru_b_r, lru_w_i, lru_b_i, lru_lambda, w_out, norm2_g, w_ffn_in, w_ffn_out, final_g, loss_target, m_norm1_g, m_w_in, m_gmlp_ln_g, m_gmlp_ln_b, m_gmlp_w_s, m_gmlp_b_s, m_conv_w, m_conv_b, m_lru_w_r, m_lru_b_r, m_lru_w_i, m_lru_b_i, m_lru_lambda, m_w_out, m_norm2_g, m_w_ffn_in, m_w_ffn_out, m_final_g, v_norm1_g, v_w_in, v_gmlp_ln_g, v_gmlp_ln_b, v_gmlp_w_s, v_gmlp_b_s, v_conv_w, v_conv_b, v_lru_w_r, v_lru_b_r, v_lru_w_i, v_lru_b_i, v_lru_lambda, v_w_out, v_norm2_g, v_w_ffn_in, v_w_ffn_out, v_final_g):
    w = dict(norm1_g=norm1_g, w_in=w_in, gmlp_ln_g=gmlp_ln_g, gmlp_ln_b=gmlp_ln_b, gmlp_w_s=gmlp_w_s, gmlp_b_s=gmlp_b_s,
             conv_w=conv_w, conv_b=conv_b, lru_w_r=lru_w_r, lru_b_r=lru_b_r, lru_w_i=lru_w_i, lru_b_i=lru_b_i,
             lru_lambda=lru_lambda, w_out=w_out, norm2_g=norm2_g, w_ffn_in=w_ffn_in, w_ffn_out=w_ffn_out, final_g=final_g)
    mom = dict(norm1_g=m_norm1_g, w_in=m_w_in, gmlp_ln_g=m_gmlp_ln_g, gmlp_ln_b=m_gmlp_ln_b, gmlp_w_s=m_gmlp_w_s,
               gmlp_b_s=m_gmlp_b_s, conv_w=m_conv_w, conv_b=m_conv_b, lru_w_r=m_lru_w_r, lru_b_r=m_lru_b_r,
               lru_w_i=m_lru_w_i, lru_b_i=m_lru_b_i, lru_lambda=m_lru_lambda, w_out=m_w_out, norm2_g=m_norm2_g,
               w_ffn_in=m_w_ffn_in, w_ffn_out=m_w_ffn_out, final_g=m_final_g)
    var = dict(norm1_g=v_norm1_g, w_in=v_w_in, gmlp_ln_g=v_gmlp_ln_g, gmlp_ln_b=v_gmlp_ln_b, gmlp_w_s=v_gmlp_w_s,
               gmlp_b_s=v_gmlp_b_s, conv_w=v_conv_w, conv_b=v_conv_b, lru_w_r=v_lru_w_r, lru_b_r=v_lru_b_r,
               lru_w_i=v_lru_w_i, lru_b_i=v_lru_b_i, lru_lambda=v_lru_lambda, w_out=v_w_out, norm2_g=v_norm2_g,
               w_ffn_in=v_w_ffn_in, w_ffn_out=v_w_ffn_out, final_g=v_final_g)
    for src in (w, mom, var):
        src["w_ffn_in"] = jnp.swapaxes(src["w_ffn_in"], 1, 2)
    xi, yi, ci = _me()
    me = 4 * xi + 2 * yi + ci

    lane_shapes = [w[k].shape for k in _LANE_SHARDED]
    lane_rows = sum(a[0] * a[1] for a in lane_shapes)
    packed = jnp.concatenate([w[k].reshape(-1, HD) for k in _LANE_SHARDED])
    packed = jnp.pad(packed, ((0, -lane_rows % 8), (0, 0)))

    me1 = jnp.reshape(me, (1,)).astype(jnp.int32)
    gathers = {}
    exchanges = {}
    views = dict(w_in=(N_DEV, D, IN_BLK), w_out=(D, D), w_ffn_in=(2, 4, FF_BLK, D), w_ffn_out=(4, FF_BLK, D))
    small_ex = {}
    small_ag = {}

    casts = {}

    def start_gather(names, l, after):
        lands = [casts[(k, l)] if (k, l) in casts else _cast_into_slot(w[k], l, me1, f"cast_{k}_l{l}") for k in names]
        started, tok = _gather2_start(lands, after, f"gather_start_{'_'.join(names)}_l{l}")
        gathers.update({(k, l): h for k, h in zip(names, started)})
        return tok

    def relay_gather(names, l, after):
        relayed, tok = _gather2_relay([gathers[(k, l)] for k in names], after, f"gather_relay_{'_'.join(names)}_l{l}")
        gathers.update({(k, l): h for k, h in zip(names, relayed)})
        return tok

    def get_w(k, l, after):
        if (k, l) == ("w_in", 1):
            after = relay_gather(_BIG[:1], l, after)
        return _gather2_wait(gathers[(k, l)], after, f"gather_wait_{k}_l{l}").reshape(views[k])

    def hook(stage, l, payload):
        if stage == "pre_inproj":
            return start_gather(_BIG[1:], l, payload)
        if stage == "pre_gmlp":
            tok = relay_gather(_BIG[1:], l, payload)
            return tok + start_gather(_BIG[:1], l + 1, tok) if l == 0 else tok
        if stage == "small_grads":
            (small_ex[l],), tok = _exchange_start([_pack_small(payload)], f"exchange_start_small_l{l}")
            return tok
        if stage == "mid_backward":
            return reduce_small(l + 1, payload) if l == 0 else None
        extra = reduce_small(0, payload["w_in"]) if (stage, l) == ("mixer_partials", 0) else None
        started, tok = _exchange_start(list(payload.values()), f"exchange_start_{'_'.join(payload)}_l{l}")
        exchanges.update({(k, l): h for k, h in zip(payload, started)})
        return tok if extra is None else tok + extra

    def reduce_small(l, after):
        got = _exchange_wait(small_ex[l], after, f"exchange_wait_small_l{l}")
        mine = _sum8_into_slot(got, me1, f"sum_small_l{l}")
        (small_ag[l],), tok = _gather_start([mine], got, f"gather_start_small_l{l}")
        return tok

    land = lax.dynamic_update_slice(jnp.zeros((N_DEV,) + packed.shape, F32), packed[None], (me, 0, 0))
    (lanes_handle,), token = _gather_start([land], packed, "gather_start_lanes")
    token = start_gather(_BIG[:1], 0, token)
    later = [(k, l) for l in range(2) for k in _BIG if (k, l) != ("w_in", 0)]
    casts.update(zip(later, _cast_all_into_slots([w[k] for k, _ in later], [l for _, l in later], me1, token,
                                                 "cast_later_weights")))
    token = relay_gather(_BIG[:1], 0, casts[later[0]])
    lanes = _gather_wait(lanes_handle, token, "gather_wait_lanes")
    params = {k: w[k] for k in _REPL}
    off = 0
    for k, shp in zip(_LANE_SHARDED, lane_shapes):
        n = shp[0] * shp[1]
        params[k] = jnp.swapaxes(lanes[:, off:off + n], 0, 1).reshape(shp[0], shp[1], D)
        off += n
    loss, dx, dg1 = _local_step(x[0], loss_target[0], params, get_w, hook)

    out = {}
    after = dx
    for k, l in [(k, l) for k in ("w_ffn_out", "w_ffn_in") for l in (1, 0)] + [("w_out", 1), ("w_in", 1)]:
        got = _exchange_wait(exchanges[(k, l)], after, f"exchange_wait_{k}_l{l}")
        out[k] = _adam_shard(got, w[k], mom[k], var[k], l, out.get(k), f"adam_{k}_l{l}")
        after = out[k][3]
    g_small = [_gather_wait(small_ag[l], after, f"gather_wait_small_l{l}").reshape(SMALL_ROWS, HD) for l in (0, 1)]
    row0 = 0
    for k, rows in SMALL_MATRICES:
        res = _adam_matrix(*g_small, *[src[k].reshape(2, rows, HD) for src in (w, mom, var)], row0, f"adam_{k}")
        out[k] = [a.reshape(w[k].shape) for a in res]
        after = res[3]
        row0 += rows
    for k in ("w_out", "w_in"):
        got = _exchange_wait(exchanges[(k, 0)], after, f"exchange_wait_{k}_l0")
        out[k] = _adam_shard(got, w[k], mom[k], var[k], 0, out[k], f"adam_{k}_l0")
    out["w_ffn_in"] = [jnp.swapaxes(a, 1, 2) for a in out["w_ffn_in"]]
    as_rows = lambda a: a.reshape(1, D) if a.ndim == 1 else a
    vec = _adam_vectors(*g_small, _all_gather(dg1, out["w_in"][3], "gather_norm1_grad"), me1,
                        *[{k: as_rows(src[k]) for k, _ in SMALL_VECTORS} for src in (w, mom, var)])
    out.update({k: [a.reshape(w[k].shape) for a in res] for k, res in vec.items()})

    loss = lax.psum(loss[0, 0], MESH_AXES)
    return (loss, dx[None], *[out[k][0] for k in _ORDER], *[out[k][1] for k in _ORDER],
            *[out[k][2] for k in _ORDER], *[out[k][3] for k in _ORDER])
```

```python
import jax
import jax.numpy as jnp
from jax import lax
from jax.experimental import pallas as pl
from jax.experimental.pallas import tpu as pltpu

F32 = jnp.float32
BF16 = jnp.bfloat16
SDS = jax.ShapeDtypeStruct

D = 1024
N_IN = 6 * D
D_FF = 2816
N_DEV = 8
IN_BLK = N_IN // N_DEV
FF_BLK = 2 * D_FF // N_DEV
HEADS = 8
HD = 128
EPS = 1e-6
LRU_C = 8.0
MESH_AXES = ("x", "y", "c")

ADAM_LR = 0.001
ADAM_B1 = 0.9
ADAM_B2 = 0.999
ADAM_EPS = 1e-08
ADAM_WD = 0.01
ADAM_STEP = 10

VMEM_LIMIT = 60 * 2**20


def _cp(*sem, **kw):
    return pltpu.CompilerParams(dimension_semantics=sem, vmem_limit_bytes=VMEM_LIMIT, **kw)


def _row_tile(s):
    return 512 if s >= 1024 else s // 2


_GELU_C = 0.7978845608028654


def _gelu(x):
    t = jnp.tanh(_GELU_C * (x + 0.044715 * (x * x * x)))
    return 0.5 * x * (1.0 + t), t


def _gelu_grad(x, t):
    return 0.5 * (1.0 + t) + 0.5 * x * (1.0 - t * t) * (_GELU_C * (1.0 + 0.134145 * (x * x)))


def _sigmoid(x):
    return 0.5 + 0.5 * jnp.tanh(0.5 * x)


def _softplus(x):
    e = jnp.exp(-jnp.abs(x))
    w = 1.0 + e
    l1p = jnp.where(w == 1.0, e, jnp.log(w) * e / jnp.where(w == 1.0, 1.0, w - 1.0))
    return jnp.maximum(x, 0.0) + l1p


def _rms_fwd(x, g):
    r = lax.rsqrt(jnp.mean(x * x, axis=-1, keepdims=True) + EPS)
    return x * r * g


def _rms_bwd(x, g, dh):
    r = lax.rsqrt(jnp.mean(x * x, axis=-1, keepdims=True) + EPS)
    xh = x * r
    dxh = dh * g
    dx = r * (dxh - xh * jnp.mean(dxh * xh, axis=-1, keepdims=True))
    dg = jnp.sum(dh * xh, axis=0, keepdims=True)
    return dx, dg


LANE_ROWS = D // HD


def _add_rows128(ref, vec, row0=0):
    for i in range(vec.shape[0]):
        for k in range(LANE_ROWS):
            j = row0 + i * LANE_ROWS + k
            ref[j:j + 1, :] += vec[i:i + 1, k * HD:(k + 1) * HD]


def _dot(a, b):
    return jnp.dot(a, b, preferred_element_type=F32)


def _dot_nt(a, b):
    return lax.dot_general(a, b, (((1,), (1,)), ((), ())), preferred_element_type=F32)


def _dot_tn(a, b):
    return lax.dot_general(a, b, (((0,), (0,)), ((), ())), preferred_element_type=F32)


def _taps(prev, cur, nxt, tm):
    hr = prev.shape[0]
    ext = jnp.concatenate([prev, cur, nxt], axis=0)
    n = tm + 2 * hr
    sl = slice(hr, hr + tm)
    return (pltpu.roll(ext, 2, 0)[sl], pltpu.roll(ext, 1, 0)[sl], cur,
            pltpu.roll(ext, n - 1, 0)[sl], pltpu.roll(ext, n - 2, 0)[sl])


def _halo_specs(tm, s, col, rows=8):
    nb = s // rows
    r = tm // rows
    return (pl.BlockSpec((rows, D), lambda i: (jnp.maximum(i * r - 1, 0), col)),
            pl.BlockSpec((tm, D), lambda i: (i, col)),
            pl.BlockSpec((rows, D), lambda i: (jnp.minimum((i + 1) * r, nb - 1), col)))


def _halo_load(prev_ref, cur_ref, next_ref, fp, fn):
    return prev_ref[...].astype(F32) * fp, cur_ref[...].astype(F32), next_ref[...].astype(F32) * fn


def _halo_flags(nt):
    i = pl.program_id(0)
    return (i > 0).astype(F32), (i < nt - 1).astype(F32)


def _full(shape):
    nd = len(shape)
    return pl.BlockSpec(shape, lambda *_: (0,) * nd)


def _resident(shape):
    nd = len(shape)
    return pl.BlockSpec(shape, lambda *_: (0,) * nd, pipeline_mode=pl.Buffered(1))


def _norm_inproj(x, g, w, layer, tm):
    s = x.shape[0]

    def body(x_ref, g_ref, w_ref, z_ref, ht_ref):
        h32 = _rms_fwd(x_ref[...], g_ref[...])
        ht_ref[...] = h32.T.astype(BF16)
        h = h32.astype(BF16)
        for j in range(N_DEV):
            z_ref[:, j * IN_BLK:(j + 1) * IN_BLK] = _dot(h, w_ref[j]).astype(BF16)

    return pl.pallas_call(
        body, name=f"norm_inproj_l{layer}", grid=(s // tm,),
        in_specs=[pl.BlockSpec((tm, D), lambda i: (i, 0)), _full((1, D)), _resident((N_DEV, D, IN_BLK))],
        out_specs=[pl.BlockSpec((tm, N_IN), lambda i: (i, 0)), pl.BlockSpec((D, tm), lambda i: (0, i))],
        out_shape=[SDS((s, N_IN), BF16), SDS((D, s), BF16)],
        compiler_params=_cp("parallel"))(x, g, w)


def _gmlp_values(zu_ref, zv_ref, lng_ref, lnb_ref):
    zu = zu_ref[...].astype(F32)
    zv = zv_ref[...].astype(F32)
    u, tu = _gelu(zu)
    gv, tv = _gelu(zv)
    xc = gv - jnp.mean(gv, axis=-1, keepdims=True)
    rstd = lax.rsqrt(jnp.mean(xc * xc, axis=-1, keepdims=True) + EPS)
    xh = xc * rstd
    vb = (xh * lng_ref[...] + lnb_ref[...]).astype(BF16)
    return zu, zv, u, tu, tv, xh, rstd, vb


def _mixer_fwd(x, h0, h1, z, lng, lnb, ws, bsb, wo, layer, tm):
    s = x.shape[0]

    def body(x_ref, h0_ref, h1_ref, zu_ref, zv_ref, zg_ref, za_ref, zb_ref, lng_ref, lnb_ref, ws_ref, bsb_ref,
             wo_ref, x1_ref, mg_ref, ya_s):
        _, _, u, _, _, _, _, vb = _gmlp_values(zu_ref, zv_ref, lng_ref, lnb_ref)
        for c in range(tm // HD):
            rs = slice(c * HD, (c + 1) * HD)
            for g in range(HEADS):
                cs = slice(g * HD, (g + 1) * HD)
                ya_s[rs, cs] = u[rs, cs] * (_dot(ws_ref[g], vb[rs, cs]) + bsb_ref[g])
        gg, _ = _gelu(zg_ref[...].astype(F32))
        yb = (h0_ref[...] + h1_ref[...]) * gg
        m32 = _sigmoid(za_ref[...].astype(F32)) * ya_s[...] + _sigmoid(zb_ref[...].astype(F32)) * yb
        mg_ref[...] = m32.T.astype(BF16)
        x1_ref[...] = x_ref[...] + _dot(m32.astype(BF16), wo_ref[...])

    tile = pl.BlockSpec((tm, D), lambda i: (i, 0))
    wspec = _full((HEADS, HD, HD))
    return pl.pallas_call(
        body, name=f"mixer_fwd_l{layer}", grid=(s // tm,),
        in_specs=[tile, tile, tile] + [pl.BlockSpec((tm, D), lambda i, c=c: (i, c)) for c in (0, 1, 3, 4, 5)]
        + [_full((1, D)), _full((1, D)), wspec, wspec, _full((D, D))],
        out_specs=[tile, pl.BlockSpec((D, tm), lambda i: (0, i))], out_shape=[SDS((s, D), F32), SDS((D, s), BF16)],
        scratch_shapes=[pltpu.VMEM((tm, D), F32)],
        compiler_params=_cp("parallel"))(x, h0, h1, z, z, z, z, z, lng, lnb, ws, bsb, wo)


def _conv(taps, cw_ref, cb_ref):
    _, m1, c0, p1, p2 = taps
    return cb_ref[...] + m1 * cw_ref[0:1, :] + c0 * cw_ref[1:2, :] + p1 * cw_ref[2:3, :] + p2 * cw_ref[3:4, :]


def _heads_dot(xb, w_ref, d):
    return jnp.concatenate([_dot(xb[:, h * HD:(h + 1) * HD], w_ref[d, h]) for h in range(HEADS)], axis=1)


def _lru_decay(r, sp):
    la = (-LRU_C) * r * sp
    a = jnp.exp(la)
    return a, jnp.tanh(-la) * (a * a + 1.0)


def _lru_gates_fwd(z, cw, cb, wr, wi, br, bi, lam, layer, tm):
    s = z.shape[0]
    nt = s // tm

    def body(zp_ref, zc_ref, zn_ref, cw_ref, cb_ref, wr_ref, wi_ref, br_ref, bi_ref, lam_ref,
             a0_ref, b0_ref, a1_ref, b1_ref, xc_ref, r0_ref, i0_ref, r1_ref, i1_ref):
        fp, fn = _halo_flags(nt)
        xc = _conv(_taps(*_halo_load(zp_ref, zc_ref, zn_ref, fp, fn), tm), cw_ref, cb_ref)
        xb = xc.astype(BF16)
        xc_ref[...] = xb
        for d, (a_ref, b_ref, r_ref, i_ref) in enumerate(((a0_ref, b0_ref, r0_ref, i0_ref),
                                                          (a1_ref, b1_ref, r1_ref, i1_ref))):
            r = _sigmoid(_heads_dot(xb, wr_ref, d) + br_ref[d:d + 1, :])
            ig = _sigmoid(_heads_dot(xb, wi_ref, d) + bi_ref[d:d + 1, :])
            a, q = _lru_decay(r, _softplus(-lam_ref[d:d + 1, :]))
            a_ref[...] = a
            b_ref[...] = jnp.sqrt(q) * (ig * xc)
            r_ref[...] = r.astype(BF16)
            i_ref[...] = ig.astype(BF16)

    tile = pl.BlockSpec((tm, D), lambda i: (i, 0))
    return pl.pallas_call(
        body, name=f"lru_gates_fwd_l{layer}", grid=(nt,),
        in_specs=[*_halo_specs(tm, s, 2, 16), _full((4, D)), _full((1, D)),
                  _full((2, HEADS, HD, HD)), _full((2, HEADS, HD, HD)), _full((2, D)), _full((2, D)), _full((2, D))],
        out_specs=[tile] * 9, out_shape=[SDS((s, D), F32)] * 4 + [SDS((s, D), BF16)] * 5,
        compiler_params=_cp("parallel"))(z, z, z, cw, cb, wr, wi, br, bi, lam)


def _scan_group(a, x, c, reverse, bwd):
    row = lax.broadcasted_iota(jnp.int32, a.shape, 0)
    b = a * x if bwd else x
    for d in (1, 2, 4):
        keep = (row < 8 - d) if reverse else (row >= d)
        sh = 8 - d if reverse else d
        a_s = jnp.where(keep, pltpu.roll(a, sh, 0), 1.0)
        b_s = jnp.where(keep, pltpu.roll(b, sh, 0), 0.0)
        b = a * b_s + b
        a = a * a_s
    h = b + a * c
    new_c = h[0:1, :] if reverse else h[7:8, :]
    if not bwd:
        return h, new_c
    if reverse:
        prev = jnp.where(row < 7, pltpu.roll(h, 7, 0), c)
    else:
        prev = jnp.where(row >= 1, pltpu.roll(h, 1, 0), c)
    return x + prev, new_c


def _lru_scan(a_f, x_f, a_r, x_r, bwd, layer):
    s = a_f.shape[0]
    ts = min(1024, s // 2)
    cb = 512
    nt = s // ts
    ng = ts // 8

    def body(af_ref, xf_ref, ar_ref, xr_ref, of_ref, or_ref, cf, cr):
        @pl.when(pl.program_id(1) == 0)
        def _():
            cf[...] = jnp.zeros_like(cf)
            cr[...] = jnp.zeros_like(cr)

        def step(j, carry):
            c_f, c_r = carry
            rf = pl.multiple_of(j * 8, 8)
            rr = pl.multiple_of((ng - 1 - j) * 8, 8)
            o, c_f = _scan_group(af_ref[pl.ds(rf, 8), :], xf_ref[pl.ds(rf, 8), :], c_f, False, bwd)
            of_ref[pl.ds(rf, 8), :] = o
            o, c_r = _scan_group(ar_ref[pl.ds(rr, 8), :], xr_ref[pl.ds(rr, 8), :], c_r, True, bwd)
            or_ref[pl.ds(rr, 8), :] = o
            return c_f, c_r

        c_f, c_r = lax.fori_loop(0, ng, step, (cf[0:1, :], cr[0:1, :]), unroll=2)
        cf[...] = jnp.broadcast_to(c_f, cf.shape)
        cr[...] = jnp.broadcast_to(c_r, cr.shape)

    fwd = pl.BlockSpec((ts, cb), lambda c, t: (t, c))
    rev = pl.BlockSpec((ts, cb), lambda c, t: (nt - 1 - t, c))
    return pl.pallas_call(
        body, name=f"lru_scan_{'bwd' if bwd else 'fwd'}_l{layer}", grid=(D // cb, nt),
        in_specs=[fwd, fwd, rev, rev], out_specs=[fwd, rev],
        out_shape=[SDS((s, D), F32)] * 2,
        scratch_shapes=[pltpu.VMEM((8, cb), F32), pltpu.VMEM((8, cb), F32)],
        compiler_params=_cp("parallel", "arbitrary"))(a_f, x_f, a_r, x_r)


def _ffn_fwd(x1, g, wfi, wfo, layer, tm, head=None):
    s = x1.shape[0]

    def ffn(x_ref, g_ref, wi_ref, wo_ref, ff_ref, dff_ref, h_ref):
        x = x_ref[...]
        h = _rms_fwd(x, g_ref[...]).astype(BF16)
        h_ref[...] = h
        acc = x
        for k in range(4):
            gate = _dot_nt(h, wi_ref[0, k])
            up = _dot_nt(h, wi_ref[1, k])
            sg = _sigmoid(gate)
            silu = gate * sg
            ff = (silu * up).astype(BF16)
            ff_ref[k] = ff
            dff_ref[0, k] = (up * (sg * (1.0 + gate * (1.0 - sg)))).astype(BF16)
            dff_ref[1, k] = silu.astype(BF16)
            acc = acc + _dot(ff, wo_ref[k])
        return acc

    def body(x_ref, g_ref, wi_ref, wo_ref, x2_ref, ff_ref, dff_ref, h_ref):
        x2_ref[...] = ffn(x_ref, g_ref, wi_ref, wo_ref, ff_ref, dff_ref, h_ref)

    def body_with_head(x_ref, g_ref, wi_ref, wo_ref, fg_ref, t_ref, dx_ref, loss_ref, dfg_ref, ff_ref, dff_ref, h_ref):
        @pl.when(pl.program_id(0) == 0)
        def _():
            loss_ref[...] = jnp.zeros_like(loss_ref)
            dfg_ref[...] = jnp.zeros_like(dfg_ref)

        x2 = ffn(x_ref, g_ref, wi_ref, wo_ref, ff_ref, dff_ref, h_ref)
        fg = fg_ref[...]
        e = _rms_fwd(x2, fg) - t_ref[...]
        rows = jnp.sum(e * e, axis=-1, keepdims=True)
        loss_ref[...] += (0.5 / D) * jnp.sum(rows, axis=0, keepdims=True)
        dx, dg = _rms_bwd(x2, fg, e * (1.0 / D))
        dx_ref[...] = dx
        _add_rows128(dfg_ref, dg)

    tile = pl.BlockSpec((tm, D), lambda i: (i, 0))
    weights = [_resident((2, 4, FF_BLK, D)), _resident((4, FF_BLK, D))]
    kept_specs = [pl.BlockSpec((4, tm, FF_BLK), lambda i: (0, i, 0)),
                  pl.BlockSpec((2, 4, tm, FF_BLK), lambda i: (0, 0, i, 0)), tile]
    kept_shapes = [SDS((4, s, FF_BLK), BF16), SDS((2, 4, s, FF_BLK), BF16), SDS((s, D), BF16)]
    if head is None:
        return pl.pallas_call(
            body, name=f"ffn_fwd_l{layer}", grid=(s // tm,),
            in_specs=[tile, _full((1, D))] + weights, out_specs=[tile] + kept_specs,
            out_shape=[SDS((s, D), F32)] + kept_shapes, compiler_params=_cp("parallel"))(x1, g, wfi, wfo)
    final_g, tgt = head
    return pl.pallas_call(
        body_with_head, name=f"ffn_fwd_loss_l{layer}", grid=(s // tm,),
        in_specs=[tile, _full((1, D))] + weights + [_full((1, D)), tile],
        out_specs=[tile, _full((1, 1)), _full((LANE_ROWS, HD))] + kept_specs,
        out_shape=[SDS((s, D), F32), SDS((1, 1), F32), SDS((LANE_ROWS, HD), F32)] + kept_shapes,
        compiler_params=_cp("arbitrary"))(x1, g, wfi, wfo, final_g, tgt)


def _ffn_bwd(dx2, wfo, factors, wfi, x1, g, layer, tm):
    s = dx2.shape[0]

    def body(dx_ref, wo_ref, f_ref, wi_ref, x_ref, g_ref, dgu_ref, dx1_ref, dg_ref):
        @pl.when(pl.program_id(0) == 0)
        def _():
            dg_ref[...] = jnp.zeros_like(dg_ref)

        dx = dx_ref[...]
        dxb = dx.astype(BF16)
        dh = None
        for k in range(4):
            dff = _dot_nt(dxb, wo_ref[k])
            d_gate = (dff * f_ref[0, k].astype(F32)).astype(BF16)
            d_up = (dff * f_ref[1, k].astype(F32)).astype(BF16)
            dgu_ref[0, k] = d_gate
            dgu_ref[1, k] = d_up
            part = _dot(d_gate, wi_ref[k]) + _dot(d_up, wi_ref[4 + k])
            dh = part if dh is None else dh + part
        dxn, dg = _rms_bwd(x_ref[...], g_ref[...], dh)
        dx1_ref[...] = dx + dxn
        _add_rows128(dg_ref, dg)

    tile = pl.BlockSpec((tm, D), lambda i: (i, 0))
    blk = pl.BlockSpec((2, 4, tm, FF_BLK), lambda i: (0, 0, i, 0))
    return pl.pallas_call(
        body, name=f"ffn_bwd_l{layer}", grid=(s // tm,),
        in_specs=[tile, _resident((4, FF_BLK, D)), blk, _resident((N_DEV, FF_BLK, D)), tile, _full((1, D))],
        out_specs=[blk, tile, _full((LANE_ROWS, HD))],
        out_shape=[SDS((2, 4, s, FF_BLK), BF16), SDS((s, D), F32), SDS((LANE_ROWS, HD), F32)],
        compiler_params=_cp("arbitrary"))(dx2, wfo, factors, wfi, x1, g)


def _mm_nt_rms_bwd(a, a_spec, a_blocks, w, w_is_transposed, x, g, dres, name, tm):
    s = x.shape[0]

    def body(a_ref, w_ref, x_ref, g_ref, dres_ref, dx_ref, dg_ref):
        @pl.when(pl.program_id(0) == 0)
        def _():
            dg_ref[...] = jnp.zeros_like(dg_ref)

        dh = None
        for k, blk in enumerate(a_blocks(a_ref)):
            part = _dot(blk, w_ref[k]) if w_is_transposed else _dot_nt(blk, w_ref[k])
            dh = part if dh is None else dh + part
        dx, dg = _rms_bwd(x_ref[...], g_ref[...], dh)
        dx_ref[...] = dres_ref[...] + dx
        _add_rows128(dg_ref, dg)

    tile = pl.BlockSpec((tm, D), lambda i: (i, 0))
    return pl.pallas_call(
        body, name=name, grid=(s // tm,),
        in_specs=[a_spec, _resident(w.shape), tile, _full((1, D)), tile],
        out_specs=[tile, _full((LANE_ROWS, HD))], out_shape=[SDS((s, D), F32), SDS((LANE_ROWS, HD), F32)],
        compiler_params=_cp("arbitrary"))(a, w, x, g, dres)


def _mm_tn(a, a_spec, b, b_spec, nb, out_shape, out_spec, name, a_is_transposed=True, after=None):
    def body(a_ref, b_ref, *rest):
        o_ref = rest[-1]
        bb = b_ref[...].astype(BF16)
        o_ref[...] = (_dot(a_ref[...], bb) if a_is_transposed else _dot_tn(a_ref[...], bb)).astype(BF16)

    deps = [] if after is None else [after]
    return pl.pallas_call(
        body, name=name, grid=(nb,), in_specs=[a_spec, b_spec] + [_ANY] * len(deps), out_specs=out_spec,
        out_shape=SDS(out_shape, BF16), compiler_params=_cp("parallel"))(a, b, *deps)


def _mixer_bwd(dx1, wo, h0, h1, z, lng, lnb, ws, wst, bsb, layer, tm):
    s = dx1.shape[0]
    nt = s // tm

    def body(dx_ref, wo_ref, h0_ref, h1_ref, zu_ref, zv_ref, zg_ref, za_ref, zb_ref, lng_ref, lnb_ref,
             ws_ref, wst_ref, bsb_ref, dz_ref, dh_ref, dws_ref, dbs_ref, dlng_ref, dlnb_ref,
             du_s, dv_s, ya_s, dbs_acc):
        i = pl.program_id(0)

        @pl.when(i == 0)
        def _():
            for r in (dws_ref, dlng_ref, dlnb_ref, dbs_acc):
                r[...] = jnp.zeros_like(r)

        dm = _dot_nt(dx_ref[...].astype(BF16), wo_ref[...])
        sa = _sigmoid(za_ref[...].astype(F32))
        sb = _sigmoid(zb_ref[...].astype(F32))
        zg = zg_ref[...].astype(F32)
        gg, tg = _gelu(zg)
        hs = h0_ref[...] + h1_ref[...]
        dyb = dm * sb
        dya = dm * sa
        dh_ref[...] = dyb * gg
        dz_ref[:, 2 * D:3 * D] = jnp.zeros((tm, D), BF16)
        dz_ref[:, 3 * D:4 * D] = (dyb * hs * _gelu_grad(zg, tg)).astype(BF16)
        dz_ref[:, 5 * D:6 * D] = (dm * (hs * gg) * (sb * (1.0 - sb))).astype(BF16)

        zu, zv, u, tu, tv, xh, rstd, vb = _gmlp_values(zu_ref, zv_ref, lng_ref, lnb_ref)
        for c in range(tm // HD):
            rs = slice(c * HD, (c + 1) * HD)
            for g in range(HEADS):
                cs = slice(g * HD, (g + 1) * HD)
                vblk = vb[rs, cs]
                mixed = _dot(ws_ref[g], vblk) + bsb_ref[g]
                ya_s[rs, cs] = u[rs, cs] * mixed
                du_s[rs, cs] = dya[rs, cs] * mixed
                dmx = dya[rs, cs] * u[rs, cs]
                dbs_acc[g] += dmx
                dmxb = dmx.astype(BF16)
                dws_ref[g] += _dot_nt(dmxb, vblk)
                dv_s[rs, cs] = _dot(wst_ref[g], dmxb)
        dz_ref[:, 4 * D:5 * D] = (dm * ya_s[...] * (sa * (1.0 - sa))).astype(BF16)
        dv = dv_s[...]
        _add_rows128(dlng_ref, jnp.sum(dv * xh, axis=0, keepdims=True))
        _add_rows128(dlnb_ref, jnp.sum(dv, axis=0, keepdims=True))
        dxh = dv * lng_ref[...]
        dgv = rstd * (dxh - jnp.mean(dxh, axis=-1, keepdims=True)
                      - xh * jnp.mean(dxh * xh, axis=-1, keepdims=True))
        dz_ref[:, 0:D] = (du_s[...] * _gelu_grad(zu, tu)).astype(BF16)
        dz_ref[:, D:2 * D] = (dgv * _gelu_grad(zv, tv)).astype(BF16)

        @pl.when(i == nt - 1)
        def _():
            for g in range(HEADS):
                dbs_ref[g:g + 1, :] = jnp.sum(dbs_acc[g].T, axis=0, keepdims=True)

    tile = pl.BlockSpec((tm, D), lambda i: (i, 0))
    wspec = _full((HEADS, HD, HD))
    return pl.pallas_call(
        body, name=f"mixer_bwd_l{layer}", grid=(nt,),
        in_specs=[tile, _full((D, D)), tile, tile]
        + [pl.BlockSpec((tm, D), lambda i, c=c: (i, c)) for c in (0, 1, 3, 4, 5)]
        + [_full((1, D)), _full((1, D)), wspec, wspec, wspec],
        out_specs=[pl.BlockSpec((tm, N_IN), lambda i: (i, 0)), tile, wspec, _full((HEADS, HD)),
                   _full((LANE_ROWS, HD)), _full((LANE_ROWS, HD))],
        out_shape=[SDS((s, N_IN), BF16), SDS((s, D), F32), SDS((HEADS, HD, HD), F32), SDS((HEADS, HD), F32),
                   SDS((LANE_ROWS, HD), F32), SDS((LANE_ROWS, HD), F32)],
        scratch_shapes=[pltpu.VMEM((tm, D), F32)] * 3 + [pltpu.VMEM((HEADS, HD, HD), F32)],
        compiler_params=_cp("arbitrary"))(dx1, wo, h0, h1, z, z, z, z, z, lng, lnb, ws, wst, bsb)


def _lru_gates_bwd(xcb, gates, h0, h1, g0, g1, wr, wi, lam, layer, tm):
    s = xcb.shape[0]
    nt = s // tm

    def body(xc_ref, r0_ref, i0_ref, r1_ref, i1_ref, h0p_ref, h0_ref, h1_ref, h1n_ref, g0_ref, g1_ref,
             wr_ref, wi_ref, lam_ref, dxc_ref, dwr_ref, dwi_ref, dbr_ref, dbi_ref, dlam_ref):
        i = pl.program_id(0)
        fp, fn = _halo_flags(nt)

        @pl.when(i == 0)
        def _():
            for r in (dwr_ref, dwi_ref, dbr_ref, dbi_ref, dlam_ref):
                r[...] = jnp.zeros_like(r)

        xb = xc_ref[...]
        xc = xb.astype(F32)
        zeros8 = jnp.zeros((8, D), F32)
        h_prev = _taps(h0p_ref[...] * fp, h0_ref[...], zeros8, tm)[1]
        h_next = _taps(zeros8, h1_ref[...], h1n_ref[...] * fn, tm)[3]
        dxc = jnp.zeros((tm, D), F32)
        for d, (g_ref, hsh, r_ref, i_ref) in enumerate(((g0_ref, h_prev, r0_ref, i0_ref),
                                                        (g1_ref, h_next, r1_ref, i1_ref))):
            sp = _softplus(-lam_ref[d:d + 1, :])
            r = r_ref[...].astype(F32)
            ig = i_ref[...].astype(F32)
            a, q = _lru_decay(r, sp)
            rmult = jnp.where(q > 0.0, lax.rsqrt(jnp.where(q > 0.0, q, 1.0)), 0.0)
            mult = q * rmult
            db = g_ref[...]
            da = db * hsh
            dmult = db * (ig * xc)
            di = db * (mult * xc)
            dxc = dxc + db * (mult * ig)
            dla = da * a - dmult * (a * a * rmult)
            dsp_dlam = -_sigmoid(-lam_ref[d:d + 1, :])
            _add_rows128(dlam_ref, jnp.sum(dla * r, axis=0, keepdims=True) * ((-LRU_C) * dsp_dlam), d * LANE_ROWS)
            dpr = dla * sp * (-LRU_C) * (r * (1.0 - r))
            dpi = di * (ig * (1.0 - ig))
            _add_rows128(dbr_ref, jnp.sum(dpr, axis=0, keepdims=True), d * LANE_ROWS)
            _add_rows128(dbi_ref, jnp.sum(dpi, axis=0, keepdims=True), d * LANE_ROWS)
            dprb = dpr.astype(BF16)
            dpib = dpi.astype(BF16)
            parts = []
            for h in range(HEADS):
                cs = slice(h * HD, (h + 1) * HD)
                dwr_ref[d, h] += _dot_tn(xb[:, cs], dprb[:, cs])
                dwi_ref[d, h] += _dot_tn(xb[:, cs], dpib[:, cs])
                parts.append(_dot_nt(dprb[:, cs], wr_ref[d, h]) + _dot_nt(dpib[:, cs], wi_ref[d, h]))
            dxc = dxc + jnp.concatenate(parts, axis=1)
        dxc_ref[...] = dxc.astype(BF16)

    tile = pl.BlockSpec((tm, D), lambda i: (i, 0))
    hp, hc, hn = _halo_specs(tm, s, 0)
    wspec = _full((2, HEADS, HD, HD))
    vspec = _full((2 * LANE_ROWS, HD))
    return pl.pallas_call(
        body, name=f"lru_gates_bwd_l{layer}", grid=(nt,),
        in_specs=[tile] * 5 + [hp, hc, hc, hn, tile, tile, wspec, wspec, _full((2, D))],
        out_specs=[tile, wspec, wspec, vspec, vspec, vspec],
        out_shape=[SDS((s, D), BF16), SDS((2, HEADS, HD, HD), F32), SDS((2, HEADS, HD, HD), F32)]
        + [SDS((2 * LANE_ROWS, HD), F32)] * 3,
        compiler_params=_cp("arbitrary"))(xcb, *gates, h0, h0, h1, h1, g0, g1, wr, wi, lam)


def _conv_bwd(dz, dxc, z, cw, layer, tm):
    s = z.shape[0]
    nt = s // tm

    def body(dz_in, dp_ref, dc_ref, dn_ref, zp_ref, zc_ref, zn_ref, cw_ref, dz_ref, dcw_ref, dcb_ref):
        del dz_in
        fp, fn = _halo_flags(nt)

        @pl.when(pl.program_id(0) == 0)
        def _():
            dcw_ref[...] = jnp.zeros_like(dcw_ref)
            dcb_ref[...] = jnp.zeros_like(dcb_ref)

        dxc_halo = _halo_load(dp_ref, dc_ref, dn_ref, fp, fn)
        dxc = dxc_halo[1]
        dm2, dm1, _, dp1, _ = _taps(*dxc_halo, tm)
        dz_ref[...] = (cw_ref[0:1, :] * dp1 + cw_ref[1:2, :] * dxc + cw_ref[2:3, :] * dm1
                       + cw_ref[3:4, :] * dm2).astype(BF16)
        _, zm1, z0, zp1, zp2 = _taps(*_halo_load(zp_ref, zc_ref, zn_ref, fp, fn), tm)
        for k, zt in enumerate((zm1, z0, zp1, zp2)):
            _add_rows128(dcw_ref, jnp.sum(dxc * zt, axis=0, keepdims=True), k * LANE_ROWS)
        _add_rows128(dcb_ref, jnp.sum(dxc, axis=0, keepdims=True))

    return pl.pallas_call(
        body, name=f"conv_bwd_l{layer}", grid=(nt,),
        in_specs=[pl.BlockSpec(memory_space=pl.ANY), *_halo_specs(tm, s, 0, 16), *_halo_specs(tm, s, 2, 16),
                  _full((4, D))],
        out_specs=[pl.BlockSpec((tm, D), lambda i: (i, 2)), _full((4 * LANE_ROWS, HD)), _full((LANE_ROWS, HD))],
        out_shape=[SDS((s, N_IN), BF16), SDS((4 * LANE_ROWS, HD), F32), SDS((LANE_ROWS, HD), F32)],
        input_output_aliases={0: 0},
        compiler_params=_cp("arbitrary"))(dz, dxc, dxc, dxc, z, z, z, cw)


def _me():
    return lax.axis_index("x"), lax.axis_index("y"), lax.axis_index("c")


def _peer(m):
    x, y, c = _me()
    px = 1 - x if m & 4 else x
    py = 1 - y if m & 2 else y
    pc = 1 - c if m & 1 else c
    return (px, py, pc), 4 * px + 2 * py + pc


_ANY = pl.BlockSpec(memory_space=pl.ANY)
_EXCHANGE_SEMS = [pltpu.SemaphoreType.DMA((N_DEV - 1,)), pltpu.SemaphoreType.DMA((N_DEV - 1,)), pltpu.SemaphoreType.DMA(())]


def _all_gather(v, after, name):
    def body(v_ref, after_ref, o_ref, send_sems, recv_sems, local_sem):
        del after_ref
        x, y, c = _me()
        me = 4 * x + 2 * y + c
        local = pltpu.make_async_copy(v_ref, o_ref.at[me], local_sem)
        local.start()
        sends = []
        for m in range(1, N_DEV):
            dev, _ = _peer(m)
            cp = pltpu.make_async_remote_copy(v_ref, o_ref.at[me], send_sems.at[m - 1], recv_sems.at[m - 1],
                                              device_id=dev, device_id_type=pl.DeviceIdType.MESH)
            cp.start()
            sends.append(cp)
        for m in range(1, N_DEV):
            dev, blk = _peer(m)
            pltpu.make_async_remote_copy(v_ref, o_ref.at[blk], send_sems.at[m - 1], recv_sems.at[m - 1],
                                         device_id=dev, device_id_type=pl.DeviceIdType.MESH).wait_recv()
        for cp in sends:
            cp.wait_send()
        local.wait()

    return pl.pallas_call(
        body, name=name, in_specs=[_ANY, _ANY], out_specs=_ANY,
        out_shape=SDS((N_DEV,) + v.shape, v.dtype), scratch_shapes=_EXCHANGE_SEMS)(v, after)


_HBM = pl.BlockSpec(memory_space=pltpu.HBM)
_SEM = pl.BlockSpec(memory_space=pltpu.SEMAPHORE)
_EFFECT = pltpu.CompilerParams(has_side_effects=pltpu.SideEffectType.DATAFLOW_SIDE_EFFECTING)
_PEER_SEMS = pltpu.SemaphoreType.DMA((N_DEV - 1,))


def _in_hbm(a):
    return pltpu.with_memory_space_constraint(a, pltpu.HBM)


def _remote(src, dst, send_sems, recv_sems, m):
    dev, _ = _peer(m)
    return pltpu.make_async_remote_copy(src, dst, send_sems.at[m - 1], recv_sems.at[m - 1],
                                        device_id=dev, device_id_type=pl.DeviceIdType.MESH)


def _gather_start(lands, after, name):
    n = len(lands)

    def body(*refs):
        land = refs[:n]
        sems = refs[n + 1:3 * n + 1]
        token = refs[-1]
        x, y, c = _me()
        me = 4 * x + 2 * y + c
        for t in range(n):
            for m in range(1, N_DEV):
                _remote(land[t].at[me], land[t].at[me], sems[2 * t], sems[2 * t + 1], m).start()
        token[...] = jnp.zeros_like(token)

    res = pl.pallas_call(
        body, name=name, in_specs=[_HBM] * n + [_ANY],
        out_specs=[_SEM] * (2 * n) + [_HBM] * n + [pl.BlockSpec(memory_space=pltpu.VMEM)],
        out_shape=[_PEER_SEMS] * (2 * n) + [pltpu.HBM(a.shape, a.dtype) for a in lands] + [SDS((8, 128), F32)],
        input_output_aliases={t: 2 * n + t for t in range(n)},
        compiler_params=_EFFECT)(*[_in_hbm(a) for a in lands], after)
    return [(res[2 * t], res[2 * t + 1], res[2 * n + t]) for t in range(n)], res[-1]


def _gather_wait(handle, after, name):
    send_sems, recv_sems, land = handle

    def body(land_ref, ssem, rsem, after_ref, out_ref):
        del after_ref, out_ref
        x, y, c = _me()
        me = 4 * x + 2 * y + c
        for m in range(1, N_DEV):
            _, blk = _peer(m)
            cp = _remote(land_ref.at[me], land_ref.at[blk], ssem, rsem, m)
            cp.wait_send()
            cp.wait_recv()

    return pl.pallas_call(
        body, name=name, in_specs=[_HBM, _SEM, _SEM, _ANY], out_specs=_HBM,
        out_shape=pltpu.HBM(land.shape, land.dtype), input_output_aliases={0: 0},
        compiler_params=_EFFECT)(land, send_sems, recv_sems, after)


FIRST_STAGE = (1, 2, 4, 6)
RELAYED = (2, 4, 6)
OTHER_CORE = 1


def _stage_copy(src, dst, send_sems, recv_sems, k, m):
    dev, _ = _peer(m)
    return pltpu.make_async_remote_copy(src, dst, send_sems.at[k], recv_sems.at[k],
                                        device_id=dev, device_id_type=pl.DeviceIdType.MESH)


def _gather2_start(lands, after, name):
    n = len(lands)

    def body(*refs):
        land = refs[:n]
        sems = refs[n + 1:3 * n + 1]
        token = refs[-1]
        x, y, c = _me()
        me = 4 * x + 2 * y + c
        for t in range(n):
            for k, m in enumerate(FIRST_STAGE):
                _stage_copy(land[t].at[me], land[t].at[me], sems[2 * t], sems[2 * t + 1], k, m).start()
        token[...] = jnp.zeros_like(token)

    stage_sems = pltpu.SemaphoreType.DMA((len(FIRST_STAGE),))
    res = pl.pallas_call(
        body, name=name, in_specs=[_HBM] * n + [_ANY],
        out_specs=[_SEM] * (2 * n) + [_HBM] * n + [pl.BlockSpec(memory_space=pltpu.VMEM)],
        out_shape=[stage_sems] * (2 * n) + [pltpu.HBM(a.shape, a.dtype) for a in lands] + [SDS((8, 128), F32)],
        input_output_aliases={t: 2 * n + t for t in range(n)},
        compiler_params=_EFFECT)(*[_in_hbm(a) for a in lands], after)
    return [(res[2 * t], res[2 * t + 1], res[2 * n + t]) for t in range(n)], res[-1]


def _gather2_relay(handles, after, name):
    n = len(handles)

    def body(*refs):
        land, send1, recv1 = refs[:n], refs[n:2 * n], refs[2 * n:3 * n]
        sems = refs[3 * n + 1:5 * n + 1]
        token = refs[-1]
        x, y, c = _me()
        me = 4 * x + 2 * y + c
        for t in range(n):
            for j, m in enumerate(RELAYED):
                _, blk = _peer(m)
                _stage_copy(land[t].at[me], land[t].at[blk], send1[t], recv1[t], 1 + j, m).wait_recv()
                _stage_copy(land[t].at[blk], land[t].at[blk], sems[2 * t], sems[2 * t + 1], j, OTHER_CORE).start()
        token[...] = jnp.zeros_like(token)

    relay_sems = pltpu.SemaphoreType.DMA((len(RELAYED),))
    lands = [h[2] for h in handles]
    res = pl.pallas_call(
        body, name=name, in_specs=[_HBM] * n + [_SEM] * (2 * n) + [_ANY],
        out_specs=[_SEM] * (2 * n) + [_HBM] * n + [pl.BlockSpec(memory_space=pltpu.VMEM)],
        out_shape=[relay_sems] * (2 * n) + [pltpu.HBM(a.shape, a.dtype) for a in lands] + [SDS((8, 128), F32)],
        input_output_aliases={t: 2 * n + t for t in range(n)},
        compiler_params=_EFFECT)(*lands, *[h[0] for h in handles], *[h[1] for h in handles], after)
    return [(h[0], h[1], res[2 * t], res[2 * t + 1], res[2 * n + t]) for t, h in enumerate(handles)], res[-1]


def _gather2_wait(handle, after, name):
    send1, recv1, send2, recv2, land = handle

    def body(land_ref, s1, r1, s2, r2, after_ref, out_ref):
        del after_ref, out_ref
        x, y, c = _me()
        me = 4 * x + 2 * y + c
        _, other = _peer(OTHER_CORE)
        _stage_copy(land_ref.at[me], land_ref.at[other], s1, r1, 0, OTHER_CORE).wait_recv()
        for k, m in enumerate(FIRST_STAGE):
            _stage_copy(land_ref.at[me], land_ref.at[me], s1, r1, k, m).wait_send()
        for j, m in enumerate(RELAYED):
            _, mine = _peer(m)
            _, theirs = _peer(m ^ OTHER_CORE)
            _stage_copy(land_ref.at[mine], land_ref.at[mine], s2, r2, j, OTHER_CORE).wait_send()
            _stage_copy(land_ref.at[mine], land_ref.at[theirs], s2, r2, j, OTHER_CORE).wait_recv()

    return pl.pallas_call(
        body, name=name, in_specs=[_HBM] + [_SEM] * 4 + [_ANY], out_specs=_HBM,
        out_shape=pltpu.HBM(land.shape, land.dtype), input_output_aliases={0: 0},
        compiler_params=_EFFECT)(land, send1, recv1, send2, recv2, after)


def _exchange_start(ps, name):
    n = len(ps)

    def body(*refs):
        p = refs[:n]
        got = refs[n:2 * n]
        sems = refs[2 * n:5 * n]
        token = refs[-1]
        x, y, c = _me()
        me = 4 * x + 2 * y + c
        for t in range(n):
            pltpu.make_async_copy(p[t].at[me], got[t].at[me], sems[3 * t + 2]).start()
            for m in range(1, N_DEV):
                _, blk = _peer(m)
                _remote(p[t].at[blk], got[t].at[me], sems[3 * t], sems[3 * t + 1], m).start()
        token[...] = jnp.zeros_like(token)

    res = pl.pallas_call(
        body, name=name, in_specs=[_HBM] * (2 * n),
        out_specs=[_SEM] * (3 * n) + [_HBM] * (2 * n) + [pl.BlockSpec(memory_space=pltpu.VMEM)],
        out_shape=[_PEER_SEMS, _PEER_SEMS, pltpu.SemaphoreType.DMA(())] * n
        + [pltpu.HBM(a.shape, a.dtype) for a in ps] * 2 + [SDS((8, 128), F32)],
        input_output_aliases={t: 3 * n + t for t in range(2 * n)},
        compiler_params=_EFFECT)(*[_in_hbm(a) for a in ps], *[_in_hbm(lax.empty(a.shape, a.dtype)) for a in ps])
    return [(res[3 * t], res[3 * t + 1], res[3 * t + 2], res[3 * n + t], res[4 * n + t]) for t in range(n)], res[-1]


def _exchange_wait(handle, after, name):
    send_sems, recv_sems, local_sem, p, got = handle

    def body(p_ref, got_ref, ssem, rsem, lsem, after_ref, p_out, got_out):
        del after_ref, p_out, got_out
        x, y, c = _me()
        me = 4 * x + 2 * y + c
        pltpu.make_async_copy(p_ref.at[me], got_ref.at[me], lsem).wait()
        for m in range(1, N_DEV):
            _, blk = _peer(m)
            cp = _remote(p_ref.at[blk], got_ref.at[blk], ssem, rsem, m)
            cp.wait_send()
            cp.wait_recv()

    return pl.pallas_call(
        body, name=name, in_specs=[_HBM, _HBM, _SEM, _SEM, _SEM, _ANY], out_specs=[_HBM, _HBM],
        out_shape=[pltpu.HBM(p.shape, p.dtype), pltpu.HBM(got.shape, got.dtype)],
        input_output_aliases={0: 0, 1: 1}, compiler_params=_EFFECT)(p, got, send_sems, recv_sems, local_sem, after)[1]


def _cast_into_slot(w, layer, me1, name):
    _, r, c = w.shape
    tr = next(t for t in (512, 352, r) if r % t == 0)

    def body(me_ref, w_ref, o_ref):
        del me_ref
        o_ref[...] = w_ref[...].astype(BF16)

    return pl.pallas_call(
        body, name=name,
        grid_spec=pltpu.PrefetchScalarGridSpec(
            num_scalar_prefetch=1, grid=(r // tr,),
            in_specs=[pl.BlockSpec((None, tr, c), lambda i, me: (layer, i, 0))],
            out_specs=pl.BlockSpec((None, tr, c), lambda i, me: (me[0], i, 0))),
        out_shape=SDS((N_DEV, r, c), BF16), compiler_params=_cp("arbitrary"))(me1, w)


def _cast_all_into_slots(ws, layers, me1, after, name):
    n = len(ws)

    def body(me_ref, *refs):
        del me_ref
        for w_ref, o_ref in zip(refs[:n], refs[n + 1:]):
            o_ref[...] = w_ref[...].astype(BF16)

    return pl.pallas_call(
        body, name=name,
        grid_spec=pltpu.PrefetchScalarGridSpec(
            num_scalar_prefetch=1, grid=(1,),
            in_specs=[pl.BlockSpec((None,) + a.shape[1:], lambda i, me, l=l: (l, 0, 0)) for a, l in zip(ws, layers)]
            + [_ANY],
            out_specs=[pl.BlockSpec((None,) + a.shape[1:], lambda i, me: (me[0], 0, 0)) for a in ws]),
        out_shape=[SDS((N_DEV,) + a.shape[1:], BF16) for a in ws],
        compiler_params=_cp("arbitrary"))(me1, *ws, after)


def _sum8_into_slot(p, me1, name):
    _, r, c = p.shape

    def body(me_ref, p_ref, o_ref):
        del me_ref
        acc = p_ref[0]
        for k in range(1, N_DEV):
            acc = acc + p_ref[k]
        o_ref[...] = acc

    return pl.pallas_call(
        body, name=name,
        grid_spec=pltpu.PrefetchScalarGridSpec(
            num_scalar_prefetch=1, grid=(1,),
            in_specs=[pl.BlockSpec(p.shape, lambda i, me: (0, 0, 0))],
            out_specs=pl.BlockSpec((None, r, c), lambda i, me: (me[0], 0, 0))),
        out_shape=SDS(p.shape, F32), compiler_params=_cp("arbitrary"))(me1, p)


def _adamw(w, g, m, v):
    m = ADAM_B1 * m + (1.0 - ADAM_B1) * g
    v = ADAM_B2 * v + (1.0 - ADAM_B2) * (g * g)
    m_hat = m / (1.0 - ADAM_B1 ** ADAM_STEP)
    v_hat = v / (1.0 - ADAM_B2 ** ADAM_STEP)
    delta = -ADAM_LR * (m_hat / (jnp.sqrt(v_hat) + ADAM_EPS) + ADAM_WD * w)
    return delta, m, v


def _adam_shard(parts, w, m, v, layer, prev, name):
    _, r, c = parts.shape
    tr = next(t for t in (512, 352, r) if r % t == 0)
    n_prev = 0 if prev is None else 4

    def body(*refs):
        p_ref, w_ref, m_ref, v_ref = refs[:4]
        g_ref, d_ref, nm_ref, nv_ref = refs[4 + n_prev:]
        g = p_ref[0].astype(F32)
        for k in range(1, N_DEV):
            g = g + p_ref[k].astype(F32)
        delta, nm, nv = _adamw(w_ref[...], g, m_ref[...], v_ref[...])
        g_ref[...] = g
        d_ref[...] = delta
        nm_ref[...] = nm
        nv_ref[...] = nv

    blk = pl.BlockSpec((None, tr, c), lambda i: (layer, i, 0))
    return pl.pallas_call(
        body, name=name, grid=(r // tr,),
        in_specs=[pl.BlockSpec((N_DEV, tr, c), lambda i: (0, i, 0)), blk, blk, blk] + [_ANY] * n_prev,
        out_specs=[blk] * 4, out_shape=[SDS(w.shape, F32)] * 4,
        input_output_aliases={4 + k: k for k in range(n_prev)},
        compiler_params=_cp("parallel"))(parts, w, m, v, *(prev or ()))


SMALL_MATRICES = [("lru_w_r", 2048), ("lru_w_i", 2048), ("gmlp_w_s", 1024)]
SMALL_VECTORS = [("norm1_g", 8), ("gmlp_ln_g", 8), ("gmlp_ln_b", 8), ("gmlp_b_s", 8), ("conv_w", 32), ("conv_b", 8),
                 ("lru_b_r", 16), ("lru_b_i", 16), ("lru_lambda", 16), ("norm2_g", 8), ("final_g", 8)]
SMALL_VECTOR_ROW0 = sum(n for _, n in SMALL_MATRICES)
SMALL_VECTOR_BLOCK = 256
SMALL_ROWS = SMALL_VECTOR_ROW0 + SMALL_VECTOR_BLOCK


def _pack_small(small):
    parts = [small[k] for k, _ in SMALL_MATRICES]
    parts += [small[k] if k in small else jnp.zeros((n, HD), F32) for k, n in SMALL_VECTORS]
    flat = jnp.concatenate(parts)
    return jnp.pad(flat, ((0, SMALL_ROWS - flat.shape[0]), (0, 0))).reshape(N_DEV, SMALL_ROWS // N_DEV, HD)


def _adam_matrix(g0, g1, w, m, v, row0, name):
    _, rows, _ = w.shape
    tr = 512

    def body(g0_ref, g1_ref, w_ref, m_ref, v_ref, g_ref, d_ref, nm_ref, nv_ref):
        for l, src in enumerate((g0_ref, g1_ref)):
            g = src[...]
            delta, nm, nv = _adamw(w_ref[l], g, m_ref[l], v_ref[l])
            g_ref[l] = g
            d_ref[l] = delta
            nm_ref[l] = nm
            nv_ref[l] = nv

    gspec = pl.BlockSpec((tr, HD), lambda i: (row0 // tr + i, 0))
    blk = pl.BlockSpec((2, tr, HD), lambda i: (0, i, 0))
    return pl.pallas_call(body, name=name, grid=(rows // tr,), in_specs=[gspec, gspec] + [blk] * 3,
                          out_specs=[blk] * 4, out_shape=[SDS(w.shape, F32)] * 4,
                          compiler_params=_cp("parallel"))(g0, g1, w, m, v)


def _adam_vectors(g0, g1, dg1_parts, me1, ws, ms, vs):
    names = [k for k, _ in SMALL_VECTORS]
    n = len(names)

    def lanes(rows8):
        return jnp.concatenate([rows8[k:k + 1, :] for k in range(LANE_ROWS)], axis=1)

    def body(me_ref, g0_ref, g1_ref, dg1_ref, *refs):
        w_refs, m_refs, v_refs = refs[:n], refs[n:2 * n], refs[2 * n:3 * n]
        outs = refs[3 * n:]
        me = me_ref[0]
        g_refs = (g0_ref, g1_ref)

        def emit(i, idx, g):
            delta, nm, nv = _adamw(w_refs[i][idx], g, m_refs[i][idx], v_refs[i][idx])
            for j, val in enumerate((g, delta, nm, nv)):
                outs[4 * i + j][idx] = val

        off = 0
        for i, (name, rows) in enumerate(SMALL_VECTORS):
            for l in range(2):
                row = (slice(l, l + 1), slice(None))
                if name == "final_g":
                    if l == 1:
                        emit(i, (slice(0, 1), slice(None)), lanes(g1_ref[off:off + rows, :]))
                elif name == "norm1_g":
                    if l == 1:
                        emit(i, row, lanes(g0_ref[off:off + rows, :]))
                    else:
                        total = dg1_ref[0]
                        for k in range(1, N_DEV):
                            total = total + dg1_ref[k]
                        emit(i, row, lanes(total))
                elif name == "gmlp_b_s":
                    emit(i, (l,), g_refs[l][off:off + rows, :])
                elif rows == LANE_ROWS:
                    emit(i, row, lanes(g_refs[l][off:off + rows, :]))
                else:
                    for r in range(rows // LANE_ROWS):
                        emit(i, (l, slice(r, r + 1), slice(None)), g_refs[l][pl.ds(off + r * LANE_ROWS + me, 1), :])
            off += rows

    args = [ws[k] for k in names] + [ms[k] for k in names] + [vs[k] for k in names]
    gspec = pl.BlockSpec((SMALL_VECTOR_BLOCK, HD), lambda i, me: (SMALL_VECTOR_ROW0 // SMALL_VECTOR_BLOCK, 0))
    res = pl.pallas_call(
        body, name="adam_vectors",
        grid_spec=pltpu.PrefetchScalarGridSpec(
            num_scalar_prefetch=1, grid=(1,),
            in_specs=[gspec, gspec, _full(dg1_parts.shape)] + [_full(a.shape) for a in args],
            out_specs=[_full(ws[k].shape) for k in names for _ in range(4)]),
        out_shape=[SDS(ws[k].shape, F32) for k in names for _ in range(4)],
        compiler_params=_cp("arbitrary"))(me1, g0, g1, dg1_parts, *args)
    return {k: list(res[4 * i:4 * i + 4]) for i, k in enumerate(names)}


def _after(a, *tokens):
    for token in tokens:
        if token is not None:
            a = a + token[0:1, 0:1]
    return a


def _local_step(x, tgt, p, get_w, hook=lambda stage, layer, payload: None):
    s = x.shape[0]
    tm = _row_tile(s)
    wsb = p["gmlp_w_s"].astype(BF16)
    wstb = jnp.swapaxes(p["gmlp_w_s"], -1, -2).astype(BF16)
    bsb = jnp.broadcast_to(p["gmlp_b_s"][..., None], p["gmlp_w_s"].shape)
    wrb = p["lru_w_r"].astype(BF16)
    wib = p["lru_w_i"].astype(BF16)
    saved = []
    for l in range(2):
        win = get_w("w_in", l, x)
        z, h1 = _norm_inproj(x, _after(p["norm1_g"][l][None], hook("pre_inproj", l, win)), win, l, tm)
        a0, b0, a1, b1, xcb, *gates = _lru_gates_fwd(z, p["conv_w"][l], p["conv_b"][l][None], wrb[l], wib[l],
                                                     p["lru_b_r"][l], p["lru_b_i"][l], p["lru_lambda"][l], l, tm)
        h0, hr = _lru_scan(a0, b0, a1, b1, False, l)
        lng = _after(p["gmlp_ln_g"][l][None], hook("pre_gmlp", l, h0))
        wout = get_w("w_out", l, lng)
        x1, mg = _mixer_fwd(x, h0, hr, z, lng, p["gmlp_ln_b"][l][None], wsb[l], bsb[l], wout, l, tm)
        wfi = get_w("w_ffn_in", l, x1)
        wfo = get_w("w_ffn_out", l, x1)
        if l == 0:
            x2, ff, dff, h2 = _ffn_fwd(x1, p["norm2_g"][l][None], wfi, wfo, l, tm)
        else:
            dx, loss, dfg, ff, dff, h2 = _ffn_fwd(x1, p["norm2_g"][l][None], wfi, wfo, l, tm,
                                                  head=(p["final_g"][None], tgt))
        saved.append((x, z, h1, a0, a1, h0, hr, x1, mg, ff, dff, h2, win, wout, wfi, wfo, xcb, gates))
        x = x2
    pending = None
    for l in (1, 0):
        x0, z, h1, a0, a1, h0, hr, x1, mg, ff, dff, h2, win, wout, wfi, wfo, xcb, gates = saved[l]
        dgu, dx1, dg2 = _ffn_bwd(dx, wfo, dff, wfi.reshape(N_DEV, FF_BLK, D), x1, p["norm2_g"][l][None], l, tm)
        d_wfo = _mm_tn(ff, pl.BlockSpec((None, s, FF_BLK), lambda j: (j, 0, 0)), dx, _resident((s, D)),
                       4, (4, FF_BLK, D), pl.BlockSpec((None, FF_BLK, D), lambda j: (j, 0, 0)),
                       f"dw_ffn_out_l{l}", a_is_transposed=False)
        dgu8 = dgu.reshape(N_DEV, s, FF_BLK)
        d_wfi = _mm_tn(dgu8, pl.BlockSpec((None, s, FF_BLK), lambda j: (j, 0, 0)), h2, _resident((s, D)),
                       N_DEV, (N_DEV, FF_BLK, D), pl.BlockSpec((None, FF_BLK, D), lambda j: (j, 0, 0)),
                       f"dw_ffn_in_l{l}", a_is_transposed=False)
        token = hook("ffn_partials", l, dict(w_ffn_out=d_wfo.reshape(N_DEV, D_FF // N_DEV, D), w_ffn_in=d_wfi))
        pending = hook("mid_backward", l, dx1)
        dz, dh, dws, dbs, dlng, dlnb = _mixer_bwd(dx1, wout, h0, hr, z, _after(p["gmlp_ln_g"][l][None], token),
                                                  p["gmlp_ln_b"][l][None], wsb[l], wstb[l], bsb[l], l, tm)
        d_wout = _mm_tn(mg, _resident((D, s)), dx1, pl.BlockSpec((s, D // 2), lambda j: (0, j)),
                        2, (D, D), pl.BlockSpec((D, D // 2), lambda j: (0, j)), f"dw_out_l{l}")
        g1, g0 = _lru_scan(a1, dh, a0, dh, True, l)
        dxc, dwr, dwi, dbr, dbi, dlam = _lru_gates_bwd(
            xcb, gates, h0, hr, g0, g1, wrb[l], wib[l], _after(p["lru_lambda"][l], pending), l, tm)
        dz, dcw, dcb = _conv_bwd(dz, dxc, z, p["conv_w"][l], l, tm)
        small = dict(lru_w_r=dwr.reshape(-1, HD), lru_w_i=dwi.reshape(-1, HD), gmlp_w_s=dws.reshape(-1, HD),
                     gmlp_ln_g=dlng, gmlp_ln_b=dlnb, gmlp_b_s=dbs, conv_w=dcw, conv_b=dcb, lru_b_r=dbr,
                     lru_b_i=dbi, lru_lambda=dlam, norm2_g=dg2)
        if l == 1:
            small["final_g"] = dfg
        else:
            small["norm1_g"] = dg1
        started = hook("small_grads", l, small)
        d_win = _mm_tn(h1, _resident((D, s)), dz, pl.BlockSpec((s, IN_BLK), lambda j: (0, j)),
                       N_DEV, (N_DEV, D, IN_BLK), pl.BlockSpec((None, D, IN_BLK), lambda j: (j, 0, 0)),
                       f"dw_in_l{l}", after=started)
        token = hook("mixer_partials", l, dict(w_out=d_wout.reshape(N_DEV, D // N_DEV, D), w_in=d_win))
        dx, dg1 = _mm_nt_rms_bwd(
            dz, pl.BlockSpec((tm, N_IN), lambda i: (i, 0)),
            lambda r: [r[:, k * IN_BLK:(k + 1) * IN_BLK] for k in range(N_DEV)],
            win, False, x0, _after(p["norm1_g"][l][None], token, started, pending), dx1, f"inproj_bwd_dx_l{l}", tm)
        pending = None
    return loss, dx, dg1


_REPL = ["norm1_g", "gmlp_ln_g", "gmlp_ln_b", "gmlp_w_s", "gmlp_b_s", "conv_b", "lru_w_r", "lru_w_i", "norm2_g", "final_g"]
_LANE_SHARDED = ["conv_w", "lru_b_r", "lru_b_i", "lru_lambda"]
_BIG = ["w_in", "w_out", "w_ffn_in", "w_ffn_out"]
_ORDER = ["norm1_g", "w_in", "gmlp_ln_g", "gmlp_ln_b", "gmlp_w_s", "gmlp_b_s", "conv_w", "conv_b", "lru_w_r", "lru_b_r",
          "lru_w_i", "lru_b_i", "lru_lambda", "w_out", "norm2_g", "w_ffn_in", "w_ffn_out", "final_g"]


def kernel(x, norm1_g, w_in, gmlp_ln_g, gmlp_ln_b, gmlp_w_s, gmlp_b_s, conv_w, conv_b, lru_w_r, lru_b_r, lru_w_i, lru_b_i, lru_lambda, w_out, norm2_g, w_ffn_in, w_ffn_out, final_g, loss_target, m_norm1_g, m_w_in, m_gmlp_ln_g, m_gmlp_ln_b, m_gmlp_w_s, m_gmlp_b_s, m_conv_w, m_conv_b, m_lru_w_r, m_lru_b_r, m_lru_w_i, m_lru_b_i, m_lru_lambda, m_w_out, m_norm2_g, m_w_ffn_in, m_w_ffn_out, m_final_g, v_norm1_g, v_w_in, v_gmlp_ln_g, v_gmlp_ln_b, v_gmlp_w_s, v_gmlp_b_s, v_conv_w, v_conv_b, v_lru_w_r, v_lru_b_r, v_lru_w_i, v_lru_b_i, v_lru_lambda, v_w_out, v_norm2_g, v_w_ffn_in, v_w_ffn_out, v_final_g):
    w = dict(norm1_g=norm1_g, w_in=w_in, gmlp_ln_g=gmlp_ln_g, gmlp_ln_b=gmlp_ln_b, gmlp_w_s=gmlp_w_s, gmlp_b_s=gmlp_b_s,
             conv_w=conv_w, conv_b=conv_b, lru_w_r=lru_w_r, lru_b_r=lru_b_r, lru_w_i=lru_w_i, lru_b_i=lru_b_i,
             lru_lambda=lru_lambda, w_out=w_out, norm2_g=norm2_g, w_ffn_in=w_ffn_in, w_ffn_out=w_ffn_out, final_g=final_g)
    mom = dict(norm1_g=m_norm1_g, w_in=m_w_in, gmlp_ln_g=m_gmlp_ln_g, gmlp_ln_b=m_gmlp_ln_b, gmlp_w_s=m_gmlp_w_s,
               gmlp_b_s=m_gmlp_b_s, conv_w=m_conv_w, conv_b=m_conv_b, lru_w_r=m_lru_w_r, lru_b_r=m_lru_b_r,
               lru_w_i=m_lru_w_i, lru_b_i=m_lru_b_i, lru_lambda=m_lru_lambda, w_out=m_w_out, norm2_g=m_norm2_g,
               w_ffn_in=m_w_ffn_in, w_ffn_out=m_w_ffn_out, final_g=m_final_g)
    var = dict(norm1_g=v_norm1_g, w_in=v_w_in, gmlp_ln_g=v_gmlp_ln_g, gmlp_ln_b=v_gmlp_ln_b, gmlp_w_s=v_gmlp_w_s,
               gmlp_b_s=v_gmlp_b_s, conv_w=v_conv_w, conv_b=v_conv_b, lru_w_r=v_lru_w_r, lru_b_r=v_lru_b_r,
               lru_w_i=v_lru_w_i, lru_b_i=v_lru_b_i, lru_lambda=v_lru_lambda, w_out=v_w_out, norm2_g=v_norm2_g,
               w_ffn_in=v_w_ffn_in, w_ffn_out=v_w_ffn_out, final_g=v_final_g)
    for src in (w, mom, var):
        src["w_ffn_in"] = jnp.swapaxes(src["w_ffn_in"], 1, 2)
    xi, yi, ci = _me()
    me = 4 * xi + 2 * yi + ci

    lane_shapes = [w[k].shape for k in _LANE_SHARDED]
    lane_rows = sum(a[0] * a[1] for a in lane_shapes)
    packed = jnp.concatenate([w[k].reshape(-1, HD) for k in _LANE_SHARDED])
    packed = jnp.pad(packed, ((0, -lane_rows % 8), (0, 0)))

    me1 = jnp.reshape(me, (1,)).astype(jnp.int32)
    gathers = {}
    exchanges = {}
    views = dict(w_in=(N_DEV, D, IN_BLK), w_out=(D, D), w_ffn_in=(2, 4, FF_BLK, D), w_ffn_out=(4, FF_BLK, D))
    small_ex = {}
    small_ag = {}

    casts = {}

    def start_gather(names, l, after):
        lands = [casts[(k, l)] if (k, l) in casts else _cast_into_slot(w[k], l, me1, f"cast_{k}_l{l}") for k in names]
        started, tok = _gather2_start(lands, after, f"gather_start_{'_'.join(names)}_l{l}")
        gathers.update({(k, l): h for k, h in zip(names, started)})
        return tok

    def relay_gather(names, l, after):
        relayed, tok = _gather2_relay([gathers[(k, l)] for k in names], after, f"gather_relay_{'_'.join(names)}_l{l}")
        gathers.update({(k, l): h for k, h in zip(names, relayed)})
        return tok

    def get_w(k, l, after):
        if (k, l) == ("w_in", 1):
            after = relay_gather(_BIG[:1], l, after)
        return _gather2_wait(gathers[(k, l)], after, f"gather_wait_{k}_l{l}").reshape(views[k])

    def hook(stage, l, payload):
        if stage == "pre_inproj":
            return start_gather(_BIG[1:], l, payload)
        if stage == "pre_gmlp":
            tok = relay_gather(_BIG[1:], l, payload)
            return tok + start_gather(_BIG[:1], l + 1, tok) if l == 0 else tok
        if stage == "small_grads":
            (small_ex[l],), tok = _exchange_start([_pack_small(payload)], f"exchange_start_small_l{l}")
            return tok
        if stage == "mid_backward":
            return reduce_small(l + 1, payload) if l == 0 else None
        extra = reduce_small(0, payload["w_in"]) if (stage, l) == ("mixer_partials", 0) else None
        started, tok = _exchange_start(list(payload.values()), f"exchange_start_{'_'.join(payload)}_l{l}")
        exchanges.update({(k, l): h for k, h in zip(payload, started)})
        return tok if extra is None else tok + extra

    def reduce_small(l, after):
        got = _exchange_wait(small_ex[l], after, f"exchange_wait_small_l{l}")
        mine = _sum8_into_slot(got, me1, f"sum_small_l{l}")
        (small_ag[l],), tok = _gather_start([mine], got, f"gather_start_small_l{l}")
        return tok

    land = lax.dynamic_update_slice(jnp.zeros((N_DEV,) + packed.shape, F32), packed[None], (me, 0, 0))
    (lanes_handle,), token = _gather_start([land], packed, "gather_start_lanes")
    token = start_gather(_BIG[:1], 0, token)
    later = [(k, l) for l in range(2) for k in _BIG if (k, l) != ("w_in", 0)]
    casts.update(zip(later, _cast_all_into_slots([w[k] for k, _ in later], [l for _, l in later], me1, token,
                                                 "cast_later_weights")))
    token = relay_gather(_BIG[:1], 0, casts[later[0]])
    lanes = _gather_wait(lanes_handle, token, "gather_wait_lanes")
    params = {k: w[k] for k in _REPL}
    off = 0
    for k, shp in zip(_LANE_SHARDED, lane_shapes):
        n = shp[0] * shp[1]
        params[k] = jnp.swapaxes(lanes[:, off:off + n], 0, 1).reshape(shp[0], shp[1], D)
        off += n
    loss, dx, dg1 = _local_step(x[0], loss_target[0], params, get_w, hook)

    out = {}
    after = dx
    for k, l in [(k, l) for k in ("w_ffn_out", "w_ffn_in") for l in (1, 0)] + [("w_out", 1), ("w_in", 1)]:
        got = _exchange_wait(exchanges[(k, l)], after, f"exchange_wait_{k}_l{l}")
        out[k] = _adam_shard(got, w[k], mom[k], var[k], l, out.get(k), f"adam_{k}_l{l}")
        after = out[k][3]
    g_small = [_gather_wait(small_ag[l], after, f"gather_wait_small_l{l}").reshape(SMALL_ROWS, HD) for l in (0, 1)]
    row0 = 0
    for k, rows in SMALL_MATRICES:
        res = _adam_matrix(*g_small, *[src[k].reshape(2, rows, HD) for src in (w, mom, var)], row0, f"adam_{k}")
        out[k] = [a.reshape(w[k].shape) for a in res]
        after = res[3]
        row0 += rows
    for k in ("w_out", "w_in"):
        got = _exchange_wait(exchanges[(k, 0)], after, f"exchange_wait_{k}_l0")
        out[k] = _adam_shard(got, w[k], mom[k], var[k], 0, out[k], f"adam_{k}_l0")
    out["w_ffn_in"] = [jnp.swapaxes(a, 1, 2) for a in out["w_ffn_in"]]
    as_rows = lambda a: a.reshape(1, D) if a.ndim == 1 else a
    vec = _adam_vectors(*g_small, _all_gather(dg1, out["w_in"][3], "gather_norm1_grad"), me1,
                        *[{k: as_rows(src[k]) for k, _ in SMALL_VECTORS} for src in (w, mom, var)])
    out.update({k: [a.reshape(w[k].shape) for a in res] for k, res in vec.items()})

    loss = lax.psum(loss[0, 0], MESH_AXES)
    return (loss, dx[None], *[out[k][0] for k in _ORDER], *[out[k][1] for k in _ORDER],
            *[out[k][2] for k in _ORDER], *[out[k][3] for k in _ORDER])
```

```python
import jax
import jax.numpy as jnp
from jax import lax
from jax.experimental import pallas as pl
from jax.experimental.pallas import tpu as pltpu

F32 = jnp.float32
BF16 = jnp.bfloat16
SDS = jax.ShapeDtypeStruct

D = 1024
N_IN = 6 * D
D_FF = 2816
N_DEV = 8
IN_BLK = N_IN // N_DEV
FF_BLK = 2 * D_FF // N_DEV
HEADS = 8
HD = 128
EPS = 1e-6
LRU_C = 8.0
MESH_AXES = ("x", "y", "c")

ADAM_LR = 0.001
ADAM_B1 = 0.9
ADAM_B2 = 0.999
ADAM_EPS = 1e-08
ADAM_WD = 0.01
ADAM_STEP = 10

VMEM_LIMIT = 60 * 2**20


def _cp(*sem, **kw):
    return pltpu.CompilerParams(dimension_semantics=sem, vmem_limit_bytes=VMEM_LIMIT, **kw)


def _row_tile(s):
    return 512 if s >= 1024 else s // 2


_GELU_C = 0.7978845608028654


def _gelu(x):
    t = jnp.tanh(_GELU_C * (x + 0.044715 * (x * x * x)))
    return 0.5 * x * (1.0 + t), t


def _gelu_grad(x, t):
    return 0.5 * (1.0 + t) + 0.5 * x * (1.0 - t * t) * (_GELU_C * (1.0 + 0.134145 * (x * x)))


def _sigmoid(x):
    return 0.5 + 0.5 * jnp.tanh(0.5 * x)


def _softplus(x):
    e = jnp.exp(-jnp.abs(x))
    w = 1.0 + e
    l1p = jnp.where(w == 1.0, e, jnp.log(w) * e / jnp.where(w == 1.0, 1.0, w - 1.0))
    return jnp.maximum(x, 0.0) + l1p


def _rms_fwd(x, g):
    r = lax.rsqrt(jnp.mean(x * x, axis=-1, keepdims=True) + EPS)
    return x * r * g


def _rms_bwd(x, g, dh):
    r = lax.rsqrt(jnp.mean(x * x, axis=-1, keepdims=True) + EPS)
    xh = x * r
    dxh = dh * g
    dx = r * (dxh - xh * jnp.mean(dxh * xh, axis=-1, keepdims=True))
    dg = jnp.sum(dh * xh, axis=0, keepdims=True)
    return dx, dg


LANE_ROWS = D // HD


def _add_rows128(ref, vec, row0=0):
    for i in range(vec.shape[0]):
        for k in range(LANE_ROWS):
            j = row0 + i * LANE_ROWS + k
            ref[j:j + 1, :] += vec[i:i + 1, k * HD:(k + 1) * HD]


def _dot(a, b):
    return jnp.dot(a, b, preferred_element_type=F32)


def _dot_nt(a, b):
    return lax.dot_general(a, b, (((1,), (1,)), ((), ())), preferred_element_type=F32)


def _dot_tn(a, b):
    return lax.dot_general(a, b, (((0,), (0,)), ((), ())), preferred_element_type=F32)


def _taps(prev, cur, nxt, tm):
    hr = prev.shape[0]
    ext = jnp.concatenate([prev, cur, nxt], axis=0)
    n = tm + 2 * hr
    sl = slice(hr, hr + tm)
    return (pltpu.roll(ext, 2, 0)[sl], pltpu.roll(ext, 1, 0)[sl], cur,
            pltpu.roll(ext, n - 1, 0)[sl], pltpu.roll(ext, n - 2, 0)[sl])


def _halo_specs(tm, s, col, rows=8):
    nb = s // rows
    r = tm // rows
    return (pl.BlockSpec((rows, D), lambda i: (jnp.maximum(i * r - 1, 0), col)),
            pl.BlockSpec((tm, D), lambda i: (i, col)),
            pl.BlockSpec((rows, D), lambda i: (jnp.minimum((i + 1) * r, nb - 1), col)))


def _halo_load(prev_ref, cur_ref, next_ref, fp, fn):
    return prev_ref[...].astype(F32) * fp, cur_ref[...].astype(F32), next_ref[...].astype(F32) * fn


def _halo_flags(nt):
    i = pl.program_id(0)
    return (i > 0).astype(F32), (i < nt - 1).astype(F32)


def _full(shape):
    nd = len(shape)
    return pl.BlockSpec(shape, lambda *_: (0,) * nd)


def _resident(shape):
    nd = len(shape)
    return pl.BlockSpec(shape, lambda *_: (0,) * nd, pipeline_mode=pl.Buffered(1))


def _norm_inproj(x, g, w, layer, tm):
    s = x.shape[0]

    def body(x_ref, g_ref, w_ref, z_ref, ht_ref):
        h32 = _rms_fwd(x_ref[...], g_ref[...])
        ht_ref[...] = h32.T.astype(BF16)
        h = h32.astype(BF16)
        for j in range(N_DEV):
            z_ref[:, j * IN_BLK:(j + 1) * IN_BLK] = _dot(h, w_ref[j]).astype(BF16)

    return pl.pallas_call(
        body, name=f"norm_inproj_l{layer}", grid=(s // tm,),
        in_specs=[pl.BlockSpec((tm, D), lambda i: (i, 0)), _full((1, D)), _resident((N_DEV, D, IN_BLK))],
        out_specs=[pl.BlockSpec((tm, N_IN), lambda i: (i, 0)), pl.BlockSpec((D, tm), lambda i: (0, i))],
        out_shape=[SDS((s, N_IN), BF16), SDS((D, s), BF16)],
        compiler_params=_cp("parallel"))(x, g, w)


def _gmlp_values(zu_ref, zv_ref, lng_ref, lnb_ref):
    zu = zu_ref[...].astype(F32)
    zv = zv_ref[...].astype(F32)
    u, tu = _gelu(zu)
    gv, tv = _gelu(zv)
    xc = gv - jnp.mean(gv, axis=-1, keepdims=True)
    rstd = lax.rsqrt(jnp.mean(xc * xc, axis=-1, keepdims=True) + EPS)
    xh = xc * rstd
    vb = (xh * lng_ref[...] + lnb_ref[...]).astype(BF16)
    return zu, zv, u, tu, tv, xh, rstd, vb


def _mixer_fwd(x, h0, h1, z, lng, lnb, ws, bsb, wo, layer, tm):
    s = x.shape[0]

    def body(x_ref, h0_ref, h1_ref, zu_ref, zv_ref, zg_ref, za_ref, zb_ref, lng_ref, lnb_ref, ws_ref, bsb_ref,
             wo_ref, x1_ref, mg_ref, ya_s):
        _, _, u, _, _, _, _, vb = _gmlp_values(zu_ref, zv_ref, lng_ref, lnb_ref)
        for c in range(tm // HD):
            rs = slice(c * HD, (c + 1) * HD)
            for g in range(HEADS):
                cs = slice(g * HD, (g + 1) * HD)
                ya_s[rs, cs] = u[rs, cs] * (_dot(ws_ref[g], vb[rs, cs]) + bsb_ref[g])
        gg, _ = _gelu(zg_ref[...].astype(F32))
        yb = (h0_ref[...] + h1_ref[...]) * gg
        m32 = _sigmoid(za_ref[...].astype(F32)) * ya_s[...] + _sigmoid(zb_ref[...].astype(F32)) * yb
        mg_ref[...] = m32.T.astype(BF16)
        x1_ref[...] = x_ref[...] + _dot(m32.astype(BF16), wo_ref[...])

    tile = pl.BlockSpec((tm, D), lambda i: (i, 0))
    wspec = _full((HEADS, HD, HD))
    return pl.pallas_call(
        body, name=f"mixer_fwd_l{layer}", grid=(s // tm,),
        in_specs=[tile, tile, tile] + [pl.BlockSpec((tm, D), lambda i, c=c: (i, c)) for c in (0, 1, 3, 4, 5)]
        + [_full((1, D)), _full((1, D)), wspec, wspec, _full((D, D))],
        out_specs=[tile, pl.BlockSpec((D, tm), lambda i: (0, i))], out_shape=[SDS((s, D), F32), SDS((D, s), BF16)],
        scratch_shapes=[pltpu.VMEM((tm, D), F32)],
        compiler_params=_cp("parallel"))(x, h0, h1, z, z, z, z, z, lng, lnb, ws, bsb, wo)


def _conv(taps, cw_ref, cb_ref):
    _, m1, c0, p1, p2 = taps
    return cb_ref[...] + m1 * cw_ref[0:1, :] + c0 * cw_ref[1:2, :] + p1 * cw_ref[2:3, :] + p2 * cw_ref[3:4, :]


def _heads_dot(xb, w_ref, d):
    return jnp.concatenate([_dot(xb[:, h * HD:(h + 1) * HD], w_ref[d, h]) for h in range(HEADS)], axis=1)


def _lru_decay(r, sp):
    la = (-LRU_C) * r * sp
    a = jnp.exp(la)
    return a, jnp.tanh(-la) * (a * a + 1.0)


def _lru_gates_fwd(z, cw, cb, wr, wi, br, bi, lam, layer, tm):
    s = z.shape[0]
    nt = s // tm

    def body(zp_ref, zc_ref, zn_ref, cw_ref, cb_ref, wr_ref, wi_ref, br_ref, bi_ref, lam_ref,
             a0_ref, b0_ref, a1_ref, b1_ref, xc_ref, r0_ref, i0_ref, r1_ref, i1_ref):
        fp, fn = _halo_flags(nt)
        xc = _conv(_taps(*_halo_load(zp_ref, zc_ref, zn_ref, fp, fn), tm), cw_ref, cb_ref)
        xb = xc.astype(BF16)
        xc_ref[...] = xb
        for d, (a_ref, b_ref, r_ref, i_ref) in enumerate(((a0_ref, b0_ref, r0_ref, i0_ref),
                                                          (a1_ref, b1_ref, r1_ref, i1_ref))):
            r = _sigmoid(_heads_dot(xb, wr_ref, d) + br_ref[d:d + 1, :])
            ig = _sigmoid(_heads_dot(xb, wi_ref, d) + bi_ref[d:d + 1, :])
            a, q = _lru_decay(r, _softplus(-lam_ref[d:d + 1, :]))
            a_ref[...] = a
            b_ref[...] = jnp.sqrt(q) * (ig * xc)
            r_ref[...] = r.astype(BF16)
            i_ref[...] = ig.astype(BF16)

    tile = pl.BlockSpec((tm, D), lambda i: (i, 0))
    return pl.pallas_call(
        body, name=f"lru_gates_fwd_l{layer}", grid=(nt,),
        in_specs=[*_halo_specs(tm, s, 2, 16), _full((4, D)), _full((1, D)),
                  _full((2, HEADS, HD, HD)), _full((2, HEADS, HD, HD)), _full((2, D)), _full((2, D)), _full((2, D))],
        out_specs=[tile] * 9, out_shape=[SDS((s, D), F32)] * 4 + [SDS((s, D), BF16)] * 5,
        compiler_params=_cp("parallel"))(z, z, z, cw, cb, wr, wi, br, bi, lam)


def _scan_group(a, x, c, reverse, bwd):
    row = lax.broadcasted_iota(jnp.int32, a.shape, 0)
    b = a * x if bwd else x
    for d in (1, 2, 4):
        keep = (row < 8 - d) if reverse else (row >= d)
        sh = 8 - d if reverse else d
        a_s = jnp.where(keep, pltpu.roll(a, sh, 0), 1.0)
        b_s = jnp.where(keep, pltpu.roll(b, sh, 0), 0.0)
        b = a * b_s + b
        a = a * a_s
    h = b + a * c
    new_c = h[0:1, :] if reverse else h[7:8, :]
    if not bwd:
        return h, new_c
    if reverse:
        prev = jnp.where(row < 7, pltpu.roll(h, 7, 0), c)
    else:
        prev = jnp.where(row >= 1, pltpu.roll(h, 1, 0), c)
    return x + prev, new_c


def _lru_scan(a_f, x_f, a_r, x_r, bwd, layer):
    s = a_f.shape[0]
    ts = min(1024, s // 2)
    cb = 512
    nt = s // ts
    ng = ts // 8

    def body(af_ref, xf_ref, ar_ref, xr_ref, of_ref, or_ref, cf, cr):
        @pl.when(pl.program_id(1) == 0)
        def _():
            cf[...] = jnp.zeros_like(cf)
            cr[...] = jnp.zeros_like(cr)

        def step(j, carry):
            c_f, c_r = carry
            rf = pl.multiple_of(j * 8, 8)
            rr = pl.multiple_of((ng - 1 - j) * 8, 8)
            o, c_f = _scan_group(af_ref[pl.ds(rf, 8), :], xf_ref[pl.ds(rf, 8), :], c_f, False, bwd)
            of_ref[pl.ds(rf, 8), :] = o
            o, c_r = _scan_group(ar_ref[pl.ds(rr, 8), :], xr_ref[pl.ds(rr, 8), :], c_r, True, bwd)
            or_ref[pl.ds(rr, 8), :] = o
            return c_f, c_r

        c_f, c_r = lax.fori_loop(0, ng, step, (cf[0:1, :], cr[0:1, :]), unroll=2)
        cf[...] = jnp.broadcast_to(c_f, cf.shape)
        cr[...] = jnp.broadcast_to(c_r, cr.shape)

    fwd = pl.BlockSpec((ts, cb), lambda c, t: (t, c))
    rev = pl.BlockSpec((ts, cb), lambda c, t: (nt - 1 - t, c))
    return pl.pallas_call(
        body, name=f"lru_scan_{'bwd' if bwd else 'fwd'}_l{layer}", grid=(D // cb, nt),
        in_specs=[fwd, fwd, rev, rev], out_specs=[fwd, rev],
        out_shape=[SDS((s, D), F32)] * 2,
        scratch_shapes=[pltpu.VMEM((8, cb), F32), pltpu.VMEM((8, cb), F32)],
        compiler_params=_cp("parallel", "arbitrary"))(a_f, x_f, a_r, x_r)


def _ffn_fwd(x1, g, wfi, wfo, layer, tm, head=None):
    s = x1.shape[0]

    def ffn(x_ref, g_ref, wi_ref, wo_ref, ff_ref, dff_ref, h_ref):
        x = x_ref[...]
        h = _rms_fwd(x, g_ref[...]).astype(BF16)
        h_ref[...] = h
        acc = x
        for k in range(4):
            gate = _dot_nt(h, wi_ref[0, k])
            up = _dot_nt(h, wi_ref[1, k])
            sg = _sigmoid(gate)
            silu = gate * sg
            ff = (silu * up).astype(BF16)
            ff_ref[k] = ff
            dff_ref[0, k] = (up * (sg * (1.0 + gate * (1.0 - sg)))).astype(BF16)
            dff_ref[1, k] = silu.astype(BF16)
            acc = acc + _dot(ff, wo_ref[k])
        return acc

    def body(x_ref, g_ref, wi_ref, wo_ref, x2_ref, ff_ref, dff_ref, h_ref):
        x2_ref[...] = ffn(x_ref, g_ref, wi_ref, wo_ref, ff_ref, dff_ref, h_ref)

    def body_with_head(x_ref, g_ref, wi_ref, wo_ref, fg_ref, t_ref, dx_ref, loss_ref, dfg_ref, ff_ref, dff_ref, h_ref):
        @pl.when(pl.program_id(0) == 0)
        def _():
            loss_ref[...] = jnp.zeros_like(loss_ref)
            dfg_ref[...] = jnp.zeros_like(dfg_ref)

        x2 = ffn(x_ref, g_ref, wi_ref, wo_ref, ff_ref, dff_ref, h_ref)
        fg = fg_ref[...]
        e = _rms_fwd(x2, fg) - t_ref[...]
        rows = jnp.sum(e * e, axis=-1, keepdims=True)
        loss_ref[...] += (0.5 / D) * jnp.sum(rows, axis=0, keepdims=True)
        dx, dg = _rms_bwd(x2, fg, e * (1.0 / D))
        dx_ref[...] = dx
        _add_rows128(dfg_ref, dg)

    tile = pl.BlockSpec((tm, D), lambda i: (i, 0))
    weights = [_resident((2, 4, FF_BLK, D)), _resident((4, FF_BLK, D))]
    kept_specs = [pl.BlockSpec((4, tm, FF_BLK), lambda i: (0, i, 0)),
                  pl.BlockSpec((2, 4, tm, FF_BLK), lambda i: (0, 0, i, 0)), tile]
    kept_shapes = [SDS((4, s, FF_BLK), BF16), SDS((2, 4, s, FF_BLK), BF16), SDS((s, D), BF16)]
    if head is None:
        return pl.pallas_call(
            body, name=f"ffn_fwd_l{layer}", grid=(s // tm,),
            in_specs=[tile, _full((1, D))] + weights, out_specs=[tile] + kept_specs,
            out_shape=[SDS((s, D), F32)] + kept_shapes, compiler_params=_cp("parallel"))(x1, g, wfi, wfo)
    final_g, tgt = head
    return pl.pallas_call(
        body_with_head, name=f"ffn_fwd_loss_l{layer}", grid=(s // tm,),
        in_specs=[tile, _full((1, D))] + weights + [_full((1, D)), tile],
        out_specs=[tile, _full((1, 1)), _full((LANE_ROWS, HD))] + kept_specs,
        out_shape=[SDS((s, D), F32), SDS((1, 1), F32), SDS((LANE_ROWS, HD), F32)] + kept_shapes,
        compiler_params=_cp("arbitrary"))(x1, g, wfi, wfo, final_g, tgt)


def _ffn_bwd(dx2, wfo, factors, wfi, x1, g, layer, tm):
    s = dx2.shape[0]

    def body(dx_ref, wo_ref, f_ref, wi_ref, x_ref, g_ref, dgu_ref, dx1_ref, dg_ref):
        @pl.when(pl.program_id(0) == 0)
        def _():
            dg_ref[...] = jnp.zeros_like(dg_ref)

        dx = dx_ref[...]
        dxb = dx.astype(BF16)
        dh = None
        for k in range(4):
            dff = _dot_nt(dxb, wo_ref[k])
            d_gate = (dff * f_ref[0, k].astype(F32)).astype(BF16)
            d_up = (dff * f_ref[1, k].astype(F32)).astype(BF16)
            dgu_ref[0, k] = d_gate
            dgu_ref[1, k] = d_up
            part = _dot(d_gate, wi_ref[k]) + _dot(d_up, wi_ref[4 + k])
            dh = part if dh is None else dh + part
        dxn, dg = _rms_bwd(x_ref[...], g_ref[...], dh)
        dx1_ref[...] = dx + dxn
        _add_rows128(dg_ref, dg)

    tile = pl.BlockSpec((tm, D), lambda i: (i, 0))
    blk = pl.BlockSpec((2, 4, tm, FF_BLK), lambda i: (0, 0, i, 0))
    return pl.pallas_call(
        body, name=f"ffn_bwd_l{layer}", grid=(s // tm,),
        in_specs=[tile, _resident((4, FF_BLK, D)), blk, _resident((N_DEV, FF_BLK, D)), tile, _full((1, D))],
        out_specs=[blk, tile, _full((LANE_ROWS, HD))],
        out_shape=[SDS((2, 4, s, FF_BLK), BF16), SDS((s, D), F32), SDS((LANE_ROWS, HD), F32)],
        compiler_params=_cp("arbitrary"))(dx2, wfo, factors, wfi, x1, g)


def _mm_nt_rms_bwd(a, a_spec, a_blocks, w, w_is_transposed, x, g, dres, name, tm, jobs=()):
    s = x.shape[0]

    def body(a_ref, w_ref, x_ref, g_ref, dres_ref, dx_ref, dg_ref):
        @pl.when(pl.program_id(0) == 0)
        def _():
            dg_ref[...] = jnp.zeros_like(dg_ref)

        dh = None
        for k, blk in enumerate(a_blocks(a_ref)):
            part = _dot(blk, w_ref[k]) if w_is_transposed else _dot_nt(blk, w_ref[k])
            dh = part if dh is None else dh + part
        dx, dg = _rms_bwd(x_ref[...], g_ref[...], dh)
        dx_ref[...] = dres_ref[...] + dx
        _add_rows128(dg_ref, dg)

    tile = pl.BlockSpec((tm, D), lambda i: (i, 0))
    body, job_in, job_args, job_out, job_shapes, aliases = _carry(jobs, s // tm, body, 5, 2)
    res = pl.pallas_call(
        body, name=name, grid=(s // tm,),
        in_specs=[a_spec, _resident(w.shape), tile, _full((1, D)), tile] + job_in,
        out_specs=[tile, _full((LANE_ROWS, HD))] + job_out,
        out_shape=[SDS((s, D), F32), SDS((LANE_ROWS, HD), F32)] + job_shapes,
        input_output_aliases=aliases, compiler_params=_cp("arbitrary"))(a, w, x, g, dres, *job_args)
    for j, job in enumerate(jobs):
        job.results = res[2 + 4 * j:6 + 4 * j]
    return res[0], res[1]


def _mm_tn(a, a_spec, b, b_spec, nb, out_shape, out_spec, name, a_is_transposed=True, after=None, jobs=()):
    def body(a_ref, b_ref, *rest):
        o_ref = rest[-1]
        bb = b_ref[...].astype(BF16)
        o_ref[...] = (_dot(a_ref[...], bb) if a_is_transposed else _dot_tn(a_ref[...], bb)).astype(BF16)

    deps = [] if after is None else [after]
    body, job_in, job_args, job_out, job_shapes, aliases = _carry(jobs, nb, body, 2 + len(deps), 1)
    res = pl.pallas_call(
        body, name=name, grid=(nb,), in_specs=[a_spec, b_spec] + [_ANY] * len(deps) + job_in,
        out_specs=[out_spec] + job_out, out_shape=[SDS(out_shape, BF16)] + job_shapes,
        input_output_aliases=aliases, compiler_params=_cp("parallel"))(a, b, *deps, *job_args)
    for j, job in enumerate(jobs):
        job.results = res[1 + 4 * j:5 + 4 * j]
    return res[0]


def _mixer_bwd(dx1, wo, h0, h1, z, lng, lnb, ws, wst, bsb, layer, tm):
    s = dx1.shape[0]
    nt = s // tm

    def body(dx_ref, wo_ref, h0_ref, h1_ref, zu_ref, zv_ref, zg_ref, za_ref, zb_ref, lng_ref, lnb_ref,
             ws_ref, wst_ref, bsb_ref, dz_ref, dh_ref, dws_ref, dbs_ref, dlng_ref, dlnb_ref,
             du_s, dv_s, ya_s, dbs_acc):
        i = pl.program_id(0)

        @pl.when(i == 0)
        def _():
            for r in (dws_ref, dlng_ref, dlnb_ref, dbs_acc):
                r[...] = jnp.zeros_like(r)

        dm = _dot_nt(dx_ref[...].astype(BF16), wo_ref[...])
        sa = _sigmoid(za_ref[...].astype(F32))
        sb = _sigmoid(zb_ref[...].astype(F32))
        zg = zg_ref[...].astype(F32)
        gg, tg = _gelu(zg)
        hs = h0_ref[...] + h1_ref[...]
        dyb = dm * sb
        dya = dm * sa
        dh_ref[...] = dyb * gg
        dz_ref[:, 2 * D:3 * D] = jnp.zeros((tm, D), BF16)
        dz_ref[:, 3 * D:4 * D] = (dyb * hs * _gelu_grad(zg, tg)).astype(BF16)
        dz_ref[:, 5 * D:6 * D] = (dm * (hs * gg) * (sb * (1.0 - sb))).astype(BF16)

        zu, zv, u, tu, tv, xh, rstd, vb = _gmlp_values(zu_ref, zv_ref, lng_ref, lnb_ref)
        for c in range(tm // HD):
            rs = slice(c * HD, (c + 1) * HD)
            for g in range(HEADS):
                cs = slice(g * HD, (g + 1) * HD)
                vblk = vb[rs, cs]
                mixed = _dot(ws_ref[g], vblk) + bsb_ref[g]
                ya_s[rs, cs] = u[rs, cs] * mixed
                du_s[rs, cs] = dya[rs, cs] * mixed
                dmx = dya[rs, cs] * u[rs, cs]
                dbs_acc[g] += dmx
                dmxb = dmx.astype(BF16)
                dws_ref[g] += _dot_nt(dmxb, vblk)
                dv_s[rs, cs] = _dot(wst_ref[g], dmxb)
        dz_ref[:, 4 * D:5 * D] = (dm * ya_s[...] * (sa * (1.0 - sa))).astype(BF16)
        dv = dv_s[...]
        _add_rows128(dlng_ref, jnp.sum(dv * xh, axis=0, keepdims=True))
        _add_rows128(dlnb_ref, jnp.sum(dv, axis=0, keepdims=True))
        dxh = dv * lng_ref[...]
        dgv = rstd * (dxh - jnp.mean(dxh, axis=-1, keepdims=True)
                      - xh * jnp.mean(dxh * xh, axis=-1, keepdims=True))
        dz_ref[:, 0:D] = (du_s[...] * _gelu_grad(zu, tu)).astype(BF16)
        dz_ref[:, D:2 * D] = (dgv * _gelu_grad(zv, tv)).astype(BF16)

        @pl.when(i == nt - 1)
        def _():
            for g in range(HEADS):
                dbs_ref[g:g + 1, :] = jnp.sum(dbs_acc[g].T, axis=0, keepdims=True)

    tile = pl.BlockSpec((tm, D), lambda i: (i, 0))
    wspec = _full((HEADS, HD, HD))
    return pl.pallas_call(
        body, name=f"mixer_bwd_l{layer}", grid=(nt,),
        in_specs=[tile, _full((D, D)), tile, tile]
        + [pl.BlockSpec((tm, D), lambda i, c=c: (i, c)) for c in (0, 1, 3, 4, 5)]
        + [_full((1, D)), _full((1, D)), wspec, wspec, wspec],
        out_specs=[pl.BlockSpec((tm, N_IN), lambda i: (i, 0)), tile, wspec, _full((HEADS, HD)),
                   _full((LANE_ROWS, HD)), _full((LANE_ROWS, HD))],
        out_shape=[SDS((s, N_IN), BF16), SDS((s, D), F32), SDS((HEADS, HD, HD), F32), SDS((HEADS, HD), F32),
                   SDS((LANE_ROWS, HD), F32), SDS((LANE_ROWS, HD), F32)],
        scratch_shapes=[pltpu.VMEM((tm, D), F32)] * 3 + [pltpu.VMEM((HEADS, HD, HD), F32)],
        compiler_params=_cp("arbitrary"))(dx1, wo, h0, h1, z, z, z, z, z, lng, lnb, ws, wst, bsb)


def _lru_gates_bwd(xcb, gates, h0, h1, g0, g1, wr, wi, lam, layer, tm):
    s = xcb.shape[0]
    nt = s // tm

    def body(xc_ref, r0_ref, i0_ref, r1_ref, i1_ref, h0p_ref, h0_ref, h1_ref, h1n_ref, g0_ref, g1_ref,
             wr_ref, wi_ref, lam_ref, dxc_ref, dwr_ref, dwi_ref, dbr_ref, dbi_ref, dlam_ref):
        i = pl.program_id(0)
        fp, fn = _halo_flags(nt)

        @pl.when(i == 0)
        def _():
            for r in (dwr_ref, dwi_ref, dbr_ref, dbi_ref, dlam_ref):
                r[...] = jnp.zeros_like(r)

        xb = xc_ref[...]
        xc = xb.astype(F32)
        zeros8 = jnp.zeros((8, D), F32)
        h_prev = _taps(h0p_ref[...] * fp, h0_ref[...], zeros8, tm)[1]
        h_next = _taps(zeros8, h1_ref[...], h1n_ref[...] * fn, tm)[3]
        dxc = jnp.zeros((tm, D), F32)
        for d, (g_ref, hsh, r_ref, i_ref) in enumerate(((g0_ref, h_prev, r0_ref, i0_ref),
                                                        (g1_ref, h_next, r1_ref, i1_ref))):
            sp = _softplus(-lam_ref[d:d + 1, :])
            r = r_ref[...].astype(F32)
            ig = i_ref[...].astype(F32)
            a, q = _lru_decay(r, sp)
            rmult = jnp.where(q > 0.0, lax.rsqrt(jnp.where(q > 0.0, q, 1.0)), 0.0)
            mult = q * rmult
            db = g_ref[...]
            da = db * hsh
            dmult = db * (ig * xc)
            di = db * (mult * xc)
            dxc = dxc + db * (mult * ig)
            dla = da * a - dmult * (a * a * rmult)
            dsp_dlam = -_sigmoid(-lam_ref[d:d + 1, :])
            _add_rows128(dlam_ref, jnp.sum(dla * r, axis=0, keepdims=True) * ((-LRU_C) * dsp_dlam), d * LANE_ROWS)
            dpr = dla * sp * (-LRU_C) * (r * (1.0 - r))
            dpi = di * (ig * (1.0 - ig))
            _add_rows128(dbr_ref, jnp.sum(dpr, axis=0, keepdims=True), d * LANE_ROWS)
            _add_rows128(dbi_ref, jnp.sum(dpi, axis=0, keepdims=True), d * LANE_ROWS)
            dprb = dpr.astype(BF16)
            dpib = dpi.astype(BF16)
            parts = []
            for h in range(HEADS):
                cs = slice(h * HD, (h + 1) * HD)
                dwr_ref[d, h] += _dot_tn(xb[:, cs], dprb[:, cs])
                dwi_ref[d, h] += _dot_tn(xb[:, cs], dpib[:, cs])
                parts.append(_dot_nt(dprb[:, cs], wr_ref[d, h]) + _dot_nt(dpib[:, cs], wi_ref[d, h]))
            dxc = dxc + jnp.concatenate(parts, axis=1)
        dxc_ref[...] = dxc.astype(BF16)

    tile = pl.BlockSpec((tm, D), lambda i: (i, 0))
    hp, hc, hn = _halo_specs(tm, s, 0)
    wspec = _full((2, HEADS, HD, HD))
    vspec = _full((2 * LANE_ROWS, HD))
    return pl.pallas_call(
        body, name=f"lru_gates_bwd_l{layer}", grid=(nt,),
        in_specs=[tile] * 5 + [hp, hc, hc, hn, tile, tile, wspec, wspec, _full((2, D))],
        out_specs=[tile, wspec, wspec, vspec, vspec, vspec],
        out_shape=[SDS((s, D), BF16), SDS((2, HEADS, HD, HD), F32), SDS((2, HEADS, HD, HD), F32)]
        + [SDS((2 * LANE_ROWS, HD), F32)] * 3,
        compiler_params=_cp("arbitrary"))(xcb, *gates, h0, h0, h1, h1, g0, g1, wr, wi, lam)


def _conv_bwd(dz, dxc, z, cw, layer, tm):
    s = z.shape[0]
    nt = s // tm

    def body(dz_in, dp_ref, dc_ref, dn_ref, zp_ref, zc_ref, zn_ref, cw_ref, dz_ref, dcw_ref, dcb_ref):
        del dz_in
        fp, fn = _halo_flags(nt)

        @pl.when(pl.program_id(0) == 0)
        def _():
            dcw_ref[...] = jnp.zeros_like(dcw_ref)
            dcb_ref[...] = jnp.zeros_like(dcb_ref)

        dxc_halo = _halo_load(dp_ref, dc_ref, dn_ref, fp, fn)
        dxc = dxc_halo[1]
        dm2, dm1, _, dp1, _ = _taps(*dxc_halo, tm)
        dz_ref[...] = (cw_ref[0:1, :] * dp1 + cw_ref[1:2, :] * dxc + cw_ref[2:3, :] * dm1
                       + cw_ref[3:4, :] * dm2).astype(BF16)
        _, zm1, z0, zp1, zp2 = _taps(*_halo_load(zp_ref, zc_ref, zn_ref, fp, fn), tm)
        for k, zt in enumerate((zm1, z0, zp1, zp2)):
            _add_rows128(dcw_ref, jnp.sum(dxc * zt, axis=0, keepdims=True), k * LANE_ROWS)
        _add_rows128(dcb_ref, jnp.sum(dxc, axis=0, keepdims=True))

    return pl.pallas_call(
        body, name=f"conv_bwd_l{layer}", grid=(nt,),
        in_specs=[pl.BlockSpec(memory_space=pl.ANY), *_halo_specs(tm, s, 0, 16), *_halo_specs(tm, s, 2, 16),
                  _full((4, D))],
        out_specs=[pl.BlockSpec((tm, D), lambda i: (i, 2)), _full((4 * LANE_ROWS, HD)), _full((LANE_ROWS, HD))],
        out_shape=[SDS((s, N_IN), BF16), SDS((4 * LANE_ROWS, HD), F32), SDS((LANE_ROWS, HD), F32)],
        input_output_aliases={0: 0},
        compiler_params=_cp("arbitrary"))(dz, dxc, dxc, dxc, z, z, z, cw)


def _me():
    return lax.axis_index("x"), lax.axis_index("y"), lax.axis_index("c")


def _peer(m):
    x, y, c = _me()
    px = 1 - x if m & 4 else x
    py = 1 - y if m & 2 else y
    pc = 1 - c if m & 1 else c
    return (px, py, pc), 4 * px + 2 * py + pc


_ANY = pl.BlockSpec(memory_space=pl.ANY)
_EXCHANGE_SEMS = [pltpu.SemaphoreType.DMA((N_DEV - 1,)), pltpu.SemaphoreType.DMA((N_DEV - 1,)), pltpu.SemaphoreType.DMA(())]


def _all_gather(v, after, name):
    def body(v_ref, after_ref, o_ref, send_sems, recv_sems, local_sem):
        del after_ref
        x, y, c = _me()
        me = 4 * x + 2 * y + c
        local = pltpu.make_async_copy(v_ref, o_ref.at[me], local_sem)
        local.start()
        sends = []
        for m in range(1, N_DEV):
            dev, _ = _peer(m)
            cp = pltpu.make_async_remote_copy(v_ref, o_ref.at[me], send_sems.at[m - 1], recv_sems.at[m - 1],
                                              device_id=dev, device_id_type=pl.DeviceIdType.MESH)
            cp.start()
            sends.append(cp)
        for m in range(1, N_DEV):
            dev, blk = _peer(m)
            pltpu.make_async_remote_copy(v_ref, o_ref.at[blk], send_sems.at[m - 1], recv_sems.at[m - 1],
                                         device_id=dev, device_id_type=pl.DeviceIdType.MESH).wait_recv()
        for cp in sends:
            cp.wait_send()
        local.wait()

    return pl.pallas_call(
        body, name=name, in_specs=[_ANY, _ANY], out_specs=_ANY,
        out_shape=SDS((N_DEV,) + v.shape, v.dtype), scratch_shapes=_EXCHANGE_SEMS)(v, after)


_HBM = pl.BlockSpec(memory_space=pltpu.HBM)
_SEM = pl.BlockSpec(memory_space=pltpu.SEMAPHORE)
_EFFECT = pltpu.CompilerParams(has_side_effects=pltpu.SideEffectType.DATAFLOW_SIDE_EFFECTING)
_PEER_SEMS = pltpu.SemaphoreType.DMA((N_DEV - 1,))


def _in_hbm(a):
    return pltpu.with_memory_space_constraint(a, pltpu.HBM)


def _remote(src, dst, send_sems, recv_sems, m):
    dev, _ = _peer(m)
    return pltpu.make_async_remote_copy(src, dst, send_sems.at[m - 1], recv_sems.at[m - 1],
                                        device_id=dev, device_id_type=pl.DeviceIdType.MESH)


def _gather_start(lands, after, name):
    n = len(lands)

    def body(*refs):
        land = refs[:n]
        sems = refs[n + 1:3 * n + 1]
        token = refs[-1]
        x, y, c = _me()
        me = 4 * x + 2 * y + c
        for t in range(n):
            for m in range(1, N_DEV):
                _remote(land[t].at[me], land[t].at[me], sems[2 * t], sems[2 * t + 1], m).start()
        token[...] = jnp.zeros_like(token)

    res = pl.pallas_call(
        body, name=name, in_specs=[_HBM] * n + [_ANY],
        out_specs=[_SEM] * (2 * n) + [_HBM] * n + [pl.BlockSpec(memory_space=pltpu.VMEM)],
        out_shape=[_PEER_SEMS] * (2 * n) + [pltpu.HBM(a.shape, a.dtype) for a in lands] + [SDS((8, 128), F32)],
        input_output_aliases={t: 2 * n + t for t in range(n)},
        compiler_params=_EFFECT)(*[_in_hbm(a) for a in lands], after)
    return [(res[2 * t], res[2 * t + 1], res[2 * n + t]) for t in range(n)], res[-1]


def _gather_wait(handle, after, name):
    send_sems, recv_sems, land = handle

    def body(land_ref, ssem, rsem, after_ref, out_ref):
        del after_ref, out_ref
        x, y, c = _me()
        me = 4 * x + 2 * y + c
        for m in range(1, N_DEV):
            _, blk = _peer(m)
            cp = _remote(land_ref.at[me], land_ref.at[blk], ssem, rsem, m)
            cp.wait_send()
            cp.wait_recv()

    return pl.pallas_call(
        body, name=name, in_specs=[_HBM, _SEM, _SEM, _ANY], out_specs=_HBM,
        out_shape=pltpu.HBM(land.shape, land.dtype), input_output_aliases={0: 0},
        compiler_params=_EFFECT)(land, send_sems, recv_sems, after)


FIRST_STAGE = (1, 2, 4, 6)
RELAYED = (2, 4, 6)
OTHER_CORE = 1


def _stage_copy(src, dst, send_sems, recv_sems, k, m):
    dev, _ = _peer(m)
    return pltpu.make_async_remote_copy(src, dst, send_sems.at[k], recv_sems.at[k],
                                        device_id=dev, device_id_type=pl.DeviceIdType.MESH)


def _gather2_start(lands, after, name):
    n = len(lands)

    def body(*refs):
        land = refs[:n]
        sems = refs[n + 1:3 * n + 1]
        token = refs[-1]
        x, y, c = _me()
        me = 4 * x + 2 * y + c
        for t in range(n):
            for k, m in enumerate(FIRST_STAGE):
                _stage_copy(land[t].at[me], land[t].at[me], sems[2 * t], sems[2 * t + 1], k, m).start()
        token[...] = jnp.zeros_like(token)

    stage_sems = pltpu.SemaphoreType.DMA((len(FIRST_STAGE),))
    res = pl.pallas_call(
        body, name=name, in_specs=[_HBM] * n + [_ANY],
        out_specs=[_SEM] * (2 * n) + [_HBM] * n + [pl.BlockSpec(memory_space=pltpu.VMEM)],
        out_shape=[stage_sems] * (2 * n) + [pltpu.HBM(a.shape, a.dtype) for a in lands] + [SDS((8, 128), F32)],
        input_output_aliases={t: 2 * n + t for t in range(n)},
        compiler_params=_EFFECT)(*[_in_hbm(a) for a in lands], after)
    return [(res[2 * t], res[2 * t + 1], res[2 * n + t]) for t in range(n)], res[-1]


def _gather2_relay(handles, after, name):
    n = len(handles)

    def body(*refs):
        land, send1, recv1 = refs[:n], refs[n:2 * n], refs[2 * n:3 * n]
        sems = refs[3 * n + 1:5 * n + 1]
        token = refs[-1]
        x, y, c = _me()
        me = 4 * x + 2 * y + c
        for t in range(n):
            for j, m in enumerate(RELAYED):
                _, blk = _peer(m)
                _stage_copy(land[t].at[me], land[t].at[blk], send1[t], recv1[t], 1 + j, m).wait_recv()
                _stage_copy(land[t].at[blk], land[t].at[blk], sems[2 * t], sems[2 * t + 1], j, OTHER_CORE).start()
        token[...] = jnp.zeros_like(token)

    relay_sems = pltpu.SemaphoreType.DMA((len(RELAYED),))
    lands = [h[2] for h in handles]
    res = pl.pallas_call(
        body, name=name, in_specs=[_HBM] * n + [_SEM] * (2 * n) + [_ANY],
        out_specs=[_SEM] * (2 * n) + [_HBM] * n + [pl.BlockSpec(memory_space=pltpu.VMEM)],
        out_shape=[relay_sems] * (2 * n) + [pltpu.HBM(a.shape, a.dtype) for a in lands] + [SDS((8, 128), F32)],
        input_output_aliases={t: 2 * n + t for t in range(n)},
        compiler_params=_EFFECT)(*lands, *[h[0] for h in handles], *[h[1] for h in handles], after)
    return [(h[0], h[1], res[2 * t], res[2 * t + 1], res[2 * n + t]) for t, h in enumerate(handles)], res[-1]


def _gather2_wait(handle, after, name):
    send1, recv1, send2, recv2, land = handle

    def body(land_ref, s1, r1, s2, r2, after_ref, out_ref):
        del after_ref, out_ref
        x, y, c = _me()
        me = 4 * x + 2 * y + c
        _, other = _peer(OTHER_CORE)
        _stage_copy(land_ref.at[me], land_ref.at[other], s1, r1, 0, OTHER_CORE).wait_recv()
        for k, m in enumerate(FIRST_STAGE):
            _stage_copy(land_ref.at[me], land_ref.at[me], s1, r1, k, m).wait_send()
        for j, m in enumerate(RELAYED):
            _, mine = _peer(m)
            _, theirs = _peer(m ^ OTHER_CORE)
            _stage_copy(land_ref.at[mine], land_ref.at[mine], s2, r2, j, OTHER_CORE).wait_send()
            _stage_copy(land_ref.at[mine], land_ref.at[theirs], s2, r2, j, OTHER_CORE).wait_recv()

    return pl.pallas_call(
        body, name=name, in_specs=[_HBM] + [_SEM] * 4 + [_ANY], out_specs=_HBM,
        out_shape=pltpu.HBM(land.shape, land.dtype), input_output_aliases={0: 0},
        compiler_params=_EFFECT)(land, send1, recv1, send2, recv2, after)


def _exchange_start(ps, name):
    n = len(ps)

    def body(*refs):
        p = refs[:n]
        got = refs[n:2 * n]
        sems = refs[2 * n:5 * n]
        token = refs[-1]
        x, y, c = _me()
        me = 4 * x + 2 * y + c
        for t in range(n):
            pltpu.make_async_copy(p[t].at[me], got[t].at[me], sems[3 * t + 2]).start()
            for m in range(1, N_DEV):
                _, blk = _peer(m)
                _remote(p[t].at[blk], got[t].at[me], sems[3 * t], sems[3 * t + 1], m).start()
        token[...] = jnp.zeros_like(token)

    res = pl.pallas_call(
        body, name=name, in_specs=[_HBM] * (2 * n),
        out_specs=[_SEM] * (3 * n) + [_HBM] * (2 * n) + [pl.BlockSpec(memory_space=pltpu.VMEM)],
        out_shape=[_PEER_SEMS, _PEER_SEMS, pltpu.SemaphoreType.DMA(())] * n
        + [pltpu.HBM(a.shape, a.dtype) for a in ps] * 2 + [SDS((8, 128), F32)],
        input_output_aliases={t: 3 * n + t for t in range(2 * n)},
        compiler_params=_EFFECT)(*[_in_hbm(a) for a in ps], *[_in_hbm(lax.empty(a.shape, a.dtype)) for a in ps])
    return [(res[3 * t], res[3 * t + 1], res[3 * t + 2], res[3 * n + t], res[4 * n + t]) for t in range(n)], res[-1]


def _exchange_wait(handle, after, name):
    send_sems, recv_sems, local_sem, p, got = handle

    def body(p_ref, got_ref, ssem, rsem, lsem, after_ref, p_out, got_out):
        del after_ref, p_out, got_out
        x, y, c = _me()
        me = 4 * x + 2 * y + c
        pltpu.make_async_copy(p_ref.at[me], got_ref.at[me], lsem).wait()
        for m in range(1, N_DEV):
            _, blk = _peer(m)
            cp = _remote(p_ref.at[blk], got_ref.at[blk], ssem, rsem, m)
            cp.wait_send()
            cp.wait_recv()

    return pl.pallas_call(
        body, name=name, in_specs=[_HBM, _HBM, _SEM, _SEM, _SEM, _ANY], out_specs=[_HBM, _HBM],
        out_shape=[pltpu.HBM(p.shape, p.dtype), pltpu.HBM(got.shape, got.dtype)],
        input_output_aliases={0: 0, 1: 1}, compiler_params=_EFFECT)(p, got, send_sems, recv_sems, local_sem, after)[1]


def _cast_into_slot(w, layer, me1, name):
    _, r, c = w.shape
    tr = next(t for t in (512, 352, r) if r % t == 0)

    def body(me_ref, w_ref, o_ref):
        del me_ref
        o_ref[...] = w_ref[...].astype(BF16)

    return pl.pallas_call(
        body, name=name,
        grid_spec=pltpu.PrefetchScalarGridSpec(
            num_scalar_prefetch=1, grid=(r // tr,),
            in_specs=[pl.BlockSpec((None, tr, c), lambda i, me: (layer, i, 0))],
            out_specs=pl.BlockSpec((None, tr, c), lambda i, me: (me[0], i, 0))),
        out_shape=SDS((N_DEV, r, c), BF16), compiler_params=_cp("arbitrary"))(me1, w)


def _cast_all_into_slots(ws, layers, me1, after, name):
    n = len(ws)

    def body(me_ref, *refs):
        del me_ref
        for w_ref, o_ref in zip(refs[:n], refs[n + 1:]):
            o_ref[...] = w_ref[...].astype(BF16)

    return pl.pallas_call(
        body, name=name,
        grid_spec=pltpu.PrefetchScalarGridSpec(
            num_scalar_prefetch=1, grid=(1,),
            in_specs=[pl.BlockSpec((None,) + a.shape[1:], lambda i, me, l=l: (l, 0, 0)) for a, l in zip(ws, layers)]
            + [_ANY],
            out_specs=[pl.BlockSpec((None,) + a.shape[1:], lambda i, me: (me[0], 0, 0)) for a in ws]),
        out_shape=[SDS((N_DEV,) + a.shape[1:], BF16) for a in ws],
        compiler_params=_cp("arbitrary"))(me1, *ws, after)


def _sum8_into_slot(p, me1, name):
    _, r, c = p.shape

    def body(me_ref, p_ref, o_ref):
        del me_ref
        acc = p_ref[0]
        for k in range(1, N_DEV):
            acc = acc + p_ref[k]
        o_ref[...] = acc

    return pl.pallas_call(
        body, name=name,
        grid_spec=pltpu.PrefetchScalarGridSpec(
            num_scalar_prefetch=1, grid=(1,),
            in_specs=[pl.BlockSpec(p.shape, lambda i, me: (0, 0, 0))],
            out_specs=pl.BlockSpec((None, r, c), lambda i, me: (me[0], 0, 0))),
        out_shape=SDS(p.shape, F32), compiler_params=_cp("arbitrary"))(me1, p)


def _adamw(w, g, m, v):
    m = ADAM_B1 * m + (1.0 - ADAM_B1) * g
    v = ADAM_B2 * v + (1.0 - ADAM_B2) * (g * g)
    m_hat = m / (1.0 - ADAM_B1 ** ADAM_STEP)
    v_hat = v / (1.0 - ADAM_B2 ** ADAM_STEP)
    delta = -ADAM_LR * (m_hat / (jnp.sqrt(v_hat) + ADAM_EPS) + ADAM_WD * w)
    return delta, m, v


def _adam_tile(p_ref, w_ref, m_ref, v_ref, g_ref, d_ref, nm_ref, nv_ref):
    g = p_ref[0].astype(F32)
    for k in range(1, N_DEV):
        g = g + p_ref[k].astype(F32)
    delta, nm, nv = _adamw(w_ref[...], g, m_ref[...], v_ref[...])
    g_ref[...] = g
    d_ref[...] = delta
    nm_ref[...] = nm
    nv_ref[...] = nv


class _AdamJob:
    def __init__(self, parts, w, m, v, layer, prev):
        self.args = [parts, w, m, v] + list(prev or ())
        self.layer, self.results = layer, None


def _carry(jobs, steps, body, n_in, n_out):
    in_specs, args, out_specs, out_shapes, aliases, n_prevs = [], [], [], [], {}, []
    for j, job in enumerate(jobs):
        _, r, c = job.args[0].shape
        nr = next(n for n in range(steps, 0, -1) if steps % n == 0 and r % (16 * n) == 0)
        nc = steps // nr
        assert c % (128 * nc) == 0
        tile = (r // nr, c // nc)
        blk = pl.BlockSpec((None,) + tile, lambda i, layer=job.layer, nc=nc: (layer, i // nc, i % nc))
        n_prev = len(job.args) - 4
        aliases.update({n_in + len(args) + 4 + k: n_out + 4 * j + k for k in range(n_prev)})
        in_specs += [pl.BlockSpec((N_DEV,) + tile, lambda i, nc=nc: (0, i // nc, i % nc)), blk, blk, blk]
        in_specs += [_ANY] * n_prev
        args += job.args
        out_specs += [blk] * 4
        out_shapes += [SDS(job.args[1].shape, F32)] * 4
        n_prevs.append(n_prev)

    def carrying(*refs):
        ins, outs = refs[:n_in + len(args)], refs[n_in + len(args):]
        body(*ins[:n_in], *outs[:n_out])
        k = n_in
        for j, n_prev in enumerate(n_prevs):
            _adam_tile(*ins[k:k + 4], *outs[n_out + 4 * j:n_out + 4 * j + 4])
            k += 4 + n_prev

    return carrying, in_specs, args, out_specs, out_shapes, aliases


def _adam_shard(parts, w, m, v, layer, prev, name):
    _, r, c = parts.shape
    tr = next(t for t in (512, 352, r) if r % t == 0)
    n_prev = 0 if prev is None else 4

    def body(*refs):
        _adam_tile(*refs[:4], *refs[4 + n_prev:])

    blk = pl.BlockSpec((None, tr, c), lambda i: (layer, i, 0))
    return pl.pallas_call(
        body, name=name, grid=(r // tr,),
        in_specs=[pl.BlockSpec((N_DEV, tr, c), lambda i: (0, i, 0)), blk, blk, blk] + [_ANY] * n_prev,
        out_specs=[blk] * 4, out_shape=[SDS(w.shape, F32)] * 4,
        input_output_aliases={4 + k: k for k in range(n_prev)},
        compiler_params=_cp("parallel"))(parts, w, m, v, *(prev or ()))


SMALL_MATRICES = [("lru_w_r", 2048), ("lru_w_i", 2048), ("gmlp_w_s", 1024)]
SMALL_VECTORS = [("norm1_g", 8), ("gmlp_ln_g", 8), ("gmlp_ln_b", 8), ("gmlp_b_s", 8), ("conv_w", 32), ("conv_b", 8),
                 ("lru_b_r", 16), ("lru_b_i", 16), ("lru_lambda", 16), ("norm2_g", 8), ("final_g", 8)]
SMALL_VECTOR_ROW0 = sum(n for _, n in SMALL_MATRICES)
SMALL_VECTOR_BLOCK = 256
SMALL_ROWS = SMALL_VECTOR_ROW0 + SMALL_VECTOR_BLOCK


def _pack_small(small):
    parts = [small[k] for k, _ in SMALL_MATRICES]
    parts += [small[k] if k in small else jnp.zeros((n, HD), F32) for k, n in SMALL_VECTORS]
    flat = jnp.concatenate(parts)
    return jnp.pad(flat, ((0, SMALL_ROWS - flat.shape[0]), (0, 0))).reshape(N_DEV, SMALL_ROWS // N_DEV, HD)


def _adam_matrix(g0, g1, w, m, v, row0, name):
    _, rows, _ = w.shape
    tr = 512

    def body(g0_ref, g1_ref, w_ref, m_ref, v_ref, g_ref, d_ref, nm_ref, nv_ref):
        for l, src in enumerate((g0_ref, g1_ref)):
            g = src[...]
            delta, nm, nv = _adamw(w_ref[l], g, m_ref[l], v_ref[l])
            g_ref[l] = g
            d_ref[l] = delta
            nm_ref[l] = nm
            nv_ref[l] = nv

    gspec = pl.BlockSpec((tr, HD), lambda i: (row0 // tr + i, 0))
    blk = pl.BlockSpec((2, tr, HD), lambda i: (0, i, 0))
    return pl.pallas_call(body, name=name, grid=(rows // tr,), in_specs=[gspec, gspec] + [blk] * 3,
                          out_specs=[blk] * 4, out_shape=[SDS(w.shape, F32)] * 4,
                          compiler_params=_cp("parallel"))(g0, g1, w, m, v)


def _adam_vectors(g0, g1, dg1_parts, me1, ws, ms, vs):
    names = [k for k, _ in SMALL_VECTORS]
    n = len(names)

    def lanes(rows8):
        return jnp.concatenate([rows8[k:k + 1, :] for k in range(LANE_ROWS)], axis=1)

    def body(me_ref, g0_ref, g1_ref, dg1_ref, *refs):
        w_refs, m_refs, v_refs = refs[:n], refs[n:2 * n], refs[2 * n:3 * n]
        outs = refs[3 * n:]
        me = me_ref[0]
        g_refs = (g0_ref, g1_ref)

        def emit(i, idx, g):
            delta, nm, nv = _adamw(w_refs[i][idx], g, m_refs[i][idx], v_refs[i][idx])
            for j, val in enumerate((g, delta, nm, nv)):
                outs[4 * i + j][idx] = val

        off = 0
        for i, (name, rows) in enumerate(SMALL_VECTORS):
            for l in range(2):
                row = (slice(l, l + 1), slice(None))
                if name == "final_g":
                    if l == 1:
                        emit(i, (slice(0, 1), slice(None)), lanes(g1_ref[off:off + rows, :]))
                elif name == "norm1_g":
                    if l == 1:
                        emit(i, row, lanes(g0_ref[off:off + rows, :]))
                    else:
                        total = dg1_ref[0]
                        for k in range(1, N_DEV):
                            total = total + dg1_ref[k]
                        emit(i, row, lanes(total))
                elif name == "gmlp_b_s":
                    emit(i, (l,), g_refs[l][off:off + rows, :])
                elif rows == LANE_ROWS:
                    emit(i, row, lanes(g_refs[l][off:off + rows, :]))
                else:
                    for r in range(rows // LANE_ROWS):
                        emit(i, (l, slice(r, r + 1), slice(None)), g_refs[l][pl.ds(off + r * LANE_ROWS + me, 1), :])
            off += rows

    args = [ws[k] for k in names] + [ms[k] for k in names] + [vs[k] for k in names]
    gspec = pl.BlockSpec((SMALL_VECTOR_BLOCK, HD), lambda i, me: (SMALL_VECTOR_ROW0 // SMALL_VECTOR_BLOCK, 0))
    res = pl.pallas_call(
        body, name="adam_vectors",
        grid_spec=pltpu.PrefetchScalarGridSpec(
            num_scalar_prefetch=1, grid=(1,),
            in_specs=[gspec, gspec, _full(dg1_parts.shape)] + [_full(a.shape) for a in args],
            out_specs=[_full(ws[k].shape) for k in names for _ in range(4)]),
        out_shape=[SDS(ws[k].shape, F32) for k in names for _ in range(4)],
        compiler_params=_cp("arbitrary"))(me1, g0, g1, dg1_parts, *args)
    return {k: list(res[4 * i:4 * i + 4]) for i, k in enumerate(names)}


def _after(a, *tokens):
    for token in tokens:
        if token is not None:
            a = a + token[0:1, 0:1]
    return a


def _local_step(x, tgt, p, get_w, hook=lambda stage, layer, payload: None):
    s = x.shape[0]
    tm = _row_tile(s)
    wsb = p["gmlp_w_s"].astype(BF16)
    wstb = jnp.swapaxes(p["gmlp_w_s"], -1, -2).astype(BF16)
    bsb = jnp.broadcast_to(p["gmlp_b_s"][..., None], p["gmlp_w_s"].shape)
    wrb = p["lru_w_r"].astype(BF16)
    wib = p["lru_w_i"].astype(BF16)
    saved = []
    for l in range(2):
        win = get_w("w_in", l, x)
        z, h1 = _norm_inproj(x, _after(p["norm1_g"][l][None], hook("pre_inproj", l, win)), win, l, tm)
        a0, b0, a1, b1, xcb, *gates = _lru_gates_fwd(z, p["conv_w"][l], p["conv_b"][l][None], wrb[l], wib[l],
                                                     p["lru_b_r"][l], p["lru_b_i"][l], p["lru_lambda"][l], l, tm)
        h0, hr = _lru_scan(a0, b0, a1, b1, False, l)
        lng = _after(p["gmlp_ln_g"][l][None], hook("pre_gmlp", l, h0))
        wout = get_w("w_out", l, lng)
        x1, mg = _mixer_fwd(x, h0, hr, z, lng, p["gmlp_ln_b"][l][None], wsb[l], bsb[l], wout, l, tm)
        wfi = get_w("w_ffn_in", l, x1)
        wfo = get_w("w_ffn_out", l, x1)
        if l == 0:
            x2, ff, dff, h2 = _ffn_fwd(x1, p["norm2_g"][l][None], wfi, wfo, l, tm)
        else:
            dx, loss, dfg, ff, dff, h2 = _ffn_fwd(x1, p["norm2_g"][l][None], wfi, wfo, l, tm,
                                                  head=(p["final_g"][None], tgt))
        saved.append((x, z, h1, a0, a1, h0, hr, x1, mg, ff, dff, h2, win, wout, wfi, wfo, xcb, gates))
        x = x2
    pending = None
    for l in (1, 0):
        x0, z, h1, a0, a1, h0, hr, x1, mg, ff, dff, h2, win, wout, wfi, wfo, xcb, gates = saved[l]
        dgu, dx1, dg2 = _ffn_bwd(dx, wfo, dff, wfi.reshape(N_DEV, FF_BLK, D), x1, p["norm2_g"][l][None], l, tm)
        d_wfo = _mm_tn(ff, pl.BlockSpec((None, s, FF_BLK), lambda j: (j, 0, 0)), dx, _resident((s, D)),
                       4, (4, FF_BLK, D), pl.BlockSpec((None, FF_BLK, D), lambda j: (j, 0, 0)),
                       f"dw_ffn_out_l{l}", a_is_transposed=False)
        dgu8 = dgu.reshape(N_DEV, s, FF_BLK)
        d_wfi = _mm_tn(dgu8, pl.BlockSpec((None, s, FF_BLK), lambda j: (j, 0, 0)), h2, _resident((s, D)),
                       N_DEV, (N_DEV, FF_BLK, D), pl.BlockSpec((None, FF_BLK, D), lambda j: (j, 0, 0)),
                       f"dw_ffn_in_l{l}", a_is_transposed=False)
        d_wout = _mm_tn(mg, _resident((D, s)), dx1, pl.BlockSpec((s, D // 2), lambda j: (0, j)),
                        2, (D, D), pl.BlockSpec((D, D // 2), lambda j: (0, j)), f"dw_out_l{l}")
        token = hook("ffn_partials", l, dict(w_ffn_out=d_wfo.reshape(N_DEV, D_FF // N_DEV, D), w_ffn_in=d_wfi,
                                             w_out=d_wout.reshape(N_DEV, D // N_DEV, D)))
        pending = hook("mid_backward", l, dx1)
        dz, dh, dws, dbs, dlng, dlnb = _mixer_bwd(dx1, wout, h0, hr, z, _after(p["gmlp_ln_g"][l][None], token),
                                                  p["gmlp_ln_b"][l][None], wsb[l], wstb[l], bsb[l], l, tm)
        g1, g0 = _lru_scan(a1, dh, a0, dh, True, l)
        dxc, dwr, dwi, dbr, dbi, dlam = _lru_gates_bwd(
            xcb, gates, h0, hr, g0, g1, wrb[l], wib[l], _after(p["lru_lambda"][l], pending), l, tm)
        dz, dcw, dcb = _conv_bwd(dz, dxc, z, p["conv_w"][l], l, tm)
        small = dict(lru_w_r=dwr.reshape(-1, HD), lru_w_i=dwi.reshape(-1, HD), gmlp_w_s=dws.reshape(-1, HD),
                     gmlp_ln_g=dlng, gmlp_ln_b=dlnb, gmlp_b_s=dbs, conv_w=dcw, conv_b=dcb, lru_b_r=dbr,
                     lru_b_i=dbi, lru_lambda=dlam, norm2_g=dg2)
        if l == 1:
            small["final_g"] = dfg
        else:
            small["norm1_g"] = dg1
        started = hook("small_grads", l, small)
        d_win = _mm_tn(h1, _resident((D, s)), dz, pl.BlockSpec((s, IN_BLK), lambda j: (0, j)),
                       N_DEV, (N_DEV, D, IN_BLK), pl.BlockSpec((None, D, IN_BLK), lambda j: (j, 0, 0)),
                       f"dw_in_l{l}", after=started, jobs=hook("dw_in", l, started) or ())
        token = hook("mixer_partials", l, dict(w_in=d_win))
        dx, dg1 = _mm_nt_rms_bwd(
            dz, pl.BlockSpec((tm, N_IN), lambda i: (i, 0)),
            lambda r: [r[:, k * IN_BLK:(k + 1) * IN_BLK] for k in range(N_DEV)],
            win, False, x0, _after(p["norm1_g"][l][None], token, started, pending), dx1, f"inproj_bwd_dx_l{l}", tm,
            jobs=hook("inproj_bwd_dx", l, token) or ())
        pending = None
    return loss, dx, dg1


_REPL = ["norm1_g", "gmlp_ln_g", "gmlp_ln_b", "gmlp_w_s", "gmlp_b_s", "conv_b", "lru_w_r", "lru_w_i", "norm2_g", "final_g"]
_LANE_SHARDED = ["conv_w", "lru_b_r", "lru_b_i", "lru_lambda"]
_BIG = ["w_in", "w_out", "w_ffn_in", "w_ffn_out"]
_ORDER = ["norm1_g", "w_in", "gmlp_ln_g", "gmlp_ln_b", "gmlp_w_s", "gmlp_b_s", "conv_w", "conv_b", "lru_w_r", "lru_b_r",
          "lru_w_i", "lru_b_i", "lru_lambda", "w_out", "norm2_g", "w_ffn_in", "w_ffn_out", "final_g"]


def kernel(x, norm1_g, w_in, gmlp_ln_g, gmlp_ln_b, gmlp_w_s, gmlp_b_s, conv_w, conv_b, lru_w_r, lru_b_r, lru_w_i, lru_b_i, lru_lambda, w_out, norm2_g, w_ffn_in, w_ffn_out, final_g, loss_target, m_norm1_g, m_w_in, m_gmlp_ln_g, m_gmlp_ln_b, m_gmlp_w_s, m_gmlp_b_s, m_conv_w, m_conv_b, m_lru_w_r, m_lru_b_r, m_lru_w_i, m_lru_b_i, m_lru_lambda, m_w_out, m_norm2_g, m_w_ffn_in, m_w_ffn_out, m_final_g, v_norm1_g, v_w_in, v_gmlp_ln_g, v_gmlp_ln_b, v_gmlp_w_s, v_gmlp_b_s, v_conv_w, v_conv_b, v_lru_w_r, v_lru_b_r, v_lru_w_i, v_lru_b_i, v_lru_lambda, v_w_out, v_norm2_g, v_w_ffn_in, v_w_ffn_out, v_final_g):
    w = dict(norm1_g=norm1_g, w_in=w_in, gmlp_ln_g=gmlp_ln_g, gmlp_ln_b=gmlp_ln_b, gmlp_w_s=gmlp_w_s, gmlp_b_s=gmlp_b_s,
             conv_w=conv_w, conv_b=conv_b, lru_w_r=lru_w_r, lru_b_r=lru_b_r, lru_w_i=lru_w_i, lru_b_i=lru_b_i,
             lru_lambda=lru_lambda, w_out=w_out, norm2_g=norm2_g, w_ffn_in=w_ffn_in, w_ffn_out=w_ffn_out, final_g=final_g)
    mom = dict(norm1_g=m_norm1_g, w_in=m_w_in, gmlp_ln_g=m_gmlp_ln_g, gmlp_ln_b=m_gmlp_ln_b, gmlp_w_s=m_gmlp_w_s,
               gmlp_b_s=m_gmlp_b_s, conv_w=m_conv_w, conv_b=m_conv_b, lru_w_r=m_lru_w_r, lru_b_r=m_lru_b_r,
               lru_w_i=m_lru_w_i, lru_b_i=m_lru_b_i, lru_lambda=m_lru_lambda, w_out=m_w_out, norm2_g=m_norm2_g,
               w_ffn_in=m_w_ffn_in, w_ffn_out=m_w_ffn_out, final_g=m_final_g)
    var = dict(norm1_g=v_norm1_g, w_in=v_w_in, gmlp_ln_g=v_gmlp_ln_g, gmlp_ln_b=v_gmlp_ln_b, gmlp_w_s=v_gmlp_w_s,
               gmlp_b_s=v_gmlp_b_s, conv_w=v_conv_w, conv_b=v_conv_b, lru_w_r=v_lru_w_r, lru_b_r=v_lru_b_r,
               lru_w_i=v_lru_w_i, lru_b_i=v_lru_b_i, lru_lambda=v_lru_lambda, w_out=v_w_out, norm2_g=v_norm2_g,
               w_ffn_in=v_w_ffn_in, w_ffn_out=v_w_ffn_out, final_g=v_final_g)
    for src in (w, mom, var):
        src["w_ffn_in"] = jnp.swapaxes(src["w_ffn_in"], 1, 2)
    xi, yi, ci = _me()
    me = 4 * xi + 2 * yi + ci

    lane_shapes = [w[k].shape for k in _LANE_SHARDED]
    lane_rows = sum(a[0] * a[1] for a in lane_shapes)
    packed = jnp.concatenate([w[k].reshape(-1, HD) for k in _LANE_SHARDED])
    packed = jnp.pad(packed, ((0, -lane_rows % 8), (0, 0)))

    me1 = jnp.reshape(me, (1,)).astype(jnp.int32)
    gathers = {}
    exchanges = {}
    views = dict(w_in=(N_DEV, D, IN_BLK), w_out=(D, D), w_ffn_in=(2, 4, FF_BLK, D), w_ffn_out=(4, FF_BLK, D))
    small_ex = {}
    small_ag = {}

    casts = {}

    def start_gather(names, l, after):
        lands = [casts[(k, l)] if (k, l) in casts else _cast_into_slot(w[k], l, me1, f"cast_{k}_l{l}") for k in names]
        started, tok = _gather2_start(lands, after, f"gather_start_{'_'.join(names)}_l{l}")
        gathers.update({(k, l): h for k, h in zip(names, started)})
        return tok

    def relay_gather(names, l, after):
        relayed, tok = _gather2_relay([gathers[(k, l)] for k in names], after, f"gather_relay_{'_'.join(names)}_l{l}")
        gathers.update({(k, l): h for k, h in zip(names, relayed)})
        return tok

    def get_w(k, l, after):
        if (k, l) == ("w_in", 1):
            after = relay_gather(_BIG[:1], l, after)
        return _gather2_wait(gathers[(k, l)], after, f"gather_wait_{k}_l{l}").reshape(views[k])

    carried = {("inproj_bwd_dx", 1): [("w_ffn_out", 1), ("w_ffn_in", 1), ("w_out", 1)], ("dw_in", 0): [("w_in", 1)],
               ("inproj_bwd_dx", 0): [("w_ffn_out", 0), ("w_ffn_in", 0), ("w_out", 0)]}
    adam = {}

    def adam_jobs(shards, after):
        for k, l in shards:
            got = _exchange_wait(exchanges[(k, l)], after, f"exchange_wait_{k}_l{l}")
            adam[k] = _AdamJob(got, w[k], mom[k], var[k], l, adam[k].results if k in adam else None)
        return [adam[k] for k, _ in shards]

    def hook(stage, l, payload):
        if stage in ("dw_in", "inproj_bwd_dx"):
            return adam_jobs(carried.get((stage, l), []), payload)
        if stage == "pre_inproj":
            return start_gather(_BIG[1:], l, payload)
        if stage == "pre_gmlp":
            tok = relay_gather(_BIG[1:], l, payload)
            return tok + start_gather(_BIG[:1], l + 1, tok) if l == 0 else tok
        if stage == "small_grads":
            (small_ex[l],), tok = _exchange_start([_pack_small(payload)], f"exchange_start_small_l{l}")
            return tok
        if stage == "mid_backward":
            return reduce_small(l + 1, payload) if l == 0 else None
        extra = reduce_small(0, payload["w_in"]) if (stage, l) == ("mixer_partials", 0) else None
        started, tok = _exchange_start(list(payload.values()), f"exchange_start_{'_'.join(payload)}_l{l}")
        exchanges.update({(k, l): h for k, h in zip(payload, started)})
        return tok if extra is None else tok + extra

    def reduce_small(l, after):
        got = _exchange_wait(small_ex[l], after, f"exchange_wait_small_l{l}")
        mine = _sum8_into_slot(got, me1, f"sum_small_l{l}")
        (small_ag[l],), tok = _gather_start([mine], got, f"gather_start_small_l{l}")
        return tok

    land = lax.dynamic_update_slice(jnp.zeros((N_DEV,) + packed.shape, F32), packed[None], (me, 0, 0))
    (lanes_handle,), token = _gather_start([land], packed, "gather_start_lanes")
    token = start_gather(_BIG[:1], 0, token)
    later = [(k, l) for l in range(2) for k in _BIG if (k, l) != ("w_in", 0)]
    casts.update(zip(later, _cast_all_into_slots([w[k] for k, _ in later], [l for _, l in later], me1, token,
                                                 "cast_later_weights")))
    token = relay_gather(_BIG[:1], 0, casts[later[0]])
    lanes = _gather_wait(lanes_handle, token, "gather_wait_lanes")
    params = {k: w[k] for k in _REPL}
    off = 0
    for k, shp in zip(_LANE_SHARDED, lane_shapes):
        n = shp[0] * shp[1]
        params[k] = jnp.swapaxes(lanes[:, off:off + n], 0, 1).reshape(shp[0], shp[1], D)
        off += n
    loss, dx, dg1 = _local_step(x[0], loss_target[0], params, get_w, hook)

    out = {k: job.results for k, job in adam.items()}
    after = dx
    g_small = [_gather_wait(small_ag[l], after, f"gather_wait_small_l{l}").reshape(SMALL_ROWS, HD) for l in (0, 1)]
    row0 = 0
    for k, rows in SMALL_MATRICES:
        res = _adam_matrix(*g_small, *[src[k].reshape(2, rows, HD) for src in (w, mom, var)], row0, f"adam_{k}")
        out[k] = [a.reshape(w[k].shape) for a in res]
        after = res[3]
        row0 += rows
    got = _exchange_wait(exchanges[("w_in", 0)], after, "exchange_wait_w_in_l0")
    out["w_in"] = _adam_shard(got, w["w_in"], mom["w_in"], var["w_in"], 0, out["w_in"], "adam_w_in_l0")
    out["w_ffn_in"] = [jnp.swapaxes(a, 1, 2) for a in out["w_ffn_in"]]
    as_rows = lambda a: a.reshape(1, D) if a.ndim == 1 else a
    vec = _adam_vectors(*g_small, _all_gather(dg1, out["w_in"][3], "gather_norm1_grad"), me1,
                        *[{k: as_rows(src[k]) for k, _ in SMALL_VECTORS} for src in (w, mom, var)])
    out.update({k: [a.reshape(w[k].shape) for a in res] for k, res in vec.items()})

    loss = lax.psum(loss[0, 0], MESH_AXES)
    return (loss, dx[None], *[out[k][0] for k in _ORDER], *[out[k][1] for k in _ORDER],
            *[out[k][2] for k in _ORDER], *[out[k][3] for k in _ORDER])
```

```python
import jax
import jax.numpy as jnp
from jax import lax
from jax.experimental import pallas as pl
from jax.experimental.pallas import tpu as pltpu

F32 = jnp.float32
BF16 = jnp.bfloat16
SDS = jax.ShapeDtypeStruct

D = 1024
N_IN = 6 * D
D_FF = 2816
N_DEV = 8
IN_BLK = N_IN // N_DEV
FF_BLK = 2 * D_FF // N_DEV
HEADS = 8
HD = 128
EPS = 1e-6
LRU_C = 8.0
MESH_AXES = ("x", "y", "c")

ADAM_LR = 0.001
ADAM_B1 = 0.9
ADAM_B2 = 0.999
ADAM_EPS = 1e-08
ADAM_WD = 0.01
ADAM_STEP = 10

VMEM_LIMIT = 60 * 2**20


def _cp(*sem, **kw):
    return pltpu.CompilerParams(dimension_semantics=sem, vmem_limit_bytes=VMEM_LIMIT, **kw)


def _row_tile(s):
    return 512 if s >= 1024 else s // 2


_GELU_C = 0.7978845608028654


def _gelu(x):
    t = jnp.tanh(_GELU_C * (x + 0.044715 * (x * x * x)))
    return 0.5 * x * (1.0 + t), t


def _gelu_grad(x, t):
    return 0.5 * (1.0 + t) + 0.5 * x * (1.0 - t * t) * (_GELU_C * (1.0 + 0.134145 * (x * x)))


def _sigmoid(x):
    return 0.5 + 0.5 * jnp.tanh(0.5 * x)


def _softplus(x):
    e = jnp.exp(-jnp.abs(x))
    w = 1.0 + e
    l1p = jnp.where(w == 1.0, e, jnp.log(w) * e / jnp.where(w == 1.0, 1.0, w - 1.0))
    return jnp.maximum(x, 0.0) + l1p


def _rms_fwd(x, g):
    r = lax.rsqrt(jnp.mean(x * x, axis=-1, keepdims=True) + EPS)
    return x * r * g


def _rms_bwd(x, g, dh):
    r = lax.rsqrt(jnp.mean(x * x, axis=-1, keepdims=True) + EPS)
    xh = x * r
    dxh = dh * g
    dx = r * (dxh - xh * jnp.mean(dxh * xh, axis=-1, keepdims=True))
    dg = jnp.sum(dh * xh, axis=0, keepdims=True)
    return dx, dg


LANE_ROWS = D // HD


def _add_rows128(ref, vec, row0=0):
    for i in range(vec.shape[0]):
        for k in range(LANE_ROWS):
            j = row0 + i * LANE_ROWS + k
            ref[j:j + 1, :] += vec[i:i + 1, k * HD:(k + 1) * HD]


def _dot(a, b):
    return jnp.dot(a, b, preferred_element_type=F32)


def _dot_nt(a, b):
    return lax.dot_general(a, b, (((1,), (1,)), ((), ())), preferred_element_type=F32)


def _dot_tn(a, b):
    return lax.dot_general(a, b, (((0,), (0,)), ((), ())), preferred_element_type=F32)


def _taps(prev, cur, nxt, tm):
    hr = prev.shape[0]
    ext = jnp.concatenate([prev, cur, nxt], axis=0)
    n = tm + 2 * hr
    sl = slice(hr, hr + tm)
    return (pltpu.roll(ext, 2, 0)[sl], pltpu.roll(ext, 1, 0)[sl], cur,
            pltpu.roll(ext, n - 1, 0)[sl], pltpu.roll(ext, n - 2, 0)[sl])


def _halo_specs(tm, s, col, rows=8):
    nb = s // rows
    r = tm // rows
    return (pl.BlockSpec((rows, D), lambda i: (jnp.maximum(i * r - 1, 0), col)),
            pl.BlockSpec((tm, D), lambda i: (i, col)),
            pl.BlockSpec((rows, D), lambda i: (jnp.minimum((i + 1) * r, nb - 1), col)))


def _halo_load(prev_ref, cur_ref, next_ref, fp, fn):
    return prev_ref[...].astype(F32) * fp, cur_ref[...].astype(F32), next_ref[...].astype(F32) * fn


def _halo_flags(nt):
    i = pl.program_id(0)
    return (i > 0).astype(F32), (i < nt - 1).astype(F32)


def _full(shape):
    nd = len(shape)
    return pl.BlockSpec(shape, lambda *_: (0,) * nd)


def _resident(shape):
    nd = len(shape)
    return pl.BlockSpec(shape, lambda *_: (0,) * nd, pipeline_mode=pl.Buffered(1))


def _norm_inproj(x, g, w, layer, tm):
    s = x.shape[0]

    def body(x_ref, g_ref, w_ref, z_ref, ht_ref):
        h32 = _rms_fwd(x_ref[...], g_ref[...])
        ht_ref[...] = h32.T.astype(BF16)
        h = h32.astype(BF16)
        for j in range(N_DEV):
            z_ref[:, j * IN_BLK:(j + 1) * IN_BLK] = _dot(h, w_ref[j]).astype(BF16)

    return pl.pallas_call(
        body, name=f"norm_inproj_l{layer}", grid=(s // tm,),
        in_specs=[pl.BlockSpec((tm, D), lambda i: (i, 0)), _full((1, D)), _resident((N_DEV, D, IN_BLK))],
        out_specs=[pl.BlockSpec((tm, N_IN), lambda i: (i, 0)), pl.BlockSpec((D, tm), lambda i: (0, i))],
        out_shape=[SDS((s, N_IN), BF16), SDS((D, s), BF16)],
        compiler_params=_cp("parallel"))(x, g, w)


def _gmlp_values(zu_ref, zv_ref, lng_ref, lnb_ref):
    zu = zu_ref[...].astype(F32)
    zv = zv_ref[...].astype(F32)
    u, tu = _gelu(zu)
    gv, tv = _gelu(zv)
    xc = gv - jnp.mean(gv, axis=-1, keepdims=True)
    rstd = lax.rsqrt(jnp.mean(xc * xc, axis=-1, keepdims=True) + EPS)
    xh = xc * rstd
    vb = (xh * lng_ref[...] + lnb_ref[...]).astype(BF16)
    return zu, zv, u, tu, tv, xh, rstd, vb


def _mixer_fwd(x, h0, h1, z, lng, lnb, ws, bsb, wo, layer, tm):
    s = x.shape[0]

    def body(x_ref, h0_ref, h1_ref, zu_ref, zv_ref, zg_ref, za_ref, zb_ref, lng_ref, lnb_ref, ws_ref, bsb_ref,
             wo_ref, x1_ref, mg_ref, ya_s):
        _, _, u, _, _, _, _, vb = _gmlp_values(zu_ref, zv_ref, lng_ref, lnb_ref)
        for c in range(tm // HD):
            rs = slice(c * HD, (c + 1) * HD)
            for g in range(HEADS):
                cs = slice(g * HD, (g + 1) * HD)
                ya_s[rs, cs] = u[rs, cs] * (_dot(ws_ref[g], vb[rs, cs]) + bsb_ref[g])
        gg, _ = _gelu(zg_ref[...].astype(F32))
        yb = (h0_ref[...] + h1_ref[...]) * gg
        m32 = _sigmoid(za_ref[...].astype(F32)) * ya_s[...] + _sigmoid(zb_ref[...].astype(F32)) * yb
        mg_ref[...] = m32.T.astype(BF16)
        x1_ref[...] = x_ref[...] + _dot(m32.astype(BF16), wo_ref[...])

    tile = pl.BlockSpec((tm, D), lambda i: (i, 0))
    wspec = _full((HEADS, HD, HD))
    return pl.pallas_call(
        body, name=f"mixer_fwd_l{layer}", grid=(s // tm,),
        in_specs=[tile, tile, tile] + [pl.BlockSpec((tm, D), lambda i, c=c: (i, c)) for c in (0, 1, 3, 4, 5)]
        + [_full((1, D)), _full((1, D)), wspec, wspec, _full((D, D))],
        out_specs=[tile, pl.BlockSpec((D, tm), lambda i: (0, i))], out_shape=[SDS((s, D), F32), SDS((D, s), BF16)],
        scratch_shapes=[pltpu.VMEM((tm, D), F32)],
        compiler_params=_cp("parallel"))(x, h0, h1, z, z, z, z, z, lng, lnb, ws, bsb, wo)


def _conv(taps, cw_ref, cb_ref):
    _, m1, c0, p1, p2 = taps
    return cb_ref[...] + m1 * cw_ref[0:1, :] + c0 * cw_ref[1:2, :] + p1 * cw_ref[2:3, :] + p2 * cw_ref[3:4, :]


def _heads_dot(xb, w_ref, d):
    return jnp.concatenate([_dot(xb[:, h * HD:(h + 1) * HD], w_ref[d, h]) for h in range(HEADS)], axis=1)


def _lru_decay(r, sp):
    la = (-LRU_C) * r * sp
    a = jnp.exp(la)
    return a, jnp.tanh(-la) * (a * a + 1.0)


def _lru_gates_fwd(z, cw, cb, wr, wi, br, bi, lam, layer, tm):
    s = z.shape[0]
    nt = s // tm

    def body(zp_ref, zc_ref, zn_ref, cw_ref, cb_ref, wr_ref, wi_ref, br_ref, bi_ref, lam_ref,
             a0_ref, b0_ref, a1_ref, b1_ref, xc_ref, r0_ref, i0_ref, r1_ref, i1_ref):
        fp, fn = _halo_flags(nt)
        xc = _conv(_taps(*_halo_load(zp_ref, zc_ref, zn_ref, fp, fn), tm), cw_ref, cb_ref)
        xb = xc.astype(BF16)
        xc_ref[...] = xb
        for d, (a_ref, b_ref, r_ref, i_ref) in enumerate(((a0_ref, b0_ref, r0_ref, i0_ref),
                                                          (a1_ref, b1_ref, r1_ref, i1_ref))):
            r = _sigmoid(_heads_dot(xb, wr_ref, d) + br_ref[d:d + 1, :])
            ig = _sigmoid(_heads_dot(xb, wi_ref, d) + bi_ref[d:d + 1, :])
            a, q = _lru_decay(r, _softplus(-lam_ref[d:d + 1, :]))
            a_ref[...] = a
            b_ref[...] = jnp.sqrt(q) * (ig * xc)
            r_ref[...] = r.astype(BF16)
            i_ref[...] = ig.astype(BF16)

    tile = pl.BlockSpec((tm, D), lambda i: (i, 0))
    return pl.pallas_call(
        body, name=f"lru_gates_fwd_l{layer}", grid=(nt,),
        in_specs=[*_halo_specs(tm, s, 2, 16), _full((4, D)), _full((1, D)),
                  _full((2, HEADS, HD, HD)), _full((2, HEADS, HD, HD)), _full((2, D)), _full((2, D)), _full((2, D))],
        out_specs=[tile] * 9, out_shape=[SDS((s, D), F32)] * 4 + [SDS((s, D), BF16)] * 5,
        compiler_params=_cp("parallel"))(z, z, z, cw, cb, wr, wi, br, bi, lam)


def _scan_group(a, x, c, reverse, bwd):
    row = lax.broadcasted_iota(jnp.int32, a.shape, 0)
    b = a * x if bwd else x
    for d in (1, 2, 4):
        keep = (row < 8 - d) if reverse else (row >= d)
        sh = 8 - d if reverse else d
        a_s = jnp.where(keep, pltpu.roll(a, sh, 0), 1.0)
        b_s = jnp.where(keep, pltpu.roll(b, sh, 0), 0.0)
        b = a * b_s + b
        a = a * a_s
    h = b + a * c
    new_c = h[0:1, :] if reverse else h[7:8, :]
    if not bwd:
        return h, new_c
    if reverse:
        prev = jnp.where(row < 7, pltpu.roll(h, 7, 0), c)
    else:
        prev = jnp.where(row >= 1, pltpu.roll(h, 1, 0), c)
    return x + prev, new_c


def _lru_scan(a_f, x_f, a_r, x_r, bwd, layer):
    s = a_f.shape[0]
    ts = min(1024, s // 2)
    cb = 512
    nt = s // ts
    ng = ts // 8

    def body(af_ref, xf_ref, ar_ref, xr_ref, of_ref, or_ref, cf, cr):
        @pl.when(pl.program_id(1) == 0)
        def _():
            cf[...] = jnp.zeros_like(cf)
            cr[...] = jnp.zeros_like(cr)

        def step(j, carry):
            c_f, c_r = carry
            rf = pl.multiple_of(j * 8, 8)
            rr = pl.multiple_of((ng - 1 - j) * 8, 8)
            o, c_f = _scan_group(af_ref[pl.ds(rf, 8), :], xf_ref[pl.ds(rf, 8), :], c_f, False, bwd)
            of_ref[pl.ds(rf, 8), :] = o
            o, c_r = _scan_group(ar_ref[pl.ds(rr, 8), :], xr_ref[pl.ds(rr, 8), :], c_r, True, bwd)
            or_ref[pl.ds(rr, 8), :] = o
            return c_f, c_r

        c_f, c_r = lax.fori_loop(0, ng, step, (cf[0:1, :], cr[0:1, :]), unroll=2)
        cf[...] = jnp.broadcast_to(c_f, cf.shape)
        cr[...] = jnp.broadcast_to(c_r, cr.shape)

    fwd = pl.BlockSpec((ts, cb), lambda c, t: (t, c))
    rev = pl.BlockSpec((ts, cb), lambda c, t: (nt - 1 - t, c))
    return pl.pallas_call(
        body, name=f"lru_scan_{'bwd' if bwd else 'fwd'}_l{layer}", grid=(D // cb, nt),
        in_specs=[fwd, fwd, rev, rev], out_specs=[fwd, rev],
        out_shape=[SDS((s, D), F32)] * 2,
        scratch_shapes=[pltpu.VMEM((8, cb), F32), pltpu.VMEM((8, cb), F32)],
        compiler_params=_cp("parallel", "arbitrary"))(a_f, x_f, a_r, x_r)


def _ffn_fwd(x1, g, wfi, wfo, layer, tm, head=None):
    s = x1.shape[0]

    def ffn(x_ref, g_ref, wi_ref, wo_ref, ff_ref, dff_ref, h_ref):
        x = x_ref[...]
        h = _rms_fwd(x, g_ref[...]).astype(BF16)
        h_ref[...] = h
        acc = x
        for k in range(4):
            gate = _dot_nt(h, wi_ref[0, k])
            up = _dot_nt(h, wi_ref[1, k])
            sg = _sigmoid(gate)
            silu = gate * sg
            ff = (silu * up).astype(BF16)
            ff_ref[k] = ff
            dff_ref[0, k] = (up * (sg * (1.0 + gate * (1.0 - sg)))).astype(BF16)
            dff_ref[1, k] = silu.astype(BF16)
            acc = acc + _dot(ff, wo_ref[k])
        return acc

    def body(x_ref, g_ref, wi_ref, wo_ref, x2_ref, ff_ref, dff_ref, h_ref):
        x2_ref[...] = ffn(x_ref, g_ref, wi_ref, wo_ref, ff_ref, dff_ref, h_ref)

    def body_with_head(x_ref, g_ref, wi_ref, wo_ref, fg_ref, t_ref, dx_ref, loss_ref, dfg_ref, ff_ref, dff_ref, h_ref):
        @pl.when(pl.program_id(0) == 0)
        def _():
            loss_ref[...] = jnp.zeros_like(loss_ref)
            dfg_ref[...] = jnp.zeros_like(dfg_ref)

        x2 = ffn(x_ref, g_ref, wi_ref, wo_ref, ff_ref, dff_ref, h_ref)
        fg = fg_ref[...]
        e = _rms_fwd(x2, fg) - t_ref[...]
        rows = jnp.sum(e * e, axis=-1, keepdims=True)
        loss_ref[...] += (0.5 / D) * jnp.sum(rows, axis=0, keepdims=True)
        dx, dg = _rms_bwd(x2, fg, e * (1.0 / D))
        dx_ref[...] = dx
        _add_rows128(dfg_ref, dg)

    tile = pl.BlockSpec((tm, D), lambda i: (i, 0))
    weights = [_resident((2, 4, FF_BLK, D)), _resident((4, FF_BLK, D))]
    kept_specs = [pl.BlockSpec((4, tm, FF_BLK), lambda i: (0, i, 0)),
                  pl.BlockSpec((2, 4, tm, FF_BLK), lambda i: (0, 0, i, 0)), tile]
    kept_shapes = [SDS((4, s, FF_BLK), BF16), SDS((2, 4, s, FF_BLK), BF16), SDS((s, D), BF16)]
    if head is None:
        return pl.pallas_call(
            body, name=f"ffn_fwd_l{layer}", grid=(s // tm,),
            in_specs=[tile, _full((1, D))] + weights, out_specs=[tile] + kept_specs,
            out_shape=[SDS((s, D), F32)] + kept_shapes, compiler_params=_cp("parallel"))(x1, g, wfi, wfo)
    final_g, tgt = head
    return pl.pallas_call(
        body_with_head, name=f"ffn_fwd_loss_l{layer}", grid=(s // tm,),
        in_specs=[tile, _full((1, D))] + weights + [_full((1, D)), tile],
        out_specs=[tile, _full((1, 1)), _full((LANE_ROWS, HD))] + kept_specs,
        out_shape=[SDS((s, D), F32), SDS((1, 1), F32), SDS((LANE_ROWS, HD), F32)] + kept_shapes,
        compiler_params=_cp("arbitrary"))(x1, g, wfi, wfo, final_g, tgt)


def _ffn_bwd(dx2, wfo, factors, wfi, x1, g, layer, tm):
    s = dx2.shape[0]

    def body(dx_ref, wo_ref, f_ref, wi_ref, x_ref, g_ref, dgu_ref, dx1_ref, dg_ref):
        @pl.when(pl.program_id(0) == 0)
        def _():
            dg_ref[...] = jnp.zeros_like(dg_ref)

        dx = dx_ref[...]
        dxb = dx.astype(BF16)
        dh = None
        for k in range(4):
            dff = _dot_nt(dxb, wo_ref[k])
            d_gate = (dff * f_ref[0, k].astype(F32)).astype(BF16)
            d_up = (dff * f_ref[1, k].astype(F32)).astype(BF16)
            dgu_ref[0, k] = d_gate
            dgu_ref[1, k] = d_up
            part = _dot(d_gate, wi_ref[k]) + _dot(d_up, wi_ref[4 + k])
            dh = part if dh is None else dh + part
        dxn, dg = _rms_bwd(x_ref[...], g_ref[...], dh)
        dx1_ref[...] = dx + dxn
        _add_rows128(dg_ref, dg)

    tile = pl.BlockSpec((tm, D), lambda i: (i, 0))
    blk = pl.BlockSpec((2, 4, tm, FF_BLK), lambda i: (0, 0, i, 0))
    return pl.pallas_call(
        body, name=f"ffn_bwd_l{layer}", grid=(s // tm,),
        in_specs=[tile, _resident((4, FF_BLK, D)), blk, _resident((N_DEV, FF_BLK, D)), tile, _full((1, D))],
        out_specs=[blk, tile, _full((LANE_ROWS, HD))],
        out_shape=[SDS((2, 4, s, FF_BLK), BF16), SDS((s, D), F32), SDS((LANE_ROWS, HD), F32)],
        compiler_params=_cp("arbitrary"))(dx2, wfo, factors, wfi, x1, g)


def _mm_nt_rms_bwd(a, a_spec, a_blocks, w, w_is_transposed, x, g, dres, name, tm, jobs=()):
    s = x.shape[0]

    def body(a_ref, w_ref, x_ref, g_ref, dres_ref, dx_ref, dg_ref):
        @pl.when(pl.program_id(0) == 0)
        def _():
            dg_ref[...] = jnp.zeros_like(dg_ref)

        dh = None
        for k, blk in enumerate(a_blocks(a_ref)):
            part = _dot(blk, w_ref[k]) if w_is_transposed else _dot_nt(blk, w_ref[k])
            dh = part if dh is None else dh + part
        dx, dg = _rms_bwd(x_ref[...], g_ref[...], dh)
        dx_ref[...] = dres_ref[...] + dx
        _add_rows128(dg_ref, dg)

    tile = pl.BlockSpec((tm, D), lambda i: (i, 0))
    body, job_in, job_args, job_out, job_shapes, aliases = _carry(jobs, s // tm, body, 5, 2)
    res = pl.pallas_call(
        body, name=name, grid=(s // tm,),
        in_specs=[a_spec, _resident(w.shape), tile, _full((1, D)), tile] + job_in,
        out_specs=[tile, _full((LANE_ROWS, HD))] + job_out,
        out_shape=[SDS((s, D), F32), SDS((LANE_ROWS, HD), F32)] + job_shapes,
        input_output_aliases=aliases, compiler_params=_cp("arbitrary"))(a, w, x, g, dres, *job_args)
    for j, job in enumerate(jobs):
        job.results = res[2 + 4 * j:6 + 4 * j]
    return res[0], res[1]


def _mm_tn(a, a_spec, b, b_spec, nb, out_shape, out_spec, name, a_is_transposed=True, after=None, jobs=()):
    def body(a_ref, b_ref, *rest):
        o_ref = rest[-1]
        bb = b_ref[...].astype(BF16)
        o_ref[...] = (_dot(a_ref[...], bb) if a_is_transposed else _dot_tn(a_ref[...], bb)).astype(BF16)

    deps = [] if after is None else [after]
    body, job_in, job_args, job_out, job_shapes, aliases = _carry(jobs, nb, body, 2 + len(deps), 1)
    res = pl.pallas_call(
        body, name=name, grid=(nb,), in_specs=[a_spec, b_spec] + [_ANY] * len(deps) + job_in,
        out_specs=[out_spec] + job_out, out_shape=[SDS(out_shape, BF16)] + job_shapes,
        input_output_aliases=aliases, compiler_params=_cp("parallel"))(a, b, *deps, *job_args)
    for j, job in enumerate(jobs):
        job.results = res[1 + 4 * j:5 + 4 * j]
    return res[0]


def _mixer_bwd(dx1, wo, h0, h1, z, lng, lnb, ws, wst, bsb, layer, tm):
    s = dx1.shape[0]
    nt = s // tm

    def body(dx_ref, wo_ref, h0_ref, h1_ref, zu_ref, zv_ref, zg_ref, za_ref, zb_ref, lng_ref, lnb_ref,
             ws_ref, wst_ref, bsb_ref, dz_ref, dh_ref, dws_ref, dbs_ref, dlng_ref, dlnb_ref,
             du_s, dv_s, ya_s, dbs_acc):
        i = pl.program_id(0)

        @pl.when(i == 0)
        def _():
            for r in (dws_ref, dlng_ref, dlnb_ref, dbs_acc):
                r[...] = jnp.zeros_like(r)

        dm = _dot_nt(dx_ref[...].astype(BF16), wo_ref[...])
        sa = _sigmoid(za_ref[...].astype(F32))
        sb = _sigmoid(zb_ref[...].astype(F32))
        zg = zg_ref[...].astype(F32)
        gg, tg = _gelu(zg)
        hs = h0_ref[...] + h1_ref[...]
        dyb = dm * sb
        dya = dm * sa
        dh_ref[...] = dyb * gg
        dz_ref[:, 2 * D:3 * D] = jnp.zeros((tm, D), BF16)
        dz_ref[:, 3 * D:4 * D] = (dyb * hs * _gelu_grad(zg, tg)).astype(BF16)
        dz_ref[:, 5 * D:6 * D] = (dm * (hs * gg) * (sb * (1.0 - sb))).astype(BF16)

        zu, zv, u, tu, tv, xh, rstd, vb = _gmlp_values(zu_ref, zv_ref, lng_ref, lnb_ref)
        for c in range(tm // HD):
            rs = slice(c * HD, (c + 1) * HD)
            for g in range(HEADS):
                cs = slice(g * HD, (g + 1) * HD)
                vblk = vb[rs, cs]
                mixed = _dot(ws_ref[g], vblk) + bsb_ref[g]
                ya_s[rs, cs] = u[rs, cs] * mixed
                du_s[rs, cs] = dya[rs, cs] * mixed
                dmx = dya[rs, cs] * u[rs, cs]
                dbs_acc[g] += dmx
                dmxb = dmx.astype(BF16)
                dws_ref[g] += _dot_nt(dmxb, vblk)
                dv_s[rs, cs] = _dot(wst_ref[g], dmxb)
        dz_ref[:, 4 * D:5 * D] = (dm * ya_s[...] * (sa * (1.0 - sa))).astype(BF16)
        dv = dv_s[...]
        _add_rows128(dlng_ref, jnp.sum(dv * xh, axis=0, keepdims=True))
        _add_rows128(dlnb_ref, jnp.sum(dv, axis=0, keepdims=True))
        dxh = dv * lng_ref[...]
        dgv = rstd * (dxh - jnp.mean(dxh, axis=-1, keepdims=True)
                      - xh * jnp.mean(dxh * xh, axis=-1, keepdims=True))
        dz_ref[:, 0:D] = (du_s[...] * _gelu_grad(zu, tu)).astype(BF16)
        dz_ref[:, D:2 * D] = (dgv * _gelu_grad(zv, tv)).astype(BF16)

        @pl.when(i == nt - 1)
        def _():
            for g in range(HEADS):
                dbs_ref[g:g + 1, :] = jnp.sum(dbs_acc[g].T, axis=0, keepdims=True)

    tile = pl.BlockSpec((tm, D), lambda i: (i, 0))
    wspec = _full((HEADS, HD, HD))
    return pl.pallas_call(
        body, name=f"mixer_bwd_l{layer}", grid=(nt,),
        in_specs=[tile, _full((D, D)), tile, tile]
        + [pl.BlockSpec((tm, D), lambda i, c=c: (i, c)) for c in (0, 1, 3, 4, 5)]
        + [_full((1, D)), _full((1, D)), wspec, wspec, wspec],
        out_specs=[pl.BlockSpec((tm, N_IN), lambda i: (i, 0)), tile, wspec, _full((HEADS, HD)),
                   _full((LANE_ROWS, HD)), _full((LANE_ROWS, HD))],
        out_shape=[SDS((s, N_IN), BF16), SDS((s, D), F32), SDS((HEADS, HD, HD), F32), SDS((HEADS, HD), F32),
                   SDS((LANE_ROWS, HD), F32), SDS((LANE_ROWS, HD), F32)],
        scratch_shapes=[pltpu.VMEM((tm, D), F32)] * 3 + [pltpu.VMEM((HEADS, HD, HD), F32)],
        compiler_params=_cp("arbitrary"))(dx1, wo, h0, h1, z, z, z, z, z, lng, lnb, ws, wst, bsb)


def _lru_gates_bwd(xcb, gates, h0, h1, g0, g1, wr, wi, lam, layer, tm):
    s = xcb.shape[0]
    nt = s // tm

    def body(xc_ref, r0_ref, i0_ref, r1_ref, i1_ref, h0p_ref, h0_ref, h1_ref, h1n_ref, g0_ref, g1_ref,
             wr_ref, wi_ref, lam_ref, dxc_ref, dwr_ref, dwi_ref, dbr_ref, dbi_ref, dlam_ref):
        i = pl.program_id(0)
        fp, fn = _halo_flags(nt)

        @pl.when(i == 0)
        def _():
            for r in (dwr_ref, dwi_ref, dbr_ref, dbi_ref, dlam_ref):
                r[...] = jnp.zeros_like(r)

        xb = xc_ref[...]
        xc = xb.astype(F32)
        zeros8 = jnp.zeros((8, D), F32)
        h_prev = _taps(h0p_ref[...] * fp, h0_ref[...], zeros8, tm)[1]
        h_next = _taps(zeros8, h1_ref[...], h1n_ref[...] * fn, tm)[3]
        dxc = jnp.zeros((tm, D), F32)
        for d, (g_ref, hsh, r_ref, i_ref) in enumerate(((g0_ref, h_prev, r0_ref, i0_ref),
                                                        (g1_ref, h_next, r1_ref, i1_ref))):
            sp = _softplus(-lam_ref[d:d + 1, :])
            r = r_ref[...].astype(F32)
            ig = i_ref[...].astype(F32)
            a, q = _lru_decay(r, sp)
            rmult = jnp.where(q > 0.0, lax.rsqrt(jnp.where(q > 0.0, q, 1.0)), 0.0)
            mult = q * rmult
            db = g_ref[...]
            da = db * hsh
            dmult = db * (ig * xc)
            di = db * (mult * xc)
            dxc = dxc + db * (mult * ig)
            dla = da * a - dmult * (a * a * rmult)
            dsp_dlam = -_sigmoid(-lam_ref[d:d + 1, :])
            _add_rows128(dlam_ref, jnp.sum(dla * r, axis=0, keepdims=True) * ((-LRU_C) * dsp_dlam), d * LANE_ROWS)
            dpr = dla * sp * (-LRU_C) * (r * (1.0 - r))
            dpi = di * (ig * (1.0 - ig))
            _add_rows128(dbr_ref, jnp.sum(dpr, axis=0, keepdims=True), d * LANE_ROWS)
            _add_rows128(dbi_ref, jnp.sum(dpi, axis=0, keepdims=True), d * LANE_ROWS)
            dprb = dpr.astype(BF16)
            dpib = dpi.astype(BF16)
            parts = []
            for h in range(HEADS):
                cs = slice(h * HD, (h + 1) * HD)
                dwr_ref[d, h] += _dot_tn(xb[:, cs], dprb[:, cs])
                dwi_ref[d, h] += _dot_tn(xb[:, cs], dpib[:, cs])
                parts.append(_dot_nt(dprb[:, cs], wr_ref[d, h]) + _dot_nt(dpib[:, cs], wi_ref[d, h]))
            dxc = dxc + jnp.concatenate(parts, axis=1)
        dxc_ref[...] = dxc.astype(BF16)

    tile = pl.BlockSpec((tm, D), lambda i: (i, 0))
    hp, hc, hn = _halo_specs(tm, s, 0)
    wspec = _full((2, HEADS, HD, HD))
    vspec = _full((2 * LANE_ROWS, HD))
    return pl.pallas_call(
        body, name=f"lru_gates_bwd_l{layer}", grid=(nt,),
        in_specs=[tile] * 5 + [hp, hc, hc, hn, tile, tile, wspec, wspec, _full((2, D))],
        out_specs=[tile, wspec, wspec, vspec, vspec, vspec],
        out_shape=[SDS((s, D), BF16), SDS((2, HEADS, HD, HD), F32), SDS((2, HEADS, HD, HD), F32)]
        + [SDS((2 * LANE_ROWS, HD), F32)] * 3,
        compiler_params=_cp("arbitrary"))(xcb, *gates, h0, h0, h1, h1, g0, g1, wr, wi, lam)


def _conv_bwd(dz, dxc, z, cw, layer, tm):
    s = z.shape[0]
    nt = s // tm

    def body(dz_in, dp_ref, dc_ref, dn_ref, zp_ref, zc_ref, zn_ref, cw_ref, dz_ref, dcw_ref, dcb_ref):
        del dz_in
        fp, fn = _halo_flags(nt)

        @pl.when(pl.program_id(0) == 0)
        def _():
            dcw_ref[...] = jnp.zeros_like(dcw_ref)
            dcb_ref[...] = jnp.zeros_like(dcb_ref)

        dxc_halo = _halo_load(dp_ref, dc_ref, dn_ref, fp, fn)
        dxc = dxc_halo[1]
        dm2, dm1, _, dp1, _ = _taps(*dxc_halo, tm)
        dz_ref[...] = (cw_ref[0:1, :] * dp1 + cw_ref[1:2, :] * dxc + cw_ref[2:3, :] * dm1
                       + cw_ref[3:4, :] * dm2).astype(BF16)
        _, zm1, z0, zp1, zp2 = _taps(*_halo_load(zp_ref, zc_ref, zn_ref, fp, fn), tm)
        for k, zt in enumerate((zm1, z0, zp1, zp2)):
            _add_rows128(dcw_ref, jnp.sum(dxc * zt, axis=0, keepdims=True), k * LANE_ROWS)
        _add_rows128(dcb_ref, jnp.sum(dxc, axis=0, keepdims=True))

    return pl.pallas_call(
        body, name=f"conv_bwd_l{layer}", grid=(nt,),
        in_specs=[pl.BlockSpec(memory_space=pl.ANY), *_halo_specs(tm, s, 0, 16), *_halo_specs(tm, s, 2, 16),
                  _full((4, D))],
        out_specs=[pl.BlockSpec((tm, D), lambda i: (i, 2)), _full((4 * LANE_ROWS, HD)), _full((LANE_ROWS, HD))],
        out_shape=[SDS((s, N_IN), BF16), SDS((4 * LANE_ROWS, HD), F32), SDS((LANE_ROWS, HD), F32)],
        input_output_aliases={0: 0},
        compiler_params=_cp("arbitrary"))(dz, dxc, dxc, dxc, z, z, z, cw)


def _me():
    return lax.axis_index("x"), lax.axis_index("y"), lax.axis_index("c")


def _peer(m):
    x, y, c = _me()
    px = 1 - x if m & 4 else x
    py = 1 - y if m & 2 else y
    pc = 1 - c if m & 1 else c
    return (px, py, pc), 4 * px + 2 * py + pc


_ANY = pl.BlockSpec(memory_space=pl.ANY)
_EXCHANGE_SEMS = [pltpu.SemaphoreType.DMA((N_DEV - 1,)), pltpu.SemaphoreType.DMA((N_DEV - 1,)), pltpu.SemaphoreType.DMA(())]


def _all_gather(v, after, name):
    def body(v_ref, after_ref, o_ref, send_sems, recv_sems, local_sem):
        del after_ref
        x, y, c = _me()
        me = 4 * x + 2 * y + c
        local = pltpu.make_async_copy(v_ref, o_ref.at[me], local_sem)
        local.start()
        sends = []
        for m in range(1, N_DEV):
            dev, _ = _peer(m)
            cp = pltpu.make_async_remote_copy(v_ref, o_ref.at[me], send_sems.at[m - 1], recv_sems.at[m - 1],
                                              device_id=dev, device_id_type=pl.DeviceIdType.MESH)
            cp.start()
            sends.append(cp)
        for m in range(1, N_DEV):
            dev, blk = _peer(m)
            pltpu.make_async_remote_copy(v_ref, o_ref.at[blk], send_sems.at[m - 1], recv_sems.at[m - 1],
                                         device_id=dev, device_id_type=pl.DeviceIdType.MESH).wait_recv()
        for cp in sends:
            cp.wait_send()
        local.wait()

    return pl.pallas_call(
        body, name=name, in_specs=[_ANY, _ANY], out_specs=_ANY,
        out_shape=SDS((N_DEV,) + v.shape, v.dtype), scratch_shapes=_EXCHANGE_SEMS)(v, after)


_HBM = pl.BlockSpec(memory_space=pltpu.HBM)
_SEM = pl.BlockSpec(memory_space=pltpu.SEMAPHORE)
_EFFECT = pltpu.CompilerParams(has_side_effects=pltpu.SideEffectType.DATAFLOW_SIDE_EFFECTING)
_PEER_SEMS = pltpu.SemaphoreType.DMA((N_DEV - 1,))


def _in_hbm(a):
    return pltpu.with_memory_space_constraint(a, pltpu.HBM)


def _remote(src, dst, send_sems, recv_sems, m):
    dev, _ = _peer(m)
    return pltpu.make_async_remote_copy(src, dst, send_sems.at[m - 1], recv_sems.at[m - 1],
                                        device_id=dev, device_id_type=pl.DeviceIdType.MESH)


def _gather_start(lands, after, name):
    n = len(lands)

    def body(*refs):
        land = refs[:n]
        sems = refs[n + 1:3 * n + 1]
        token = refs[-1]
        x, y, c = _me()
        me = 4 * x + 2 * y + c
        for t in range(n):
            for m in range(1, N_DEV):
                _remote(land[t].at[me], land[t].at[me], sems[2 * t], sems[2 * t + 1], m).start()
        token[...] = jnp.zeros_like(token)

    res = pl.pallas_call(
        body, name=name, in_specs=[_HBM] * n + [_ANY],
        out_specs=[_SEM] * (2 * n) + [_HBM] * n + [pl.BlockSpec(memory_space=pltpu.VMEM)],
        out_shape=[_PEER_SEMS] * (2 * n) + [pltpu.HBM(a.shape, a.dtype) for a in lands] + [SDS((8, 128), F32)],
        input_output_aliases={t: 2 * n + t for t in range(n)},
        compiler_params=_EFFECT)(*[_in_hbm(a) for a in lands], after)
    return [(res[2 * t], res[2 * t + 1], res[2 * n + t]) for t in range(n)], res[-1]


def _gather_wait(handle, after, name):
    send_sems, recv_sems, land = handle

    def body(land_ref, ssem, rsem, after_ref, out_ref):
        del after_ref, out_ref
        x, y, c = _me()
        me = 4 * x + 2 * y + c
        for m in range(1, N_DEV):
            _, blk = _peer(m)
            cp = _remote(land_ref.at[me], land_ref.at[blk], ssem, rsem, m)
            cp.wait_send()
            cp.wait_recv()

    return pl.pallas_call(
        body, name=name, in_specs=[_HBM, _SEM, _SEM, _ANY], out_specs=_HBM,
        out_shape=pltpu.HBM(land.shape, land.dtype), input_output_aliases={0: 0},
        compiler_params=_EFFECT)(land, send_sems, recv_sems, after)


FIRST_STAGE = (1, 2, 4, 6)
RELAYED = (2, 4, 6)
OTHER_CORE = 1


def _stage_copy(src, dst, send_sems, recv_sems, k, m):
    dev, _ = _peer(m)
    return pltpu.make_async_remote_copy(src, dst, send_sems.at[k], recv_sems.at[k],
                                        device_id=dev, device_id_type=pl.DeviceIdType.MESH)


def _gather2_start(lands, after, name):
    n = len(lands)

    def body(*refs):
        land = refs[:n]
        sems = refs[n + 1:3 * n + 1]
        token = refs[-1]
        x, y, c = _me()
        me = 4 * x + 2 * y + c
        for t in range(n):
            for k, m in enumerate(FIRST_STAGE):
                _stage_copy(land[t].at[me], land[t].at[me], sems[2 * t], sems[2 * t + 1], k, m).start()
        token[...] = jnp.zeros_like(token)

    stage_sems = pltpu.SemaphoreType.DMA((len(FIRST_STAGE),))
    res = pl.pallas_call(
        body, name=name, in_specs=[_HBM] * n + [_ANY],
        out_specs=[_SEM] * (2 * n) + [_HBM] * n + [pl.BlockSpec(memory_space=pltpu.VMEM)],
        out_shape=[stage_sems] * (2 * n) + [pltpu.HBM(a.shape, a.dtype) for a in lands] + [SDS((8, 128), F32)],
        input_output_aliases={t: 2 * n + t for t in range(n)},
        compiler_params=_EFFECT)(*[_in_hbm(a) for a in lands], after)
    return [(res[2 * t], res[2 * t + 1], res[2 * n + t]) for t in range(n)], res[-1]


def _gather2_relay(handles, after, name):
    n = len(handles)

    def body(*refs):
        land, send1, recv1 = refs[:n], refs[n:2 * n], refs[2 * n:3 * n]
        sems = refs[3 * n + 1:5 * n + 1]
        token = refs[-1]
        x, y, c = _me()
        me = 4 * x + 2 * y + c
        for t in range(n):
            for j, m in enumerate(RELAYED):
                _, blk = _peer(m)
                _stage_copy(land[t].at[me], land[t].at[blk], send1[t], recv1[t], 1 + j, m).wait_recv()
                _stage_copy(land[t].at[blk], land[t].at[blk], sems[2 * t], sems[2 * t + 1], j, OTHER_CORE).start()
        token[...] = jnp.zeros_like(token)

    relay_sems = pltpu.SemaphoreType.DMA((len(RELAYED),))
    lands = [h[2] for h in handles]
    res = pl.pallas_call(
        body, name=name, in_specs=[_HBM] * n + [_SEM] * (2 * n) + [_ANY],
        out_specs=[_SEM] * (2 * n) + [_HBM] * n + [pl.BlockSpec(memory_space=pltpu.VMEM)],
        out_shape=[relay_sems] * (2 * n) + [pltpu.HBM(a.shape, a.dtype) for a in lands] + [SDS((8, 128), F32)],
        input_output_aliases={t: 2 * n + t for t in range(n)},
        compiler_params=_EFFECT)(*lands, *[h[0] for h in handles], *[h[1] for h in handles], after)
    return [(h[0], h[1], res[2 * t], res[2 * t + 1], res[2 * n + t]) for t, h in enumerate(handles)], res[-1]


def _gather2_wait(handle, after, name):
    send1, recv1, send2, recv2, land = handle

    def body(land_ref, s1, r1, s2, r2, after_ref, out_ref):
        del after_ref, out_ref
        x, y, c = _me()
        me = 4 * x + 2 * y + c
        _, other = _peer(OTHER_CORE)
        _stage_copy(land_ref.at[me], land_ref.at[other], s1, r1, 0, OTHER_CORE).wait_recv()
        for k, m in enumerate(FIRST_STAGE):
            _stage_copy(land_ref.at[me], land_ref.at[me], s1, r1, k, m).wait_send()
        for j, m in enumerate(RELAYED):
            _, mine = _peer(m)
            _, theirs = _peer(m ^ OTHER_CORE)
            _stage_copy(land_ref.at[mine], land_ref.at[mine], s2, r2, j, OTHER_CORE).wait_send()
            _stage_copy(land_ref.at[mine], land_ref.at[theirs], s2, r2, j, OTHER_CORE).wait_recv()

    return pl.pallas_call(
        body, name=name, in_specs=[_HBM] + [_SEM] * 4 + [_ANY], out_specs=_HBM,
        out_shape=pltpu.HBM(land.shape, land.dtype), input_output_aliases={0: 0},
        compiler_params=_EFFECT)(land, send1, recv1, send2, recv2, after)


def _exchange_start(ps, name):
    n = len(ps)

    def body(*refs):
        p = refs[:n]
        got = refs[n:2 * n]
        sems = refs[2 * n:5 * n]
        token = refs[-1]
        x, y, c = _me()
        me = 4 * x + 2 * y + c
        for t in range(n):
            pltpu.make_async_copy(p[t].at[me], got[t].at[me], sems[3 * t + 2]).start()
            for m in range(1, N_DEV):
                _, blk = _peer(m)
                _remote(p[t].at[blk], got[t].at[me], sems[3 * t], sems[3 * t + 1], m).start()
        token[...] = jnp.zeros_like(token)

    res = pl.pallas_call(
        body, name=name, in_specs=[_HBM] * (2 * n),
        out_specs=[_SEM] * (3 * n) + [_HBM] * (2 * n) + [pl.BlockSpec(memory_space=pltpu.VMEM)],
        out_shape=[_PEER_SEMS, _PEER_SEMS, pltpu.SemaphoreType.DMA(())] * n
        + [pltpu.HBM(a.shape, a.dtype) for a in ps] * 2 + [SDS((8, 128), F32)],
        input_output_aliases={t: 3 * n + t for t in range(2 * n)},
        compiler_params=_EFFECT)(*[_in_hbm(a) for a in ps], *[_in_hbm(lax.empty(a.shape, a.dtype)) for a in ps])
    return [(res[3 * t], res[3 * t + 1], res[3 * t + 2], res[3 * n + t], res[4 * n + t]) for t in range(n)], res[-1]


def _exchange_wait(handle, after, name):
    send_sems, recv_sems, local_sem, p, got = handle

    def body(p_ref, got_ref, ssem, rsem, lsem, after_ref, p_out, got_out):
        del after_ref, p_out, got_out
        x, y, c = _me()
        me = 4 * x + 2 * y + c
        pltpu.make_async_copy(p_ref.at[me], got_ref.at[me], lsem).wait()
        for m in range(1, N_DEV):
            _, blk = _peer(m)
            cp = _remote(p_ref.at[blk], got_ref.at[blk], ssem, rsem, m)
            cp.wait_send()
            cp.wait_recv()

    return pl.pallas_call(
        body, name=name, in_specs=[_HBM, _HBM, _SEM, _SEM, _SEM, _ANY], out_specs=[_HBM, _HBM],
        out_shape=[pltpu.HBM(p.shape, p.dtype), pltpu.HBM(got.shape, got.dtype)],
        input_output_aliases={0: 0, 1: 1}, compiler_params=_EFFECT)(p, got, send_sems, recv_sems, local_sem, after)[1]


CHIPS = (0, 2, 4, 6)


def _pairs_start(p, name):
    def body(p_ref, pair_ref, ssem, rsem, p_out, pair_out):
        del p_out, pair_out
        for k, chip in enumerate(CHIPS):
            _, blk = _peer(chip ^ OTHER_CORE)
            _stage_copy(p_ref.at[blk], pair_ref.at[k], ssem, rsem, k, OTHER_CORE).start()

    sems = pltpu.SemaphoreType.DMA((len(CHIPS),))
    pair = lax.empty((len(CHIPS),) + p.shape[1:], p.dtype)
    res = pl.pallas_call(
        body, name=name, in_specs=[_HBM] * 2, out_specs=[_SEM] * 2 + [_HBM] * 2,
        out_shape=[sems, sems, pltpu.HBM(p.shape, p.dtype), pltpu.HBM(pair.shape, pair.dtype)],
        input_output_aliases={0: 2, 1: 3}, compiler_params=_EFFECT)(_in_hbm(p), _in_hbm(pair))
    return res


def _pairs_wait(handle, name):
    send_sems, recv_sems, p, pair = handle

    def body(p_ref, pair_ref, ssem, rsem, p_out, pair_out):
        del p_out, pair_out
        for k, chip in enumerate(CHIPS):
            _, blk = _peer(chip ^ OTHER_CORE)
            cp = _stage_copy(p_ref.at[blk], pair_ref.at[k], ssem, rsem, k, OTHER_CORE)
            cp.wait_send()
            cp.wait_recv()

    return pl.pallas_call(
        body, name=name, in_specs=[_HBM, _HBM, _SEM, _SEM], out_specs=[_HBM, _HBM],
        out_shape=[pltpu.HBM(p.shape, p.dtype), pltpu.HBM(pair.shape, pair.dtype)],
        input_output_aliases={0: 0, 1: 1}, compiler_params=_EFFECT)(p, pair, send_sems, recv_sems)


def _sum_pairs(p, pair, me1, name):
    _, r, c = pair.shape
    tr = _row_tile(r)

    def body(me_ref, p_ref, pair_ref, o_ref):
        del me_ref
        o_ref[...] = (p_ref[...].astype(F32) + pair_ref[...].astype(F32)).astype(o_ref.dtype)

    def mine(k, i, me):
        chip = 2 * k
        return (jnp.bitwise_xor(me[0], chip), i, 0)

    assert CHIPS == tuple(2 * k for k in range(len(CHIPS)))
    blk = pl.BlockSpec((None, tr, c), lambda k, i, me: (k, i, 0))
    return pl.pallas_call(
        body, name=name,
        grid_spec=pltpu.PrefetchScalarGridSpec(
            num_scalar_prefetch=1, grid=(len(CHIPS), r // tr),
            in_specs=[pl.BlockSpec((None, tr, c), mine), blk], out_specs=blk),
        out_shape=SDS(pair.shape, pair.dtype), compiler_params=_cp("parallel", "parallel"))(me1, p, pair)


def _chips_start(q, name):
    def body(q_ref, got_ref, ssem, rsem, lsem, q_out, got_out, token):
        del q_out, got_out
        pltpu.make_async_copy(q_ref.at[0], got_ref.at[0], lsem).start()
        for k, chip in enumerate(CHIPS[1:]):
            _stage_copy(q_ref.at[k + 1], got_ref.at[k + 1], ssem, rsem, k, chip).start()
        token[...] = jnp.zeros_like(token)

    sems = pltpu.SemaphoreType.DMA((len(CHIPS) - 1,))
    res = pl.pallas_call(
        body, name=name, in_specs=[_HBM] * 2,
        out_specs=[_SEM] * 3 + [_HBM] * 2 + [pl.BlockSpec(memory_space=pltpu.VMEM)],
        out_shape=[sems, sems, pltpu.SemaphoreType.DMA(()), pltpu.HBM(q.shape, q.dtype), pltpu.HBM(q.shape, q.dtype),
                   SDS((8, 128), F32)],
        input_output_aliases={0: 3, 1: 4}, compiler_params=_EFFECT)(_in_hbm(q), _in_hbm(lax.empty(q.shape, q.dtype)))
    return res[:5], res[5]


def _chips_wait(handle, after, name):
    send_sems, recv_sems, local_sem, q, got = handle

    def body(q_ref, got_ref, ssem, rsem, lsem, after_ref, q_out, got_out):
        del after_ref, q_out, got_out
        pltpu.make_async_copy(q_ref.at[0], got_ref.at[0], lsem).wait()
        for k, chip in enumerate(CHIPS[1:]):
            cp = _stage_copy(q_ref.at[k + 1], got_ref.at[k + 1], ssem, rsem, k, chip)
            cp.wait_send()
            cp.wait_recv()

    return pl.pallas_call(
        body, name=name, in_specs=[_HBM, _HBM, _SEM, _SEM, _SEM, _ANY], out_specs=[_HBM, _HBM],
        out_shape=[pltpu.HBM(q.shape, q.dtype), pltpu.HBM(got.shape, got.dtype)],
        input_output_aliases={0: 0, 1: 1}, compiler_params=_EFFECT)(q, got, send_sems, recv_sems, local_sem, after)[1]


def _cast_into_slot(w, layer, me1, name):
    _, r, c = w.shape
    tr = next(t for t in (512, 352, r) if r % t == 0)

    def body(me_ref, w_ref, o_ref):
        del me_ref
        o_ref[...] = w_ref[...].astype(BF16)

    return pl.pallas_call(
        body, name=name,
        grid_spec=pltpu.PrefetchScalarGridSpec(
            num_scalar_prefetch=1, grid=(r // tr,),
            in_specs=[pl.BlockSpec((None, tr, c), lambda i, me: (layer, i, 0))],
            out_specs=pl.BlockSpec((None, tr, c), lambda i, me: (me[0], i, 0))),
        out_shape=SDS((N_DEV, r, c), BF16), compiler_params=_cp("arbitrary"))(me1, w)


def _cast_all_into_slots(ws, layers, me1, after, name):
    n = len(ws)

    def body(me_ref, *refs):
        del me_ref
        for w_ref, o_ref in zip(refs[:n], refs[n + 1:]):
            o_ref[...] = w_ref[...].astype(BF16)

    return pl.pallas_call(
        body, name=name,
        grid_spec=pltpu.PrefetchScalarGridSpec(
            num_scalar_prefetch=1, grid=(1,),
            in_specs=[pl.BlockSpec((None,) + a.shape[1:], lambda i, me, l=l: (l, 0, 0)) for a, l in zip(ws, layers)]
            + [_ANY],
            out_specs=[pl.BlockSpec((None,) + a.shape[1:], lambda i, me: (me[0], 0, 0)) for a in ws]),
        out_shape=[SDS((N_DEV,) + a.shape[1:], BF16) for a in ws],
        compiler_params=_cp("arbitrary"))(me1, *ws, after)


def _sum8_into_slot(p, me1, name):
    _, r, c = p.shape

    def body(me_ref, p_ref, o_ref):
        del me_ref
        acc = p_ref[0]
        for k in range(1, N_DEV):
            acc = acc + p_ref[k]
        o_ref[...] = acc

    return pl.pallas_call(
        body, name=name,
        grid_spec=pltpu.PrefetchScalarGridSpec(
            num_scalar_prefetch=1, grid=(1,),
            in_specs=[pl.BlockSpec(p.shape, lambda i, me: (0, 0, 0))],
            out_specs=pl.BlockSpec((None, r, c), lambda i, me: (me[0], 0, 0))),
        out_shape=SDS(p.shape, F32), compiler_params=_cp("arbitrary"))(me1, p)


def _adamw(w, g, m, v):
    m = ADAM_B1 * m + (1.0 - ADAM_B1) * g
    v = ADAM_B2 * v + (1.0 - ADAM_B2) * (g * g)
    m_hat = m / (1.0 - ADAM_B1 ** ADAM_STEP)
    v_hat = v / (1.0 - ADAM_B2 ** ADAM_STEP)
    delta = -ADAM_LR * (m_hat / (jnp.sqrt(v_hat) + ADAM_EPS) + ADAM_WD * w)
    return delta, m, v


def _adam_tile(p_ref, w_ref, m_ref, v_ref, g_ref, d_ref, nm_ref, nv_ref):
    g = p_ref[0].astype(F32)
    for k in range(1, p_ref.shape[0]):
        g = g + p_ref[k].astype(F32)
    delta, nm, nv = _adamw(w_ref[...], g, m_ref[...], v_ref[...])
    g_ref[...] = g
    d_ref[...] = delta
    nm_ref[...] = nm
    nv_ref[...] = nv


class _AdamJob:
    def __init__(self, parts, w, m, v, layer, prev):
        self.args = [parts, w, m, v] + list(prev or ())
        self.layer, self.results = layer, None


def _carry(jobs, steps, body, n_in, n_out):
    in_specs, args, out_specs, out_shapes, aliases, n_prevs = [], [], [], [], {}, []
    for j, job in enumerate(jobs):
        _, r, c = job.args[0].shape
        nr = next(n for n in range(steps, 0, -1) if steps % n == 0 and r % (16 * n) == 0)
        nc = steps // nr
        assert c % (128 * nc) == 0
        tile = (r // nr, c // nc)
        blk = pl.BlockSpec((None,) + tile, lambda i, layer=job.layer, nc=nc: (layer, i // nc, i % nc))
        n_prev = len(job.args) - 4
        aliases.update({n_in + len(args) + 4 + k: n_out + 4 * j + k for k in range(n_prev)})
        in_specs += [pl.BlockSpec(job.args[0].shape[:1] + tile, lambda i, nc=nc: (0, i // nc, i % nc)), blk, blk, blk]
        in_specs += [_ANY] * n_prev
        args += job.args
        out_specs += [blk] * 4
        out_shapes += [SDS(job.args[1].shape, F32)] * 4
        n_prevs.append(n_prev)

    def carrying(*refs):
        ins, outs = refs[:n_in + len(args)], refs[n_in + len(args):]
        body(*ins[:n_in], *outs[:n_out])
        k = n_in
        for j, n_prev in enumerate(n_prevs):
            _adam_tile(*ins[k:k + 4], *outs[n_out + 4 * j:n_out + 4 * j + 4])
            k += 4 + n_prev

    return carrying, in_specs, args, out_specs, out_shapes, aliases


def _adam_shard(parts, w, m, v, layer, prev, name):
    n, r, c = parts.shape
    tr = next(t for t in (512, 352, r) if r % t == 0)
    n_prev = 0 if prev is None else 4

    def body(*refs):
        _adam_tile(*refs[:4], *refs[4 + n_prev:])

    blk = pl.BlockSpec((None, tr, c), lambda i: (layer, i, 0))
    return pl.pallas_call(
        body, name=name, grid=(r // tr,),
        in_specs=[pl.BlockSpec((n, tr, c), lambda i: (0, i, 0)), blk, blk, blk] + [_ANY] * n_prev,
        out_specs=[blk] * 4, out_shape=[SDS(w.shape, F32)] * 4,
        input_output_aliases={4 + k: k for k in range(n_prev)},
        compiler_params=_cp("parallel"))(parts, w, m, v, *(prev or ()))


SMALL_MATRICES = [("lru_w_r", 2048), ("lru_w_i", 2048), ("gmlp_w_s", 1024)]
SMALL_VECTORS = [("norm1_g", 8), ("gmlp_ln_g", 8), ("gmlp_ln_b", 8), ("gmlp_b_s", 8), ("conv_w", 32), ("conv_b", 8),
                 ("lru_b_r", 16), ("lru_b_i", 16), ("lru_lambda", 16), ("norm2_g", 8), ("final_g", 8)]
SMALL_VECTOR_ROW0 = sum(n for _, n in SMALL_MATRICES)
SMALL_VECTOR_BLOCK = 256
SMALL_ROWS = SMALL_VECTOR_ROW0 + SMALL_VECTOR_BLOCK


def _pack_small(small):
    parts = [small[k] for k, _ in SMALL_MATRICES]
    parts += [small[k] if k in small else jnp.zeros((n, HD), F32) for k, n in SMALL_VECTORS]
    flat = jnp.concatenate(parts)
    return jnp.pad(flat, ((0, SMALL_ROWS - flat.shape[0]), (0, 0))).reshape(N_DEV, SMALL_ROWS // N_DEV, HD)


def _adam_matrix(g0, g1, w, m, v, row0, name):
    _, rows, _ = w.shape
    tr = 512

    def body(g0_ref, g1_ref, w_ref, m_ref, v_ref, g_ref, d_ref, nm_ref, nv_ref):
        for l, src in enumerate((g0_ref, g1_ref)):
            g = src[...]
            delta, nm, nv = _adamw(w_ref[l], g, m_ref[l], v_ref[l])
            g_ref[l] = g
            d_ref[l] = delta
            nm_ref[l] = nm
            nv_ref[l] = nv

    gspec = pl.BlockSpec((tr, HD), lambda i: (row0 // tr + i, 0))
    blk = pl.BlockSpec((2, tr, HD), lambda i: (0, i, 0))
    return pl.pallas_call(body, name=name, grid=(rows // tr,), in_specs=[gspec, gspec] + [blk] * 3,
                          out_specs=[blk] * 4, out_shape=[SDS(w.shape, F32)] * 4,
                          compiler_params=_cp("parallel"))(g0, g1, w, m, v)


def _adam_vectors(g0, g1, dg1_parts, me1, ws, ms, vs):
    names = [k for k, _ in SMALL_VECTORS]
    n = len(names)

    def lanes(rows8):
        return jnp.concatenate([rows8[k:k + 1, :] for k in range(LANE_ROWS)], axis=1)

    def body(me_ref, g0_ref, g1_ref, dg1_ref, *refs):
        w_refs, m_refs, v_refs = refs[:n], refs[n:2 * n], refs[2 * n:3 * n]
        outs = refs[3 * n:]
        me = me_ref[0]
        g_refs = (g0_ref, g1_ref)

        def emit(i, idx, g):
            delta, nm, nv = _adamw(w_refs[i][idx], g, m_refs[i][idx], v_refs[i][idx])
            for j, val in enumerate((g, delta, nm, nv)):
                outs[4 * i + j][idx] = val

        off = 0
        for i, (name, rows) in enumerate(SMALL_VECTORS):
            for l in range(2):
                row = (slice(l, l + 1), slice(None))
                if name == "final_g":
                    if l == 1:
                        emit(i, (slice(0, 1), slice(None)), lanes(g1_ref[off:off + rows, :]))
                elif name == "norm1_g":
                    if l == 1:
                        emit(i, row, lanes(g0_ref[off:off + rows, :]))
                    else:
                        total = dg1_ref[0]
                        for k in range(1, N_DEV):
                            total = total + dg1_ref[k]
                        emit(i, row, lanes(total))
                elif name == "gmlp_b_s":
                    emit(i, (l,), g_refs[l][off:off + rows, :])
                elif rows == LANE_ROWS:
                    emit(i, row, lanes(g_refs[l][off:off + rows, :]))
                else:
                    for r in range(rows // LANE_ROWS):
                        emit(i, (l, slice(r, r + 1), slice(None)), g_refs[l][pl.ds(off + r * LANE_ROWS + me, 1), :])
            off += rows

    args = [ws[k] for k in names] + [ms[k] for k in names] + [vs[k] for k in names]
    gspec = pl.BlockSpec((SMALL_VECTOR_BLOCK, HD), lambda i, me: (SMALL_VECTOR_ROW0 // SMALL_VECTOR_BLOCK, 0))
    res = pl.pallas_call(
        body, name="adam_vectors",
        grid_spec=pltpu.PrefetchScalarGridSpec(
            num_scalar_prefetch=1, grid=(1,),
            in_specs=[gspec, gspec, _full(dg1_parts.shape)] + [_full(a.shape) for a in args],
            out_specs=[_full(ws[k].shape) for k in names for _ in range(4)]),
        out_shape=[SDS(ws[k].shape, F32) for k in names for _ in range(4)],
        compiler_params=_cp("arbitrary"))(me1, g0, g1, dg1_parts, *args)
    return {k: list(res[4 * i:4 * i + 4]) for i, k in enumerate(names)}


def _after(a, *tokens):
    for token in tokens:
        if token is not None:
            a = a + token[0:1, 0:1]
    return a


def _local_step(x, tgt, p, get_w, hook=lambda stage, layer, payload: None):
    s = x.shape[0]
    tm = _row_tile(s)
    wsb = p["gmlp_w_s"].astype(BF16)
    wstb = jnp.swapaxes(p["gmlp_w_s"], -1, -2).astype(BF16)
    bsb = jnp.broadcast_to(p["gmlp_b_s"][..., None], p["gmlp_w_s"].shape)
    wrb = p["lru_w_r"].astype(BF16)
    wib = p["lru_w_i"].astype(BF16)
    saved = []
    for l in range(2):
        win = get_w("w_in", l, x)
        z, h1 = _norm_inproj(x, _after(p["norm1_g"][l][None], hook("pre_inproj", l, win)), win, l, tm)
        a0, b0, a1, b1, xcb, *gates = _lru_gates_fwd(z, p["conv_w"][l], p["conv_b"][l][None], wrb[l], wib[l],
                                                     p["lru_b_r"][l], p["lru_b_i"][l], p["lru_lambda"][l], l, tm)
        h0, hr = _lru_scan(a0, b0, a1, b1, False, l)
        lng = _after(p["gmlp_ln_g"][l][None], hook("pre_gmlp", l, h0))
        wout = get_w("w_out", l, lng)
        x1, mg = _mixer_fwd(x, h0, hr, z, lng, p["gmlp_ln_b"][l][None], wsb[l], bsb[l], wout, l, tm)
        wfi = get_w("w_ffn_in", l, x1)
        wfo = get_w("w_ffn_out", l, x1)
        if l == 0:
            x2, ff, dff, h2 = _ffn_fwd(x1, p["norm2_g"][l][None], wfi, wfo, l, tm)
        else:
            dx, loss, dfg, ff, dff, h2 = _ffn_fwd(x1, p["norm2_g"][l][None], wfi, wfo, l, tm,
                                                  head=(p["final_g"][None], tgt))
        saved.append((x, z, h1, a0, a1, h0, hr, x1, mg, ff, dff, h2, win, wout, wfi, wfo, xcb, gates))
        x = x2
    pending = None
    for l in (1, 0):
        x0, z, h1, a0, a1, h0, hr, x1, mg, ff, dff, h2, win, wout, wfi, wfo, xcb, gates = saved[l]
        dgu, dx1, dg2 = _ffn_bwd(dx, wfo, dff, wfi.reshape(N_DEV, FF_BLK, D), x1, p["norm2_g"][l][None], l, tm)
        d_wfo = _mm_tn(ff, pl.BlockSpec((None, s, FF_BLK), lambda j: (j, 0, 0)), dx, _resident((s, D)),
                       4, (4, FF_BLK, D), pl.BlockSpec((None, FF_BLK, D), lambda j: (j, 0, 0)),
                       f"dw_ffn_out_l{l}", a_is_transposed=False)
        dgu8 = dgu.reshape(N_DEV, s, FF_BLK)
        d_wfi = _mm_tn(dgu8, pl.BlockSpec((None, s, FF_BLK), lambda j: (j, 0, 0)), h2, _resident((s, D)),
                       N_DEV, (N_DEV, FF_BLK, D), pl.BlockSpec((None, FF_BLK, D), lambda j: (j, 0, 0)),
                       f"dw_ffn_in_l{l}", a_is_transposed=False)
        d_wout = _mm_tn(mg, _resident((D, s)), dx1, pl.BlockSpec((s, D // 2), lambda j: (0, j)),
                        2, (D, D), pl.BlockSpec((D, D // 2), lambda j: (0, j)), f"dw_out_l{l}")
        token = hook("ffn_partials", l, dict(w_ffn_out=d_wfo.reshape(N_DEV, D_FF // N_DEV, D), w_ffn_in=d_wfi,
                                             w_out=d_wout.reshape(N_DEV, D // N_DEV, D)))
        pending = hook("mid_backward", l, dx1)
        dz, dh, dws, dbs, dlng, dlnb = _mixer_bwd(dx1, wout, h0, hr, z, _after(p["gmlp_ln_g"][l][None], token),
                                                  p["gmlp_ln_b"][l][None], wsb[l], wstb[l], bsb[l], l, tm)
        g1, g0 = _lru_scan(a1, dh, a0, dh, True, l)
        dxc, dwr, dwi, dbr, dbi, dlam = _lru_gates_bwd(
            xcb, gates, h0, hr, g0, g1, wrb[l], wib[l], _after(p["lru_lambda"][l], pending), l, tm)
        dz, dcw, dcb = _conv_bwd(dz, dxc, z, p["conv_w"][l], l, tm)
        small = dict(lru_w_r=dwr.reshape(-1, HD), lru_w_i=dwi.reshape(-1, HD), gmlp_w_s=dws.reshape(-1, HD),
                     gmlp_ln_g=dlng, gmlp_ln_b=dlnb, gmlp_b_s=dbs, conv_w=dcw, conv_b=dcb, lru_b_r=dbr,
                     lru_b_i=dbi, lru_lambda=dlam, norm2_g=dg2)
        if l == 1:
            small["final_g"] = dfg
        else:
            small["norm1_g"] = dg1
        started = hook("small_grads", l, small)
        d_win = _mm_tn(h1, _resident((D, s)), dz, pl.BlockSpec((s, IN_BLK), lambda j: (0, j)),
                       N_DEV, (N_DEV, D, IN_BLK), pl.BlockSpec((None, D, IN_BLK), lambda j: (j, 0, 0)),
                       f"dw_in_l{l}", after=started, jobs=hook("dw_in", l, started) or ())
        token = hook("mixer_partials", l, dict(w_in=d_win))
        dx, dg1 = _mm_nt_rms_bwd(
            dz, pl.BlockSpec((tm, N_IN), lambda i: (i, 0)),
            lambda r: [r[:, k * IN_BLK:(k + 1) * IN_BLK] for k in range(N_DEV)],
            win, False, x0, _after(p["norm1_g"][l][None], token, started, pending), dx1, f"inproj_bwd_dx_l{l}", tm,
            jobs=hook("inproj_bwd_dx", l, token) or ())
        pending = None
    return loss, dx, dg1


_REPL = ["norm1_g", "gmlp_ln_g", "gmlp_ln_b", "gmlp_w_s", "gmlp_b_s", "conv_b", "lru_w_r", "lru_w_i", "norm2_g", "final_g"]
_LANE_SHARDED = ["conv_w", "lru_b_r", "lru_b_i", "lru_lambda"]
_BIG = ["w_in", "w_out", "w_ffn_in", "w_ffn_out"]
_ORDER = ["norm1_g", "w_in", "gmlp_ln_g", "gmlp_ln_b", "gmlp_w_s", "gmlp_b_s", "conv_w", "conv_b", "lru_w_r", "lru_b_r",
          "lru_w_i", "lru_b_i", "lru_lambda", "w_out", "norm2_g", "w_ffn_in", "w_ffn_out", "final_g"]


def kernel(x, norm1_g, w_in, gmlp_ln_g, gmlp_ln_b, gmlp_w_s, gmlp_b_s, conv_w, conv_b, lru_w_r, lru_b_r, lru_w_i, lru_b_i, lru_lambda, w_out, norm2_g, w_ffn_in, w_ffn_out, final_g, loss_target, m_norm1_g, m_w_in, m_gmlp_ln_g, m_gmlp_ln_b, m_gmlp_w_s, m_gmlp_b_s, m_conv_w, m_conv_b, m_lru_w_r, m_lru_b_r, m_lru_w_i, m_lru_b_i, m_lru_lambda, m_w_out, m_norm2_g, m_w_ffn_in, m_w_ffn_out, m_final_g, v_norm1_g, v_w_in, v_gmlp_ln_g, v_gmlp_ln_b, v_gmlp_w_s, v_gmlp_b_s, v_conv_w, v_conv_b, v_lru_w_r, v_lru_b_r, v_lru_w_i, v_lru_b_i, v_lru_lambda, v_w_out, v_norm2_g, v_w_ffn_in, v_w_ffn_out, v_final_g):
    w = dict(norm1_g=norm1_g, w_in=w_in, gmlp_ln_g=gmlp_ln_g, gmlp_ln_b=gmlp_ln_b, gmlp_w_s=gmlp_w_s, gmlp_b_s=gmlp_b_s,
             conv_w=conv_w, conv_b=conv_b, lru_w_r=lru_w_r, lru_b_r=lru_b_r, lru_w_i=lru_w_i, lru_b_i=lru_b_i,
             lru_lambda=lru_lambda, w_out=w_out, norm2_g=norm2_g, w_ffn_in=w_ffn_in, w_ffn_out=w_ffn_out, final_g=final_g)
    mom = dict(norm1_g=m_norm1_g, w_in=m_w_in, gmlp_ln_g=m_gmlp_ln_g, gmlp_ln_b=m_gmlp_ln_b, gmlp_w_s=m_gmlp_w_s,
               gmlp_b_s=m_gmlp_b_s, conv_w=m_conv_w, conv_b=m_conv_b, lru_w_r=m_lru_w_r, lru_b_r=m_lru_b_r,
               lru_w_i=m_lru_w_i, lru_b_i=m_lru_b_i, lru_lambda=m_lru_lambda, w_out=m_w_out, norm2_g=m_norm2_g,
               w_ffn_in=m_w_ffn_in, w_ffn_out=m_w_ffn_out, final_g=m_final_g)
    var = dict(norm1_g=v_norm1_g, w_in=v_w_in, gmlp_ln_g=v_gmlp_ln_g, gmlp_ln_b=v_gmlp_ln_b, gmlp_w_s=v_gmlp_w_s,
               gmlp_b_s=v_gmlp_b_s, conv_w=v_conv_w, conv_b=v_conv_b, lru_w_r=v_lru_w_r, lru_b_r=v_lru_b_r,
               lru_w_i=v_lru_w_i, lru_b_i=v_lru_b_i, lru_lambda=v_lru_lambda, w_out=v_w_out, norm2_g=v_norm2_g,
               w_ffn_in=v_w_ffn_in, w_ffn_out=v_w_ffn_out, final_g=v_final_g)
    for src in (w, mom, var):
        src["w_ffn_in"] = jnp.swapaxes(src["w_ffn_in"], 1, 2)
    xi, yi, ci = _me()
    me = 4 * xi + 2 * yi + ci

    lane_shapes = [w[k].shape for k in _LANE_SHARDED]
    lane_rows = sum(a[0] * a[1] for a in lane_shapes)
    packed = jnp.concatenate([w[k].reshape(-1, HD) for k in _LANE_SHARDED])
    packed = jnp.pad(packed, ((0, -lane_rows % 8), (0, 0)))

    me1 = jnp.reshape(me, (1,)).astype(jnp.int32)
    gathers = {}
    exchanges = {}
    views = dict(w_in=(N_DEV, D, IN_BLK), w_out=(D, D), w_ffn_in=(2, 4, FF_BLK, D), w_ffn_out=(4, FF_BLK, D))
    small_ex = {}
    small_ag = {}

    casts = {}

    def start_gather(names, l, after):
        lands = [casts[(k, l)] if (k, l) in casts else _cast_into_slot(w[k], l, me1, f"cast_{k}_l{l}") for k in names]
        started, tok = _gather2_start(lands, after, f"gather_start_{'_'.join(names)}_l{l}")
        gathers.update({(k, l): h for k, h in zip(names, started)})
        return tok

    def relay_gather(names, l, after):
        relayed, tok = _gather2_relay([gathers[(k, l)] for k in names], after, f"gather_relay_{'_'.join(names)}_l{l}")
        gathers.update({(k, l): h for k, h in zip(names, relayed)})
        return tok

    def get_w(k, l, after):
        if (k, l) == ("w_in", 1):
            after = relay_gather(_BIG[:1], l, after)
        return _gather2_wait(gathers[(k, l)], after, f"gather_wait_{k}_l{l}").reshape(views[k])

    carried = {("inproj_bwd_dx", 1): [("w_ffn_out", 1), ("w_ffn_in", 1), ("w_out", 1)], ("dw_in", 0): [("w_in", 1)],
               ("inproj_bwd_dx", 0): [("w_ffn_out", 0), ("w_ffn_in", 0), ("w_out", 0)]}
    adam = {}

    def adam_jobs(shards, after):
        for k, l in shards:
            got = _exchange_wait(exchanges[(k, l)], after, f"exchange_wait_{k}_l{l}")
            adam[k] = _AdamJob(got, w[k], mom[k], var[k], l, adam[k].results if k in adam else None)
        return [adam[k] for k, _ in shards]

    def hook(stage, l, payload):
        if stage in ("dw_in", "inproj_bwd_dx"):
            return adam_jobs(carried.get((stage, l), []), payload)
        if stage == "pre_inproj":
            return start_gather(_BIG[1:], l, payload)
        if stage == "pre_gmlp":
            tok = relay_gather(_BIG[1:], l, payload)
            return tok + start_gather(_BIG[:1], l + 1, tok) if l == 0 else tok
        if stage == "small_grads":
            (small_ex[l],), tok = _exchange_start([_pack_small(payload)], f"exchange_start_small_l{l}")
            return tok
        if stage == "mid_backward":
            return reduce_small(l + 1, payload) if l == 0 else None
        if (stage, l) == ("mixer_partials", 0):
            _, pair = _pairs_wait(_pairs_start(payload["w_in"], "pairs_start_w_in_l0"), "pairs_wait_w_in_l0")
            sums = _sum_pairs(payload["w_in"], pair, me1, "sum_pairs_w_in_l0")
            exchanges[("w_in", 0)], tok = _chips_start(sums, "chips_start_w_in_l0")
            return tok + reduce_small(0, tok)
        started, tok = _exchange_start(list(payload.values()), f"exchange_start_{'_'.join(payload)}_l{l}")
        exchanges.update({(k, l): h for k, h in zip(payload, started)})
        return tok

    def reduce_small(l, after):
        got = _exchange_wait(small_ex[l], after, f"exchange_wait_small_l{l}")
        mine = _sum8_into_slot(got, me1, f"sum_small_l{l}")
        (small_ag[l],), tok = _gather_start([mine], got, f"gather_start_small_l{l}")
        return tok

    land = lax.dynamic_update_slice(jnp.zeros((N_DEV,) + packed.shape, F32), packed[None], (me, 0, 0))
    (lanes_handle,), token = _gather_start([land], packed, "gather_start_lanes")
    token = start_gather(_BIG[:1], 0, token)
    later = [(k, l) for l in range(2) for k in _BIG if (k, l) != ("w_in", 0)]
    casts.update(zip(later, _cast_all_into_slots([w[k] for k, _ in later], [l for _, l in later], me1, token,
                                                 "cast_later_weights")))
    token = relay_gather(_BIG[:1], 0, casts[later[0]])
    lanes = _gather_wait(lanes_handle, token, "gather_wait_lanes")
    params = {k: w[k] for k in _REPL}
    off = 0
    for k, shp in zip(_LANE_SHARDED, lane_shapes):
        n = shp[0] * shp[1]
        params[k] = jnp.swapaxes(lanes[:, off:off + n], 0, 1).reshape(shp[0], shp[1], D)
        off += n
    loss, dx, dg1 = _local_step(x[0], loss_target[0], params, get_w, hook)

    out = {k: job.results for k, job in adam.items()}
    after = dx
    g_small = [_gather_wait(small_ag[l], after, f"gather_wait_small_l{l}").reshape(SMALL_ROWS, HD) for l in (0, 1)]
    row0 = 0
    for k, rows in SMALL_MATRICES:
        res = _adam_matrix(*g_small, *[src[k].reshape(2, rows, HD) for src in (w, mom, var)], row0, f"adam_{k}")
        out[k] = [a.reshape(w[k].shape) for a in res]
        after = res[3]
        row0 += rows
    got = _chips_wait(exchanges[("w_in", 0)], after, "chips_wait_w_in_l0")
    out["w_in"] = _adam_shard(got, w["w_in"], mom["w_in"], var["w_in"], 0, out["w_in"], "adam_w_in_l0")
    out["w_ffn_in"] = [jnp.swapaxes(a, 1, 2) for a in out["w_ffn_in"]]
    as_rows = lambda a: a.reshape(1, D) if a.ndim == 1 else a
    vec = _adam_vectors(*g_small, _all_gather(dg1, out["w_in"][3], "gather_norm1_grad"), me1,
                        *[{k: as_rows(src[k]) for k, _ in SMALL_VECTORS} for src in (w, mom, var)])
    out.update({k: [a.reshape(w[k].shape) for a in res] for k, res in vec.items()})

    loss = lax.psum(loss[0, 0], MESH_AXES)
    return (loss, dx[None], *[out[k][0] for k in _ORDER], *[out[k][1] for k in _ORDER],
            *[out[k][2] for k in _ORDER], *[out[k][3] for k in _ORDER])
```

```python
import jax
import jax.numpy as jnp
from jax import lax
from jax.experimental import pallas as pl
from jax.experimental.pallas import tpu as pltpu

F32 = jnp.float32
BF16 = jnp.bfloat16
SDS = jax.ShapeDtypeStruct

D = 1024
N_IN = 6 * D
D_FF = 2816
N_DEV = 8
IN_BLK = N_IN // N_DEV
FF_BLK = 2 * D_FF // N_DEV
HEADS = 8
HD = 128
EPS = 1e-6
LRU_C = 8.0
MESH_AXES = ("x", "y", "c")

ADAM_LR = 0.001
ADAM_B1 = 0.9
ADAM_B2 = 0.999
ADAM_EPS = 1e-08
ADAM_WD = 0.01
ADAM_STEP = 10

VMEM_LIMIT = 60 * 2**20


def _cp(*sem, **kw):
    return pltpu.CompilerParams(dimension_semantics=sem, vmem_limit_bytes=VMEM_LIMIT, **kw)


def _row_tile(s):
    return 512 if s >= 1024 else s // 2


_GELU_C = 0.7978845608028654


def _gelu(x):
    t = jnp.tanh(_GELU_C * (x + 0.044715 * (x * x * x)))
    return 0.5 * x * (1.0 + t), t


def _gelu_grad(x, t):
    return 0.5 * (1.0 + t) + 0.5 * x * (1.0 - t * t) * (_GELU_C * (1.0 + 0.134145 * (x * x)))


def _sigmoid(x):
    return 0.5 + 0.5 * jnp.tanh(0.5 * x)


def _softplus(x):
    e = jnp.exp(-jnp.abs(x))
    w = 1.0 + e
    l1p = jnp.where(w == 1.0, e, jnp.log(w) * e / jnp.where(w == 1.0, 1.0, w - 1.0))
    return jnp.maximum(x, 0.0) + l1p


def _rms_fwd(x, g):
    r = lax.rsqrt(jnp.mean(x * x, axis=-1, keepdims=True) + EPS)
    return x * r * g


def _rms_bwd(x, g, dh):
    r = lax.rsqrt(jnp.mean(x * x, axis=-1, keepdims=True) + EPS)
    xh = x * r
    dxh = dh * g
    dx = r * (dxh - xh * jnp.mean(dxh * xh, axis=-1, keepdims=True))
    dg = jnp.sum(dh * xh, axis=0, keepdims=True)
    return dx, dg


LANE_ROWS = D // HD


def _add_rows128(ref, vec, row0=0):
    for i in range(vec.shape[0]):
        for k in range(LANE_ROWS):
            j = row0 + i * LANE_ROWS + k
            ref[j:j + 1, :] += vec[i:i + 1, k * HD:(k + 1) * HD]


def _dot(a, b):
    return jnp.dot(a, b, preferred_element_type=F32)


def _dot_nt(a, b):
    return lax.dot_general(a, b, (((1,), (1,)), ((), ())), preferred_element_type=F32)


def _dot_tn(a, b):
    return lax.dot_general(a, b, (((0,), (0,)), ((), ())), preferred_element_type=F32)


def _taps(prev, cur, nxt, tm):
    hr = prev.shape[0]
    ext = jnp.concatenate([prev, cur, nxt], axis=0)
    n = tm + 2 * hr
    sl = slice(hr, hr + tm)
    return (pltpu.roll(ext, 2, 0)[sl], pltpu.roll(ext, 1, 0)[sl], cur,
            pltpu.roll(ext, n - 1, 0)[sl], pltpu.roll(ext, n - 2, 0)[sl])


def _halo_specs(tm, s, col, rows=8):
    nb = s // rows
    r = tm // rows
    return (pl.BlockSpec((rows, D), lambda i: (jnp.maximum(i * r - 1, 0), col)),
            pl.BlockSpec((tm, D), lambda i: (i, col)),
            pl.BlockSpec((rows, D), lambda i: (jnp.minimum((i + 1) * r, nb - 1), col)))


def _halo_load(prev_ref, cur_ref, next_ref, fp, fn):
    return prev_ref[...].astype(F32) * fp, cur_ref[...].astype(F32), next_ref[...].astype(F32) * fn


def _halo_flags(nt):
    i = pl.program_id(0)
    return (i > 0).astype(F32), (i < nt - 1).astype(F32)


def _full(shape):
    nd = len(shape)
    return pl.BlockSpec(shape, lambda *_: (0,) * nd)


def _resident(shape):
    nd = len(shape)
    return pl.BlockSpec(shape, lambda *_: (0,) * nd, pipeline_mode=pl.Buffered(1))


def _norm_inproj(x, g, w, layer, tm):
    s = x.shape[0]

    def body(x_ref, g_ref, w_ref, z_ref, ht_ref):
        h32 = _rms_fwd(x_ref[...], g_ref[...])
        ht_ref[...] = h32.T.astype(BF16)
        h = h32.astype(BF16)
        for j in range(N_DEV):
            z_ref[:, j * IN_BLK:(j + 1) * IN_BLK] = _dot(h, w_ref[j]).astype(BF16)

    return pl.pallas_call(
        body, name=f"norm_inproj_l{layer}", grid=(s // tm,),
        in_specs=[pl.BlockSpec((tm, D), lambda i: (i, 0)), _full((1, D)), _resident((N_DEV, D, IN_BLK))],
        out_specs=[pl.BlockSpec((tm, N_IN), lambda i: (i, 0)), pl.BlockSpec((D, tm), lambda i: (0, i))],
        out_shape=[SDS((s, N_IN), BF16), SDS((D, s), BF16)],
        compiler_params=_cp("parallel"))(x, g, w)


def _gmlp_values(zu_ref, zv_ref, lng_ref, lnb_ref):
    zu = zu_ref[...].astype(F32)
    zv = zv_ref[...].astype(F32)
    u, tu = _gelu(zu)
    gv, tv = _gelu(zv)
    xc = gv - jnp.mean(gv, axis=-1, keepdims=True)
    rstd = lax.rsqrt(jnp.mean(xc * xc, axis=-1, keepdims=True) + EPS)
    xh = xc * rstd
    vb = (xh * lng_ref[...] + lnb_ref[...]).astype(BF16)
    return zu, zv, u, tu, tv, xh, rstd, vb


def _mixer_fwd(x, h0, h1, z, lng, lnb, ws, bsb, wo, layer, tm):
    s = x.shape[0]

    def body(x_ref, h0_ref, h1_ref, zu_ref, zv_ref, zg_ref, za_ref, zb_ref, lng_ref, lnb_ref, ws_ref, bsb_ref,
             wo_ref, x1_ref, mg_ref, ya_s):
        _, _, u, _, _, _, _, vb = _gmlp_values(zu_ref, zv_ref, lng_ref, lnb_ref)
        for c in range(tm // HD):
            rs = slice(c * HD, (c + 1) * HD)
            for g in range(HEADS):
                cs = slice(g * HD, (g + 1) * HD)
                ya_s[rs, cs] = u[rs, cs] * (_dot(ws_ref[g], vb[rs, cs]) + bsb_ref[g])
        gg, _ = _gelu(zg_ref[...].astype(F32))
        yb = (h0_ref[...] + h1_ref[...]) * gg
        m32 = _sigmoid(za_ref[...].astype(F32)) * ya_s[...] + _sigmoid(zb_ref[...].astype(F32)) * yb
        mg_ref[...] = m32.T.astype(BF16)
        x1_ref[...] = x_ref[...] + _dot(m32.astype(BF16), wo_ref[...])

    tile = pl.BlockSpec((tm, D), lambda i: (i, 0))
    wspec = _full((HEADS, HD, HD))
    return pl.pallas_call(
        body, name=f"mixer_fwd_l{layer}", grid=(s // tm,),
        in_specs=[tile, tile, tile] + [pl.BlockSpec((tm, D), lambda i, c=c: (i, c)) for c in (0, 1, 3, 4, 5)]
        + [_full((1, D)), _full((1, D)), wspec, wspec, _full((D, D))],
        out_specs=[tile, pl.BlockSpec((D, tm), lambda i: (0, i))], out_shape=[SDS((s, D), F32), SDS((D, s), BF16)],
        scratch_shapes=[pltpu.VMEM((tm, D), F32)],
        compiler_params=_cp("parallel"))(x, h0, h1, z, z, z, z, z, lng, lnb, ws, bsb, wo)


def _conv(taps, cw_ref, cb_ref):
    _, m1, c0, p1, p2 = taps
    return cb_ref[...] + m1 * cw_ref[0:1, :] + c0 * cw_ref[1:2, :] + p1 * cw_ref[2:3, :] + p2 * cw_ref[3:4, :]


def _heads_dot(xb, w_ref, d):
    return jnp.concatenate([_dot(xb[:, h * HD:(h + 1) * HD], w_ref[d, h]) for h in range(HEADS)], axis=1)


def _lru_decay(r, sp):
    la = (-LRU_C) * r * sp
    a = jnp.exp(la)
    return a, jnp.tanh(-la) * (a * a + 1.0)


def _lru_gates_fwd(z, cw, cb, wr, wi, br, bi, lam, layer, tm):
    s = z.shape[0]
    nt = s // tm

    def body(zp_ref, zc_ref, zn_ref, cw_ref, cb_ref, wr_ref, wi_ref, br_ref, bi_ref, lam_ref,
             a0_ref, b0_ref, a1_ref, b1_ref, xc_ref, r0_ref, i0_ref, r1_ref, i1_ref):
        fp, fn = _halo_flags(nt)
        xc = _conv(_taps(*_halo_load(zp_ref, zc_ref, zn_ref, fp, fn), tm), cw_ref, cb_ref)
        xb = xc.astype(BF16)
        xc_ref[...] = xb
        for d, (a_ref, b_ref, r_ref, i_ref) in enumerate(((a0_ref, b0_ref, r0_ref, i0_ref),
                                                          (a1_ref, b1_ref, r1_ref, i1_ref))):
            r = _sigmoid(_heads_dot(xb, wr_ref, d) + br_ref[d:d + 1, :])
            ig = _sigmoid(_heads_dot(xb, wi_ref, d) + bi_ref[d:d + 1, :])
            a, q = _lru_decay(r, _softplus(-lam_ref[d:d + 1, :]))
            a_ref[...] = a
            b_ref[...] = jnp.sqrt(q) * (ig * xc)
            r_ref[...] = r.astype(BF16)
            i_ref[...] = ig.astype(BF16)

    tile = pl.BlockSpec((tm, D), lambda i: (i, 0))
    return pl.pallas_call(
        body, name=f"lru_gates_fwd_l{layer}", grid=(nt,),
        in_specs=[*_halo_specs(tm, s, 2, 16), _full((4, D)), _full((1, D)),
                  _full((2, HEADS, HD, HD)), _full((2, HEADS, HD, HD)), _full((2, D)), _full((2, D)), _full((2, D))],
        out_specs=[tile] * 9, out_shape=[SDS((s, D), F32)] * 4 + [SDS((s, D), BF16)] * 5,
        compiler_params=_cp("parallel"))(z, z, z, cw, cb, wr, wi, br, bi, lam)


def _scan_group(a, x, c, reverse, bwd):
    row = lax.broadcasted_iota(jnp.int32, a.shape, 0)
    b = a * x if bwd else x
    for d in (1, 2, 4):
        keep = (row < 8 - d) if reverse else (row >= d)
        sh = 8 - d if reverse else d
        a_s = jnp.where(keep, pltpu.roll(a, sh, 0), 1.0)
        b_s = jnp.where(keep, pltpu.roll(b, sh, 0), 0.0)
        b = a * b_s + b
        a = a * a_s
    h = b + a * c
    new_c = h[0:1, :] if reverse else h[7:8, :]
    if not bwd:
        return h, new_c
    if reverse:
        prev = jnp.where(row < 7, pltpu.roll(h, 7, 0), c)
    else:
        prev = jnp.where(row >= 1, pltpu.roll(h, 1, 0), c)
    return x + prev, new_c


def _lru_scan(a_f, x_f, a_r, x_r, bwd, layer):
    s = a_f.shape[0]
    ts = min(1024, s // 2)
    cb = 512
    nt = s // ts
    ng = ts // 8

    def body(af_ref, xf_ref, ar_ref, xr_ref, of_ref, or_ref, cf, cr):
        @pl.when(pl.program_id(1) == 0)
        def _():
            cf[...] = jnp.zeros_like(cf)
            cr[...] = jnp.zeros_like(cr)

        def step(j, carry):
            c_f, c_r = carry
            rf = pl.multiple_of(j * 8, 8)
            rr = pl.multiple_of((ng - 1 - j) * 8, 8)
            o, c_f = _scan_group(af_ref[pl.ds(rf, 8), :], xf_ref[pl.ds(rf, 8), :], c_f, False, bwd)
            of_ref[pl.ds(rf, 8), :] = o
            o, c_r = _scan_group(ar_ref[pl.ds(rr, 8), :], xr_ref[pl.ds(rr, 8), :], c_r, True, bwd)
            or_ref[pl.ds(rr, 8), :] = o
            return c_f, c_r

        c_f, c_r = lax.fori_loop(0, ng, step, (cf[0:1, :], cr[0:1, :]), unroll=2)
        cf[...] = jnp.broadcast_to(c_f, cf.shape)
        cr[...] = jnp.broadcast_to(c_r, cr.shape)

    fwd = pl.BlockSpec((ts, cb), lambda c, t: (t, c))
    rev = pl.BlockSpec((ts, cb), lambda c, t: (nt - 1 - t, c))
    return pl.pallas_call(
        body, name=f"lru_scan_{'bwd' if bwd else 'fwd'}_l{layer}", grid=(D // cb, nt),
        in_specs=[fwd, fwd, rev, rev], out_specs=[fwd, rev],
        out_shape=[SDS((s, D), F32)] * 2,
        scratch_shapes=[pltpu.VMEM((8, cb), F32), pltpu.VMEM((8, cb), F32)],
        compiler_params=_cp("parallel", "arbitrary"))(a_f, x_f, a_r, x_r)


def _ffn_fwd(x1, g, wfi, wfo, layer, tm, head=None):
    s = x1.shape[0]

    def ffn(x_ref, g_ref, wi_ref, wo_ref, ff_ref, dff_ref, h_ref):
        x = x_ref[...]
        h = _rms_fwd(x, g_ref[...]).astype(BF16)
        h_ref[...] = h
        acc = x
        for k in range(4):
            gate = _dot_nt(h, wi_ref[0, k])
            up = _dot_nt(h, wi_ref[1, k])
            sg = _sigmoid(gate)
            silu = gate * sg
            ff = (silu * up).astype(BF16)
            ff_ref[k] = ff
            dff_ref[0, k] = (up * (sg * (1.0 + gate * (1.0 - sg)))).astype(BF16)
            dff_ref[1, k] = silu.astype(BF16)
            acc = acc + _dot(ff, wo_ref[k])
        return acc

    def body(x_ref, g_ref, wi_ref, wo_ref, x2_ref, ff_ref, dff_ref, h_ref):
        x2_ref[...] = ffn(x_ref, g_ref, wi_ref, wo_ref, ff_ref, dff_ref, h_ref)

    def body_with_head(x_ref, g_ref, wi_ref, wo_ref, fg_ref, t_ref, dx_ref, loss_ref, dfg_ref, ff_ref, dff_ref, h_ref):
        @pl.when(pl.program_id(0) == 0)
        def _():
            loss_ref[...] = jnp.zeros_like(loss_ref)
            dfg_ref[...] = jnp.zeros_like(dfg_ref)

        x2 = ffn(x_ref, g_ref, wi_ref, wo_ref, ff_ref, dff_ref, h_ref)
        fg = fg_ref[...]
        e = _rms_fwd(x2, fg) - t_ref[...]
        rows = jnp.sum(e * e, axis=-1, keepdims=True)
        loss_ref[...] += (0.5 / D) * jnp.sum(rows, axis=0, keepdims=True)
        dx, dg = _rms_bwd(x2, fg, e * (1.0 / D))
        dx_ref[...] = dx
        _add_rows128(dfg_ref, dg)

    tile = pl.BlockSpec((tm, D), lambda i: (i, 0))
    weights = [_resident((2, 4, FF_BLK, D)), _resident((4, FF_BLK, D))]
    kept_specs = [pl.BlockSpec((4, tm, FF_BLK), lambda i: (0, i, 0)),
                  pl.BlockSpec((2, 4, tm, FF_BLK), lambda i: (0, 0, i, 0)), tile]
    kept_shapes = [SDS((4, s, FF_BLK), BF16), SDS((2, 4, s, FF_BLK), BF16), SDS((s, D), BF16)]
    if head is None:
        return pl.pallas_call(
            body, name=f"ffn_fwd_l{layer}", grid=(s // tm,),
            in_specs=[tile, _full((1, D))] + weights, out_specs=[tile] + kept_specs,
            out_shape=[SDS((s, D), F32)] + kept_shapes, compiler_params=_cp("parallel"))(x1, g, wfi, wfo)
    final_g, tgt = head
    return pl.pallas_call(
        body_with_head, name=f"ffn_fwd_loss_l{layer}", grid=(s // tm,),
        in_specs=[tile, _full((1, D))] + weights + [_full((1, D)), tile],
        out_specs=[tile, _full((1, 1)), _full((LANE_ROWS, HD))] + kept_specs,
        out_shape=[SDS((s, D), F32), SDS((1, 1), F32), SDS((LANE_ROWS, HD), F32)] + kept_shapes,
        compiler_params=_cp("arbitrary"))(x1, g, wfi, wfo, final_g, tgt)


def _ffn_bwd(dx2, wfo, factors, wfi, x1, g, layer, tm):
    s = dx2.shape[0]

    def body(dx_ref, wo_ref, f_ref, wi_ref, x_ref, g_ref, dgu_ref, dx1_ref, dg_ref):
        @pl.when(pl.program_id(0) == 0)
        def _():
            dg_ref[...] = jnp.zeros_like(dg_ref)

        dx = dx_ref[...]
        dxb = dx.astype(BF16)
        dh = None
        for k in range(4):
            dff = _dot_nt(dxb, wo_ref[k])
            d_gate = (dff * f_ref[0, k].astype(F32)).astype(BF16)
            d_up = (dff * f_ref[1, k].astype(F32)).astype(BF16)
            dgu_ref[0, k] = d_gate
            dgu_ref[1, k] = d_up
            part = _dot(d_gate, wi_ref[k]) + _dot(d_up, wi_ref[4 + k])
            dh = part if dh is None else dh + part
        dxn, dg = _rms_bwd(x_ref[...], g_ref[...], dh)
        dx1_ref[...] = dx + dxn
        _add_rows128(dg_ref, dg)

    tile = pl.BlockSpec((tm, D), lambda i: (i, 0))
    blk = pl.BlockSpec((2, 4, tm, FF_BLK), lambda i: (0, 0, i, 0))
    return pl.pallas_call(
        body, name=f"ffn_bwd_l{layer}", grid=(s // tm,),
        in_specs=[tile, _resident((4, FF_BLK, D)), blk, _resident((N_DEV, FF_BLK, D)), tile, _full((1, D))],
        out_specs=[blk, tile, _full((LANE_ROWS, HD))],
        out_shape=[SDS((2, 4, s, FF_BLK), BF16), SDS((s, D), F32), SDS((LANE_ROWS, HD), F32)],
        compiler_params=_cp("arbitrary"))(dx2, wfo, factors, wfi, x1, g)


def _mm_nt_rms_bwd(a, a_spec, a_blocks, w, w_is_transposed, x, g, dres, name, tm, jobs=()):
    s = x.shape[0]

    def body(a_ref, w_ref, x_ref, g_ref, dres_ref, dx_ref, dg_ref):
        @pl.when(pl.program_id(0) == 0)
        def _():
            dg_ref[...] = jnp.zeros_like(dg_ref)

        dh = None
        for k, blk in enumerate(a_blocks(a_ref)):
            part = _dot(blk, w_ref[k]) if w_is_transposed else _dot_nt(blk, w_ref[k])
            dh = part if dh is None else dh + part
        dx, dg = _rms_bwd(x_ref[...], g_ref[...], dh)
        dx_ref[...] = dres_ref[...] + dx
        _add_rows128(dg_ref, dg)

    tile = pl.BlockSpec((tm, D), lambda i: (i, 0))
    body, job_in, job_args, job_out, job_shapes, aliases = _carry(jobs, s // tm, body, 5, 2)
    res = pl.pallas_call(
        body, name=name, grid=(s // tm,),
        in_specs=[a_spec, _resident(w.shape), tile, _full((1, D)), tile] + job_in,
        out_specs=[tile, _full((LANE_ROWS, HD))] + job_out,
        out_shape=[SDS((s, D), F32), SDS((LANE_ROWS, HD), F32)] + job_shapes,
        input_output_aliases=aliases, compiler_params=_cp("arbitrary"))(a, w, x, g, dres, *job_args)
    for j, job in enumerate(jobs):
        job.results = res[2 + 4 * j:6 + 4 * j]
    return res[0], res[1]


def _mm_tn(a, a_spec, b, b_spec, nb, out_shape, out_spec, name, a_is_transposed=True, after=None, jobs=()):
    def body(a_ref, b_ref, *rest):
        o_ref = rest[-1]
        bb = b_ref[...].astype(BF16)
        o_ref[...] = (_dot(a_ref[...], bb) if a_is_transposed else _dot_tn(a_ref[...], bb)).astype(BF16)

    deps = [] if after is None else [after]
    body, job_in, job_args, job_out, job_shapes, aliases = _carry(jobs, nb, body, 2 + len(deps), 1)
    res = pl.pallas_call(
        body, name=name, grid=(nb,), in_specs=[a_spec, b_spec] + [_ANY] * len(deps) + job_in,
        out_specs=[out_spec] + job_out, out_shape=[SDS(out_shape, BF16)] + job_shapes,
        input_output_aliases=aliases, compiler_params=_cp("parallel"))(a, b, *deps, *job_args)
    for j, job in enumerate(jobs):
        job.results = res[1 + 4 * j:5 + 4 * j]
    return res[0]


def _mixer_bwd(dx1, wo, h0, h1, z, lng, lnb, ws, wst, bsb, layer, tm):
    s = dx1.shape[0]
    nt = s // tm

    def body(dx_ref, wo_ref, h0_ref, h1_ref, zu_ref, zv_ref, zg_ref, za_ref, zb_ref, lng_ref, lnb_ref,
             ws_ref, wst_ref, bsb_ref, dz_ref, dh_ref, dws_ref, dbs_ref, dlng_ref, dlnb_ref,
             du_s, dv_s, ya_s, dbs_acc):
        i = pl.program_id(0)

        @pl.when(i == 0)
        def _():
            for r in (dws_ref, dlng_ref, dlnb_ref, dbs_acc):
                r[...] = jnp.zeros_like(r)

        dm = _dot_nt(dx_ref[...].astype(BF16), wo_ref[...])
        sa = _sigmoid(za_ref[...].astype(F32))
        sb = _sigmoid(zb_ref[...].astype(F32))
        zg = zg_ref[...].astype(F32)
        gg, tg = _gelu(zg)
        hs = h0_ref[...] + h1_ref[...]
        dyb = dm * sb
        dya = dm * sa
        dh_ref[...] = dyb * gg
        dz_ref[:, 2 * D:3 * D] = jnp.zeros((tm, D), BF16)
        dz_ref[:, 3 * D:4 * D] = (dyb * hs * _gelu_grad(zg, tg)).astype(BF16)
        dz_ref[:, 5 * D:6 * D] = (dm * (hs * gg) * (sb * (1.0 - sb))).astype(BF16)

        zu, zv, u, tu, tv, xh, rstd, vb = _gmlp_values(zu_ref, zv_ref, lng_ref, lnb_ref)
        for c in range(tm // HD):
            rs = slice(c * HD, (c + 1) * HD)
            for g in range(HEADS):
                cs = slice(g * HD, (g + 1) * HD)
                vblk = vb[rs, cs]
                mixed = _dot(ws_ref[g], vblk) + bsb_ref[g]
                ya_s[rs, cs] = u[rs, cs] * mixed
                du_s[rs, cs] = dya[rs, cs] * mixed
                dmx = dya[rs, cs] * u[rs, cs]
                dbs_acc[g] += dmx
                dmxb = dmx.astype(BF16)
                dws_ref[g] += _dot_nt(dmxb, vblk)
                dv_s[rs, cs] = _dot(wst_ref[g], dmxb)
        dz_ref[:, 4 * D:5 * D] = (dm * ya_s[...] * (sa * (1.0 - sa))).astype(BF16)
        dv = dv_s[...]
        _add_rows128(dlng_ref, jnp.sum(dv * xh, axis=0, keepdims=True))
        _add_rows128(dlnb_ref, jnp.sum(dv, axis=0, keepdims=True))
        dxh = dv * lng_ref[...]
        dgv = rstd * (dxh - jnp.mean(dxh, axis=-1, keepdims=True)
                      - xh * jnp.mean(dxh * xh, axis=-1, keepdims=True))
        dz_ref[:, 0:D] = (du_s[...] * _gelu_grad(zu, tu)).astype(BF16)
        dz_ref[:, D:2 * D] = (dgv * _gelu_grad(zv, tv)).astype(BF16)

        @pl.when(i == nt - 1)
        def _():
            for g in range(HEADS):
                dbs_ref[g:g + 1, :] = jnp.sum(dbs_acc[g].T, axis=0, keepdims=True)

    tile = pl.BlockSpec((tm, D), lambda i: (i, 0))
    wspec = _full((HEADS, HD, HD))
    return pl.pallas_call(
        body, name=f"mixer_bwd_l{layer}", grid=(nt,),
        in_specs=[tile, _full((D, D)), tile, tile]
        + [pl.BlockSpec((tm, D), lambda i, c=c: (i, c)) for c in (0, 1, 3, 4, 5)]
        + [_full((1, D)), _full((1, D)), wspec, wspec, wspec],
        out_specs=[pl.BlockSpec((tm, N_IN), lambda i: (i, 0)), tile, wspec, _full((HEADS, HD)),
                   _full((LANE_ROWS, HD)), _full((LANE_ROWS, HD))],
        out_shape=[SDS((s, N_IN), BF16), SDS((s, D), F32), SDS((HEADS, HD, HD), F32), SDS((HEADS, HD), F32),
                   SDS((LANE_ROWS, HD), F32), SDS((LANE_ROWS, HD), F32)],
        scratch_shapes=[pltpu.VMEM((tm, D), F32)] * 3 + [pltpu.VMEM((HEADS, HD, HD), F32)],
        compiler_params=_cp("arbitrary"))(dx1, wo, h0, h1, z, z, z, z, z, lng, lnb, ws, wst, bsb)


def _lru_gates_bwd(xcb, gates, h0, h1, g0, g1, wr, wi, lam, layer, tm):
    s = xcb.shape[0]
    nt = s // tm

    def body(xc_ref, r0_ref, i0_ref, r1_ref, i1_ref, h0p_ref, h0_ref, h1_ref, h1n_ref, g0_ref, g1_ref,
             wr_ref, wi_ref, lam_ref, dxc_ref, dwr_ref, dwi_ref, dbr_ref, dbi_ref, dlam_ref):
        i = pl.program_id(0)
        fp, fn = _halo_flags(nt)

        @pl.when(i == 0)
        def _():
            for r in (dwr_ref, dwi_ref, dbr_ref, dbi_ref, dlam_ref):
                r[...] = jnp.zeros_like(r)

        xb = xc_ref[...]
        xc = xb.astype(F32)
        zeros8 = jnp.zeros((8, D), F32)
        h_prev = _taps(h0p_ref[...] * fp, h0_ref[...], zeros8, tm)[1]
        h_next = _taps(zeros8, h1_ref[...], h1n_ref[...] * fn, tm)[3]
        dxc = jnp.zeros((tm, D), F32)
        for d, (g_ref, hsh, r_ref, i_ref) in enumerate(((g0_ref, h_prev, r0_ref, i0_ref),
                                                        (g1_ref, h_next, r1_ref, i1_ref))):
            sp = _softplus(-lam_ref[d:d + 1, :])
            r = r_ref[...].astype(F32)
            ig = i_ref[...].astype(F32)
            a, q = _lru_decay(r, sp)
            rmult = jnp.where(q > 0.0, lax.rsqrt(jnp.where(q > 0.0, q, 1.0)), 0.0)
            mult = q * rmult
            db = g_ref[...]
            da = db * hsh
            dmult = db * (ig * xc)
            di = db * (mult * xc)
            dxc = dxc + db * (mult * ig)
            dla = da * a - dmult * (a * a * rmult)
            dsp_dlam = -_sigmoid(-lam_ref[d:d + 1, :])
            _add_rows128(dlam_ref, jnp.sum(dla * r, axis=0, keepdims=True) * ((-LRU_C) * dsp_dlam), d * LANE_ROWS)
            dpr = dla * sp * (-LRU_C) * (r * (1.0 - r))
            dpi = di * (ig * (1.0 - ig))
            _add_rows128(dbr_ref, jnp.sum(dpr, axis=0, keepdims=True), d * LANE_ROWS)
            _add_rows128(dbi_ref, jnp.sum(dpi, axis=0, keepdims=True), d * LANE_ROWS)
            dprb = dpr.astype(BF16)
            dpib = dpi.astype(BF16)
            parts = []
            for h in range(HEADS):
                cs = slice(h * HD, (h + 1) * HD)
                dwr_ref[d, h] += _dot_tn(xb[:, cs], dprb[:, cs])
                dwi_ref[d, h] += _dot_tn(xb[:, cs], dpib[:, cs])
                parts.append(_dot_nt(dprb[:, cs], wr_ref[d, h]) + _dot_nt(dpib[:, cs], wi_ref[d, h]))
            dxc = dxc + jnp.concatenate(parts, axis=1)
        dxc_ref[...] = dxc.astype(BF16)

    tile = pl.BlockSpec((tm, D), lambda i: (i, 0))
    hp, hc, hn = _halo_specs(tm, s, 0)
    wspec = _full((2, HEADS, HD, HD))
    vspec = _full((2 * LANE_ROWS, HD))
    return pl.pallas_call(
        body, name=f"lru_gates_bwd_l{layer}", grid=(nt,),
        in_specs=[tile] * 5 + [hp, hc, hc, hn, tile, tile, wspec, wspec, _full((2, D))],
        out_specs=[tile, wspec, wspec, vspec, vspec, vspec],
        out_shape=[SDS((s, D), BF16), SDS((2, HEADS, HD, HD), F32), SDS((2, HEADS, HD, HD), F32)]
        + [SDS((2 * LANE_ROWS, HD), F32)] * 3,
        compiler_params=_cp("arbitrary"))(xcb, *gates, h0, h0, h1, h1, g0, g1, wr, wi, lam)


def _conv_bwd(dz, dxc, z, cw, layer, tm):
    s = z.shape[0]
    nt = s // tm

    def body(dz_in, dp_ref, dc_ref, dn_ref, zp_ref, zc_ref, zn_ref, cw_ref, dz_ref, dcw_ref, dcb_ref):
        del dz_in
        fp, fn = _halo_flags(nt)

        @pl.when(pl.program_id(0) == 0)
        def _():
            dcw_ref[...] = jnp.zeros_like(dcw_ref)
            dcb_ref[...] = jnp.zeros_like(dcb_ref)

        dxc_halo = _halo_load(dp_ref, dc_ref, dn_ref, fp, fn)
        dxc = dxc_halo[1]
        dm2, dm1, _, dp1, _ = _taps(*dxc_halo, tm)
        dz_ref[...] = (cw_ref[0:1, :] * dp1 + cw_ref[1:2, :] * dxc + cw_ref[2:3, :] * dm1
                       + cw_ref[3:4, :] * dm2).astype(BF16)
        _, zm1, z0, zp1, zp2 = _taps(*_halo_load(zp_ref, zc_ref, zn_ref, fp, fn), tm)
        for k, zt in enumerate((zm1, z0, zp1, zp2)):
            _add_rows128(dcw_ref, jnp.sum(dxc * zt, axis=0, keepdims=True), k * LANE_ROWS)
        _add_rows128(dcb_ref, jnp.sum(dxc, axis=0, keepdims=True))

    return pl.pallas_call(
        body, name=f"conv_bwd_l{layer}", grid=(nt,),
        in_specs=[pl.BlockSpec(memory_space=pl.ANY), *_halo_specs(tm, s, 0, 16), *_halo_specs(tm, s, 2, 16),
                  _full((4, D))],
        out_specs=[pl.BlockSpec((tm, D), lambda i: (i, 2)), _full((4 * LANE_ROWS, HD)), _full((LANE_ROWS, HD))],
        out_shape=[SDS((s, N_IN), BF16), SDS((4 * LANE_ROWS, HD), F32), SDS((LANE_ROWS, HD), F32)],
        input_output_aliases={0: 0},
        compiler_params=_cp("arbitrary"))(dz, dxc, dxc, dxc, z, z, z, cw)


def _me():
    return lax.axis_index("x"), lax.axis_index("y"), lax.axis_index("c")


def _peer(m):
    x, y, c = _me()
    px = 1 - x if m & 4 else x
    py = 1 - y if m & 2 else y
    pc = 1 - c if m & 1 else c
    return (px, py, pc), 4 * px + 2 * py + pc


_ANY = pl.BlockSpec(memory_space=pl.ANY)
_EXCHANGE_SEMS = [pltpu.SemaphoreType.DMA((N_DEV - 1,)), pltpu.SemaphoreType.DMA((N_DEV - 1,)), pltpu.SemaphoreType.DMA(())]


def _all_gather(v, after, name):
    def body(v_ref, after_ref, o_ref, send_sems, recv_sems, local_sem):
        del after_ref
        x, y, c = _me()
        me = 4 * x + 2 * y + c
        local = pltpu.make_async_copy(v_ref, o_ref.at[me], local_sem)
        local.start()
        sends = []
        for m in range(1, N_DEV):
            dev, _ = _peer(m)
            cp = pltpu.make_async_remote_copy(v_ref, o_ref.at[me], send_sems.at[m - 1], recv_sems.at[m - 1],
                                              device_id=dev, device_id_type=pl.DeviceIdType.MESH)
            cp.start()
            sends.append(cp)
        for m in range(1, N_DEV):
            dev, blk = _peer(m)
            pltpu.make_async_remote_copy(v_ref, o_ref.at[blk], send_sems.at[m - 1], recv_sems.at[m - 1],
                                         device_id=dev, device_id_type=pl.DeviceIdType.MESH).wait_recv()
        for cp in sends:
            cp.wait_send()
        local.wait()

    return pl.pallas_call(
        body, name=name, in_specs=[_ANY, _ANY], out_specs=_ANY,
        out_shape=SDS((N_DEV,) + v.shape, v.dtype), scratch_shapes=_EXCHANGE_SEMS)(v, after)


_HBM = pl.BlockSpec(memory_space=pltpu.HBM)
_SEM = pl.BlockSpec(memory_space=pltpu.SEMAPHORE)
_EFFECT = pltpu.CompilerParams(has_side_effects=pltpu.SideEffectType.DATAFLOW_SIDE_EFFECTING)
_PEER_SEMS = pltpu.SemaphoreType.DMA((N_DEV - 1,))


def _in_hbm(a):
    return pltpu.with_memory_space_constraint(a, pltpu.HBM)


def _remote(src, dst, send_sems, recv_sems, m):
    dev, _ = _peer(m)
    return pltpu.make_async_remote_copy(src, dst, send_sems.at[m - 1], recv_sems.at[m - 1],
                                        device_id=dev, device_id_type=pl.DeviceIdType.MESH)


def _gather_start(lands, after, name):
    n = len(lands)

    def body(*refs):
        land = refs[:n]
        sems = refs[n + 1:3 * n + 1]
        token = refs[-1]
        x, y, c = _me()
        me = 4 * x + 2 * y + c
        for t in range(n):
            for m in range(1, N_DEV):
                _remote(land[t].at[me], land[t].at[me], sems[2 * t], sems[2 * t + 1], m).start()
        token[...] = jnp.zeros_like(token)

    res = pl.pallas_call(
        body, name=name, in_specs=[_HBM] * n + [_ANY],
        out_specs=[_SEM] * (2 * n) + [_HBM] * n + [pl.BlockSpec(memory_space=pltpu.VMEM)],
        out_shape=[_PEER_SEMS] * (2 * n) + [pltpu.HBM(a.shape, a.dtype) for a in lands] + [SDS((8, 128), F32)],
        input_output_aliases={t: 2 * n + t for t in range(n)},
        compiler_params=_EFFECT)(*[_in_hbm(a) for a in lands], after)
    return [(res[2 * t], res[2 * t + 1], res[2 * n + t]) for t in range(n)], res[-1]


def _gather_wait(handle, after, name):
    send_sems, recv_sems, land = handle

    def body(land_ref, ssem, rsem, after_ref, out_ref):
        del after_ref, out_ref
        x, y, c = _me()
        me = 4 * x + 2 * y + c
        for m in range(1, N_DEV):
            _, blk = _peer(m)
            cp = _remote(land_ref.at[me], land_ref.at[blk], ssem, rsem, m)
            cp.wait_send()
            cp.wait_recv()

    return pl.pallas_call(
        body, name=name, in_specs=[_HBM, _SEM, _SEM, _ANY], out_specs=_HBM,
        out_shape=pltpu.HBM(land.shape, land.dtype), input_output_aliases={0: 0},
        compiler_params=_EFFECT)(land, send_sems, recv_sems, after)


FIRST_STAGE = (1, 2, 4, 6)
RELAYED = (2, 4, 6)
OTHER_CORE = 1


def _stage_copy(src, dst, send_sems, recv_sems, k, m):
    dev, _ = _peer(m)
    return pltpu.make_async_remote_copy(src, dst, send_sems.at[k], recv_sems.at[k],
                                        device_id=dev, device_id_type=pl.DeviceIdType.MESH)


def _gather2_start(lands, after, name):
    n = len(lands)

    def body(*refs):
        land = refs[:n]
        sems = refs[n + 1:3 * n + 1]
        token = refs[-1]
        x, y, c = _me()
        me = 4 * x + 2 * y + c
        for t in range(n):
            for k, m in enumerate(FIRST_STAGE):
                _stage_copy(land[t].at[me], land[t].at[me], sems[2 * t], sems[2 * t + 1], k, m).start()
        token[...] = jnp.zeros_like(token)

    stage_sems = pltpu.SemaphoreType.DMA((len(FIRST_STAGE),))
    res = pl.pallas_call(
        body, name=name, in_specs=[_HBM] * n + [_ANY],
        out_specs=[_SEM] * (2 * n) + [_HBM] * n + [pl.BlockSpec(memory_space=pltpu.VMEM)],
        out_shape=[stage_sems] * (2 * n) + [pltpu.HBM(a.shape, a.dtype) for a in lands] + [SDS((8, 128), F32)],
        input_output_aliases={t: 2 * n + t for t in range(n)},
        compiler_params=_EFFECT)(*[_in_hbm(a) for a in lands], after)
    return [(res[2 * t], res[2 * t + 1], res[2 * n + t]) for t in range(n)], res[-1]


def _gather2_relay(handles, after, name):
    n = len(handles)

    def body(*refs):
        land, send1, recv1 = refs[:n], refs[n:2 * n], refs[2 * n:3 * n]
        sems = refs[3 * n + 1:5 * n + 1]
        token = refs[-1]
        x, y, c = _me()
        me = 4 * x + 2 * y + c
        for t in range(n):
            for j, m in enumerate(RELAYED):
                _, blk = _peer(m)
                _stage_copy(land[t].at[me], land[t].at[blk], send1[t], recv1[t], 1 + j, m).wait_recv()
                _stage_copy(land[t].at[blk], land[t].at[blk], sems[2 * t], sems[2 * t + 1], j, OTHER_CORE).start()
        token[...] = jnp.zeros_like(token)

    relay_sems = pltpu.SemaphoreType.DMA((len(RELAYED),))
    lands = [h[2] for h in handles]
    res = pl.pallas_call(
        body, name=name, in_specs=[_HBM] * n + [_SEM] * (2 * n) + [_ANY],
        out_specs=[_SEM] * (2 * n) + [_HBM] * n + [pl.BlockSpec(memory_space=pltpu.VMEM)],
        out_shape=[relay_sems] * (2 * n) + [pltpu.HBM(a.shape, a.dtype) for a in lands] + [SDS((8, 128), F32)],
        input_output_aliases={t: 2 * n + t for t in range(n)},
        compiler_params=_EFFECT)(*lands, *[h[0] for h in handles], *[h[1] for h in handles], after)
    return [(h[0], h[1], res[2 * t], res[2 * t + 1], res[2 * n + t]) for t, h in enumerate(handles)], res[-1]


def _gather2_wait(handle, after, name):
    send1, recv1, send2, recv2, land = handle

    def body(land_ref, s1, r1, s2, r2, after_ref, out_ref):
        del after_ref, out_ref
        x, y, c = _me()
        me = 4 * x + 2 * y + c
        _, other = _peer(OTHER_CORE)
        _stage_copy(land_ref.at[me], land_ref.at[other], s1, r1, 0, OTHER_CORE).wait_recv()
        for k, m in enumerate(FIRST_STAGE):
            _stage_copy(land_ref.at[me], land_ref.at[me], s1, r1, k, m).wait_send()
        for j, m in enumerate(RELAYED):
            _, mine = _peer(m)
            _, theirs = _peer(m ^ OTHER_CORE)
            _stage_copy(land_ref.at[mine], land_ref.at[mine], s2, r2, j, OTHER_CORE).wait_send()
            _stage_copy(land_ref.at[mine], land_ref.at[theirs], s2, r2, j, OTHER_CORE).wait_recv()

    return pl.pallas_call(
        body, name=name, in_specs=[_HBM] + [_SEM] * 4 + [_ANY], out_specs=_HBM,
        out_shape=pltpu.HBM(land.shape, land.dtype), input_output_aliases={0: 0},
        compiler_params=_EFFECT)(land, send1, recv1, send2, recv2, after)


def _exchange_start(ps, name):
    n = len(ps)

    def body(*refs):
        p = refs[:n]
        got = refs[n:2 * n]
        sems = refs[2 * n:5 * n]
        token = refs[-1]
        x, y, c = _me()
        me = 4 * x + 2 * y + c
        for t in range(n):
            pltpu.make_async_copy(p[t].at[me], got[t].at[me], sems[3 * t + 2]).start()
            for m in range(1, N_DEV):
                _, blk = _peer(m)
                _remote(p[t].at[blk], got[t].at[me], sems[3 * t], sems[3 * t + 1], m).start()
        token[...] = jnp.zeros_like(token)

    res = pl.pallas_call(
        body, name=name, in_specs=[_HBM] * (2 * n),
        out_specs=[_SEM] * (3 * n) + [_HBM] * (2 * n) + [pl.BlockSpec(memory_space=pltpu.VMEM)],
        out_shape=[_PEER_SEMS, _PEER_SEMS, pltpu.SemaphoreType.DMA(())] * n
        + [pltpu.HBM(a.shape, a.dtype) for a in ps] * 2 + [SDS((8, 128), F32)],
        input_output_aliases={t: 3 * n + t for t in range(2 * n)},
        compiler_params=_EFFECT)(*[_in_hbm(a) for a in ps], *[_in_hbm(lax.empty(a.shape, a.dtype)) for a in ps])
    return [(res[3 * t], res[3 * t + 1], res[3 * t + 2], res[3 * n + t], res[4 * n + t]) for t in range(n)], res[-1]


def _exchange_wait(handle, after, name):
    send_sems, recv_sems, local_sem, p, got = handle

    def body(p_ref, got_ref, ssem, rsem, lsem, after_ref, p_out, got_out):
        del after_ref, p_out, got_out
        x, y, c = _me()
        me = 4 * x + 2 * y + c
        pltpu.make_async_copy(p_ref.at[me], got_ref.at[me], lsem).wait()
        for m in range(1, N_DEV):
            _, blk = _peer(m)
            cp = _remote(p_ref.at[blk], got_ref.at[blk], ssem, rsem, m)
            cp.wait_send()
            cp.wait_recv()

    return pl.pallas_call(
        body, name=name, in_specs=[_HBM, _HBM, _SEM, _SEM, _SEM, _ANY], out_specs=[_HBM, _HBM],
        out_shape=[pltpu.HBM(p.shape, p.dtype), pltpu.HBM(got.shape, got.dtype)],
        input_output_aliases={0: 0, 1: 1}, compiler_params=_EFFECT)(p, got, send_sems, recv_sems, local_sem, after)[1]


CHIPS = (0, 2, 4, 6)


def _pairs_start(p, name):
    def body(p_ref, pair_ref, ssem, rsem, p_out, pair_out, token):
        del p_out, pair_out
        for k, chip in enumerate(CHIPS):
            _, blk = _peer(chip ^ OTHER_CORE)
            _stage_copy(p_ref.at[blk], pair_ref.at[k], ssem, rsem, k, OTHER_CORE).start()
        token[...] = jnp.zeros_like(token)

    sems = pltpu.SemaphoreType.DMA((len(CHIPS),))
    pair = lax.empty((len(CHIPS),) + p.shape[1:], p.dtype)
    res = pl.pallas_call(
        body, name=name, in_specs=[_HBM] * 2,
        out_specs=[_SEM] * 2 + [_HBM] * 2 + [pl.BlockSpec(memory_space=pltpu.VMEM)],
        out_shape=[sems, sems, pltpu.HBM(p.shape, p.dtype), pltpu.HBM(pair.shape, pair.dtype), SDS((8, 128), F32)],
        input_output_aliases={0: 2, 1: 3}, compiler_params=_EFFECT)(_in_hbm(p), _in_hbm(pair))
    return res[:4], res[4]


def _pairs_wait(handle, after, name):
    send_sems, recv_sems, p, pair = handle

    def body(p_ref, pair_ref, ssem, rsem, after_ref, p_out, pair_out):
        del after_ref, p_out, pair_out
        for k, chip in enumerate(CHIPS):
            _, blk = _peer(chip ^ OTHER_CORE)
            cp = _stage_copy(p_ref.at[blk], pair_ref.at[k], ssem, rsem, k, OTHER_CORE)
            cp.wait_send()
            cp.wait_recv()

    return pl.pallas_call(
        body, name=name, in_specs=[_HBM, _HBM, _SEM, _SEM, _ANY], out_specs=[_HBM, _HBM],
        out_shape=[pltpu.HBM(p.shape, p.dtype), pltpu.HBM(pair.shape, pair.dtype)],
        input_output_aliases={0: 0, 1: 1}, compiler_params=_EFFECT)(p, pair, send_sems, recv_sems, after)


def _sum_pairs(p, pair, me1, name):
    _, r, c = pair.shape
    tr = _row_tile(r)

    def body(me_ref, p_ref, pair_ref, o_ref):
        del me_ref
        o_ref[...] = (p_ref[...].astype(F32) + pair_ref[...].astype(F32)).astype(o_ref.dtype)

    def mine(k, i, me):
        chip = 2 * k
        return (jnp.bitwise_xor(me[0], chip), i, 0)

    assert CHIPS == tuple(2 * k for k in range(len(CHIPS)))
    blk = pl.BlockSpec((None, tr, c), lambda k, i, me: (k, i, 0))
    return pl.pallas_call(
        body, name=name,
        grid_spec=pltpu.PrefetchScalarGridSpec(
            num_scalar_prefetch=1, grid=(len(CHIPS), r // tr),
            in_specs=[pl.BlockSpec((None, tr, c), mine), blk], out_specs=blk),
        out_shape=SDS(pair.shape, pair.dtype), compiler_params=_cp("parallel", "parallel"))(me1, p, pair)


def _chips_start(q, name):
    def body(q_ref, got_ref, ssem, rsem, lsem, q_out, got_out, token):
        del q_out, got_out
        pltpu.make_async_copy(q_ref.at[0], got_ref.at[0], lsem).start()
        for k, chip in enumerate(CHIPS[1:]):
            _stage_copy(q_ref.at[k + 1], got_ref.at[k + 1], ssem, rsem, k, chip).start()
        token[...] = jnp.zeros_like(token)

    sems = pltpu.SemaphoreType.DMA((len(CHIPS) - 1,))
    res = pl.pallas_call(
        body, name=name, in_specs=[_HBM] * 2,
        out_specs=[_SEM] * 3 + [_HBM] * 2 + [pl.BlockSpec(memory_space=pltpu.VMEM)],
        out_shape=[sems, sems, pltpu.SemaphoreType.DMA(()), pltpu.HBM(q.shape, q.dtype), pltpu.HBM(q.shape, q.dtype),
                   SDS((8, 128), F32)],
        input_output_aliases={0: 3, 1: 4}, compiler_params=_EFFECT)(_in_hbm(q), _in_hbm(lax.empty(q.shape, q.dtype)))
    return res[:5], res[5]


def _chips_wait(handle, after, name):
    send_sems, recv_sems, local_sem, q, got = handle

    def body(q_ref, got_ref, ssem, rsem, lsem, after_ref, q_out, got_out):
        del after_ref, q_out, got_out
        pltpu.make_async_copy(q_ref.at[0], got_ref.at[0], lsem).wait()
        for k, chip in enumerate(CHIPS[1:]):
            cp = _stage_copy(q_ref.at[k + 1], got_ref.at[k + 1], ssem, rsem, k, chip)
            cp.wait_send()
            cp.wait_recv()

    return pl.pallas_call(
        body, name=name, in_specs=[_HBM, _HBM, _SEM, _SEM, _SEM, _ANY], out_specs=[_HBM, _HBM],
        out_shape=[pltpu.HBM(q.shape, q.dtype), pltpu.HBM(got.shape, got.dtype)],
        input_output_aliases={0: 0, 1: 1}, compiler_params=_EFFECT)(q, got, send_sems, recv_sems, local_sem, after)[1]


def _cast_into_slot(w, layer, me1, name):
    _, r, c = w.shape
    tr = next(t for t in (512, 352, r) if r % t == 0)

    def body(me_ref, w_ref, o_ref):
        del me_ref
        o_ref[...] = w_ref[...].astype(BF16)

    return pl.pallas_call(
        body, name=name,
        grid_spec=pltpu.PrefetchScalarGridSpec(
            num_scalar_prefetch=1, grid=(r // tr,),
            in_specs=[pl.BlockSpec((None, tr, c), lambda i, me: (layer, i, 0))],
            out_specs=pl.BlockSpec((None, tr, c), lambda i, me: (me[0], i, 0))),
        out_shape=SDS((N_DEV, r, c), BF16), compiler_params=_cp("arbitrary"))(me1, w)


def _cast_all_into_slots(ws, layers, me1, after, name):
    n = len(ws)

    def body(me_ref, *refs):
        del me_ref
        for w_ref, o_ref in zip(refs[:n], refs[n + 1:]):
            o_ref[...] = w_ref[...].astype(BF16)

    return pl.pallas_call(
        body, name=name,
        grid_spec=pltpu.PrefetchScalarGridSpec(
            num_scalar_prefetch=1, grid=(1,),
            in_specs=[pl.BlockSpec((None,) + a.shape[1:], lambda i, me, l=l: (l, 0, 0)) for a, l in zip(ws, layers)]
            + [_ANY],
            out_specs=[pl.BlockSpec((None,) + a.shape[1:], lambda i, me: (me[0], 0, 0)) for a in ws]),
        out_shape=[SDS((N_DEV,) + a.shape[1:], BF16) for a in ws],
        compiler_params=_cp("arbitrary"))(me1, *ws, after)


def _sum8_into_slot(p, me1, name):
    _, r, c = p.shape

    def body(me_ref, p_ref, o_ref):
        del me_ref
        acc = p_ref[0]
        for k in range(1, N_DEV):
            acc = acc + p_ref[k]
        o_ref[...] = acc

    return pl.pallas_call(
        body, name=name,
        grid_spec=pltpu.PrefetchScalarGridSpec(
            num_scalar_prefetch=1, grid=(1,),
            in_specs=[pl.BlockSpec(p.shape, lambda i, me: (0, 0, 0))],
            out_specs=pl.BlockSpec((None, r, c), lambda i, me: (me[0], 0, 0))),
        out_shape=SDS(p.shape, F32), compiler_params=_cp("arbitrary"))(me1, p)


def _adamw(w, g, m, v):
    m = ADAM_B1 * m + (1.0 - ADAM_B1) * g
    v = ADAM_B2 * v + (1.0 - ADAM_B2) * (g * g)
    m_hat = m / (1.0 - ADAM_B1 ** ADAM_STEP)
    v_hat = v / (1.0 - ADAM_B2 ** ADAM_STEP)
    delta = -ADAM_LR * (m_hat / (jnp.sqrt(v_hat) + ADAM_EPS) + ADAM_WD * w)
    return delta, m, v


def _adam_tile(p_ref, w_ref, m_ref, v_ref, g_ref, d_ref, nm_ref, nv_ref):
    g = p_ref[0].astype(F32)
    for k in range(1, p_ref.shape[0]):
        g = g + p_ref[k].astype(F32)
    delta, nm, nv = _adamw(w_ref[...], g, m_ref[...], v_ref[...])
    g_ref[...] = g
    d_ref[...] = delta
    nm_ref[...] = nm
    nv_ref[...] = nv


class _AdamJob:
    def __init__(self, parts, w, m, v, layer, prev):
        self.args = [parts, w, m, v] + list(prev or ())
        self.layer, self.results = layer, None


def _carry(jobs, steps, body, n_in, n_out):
    in_specs, args, out_specs, out_shapes, aliases, n_prevs = [], [], [], [], {}, []
    for j, job in enumerate(jobs):
        _, r, c = job.args[0].shape
        nr = next(n for n in range(steps, 0, -1) if steps % n == 0 and r % (16 * n) == 0)
        nc = steps // nr
        assert c % (128 * nc) == 0
        tile = (r // nr, c // nc)
        blk = pl.BlockSpec((None,) + tile, lambda i, layer=job.layer, nc=nc: (layer, i // nc, i % nc))
        n_prev = len(job.args) - 4
        aliases.update({n_in + len(args) + 4 + k: n_out + 4 * j + k for k in range(n_prev)})
        in_specs += [pl.BlockSpec(job.args[0].shape[:1] + tile, lambda i, nc=nc: (0, i // nc, i % nc)), blk, blk, blk]
        in_specs += [_ANY] * n_prev
        args += job.args
        out_specs += [blk] * 4
        out_shapes += [SDS(job.args[1].shape, F32)] * 4
        n_prevs.append(n_prev)

    def carrying(*refs):
        ins, outs = refs[:n_in + len(args)], refs[n_in + len(args):]
        body(*ins[:n_in], *outs[:n_out])
        k = n_in
        for j, n_prev in enumerate(n_prevs):
            _adam_tile(*ins[k:k + 4], *outs[n_out + 4 * j:n_out + 4 * j + 4])
            k += 4 + n_prev

    return carrying, in_specs, args, out_specs, out_shapes, aliases


def _adam_shard(parts, w, m, v, layer, prev, name):
    n, r, c = parts.shape
    tr = next(t for t in (512, 352, r) if r % t == 0)
    n_prev = 0 if prev is None else 4

    def body(*refs):
        _adam_tile(*refs[:4], *refs[4 + n_prev:])

    blk = pl.BlockSpec((None, tr, c), lambda i: (layer, i, 0))
    return pl.pallas_call(
        body, name=name, grid=(r // tr,),
        in_specs=[pl.BlockSpec((n, tr, c), lambda i: (0, i, 0)), blk, blk, blk] + [_ANY] * n_prev,
        out_specs=[blk] * 4, out_shape=[SDS(w.shape, F32)] * 4,
        input_output_aliases={4 + k: k for k in range(n_prev)},
        compiler_params=_cp("parallel"))(parts, w, m, v, *(prev or ()))


SMALL_MATRICES = [("lru_w_r", 2048), ("lru_w_i", 2048), ("gmlp_w_s", 1024)]
SMALL_VECTORS = [("norm1_g", 8), ("gmlp_ln_g", 8), ("gmlp_ln_b", 8), ("gmlp_b_s", 8), ("conv_w", 32), ("conv_b", 8),
                 ("lru_b_r", 16), ("lru_b_i", 16), ("lru_lambda", 16), ("norm2_g", 8), ("final_g", 8)]
SMALL_VECTOR_ROW0 = sum(n for _, n in SMALL_MATRICES)
SMALL_VECTOR_BLOCK = 256
SMALL_ROWS = SMALL_VECTOR_ROW0 + SMALL_VECTOR_BLOCK


def _pack_small(small):
    parts = [small[k] for k, _ in SMALL_MATRICES]
    parts += [small[k] if k in small else jnp.zeros((n, HD), F32) for k, n in SMALL_VECTORS]
    flat = jnp.concatenate(parts)
    return jnp.pad(flat, ((0, SMALL_ROWS - flat.shape[0]), (0, 0))).reshape(N_DEV, SMALL_ROWS // N_DEV, HD)


def _adam_matrix(g0, g1, w, m, v, row0, name):
    _, rows, _ = w.shape
    tr = 512

    def body(g0_ref, g1_ref, w_ref, m_ref, v_ref, g_ref, d_ref, nm_ref, nv_ref):
        for l, src in enumerate((g0_ref, g1_ref)):
            g = src[...]
            delta, nm, nv = _adamw(w_ref[l], g, m_ref[l], v_ref[l])
            g_ref[l] = g
            d_ref[l] = delta
            nm_ref[l] = nm
            nv_ref[l] = nv

    gspec = pl.BlockSpec((tr, HD), lambda i: (row0 // tr + i, 0))
    blk = pl.BlockSpec((2, tr, HD), lambda i: (0, i, 0))
    return pl.pallas_call(body, name=name, grid=(rows // tr,), in_specs=[gspec, gspec] + [blk] * 3,
                          out_specs=[blk] * 4, out_shape=[SDS(w.shape, F32)] * 4,
                          compiler_params=_cp("parallel"))(g0, g1, w, m, v)


def _adam_vectors(g0, g1, dg1_parts, me1, ws, ms, vs):
    names = [k for k, _ in SMALL_VECTORS]
    n = len(names)

    def lanes(rows8):
        return jnp.concatenate([rows8[k:k + 1, :] for k in range(LANE_ROWS)], axis=1)

    def body(me_ref, g0_ref, g1_ref, dg1_ref, *refs):
        w_refs, m_refs, v_refs = refs[:n], refs[n:2 * n], refs[2 * n:3 * n]
        outs = refs[3 * n:]
        me = me_ref[0]
        g_refs = (g0_ref, g1_ref)

        def emit(i, idx, g):
            delta, nm, nv = _adamw(w_refs[i][idx], g, m_refs[i][idx], v_refs[i][idx])
            for j, val in enumerate((g, delta, nm, nv)):
                outs[4 * i + j][idx] = val

        off = 0
        for i, (name, rows) in enumerate(SMALL_VECTORS):
            for l in range(2):
                row = (slice(l, l + 1), slice(None))
                if name == "final_g":
                    if l == 1:
                        emit(i, (slice(0, 1), slice(None)), lanes(g1_ref[off:off + rows, :]))
                elif name == "norm1_g":
                    if l == 1:
                        emit(i, row, lanes(g0_ref[off:off + rows, :]))
                    else:
                        total = dg1_ref[0]
                        for k in range(1, N_DEV):
                            total = total + dg1_ref[k]
                        emit(i, row, lanes(total))
                elif name == "gmlp_b_s":
                    emit(i, (l,), g_refs[l][off:off + rows, :])
                elif rows == LANE_ROWS:
                    emit(i, row, lanes(g_refs[l][off:off + rows, :]))
                else:
                    for r in range(rows // LANE_ROWS):
                        emit(i, (l, slice(r, r + 1), slice(None)), g_refs[l][pl.ds(off + r * LANE_ROWS + me, 1), :])
            off += rows

    args = [ws[k] for k in names] + [ms[k] for k in names] + [vs[k] for k in names]
    gspec = pl.BlockSpec((SMALL_VECTOR_BLOCK, HD), lambda i, me: (SMALL_VECTOR_ROW0 // SMALL_VECTOR_BLOCK, 0))
    res = pl.pallas_call(
        body, name="adam_vectors",
        grid_spec=pltpu.PrefetchScalarGridSpec(
            num_scalar_prefetch=1, grid=(1,),
            in_specs=[gspec, gspec, _full(dg1_parts.shape)] + [_full(a.shape) for a in args],
            out_specs=[_full(ws[k].shape) for k in names for _ in range(4)]),
        out_shape=[SDS(ws[k].shape, F32) for k in names for _ in range(4)],
        compiler_params=_cp("arbitrary"))(me1, g0, g1, dg1_parts, *args)
    return {k: list(res[4 * i:4 * i + 4]) for i, k in enumerate(names)}


def _after(a, *tokens):
    for token in tokens:
        if token is not None:
            a = a + token[0:1, 0:1]
    return a


def _local_step(x, tgt, p, get_w, hook=lambda stage, layer, payload: None):
    s = x.shape[0]
    tm = _row_tile(s)
    wsb = p["gmlp_w_s"].astype(BF16)
    wstb = jnp.swapaxes(p["gmlp_w_s"], -1, -2).astype(BF16)
    bsb = jnp.broadcast_to(p["gmlp_b_s"][..., None], p["gmlp_w_s"].shape)
    wrb = p["lru_w_r"].astype(BF16)
    wib = p["lru_w_i"].astype(BF16)
    saved = []
    for l in range(2):
        win = get_w("w_in", l, x)
        z, h1 = _norm_inproj(x, _after(p["norm1_g"][l][None], hook("pre_inproj", l, win)), win, l, tm)
        a0, b0, a1, b1, xcb, *gates = _lru_gates_fwd(z, p["conv_w"][l], p["conv_b"][l][None], wrb[l], wib[l],
                                                     p["lru_b_r"][l], p["lru_b_i"][l], p["lru_lambda"][l], l, tm)
        h0, hr = _lru_scan(a0, b0, a1, b1, False, l)
        lng = _after(p["gmlp_ln_g"][l][None], hook("pre_gmlp", l, h0))
        wout = get_w("w_out", l, lng)
        x1, mg = _mixer_fwd(x, h0, hr, z, lng, p["gmlp_ln_b"][l][None], wsb[l], bsb[l], wout, l, tm)
        wfi = get_w("w_ffn_in", l, x1)
        wfo = get_w("w_ffn_out", l, x1)
        if l == 0:
            x2, ff, dff, h2 = _ffn_fwd(x1, p["norm2_g"][l][None], wfi, wfo, l, tm)
        else:
            dx, loss, dfg, ff, dff, h2 = _ffn_fwd(x1, p["norm2_g"][l][None], wfi, wfo, l, tm,
                                                  head=(p["final_g"][None], tgt))
        saved.append((x, z, h1, a0, a1, h0, hr, x1, mg, ff, dff, h2, win, wout, wfi, wfo, xcb, gates))
        x = x2
    pending = None
    for l in (1, 0):
        x0, z, h1, a0, a1, h0, hr, x1, mg, ff, dff, h2, win, wout, wfi, wfo, xcb, gates = saved[l]
        dgu, dx1, dg2 = _ffn_bwd(dx, wfo, dff, wfi.reshape(N_DEV, FF_BLK, D), x1, p["norm2_g"][l][None], l, tm)
        d_wfo = _mm_tn(ff, pl.BlockSpec((None, s, FF_BLK), lambda j: (j, 0, 0)), dx, _resident((s, D)),
                       4, (4, FF_BLK, D), pl.BlockSpec((None, FF_BLK, D), lambda j: (j, 0, 0)),
                       f"dw_ffn_out_l{l}", a_is_transposed=False)
        dgu8 = dgu.reshape(N_DEV, s, FF_BLK)
        d_wfi = _mm_tn(dgu8, pl.BlockSpec((None, s, FF_BLK), lambda j: (j, 0, 0)), h2, _resident((s, D)),
                       N_DEV, (N_DEV, FF_BLK, D), pl.BlockSpec((None, FF_BLK, D), lambda j: (j, 0, 0)),
                       f"dw_ffn_in_l{l}", a_is_transposed=False)
        d_wout = _mm_tn(mg, _resident((D, s)), dx1, pl.BlockSpec((s, D // 2), lambda j: (0, j)),
                        2, (D, D), pl.BlockSpec((D, D // 2), lambda j: (0, j)), f"dw_out_l{l}")
        token = hook("ffn_partials", l, dict(w_ffn_out=d_wfo.reshape(N_DEV, D_FF // N_DEV, D), w_ffn_in=d_wfi,
                                             w_out=d_wout.reshape(N_DEV, D // N_DEV, D)))
        pending = hook("mid_backward", l, dx1)
        dz, dh, dws, dbs, dlng, dlnb = _mixer_bwd(dx1, wout, h0, hr, z, _after(p["gmlp_ln_g"][l][None], token),
                                                  p["gmlp_ln_b"][l][None], wsb[l], wstb[l], bsb[l], l, tm)
        g1, g0 = _lru_scan(a1, dh, a0, dh, True, l)
        dxc, dwr, dwi, dbr, dbi, dlam = _lru_gates_bwd(
            xcb, gates, h0, hr, g0, g1, wrb[l], wib[l], _after(p["lru_lambda"][l], pending), l, tm)
        dz, dcw, dcb = _conv_bwd(dz, dxc, z, p["conv_w"][l], l, tm)
        small = dict(lru_w_r=dwr.reshape(-1, HD), lru_w_i=dwi.reshape(-1, HD), gmlp_w_s=dws.reshape(-1, HD),
                     gmlp_ln_g=dlng, gmlp_ln_b=dlnb, gmlp_b_s=dbs, conv_w=dcw, conv_b=dcb, lru_b_r=dbr,
                     lru_b_i=dbi, lru_lambda=dlam, norm2_g=dg2)
        if l == 1:
            small["final_g"] = dfg
        else:
            small["norm1_g"] = dg1
        started = hook("small_grads", l, small)
        d_win = _mm_tn(h1, _resident((D, s)), dz, pl.BlockSpec((s, IN_BLK), lambda j: (0, j)),
                       N_DEV, (N_DEV, D, IN_BLK), pl.BlockSpec((None, D, IN_BLK), lambda j: (j, 0, 0)),
                       f"dw_in_l{l}", after=started, jobs=hook("dw_in", l, started) or ())
        token = hook("mixer_partials", l, dict(w_in=d_win))
        dx, dg1 = _mm_nt_rms_bwd(
            dz, pl.BlockSpec((tm, N_IN), lambda i: (i, 0)),
            lambda r: [r[:, k * IN_BLK:(k + 1) * IN_BLK] for k in range(N_DEV)],
            win, False, x0, _after(p["norm1_g"][l][None], token, started, pending), dx1, f"inproj_bwd_dx_l{l}", tm,
            jobs=hook("inproj_bwd_dx", l, token) or ())
        pending = None
    return loss, dx, dg1


_REPL = ["norm1_g", "gmlp_ln_g", "gmlp_ln_b", "gmlp_w_s", "gmlp_b_s", "conv_b", "lru_w_r", "lru_w_i", "norm2_g", "final_g"]
_LANE_SHARDED = ["conv_w", "lru_b_r", "lru_b_i", "lru_lambda"]
_BIG = ["w_in", "w_out", "w_ffn_in", "w_ffn_out"]
_ORDER = ["norm1_g", "w_in", "gmlp_ln_g", "gmlp_ln_b", "gmlp_w_s", "gmlp_b_s", "conv_w", "conv_b", "lru_w_r", "lru_b_r",
          "lru_w_i", "lru_b_i", "lru_lambda", "w_out", "norm2_g", "w_ffn_in", "w_ffn_out", "final_g"]


def kernel(x, norm1_g, w_in, gmlp_ln_g, gmlp_ln_b, gmlp_w_s, gmlp_b_s, conv_w, conv_b, lru_w_r, lru_b_r, lru_w_i, lru_b_i, lru_lambda, w_out, norm2_g, w_ffn_in, w_ffn_out, final_g, loss_target, m_norm1_g, m_w_in, m_gmlp_ln_g, m_gmlp_ln_b, m_gmlp_w_s, m_gmlp_b_s, m_conv_w, m_conv_b, m_lru_w_r, m_lru_b_r, m_lru_w_i, m_lru_b_i, m_lru_lambda, m_w_out, m_norm2_g, m_w_ffn_in, m_w_ffn_out, m_final_g, v_norm1_g, v_w_in, v_gmlp_ln_g, v_gmlp_ln_b, v_gmlp_w_s, v_gmlp_b_s, v_conv_w, v_conv_b, v_lru_w_r, v_lru_b_r, v_lru_w_i, v_lru_b_i, v_lru_lambda, v_w_out, v_norm2_g, v_w_ffn_in, v_w_ffn_out, v_final_g):
    w = dict(norm1_g=norm1_g, w_in=w_in, gmlp_ln_g=gmlp_ln_g, gmlp_ln_b=gmlp_ln_b, gmlp_w_s=gmlp_w_s, gmlp_b_s=gmlp_b_s,
             conv_w=conv_w, conv_b=conv_b, lru_w_r=lru_w_r, lru_b_r=lru_b_r, lru_w_i=lru_w_i, lru_b_i=lru_b_i,
             lru_lambda=lru_lambda, w_out=w_out, norm2_g=norm2_g, w_ffn_in=w_ffn_in, w_ffn_out=w_ffn_out, final_g=final_g)
    mom = dict(norm1_g=m_norm1_g, w_in=m_w_in, gmlp_ln_g=m_gmlp_ln_g, gmlp_ln_b=m_gmlp_ln_b, gmlp_w_s=m_gmlp_w_s,
               gmlp_b_s=m_gmlp_b_s, conv_w=m_conv_w, conv_b=m_conv_b, lru_w_r=m_lru_w_r, lru_b_r=m_lru_b_r,
               lru_w_i=m_lru_w_i, lru_b_i=m_lru_b_i, lru_lambda=m_lru_lambda, w_out=m_w_out, norm2_g=m_norm2_g,
               w_ffn_in=m_w_ffn_in, w_ffn_out=m_w_ffn_out, final_g=m_final_g)
    var = dict(norm1_g=v_norm1_g, w_in=v_w_in, gmlp_ln_g=v_gmlp_ln_g, gmlp_ln_b=v_gmlp_ln_b, gmlp_w_s=v_gmlp_w_s,
               gmlp_b_s=v_gmlp_b_s, conv_w=v_conv_w, conv_b=v_conv_b, lru_w_r=v_lru_w_r, lru_b_r=v_lru_b_r,
               lru_w_i=v_lru_w_i, lru_b_i=v_lru_b_i, lru_lambda=v_lru_lambda, w_out=v_w_out, norm2_g=v_norm2_g,
               w_ffn_in=v_w_ffn_in, w_ffn_out=v_w_ffn_out, final_g=v_final_g)
    for src in (w, mom, var):
        src["w_ffn_in"] = jnp.swapaxes(src["w_ffn_in"], 1, 2)
    xi, yi, ci = _me()
    me = 4 * xi + 2 * yi + ci

    lane_shapes = [w[k].shape for k in _LANE_SHARDED]
    lane_rows = sum(a[0] * a[1] for a in lane_shapes)
    packed = jnp.concatenate([w[k].reshape(-1, HD) for k in _LANE_SHARDED])
    packed = jnp.pad(packed, ((0, -lane_rows % 8), (0, 0)))

    me1 = jnp.reshape(me, (1,)).astype(jnp.int32)
    gathers = {}
    exchanges = {}
    views = dict(w_in=(N_DEV, D, IN_BLK), w_out=(D, D), w_ffn_in=(2, 4, FF_BLK, D), w_ffn_out=(4, FF_BLK, D))
    small_ex = {}
    small_ag = {}

    casts = {}

    def start_gather(names, l, after):
        lands = [casts[(k, l)] if (k, l) in casts else _cast_into_slot(w[k], l, me1, f"cast_{k}_l{l}") for k in names]
        started, tok = _gather2_start(lands, after, f"gather_start_{'_'.join(names)}_l{l}")
        gathers.update({(k, l): h for k, h in zip(names, started)})
        return tok

    def relay_gather(names, l, after):
        relayed, tok = _gather2_relay([gathers[(k, l)] for k in names], after, f"gather_relay_{'_'.join(names)}_l{l}")
        gathers.update({(k, l): h for k, h in zip(names, relayed)})
        return tok

    def get_w(k, l, after):
        if (k, l) == ("w_in", 1):
            after = relay_gather(_BIG[:1], l, after)
        return _gather2_wait(gathers[(k, l)], after, f"gather_wait_{k}_l{l}").reshape(views[k])

    carried = {("inproj_bwd_dx", 1): [("w_ffn_out", 1), ("w_ffn_in", 1), ("w_out", 1)], ("dw_in", 0): [("w_in", 1)],
               ("inproj_bwd_dx", 0): [("w_ffn_out", 0), ("w_ffn_in", 0), ("w_out", 0)]}
    adam = {}

    def adam_jobs(shards, after):
        for k, l in shards:
            got = _exchange_wait(exchanges[(k, l)], after, f"exchange_wait_{k}_l{l}")
            adam[k] = _AdamJob(got, w[k], mom[k], var[k], l, adam[k].results if k in adam else None)
        return [adam[k] for k, _ in shards]

    def hook(stage, l, payload):
        if stage in ("dw_in", "inproj_bwd_dx"):
            return adam_jobs(carried.get((stage, l), []), payload)
        if stage == "pre_inproj":
            return start_gather(_BIG[1:], l, payload)
        if stage == "pre_gmlp":
            tok = relay_gather(_BIG[1:], l, payload)
            return tok + start_gather(_BIG[:1], l + 1, tok) if l == 0 else tok
        if stage == "small_grads":
            (small_ex[l],), tok = _exchange_start([_pack_small(payload)], f"exchange_start_small_l{l}")
            return tok
        if stage == "mid_backward":
            return reduce_small(l + 1, payload) if l == 0 else None
        if (stage, l) == ("mixer_partials", 0):
            pairs, tok = _pairs_start(payload["w_in"], "pairs_start_w_in_l0")
            p, pair = _pairs_wait(pairs, reduce_small(0, tok), "pairs_wait_w_in_l0")
            sums = _sum_pairs(p, pair, me1, "sum_pairs_w_in_l0")
            exchanges[("w_in", 0)], tok = _chips_start(sums, "chips_start_w_in_l0")
            return tok
        started, tok = _exchange_start(list(payload.values()), f"exchange_start_{'_'.join(payload)}_l{l}")
        exchanges.update({(k, l): h for k, h in zip(payload, started)})
        return tok

    def reduce_small(l, after):
        got = _exchange_wait(small_ex[l], after, f"exchange_wait_small_l{l}")
        mine = _sum8_into_slot(got, me1, f"sum_small_l{l}")
        (small_ag[l],), tok = _gather_start([mine], got, f"gather_start_small_l{l}")
        return tok

    land = lax.dynamic_update_slice(jnp.zeros((N_DEV,) + packed.shape, F32), packed[None], (me, 0, 0))
    (lanes_handle,), token = _gather_start([land], packed, "gather_start_lanes")
    token = start_gather(_BIG[:1], 0, token)
    later = [(k, l) for l in range(2) for k in _BIG if (k, l) != ("w_in", 0)]
    casts.update(zip(later, _cast_all_into_slots([w[k] for k, _ in later], [l for _, l in later], me1, token,
                                                 "cast_later_weights")))
    token = relay_gather(_BIG[:1], 0, casts[later[0]])
    lanes = _gather_wait(lanes_handle, token, "gather_wait_lanes")
    params = {k: w[k] for k in _REPL}
    off = 0
    for k, shp in zip(_LANE_SHARDED, lane_shapes):
        n = shp[0] * shp[1]
        params[k] = jnp.swapaxes(lanes[:, off:off + n], 0, 1).reshape(shp[0], shp[1], D)
        off += n
    loss, dx, dg1 = _local_step(x[0], loss_target[0], params, get_w, hook)

    out = {k: job.results for k, job in adam.items()}
    after = dx
    g_small = [_gather_wait(small_ag[l], after, f"gather_wait_small_l{l}").reshape(SMALL_ROWS, HD) for l in (0, 1)]
    row0 = 0
    for k, rows in SMALL_MATRICES:
        res = _adam_matrix(*g_small, *[src[k].reshape(2, rows, HD) for src in (w, mom, var)], row0, f"adam_{k}")
        out[k] = [a.reshape(w[k].shape) for a in res]
        after = res[3]
        row0 += rows
    got = _chips_wait(exchanges[("w_in", 0)], after, "chips_wait_w_in_l0")
    out["w_in"] = _adam_shard(got, w["w_in"], mom["w_in"], var["w_in"], 0, out["w_in"], "adam_w_in_l0")
    out["w_ffn_in"] = [jnp.swapaxes(a, 1, 2) for a in out["w_ffn_in"]]
    as_rows = lambda a: a.reshape(1, D) if a.ndim == 1 else a
    vec = _adam_vectors(*g_small, _all_gather(dg1, out["w_in"][3], "gather_norm1_grad"), me1,
                        *[{k: as_rows(src[k]) for k, _ in SMALL_VECTORS} for src in (w, mom, var)])
    out.update({k: [a.reshape(w[k].shape) for a in res] for k, res in vec.items()})

    loss = lax.psum(loss[0, 0], MESH_AXES)
    return (loss, dx[None], *[out[k][0] for k in _ORDER], *[out[k][1] for k in _ORDER],
            *[out[k][2] for k in _ORDER], *[out[k][3] for k in _ORDER])
```

```python
import jax
import jax.numpy as jnp
from jax import lax
from jax.experimental import pallas as pl
from jax.experimental.pallas import tpu as pltpu

F32 = jnp.float32
BF16 = jnp.bfloat16
SDS = jax.ShapeDtypeStruct

D = 1024
N_IN = 6 * D
D_FF = 2816
N_DEV = 8
IN_BLK = N_IN // N_DEV
FF_BLK = 2 * D_FF // N_DEV
HEADS = 8
HD = 128
EPS = 1e-6
LRU_C = 8.0
MESH_AXES = ("x", "y", "c")

ADAM_LR = 0.001
ADAM_B1 = 0.9
ADAM_B2 = 0.999
ADAM_EPS = 1e-08
ADAM_WD = 0.01
ADAM_STEP = 10

VMEM_LIMIT = 60 * 2**20


def _cp(*sem, **kw):
    return pltpu.CompilerParams(dimension_semantics=sem, vmem_limit_bytes=VMEM_LIMIT, **kw)


def _row_tile(s):
    return 512 if s >= 1024 else s // 2


_GELU_C = 0.7978845608028654


def _gelu(x):
    t = jnp.tanh(_GELU_C * (x + 0.044715 * (x * x * x)))
    return 0.5 * x * (1.0 + t), t


def _gelu_grad(x, t):
    return 0.5 * (1.0 + t) + 0.5 * x * (1.0 - t * t) * (_GELU_C * (1.0 + 0.134145 * (x * x)))


def _sigmoid(x):
    return 0.5 + 0.5 * jnp.tanh(0.5 * x)


def _softplus(x):
    e = jnp.exp(-jnp.abs(x))
    w = 1.0 + e
    l1p = jnp.where(w == 1.0, e, jnp.log(w) * e / jnp.where(w == 1.0, 1.0, w - 1.0))
    return jnp.maximum(x, 0.0) + l1p


def _rms_fwd(x, g):
    r = lax.rsqrt(jnp.mean(x * x, axis=-1, keepdims=True) + EPS)
    return x * r * g


def _rms_bwd(x, g, dh):
    r = lax.rsqrt(jnp.mean(x * x, axis=-1, keepdims=True) + EPS)
    xh = x * r
    dxh = dh * g
    dx = r * (dxh - xh * jnp.mean(dxh * xh, axis=-1, keepdims=True))
    dg = jnp.sum(dh * xh, axis=0, keepdims=True)
    return dx, dg


LANE_ROWS = D // HD


def _add_rows128(ref, vec, row0=0):
    for i in range(vec.shape[0]):
        for k in range(LANE_ROWS):
            j = row0 + i * LANE_ROWS + k
            ref[j:j + 1, :] += vec[i:i + 1, k * HD:(k + 1) * HD]


def _dot(a, b):
    return jnp.dot(a, b, preferred_element_type=F32)


def _dot_nt(a, b):
    return lax.dot_general(a, b, (((1,), (1,)), ((), ())), preferred_element_type=F32)


def _dot_tn(a, b):
    return lax.dot_general(a, b, (((0,), (0,)), ((), ())), preferred_element_type=F32)


def _taps(prev, cur, nxt, tm):
    hr = prev.shape[0]
    ext = jnp.concatenate([prev, cur, nxt], axis=0)
    n = tm + 2 * hr
    sl = slice(hr, hr + tm)
    return (pltpu.roll(ext, 2, 0)[sl], pltpu.roll(ext, 1, 0)[sl], cur,
            pltpu.roll(ext, n - 1, 0)[sl], pltpu.roll(ext, n - 2, 0)[sl])


def _halo_specs(tm, s, col, rows=8):
    nb = s // rows
    r = tm // rows
    return (pl.BlockSpec((rows, D), lambda i: (jnp.maximum(i * r - 1, 0), col)),
            pl.BlockSpec((tm, D), lambda i: (i, col)),
            pl.BlockSpec((rows, D), lambda i: (jnp.minimum((i + 1) * r, nb - 1), col)))


def _halo_load(prev_ref, cur_ref, next_ref, fp, fn):
    return prev_ref[...].astype(F32) * fp, cur_ref[...].astype(F32), next_ref[...].astype(F32) * fn


def _halo_flags(nt):
    i = pl.program_id(0)
    return (i > 0).astype(F32), (i < nt - 1).astype(F32)


def _full(shape):
    nd = len(shape)
    return pl.BlockSpec(shape, lambda *_: (0,) * nd)


def _resident(shape):
    nd = len(shape)
    return pl.BlockSpec(shape, lambda *_: (0,) * nd, pipeline_mode=pl.Buffered(1))


def _behind(tokens, body, n_in):
    deps = [t for t in tokens if t is not None]

    def ordered(*refs):
        body(*refs[:n_in], *refs[n_in + len(deps):])

    return ordered, [_ANY] * len(deps), deps


def _norm_inproj(x, g, w, layer, tm, after=()):
    s = x.shape[0]

    def body(x_ref, g_ref, w_ref, z_ref, ht_ref):
        h32 = _rms_fwd(x_ref[...], g_ref[...])
        ht_ref[...] = h32.T.astype(BF16)
        h = h32.astype(BF16)
        for j in range(N_DEV):
            z_ref[:, j * IN_BLK:(j + 1) * IN_BLK] = _dot(h, w_ref[j]).astype(BF16)

    body, dep_specs, deps = _behind(after, body, 3)
    return pl.pallas_call(
        body, name=f"norm_inproj_l{layer}", grid=(s // tm,),
        in_specs=[pl.BlockSpec((tm, D), lambda i: (i, 0)), _full((1, D)), _resident((N_DEV, D, IN_BLK))] + dep_specs,
        out_specs=[pl.BlockSpec((tm, N_IN), lambda i: (i, 0)), pl.BlockSpec((D, tm), lambda i: (0, i))],
        out_shape=[SDS((s, N_IN), BF16), SDS((D, s), BF16)],
        compiler_params=_cp("parallel"))(x, g, w, *deps)


def _gmlp_values(zu_ref, zv_ref, lng_ref, lnb_ref):
    zu = zu_ref[...].astype(F32)
    zv = zv_ref[...].astype(F32)
    u, tu = _gelu(zu)
    gv, tv = _gelu(zv)
    xc = gv - jnp.mean(gv, axis=-1, keepdims=True)
    rstd = lax.rsqrt(jnp.mean(xc * xc, axis=-1, keepdims=True) + EPS)
    xh = xc * rstd
    vb = (xh * lng_ref[...] + lnb_ref[...]).astype(BF16)
    return zu, zv, u, tu, tv, xh, rstd, vb


def _mixer_fwd(x, h0, h1, z, lng, lnb, ws, bsb, wo, layer, tm):
    s = x.shape[0]

    def body(x_ref, h0_ref, h1_ref, zu_ref, zv_ref, zg_ref, za_ref, zb_ref, lng_ref, lnb_ref, ws_ref, bsb_ref,
             wo_ref, x1_ref, mg_ref, ya_s):
        _, _, u, _, _, _, _, vb = _gmlp_values(zu_ref, zv_ref, lng_ref, lnb_ref)
        for c in range(tm // HD):
            rs = slice(c * HD, (c + 1) * HD)
            for g in range(HEADS):
                cs = slice(g * HD, (g + 1) * HD)
                ya_s[rs, cs] = u[rs, cs] * (_dot(ws_ref[g], vb[rs, cs]) + bsb_ref[g])
        gg, _ = _gelu(zg_ref[...].astype(F32))
        yb = (h0_ref[...] + h1_ref[...]) * gg
        m32 = _sigmoid(za_ref[...].astype(F32)) * ya_s[...] + _sigmoid(zb_ref[...].astype(F32)) * yb
        mg_ref[...] = m32.T.astype(BF16)
        x1_ref[...] = x_ref[...] + _dot(m32.astype(BF16), wo_ref[...])

    tile = pl.BlockSpec((tm, D), lambda i: (i, 0))
    wspec = _full((HEADS, HD, HD))
    return pl.pallas_call(
        body, name=f"mixer_fwd_l{layer}", grid=(s // tm,),
        in_specs=[tile, tile, tile] + [pl.BlockSpec((tm, D), lambda i, c=c: (i, c)) for c in (0, 1, 3, 4, 5)]
        + [_full((1, D)), _full((1, D)), wspec, wspec, _full((D, D))],
        out_specs=[tile, pl.BlockSpec((D, tm), lambda i: (0, i))], out_shape=[SDS((s, D), F32), SDS((D, s), BF16)],
        scratch_shapes=[pltpu.VMEM((tm, D), F32)],
        compiler_params=_cp("parallel"))(x, h0, h1, z, z, z, z, z, lng, lnb, ws, bsb, wo)


def _conv(taps, cw_ref, cb_ref):
    _, m1, c0, p1, p2 = taps
    return cb_ref[...] + m1 * cw_ref[0:1, :] + c0 * cw_ref[1:2, :] + p1 * cw_ref[2:3, :] + p2 * cw_ref[3:4, :]


def _heads_dot(xb, w_ref, d):
    return jnp.concatenate([_dot(xb[:, h * HD:(h + 1) * HD], w_ref[d, h]) for h in range(HEADS)], axis=1)


def _lru_decay(r, sp):
    la = (-LRU_C) * r * sp
    a = jnp.exp(la)
    return a, jnp.tanh(-la) * (a * a + 1.0)


def _lru_gates_fwd(z, cw, cb, wr, wi, br, bi, lam, layer, tm):
    s = z.shape[0]
    nt = s // tm

    def body(zp_ref, zc_ref, zn_ref, cw_ref, cb_ref, wr_ref, wi_ref, br_ref, bi_ref, lam_ref,
             a0_ref, b0_ref, a1_ref, b1_ref, xc_ref, r0_ref, i0_ref, r1_ref, i1_ref):
        fp, fn = _halo_flags(nt)
        xc = _conv(_taps(*_halo_load(zp_ref, zc_ref, zn_ref, fp, fn), tm), cw_ref, cb_ref)
        xb = xc.astype(BF16)
        xc_ref[...] = xb
        for d, (a_ref, b_ref, r_ref, i_ref) in enumerate(((a0_ref, b0_ref, r0_ref, i0_ref),
                                                          (a1_ref, b1_ref, r1_ref, i1_ref))):
            r = _sigmoid(_heads_dot(xb, wr_ref, d) + br_ref[d:d + 1, :])
            ig = _sigmoid(_heads_dot(xb, wi_ref, d) + bi_ref[d:d + 1, :])
            a, q = _lru_decay(r, _softplus(-lam_ref[d:d + 1, :]))
            a_ref[...] = a
            b_ref[...] = jnp.sqrt(q) * (ig * xc)
            r_ref[...] = r.astype(BF16)
            i_ref[...] = ig.astype(BF16)

    tile = pl.BlockSpec((tm, D), lambda i: (i, 0))
    return pl.pallas_call(
        body, name=f"lru_gates_fwd_l{layer}", grid=(nt,),
        in_specs=[*_halo_specs(tm, s, 2, 16), _full((4, D)), _full((1, D)),
                  _full((2, HEADS, HD, HD)), _full((2, HEADS, HD, HD)), _full((2, D)), _full((2, D)), _full((2, D))],
        out_specs=[tile] * 9, out_shape=[SDS((s, D), F32)] * 4 + [SDS((s, D), BF16)] * 5,
        compiler_params=_cp("parallel"))(z, z, z, cw, cb, wr, wi, br, bi, lam)


def _scan_group(a, x, c, reverse, bwd):
    row = lax.broadcasted_iota(jnp.int32, a.shape, 0)
    b = a * x if bwd else x
    for d in (1, 2, 4):
        keep = (row < 8 - d) if reverse else (row >= d)
        sh = 8 - d if reverse else d
        a_s = jnp.where(keep, pltpu.roll(a, sh, 0), 1.0)
        b_s = jnp.where(keep, pltpu.roll(b, sh, 0), 0.0)
        b = a * b_s + b
        a = a * a_s
    h = b + a * c
    new_c = h[0:1, :] if reverse else h[7:8, :]
    if not bwd:
        return h, new_c
    if reverse:
        prev = jnp.where(row < 7, pltpu.roll(h, 7, 0), c)
    else:
        prev = jnp.where(row >= 1, pltpu.roll(h, 1, 0), c)
    return x + prev, new_c


def _lru_scan(a_f, x_f, a_r, x_r, bwd, layer):
    s = a_f.shape[0]
    ts = min(1024, s // 2)
    cb = 512
    nt = s // ts
    ng = ts // 8

    def body(af_ref, xf_ref, ar_ref, xr_ref, of_ref, or_ref, cf, cr):
        @pl.when(pl.program_id(1) == 0)
        def _():
            cf[...] = jnp.zeros_like(cf)
            cr[...] = jnp.zeros_like(cr)

        def step(j, carry):
            c_f, c_r = carry
            rf = pl.multiple_of(j * 8, 8)
            rr = pl.multiple_of((ng - 1 - j) * 8, 8)
            o, c_f = _scan_group(af_ref[pl.ds(rf, 8), :], xf_ref[pl.ds(rf, 8), :], c_f, False, bwd)
            of_ref[pl.ds(rf, 8), :] = o
            o, c_r = _scan_group(ar_ref[pl.ds(rr, 8), :], xr_ref[pl.ds(rr, 8), :], c_r, True, bwd)
            or_ref[pl.ds(rr, 8), :] = o
            return c_f, c_r

        c_f, c_r = lax.fori_loop(0, ng, step, (cf[0:1, :], cr[0:1, :]), unroll=2)
        cf[...] = jnp.broadcast_to(c_f, cf.shape)
        cr[...] = jnp.broadcast_to(c_r, cr.shape)

    fwd = pl.BlockSpec((ts, cb), lambda c, t: (t, c))
    rev = pl.BlockSpec((ts, cb), lambda c, t: (nt - 1 - t, c))
    return pl.pallas_call(
        body, name=f"lru_scan_{'bwd' if bwd else 'fwd'}_l{layer}", grid=(D // cb, nt),
        in_specs=[fwd, fwd, rev, rev], out_specs=[fwd, rev],
        out_shape=[SDS((s, D), F32)] * 2,
        scratch_shapes=[pltpu.VMEM((8, cb), F32), pltpu.VMEM((8, cb), F32)],
        compiler_params=_cp("parallel", "arbitrary"))(a_f, x_f, a_r, x_r)


def _ffn_fwd(x1, g, wfi, wfo, layer, tm, head=None):
    s = x1.shape[0]

    def ffn(x_ref, g_ref, wi_ref, wo_ref, ff_ref, dff_ref, h_ref):
        x = x_ref[...]
        h = _rms_fwd(x, g_ref[...]).astype(BF16)
        h_ref[...] = h
        acc = x
        for k in range(4):
            gate = _dot_nt(h, wi_ref[0, k])
            up = _dot_nt(h, wi_ref[1, k])
            sg = _sigmoid(gate)
            silu = gate * sg
            ff = (silu * up).astype(BF16)
            ff_ref[k] = ff
            dff_ref[0, k] = (up * (sg * (1.0 + gate * (1.0 - sg)))).astype(BF16)
            dff_ref[1, k] = silu.astype(BF16)
            acc = acc + _dot(ff, wo_ref[k])
        return acc

    def body(x_ref, g_ref, wi_ref, wo_ref, x2_ref, ff_ref, dff_ref, h_ref):
        x2_ref[...] = ffn(x_ref, g_ref, wi_ref, wo_ref, ff_ref, dff_ref, h_ref)

    def body_with_head(x_ref, g_ref, wi_ref, wo_ref, fg_ref, t_ref, dx_ref, loss_ref, dfg_ref, ff_ref, dff_ref, h_ref):
        @pl.when(pl.program_id(0) == 0)
        def _():
            loss_ref[...] = jnp.zeros_like(loss_ref)
            dfg_ref[...] = jnp.zeros_like(dfg_ref)

        x2 = ffn(x_ref, g_ref, wi_ref, wo_ref, ff_ref, dff_ref, h_ref)
        fg = fg_ref[...]
        e = _rms_fwd(x2, fg) - t_ref[...]
        rows = jnp.sum(e * e, axis=-1, keepdims=True)
        loss_ref[...] += (0.5 / D) * jnp.sum(rows, axis=0, keepdims=True)
        dx, dg = _rms_bwd(x2, fg, e * (1.0 / D))
        dx_ref[...] = dx
        _add_rows128(dfg_ref, dg)

    tile = pl.BlockSpec((tm, D), lambda i: (i, 0))
    weights = [_resident((2, 4, FF_BLK, D)), _resident((4, FF_BLK, D))]
    kept_specs = [pl.BlockSpec((4, tm, FF_BLK), lambda i: (0, i, 0)),
                  pl.BlockSpec((2, 4, tm, FF_BLK), lambda i: (0, 0, i, 0)), tile]
    kept_shapes = [SDS((4, s, FF_BLK), BF16), SDS((2, 4, s, FF_BLK), BF16), SDS((s, D), BF16)]
    if head is None:
        return pl.pallas_call(
            body, name=f"ffn_fwd_l{layer}", grid=(s // tm,),
            in_specs=[tile, _full((1, D))] + weights, out_specs=[tile] + kept_specs,
            out_shape=[SDS((s, D), F32)] + kept_shapes, compiler_params=_cp("parallel"))(x1, g, wfi, wfo)
    final_g, tgt = head
    return pl.pallas_call(
        body_with_head, name=f"ffn_fwd_loss_l{layer}", grid=(s // tm,),
        in_specs=[tile, _full((1, D))] + weights + [_full((1, D)), tile],
        out_specs=[tile, _full((1, 1)), _full((LANE_ROWS, HD))] + kept_specs,
        out_shape=[SDS((s, D), F32), SDS((1, 1), F32), SDS((LANE_ROWS, HD), F32)] + kept_shapes,
        compiler_params=_cp("arbitrary"))(x1, g, wfi, wfo, final_g, tgt)


def _ffn_bwd(dx2, wfo, factors, wfi, x1, g, layer, tm):
    s = dx2.shape[0]

    def body(dx_ref, wo_ref, f_ref, wi_ref, x_ref, g_ref, dgu_ref, dx1_ref, dg_ref):
        @pl.when(pl.program_id(0) == 0)
        def _():
            dg_ref[...] = jnp.zeros_like(dg_ref)

        dx = dx_ref[...]
        dxb = dx.astype(BF16)
        dh = None
        for k in range(4):
            dff = _dot_nt(dxb, wo_ref[k])
            d_gate = (dff * f_ref[0, k].astype(F32)).astype(BF16)
            d_up = (dff * f_ref[1, k].astype(F32)).astype(BF16)
            dgu_ref[0, k] = d_gate
            dgu_ref[1, k] = d_up
            part = _dot(d_gate, wi_ref[k]) + _dot(d_up, wi_ref[4 + k])
            dh = part if dh is None else dh + part
        dxn, dg = _rms_bwd(x_ref[...], g_ref[...], dh)
        dx1_ref[...] = dx + dxn
        _add_rows128(dg_ref, dg)

    tile = pl.BlockSpec((tm, D), lambda i: (i, 0))
    blk = pl.BlockSpec((2, 4, tm, FF_BLK), lambda i: (0, 0, i, 0))
    return pl.pallas_call(
        body, name=f"ffn_bwd_l{layer}", grid=(s // tm,),
        in_specs=[tile, _resident((4, FF_BLK, D)), blk, _resident((N_DEV, FF_BLK, D)), tile, _full((1, D))],
        out_specs=[blk, tile, _full((LANE_ROWS, HD))],
        out_shape=[SDS((2, 4, s, FF_BLK), BF16), SDS((s, D), F32), SDS((LANE_ROWS, HD), F32)],
        compiler_params=_cp("arbitrary"))(dx2, wfo, factors, wfi, x1, g)


def _mm_nt_rms_bwd(a, a_spec, a_blocks, w, w_is_transposed, x, g, dres, name, tm, after=(), jobs=()):
    s = x.shape[0]

    def body(a_ref, w_ref, x_ref, g_ref, dres_ref, dx_ref, dg_ref):
        @pl.when(pl.program_id(0) == 0)
        def _():
            dg_ref[...] = jnp.zeros_like(dg_ref)

        dh = None
        for k, blk in enumerate(a_blocks(a_ref)):
            part = _dot(blk, w_ref[k]) if w_is_transposed else _dot_nt(blk, w_ref[k])
            dh = part if dh is None else dh + part
        dx, dg = _rms_bwd(x_ref[...], g_ref[...], dh)
        dx_ref[...] = dres_ref[...] + dx
        _add_rows128(dg_ref, dg)

    tile = pl.BlockSpec((tm, D), lambda i: (i, 0))
    body, dep_specs, deps = _behind(after, body, 5)
    body, job_in, job_args, job_out, job_shapes, aliases = _carry(jobs, s // tm, body, 5 + len(deps), 2)
    res = pl.pallas_call(
        body, name=name, grid=(s // tm,),
        in_specs=[a_spec, _resident(w.shape), tile, _full((1, D)), tile] + dep_specs + job_in,
        out_specs=[tile, _full((LANE_ROWS, HD))] + job_out,
        out_shape=[SDS((s, D), F32), SDS((LANE_ROWS, HD), F32)] + job_shapes,
        input_output_aliases=aliases, compiler_params=_cp("arbitrary"))(a, w, x, g, dres, *deps, *job_args)
    for j, job in enumerate(jobs):
        job.results = res[2 + 4 * j:6 + 4 * j]
    return res[0], res[1]


def _mm_tn(a, a_spec, b, b_spec, nb, out_shape, out_spec, name, a_is_transposed=True, after=None, jobs=()):
    def body(a_ref, b_ref, *rest):
        o_ref = rest[-1]
        bb = b_ref[...].astype(BF16)
        o_ref[...] = (_dot(a_ref[...], bb) if a_is_transposed else _dot_tn(a_ref[...], bb)).astype(BF16)

    deps = [] if after is None else [after]
    body, job_in, job_args, job_out, job_shapes, aliases = _carry(jobs, nb, body, 2 + len(deps), 1)
    res = pl.pallas_call(
        body, name=name, grid=(nb,), in_specs=[a_spec, b_spec] + [_ANY] * len(deps) + job_in,
        out_specs=[out_spec] + job_out, out_shape=[SDS(out_shape, BF16)] + job_shapes,
        input_output_aliases=aliases, compiler_params=_cp("parallel"))(a, b, *deps, *job_args)
    for j, job in enumerate(jobs):
        job.results = res[1 + 4 * j:5 + 4 * j]
    return res[0]


def _mixer_bwd(dx1, wo, h0, h1, z, lng, lnb, ws, wst, bsb, layer, tm, after=()):
    s = dx1.shape[0]
    nt = s // tm

    def body(dx_ref, wo_ref, h0_ref, h1_ref, zu_ref, zv_ref, zg_ref, za_ref, zb_ref, lng_ref, lnb_ref,
             ws_ref, wst_ref, bsb_ref, dz_ref, dh_ref, dws_ref, dbs_ref, dlng_ref, dlnb_ref,
             du_s, dv_s, ya_s, dbs_acc):
        i = pl.program_id(0)

        @pl.when(i == 0)
        def _():
            for r in (dws_ref, dlng_ref, dlnb_ref, dbs_acc):
                r[...] = jnp.zeros_like(r)

        dm = _dot_nt(dx_ref[...].astype(BF16), wo_ref[...])
        sa = _sigmoid(za_ref[...].astype(F32))
        sb = _sigmoid(zb_ref[...].astype(F32))
        zg = zg_ref[...].astype(F32)
        gg, tg = _gelu(zg)
        hs = h0_ref[...] + h1_ref[...]
        dyb = dm * sb
        dya = dm * sa
        dh_ref[...] = dyb * gg
        dz_ref[:, 2 * D:3 * D] = jnp.zeros((tm, D), BF16)
        dz_ref[:, 3 * D:4 * D] = (dyb * hs * _gelu_grad(zg, tg)).astype(BF16)
        dz_ref[:, 5 * D:6 * D] = (dm * (hs * gg) * (sb * (1.0 - sb))).astype(BF16)

        zu, zv, u, tu, tv, xh, rstd, vb = _gmlp_values(zu_ref, zv_ref, lng_ref, lnb_ref)
        for c in range(tm // HD):
            rs = slice(c * HD, (c + 1) * HD)
            for g in range(HEADS):
                cs = slice(g * HD, (g + 1) * HD)
                vblk = vb[rs, cs]
                mixed = _dot(ws_ref[g], vblk) + bsb_ref[g]
                ya_s[rs, cs] = u[rs, cs] * mixed
                du_s[rs, cs] = dya[rs, cs] * mixed
                dmx = dya[rs, cs] * u[rs, cs]
                dbs_acc[g] += dmx
                dmxb = dmx.astype(BF16)
                dws_ref[g] += _dot_nt(dmxb, vblk)
                dv_s[rs, cs] = _dot(wst_ref[g], dmxb)
        dz_ref[:, 4 * D:5 * D] = (dm * ya_s[...] * (sa * (1.0 - sa))).astype(BF16)
        dv = dv_s[...]
        _add_rows128(dlng_ref, jnp.sum(dv * xh, axis=0, keepdims=True))
        _add_rows128(dlnb_ref, jnp.sum(dv, axis=0, keepdims=True))
        dxh = dv * lng_ref[...]
        dgv = rstd * (dxh - jnp.mean(dxh, axis=-1, keepdims=True)
                      - xh * jnp.mean(dxh * xh, axis=-1, keepdims=True))
        dz_ref[:, 0:D] = (du_s[...] * _gelu_grad(zu, tu)).astype(BF16)
        dz_ref[:, D:2 * D] = (dgv * _gelu_grad(zv, tv)).astype(BF16)

        @pl.when(i == nt - 1)
        def _():
            for g in range(HEADS):
                dbs_ref[g:g + 1, :] = jnp.sum(dbs_acc[g].T, axis=0, keepdims=True)

    tile = pl.BlockSpec((tm, D), lambda i: (i, 0))
    wspec = _full((HEADS, HD, HD))
    body, dep_specs, deps = _behind(after, body, 14)
    return pl.pallas_call(
        body, name=f"mixer_bwd_l{layer}", grid=(nt,),
        in_specs=[tile, _full((D, D)), tile, tile]
        + [pl.BlockSpec((tm, D), lambda i, c=c: (i, c)) for c in (0, 1, 3, 4, 5)]
        + [_full((1, D)), _full((1, D)), wspec, wspec, wspec] + dep_specs,
        out_specs=[pl.BlockSpec((tm, N_IN), lambda i: (i, 0)), tile, wspec, _full((HEADS, HD)),
                   _full((LANE_ROWS, HD)), _full((LANE_ROWS, HD))],
        out_shape=[SDS((s, N_IN), BF16), SDS((s, D), F32), SDS((HEADS, HD, HD), F32), SDS((HEADS, HD), F32),
                   SDS((LANE_ROWS, HD), F32), SDS((LANE_ROWS, HD), F32)],
        scratch_shapes=[pltpu.VMEM((tm, D), F32)] * 3 + [pltpu.VMEM((HEADS, HD, HD), F32)],
        compiler_params=_cp("arbitrary"))(dx1, wo, h0, h1, z, z, z, z, z, lng, lnb, ws, wst, bsb, *deps)


def _lru_gates_bwd(xcb, gates, h0, h1, g0, g1, wr, wi, lam, layer, tm, after=()):
    s = xcb.shape[0]
    nt = s // tm

    def body(xc_ref, r0_ref, i0_ref, r1_ref, i1_ref, h0p_ref, h0_ref, h1_ref, h1n_ref, g0_ref, g1_ref,
             wr_ref, wi_ref, lam_ref, dxc_ref, dwr_ref, dwi_ref, dbr_ref, dbi_ref, dlam_ref):
        i = pl.program_id(0)
        fp, fn = _halo_flags(nt)

        @pl.when(i == 0)
        def _():
            for r in (dwr_ref, dwi_ref, dbr_ref, dbi_ref, dlam_ref):
                r[...] = jnp.zeros_like(r)

        xb = xc_ref[...]
        xc = xb.astype(F32)
        zeros8 = jnp.zeros((8, D), F32)
        h_prev = _taps(h0p_ref[...] * fp, h0_ref[...], zeros8, tm)[1]
        h_next = _taps(zeros8, h1_ref[...], h1n_ref[...] * fn, tm)[3]
        dxc = jnp.zeros((tm, D), F32)
        for d, (g_ref, hsh, r_ref, i_ref) in enumerate(((g0_ref, h_prev, r0_ref, i0_ref),
                                                        (g1_ref, h_next, r1_ref, i1_ref))):
            sp = _softplus(-lam_ref[d:d + 1, :])
            r = r_ref[...].astype(F32)
            ig = i_ref[...].astype(F32)
            a, q = _lru_decay(r, sp)
            rmult = jnp.where(q > 0.0, lax.rsqrt(jnp.where(q > 0.0, q, 1.0)), 0.0)
            mult = q * rmult
            db = g_ref[...]
            da = db * hsh
            dmult = db * (ig * xc)
            di = db * (mult * xc)
            dxc = dxc + db * (mult * ig)
            dla = da * a - dmult * (a * a * rmult)
            dsp_dlam = -_sigmoid(-lam_ref[d:d + 1, :])
            _add_rows128(dlam_ref, jnp.sum(dla * r, axis=0, keepdims=True) * ((-LRU_C) * dsp_dlam), d * LANE_ROWS)
            dpr = dla * sp * (-LRU_C) * (r * (1.0 - r))
            dpi = di * (ig * (1.0 - ig))
            _add_rows128(dbr_ref, jnp.sum(dpr, axis=0, keepdims=True), d * LANE_ROWS)
            _add_rows128(dbi_ref, jnp.sum(dpi, axis=0, keepdims=True), d * LANE_ROWS)
            dprb = dpr.astype(BF16)
            dpib = dpi.astype(BF16)
            parts = []
            for h in range(HEADS):
                cs = slice(h * HD, (h + 1) * HD)
                dwr_ref[d, h] += _dot_tn(xb[:, cs], dprb[:, cs])
                dwi_ref[d, h] += _dot_tn(xb[:, cs], dpib[:, cs])
                parts.append(_dot_nt(dprb[:, cs], wr_ref[d, h]) + _dot_nt(dpib[:, cs], wi_ref[d, h]))
            dxc = dxc + jnp.concatenate(parts, axis=1)
        dxc_ref[...] = dxc.astype(BF16)

    tile = pl.BlockSpec((tm, D), lambda i: (i, 0))
    hp, hc, hn = _halo_specs(tm, s, 0)
    wspec = _full((2, HEADS, HD, HD))
    vspec = _full((2 * LANE_ROWS, HD))
    body, dep_specs, deps = _behind(after, body, 14)
    return pl.pallas_call(
        body, name=f"lru_gates_bwd_l{layer}", grid=(nt,),
        in_specs=[tile] * 5 + [hp, hc, hc, hn, tile, tile, wspec, wspec, _full((2, D))] + dep_specs,
        out_specs=[tile, wspec, wspec, vspec, vspec, vspec],
        out_shape=[SDS((s, D), BF16), SDS((2, HEADS, HD, HD), F32), SDS((2, HEADS, HD, HD), F32)]
        + [SDS((2 * LANE_ROWS, HD), F32)] * 3,
        compiler_params=_cp("arbitrary"))(xcb, *gates, h0, h0, h1, h1, g0, g1, wr, wi, lam, *deps)


def _conv_bwd(dz, dxc, z, cw, layer, tm):
    s = z.shape[0]
    nt = s // tm

    def body(dz_in, dp_ref, dc_ref, dn_ref, zp_ref, zc_ref, zn_ref, cw_ref, dz_ref, dcw_ref, dcb_ref):
        del dz_in
        fp, fn = _halo_flags(nt)

        @pl.when(pl.program_id(0) == 0)
        def _():
            dcw_ref[...] = jnp.zeros_like(dcw_ref)
            dcb_ref[...] = jnp.zeros_like(dcb_ref)

        dxc_halo = _halo_load(dp_ref, dc_ref, dn_ref, fp, fn)
        dxc = dxc_halo[1]
        dm2, dm1, _, dp1, _ = _taps(*dxc_halo, tm)
        dz_ref[...] = (cw_ref[0:1, :] * dp1 + cw_ref[1:2, :] * dxc + cw_ref[2:3, :] * dm1
                       + cw_ref[3:4, :] * dm2).astype(BF16)
        _, zm1, z0, zp1, zp2 = _taps(*_halo_load(zp_ref, zc_ref, zn_ref, fp, fn), tm)
        for k, zt in enumerate((zm1, z0, zp1, zp2)):
            _add_rows128(dcw_ref, jnp.sum(dxc * zt, axis=0, keepdims=True), k * LANE_ROWS)
        _add_rows128(dcb_ref, jnp.sum(dxc, axis=0, keepdims=True))

    return pl.pallas_call(
        body, name=f"conv_bwd_l{layer}", grid=(nt,),
        in_specs=[pl.BlockSpec(memory_space=pl.ANY), *_halo_specs(tm, s, 0, 16), *_halo_specs(tm, s, 2, 16),
                  _full((4, D))],
        out_specs=[pl.BlockSpec((tm, D), lambda i: (i, 2)), _full((4 * LANE_ROWS, HD)), _full((LANE_ROWS, HD))],
        out_shape=[SDS((s, N_IN), BF16), SDS((4 * LANE_ROWS, HD), F32), SDS((LANE_ROWS, HD), F32)],
        input_output_aliases={0: 0},
        compiler_params=_cp("arbitrary"))(dz, dxc, dxc, dxc, z, z, z, cw)


def _me():
    return lax.axis_index("x"), lax.axis_index("y"), lax.axis_index("c")


def _peer(m):
    x, y, c = _me()
    px = 1 - x if m & 4 else x
    py = 1 - y if m & 2 else y
    pc = 1 - c if m & 1 else c
    return (px, py, pc), 4 * px + 2 * py + pc


_ANY = pl.BlockSpec(memory_space=pl.ANY)
_EXCHANGE_SEMS = [pltpu.SemaphoreType.DMA((N_DEV - 1,)), pltpu.SemaphoreType.DMA((N_DEV - 1,)), pltpu.SemaphoreType.DMA(())]


def _all_gather(v, after, name):
    def body(v_ref, after_ref, o_ref, send_sems, recv_sems, local_sem):
        del after_ref
        x, y, c = _me()
        me = 4 * x + 2 * y + c
        local = pltpu.make_async_copy(v_ref, o_ref.at[me], local_sem)
        local.start()
        sends = []
        for m in range(1, N_DEV):
            dev, _ = _peer(m)
            cp = pltpu.make_async_remote_copy(v_ref, o_ref.at[me], send_sems.at[m - 1], recv_sems.at[m - 1],
                                              device_id=dev, device_id_type=pl.DeviceIdType.MESH)
            cp.start()
            sends.append(cp)
        for m in range(1, N_DEV):
            dev, blk = _peer(m)
            pltpu.make_async_remote_copy(v_ref, o_ref.at[blk], send_sems.at[m - 1], recv_sems.at[m - 1],
                                         device_id=dev, device_id_type=pl.DeviceIdType.MESH).wait_recv()
        for cp in sends:
            cp.wait_send()
        local.wait()

    return pl.pallas_call(
        body, name=name, in_specs=[_ANY, _ANY], out_specs=_ANY,
        out_shape=SDS((N_DEV,) + v.shape, v.dtype), scratch_shapes=_EXCHANGE_SEMS)(v, after)


_HBM = pl.BlockSpec(memory_space=pltpu.HBM)
_SEM = pl.BlockSpec(memory_space=pltpu.SEMAPHORE)
_EFFECT = pltpu.CompilerParams(has_side_effects=pltpu.SideEffectType.DATAFLOW_SIDE_EFFECTING)
_PEER_SEMS = pltpu.SemaphoreType.DMA((N_DEV - 1,))


def _in_hbm(a):
    return pltpu.with_memory_space_constraint(a, pltpu.HBM)


def _remote(src, dst, send_sems, recv_sems, m):
    dev, _ = _peer(m)
    return pltpu.make_async_remote_copy(src, dst, send_sems.at[m - 1], recv_sems.at[m - 1],
                                        device_id=dev, device_id_type=pl.DeviceIdType.MESH)


def _gather_start(lands, after, name):
    n = len(lands)

    def body(*refs):
        land = refs[:n]
        sems = refs[n + 1:3 * n + 1]
        token = refs[-1]
        x, y, c = _me()
        me = 4 * x + 2 * y + c
        for t in range(n):
            for m in range(1, N_DEV):
                _remote(land[t].at[me], land[t].at[me], sems[2 * t], sems[2 * t + 1], m).start()
        token[...] = jnp.zeros_like(token)

    res = pl.pallas_call(
        body, name=name, in_specs=[_HBM] * n + [_ANY],
        out_specs=[_SEM] * (2 * n) + [_HBM] * n + [pl.BlockSpec(memory_space=pltpu.VMEM)],
        out_shape=[_PEER_SEMS] * (2 * n) + [pltpu.HBM(a.shape, a.dtype) for a in lands] + [SDS((8, 128), F32)],
        input_output_aliases={t: 2 * n + t for t in range(n)},
        compiler_params=_EFFECT)(*[_in_hbm(a) for a in lands], after)
    return [(res[2 * t], res[2 * t + 1], res[2 * n + t]) for t in range(n)], res[-1]


def _gather_wait(handle, after, name):
    send_sems, recv_sems, land = handle

    def body(land_ref, ssem, rsem, after_ref, out_ref):
        del after_ref, out_ref
        x, y, c = _me()
        me = 4 * x + 2 * y + c
        for m in range(1, N_DEV):
            _, blk = _peer(m)
            cp = _remote(land_ref.at[me], land_ref.at[blk], ssem, rsem, m)
            cp.wait_send()
            cp.wait_recv()

    return pl.pallas_call(
        body, name=name, in_specs=[_HBM, _SEM, _SEM, _ANY], out_specs=_HBM,
        out_shape=pltpu.HBM(land.shape, land.dtype), input_output_aliases={0: 0},
        compiler_params=_EFFECT)(land, send_sems, recv_sems, after)


FIRST_STAGE = (1, 2, 4, 6)
RELAYED = (2, 4, 6)
OTHER_CORE = 1


def _stage_copy(src, dst, send_sems, recv_sems, k, m):
    dev, _ = _peer(m)
    return pltpu.make_async_remote_copy(src, dst, send_sems.at[k], recv_sems.at[k],
                                        device_id=dev, device_id_type=pl.DeviceIdType.MESH)


def _gather2_start(lands, after, name):
    n = len(lands)

    def body(*refs):
        land = refs[:n]
        sems = refs[n + 1:3 * n + 1]
        token = refs[-1]
        x, y, c = _me()
        me = 4 * x + 2 * y + c
        for t in range(n):
            for k, m in enumerate(FIRST_STAGE):
                _stage_copy(land[t].at[me], land[t].at[me], sems[2 * t], sems[2 * t + 1], k, m).start()
        token[...] = jnp.zeros_like(token)

    stage_sems = pltpu.SemaphoreType.DMA((len(FIRST_STAGE),))
    res = pl.pallas_call(
        body, name=name, in_specs=[_HBM] * n + [_ANY],
        out_specs=[_SEM] * (2 * n) + [_HBM] * n + [pl.BlockSpec(memory_space=pltpu.VMEM)],
        out_shape=[stage_sems] * (2 * n) + [pltpu.HBM(a.shape, a.dtype) for a in lands] + [SDS((8, 128), F32)],
        input_output_aliases={t: 2 * n + t for t in range(n)},
        compiler_params=_EFFECT)(*[_in_hbm(a) for a in lands], after)
    return [(res[2 * t], res[2 * t + 1], res[2 * n + t]) for t in range(n)], res[-1]


def _gather2_relay(handles, after, name):
    n = len(handles)

    def body(*refs):
        land, send1, recv1 = refs[:n], refs[n:2 * n], refs[2 * n:3 * n]
        sems = refs[3 * n + 1:5 * n + 1]
        token = refs[-1]
        x, y, c = _me()
        me = 4 * x + 2 * y + c
        for t in range(n):
            for j, m in enumerate(RELAYED):
                _, blk = _peer(m)
                _stage_copy(land[t].at[me], land[t].at[blk], send1[t], recv1[t], 1 + j, m).wait_recv()
                _stage_copy(land[t].at[blk], land[t].at[blk], sems[2 * t], sems[2 * t + 1], j, OTHER_CORE).start()
        token[...] = jnp.zeros_like(token)

    relay_sems = pltpu.SemaphoreType.DMA((len(RELAYED),))
    lands = [h[2] for h in handles]
    res = pl.pallas_call(
        body, name=name, in_specs=[_HBM] * n + [_SEM] * (2 * n) + [_ANY],
        out_specs=[_SEM] * (2 * n) + [_HBM] * n + [pl.BlockSpec(memory_space=pltpu.VMEM)],
        out_shape=[relay_sems] * (2 * n) + [pltpu.HBM(a.shape, a.dtype) for a in lands] + [SDS((8, 128), F32)],
        input_output_aliases={t: 2 * n + t for t in range(n)},
        compiler_params=_EFFECT)(*lands, *[h[0] for h in handles], *[h[1] for h in handles], after)
    return [(h[0], h[1], res[2 * t], res[2 * t + 1], res[2 * n + t]) for t, h in enumerate(handles)], res[-1]


def _gather2_wait(handle, after, name):
    send1, recv1, send2, recv2, land = handle

    def body(land_ref, s1, r1, s2, r2, after_ref, out_ref):
        del after_ref, out_ref
        x, y, c = _me()
        me = 4 * x + 2 * y + c
        _, other = _peer(OTHER_CORE)
        _stage_copy(land_ref.at[me], land_ref.at[other], s1, r1, 0, OTHER_CORE).wait_recv()
        for k, m in enumerate(FIRST_STAGE):
            _stage_copy(land_ref.at[me], land_ref.at[me], s1, r1, k, m).wait_send()
        for j, m in enumerate(RELAYED):
            _, mine = _peer(m)
            _, theirs = _peer(m ^ OTHER_CORE)
            _stage_copy(land_ref.at[mine], land_ref.at[mine], s2, r2, j, OTHER_CORE).wait_send()
            _stage_copy(land_ref.at[mine], land_ref.at[theirs], s2, r2, j, OTHER_CORE).wait_recv()

    return pl.pallas_call(
        body, name=name, in_specs=[_HBM] + [_SEM] * 4 + [_ANY], out_specs=_HBM,
        out_shape=pltpu.HBM(land.shape, land.dtype), input_output_aliases={0: 0},
        compiler_params=_EFFECT)(land, send1, recv1, send2, recv2, after)


def _exchange_start(ps, name):
    n = len(ps)

    def body(*refs):
        p = refs[:n]
        got = refs[n:2 * n]
        sems = refs[2 * n:5 * n]
        token = refs[-1]
        x, y, c = _me()
        me = 4 * x + 2 * y + c
        for t in range(n):
            pltpu.make_async_copy(p[t].at[me], got[t].at[me], sems[3 * t + 2]).start()
            for m in range(1, N_DEV):
                _, blk = _peer(m)
                _remote(p[t].at[blk], got[t].at[me], sems[3 * t], sems[3 * t + 1], m).start()
        token[...] = jnp.zeros_like(token)

    res = pl.pallas_call(
        body, name=name, in_specs=[_HBM] * (2 * n),
        out_specs=[_SEM] * (3 * n) + [_HBM] * (2 * n) + [pl.BlockSpec(memory_space=pltpu.VMEM)],
        out_shape=[_PEER_SEMS, _PEER_SEMS, pltpu.SemaphoreType.DMA(())] * n
        + [pltpu.HBM(a.shape, a.dtype) for a in ps] * 2 + [SDS((8, 128), F32)],
        input_output_aliases={t: 3 * n + t for t in range(2 * n)},
        compiler_params=_EFFECT)(*[_in_hbm(a) for a in ps], *[_in_hbm(lax.empty(a.shape, a.dtype)) for a in ps])
    return [(res[3 * t], res[3 * t + 1], res[3 * t + 2], res[3 * n + t], res[4 * n + t]) for t in range(n)], res[-1]


def _exchange_wait(handle, after, name):
    send_sems, recv_sems, local_sem, p, got = handle

    def body(p_ref, got_ref, ssem, rsem, lsem, after_ref, p_out, got_out):
        del after_ref, p_out, got_out
        x, y, c = _me()
        me = 4 * x + 2 * y + c
        pltpu.make_async_copy(p_ref.at[me], got_ref.at[me], lsem).wait()
        for m in range(1, N_DEV):
            _, blk = _peer(m)
            cp = _remote(p_ref.at[blk], got_ref.at[blk], ssem, rsem, m)
            cp.wait_send()
            cp.wait_recv()

    return pl.pallas_call(
        body, name=name, in_specs=[_HBM, _HBM, _SEM, _SEM, _SEM, _ANY], out_specs=[_HBM, _HBM],
        out_shape=[pltpu.HBM(p.shape, p.dtype), pltpu.HBM(got.shape, got.dtype)],
        input_output_aliases={0: 0, 1: 1}, compiler_params=_EFFECT)(p, got, send_sems, recv_sems, local_sem, after)[1]


CHIPS = (0, 2, 4, 6)


def _pairs_start(p, name):
    def body(p_ref, pair_ref, ssem, rsem, p_out, pair_out, token):
        del p_out, pair_out
        for k, chip in enumerate(CHIPS):
            _, blk = _peer(chip ^ OTHER_CORE)
            _stage_copy(p_ref.at[blk], pair_ref.at[k], ssem, rsem, k, OTHER_CORE).start()
        token[...] = jnp.zeros_like(token)

    sems = pltpu.SemaphoreType.DMA((len(CHIPS),))
    pair = lax.empty((len(CHIPS),) + p.shape[1:], p.dtype)
    res = pl.pallas_call(
        body, name=name, in_specs=[_HBM] * 2,
        out_specs=[_SEM] * 2 + [_HBM] * 2 + [pl.BlockSpec(memory_space=pltpu.VMEM)],
        out_shape=[sems, sems, pltpu.HBM(p.shape, p.dtype), pltpu.HBM(pair.shape, pair.dtype), SDS((8, 128), F32)],
        input_output_aliases={0: 2, 1: 3}, compiler_params=_EFFECT)(_in_hbm(p), _in_hbm(pair))
    return res[:4], res[4]


def _pairs_wait(handle, after, name):
    send_sems, recv_sems, p, pair = handle

    def body(p_ref, pair_ref, ssem, rsem, after_ref, p_out, pair_out):
        del after_ref, p_out, pair_out
        for k, chip in enumerate(CHIPS):
            _, blk = _peer(chip ^ OTHER_CORE)
            cp = _stage_copy(p_ref.at[blk], pair_ref.at[k], ssem, rsem, k, OTHER_CORE)
            cp.wait_send()
            cp.wait_recv()

    return pl.pallas_call(
        body, name=name, in_specs=[_HBM, _HBM, _SEM, _SEM, _ANY], out_specs=[_HBM, _HBM],
        out_shape=[pltpu.HBM(p.shape, p.dtype), pltpu.HBM(pair.shape, pair.dtype)],
        input_output_aliases={0: 0, 1: 1}, compiler_params=_EFFECT)(p, pair, send_sems, recv_sems, after)


def _sum_pairs(p, pair, me1, name):
    _, r, c = pair.shape
    tr = _row_tile(r)

    def body(me_ref, p_ref, pair_ref, o_ref):
        del me_ref
        o_ref[...] = (p_ref[...].astype(F32) + pair_ref[...].astype(F32)).astype(o_ref.dtype)

    def mine(k, i, me):
        chip = 2 * k
        return (jnp.bitwise_xor(me[0], chip), i, 0)

    assert CHIPS == tuple(2 * k for k in range(len(CHIPS)))
    blk = pl.BlockSpec((None, tr, c), lambda k, i, me: (k, i, 0))
    return pl.pallas_call(
        body, name=name,
        grid_spec=pltpu.PrefetchScalarGridSpec(
            num_scalar_prefetch=1, grid=(len(CHIPS), r // tr),
            in_specs=[pl.BlockSpec((None, tr, c), mine), blk], out_specs=blk),
        out_shape=SDS(pair.shape, pair.dtype), compiler_params=_cp("parallel", "parallel"))(me1, p, pair)


def _chips_start(q, name):
    def body(q_ref, got_ref, ssem, rsem, lsem, q_out, got_out, token):
        del q_out, got_out
        pltpu.make_async_copy(q_ref.at[0], got_ref.at[0], lsem).start()
        for k, chip in enumerate(CHIPS[1:]):
            _stage_copy(q_ref.at[k + 1], got_ref.at[k + 1], ssem, rsem, k, chip).start()
        token[...] = jnp.zeros_like(token)

    sems = pltpu.SemaphoreType.DMA((len(CHIPS) - 1,))
    res = pl.pallas_call(
        body, name=name, in_specs=[_HBM] * 2,
        out_specs=[_SEM] * 3 + [_HBM] * 2 + [pl.BlockSpec(memory_space=pltpu.VMEM)],
        out_shape=[sems, sems, pltpu.SemaphoreType.DMA(()), pltpu.HBM(q.shape, q.dtype), pltpu.HBM(q.shape, q.dtype),
                   SDS((8, 128), F32)],
        input_output_aliases={0: 3, 1: 4}, compiler_params=_EFFECT)(_in_hbm(q), _in_hbm(lax.empty(q.shape, q.dtype)))
    return res[:5], res[5]


def _chips_wait(handle, after, name):
    send_sems, recv_sems, local_sem, q, got = handle

    def body(q_ref, got_ref, ssem, rsem, lsem, after_ref, q_out, got_out):
        del after_ref, q_out, got_out
        pltpu.make_async_copy(q_ref.at[0], got_ref.at[0], lsem).wait()
        for k, chip in enumerate(CHIPS[1:]):
            cp = _stage_copy(q_ref.at[k + 1], got_ref.at[k + 1], ssem, rsem, k, chip)
            cp.wait_send()
            cp.wait_recv()

    return pl.pallas_call(
        body, name=name, in_specs=[_HBM, _HBM, _SEM, _SEM, _SEM, _ANY], out_specs=[_HBM, _HBM],
        out_shape=[pltpu.HBM(q.shape, q.dtype), pltpu.HBM(got.shape, got.dtype)],
        input_output_aliases={0: 0, 1: 1}, compiler_params=_EFFECT)(q, got, send_sems, recv_sems, local_sem, after)[1]


def _cast_into_slot(w, layer, me1, name):
    _, r, c = w.shape
    tr = next(t for t in (512, 352, r) if r % t == 0)

    def body(me_ref, w_ref, o_ref):
        del me_ref
        o_ref[...] = w_ref[...].astype(BF16)

    return pl.pallas_call(
        body, name=name,
        grid_spec=pltpu.PrefetchScalarGridSpec(
            num_scalar_prefetch=1, grid=(r // tr,),
            in_specs=[pl.BlockSpec((None, tr, c), lambda i, me: (layer, i, 0))],
            out_specs=pl.BlockSpec((None, tr, c), lambda i, me: (me[0], i, 0))),
        out_shape=SDS((N_DEV, r, c), BF16), compiler_params=_cp("arbitrary"))(me1, w)


def _cast_all_into_slots(ws, layers, me1, after, name):
    n = len(ws)

    def body(me_ref, *refs):
        del me_ref
        for w_ref, o_ref in zip(refs[:n], refs[n + 1:]):
            o_ref[...] = w_ref[...].astype(BF16)

    return pl.pallas_call(
        body, name=name,
        grid_spec=pltpu.PrefetchScalarGridSpec(
            num_scalar_prefetch=1, grid=(1,),
            in_specs=[pl.BlockSpec((None,) + a.shape[1:], lambda i, me, l=l: (l, 0, 0)) for a, l in zip(ws, layers)]
            + [_ANY],
            out_specs=[pl.BlockSpec((None,) + a.shape[1:], lambda i, me: (me[0], 0, 0)) for a in ws]),
        out_shape=[SDS((N_DEV,) + a.shape[1:], BF16) for a in ws],
        compiler_params=_cp("arbitrary"))(me1, *ws, after)


def _sum8_into_slot(p, me1, name):
    _, r, c = p.shape

    def body(me_ref, p_ref, o_ref):
        del me_ref
        acc = p_ref[0]
        for k in range(1, N_DEV):
            acc = acc + p_ref[k]
        o_ref[...] = acc

    return pl.pallas_call(
        body, name=name,
        grid_spec=pltpu.PrefetchScalarGridSpec(
            num_scalar_prefetch=1, grid=(1,),
            in_specs=[pl.BlockSpec(p.shape, lambda i, me: (0, 0, 0))],
            out_specs=pl.BlockSpec((None, r, c), lambda i, me: (me[0], 0, 0))),
        out_shape=SDS(p.shape, F32), compiler_params=_cp("arbitrary"))(me1, p)


def _adamw(w, g, m, v):
    m = ADAM_B1 * m + (1.0 - ADAM_B1) * g
    v = ADAM_B2 * v + (1.0 - ADAM_B2) * (g * g)
    m_hat = m / (1.0 - ADAM_B1 ** ADAM_STEP)
    v_hat = v / (1.0 - ADAM_B2 ** ADAM_STEP)
    delta = -ADAM_LR * (m_hat / (jnp.sqrt(v_hat) + ADAM_EPS) + ADAM_WD * w)
    return delta, m, v


def _adam_tile(p_ref, w_ref, m_ref, v_ref, g_ref, d_ref, nm_ref, nv_ref):
    g = p_ref[0].astype(F32)
    for k in range(1, p_ref.shape[0]):
        g = g + p_ref[k].astype(F32)
    delta, nm, nv = _adamw(w_ref[...], g, m_ref[...], v_ref[...])
    g_ref[...] = g
    d_ref[...] = delta
    nm_ref[...] = nm
    nv_ref[...] = nv


class _AdamJob:
    def __init__(self, parts, w, m, v, layer, prev):
        self.args = [parts, w, m, v] + list(prev or ())
        self.layer, self.results = layer, None


def _carry(jobs, steps, body, n_in, n_out):
    in_specs, args, out_specs, out_shapes, aliases, n_prevs = [], [], [], [], {}, []
    for j, job in enumerate(jobs):
        _, r, c = job.args[0].shape
        nr = next(n for n in range(steps, 0, -1) if steps % n == 0 and r % (16 * n) == 0)
        nc = steps // nr
        assert c % (128 * nc) == 0
        tile = (r // nr, c // nc)
        blk = pl.BlockSpec((None,) + tile, lambda i, layer=job.layer, nc=nc: (layer, i // nc, i % nc))
        n_prev = len(job.args) - 4
        aliases.update({n_in + len(args) + 4 + k: n_out + 4 * j + k for k in range(n_prev)})
        in_specs += [pl.BlockSpec(job.args[0].shape[:1] + tile, lambda i, nc=nc: (0, i // nc, i % nc)), blk, blk, blk]
        in_specs += [_ANY] * n_prev
        args += job.args
        out_specs += [blk] * 4
        out_shapes += [SDS(job.args[1].shape, F32)] * 4
        n_prevs.append(n_prev)

    def carrying(*refs):
        ins, outs = refs[:n_in + len(args)], refs[n_in + len(args):]
        body(*ins[:n_in], *outs[:n_out])
        k = n_in
        for j, n_prev in enumerate(n_prevs):
            _adam_tile(*ins[k:k + 4], *outs[n_out + 4 * j:n_out + 4 * j + 4])
            k += 4 + n_prev

    return carrying, in_specs, args, out_specs, out_shapes, aliases


def _adam_shard(parts, w, m, v, layer, prev, name):
    n, r, c = parts.shape
    tr = next(t for t in (512, 352, r) if r % t == 0)
    n_prev = 0 if prev is None else 4

    def body(*refs):
        _adam_tile(*refs[:4], *refs[4 + n_prev:])

    blk = pl.BlockSpec((None, tr, c), lambda i: (layer, i, 0))
    return pl.pallas_call(
        body, name=name, grid=(r // tr,),
        in_specs=[pl.BlockSpec((n, tr, c), lambda i: (0, i, 0)), blk, blk, blk] + [_ANY] * n_prev,
        out_specs=[blk] * 4, out_shape=[SDS(w.shape, F32)] * 4,
        input_output_aliases={4 + k: k for k in range(n_prev)},
        compiler_params=_cp("parallel"))(parts, w, m, v, *(prev or ()))


SMALL_MATRICES = [("lru_w_r", 2048), ("lru_w_i", 2048), ("gmlp_w_s", 1024)]
SMALL_VECTORS = [("norm1_g", 8), ("gmlp_ln_g", 8), ("gmlp_ln_b", 8), ("gmlp_b_s", 8), ("conv_w", 32), ("conv_b", 8),
                 ("lru_b_r", 16), ("lru_b_i", 16), ("lru_lambda", 16), ("norm2_g", 8), ("final_g", 8)]
SMALL_VECTOR_ROW0 = sum(n for _, n in SMALL_MATRICES)
SMALL_VECTOR_BLOCK = 256
SMALL_ROWS = SMALL_VECTOR_ROW0 + SMALL_VECTOR_BLOCK


def _pack_small(small):
    parts = [small[k] for k, _ in SMALL_MATRICES]
    parts += [small[k] if k in small else jnp.zeros((n, HD), F32) for k, n in SMALL_VECTORS]
    flat = jnp.concatenate(parts)
    return jnp.pad(flat, ((0, SMALL_ROWS - flat.shape[0]), (0, 0))).reshape(N_DEV, SMALL_ROWS // N_DEV, HD)


def _adam_matrix(g0, g1, w, m, v, row0, name):
    _, rows, _ = w.shape
    tr = 512

    def body(g0_ref, g1_ref, w_ref, m_ref, v_ref, g_ref, d_ref, nm_ref, nv_ref):
        for l, src in enumerate((g0_ref, g1_ref)):
            g = src[...]
            delta, nm, nv = _adamw(w_ref[l], g, m_ref[l], v_ref[l])
            g_ref[l] = g
            d_ref[l] = delta
            nm_ref[l] = nm
            nv_ref[l] = nv

    gspec = pl.BlockSpec((tr, HD), lambda i: (row0 // tr + i, 0))
    blk = pl.BlockSpec((2, tr, HD), lambda i: (0, i, 0))
    return pl.pallas_call(body, name=name, grid=(rows // tr,), in_specs=[gspec, gspec] + [blk] * 3,
                          out_specs=[blk] * 4, out_shape=[SDS(w.shape, F32)] * 4,
                          compiler_params=_cp("parallel"))(g0, g1, w, m, v)


def _adam_vectors(g0, g1, dg1_parts, me1, ws, ms, vs):
    names = [k for k, _ in SMALL_VECTORS]
    n = len(names)

    def lanes(rows8):
        return jnp.concatenate([rows8[k:k + 1, :] for k in range(LANE_ROWS)], axis=1)

    def body(me_ref, g0_ref, g1_ref, dg1_ref, *refs):
        w_refs, m_refs, v_refs = refs[:n], refs[n:2 * n], refs[2 * n:3 * n]
        outs = refs[3 * n:]
        me = me_ref[0]
        g_refs = (g0_ref, g1_ref)

        def emit(i, idx, g):
            delta, nm, nv = _adamw(w_refs[i][idx], g, m_refs[i][idx], v_refs[i][idx])
            for j, val in enumerate((g, delta, nm, nv)):
                outs[4 * i + j][idx] = val

        off = 0
        for i, (name, rows) in enumerate(SMALL_VECTORS):
            for l in range(2):
                row = (slice(l, l + 1), slice(None))
                if name == "final_g":
                    if l == 1:
                        emit(i, (slice(0, 1), slice(None)), lanes(g1_ref[off:off + rows, :]))
                elif name == "norm1_g":
                    if l == 1:
                        emit(i, row, lanes(g0_ref[off:off + rows, :]))
                    else:
                        total = dg1_ref[0]
                        for k in range(1, N_DEV):
                            total = total + dg1_ref[k]
                        emit(i, row, lanes(total))
                elif name == "gmlp_b_s":
                    emit(i, (l,), g_refs[l][off:off + rows, :])
                elif rows == LANE_ROWS:
                    emit(i, row, lanes(g_refs[l][off:off + rows, :]))
                else:
                    for r in range(rows // LANE_ROWS):
                        emit(i, (l, slice(r, r + 1), slice(None)), g_refs[l][pl.ds(off + r * LANE_ROWS + me, 1), :])
            off += rows

    args = [ws[k] for k in names] + [ms[k] for k in names] + [vs[k] for k in names]
    gspec = pl.BlockSpec((SMALL_VECTOR_BLOCK, HD), lambda i, me: (SMALL_VECTOR_ROW0 // SMALL_VECTOR_BLOCK, 0))
    res = pl.pallas_call(
        body, name="adam_vectors",
        grid_spec=pltpu.PrefetchScalarGridSpec(
            num_scalar_prefetch=1, grid=(1,),
            in_specs=[gspec, gspec, _full(dg1_parts.shape)] + [_full(a.shape) for a in args],
            out_specs=[_full(ws[k].shape) for k in names for _ in range(4)]),
        out_shape=[SDS(ws[k].shape, F32) for k in names for _ in range(4)],
        compiler_params=_cp("arbitrary"))(me1, g0, g1, dg1_parts, *args)
    return {k: list(res[4 * i:4 * i + 4]) for i, k in enumerate(names)}


def _local_step(x, tgt, p, get_w, hook=lambda stage, layer, payload: None):
    s = x.shape[0]
    tm = _row_tile(s)
    wsb = p["gmlp_w_s"].astype(BF16)
    wstb = jnp.swapaxes(p["gmlp_w_s"], -1, -2).astype(BF16)
    bsb = jnp.broadcast_to(p["gmlp_b_s"][..., None], p["gmlp_w_s"].shape)
    wrb = p["lru_w_r"].astype(BF16)
    wib = p["lru_w_i"].astype(BF16)
    saved = []
    for l in range(2):
        win = get_w("w_in", l, x)
        z, h1 = _norm_inproj(x, p["norm1_g"][l][None], win, l, tm, after=[hook("pre_inproj", l, win)])
        a0, b0, a1, b1, xcb, *gates = _lru_gates_fwd(z, p["conv_w"][l], p["conv_b"][l][None], wrb[l], wib[l],
                                                     p["lru_b_r"][l], p["lru_b_i"][l], p["lru_lambda"][l], l, tm)
        h0, hr = _lru_scan(a0, b0, a1, b1, False, l)
        token = hook("pre_gmlp", l, h0)
        wout = get_w("w_out", l, h0 if token is None else token)
        x1, mg = _mixer_fwd(x, h0, hr, z, p["gmlp_ln_g"][l][None], p["gmlp_ln_b"][l][None], wsb[l], bsb[l], wout, l, tm)
        wfi = get_w("w_ffn_in", l, x1)
        wfo = get_w("w_ffn_out", l, x1)
        if l == 0:
            x2, ff, dff, h2 = _ffn_fwd(x1, p["norm2_g"][l][None], wfi, wfo, l, tm)
        else:
            dx, loss, dfg, ff, dff, h2 = _ffn_fwd(x1, p["norm2_g"][l][None], wfi, wfo, l, tm,
                                                  head=(p["final_g"][None], tgt))
        saved.append((x, z, h1, a0, a1, h0, hr, x1, mg, ff, dff, h2, win, wout, wfi, wfo, xcb, gates))
        x = x2
    for l in (1, 0):
        x0, z, h1, a0, a1, h0, hr, x1, mg, ff, dff, h2, win, wout, wfi, wfo, xcb, gates = saved[l]
        dgu, dx1, dg2 = _ffn_bwd(dx, wfo, dff, wfi.reshape(N_DEV, FF_BLK, D), x1, p["norm2_g"][l][None], l, tm)
        d_wfo = _mm_tn(ff, pl.BlockSpec((None, s, FF_BLK), lambda j: (j, 0, 0)), dx, _resident((s, D)),
                       4, (4, FF_BLK, D), pl.BlockSpec((None, FF_BLK, D), lambda j: (j, 0, 0)),
                       f"dw_ffn_out_l{l}", a_is_transposed=False)
        dgu8 = dgu.reshape(N_DEV, s, FF_BLK)
        d_wfi = _mm_tn(dgu8, pl.BlockSpec((None, s, FF_BLK), lambda j: (j, 0, 0)), h2, _resident((s, D)),
                       N_DEV, (N_DEV, FF_BLK, D), pl.BlockSpec((None, FF_BLK, D), lambda j: (j, 0, 0)),
                       f"dw_ffn_in_l{l}", a_is_transposed=False)
        d_wout = _mm_tn(mg, _resident((D, s)), dx1, pl.BlockSpec((s, D // 2), lambda j: (0, j)),
                        2, (D, D), pl.BlockSpec((D, D // 2), lambda j: (0, j)), f"dw_out_l{l}")
        token = hook("ffn_partials", l, dict(w_ffn_out=d_wfo.reshape(N_DEV, D_FF // N_DEV, D), w_ffn_in=d_wfi,
                                             w_out=d_wout.reshape(N_DEV, D // N_DEV, D)))
        pending = hook("mid_backward", l, dx1)
        dz, dh, dws, dbs, dlng, dlnb = _mixer_bwd(dx1, wout, h0, hr, z, p["gmlp_ln_g"][l][None], p["gmlp_ln_b"][l][None],
                                                  wsb[l], wstb[l], bsb[l], l, tm, after=[token])
        g1, g0 = _lru_scan(a1, dh, a0, dh, True, l)
        dxc, dwr, dwi, dbr, dbi, dlam = _lru_gates_bwd(
            xcb, gates, h0, hr, g0, g1, wrb[l], wib[l], p["lru_lambda"][l], l, tm, after=[pending])
        dz, dcw, dcb = _conv_bwd(dz, dxc, z, p["conv_w"][l], l, tm)
        small = dict(lru_w_r=dwr.reshape(-1, HD), lru_w_i=dwi.reshape(-1, HD), gmlp_w_s=dws.reshape(-1, HD),
                     gmlp_ln_g=dlng, gmlp_ln_b=dlnb, gmlp_b_s=dbs, conv_w=dcw, conv_b=dcb, lru_b_r=dbr,
                     lru_b_i=dbi, lru_lambda=dlam, norm2_g=dg2)
        if l == 1:
            small["final_g"] = dfg
        else:
            small["norm1_g"] = dg1
        started = hook("small_grads", l, small)
        d_win = _mm_tn(h1, _resident((D, s)), dz, pl.BlockSpec((s, IN_BLK), lambda j: (0, j)),
                       N_DEV, (N_DEV, D, IN_BLK), pl.BlockSpec((None, D, IN_BLK), lambda j: (j, 0, 0)),
                       f"dw_in_l{l}", after=started, jobs=hook("dw_in", l, started) or ())
        token = hook("mixer_partials", l, dict(w_in=d_win))
        dx, dg1 = _mm_nt_rms_bwd(
            dz, pl.BlockSpec((tm, N_IN), lambda i: (i, 0)),
            lambda r: [r[:, k * IN_BLK:(k + 1) * IN_BLK] for k in range(N_DEV)],
            win, False, x0, p["norm1_g"][l][None], dx1, f"inproj_bwd_dx_l{l}", tm,
            after=[token], jobs=hook("inproj_bwd_dx", l, token) or ())
    return loss, dx, dg1


_REPL = ["norm1_g", "gmlp_ln_g", "gmlp_ln_b", "gmlp_w_s", "gmlp_b_s", "conv_b", "lru_w_r", "lru_w_i", "norm2_g", "final_g"]
_LANE_SHARDED = ["conv_w", "lru_b_r", "lru_b_i", "lru_lambda"]
_BIG = ["w_in", "w_out", "w_ffn_in", "w_ffn_out"]
_ORDER = ["norm1_g", "w_in", "gmlp_ln_g", "gmlp_ln_b", "gmlp_w_s", "gmlp_b_s", "conv_w", "conv_b", "lru_w_r", "lru_b_r",
          "lru_w_i", "lru_b_i", "lru_lambda", "w_out", "norm2_g", "w_ffn_in", "w_ffn_out", "final_g"]


def kernel(x, norm1_g, w_in, gmlp_ln_g, gmlp_ln_b, gmlp_w_s, gmlp_b_s, conv_w, conv_b, lru_w_r, lru_b_r, lru_w_i, lru_b_i, lru_lambda, w_out, norm2_g, w_ffn_in, w_ffn_out, final_g, loss_target, m_norm1_g, m_w_in, m_gmlp_ln_g, m_gmlp_ln_b, m_gmlp_w_s, m_gmlp_b_s, m_conv_w, m_conv_b, m_lru_w_r, m_lru_b_r, m_lru_w_i, m_lru_b_i, m_lru_lambda, m_w_out, m_norm2_g, m_w_ffn_in, m_w_ffn_out, m_final_g, v_norm1_g, v_w_in, v_gmlp_ln_g, v_gmlp_ln_b, v_gmlp_w_s, v_gmlp_b_s, v_conv_w, v_conv_b, v_lru_w_r, v_lru_b_r, v_lru_w_i, v_lru_b_i, v_lru_lambda, v_w_out, v_norm2_g, v_w_ffn_in, v_w_ffn_out, v_final_g):
    w = dict(norm1_g=norm1_g, w_in=w_in, gmlp_ln_g=gmlp_ln_g, gmlp_ln_b=gmlp_ln_b, gmlp_w_s=gmlp_w_s, gmlp_b_s=gmlp_b_s,
             conv_w=conv_w, conv_b=conv_b, lru_w_r=lru_w_r, lru_b_r=lru_b_r, lru_w_i=lru_w_i, lru_b_i=lru_b_i,
             lru_lambda=lru_lambda, w_out=w_out, norm2_g=norm2_g, w_ffn_in=w_ffn_in, w_ffn_out=w_ffn_out, final_g=final_g)
    mom = dict(norm1_g=m_norm1_g, w_in=m_w_in, gmlp_ln_g=m_gmlp_ln_g, gmlp_ln_b=m_gmlp_ln_b, gmlp_w_s=m_gmlp_w_s,
               gmlp_b_s=m_gmlp_b_s, conv_w=m_conv_w, conv_b=m_conv_b, lru_w_r=m_lru_w_r, lru_b_r=m_lru_b_r,
               lru_w_i=m_lru_w_i, lru_b_i=m_lru_b_i, lru_lambda=m_lru_lambda, w_out=m_w_out, norm2_g=m_norm2_g,
               w_ffn_in=m_w_ffn_in, w_ffn_out=m_w_ffn_out, final_g=m_final_g)
    var = dict(norm1_g=v_norm1_g, w_in=v_w_in, gmlp_ln_g=v_gmlp_ln_g, gmlp_ln_b=v_gmlp_ln_b, gmlp_w_s=v_gmlp_w_s,
               gmlp_b_s=v_gmlp_b_s, conv_w=v_conv_w, conv_b=v_conv_b, lru_w_r=v_lru_w_r, lru_b_r=v_lru_b_r,
               lru_w_i=v_lru_w_i, lru_b_i=v_lru_b_i, lru_lambda=v_lru_lambda, w_out=v_w_out, norm2_g=v_norm2_g,
               w_ffn_in=v_w_ffn_in, w_ffn_out=v_w_ffn_out, final_g=v_final_g)
    for src in (w, mom, var):
        src["w_ffn_in"] = jnp.swapaxes(src["w_ffn_in"], 1, 2)
    xi, yi, ci = _me()
    me = 4 * xi + 2 * yi + ci

    lane_shapes = [w[k].shape for k in _LANE_SHARDED]
    lane_rows = sum(a[0] * a[1] for a in lane_shapes)
    packed = jnp.concatenate([w[k].reshape(-1, HD) for k in _LANE_SHARDED])
    packed = jnp.pad(packed, ((0, -lane_rows % 8), (0, 0)))

    me1 = jnp.reshape(me, (1,)).astype(jnp.int32)
    gathers = {}
    exchanges = {}
    views = dict(w_in=(N_DEV, D, IN_BLK), w_out=(D, D), w_ffn_in=(2, 4, FF_BLK, D), w_ffn_out=(4, FF_BLK, D))
    small_ex = {}
    small_ag = {}

    casts = {}

    def start_gather(names, l, after):
        lands = [casts[(k, l)] if (k, l) in casts else _cast_into_slot(w[k], l, me1, f"cast_{k}_l{l}") for k in names]
        started, tok = _gather2_start(lands, after, f"gather_start_{'_'.join(names)}_l{l}")
        gathers.update({(k, l): h for k, h in zip(names, started)})
        return tok

    def relay_gather(names, l, after):
        relayed, tok = _gather2_relay([gathers[(k, l)] for k in names], after, f"gather_relay_{'_'.join(names)}_l{l}")
        gathers.update({(k, l): h for k, h in zip(names, relayed)})
        return tok

    def get_w(k, l, after):
        if (k, l) == ("w_in", 1):
            after = relay_gather(_BIG[:1], l, after)
        return _gather2_wait(gathers[(k, l)], after, f"gather_wait_{k}_l{l}").reshape(views[k])

    carried = {("inproj_bwd_dx", 1): [("w_ffn_out", 1), ("w_ffn_in", 1), ("w_out", 1)], ("dw_in", 0): [("w_in", 1)],
               ("inproj_bwd_dx", 0): [("w_ffn_out", 0), ("w_ffn_in", 0), ("w_out", 0)]}
    adam = {}

    def adam_jobs(shards, after):
        for k, l in shards:
            got = _exchange_wait(exchanges[(k, l)], after, f"exchange_wait_{k}_l{l}")
            adam[k] = _AdamJob(got, w[k], mom[k], var[k], l, adam[k].results if k in adam else None)
        return [adam[k] for k, _ in shards]

    def hook(stage, l, payload):
        if stage in ("dw_in", "inproj_bwd_dx"):
            return adam_jobs(carried.get((stage, l), []), payload)
        if stage == "pre_inproj":
            return start_gather(_BIG[1:], l, payload)
        if stage == "pre_gmlp":
            tok = relay_gather(_BIG[1:], l, payload)
            return tok + start_gather(_BIG[:1], l + 1, tok) if l == 0 else tok
        if stage == "small_grads":
            (small_ex[l],), tok = _exchange_start([_pack_small(payload)], f"exchange_start_small_l{l}")
            return tok
        if stage == "mid_backward":
            return reduce_small(l + 1, payload) if l == 0 else None
        if (stage, l) == ("mixer_partials", 0):
            pairs, tok = _pairs_start(payload["w_in"], "pairs_start_w_in_l0")
            p, pair = _pairs_wait(pairs, reduce_small(0, tok), "pairs_wait_w_in_l0")
            sums = _sum_pairs(p, pair, me1, "sum_pairs_w_in_l0")
            exchanges[("w_in", 0)], tok = _chips_start(sums, "chips_start_w_in_l0")
            return tok
        started, tok = _exchange_start(list(payload.values()), f"exchange_start_{'_'.join(payload)}_l{l}")
        exchanges.update({(k, l): h for k, h in zip(payload, started)})
        return tok

    def reduce_small(l, after):
        got = _exchange_wait(small_ex[l], after, f"exchange_wait_small_l{l}")
        mine = _sum8_into_slot(got, me1, f"sum_small_l{l}")
        (small_ag[l],), tok = _gather_start([mine], got, f"gather_start_small_l{l}")
        return tok

    land = lax.dynamic_update_slice(jnp.zeros((N_DEV,) + packed.shape, F32), packed[None], (me, 0, 0))
    (lanes_handle,), token = _gather_start([land], packed, "gather_start_lanes")
    token = start_gather(_BIG[:1], 0, token)
    later = [(k, l) for l in range(2) for k in _BIG if (k, l) != ("w_in", 0)]
    casts.update(zip(later, _cast_all_into_slots([w[k] for k, _ in later], [l for _, l in later], me1, token,
                                                 "cast_later_weights")))
    token = relay_gather(_BIG[:1], 0, casts[later[0]])
    lanes = _gather_wait(lanes_handle, token, "gather_wait_lanes")
    params = {k: w[k] for k in _REPL}
    off = 0
    for k, shp in zip(_LANE_SHARDED, lane_shapes):
        n = shp[0] * shp[1]
        params[k] = jnp.swapaxes(lanes[:, off:off + n], 0, 1).reshape(shp[0], shp[1], D)
        off += n
    loss, dx, dg1 = _local_step(x[0], loss_target[0], params, get_w, hook)

    out = {k: job.results for k, job in adam.items()}
    after = dx
    g_small = [_gather_wait(small_ag[l], after, f"gather_wait_small_l{l}").reshape(SMALL_ROWS, HD) for l in (0, 1)]
    row0 = 0
    for k, rows in SMALL_MATRICES:
        res = _adam_matrix(*g_small, *[src[k].reshape(2, rows, HD) for src in (w, mom, var)], row0, f"adam_{k}")
        out[k] = [a.reshape(w[k].shape) for a in res]
        after = res[3]
        row0 += rows
    got = _chips_wait(exchanges[("w_in", 0)], after, "chips_wait_w_in_l0")
    out["w_in"] = _adam_shard(got, w["w_in"], mom["w_in"], var["w_in"], 0, out["w_in"], "adam_w_in_l0")
    out["w_ffn_in"] = [jnp.swapaxes(a, 1, 2) for a in out["w_ffn_in"]]
    as_rows = lambda a: a.reshape(1, D) if a.ndim == 1 else a
    vec = _adam_vectors(*g_small, _all_gather(dg1, out["w_in"][3], "gather_norm1_grad"), me1,
                        *[{k: as_rows(src[k]) for k, _ in SMALL_VECTORS} for src in (w, mom, var)])
    out.update({k: [a.reshape(w[k].shape) for a in res] for k, res in vec.items()})

    loss = lax.psum(loss[0, 0], MESH_AXES)
    return (loss, dx[None], *[out[k][0] for k in _ORDER], *[out[k][1] for k in _ORDER],
            *[out[k][2] for k in _ORDER], *[out[k][3] for k in _ORDER])
```

```python
import jax
import jax.numpy as jnp
from jax import lax
from jax.experimental import pallas as pl
from jax.experimental.pallas import tpu as pltpu

F32 = jnp.float32
BF16 = jnp.bfloat16
SDS = jax.ShapeDtypeStruct

D = 1024
N_IN = 6 * D
D_FF = 2816
N_DEV = 8
IN_BLK = N_IN // N_DEV
FF_BLK = 2 * D_FF // N_DEV
HEADS = 8
HD = 128
EPS = 1e-6
LRU_C = 8.0
MESH_AXES = ("x", "y", "c")

ADAM_LR = 0.001
ADAM_B1 = 0.9
ADAM_B2 = 0.999
ADAM_EPS = 1e-08
ADAM_WD = 0.01
ADAM_STEP = 10

VMEM_LIMIT = 60 * 2**20


def _cp(*sem, **kw):
    return pltpu.CompilerParams(dimension_semantics=sem, vmem_limit_bytes=VMEM_LIMIT, **kw)


def _row_tile(s):
    return 512 if s >= 1024 else s // 2


_GELU_C = 0.7978845608028654


def _gelu(x):
    t = jnp.tanh(_GELU_C * (x + 0.044715 * (x * x * x)))
    return 0.5 * x * (1.0 + t), t


def _gelu_grad(x, t):
    return 0.5 * (1.0 + t) + 0.5 * x * (1.0 - t * t) * (_GELU_C * (1.0 + 0.134145 * (x * x)))


def _sigmoid(x):
    return 0.5 + 0.5 * jnp.tanh(0.5 * x)


def _softplus(x):
    e = jnp.exp(-jnp.abs(x))
    w = 1.0 + e
    l1p = jnp.where(w == 1.0, e, jnp.log(w) * e / jnp.where(w == 1.0, 1.0, w - 1.0))
    return jnp.maximum(x, 0.0) + l1p


def _rms_fwd(x, g):
    r = lax.rsqrt(jnp.mean(x * x, axis=-1, keepdims=True) + EPS)
    return x * r * g


def _rms_bwd(x, g, dh):
    r = lax.rsqrt(jnp.mean(x * x, axis=-1, keepdims=True) + EPS)
    xh = x * r
    dxh = dh * g
    dx = r * (dxh - xh * jnp.mean(dxh * xh, axis=-1, keepdims=True))
    dg = jnp.sum(dh * xh, axis=0, keepdims=True)
    return dx, dg


LANE_ROWS = D // HD


def _add_rows128(ref, vec, row0=0):
    for i in range(vec.shape[0]):
        for k in range(LANE_ROWS):
            j = row0 + i * LANE_ROWS + k
            ref[j:j + 1, :] += vec[i:i + 1, k * HD:(k + 1) * HD]


def _dot(a, b):
    return jnp.dot(a, b, preferred_element_type=F32)


def _dot_nt(a, b):
    return lax.dot_general(a, b, (((1,), (1,)), ((), ())), preferred_element_type=F32)


def _dot_tn(a, b):
    return lax.dot_general(a, b, (((0,), (0,)), ((), ())), preferred_element_type=F32)


def _taps(prev, cur, nxt, tm):
    hr = prev.shape[0]
    ext = jnp.concatenate([prev, cur, nxt], axis=0)
    n = tm + 2 * hr
    sl = slice(hr, hr + tm)
    return (pltpu.roll(ext, 2, 0)[sl], pltpu.roll(ext, 1, 0)[sl], cur,
            pltpu.roll(ext, n - 1, 0)[sl], pltpu.roll(ext, n - 2, 0)[sl])


def _halo_specs(tm, s, col, rows=8):
    nb = s // rows
    r = tm // rows
    return (pl.BlockSpec((rows, D), lambda i: (jnp.maximum(i * r - 1, 0), col)),
            pl.BlockSpec((tm, D), lambda i: (i, col)),
            pl.BlockSpec((rows, D), lambda i: (jnp.minimum((i + 1) * r, nb - 1), col)))


def _halo_load(prev_ref, cur_ref, next_ref, fp, fn):
    return prev_ref[...].astype(F32) * fp, cur_ref[...].astype(F32), next_ref[...].astype(F32) * fn


def _halo_flags(nt):
    i = pl.program_id(0)
    return (i > 0).astype(F32), (i < nt - 1).astype(F32)


def _full(shape):
    nd = len(shape)
    return pl.BlockSpec(shape, lambda *_: (0,) * nd)


def _resident(shape):
    nd = len(shape)
    return pl.BlockSpec(shape, lambda *_: (0,) * nd, pipeline_mode=pl.Buffered(1))


def _behind(tokens, body, n_in):
    deps = [t for t in tokens if t is not None]

    def ordered(*refs):
        body(*refs[:n_in], *refs[n_in + len(deps):])

    return ordered, [_ANY] * len(deps), deps


def _norm_inproj(x, g, w, layer, tm, after=()):
    s = x.shape[0]

    def body(x_ref, g_ref, w_ref, z_ref, ht_ref):
        h32 = _rms_fwd(x_ref[...], g_ref[...])
        ht_ref[...] = h32.T.astype(BF16)
        h = h32.astype(BF16)
        for j in range(N_DEV):
            z_ref[:, j * IN_BLK:(j + 1) * IN_BLK] = _dot(h, w_ref[j]).astype(BF16)

    body, dep_specs, deps = _behind(after, body, 3)
    return pl.pallas_call(
        body, name=f"norm_inproj_l{layer}", grid=(s // tm,),
        in_specs=[pl.BlockSpec((tm, D), lambda i: (i, 0)), _full((1, D)), _resident((N_DEV, D, IN_BLK))] + dep_specs,
        out_specs=[pl.BlockSpec((tm, N_IN), lambda i: (i, 0)), pl.BlockSpec((D, tm), lambda i: (0, i))],
        out_shape=[SDS((s, N_IN), BF16), SDS((D, s), BF16)],
        compiler_params=_cp("parallel"))(x, g, w, *deps)


def _gmlp_values(zu_ref, zv_ref, lng_ref, lnb_ref):
    zu = zu_ref[...].astype(F32)
    zv = zv_ref[...].astype(F32)
    u, tu = _gelu(zu)
    gv, tv = _gelu(zv)
    xc = gv - jnp.mean(gv, axis=-1, keepdims=True)
    rstd = lax.rsqrt(jnp.mean(xc * xc, axis=-1, keepdims=True) + EPS)
    xh = xc * rstd
    vb = (xh * lng_ref[...] + lnb_ref[...]).astype(BF16)
    return zu, zv, u, tu, tv, xh, rstd, vb


def _mixer_fwd(x, h0, h1, z, lng, lnb, ws, bsb, wo, layer, tm):
    s = x.shape[0]

    def body(x_ref, h0_ref, h1_ref, zu_ref, zv_ref, zg_ref, za_ref, zb_ref, lng_ref, lnb_ref, ws_ref, bsb_ref,
             wo_ref, x1_ref, mg_ref, ya_s):
        _, _, u, _, _, _, _, vb = _gmlp_values(zu_ref, zv_ref, lng_ref, lnb_ref)
        for c in range(tm // HD):
            rs = slice(c * HD, (c + 1) * HD)
            for g in range(HEADS):
                cs = slice(g * HD, (g + 1) * HD)
                ya_s[rs, cs] = u[rs, cs] * (_dot(ws_ref[g], vb[rs, cs]) + bsb_ref[g])
        gg, _ = _gelu(zg_ref[...].astype(F32))
        yb = (h0_ref[...] + h1_ref[...]) * gg
        m32 = _sigmoid(za_ref[...].astype(F32)) * ya_s[...] + _sigmoid(zb_ref[...].astype(F32)) * yb
        mg_ref[...] = m32.T.astype(BF16)
        x1_ref[...] = x_ref[...] + _dot(m32.astype(BF16), wo_ref[...])

    tile = pl.BlockSpec((tm, D), lambda i: (i, 0))
    wspec = _full((HEADS, HD, HD))
    return pl.pallas_call(
        body, name=f"mixer_fwd_l{layer}", grid=(s // tm,),
        in_specs=[tile, tile, tile] + [pl.BlockSpec((tm, D), lambda i, c=c: (i, c)) for c in (0, 1, 3, 4, 5)]
        + [_full((1, D)), _full((1, D)), wspec, wspec, _full((D, D))],
        out_specs=[tile, pl.BlockSpec((D, tm), lambda i: (0, i))], out_shape=[SDS((s, D), F32), SDS((D, s), BF16)],
        scratch_shapes=[pltpu.VMEM((tm, D), F32)],
        compiler_params=_cp("parallel"))(x, h0, h1, z, z, z, z, z, lng, lnb, ws, bsb, wo)


def _conv(taps, cw_ref, cb_ref):
    _, m1, c0, p1, p2 = taps
    return cb_ref[...] + m1 * cw_ref[0:1, :] + c0 * cw_ref[1:2, :] + p1 * cw_ref[2:3, :] + p2 * cw_ref[3:4, :]


def _heads_dot(xb, w_ref, d):
    return jnp.concatenate([_dot(xb[:, h * HD:(h + 1) * HD], w_ref[d, h]) for h in range(HEADS)], axis=1)


def _lru_decay(r, sp):
    la = (-LRU_C) * r * sp
    a = jnp.exp(la)
    return a, jnp.tanh(-la) * (a * a + 1.0)


def _lru_gates_fwd(z, cw, cb, wr, wi, br, bi, lam, layer, tm):
    s = z.shape[0]
    nt = s // tm

    def body(zp_ref, zc_ref, zn_ref, cw_ref, cb_ref, wr_ref, wi_ref, br_ref, bi_ref, lam_ref,
             a0_ref, b0_ref, a1_ref, b1_ref, xc_ref, r0_ref, i0_ref, r1_ref, i1_ref):
        fp, fn = _halo_flags(nt)
        xc = _conv(_taps(*_halo_load(zp_ref, zc_ref, zn_ref, fp, fn), tm), cw_ref, cb_ref)
        xb = xc.astype(BF16)
        xc_ref[...] = xb
        for d, (a_ref, b_ref, r_ref, i_ref) in enumerate(((a0_ref, b0_ref, r0_ref, i0_ref),
                                                          (a1_ref, b1_ref, r1_ref, i1_ref))):
            r = _sigmoid(_heads_dot(xb, wr_ref, d) + br_ref[d:d + 1, :])
            ig = _sigmoid(_heads_dot(xb, wi_ref, d) + bi_ref[d:d + 1, :])
            a, q = _lru_decay(r, _softplus(-lam_ref[d:d + 1, :]))
            a_ref[...] = a
            b_ref[...] = jnp.sqrt(q) * (ig * xc)
            r_ref[...] = r.astype(BF16)
            i_ref[...] = ig.astype(BF16)

    tile = pl.BlockSpec((tm, D), lambda i: (i, 0))
    return pl.pallas_call(
        body, name=f"lru_gates_fwd_l{layer}", grid=(nt,),
        in_specs=[*_halo_specs(tm, s, 2, 16), _full((4, D)), _full((1, D)),
                  _full((2, HEADS, HD, HD)), _full((2, HEADS, HD, HD)), _full((2, D)), _full((2, D)), _full((2, D))],
        out_specs=[tile] * 9, out_shape=[SDS((s, D), F32)] * 4 + [SDS((s, D), BF16)] * 5,
        compiler_params=_cp("parallel"))(z, z, z, cw, cb, wr, wi, br, bi, lam)


def _scan_group(a, x, c, reverse, bwd):
    row = lax.broadcasted_iota(jnp.int32, a.shape, 0)
    b = a * x if bwd else x
    for d in (1, 2, 4):
        keep = (row < 8 - d) if reverse else (row >= d)
        sh = 8 - d if reverse else d
        a_s = jnp.where(keep, pltpu.roll(a, sh, 0), 1.0)
        b_s = jnp.where(keep, pltpu.roll(b, sh, 0), 0.0)
        b = a * b_s + b
        a = a * a_s
    h = b + a * c
    new_c = h[0:1, :] if reverse else h[7:8, :]
    if not bwd:
        return h, new_c
    if reverse:
        prev = jnp.where(row < 7, pltpu.roll(h, 7, 0), c)
    else:
        prev = jnp.where(row >= 1, pltpu.roll(h, 1, 0), c)
    return x + prev, new_c


def _lru_scan(a_f, x_f, a_r, x_r, bwd, layer):
    s = a_f.shape[0]
    ts = min(1024, s // 2)
    cb = 512
    nt = s // ts
    ng = ts // 8

    def body(af_ref, xf_ref, ar_ref, xr_ref, of_ref, or_ref, cf, cr):
        @pl.when(pl.program_id(1) == 0)
        def _():
            cf[...] = jnp.zeros_like(cf)
            cr[...] = jnp.zeros_like(cr)

        def step(j, carry):
            c_f, c_r = carry
            rf = pl.multiple_of(j * 8, 8)
            rr = pl.multiple_of((ng - 1 - j) * 8, 8)
            o, c_f = _scan_group(af_ref[pl.ds(rf, 8), :], xf_ref[pl.ds(rf, 8), :], c_f, False, bwd)
            of_ref[pl.ds(rf, 8), :] = o
            o, c_r = _scan_group(ar_ref[pl.ds(rr, 8), :], xr_ref[pl.ds(rr, 8), :], c_r, True, bwd)
            or_ref[pl.ds(rr, 8), :] = o
            return c_f, c_r

        c_f, c_r = lax.fori_loop(0, ng, step, (cf[0:1, :], cr[0:1, :]), unroll=2)
        cf[...] = jnp.broadcast_to(c_f, cf.shape)
        cr[...] = jnp.broadcast_to(c_r, cr.shape)

    fwd = pl.BlockSpec((ts, cb), lambda c, t: (t, c))
    rev = pl.BlockSpec((ts, cb), lambda c, t: (nt - 1 - t, c))
    return pl.pallas_call(
        body, name=f"lru_scan_{'bwd' if bwd else 'fwd'}_l{layer}", grid=(D // cb, nt),
        in_specs=[fwd, fwd, rev, rev], out_specs=[fwd, rev],
        out_shape=[SDS((s, D), F32)] * 2,
        scratch_shapes=[pltpu.VMEM((8, cb), F32), pltpu.VMEM((8, cb), F32)],
        compiler_params=_cp("parallel", "arbitrary"))(a_f, x_f, a_r, x_r)


def _ffn_fwd(x1, g, wfi, wfo, layer, tm, head=None):
    s = x1.shape[0]

    def ffn(x_ref, g_ref, wi_ref, wo_ref, ff_ref, dff_ref, h_ref):
        x = x_ref[...]
        h = _rms_fwd(x, g_ref[...]).astype(BF16)
        h_ref[...] = h
        acc = x
        for k in range(4):
            gate = _dot_nt(h, wi_ref[0, k])
            up = _dot_nt(h, wi_ref[1, k])
            sg = _sigmoid(gate)
            silu = gate * sg
            ff = (silu * up).astype(BF16)
            ff_ref[k] = ff
            dff_ref[0, k] = (up * (sg * (1.0 + gate * (1.0 - sg)))).astype(BF16)
            dff_ref[1, k] = silu.astype(BF16)
            acc = acc + _dot(ff, wo_ref[k])
        return acc

    def body(x_ref, g_ref, wi_ref, wo_ref, x2_ref, ff_ref, dff_ref, h_ref):
        x2_ref[...] = ffn(x_ref, g_ref, wi_ref, wo_ref, ff_ref, dff_ref, h_ref)

    def body_with_head(x_ref, g_ref, wi_ref, wo_ref, fg_ref, t_ref, dx_ref, loss_ref, dfg_ref, ff_ref, dff_ref, h_ref):
        @pl.when(pl.program_id(0) == 0)
        def _():
            loss_ref[...] = jnp.zeros_like(loss_ref)
            dfg_ref[...] = jnp.zeros_like(dfg_ref)

        x2 = ffn(x_ref, g_ref, wi_ref, wo_ref, ff_ref, dff_ref, h_ref)
        fg = fg_ref[...]
        e = _rms_fwd(x2, fg) - t_ref[...]
        rows = jnp.sum(e * e, axis=-1, keepdims=True)
        loss_ref[...] += (0.5 / D) * jnp.sum(rows, axis=0, keepdims=True)
        dx, dg = _rms_bwd(x2, fg, e * (1.0 / D))
        dx_ref[...] = dx
        _add_rows128(dfg_ref, dg)

    tile = pl.BlockSpec((tm, D), lambda i: (i, 0))
    weights = [_resident((2, 4, FF_BLK, D)), _resident((4, FF_BLK, D))]
    kept_specs = [pl.BlockSpec((4, tm, FF_BLK), lambda i: (0, i, 0)),
                  pl.BlockSpec((2, 4, tm, FF_BLK), lambda i: (0, 0, i, 0)), tile]
    kept_shapes = [SDS((4, s, FF_BLK), BF16), SDS((2, 4, s, FF_BLK), BF16), SDS((s, D), BF16)]
    if head is None:
        return pl.pallas_call(
            body, name=f"ffn_fwd_l{layer}", grid=(s // tm,),
            in_specs=[tile, _full((1, D))] + weights, out_specs=[tile] + kept_specs,
            out_shape=[SDS((s, D), F32)] + kept_shapes, compiler_params=_cp("parallel"))(x1, g, wfi, wfo)
    final_g, tgt = head
    return pl.pallas_call(
        body_with_head, name=f"ffn_fwd_loss_l{layer}", grid=(s // tm,),
        in_specs=[tile, _full((1, D))] + weights + [_full((1, D)), tile],
        out_specs=[tile, _full((1, 1)), _full((LANE_ROWS, HD))] + kept_specs,
        out_shape=[SDS((s, D), F32), SDS((1, 1), F32), SDS((LANE_ROWS, HD), F32)] + kept_shapes,
        compiler_params=_cp("arbitrary"))(x1, g, wfi, wfo, final_g, tgt)


def _ffn_bwd(dx2, wfo, factors, wfi, x1, g, layer, tm):
    s = dx2.shape[0]

    def body(dx_ref, wo_ref, f_ref, wi_ref, x_ref, g_ref, dgu_ref, dx1_ref, dg_ref):
        @pl.when(pl.program_id(0) == 0)
        def _():
            dg_ref[...] = jnp.zeros_like(dg_ref)

        dx = dx_ref[...]
        dxb = dx.astype(BF16)
        dh = None
        for k in range(4):
            dff = _dot_nt(dxb, wo_ref[k])
            d_gate = (dff * f_ref[0, k].astype(F32)).astype(BF16)
            d_up = (dff * f_ref[1, k].astype(F32)).astype(BF16)
            dgu_ref[0, k] = d_gate
            dgu_ref[1, k] = d_up
            part = _dot(d_gate, wi_ref[k]) + _dot(d_up, wi_ref[4 + k])
            dh = part if dh is None else dh + part
        dxn, dg = _rms_bwd(x_ref[...], g_ref[...], dh)
        dx1_ref[...] = dx + dxn
        _add_rows128(dg_ref, dg)

    tile = pl.BlockSpec((tm, D), lambda i: (i, 0))
    blk = pl.BlockSpec((2, 4, tm, FF_BLK), lambda i: (0, 0, i, 0))
    return pl.pallas_call(
        body, name=f"ffn_bwd_l{layer}", grid=(s // tm,),
        in_specs=[tile, _resident((4, FF_BLK, D)), blk, _resident((N_DEV, FF_BLK, D)), tile, _full((1, D))],
        out_specs=[blk, tile, _full((LANE_ROWS, HD))],
        out_shape=[SDS((2, 4, s, FF_BLK), BF16), SDS((s, D), F32), SDS((LANE_ROWS, HD), F32)],
        compiler_params=_cp("arbitrary"))(dx2, wfo, factors, wfi, x1, g)


def _mm_nt_rms_bwd(a, a_spec, a_blocks, w, w_is_transposed, x, g, dres, name, tm, after=(), jobs=()):
    s = x.shape[0]

    def body(a_ref, w_ref, x_ref, g_ref, dres_ref, dx_ref, dg_ref):
        @pl.when(pl.program_id(0) == 0)
        def _():
            dg_ref[...] = jnp.zeros_like(dg_ref)

        dh = None
        for k, blk in enumerate(a_blocks(a_ref)):
            part = _dot(blk, w_ref[k]) if w_is_transposed else _dot_nt(blk, w_ref[k])
            dh = part if dh is None else dh + part
        dx, dg = _rms_bwd(x_ref[...], g_ref[...], dh)
        dx_ref[...] = dres_ref[...] + dx
        _add_rows128(dg_ref, dg)

    tile = pl.BlockSpec((tm, D), lambda i: (i, 0))
    body, dep_specs, deps = _behind(after, body, 5)
    body, job_in, job_args, job_out, job_shapes, aliases = _carry(jobs, s // tm, body, 5 + len(deps), 2)
    res = pl.pallas_call(
        body, name=name, grid=(s // tm,),
        in_specs=[a_spec, _resident(w.shape), tile, _full((1, D)), tile] + dep_specs + job_in,
        out_specs=[tile, _full((LANE_ROWS, HD))] + job_out,
        out_shape=[SDS((s, D), F32), SDS((LANE_ROWS, HD), F32)] + job_shapes,
        input_output_aliases=aliases, compiler_params=_cp("arbitrary"))(a, w, x, g, dres, *deps, *job_args)
    for j, job in enumerate(jobs):
        job.results = res[2 + 4 * j:6 + 4 * j]
    return res[0], res[1]


def _mm_tn(a, a_spec, b, b_spec, nb, out_shape, out_spec, name, a_is_transposed=True, after=None, jobs=()):
    def body(a_ref, b_ref, *rest):
        o_ref = rest[-1]
        bb = b_ref[...].astype(BF16)
        o_ref[...] = (_dot(a_ref[...], bb) if a_is_transposed else _dot_tn(a_ref[...], bb)).astype(BF16)

    deps = [] if after is None else [after]
    body, job_in, job_args, job_out, job_shapes, aliases = _carry(jobs, nb, body, 2 + len(deps), 1)
    res = pl.pallas_call(
        body, name=name, grid=(nb,), in_specs=[a_spec, b_spec] + [_ANY] * len(deps) + job_in,
        out_specs=[out_spec] + job_out, out_shape=[SDS(out_shape, BF16)] + job_shapes,
        input_output_aliases=aliases, compiler_params=_cp("parallel"))(a, b, *deps, *job_args)
    for j, job in enumerate(jobs):
        job.results = res[1 + 4 * j:5 + 4 * j]
    return res[0]


def _mixer_bwd(dx1, wo, h0, h1, z, lng, lnb, ws, wst, bsb, layer, tm, after=()):
    s = dx1.shape[0]
    nt = s // tm

    def body(dx_ref, wo_ref, h0_ref, h1_ref, zu_ref, zv_ref, zg_ref, za_ref, zb_ref, lng_ref, lnb_ref,
             ws_ref, wst_ref, bsb_ref, dz_ref, dh_ref, dws_ref, dbs_ref, dlng_ref, dlnb_ref,
             du_s, dv_s, ya_s, dbs_acc):
        i = pl.program_id(0)

        @pl.when(i == 0)
        def _():
            for r in (dws_ref, dlng_ref, dlnb_ref, dbs_acc):
                r[...] = jnp.zeros_like(r)

        dm = _dot_nt(dx_ref[...].astype(BF16), wo_ref[...])
        sa = _sigmoid(za_ref[...].astype(F32))
        sb = _sigmoid(zb_ref[...].astype(F32))
        zg = zg_ref[...].astype(F32)
        gg, tg = _gelu(zg)
        hs = h0_ref[...] + h1_ref[...]
        dyb = dm * sb
        dya = dm * sa
        dh_ref[...] = dyb * gg
        dz_ref[:, 2 * D:3 * D] = jnp.zeros((tm, D), BF16)
        dz_ref[:, 3 * D:4 * D] = (dyb * hs * _gelu_grad(zg, tg)).astype(BF16)
        dz_ref[:, 5 * D:6 * D] = (dm * (hs * gg) * (sb * (1.0 - sb))).astype(BF16)

        zu, zv, u, tu, tv, xh, rstd, vb = _gmlp_values(zu_ref, zv_ref, lng_ref, lnb_ref)
        for c in range(tm // HD):
            rs = slice(c * HD, (c + 1) * HD)
            for g in range(HEADS):
                cs = slice(g * HD, (g + 1) * HD)
                vblk = vb[rs, cs]
                mixed = _dot(ws_ref[g], vblk) + bsb_ref[g]
                ya_s[rs, cs] = u[rs, cs] * mixed
                du_s[rs, cs] = dya[rs, cs] * mixed
                dmx = dya[rs, cs] * u[rs, cs]
                dbs_acc[g] += dmx
                dmxb = dmx.astype(BF16)
                dws_ref[g] += _dot_nt(dmxb, vblk)
                dv_s[rs, cs] = _dot(wst_ref[g], dmxb)
        dz_ref[:, 4 * D:5 * D] = (dm * ya_s[...] * (sa * (1.0 - sa))).astype(BF16)
        dv = dv_s[...]
        _add_rows128(dlng_ref, jnp.sum(dv * xh, axis=0, keepdims=True))
        _add_rows128(dlnb_ref, jnp.sum(dv, axis=0, keepdims=True))
        dxh = dv * lng_ref[...]
        dgv = rstd * (dxh - jnp.mean(dxh, axis=-1, keepdims=True)
                      - xh * jnp.mean(dxh * xh, axis=-1, keepdims=True))
        dz_ref[:, 0:D] = (du_s[...] * _gelu_grad(zu, tu)).astype(BF16)
        dz_ref[:, D:2 * D] = (dgv * _gelu_grad(zv, tv)).astype(BF16)

        @pl.when(i == nt - 1)
        def _():
            for g in range(HEADS):
                dbs_ref[g:g + 1, :] = jnp.sum(dbs_acc[g].T, axis=0, keepdims=True)

    tile = pl.BlockSpec((tm, D), lambda i: (i, 0))
    wspec = _full((HEADS, HD, HD))
    body, dep_specs, deps = _behind(after, body, 14)
    return pl.pallas_call(
        body, name=f"mixer_bwd_l{layer}", grid=(nt,),
        in_specs=[tile, _full((D, D)), tile, tile]
        + [pl.BlockSpec((tm, D), lambda i, c=c: (i, c)) for c in (0, 1, 3, 4, 5)]
        + [_full((1, D)), _full((1, D)), wspec, wspec, wspec] + dep_specs,
        out_specs=[pl.BlockSpec((tm, N_IN), lambda i: (i, 0)), tile, wspec, _full((HEADS, HD)),
                   _full((LANE_ROWS, HD)), _full((LANE_ROWS, HD))],
        out_shape=[SDS((s, N_IN), BF16), SDS((s, D), F32), SDS((HEADS, HD, HD), F32), SDS((HEADS, HD), F32),
                   SDS((LANE_ROWS, HD), F32), SDS((LANE_ROWS, HD), F32)],
        scratch_shapes=[pltpu.VMEM((tm, D), F32)] * 3 + [pltpu.VMEM((HEADS, HD, HD), F32)],
        compiler_params=_cp("arbitrary"))(dx1, wo, h0, h1, z, z, z, z, z, lng, lnb, ws, wst, bsb, *deps)


def _lru_gates_bwd(xcb, gates, h0, h1, g0, g1, wr, wi, lam, layer, tm, after=()):
    s = xcb.shape[0]
    nt = s // tm

    def body(xc_ref, r0_ref, i0_ref, r1_ref, i1_ref, h0p_ref, h0_ref, h1_ref, h1n_ref, g0_ref, g1_ref,
             wr_ref, wi_ref, lam_ref, dxc_ref, dwr_ref, dwi_ref, dbr_ref, dbi_ref, dlam_ref):
        i = pl.program_id(0)
        fp, fn = _halo_flags(nt)

        @pl.when(i == 0)
        def _():
            for r in (dwr_ref, dwi_ref, dbr_ref, dbi_ref, dlam_ref):
                r[...] = jnp.zeros_like(r)

        xb = xc_ref[...]
        xc = xb.astype(F32)
        zeros8 = jnp.zeros((8, D), F32)
        h_prev = _taps(h0p_ref[...] * fp, h0_ref[...], zeros8, tm)[1]
        h_next = _taps(zeros8, h1_ref[...], h1n_ref[...] * fn, tm)[3]
        dxc = jnp.zeros((tm, D), F32)
        for d, (g_ref, hsh, r_ref, i_ref) in enumerate(((g0_ref, h_prev, r0_ref, i0_ref),
                                                        (g1_ref, h_next, r1_ref, i1_ref))):
            sp = _softplus(-lam_ref[d:d + 1, :])
            r = r_ref[...].astype(F32)
            ig = i_ref[...].astype(F32)
            a, q = _lru_decay(r, sp)
            rmult = jnp.where(q > 0.0, lax.rsqrt(jnp.where(q > 0.0, q, 1.0)), 0.0)
            mult = q * rmult
            db = g_ref[...]
            da = db * hsh
            dmult = db * (ig * xc)
            di = db * (mult * xc)
            dxc = dxc + db * (mult * ig)
            dla = da * a - dmult * (a * a * rmult)
            dsp_dlam = -_sigmoid(-lam_ref[d:d + 1, :])
            _add_rows128(dlam_ref, jnp.sum(dla * r, axis=0, keepdims=True) * ((-LRU_C) * dsp_dlam), d * LANE_ROWS)
            dpr = dla * sp * (-LRU_C) * (r * (1.0 - r))
            dpi = di * (ig * (1.0 - ig))
            _add_rows128(dbr_ref, jnp.sum(dpr, axis=0, keepdims=True), d * LANE_ROWS)
            _add_rows128(dbi_ref, jnp.sum(dpi, axis=0, keepdims=True), d * LANE_ROWS)
            dprb = dpr.astype(BF16)
            dpib = dpi.astype(BF16)
            parts = []
            for h in range(HEADS):
                cs = slice(h * HD, (h + 1) * HD)
                dwr_ref[d, h] += _dot_tn(xb[:, cs], dprb[:, cs])
                dwi_ref[d, h] += _dot_tn(xb[:, cs], dpib[:, cs])
                parts.append(_dot_nt(dprb[:, cs], wr_ref[d, h]) + _dot_nt(dpib[:, cs], wi_ref[d, h]))
            dxc = dxc + jnp.concatenate(parts, axis=1)
        dxc_ref[...] = dxc.astype(BF16)

    tile = pl.BlockSpec((tm, D), lambda i: (i, 0))
    hp, hc, hn = _halo_specs(tm, s, 0)
    wspec = _full((2, HEADS, HD, HD))
    vspec = _full((2 * LANE_ROWS, HD))
    body, dep_specs, deps = _behind(after, body, 14)
    return pl.pallas_call(
        body, name=f"lru_gates_bwd_l{layer}", grid=(nt,),
        in_specs=[tile] * 5 + [hp, hc, hc, hn, tile, tile, wspec, wspec, _full((2, D))] + dep_specs,
        out_specs=[tile, wspec, wspec, vspec, vspec, vspec],
        out_shape=[SDS((s, D), BF16), SDS((2, HEADS, HD, HD), F32), SDS((2, HEADS, HD, HD), F32)]
        + [SDS((2 * LANE_ROWS, HD), F32)] * 3,
        compiler_params=_cp("arbitrary"))(xcb, *gates, h0, h0, h1, h1, g0, g1, wr, wi, lam, *deps)


def _conv_bwd(dz, dxc, z, cw, layer, tm):
    s = z.shape[0]
    nt = s // tm

    def body(dz_in, dp_ref, dc_ref, dn_ref, zp_ref, zc_ref, zn_ref, cw_ref, dz_ref, dcw_ref, dcb_ref):
        del dz_in
        fp, fn = _halo_flags(nt)

        @pl.when(pl.program_id(0) == 0)
        def _():
            dcw_ref[...] = jnp.zeros_like(dcw_ref)
            dcb_ref[...] = jnp.zeros_like(dcb_ref)

        dxc_halo = _halo_load(dp_ref, dc_ref, dn_ref, fp, fn)
        dxc = dxc_halo[1]
        dm2, dm1, _, dp1, _ = _taps(*dxc_halo, tm)
        dz_ref[...] = (cw_ref[0:1, :] * dp1 + cw_ref[1:2, :] * dxc + cw_ref[2:3, :] * dm1
                       + cw_ref[3:4, :] * dm2).astype(BF16)
        _, zm1, z0, zp1, zp2 = _taps(*_halo_load(zp_ref, zc_ref, zn_ref, fp, fn), tm)
        for k, zt in enumerate((zm1, z0, zp1, zp2)):
            _add_rows128(dcw_ref, jnp.sum(dxc * zt, axis=0, keepdims=True), k * LANE_ROWS)
        _add_rows128(dcb_ref, jnp.sum(dxc, axis=0, keepdims=True))

    return pl.pallas_call(
        body, name=f"conv_bwd_l{layer}", grid=(nt,),
        in_specs=[pl.BlockSpec(memory_space=pl.ANY), *_halo_specs(tm, s, 0, 16), *_halo_specs(tm, s, 2, 16),
                  _full((4, D))],
        out_specs=[pl.BlockSpec((tm, D), lambda i: (i, 2)), _full((4 * LANE_ROWS, HD)), _full((LANE_ROWS, HD))],
        out_shape=[SDS((s, N_IN), BF16), SDS((4 * LANE_ROWS, HD), F32), SDS((LANE_ROWS, HD), F32)],
        input_output_aliases={0: 0},
        compiler_params=_cp("arbitrary"))(dz, dxc, dxc, dxc, z, z, z, cw)


def _me():
    return lax.axis_index("x"), lax.axis_index("y"), lax.axis_index("c")


def _peer(m):
    x, y, c = _me()
    px = 1 - x if m & 4 else x
    py = 1 - y if m & 2 else y
    pc = 1 - c if m & 1 else c
    return (px, py, pc), 4 * px + 2 * py + pc


_ANY = pl.BlockSpec(memory_space=pl.ANY)
_EXCHANGE_SEMS = [pltpu.SemaphoreType.DMA((N_DEV - 1,)), pltpu.SemaphoreType.DMA((N_DEV - 1,)), pltpu.SemaphoreType.DMA(())]


def _all_gather(v, after, name):
    def body(v_ref, after_ref, o_ref, send_sems, recv_sems, local_sem):
        del after_ref
        x, y, c = _me()
        me = 4 * x + 2 * y + c
        local = pltpu.make_async_copy(v_ref, o_ref.at[me], local_sem)
        local.start()
        sends = []
        for m in range(1, N_DEV):
            dev, _ = _peer(m)
            cp = pltpu.make_async_remote_copy(v_ref, o_ref.at[me], send_sems.at[m - 1], recv_sems.at[m - 1],
                                              device_id=dev, device_id_type=pl.DeviceIdType.MESH)
            cp.start()
            sends.append(cp)
        for m in range(1, N_DEV):
            dev, blk = _peer(m)
            pltpu.make_async_remote_copy(v_ref, o_ref.at[blk], send_sems.at[m - 1], recv_sems.at[m - 1],
                                         device_id=dev, device_id_type=pl.DeviceIdType.MESH).wait_recv()
        for cp in sends:
            cp.wait_send()
        local.wait()

    return pl.pallas_call(
        body, name=name, in_specs=[_ANY, _ANY], out_specs=_ANY,
        out_shape=SDS((N_DEV,) + v.shape, v.dtype), scratch_shapes=_EXCHANGE_SEMS)(v, after)


_HBM = pl.BlockSpec(memory_space=pltpu.HBM)
_SEM = pl.BlockSpec(memory_space=pltpu.SEMAPHORE)
_EFFECT = pltpu.CompilerParams(has_side_effects=pltpu.SideEffectType.DATAFLOW_SIDE_EFFECTING)
_PEER_SEMS = pltpu.SemaphoreType.DMA((N_DEV - 1,))


def _in_hbm(a):
    return pltpu.with_memory_space_constraint(a, pltpu.HBM)


def _remote(src, dst, send_sems, recv_sems, m):
    dev, _ = _peer(m)
    return pltpu.make_async_remote_copy(src, dst, send_sems.at[m - 1], recv_sems.at[m - 1],
                                        device_id=dev, device_id_type=pl.DeviceIdType.MESH)


def _gather_start(lands, after, name):
    n = len(lands)

    def body(*refs):
        land = refs[:n]
        sems = refs[n + 1:3 * n + 1]
        token = refs[-1]
        x, y, c = _me()
        me = 4 * x + 2 * y + c
        for t in range(n):
            for m in range(1, N_DEV):
                _remote(land[t].at[me], land[t].at[me], sems[2 * t], sems[2 * t + 1], m).start()
        token[...] = jnp.zeros_like(token)

    res = pl.pallas_call(
        body, name=name, in_specs=[_HBM] * n + [_ANY],
        out_specs=[_SEM] * (2 * n) + [_HBM] * n + [pl.BlockSpec(memory_space=pltpu.VMEM)],
        out_shape=[_PEER_SEMS] * (2 * n) + [pltpu.HBM(a.shape, a.dtype) for a in lands] + [SDS((8, 128), F32)],
        input_output_aliases={t: 2 * n + t for t in range(n)},
        compiler_params=_EFFECT)(*[_in_hbm(a) for a in lands], after)
    return [(res[2 * t], res[2 * t + 1], res[2 * n + t]) for t in range(n)], res[-1]


def _gather_wait(handle, after, name):
    send_sems, recv_sems, land = handle

    def body(land_ref, ssem, rsem, after_ref, out_ref):
        del after_ref, out_ref
        x, y, c = _me()
        me = 4 * x + 2 * y + c
        for m in range(1, N_DEV):
            _, blk = _peer(m)
            cp = _remote(land_ref.at[me], land_ref.at[blk], ssem, rsem, m)
            cp.wait_send()
            cp.wait_recv()

    return pl.pallas_call(
        body, name=name, in_specs=[_HBM, _SEM, _SEM, _ANY], out_specs=_HBM,
        out_shape=pltpu.HBM(land.shape, land.dtype), input_output_aliases={0: 0},
        compiler_params=_EFFECT)(land, send_sems, recv_sems, after)


FIRST_STAGE = (1, 2, 4, 6)
RELAYED = (2, 4, 6)
OTHER_CORE = 1


def _stage_copy(src, dst, send_sems, recv_sems, k, m):
    dev, _ = _peer(m)
    return pltpu.make_async_remote_copy(src, dst, send_sems.at[k], recv_sems.at[k],
                                        device_id=dev, device_id_type=pl.DeviceIdType.MESH)


def _gather2_start(lands, after, name):
    n = len(lands)

    def body(*refs):
        land = refs[:n]
        sems = refs[n + 1:3 * n + 1]
        token = refs[-1]
        x, y, c = _me()
        me = 4 * x + 2 * y + c
        for t in range(n):
            for k, m in enumerate(FIRST_STAGE):
                _stage_copy(land[t].at[me], land[t].at[me], sems[2 * t], sems[2 * t + 1], k, m).start()
        token[...] = jnp.zeros_like(token)

    stage_sems = pltpu.SemaphoreType.DMA((len(FIRST_STAGE),))
    res = pl.pallas_call(
        body, name=name, in_specs=[_HBM] * n + [_ANY],
        out_specs=[_SEM] * (2 * n) + [_HBM] * n + [pl.BlockSpec(memory_space=pltpu.VMEM)],
        out_shape=[stage_sems] * (2 * n) + [pltpu.HBM(a.shape, a.dtype) for a in lands] + [SDS((8, 128), F32)],
        input_output_aliases={t: 2 * n + t for t in range(n)},
        compiler_params=_EFFECT)(*[_in_hbm(a) for a in lands], after)
    return [(res[2 * t], res[2 * t + 1], res[2 * n + t]) for t in range(n)], res[-1]


def _gather2_relay(handles, after, name):
    n = len(handles)

    def body(*refs):
        land, send1, recv1 = refs[:n], refs[n:2 * n], refs[2 * n:3 * n]
        sems = refs[3 * n + 1:5 * n + 1]
        token = refs[-1]
        x, y, c = _me()
        me = 4 * x + 2 * y + c
        for t in range(n):
            for j, m in enumerate(RELAYED):
                _, blk = _peer(m)
                _stage_copy(land[t].at[me], land[t].at[blk], send1[t], recv1[t], 1 + j, m).wait_recv()
                _stage_copy(land[t].at[blk], land[t].at[blk], sems[2 * t], sems[2 * t + 1], j, OTHER_CORE).start()
        token[...] = jnp.zeros_like(token)

    relay_sems = pltpu.SemaphoreType.DMA((len(RELAYED),))
    lands = [h[2] for h in handles]
    res = pl.pallas_call(
        body, name=name, in_specs=[_HBM] * n + [_SEM] * (2 * n) + [_ANY],
        out_specs=[_SEM] * (2 * n) + [_HBM] * n + [pl.BlockSpec(memory_space=pltpu.VMEM)],
        out_shape=[relay_sems] * (2 * n) + [pltpu.HBM(a.shape, a.dtype) for a in lands] + [SDS((8, 128), F32)],
        input_output_aliases={t: 2 * n + t for t in range(n)},
        compiler_params=_EFFECT)(*lands, *[h[0] for h in handles], *[h[1] for h in handles], after)
    return [(h[0], h[1], res[2 * t], res[2 * t + 1], res[2 * n + t]) for t, h in enumerate(handles)], res[-1]


def _gather2_wait(handle, after, name):
    send1, recv1, send2, recv2, land = handle

    def body(land_ref, s1, r1, s2, r2, after_ref, out_ref):
        del after_ref, out_ref
        x, y, c = _me()
        me = 4 * x + 2 * y + c
        _, other = _peer(OTHER_CORE)
        _stage_copy(land_ref.at[me], land_ref.at[other], s1, r1, 0, OTHER_CORE).wait_recv()
        for k, m in enumerate(FIRST_STAGE):
            _stage_copy(land_ref.at[me], land_ref.at[me], s1, r1, k, m).wait_send()
        for j, m in enumerate(RELAYED):
            _, mine = _peer(m)
            _, theirs = _peer(m ^ OTHER_CORE)
            _stage_copy(land_ref.at[mine], land_ref.at[mine], s2, r2, j, OTHER_CORE).wait_send()
            _stage_copy(land_ref.at[mine], land_ref.at[theirs], s2, r2, j, OTHER_CORE).wait_recv()

    return pl.pallas_call(
        body, name=name, in_specs=[_HBM] + [_SEM] * 4 + [_ANY], out_specs=_HBM,
        out_shape=pltpu.HBM(land.shape, land.dtype), input_output_aliases={0: 0},
        compiler_params=_EFFECT)(land, send1, recv1, send2, recv2, after)


def _exchange_start(ps, name):
    n = len(ps)

    def body(*refs):
        p = refs[:n]
        got = refs[n:2 * n]
        sems = refs[2 * n:5 * n]
        token = refs[-1]
        x, y, c = _me()
        me = 4 * x + 2 * y + c
        for t in range(n):
            pltpu.make_async_copy(p[t].at[me], got[t].at[me], sems[3 * t + 2]).start()
            for m in range(1, N_DEV):
                _, blk = _peer(m)
                _remote(p[t].at[blk], got[t].at[me], sems[3 * t], sems[3 * t + 1], m).start()
        token[...] = jnp.zeros_like(token)

    res = pl.pallas_call(
        body, name=name, in_specs=[_HBM] * (2 * n),
        out_specs=[_SEM] * (3 * n) + [_HBM] * (2 * n) + [pl.BlockSpec(memory_space=pltpu.VMEM)],
        out_shape=[_PEER_SEMS, _PEER_SEMS, pltpu.SemaphoreType.DMA(())] * n
        + [pltpu.HBM(a.shape, a.dtype) for a in ps] * 2 + [SDS((8, 128), F32)],
        input_output_aliases={t: 3 * n + t for t in range(2 * n)},
        compiler_params=_EFFECT)(*[_in_hbm(a) for a in ps], *[_in_hbm(lax.empty(a.shape, a.dtype)) for a in ps])
    return [(res[3 * t], res[3 * t + 1], res[3 * t + 2], res[3 * n + t], res[4 * n + t]) for t in range(n)], res[-1]


def _exchange_wait(handle, after, name):
    send_sems, recv_sems, local_sem, p, got = handle

    def body(p_ref, got_ref, ssem, rsem, lsem, after_ref, p_out, got_out):
        del after_ref, p_out, got_out
        x, y, c = _me()
        me = 4 * x + 2 * y + c
        pltpu.make_async_copy(p_ref.at[me], got_ref.at[me], lsem).wait()
        for m in range(1, N_DEV):
            _, blk = _peer(m)
            cp = _remote(p_ref.at[blk], got_ref.at[blk], ssem, rsem, m)
            cp.wait_send()
            cp.wait_recv()

    return pl.pallas_call(
        body, name=name, in_specs=[_HBM, _HBM, _SEM, _SEM, _SEM, _ANY], out_specs=[_HBM, _HBM],
        out_shape=[pltpu.HBM(p.shape, p.dtype), pltpu.HBM(got.shape, got.dtype)],
        input_output_aliases={0: 0, 1: 1}, compiler_params=_EFFECT)(p, got, send_sems, recv_sems, local_sem, after)[1]


CHIPS = (0, 2, 4, 6)


def _pairs_start(p, name):
    def body(p_ref, pair_ref, ssem, rsem, p_out, pair_out, token):
        del p_out, pair_out
        for k, chip in enumerate(CHIPS):
            _, blk = _peer(chip ^ OTHER_CORE)
            _stage_copy(p_ref.at[blk], pair_ref.at[k], ssem, rsem, k, OTHER_CORE).start()
        token[...] = jnp.zeros_like(token)

    sems = pltpu.SemaphoreType.DMA((len(CHIPS),))
    pair = lax.empty((len(CHIPS),) + p.shape[1:], p.dtype)
    res = pl.pallas_call(
        body, name=name, in_specs=[_HBM] * 2,
        out_specs=[_SEM] * 2 + [_HBM] * 2 + [pl.BlockSpec(memory_space=pltpu.VMEM)],
        out_shape=[sems, sems, pltpu.HBM(p.shape, p.dtype), pltpu.HBM(pair.shape, pair.dtype), SDS((8, 128), F32)],
        input_output_aliases={0: 2, 1: 3}, compiler_params=_EFFECT)(_in_hbm(p), _in_hbm(pair))
    return res[:4], res[4]


def _pairs_wait(handle, after, name):
    send_sems, recv_sems, p, pair = handle

    def body(p_ref, pair_ref, ssem, rsem, after_ref, p_out, pair_out):
        del after_ref, p_out, pair_out
        for k, chip in enumerate(CHIPS):
            _, blk = _peer(chip ^ OTHER_CORE)
            cp = _stage_copy(p_ref.at[blk], pair_ref.at[k], ssem, rsem, k, OTHER_CORE)
            cp.wait_send()
            cp.wait_recv()

    return pl.pallas_call(
        body, name=name, in_specs=[_HBM, _HBM, _SEM, _SEM, _ANY], out_specs=[_HBM, _HBM],
        out_shape=[pltpu.HBM(p.shape, p.dtype), pltpu.HBM(pair.shape, pair.dtype)],
        input_output_aliases={0: 0, 1: 1}, compiler_params=_EFFECT)(p, pair, send_sems, recv_sems, after)


def _sum_pairs(p, pair, me1, name):
    _, r, c = pair.shape
    tr = _row_tile(r)

    def body(me_ref, p_ref, pair_ref, o_ref):
        del me_ref
        o_ref[...] = (p_ref[...].astype(F32) + pair_ref[...].astype(F32)).astype(o_ref.dtype)

    def mine(k, i, me):
        chip = 2 * k
        return (jnp.bitwise_xor(me[0], chip), i, 0)

    assert CHIPS == tuple(2 * k for k in range(len(CHIPS)))
    blk = pl.BlockSpec((None, tr, c), lambda k, i, me: (k, i, 0))
    return pl.pallas_call(
        body, name=name,
        grid_spec=pltpu.PrefetchScalarGridSpec(
            num_scalar_prefetch=1, grid=(len(CHIPS), r // tr),
            in_specs=[pl.BlockSpec((None, tr, c), mine), blk], out_specs=blk),
        out_shape=SDS(pair.shape, pair.dtype), compiler_params=_cp("parallel", "parallel"))(me1, p, pair)


def _chips_start(q, name):
    def body(q_ref, got_ref, ssem, rsem, lsem, q_out, got_out, token):
        del q_out, got_out
        pltpu.make_async_copy(q_ref.at[0], got_ref.at[0], lsem).start()
        for k, chip in enumerate(CHIPS[1:]):
            _stage_copy(q_ref.at[k + 1], got_ref.at[k + 1], ssem, rsem, k, chip).start()
        token[...] = jnp.zeros_like(token)

    sems = pltpu.SemaphoreType.DMA((len(CHIPS) - 1,))
    res = pl.pallas_call(
        body, name=name, in_specs=[_HBM] * 2,
        out_specs=[_SEM] * 3 + [_HBM] * 2 + [pl.BlockSpec(memory_space=pltpu.VMEM)],
        out_shape=[sems, sems, pltpu.SemaphoreType.DMA(()), pltpu.HBM(q.shape, q.dtype), pltpu.HBM(q.shape, q.dtype),
                   SDS((8, 128), F32)],
        input_output_aliases={0: 3, 1: 4}, compiler_params=_EFFECT)(_in_hbm(q), _in_hbm(lax.empty(q.shape, q.dtype)))
    return res[:5], res[5]


def _chips_wait(handle, after, name):
    send_sems, recv_sems, local_sem, q, got = handle

    def body(q_ref, got_ref, ssem, rsem, lsem, after_ref, q_out, got_out):
        del after_ref, q_out, got_out
        pltpu.make_async_copy(q_ref.at[0], got_ref.at[0], lsem).wait()
        for k, chip in enumerate(CHIPS[1:]):
            cp = _stage_copy(q_ref.at[k + 1], got_ref.at[k + 1], ssem, rsem, k, chip)
            cp.wait_send()
            cp.wait_recv()

    return pl.pallas_call(
        body, name=name, in_specs=[_HBM, _HBM, _SEM, _SEM, _SEM, _ANY], out_specs=[_HBM, _HBM],
        out_shape=[pltpu.HBM(q.shape, q.dtype), pltpu.HBM(got.shape, got.dtype)],
        input_output_aliases={0: 0, 1: 1}, compiler_params=_EFFECT)(q, got, send_sems, recv_sems, local_sem, after)[1]


def _cast_into_slot(w, layer, me1, name):
    _, r, c = w.shape
    tr = next(t for t in (512, 352, r) if r % t == 0)

    def body(me_ref, w_ref, o_ref):
        del me_ref
        o_ref[...] = w_ref[...].astype(BF16)

    return pl.pallas_call(
        body, name=name,
        grid_spec=pltpu.PrefetchScalarGridSpec(
            num_scalar_prefetch=1, grid=(r // tr,),
            in_specs=[pl.BlockSpec((None, tr, c), lambda i, me: (layer, i, 0))],
            out_specs=pl.BlockSpec((None, tr, c), lambda i, me: (me[0], i, 0))),
        out_shape=SDS((N_DEV, r, c), BF16), compiler_params=_cp("arbitrary"))(me1, w)


def _cast_all_into_slots(ws, layers, me1, after, name):
    n = len(ws)

    def body(me_ref, *refs):
        del me_ref
        for w_ref, o_ref in zip(refs[:n], refs[n + 1:]):
            o_ref[...] = w_ref[...].astype(BF16)

    return pl.pallas_call(
        body, name=name,
        grid_spec=pltpu.PrefetchScalarGridSpec(
            num_scalar_prefetch=1, grid=(1,),
            in_specs=[pl.BlockSpec((None,) + a.shape[1:], lambda i, me, l=l: (l, 0, 0)) for a, l in zip(ws, layers)]
            + [_ANY],
            out_specs=[pl.BlockSpec((None,) + a.shape[1:], lambda i, me: (me[0], 0, 0)) for a in ws]),
        out_shape=[SDS((N_DEV,) + a.shape[1:], BF16) for a in ws],
        compiler_params=_cp("arbitrary"))(me1, *ws, after)


def _sum8_into_slot(p, me1, name):
    _, r, c = p.shape

    def body(me_ref, p_ref, o_ref):
        del me_ref
        acc = p_ref[0]
        for k in range(1, N_DEV):
            acc = acc + p_ref[k]
        o_ref[...] = acc

    return pl.pallas_call(
        body, name=name,
        grid_spec=pltpu.PrefetchScalarGridSpec(
            num_scalar_prefetch=1, grid=(1,),
            in_specs=[pl.BlockSpec(p.shape, lambda i, me: (0, 0, 0))],
            out_specs=pl.BlockSpec((None, r, c), lambda i, me: (me[0], 0, 0))),
        out_shape=SDS(p.shape, F32), compiler_params=_cp("arbitrary"))(me1, p)


def _adamw(w, g, m, v):
    m = ADAM_B1 * m + (1.0 - ADAM_B1) * g
    v = ADAM_B2 * v + (1.0 - ADAM_B2) * (g * g)
    m_hat = m / (1.0 - ADAM_B1 ** ADAM_STEP)
    v_hat = v / (1.0 - ADAM_B2 ** ADAM_STEP)
    delta = -ADAM_LR * (m_hat / (jnp.sqrt(v_hat) + ADAM_EPS) + ADAM_WD * w)
    return delta, m, v


def _adam_tile(p_ref, w_ref, m_ref, v_ref, g_ref, d_ref, nm_ref, nv_ref):
    g = p_ref[0].astype(F32)
    for k in range(1, p_ref.shape[0]):
        g = g + p_ref[k].astype(F32)
    delta, nm, nv = _adamw(w_ref[...], g, m_ref[...], v_ref[...])
    g_ref[...] = g
    d_ref[...] = delta
    nm_ref[...] = nm
    nv_ref[...] = nv


class _AdamJob:
    def __init__(self, parts, w, m, v, layer, prev):
        self.args = [parts, w, m, v] + list(prev or ())
        self.layer, self.results = layer, None


def _carry(jobs, steps, body, n_in, n_out):
    in_specs, args, out_specs, out_shapes, aliases, n_prevs = [], [], [], [], {}, []
    for j, job in enumerate(jobs):
        _, r, c = job.args[0].shape
        nr = next(n for n in range(steps, 0, -1) if steps % n == 0 and r % (16 * n) == 0)
        nc = steps // nr
        assert c % (128 * nc) == 0
        tile = (r // nr, c // nc)
        blk = pl.BlockSpec((None,) + tile, lambda i, layer=job.layer, nc=nc: (layer, i // nc, i % nc))
        n_prev = len(job.args) - 4
        aliases.update({n_in + len(args) + 4 + k: n_out + 4 * j + k for k in range(n_prev)})
        in_specs += [pl.BlockSpec(job.args[0].shape[:1] + tile, lambda i, nc=nc: (0, i // nc, i % nc)), blk, blk, blk]
        in_specs += [_ANY] * n_prev
        args += job.args
        out_specs += [blk] * 4
        out_shapes += [SDS(job.args[1].shape, F32)] * 4
        n_prevs.append(n_prev)

    def carrying(*refs):
        ins, outs = refs[:n_in + len(args)], refs[n_in + len(args):]
        body(*ins[:n_in], *outs[:n_out])
        k = n_in
        for j, n_prev in enumerate(n_prevs):
            _adam_tile(*ins[k:k + 4], *outs[n_out + 4 * j:n_out + 4 * j + 4])
            k += 4 + n_prev

    return carrying, in_specs, args, out_specs, out_shapes, aliases


def _adam_shard(parts, w, m, v, layer, prev, name):
    n, r, c = parts.shape
    tr = next(t for t in (512, 352, r) if r % t == 0)
    n_prev = 0 if prev is None else 4

    def body(*refs):
        _adam_tile(*refs[:4], *refs[4 + n_prev:])

    blk = pl.BlockSpec((None, tr, c), lambda i: (layer, i, 0))
    return pl.pallas_call(
        body, name=name, grid=(r // tr,),
        in_specs=[pl.BlockSpec((n, tr, c), lambda i: (0, i, 0)), blk, blk, blk] + [_ANY] * n_prev,
        out_specs=[blk] * 4, out_shape=[SDS(w.shape, F32)] * 4,
        input_output_aliases={4 + k: k for k in range(n_prev)},
        compiler_params=_cp("parallel"))(parts, w, m, v, *(prev or ()))


SMALL_MATRICES = [("lru_w_r", 2048), ("lru_w_i", 2048), ("gmlp_w_s", 1024)]
SMALL_VECTORS = [("norm1_g", 8), ("gmlp_ln_g", 8), ("gmlp_ln_b", 8), ("gmlp_b_s", 8), ("conv_w", 32), ("conv_b", 8),
                 ("lru_b_r", 16), ("lru_b_i", 16), ("lru_lambda", 16), ("norm2_g", 8), ("final_g", 8)]
SMALL_VECTOR_ROW0 = sum(n for _, n in SMALL_MATRICES)
SMALL_VECTOR_BLOCK = 256
SMALL_ROWS = SMALL_VECTOR_ROW0 + SMALL_VECTOR_BLOCK


def _pack_small(small):
    parts = [small[k] for k, _ in SMALL_MATRICES]
    parts += [small[k] if k in small else jnp.zeros((n, HD), F32) for k, n in SMALL_VECTORS]
    flat = jnp.concatenate(parts)
    return jnp.pad(flat, ((0, SMALL_ROWS - flat.shape[0]), (0, 0))).reshape(N_DEV, SMALL_ROWS // N_DEV, HD)


def _adam_matrix(g0, g1, w, m, v, row0, name):
    _, rows, _ = w.shape
    tr = 512

    def body(g0_ref, g1_ref, w_ref, m_ref, v_ref, g_ref, d_ref, nm_ref, nv_ref):
        for l, src in enumerate((g0_ref, g1_ref)):
            g = src[...]
            delta, nm, nv = _adamw(w_ref[l], g, m_ref[l], v_ref[l])
            g_ref[l] = g
            d_ref[l] = delta
            nm_ref[l] = nm
            nv_ref[l] = nv

    gspec = pl.BlockSpec((tr, HD), lambda i: (row0 // tr + i, 0))
    blk = pl.BlockSpec((2, tr, HD), lambda i: (0, i, 0))
    return pl.pallas_call(body, name=name, grid=(rows // tr,), in_specs=[gspec, gspec] + [blk] * 3,
                          out_specs=[blk] * 4, out_shape=[SDS(w.shape, F32)] * 4,
                          compiler_params=_cp("parallel"))(g0, g1, w, m, v)


def _adam_vectors(g0, g1, dg1_parts, me1, ws, ms, vs):
    names = [k for k, _ in SMALL_VECTORS]
    n = len(names)

    def lanes(rows8):
        return jnp.concatenate([rows8[k:k + 1, :] for k in range(LANE_ROWS)], axis=1)

    def body(me_ref, g0_ref, g1_ref, dg1_ref, *refs):
        w_refs, m_refs, v_refs = refs[:n], refs[n:2 * n], refs[2 * n:3 * n]
        outs = refs[3 * n:]
        me = me_ref[0]
        g_refs = (g0_ref, g1_ref)

        def emit(i, idx, g):
            delta, nm, nv = _adamw(w_refs[i][idx], g, m_refs[i][idx], v_refs[i][idx])
            for j, val in enumerate((g, delta, nm, nv)):
                outs[4 * i + j][idx] = val

        off = 0
        for i, (name, rows) in enumerate(SMALL_VECTORS):
            for l in range(2):
                row = (slice(l, l + 1), slice(None))
                if name == "final_g":
                    if l == 1:
                        emit(i, (slice(0, 1), slice(None)), lanes(g1_ref[off:off + rows, :]))
                elif name == "norm1_g":
                    if l == 1:
                        emit(i, row, lanes(g0_ref[off:off + rows, :]))
                    else:
                        total = dg1_ref[0]
                        for k in range(1, N_DEV):
                            total = total + dg1_ref[k]
                        emit(i, row, lanes(total))
                elif name == "gmlp_b_s":
                    emit(i, (l,), g_refs[l][off:off + rows, :])
                elif rows == LANE_ROWS:
                    emit(i, row, lanes(g_refs[l][off:off + rows, :]))
                else:
                    for r in range(rows // LANE_ROWS):
                        emit(i, (l, slice(r, r + 1), slice(None)), g_refs[l][pl.ds(off + r * LANE_ROWS + me, 1), :])
            off += rows

    args = [ws[k] for k in names] + [ms[k] for k in names] + [vs[k] for k in names]
    gspec = pl.BlockSpec((SMALL_VECTOR_BLOCK, HD), lambda i, me: (SMALL_VECTOR_ROW0 // SMALL_VECTOR_BLOCK, 0))
    res = pl.pallas_call(
        body, name="adam_vectors",
        grid_spec=pltpu.PrefetchScalarGridSpec(
            num_scalar_prefetch=1, grid=(1,),
            in_specs=[gspec, gspec, _full(dg1_parts.shape)] + [_full(a.shape) for a in args],
            out_specs=[_full(ws[k].shape) for k in names for _ in range(4)]),
        out_shape=[SDS(ws[k].shape, F32) for k in names for _ in range(4)],
        compiler_params=_cp("arbitrary"))(me1, g0, g1, dg1_parts, *args)
    return {k: list(res[4 * i:4 * i + 4]) for i, k in enumerate(names)}


def _local_step(x, tgt, p, get_w, hook=lambda stage, layer, payload: None):
    s = x.shape[0]
    tm = _row_tile(s)
    wsb = p["gmlp_w_s"].astype(BF16)
    wstb = jnp.swapaxes(p["gmlp_w_s"], -1, -2).astype(BF16)
    bsb = jnp.broadcast_to(p["gmlp_b_s"][..., None], p["gmlp_w_s"].shape)
    wrb = p["lru_w_r"].astype(BF16)
    wib = p["lru_w_i"].astype(BF16)
    saved = []
    for l in range(2):
        win = get_w("w_in", l, x)
        z, h1 = _norm_inproj(x, p["norm1_g"][l][None], win, l, tm, after=[hook("pre_inproj", l, win)])
        a0, b0, a1, b1, xcb, *gates = _lru_gates_fwd(z, p["conv_w"][l], p["conv_b"][l][None], wrb[l], wib[l],
                                                     p["lru_b_r"][l], p["lru_b_i"][l], p["lru_lambda"][l], l, tm)
        h0, hr = _lru_scan(a0, b0, a1, b1, False, l)
        token = hook("pre_gmlp", l, h0)
        wout = get_w("w_out", l, h0 if token is None else token)
        x1, mg = _mixer_fwd(x, h0, hr, z, p["gmlp_ln_g"][l][None], p["gmlp_ln_b"][l][None], wsb[l], bsb[l], wout, l, tm)
        wfi = get_w("w_ffn_in", l, x1)
        wfo = get_w("w_ffn_out", l, x1)
        if l == 0:
            x2, ff, dff, h2 = _ffn_fwd(x1, p["norm2_g"][l][None], wfi, wfo, l, tm)
        else:
            dx, loss, dfg, ff, dff, h2 = _ffn_fwd(x1, p["norm2_g"][l][None], wfi, wfo, l, tm,
                                                  head=(p["final_g"][None], tgt))
        saved.append((x, z, h1, a0, a1, h0, hr, x1, mg, ff, dff, h2, win, wout, wfi, wfo, xcb, gates))
        x = x2
    for l in (1, 0):
        x0, z, h1, a0, a1, h0, hr, x1, mg, ff, dff, h2, win, wout, wfi, wfo, xcb, gates = saved[l]
        dgu, dx1, dg2 = _ffn_bwd(dx, wfo, dff, wfi.reshape(N_DEV, FF_BLK, D), x1, p["norm2_g"][l][None], l, tm)
        d_wfo = _mm_tn(ff, pl.BlockSpec((None, s, FF_BLK), lambda j: (j, 0, 0)), dx, _resident((s, D)),
                       4, (4, FF_BLK, D), pl.BlockSpec((None, FF_BLK, D), lambda j: (j, 0, 0)),
                       f"dw_ffn_out_l{l}", a_is_transposed=False)
        dgu8 = dgu.reshape(N_DEV, s, FF_BLK)
        d_wfi = _mm_tn(dgu8, pl.BlockSpec((None, s, FF_BLK), lambda j: (j, 0, 0)), h2, _resident((s, D)),
                       N_DEV, (N_DEV, FF_BLK, D), pl.BlockSpec((None, FF_BLK, D), lambda j: (j, 0, 0)),
                       f"dw_ffn_in_l{l}", a_is_transposed=False)
        d_wout = _mm_tn(mg, _resident((D, s)), dx1, pl.BlockSpec((s, D // 2), lambda j: (0, j)),
                        2, (D, D), pl.BlockSpec((D, D // 2), lambda j: (0, j)), f"dw_out_l{l}")
        token = hook("ffn_partials", l, dict(w_ffn_out=d_wfo.reshape(N_DEV, D_FF // N_DEV, D), w_ffn_in=d_wfi,
                                             w_out=d_wout.reshape(N_DEV, D // N_DEV, D)))
        pending = hook("mid_backward", l, dx1)
        dz, dh, dws, dbs, dlng, dlnb = _mixer_bwd(dx1, wout, h0, hr, z, p["gmlp_ln_g"][l][None], p["gmlp_ln_b"][l][None],
                                                  wsb[l], wstb[l], bsb[l], l, tm, after=[token])
        g1, g0 = _lru_scan(a1, dh, a0, dh, True, l)
        dxc, dwr, dwi, dbr, dbi, dlam = _lru_gates_bwd(
            xcb, gates, h0, hr, g0, g1, wrb[l], wib[l], p["lru_lambda"][l], l, tm, after=[pending])
        dz, dcw, dcb = _conv_bwd(dz, dxc, z, p["conv_w"][l], l, tm)
        small = dict(lru_w_r=dwr.reshape(-1, HD), lru_w_i=dwi.reshape(-1, HD), gmlp_w_s=dws.reshape(-1, HD),
                     gmlp_ln_g=dlng, gmlp_ln_b=dlnb, gmlp_b_s=dbs, conv_w=dcw, conv_b=dcb, lru_b_r=dbr,
                     lru_b_i=dbi, lru_lambda=dlam, norm2_g=dg2)
        if l == 1:
            small["final_g"] = dfg
        else:
            small["norm1_g"] = dg1
        started = hook("small_grads", l, small)
        d_win = _mm_tn(h1, _resident((D, s)), dz, pl.BlockSpec((s, IN_BLK), lambda j: (0, j)),
                       N_DEV, (N_DEV, D, IN_BLK), pl.BlockSpec((None, D, IN_BLK), lambda j: (j, 0, 0)),
                       f"dw_in_l{l}", after=started, jobs=hook("dw_in", l, started) or ())
        token = hook("mixer_partials", l, dict(w_in=d_win))
        dx, dg1 = _mm_nt_rms_bwd(
            dz, pl.BlockSpec((tm, N_IN), lambda i: (i, 0)),
            lambda r: [r[:, k * IN_BLK:(k + 1) * IN_BLK] for k in range(N_DEV)],
            win, False, x0, p["norm1_g"][l][None], dx1, f"inproj_bwd_dx_l{l}", tm,
            after=[token], jobs=hook("inproj_bwd_dx", l, token) or ())
    return loss, dx, dg1


_REPL = ["norm1_g", "gmlp_ln_g", "gmlp_ln_b", "gmlp_w_s", "gmlp_b_s", "conv_b", "lru_w_r", "lru_w_i", "norm2_g", "final_g"]
_LANE_SHARDED = ["conv_w", "lru_b_r", "lru_b_i", "lru_lambda"]
_BIG = ["w_in", "w_out", "w_ffn_in", "w_ffn_out"]
_ORDER = ["norm1_g", "w_in", "gmlp_ln_g", "gmlp_ln_b", "gmlp_w_s", "gmlp_b_s", "conv_w", "conv_b", "lru_w_r", "lru_b_r",
          "lru_w_i", "lru_b_i", "lru_lambda", "w_out", "norm2_g", "w_ffn_in", "w_ffn_out", "final_g"]


def kernel(x, norm1_g, w_in, gmlp_ln_g, gmlp_ln_b, gmlp_w_s, gmlp_b_s, conv_w, conv_b, lru_w_r, lru_b_r, lru_w_i, lru_b_i, lru_lambda, w_out, norm2_g, w_ffn_in, w_ffn_out, final_g, loss_target, m_norm1_g, m_w_in, m_gmlp_ln_g, m_gmlp_ln_b, m_gmlp_w_s, m_gmlp_b_s, m_conv_w, m_conv_b, m_lru_w_r, m_lru_b_r, m_lru_w_i, m_lru_b_i, m_lru_lambda, m_w_out, m_norm2_g, m_w_ffn_in, m_w_ffn_out, m_final_g, v_norm1_g, v_w_in, v_gmlp_ln_g, v_gmlp_ln_b, v_gmlp_w_s, v_gmlp_b_s, v_conv_w, v_conv_b, v_lru_w_r, v_lru_b_r, v_lru_w_i, v_lru_b_i, v_lru_lambda, v_w_out, v_norm2_g, v_w_ffn_in, v_w_ffn_out, v_final_g):
    w = dict(norm1_g=norm1_g, w_in=w_in, gmlp_ln_g=gmlp_ln_g, gmlp_ln_b=gmlp_ln_b, gmlp_w_s=gmlp_w_s, gmlp_b_s=gmlp_b_s,
             conv_w=conv_w, conv_b=conv_b, lru_w_r=lru_w_r, lru_b_r=lru_b_r, lru_w_i=lru_w_i, lru_b_i=lru_b_i,
             lru_lambda=lru_lambda, w_out=w_out, norm2_g=norm2_g, w_ffn_in=w_ffn_in, w_ffn_out=w_ffn_out, final_g=final_g)
    mom = dict(norm1_g=m_norm1_g, w_in=m_w_in, gmlp_ln_g=m_gmlp_ln_g, gmlp_ln_b=m_gmlp_ln_b, gmlp_w_s=m_gmlp_w_s,
               gmlp_b_s=m_gmlp_b_s, conv_w=m_conv_w, conv_b=m_conv_b, lru_w_r=m_lru_w_r, lru_b_r=m_lru_b_r,
               lru_w_i=m_lru_w_i, lru_b_i=m_lru_b_i, lru_lambda=m_lru_lambda, w_out=m_w_out, norm2_g=m_norm2_g,
               w_ffn_in=m_w_ffn_in, w_ffn_out=m_w_ffn_out, final_g=m_final_g)
    var = dict(norm1_g=v_norm1_g, w_in=v_w_in, gmlp_ln_g=v_gmlp_ln_g, gmlp_ln_b=v_gmlp_ln_b, gmlp_w_s=v_gmlp_w_s,
               gmlp_b_s=v_gmlp_b_s, conv_w=v_conv_w, conv_b=v_conv_b, lru_w_r=v_lru_w_r, lru_b_r=v_lru_b_r,
               lru_w_i=v_lru_w_i, lru_b_i=v_lru_b_i, lru_lambda=v_lru_lambda, w_out=v_w_out, norm2_g=v_norm2_g,
               w_ffn_in=v_w_ffn_in, w_ffn_out=v_w_ffn_out, final_g=v_final_g)
    for src in (w, mom, var):
        src["w_ffn_in"] = jnp.swapaxes(src["w_ffn_in"], 1, 2)
    xi, yi, ci = _me()
    me = 4 * xi + 2 * yi + ci

    lane_shapes = [w[k].shape for k in _LANE_SHARDED]
    lane_rows = sum(a[0] * a[1] for a in lane_shapes)
    packed = jnp.concatenate([w[k].reshape(-1, HD) for k in _LANE_SHARDED])
    packed = jnp.pad(packed, ((0, -lane_rows % 8), (0, 0)))

    me1 = jnp.reshape(me, (1,)).astype(jnp.int32)
    gathers = {}
    exchanges = {}
    views = dict(w_in=(N_DEV, D, IN_BLK), w_out=(D, D), w_ffn_in=(2, 4, FF_BLK, D), w_ffn_out=(4, FF_BLK, D))
    small_ex = {}
    small_ag = {}

    casts = {}

    def start_gather(names, l, after):
        lands = [casts[(k, l)] if (k, l) in casts else _cast_into_slot(w[k], l, me1, f"cast_{k}_l{l}") for k in names]
        started, tok = _gather2_start(lands, after, f"gather_start_{'_'.join(names)}_l{l}")
        gathers.update({(k, l): h for k, h in zip(names, started)})
        return tok

    def relay_gather(names, l, after):
        relayed, tok = _gather2_relay([gathers[(k, l)] for k in names], after, f"gather_relay_{'_'.join(names)}_l{l}")
        gathers.update({(k, l): h for k, h in zip(names, relayed)})
        return tok

    def get_w(k, l, after):
        return _gather2_wait(gathers[(k, l)], after, f"gather_wait_{k}_l{l}").reshape(views[k])

    carried = {("inproj_bwd_dx", 1): [("w_ffn_out", 1), ("w_ffn_in", 1), ("w_out", 1)], ("dw_in", 0): [("w_in", 1)],
               ("inproj_bwd_dx", 0): [("w_ffn_out", 0), ("w_ffn_in", 0), ("w_out", 0)]}
    adam = {}

    def adam_jobs(shards, after):
        for k, l in shards:
            got = _exchange_wait(exchanges[(k, l)], after, f"exchange_wait_{k}_l{l}")
            adam[k] = _AdamJob(got, w[k], mom[k], var[k], l, adam[k].results if k in adam else None)
        return [adam[k] for k, _ in shards]

    def hook(stage, l, payload):
        if stage in ("dw_in", "inproj_bwd_dx"):
            return adam_jobs(carried.get((stage, l), []), payload)
        if stage == "pre_inproj":
            tok = start_gather(_BIG[1:], l, payload)
            return start_gather(_BIG[:1], l + 1, tok) if l == 0 else tok
        if stage == "pre_gmlp":
            tok = relay_gather(_BIG[1:], l, payload)
            return relay_gather(_BIG[:1], l + 1, tok) if l == 0 else tok
        if stage == "small_grads":
            (small_ex[l],), tok = _exchange_start([_pack_small(payload)], f"exchange_start_small_l{l}")
            return tok
        if stage == "mid_backward":
            return reduce_small(l + 1, payload) if l == 0 else None
        if (stage, l) == ("mixer_partials", 0):
            pairs, tok = _pairs_start(payload["w_in"], "pairs_start_w_in_l0")
            p, pair = _pairs_wait(pairs, reduce_small(0, tok), "pairs_wait_w_in_l0")
            sums = _sum_pairs(p, pair, me1, "sum_pairs_w_in_l0")
            exchanges[("w_in", 0)], tok = _chips_start(sums, "chips_start_w_in_l0")
            return tok
        started, tok = _exchange_start(list(payload.values()), f"exchange_start_{'_'.join(payload)}_l{l}")
        exchanges.update({(k, l): h for k, h in zip(payload, started)})
        return tok

    def reduce_small(l, after):
        got = _exchange_wait(small_ex[l], after, f"exchange_wait_small_l{l}")
        mine = _sum8_into_slot(got, me1, f"sum_small_l{l}")
        (small_ag[l],), tok = _gather_start([mine], got, f"gather_start_small_l{l}")
        return tok

    land = lax.dynamic_update_slice(jnp.zeros((N_DEV,) + packed.shape, F32), packed[None], (me, 0, 0))
    (lanes_handle,), token = _gather_start([land], packed, "gather_start_lanes")
    token = start_gather(_BIG[:1], 0, token)
    later = [(k, l) for l in range(2) for k in _BIG if (k, l) != ("w_in", 0)]
    casts.update(zip(later, _cast_all_into_slots([w[k] for k, _ in later], [l for _, l in later], me1, token,
                                                 "cast_later_weights")))
    token = relay_gather(_BIG[:1], 0, casts[later[0]])
    lanes = _gather_wait(lanes_handle, token, "gather_wait_lanes")
    params = {k: w[k] for k in _REPL}
    off = 0
    for k, shp in zip(_LANE_SHARDED, lane_shapes):
        n = shp[0] * shp[1]
        params[k] = jnp.swapaxes(lanes[:, off:off + n], 0, 1).reshape(shp[0], shp[1], D)
        off += n
    loss, dx, dg1 = _local_step(x[0], loss_target[0], params, get_w, hook)

    out = {k: job.results for k, job in adam.items()}
    after = dx
    g_small = [_gather_wait(small_ag[l], after, f"gather_wait_small_l{l}").reshape(SMALL_ROWS, HD) for l in (0, 1)]
    row0 = 0
    for k, rows in SMALL_MATRICES:
        res = _adam_matrix(*g_small, *[src[k].reshape(2, rows, HD) for src in (w, mom, var)], row0, f"adam_{k}")
        out[k] = [a.reshape(w[k].shape) for a in res]
        after = res[3]
        row0 += rows
    got = _chips_wait(exchanges[("w_in", 0)], after, "chips_wait_w_in_l0")
    out["w_in"] = _adam_shard(got, w["w_in"], mom["w_in"], var["w_in"], 0, out["w_in"], "adam_w_in_l0")
    out["w_ffn_in"] = [jnp.swapaxes(a, 1, 2) for a in out["w_ffn_in"]]
    as_rows = lambda a: a.reshape(1, D) if a.ndim == 1 else a
    vec = _adam_vectors(*g_small, _all_gather(dg1, out["w_in"][3], "gather_norm1_grad"), me1,
                        *[{k: as_rows(src[k]) for k, _ in SMALL_VECTORS} for src in (w, mom, var)])
    out.update({k: [a.reshape(w[k].shape) for a in res] for k, res in vec.items()})

    loss = lax.psum(loss[0, 0], MESH_AXES)
    return (loss, dx[None], *[out[k][0] for k in _ORDER], *[out[k][1] for k in _ORDER],
            *[out[k][2] for k in _ORDER], *[out[k][3] for k in _ORDER])
```

```python
import jax
import jax.numpy as jnp
from jax import lax
from jax.experimental import pallas as pl
from jax.experimental.pallas import tpu as pltpu

F32 = jnp.float32
BF16 = jnp.bfloat16
SDS = jax.ShapeDtypeStruct

D = 1024
N_IN = 6 * D
D_FF = 2816
N_DEV = 8
IN_BLK = N_IN // N_DEV
FF_BLK = 2 * D_FF // N_DEV
HEADS = 8
HD = 128
EPS = 1e-6
LRU_C = 8.0

ADAM_LR = 0.001
ADAM_B1 = 0.9
ADAM_B2 = 0.999
ADAM_EPS = 1e-08
ADAM_WD = 0.01
ADAM_STEP = 10

VMEM_LIMIT = 60 * 2**20


def _cp(*sem, **kw):
    return pltpu.CompilerParams(dimension_semantics=sem, vmem_limit_bytes=VMEM_LIMIT, **kw)


def _row_tile(s):
    return 512 if s >= 1024 else s // 2


_GELU_C = 0.7978845608028654


def _gelu(x):
    t = jnp.tanh(_GELU_C * (x + 0.044715 * (x * x * x)))
    return 0.5 * x * (1.0 + t), t


def _gelu_grad(x, t):
    return 0.5 * (1.0 + t) + 0.5 * x * (1.0 - t * t) * (_GELU_C * (1.0 + 0.134145 * (x * x)))


def _sigmoid(x):
    return 0.5 + 0.5 * jnp.tanh(0.5 * x)


def _softplus(x):
    e = jnp.exp(-jnp.abs(x))
    w = 1.0 + e
    l1p = jnp.where(w == 1.0, e, jnp.log(w) * e / jnp.where(w == 1.0, 1.0, w - 1.0))
    return jnp.maximum(x, 0.0) + l1p


def _rms_fwd(x, g):
    r = lax.rsqrt(jnp.mean(x * x, axis=-1, keepdims=True) + EPS)
    return x * r * g


def _rms_bwd(x, g, dh):
    r = lax.rsqrt(jnp.mean(x * x, axis=-1, keepdims=True) + EPS)
    xh = x * r
    dxh = dh * g
    dx = r * (dxh - xh * jnp.mean(dxh * xh, axis=-1, keepdims=True))
    dg = jnp.sum(dh * xh, axis=0, keepdims=True)
    return dx, dg


LANE_ROWS = D // HD


def _add_rows128(ref, vec, row0=0):
    for i in range(vec.shape[0]):
        for k in range(LANE_ROWS):
            j = row0 + i * LANE_ROWS + k
            ref[j:j + 1, :] += vec[i:i + 1, k * HD:(k + 1) * HD]


def _dot(a, b):
    return jnp.dot(a, b, preferred_element_type=F32)


def _dot_nt(a, b):
    return lax.dot_general(a, b, (((1,), (1,)), ((), ())), preferred_element_type=F32)


def _dot_tn(a, b):
    return lax.dot_general(a, b, (((0,), (0,)), ((), ())), preferred_element_type=F32)


def _taps(prev, cur, nxt, tm):
    hr = prev.shape[0]
    ext = jnp.concatenate([prev, cur, nxt], axis=0)
    n = tm + 2 * hr
    sl = slice(hr, hr + tm)
    return (pltpu.roll(ext, 2, 0)[sl], pltpu.roll(ext, 1, 0)[sl], cur,
            pltpu.roll(ext, n - 1, 0)[sl], pltpu.roll(ext, n - 2, 0)[sl])


def _halo_specs(tm, s, col, rows=8):
    nb = s // rows
    r = tm // rows
    return (pl.BlockSpec((rows, D), lambda i: (jnp.maximum(i * r - 1, 0), col)),
            pl.BlockSpec((tm, D), lambda i: (i, col)),
            pl.BlockSpec((rows, D), lambda i: (jnp.minimum((i + 1) * r, nb - 1), col)))


def _halo_load(prev_ref, cur_ref, next_ref, fp, fn):
    return prev_ref[...].astype(F32) * fp, cur_ref[...].astype(F32), next_ref[...].astype(F32) * fn


def _halo_flags(nt):
    i = pl.program_id(0)
    return (i > 0).astype(F32), (i < nt - 1).astype(F32)


def _full(shape):
    nd = len(shape)
    return pl.BlockSpec(shape, lambda *_: (0,) * nd)


def _resident(shape):
    nd = len(shape)
    return pl.BlockSpec(shape, lambda *_: (0,) * nd, pipeline_mode=pl.Buffered(1))


def _behind(tokens, body, n_in):
    deps = [t for t in tokens if t is not None]

    def ordered(*refs):
        body(*refs[:n_in], *refs[n_in + len(deps):])

    return ordered, [_ANY] * len(deps), deps


def _norm_inproj(x, g, w, layer, tm, after=()):
    s = x.shape[0]

    def body(x_ref, g_ref, w_ref, z_ref, ht_ref):
        h32 = _rms_fwd(x_ref[...], g_ref[...])
        ht_ref[...] = h32.T.astype(BF16)
        h = h32.astype(BF16)
        for j in range(N_DEV):
            z_ref[:, j * IN_BLK:(j + 1) * IN_BLK] = _dot(h, w_ref[j]).astype(BF16)

    body, dep_specs, deps = _behind(after, body, 3)
    return pl.pallas_call(
        body, name=f"norm_inproj_l{layer}", grid=(s // tm,),
        in_specs=[pl.BlockSpec((tm, D), lambda i: (i, 0)), _full((1, D)), _resident((N_DEV, D, IN_BLK))] + dep_specs,
        out_specs=[pl.BlockSpec((tm, N_IN), lambda i: (i, 0)), pl.BlockSpec((D, tm), lambda i: (0, i))],
        out_shape=[SDS((s, N_IN), BF16), SDS((D, s), BF16)],
        compiler_params=_cp("parallel"))(x, g, w, *deps)


def _gmlp_values(zu_ref, zv_ref, lng_ref, lnb_ref):
    zu = zu_ref[...].astype(F32)
    zv = zv_ref[...].astype(F32)
    u, tu = _gelu(zu)
    gv, tv = _gelu(zv)
    xc = gv - jnp.mean(gv, axis=-1, keepdims=True)
    rstd = lax.rsqrt(jnp.mean(xc * xc, axis=-1, keepdims=True) + EPS)
    xh = xc * rstd
    vb = (xh * lng_ref[...] + lnb_ref[...]).astype(BF16)
    return zu, zv, u, tu, tv, xh, rstd, vb


def _mixer_fwd(x, h0, h1, z, lng, lnb, ws, bsb, wo, layer, tm):
    s = x.shape[0]

    def body(x_ref, h0_ref, h1_ref, zu_ref, zv_ref, zg_ref, za_ref, zb_ref, lng_ref, lnb_ref, ws_ref, bsb_ref,
             wo_ref, x1_ref, mg_ref, ya_s):
        _, _, u, _, _, _, _, vb = _gmlp_values(zu_ref, zv_ref, lng_ref, lnb_ref)
        for c in range(tm // HD):
            rs = slice(c * HD, (c + 1) * HD)
            for g in range(HEADS):
                cs = slice(g * HD, (g + 1) * HD)
                ya_s[rs, cs] = u[rs, cs] * (_dot(ws_ref[g], vb[rs, cs]) + bsb_ref[g])
        gg, _ = _gelu(zg_ref[...].astype(F32))
        yb = (h0_ref[...] + h1_ref[...]) * gg
        m32 = _sigmoid(za_ref[...].astype(F32)) * ya_s[...] + _sigmoid(zb_ref[...].astype(F32)) * yb
        mg_ref[...] = m32.T.astype(BF16)
        x1_ref[...] = x_ref[...] + _dot(m32.astype(BF16), wo_ref[...])

    tile = pl.BlockSpec((tm, D), lambda i: (i, 0))
    wspec = _full((HEADS, HD, HD))
    return pl.pallas_call(
        body, name=f"mixer_fwd_l{layer}", grid=(s // tm,),
        in_specs=[tile, tile, tile] + [pl.BlockSpec((tm, D), lambda i, c=c: (i, c)) for c in (0, 1, 3, 4, 5)]
        + [_full((1, D)), _full((1, D)), wspec, wspec, _full((D, D))],
        out_specs=[tile, pl.BlockSpec((D, tm), lambda i: (0, i))], out_shape=[SDS((s, D), F32), SDS((D, s), BF16)],
        scratch_shapes=[pltpu.VMEM((tm, D), F32)],
        compiler_params=_cp("parallel"))(x, h0, h1, z, z, z, z, z, lng, lnb, ws, bsb, wo)


def _conv(taps, cw_ref, cb_ref):
    _, m1, c0, p1, p2 = taps
    return cb_ref[...] + m1 * cw_ref[0:1, :] + c0 * cw_ref[1:2, :] + p1 * cw_ref[2:3, :] + p2 * cw_ref[3:4, :]


def _heads_dot(xb, w_ref, d):
    return jnp.concatenate([_dot(xb[:, h * HD:(h + 1) * HD], w_ref[d, h]) for h in range(HEADS)], axis=1)


def _lru_decay(r, sp):
    la = (-LRU_C) * r * sp
    a = jnp.exp(la)
    return a, jnp.tanh(-la) * (a * a + 1.0)


def _lru_gates_fwd(z, cw, cb, wr, wi, br, bi, lam, layer, tm):
    s = z.shape[0]
    nt = s // tm

    def body(zp_ref, zc_ref, zn_ref, cw_ref, cb_ref, wr_ref, wi_ref, br_ref, bi_ref, lam_ref,
             a0_ref, b0_ref, a1_ref, b1_ref, xc_ref, r0_ref, i0_ref, r1_ref, i1_ref):
        fp, fn = _halo_flags(nt)
        xc = _conv(_taps(*_halo_load(zp_ref, zc_ref, zn_ref, fp, fn), tm), cw_ref, cb_ref)
        xb = xc.astype(BF16)
        xc_ref[...] = xb
        for d, (a_ref, b_ref, r_ref, i_ref) in enumerate(((a0_ref, b0_ref, r0_ref, i0_ref),
                                                          (a1_ref, b1_ref, r1_ref, i1_ref))):
            r = _sigmoid(_heads_dot(xb, wr_ref, d) + br_ref[d:d + 1, :])
            ig = _sigmoid(_heads_dot(xb, wi_ref, d) + bi_ref[d:d + 1, :])
            a, q = _lru_decay(r, _softplus(-lam_ref[d:d + 1, :]))
            a_ref[...] = a
            b_ref[...] = jnp.sqrt(q) * (ig * xc)
            r_ref[...] = r.astype(BF16)
            i_ref[...] = ig.astype(BF16)

    tile = pl.BlockSpec((tm, D), lambda i: (i, 0))
    return pl.pallas_call(
        body, name=f"lru_gates_fwd_l{layer}", grid=(nt,),
        in_specs=[*_halo_specs(tm, s, 2, 16), _full((4, D)), _full((1, D)),
                  _full((2, HEADS, HD, HD)), _full((2, HEADS, HD, HD)), _full((2, D)), _full((2, D)), _full((2, D))],
        out_specs=[tile] * 9, out_shape=[SDS((s, D), F32)] * 4 + [SDS((s, D), BF16)] * 5,
        compiler_params=_cp("parallel"))(z, z, z, cw, cb, wr, wi, br, bi, lam)


def _scan_group(a, x, c, reverse, bwd):
    row = lax.broadcasted_iota(jnp.int32, a.shape, 0)
    b = a * x if bwd else x
    for d in (1, 2, 4):
        keep = (row < 8 - d) if reverse else (row >= d)
        sh = 8 - d if reverse else d
        a_s = jnp.where(keep, pltpu.roll(a, sh, 0), 1.0)
        b_s = jnp.where(keep, pltpu.roll(b, sh, 0), 0.0)
        b = a * b_s + b
        a = a * a_s
    h = b + a * c
    new_c = h[0:1, :] if reverse else h[7:8, :]
    if not bwd:
        return h, new_c
    if reverse:
        prev = jnp.where(row < 7, pltpu.roll(h, 7, 0), c)
    else:
        prev = jnp.where(row >= 1, pltpu.roll(h, 1, 0), c)
    return x + prev, new_c


def _lru_scan(a_f, x_f, a_r, x_r, bwd, layer):
    s = a_f.shape[0]
    ts = min(1024, s // 2)
    cb = 512
    nt = s // ts
    ng = ts // 8

    def body(af_ref, xf_ref, ar_ref, xr_ref, of_ref, or_ref, cf, cr):
        @pl.when(pl.program_id(1) == 0)
        def _():
            cf[...] = jnp.zeros_like(cf)
            cr[...] = jnp.zeros_like(cr)

        def step(j, carry):
            c_f, c_r = carry
            rf = pl.multiple_of(j * 8, 8)
            rr = pl.multiple_of((ng - 1 - j) * 8, 8)
            o, c_f = _scan_group(af_ref[pl.ds(rf, 8), :], xf_ref[pl.ds(rf, 8), :], c_f, False, bwd)
            of_ref[pl.ds(rf, 8), :] = o
            o, c_r = _scan_group(ar_ref[pl.ds(rr, 8), :], xr_ref[pl.ds(rr, 8), :], c_r, True, bwd)
            or_ref[pl.ds(rr, 8), :] = o
            return c_f, c_r

        c_f, c_r = lax.fori_loop(0, ng, step, (cf[0:1, :], cr[0:1, :]), unroll=2)
        cf[...] = jnp.broadcast_to(c_f, cf.shape)
        cr[...] = jnp.broadcast_to(c_r, cr.shape)

    fwd = pl.BlockSpec((ts, cb), lambda c, t: (t, c))
    rev = pl.BlockSpec((ts, cb), lambda c, t: (nt - 1 - t, c))
    return pl.pallas_call(
        body, name=f"lru_scan_{'bwd' if bwd else 'fwd'}_l{layer}", grid=(D // cb, nt),
        in_specs=[fwd, fwd, rev, rev], out_specs=[fwd, rev],
        out_shape=[SDS((s, D), F32)] * 2,
        scratch_shapes=[pltpu.VMEM((8, cb), F32), pltpu.VMEM((8, cb), F32)],
        compiler_params=_cp("parallel", "arbitrary"))(a_f, x_f, a_r, x_r)


def _ffn_fwd(x1, g, wfi, wfo, layer, tm, head=None):
    s = x1.shape[0]

    def ffn(x_ref, g_ref, wi_ref, wo_ref, ff_ref, dff_ref, h_ref):
        x = x_ref[...]
        h = _rms_fwd(x, g_ref[...]).astype(BF16)
        h_ref[...] = h
        acc = x
        for k in range(4):
            gate = _dot_nt(h, wi_ref[0, k])
            up = _dot_nt(h, wi_ref[1, k])
            sg = _sigmoid(gate)
            silu = gate * sg
            ff = (silu * up).astype(BF16)
            ff_ref[k] = ff
            dff_ref[0, k] = (up * (sg * (1.0 + gate * (1.0 - sg)))).astype(BF16)
            dff_ref[1, k] = silu.astype(BF16)
            acc = acc + _dot(ff, wo_ref[k])
        return acc

    def body(x_ref, g_ref, wi_ref, wo_ref, x2_ref, ff_ref, dff_ref, h_ref):
        x2_ref[...] = ffn(x_ref, g_ref, wi_ref, wo_ref, ff_ref, dff_ref, h_ref)

    def body_with_head(x_ref, g_ref, wi_ref, wo_ref, fg_ref, t_ref, dx_ref, loss_ref, dfg_ref, ff_ref, dff_ref, h_ref):
        @pl.when(pl.program_id(0) == 0)
        def _():
            loss_ref[...] = jnp.zeros_like(loss_ref)
            dfg_ref[...] = jnp.zeros_like(dfg_ref)

        x2 = ffn(x_ref, g_ref, wi_ref, wo_ref, ff_ref, dff_ref, h_ref)
        fg = fg_ref[...]
        e = _rms_fwd(x2, fg) - t_ref[...]
        rows = jnp.sum(e * e, axis=-1, keepdims=True)
        loss_ref[...] += (0.5 / D) * jnp.sum(rows, axis=0, keepdims=True)
        dx, dg = _rms_bwd(x2, fg, e * (1.0 / D))
        dx_ref[...] = dx
        _add_rows128(dfg_ref, dg)

    tile = pl.BlockSpec((tm, D), lambda i: (i, 0))
    weights = [_resident((2, 4, FF_BLK, D)), _resident((4, FF_BLK, D))]
    kept_specs = [pl.BlockSpec((4, tm, FF_BLK), lambda i: (0, i, 0)),
                  pl.BlockSpec((2, 4, tm, FF_BLK), lambda i: (0, 0, i, 0)), tile]
    kept_shapes = [SDS((4, s, FF_BLK), BF16), SDS((2, 4, s, FF_BLK), BF16), SDS((s, D), BF16)]
    if head is None:
        return pl.pallas_call(
            body, name=f"ffn_fwd_l{layer}", grid=(s // tm,),
            in_specs=[tile, _full((1, D))] + weights, out_specs=[tile] + kept_specs,
            out_shape=[SDS((s, D), F32)] + kept_shapes, compiler_params=_cp("parallel"))(x1, g, wfi, wfo)
    final_g, tgt = head
    return pl.pallas_call(
        body_with_head, name=f"ffn_fwd_loss_l{layer}", grid=(s // tm,),
        in_specs=[tile, _full((1, D))] + weights + [_full((1, D)), tile],
        out_specs=[tile, _full((1, 1)), _full((LANE_ROWS, HD))] + kept_specs,
        out_shape=[SDS((s, D), F32), SDS((1, 1), F32), SDS((LANE_ROWS, HD), F32)] + kept_shapes,
        compiler_params=_cp("arbitrary"))(x1, g, wfi, wfo, final_g, tgt)


def _ffn_bwd(dx2, wfo, factors, wfi, x1, g, layer, tm):
    s = dx2.shape[0]

    def body(dx_ref, wo_ref, f_ref, wi_ref, x_ref, g_ref, dgu_ref, dx1_ref, dg_ref):
        @pl.when(pl.program_id(0) == 0)
        def _():
            dg_ref[...] = jnp.zeros_like(dg_ref)

        dx = dx_ref[...]
        dxb = dx.astype(BF16)
        dh = None
        for k in range(4):
            dff = _dot_nt(dxb, wo_ref[k])
            d_gate = (dff * f_ref[0, k].astype(F32)).astype(BF16)
            d_up = (dff * f_ref[1, k].astype(F32)).astype(BF16)
            dgu_ref[0, k] = d_gate
            dgu_ref[1, k] = d_up
            part = _dot(d_gate, wi_ref[k]) + _dot(d_up, wi_ref[4 + k])
            dh = part if dh is None else dh + part
        dxn, dg = _rms_bwd(x_ref[...], g_ref[...], dh)
        dx1_ref[...] = dx + dxn
        _add_rows128(dg_ref, dg)

    tile = pl.BlockSpec((tm, D), lambda i: (i, 0))
    blk = pl.BlockSpec((2, 4, tm, FF_BLK), lambda i: (0, 0, i, 0))
    return pl.pallas_call(
        body, name=f"ffn_bwd_l{layer}", grid=(s // tm,),
        in_specs=[tile, _resident((4, FF_BLK, D)), blk, _resident((N_DEV, FF_BLK, D)), tile, _full((1, D))],
        out_specs=[blk, tile, _full((LANE_ROWS, HD))],
        out_shape=[SDS((2, 4, s, FF_BLK), BF16), SDS((s, D), F32), SDS((LANE_ROWS, HD), F32)],
        compiler_params=_cp("arbitrary"))(dx2, wfo, factors, wfi, x1, g)


def _mm_nt_rms_bwd(a, a_spec, a_blocks, w, w_is_transposed, x, g, dres, name, tm, after=(), jobs=()):
    s = x.shape[0]

    def body(a_ref, w_ref, x_ref, g_ref, dres_ref, dx_ref, dg_ref):
        @pl.when(pl.program_id(0) == 0)
        def _():
            dg_ref[...] = jnp.zeros_like(dg_ref)

        dh = None
        for k, blk in enumerate(a_blocks(a_ref)):
            part = _dot(blk, w_ref[k]) if w_is_transposed else _dot_nt(blk, w_ref[k])
            dh = part if dh is None else dh + part
        dx, dg = _rms_bwd(x_ref[...], g_ref[...], dh)
        dx_ref[...] = dres_ref[...] + dx
        _add_rows128(dg_ref, dg)

    tile = pl.BlockSpec((tm, D), lambda i: (i, 0))
    body, dep_specs, deps = _behind(after, body, 5)
    body, job_in, job_args, job_out, job_shapes, aliases = _carry(jobs, s // tm, body, 5 + len(deps), 2)
    res = pl.pallas_call(
        body, name=name, grid=(s // tm,),
        in_specs=[a_spec, _resident(w.shape), tile, _full((1, D)), tile] + dep_specs + job_in,
        out_specs=[tile, _full((LANE_ROWS, HD))] + job_out,
        out_shape=[SDS((s, D), F32), SDS((LANE_ROWS, HD), F32)] + job_shapes,
        input_output_aliases=aliases, compiler_params=_cp("arbitrary"))(a, w, x, g, dres, *deps, *job_args)
    for j, job in enumerate(jobs):
        job.results = res[2 + 4 * j:6 + 4 * j]
    return res[0], res[1]


def _mm_tn(a, a_spec, b, b_spec, nb, out_shape, out_spec, name, a_is_transposed=True, after=None, jobs=()):
    def body(a_ref, b_ref, *rest):
        o_ref = rest[-1]
        bb = b_ref[...].astype(BF16)
        o_ref[...] = (_dot(a_ref[...], bb) if a_is_transposed else _dot_tn(a_ref[...], bb)).astype(BF16)

    deps = [] if after is None else [after]
    body, job_in, job_args, job_out, job_shapes, aliases = _carry(jobs, nb, body, 2 + len(deps), 1)
    res = pl.pallas_call(
        body, name=name, grid=(nb,), in_specs=[a_spec, b_spec] + [_ANY] * len(deps) + job_in,
        out_specs=[out_spec] + job_out, out_shape=[SDS(out_shape, BF16)] + job_shapes,
        input_output_aliases=aliases, compiler_params=_cp("parallel"))(a, b, *deps, *job_args)
    for j, job in enumerate(jobs):
        job.results = res[1 + 4 * j:5 + 4 * j]
    return res[0]


def _mixer_bwd(dx1, wo, h0, h1, z, lng, lnb, ws, wst, bsb, layer, tm, after=()):
    s = dx1.shape[0]
    nt = s // tm

    def body(dx_ref, wo_ref, h0_ref, h1_ref, zu_ref, zv_ref, zg_ref, za_ref, zb_ref, lng_ref, lnb_ref,
             ws_ref, wst_ref, bsb_ref, dz_ref, dh_ref, dws_ref, dbs_ref, dlng_ref, dlnb_ref,
             du_s, dv_s, ya_s, dbs_acc):
        i = pl.program_id(0)

        @pl.when(i == 0)
        def _():
            for r in (dws_ref, dlng_ref, dlnb_ref, dbs_acc):
                r[...] = jnp.zeros_like(r)

        dm = _dot_nt(dx_ref[...].astype(BF16), wo_ref[...])
        sa = _sigmoid(za_ref[...].astype(F32))
        sb = _sigmoid(zb_ref[...].astype(F32))
        zg = zg_ref[...].astype(F32)
        gg, tg = _gelu(zg)
        hs = h0_ref[...] + h1_ref[...]
        dyb = dm * sb
        dya = dm * sa
        dh_ref[...] = dyb * gg
        dz_ref[:, 2 * D:3 * D] = jnp.zeros((tm, D), BF16)
        dz_ref[:, 3 * D:4 * D] = (dyb * hs * _gelu_grad(zg, tg)).astype(BF16)
        dz_ref[:, 5 * D:6 * D] = (dm * (hs * gg) * (sb * (1.0 - sb))).astype(BF16)

        zu, zv, u, tu, tv, xh, rstd, vb = _gmlp_values(zu_ref, zv_ref, lng_ref, lnb_ref)
        for c in range(tm // HD):
            rs = slice(c * HD, (c + 1) * HD)
            for g in range(HEADS):
                cs = slice(g * HD, (g + 1) * HD)
                vblk = vb[rs, cs]
                mixed = _dot(ws_ref[g], vblk) + bsb_ref[g]
                ya_s[rs, cs] = u[rs, cs] * mixed
                du_s[rs, cs] = dya[rs, cs] * mixed
                dmx = dya[rs, cs] * u[rs, cs]
                dbs_acc[g] += dmx
                dmxb = dmx.astype(BF16)
                dws_ref[g] += _dot_nt(dmxb, vblk)
                dv_s[rs, cs] = _dot(wst_ref[g], dmxb)
        dz_ref[:, 4 * D:5 * D] = (dm * ya_s[...] * (sa * (1.0 - sa))).astype(BF16)
        dv = dv_s[...]
        _add_rows128(dlng_ref, jnp.sum(dv * xh, axis=0, keepdims=True))
        _add_rows128(dlnb_ref, jnp.sum(dv, axis=0, keepdims=True))
        dxh = dv * lng_ref[...]
        dgv = rstd * (dxh - jnp.mean(dxh, axis=-1, keepdims=True)
                      - xh * jnp.mean(dxh * xh, axis=-1, keepdims=True))
        dz_ref[:, 0:D] = (du_s[...] * _gelu_grad(zu, tu)).astype(BF16)
        dz_ref[:, D:2 * D] = (dgv * _gelu_grad(zv, tv)).astype(BF16)

        @pl.when(i == nt - 1)
        def _():
            for g in range(HEADS):
                dbs_ref[g:g + 1, :] = jnp.sum(dbs_acc[g].T, axis=0, keepdims=True)

    tile = pl.BlockSpec((tm, D), lambda i: (i, 0))
    wspec = _full((HEADS, HD, HD))
    body, dep_specs, deps = _behind(after, body, 14)
    return pl.pallas_call(
        body, name=f"mixer_bwd_l{layer}", grid=(nt,),
        in_specs=[tile, _full((D, D)), tile, tile]
        + [pl.BlockSpec((tm, D), lambda i, c=c: (i, c)) for c in (0, 1, 3, 4, 5)]
        + [_full((1, D)), _full((1, D)), wspec, wspec, wspec] + dep_specs,
        out_specs=[pl.BlockSpec((tm, N_IN), lambda i: (i, 0)), tile, wspec, _full((HEADS, HD)),
                   _full((LANE_ROWS, HD)), _full((LANE_ROWS, HD))],
        out_shape=[SDS((s, N_IN), BF16), SDS((s, D), F32), SDS((HEADS, HD, HD), F32), SDS((HEADS, HD), F32),
                   SDS((LANE_ROWS, HD), F32), SDS((LANE_ROWS, HD), F32)],
        scratch_shapes=[pltpu.VMEM((tm, D), F32)] * 3 + [pltpu.VMEM((HEADS, HD, HD), F32)],
        compiler_params=_cp("arbitrary"))(dx1, wo, h0, h1, z, z, z, z, z, lng, lnb, ws, wst, bsb, *deps)


def _lru_gates_bwd(xcb, gates, h0, h1, g0, g1, wr, wi, lam, layer, tm, after=()):
    s = xcb.shape[0]
    nt = s // tm

    def body(xc_ref, r0_ref, i0_ref, r1_ref, i1_ref, h0p_ref, h0_ref, h1_ref, h1n_ref, g0_ref, g1_ref,
             wr_ref, wi_ref, lam_ref, dxc_ref, dwr_ref, dwi_ref, dbr_ref, dbi_ref, dlam_ref):
        i = pl.program_id(0)
        fp, fn = _halo_flags(nt)

        @pl.when(i == 0)
        def _():
            for r in (dwr_ref, dwi_ref, dbr_ref, dbi_ref, dlam_ref):
                r[...] = jnp.zeros_like(r)

        xb = xc_ref[...]
        xc = xb.astype(F32)
        zeros8 = jnp.zeros((8, D), F32)
        h_prev = _taps(h0p_ref[...] * fp, h0_ref[...], zeros8, tm)[1]
        h_next = _taps(zeros8, h1_ref[...], h1n_ref[...] * fn, tm)[3]
        dxc = jnp.zeros((tm, D), F32)
        for d, (g_ref, hsh, r_ref, i_ref) in enumerate(((g0_ref, h_prev, r0_ref, i0_ref),
                                                        (g1_ref, h_next, r1_ref, i1_ref))):
            sp = _softplus(-lam_ref[d:d + 1, :])
            r = r_ref[...].astype(F32)
            ig = i_ref[...].astype(F32)
            a, q = _lru_decay(r, sp)
            rmult = jnp.where(q > 0.0, lax.rsqrt(jnp.where(q > 0.0, q, 1.0)), 0.0)
            mult = q * rmult
            db = g_ref[...]
            da = db * hsh
            dmult = db * (ig * xc)
            di = db * (mult * xc)
            dxc = dxc + db * (mult * ig)
            dla = da * a - dmult * (a * a * rmult)
            dsp_dlam = -_sigmoid(-lam_ref[d:d + 1, :])
            _add_rows128(dlam_ref, jnp.sum(dla * r, axis=0, keepdims=True) * ((-LRU_C) * dsp_dlam), d * LANE_ROWS)
            dpr = dla * sp * (-LRU_C) * (r * (1.0 - r))
            dpi = di * (ig * (1.0 - ig))
            _add_rows128(dbr_ref, jnp.sum(dpr, axis=0, keepdims=True), d * LANE_ROWS)
            _add_rows128(dbi_ref, jnp.sum(dpi, axis=0, keepdims=True), d * LANE_ROWS)
            dprb = dpr.astype(BF16)
            dpib = dpi.astype(BF16)
            parts = []
            for h in range(HEADS):
                cs = slice(h * HD, (h + 1) * HD)
                dwr_ref[d, h] += _dot_tn(xb[:, cs], dprb[:, cs])
                dwi_ref[d, h] += _dot_tn(xb[:, cs], dpib[:, cs])
                parts.append(_dot_nt(dprb[:, cs], wr_ref[d, h]) + _dot_nt(dpib[:, cs], wi_ref[d, h]))
            dxc = dxc + jnp.concatenate(parts, axis=1)
        dxc_ref[...] = dxc.astype(BF16)

    tile = pl.BlockSpec((tm, D), lambda i: (i, 0))
    hp, hc, hn = _halo_specs(tm, s, 0)
    wspec = _full((2, HEADS, HD, HD))
    vspec = _full((2 * LANE_ROWS, HD))
    body, dep_specs, deps = _behind(after, body, 14)
    return pl.pallas_call(
        body, name=f"lru_gates_bwd_l{layer}", grid=(nt,),
        in_specs=[tile] * 5 + [hp, hc, hc, hn, tile, tile, wspec, wspec, _full((2, D))] + dep_specs,
        out_specs=[tile, wspec, wspec, vspec, vspec, vspec],
        out_shape=[SDS((s, D), BF16), SDS((2, HEADS, HD, HD), F32), SDS((2, HEADS, HD, HD), F32)]
        + [SDS((2 * LANE_ROWS, HD), F32)] * 3,
        compiler_params=_cp("arbitrary"))(xcb, *gates, h0, h0, h1, h1, g0, g1, wr, wi, lam, *deps)


def _conv_bwd(dz, dxc, z, cw, layer, tm):
    s = z.shape[0]
    nt = s // tm

    def body(dz_in, dp_ref, dc_ref, dn_ref, zp_ref, zc_ref, zn_ref, cw_ref, dz_ref, dcw_ref, dcb_ref):
        del dz_in
        fp, fn = _halo_flags(nt)

        @pl.when(pl.program_id(0) == 0)
        def _():
            dcw_ref[...] = jnp.zeros_like(dcw_ref)
            dcb_ref[...] = jnp.zeros_like(dcb_ref)

        dxc_halo = _halo_load(dp_ref, dc_ref, dn_ref, fp, fn)
        dxc = dxc_halo[1]
        dm2, dm1, _, dp1, _ = _taps(*dxc_halo, tm)
        dz_ref[...] = (cw_ref[0:1, :] * dp1 + cw_ref[1:2, :] * dxc + cw_ref[2:3, :] * dm1
                       + cw_ref[3:4, :] * dm2).astype(BF16)
        _, zm1, z0, zp1, zp2 = _taps(*_halo_load(zp_ref, zc_ref, zn_ref, fp, fn), tm)
        for k, zt in enumerate((zm1, z0, zp1, zp2)):
            _add_rows128(dcw_ref, jnp.sum(dxc * zt, axis=0, keepdims=True), k * LANE_ROWS)
        _add_rows128(dcb_ref, jnp.sum(dxc, axis=0, keepdims=True))

    return pl.pallas_call(
        body, name=f"conv_bwd_l{layer}", grid=(nt,),
        in_specs=[pl.BlockSpec(memory_space=pl.ANY), *_halo_specs(tm, s, 0, 16), *_halo_specs(tm, s, 2, 16),
                  _full((4, D))],
        out_specs=[pl.BlockSpec((tm, D), lambda i: (i, 2)), _full((4 * LANE_ROWS, HD)), _full((LANE_ROWS, HD))],
        out_shape=[SDS((s, N_IN), BF16), SDS((4 * LANE_ROWS, HD), F32), SDS((LANE_ROWS, HD), F32)],
        input_output_aliases={0: 0},
        compiler_params=_cp("arbitrary"))(dz, dxc, dxc, dxc, z, z, z, cw)


def _me():
    return lax.axis_index("x"), lax.axis_index("y"), lax.axis_index("c")


def _peer(m):
    x, y, c = _me()
    px = 1 - x if m & 4 else x
    py = 1 - y if m & 2 else y
    pc = 1 - c if m & 1 else c
    return (px, py, pc), 4 * px + 2 * py + pc


_ANY = pl.BlockSpec(memory_space=pl.ANY)
_EXCHANGE_SEMS = [pltpu.SemaphoreType.DMA((N_DEV - 1,)), pltpu.SemaphoreType.DMA((N_DEV - 1,)), pltpu.SemaphoreType.DMA(())]


def _all_gather(v, after, name):
    def body(v_ref, after_ref, o_ref, send_sems, recv_sems, local_sem):
        del after_ref
        x, y, c = _me()
        me = 4 * x + 2 * y + c
        local = pltpu.make_async_copy(v_ref, o_ref.at[me], local_sem)
        local.start()
        sends = []
        for m in range(1, N_DEV):
            dev, _ = _peer(m)
            cp = pltpu.make_async_remote_copy(v_ref, o_ref.at[me], send_sems.at[m - 1], recv_sems.at[m - 1],
                                              device_id=dev, device_id_type=pl.DeviceIdType.MESH)
            cp.start()
            sends.append(cp)
        for m in range(1, N_DEV):
            dev, blk = _peer(m)
            pltpu.make_async_remote_copy(v_ref, o_ref.at[blk], send_sems.at[m - 1], recv_sems.at[m - 1],
                                         device_id=dev, device_id_type=pl.DeviceIdType.MESH).wait_recv()
        for cp in sends:
            cp.wait_send()
        local.wait()

    return pl.pallas_call(
        body, name=name, in_specs=[_ANY, _ANY], out_specs=_ANY,
        out_shape=SDS((N_DEV,) + v.shape, v.dtype), scratch_shapes=_EXCHANGE_SEMS)(v, after)


_HBM = pl.BlockSpec(memory_space=pltpu.HBM)
_SEM = pl.BlockSpec(memory_space=pltpu.SEMAPHORE)
_EFFECT = pltpu.CompilerParams(has_side_effects=pltpu.SideEffectType.DATAFLOW_SIDE_EFFECTING)
_PEER_SEMS = pltpu.SemaphoreType.DMA((N_DEV - 1,))


def _in_hbm(a):
    return pltpu.with_memory_space_constraint(a, pltpu.HBM)


def _remote(src, dst, send_sems, recv_sems, m):
    dev, _ = _peer(m)
    return pltpu.make_async_remote_copy(src, dst, send_sems.at[m - 1], recv_sems.at[m - 1],
                                        device_id=dev, device_id_type=pl.DeviceIdType.MESH)


def _gather_start(lands, after, name):
    n = len(lands)

    def body(*refs):
        land = refs[:n]
        sems = refs[n + 1:3 * n + 1]
        token = refs[-1]
        x, y, c = _me()
        me = 4 * x + 2 * y + c
        for t in range(n):
            for m in range(1, N_DEV):
                _remote(land[t].at[me], land[t].at[me], sems[2 * t], sems[2 * t + 1], m).start()
        token[...] = jnp.zeros_like(token)

    res = pl.pallas_call(
        body, name=name, in_specs=[_HBM] * n + [_ANY],
        out_specs=[_SEM] * (2 * n) + [_HBM] * n + [pl.BlockSpec(memory_space=pltpu.VMEM)],
        out_shape=[_PEER_SEMS] * (2 * n) + [pltpu.HBM(a.shape, a.dtype) for a in lands] + [SDS((8, 128), F32)],
        input_output_aliases={t: 2 * n + t for t in range(n)},
        compiler_params=_EFFECT)(*[_in_hbm(a) for a in lands], after)
    return [(res[2 * t], res[2 * t + 1], res[2 * n + t]) for t in range(n)], res[-1]


def _gather_wait(handle, after, name):
    send_sems, recv_sems, land = handle

    def body(land_ref, ssem, rsem, after_ref, out_ref):
        del after_ref, out_ref
        x, y, c = _me()
        me = 4 * x + 2 * y + c
        for m in range(1, N_DEV):
            _, blk = _peer(m)
            cp = _remote(land_ref.at[me], land_ref.at[blk], ssem, rsem, m)
            cp.wait_send()
            cp.wait_recv()

    return pl.pallas_call(
        body, name=name, in_specs=[_HBM, _SEM, _SEM, _ANY], out_specs=_HBM,
        out_shape=pltpu.HBM(land.shape, land.dtype), input_output_aliases={0: 0},
        compiler_params=_EFFECT)(land, send_sems, recv_sems, after)


FIRST_STAGE = (1, 2, 4, 6)
RELAYED = (2, 4, 6)
OTHER_CORE = 1


def _stage_copy(src, dst, send_sems, recv_sems, k, m):
    dev, _ = _peer(m)
    return pltpu.make_async_remote_copy(src, dst, send_sems.at[k], recv_sems.at[k],
                                        device_id=dev, device_id_type=pl.DeviceIdType.MESH)


def _gather2_start(lands, after, name):
    n = len(lands)

    def body(*refs):
        land = refs[:n]
        sems = refs[n + 1:3 * n + 1]
        token = refs[-1]
        x, y, c = _me()
        me = 4 * x + 2 * y + c
        for t in range(n):
            for k, m in enumerate(FIRST_STAGE):
                _stage_copy(land[t].at[me], land[t].at[me], sems[2 * t], sems[2 * t + 1], k, m).start()
        token[...] = jnp.zeros_like(token)

    stage_sems = pltpu.SemaphoreType.DMA((len(FIRST_STAGE),))
    res = pl.pallas_call(
        body, name=name, in_specs=[_HBM] * n + [_ANY],
        out_specs=[_SEM] * (2 * n) + [_HBM] * n + [pl.BlockSpec(memory_space=pltpu.VMEM)],
        out_shape=[stage_sems] * (2 * n) + [pltpu.HBM(a.shape, a.dtype) for a in lands] + [SDS((8, 128), F32)],
        input_output_aliases={t: 2 * n + t for t in range(n)},
        compiler_params=_EFFECT)(*[_in_hbm(a) for a in lands], after)
    return [(res[2 * t], res[2 * t + 1], res[2 * n + t]) for t in range(n)], res[-1]


def _gather2_relay(handles, after, name):
    n = len(handles)

    def body(*refs):
        land, send1, recv1 = refs[:n], refs[n:2 * n], refs[2 * n:3 * n]
        sems = refs[3 * n + 1:5 * n + 1]
        token = refs[-1]
        x, y, c = _me()
        me = 4 * x + 2 * y + c
        for t in range(n):
            for j, m in enumerate(RELAYED):
                _, blk = _peer(m)
                _stage_copy(land[t].at[me], land[t].at[blk], send1[t], recv1[t], 1 + j, m).wait_recv()
                _stage_copy(land[t].at[blk], land[t].at[blk], sems[2 * t], sems[2 * t + 1], j, OTHER_CORE).start()
        token[...] = jnp.zeros_like(token)

    relay_sems = pltpu.SemaphoreType.DMA((len(RELAYED),))
    lands = [h[2] for h in handles]
    res = pl.pallas_call(
        body, name=name, in_specs=[_HBM] * n + [_SEM] * (2 * n) + [_ANY],
        out_specs=[_SEM] * (2 * n) + [_HBM] * n + [pl.BlockSpec(memory_space=pltpu.VMEM)],
        out_shape=[relay_sems] * (2 * n) + [pltpu.HBM(a.shape, a.dtype) for a in lands] + [SDS((8, 128), F32)],
        input_output_aliases={t: 2 * n + t for t in range(n)},
        compiler_params=_EFFECT)(*lands, *[h[0] for h in handles], *[h[1] for h in handles], after)
    return [(h[0], h[1], res[2 * t], res[2 * t + 1], res[2 * n + t]) for t, h in enumerate(handles)], res[-1]


def _gather2_wait(handle, after, name):
    send1, recv1, send2, recv2, land = handle

    def body(land_ref, s1, r1, s2, r2, after_ref, out_ref):
        del after_ref, out_ref
        x, y, c = _me()
        me = 4 * x + 2 * y + c
        _, other = _peer(OTHER_CORE)
        _stage_copy(land_ref.at[me], land_ref.at[other], s1, r1, 0, OTHER_CORE).wait_recv()
        for k, m in enumerate(FIRST_STAGE):
            _stage_copy(land_ref.at[me], land_ref.at[me], s1, r1, k, m).wait_send()
        for j, m in enumerate(RELAYED):
            _, mine = _peer(m)
            _, theirs = _peer(m ^ OTHER_CORE)
            _stage_copy(land_ref.at[mine], land_ref.at[mine], s2, r2, j, OTHER_CORE).wait_send()
            _stage_copy(land_ref.at[mine], land_ref.at[theirs], s2, r2, j, OTHER_CORE).wait_recv()

    return pl.pallas_call(
        body, name=name, in_specs=[_HBM] + [_SEM] * 4 + [_ANY], out_specs=_HBM,
        out_shape=pltpu.HBM(land.shape, land.dtype), input_output_aliases={0: 0},
        compiler_params=_EFFECT)(land, send1, recv1, send2, recv2, after)


def _exchange_start(ps, name):
    n = len(ps)

    def body(*refs):
        p = refs[:n]
        got = refs[n:2 * n]
        sems = refs[2 * n:5 * n]
        token = refs[-1]
        x, y, c = _me()
        me = 4 * x + 2 * y + c
        for t in range(n):
            pltpu.make_async_copy(p[t].at[me], got[t].at[me], sems[3 * t + 2]).start()
            for m in range(1, N_DEV):
                _, blk = _peer(m)
                _remote(p[t].at[blk], got[t].at[me], sems[3 * t], sems[3 * t + 1], m).start()
        token[...] = jnp.zeros_like(token)

    res = pl.pallas_call(
        body, name=name, in_specs=[_HBM] * (2 * n),
        out_specs=[_SEM] * (3 * n) + [_HBM] * (2 * n) + [pl.BlockSpec(memory_space=pltpu.VMEM)],
        out_shape=[_PEER_SEMS, _PEER_SEMS, pltpu.SemaphoreType.DMA(())] * n
        + [pltpu.HBM(a.shape, a.dtype) for a in ps] * 2 + [SDS((8, 128), F32)],
        input_output_aliases={t: 3 * n + t for t in range(2 * n)},
        compiler_params=_EFFECT)(*[_in_hbm(a) for a in ps], *[_in_hbm(lax.empty(a.shape, a.dtype)) for a in ps])
    return [(res[3 * t], res[3 * t + 1], res[3 * t + 2], res[3 * n + t], res[4 * n + t]) for t in range(n)], res[-1]


def _exchange_wait(handle, after, name):
    send_sems, recv_sems, local_sem, p, got = handle

    def body(p_ref, got_ref, ssem, rsem, lsem, after_ref, p_out, got_out):
        del after_ref, p_out, got_out
        x, y, c = _me()
        me = 4 * x + 2 * y + c
        pltpu.make_async_copy(p_ref.at[me], got_ref.at[me], lsem).wait()
        for m in range(1, N_DEV):
            _, blk = _peer(m)
            cp = _remote(p_ref.at[blk], got_ref.at[blk], ssem, rsem, m)
            cp.wait_send()
            cp.wait_recv()

    return pl.pallas_call(
        body, name=name, in_specs=[_HBM, _HBM, _SEM, _SEM, _SEM, _ANY], out_specs=[_HBM, _HBM],
        out_shape=[pltpu.HBM(p.shape, p.dtype), pltpu.HBM(got.shape, got.dtype)],
        input_output_aliases={0: 0, 1: 1}, compiler_params=_EFFECT)(p, got, send_sems, recv_sems, local_sem, after)[1]


CHIPS = (0, 2, 4, 6)


def _pairs_start(p, name):
    def body(p_ref, pair_ref, ssem, rsem, p_out, pair_out, token):
        del p_out, pair_out
        for k, chip in enumerate(CHIPS):
            _, blk = _peer(chip ^ OTHER_CORE)
            _stage_copy(p_ref.at[blk], pair_ref.at[k], ssem, rsem, k, OTHER_CORE).start()
        token[...] = jnp.zeros_like(token)

    sems = pltpu.SemaphoreType.DMA((len(CHIPS),))
    pair = lax.empty((len(CHIPS),) + p.shape[1:], p.dtype)
    res = pl.pallas_call(
        body, name=name, in_specs=[_HBM] * 2,
        out_specs=[_SEM] * 2 + [_HBM] * 2 + [pl.BlockSpec(memory_space=pltpu.VMEM)],
        out_shape=[sems, sems, pltpu.HBM(p.shape, p.dtype), pltpu.HBM(pair.shape, pair.dtype), SDS((8, 128), F32)],
        input_output_aliases={0: 2, 1: 3}, compiler_params=_EFFECT)(_in_hbm(p), _in_hbm(pair))
    return res[:4], res[4]


def _pairs_wait(handle, after, name):
    send_sems, recv_sems, p, pair = handle

    def body(p_ref, pair_ref, ssem, rsem, after_ref, p_out, pair_out):
        del after_ref, p_out, pair_out
        for k, chip in enumerate(CHIPS):
            _, blk = _peer(chip ^ OTHER_CORE)
            cp = _stage_copy(p_ref.at[blk], pair_ref.at[k], ssem, rsem, k, OTHER_CORE)
            cp.wait_send()
            cp.wait_recv()

    return pl.pallas_call(
        body, name=name, in_specs=[_HBM, _HBM, _SEM, _SEM, _ANY], out_specs=[_HBM, _HBM],
        out_shape=[pltpu.HBM(p.shape, p.dtype), pltpu.HBM(pair.shape, pair.dtype)],
        input_output_aliases={0: 0, 1: 1}, compiler_params=_EFFECT)(p, pair, send_sems, recv_sems, after)


def _sum_pairs(p, pair, me1, name):
    _, r, c = pair.shape
    tr = _row_tile(r)

    def body(me_ref, p_ref, pair_ref, o_ref):
        del me_ref
        o_ref[...] = (p_ref[...].astype(F32) + pair_ref[...].astype(F32)).astype(o_ref.dtype)

    def mine(k, i, me):
        chip = 2 * k
        return (jnp.bitwise_xor(me[0], chip), i, 0)

    assert CHIPS == tuple(2 * k for k in range(len(CHIPS)))
    blk = pl.BlockSpec((None, tr, c), lambda k, i, me: (k, i, 0))
    return pl.pallas_call(
        body, name=name,
        grid_spec=pltpu.PrefetchScalarGridSpec(
            num_scalar_prefetch=1, grid=(len(CHIPS), r // tr),
            in_specs=[pl.BlockSpec((None, tr, c), mine), blk], out_specs=blk),
        out_shape=SDS(pair.shape, pair.dtype), compiler_params=_cp("parallel", "parallel"))(me1, p, pair)


def _chips_start(q, name):
    def body(q_ref, got_ref, ssem, rsem, lsem, q_out, got_out, token):
        del q_out, got_out
        pltpu.make_async_copy(q_ref.at[0], got_ref.at[0], lsem).start()
        for k, chip in enumerate(CHIPS[1:]):
            _stage_copy(q_ref.at[k + 1], got_ref.at[k + 1], ssem, rsem, k, chip).start()
        token[...] = jnp.zeros_like(token)

    sems = pltpu.SemaphoreType.DMA((len(CHIPS) - 1,))
    res = pl.pallas_call(
        body, name=name, in_specs=[_HBM] * 2,
        out_specs=[_SEM] * 3 + [_HBM] * 2 + [pl.BlockSpec(memory_space=pltpu.VMEM)],
        out_shape=[sems, sems, pltpu.SemaphoreType.DMA(()), pltpu.HBM(q.shape, q.dtype), pltpu.HBM(q.shape, q.dtype),
                   SDS((8, 128), F32)],
        input_output_aliases={0: 3, 1: 4}, compiler_params=_EFFECT)(_in_hbm(q), _in_hbm(lax.empty(q.shape, q.dtype)))
    return res[:5], res[5]


def _chips_wait(handle, after, name):
    send_sems, recv_sems, local_sem, q, got = handle

    def body(q_ref, got_ref, ssem, rsem, lsem, after_ref, q_out, got_out):
        del after_ref, q_out, got_out
        pltpu.make_async_copy(q_ref.at[0], got_ref.at[0], lsem).wait()
        for k, chip in enumerate(CHIPS[1:]):
            cp = _stage_copy(q_ref.at[k + 1], got_ref.at[k + 1], ssem, rsem, k, chip)
            cp.wait_send()
            cp.wait_recv()

    return pl.pallas_call(
        body, name=name, in_specs=[_HBM, _HBM, _SEM, _SEM, _SEM, _ANY], out_specs=[_HBM, _HBM],
        out_shape=[pltpu.HBM(q.shape, q.dtype), pltpu.HBM(got.shape, got.dtype)],
        input_output_aliases={0: 0, 1: 1}, compiler_params=_EFFECT)(q, got, send_sems, recv_sems, local_sem, after)[1]


def _cast_into_slot(w, layer, me1, name):
    _, r, c = w.shape
    tr = next(t for t in (512, 352, r) if r % t == 0)

    def body(me_ref, w_ref, o_ref):
        del me_ref
        o_ref[...] = w_ref[...].astype(BF16)

    return pl.pallas_call(
        body, name=name,
        grid_spec=pltpu.PrefetchScalarGridSpec(
            num_scalar_prefetch=1, grid=(r // tr,),
            in_specs=[pl.BlockSpec((None, tr, c), lambda i, me: (layer, i, 0))],
            out_specs=pl.BlockSpec((None, tr, c), lambda i, me: (me[0], i, 0))),
        out_shape=SDS((N_DEV, r, c), BF16), compiler_params=_cp("arbitrary"))(me1, w)


def _cast_all_into_slots(ws, layers, me1, after, name):
    n = len(ws)

    def body(me_ref, *refs):
        del me_ref
        for w_ref, o_ref in zip(refs[:n], refs[n + 1:]):
            o_ref[...] = w_ref[...].astype(BF16)

    return pl.pallas_call(
        body, name=name,
        grid_spec=pltpu.PrefetchScalarGridSpec(
            num_scalar_prefetch=1, grid=(1,),
            in_specs=[pl.BlockSpec((None,) + a.shape[1:], lambda i, me, l=l: (l, 0, 0)) for a, l in zip(ws, layers)]
            + [_ANY],
            out_specs=[pl.BlockSpec((None,) + a.shape[1:], lambda i, me: (me[0], 0, 0)) for a in ws]),
        out_shape=[SDS((N_DEV,) + a.shape[1:], BF16) for a in ws],
        compiler_params=_cp("arbitrary"))(me1, *ws, after)


def _sum8_into_slot(p, me1, name):
    _, r, c = p.shape

    def body(me_ref, p_ref, o_ref):
        del me_ref
        acc = p_ref[0]
        for k in range(1, N_DEV):
            acc = acc + p_ref[k]
        o_ref[...] = acc

    return pl.pallas_call(
        body, name=name,
        grid_spec=pltpu.PrefetchScalarGridSpec(
            num_scalar_prefetch=1, grid=(1,),
            in_specs=[pl.BlockSpec(p.shape, lambda i, me: (0, 0, 0))],
            out_specs=pl.BlockSpec((None, r, c), lambda i, me: (me[0], 0, 0))),
        out_shape=SDS(p.shape, F32), compiler_params=_cp("arbitrary"))(me1, p)


def _adamw(w, g, m, v):
    m = ADAM_B1 * m + (1.0 - ADAM_B1) * g
    v = ADAM_B2 * v + (1.0 - ADAM_B2) * (g * g)
    m_hat = m / (1.0 - ADAM_B1 ** ADAM_STEP)
    v_hat = v / (1.0 - ADAM_B2 ** ADAM_STEP)
    delta = -ADAM_LR * (m_hat / (jnp.sqrt(v_hat) + ADAM_EPS) + ADAM_WD * w)
    return delta, m, v


def _adam_tile(p_ref, w_ref, m_ref, v_ref, g_ref, d_ref, nm_ref, nv_ref):
    g = p_ref[0].astype(F32)
    for k in range(1, p_ref.shape[0]):
        g = g + p_ref[k].astype(F32)
    delta, nm, nv = _adamw(w_ref[...], g, m_ref[...], v_ref[...])
    g_ref[...] = g
    d_ref[...] = delta
    nm_ref[...] = nm
    nv_ref[...] = nv


class _AdamJob:
    def __init__(self, parts, w, m, v, layer, prev):
        self.args = [parts, w, m, v] + list(prev or ())
        self.layer, self.results = layer, None


def _carry(jobs, steps, body, n_in, n_out):
    in_specs, args, out_specs, out_shapes, aliases, n_prevs = [], [], [], [], {}, []
    for j, job in enumerate(jobs):
        _, r, c = job.args[0].shape
        nr = next(n for n in range(steps, 0, -1) if steps % n == 0 and r % (16 * n) == 0)
        nc = steps // nr
        assert c % (128 * nc) == 0
        tile = (r // nr, c // nc)
        blk = pl.BlockSpec((None,) + tile, lambda i, layer=job.layer, nc=nc: (layer, i // nc, i % nc))
        n_prev = len(job.args) - 4
        aliases.update({n_in + len(args) + 4 + k: n_out + 4 * j + k for k in range(n_prev)})
        in_specs += [pl.BlockSpec(job.args[0].shape[:1] + tile, lambda i, nc=nc: (0, i // nc, i % nc)), blk, blk, blk]
        in_specs += [_ANY] * n_prev
        args += job.args
        out_specs += [blk] * 4
        out_shapes += [SDS(job.args[1].shape, F32)] * 4
        n_prevs.append(n_prev)

    def carrying(*refs):
        ins, outs = refs[:n_in + len(args)], refs[n_in + len(args):]
        body(*ins[:n_in], *outs[:n_out])
        k = n_in
        for j, n_prev in enumerate(n_prevs):
            _adam_tile(*ins[k:k + 4], *outs[n_out + 4 * j:n_out + 4 * j + 4])
            k += 4 + n_prev

    return carrying, in_specs, args, out_specs, out_shapes, aliases


def _adam_shard(parts, w, m, v, layer, prev, name):
    n, r, c = parts.shape
    tr = next(t for t in (512, 352, r) if r % t == 0)
    n_prev = 0 if prev is None else 4

    def body(*refs):
        _adam_tile(*refs[:4], *refs[4 + n_prev:])

    blk = pl.BlockSpec((None, tr, c), lambda i: (layer, i, 0))
    return pl.pallas_call(
        body, name=name, grid=(r // tr,),
        in_specs=[pl.BlockSpec((n, tr, c), lambda i: (0, i, 0)), blk, blk, blk] + [_ANY] * n_prev,
        out_specs=[blk] * 4, out_shape=[SDS(w.shape, F32)] * 4,
        input_output_aliases={4 + k: k for k in range(n_prev)},
        compiler_params=_cp("parallel"))(parts, w, m, v, *(prev or ()))


SMALL_MATRICES = [("lru_w_r", 2048), ("lru_w_i", 2048), ("gmlp_w_s", 1024)]
SMALL_VECTORS = [("norm1_g", 8), ("gmlp_ln_g", 8), ("gmlp_ln_b", 8), ("gmlp_b_s", 8), ("conv_w", 32), ("conv_b", 8),
                 ("lru_b_r", 16), ("lru_b_i", 16), ("lru_lambda", 16), ("norm2_g", 8), ("final_g", 8)]
SMALL_VECTOR_ROW0 = sum(n for _, n in SMALL_MATRICES)
SMALL_VECTOR_BLOCK = 256
SMALL_ROWS = SMALL_VECTOR_ROW0 + SMALL_VECTOR_BLOCK
LOSS_ROW = SMALL_VECTOR_ROW0 + sum(n for _, n in SMALL_VECTORS)
assert LOSS_ROW + LANE_ROWS <= SMALL_ROWS


def _pack_small(small):
    parts = [small[k] for k, _ in SMALL_MATRICES]
    parts += [small[k] if k in small else jnp.zeros((n, HD), F32) for k, n in SMALL_VECTORS]
    parts += [small["loss"]] if "loss" in small else []
    flat = jnp.concatenate(parts)
    return jnp.pad(flat, ((0, SMALL_ROWS - flat.shape[0]), (0, 0))).reshape(N_DEV, SMALL_ROWS // N_DEV, HD)


def _adam_matrix(g0, g1, w, m, v, row0, name):
    _, rows, _ = w.shape
    tr = 512

    def body(g0_ref, g1_ref, w_ref, m_ref, v_ref, g_ref, d_ref, nm_ref, nv_ref):
        for l, src in enumerate((g0_ref, g1_ref)):
            g = src[...]
            delta, nm, nv = _adamw(w_ref[l], g, m_ref[l], v_ref[l])
            g_ref[l] = g
            d_ref[l] = delta
            nm_ref[l] = nm
            nv_ref[l] = nv

    gspec = pl.BlockSpec((tr, HD), lambda i: (row0 // tr + i, 0))
    blk = pl.BlockSpec((2, tr, HD), lambda i: (0, i, 0))
    return pl.pallas_call(body, name=name, grid=(rows // tr,), in_specs=[gspec, gspec] + [blk] * 3,
                          out_specs=[blk] * 4, out_shape=[SDS(w.shape, F32)] * 4,
                          compiler_params=_cp("parallel"))(g0, g1, w, m, v)


def _adam_vectors(g0, g1, dg1_parts, me1, ws, ms, vs):
    names = [k for k, _ in SMALL_VECTORS]
    n = len(names)

    def lanes(rows8):
        return jnp.concatenate([rows8[k:k + 1, :] for k in range(LANE_ROWS)], axis=1)

    def body(me_ref, g0_ref, g1_ref, dg1_ref, *refs):
        w_refs, m_refs, v_refs = refs[:n], refs[n:2 * n], refs[2 * n:3 * n]
        outs = refs[3 * n:]
        me = me_ref[0]
        g_refs = (g0_ref, g1_ref)

        def emit(i, idx, g):
            delta, nm, nv = _adamw(w_refs[i][idx], g, m_refs[i][idx], v_refs[i][idx])
            for j, val in enumerate((g, delta, nm, nv)):
                outs[4 * i + j][idx] = val

        off = 0
        for i, (name, rows) in enumerate(SMALL_VECTORS):
            for l in range(2):
                row = (slice(l, l + 1), slice(None))
                if name == "final_g":
                    if l == 1:
                        emit(i, (slice(0, 1), slice(None)), lanes(g1_ref[off:off + rows, :]))
                elif name == "norm1_g":
                    if l == 1:
                        emit(i, row, lanes(g0_ref[off:off + rows, :]))
                    else:
                        total = dg1_ref[0]
                        for k in range(1, N_DEV):
                            total = total + dg1_ref[k]
                        emit(i, row, lanes(total))
                elif name == "gmlp_b_s":
                    emit(i, (l,), g_refs[l][off:off + rows, :])
                elif rows == LANE_ROWS:
                    emit(i, row, lanes(g_refs[l][off:off + rows, :]))
                else:
                    for r in range(rows // LANE_ROWS):
                        emit(i, (l, slice(r, r + 1), slice(None)), g_refs[l][pl.ds(off + r * LANE_ROWS + me, 1), :])
            off += rows

    args = [ws[k] for k in names] + [ms[k] for k in names] + [vs[k] for k in names]
    gspec = pl.BlockSpec((SMALL_VECTOR_BLOCK, HD), lambda i, me: (SMALL_VECTOR_ROW0 // SMALL_VECTOR_BLOCK, 0))
    res = pl.pallas_call(
        body, name="adam_vectors",
        grid_spec=pltpu.PrefetchScalarGridSpec(
            num_scalar_prefetch=1, grid=(1,),
            in_specs=[gspec, gspec, _full(dg1_parts.shape)] + [_full(a.shape) for a in args],
            out_specs=[_full(ws[k].shape) for k in names for _ in range(4)]),
        out_shape=[SDS(ws[k].shape, F32) for k in names for _ in range(4)],
        compiler_params=_cp("arbitrary"))(me1, g0, g1, dg1_parts, *args)
    return {k: list(res[4 * i:4 * i + 4]) for i, k in enumerate(names)}


def _local_step(x, tgt, p, get_w, hook=lambda stage, layer, payload: None):
    s = x.shape[0]
    tm = _row_tile(s)
    wsb = p["gmlp_w_s"].astype(BF16)
    wstb = jnp.swapaxes(p["gmlp_w_s"], -1, -2).astype(BF16)
    bsb = jnp.broadcast_to(p["gmlp_b_s"][..., None], p["gmlp_w_s"].shape)
    wrb = p["lru_w_r"].astype(BF16)
    wib = p["lru_w_i"].astype(BF16)
    saved = []
    for l in range(2):
        win = get_w("w_in", l, x)
        z, h1 = _norm_inproj(x, p["norm1_g"][l][None], win, l, tm, after=[hook("pre_inproj", l, win)])
        a0, b0, a1, b1, xcb, *gates = _lru_gates_fwd(z, p["conv_w"][l], p["conv_b"][l][None], wrb[l], wib[l],
                                                     p["lru_b_r"][l], p["lru_b_i"][l], p["lru_lambda"][l], l, tm)
        h0, hr = _lru_scan(a0, b0, a1, b1, False, l)
        token = hook("pre_gmlp", l, h0)
        wout = get_w("w_out", l, h0 if token is None else token)
        x1, mg = _mixer_fwd(x, h0, hr, z, p["gmlp_ln_g"][l][None], p["gmlp_ln_b"][l][None], wsb[l], bsb[l], wout, l, tm)
        wfi = get_w("w_ffn_in", l, x1)
        wfo = get_w("w_ffn_out", l, x1)
        if l == 0:
            x2, ff, dff, h2 = _ffn_fwd(x1, p["norm2_g"][l][None], wfi, wfo, l, tm)
        else:
            dx, loss, dfg, ff, dff, h2 = _ffn_fwd(x1, p["norm2_g"][l][None], wfi, wfo, l, tm,
                                                  head=(p["final_g"][None], tgt))
        saved.append((x, z, h1, a0, a1, h0, hr, x1, mg, ff, dff, h2, win, wout, wfi, wfo, xcb, gates))
        x = x2
    for l in (1, 0):
        x0, z, h1, a0, a1, h0, hr, x1, mg, ff, dff, h2, win, wout, wfi, wfo, xcb, gates = saved[l]
        dgu, dx1, dg2 = _ffn_bwd(dx, wfo, dff, wfi.reshape(N_DEV, FF_BLK, D), x1, p["norm2_g"][l][None], l, tm)
        d_wfo = _mm_tn(ff, pl.BlockSpec((None, s, FF_BLK), lambda j: (j, 0, 0)), dx, _resident((s, D)),
                       4, (4, FF_BLK, D), pl.BlockSpec((None, FF_BLK, D), lambda j: (j, 0, 0)),
                       f"dw_ffn_out_l{l}", a_is_transposed=False)
        dgu8 = dgu.reshape(N_DEV, s, FF_BLK)
        d_wfi = _mm_tn(dgu8, pl.BlockSpec((None, s, FF_BLK), lambda j: (j, 0, 0)), h2, _resident((s, D)),
                       N_DEV, (N_DEV, FF_BLK, D), pl.BlockSpec((None, FF_BLK, D), lambda j: (j, 0, 0)),
                       f"dw_ffn_in_l{l}", a_is_transposed=False)
        d_wout = _mm_tn(mg, _resident((D, s)), dx1, pl.BlockSpec((s, D // 2), lambda j: (0, j)),
                        2, (D, D), pl.BlockSpec((D, D // 2), lambda j: (0, j)), f"dw_out_l{l}")
        token = hook("ffn_partials", l, dict(w_ffn_out=d_wfo.reshape(N_DEV, D_FF // N_DEV, D), w_ffn_in=d_wfi,
                                             w_out=d_wout.reshape(N_DEV, D // N_DEV, D)))
        pending = hook("mid_backward", l, dx1)
        dz, dh, dws, dbs, dlng, dlnb = _mixer_bwd(dx1, wout, h0, hr, z, p["gmlp_ln_g"][l][None], p["gmlp_ln_b"][l][None],
                                                  wsb[l], wstb[l], bsb[l], l, tm, after=[token])
        g1, g0 = _lru_scan(a1, dh, a0, dh, True, l)
        dxc, dwr, dwi, dbr, dbi, dlam = _lru_gates_bwd(
            xcb, gates, h0, hr, g0, g1, wrb[l], wib[l], p["lru_lambda"][l], l, tm, after=[pending])
        dz, dcw, dcb = _conv_bwd(dz, dxc, z, p["conv_w"][l], l, tm)
        small = dict(lru_w_r=dwr.reshape(-1, HD), lru_w_i=dwi.reshape(-1, HD), gmlp_w_s=dws.reshape(-1, HD),
                     gmlp_ln_g=dlng, gmlp_ln_b=dlnb, gmlp_b_s=dbs, conv_w=dcw, conv_b=dcb, lru_b_r=dbr,
                     lru_b_i=dbi, lru_lambda=dlam, norm2_g=dg2)
        if l == 1:
            small["final_g"] = dfg
            small["loss"] = jnp.broadcast_to(loss, (LANE_ROWS, HD))
        else:
            small["norm1_g"] = dg1
        started = hook("small_grads", l, small)
        d_win = _mm_tn(h1, _resident((D, s)), dz, pl.BlockSpec((s, IN_BLK), lambda j: (0, j)),
                       N_DEV, (N_DEV, D, IN_BLK), pl.BlockSpec((None, D, IN_BLK), lambda j: (j, 0, 0)),
                       f"dw_in_l{l}", after=started, jobs=hook("dw_in", l, started) or ())
        token = hook("mixer_partials", l, dict(w_in=d_win))
        dx, dg1 = _mm_nt_rms_bwd(
            dz, pl.BlockSpec((tm, N_IN), lambda i: (i, 0)),
            lambda r: [r[:, k * IN_BLK:(k + 1) * IN_BLK] for k in range(N_DEV)],
            win, False, x0, p["norm1_g"][l][None], dx1, f"inproj_bwd_dx_l{l}", tm,
            after=[token], jobs=hook("inproj_bwd_dx", l, token) or ())
    return loss, dx, dg1


_REPL = ["norm1_g", "gmlp_ln_g", "gmlp_ln_b", "gmlp_w_s", "gmlp_b_s", "conv_b", "lru_w_r", "lru_w_i", "norm2_g", "final_g"]
_LANE_SHARDED = ["conv_w", "lru_b_r", "lru_b_i", "lru_lambda"]
_BIG = ["w_in", "w_out", "w_ffn_in", "w_ffn_out"]
_ORDER = ["norm1_g", "w_in", "gmlp_ln_g", "gmlp_ln_b", "gmlp_w_s", "gmlp_b_s", "conv_w", "conv_b", "lru_w_r", "lru_b_r",
          "lru_w_i", "lru_b_i", "lru_lambda", "w_out", "norm2_g", "w_ffn_in", "w_ffn_out", "final_g"]


def kernel(x, norm1_g, w_in, gmlp_ln_g, gmlp_ln_b, gmlp_w_s, gmlp_b_s, conv_w, conv_b, lru_w_r, lru_b_r, lru_w_i, lru_b_i, lru_lambda, w_out, norm2_g, w_ffn_in, w_ffn_out, final_g, loss_target, m_norm1_g, m_w_in, m_gmlp_ln_g, m_gmlp_ln_b, m_gmlp_w_s, m_gmlp_b_s, m_conv_w, m_conv_b, m_lru_w_r, m_lru_b_r, m_lru_w_i, m_lru_b_i, m_lru_lambda, m_w_out, m_norm2_g, m_w_ffn_in, m_w_ffn_out, m_final_g, v_norm1_g, v_w_in, v_gmlp_ln_g, v_gmlp_ln_b, v_gmlp_w_s, v_gmlp_b_s, v_conv_w, v_conv_b, v_lru_w_r, v_lru_b_r, v_lru_w_i, v_lru_b_i, v_lru_lambda, v_w_out, v_norm2_g, v_w_ffn_in, v_w_ffn_out, v_final_g):
    w = dict(norm1_g=norm1_g, w_in=w_in, gmlp_ln_g=gmlp_ln_g, gmlp_ln_b=gmlp_ln_b, gmlp_w_s=gmlp_w_s, gmlp_b_s=gmlp_b_s,
             conv_w=conv_w, conv_b=conv_b, lru_w_r=lru_w_r, lru_b_r=lru_b_r, lru_w_i=lru_w_i, lru_b_i=lru_b_i,
             lru_lambda=lru_lambda, w_out=w_out, norm2_g=norm2_g, w_ffn_in=w_ffn_in, w_ffn_out=w_ffn_out, final_g=final_g)
    mom = dict(norm1_g=m_norm1_g, w_in=m_w_in, gmlp_ln_g=m_gmlp_ln_g, gmlp_ln_b=m_gmlp_ln_b, gmlp_w_s=m_gmlp_w_s,
               gmlp_b_s=m_gmlp_b_s, conv_w=m_conv_w, conv_b=m_conv_b, lru_w_r=m_lru_w_r, lru_b_r=m_lru_b_r,
               lru_w_i=m_lru_w_i, lru_b_i=m_lru_b_i, lru_lambda=m_lru_lambda, w_out=m_w_out, norm2_g=m_norm2_g,
               w_ffn_in=m_w_ffn_in, w_ffn_out=m_w_ffn_out, final_g=m_final_g)
    var = dict(norm1_g=v_norm1_g, w_in=v_w_in, gmlp_ln_g=v_gmlp_ln_g, gmlp_ln_b=v_gmlp_ln_b, gmlp_w_s=v_gmlp_w_s,
               gmlp_b_s=v_gmlp_b_s, conv_w=v_conv_w, conv_b=v_conv_b, lru_w_r=v_lru_w_r, lru_b_r=v_lru_b_r,
               lru_w_i=v_lru_w_i, lru_b_i=v_lru_b_i, lru_lambda=v_lru_lambda, w_out=v_w_out, norm2_g=v_norm2_g,
               w_ffn_in=v_w_ffn_in, w_ffn_out=v_w_ffn_out, final_g=v_final_g)
    for src in (w, mom, var):
        src["w_ffn_in"] = jnp.swapaxes(src["w_ffn_in"], 1, 2)
    xi, yi, ci = _me()
    me = 4 * xi + 2 * yi + ci

    lane_shapes = [w[k].shape for k in _LANE_SHARDED]
    lane_rows = sum(a[0] * a[1] for a in lane_shapes)
    packed = jnp.concatenate([w[k].reshape(-1, HD) for k in _LANE_SHARDED])
    packed = jnp.pad(packed, ((0, -lane_rows % 8), (0, 0)))

    me1 = jnp.reshape(me, (1,)).astype(jnp.int32)
    gathers = {}
    exchanges = {}
    views = dict(w_in=(N_DEV, D, IN_BLK), w_out=(D, D), w_ffn_in=(2, 4, FF_BLK, D), w_ffn_out=(4, FF_BLK, D))
    small_ex = {}
    small_ag = {}

    casts = {}

    def start_gather(names, l, after):
        lands = [casts[(k, l)] if (k, l) in casts else _cast_into_slot(w[k], l, me1, f"cast_{k}_l{l}") for k in names]
        started, tok = _gather2_start(lands, after, f"gather_start_{'_'.join(names)}_l{l}")
        gathers.update({(k, l): h for k, h in zip(names, started)})
        return tok

    def relay_gather(names, l, after):
        relayed, tok = _gather2_relay([gathers[(k, l)] for k in names], after, f"gather_relay_{'_'.join(names)}_l{l}")
        gathers.update({(k, l): h for k, h in zip(names, relayed)})
        return tok

    def get_w(k, l, after):
        return _gather2_wait(gathers[(k, l)], after, f"gather_wait_{k}_l{l}").reshape(views[k])

    carried = {("inproj_bwd_dx", 1): [("w_ffn_out", 1), ("w_ffn_in", 1), ("w_out", 1)], ("dw_in", 0): [("w_in", 1)],
               ("inproj_bwd_dx", 0): [("w_ffn_out", 0), ("w_ffn_in", 0), ("w_out", 0)]}
    adam = {}

    def adam_jobs(shards, after):
        for k, l in shards:
            got = _exchange_wait(exchanges[(k, l)], after, f"exchange_wait_{k}_l{l}")
            adam[k] = _AdamJob(got, w[k], mom[k], var[k], l, adam[k].results if k in adam else None)
        return [adam[k] for k, _ in shards]

    def hook(stage, l, payload):
        if stage in ("dw_in", "inproj_bwd_dx"):
            return adam_jobs(carried.get((stage, l), []), payload)
        if stage == "pre_inproj":
            tok = start_gather(_BIG[1:], l, payload)
            return start_gather(_BIG[:1], l + 1, tok) if l == 0 else tok
        if stage == "pre_gmlp":
            tok = relay_gather(_BIG[1:], l, payload)
            return relay_gather(_BIG[:1], l + 1, tok) if l == 0 else tok
        if stage == "small_grads":
            (small_ex[l],), tok = _exchange_start([_pack_small(payload)], f"exchange_start_small_l{l}")
            return tok
        if stage == "mid_backward":
            return reduce_small(l + 1, payload) if l == 0 else None
        if (stage, l) == ("mixer_partials", 0):
            pairs, tok = _pairs_start(payload["w_in"], "pairs_start_w_in_l0")
            p, pair = _pairs_wait(pairs, reduce_small(0, tok), "pairs_wait_w_in_l0")
            sums = _sum_pairs(p, pair, me1, "sum_pairs_w_in_l0")
            exchanges[("w_in", 0)], tok = _chips_start(sums, "chips_start_w_in_l0")
            return tok
        started, tok = _exchange_start(list(payload.values()), f"exchange_start_{'_'.join(payload)}_l{l}")
        exchanges.update({(k, l): h for k, h in zip(payload, started)})
        return tok

    def reduce_small(l, after):
        got = _exchange_wait(small_ex[l], after, f"exchange_wait_small_l{l}")
        mine = _sum8_into_slot(got, me1, f"sum_small_l{l}")
        (small_ag[l],), tok = _gather_start([mine], got, f"gather_start_small_l{l}")
        return tok

    land = lax.dynamic_update_slice(jnp.zeros((N_DEV,) + packed.shape, F32), packed[None], (me, 0, 0))
    (lanes_handle,), token = _gather_start([land], packed, "gather_start_lanes")
    token = start_gather(_BIG[:1], 0, token)
    later = [(k, l) for l in range(2) for k in _BIG if (k, l) != ("w_in", 0)]
    casts.update(zip(later, _cast_all_into_slots([w[k] for k, _ in later], [l for _, l in later], me1, token,
                                                 "cast_later_weights")))
    token = relay_gather(_BIG[:1], 0, casts[later[0]])
    lanes = _gather_wait(lanes_handle, token, "gather_wait_lanes")
    params = {k: w[k] for k in _REPL}
    off = 0
    for k, shp in zip(_LANE_SHARDED, lane_shapes):
        n = shp[0] * shp[1]
        params[k] = jnp.swapaxes(lanes[:, off:off + n], 0, 1).reshape(shp[0], shp[1], D)
        off += n
    _, dx, dg1 = _local_step(x[0], loss_target[0], params, get_w, hook)

    out = {k: job.results for k, job in adam.items()}
    after = dx
    g_small = [_gather_wait(small_ag[l], after, f"gather_wait_small_l{l}").reshape(SMALL_ROWS, HD) for l in (0, 1)]
    row0 = 0
    for k, rows in SMALL_MATRICES:
        res = _adam_matrix(*g_small, *[src[k].reshape(2, rows, HD) for src in (w, mom, var)], row0, f"adam_{k}")
        out[k] = [a.reshape(w[k].shape) for a in res]
        after = res[3]
        row0 += rows
    got = _chips_wait(exchanges[("w_in", 0)], after, "chips_wait_w_in_l0")
    out["w_in"] = _adam_shard(got, w["w_in"], mom["w_in"], var["w_in"], 0, out["w_in"], "adam_w_in_l0")
    out["w_ffn_in"] = [jnp.swapaxes(a, 1, 2) for a in out["w_ffn_in"]]
    as_rows = lambda a: a.reshape(1, D) if a.ndim == 1 else a
    vec = _adam_vectors(*g_small, _all_gather(dg1, out["w_in"][3], "gather_norm1_grad"), me1,
                        *[{k: as_rows(src[k]) for k, _ in SMALL_VECTORS} for src in (w, mom, var)])
    out.update({k: [a.reshape(w[k].shape) for a in res] for k, res in vec.items()})

    return (g_small[1][LOSS_ROW, 0], dx[None], *[out[k][0] for k in _ORDER], *[out[k][1] for k in _ORDER],
            *[out[k][2] for k in _ORDER], *[out[k][3] for k in _ORDER])
```

```python
import jax
import jax.numpy as jnp
from jax import lax
from jax.experimental import pallas as pl
from jax.experimental.pallas import tpu as pltpu

F32 = jnp.float32
BF16 = jnp.bfloat16
SDS = jax.ShapeDtypeStruct

D = 1024
N_IN = 6 * D
D_FF = 2816
N_DEV = 8
IN_BLK = N_IN // N_DEV
FF_BLK = 2 * D_FF // N_DEV
HEADS = 8
HD = 128
EPS = 1e-6
LRU_C = 8.0

ADAM_LR = 0.001
ADAM_B1 = 0.9
ADAM_B2 = 0.999
ADAM_EPS = 1e-08
ADAM_WD = 0.01
ADAM_STEP = 10

VMEM_LIMIT = 60 * 2**20


def _cp(*sem, **kw):
    return pltpu.CompilerParams(dimension_semantics=sem, vmem_limit_bytes=VMEM_LIMIT, **kw)


def _row_tile(s):
    return 512 if s >= 1024 else s // 2


_GELU_C = 0.7978845608028654


def _gelu(x):
    t = jnp.tanh(_GELU_C * (x + 0.044715 * (x * x * x)))
    return 0.5 * x * (1.0 + t), t


def _gelu_grad(x, t):
    return 0.5 * (1.0 + t) + 0.5 * x * (1.0 - t * t) * (_GELU_C * (1.0 + 0.134145 * (x * x)))


def _sigmoid(x):
    return 0.5 + 0.5 * jnp.tanh(0.5 * x)


def _softplus(x):
    e = jnp.exp(-jnp.abs(x))
    w = 1.0 + e
    l1p = jnp.where(w == 1.0, e, jnp.log(w) * e / jnp.where(w == 1.0, 1.0, w - 1.0))
    return jnp.maximum(x, 0.0) + l1p


def _rms_fwd(x, g):
    r = lax.rsqrt(jnp.mean(x * x, axis=-1, keepdims=True) + EPS)
    return x * r * g


def _rms_bwd(x, g, dh):
    r = lax.rsqrt(jnp.mean(x * x, axis=-1, keepdims=True) + EPS)
    xh = x * r
    dxh = dh * g
    dx = r * (dxh - xh * jnp.mean(dxh * xh, axis=-1, keepdims=True))
    dg = jnp.sum(dh * xh, axis=0, keepdims=True)
    return dx, dg


LANE_ROWS = D // HD


def _add_rows128(ref, vec, row0=0):
    for i in range(vec.shape[0]):
        for k in range(LANE_ROWS):
            j = row0 + i * LANE_ROWS + k
            ref[j:j + 1, :] += vec[i:i + 1, k * HD:(k + 1) * HD]


def _dot(a, b):
    return jnp.dot(a, b, preferred_element_type=F32)


def _dot_nt(a, b):
    return lax.dot_general(a, b, (((1,), (1,)), ((), ())), preferred_element_type=F32)


def _dot_tn(a, b):
    return lax.dot_general(a, b, (((0,), (0,)), ((), ())), preferred_element_type=F32)


def _taps(prev, cur, nxt, tm):
    hr = prev.shape[0]
    ext = jnp.concatenate([prev, cur, nxt], axis=0)
    n = tm + 2 * hr
    sl = slice(hr, hr + tm)
    return (pltpu.roll(ext, 2, 0)[sl], pltpu.roll(ext, 1, 0)[sl], cur,
            pltpu.roll(ext, n - 1, 0)[sl], pltpu.roll(ext, n - 2, 0)[sl])


def _halo_specs(tm, s, col, rows=8):
    nb = s // rows
    r = tm // rows
    return (pl.BlockSpec((rows, D), lambda i: (jnp.maximum(i * r - 1, 0), col)),
            pl.BlockSpec((tm, D), lambda i: (i, col)),
            pl.BlockSpec((rows, D), lambda i: (jnp.minimum((i + 1) * r, nb - 1), col)))


def _halo_load(prev_ref, cur_ref, next_ref, fp, fn):
    return prev_ref[...].astype(F32) * fp, cur_ref[...].astype(F32), next_ref[...].astype(F32) * fn


def _halo_flags(nt):
    i = pl.program_id(0)
    return (i > 0).astype(F32), (i < nt - 1).astype(F32)


def _full(shape):
    nd = len(shape)
    return pl.BlockSpec(shape, lambda *_: (0,) * nd)


def _resident(shape):
    nd = len(shape)
    return pl.BlockSpec(shape, lambda *_: (0,) * nd, pipeline_mode=pl.Buffered(1))


def _behind(tokens, body, n_in):
    deps = [t for t in tokens if t is not None]

    def ordered(*refs):
        body(*refs[:n_in], *refs[n_in + len(deps):])

    return ordered, [_ANY] * len(deps), deps


def _norm_inproj(x, g, w, layer, tm, after=()):
    s = x.shape[0]

    def body(x_ref, g_ref, w_ref, z_ref, ht_ref):
        h32 = _rms_fwd(x_ref[...], g_ref[...])
        ht_ref[...] = h32.T.astype(BF16)
        h = h32.astype(BF16)
        for j in range(N_DEV):
            z_ref[:, j * IN_BLK:(j + 1) * IN_BLK] = _dot(h, w_ref[j]).astype(BF16)

    body, dep_specs, deps = _behind(after, body, 3)
    return pl.pallas_call(
        body, name=f"norm_inproj_l{layer}", grid=(s // tm,),
        in_specs=[pl.BlockSpec((tm, D), lambda i: (i, 0)), _full((1, D)), _resident((N_DEV, D, IN_BLK))] + dep_specs,
        out_specs=[pl.BlockSpec((tm, N_IN), lambda i: (i, 0)), pl.BlockSpec((D, tm), lambda i: (0, i))],
        out_shape=[SDS((s, N_IN), BF16), SDS((D, s), BF16)],
        compiler_params=_cp("parallel"))(x, g, w, *deps)


def _gmlp_values(zu_ref, zv_ref, lng_ref, lnb_ref):
    zu = zu_ref[...].astype(F32)
    zv = zv_ref[...].astype(F32)
    u, tu = _gelu(zu)
    gv, tv = _gelu(zv)
    xc = gv - jnp.mean(gv, axis=-1, keepdims=True)
    rstd = lax.rsqrt(jnp.mean(xc * xc, axis=-1, keepdims=True) + EPS)
    xh = xc * rstd
    vb = (xh * lng_ref[...] + lnb_ref[...]).astype(BF16)
    return zu, zv, u, tu, tv, xh, rstd, vb


def _mixer_fwd(x, h0, h1, z, lng, lnb, ws, bsb, wo, layer, tm):
    s = x.shape[0]

    def body(x_ref, h0_ref, h1_ref, zu_ref, zv_ref, zg_ref, za_ref, zb_ref, lng_ref, lnb_ref, ws_ref, bsb_ref,
             wo_ref, x1_ref, mg_ref, ya_s):
        _, _, u, _, _, _, _, vb = _gmlp_values(zu_ref, zv_ref, lng_ref, lnb_ref)
        for c in range(tm // HD):
            rs = slice(c * HD, (c + 1) * HD)
            for g in range(HEADS):
                cs = slice(g * HD, (g + 1) * HD)
                ya_s[rs, cs] = u[rs, cs] * (_dot(ws_ref[g], vb[rs, cs]) + bsb_ref[g])
        gg, _ = _gelu(zg_ref[...].astype(F32))
        yb = (h0_ref[...] + h1_ref[...]) * gg
        m32 = _sigmoid(za_ref[...].astype(F32)) * ya_s[...] + _sigmoid(zb_ref[...].astype(F32)) * yb
        mg_ref[...] = m32.T.astype(BF16)
        x1_ref[...] = x_ref[...] + _dot(m32.astype(BF16), wo_ref[...])

    tile = pl.BlockSpec((tm, D), lambda i: (i, 0))
    wspec = _full((HEADS, HD, HD))
    return pl.pallas_call(
        body, name=f"mixer_fwd_l{layer}", grid=(s // tm,),
        in_specs=[tile, tile, tile] + [pl.BlockSpec((tm, D), lambda i, c=c: (i, c)) for c in (0, 1, 3, 4, 5)]
        + [_full((1, D)), _full((1, D)), wspec, wspec, _full((D, D))],
        out_specs=[tile, pl.BlockSpec((D, tm), lambda i: (0, i))], out_shape=[SDS((s, D), F32), SDS((D, s), BF16)],
        scratch_shapes=[pltpu.VMEM((tm, D), F32)],
        compiler_params=_cp("parallel"))(x, h0, h1, z, z, z, z, z, lng, lnb, ws, bsb, wo)


def _conv(taps, cw_ref, cb_ref):
    _, m1, c0, p1, p2 = taps
    return cb_ref[...] + m1 * cw_ref[0:1, :] + c0 * cw_ref[1:2, :] + p1 * cw_ref[2:3, :] + p2 * cw_ref[3:4, :]


def _heads_dot(xb, w_ref, d):
    return jnp.concatenate([_dot(xb[:, h * HD:(h + 1) * HD], w_ref[d, h]) for h in range(HEADS)], axis=1)


def _lru_decay(r, sp):
    la = (-LRU_C) * r * sp
    a = jnp.exp(la)
    return a, jnp.tanh(-la) * (a * a + 1.0)


def _lru_gates_fwd(z, cw, cb, wr, wi, br, bi, lam, layer, tm):
    s = z.shape[0]
    nt = s // tm

    def body(zp_ref, zc_ref, zn_ref, cw_ref, cb_ref, wr_ref, wi_ref, br_ref, bi_ref, lam_ref,
             a0_ref, b0_ref, a1_ref, b1_ref, xc_ref, r0_ref, i0_ref, r1_ref, i1_ref):
        fp, fn = _halo_flags(nt)
        xc = _conv(_taps(*_halo_load(zp_ref, zc_ref, zn_ref, fp, fn), tm), cw_ref, cb_ref)
        xb = xc.astype(BF16)
        xc_ref[...] = xb
        for d, (a_ref, b_ref, r_ref, i_ref) in enumerate(((a0_ref, b0_ref, r0_ref, i0_ref),
                                                          (a1_ref, b1_ref, r1_ref, i1_ref))):
            r = _sigmoid(_heads_dot(xb, wr_ref, d) + br_ref[d:d + 1, :])
            ig = _sigmoid(_heads_dot(xb, wi_ref, d) + bi_ref[d:d + 1, :])
            a, q = _lru_decay(r, _softplus(-lam_ref[d:d + 1, :]))
            a_ref[...] = a
            b_ref[...] = jnp.sqrt(q) * (ig * xc)
            r_ref[...] = r.astype(BF16)
            i_ref[...] = ig.astype(BF16)

    tile = pl.BlockSpec((tm, D), lambda i: (i, 0))
    return pl.pallas_call(
        body, name=f"lru_gates_fwd_l{layer}", grid=(nt,),
        in_specs=[*_halo_specs(tm, s, 2, 16), _full((4, D)), _full((1, D)),
                  _full((2, HEADS, HD, HD)), _full((2, HEADS, HD, HD)), _full((2, D)), _full((2, D)), _full((2, D))],
        out_specs=[tile] * 9, out_shape=[SDS((s, D), F32)] * 4 + [SDS((s, D), BF16)] * 5,
        compiler_params=_cp("parallel"))(z, z, z, cw, cb, wr, wi, br, bi, lam)


def _scan_group(a, x, c, reverse, bwd):
    row = lax.broadcasted_iota(jnp.int32, a.shape, 0)
    b = a * x if bwd else x
    for d in (1, 2, 4):
        keep = (row < 8 - d) if reverse else (row >= d)
        sh = 8 - d if reverse else d
        a_s = jnp.where(keep, pltpu.roll(a, sh, 0), 1.0)
        b_s = jnp.where(keep, pltpu.roll(b, sh, 0), 0.0)
        b = a * b_s + b
        a = a * a_s
    h = b + a * c
    new_c = h[0:1, :] if reverse else h[7:8, :]
    if not bwd:
        return h, new_c
    if reverse:
        prev = jnp.where(row < 7, pltpu.roll(h, 7, 0), c)
    else:
        prev = jnp.where(row >= 1, pltpu.roll(h, 1, 0), c)
    return x + prev, new_c


def _lru_scan(a_f, x_f, a_r, x_r, bwd, layer):
    s = a_f.shape[0]
    ts = min(1024, s // 2)
    cb = 512
    nt = s // ts
    ng = ts // 8

    def body(af_ref, xf_ref, ar_ref, xr_ref, of_ref, or_ref, cf, cr):
        @pl.when(pl.program_id(1) == 0)
        def _():
            cf[...] = jnp.zeros_like(cf)
            cr[...] = jnp.zeros_like(cr)

        def step(j, carry):
            c_f, c_r = carry
            rf = pl.multiple_of(j * 8, 8)
            rr = pl.multiple_of((ng - 1 - j) * 8, 8)
            o, c_f = _scan_group(af_ref[pl.ds(rf, 8), :], xf_ref[pl.ds(rf, 8), :], c_f, False, bwd)
            of_ref[pl.ds(rf, 8), :] = o
            o, c_r = _scan_group(ar_ref[pl.ds(rr, 8), :], xr_ref[pl.ds(rr, 8), :], c_r, True, bwd)
            or_ref[pl.ds(rr, 8), :] = o
            return c_f, c_r

        c_f, c_r = lax.fori_loop(0, ng, step, (cf[0:1, :], cr[0:1, :]), unroll=2)
        cf[...] = jnp.broadcast_to(c_f, cf.shape)
        cr[...] = jnp.broadcast_to(c_r, cr.shape)

    fwd = pl.BlockSpec((ts, cb), lambda c, t: (t, c))
    rev = pl.BlockSpec((ts, cb), lambda c, t: (nt - 1 - t, c))
    return pl.pallas_call(
        body, name=f"lru_scan_{'bwd' if bwd else 'fwd'}_l{layer}", grid=(D // cb, nt),
        in_specs=[fwd, fwd, rev, rev], out_specs=[fwd, rev],
        out_shape=[SDS((s, D), F32)] * 2,
        scratch_shapes=[pltpu.VMEM((8, cb), F32), pltpu.VMEM((8, cb), F32)],
        compiler_params=_cp("parallel", "arbitrary"))(a_f, x_f, a_r, x_r)


def _ffn_fwd(x1, g, wfi, wfo, layer, tm, head=None):
    s = x1.shape[0]

    def ffn(x_ref, g_ref, wi_ref, wo_ref, ff_ref, dff_ref, h_ref):
        x = x_ref[...]
        h = _rms_fwd(x, g_ref[...]).astype(BF16)
        h_ref[...] = h
        acc = x
        for k in range(4):
            gate = _dot_nt(h, wi_ref[0, k])
            up = _dot_nt(h, wi_ref[1, k])
            sg = _sigmoid(gate)
            silu = gate * sg
            ff = (silu * up).astype(BF16)
            ff_ref[k] = ff
            dff_ref[0, k] = (up * (sg * (1.0 + gate * (1.0 - sg)))).astype(BF16)
            dff_ref[1, k] = silu.astype(BF16)
            acc = acc + _dot(ff, wo_ref[k])
        return acc

    def body(x_ref, g_ref, wi_ref, wo_ref, x2_ref, ff_ref, dff_ref, h_ref):
        x2_ref[...] = ffn(x_ref, g_ref, wi_ref, wo_ref, ff_ref, dff_ref, h_ref)

    def body_with_head(x_ref, g_ref, wi_ref, wo_ref, fg_ref, t_ref, dx_ref, loss_ref, dfg_ref, ff_ref, dff_ref, h_ref):
        @pl.when(pl.program_id(0) == 0)
        def _():
            loss_ref[...] = jnp.zeros_like(loss_ref)
            dfg_ref[...] = jnp.zeros_like(dfg_ref)

        x2 = ffn(x_ref, g_ref, wi_ref, wo_ref, ff_ref, dff_ref, h_ref)
        fg = fg_ref[...]
        e = _rms_fwd(x2, fg) - t_ref[...]
        rows = jnp.sum(e * e, axis=-1, keepdims=True)
        loss_ref[...] += (0.5 / D) * jnp.sum(rows, axis=0, keepdims=True)
        dx, dg = _rms_bwd(x2, fg, e * (1.0 / D))
        dx_ref[...] = dx
        _add_rows128(dfg_ref, dg)

    tile = pl.BlockSpec((tm, D), lambda i: (i, 0))
    weights = [_resident((2, 4, FF_BLK, D)), _resident((4, FF_BLK, D))]
    kept_specs = [pl.BlockSpec((4, tm, FF_BLK), lambda i: (0, i, 0)),
                  pl.BlockSpec((2, 4, tm, FF_BLK), lambda i: (0, 0, i, 0)), tile]
    kept_shapes = [SDS((4, s, FF_BLK), BF16), SDS((2, 4, s, FF_BLK), BF16), SDS((s, D), BF16)]
    if head is None:
        return pl.pallas_call(
            body, name=f"ffn_fwd_l{layer}", grid=(s // tm,),
            in_specs=[tile, _full((1, D))] + weights, out_specs=[tile] + kept_specs,
            out_shape=[SDS((s, D), F32)] + kept_shapes, compiler_params=_cp("parallel"))(x1, g, wfi, wfo)
    final_g, tgt = head
    return pl.pallas_call(
        body_with_head, name=f"ffn_fwd_loss_l{layer}", grid=(s // tm,),
        in_specs=[tile, _full((1, D))] + weights + [_full((1, D)), tile],
        out_specs=[tile, _full((1, 1)), _full((LANE_ROWS, HD))] + kept_specs,
        out_shape=[SDS((s, D), F32), SDS((1, 1), F32), SDS((LANE_ROWS, HD), F32)] + kept_shapes,
        compiler_params=_cp("arbitrary"))(x1, g, wfi, wfo, final_g, tgt)


def _ffn_bwd(dx2, wfo, factors, wfi, x1, g, layer, tm):
    s = dx2.shape[0]

    def body(dx_ref, wo_ref, f_ref, wi_ref, x_ref, g_ref, dgu_ref, dx1_ref, dg_ref):
        @pl.when(pl.program_id(0) == 0)
        def _():
            dg_ref[...] = jnp.zeros_like(dg_ref)

        dx = dx_ref[...]
        dxb = dx.astype(BF16)
        dh = None
        for k in range(4):
            dff = _dot_nt(dxb, wo_ref[k])
            d_gate = (dff * f_ref[0, k].astype(F32)).astype(BF16)
            d_up = (dff * f_ref[1, k].astype(F32)).astype(BF16)
            dgu_ref[0, k] = d_gate
            dgu_ref[1, k] = d_up
            part = _dot(d_gate, wi_ref[k]) + _dot(d_up, wi_ref[4 + k])
            dh = part if dh is None else dh + part
        dxn, dg = _rms_bwd(x_ref[...], g_ref[...], dh)
        dx1_ref[...] = dx + dxn
        _add_rows128(dg_ref, dg)

    tile = pl.BlockSpec((tm, D), lambda i: (i, 0))
    blk = pl.BlockSpec((2, 4, tm, FF_BLK), lambda i: (0, 0, i, 0))
    return pl.pallas_call(
        body, name=f"ffn_bwd_l{layer}", grid=(s // tm,),
        in_specs=[tile, _resident((4, FF_BLK, D)), blk, _resident((N_DEV, FF_BLK, D)), tile, _full((1, D))],
        out_specs=[blk, tile, _full((LANE_ROWS, HD))],
        out_shape=[SDS((2, 4, s, FF_BLK), BF16), SDS((s, D), F32), SDS((LANE_ROWS, HD), F32)],
        compiler_params=_cp("arbitrary"))(dx2, wfo, factors, wfi, x1, g)


def _mm_nt_rms_bwd(a, a_spec, a_blocks, w, w_is_transposed, x, g, dres, name, tm, after=(), jobs=()):
    s = x.shape[0]

    def body(a_ref, w_ref, x_ref, g_ref, dres_ref, dx_ref, dg_ref):
        @pl.when(pl.program_id(0) == 0)
        def _():
            dg_ref[...] = jnp.zeros_like(dg_ref)

        dh = None
        for k, blk in enumerate(a_blocks(a_ref)):
            part = _dot(blk, w_ref[k]) if w_is_transposed else _dot_nt(blk, w_ref[k])
            dh = part if dh is None else dh + part
        dx, dg = _rms_bwd(x_ref[...], g_ref[...], dh)
        dx_ref[...] = dres_ref[...] + dx
        _add_rows128(dg_ref, dg)

    tile = pl.BlockSpec((tm, D), lambda i: (i, 0))
    body, dep_specs, deps = _behind(after, body, 5)
    body, job_in, job_args, job_out, job_shapes, aliases = _carry(jobs, s // tm, body, 5 + len(deps), 2)
    res = pl.pallas_call(
        body, name=name, grid=(s // tm,),
        in_specs=[a_spec, _resident(w.shape), tile, _full((1, D)), tile] + dep_specs + job_in,
        out_specs=[tile, _full((LANE_ROWS, HD))] + job_out,
        out_shape=[SDS((s, D), F32), SDS((LANE_ROWS, HD), F32)] + job_shapes,
        input_output_aliases=aliases, compiler_params=_cp("arbitrary"))(a, w, x, g, dres, *deps, *job_args)
    for j, job in enumerate(jobs):
        job.results = res[2 + 4 * j:6 + 4 * j]
    return res[0], res[1]


def _mm_tn(a, a_spec, b, b_spec, nb, out_shape, out_spec, name, a_is_transposed=True, after=None, jobs=()):
    def body(a_ref, b_ref, *rest):
        o_ref = rest[-1]
        bb = b_ref[...].astype(BF16)
        o_ref[...] = (_dot(a_ref[...], bb) if a_is_transposed else _dot_tn(a_ref[...], bb)).astype(BF16)

    deps = [] if after is None else [after]
    body, job_in, job_args, job_out, job_shapes, aliases = _carry(jobs, nb, body, 2 + len(deps), 1)
    res = pl.pallas_call(
        body, name=name, grid=(nb,), in_specs=[a_spec, b_spec] + [_ANY] * len(deps) + job_in,
        out_specs=[out_spec] + job_out, out_shape=[SDS(out_shape, BF16)] + job_shapes,
        input_output_aliases=aliases, compiler_params=_cp("parallel"))(a, b, *deps, *job_args)
    for j, job in enumerate(jobs):
        job.results = res[1 + 4 * j:5 + 4 * j]
    return res[0]


def _mixer_bwd(dx1, wo, h0, h1, z, lng, lnb, ws, wst, bsb, layer, tm, after=()):
    s = dx1.shape[0]
    nt = s // tm

    def body(dx_ref, wo_ref, h0_ref, h1_ref, zu_ref, zv_ref, zg_ref, za_ref, zb_ref, lng_ref, lnb_ref,
             ws_ref, wst_ref, bsb_ref, dz_ref, dh_ref, dws_ref, dbs_ref, dlng_ref, dlnb_ref,
             du_s, dv_s, ya_s, dbs_acc):
        i = pl.program_id(0)

        @pl.when(i == 0)
        def _():
            for r in (dws_ref, dlng_ref, dlnb_ref, dbs_acc):
                r[...] = jnp.zeros_like(r)

        dm = _dot_nt(dx_ref[...].astype(BF16), wo_ref[...])
        sa = _sigmoid(za_ref[...].astype(F32))
        sb = _sigmoid(zb_ref[...].astype(F32))
        zg = zg_ref[...].astype(F32)
        gg, tg = _gelu(zg)
        hs = h0_ref[...] + h1_ref[...]
        dyb = dm * sb
        dya = dm * sa
        dh_ref[...] = dyb * gg
        dz_ref[:, 2 * D:3 * D] = jnp.zeros((tm, D), BF16)
        dz_ref[:, 3 * D:4 * D] = (dyb * hs * _gelu_grad(zg, tg)).astype(BF16)
        dz_ref[:, 5 * D:6 * D] = (dm * (hs * gg) * (sb * (1.0 - sb))).astype(BF16)

        zu, zv, u, tu, tv, xh, rstd, vb = _gmlp_values(zu_ref, zv_ref, lng_ref, lnb_ref)
        for c in range(tm // HD):
            rs = slice(c * HD, (c + 1) * HD)
            for g in range(HEADS):
                cs = slice(g * HD, (g + 1) * HD)
                vblk = vb[rs, cs]
                mixed = _dot(ws_ref[g], vblk) + bsb_ref[g]
                ya_s[rs, cs] = u[rs, cs] * mixed
                du_s[rs, cs] = dya[rs, cs] * mixed
                dmx = dya[rs, cs] * u[rs, cs]
                dbs_acc[g] += dmx
                dmxb = dmx.astype(BF16)
                dws_ref[g] += _dot_nt(dmxb, vblk)
                dv_s[rs, cs] = _dot(wst_ref[g], dmxb)
        dz_ref[:, 4 * D:5 * D] = (dm * ya_s[...] * (sa * (1.0 - sa))).astype(BF16)
        dv = dv_s[...]
        _add_rows128(dlng_ref, jnp.sum(dv * xh, axis=0, keepdims=True))
        _add_rows128(dlnb_ref, jnp.sum(dv, axis=0, keepdims=True))
        dxh = dv * lng_ref[...]
        dgv = rstd * (dxh - jnp.mean(dxh, axis=-1, keepdims=True)
                      - xh * jnp.mean(dxh * xh, axis=-1, keepdims=True))
        dz_ref[:, 0:D] = (du_s[...] * _gelu_grad(zu, tu)).astype(BF16)
        dz_ref[:, D:2 * D] = (dgv * _gelu_grad(zv, tv)).astype(BF16)

        @pl.when(i == nt - 1)
        def _():
            for g in range(HEADS):
                dbs_ref[g:g + 1, :] = jnp.sum(dbs_acc[g].T, axis=0, keepdims=True)

    tile = pl.BlockSpec((tm, D), lambda i: (i, 0))
    wspec = _full((HEADS, HD, HD))
    body, dep_specs, deps = _behind(after, body, 14)
    return pl.pallas_call(
        body, name=f"mixer_bwd_l{layer}", grid=(nt,),
        in_specs=[tile, _full((D, D)), tile, tile]
        + [pl.BlockSpec((tm, D), lambda i, c=c: (i, c)) for c in (0, 1, 3, 4, 5)]
        + [_full((1, D)), _full((1, D)), wspec, wspec, wspec] + dep_specs,
        out_specs=[pl.BlockSpec((tm, N_IN), lambda i: (i, 0)), tile, wspec, _full((HEADS, HD)),
                   _full((LANE_ROWS, HD)), _full((LANE_ROWS, HD))],
        out_shape=[SDS((s, N_IN), BF16), SDS((s, D), F32), SDS((HEADS, HD, HD), F32), SDS((HEADS, HD), F32),
                   SDS((LANE_ROWS, HD), F32), SDS((LANE_ROWS, HD), F32)],
        scratch_shapes=[pltpu.VMEM((tm, D), F32)] * 3 + [pltpu.VMEM((HEADS, HD, HD), F32)],
        compiler_params=_cp("arbitrary"))(dx1, wo, h0, h1, z, z, z, z, z, lng, lnb, ws, wst, bsb, *deps)


def _lru_gates_bwd(xcb, gates, h0, h1, g0, g1, wr, wi, lam, layer, tm, after=()):
    s = xcb.shape[0]
    nt = s // tm

    def body(xc_ref, r0_ref, i0_ref, r1_ref, i1_ref, h0p_ref, h0_ref, h1_ref, h1n_ref, g0_ref, g1_ref,
             wr_ref, wi_ref, lam_ref, dxc_ref, dwr_ref, dwi_ref, dbr_ref, dbi_ref, dlam_ref):
        i = pl.program_id(0)
        fp, fn = _halo_flags(nt)

        @pl.when(i == 0)
        def _():
            for r in (dwr_ref, dwi_ref, dbr_ref, dbi_ref, dlam_ref):
                r[...] = jnp.zeros_like(r)

        xb = xc_ref[...]
        xc = xb.astype(F32)
        zeros8 = jnp.zeros((8, D), F32)
        h_prev = _taps(h0p_ref[...] * fp, h0_ref[...], zeros8, tm)[1]
        h_next = _taps(zeros8, h1_ref[...], h1n_ref[...] * fn, tm)[3]
        dxc = jnp.zeros((tm, D), F32)
        for d, (g_ref, hsh, r_ref, i_ref) in enumerate(((g0_ref, h_prev, r0_ref, i0_ref),
                                                        (g1_ref, h_next, r1_ref, i1_ref))):
            sp = _softplus(-lam_ref[d:d + 1, :])
            r = r_ref[...].astype(F32)
            ig = i_ref[...].astype(F32)
            a, q = _lru_decay(r, sp)
            rmult = jnp.where(q > 0.0, lax.rsqrt(jnp.where(q > 0.0, q, 1.0)), 0.0)
            mult = q * rmult
            db = g_ref[...]
            da = db * hsh
            dmult = db * (ig * xc)
            di = db * (mult * xc)
            dxc = dxc + db * (mult * ig)
            dla = da * a - dmult * (a * a * rmult)
            dsp_dlam = -_sigmoid(-lam_ref[d:d + 1, :])
            _add_rows128(dlam_ref, jnp.sum(dla * r, axis=0, keepdims=True) * ((-LRU_C) * dsp_dlam), d * LANE_ROWS)
            dpr = dla * sp * (-LRU_C) * (r * (1.0 - r))
            dpi = di * (ig * (1.0 - ig))
            _add_rows128(dbr_ref, jnp.sum(dpr, axis=0, keepdims=True), d * LANE_ROWS)
            _add_rows128(dbi_ref, jnp.sum(dpi, axis=0, keepdims=True), d * LANE_ROWS)
            dprb = dpr.astype(BF16)
            dpib = dpi.astype(BF16)
            parts = []
            for h in range(HEADS):
                cs = slice(h * HD, (h + 1) * HD)
                dwr_ref[d, h] += _dot_tn(xb[:, cs], dprb[:, cs])
                dwi_ref[d, h] += _dot_tn(xb[:, cs], dpib[:, cs])
                parts.append(_dot_nt(dprb[:, cs], wr_ref[d, h]) + _dot_nt(dpib[:, cs], wi_ref[d, h]))
            dxc = dxc + jnp.concatenate(parts, axis=1)
        dxc_ref[...] = dxc.astype(BF16)

    tile = pl.BlockSpec((tm, D), lambda i: (i, 0))
    hp, hc, hn = _halo_specs(tm, s, 0)
    wspec = _full((2, HEADS, HD, HD))
    vspec = _full((2 * LANE_ROWS, HD))
    body, dep_specs, deps = _behind(after, body, 14)
    return pl.pallas_call(
        body, name=f"lru_gates_bwd_l{layer}", grid=(nt,),
        in_specs=[tile] * 5 + [hp, hc, hc, hn, tile, tile, wspec, wspec, _full((2, D))] + dep_specs,
        out_specs=[tile, wspec, wspec, vspec, vspec, vspec],
        out_shape=[SDS((s, D), BF16), SDS((2, HEADS, HD, HD), F32), SDS((2, HEADS, HD, HD), F32)]
        + [SDS((2 * LANE_ROWS, HD), F32)] * 3,
        compiler_params=_cp("arbitrary"))(xcb, *gates, h0, h0, h1, h1, g0, g1, wr, wi, lam, *deps)


def _conv_bwd(dz, dxc, z, cw, layer, tm):
    s = z.shape[0]
    nt = s // tm

    def body(dz_in, dp_ref, dc_ref, dn_ref, zp_ref, zc_ref, zn_ref, cw_ref, dz_ref, dcw_ref, dcb_ref):
        del dz_in
        fp, fn = _halo_flags(nt)

        @pl.when(pl.program_id(0) == 0)
        def _():
            dcw_ref[...] = jnp.zeros_like(dcw_ref)
            dcb_ref[...] = jnp.zeros_like(dcb_ref)

        dxc_halo = _halo_load(dp_ref, dc_ref, dn_ref, fp, fn)
        dxc = dxc_halo[1]
        dm2, dm1, _, dp1, _ = _taps(*dxc_halo, tm)
        dz_ref[...] = (cw_ref[0:1, :] * dp1 + cw_ref[1:2, :] * dxc + cw_ref[2:3, :] * dm1
                       + cw_ref[3:4, :] * dm2).astype(BF16)
        _, zm1, z0, zp1, zp2 = _taps(*_halo_load(zp_ref, zc_ref, zn_ref, fp, fn), tm)
        for k, zt in enumerate((zm1, z0, zp1, zp2)):
            _add_rows128(dcw_ref, jnp.sum(dxc * zt, axis=0, keepdims=True), k * LANE_ROWS)
        _add_rows128(dcb_ref, jnp.sum(dxc, axis=0, keepdims=True))

    return pl.pallas_call(
        body, name=f"conv_bwd_l{layer}", grid=(nt,),
        in_specs=[pl.BlockSpec(memory_space=pl.ANY), *_halo_specs(tm, s, 0, 16), *_halo_specs(tm, s, 2, 16),
                  _full((4, D))],
        out_specs=[pl.BlockSpec((tm, D), lambda i: (i, 2)), _full((4 * LANE_ROWS, HD)), _full((LANE_ROWS, HD))],
        out_shape=[SDS((s, N_IN), BF16), SDS((4 * LANE_ROWS, HD), F32), SDS((LANE_ROWS, HD), F32)],
        input_output_aliases={0: 0},
        compiler_params=_cp("arbitrary"))(dz, dxc, dxc, dxc, z, z, z, cw)


def _me():
    return lax.axis_index("x"), lax.axis_index("y"), lax.axis_index("c")


def _peer(m):
    x, y, c = _me()
    px = 1 - x if m & 4 else x
    py = 1 - y if m & 2 else y
    pc = 1 - c if m & 1 else c
    return (px, py, pc), 4 * px + 2 * py + pc


_ANY = pl.BlockSpec(memory_space=pl.ANY)
_EXCHANGE_SEMS = [pltpu.SemaphoreType.DMA((N_DEV - 1,)), pltpu.SemaphoreType.DMA((N_DEV - 1,)), pltpu.SemaphoreType.DMA(())]


def _all_gather(v, after, name):
    def body(v_ref, after_ref, o_ref, send_sems, recv_sems, local_sem):
        del after_ref
        x, y, c = _me()
        me = 4 * x + 2 * y + c
        local = pltpu.make_async_copy(v_ref, o_ref.at[me], local_sem)
        local.start()
        sends = []
        for m in range(1, N_DEV):
            dev, _ = _peer(m)
            cp = pltpu.make_async_remote_copy(v_ref, o_ref.at[me], send_sems.at[m - 1], recv_sems.at[m - 1],
                                              device_id=dev, device_id_type=pl.DeviceIdType.MESH)
            cp.start()
            sends.append(cp)
        for m in range(1, N_DEV):
            dev, blk = _peer(m)
            pltpu.make_async_remote_copy(v_ref, o_ref.at[blk], send_sems.at[m - 1], recv_sems.at[m - 1],
                                         device_id=dev, device_id_type=pl.DeviceIdType.MESH).wait_recv()
        for cp in sends:
            cp.wait_send()
        local.wait()

    return pl.pallas_call(
        body, name=name, in_specs=[_ANY, _ANY], out_specs=_ANY,
        out_shape=SDS((N_DEV,) + v.shape, v.dtype), scratch_shapes=_EXCHANGE_SEMS)(v, after)


_HBM = pl.BlockSpec(memory_space=pltpu.HBM)
_SEM = pl.BlockSpec(memory_space=pltpu.SEMAPHORE)
_EFFECT = pltpu.CompilerParams(has_side_effects=pltpu.SideEffectType.DATAFLOW_SIDE_EFFECTING)
_PEER_SEMS = pltpu.SemaphoreType.DMA((N_DEV - 1,))


def _in_hbm(a):
    return pltpu.with_memory_space_constraint(a, pltpu.HBM)


def _remote(src, dst, send_sems, recv_sems, m):
    dev, _ = _peer(m)
    return pltpu.make_async_remote_copy(src, dst, send_sems.at[m - 1], recv_sems.at[m - 1],
                                        device_id=dev, device_id_type=pl.DeviceIdType.MESH)


def _gather_start(lands, after, name):
    n = len(lands)

    def body(*refs):
        land = refs[:n]
        sems = refs[n + 1:3 * n + 1]
        token = refs[-1]
        x, y, c = _me()
        me = 4 * x + 2 * y + c
        for t in range(n):
            for m in range(1, N_DEV):
                _remote(land[t].at[me], land[t].at[me], sems[2 * t], sems[2 * t + 1], m).start()
        token[...] = jnp.zeros_like(token)

    res = pl.pallas_call(
        body, name=name, in_specs=[_HBM] * n + [_ANY],
        out_specs=[_SEM] * (2 * n) + [_HBM] * n + [pl.BlockSpec(memory_space=pltpu.VMEM)],
        out_shape=[_PEER_SEMS] * (2 * n) + [pltpu.HBM(a.shape, a.dtype) for a in lands] + [SDS((8, 128), F32)],
        input_output_aliases={t: 2 * n + t for t in range(n)},
        compiler_params=_EFFECT)(*[_in_hbm(a) for a in lands], after)
    return [(res[2 * t], res[2 * t + 1], res[2 * n + t]) for t in range(n)], res[-1]


def _gather_wait(handle, after, name):
    send_sems, recv_sems, land = handle

    def body(land_ref, ssem, rsem, after_ref, out_ref):
        del after_ref, out_ref
        x, y, c = _me()
        me = 4 * x + 2 * y + c
        for m in range(1, N_DEV):
            _, blk = _peer(m)
            cp = _remote(land_ref.at[me], land_ref.at[blk], ssem, rsem, m)
            cp.wait_send()
            cp.wait_recv()

    return pl.pallas_call(
        body, name=name, in_specs=[_HBM, _SEM, _SEM, _ANY], out_specs=_HBM,
        out_shape=pltpu.HBM(land.shape, land.dtype), input_output_aliases={0: 0},
        compiler_params=_EFFECT)(land, send_sems, recv_sems, after)


FIRST_STAGE = (1, 2, 4, 6)
RELAYED = (2, 4, 6)
OTHER_CORE = 1


def _stage_copy(src, dst, send_sems, recv_sems, k, m):
    dev, _ = _peer(m)
    return pltpu.make_async_remote_copy(src, dst, send_sems.at[k], recv_sems.at[k],
                                        device_id=dev, device_id_type=pl.DeviceIdType.MESH)


def _gather2_start(lands, after, name):
    n = len(lands)

    def body(*refs):
        land = refs[:n]
        sems = refs[n + 1:3 * n + 1]
        token = refs[-1]
        x, y, c = _me()
        me = 4 * x + 2 * y + c
        for t in range(n):
            for k, m in enumerate(FIRST_STAGE):
                _stage_copy(land[t].at[me], land[t].at[me], sems[2 * t], sems[2 * t + 1], k, m).start()
        token[...] = jnp.zeros_like(token)

    stage_sems = pltpu.SemaphoreType.DMA((len(FIRST_STAGE),))
    res = pl.pallas_call(
        body, name=name, in_specs=[_HBM] * n + [_ANY],
        out_specs=[_SEM] * (2 * n) + [_HBM] * n + [pl.BlockSpec(memory_space=pltpu.VMEM)],
        out_shape=[stage_sems] * (2 * n) + [pltpu.HBM(a.shape, a.dtype) for a in lands] + [SDS((8, 128), F32)],
        input_output_aliases={t: 2 * n + t for t in range(n)},
        compiler_params=_EFFECT)(*[_in_hbm(a) for a in lands], after)
    return [(res[2 * t], res[2 * t + 1], res[2 * n + t]) for t in range(n)], res[-1]


def _gather2_relay(handles, after, name):
    n = len(handles)

    def body(*refs):
        land, send1, recv1 = refs[:n], refs[n:2 * n], refs[2 * n:3 * n]
        sems = refs[3 * n + 1:5 * n + 1]
        token = refs[-1]
        x, y, c = _me()
        me = 4 * x + 2 * y + c
        for t in range(n):
            for j, m in enumerate(RELAYED):
                _, blk = _peer(m)
                _stage_copy(land[t].at[me], land[t].at[blk], send1[t], recv1[t], 1 + j, m).wait_recv()
                _stage_copy(land[t].at[blk], land[t].at[blk], sems[2 * t], sems[2 * t + 1], j, OTHER_CORE).start()
        token[...] = jnp.zeros_like(token)

    relay_sems = pltpu.SemaphoreType.DMA((len(RELAYED),))
    lands = [h[2] for h in handles]
    res = pl.pallas_call(
        body, name=name, in_specs=[_HBM] * n + [_SEM] * (2 * n) + [_ANY],
        out_specs=[_SEM] * (2 * n) + [_HBM] * n + [pl.BlockSpec(memory_space=pltpu.VMEM)],
        out_shape=[relay_sems] * (2 * n) + [pltpu.HBM(a.shape, a.dtype) for a in lands] + [SDS((8, 128), F32)],
        input_output_aliases={t: 2 * n + t for t in range(n)},
        compiler_params=_EFFECT)(*lands, *[h[0] for h in handles], *[h[1] for h in handles], after)
    return [(h[0], h[1], res[2 * t], res[2 * t + 1], res[2 * n + t]) for t, h in enumerate(handles)], res[-1]


def _gather2_wait(handle, after, name):
    send1, recv1, send2, recv2, land = handle

    def body(land_ref, s1, r1, s2, r2, after_ref, out_ref):
        del after_ref, out_ref
        x, y, c = _me()
        me = 4 * x + 2 * y + c
        _, other = _peer(OTHER_CORE)
        _stage_copy(land_ref.at[me], land_ref.at[other], s1, r1, 0, OTHER_CORE).wait_recv()
        for k, m in enumerate(FIRST_STAGE):
            _stage_copy(land_ref.at[me], land_ref.at[me], s1, r1, k, m).wait_send()
        for j, m in enumerate(RELAYED):
            _, mine = _peer(m)
            _, theirs = _peer(m ^ OTHER_CORE)
            _stage_copy(land_ref.at[mine], land_ref.at[mine], s2, r2, j, OTHER_CORE).wait_send()
            _stage_copy(land_ref.at[mine], land_ref.at[theirs], s2, r2, j, OTHER_CORE).wait_recv()

    return pl.pallas_call(
        body, name=name, in_specs=[_HBM] + [_SEM] * 4 + [_ANY], out_specs=_HBM,
        out_shape=pltpu.HBM(land.shape, land.dtype), input_output_aliases={0: 0},
        compiler_params=_EFFECT)(land, send1, recv1, send2, recv2, after)


def _exchange_start(ps, name):
    n = len(ps)

    def body(*refs):
        p = refs[:n]
        got = refs[n:2 * n]
        sems = refs[2 * n:5 * n]
        token = refs[-1]
        x, y, c = _me()
        me = 4 * x + 2 * y + c
        for t in range(n):
            pltpu.make_async_copy(p[t].at[me], got[t].at[me], sems[3 * t + 2]).start()
            for m in range(1, N_DEV):
                _, blk = _peer(m)
                _remote(p[t].at[blk], got[t].at[me], sems[3 * t], sems[3 * t + 1], m).start()
        token[...] = jnp.zeros_like(token)

    res = pl.pallas_call(
        body, name=name, in_specs=[_HBM] * (2 * n),
        out_specs=[_SEM] * (3 * n) + [_HBM] * (2 * n) + [pl.BlockSpec(memory_space=pltpu.VMEM)],
        out_shape=[_PEER_SEMS, _PEER_SEMS, pltpu.SemaphoreType.DMA(())] * n
        + [pltpu.HBM(a.shape, a.dtype) for a in ps] * 2 + [SDS((8, 128), F32)],
        input_output_aliases={t: 3 * n + t for t in range(2 * n)},
        compiler_params=_EFFECT)(*[_in_hbm(a) for a in ps], *[_in_hbm(lax.empty(a.shape, a.dtype)) for a in ps])
    return [(res[3 * t], res[3 * t + 1], res[3 * t + 2], res[3 * n + t], res[4 * n + t]) for t in range(n)], res[-1]


def _exchange_wait(handle, after, name):
    send_sems, recv_sems, local_sem, p, got = handle

    def body(p_ref, got_ref, ssem, rsem, lsem, after_ref, p_out, got_out):
        del after_ref, p_out, got_out
        x, y, c = _me()
        me = 4 * x + 2 * y + c
        pltpu.make_async_copy(p_ref.at[me], got_ref.at[me], lsem).wait()
        for m in range(1, N_DEV):
            _, blk = _peer(m)
            cp = _remote(p_ref.at[blk], got_ref.at[blk], ssem, rsem, m)
            cp.wait_send()
            cp.wait_recv()

    return pl.pallas_call(
        body, name=name, in_specs=[_HBM, _HBM, _SEM, _SEM, _SEM, _ANY], out_specs=[_HBM, _HBM],
        out_shape=[pltpu.HBM(p.shape, p.dtype), pltpu.HBM(got.shape, got.dtype)],
        input_output_aliases={0: 0, 1: 1}, compiler_params=_EFFECT)(p, got, send_sems, recv_sems, local_sem, after)[1]


CHIPS = (0, 2, 4, 6)


def _pairs_start(p, name):
    def body(p_ref, pair_ref, ssem, rsem, p_out, pair_out, token):
        del p_out, pair_out
        for k, chip in enumerate(CHIPS):
            _, blk = _peer(chip ^ OTHER_CORE)
            _stage_copy(p_ref.at[blk], pair_ref.at[k], ssem, rsem, k, OTHER_CORE).start()
        token[...] = jnp.zeros_like(token)

    sems = pltpu.SemaphoreType.DMA((len(CHIPS),))
    pair = lax.empty((len(CHIPS),) + p.shape[1:], p.dtype)
    res = pl.pallas_call(
        body, name=name, in_specs=[_HBM] * 2,
        out_specs=[_SEM] * 2 + [_HBM] * 2 + [pl.BlockSpec(memory_space=pltpu.VMEM)],
        out_shape=[sems, sems, pltpu.HBM(p.shape, p.dtype), pltpu.HBM(pair.shape, pair.dtype), SDS((8, 128), F32)],
        input_output_aliases={0: 2, 1: 3}, compiler_params=_EFFECT)(_in_hbm(p), _in_hbm(pair))
    return res[:4], res[4]


def _pairs_wait(handle, after, name):
    send_sems, recv_sems, p, pair = handle

    def body(p_ref, pair_ref, ssem, rsem, after_ref, p_out, pair_out):
        del after_ref, p_out, pair_out
        for k, chip in enumerate(CHIPS):
            _, blk = _peer(chip ^ OTHER_CORE)
            cp = _stage_copy(p_ref.at[blk], pair_ref.at[k], ssem, rsem, k, OTHER_CORE)
            cp.wait_send()
            cp.wait_recv()

    return pl.pallas_call(
        body, name=name, in_specs=[_HBM, _HBM, _SEM, _SEM, _ANY], out_specs=[_HBM, _HBM],
        out_shape=[pltpu.HBM(p.shape, p.dtype), pltpu.HBM(pair.shape, pair.dtype)],
        input_output_aliases={0: 0, 1: 1}, compiler_params=_EFFECT)(p, pair, send_sems, recv_sems, after)


def _sum_pairs(p, pair, me1, name):
    _, r, c = pair.shape
    tr = _row_tile(r)

    def body(me_ref, p_ref, pair_ref, o_ref):
        del me_ref
        o_ref[...] = (p_ref[...].astype(F32) + pair_ref[...].astype(F32)).astype(o_ref.dtype)

    def mine(k, i, me):
        chip = 2 * k
        return (jnp.bitwise_xor(me[0], chip), i, 0)

    assert CHIPS == tuple(2 * k for k in range(len(CHIPS)))
    blk = pl.BlockSpec((None, tr, c), lambda k, i, me: (k, i, 0))
    return pl.pallas_call(
        body, name=name,
        grid_spec=pltpu.PrefetchScalarGridSpec(
            num_scalar_prefetch=1, grid=(len(CHIPS), r // tr),
            in_specs=[pl.BlockSpec((None, tr, c), mine), blk], out_specs=blk),
        out_shape=SDS(pair.shape, pair.dtype), compiler_params=_cp("parallel", "parallel"))(me1, p, pair)


def _chips_start(q, name):
    def body(q_ref, got_ref, ssem, rsem, lsem, q_out, got_out, token):
        del q_out, got_out
        pltpu.make_async_copy(q_ref.at[0], got_ref.at[0], lsem).start()
        for k, chip in enumerate(CHIPS[1:]):
            _stage_copy(q_ref.at[k + 1], got_ref.at[k + 1], ssem, rsem, k, chip).start()
        token[...] = jnp.zeros_like(token)

    sems = pltpu.SemaphoreType.DMA((len(CHIPS) - 1,))
    res = pl.pallas_call(
        body, name=name, in_specs=[_HBM] * 2,
        out_specs=[_SEM] * 3 + [_HBM] * 2 + [pl.BlockSpec(memory_space=pltpu.VMEM)],
        out_shape=[sems, sems, pltpu.SemaphoreType.DMA(()), pltpu.HBM(q.shape, q.dtype), pltpu.HBM(q.shape, q.dtype),
                   SDS((8, 128), F32)],
        input_output_aliases={0: 3, 1: 4}, compiler_params=_EFFECT)(_in_hbm(q), _in_hbm(lax.empty(q.shape, q.dtype)))
    return res[:5], res[5]


def _chips_wait(handle, after, name):
    send_sems, recv_sems, local_sem, q, got = handle

    def body(q_ref, got_ref, ssem, rsem, lsem, after_ref, q_out, got_out):
        del after_ref, q_out, got_out
        pltpu.make_async_copy(q_ref.at[0], got_ref.at[0], lsem).wait()
        for k, chip in enumerate(CHIPS[1:]):
            cp = _stage_copy(q_ref.at[k + 1], got_ref.at[k + 1], ssem, rsem, k, chip)
            cp.wait_send()
            cp.wait_recv()

    return pl.pallas_call(
        body, name=name, in_specs=[_HBM, _HBM, _SEM, _SEM, _SEM, _ANY], out_specs=[_HBM, _HBM],
        out_shape=[pltpu.HBM(q.shape, q.dtype), pltpu.HBM(got.shape, got.dtype)],
        input_output_aliases={0: 0, 1: 1}, compiler_params=_EFFECT)(q, got, send_sems, recv_sems, local_sem, after)[1]


def _cast_into_slot(w, layer, me1, name):
    _, r, c = w.shape
    tr = next(t for t in (512, 352, r) if r % t == 0)

    def body(me_ref, w_ref, o_ref):
        del me_ref
        o_ref[...] = w_ref[...].astype(BF16)

    return pl.pallas_call(
        body, name=name,
        grid_spec=pltpu.PrefetchScalarGridSpec(
            num_scalar_prefetch=1, grid=(r // tr,),
            in_specs=[pl.BlockSpec((None, tr, c), lambda i, me: (layer, i, 0))],
            out_specs=pl.BlockSpec((None, tr, c), lambda i, me: (me[0], i, 0))),
        out_shape=SDS((N_DEV, r, c), BF16), compiler_params=_cp("arbitrary"))(me1, w)


def _cast_all_into_slots(ws, layers, me1, after, name):
    n = len(ws)

    def body(me_ref, *refs):
        del me_ref
        for w_ref, o_ref in zip(refs[:n], refs[n + 1:]):
            o_ref[...] = w_ref[...].astype(BF16)

    return pl.pallas_call(
        body, name=name,
        grid_spec=pltpu.PrefetchScalarGridSpec(
            num_scalar_prefetch=1, grid=(1,),
            in_specs=[pl.BlockSpec((None,) + a.shape[1:], lambda i, me, l=l: (l, 0, 0)) for a, l in zip(ws, layers)]
            + [_ANY],
            out_specs=[pl.BlockSpec((None,) + a.shape[1:], lambda i, me: (me[0], 0, 0)) for a in ws]),
        out_shape=[SDS((N_DEV,) + a.shape[1:], BF16) for a in ws],
        compiler_params=_cp("arbitrary"))(me1, *ws, after)


def _sum8_into_slot(p, me1, name):
    _, r, c = p.shape

    def body(me_ref, p_ref, o_ref):
        del me_ref
        acc = p_ref[0]
        for k in range(1, N_DEV):
            acc = acc + p_ref[k]
        o_ref[...] = acc

    return pl.pallas_call(
        body, name=name,
        grid_spec=pltpu.PrefetchScalarGridSpec(
            num_scalar_prefetch=1, grid=(1,),
            in_specs=[pl.BlockSpec(p.shape, lambda i, me: (0, 0, 0))],
            out_specs=pl.BlockSpec((None, r, c), lambda i, me: (me[0], 0, 0))),
        out_shape=SDS(p.shape, F32), compiler_params=_cp("arbitrary"))(me1, p)


def _adamw(w, g, m, v):
    m = ADAM_B1 * m + (1.0 - ADAM_B1) * g
    v = ADAM_B2 * v + (1.0 - ADAM_B2) * (g * g)
    m_hat = m / (1.0 - ADAM_B1 ** ADAM_STEP)
    v_hat = v / (1.0 - ADAM_B2 ** ADAM_STEP)
    delta = -ADAM_LR * (m_hat / (jnp.sqrt(v_hat) + ADAM_EPS) + ADAM_WD * w)
    return delta, m, v


def _adam_tile(p_ref, w_ref, m_ref, v_ref, g_ref, d_ref, nm_ref, nv_ref):
    g = p_ref[0].astype(F32)
    for k in range(1, p_ref.shape[0]):
        g = g + p_ref[k].astype(F32)
    delta, nm, nv = _adamw(w_ref[...], g, m_ref[...], v_ref[...])
    g_ref[...] = g
    d_ref[...] = delta
    nm_ref[...] = nm
    nv_ref[...] = nv


class _AdamJob:
    def __init__(self, parts, w, m, v, layer, prev):
        self.args = [parts, w, m, v] + list(prev or ())
        self.layer, self.results = layer, None


def _carry(jobs, steps, body, n_in, n_out):
    in_specs, args, out_specs, out_shapes, aliases, n_prevs = [], [], [], [], {}, []
    for j, job in enumerate(jobs):
        _, r, c = job.args[0].shape
        nr = next(n for n in range(steps, 0, -1) if steps % n == 0 and r % (16 * n) == 0)
        nc = steps // nr
        assert c % (128 * nc) == 0
        tile = (r // nr, c // nc)
        blk = pl.BlockSpec((None,) + tile, lambda i, layer=job.layer, nc=nc: (layer, i // nc, i % nc))
        n_prev = len(job.args) - 4
        aliases.update({n_in + len(args) + 4 + k: n_out + 4 * j + k for k in range(n_prev)})
        in_specs += [pl.BlockSpec(job.args[0].shape[:1] + tile, lambda i, nc=nc: (0, i // nc, i % nc)), blk, blk, blk]
        in_specs += [_ANY] * n_prev
        args += job.args
        out_specs += [blk] * 4
        out_shapes += [SDS(job.args[1].shape, F32)] * 4
        n_prevs.append(n_prev)

    def carrying(*refs):
        ins, outs = refs[:n_in + len(args)], refs[n_in + len(args):]
        body(*ins[:n_in], *outs[:n_out])
        k = n_in
        for j, n_prev in enumerate(n_prevs):
            _adam_tile(*ins[k:k + 4], *outs[n_out + 4 * j:n_out + 4 * j + 4])
            k += 4 + n_prev

    return carrying, in_specs, args, out_specs, out_shapes, aliases


def _adam_shard(parts, w, m, v, layer, prev, name):
    n, r, c = parts.shape
    tr = next(t for t in (512, 352, r) if r % t == 0)
    n_prev = 0 if prev is None else 4

    def body(*refs):
        _adam_tile(*refs[:4], *refs[4 + n_prev:])

    blk = pl.BlockSpec((None, tr, c), lambda i: (layer, i, 0))
    return pl.pallas_call(
        body, name=name, grid=(r // tr,),
        in_specs=[pl.BlockSpec((n, tr, c), lambda i: (0, i, 0)), blk, blk, blk] + [_ANY] * n_prev,
        out_specs=[blk] * 4, out_shape=[SDS(w.shape, F32)] * 4,
        input_output_aliases={4 + k: k for k in range(n_prev)},
        compiler_params=_cp("parallel"))(parts, w, m, v, *(prev or ()))


SMALL_MATRICES = [("lru_w_r", 2048), ("lru_w_i", 2048), ("gmlp_w_s", 1024)]
SMALL_VECTORS = [("norm1_g", 8), ("gmlp_ln_g", 8), ("gmlp_ln_b", 8), ("gmlp_b_s", 8), ("conv_w", 32), ("conv_b", 8),
                 ("lru_b_r", 16), ("lru_b_i", 16), ("lru_lambda", 16), ("norm2_g", 8), ("final_g", 8)]
SMALL_VECTOR_ROW0 = sum(n for _, n in SMALL_MATRICES)
SMALL_VECTOR_BLOCK = 256
SMALL_ROWS = SMALL_VECTOR_ROW0 + SMALL_VECTOR_BLOCK
LOSS_ROW = SMALL_VECTOR_ROW0 + sum(n for _, n in SMALL_VECTORS)
assert LOSS_ROW + LANE_ROWS <= SMALL_ROWS


def _pack_small(small):
    parts = [small[k] for k, _ in SMALL_MATRICES]
    parts += [small[k] if k in small else jnp.zeros((n, HD), F32) for k, n in SMALL_VECTORS]
    parts += [small["loss"]] if "loss" in small else []
    flat = jnp.concatenate(parts)
    return jnp.pad(flat, ((0, SMALL_ROWS - flat.shape[0]), (0, 0))).reshape(N_DEV, SMALL_ROWS // N_DEV, HD)


def _adam_matrix(g0, g1, w, m, v, row0, name):
    _, rows, _ = w.shape
    tr = 512

    def body(g0_ref, g1_ref, w_ref, m_ref, v_ref, g_ref, d_ref, nm_ref, nv_ref):
        for l, src in enumerate((g0_ref, g1_ref)):
            g = src[...]
            delta, nm, nv = _adamw(w_ref[l], g, m_ref[l], v_ref[l])
            g_ref[l] = g
            d_ref[l] = delta
            nm_ref[l] = nm
            nv_ref[l] = nv

    gspec = pl.BlockSpec((tr, HD), lambda i: (row0 // tr + i, 0))
    blk = pl.BlockSpec((2, tr, HD), lambda i: (0, i, 0))
    return pl.pallas_call(body, name=name, grid=(rows // tr,), in_specs=[gspec, gspec] + [blk] * 3,
                          out_specs=[blk] * 4, out_shape=[SDS(w.shape, F32)] * 4,
                          compiler_params=_cp("parallel"))(g0, g1, w, m, v)


def _adam_vectors(g0, g1, dg1_parts, me1, ws, ms, vs):
    names = [k for k, _ in SMALL_VECTORS]
    n = len(names)

    def lanes(rows8):
        return jnp.concatenate([rows8[k:k + 1, :] for k in range(LANE_ROWS)], axis=1)

    def body(me_ref, g0_ref, g1_ref, dg1_ref, *refs):
        w_refs, m_refs, v_refs = refs[:n], refs[n:2 * n], refs[2 * n:3 * n]
        outs = refs[3 * n:]
        me = me_ref[0]
        g_refs = (g0_ref, g1_ref)

        def emit(i, idx, g):
            delta, nm, nv = _adamw(w_refs[i][idx], g, m_refs[i][idx], v_refs[i][idx])
            for j, val in enumerate((g, delta, nm, nv)):
                outs[4 * i + j][idx] = val

        off = 0
        for i, (name, rows) in enumerate(SMALL_VECTORS):
            for l in range(2):
                row = (slice(l, l + 1), slice(None))
                if name == "final_g":
                    if l == 1:
                        emit(i, (slice(0, 1), slice(None)), lanes(g1_ref[off:off + rows, :]))
                elif name == "norm1_g":
                    if l == 1:
                        emit(i, row, lanes(g0_ref[off:off + rows, :]))
                    else:
                        total = dg1_ref[0]
                        for k in range(1, N_DEV):
                            total = total + dg1_ref[k]
                        emit(i, row, lanes(total))
                elif name == "gmlp_b_s":
                    emit(i, (l,), g_refs[l][off:off + rows, :])
                elif rows == LANE_ROWS:
                    emit(i, row, lanes(g_refs[l][off:off + rows, :]))
                else:
                    for r in range(rows // LANE_ROWS):
                        emit(i, (l, slice(r, r + 1), slice(None)), g_refs[l][pl.ds(off + r * LANE_ROWS + me, 1), :])
            off += rows

    args = [ws[k] for k in names] + [ms[k] for k in names] + [vs[k] for k in names]
    gspec = pl.BlockSpec((SMALL_VECTOR_BLOCK, HD), lambda i, me: (SMALL_VECTOR_ROW0 // SMALL_VECTOR_BLOCK, 0))
    res = pl.pallas_call(
        body, name="adam_vectors",
        grid_spec=pltpu.PrefetchScalarGridSpec(
            num_scalar_prefetch=1, grid=(1,),
            in_specs=[gspec, gspec, _full(dg1_parts.shape)] + [_full(a.shape) for a in args],
            out_specs=[_full(ws[k].shape) for k in names for _ in range(4)]),
        out_shape=[SDS(ws[k].shape, F32) for k in names for _ in range(4)],
        compiler_params=_cp("arbitrary"))(me1, g0, g1, dg1_parts, *args)
    return {k: list(res[4 * i:4 * i + 4]) for i, k in enumerate(names)}


def _local_step(x, tgt, p, get_w, hook=lambda stage, layer, payload: None):
    s = x.shape[0]
    tm = _row_tile(s)
    wsb = p["gmlp_w_s"].astype(BF16)
    wstb = jnp.swapaxes(p["gmlp_w_s"], -1, -2).astype(BF16)
    bsb = jnp.broadcast_to(p["gmlp_b_s"][..., None], p["gmlp_w_s"].shape)
    wrb = p["lru_w_r"].astype(BF16)
    wib = p["lru_w_i"].astype(BF16)
    saved = []
    for l in range(2):
        win = get_w("w_in", l, x)
        z, h1 = _norm_inproj(x, p["norm1_g"][l][None], win, l, tm, after=[hook("pre_inproj", l, win)])
        a0, b0, a1, b1, xcb, *gates = _lru_gates_fwd(z, p["conv_w"][l], p["conv_b"][l][None], wrb[l], wib[l],
                                                     p["lru_b_r"][l], p["lru_b_i"][l], p["lru_lambda"][l], l, tm)
        h0, hr = _lru_scan(a0, b0, a1, b1, False, l)
        token = hook("pre_gmlp", l, h0)
        wout = get_w("w_out", l, h0 if token is None else token)
        x1, mg = _mixer_fwd(x, h0, hr, z, p["gmlp_ln_g"][l][None], p["gmlp_ln_b"][l][None], wsb[l], bsb[l], wout, l, tm)
        wfi = get_w("w_ffn_in", l, x1)
        wfo = get_w("w_ffn_out", l, x1)
        if l == 0:
            x2, ff, dff, h2 = _ffn_fwd(x1, p["norm2_g"][l][None], wfi, wfo, l, tm)
        else:
            dx, loss, dfg, ff, dff, h2 = _ffn_fwd(x1, p["norm2_g"][l][None], wfi, wfo, l, tm,
                                                  head=(p["final_g"][None], tgt))
        saved.append((x, z, h1, a0, a1, h0, hr, x1, mg, ff, dff, h2, win, wout, wfi, wfo, xcb, gates))
        x = x2
    for l in (1, 0):
        x0, z, h1, a0, a1, h0, hr, x1, mg, ff, dff, h2, win, wout, wfi, wfo, xcb, gates = saved[l]
        dgu, dx1, dg2 = _ffn_bwd(dx, wfo, dff, wfi.reshape(N_DEV, FF_BLK, D), x1, p["norm2_g"][l][None], l, tm)
        d_wfo = _mm_tn(ff, pl.BlockSpec((None, s, FF_BLK), lambda j: (j, 0, 0)), dx, _resident((s, D)),
                       4, (4, FF_BLK, D), pl.BlockSpec((None, FF_BLK, D), lambda j: (j, 0, 0)),
                       f"dw_ffn_out_l{l}", a_is_transposed=False)
        dgu8 = dgu.reshape(N_DEV, s, FF_BLK)
        d_wfi = _mm_tn(dgu8, pl.BlockSpec((None, s, FF_BLK), lambda j: (j, 0, 0)), h2, _resident((s, D)),
                       N_DEV, (N_DEV, FF_BLK, D), pl.BlockSpec((None, FF_BLK, D), lambda j: (j, 0, 0)),
                       f"dw_ffn_in_l{l}", a_is_transposed=False)
        d_wout = _mm_tn(mg, _resident((D, s)), dx1, pl.BlockSpec((s, D // 2), lambda j: (0, j)),
                        2, (D, D), pl.BlockSpec((D, D // 2), lambda j: (0, j)), f"dw_out_l{l}")
        token = hook("ffn_partials", l, dict(w_ffn_out=d_wfo.reshape(N_DEV, D_FF // N_DEV, D), w_ffn_in=d_wfi,
                                             w_out=d_wout.reshape(N_DEV, D // N_DEV, D)))
        pending = hook("mid_backward", l, dx1)
        dz, dh, dws, dbs, dlng, dlnb = _mixer_bwd(dx1, wout, h0, hr, z, p["gmlp_ln_g"][l][None], p["gmlp_ln_b"][l][None],
                                                  wsb[l], wstb[l], bsb[l], l, tm, after=[token])
        g1, g0 = _lru_scan(a1, dh, a0, dh, True, l)
        dxc, dwr, dwi, dbr, dbi, dlam = _lru_gates_bwd(
            xcb, gates, h0, hr, g0, g1, wrb[l], wib[l], p["lru_lambda"][l], l, tm, after=[pending])
        dz, dcw, dcb = _conv_bwd(dz, dxc, z, p["conv_w"][l], l, tm)
        small = dict(lru_w_r=dwr.reshape(-1, HD), lru_w_i=dwi.reshape(-1, HD), gmlp_w_s=dws.reshape(-1, HD),
                     gmlp_ln_g=dlng, gmlp_ln_b=dlnb, gmlp_b_s=dbs, conv_w=dcw, conv_b=dcb, lru_b_r=dbr,
                     lru_b_i=dbi, lru_lambda=dlam, norm2_g=dg2)
        if l == 1:
            small["final_g"] = dfg
            small["loss"] = jnp.broadcast_to(loss, (LANE_ROWS, HD))
        else:
            small["norm1_g"] = dg1
        started = hook("small_grads", l, small)
        d_win = _mm_tn(h1, _resident((D, s)), dz, pl.BlockSpec((s, IN_BLK), lambda j: (0, j)),
                       N_DEV, (N_DEV, D, IN_BLK), pl.BlockSpec((None, D, IN_BLK), lambda j: (j, 0, 0)),
                       f"dw_in_l{l}", after=started, jobs=hook("dw_in", l, started) or ())
        token = hook("mixer_partials", l, dict(w_in=d_win))
        dx, dg1 = _mm_nt_rms_bwd(
            dz, pl.BlockSpec((tm, N_IN), lambda i: (i, 0)),
            lambda r: [r[:, k * IN_BLK:(k + 1) * IN_BLK] for k in range(N_DEV)],
            win, False, x0, p["norm1_g"][l][None], dx1, f"inproj_bwd_dx_l{l}", tm,
            after=[token], jobs=hook("inproj_bwd_dx", l, token) or ())
    return loss, dx, dg1


_REPL = ["norm1_g", "gmlp_ln_g", "gmlp_ln_b", "gmlp_w_s", "gmlp_b_s", "conv_b", "lru_w_r", "lru_w_i", "norm2_g", "final_g"]
_LANE_SHARDED = ["conv_w", "lru_b_r", "lru_b_i", "lru_lambda"]
_BIG = ["w_in", "w_out", "w_ffn_in", "w_ffn_out"]
_ORDER = ["norm1_g", "w_in", "gmlp_ln_g", "gmlp_ln_b", "gmlp_w_s", "gmlp_b_s", "conv_w", "conv_b", "lru_w_r", "lru_b_r",
          "lru_w_i", "lru_b_i", "lru_lambda", "w_out", "norm2_g", "w_ffn_in", "w_ffn_out", "final_g"]


def kernel(x, norm1_g, w_in, gmlp_ln_g, gmlp_ln_b, gmlp_w_s, gmlp_b_s, conv_w, conv_b, lru_w_r, lru_b_r, lru_w_i, lru_b_i, lru_lambda, w_out, norm2_g, w_ffn_in, w_ffn_out, final_g, loss_target, m_norm1_g, m_w_in, m_gmlp_ln_g, m_gmlp_ln_b, m_gmlp_w_s, m_gmlp_b_s, m_conv_w, m_conv_b, m_lru_w_r, m_lru_b_r, m_lru_w_i, m_lru_b_i, m_lru_lambda, m_w_out, m_norm2_g, m_w_ffn_in, m_w_ffn_out, m_final_g, v_norm1_g, v_w_in, v_gmlp_ln_g, v_gmlp_ln_b, v_gmlp_w_s, v_gmlp_b_s, v_conv_w, v_conv_b, v_lru_w_r, v_lru_b_r, v_lru_w_i, v_lru_b_i, v_lru_lambda, v_w_out, v_norm2_g, v_w_ffn_in, v_w_ffn_out, v_final_g):
    w = dict(norm1_g=norm1_g, w_in=w_in, gmlp_ln_g=gmlp_ln_g, gmlp_ln_b=gmlp_ln_b, gmlp_w_s=gmlp_w_s, gmlp_b_s=gmlp_b_s,
             conv_w=conv_w, conv_b=conv_b, lru_w_r=lru_w_r, lru_b_r=lru_b_r, lru_w_i=lru_w_i, lru_b_i=lru_b_i,
             lru_lambda=lru_lambda, w_out=w_out, norm2_g=norm2_g, w_ffn_in=w_ffn_in, w_ffn_out=w_ffn_out, final_g=final_g)
    mom = dict(norm1_g=m_norm1_g, w_in=m_w_in, gmlp_ln_g=m_gmlp_ln_g, gmlp_ln_b=m_gmlp_ln_b, gmlp_w_s=m_gmlp_w_s,
               gmlp_b_s=m_gmlp_b_s, conv_w=m_conv_w, conv_b=m_conv_b, lru_w_r=m_lru_w_r, lru_b_r=m_lru_b_r,
               lru_w_i=m_lru_w_i, lru_b_i=m_lru_b_i, lru_lambda=m_lru_lambda, w_out=m_w_out, norm2_g=m_norm2_g,
               w_ffn_in=m_w_ffn_in, w_ffn_out=m_w_ffn_out, final_g=m_final_g)
    var = dict(norm1_g=v_norm1_g, w_in=v_w_in, gmlp_ln_g=v_gmlp_ln_g, gmlp_ln_b=v_gmlp_ln_b, gmlp_w_s=v_gmlp_w_s,
               gmlp_b_s=v_gmlp_b_s, conv_w=v_conv_w, conv_b=v_conv_b, lru_w_r=v_lru_w_r, lru_b_r=v_lru_b_r,
               lru_w_i=v_lru_w_i, lru_b_i=v_lru_b_i, lru_lambda=v_lru_lambda, w_out=v_w_out, norm2_g=v_norm2_g,
               w_ffn_in=v_w_ffn_in, w_ffn_out=v_w_ffn_out, final_g=v_final_g)
    for src in (w, mom, var):
        src["w_ffn_in"] = jnp.swapaxes(src["w_ffn_in"], 1, 2)
    xi, yi, ci = _me()
    me = 4 * xi + 2 * yi + ci

    lane_shapes = [w[k].shape for k in _LANE_SHARDED]
    lane_rows = sum(a[0] * a[1] for a in lane_shapes)
    packed = jnp.concatenate([w[k].reshape(-1, HD) for k in _LANE_SHARDED])
    packed = jnp.pad(packed, ((0, -lane_rows % 8), (0, 0)))

    me1 = jnp.reshape(me, (1,)).astype(jnp.int32)
    gathers = {}
    exchanges = {}
    views = dict(w_in=(N_DEV, D, IN_BLK), w_out=(D, D), w_ffn_in=(2, 4, FF_BLK, D), w_ffn_out=(4, FF_BLK, D))
    small_ex = {}
    small_ag = {}

    casts = {}

    def start_gather(names, l, after):
        lands = [casts[(k, l)] if (k, l) in casts else _cast_into_slot(w[k], l, me1, f"cast_{k}_l{l}") for k in names]
        started, tok = _gather2_start(lands, after, f"gather_start_{'_'.join(names)}_l{l}")
        gathers.update({(k, l): h for k, h in zip(names, started)})
        return tok

    def relay_gather(names, l, after):
        relayed, tok = _gather2_relay([gathers[(k, l)] for k in names], after, f"gather_relay_{'_'.join(names)}_l{l}")
        gathers.update({(k, l): h for k, h in zip(names, relayed)})
        return tok

    def get_w(k, l, after):
        return _gather2_wait(gathers[(k, l)], after, f"gather_wait_{k}_l{l}").reshape(views[k])

    carried = {("inproj_bwd_dx", 1): [("w_ffn_out", 1), ("w_ffn_in", 1), ("w_out", 1)], ("dw_in", 0): [("w_in", 1)],
               ("inproj_bwd_dx", 0): [("w_ffn_out", 0), ("w_ffn_in", 0), ("w_out", 0)]}
    adam = {}

    def adam_jobs(shards, after):
        for k, l in shards:
            got = _exchange_wait(exchanges[(k, l)], after, f"exchange_wait_{k}_l{l}")
            adam[k] = _AdamJob(got, w[k], mom[k], var[k], l, adam[k].results if k in adam else None)
        return [adam[k] for k, _ in shards]

    def hook(stage, l, payload):
        if stage in ("dw_in", "inproj_bwd_dx"):
            return adam_jobs(carried.get((stage, l), []), payload)
        if stage == "pre_inproj":
            tok = start_gather(_BIG[1:], l, payload)
            return start_gather(_BIG[:1], l + 1, tok) if l == 0 else tok
        if stage == "pre_gmlp":
            tok = relay_gather(_BIG[1:], l, payload)
            return relay_gather(_BIG[:1], l + 1, tok) if l == 0 else tok
        if stage == "small_grads":
            (small_ex[l],), tok = _exchange_start([_pack_small(payload)], f"exchange_start_small_l{l}")
            return tok
        if stage == "mid_backward":
            return reduce_small(l + 1, payload) if l == 0 else None
        if (stage, l) == ("mixer_partials", 0):
            pairs, tok = _pairs_start(payload["w_in"], "pairs_start_w_in_l0")
            p, pair = _pairs_wait(pairs, reduce_small(0, tok), "pairs_wait_w_in_l0")
            sums = _sum_pairs(p, pair, me1, "sum_pairs_w_in_l0")
            exchanges[("w_in", 0)], tok = _chips_start(sums, "chips_start_w_in_l0")
            return tok
        started, tok = _exchange_start(list(payload.values()), f"exchange_start_{'_'.join(payload)}_l{l}")
        exchanges.update({(k, l): h for k, h in zip(payload, started)})
        return tok

    def reduce_small(l, after):
        got = _exchange_wait(small_ex[l], after, f"exchange_wait_small_l{l}")
        mine = _sum8_into_slot(got, me1, f"sum_small_l{l}")
        (small_ag[l],), tok = _gather_start([mine], got, f"gather_start_small_l{l}")
        return tok

    land = lax.dynamic_update_slice(jnp.zeros((N_DEV,) + packed.shape, F32), packed[None], (me, 0, 0))
    token = start_gather(_BIG[:1], 0, x)
    (lanes_handle,), token = _gather_start([land], token, "gather_start_lanes")
    later = [(k, l) for l in range(2) for k in _BIG if (k, l) != ("w_in", 0)]
    casts.update(zip(later, _cast_all_into_slots([w[k] for k, _ in later], [l for _, l in later], me1, token,
                                                 "cast_later_weights")))
    token = relay_gather(_BIG[:1], 0, casts[later[0]])
    lanes = _gather_wait(lanes_handle, token, "gather_wait_lanes")
    params = {k: w[k] for k in _REPL}
    off = 0
    for k, shp in zip(_LANE_SHARDED, lane_shapes):
        n = shp[0] * shp[1]
        params[k] = jnp.swapaxes(lanes[:, off:off + n], 0, 1).reshape(shp[0], shp[1], D)
        off += n
    _, dx, dg1 = _local_step(x[0], loss_target[0], params, get_w, hook)

    out = {k: job.results for k, job in adam.items()}
    after = dx
    g_small = [_gather_wait(small_ag[l], after, f"gather_wait_small_l{l}").reshape(SMALL_ROWS, HD) for l in (0, 1)]
    row0 = 0
    for k, rows in SMALL_MATRICES:
        res = _adam_matrix(*g_small, *[src[k].reshape(2, rows, HD) for src in (w, mom, var)], row0, f"adam_{k}")
        out[k] = [a.reshape(w[k].shape) for a in res]
        after = res[3]
        row0 += rows
    got = _chips_wait(exchanges[("w_in", 0)], after, "chips_wait_w_in_l0")
    out["w_in"] = _adam_shard(got, w["w_in"], mom["w_in"], var["w_in"], 0, out["w_in"], "adam_w_in_l0")
    out["w_ffn_in"] = [jnp.swapaxes(a, 1, 2) for a in out["w_ffn_in"]]
    as_rows = lambda a: a.reshape(1, D) if a.ndim == 1 else a
    vec = _adam_vectors(*g_small, _all_gather(dg1, out["w_in"][3], "gather_norm1_grad"), me1,
                        *[{k: as_rows(src[k]) for k, _ in SMALL_VECTORS} for src in (w, mom, var)])
    out.update({k: [a.reshape(w[k].shape) for a in res] for k, res in vec.items()})

    return (g_small[1][LOSS_ROW, 0], dx[None], *[out[k][0] for k in _ORDER], *[out[k][1] for k in _ORDER],
            *[out[k][2] for k in _ORDER], *[out[k][3] for k in _ORDER])
```

```python
import jax
import jax.numpy as jnp
from jax import lax
from jax.experimental import pallas as pl
from jax.experimental.pallas import tpu as pltpu

F32 = jnp.float32
BF16 = jnp.bfloat16
SDS = jax.ShapeDtypeStruct

D = 1024
N_IN = 6 * D
D_FF = 2816
N_DEV = 8
IN_BLK = N_IN // N_DEV
FF_BLK = 2 * D_FF // N_DEV
HEADS = 8
HD = 128
EPS = 1e-6
LRU_C = 8.0

ADAM_LR = 0.001
ADAM_B1 = 0.9
ADAM_B2 = 0.999
ADAM_EPS = 1e-08
ADAM_WD = 0.01
ADAM_STEP = 10

VMEM_LIMIT = 60 * 2**20


def _cp(*sem, **kw):
    return pltpu.CompilerParams(dimension_semantics=sem, vmem_limit_bytes=VMEM_LIMIT, **kw)


def _row_tile(s):
    return 512 if s >= 1024 else s // 2


_GELU_C = 0.7978845608028654


def _gelu(x):
    t = jnp.tanh(_GELU_C * (x + 0.044715 * (x * x * x)))
    return 0.5 * x * (1.0 + t), t


def _gelu_grad(x, t):
    return 0.5 * (1.0 + t) + 0.5 * x * (1.0 - t * t) * (_GELU_C * (1.0 + 0.134145 * (x * x)))


def _sigmoid(x):
    return 0.5 + 0.5 * jnp.tanh(0.5 * x)


def _softplus(x):
    e = jnp.exp(-jnp.abs(x))
    w = 1.0 + e
    l1p = jnp.where(w == 1.0, e, jnp.log(w) * e / jnp.where(w == 1.0, 1.0, w - 1.0))
    return jnp.maximum(x, 0.0) + l1p


def _rms_fwd(x, g):
    r = lax.rsqrt(jnp.mean(x * x, axis=-1, keepdims=True) + EPS)
    return x * r * g


def _rms_bwd(x, g, dh):
    r = lax.rsqrt(jnp.mean(x * x, axis=-1, keepdims=True) + EPS)
    xh = x * r
    dxh = dh * g
    dx = r * (dxh - xh * jnp.mean(dxh * xh, axis=-1, keepdims=True))
    dg = jnp.sum(dh * xh, axis=0, keepdims=True)
    return dx, dg


LANE_ROWS = D // HD


def _add_rows128(ref, vec, row0=0):
    for i in range(vec.shape[0]):
        for k in range(LANE_ROWS):
            j = row0 + i * LANE_ROWS + k
            ref[j:j + 1, :] += vec[i:i + 1, k * HD:(k + 1) * HD]


def _dot(a, b):
    return jnp.dot(a, b, preferred_element_type=F32)


def _dot_nt(a, b):
    return lax.dot_general(a, b, (((1,), (1,)), ((), ())), preferred_element_type=F32)


def _dot_tn(a, b):
    return lax.dot_general(a, b, (((0,), (0,)), ((), ())), preferred_element_type=F32)


def _taps(prev, cur, nxt, tm):
    hr = prev.shape[0]
    ext = jnp.concatenate([prev, cur, nxt], axis=0)
    n = tm + 2 * hr
    sl = slice(hr, hr + tm)
    return (pltpu.roll(ext, 2, 0)[sl], pltpu.roll(ext, 1, 0)[sl], cur,
            pltpu.roll(ext, n - 1, 0)[sl], pltpu.roll(ext, n - 2, 0)[sl])


def _halo_specs(tm, s, col, rows=8):
    nb = s // rows
    r = tm // rows
    return (pl.BlockSpec((rows, D), lambda i: (jnp.maximum(i * r - 1, 0), col)),
            pl.BlockSpec((tm, D), lambda i: (i, col)),
            pl.BlockSpec((rows, D), lambda i: (jnp.minimum((i + 1) * r, nb - 1), col)))


def _halo_load(prev_ref, cur_ref, next_ref, fp, fn):
    return prev_ref[...].astype(F32) * fp, cur_ref[...].astype(F32), next_ref[...].astype(F32) * fn


def _halo_flags(nt):
    i = pl.program_id(0)
    return (i > 0).astype(F32), (i < nt - 1).astype(F32)


def _full(shape):
    nd = len(shape)
    return pl.BlockSpec(shape, lambda *_: (0,) * nd)


def _resident(shape):
    nd = len(shape)
    return pl.BlockSpec(shape, lambda *_: (0,) * nd, pipeline_mode=pl.Buffered(1))


def _behind(tokens, body, n_in):
    deps = [t for t in tokens if t is not None]

    def ordered(*refs):
        body(*refs[:n_in], *refs[n_in + len(deps):])

    return ordered, [_ANY] * len(deps), deps


def _norm_inproj(x, g, w, layer, tm, after=()):
    s = x.shape[0]

    def body(x_ref, g_ref, w_ref, z_ref, ht_ref):
        h32 = _rms_fwd(x_ref[...], g_ref[...])
        ht_ref[...] = h32.T.astype(BF16)
        h = h32.astype(BF16)
        for j in range(N_DEV):
            z_ref[:, j * IN_BLK:(j + 1) * IN_BLK] = _dot(h, w_ref[j]).astype(BF16)

    body, dep_specs, deps = _behind(after, body, 3)
    return pl.pallas_call(
        body, name=f"norm_inproj_l{layer}", grid=(s // tm,),
        in_specs=[pl.BlockSpec((tm, D), lambda i: (i, 0)), _full((1, D)), _resident((N_DEV, D, IN_BLK))] + dep_specs,
        out_specs=[pl.BlockSpec((tm, N_IN), lambda i: (i, 0)), pl.BlockSpec((D, tm), lambda i: (0, i))],
        out_shape=[SDS((s, N_IN), BF16), SDS((D, s), BF16)],
        compiler_params=_cp("parallel"))(x, g, w, *deps)


def _gmlp_values(zu_ref, zv_ref, lng_ref, lnb_ref):
    zu = zu_ref[...].astype(F32)
    zv = zv_ref[...].astype(F32)
    u, tu = _gelu(zu)
    gv, tv = _gelu(zv)
    xc = gv - jnp.mean(gv, axis=-1, keepdims=True)
    rstd = lax.rsqrt(jnp.mean(xc * xc, axis=-1, keepdims=True) + EPS)
    xh = xc * rstd
    vb = (xh * lng_ref[...] + lnb_ref[...]).astype(BF16)
    return zu, zv, u, tu, tv, xh, rstd, vb


def _mixer_fwd(x, h0, h1, z, lng, lnb, ws, bsb, wo, layer, tm):
    s = x.shape[0]

    def body(x_ref, h0_ref, h1_ref, zu_ref, zv_ref, zg_ref, za_ref, zb_ref, lng_ref, lnb_ref, ws_ref, bsb_ref,
             wo_ref, x1_ref, mg_ref, ya_s):
        _, _, u, _, _, _, _, vb = _gmlp_values(zu_ref, zv_ref, lng_ref, lnb_ref)
        for c in range(tm // HD):
            rs = slice(c * HD, (c + 1) * HD)
            for g in range(HEADS):
                cs = slice(g * HD, (g + 1) * HD)
                ya_s[rs, cs] = u[rs, cs] * (_dot(ws_ref[g], vb[rs, cs]) + bsb_ref[g])
        gg, _ = _gelu(zg_ref[...].astype(F32))
        yb = (h0_ref[...] + h1_ref[...]) * gg
        m32 = _sigmoid(za_ref[...].astype(F32)) * ya_s[...] + _sigmoid(zb_ref[...].astype(F32)) * yb
        mg_ref[...] = m32.T.astype(BF16)
        x1_ref[...] = x_ref[...] + _dot(m32.astype(BF16), wo_ref[...])

    tile = pl.BlockSpec((tm, D), lambda i: (i, 0))
    wspec = _full((HEADS, HD, HD))
    return pl.pallas_call(
        body, name=f"mixer_fwd_l{layer}", grid=(s // tm,),
        in_specs=[tile, tile, tile] + [pl.BlockSpec((tm, D), lambda i, c=c: (i, c)) for c in (0, 1, 3, 4, 5)]
        + [_full((1, D)), _full((1, D)), wspec, wspec, _full((D, D))],
        out_specs=[tile, pl.BlockSpec((D, tm), lambda i: (0, i))], out_shape=[SDS((s, D), F32), SDS((D, s), BF16)],
        scratch_shapes=[pltpu.VMEM((tm, D), F32)],
        compiler_params=_cp("parallel"))(x, h0, h1, z, z, z, z, z, lng, lnb, ws, bsb, wo)


def _conv(taps, cw_ref, cb_ref):
    _, m1, c0, p1, p2 = taps
    return cb_ref[...] + m1 * cw_ref[0:1, :] + c0 * cw_ref[1:2, :] + p1 * cw_ref[2:3, :] + p2 * cw_ref[3:4, :]


def _heads_dot(xb, w_ref, d):
    return jnp.concatenate([_dot(xb[:, h * HD:(h + 1) * HD], w_ref[d, h]) for h in range(HEADS)], axis=1)


def _lru_decay(r, sp):
    la = (-LRU_C) * r * sp
    a = jnp.exp(la)
    return a, jnp.tanh(-la) * (a * a + 1.0)


def _lru_gates_fwd(z, cw, cb, wr, wi, br, bi, lam, layer, tm):
    s = z.shape[0]
    nt = s // tm

    def body(zp_ref, zc_ref, zn_ref, cw_ref, cb_ref, wr_ref, wi_ref, br_ref, bi_ref, lam_ref,
             a0_ref, b0_ref, a1_ref, b1_ref, xc_ref, r0_ref, i0_ref, r1_ref, i1_ref):
        fp, fn = _halo_flags(nt)
        xc = _conv(_taps(*_halo_load(zp_ref, zc_ref, zn_ref, fp, fn), tm), cw_ref, cb_ref)
        xb = xc.astype(BF16)
        xc_ref[...] = xb
        for d, (a_ref, b_ref, r_ref, i_ref) in enumerate(((a0_ref, b0_ref, r0_ref, i0_ref),
                                                          (a1_ref, b1_ref, r1_ref, i1_ref))):
            r = _sigmoid(_heads_dot(xb, wr_ref, d) + br_ref[d:d + 1, :])
            ig = _sigmoid(_heads_dot(xb, wi_ref, d) + bi_ref[d:d + 1, :])
            a, q = _lru_decay(r, _softplus(-lam_ref[d:d + 1, :]))
            a_ref[...] = a
            b_ref[...] = jnp.sqrt(q) * (ig * xc)
            r_ref[...] = r.astype(BF16)
            i_ref[...] = ig.astype(BF16)

    tile = pl.BlockSpec((tm, D), lambda i: (i, 0))
    return pl.pallas_call(
        body, name=f"lru_gates_fwd_l{layer}", grid=(nt,),
        in_specs=[*_halo_specs(tm, s, 2, 16), _full((4, D)), _full((1, D)),
                  _full((2, HEADS, HD, HD)), _full((2, HEADS, HD, HD)), _full((2, D)), _full((2, D)), _full((2, D))],
        out_specs=[tile] * 9, out_shape=[SDS((s, D), F32)] * 4 + [SDS((s, D), BF16)] * 5,
        compiler_params=_cp("parallel"))(z, z, z, cw, cb, wr, wi, br, bi, lam)


def _scan_group(a, x, c, reverse, bwd):
    row = lax.broadcasted_iota(jnp.int32, a.shape, 0)
    b = a * x if bwd else x
    for d in (1, 2, 4):
        keep = (row < 8 - d) if reverse else (row >= d)
        sh = 8 - d if reverse else d
        a_s = jnp.where(keep, pltpu.roll(a, sh, 0), 1.0)
        b_s = jnp.where(keep, pltpu.roll(b, sh, 0), 0.0)
        b = a * b_s + b
        a = a * a_s
    h = b + a * c
    new_c = h[0:1, :] if reverse else h[7:8, :]
    if not bwd:
        return h, new_c
    if reverse:
        prev = jnp.where(row < 7, pltpu.roll(h, 7, 0), c)
    else:
        prev = jnp.where(row >= 1, pltpu.roll(h, 1, 0), c)
    return x + prev, new_c


def _lru_scan(a_f, x_f, a_r, x_r, bwd, layer):
    s = a_f.shape[0]
    ts = min(1024, s // 2)
    cb = 512
    nt = s // ts
    ng = ts // 8

    def body(af_ref, xf_ref, ar_ref, xr_ref, of_ref, or_ref, cf, cr):
        @pl.when(pl.program_id(1) == 0)
        def _():
            cf[...] = jnp.zeros_like(cf)
            cr[...] = jnp.zeros_like(cr)

        def step(j, carry):
            c_f, c_r = carry
            rf = pl.multiple_of(j * 8, 8)
            rr = pl.multiple_of((ng - 1 - j) * 8, 8)
            o, c_f = _scan_group(af_ref[pl.ds(rf, 8), :], xf_ref[pl.ds(rf, 8), :], c_f, False, bwd)
            of_ref[pl.ds(rf, 8), :] = o
            o, c_r = _scan_group(ar_ref[pl.ds(rr, 8), :], xr_ref[pl.ds(rr, 8), :], c_r, True, bwd)
            or_ref[pl.ds(rr, 8), :] = o
            return c_f, c_r

        c_f, c_r = lax.fori_loop(0, ng, step, (cf[0:1, :], cr[0:1, :]), unroll=2)
        cf[...] = jnp.broadcast_to(c_f, cf.shape)
        cr[...] = jnp.broadcast_to(c_r, cr.shape)

    fwd = pl.BlockSpec((ts, cb), lambda c, t: (t, c))
    rev = pl.BlockSpec((ts, cb), lambda c, t: (nt - 1 - t, c))
    return pl.pallas_call(
        body, name=f"lru_scan_{'bwd' if bwd else 'fwd'}_l{layer}", grid=(D // cb, nt),
        in_specs=[fwd, fwd, rev, rev], out_specs=[fwd, rev],
        out_shape=[SDS((s, D), F32)] * 2,
        scratch_shapes=[pltpu.VMEM((8, cb), F32), pltpu.VMEM((8, cb), F32)],
        compiler_params=_cp("parallel", "arbitrary"))(a_f, x_f, a_r, x_r)


def _ffn_fwd(x1, g, wfi, wfo, layer, tm, head=None):
    s = x1.shape[0]

    def ffn(x_ref, g_ref, wi_ref, wo_ref, ff_ref, dff_ref, h_ref):
        x = x_ref[...]
        h = _rms_fwd(x, g_ref[...]).astype(BF16)
        h_ref[...] = h
        acc = x
        for k in range(4):
            gate = _dot_nt(h, wi_ref[0, k])
            up = _dot_nt(h, wi_ref[1, k])
            sg = _sigmoid(gate)
            silu = gate * sg
            ff = (silu * up).astype(BF16)
            ff_ref[k] = ff
            dff_ref[0, k] = (up * (sg * (1.0 + gate * (1.0 - sg)))).astype(BF16)
            dff_ref[1, k] = silu.astype(BF16)
            acc = acc + _dot(ff, wo_ref[k])
        return acc

    def body(x_ref, g_ref, wi_ref, wo_ref, x2_ref, ff_ref, dff_ref, h_ref):
        x2_ref[...] = ffn(x_ref, g_ref, wi_ref, wo_ref, ff_ref, dff_ref, h_ref)

    def body_with_head(x_ref, g_ref, wi_ref, wo_ref, fg_ref, t_ref, dx_ref, loss_ref, dfg_ref, ff_ref, dff_ref, h_ref):
        @pl.when(pl.program_id(0) == 0)
        def _():
            loss_ref[...] = jnp.zeros_like(loss_ref)
            dfg_ref[...] = jnp.zeros_like(dfg_ref)

        x2 = ffn(x_ref, g_ref, wi_ref, wo_ref, ff_ref, dff_ref, h_ref)
        fg = fg_ref[...]
        e = _rms_fwd(x2, fg) - t_ref[...]
        rows = jnp.sum(e * e, axis=-1, keepdims=True)
        loss_ref[...] += (0.5 / D) * jnp.sum(rows, axis=0, keepdims=True)
        dx, dg = _rms_bwd(x2, fg, e * (1.0 / D))
        dx_ref[...] = dx
        _add_rows128(dfg_ref, dg)

    tile = pl.BlockSpec((tm, D), lambda i: (i, 0))
    weights = [_resident((2, 4, FF_BLK, D)), _resident((4, FF_BLK, D))]
    kept_specs = [pl.BlockSpec((4, tm, FF_BLK), lambda i: (0, i, 0)),
                  pl.BlockSpec((2, 4, tm, FF_BLK), lambda i: (0, 0, i, 0)), tile]
    kept_shapes = [SDS((4, s, FF_BLK), BF16), SDS((2, 4, s, FF_BLK), BF16), SDS((s, D), BF16)]
    if head is None:
        return pl.pallas_call(
            body, name=f"ffn_fwd_l{layer}", grid=(s // tm,),
            in_specs=[tile, _full((1, D))] + weights, out_specs=[tile] + kept_specs,
            out_shape=[SDS((s, D), F32)] + kept_shapes, compiler_params=_cp("parallel"))(x1, g, wfi, wfo)
    final_g, tgt = head
    return pl.pallas_call(
        body_with_head, name=f"ffn_fwd_loss_l{layer}", grid=(s // tm,),
        in_specs=[tile, _full((1, D))] + weights + [_full((1, D)), tile],
        out_specs=[tile, _full((1, 1)), _full((LANE_ROWS, HD))] + kept_specs,
        out_shape=[SDS((s, D), F32), SDS((1, 1), F32), SDS((LANE_ROWS, HD), F32)] + kept_shapes,
        compiler_params=_cp("arbitrary"))(x1, g, wfi, wfo, final_g, tgt)


def _ffn_bwd(dx2, wfo, factors, wfi, x1, g, layer, tm):
    s = dx2.shape[0]

    def body(dx_ref, wo_ref, f_ref, wi_ref, x_ref, g_ref, dgu_ref, dx1_ref, dg_ref):
        @pl.when(pl.program_id(0) == 0)
        def _():
            dg_ref[...] = jnp.zeros_like(dg_ref)

        dx = dx_ref[...]
        dxb = dx.astype(BF16)
        dh = None
        for k in range(4):
            dff = _dot_nt(dxb, wo_ref[k])
            d_gate = (dff * f_ref[0, k].astype(F32)).astype(BF16)
            d_up = (dff * f_ref[1, k].astype(F32)).astype(BF16)
            dgu_ref[0, k] = d_gate
            dgu_ref[1, k] = d_up
            part = _dot(d_gate, wi_ref[k]) + _dot(d_up, wi_ref[4 + k])
            dh = part if dh is None else dh + part
        dxn, dg = _rms_bwd(x_ref[...], g_ref[...], dh)
        dx1_ref[...] = dx + dxn
        _add_rows128(dg_ref, dg)

    tile = pl.BlockSpec((tm, D), lambda i: (i, 0))
    blk = pl.BlockSpec((2, 4, tm, FF_BLK), lambda i: (0, 0, i, 0))
    return pl.pallas_call(
        body, name=f"ffn_bwd_l{layer}", grid=(s // tm,),
        in_specs=[tile, _resident((4, FF_BLK, D)), blk, _resident((N_DEV, FF_BLK, D)), tile, _full((1, D))],
        out_specs=[blk, tile, _full((LANE_ROWS, HD))],
        out_shape=[SDS((2, 4, s, FF_BLK), BF16), SDS((s, D), F32), SDS((LANE_ROWS, HD), F32)],
        compiler_params=_cp("arbitrary"))(dx2, wfo, factors, wfi, x1, g)


def _mm_nt_rms_bwd(a, a_spec, a_blocks, w, w_is_transposed, x, g, dres, name, tm, after=(), jobs=()):
    s = x.shape[0]

    def body(a_ref, w_ref, x_ref, g_ref, dres_ref, dx_ref, dg_ref):
        @pl.when(pl.program_id(0) == 0)
        def _():
            dg_ref[...] = jnp.zeros_like(dg_ref)

        dh = None
        for k, blk in enumerate(a_blocks(a_ref)):
            part = _dot(blk, w_ref[k]) if w_is_transposed else _dot_nt(blk, w_ref[k])
            dh = part if dh is None else dh + part
        dx, dg = _rms_bwd(x_ref[...], g_ref[...], dh)
        dx_ref[...] = dres_ref[...] + dx
        _add_rows128(dg_ref, dg)

    tile = pl.BlockSpec((tm, D), lambda i: (i, 0))
    body, dep_specs, deps = _behind(after, body, 5)
    body, job_in, job_args, job_out, job_shapes, aliases = _carry(jobs, s // tm, body, 5 + len(deps), 2)
    res = pl.pallas_call(
        body, name=name, grid=(s // tm,),
        in_specs=[a_spec, _resident(w.shape), tile, _full((1, D)), tile] + dep_specs + job_in,
        out_specs=[tile, _full((LANE_ROWS, HD))] + job_out,
        out_shape=[SDS((s, D), F32), SDS((LANE_ROWS, HD), F32)] + job_shapes,
        input_output_aliases=aliases, compiler_params=_cp("arbitrary"))(a, w, x, g, dres, *deps, *job_args)
    for j, job in enumerate(jobs):
        job.results = res[2 + 4 * j:6 + 4 * j]
    return res[0], res[1]


def _mm_tn(a, a_spec, b, b_spec, nb, out_shape, out_spec, name, a_is_transposed=True, after=None, jobs=()):
    def body(a_ref, b_ref, *rest):
        o_ref = rest[-1]
        bb = b_ref[...].astype(BF16)
        o_ref[...] = (_dot(a_ref[...], bb) if a_is_transposed else _dot_tn(a_ref[...], bb)).astype(BF16)

    deps = [] if after is None else [after]
    body, job_in, job_args, job_out, job_shapes, aliases = _carry(jobs, nb, body, 2 + len(deps), 1)
    res = pl.pallas_call(
        body, name=name, grid=(nb,), in_specs=[a_spec, b_spec] + [_ANY] * len(deps) + job_in,
        out_specs=[out_spec] + job_out, out_shape=[SDS(out_shape, BF16)] + job_shapes,
        input_output_aliases=aliases, compiler_params=_cp("parallel"))(a, b, *deps, *job_args)
    for j, job in enumerate(jobs):
        job.results = res[1 + 4 * j:5 + 4 * j]
    return res[0]


def _mixer_bwd(dx1, wo, h0, h1, z, lng, lnb, ws, wst, bsb, layer, tm, after=()):
    s = dx1.shape[0]
    nt = s // tm

    def body(dx_ref, wo_ref, h0_ref, h1_ref, zu_ref, zv_ref, zg_ref, za_ref, zb_ref, lng_ref, lnb_ref,
             ws_ref, wst_ref, bsb_ref, dz_ref, dh_ref, dws_ref, dbs_ref, dlng_ref, dlnb_ref,
             du_s, dv_s, ya_s, dbs_acc):
        i = pl.program_id(0)

        @pl.when(i == 0)
        def _():
            for r in (dws_ref, dlng_ref, dlnb_ref, dbs_acc):
                r[...] = jnp.zeros_like(r)

        dm = _dot_nt(dx_ref[...].astype(BF16), wo_ref[...])
        sa = _sigmoid(za_ref[...].astype(F32))
        sb = _sigmoid(zb_ref[...].astype(F32))
        zg = zg_ref[...].astype(F32)
        gg, tg = _gelu(zg)
        hs = h0_ref[...] + h1_ref[...]
        dyb = dm * sb
        dya = dm * sa
        dh_ref[...] = dyb * gg
        dz_ref[:, 2 * D:3 * D] = jnp.zeros((tm, D), BF16)
        dz_ref[:, 3 * D:4 * D] = (dyb * hs * _gelu_grad(zg, tg)).astype(BF16)
        dz_ref[:, 5 * D:6 * D] = (dm * (hs * gg) * (sb * (1.0 - sb))).astype(BF16)

        zu, zv, u, tu, tv, xh, rstd, vb = _gmlp_values(zu_ref, zv_ref, lng_ref, lnb_ref)
        for c in range(tm // HD):
            rs = slice(c * HD, (c + 1) * HD)
            for g in range(HEADS):
                cs = slice(g * HD, (g + 1) * HD)
                vblk = vb[rs, cs]
                mixed = _dot(ws_ref[g], vblk) + bsb_ref[g]
                ya_s[rs, cs] = u[rs, cs] * mixed
                du_s[rs, cs] = dya[rs, cs] * mixed
                dmx = dya[rs, cs] * u[rs, cs]
                dbs_acc[g] += dmx
                dmxb = dmx.astype(BF16)
                dws_ref[g] += _dot_nt(dmxb, vblk)
                dv_s[rs, cs] = _dot(wst_ref[g], dmxb)
        dz_ref[:, 4 * D:5 * D] = (dm * ya_s[...] * (sa * (1.0 - sa))).astype(BF16)
        dv = dv_s[...]
        _add_rows128(dlng_ref, jnp.sum(dv * xh, axis=0, keepdims=True))
        _add_rows128(dlnb_ref, jnp.sum(dv, axis=0, keepdims=True))
        dxh = dv * lng_ref[...]
        dgv = rstd * (dxh - jnp.mean(dxh, axis=-1, keepdims=True)
                      - xh * jnp.mean(dxh * xh, axis=-1, keepdims=True))
        dz_ref[:, 0:D] = (du_s[...] * _gelu_grad(zu, tu)).astype(BF16)
        dz_ref[:, D:2 * D] = (dgv * _gelu_grad(zv, tv)).astype(BF16)

        @pl.when(i == nt - 1)
        def _():
            for g in range(HEADS):
                dbs_ref[g:g + 1, :] = jnp.sum(dbs_acc[g].T, axis=0, keepdims=True)

    tile = pl.BlockSpec((tm, D), lambda i: (i, 0))
    wspec = _full((HEADS, HD, HD))
    body, dep_specs, deps = _behind(after, body, 14)
    return pl.pallas_call(
        body, name=f"mixer_bwd_l{layer}", grid=(nt,),
        in_specs=[tile, _full((D, D)), tile, tile]
        + [pl.BlockSpec((tm, D), lambda i, c=c: (i, c)) for c in (0, 1, 3, 4, 5)]
        + [_full((1, D)), _full((1, D)), wspec, wspec, wspec] + dep_specs,
        out_specs=[pl.BlockSpec((tm, N_IN), lambda i: (i, 0)), tile, wspec, _full((HEADS, HD)),
                   _full((LANE_ROWS, HD)), _full((LANE_ROWS, HD))],
        out_shape=[SDS((s, N_IN), BF16), SDS((s, D), F32), SDS((HEADS, HD, HD), F32), SDS((HEADS, HD), F32),
                   SDS((LANE_ROWS, HD), F32), SDS((LANE_ROWS, HD), F32)],
        scratch_shapes=[pltpu.VMEM((tm, D), F32)] * 3 + [pltpu.VMEM((HEADS, HD, HD), F32)],
        compiler_params=_cp("arbitrary"))(dx1, wo, h0, h1, z, z, z, z, z, lng, lnb, ws, wst, bsb, *deps)


def _lru_gates_bwd(xcb, gates, h0, h1, g0, g1, wr, wi, lam, layer, tm, after=()):
    s = xcb.shape[0]
    nt = s // tm

    def body(xc_ref, r0_ref, i0_ref, r1_ref, i1_ref, h0p_ref, h0_ref, h1_ref, h1n_ref, g0_ref, g1_ref,
             wr_ref, wi_ref, lam_ref, dxc_ref, dwr_ref, dwi_ref, dbr_ref, dbi_ref, dlam_ref):
        i = pl.program_id(0)
        fp, fn = _halo_flags(nt)

        @pl.when(i == 0)
        def _():
            for r in (dwr_ref, dwi_ref, dbr_ref, dbi_ref, dlam_ref):
                r[...] = jnp.zeros_like(r)

        xb = xc_ref[...]
        xc = xb.astype(F32)
        zeros8 = jnp.zeros((8, D), F32)
        h_prev = _taps(h0p_ref[...] * fp, h0_ref[...], zeros8, tm)[1]
        h_next = _taps(zeros8, h1_ref[...], h1n_ref[...] * fn, tm)[3]
        dxc = jnp.zeros((tm, D), F32)
        for d, (g_ref, hsh, r_ref, i_ref) in enumerate(((g0_ref, h_prev, r0_ref, i0_ref),
                                                        (g1_ref, h_next, r1_ref, i1_ref))):
            sp = _softplus(-lam_ref[d:d + 1, :])
            r = r_ref[...].astype(F32)
            ig = i_ref[...].astype(F32)
            a, q = _lru_decay(r, sp)
            rmult = jnp.where(q > 0.0, lax.rsqrt(jnp.where(q > 0.0, q, 1.0)), 0.0)
            mult = q * rmult
            db = g_ref[...]
            da = db * hsh
            dmult = db * (ig * xc)
            di = db * (mult * xc)
            dxc = dxc + db * (mult * ig)
            dla = da * a - dmult * (a * a * rmult)
            dsp_dlam = -_sigmoid(-lam_ref[d:d + 1, :])
            _add_rows128(dlam_ref, jnp.sum(dla * r, axis=0, keepdims=True) * ((-LRU_C) * dsp_dlam), d * LANE_ROWS)
            dpr = dla * sp * (-LRU_C) * (r * (1.0 - r))
            dpi = di * (ig * (1.0 - ig))
            _add_rows128(dbr_ref, jnp.sum(dpr, axis=0, keepdims=True), d * LANE_ROWS)
            _add_rows128(dbi_ref, jnp.sum(dpi, axis=0, keepdims=True), d * LANE_ROWS)
            dprb = dpr.astype(BF16)
            dpib = dpi.astype(BF16)
            parts = []
            for h in range(HEADS):
                cs = slice(h * HD, (h + 1) * HD)
                dwr_ref[d, h] += _dot_tn(xb[:, cs], dprb[:, cs])
                dwi_ref[d, h] += _dot_tn(xb[:, cs], dpib[:, cs])
                parts.append(_dot_nt(dprb[:, cs], wr_ref[d, h]) + _dot_nt(dpib[:, cs], wi_ref[d, h]))
            dxc = dxc + jnp.concatenate(parts, axis=1)
        dxc_ref[...] = dxc.astype(BF16)

    tile = pl.BlockSpec((tm, D), lambda i: (i, 0))
    hp, hc, hn = _halo_specs(tm, s, 0)
    wspec = _full((2, HEADS, HD, HD))
    vspec = _full((2 * LANE_ROWS, HD))
    body, dep_specs, deps = _behind(after, body, 14)
    return pl.pallas_call(
        body, name=f"lru_gates_bwd_l{layer}", grid=(nt,),
        in_specs=[tile] * 5 + [hp, hc, hc, hn, tile, tile, wspec, wspec, _full((2, D))] + dep_specs,
        out_specs=[tile, wspec, wspec, vspec, vspec, vspec],
        out_shape=[SDS((s, D), BF16), SDS((2, HEADS, HD, HD), F32), SDS((2, HEADS, HD, HD), F32)]
        + [SDS((2 * LANE_ROWS, HD), F32)] * 3,
        compiler_params=_cp("arbitrary"))(xcb, *gates, h0, h0, h1, h1, g0, g1, wr, wi, lam, *deps)


def _conv_bwd(dz, dxc, z, cw, layer, tm):
    s = z.shape[0]
    nt = s // tm

    def body(dz_in, dp_ref, dc_ref, dn_ref, zp_ref, zc_ref, zn_ref, cw_ref, dz_ref, dcw_ref, dcb_ref):
        del dz_in
        fp, fn = _halo_flags(nt)

        @pl.when(pl.program_id(0) == 0)
        def _():
            dcw_ref[...] = jnp.zeros_like(dcw_ref)
            dcb_ref[...] = jnp.zeros_like(dcb_ref)

        dxc_halo = _halo_load(dp_ref, dc_ref, dn_ref, fp, fn)
        dxc = dxc_halo[1]
        dm2, dm1, _, dp1, _ = _taps(*dxc_halo, tm)
        dz_ref[...] = (cw_ref[0:1, :] * dp1 + cw_ref[1:2, :] * dxc + cw_ref[2:3, :] * dm1
                       + cw_ref[3:4, :] * dm2).astype(BF16)
        _, zm1, z0, zp1, zp2 = _taps(*_halo_load(zp_ref, zc_ref, zn_ref, fp, fn), tm)
        for k, zt in enumerate((zm1, z0, zp1, zp2)):
            _add_rows128(dcw_ref, jnp.sum(dxc * zt, axis=0, keepdims=True), k * LANE_ROWS)
        _add_rows128(dcb_ref, jnp.sum(dxc, axis=0, keepdims=True))

    return pl.pallas_call(
        body, name=f"conv_bwd_l{layer}", grid=(nt,),
        in_specs=[pl.BlockSpec(memory_space=pl.ANY), *_halo_specs(tm, s, 0, 16), *_halo_specs(tm, s, 2, 16),
                  _full((4, D))],
        out_specs=[pl.BlockSpec((tm, D), lambda i: (i, 2)), _full((4 * LANE_ROWS, HD)), _full((LANE_ROWS, HD))],
        out_shape=[SDS((s, N_IN), BF16), SDS((4 * LANE_ROWS, HD), F32), SDS((LANE_ROWS, HD), F32)],
        input_output_aliases={0: 0},
        compiler_params=_cp("arbitrary"))(dz, dxc, dxc, dxc, z, z, z, cw)


def _me():
    return lax.axis_index("x"), lax.axis_index("y"), lax.axis_index("c")


def _peer(m):
    x, y, c = _me()
    px = 1 - x if m & 4 else x
    py = 1 - y if m & 2 else y
    pc = 1 - c if m & 1 else c
    return (px, py, pc), 4 * px + 2 * py + pc


_ANY = pl.BlockSpec(memory_space=pl.ANY)
_EXCHANGE_SEMS = [pltpu.SemaphoreType.DMA((N_DEV - 1,)), pltpu.SemaphoreType.DMA((N_DEV - 1,)), pltpu.SemaphoreType.DMA(())]


def _all_gather(v, after, name):
    def body(v_ref, after_ref, o_ref, send_sems, recv_sems, local_sem):
        del after_ref
        x, y, c = _me()
        me = 4 * x + 2 * y + c
        local = pltpu.make_async_copy(v_ref, o_ref.at[me], local_sem)
        local.start()
        sends = []
        for m in range(1, N_DEV):
            dev, _ = _peer(m)
            cp = pltpu.make_async_remote_copy(v_ref, o_ref.at[me], send_sems.at[m - 1], recv_sems.at[m - 1],
                                              device_id=dev, device_id_type=pl.DeviceIdType.MESH)
            cp.start()
            sends.append(cp)
        for m in range(1, N_DEV):
            dev, blk = _peer(m)
            pltpu.make_async_remote_copy(v_ref, o_ref.at[blk], send_sems.at[m - 1], recv_sems.at[m - 1],
                                         device_id=dev, device_id_type=pl.DeviceIdType.MESH).wait_recv()
        for cp in sends:
            cp.wait_send()
        local.wait()

    return pl.pallas_call(
        body, name=name, in_specs=[_ANY, _ANY], out_specs=_ANY,
        out_shape=SDS((N_DEV,) + v.shape, v.dtype), scratch_shapes=_EXCHANGE_SEMS)(v, after)


_HBM = pl.BlockSpec(memory_space=pltpu.HBM)
_SEM = pl.BlockSpec(memory_space=pltpu.SEMAPHORE)
_EFFECT = pltpu.CompilerParams(has_side_effects=pltpu.SideEffectType.DATAFLOW_SIDE_EFFECTING)
_PEER_SEMS = pltpu.SemaphoreType.DMA((N_DEV - 1,))


def _in_hbm(a):
    return pltpu.with_memory_space_constraint(a, pltpu.HBM)


def _remote(src, dst, send_sems, recv_sems, m):
    dev, _ = _peer(m)
    return pltpu.make_async_remote_copy(src, dst, send_sems.at[m - 1], recv_sems.at[m - 1],
                                        device_id=dev, device_id_type=pl.DeviceIdType.MESH)


def _gather_start(lands, after, name):
    n = len(lands)

    def body(*refs):
        land = refs[:n]
        sems = refs[n + 1:3 * n + 1]
        token = refs[-1]
        x, y, c = _me()
        me = 4 * x + 2 * y + c
        for t in range(n):
            for m in range(1, N_DEV):
                _remote(land[t].at[me], land[t].at[me], sems[2 * t], sems[2 * t + 1], m).start()
        token[...] = jnp.zeros_like(token)

    res = pl.pallas_call(
        body, name=name, in_specs=[_HBM] * n + [_ANY],
        out_specs=[_SEM] * (2 * n) + [_HBM] * n + [pl.BlockSpec(memory_space=pltpu.VMEM)],
        out_shape=[_PEER_SEMS] * (2 * n) + [pltpu.HBM(a.shape, a.dtype) for a in lands] + [SDS((8, 128), F32)],
        input_output_aliases={t: 2 * n + t for t in range(n)},
        compiler_params=_EFFECT)(*[_in_hbm(a) for a in lands], after)
    return [(res[2 * t], res[2 * t + 1], res[2 * n + t]) for t in range(n)], res[-1]


def _gather_wait(handle, after, name):
    send_sems, recv_sems, land = handle

    def body(land_ref, ssem, rsem, after_ref, out_ref):
        del after_ref, out_ref
        x, y, c = _me()
        me = 4 * x + 2 * y + c
        for m in range(1, N_DEV):
            _, blk = _peer(m)
            cp = _remote(land_ref.at[me], land_ref.at[blk], ssem, rsem, m)
            cp.wait_send()
            cp.wait_recv()

    return pl.pallas_call(
        body, name=name, in_specs=[_HBM, _SEM, _SEM, _ANY], out_specs=_HBM,
        out_shape=pltpu.HBM(land.shape, land.dtype), input_output_aliases={0: 0},
        compiler_params=_EFFECT)(land, send_sems, recv_sems, after)


FIRST_STAGE = (1, 2, 4, 6)
RELAYED = (2, 4, 6)
OTHER_CORE = 1


def _stage_copy(src, dst, send_sems, recv_sems, k, m):
    dev, _ = _peer(m)
    return pltpu.make_async_remote_copy(src, dst, send_sems.at[k], recv_sems.at[k],
                                        device_id=dev, device_id_type=pl.DeviceIdType.MESH)


def _gather2_start(lands, after, name):
    n = len(lands)

    def body(*refs):
        land = refs[:n]
        sems = refs[n + 1:3 * n + 1]
        token = refs[-1]
        x, y, c = _me()
        me = 4 * x + 2 * y + c
        for t in range(n):
            for k, m in enumerate(FIRST_STAGE):
                _stage_copy(land[t].at[me], land[t].at[me], sems[2 * t], sems[2 * t + 1], k, m).start()
        token[...] = jnp.zeros_like(token)

    stage_sems = pltpu.SemaphoreType.DMA((len(FIRST_STAGE),))
    res = pl.pallas_call(
        body, name=name, in_specs=[_HBM] * n + [_ANY],
        out_specs=[_SEM] * (2 * n) + [_HBM] * n + [pl.BlockSpec(memory_space=pltpu.VMEM)],
        out_shape=[stage_sems] * (2 * n) + [pltpu.HBM(a.shape, a.dtype) for a in lands] + [SDS((8, 128), F32)],
        input_output_aliases={t: 2 * n + t for t in range(n)},
        compiler_params=_EFFECT)(*[_in_hbm(a) for a in lands], after)
    return [(res[2 * t], res[2 * t + 1], res[2 * n + t]) for t in range(n)], res[-1]


def _gather2_relay(handles, after, name):
    n = len(handles)

    def body(*refs):
        land, send1, recv1 = refs[:n], refs[n:2 * n], refs[2 * n:3 * n]
        sems = refs[3 * n + 1:5 * n + 1]
        token = refs[-1]
        x, y, c = _me()
        me = 4 * x + 2 * y + c
        for t in range(n):
            for j, m in enumerate(RELAYED):
                _, blk = _peer(m)
                _stage_copy(land[t].at[me], land[t].at[blk], send1[t], recv1[t], 1 + j, m).wait_recv()
                _stage_copy(land[t].at[blk], land[t].at[blk], sems[2 * t], sems[2 * t + 1], j, OTHER_CORE).start()
        token[...] = jnp.zeros_like(token)

    relay_sems = pltpu.SemaphoreType.DMA((len(RELAYED),))
    lands = [h[2] for h in handles]
    res = pl.pallas_call(
        body, name=name, in_specs=[_HBM] * n + [_SEM] * (2 * n) + [_ANY],
        out_specs=[_SEM] * (2 * n) + [_HBM] * n + [pl.BlockSpec(memory_space=pltpu.VMEM)],
        out_shape=[relay_sems] * (2 * n) + [pltpu.HBM(a.shape, a.dtype) for a in lands] + [SDS((8, 128), F32)],
        input_output_aliases={t: 2 * n + t for t in range(n)},
        compiler_params=_EFFECT)(*lands, *[h[0] for h in handles], *[h[1] for h in handles], after)
    return [(h[0], h[1], res[2 * t], res[2 * t + 1], res[2 * n + t]) for t, h in enumerate(handles)], res[-1]


def _gather2_wait(handle, after, name):
    send1, recv1, send2, recv2, land = handle

    def body(land_ref, s1, r1, s2, r2, after_ref, out_ref):
        del after_ref, out_ref
        x, y, c = _me()
        me = 4 * x + 2 * y + c
        _, other = _peer(OTHER_CORE)
        _stage_copy(land_ref.at[me], land_ref.at[other], s1, r1, 0, OTHER_CORE).wait_recv()
        for k, m in enumerate(FIRST_STAGE):
            _stage_copy(land_ref.at[me], land_ref.at[me], s1, r1, k, m).wait_send()
        for j, m in enumerate(RELAYED):
            _, mine = _peer(m)
            _, theirs = _peer(m ^ OTHER_CORE)
            _stage_copy(land_ref.at[mine], land_ref.at[mine], s2, r2, j, OTHER_CORE).wait_send()
            _stage_copy(land_ref.at[mine], land_ref.at[theirs], s2, r2, j, OTHER_CORE).wait_recv()

    return pl.pallas_call(
        body, name=name, in_specs=[_HBM] + [_SEM] * 4 + [_ANY], out_specs=_HBM,
        out_shape=pltpu.HBM(land.shape, land.dtype), input_output_aliases={0: 0},
        compiler_params=_EFFECT)(land, send1, recv1, send2, recv2, after)


def _exchange_start(ps, name):
    n = len(ps)

    def body(*refs):
        p = refs[:n]
        got = refs[n:2 * n]
        sems = refs[2 * n:5 * n]
        token = refs[-1]
        x, y, c = _me()
        me = 4 * x + 2 * y + c
        for t in range(n):
            pltpu.make_async_copy(p[t].at[me], got[t].at[me], sems[3 * t + 2]).start()
            for m in range(1, N_DEV):
                _, blk = _peer(m)
                _remote(p[t].at[blk], got[t].at[me], sems[3 * t], sems[3 * t + 1], m).start()
        token[...] = jnp.zeros_like(token)

    res = pl.pallas_call(
        body, name=name, in_specs=[_HBM] * (2 * n),
        out_specs=[_SEM] * (3 * n) + [_HBM] * (2 * n) + [pl.BlockSpec(memory_space=pltpu.VMEM)],
        out_shape=[_PEER_SEMS, _PEER_SEMS, pltpu.SemaphoreType.DMA(())] * n
        + [pltpu.HBM(a.shape, a.dtype) for a in ps] * 2 + [SDS((8, 128), F32)],
        input_output_aliases={t: 3 * n + t for t in range(2 * n)},
        compiler_params=_EFFECT)(*[_in_hbm(a) for a in ps], *[_in_hbm(lax.empty(a.shape, a.dtype)) for a in ps])
    return [(res[3 * t], res[3 * t + 1], res[3 * t + 2], res[3 * n + t], res[4 * n + t]) for t in range(n)], res[-1]


def _exchange_wait(handle, after, name):
    send_sems, recv_sems, local_sem, p, got = handle

    def body(p_ref, got_ref, ssem, rsem, lsem, after_ref, p_out, got_out):
        del after_ref, p_out, got_out
        x, y, c = _me()
        me = 4 * x + 2 * y + c
        pltpu.make_async_copy(p_ref.at[me], got_ref.at[me], lsem).wait()
        for m in range(1, N_DEV):
            _, blk = _peer(m)
            cp = _remote(p_ref.at[blk], got_ref.at[blk], ssem, rsem, m)
            cp.wait_send()
            cp.wait_recv()

    return pl.pallas_call(
        body, name=name, in_specs=[_HBM, _HBM, _SEM, _SEM, _SEM, _ANY], out_specs=[_HBM, _HBM],
        out_shape=[pltpu.HBM(p.shape, p.dtype), pltpu.HBM(got.shape, got.dtype)],
        input_output_aliases={0: 0, 1: 1}, compiler_params=_EFFECT)(p, got, send_sems, recv_sems, local_sem, after)[1]


CHIPS = (0, 2, 4, 6)


def _pairs_start(p, name):
    def body(p_ref, pair_ref, ssem, rsem, p_out, pair_out, token):
        del p_out, pair_out
        for k, chip in enumerate(CHIPS):
            _, blk = _peer(chip ^ OTHER_CORE)
            _stage_copy(p_ref.at[blk], pair_ref.at[k], ssem, rsem, k, OTHER_CORE).start()
        token[...] = jnp.zeros_like(token)

    sems = pltpu.SemaphoreType.DMA((len(CHIPS),))
    pair = lax.empty((len(CHIPS),) + p.shape[1:], p.dtype)
    res = pl.pallas_call(
        body, name=name, in_specs=[_HBM] * 2,
        out_specs=[_SEM] * 2 + [_HBM] * 2 + [pl.BlockSpec(memory_space=pltpu.VMEM)],
        out_shape=[sems, sems, pltpu.HBM(p.shape, p.dtype), pltpu.HBM(pair.shape, pair.dtype), SDS((8, 128), F32)],
        input_output_aliases={0: 2, 1: 3}, compiler_params=_EFFECT)(_in_hbm(p), _in_hbm(pair))
    return res[:4], res[4]


def _pairs_wait(handle, after, name):
    send_sems, recv_sems, p, pair = handle

    def body(p_ref, pair_ref, ssem, rsem, after_ref, p_out, pair_out):
        del after_ref, p_out, pair_out
        for k, chip in enumerate(CHIPS):
            _, blk = _peer(chip ^ OTHER_CORE)
            cp = _stage_copy(p_ref.at[blk], pair_ref.at[k], ssem, rsem, k, OTHER_CORE)
            cp.wait_send()
            cp.wait_recv()

    return pl.pallas_call(
        body, name=name, in_specs=[_HBM, _HBM, _SEM, _SEM, _ANY], out_specs=[_HBM, _HBM],
        out_shape=[pltpu.HBM(p.shape, p.dtype), pltpu.HBM(pair.shape, pair.dtype)],
        input_output_aliases={0: 0, 1: 1}, compiler_params=_EFFECT)(p, pair, send_sems, recv_sems, after)


def _sum_pairs(p, pair, me1, name):
    _, r, c = pair.shape
    tr = _row_tile(r)

    def body(me_ref, p_ref, pair_ref, o_ref):
        del me_ref
        o_ref[...] = (p_ref[...].astype(F32) + pair_ref[...].astype(F32)).astype(o_ref.dtype)

    def mine(k, i, me):
        chip = 2 * k
        return (jnp.bitwise_xor(me[0], chip), i, 0)

    assert CHIPS == tuple(2 * k for k in range(len(CHIPS)))
    blk = pl.BlockSpec((None, tr, c), lambda k, i, me: (k, i, 0))
    return pl.pallas_call(
        body, name=name,
        grid_spec=pltpu.PrefetchScalarGridSpec(
            num_scalar_prefetch=1, grid=(len(CHIPS), r // tr),
            in_specs=[pl.BlockSpec((None, tr, c), mine), blk], out_specs=blk),
        out_shape=SDS(pair.shape, pair.dtype), compiler_params=_cp("parallel", "parallel"))(me1, p, pair)


def _chips_start(q, name):
    def body(q_ref, got_ref, ssem, rsem, lsem, q_out, got_out, token):
        del q_out, got_out
        pltpu.make_async_copy(q_ref.at[0], got_ref.at[0], lsem).start()
        for k, chip in enumerate(CHIPS[1:]):
            _stage_copy(q_ref.at[k + 1], got_ref.at[k + 1], ssem, rsem, k, chip).start()
        token[...] = jnp.zeros_like(token)

    sems = pltpu.SemaphoreType.DMA((len(CHIPS) - 1,))
    res = pl.pallas_call(
        body, name=name, in_specs=[_HBM] * 2,
        out_specs=[_SEM] * 3 + [_HBM] * 2 + [pl.BlockSpec(memory_space=pltpu.VMEM)],
        out_shape=[sems, sems, pltpu.SemaphoreType.DMA(()), pltpu.HBM(q.shape, q.dtype), pltpu.HBM(q.shape, q.dtype),
                   SDS((8, 128), F32)],
        input_output_aliases={0: 3, 1: 4}, compiler_params=_EFFECT)(_in_hbm(q), _in_hbm(lax.empty(q.shape, q.dtype)))
    return res[:5], res[5]


def _chips_wait(handle, after, name):
    send_sems, recv_sems, local_sem, q, got = handle

    def body(q_ref, got_ref, ssem, rsem, lsem, after_ref, q_out, got_out):
        del after_ref, q_out, got_out
        pltpu.make_async_copy(q_ref.at[0], got_ref.at[0], lsem).wait()
        for k, chip in enumerate(CHIPS[1:]):
            cp = _stage_copy(q_ref.at[k + 1], got_ref.at[k + 1], ssem, rsem, k, chip)
            cp.wait_send()
            cp.wait_recv()

    return pl.pallas_call(
        body, name=name, in_specs=[_HBM, _HBM, _SEM, _SEM, _SEM, _ANY], out_specs=[_HBM, _HBM],
        out_shape=[pltpu.HBM(q.shape, q.dtype), pltpu.HBM(got.shape, got.dtype)],
        input_output_aliases={0: 0, 1: 1}, compiler_params=_EFFECT)(q, got, send_sems, recv_sems, local_sem, after)[1]


def _cast_into_slot(w, layer, me1, name):
    _, r, c = w.shape
    tr = next(t for t in (512, 352, r) if r % t == 0)

    def body(me_ref, w_ref, o_ref):
        del me_ref
        o_ref[...] = w_ref[...].astype(BF16)

    return pl.pallas_call(
        body, name=name,
        grid_spec=pltpu.PrefetchScalarGridSpec(
            num_scalar_prefetch=1, grid=(r // tr,),
            in_specs=[pl.BlockSpec((None, tr, c), lambda i, me: (layer, i, 0))],
            out_specs=pl.BlockSpec((None, tr, c), lambda i, me: (me[0], i, 0))),
        out_shape=SDS((N_DEV, r, c), BF16), compiler_params=_cp("arbitrary"))(me1, w)


def _cast_all_into_slots(ws, layers, me1, after, name):
    n = len(ws)

    def body(me_ref, *refs):
        del me_ref
        for w_ref, o_ref in zip(refs[:n], refs[n + 1:]):
            o_ref[...] = w_ref[...].astype(BF16)

    return pl.pallas_call(
        body, name=name,
        grid_spec=pltpu.PrefetchScalarGridSpec(
            num_scalar_prefetch=1, grid=(1,),
            in_specs=[pl.BlockSpec((None,) + a.shape[1:], lambda i, me, l=l: (l, 0, 0)) for a, l in zip(ws, layers)]
            + [_ANY],
            out_specs=[pl.BlockSpec((None,) + a.shape[1:], lambda i, me: (me[0], 0, 0)) for a in ws]),
        out_shape=[SDS((N_DEV,) + a.shape[1:], BF16) for a in ws],
        compiler_params=_cp("arbitrary"))(me1, *ws, after)


def _sum8_into_slot(p, me1, name):
    _, r, c = p.shape

    def body(me_ref, p_ref, o_ref):
        del me_ref
        acc = p_ref[0]
        for k in range(1, N_DEV):
            acc = acc + p_ref[k]
        o_ref[...] = acc

    return pl.pallas_call(
        body, name=name,
        grid_spec=pltpu.PrefetchScalarGridSpec(
            num_scalar_prefetch=1, grid=(1,),
            in_specs=[pl.BlockSpec(p.shape, lambda i, me: (0, 0, 0))],
            out_specs=pl.BlockSpec((None, r, c), lambda i, me: (me[0], 0, 0))),
        out_shape=SDS(p.shape, F32), compiler_params=_cp("arbitrary"))(me1, p)


def _adamw(w, g, m, v):
    m = ADAM_B1 * m + (1.0 - ADAM_B1) * g
    v = ADAM_B2 * v + (1.0 - ADAM_B2) * (g * g)
    m_hat = m / (1.0 - ADAM_B1 ** ADAM_STEP)
    v_hat = v / (1.0 - ADAM_B2 ** ADAM_STEP)
    delta = -ADAM_LR * (m_hat / (jnp.sqrt(v_hat) + ADAM_EPS) + ADAM_WD * w)
    return delta, m, v


def _adam_tile(p_ref, w_ref, m_ref, v_ref, g_ref, d_ref, nm_ref, nv_ref):
    g = p_ref[0].astype(F32)
    for k in range(1, p_ref.shape[0]):
        g = g + p_ref[k].astype(F32)
    delta, nm, nv = _adamw(w_ref[...], g, m_ref[...], v_ref[...])
    g_ref[...] = g
    d_ref[...] = delta
    nm_ref[...] = nm
    nv_ref[...] = nv


class _AdamJob:
    def __init__(self, parts, w, m, v, layer, prev):
        self.args = [parts, w, m, v] + list(prev or ())
        self.layer, self.results = layer, None


def _carry(jobs, steps, body, n_in, n_out):
    in_specs, args, out_specs, out_shapes, aliases, n_prevs = [], [], [], [], {}, []
    for j, job in enumerate(jobs):
        _, r, c = job.args[0].shape
        nr = next(n for n in range(steps, 0, -1) if steps % n == 0 and r % (16 * n) == 0)
        nc = steps // nr
        assert c % (128 * nc) == 0
        tile = (r // nr, c // nc)
        blk = pl.BlockSpec((None,) + tile, lambda i, layer=job.layer, nc=nc: (layer, i // nc, i % nc))
        n_prev = len(job.args) - 4
        aliases.update({n_in + len(args) + 4 + k: n_out + 4 * j + k for k in range(n_prev)})
        in_specs += [pl.BlockSpec(job.args[0].shape[:1] + tile, lambda i, nc=nc: (0, i // nc, i % nc)), blk, blk, blk]
        in_specs += [_ANY] * n_prev
        args += job.args
        out_specs += [blk] * 4
        out_shapes += [SDS(job.args[1].shape, F32)] * 4
        n_prevs.append(n_prev)

    def carrying(*refs):
        ins, outs = refs[:n_in + len(args)], refs[n_in + len(args):]
        body(*ins[:n_in], *outs[:n_out])
        k = n_in
        for j, n_prev in enumerate(n_prevs):
            _adam_tile(*ins[k:k + 4], *outs[n_out + 4 * j:n_out + 4 * j + 4])
            k += 4 + n_prev

    return carrying, in_specs, args, out_specs, out_shapes, aliases


def _adam_shard(parts, w, m, v, layer, prev, name):
    n, r, c = parts.shape
    tr = next(t for t in (512, 352, r) if r % t == 0)
    n_prev = 0 if prev is None else 4

    def body(*refs):
        _adam_tile(*refs[:4], *refs[4 + n_prev:])

    blk = pl.BlockSpec((None, tr, c), lambda i: (layer, i, 0))
    return pl.pallas_call(
        body, name=name, grid=(r // tr,),
        in_specs=[pl.BlockSpec((n, tr, c), lambda i: (0, i, 0)), blk, blk, blk] + [_ANY] * n_prev,
        out_specs=[blk] * 4, out_shape=[SDS(w.shape, F32)] * 4,
        input_output_aliases={4 + k: k for k in range(n_prev)},
        compiler_params=_cp("parallel"))(parts, w, m, v, *(prev or ()))


SMALL_MATRICES = [("lru_w_r", 2048), ("lru_w_i", 2048), ("gmlp_w_s", 1024)]
SMALL_VECTORS = [("norm1_g", 8), ("gmlp_ln_g", 8), ("gmlp_ln_b", 8), ("gmlp_b_s", 8), ("conv_w", 32), ("conv_b", 8),
                 ("lru_b_r", 16), ("lru_b_i", 16), ("lru_lambda", 16), ("norm2_g", 8), ("final_g", 8)]
SMALL_VECTOR_ROW0 = sum(n for _, n in SMALL_MATRICES)
SMALL_VECTOR_BLOCK = 256
SMALL_ROWS = SMALL_VECTOR_ROW0 + SMALL_VECTOR_BLOCK
LOSS_ROW = SMALL_VECTOR_ROW0 + sum(n for _, n in SMALL_VECTORS)
assert LOSS_ROW + LANE_ROWS <= SMALL_ROWS


def _pack_small(small, name):
    rows, row0 = {}, 0
    for k, n in SMALL_MATRICES + SMALL_VECTORS + [("loss", LANE_ROWS)]:
        rows[k] = (row0, n)
        row0 += n
    names = [k for k in rows if k in small]

    def body(*refs):
        o_ref = refs[-1]
        o_ref[...] = jnp.zeros_like(o_ref)
        for k, ref in zip(names, refs):
            o_ref[pl.ds(*rows[k]), :] = ref[...]

    flat = pl.pallas_call(body, name=name, out_shape=SDS((SMALL_ROWS, HD), F32))(*[small[k] for k in names])
    return flat.reshape(N_DEV, SMALL_ROWS // N_DEV, HD)


def _adam_matrix(g0, g1, w, m, v, row0, name):
    _, rows, _ = w.shape
    tr = 512

    def body(g0_ref, g1_ref, w_ref, m_ref, v_ref, g_ref, d_ref, nm_ref, nv_ref):
        for l, src in enumerate((g0_ref, g1_ref)):
            g = src[...]
            delta, nm, nv = _adamw(w_ref[l], g, m_ref[l], v_ref[l])
            g_ref[l] = g
            d_ref[l] = delta
            nm_ref[l] = nm
            nv_ref[l] = nv

    gspec = pl.BlockSpec((tr, HD), lambda i: (row0 // tr + i, 0))
    blk = pl.BlockSpec((2, tr, HD), lambda i: (0, i, 0))
    return pl.pallas_call(body, name=name, grid=(rows // tr,), in_specs=[gspec, gspec] + [blk] * 3,
                          out_specs=[blk] * 4, out_shape=[SDS(w.shape, F32)] * 4,
                          compiler_params=_cp("parallel"))(g0, g1, w, m, v)


def _adam_vectors(g0, g1, dg1_parts, me1, ws, ms, vs):
    names = [k for k, _ in SMALL_VECTORS]
    n = len(names)

    def lanes(rows8):
        return jnp.concatenate([rows8[k:k + 1, :] for k in range(LANE_ROWS)], axis=1)

    def body(me_ref, g0_ref, g1_ref, dg1_ref, *refs):
        w_refs, m_refs, v_refs = refs[:n], refs[n:2 * n], refs[2 * n:3 * n]
        outs = refs[3 * n:]
        me = me_ref[0]
        g_refs = (g0_ref, g1_ref)

        def emit(i, idx, g):
            delta, nm, nv = _adamw(w_refs[i][idx], g, m_refs[i][idx], v_refs[i][idx])
            for j, val in enumerate((g, delta, nm, nv)):
                outs[4 * i + j][idx] = val

        off = 0
        for i, (name, rows) in enumerate(SMALL_VECTORS):
            for l in range(2):
                row = (slice(l, l + 1), slice(None))
                if name == "final_g":
                    if l == 1:
                        emit(i, (slice(0, 1), slice(None)), lanes(g1_ref[off:off + rows, :]))
                elif name == "norm1_g":
                    if l == 1:
                        emit(i, row, lanes(g0_ref[off:off + rows, :]))
                    else:
                        total = dg1_ref[0]
                        for k in range(1, N_DEV):
                            total = total + dg1_ref[k]
                        emit(i, row, lanes(total))
                elif name == "gmlp_b_s":
                    emit(i, (l,), g_refs[l][off:off + rows, :])
                elif rows == LANE_ROWS:
                    emit(i, row, lanes(g_refs[l][off:off + rows, :]))
                else:
                    for r in range(rows // LANE_ROWS):
                        emit(i, (l, slice(r, r + 1), slice(None)), g_refs[l][pl.ds(off + r * LANE_ROWS + me, 1), :])
            off += rows

    args = [ws[k] for k in names] + [ms[k] for k in names] + [vs[k] for k in names]
    gspec = pl.BlockSpec((SMALL_VECTOR_BLOCK, HD), lambda i, me: (SMALL_VECTOR_ROW0 // SMALL_VECTOR_BLOCK, 0))
    res = pl.pallas_call(
        body, name="adam_vectors",
        grid_spec=pltpu.PrefetchScalarGridSpec(
            num_scalar_prefetch=1, grid=(1,),
            in_specs=[gspec, gspec, _full(dg1_parts.shape)] + [_full(a.shape) for a in args],
            out_specs=[_full(ws[k].shape) for k in names for _ in range(4)]),
        out_shape=[SDS(ws[k].shape, F32) for k in names for _ in range(4)],
        compiler_params=_cp("arbitrary"))(me1, g0, g1, dg1_parts, *args)
    return {k: list(res[4 * i:4 * i + 4]) for i, k in enumerate(names)}


def _local_step(x, tgt, p, get_w, hook=lambda stage, layer, payload: None):
    s = x.shape[0]
    tm = _row_tile(s)
    wsb = p["gmlp_w_s"].astype(BF16)
    wstb = jnp.swapaxes(p["gmlp_w_s"], -1, -2).astype(BF16)
    bsb = jnp.broadcast_to(p["gmlp_b_s"][..., None], p["gmlp_w_s"].shape)
    wrb = p["lru_w_r"].astype(BF16)
    wib = p["lru_w_i"].astype(BF16)
    saved = []
    for l in range(2):
        win = get_w("w_in", l, x)
        z, h1 = _norm_inproj(x, p["norm1_g"][l][None], win, l, tm, after=[hook("pre_inproj", l, win)])
        a0, b0, a1, b1, xcb, *gates = _lru_gates_fwd(z, p["conv_w"][l], p["conv_b"][l][None], wrb[l], wib[l],
                                                     p["lru_b_r"][l], p["lru_b_i"][l], p["lru_lambda"][l], l, tm)
        h0, hr = _lru_scan(a0, b0, a1, b1, False, l)
        token = hook("pre_gmlp", l, h0)
        wout = get_w("w_out", l, h0 if token is None else token)
        x1, mg = _mixer_fwd(x, h0, hr, z, p["gmlp_ln_g"][l][None], p["gmlp_ln_b"][l][None], wsb[l], bsb[l], wout, l, tm)
        wfi = get_w("w_ffn_in", l, x1)
        wfo = get_w("w_ffn_out", l, x1)
        if l == 0:
            x2, ff, dff, h2 = _ffn_fwd(x1, p["norm2_g"][l][None], wfi, wfo, l, tm)
        else:
            dx, loss, dfg, ff, dff, h2 = _ffn_fwd(x1, p["norm2_g"][l][None], wfi, wfo, l, tm,
                                                  head=(p["final_g"][None], tgt))
        saved.append((x, z, h1, a0, a1, h0, hr, x1, mg, ff, dff, h2, win, wout, wfi, wfo, xcb, gates))
        x = x2
    for l in (1, 0):
        x0, z, h1, a0, a1, h0, hr, x1, mg, ff, dff, h2, win, wout, wfi, wfo, xcb, gates = saved[l]
        dgu, dx1, dg2 = _ffn_bwd(dx, wfo, dff, wfi.reshape(N_DEV, FF_BLK, D), x1, p["norm2_g"][l][None], l, tm)
        d_wfo = _mm_tn(ff, pl.BlockSpec((None, s, FF_BLK), lambda j: (j, 0, 0)), dx, _resident((s, D)),
                       4, (4, FF_BLK, D), pl.BlockSpec((None, FF_BLK, D), lambda j: (j, 0, 0)),
                       f"dw_ffn_out_l{l}", a_is_transposed=False)
        dgu8 = dgu.reshape(N_DEV, s, FF_BLK)
        d_wfi = _mm_tn(dgu8, pl.BlockSpec((None, s, FF_BLK), lambda j: (j, 0, 0)), h2, _resident((s, D)),
                       N_DEV, (N_DEV, FF_BLK, D), pl.BlockSpec((None, FF_BLK, D), lambda j: (j, 0, 0)),
                       f"dw_ffn_in_l{l}", a_is_transposed=False)
        d_wout = _mm_tn(mg, _resident((D, s)), dx1, pl.BlockSpec((s, D // 2), lambda j: (0, j)),
                        2, (D, D), pl.BlockSpec((D, D // 2), lambda j: (0, j)), f"dw_out_l{l}")
        token = hook("ffn_partials", l, dict(w_ffn_out=d_wfo.reshape(N_DEV, D_FF // N_DEV, D), w_ffn_in=d_wfi,
                                             w_out=d_wout.reshape(N_DEV, D // N_DEV, D)))
        pending = hook("mid_backward", l, dx1)
        dz, dh, dws, dbs, dlng, dlnb = _mixer_bwd(dx1, wout, h0, hr, z, p["gmlp_ln_g"][l][None], p["gmlp_ln_b"][l][None],
                                                  wsb[l], wstb[l], bsb[l], l, tm, after=[token])
        g1, g0 = _lru_scan(a1, dh, a0, dh, True, l)
        dxc, dwr, dwi, dbr, dbi, dlam = _lru_gates_bwd(
            xcb, gates, h0, hr, g0, g1, wrb[l], wib[l], p["lru_lambda"][l], l, tm, after=[pending])
        dz, dcw, dcb = _conv_bwd(dz, dxc, z, p["conv_w"][l], l, tm)
        small = dict(lru_w_r=dwr.reshape(-1, HD), lru_w_i=dwi.reshape(-1, HD), gmlp_w_s=dws.reshape(-1, HD),
                     gmlp_ln_g=dlng, gmlp_ln_b=dlnb, gmlp_b_s=dbs, conv_w=dcw, conv_b=dcb, lru_b_r=dbr,
                     lru_b_i=dbi, lru_lambda=dlam, norm2_g=dg2)
        if l == 1:
            small["final_g"] = dfg
            small["loss"] = jnp.broadcast_to(loss, (LANE_ROWS, HD))
        else:
            small["norm1_g"] = dg1
        started = hook("small_grads", l, small)
        d_win = _mm_tn(h1, _resident((D, s)), dz, pl.BlockSpec((s, IN_BLK), lambda j: (0, j)),
                       N_DEV, (N_DEV, D, IN_BLK), pl.BlockSpec((None, D, IN_BLK), lambda j: (j, 0, 0)),
                       f"dw_in_l{l}", after=started, jobs=hook("dw_in", l, started) or ())
        token = hook("mixer_partials", l, dict(w_in=d_win))
        dx, dg1 = _mm_nt_rms_bwd(
            dz, pl.BlockSpec((tm, N_IN), lambda i: (i, 0)),
            lambda r: [r[:, k * IN_BLK:(k + 1) * IN_BLK] for k in range(N_DEV)],
            win, False, x0, p["norm1_g"][l][None], dx1, f"inproj_bwd_dx_l{l}", tm,
            after=[token], jobs=hook("inproj_bwd_dx", l, token) or ())
    return loss, dx, dg1


_REPL = ["norm1_g", "gmlp_ln_g", "gmlp_ln_b", "gmlp_w_s", "gmlp_b_s", "conv_b", "lru_w_r", "lru_w_i", "norm2_g", "final_g"]
_LANE_SHARDED = ["conv_w", "lru_b_r", "lru_b_i", "lru_lambda"]
_BIG = ["w_in", "w_out", "w_ffn_in", "w_ffn_out"]
_ORDER = ["norm1_g", "w_in", "gmlp_ln_g", "gmlp_ln_b", "gmlp_w_s", "gmlp_b_s", "conv_w", "conv_b", "lru_w_r", "lru_b_r",
          "lru_w_i", "lru_b_i", "lru_lambda", "w_out", "norm2_g", "w_ffn_in", "w_ffn_out", "final_g"]


def kernel(x, norm1_g, w_in, gmlp_ln_g, gmlp_ln_b, gmlp_w_s, gmlp_b_s, conv_w, conv_b, lru_w_r, lru_b_r, lru_w_i, lru_b_i, lru_lambda, w_out, norm2_g, w_ffn_in, w_ffn_out, final_g, loss_target, m_norm1_g, m_w_in, m_gmlp_ln_g, m_gmlp_ln_b, m_gmlp_w_s, m_gmlp_b_s, m_conv_w, m_conv_b, m_lru_w_r, m_lru_b_r, m_lru_w_i, m_lru_b_i, m_lru_lambda, m_w_out, m_norm2_g, m_w_ffn_in, m_w_ffn_out, m_final_g, v_norm1_g, v_w_in, v_gmlp_ln_g, v_gmlp_ln_b, v_gmlp_w_s, v_gmlp_b_s, v_conv_w, v_conv_b, v_lru_w_r, v_lru_b_r, v_lru_w_i, v_lru_b_i, v_lru_lambda, v_w_out, v_norm2_g, v_w_ffn_in, v_w_ffn_out, v_final_g):
    w = dict(norm1_g=norm1_g, w_in=w_in, gmlp_ln_g=gmlp_ln_g, gmlp_ln_b=gmlp_ln_b, gmlp_w_s=gmlp_w_s, gmlp_b_s=gmlp_b_s,
             conv_w=conv_w, conv_b=conv_b, lru_w_r=lru_w_r, lru_b_r=lru_b_r, lru_w_i=lru_w_i, lru_b_i=lru_b_i,
             lru_lambda=lru_lambda, w_out=w_out, norm2_g=norm2_g, w_ffn_in=w_ffn_in, w_ffn_out=w_ffn_out, final_g=final_g)
    mom = dict(norm1_g=m_norm1_g, w_in=m_w_in, gmlp_ln_g=m_gmlp_ln_g, gmlp_ln_b=m_gmlp_ln_b, gmlp_w_s=m_gmlp_w_s,
               gmlp_b_s=m_gmlp_b_s, conv_w=m_conv_w, conv_b=m_conv_b, lru_w_r=m_lru_w_r, lru_b_r=m_lru_b_r,
               lru_w_i=m_lru_w_i, lru_b_i=m_lru_b_i, lru_lambda=m_lru_lambda, w_out=m_w_out, norm2_g=m_norm2_g,
               w_ffn_in=m_w_ffn_in, w_ffn_out=m_w_ffn_out, final_g=m_final_g)
    var = dict(norm1_g=v_norm1_g, w_in=v_w_in, gmlp_ln_g=v_gmlp_ln_g, gmlp_ln_b=v_gmlp_ln_b, gmlp_w_s=v_gmlp_w_s,
               gmlp_b_s=v_gmlp_b_s, conv_w=v_conv_w, conv_b=v_conv_b, lru_w_r=v_lru_w_r, lru_b_r=v_lru_b_r,
               lru_w_i=v_lru_w_i, lru_b_i=v_lru_b_i, lru_lambda=v_lru_lambda, w_out=v_w_out, norm2_g=v_norm2_g,
               w_ffn_in=v_w_ffn_in, w_ffn_out=v_w_ffn_out, final_g=v_final_g)
    for src in (w, mom, var):
        src["w_ffn_in"] = jnp.swapaxes(src["w_ffn_in"], 1, 2)
    xi, yi, ci = _me()
    me = 4 * xi + 2 * yi + ci

    lane_shapes = [w[k].shape for k in _LANE_SHARDED]
    lane_rows = sum(a[0] * a[1] for a in lane_shapes)
    packed = jnp.concatenate([w[k].reshape(-1, HD) for k in _LANE_SHARDED])
    packed = jnp.pad(packed, ((0, -lane_rows % 8), (0, 0)))

    me1 = jnp.reshape(me, (1,)).astype(jnp.int32)
    gathers = {}
    exchanges = {}
    views = dict(w_in=(N_DEV, D, IN_BLK), w_out=(D, D), w_ffn_in=(2, 4, FF_BLK, D), w_ffn_out=(4, FF_BLK, D))
    small_ex = {}
    small_ag = {}

    casts = {}

    def start_gather(names, l, after):
        lands = [casts[(k, l)] if (k, l) in casts else _cast_into_slot(w[k], l, me1, f"cast_{k}_l{l}") for k in names]
        started, tok = _gather2_start(lands, after, f"gather_start_{'_'.join(names)}_l{l}")
        gathers.update({(k, l): h for k, h in zip(names, started)})
        return tok

    def relay_gather(names, l, after):
        relayed, tok = _gather2_relay([gathers[(k, l)] for k in names], after, f"gather_relay_{'_'.join(names)}_l{l}")
        gathers.update({(k, l): h for k, h in zip(names, relayed)})
        return tok

    def get_w(k, l, after):
        return _gather2_wait(gathers[(k, l)], after, f"gather_wait_{k}_l{l}").reshape(views[k])

    carried = {("inproj_bwd_dx", 1): [("w_ffn_out", 1), ("w_ffn_in", 1), ("w_out", 1)], ("dw_in", 0): [("w_in", 1)],
               ("inproj_bwd_dx", 0): [("w_ffn_out", 0), ("w_ffn_in", 0), ("w_out", 0)]}
    adam = {}

    def adam_jobs(shards, after):
        for k, l in shards:
            got = _exchange_wait(exchanges[(k, l)], after, f"exchange_wait_{k}_l{l}")
            adam[k] = _AdamJob(got, w[k], mom[k], var[k], l, adam[k].results if k in adam else None)
        return [adam[k] for k, _ in shards]

    def hook(stage, l, payload):
        if stage in ("dw_in", "inproj_bwd_dx"):
            return adam_jobs(carried.get((stage, l), []), payload)
        if stage == "pre_inproj":
            tok = start_gather(_BIG[1:], l, payload)
            return start_gather(_BIG[:1], l + 1, tok) if l == 0 else tok
        if stage == "pre_gmlp":
            tok = relay_gather(_BIG[1:], l, payload)
            return relay_gather(_BIG[:1], l + 1, tok) if l == 0 else tok
        if stage == "small_grads":
            (small_ex[l],), tok = _exchange_start([_pack_small(payload, f"pack_small_l{l}")], f"exchange_start_small_l{l}")
            return tok
        if stage == "mid_backward":
            return reduce_small(l + 1, payload) if l == 0 else None
        if (stage, l) == ("mixer_partials", 0):
            pairs, tok = _pairs_start(payload["w_in"], "pairs_start_w_in_l0")
            p, pair = _pairs_wait(pairs, reduce_small(0, tok), "pairs_wait_w_in_l0")
            sums = _sum_pairs(p, pair, me1, "sum_pairs_w_in_l0")
            exchanges[("w_in", 0)], tok = _chips_start(sums, "chips_start_w_in_l0")
            return tok
        started, tok = _exchange_start(list(payload.values()), f"exchange_start_{'_'.join(payload)}_l{l}")
        exchanges.update({(k, l): h for k, h in zip(payload, started)})
        return tok

    def reduce_small(l, after):
        got = _exchange_wait(small_ex[l], after, f"exchange_wait_small_l{l}")
        mine = _sum8_into_slot(got, me1, f"sum_small_l{l}")
        (small_ag[l],), tok = _gather_start([mine], got, f"gather_start_small_l{l}")
        return tok

    land = lax.dynamic_update_slice(jnp.zeros((N_DEV,) + packed.shape, F32), packed[None], (me, 0, 0))
    token = start_gather(_BIG[:1], 0, x)
    (lanes_handle,), token = _gather_start([land], token, "gather_start_lanes")
    later = [(k, l) for l in range(2) for k in _BIG if (k, l) != ("w_in", 0)]
    casts.update(zip(later, _cast_all_into_slots([w[k] for k, _ in later], [l for _, l in later], me1, token,
                                                 "cast_later_weights")))
    token = relay_gather(_BIG[:1], 0, casts[later[0]])
    lanes = _gather_wait(lanes_handle, token, "gather_wait_lanes")
    params = {k: w[k] for k in _REPL}
    off = 0
    for k, shp in zip(_LANE_SHARDED, lane_shapes):
        n = shp[0] * shp[1]
        params[k] = jnp.swapaxes(lanes[:, off:off + n], 0, 1).reshape(shp[0], shp[1], D)
        off += n
    _, dx, dg1 = _local_step(x[0], loss_target[0], params, get_w, hook)

    out = {k: job.results for k, job in adam.items()}
    after = dx
    g_small = [_gather_wait(small_ag[l], after, f"gather_wait_small_l{l}").reshape(SMALL_ROWS, HD) for l in (0, 1)]
    row0 = 0
    for k, rows in SMALL_MATRICES:
        res = _adam_matrix(*g_small, *[src[k].reshape(2, rows, HD) for src in (w, mom, var)], row0, f"adam_{k}")
        out[k] = [a.reshape(w[k].shape) for a in res]
        after = res[3]
        row0 += rows
    got = _chips_wait(exchanges[("w_in", 0)], after, "chips_wait_w_in_l0")
    out["w_in"] = _adam_shard(got, w["w_in"], mom["w_in"], var["w_in"], 0, out["w_in"], "adam_w_in_l0")
    out["w_ffn_in"] = [jnp.swapaxes(a, 1, 2) for a in out["w_ffn_in"]]
    as_rows = lambda a: a.reshape(1, D) if a.ndim == 1 else a
    vec = _adam_vectors(*g_small, _all_gather(dg1, out["w_in"][3], "gather_norm1_grad"), me1,
                        *[{k: as_rows(src[k]) for k, _ in SMALL_VECTORS} for src in (w, mom, var)])
    out.update({k: [a.reshape(w[k].shape) for a in res] for k, res in vec.items()})

    return (g_small[1][LOSS_ROW, 0], dx[None], *[out[k][0] for k in _ORDER], *[out[k][1] for k in _ORDER],
            *[out[k][2] for k in _ORDER], *[out[k][3] for k in _ORDER])
```

```python
import jax
import jax.numpy as jnp
from jax import lax
from jax.experimental import pallas as pl
from jax.experimental.pallas import tpu as pltpu

F32 = jnp.float32
BF16 = jnp.bfloat16
SDS = jax.ShapeDtypeStruct

D = 1024
N_IN = 6 * D
D_FF = 2816
N_DEV = 8
IN_BLK = N_IN // N_DEV
FF_BLK = 2 * D_FF // N_DEV
HEADS = 8
HD = 128
EPS = 1e-6
LRU_C = 8.0

ADAM_LR = 0.001
ADAM_B1 = 0.9
ADAM_B2 = 0.999
ADAM_EPS = 1e-08
ADAM_WD = 0.01
ADAM_STEP = 10

VMEM_LIMIT = 60 * 2**20


def _cp(*sem, **kw):
    return pltpu.CompilerParams(dimension_semantics=sem, vmem_limit_bytes=VMEM_LIMIT, **kw)


def _row_tile(s):
    return 512 if s >= 1024 else s // 2


_GELU_C = 0.7978845608028654


def _gelu(x):
    t = jnp.tanh(_GELU_C * (x + 0.044715 * (x * x * x)))
    return 0.5 * x * (1.0 + t), t


def _gelu_grad(x, t):
    return 0.5 * (1.0 + t) + 0.5 * x * (1.0 - t * t) * (_GELU_C * (1.0 + 0.134145 * (x * x)))


def _sigmoid(x):
    return 0.5 + 0.5 * jnp.tanh(0.5 * x)


def _softplus(x):
    e = jnp.exp(-jnp.abs(x))
    w = 1.0 + e
    l1p = jnp.where(w == 1.0, e, jnp.log(w) * e / jnp.where(w == 1.0, 1.0, w - 1.0))
    return jnp.maximum(x, 0.0) + l1p


def _rms_fwd(x, g):
    r = lax.rsqrt(jnp.mean(x * x, axis=-1, keepdims=True) + EPS)
    return x * r * g


def _rms_bwd(x, g, dh):
    r = lax.rsqrt(jnp.mean(x * x, axis=-1, keepdims=True) + EPS)
    xh = x * r
    dxh = dh * g
    dx = r * (dxh - xh * jnp.mean(dxh * xh, axis=-1, keepdims=True))
    dg = jnp.sum(dh * xh, axis=0, keepdims=True)
    return dx, dg


LANE_ROWS = D // HD


def _add_rows128(ref, vec, row0=0):
    for i in range(vec.shape[0]):
        for k in range(LANE_ROWS):
            j = row0 + i * LANE_ROWS + k
            ref[j:j + 1, :] += vec[i:i + 1, k * HD:(k + 1) * HD]


def _dot(a, b):
    return jnp.dot(a, b, preferred_element_type=F32)


def _dot_nt(a, b):
    return lax.dot_general(a, b, (((1,), (1,)), ((), ())), preferred_element_type=F32)


def _dot_tn(a, b):
    return lax.dot_general(a, b, (((0,), (0,)), ((), ())), preferred_element_type=F32)


def _taps(prev, cur, nxt, tm):
    hr = prev.shape[0]
    ext = jnp.concatenate([prev, cur, nxt], axis=0)
    n = tm + 2 * hr
    sl = slice(hr, hr + tm)
    return (pltpu.roll(ext, 2, 0)[sl], pltpu.roll(ext, 1, 0)[sl], cur,
            pltpu.roll(ext, n - 1, 0)[sl], pltpu.roll(ext, n - 2, 0)[sl])


def _halo_specs(tm, s, col, rows=8):
    nb = s // rows
    r = tm // rows
    return (pl.BlockSpec((rows, D), lambda i: (jnp.maximum(i * r - 1, 0), col)),
            pl.BlockSpec((tm, D), lambda i: (i, col)),
            pl.BlockSpec((rows, D), lambda i: (jnp.minimum((i + 1) * r, nb - 1), col)))


def _halo_load(prev_ref, cur_ref, next_ref, fp, fn):
    return prev_ref[...].astype(F32) * fp, cur_ref[...].astype(F32), next_ref[...].astype(F32) * fn


def _halo_flags(nt):
    i = pl.program_id(0)
    return (i > 0).astype(F32), (i < nt - 1).astype(F32)


def _full(shape):
    nd = len(shape)
    return pl.BlockSpec(shape, lambda *_: (0,) * nd)


def _resident(shape):
    nd = len(shape)
    return pl.BlockSpec(shape, lambda *_: (0,) * nd, pipeline_mode=pl.Buffered(1))


def _behind(tokens, body, n_in):
    deps = [t for t in tokens if t is not None]

    def ordered(*refs):
        body(*refs[:n_in], *refs[n_in + len(deps):])

    return ordered, [_ANY] * len(deps), deps


def _norm_inproj(x, g, w, layer, tm, after=()):
    s = x.shape[0]

    def body(x_ref, g_ref, w_ref, z_ref, ht_ref):
        h32 = _rms_fwd(x_ref[...], g_ref[...])
        ht_ref[...] = h32.T.astype(BF16)
        h = h32.astype(BF16)
        for j in range(N_DEV):
            z_ref[:, j * IN_BLK:(j + 1) * IN_BLK] = _dot(h, w_ref[j]).astype(BF16)

    body, dep_specs, deps = _behind(after, body, 3)
    return pl.pallas_call(
        body, name=f"norm_inproj_l{layer}", grid=(s // tm,),
        in_specs=[pl.BlockSpec((tm, D), lambda i: (i, 0)), _full((1, D)), _resident((N_DEV, D, IN_BLK))] + dep_specs,
        out_specs=[pl.BlockSpec((tm, N_IN), lambda i: (i, 0)), pl.BlockSpec((D, tm), lambda i: (0, i))],
        out_shape=[SDS((s, N_IN), BF16), SDS((D, s), BF16)],
        compiler_params=_cp("parallel"))(x, g, w, *deps)


def _gmlp_values(zu_ref, zv_ref, lng_ref, lnb_ref):
    zu = zu_ref[...].astype(F32)
    zv = zv_ref[...].astype(F32)
    u, tu = _gelu(zu)
    gv, tv = _gelu(zv)
    xc = gv - jnp.mean(gv, axis=-1, keepdims=True)
    rstd = lax.rsqrt(jnp.mean(xc * xc, axis=-1, keepdims=True) + EPS)
    xh = xc * rstd
    vb = (xh * lng_ref[...] + lnb_ref[...]).astype(BF16)
    return zu, zv, u, tu, tv, xh, rstd, vb


def _mixer_fwd(x, h0, h1, z, lng, lnb, ws, bsb, wo, layer, tm):
    s = x.shape[0]

    def body(x_ref, h0_ref, h1_ref, zu_ref, zv_ref, zg_ref, za_ref, zb_ref, lng_ref, lnb_ref, ws_ref, bsb_ref,
             wo_ref, x1_ref, mg_ref, ya_s):
        _, _, u, _, _, _, _, vb = _gmlp_values(zu_ref, zv_ref, lng_ref, lnb_ref)
        for c in range(tm // HD):
            rs = slice(c * HD, (c + 1) * HD)
            for g in range(HEADS):
                cs = slice(g * HD, (g + 1) * HD)
                ya_s[rs, cs] = u[rs, cs] * (_dot(ws_ref[g], vb[rs, cs]) + bsb_ref[g])
        gg, _ = _gelu(zg_ref[...].astype(F32))
        yb = (h0_ref[...] + h1_ref[...]) * gg
        m32 = _sigmoid(za_ref[...].astype(F32)) * ya_s[...] + _sigmoid(zb_ref[...].astype(F32)) * yb
        mg_ref[...] = m32.T.astype(BF16)
        x1_ref[...] = x_ref[...] + _dot(m32.astype(BF16), wo_ref[...])

    tile = pl.BlockSpec((tm, D), lambda i: (i, 0))
    wspec = _full((HEADS, HD, HD))
    return pl.pallas_call(
        body, name=f"mixer_fwd_l{layer}", grid=(s // tm,),
        in_specs=[tile, tile, tile] + [pl.BlockSpec((tm, D), lambda i, c=c: (i, c)) for c in (0, 1, 3, 4, 5)]
        + [_full((1, D)), _full((1, D)), wspec, wspec, _full((D, D))],
        out_specs=[tile, pl.BlockSpec((D, tm), lambda i: (0, i))], out_shape=[SDS((s, D), F32), SDS((D, s), BF16)],
        scratch_shapes=[pltpu.VMEM((tm, D), F32)],
        compiler_params=_cp("parallel"))(x, h0, h1, z, z, z, z, z, lng, lnb, ws, bsb, wo)


def _conv(taps, cw_ref, cb_ref):
    _, m1, c0, p1, p2 = taps
    return cb_ref[...] + m1 * cw_ref[0:1, :] + c0 * cw_ref[1:2, :] + p1 * cw_ref[2:3, :] + p2 * cw_ref[3:4, :]


def _heads_dot(xb, w_ref, d):
    return jnp.concatenate([_dot(xb[:, h * HD:(h + 1) * HD], w_ref[d, h]) for h in range(HEADS)], axis=1)


def _lru_decay(r, sp):
    la = (-LRU_C) * r * sp
    a = jnp.exp(la)
    return a, jnp.tanh(-la) * (a * a + 1.0)


def _lru_gates_fwd(z, cw, cb, wr, wi, br, bi, lam, layer, tm):
    s = z.shape[0]
    nt = s // tm

    def body(zp_ref, zc_ref, zn_ref, cw_ref, cb_ref, wr_ref, wi_ref, br_ref, bi_ref, lam_ref,
             a0_ref, b0_ref, a1_ref, b1_ref, xc_ref, r0_ref, i0_ref, r1_ref, i1_ref):
        fp, fn = _halo_flags(nt)
        xc = _conv(_taps(*_halo_load(zp_ref, zc_ref, zn_ref, fp, fn), tm), cw_ref, cb_ref)
        xb = xc.astype(BF16)
        xc_ref[...] = xb
        for d, (a_ref, b_ref, r_ref, i_ref) in enumerate(((a0_ref, b0_ref, r0_ref, i0_ref),
                                                          (a1_ref, b1_ref, r1_ref, i1_ref))):
            r = _sigmoid(_heads_dot(xb, wr_ref, d) + br_ref[d:d + 1, :])
            ig = _sigmoid(_heads_dot(xb, wi_ref, d) + bi_ref[d:d + 1, :])
            a, q = _lru_decay(r, _softplus(-lam_ref[d:d + 1, :]))
            a_ref[...] = a
            b_ref[...] = jnp.sqrt(q) * (ig * xc)
            r_ref[...] = r.astype(BF16)
            i_ref[...] = ig.astype(BF16)

    tile = pl.BlockSpec((tm, D), lambda i: (i, 0))
    return pl.pallas_call(
        body, name=f"lru_gates_fwd_l{layer}", grid=(nt,),
        in_specs=[*_halo_specs(tm, s, 2, 16), _full((4, D)), _full((1, D)),
                  _full((2, HEADS, HD, HD)), _full((2, HEADS, HD, HD)), _full((2, D)), _full((2, D)), _full((2, D))],
        out_specs=[tile] * 9, out_shape=[SDS((s, D), F32)] * 4 + [SDS((s, D), BF16)] * 5,
        compiler_params=_cp("parallel"))(z, z, z, cw, cb, wr, wi, br, bi, lam)


def _scan_group(a, x, c, reverse, bwd):
    row = lax.broadcasted_iota(jnp.int32, a.shape, 0)
    b = a * x if bwd else x
    for d in (1, 2, 4):
        keep = (row < 8 - d) if reverse else (row >= d)
        sh = 8 - d if reverse else d
        a_s = jnp.where(keep, pltpu.roll(a, sh, 0), 1.0)
        b_s = jnp.where(keep, pltpu.roll(b, sh, 0), 0.0)
        b = a * b_s + b
        a = a * a_s
    h = b + a * c
    new_c = h[0:1, :] if reverse else h[7:8, :]
    if not bwd:
        return h, new_c
    if reverse:
        prev = jnp.where(row < 7, pltpu.roll(h, 7, 0), c)
    else:
        prev = jnp.where(row >= 1, pltpu.roll(h, 1, 0), c)
    return x + prev, new_c


def _lru_scan(a_f, x_f, a_r, x_r, bwd, layer):
    s = a_f.shape[0]
    ts = min(1024, s // 2)
    cb = 512
    nt = s // ts
    ng = ts // 8

    def body(af_ref, xf_ref, ar_ref, xr_ref, of_ref, or_ref, cf, cr):
        @pl.when(pl.program_id(1) == 0)
        def _():
            cf[...] = jnp.zeros_like(cf)
            cr[...] = jnp.zeros_like(cr)

        def step(j, carry):
            c_f, c_r = carry
            rf = pl.multiple_of(j * 8, 8)
            rr = pl.multiple_of((ng - 1 - j) * 8, 8)
            o, c_f = _scan_group(af_ref[pl.ds(rf, 8), :], xf_ref[pl.ds(rf, 8), :], c_f, False, bwd)
            of_ref[pl.ds(rf, 8), :] = o
            o, c_r = _scan_group(ar_ref[pl.ds(rr, 8), :], xr_ref[pl.ds(rr, 8), :], c_r, True, bwd)
            or_ref[pl.ds(rr, 8), :] = o
            return c_f, c_r

        c_f, c_r = lax.fori_loop(0, ng, step, (cf[0:1, :], cr[0:1, :]), unroll=2)
        cf[...] = jnp.broadcast_to(c_f, cf.shape)
        cr[...] = jnp.broadcast_to(c_r, cr.shape)

    fwd = pl.BlockSpec((ts, cb), lambda c, t: (t, c))
    rev = pl.BlockSpec((ts, cb), lambda c, t: (nt - 1 - t, c))
    return pl.pallas_call(
        body, name=f"lru_scan_{'bwd' if bwd else 'fwd'}_l{layer}", grid=(D // cb, nt),
        in_specs=[fwd, fwd, rev, rev], out_specs=[fwd, rev],
        out_shape=[SDS((s, D), F32)] * 2,
        scratch_shapes=[pltpu.VMEM((8, cb), F32), pltpu.VMEM((8, cb), F32)],
        compiler_params=_cp("parallel", "arbitrary"))(a_f, x_f, a_r, x_r)


def _ffn_fwd(x1, g, wfi, wfo, layer, tm, head=None):
    s = x1.shape[0]

    def ffn(x_ref, g_ref, wi_ref, wo_ref, ff_ref, dff_ref, h_ref):
        x = x_ref[...]
        h = _rms_fwd(x, g_ref[...]).astype(BF16)
        h_ref[...] = h
        acc = x
        for k in range(4):
            gate = _dot_nt(h, wi_ref[0, k])
            up = _dot_nt(h, wi_ref[1, k])
            sg = _sigmoid(gate)
            silu = gate * sg
            ff = (silu * up).astype(BF16)
            ff_ref[k] = ff
            dff_ref[0, k] = (up * (sg * (1.0 + gate * (1.0 - sg)))).astype(BF16)
            dff_ref[1, k] = silu.astype(BF16)
            acc = acc + _dot(ff, wo_ref[k])
        return acc

    def body(x_ref, g_ref, wi_ref, wo_ref, x2_ref, ff_ref, dff_ref, h_ref):
        x2_ref[...] = ffn(x_ref, g_ref, wi_ref, wo_ref, ff_ref, dff_ref, h_ref)

    def body_with_head(x_ref, g_ref, wi_ref, wo_ref, fg_ref, t_ref, dx_ref, loss_ref, dfg_ref, ff_ref, dff_ref, h_ref):
        @pl.when(pl.program_id(0) == 0)
        def _():
            loss_ref[...] = jnp.zeros_like(loss_ref)
            dfg_ref[...] = jnp.zeros_like(dfg_ref)

        x2 = ffn(x_ref, g_ref, wi_ref, wo_ref, ff_ref, dff_ref, h_ref)
        fg = fg_ref[...]
        e = _rms_fwd(x2, fg) - t_ref[...]
        rows = jnp.sum(e * e, axis=-1, keepdims=True)
        loss_ref[...] += (0.5 / D) * jnp.sum(rows, axis=0, keepdims=True)
        dx, dg = _rms_bwd(x2, fg, e * (1.0 / D))
        dx_ref[...] = dx
        _add_rows128(dfg_ref, dg)

    tile = pl.BlockSpec((tm, D), lambda i: (i, 0))
    weights = [_resident((2, 4, FF_BLK, D)), _resident((4, FF_BLK, D))]
    kept_specs = [pl.BlockSpec((4, tm, FF_BLK), lambda i: (0, i, 0)),
                  pl.BlockSpec((2, 4, tm, FF_BLK), lambda i: (0, 0, i, 0)), tile]
    kept_shapes = [SDS((4, s, FF_BLK), BF16), SDS((2, 4, s, FF_BLK), BF16), SDS((s, D), BF16)]
    if head is None:
        return pl.pallas_call(
            body, name=f"ffn_fwd_l{layer}", grid=(s // tm,),
            in_specs=[tile, _full((1, D))] + weights, out_specs=[tile] + kept_specs,
            out_shape=[SDS((s, D), F32)] + kept_shapes, compiler_params=_cp("parallel"))(x1, g, wfi, wfo)
    final_g, tgt = head
    return pl.pallas_call(
        body_with_head, name=f"ffn_fwd_loss_l{layer}", grid=(s // tm,),
        in_specs=[tile, _full((1, D))] + weights + [_full((1, D)), tile],
        out_specs=[tile, _full((1, 1)), _full((LANE_ROWS, HD))] + kept_specs,
        out_shape=[SDS((s, D), F32), SDS((1, 1), F32), SDS((LANE_ROWS, HD), F32)] + kept_shapes,
        compiler_params=_cp("arbitrary"))(x1, g, wfi, wfo, final_g, tgt)


def _ffn_bwd(dx2, wfo, factors, wfi, x1, g, layer, tm):
    s = dx2.shape[0]

    def body(dx_ref, wo_ref, f_ref, wi_ref, x_ref, g_ref, dgu_ref, dx1_ref, dg_ref):
        @pl.when(pl.program_id(0) == 0)
        def _():
            dg_ref[...] = jnp.zeros_like(dg_ref)

        dx = dx_ref[...]
        dxb = dx.astype(BF16)
        dh = None
        for k in range(4):
            dff = _dot_nt(dxb, wo_ref[k])
            d_gate = (dff * f_ref[0, k].astype(F32)).astype(BF16)
            d_up = (dff * f_ref[1, k].astype(F32)).astype(BF16)
            dgu_ref[0, k] = d_gate
            dgu_ref[1, k] = d_up
            part = _dot(d_gate, wi_ref[k]) + _dot(d_up, wi_ref[4 + k])
            dh = part if dh is None else dh + part
        dxn, dg = _rms_bwd(x_ref[...], g_ref[...], dh)
        dx1_ref[...] = dx + dxn
        _add_rows128(dg_ref, dg)

    tile = pl.BlockSpec((tm, D), lambda i: (i, 0))
    blk = pl.BlockSpec((2, 4, tm, FF_BLK), lambda i: (0, 0, i, 0))
    return pl.pallas_call(
        body, name=f"ffn_bwd_l{layer}", grid=(s // tm,),
        in_specs=[tile, _resident((4, FF_BLK, D)), blk, _resident((N_DEV, FF_BLK, D)), tile, _full((1, D))],
        out_specs=[blk, tile, _full((LANE_ROWS, HD))],
        out_shape=[SDS((2, 4, s, FF_BLK), BF16), SDS((s, D), F32), SDS((LANE_ROWS, HD), F32)],
        compiler_params=_cp("arbitrary"))(dx2, wfo, factors, wfi, x1, g)


def _mm_nt_rms_bwd(a, a_spec, a_blocks, w, w_is_transposed, x, g, dres, name, tm, after=(), jobs=()):
    s = x.shape[0]

    def body(a_ref, w_ref, x_ref, g_ref, dres_ref, dx_ref, dg_ref):
        @pl.when(pl.program_id(0) == 0)
        def _():
            dg_ref[...] = jnp.zeros_like(dg_ref)

        dh = None
        for k, blk in enumerate(a_blocks(a_ref)):
            part = _dot(blk, w_ref[k]) if w_is_transposed else _dot_nt(blk, w_ref[k])
            dh = part if dh is None else dh + part
        dx, dg = _rms_bwd(x_ref[...], g_ref[...], dh)
        dx_ref[...] = dres_ref[...] + dx
        _add_rows128(dg_ref, dg)

    tile = pl.BlockSpec((tm, D), lambda i: (i, 0))
    body, dep_specs, deps = _behind(after, body, 5)
    body, job_in, job_args, job_out, job_shapes, aliases = _carry(jobs, s // tm, body, 5 + len(deps), 2)
    res = pl.pallas_call(
        body, name=name, grid=(s // tm,),
        in_specs=[a_spec, _resident(w.shape), tile, _full((1, D)), tile] + dep_specs + job_in,
        out_specs=[tile, _full((LANE_ROWS, HD))] + job_out,
        out_shape=[SDS((s, D), F32), SDS((LANE_ROWS, HD), F32)] + job_shapes,
        input_output_aliases=aliases, compiler_params=_cp("arbitrary"))(a, w, x, g, dres, *deps, *job_args)
    for j, job in enumerate(jobs):
        job.results = res[2 + 4 * j:6 + 4 * j]
    return res[0], res[1]


def _mm_tn(a, a_spec, b, b_spec, nb, out_shape, out_spec, name, a_is_transposed=True, after=None, jobs=()):
    def body(a_ref, b_ref, *rest):
        o_ref = rest[-1]
        bb = b_ref[...].astype(BF16)
        o_ref[...] = (_dot(a_ref[...], bb) if a_is_transposed else _dot_tn(a_ref[...], bb)).astype(BF16)

    deps = [] if after is None else [after]
    body, job_in, job_args, job_out, job_shapes, aliases = _carry(jobs, nb, body, 2 + len(deps), 1)
    res = pl.pallas_call(
        body, name=name, grid=(nb,), in_specs=[a_spec, b_spec] + [_ANY] * len(deps) + job_in,
        out_specs=[out_spec] + job_out, out_shape=[SDS(out_shape, BF16)] + job_shapes,
        input_output_aliases=aliases, compiler_params=_cp("parallel"))(a, b, *deps, *job_args)
    for j, job in enumerate(jobs):
        job.results = res[1 + 4 * j:5 + 4 * j]
    return res[0]


def _mixer_bwd(dx1, wo, h0, h1, z, lng, lnb, ws, wst, bsb, layer, tm, after=()):
    s = dx1.shape[0]
    nt = s // tm

    def body(dx_ref, wo_ref, h0_ref, h1_ref, zu_ref, zv_ref, zg_ref, za_ref, zb_ref, lng_ref, lnb_ref,
             ws_ref, wst_ref, bsb_ref, dz_ref, dh_ref, dws_ref, dbs_ref, dlng_ref, dlnb_ref,
             du_s, dv_s, ya_s, dbs_acc):
        i = pl.program_id(0)

        @pl.when(i == 0)
        def _():
            for r in (dws_ref, dlng_ref, dlnb_ref, dbs_acc):
                r[...] = jnp.zeros_like(r)

        dm = _dot_nt(dx_ref[...].astype(BF16), wo_ref[...])
        sa = _sigmoid(za_ref[...].astype(F32))
        sb = _sigmoid(zb_ref[...].astype(F32))
        zg = zg_ref[...].astype(F32)
        gg, tg = _gelu(zg)
        hs = h0_ref[...] + h1_ref[...]
        dyb = dm * sb
        dya = dm * sa
        dh_ref[...] = dyb * gg
        dz_ref[:, 2 * D:3 * D] = jnp.zeros((tm, D), BF16)
        dz_ref[:, 3 * D:4 * D] = (dyb * hs * _gelu_grad(zg, tg)).astype(BF16)
        dz_ref[:, 5 * D:6 * D] = (dm * (hs * gg) * (sb * (1.0 - sb))).astype(BF16)

        zu, zv, u, tu, tv, xh, rstd, vb = _gmlp_values(zu_ref, zv_ref, lng_ref, lnb_ref)
        for c in range(tm // HD):
            rs = slice(c * HD, (c + 1) * HD)
            for g in range(HEADS):
                cs = slice(g * HD, (g + 1) * HD)
                vblk = vb[rs, cs]
                mixed = _dot(ws_ref[g], vblk) + bsb_ref[g]
                ya_s[rs, cs] = u[rs, cs] * mixed
                du_s[rs, cs] = dya[rs, cs] * mixed
                dmx = dya[rs, cs] * u[rs, cs]
                dbs_acc[g] += dmx
                dmxb = dmx.astype(BF16)
                dws_ref[g] += _dot_nt(dmxb, vblk)
                dv_s[rs, cs] = _dot(wst_ref[g], dmxb)
        dz_ref[:, 4 * D:5 * D] = (dm * ya_s[...] * (sa * (1.0 - sa))).astype(BF16)
        dv = dv_s[...]
        _add_rows128(dlng_ref, jnp.sum(dv * xh, axis=0, keepdims=True))
        _add_rows128(dlnb_ref, jnp.sum(dv, axis=0, keepdims=True))
        dxh = dv * lng_ref[...]
        dgv = rstd * (dxh - jnp.mean(dxh, axis=-1, keepdims=True)
                      - xh * jnp.mean(dxh * xh, axis=-1, keepdims=True))
        dz_ref[:, 0:D] = (du_s[...] * _gelu_grad(zu, tu)).astype(BF16)
        dz_ref[:, D:2 * D] = (dgv * _gelu_grad(zv, tv)).astype(BF16)

        @pl.when(i == nt - 1)
        def _():
            for g in range(HEADS):
                dbs_ref[g:g + 1, :] = jnp.sum(dbs_acc[g].T, axis=0, keepdims=True)

    tile = pl.BlockSpec((tm, D), lambda i: (i, 0))
    wspec = _full((HEADS, HD, HD))
    body, dep_specs, deps = _behind(after, body, 14)
    return pl.pallas_call(
        body, name=f"mixer_bwd_l{layer}", grid=(nt,),
        in_specs=[tile, _full((D, D)), tile, tile]
        + [pl.BlockSpec((tm, D), lambda i, c=c: (i, c)) for c in (0, 1, 3, 4, 5)]
        + [_full((1, D)), _full((1, D)), wspec, wspec, wspec] + dep_specs,
        out_specs=[pl.BlockSpec((tm, N_IN), lambda i: (i, 0)), tile, wspec, _full((HEADS, HD)),
                   _full((LANE_ROWS, HD)), _full((LANE_ROWS, HD))],
        out_shape=[SDS((s, N_IN), BF16), SDS((s, D), F32), SDS((HEADS, HD, HD), F32), SDS((HEADS, HD), F32),
                   SDS((LANE_ROWS, HD), F32), SDS((LANE_ROWS, HD), F32)],
        scratch_shapes=[pltpu.VMEM((tm, D), F32)] * 3 + [pltpu.VMEM((HEADS, HD, HD), F32)],
        compiler_params=_cp("arbitrary"))(dx1, wo, h0, h1, z, z, z, z, z, lng, lnb, ws, wst, bsb, *deps)


def _lru_gates_bwd(xcb, gates, h0, h1, g0, g1, wr, wi, lam, layer, tm, after=()):
    s = xcb.shape[0]
    nt = s // tm

    def body(xc_ref, r0_ref, i0_ref, r1_ref, i1_ref, h0p_ref, h0_ref, h1_ref, h1n_ref, g0_ref, g1_ref,
             wr_ref, wi_ref, lam_ref, dxc_ref, dwr_ref, dwi_ref, dbr_ref, dbi_ref, dlam_ref):
        i = pl.program_id(0)
        fp, fn = _halo_flags(nt)

        @pl.when(i == 0)
        def _():
            for r in (dwr_ref, dwi_ref, dbr_ref, dbi_ref, dlam_ref):
                r[...] = jnp.zeros_like(r)

        xb = xc_ref[...]
        xc = xb.astype(F32)
        zeros8 = jnp.zeros((8, D), F32)
        h_prev = _taps(h0p_ref[...] * fp, h0_ref[...], zeros8, tm)[1]
        h_next = _taps(zeros8, h1_ref[...], h1n_ref[...] * fn, tm)[3]
        dxc = jnp.zeros((tm, D), F32)
        for d, (g_ref, hsh, r_ref, i_ref) in enumerate(((g0_ref, h_prev, r0_ref, i0_ref),
                                                        (g1_ref, h_next, r1_ref, i1_ref))):
            sp = _softplus(-lam_ref[d:d + 1, :])
            r = r_ref[...].astype(F32)
            ig = i_ref[...].astype(F32)
            a, q = _lru_decay(r, sp)
            rmult = jnp.where(q > 0.0, lax.rsqrt(jnp.where(q > 0.0, q, 1.0)), 0.0)
            mult = q * rmult
            db = g_ref[...]
            da = db * hsh
            dmult = db * (ig * xc)
            di = db * (mult * xc)
            dxc = dxc + db * (mult * ig)
            dla = da * a - dmult * (a * a * rmult)
            dsp_dlam = -_sigmoid(-lam_ref[d:d + 1, :])
            _add_rows128(dlam_ref, jnp.sum(dla * r, axis=0, keepdims=True) * ((-LRU_C) * dsp_dlam), d * LANE_ROWS)
            dpr = dla * sp * (-LRU_C) * (r * (1.0 - r))
            dpi = di * (ig * (1.0 - ig))
            _add_rows128(dbr_ref, jnp.sum(dpr, axis=0, keepdims=True), d * LANE_ROWS)
            _add_rows128(dbi_ref, jnp.sum(dpi, axis=0, keepdims=True), d * LANE_ROWS)
            dprb = dpr.astype(BF16)
            dpib = dpi.astype(BF16)
            parts = []
            for h in range(HEADS):
                cs = slice(h * HD, (h + 1) * HD)
                dwr_ref[d, h] += _dot_tn(xb[:, cs], dprb[:, cs])
                dwi_ref[d, h] += _dot_tn(xb[:, cs], dpib[:, cs])
                parts.append(_dot_nt(dprb[:, cs], wr_ref[d, h]) + _dot_nt(dpib[:, cs], wi_ref[d, h]))
            dxc = dxc + jnp.concatenate(parts, axis=1)
        dxc_ref[...] = dxc.astype(BF16)

    tile = pl.BlockSpec((tm, D), lambda i: (i, 0))
    hp, hc, hn = _halo_specs(tm, s, 0)
    wspec = _full((2, HEADS, HD, HD))
    vspec = _full((2 * LANE_ROWS, HD))
    body, dep_specs, deps = _behind(after, body, 14)
    return pl.pallas_call(
        body, name=f"lru_gates_bwd_l{layer}", grid=(nt,),
        in_specs=[tile] * 5 + [hp, hc, hc, hn, tile, tile, wspec, wspec, _full((2, D))] + dep_specs,
        out_specs=[tile, wspec, wspec, vspec, vspec, vspec],
        out_shape=[SDS((s, D), BF16), SDS((2, HEADS, HD, HD), F32), SDS((2, HEADS, HD, HD), F32)]
        + [SDS((2 * LANE_ROWS, HD), F32)] * 3,
        compiler_params=_cp("arbitrary"))(xcb, *gates, h0, h0, h1, h1, g0, g1, wr, wi, lam, *deps)


def _conv_bwd(dz, dxc, z, cw, layer, tm):
    s = z.shape[0]
    nt = s // tm

    def body(dz_in, dp_ref, dc_ref, dn_ref, zp_ref, zc_ref, zn_ref, cw_ref, dz_ref, dcw_ref, dcb_ref):
        del dz_in
        fp, fn = _halo_flags(nt)

        @pl.when(pl.program_id(0) == 0)
        def _():
            dcw_ref[...] = jnp.zeros_like(dcw_ref)
            dcb_ref[...] = jnp.zeros_like(dcb_ref)

        dxc_halo = _halo_load(dp_ref, dc_ref, dn_ref, fp, fn)
        dxc = dxc_halo[1]
        dm2, dm1, _, dp1, _ = _taps(*dxc_halo, tm)
        dz_ref[...] = (cw_ref[0:1, :] * dp1 + cw_ref[1:2, :] * dxc + cw_ref[2:3, :] * dm1
                       + cw_ref[3:4, :] * dm2).astype(BF16)
        _, zm1, z0, zp1, zp2 = _taps(*_halo_load(zp_ref, zc_ref, zn_ref, fp, fn), tm)
        for k, zt in enumerate((zm1, z0, zp1, zp2)):
            _add_rows128(dcw_ref, jnp.sum(dxc * zt, axis=0, keepdims=True), k * LANE_ROWS)
        _add_rows128(dcb_ref, jnp.sum(dxc, axis=0, keepdims=True))

    return pl.pallas_call(
        body, name=f"conv_bwd_l{layer}", grid=(nt,),
        in_specs=[pl.BlockSpec(memory_space=pl.ANY), *_halo_specs(tm, s, 0, 16), *_halo_specs(tm, s, 2, 16),
                  _full((4, D))],
        out_specs=[pl.BlockSpec((tm, D), lambda i: (i, 2)), _full((4 * LANE_ROWS, HD)), _full((LANE_ROWS, HD))],
        out_shape=[SDS((s, N_IN), BF16), SDS((4 * LANE_ROWS, HD), F32), SDS((LANE_ROWS, HD), F32)],
        input_output_aliases={0: 0},
        compiler_params=_cp("arbitrary"))(dz, dxc, dxc, dxc, z, z, z, cw)


def _me():
    return lax.axis_index("x"), lax.axis_index("y"), lax.axis_index("c")


def _peer(m):
    x, y, c = _me()
    px = 1 - x if m & 4 else x
    py = 1 - y if m & 2 else y
    pc = 1 - c if m & 1 else c
    return (px, py, pc), 4 * px + 2 * py + pc


_ANY = pl.BlockSpec(memory_space=pl.ANY)
_EXCHANGE_SEMS = [pltpu.SemaphoreType.DMA((N_DEV - 1,)), pltpu.SemaphoreType.DMA((N_DEV - 1,)), pltpu.SemaphoreType.DMA(())]


def _all_gather(v, after, name):
    def body(v_ref, after_ref, o_ref, send_sems, recv_sems, local_sem):
        del after_ref
        x, y, c = _me()
        me = 4 * x + 2 * y + c
        local = pltpu.make_async_copy(v_ref, o_ref.at[me], local_sem)
        local.start()
        sends = []
        for m in range(1, N_DEV):
            dev, _ = _peer(m)
            cp = pltpu.make_async_remote_copy(v_ref, o_ref.at[me], send_sems.at[m - 1], recv_sems.at[m - 1],
                                              device_id=dev, device_id_type=pl.DeviceIdType.MESH)
            cp.start()
            sends.append(cp)
        for m in range(1, N_DEV):
            dev, blk = _peer(m)
            pltpu.make_async_remote_copy(v_ref, o_ref.at[blk], send_sems.at[m - 1], recv_sems.at[m - 1],
                                         device_id=dev, device_id_type=pl.DeviceIdType.MESH).wait_recv()
        for cp in sends:
            cp.wait_send()
        local.wait()

    return pl.pallas_call(
        body, name=name, in_specs=[_ANY, _ANY], out_specs=_ANY,
        out_shape=SDS((N_DEV,) + v.shape, v.dtype), scratch_shapes=_EXCHANGE_SEMS)(v, after)


_HBM = pl.BlockSpec(memory_space=pltpu.HBM)
_SEM = pl.BlockSpec(memory_space=pltpu.SEMAPHORE)
_EFFECT = pltpu.CompilerParams(has_side_effects=pltpu.SideEffectType.DATAFLOW_SIDE_EFFECTING)
_PEER_SEMS = pltpu.SemaphoreType.DMA((N_DEV - 1,))


def _in_hbm(a):
    return pltpu.with_memory_space_constraint(a, pltpu.HBM)


def _remote(src, dst, send_sems, recv_sems, m):
    dev, _ = _peer(m)
    return pltpu.make_async_remote_copy(src, dst, send_sems.at[m - 1], recv_sems.at[m - 1],
                                        device_id=dev, device_id_type=pl.DeviceIdType.MESH)


def _gather_start(lands, after, name):
    n = len(lands)

    def body(*refs):
        land = refs[:n]
        sems = refs[n + 1:3 * n + 1]
        token = refs[-1]
        x, y, c = _me()
        me = 4 * x + 2 * y + c
        for t in range(n):
            for m in range(1, N_DEV):
                _remote(land[t].at[me], land[t].at[me], sems[2 * t], sems[2 * t + 1], m).start()
        token[...] = jnp.zeros_like(token)

    res = pl.pallas_call(
        body, name=name, in_specs=[_HBM] * n + [_ANY],
        out_specs=[_SEM] * (2 * n) + [_HBM] * n + [pl.BlockSpec(memory_space=pltpu.VMEM)],
        out_shape=[_PEER_SEMS] * (2 * n) + [pltpu.HBM(a.shape, a.dtype) for a in lands] + [SDS((8, 128), F32)],
        input_output_aliases={t: 2 * n + t for t in range(n)},
        compiler_params=_EFFECT)(*[_in_hbm(a) for a in lands], after)
    return [(res[2 * t], res[2 * t + 1], res[2 * n + t]) for t in range(n)], res[-1]


def _gather_wait(handle, after, name):
    send_sems, recv_sems, land = handle

    def body(land_ref, ssem, rsem, after_ref, out_ref):
        del after_ref, out_ref
        x, y, c = _me()
        me = 4 * x + 2 * y + c
        for m in range(1, N_DEV):
            _, blk = _peer(m)
            cp = _remote(land_ref.at[me], land_ref.at[blk], ssem, rsem, m)
            cp.wait_send()
            cp.wait_recv()

    return pl.pallas_call(
        body, name=name, in_specs=[_HBM, _SEM, _SEM, _ANY], out_specs=_HBM,
        out_shape=pltpu.HBM(land.shape, land.dtype), input_output_aliases={0: 0},
        compiler_params=_EFFECT)(land, send_sems, recv_sems, after)


FIRST_STAGE = (1, 2, 4, 6)
RELAYED = (2, 4, 6)
OTHER_CORE = 1


def _stage_copy(src, dst, send_sems, recv_sems, k, m):
    dev, _ = _peer(m)
    return pltpu.make_async_remote_copy(src, dst, send_sems.at[k], recv_sems.at[k],
                                        device_id=dev, device_id_type=pl.DeviceIdType.MESH)


def _gather2_start(lands, after, name):
    n = len(lands)

    def body(*refs):
        land = refs[:n]
        sems = refs[n + 1:3 * n + 1]
        token = refs[-1]
        x, y, c = _me()
        me = 4 * x + 2 * y + c
        for t in range(n):
            for k, m in enumerate(FIRST_STAGE):
                _stage_copy(land[t].at[me], land[t].at[me], sems[2 * t], sems[2 * t + 1], k, m).start()
        token[...] = jnp.zeros_like(token)

    stage_sems = pltpu.SemaphoreType.DMA((len(FIRST_STAGE),))
    res = pl.pallas_call(
        body, name=name, in_specs=[_HBM] * n + [_ANY],
        out_specs=[_SEM] * (2 * n) + [_HBM] * n + [pl.BlockSpec(memory_space=pltpu.VMEM)],
        out_shape=[stage_sems] * (2 * n) + [pltpu.HBM(a.shape, a.dtype) for a in lands] + [SDS((8, 128), F32)],
        input_output_aliases={t: 2 * n + t for t in range(n)},
        compiler_params=_EFFECT)(*[_in_hbm(a) for a in lands], after)
    return [(res[2 * t], res[2 * t + 1], res[2 * n + t]) for t in range(n)], res[-1]


def _gather2_relay(handles, after, name):
    n = len(handles)

    def body(*refs):
        land, send1, recv1 = refs[:n], refs[n:2 * n], refs[2 * n:3 * n]
        sems = refs[3 * n + 1:5 * n + 1]
        token = refs[-1]
        x, y, c = _me()
        me = 4 * x + 2 * y + c
        for t in range(n):
            for j, m in enumerate(RELAYED):
                _, blk = _peer(m)
                _stage_copy(land[t].at[me], land[t].at[blk], send1[t], recv1[t], 1 + j, m).wait_recv()
                _stage_copy(land[t].at[blk], land[t].at[blk], sems[2 * t], sems[2 * t + 1], j, OTHER_CORE).start()
        token[...] = jnp.zeros_like(token)

    relay_sems = pltpu.SemaphoreType.DMA((len(RELAYED),))
    lands = [h[2] for h in handles]
    res = pl.pallas_call(
        body, name=name, in_specs=[_HBM] * n + [_SEM] * (2 * n) + [_ANY],
        out_specs=[_SEM] * (2 * n) + [_HBM] * n + [pl.BlockSpec(memory_space=pltpu.VMEM)],
        out_shape=[relay_sems] * (2 * n) + [pltpu.HBM(a.shape, a.dtype) for a in lands] + [SDS((8, 128), F32)],
        input_output_aliases={t: 2 * n + t for t in range(n)},
        compiler_params=_EFFECT)(*lands, *[h[0] for h in handles], *[h[1] for h in handles], after)
    return [(h[0], h[1], res[2 * t], res[2 * t + 1], res[2 * n + t]) for t, h in enumerate(handles)], res[-1]


def _gather2_wait(handle, after, name):
    send1, recv1, send2, recv2, land = handle

    def body(land_ref, s1, r1, s2, r2, after_ref, out_ref):
        del after_ref, out_ref
        x, y, c = _me()
        me = 4 * x + 2 * y + c
        _, other = _peer(OTHER_CORE)
        _stage_copy(land_ref.at[me], land_ref.at[other], s1, r1, 0, OTHER_CORE).wait_recv()
        for k, m in enumerate(FIRST_STAGE):
            _stage_copy(land_ref.at[me], land_ref.at[me], s1, r1, k, m).wait_send()
        for j, m in enumerate(RELAYED):
            _, mine = _peer(m)
            _, theirs = _peer(m ^ OTHER_CORE)
            _stage_copy(land_ref.at[mine], land_ref.at[mine], s2, r2, j, OTHER_CORE).wait_send()
            _stage_copy(land_ref.at[mine], land_ref.at[theirs], s2, r2, j, OTHER_CORE).wait_recv()

    return pl.pallas_call(
        body, name=name, in_specs=[_HBM] + [_SEM] * 4 + [_ANY], out_specs=_HBM,
        out_shape=pltpu.HBM(land.shape, land.dtype), input_output_aliases={0: 0},
        compiler_params=_EFFECT)(land, send1, recv1, send2, recv2, after)


def _exchange_start(ps, name):
    n = len(ps)

    def body(*refs):
        p = refs[:n]
        got = refs[n:2 * n]
        sems = refs[2 * n:5 * n]
        token = refs[-1]
        x, y, c = _me()
        me = 4 * x + 2 * y + c
        for t in range(n):
            pltpu.make_async_copy(p[t].at[me], got[t].at[me], sems[3 * t + 2]).start()
            for m in range(1, N_DEV):
                _, blk = _peer(m)
                _remote(p[t].at[blk], got[t].at[me], sems[3 * t], sems[3 * t + 1], m).start()
        token[...] = jnp.zeros_like(token)

    res = pl.pallas_call(
        body, name=name, in_specs=[_HBM] * (2 * n),
        out_specs=[_SEM] * (3 * n) + [_HBM] * (2 * n) + [pl.BlockSpec(memory_space=pltpu.VMEM)],
        out_shape=[_PEER_SEMS, _PEER_SEMS, pltpu.SemaphoreType.DMA(())] * n
        + [pltpu.HBM(a.shape, a.dtype) for a in ps] * 2 + [SDS((8, 128), F32)],
        input_output_aliases={t: 3 * n + t for t in range(2 * n)},
        compiler_params=_EFFECT)(*[_in_hbm(a) for a in ps], *[_in_hbm(lax.empty(a.shape, a.dtype)) for a in ps])
    return [(res[3 * t], res[3 * t + 1], res[3 * t + 2], res[3 * n + t], res[4 * n + t]) for t in range(n)], res[-1]


def _exchange_wait(handle, after, name):
    send_sems, recv_sems, local_sem, p, got = handle

    def body(p_ref, got_ref, ssem, rsem, lsem, after_ref, p_out, got_out):
        del after_ref, p_out, got_out
        x, y, c = _me()
        me = 4 * x + 2 * y + c
        pltpu.make_async_copy(p_ref.at[me], got_ref.at[me], lsem).wait()
        for m in range(1, N_DEV):
            _, blk = _peer(m)
            cp = _remote(p_ref.at[blk], got_ref.at[blk], ssem, rsem, m)
            cp.wait_send()
            cp.wait_recv()

    return pl.pallas_call(
        body, name=name, in_specs=[_HBM, _HBM, _SEM, _SEM, _SEM, _ANY], out_specs=[_HBM, _HBM],
        out_shape=[pltpu.HBM(p.shape, p.dtype), pltpu.HBM(got.shape, got.dtype)],
        input_output_aliases={0: 0, 1: 1}, compiler_params=_EFFECT)(p, got, send_sems, recv_sems, local_sem, after)[1]


CHIPS = (0, 2, 4, 6)


def _pairs_start(p, name):
    def body(p_ref, pair_ref, ssem, rsem, p_out, pair_out, token):
        del p_out, pair_out
        for k, chip in enumerate(CHIPS):
            _, blk = _peer(chip ^ OTHER_CORE)
            _stage_copy(p_ref.at[blk], pair_ref.at[k], ssem, rsem, k, OTHER_CORE).start()
        token[...] = jnp.zeros_like(token)

    sems = pltpu.SemaphoreType.DMA((len(CHIPS),))
    pair = lax.empty((len(CHIPS),) + p.shape[1:], p.dtype)
    res = pl.pallas_call(
        body, name=name, in_specs=[_HBM] * 2,
        out_specs=[_SEM] * 2 + [_HBM] * 2 + [pl.BlockSpec(memory_space=pltpu.VMEM)],
        out_shape=[sems, sems, pltpu.HBM(p.shape, p.dtype), pltpu.HBM(pair.shape, pair.dtype), SDS((8, 128), F32)],
        input_output_aliases={0: 2, 1: 3}, compiler_params=_EFFECT)(_in_hbm(p), _in_hbm(pair))
    return res[:4], res[4]


def _pairs_wait(handle, after, name):
    send_sems, recv_sems, p, pair = handle

    def body(p_ref, pair_ref, ssem, rsem, after_ref, p_out, pair_out):
        del after_ref, p_out, pair_out
        for k, chip in enumerate(CHIPS):
            _, blk = _peer(chip ^ OTHER_CORE)
            cp = _stage_copy(p_ref.at[blk], pair_ref.at[k], ssem, rsem, k, OTHER_CORE)
            cp.wait_send()
            cp.wait_recv()

    return pl.pallas_call(
        body, name=name, in_specs=[_HBM, _HBM, _SEM, _SEM, _ANY], out_specs=[_HBM, _HBM],
        out_shape=[pltpu.HBM(p.shape, p.dtype), pltpu.HBM(pair.shape, pair.dtype)],
        input_output_aliases={0: 0, 1: 1}, compiler_params=_EFFECT)(p, pair, send_sems, recv_sems, after)


def _sum_pairs(p, pair, me1, name):
    _, r, c = pair.shape
    tr = _row_tile(r)

    def body(me_ref, p_ref, pair_ref, o_ref):
        del me_ref
        o_ref[...] = (p_ref[...].astype(F32) + pair_ref[...].astype(F32)).astype(o_ref.dtype)

    def mine(k, i, me):
        chip = 2 * k
        return (jnp.bitwise_xor(me[0], chip), i, 0)

    assert CHIPS == tuple(2 * k for k in range(len(CHIPS)))
    blk = pl.BlockSpec((None, tr, c), lambda k, i, me: (k, i, 0))
    return pl.pallas_call(
        body, name=name,
        grid_spec=pltpu.PrefetchScalarGridSpec(
            num_scalar_prefetch=1, grid=(len(CHIPS), r // tr),
            in_specs=[pl.BlockSpec((None, tr, c), mine), blk], out_specs=blk),
        out_shape=SDS(pair.shape, pair.dtype), compiler_params=_cp("parallel", "parallel"))(me1, p, pair)


def _chips_start(q, name):
    def body(q_ref, got_ref, ssem, rsem, lsem, q_out, got_out, token):
        del q_out, got_out
        pltpu.make_async_copy(q_ref.at[0], got_ref.at[0], lsem).start()
        for k, chip in enumerate(CHIPS[1:]):
            _stage_copy(q_ref.at[k + 1], got_ref.at[k + 1], ssem, rsem, k, chip).start()
        token[...] = jnp.zeros_like(token)

    sems = pltpu.SemaphoreType.DMA((len(CHIPS) - 1,))
    res = pl.pallas_call(
        body, name=name, in_specs=[_HBM] * 2,
        out_specs=[_SEM] * 3 + [_HBM] * 2 + [pl.BlockSpec(memory_space=pltpu.VMEM)],
        out_shape=[sems, sems, pltpu.SemaphoreType.DMA(()), pltpu.HBM(q.shape, q.dtype), pltpu.HBM(q.shape, q.dtype),
                   SDS((8, 128), F32)],
        input_output_aliases={0: 3, 1: 4}, compiler_params=_EFFECT)(_in_hbm(q), _in_hbm(lax.empty(q.shape, q.dtype)))
    return res[:5], res[5]


def _chips_wait(handle, after, name):
    send_sems, recv_sems, local_sem, q, got = handle

    def body(q_ref, got_ref, ssem, rsem, lsem, after_ref, q_out, got_out):
        del after_ref, q_out, got_out
        pltpu.make_async_copy(q_ref.at[0], got_ref.at[0], lsem).wait()
        for k, chip in enumerate(CHIPS[1:]):
            cp = _stage_copy(q_ref.at[k + 1], got_ref.at[k + 1], ssem, rsem, k, chip)
            cp.wait_send()
            cp.wait_recv()

    return pl.pallas_call(
        body, name=name, in_specs=[_HBM, _HBM, _SEM, _SEM, _SEM, _ANY], out_specs=[_HBM, _HBM],
        out_shape=[pltpu.HBM(q.shape, q.dtype), pltpu.HBM(got.shape, got.dtype)],
        input_output_aliases={0: 0, 1: 1}, compiler_params=_EFFECT)(q, got, send_sems, recv_sems, local_sem, after)[1]


def _cast_into_slot(w, layer, me1, name):
    _, r, c = w.shape
    tr = next(t for t in (512, 352, r) if r % t == 0)

    def body(me_ref, w_ref, o_ref):
        del me_ref
        o_ref[...] = w_ref[...].astype(BF16)

    return pl.pallas_call(
        body, name=name,
        grid_spec=pltpu.PrefetchScalarGridSpec(
            num_scalar_prefetch=1, grid=(r // tr,),
            in_specs=[pl.BlockSpec((None, tr, c), lambda i, me: (layer, i, 0))],
            out_specs=pl.BlockSpec((None, tr, c), lambda i, me: (me[0], i, 0))),
        out_shape=SDS((N_DEV, r, c), BF16), compiler_params=_cp("arbitrary"))(me1, w)


def _cast_all_into_slots(ws, layers, me1, after, name):
    n = len(ws)

    def body(me_ref, *refs):
        del me_ref
        for w_ref, o_ref in zip(refs[:n], refs[n + 1:]):
            o_ref[...] = w_ref[...].astype(BF16)

    return pl.pallas_call(
        body, name=name,
        grid_spec=pltpu.PrefetchScalarGridSpec(
            num_scalar_prefetch=1, grid=(1,),
            in_specs=[pl.BlockSpec((None,) + a.shape[1:], lambda i, me, l=l: (l, 0, 0)) for a, l in zip(ws, layers)]
            + [_ANY],
            out_specs=[pl.BlockSpec((None,) + a.shape[1:], lambda i, me: (me[0], 0, 0)) for a in ws]),
        out_shape=[SDS((N_DEV,) + a.shape[1:], BF16) for a in ws],
        compiler_params=_cp("arbitrary"))(me1, *ws, after)


def _sum8_into_slot(p, me1, name):
    _, r, c = p.shape

    def body(me_ref, p_ref, o_ref):
        del me_ref
        acc = p_ref[0]
        for k in range(1, N_DEV):
            acc = acc + p_ref[k]
        o_ref[...] = acc

    return pl.pallas_call(
        body, name=name,
        grid_spec=pltpu.PrefetchScalarGridSpec(
            num_scalar_prefetch=1, grid=(1,),
            in_specs=[pl.BlockSpec(p.shape, lambda i, me: (0, 0, 0))],
            out_specs=pl.BlockSpec((None, r, c), lambda i, me: (me[0], 0, 0))),
        out_shape=SDS(p.shape, F32), compiler_params=_cp("arbitrary"))(me1, p)


def _adamw(w, g, m, v):
    m = ADAM_B1 * m + (1.0 - ADAM_B1) * g
    v = ADAM_B2 * v + (1.0 - ADAM_B2) * (g * g)
    m_hat = m / (1.0 - ADAM_B1 ** ADAM_STEP)
    v_hat = v / (1.0 - ADAM_B2 ** ADAM_STEP)
    delta = -ADAM_LR * (m_hat / (jnp.sqrt(v_hat) + ADAM_EPS) + ADAM_WD * w)
    return delta, m, v


def _adam_tile(p_ref, w_ref, m_ref, v_ref, g_ref, d_ref, nm_ref, nv_ref):
    g = p_ref[0].astype(F32)
    for k in range(1, p_ref.shape[0]):
        g = g + p_ref[k].astype(F32)
    delta, nm, nv = _adamw(w_ref[...], g, m_ref[...], v_ref[...])
    g_ref[...] = g
    d_ref[...] = delta
    nm_ref[...] = nm
    nv_ref[...] = nv


class _AdamJob:
    def __init__(self, parts, w, m, v, layer, prev):
        self.args = [parts, w, m, v] + list(prev or ())
        self.layer, self.results = layer, None


def _carry(jobs, steps, body, n_in, n_out):
    in_specs, args, out_specs, out_shapes, aliases, n_prevs = [], [], [], [], {}, []
    for j, job in enumerate(jobs):
        _, r, c = job.args[0].shape
        nr = next(n for n in range(steps, 0, -1) if steps % n == 0 and r % (16 * n) == 0)
        nc = steps // nr
        assert c % (128 * nc) == 0
        tile = (r // nr, c // nc)
        blk = pl.BlockSpec((None,) + tile, lambda i, layer=job.layer, nc=nc: (layer, i // nc, i % nc))
        n_prev = len(job.args) - 4
        aliases.update({n_in + len(args) + 4 + k: n_out + 4 * j + k for k in range(n_prev)})
        in_specs += [pl.BlockSpec(job.args[0].shape[:1] + tile, lambda i, nc=nc: (0, i // nc, i % nc)), blk, blk, blk]
        in_specs += [_ANY] * n_prev
        args += job.args
        out_specs += [blk] * 4
        out_shapes += [SDS(job.args[1].shape, F32)] * 4
        n_prevs.append(n_prev)

    def carrying(*refs):
        ins, outs = refs[:n_in + len(args)], refs[n_in + len(args):]
        body(*ins[:n_in], *outs[:n_out])
        k = n_in
        for j, n_prev in enumerate(n_prevs):
            _adam_tile(*ins[k:k + 4], *outs[n_out + 4 * j:n_out + 4 * j + 4])
            k += 4 + n_prev

    return carrying, in_specs, args, out_specs, out_shapes, aliases


def _adam_shard(parts, w, m, v, layer, prev, name):
    n, r, c = parts.shape
    tr = next(t for t in (512, 352, r) if r % t == 0)
    n_prev = 0 if prev is None else 4

    def body(*refs):
        _adam_tile(*refs[:4], *refs[4 + n_prev:])

    blk = pl.BlockSpec((None, tr, c), lambda i: (layer, i, 0))
    return pl.pallas_call(
        body, name=name, grid=(r // tr,),
        in_specs=[pl.BlockSpec((n, tr, c), lambda i: (0, i, 0)), blk, blk, blk] + [_ANY] * n_prev,
        out_specs=[blk] * 4, out_shape=[SDS(w.shape, F32)] * 4,
        input_output_aliases={4 + k: k for k in range(n_prev)},
        compiler_params=_cp("parallel"))(parts, w, m, v, *(prev or ()))


SMALL_MATRICES = [("lru_w_r", 2048), ("lru_w_i", 2048), ("gmlp_w_s", 1024)]
SMALL_VECTORS = [("norm1_g", 8), ("gmlp_ln_g", 8), ("gmlp_ln_b", 8), ("gmlp_b_s", 8), ("conv_w", 32), ("conv_b", 8),
                 ("lru_b_r", 16), ("lru_b_i", 16), ("lru_lambda", 16), ("norm2_g", 8), ("final_g", 8)]
SMALL_VECTOR_ROW0 = sum(n for _, n in SMALL_MATRICES)
SMALL_VECTOR_BLOCK = 256
SMALL_ROWS = SMALL_VECTOR_ROW0 + SMALL_VECTOR_BLOCK
LOSS_ROW = SMALL_VECTOR_ROW0 + sum(n for _, n in SMALL_VECTORS)
assert LOSS_ROW + LANE_ROWS <= SMALL_ROWS


def _pack_small(small):
    parts = [small[k] for k, _ in SMALL_MATRICES]
    parts += [small[k] if k in small else jnp.zeros((n, HD), F32) for k, n in SMALL_VECTORS]
    parts += [small["loss"]] if "loss" in small else []
    flat = jnp.concatenate(parts)
    return jnp.pad(flat, ((0, SMALL_ROWS - flat.shape[0]), (0, 0))).reshape(N_DEV, SMALL_ROWS // N_DEV, HD)


def _adam_matrix(g0, g1, w, m, v, row0, name):
    _, rows, _ = w.shape
    tr = 512

    def body(g0_ref, g1_ref, w_ref, m_ref, v_ref, g_ref, d_ref, nm_ref, nv_ref):
        for l, src in enumerate((g0_ref, g1_ref)):
            g = src[...]
            delta, nm, nv = _adamw(w_ref[l], g, m_ref[l], v_ref[l])
            g_ref[l] = g
            d_ref[l] = delta
            nm_ref[l] = nm
            nv_ref[l] = nv

    gspec = pl.BlockSpec((tr, HD), lambda i: (row0 // tr + i, 0))
    blk = pl.BlockSpec((2, tr, HD), lambda i: (0, i, 0))
    return pl.pallas_call(body, name=name, grid=(rows // tr,), in_specs=[gspec, gspec] + [blk] * 3,
                          out_specs=[blk] * 4, out_shape=[SDS(w.shape, F32)] * 4,
                          compiler_params=_cp("parallel"))(g0, g1, w, m, v)


def _adam_vectors(g0, g1, dg1_parts, me1, ws, ms, vs):
    names = [k for k, _ in SMALL_VECTORS]
    n = len(names)

    def lanes(rows8):
        return jnp.concatenate([rows8[k:k + 1, :] for k in range(LANE_ROWS)], axis=1)

    def body(me_ref, g0_ref, g1_ref, dg1_ref, *refs):
        w_refs, m_refs, v_refs = refs[:n], refs[n:2 * n], refs[2 * n:3 * n]
        outs = refs[3 * n:]
        me = me_ref[0]
        g_refs = (g0_ref, g1_ref)

        def emit(i, idx, g):
            delta, nm, nv = _adamw(w_refs[i][idx], g, m_refs[i][idx], v_refs[i][idx])
            for j, val in enumerate((g, delta, nm, nv)):
                outs[4 * i + j][idx] = val

        off = 0
        for i, (name, rows) in enumerate(SMALL_VECTORS):
            for l in range(2):
                row = (slice(l, l + 1), slice(None))
                if name == "final_g":
                    if l == 1:
                        emit(i, (slice(0, 1), slice(None)), lanes(g1_ref[off:off + rows, :]))
                elif name == "norm1_g":
                    if l == 1:
                        emit(i, row, lanes(g0_ref[off:off + rows, :]))
                    else:
                        total = dg1_ref[0]
                        for k in range(1, N_DEV):
                            total = total + dg1_ref[k]
                        emit(i, row, lanes(total))
                elif name == "gmlp_b_s":
                    emit(i, (l,), g_refs[l][off:off + rows, :])
                elif rows == LANE_ROWS:
                    emit(i, row, lanes(g_refs[l][off:off + rows, :]))
                else:
                    for r in range(rows // LANE_ROWS):
                        emit(i, (l, slice(r, r + 1), slice(None)), g_refs[l][pl.ds(off + r * LANE_ROWS + me, 1), :])
            off += rows

    args = [ws[k] for k in names] + [ms[k] for k in names] + [vs[k] for k in names]
    gspec = pl.BlockSpec((SMALL_VECTOR_BLOCK, HD), lambda i, me: (SMALL_VECTOR_ROW0 // SMALL_VECTOR_BLOCK, 0))
    res = pl.pallas_call(
        body, name="adam_vectors",
        grid_spec=pltpu.PrefetchScalarGridSpec(
            num_scalar_prefetch=1, grid=(1,),
            in_specs=[gspec, gspec, _full(dg1_parts.shape)] + [_full(a.shape) for a in args],
            out_specs=[_full(ws[k].shape) for k in names for _ in range(4)]),
        out_shape=[SDS(ws[k].shape, F32) for k in names for _ in range(4)],
        compiler_params=_cp("arbitrary"))(me1, g0, g1, dg1_parts, *args)
    return {k: list(res[4 * i:4 * i + 4]) for i, k in enumerate(names)}


def _local_step(x, tgt, p, get_w, hook=lambda stage, layer, payload: None):
    s = x.shape[0]
    tm = _row_tile(s)
    wsb = p["gmlp_w_s"].astype(BF16)
    wstb = jnp.swapaxes(p["gmlp_w_s"], -1, -2).astype(BF16)
    bsb = jnp.broadcast_to(p["gmlp_b_s"][..., None], p["gmlp_w_s"].shape)
    wrb = p["lru_w_r"].astype(BF16)
    wib = p["lru_w_i"].astype(BF16)
    saved = []
    for l in range(2):
        win = get_w("w_in", l, x)
        z, h1 = _norm_inproj(x, p["norm1_g"][l][None], win, l, tm, after=[hook("pre_inproj", l, win)])
        a0, b0, a1, b1, xcb, *gates = _lru_gates_fwd(z, p["conv_w"][l], p["conv_b"][l][None], wrb[l], wib[l],
                                                     p["lru_b_r"][l], p["lru_b_i"][l], p["lru_lambda"][l], l, tm)
        h0, hr = _lru_scan(a0, b0, a1, b1, False, l)
        token = hook("pre_gmlp", l, h0)
        wout = get_w("w_out", l, h0 if token is None else token)
        x1, mg = _mixer_fwd(x, h0, hr, z, p["gmlp_ln_g"][l][None], p["gmlp_ln_b"][l][None], wsb[l], bsb[l], wout, l, tm)
        wfi = get_w("w_ffn_in", l, x1)
        wfo = get_w("w_ffn_out", l, x1)
        if l == 0:
            x2, ff, dff, h2 = _ffn_fwd(x1, p["norm2_g"][l][None], wfi, wfo, l, tm)
        else:
            dx, loss, dfg, ff, dff, h2 = _ffn_fwd(x1, p["norm2_g"][l][None], wfi, wfo, l, tm,
                                                  head=(p["final_g"][None], tgt))
        saved.append((x, z, h1, a0, a1, h0, hr, x1, mg, ff, dff, h2, win, wout, wfi, wfo, xcb, gates))
        x = x2
    for l in (1, 0):
        x0, z, h1, a0, a1, h0, hr, x1, mg, ff, dff, h2, win, wout, wfi, wfo, xcb, gates = saved[l]
        dgu, dx1, dg2 = _ffn_bwd(dx, wfo, dff, wfi.reshape(N_DEV, FF_BLK, D), x1, p["norm2_g"][l][None], l, tm)
        d_wfo = _mm_tn(ff, pl.BlockSpec((None, s, FF_BLK), lambda j: (j, 0, 0)), dx, _resident((s, D)),
                       4, (4, FF_BLK, D), pl.BlockSpec((None, FF_BLK, D), lambda j: (j, 0, 0)),
                       f"dw_ffn_out_l{l}", a_is_transposed=False)
        dgu8 = dgu.reshape(N_DEV, s, FF_BLK)
        d_wfi = _mm_tn(dgu8, pl.BlockSpec((None, s, FF_BLK), lambda j: (j, 0, 0)), h2, _resident((s, D)),
                       N_DEV, (N_DEV, FF_BLK, D), pl.BlockSpec((None, FF_BLK, D), lambda j: (j, 0, 0)),
                       f"dw_ffn_in_l{l}", a_is_transposed=False)
        d_wout = _mm_tn(mg, _resident((D, s)), dx1, pl.BlockSpec((s, D // 2), lambda j: (0, j)),
                        2, (D, D), pl.BlockSpec((D, D // 2), lambda j: (0, j)), f"dw_out_l{l}")
        token = hook("ffn_partials", l, dict(w_ffn_out=d_wfo.reshape(N_DEV, D_FF // N_DEV, D), w_ffn_in=d_wfi,
                                             w_out=d_wout.reshape(N_DEV, D // N_DEV, D)))
        pending = hook("mid_backward", l, dx1)
        dz, dh, dws, dbs, dlng, dlnb = _mixer_bwd(dx1, wout, h0, hr, z, p["gmlp_ln_g"][l][None], p["gmlp_ln_b"][l][None],
                                                  wsb[l], wstb[l], bsb[l], l, tm, after=[token])
        g1, g0 = _lru_scan(a1, dh, a0, dh, True, l)
        dxc, dwr, dwi, dbr, dbi, dlam = _lru_gates_bwd(
            xcb, gates, h0, hr, g0, g1, wrb[l], wib[l], p["lru_lambda"][l], l, tm, after=[pending])
        dz, dcw, dcb = _conv_bwd(dz, dxc, z, p["conv_w"][l], l, tm)
        small = dict(lru_w_r=dwr.reshape(-1, HD), lru_w_i=dwi.reshape(-1, HD), gmlp_w_s=dws.reshape(-1, HD),
                     gmlp_ln_g=dlng, gmlp_ln_b=dlnb, gmlp_b_s=dbs, conv_w=dcw, conv_b=dcb, lru_b_r=dbr,
                     lru_b_i=dbi, lru_lambda=dlam, norm2_g=dg2)
        if l == 1:
            small["final_g"] = dfg
            small["loss"] = jnp.broadcast_to(loss, (LANE_ROWS, HD))
        else:
            small["norm1_g"] = dg1
        started = hook("small_grads", l, small)
        d_win = _mm_tn(h1, _resident((D, s)), dz, pl.BlockSpec((s, IN_BLK), lambda j: (0, j)),
                       N_DEV, (N_DEV, D, IN_BLK), pl.BlockSpec((None, D, IN_BLK), lambda j: (j, 0, 0)),
                       f"dw_in_l{l}", after=started, jobs=hook("dw_in", l, started) or ())
        token = hook("mixer_partials", l, dict(w_in=d_win))
        dx, dg1 = _mm_nt_rms_bwd(
            dz, pl.BlockSpec((tm, N_IN), lambda i: (i, 0)),
            lambda r: [r[:, k * IN_BLK:(k + 1) * IN_BLK] for k in range(N_DEV)],
            win, False, x0, p["norm1_g"][l][None], dx1, f"inproj_bwd_dx_l{l}", tm,
            after=[token], jobs=hook("inproj_bwd_dx", l, token) or ())
    return loss, dx, dg1


_REPL = ["norm1_g", "gmlp_ln_g", "gmlp_ln_b", "gmlp_w_s", "gmlp_b_s", "conv_b", "lru_w_r", "lru_w_i", "norm2_g", "final_g"]
_LANE_SHARDED = ["conv_w", "lru_b_r", "lru_b_i", "lru_lambda"]
_BIG = ["w_in", "w_out", "w_ffn_in", "w_ffn_out"]
_ORDER = ["norm1_g", "w_in", "gmlp_ln_g", "gmlp_ln_b", "gmlp_w_s", "gmlp_b_s", "conv_w", "conv_b", "lru_w_r", "lru_b_r",
          "lru_w_i", "lru_b_i", "lru_lambda", "w_out", "norm2_g", "w_ffn_in", "w_ffn_out", "final_g"]


def kernel(x, norm1_g, w_in, gmlp_ln_g, gmlp_ln_b, gmlp_w_s, gmlp_b_s, conv_w, conv_b, lru_w_r, lru_b_r, lru_w_i, lru_b_i, lru_lambda, w_out, norm2_g, w_ffn_in, w_ffn_out, final_g, loss_target, m_norm1_g, m_w_in, m_gmlp_ln_g, m_gmlp_ln_b, m_gmlp_w_s, m_gmlp_b_s, m_conv_w, m_conv_b, m_lru_w_r, m_lru_b_r, m_lru_w_i, m_lru_b_i, m_lru_lambda, m_w_out, m_norm2_g, m_w_ffn_in, m_w_ffn_out, m_final_g, v_norm1_g, v_w_in, v_gmlp_ln_g, v_gmlp_ln_b, v_gmlp_w_s, v_gmlp_b_s, v_conv_w, v_conv_b, v_lru_w_r, v_lru_b_r, v_lru_w_i, v_lru_b_i, v_lru_lambda, v_w_out, v_norm2_g, v_w_ffn_in, v_w_ffn_out, v_final_g):
    w = dict(norm1_g=norm1_g, w_in=w_in, gmlp_ln_g=gmlp_ln_g, gmlp_ln_b=gmlp_ln_b, gmlp_w_s=gmlp_w_s, gmlp_b_s=gmlp_b_s,
             conv_w=conv_w, conv_b=conv_b, lru_w_r=lru_w_r, lru_b_r=lru_b_r, lru_w_i=lru_w_i, lru_b_i=lru_b_i,
             lru_lambda=lru_lambda, w_out=w_out, norm2_g=norm2_g, w_ffn_in=w_ffn_in, w_ffn_out=w_ffn_out, final_g=final_g)
    mom = dict(norm1_g=m_norm1_g, w_in=m_w_in, gmlp_ln_g=m_gmlp_ln_g, gmlp_ln_b=m_gmlp_ln_b, gmlp_w_s=m_gmlp_w_s,
               gmlp_b_s=m_gmlp_b_s, conv_w=m_conv_w, conv_b=m_conv_b, lru_w_r=m_lru_w_r, lru_b_r=m_lru_b_r,
               lru_w_i=m_lru_w_i, lru_b_i=m_lru_b_i, lru_lambda=m_lru_lambda, w_out=m_w_out, norm2_g=m_norm2_g,
               w_ffn_in=m_w_ffn_in, w_ffn_out=m_w_ffn_out, final_g=m_final_g)
    var = dict(norm1_g=v_norm1_g, w_in=v_w_in, gmlp_ln_g=v_gmlp_ln_g, gmlp_ln_b=v_gmlp_ln_b, gmlp_w_s=v_gmlp_w_s,
               gmlp_b_s=v_gmlp_b_s, conv_w=v_conv_w, conv_b=v_conv_b, lru_w_r=v_lru_w_r, lru_b_r=v_lru_b_r,
               lru_w_i=v_lru_w_i, lru_b_i=v_lru_b_i, lru_lambda=v_lru_lambda, w_out=v_w_out, norm2_g=v_norm2_g,
               w_ffn_in=v_w_ffn_in, w_ffn_out=v_w_ffn_out, final_g=v_final_g)
    for src in (w, mom, var):
        src["w_ffn_in"] = jnp.swapaxes(src["w_ffn_in"], 1, 2)
    xi, yi, ci = _me()
    me = 4 * xi + 2 * yi + ci

    lane_shapes = [w[k].shape for k in _LANE_SHARDED]
    lane_rows = sum(a[0] * a[1] for a in lane_shapes)
    packed = jnp.concatenate([w[k].reshape(-1, HD) for k in _LANE_SHARDED])
    packed = jnp.pad(packed, ((0, -lane_rows % 8), (0, 0)))

    me1 = jnp.reshape(me, (1,)).astype(jnp.int32)
    gathers = {}
    exchanges = {}
    views = dict(w_in=(N_DEV, D, IN_BLK), w_out=(D, D), w_ffn_in=(2, 4, FF_BLK, D), w_ffn_out=(4, FF_BLK, D))
    small_ex = {}
    small_ag = {}

    casts = {}

    def start_gather(names, l, after):
        lands = [casts[(k, l)] if (k, l) in casts else _cast_into_slot(w[k], l, me1, f"cast_{k}_l{l}") for k in names]
        started, tok = _gather2_start(lands, after, f"gather_start_{'_'.join(names)}_l{l}")
        gathers.update({(k, l): h for k, h in zip(names, started)})
        return tok

    def relay_gather(names, l, after):
        relayed, tok = _gather2_relay([gathers[(k, l)] for k in names], after, f"gather_relay_{'_'.join(names)}_l{l}")
        gathers.update({(k, l): h for k, h in zip(names, relayed)})
        return tok

    def get_w(k, l, after):
        return _gather2_wait(gathers[(k, l)], after, f"gather_wait_{k}_l{l}").reshape(views[k])

    carried = {("inproj_bwd_dx", 1): [("w_ffn_out", 1), ("w_ffn_in", 1), ("w_out", 1)], ("dw_in", 0): [("w_in", 1)],
               ("inproj_bwd_dx", 0): [("w_ffn_out", 0), ("w_ffn_in", 0), ("w_out", 0)]}
    adam = {}

    def adam_jobs(shards, after):
        for k, l in shards:
            got = _exchange_wait(exchanges[(k, l)], after, f"exchange_wait_{k}_l{l}")
            adam[k] = _AdamJob(got, w[k], mom[k], var[k], l, adam[k].results if k in adam else None)
        return [adam[k] for k, _ in shards]

    def hook(stage, l, payload):
        if stage in ("dw_in", "inproj_bwd_dx"):
            return adam_jobs(carried.get((stage, l), []), payload)
        if stage == "pre_inproj":
            return start_gather(_BIG[1:], l, payload) if l == 1 else None
        if stage == "pre_gmlp":
            tok = relay_gather(_BIG[1:], l, payload)
            return relay_gather(_BIG[:1], l + 1, tok) if l == 0 else tok
        if stage == "small_grads":
            (small_ex[l],), tok = _exchange_start([_pack_small(payload)], f"exchange_start_small_l{l}")
            return tok
        if stage == "mid_backward":
            return reduce_small(l + 1, payload) if l == 0 else None
        if (stage, l) == ("mixer_partials", 0):
            pairs, tok = _pairs_start(payload["w_in"], "pairs_start_w_in_l0")
            p, pair = _pairs_wait(pairs, reduce_small(0, tok), "pairs_wait_w_in_l0")
            sums = _sum_pairs(p, pair, me1, "sum_pairs_w_in_l0")
            exchanges[("w_in", 0)], tok = _chips_start(sums, "chips_start_w_in_l0")
            return tok
        started, tok = _exchange_start(list(payload.values()), f"exchange_start_{'_'.join(payload)}_l{l}")
        exchanges.update({(k, l): h for k, h in zip(payload, started)})
        return tok

    def reduce_small(l, after):
        got = _exchange_wait(small_ex[l], after, f"exchange_wait_small_l{l}")
        mine = _sum8_into_slot(got, me1, f"sum_small_l{l}")
        (small_ag[l],), tok = _gather_start([mine], got, f"gather_start_small_l{l}")
        return tok

    land = lax.dynamic_update_slice(jnp.zeros((N_DEV,) + packed.shape, F32), packed[None], (me, 0, 0))
    token = start_gather(_BIG[:1], 0, x)
    (lanes_handle,), token = _gather_start([land], token, "gather_start_lanes")
    later = [(k, l) for l in range(2) for k in _BIG if (k, l) != ("w_in", 0)]
    casts.update(zip(later, _cast_all_into_slots([w[k] for k, _ in later], [l for _, l in later], me1, token,
                                                 "cast_later_weights")))
    token = start_gather(_BIG[:1], 1, start_gather(_BIG[1:], 0, casts[later[0]]))
    token = relay_gather(_BIG[:1], 0, token)
    lanes = _gather_wait(lanes_handle, token, "gather_wait_lanes")
    params = {k: w[k] for k in _REPL}
    off = 0
    for k, shp in zip(_LANE_SHARDED, lane_shapes):
        n = shp[0] * shp[1]
        params[k] = jnp.swapaxes(lanes[:, off:off + n], 0, 1).reshape(shp[0], shp[1], D)
        off += n
    _, dx, dg1 = _local_step(x[0], loss_target[0], params, get_w, hook)

    out = {k: job.results for k, job in adam.items()}
    after = dx
    g_small = [_gather_wait(small_ag[l], after, f"gather_wait_small_l{l}").reshape(SMALL_ROWS, HD) for l in (0, 1)]
    row0 = 0
    for k, rows in SMALL_MATRICES:
        res = _adam_matrix(*g_small, *[src[k].reshape(2, rows, HD) for src in (w, mom, var)], row0, f"adam_{k}")
        out[k] = [a.reshape(w[k].shape) for a in res]
        after = res[3]
        row0 += rows
    got = _chips_wait(exchanges[("w_in", 0)], after, "chips_wait_w_in_l0")
    out["w_in"] = _adam_shard(got, w["w_in"], mom["w_in"], var["w_in"], 0, out["w_in"], "adam_w_in_l0")
    out["w_ffn_in"] = [jnp.swapaxes(a, 1, 2) for a in out["w_ffn_in"]]
    as_rows = lambda a: a.reshape(1, D) if a.ndim == 1 else a
    vec = _adam_vectors(*g_small, _all_gather(dg1, out["w_in"][3], "gather_norm1_grad"), me1,
                        *[{k: as_rows(src[k]) for k, _ in SMALL_VECTORS} for src in (w, mom, var)])
    out.update({k: [a.reshape(w[k].shape) for a in res] for k, res in vec.items()})

    return (g_small[1][LOSS_ROW, 0], dx[None], *[out[k][0] for k in _ORDER], *[out[k][1] for k in _ORDER],
            *[out[k][2] for k in _ORDER], *[out[k][3] for k in _ORDER])
```

```python
import jax
import jax.numpy as jnp
from jax import lax
from jax.experimental import pallas as pl
from jax.experimental.pallas import tpu as pltpu

F32 = jnp.float32
BF16 = jnp.bfloat16
SDS = jax.ShapeDtypeStruct

D = 1024
N_IN = 6 * D
D_FF = 2816
N_DEV = 8
IN_BLK = N_IN // N_DEV
FF_BLK = 2 * D_FF // N_DEV
HEADS = 8
HD = 128
EPS = 1e-6
LRU_C = 8.0

ADAM_LR = 0.001
ADAM_B1 = 0.9
ADAM_B2 = 0.999
ADAM_EPS = 1e-08
ADAM_WD = 0.01
ADAM_STEP = 10

VMEM_LIMIT = 60 * 2**20


def _cp(*sem, **kw):
    return pltpu.CompilerParams(dimension_semantics=sem, vmem_limit_bytes=VMEM_LIMIT, **kw)


def _row_tile(s):
    return 512 if s >= 1024 else s // 2


_GELU_C = 0.7978845608028654


def _gelu(x):
    t = jnp.tanh(_GELU_C * (x + 0.044715 * (x * x * x)))
    return 0.5 * x * (1.0 + t), t


def _gelu_grad(x, t):
    return 0.5 * (1.0 + t) + 0.5 * x * (1.0 - t * t) * (_GELU_C * (1.0 + 0.134145 * (x * x)))


def _sigmoid(x):
    return 0.5 + 0.5 * jnp.tanh(0.5 * x)


def _softplus(x):
    e = jnp.exp(-jnp.abs(x))
    w = 1.0 + e
    l1p = jnp.where(w == 1.0, e, jnp.log(w) * e / jnp.where(w == 1.0, 1.0, w - 1.0))
    return jnp.maximum(x, 0.0) + l1p


def _rms_fwd(x, g):
    r = lax.rsqrt(jnp.mean(x * x, axis=-1, keepdims=True) + EPS)
    return x * r * g


def _rms_bwd(x, g, dh):
    r = lax.rsqrt(jnp.mean(x * x, axis=-1, keepdims=True) + EPS)
    xh = x * r
    dxh = dh * g
    dx = r * (dxh - xh * jnp.mean(dxh * xh, axis=-1, keepdims=True))
    dg = jnp.sum(dh * xh, axis=0, keepdims=True)
    return dx, dg


LANE_ROWS = D // HD


def _add_rows128(ref, vec, row0=0):
    for i in range(vec.shape[0]):
        for k in range(LANE_ROWS):
            j = row0 + i * LANE_ROWS + k
            ref[j:j + 1, :] += vec[i:i + 1, k * HD:(k + 1) * HD]


def _dot(a, b):
    return jnp.dot(a, b, preferred_element_type=F32)


def _dot_nt(a, b):
    return lax.dot_general(a, b, (((1,), (1,)), ((), ())), preferred_element_type=F32)


def _dot_tn(a, b):
    return lax.dot_general(a, b, (((0,), (0,)), ((), ())), preferred_element_type=F32)


def _taps(prev, cur, nxt, tm):
    hr = prev.shape[0]
    ext = jnp.concatenate([prev, cur, nxt], axis=0)
    n = tm + 2 * hr
    sl = slice(hr, hr + tm)
    return (pltpu.roll(ext, 2, 0)[sl], pltpu.roll(ext, 1, 0)[sl], cur,
            pltpu.roll(ext, n - 1, 0)[sl], pltpu.roll(ext, n - 2, 0)[sl])


def _halo_specs(tm, s, col, rows=8):
    nb = s // rows
    r = tm // rows
    return (pl.BlockSpec((rows, D), lambda i: (jnp.maximum(i * r - 1, 0), col)),
            pl.BlockSpec((tm, D), lambda i: (i, col)),
            pl.BlockSpec((rows, D), lambda i: (jnp.minimum((i + 1) * r, nb - 1), col)))


def _halo_load(prev_ref, cur_ref, next_ref, fp, fn):
    return prev_ref[...].astype(F32) * fp, cur_ref[...].astype(F32), next_ref[...].astype(F32) * fn


def _halo_flags(nt):
    i = pl.program_id(0)
    return (i > 0).astype(F32), (i < nt - 1).astype(F32)


def _full(shape):
    nd = len(shape)
    return pl.BlockSpec(shape, lambda *_: (0,) * nd)


def _resident(shape):
    nd = len(shape)
    return pl.BlockSpec(shape, lambda *_: (0,) * nd, pipeline_mode=pl.Buffered(1))


def _behind(tokens, body, n_in):
    deps = [t for t in tokens if t is not None]

    def ordered(*refs):
        body(*refs[:n_in], *refs[n_in + len(deps):])

    return ordered, [_ANY] * len(deps), deps


def _norm_inproj(x, g, w, layer, tm, after=()):
    s = x.shape[0]

    def body(x_ref, g_ref, w_ref, z_ref, ht_ref):
        h32 = _rms_fwd(x_ref[...], g_ref[...])
        ht_ref[...] = h32.T.astype(BF16)
        h = h32.astype(BF16)
        for j in range(N_DEV):
            z_ref[:, j * IN_BLK:(j + 1) * IN_BLK] = _dot(h, w_ref[j]).astype(BF16)

    body, dep_specs, deps = _behind(after, body, 3)
    return pl.pallas_call(
        body, name=f"norm_inproj_l{layer}", grid=(s // tm,),
        in_specs=[pl.BlockSpec((tm, D), lambda i: (i, 0)), _full((1, D)), _resident((N_DEV, D, IN_BLK))] + dep_specs,
        out_specs=[pl.BlockSpec((tm, N_IN), lambda i: (i, 0)), pl.BlockSpec((D, tm), lambda i: (0, i))],
        out_shape=[SDS((s, N_IN), BF16), SDS((D, s), BF16)],
        compiler_params=_cp("parallel"))(x, g, w, *deps)


def _gmlp_values(zu_ref, zv_ref, lng_ref, lnb_ref):
    zu = zu_ref[...].astype(F32)
    zv = zv_ref[...].astype(F32)
    u, tu = _gelu(zu)
    gv, tv = _gelu(zv)
    xc = gv - jnp.mean(gv, axis=-1, keepdims=True)
    rstd = lax.rsqrt(jnp.mean(xc * xc, axis=-1, keepdims=True) + EPS)
    xh = xc * rstd
    vb = (xh * lng_ref[...] + lnb_ref[...]).astype(BF16)
    return zu, zv, u, tu, tv, xh, rstd, vb


def _mixer_fwd(x, h0, h1, z, lng, lnb, ws, bsb, wo, layer, tm):
    s = x.shape[0]

    def body(x_ref, h0_ref, h1_ref, zu_ref, zv_ref, zg_ref, za_ref, zb_ref, lng_ref, lnb_ref, ws_ref, bsb_ref,
             wo_ref, x1_ref, mg_ref, ya_s):
        _, _, u, _, _, _, _, vb = _gmlp_values(zu_ref, zv_ref, lng_ref, lnb_ref)
        for c in range(tm // HD):
            rs = slice(c * HD, (c + 1) * HD)
            for g in range(HEADS):
                cs = slice(g * HD, (g + 1) * HD)
                ya_s[rs, cs] = u[rs, cs] * (_dot(ws_ref[g], vb[rs, cs]) + bsb_ref[g])
        gg, _ = _gelu(zg_ref[...].astype(F32))
        yb = (h0_ref[...] + h1_ref[...]) * gg
        m32 = _sigmoid(za_ref[...].astype(F32)) * ya_s[...] + _sigmoid(zb_ref[...].astype(F32)) * yb
        mg_ref[...] = m32.T.astype(BF16)
        x1_ref[...] = x_ref[...] + _dot(m32.astype(BF16), wo_ref[...])

    tile = pl.BlockSpec((tm, D), lambda i: (i, 0))
    wspec = _full((HEADS, HD, HD))
    return pl.pallas_call(
        body, name=f"mixer_fwd_l{layer}", grid=(s // tm,),
        in_specs=[tile, tile, tile] + [pl.BlockSpec((tm, D), lambda i, c=c: (i, c)) for c in (0, 1, 3, 4, 5)]
        + [_full((1, D)), _full((1, D)), wspec, wspec, _full((D, D))],
        out_specs=[tile, pl.BlockSpec((D, tm), lambda i: (0, i))], out_shape=[SDS((s, D), F32), SDS((D, s), BF16)],
        scratch_shapes=[pltpu.VMEM((tm, D), F32)],
        compiler_params=_cp("parallel"))(x, h0, h1, z, z, z, z, z, lng, lnb, ws, bsb, wo)


def _conv(taps, cw_ref, cb_ref):
    _, m1, c0, p1, p2 = taps
    return cb_ref[...] + m1 * cw_ref[0:1, :] + c0 * cw_ref[1:2, :] + p1 * cw_ref[2:3, :] + p2 * cw_ref[3:4, :]


def _heads_dot(xb, w_ref, d):
    return jnp.concatenate([_dot(xb[:, h * HD:(h + 1) * HD], w_ref[d, h]) for h in range(HEADS)], axis=1)


def _lru_decay(r, sp):
    la = (-LRU_C) * r * sp
    a = jnp.exp(la)
    return a, jnp.tanh(-la) * (a * a + 1.0)


def _lru_gates_fwd(z, cw, cb, wr, wi, br, bi, lam, layer, tm):
    s = z.shape[0]
    nt = s // tm

    def body(zp_ref, zc_ref, zn_ref, cw_ref, cb_ref, wr_ref, wi_ref, br_ref, bi_ref, lam_ref,
             a0_ref, b0_ref, a1_ref, b1_ref, xc_ref, r0_ref, i0_ref, r1_ref, i1_ref):
        fp, fn = _halo_flags(nt)
        xc = _conv(_taps(*_halo_load(zp_ref, zc_ref, zn_ref, fp, fn), tm), cw_ref, cb_ref)
        xb = xc.astype(BF16)
        xc_ref[...] = xb
        for d, (a_ref, b_ref, r_ref, i_ref) in enumerate(((a0_ref, b0_ref, r0_ref, i0_ref),
                                                          (a1_ref, b1_ref, r1_ref, i1_ref))):
            r = _sigmoid(_heads_dot(xb, wr_ref, d) + br_ref[d:d + 1, :])
            ig = _sigmoid(_heads_dot(xb, wi_ref, d) + bi_ref[d:d + 1, :])
            a, q = _lru_decay(r, _softplus(-lam_ref[d:d + 1, :]))
            a_ref[...] = a
            b_ref[...] = jnp.sqrt(q) * (ig * xc)
            r_ref[...] = r.astype(BF16)
            i_ref[...] = ig.astype(BF16)

    tile = pl.BlockSpec((tm, D), lambda i: (i, 0))
    return pl.pallas_call(
        body, name=f"lru_gates_fwd_l{layer}", grid=(nt,),
        in_specs=[*_halo_specs(tm, s, 2, 16), _full((4, D)), _full((1, D)),
                  _full((2, HEADS, HD, HD)), _full((2, HEADS, HD, HD)), _full((2, D)), _full((2, D)), _full((2, D))],
        out_specs=[tile] * 9, out_shape=[SDS((s, D), F32)] * 4 + [SDS((s, D), BF16)] * 5,
        compiler_params=_cp("parallel"))(z, z, z, cw, cb, wr, wi, br, bi, lam)


def _scan_group(a, x, c, reverse, bwd):
    row = lax.broadcasted_iota(jnp.int32, a.shape, 0)
    b = a * x if bwd else x
    for d in (1, 2, 4):
        keep = (row < 8 - d) if reverse else (row >= d)
        sh = 8 - d if reverse else d
        a_s = jnp.where(keep, pltpu.roll(a, sh, 0), 1.0)
        b_s = jnp.where(keep, pltpu.roll(b, sh, 0), 0.0)
        b = a * b_s + b
        a = a * a_s
    h = b + a * c
    new_c = h[0:1, :] if reverse else h[7:8, :]
    if not bwd:
        return h, new_c
    if reverse:
        prev = jnp.where(row < 7, pltpu.roll(h, 7, 0), c)
    else:
        prev = jnp.where(row >= 1, pltpu.roll(h, 1, 0), c)
    return x + prev, new_c


def _lru_scan(a_f, x_f, a_r, x_r, bwd, layer):
    s = a_f.shape[0]
    ts = min(1024, s // 2)
    cb = 512
    nt = s // ts
    ng = ts // 8

    def body(af_ref, xf_ref, ar_ref, xr_ref, of_ref, or_ref, cf, cr):
        @pl.when(pl.program_id(1) == 0)
        def _():
            cf[...] = jnp.zeros_like(cf)
            cr[...] = jnp.zeros_like(cr)

        def step(j, carry):
            c_f, c_r = carry
            rf = pl.multiple_of(j * 8, 8)
            rr = pl.multiple_of((ng - 1 - j) * 8, 8)
            o, c_f = _scan_group(af_ref[pl.ds(rf, 8), :], xf_ref[pl.ds(rf, 8), :], c_f, False, bwd)
            of_ref[pl.ds(rf, 8), :] = o
            o, c_r = _scan_group(ar_ref[pl.ds(rr, 8), :], xr_ref[pl.ds(rr, 8), :], c_r, True, bwd)
            or_ref[pl.ds(rr, 8), :] = o
            return c_f, c_r

        c_f, c_r = lax.fori_loop(0, ng, step, (cf[0:1, :], cr[0:1, :]), unroll=2)
        cf[...] = jnp.broadcast_to(c_f, cf.shape)
        cr[...] = jnp.broadcast_to(c_r, cr.shape)

    fwd = pl.BlockSpec((ts, cb), lambda c, t: (t, c))
    rev = pl.BlockSpec((ts, cb), lambda c, t: (nt - 1 - t, c))
    return pl.pallas_call(
        body, name=f"lru_scan_{'bwd' if bwd else 'fwd'}_l{layer}", grid=(D // cb, nt),
        in_specs=[fwd, fwd, rev, rev], out_specs=[fwd, rev],
        out_shape=[SDS((s, D), F32)] * 2,
        scratch_shapes=[pltpu.VMEM((8, cb), F32), pltpu.VMEM((8, cb), F32)],
        compiler_params=_cp("parallel", "arbitrary"))(a_f, x_f, a_r, x_r)


def _ffn_fwd(x1, g, wfi, wfo, layer, tm, head=None):
    s = x1.shape[0]

    def ffn(x_ref, g_ref, wi_ref, wo_ref, ff_ref, dff_ref, h_ref):
        x = x_ref[...]
        h = _rms_fwd(x, g_ref[...]).astype(BF16)
        h_ref[...] = h
        acc = x
        for k in range(4):
            gate = _dot_nt(h, wi_ref[0, k])
            up = _dot_nt(h, wi_ref[1, k])
            sg = _sigmoid(gate)
            silu = gate * sg
            ff = (silu * up).astype(BF16)
            ff_ref[k] = ff
            dff_ref[0, k] = (up * (sg * (1.0 + gate * (1.0 - sg)))).astype(BF16)
            dff_ref[1, k] = silu.astype(BF16)
            acc = acc + _dot(ff, wo_ref[k])
        return acc

    def body(x_ref, g_ref, wi_ref, wo_ref, x2_ref, ff_ref, dff_ref, h_ref):
        x2_ref[...] = ffn(x_ref, g_ref, wi_ref, wo_ref, ff_ref, dff_ref, h_ref)

    def body_with_head(x_ref, g_ref, wi_ref, wo_ref, fg_ref, t_ref, dx_ref, loss_ref, dfg_ref, ff_ref, dff_ref, h_ref):
        @pl.when(pl.program_id(0) == 0)
        def _():
            loss_ref[...] = jnp.zeros_like(loss_ref)
            dfg_ref[...] = jnp.zeros_like(dfg_ref)

        x2 = ffn(x_ref, g_ref, wi_ref, wo_ref, ff_ref, dff_ref, h_ref)
        fg = fg_ref[...]
        e = _rms_fwd(x2, fg) - t_ref[...]
        rows = jnp.sum(e * e, axis=-1, keepdims=True)
        loss_ref[...] += (0.5 / D) * jnp.sum(rows, axis=0, keepdims=True)
        dx, dg = _rms_bwd(x2, fg, e * (1.0 / D))
        dx_ref[...] = dx
        _add_rows128(dfg_ref, dg)

    tile = pl.BlockSpec((tm, D), lambda i: (i, 0))
    weights = [_resident((2, 4, FF_BLK, D)), _resident((4, FF_BLK, D))]
    kept_specs = [pl.BlockSpec((4, tm, FF_BLK), lambda i: (0, i, 0)),
                  pl.BlockSpec((2, 4, tm, FF_BLK), lambda i: (0, 0, i, 0)), tile]
    kept_shapes = [SDS((4, s, FF_BLK), BF16), SDS((2, 4, s, FF_BLK), BF16), SDS((s, D), BF16)]
    if head is None:
        return pl.pallas_call(
            body, name=f"ffn_fwd_l{layer}", grid=(s // tm,),
            in_specs=[tile, _full((1, D))] + weights, out_specs=[tile] + kept_specs,
            out_shape=[SDS((s, D), F32)] + kept_shapes, compiler_params=_cp("parallel"))(x1, g, wfi, wfo)
    final_g, tgt = head
    return pl.pallas_call(
        body_with_head, name=f"ffn_fwd_loss_l{layer}", grid=(s // tm,),
        in_specs=[tile, _full((1, D))] + weights + [_full((1, D)), tile],
        out_specs=[tile, _full((1, 1)), _full((LANE_ROWS, HD))] + kept_specs,
        out_shape=[SDS((s, D), F32), SDS((1, 1), F32), SDS((LANE_ROWS, HD), F32)] + kept_shapes,
        compiler_params=_cp("arbitrary"))(x1, g, wfi, wfo, final_g, tgt)


def _ffn_bwd(dx2, wfo, factors, wfi, x1, g, layer, tm):
    s = dx2.shape[0]

    def body(dx_ref, wo_ref, f_ref, wi_ref, x_ref, g_ref, dgu_ref, dx1_ref, dg_ref):
        @pl.when(pl.program_id(0) == 0)
        def _():
            dg_ref[...] = jnp.zeros_like(dg_ref)

        dx = dx_ref[...]
        dxb = dx.astype(BF16)
        dh = None
        for k in range(4):
            dff = _dot_nt(dxb, wo_ref[k])
            d_gate = (dff * f_ref[0, k].astype(F32)).astype(BF16)
            d_up = (dff * f_ref[1, k].astype(F32)).astype(BF16)
            dgu_ref[0, k] = d_gate
            dgu_ref[1, k] = d_up
            part = _dot(d_gate, wi_ref[k]) + _dot(d_up, wi_ref[4 + k])
            dh = part if dh is None else dh + part
        dxn, dg = _rms_bwd(x_ref[...], g_ref[...], dh)
        dx1_ref[...] = dx + dxn
        _add_rows128(dg_ref, dg)

    tile = pl.BlockSpec((tm, D), lambda i: (i, 0))
    blk = pl.BlockSpec((2, 4, tm, FF_BLK), lambda i: (0, 0, i, 0))
    return pl.pallas_call(
        body, name=f"ffn_bwd_l{layer}", grid=(s // tm,),
        in_specs=[tile, _resident((4, FF_BLK, D)), blk, _resident((N_DEV, FF_BLK, D)), tile, _full((1, D))],
        out_specs=[blk, tile, _full((LANE_ROWS, HD))],
        out_shape=[SDS((2, 4, s, FF_BLK), BF16), SDS((s, D), F32), SDS((LANE_ROWS, HD), F32)],
        compiler_params=_cp("arbitrary"))(dx2, wfo, factors, wfi, x1, g)


def _mm_nt_rms_bwd(a, a_spec, a_blocks, w, w_is_transposed, x, g, dres, name, tm, after=(), jobs=()):
    s = x.shape[0]

    def body(a_ref, w_ref, x_ref, g_ref, dres_ref, dx_ref, dg_ref):
        @pl.when(pl.program_id(0) == 0)
        def _():
            dg_ref[...] = jnp.zeros_like(dg_ref)

        dh = None
        for k, blk in enumerate(a_blocks(a_ref)):
            part = _dot(blk, w_ref[k]) if w_is_transposed else _dot_nt(blk, w_ref[k])
            dh = part if dh is None else dh + part
        dx, dg = _rms_bwd(x_ref[...], g_ref[...], dh)
        dx_ref[...] = dres_ref[...] + dx
        _add_rows128(dg_ref, dg)

    tile = pl.BlockSpec((tm, D), lambda i: (i, 0))
    body, dep_specs, deps = _behind(after, body, 5)
    body, job_in, job_args, job_out, job_shapes, aliases = _carry(jobs, s // tm, body, 5 + len(deps), 2)
    res = pl.pallas_call(
        body, name=name, grid=(s // tm,),
        in_specs=[a_spec, _resident(w.shape), tile, _full((1, D)), tile] + dep_specs + job_in,
        out_specs=[tile, _full((LANE_ROWS, HD))] + job_out,
        out_shape=[SDS((s, D), F32), SDS((LANE_ROWS, HD), F32)] + job_shapes,
        input_output_aliases=aliases, compiler_params=_cp("arbitrary"))(a, w, x, g, dres, *deps, *job_args)
    for j, job in enumerate(jobs):
        job.results = res[2 + 4 * j:6 + 4 * j]
    return res[0], res[1]


def _mm_tn(a, a_spec, b, b_spec, nb, out_shape, out_spec, name, a_is_transposed=True, after=None, jobs=()):
    def body(a_ref, b_ref, *rest):
        o_ref = rest[-1]
        bb = b_ref[...].astype(BF16)
        o_ref[...] = (_dot(a_ref[...], bb) if a_is_transposed else _dot_tn(a_ref[...], bb)).astype(BF16)

    deps = [] if after is None else [after]
    body, job_in, job_args, job_out, job_shapes, aliases = _carry(jobs, nb, body, 2 + len(deps), 1)
    res = pl.pallas_call(
        body, name=name, grid=(nb,), in_specs=[a_spec, b_spec] + [_ANY] * len(deps) + job_in,
        out_specs=[out_spec] + job_out, out_shape=[SDS(out_shape, BF16)] + job_shapes,
        input_output_aliases=aliases, compiler_params=_cp("parallel"))(a, b, *deps, *job_args)
    for j, job in enumerate(jobs):
        job.results = res[1 + 4 * j:5 + 4 * j]
    return res[0]


def _mixer_bwd(dx1, wo, h0, h1, z, lng, lnb, ws, wst, bsb, layer, tm, after=()):
    s = dx1.shape[0]
    nt = s // tm

    def body(dx_ref, wo_ref, h0_ref, h1_ref, zu_ref, zv_ref, zg_ref, za_ref, zb_ref, lng_ref, lnb_ref,
             ws_ref, wst_ref, bsb_ref, dz_ref, dh_ref, dws_ref, dbs_ref, dlng_ref, dlnb_ref,
             du_s, dv_s, ya_s, dbs_acc):
        i = pl.program_id(0)

        @pl.when(i == 0)
        def _():
            for r in (dws_ref, dlng_ref, dlnb_ref, dbs_acc):
                r[...] = jnp.zeros_like(r)

        dm = _dot_nt(dx_ref[...].astype(BF16), wo_ref[...])
        sa = _sigmoid(za_ref[...].astype(F32))
        sb = _sigmoid(zb_ref[...].astype(F32))
        zg = zg_ref[...].astype(F32)
        gg, tg = _gelu(zg)
        hs = h0_ref[...] + h1_ref[...]
        dyb = dm * sb
        dya = dm * sa
        dh_ref[...] = dyb * gg
        dz_ref[:, 2 * D:3 * D] = jnp.zeros((tm, D), BF16)
        dz_ref[:, 3 * D:4 * D] = (dyb * hs * _gelu_grad(zg, tg)).astype(BF16)
        dz_ref[:, 5 * D:6 * D] = (dm * (hs * gg) * (sb * (1.0 - sb))).astype(BF16)

        zu, zv, u, tu, tv, xh, rstd, vb = _gmlp_values(zu_ref, zv_ref, lng_ref, lnb_ref)
        for c in range(tm // HD):
            rs = slice(c * HD, (c + 1) * HD)
            for g in range(HEADS):
                cs = slice(g * HD, (g + 1) * HD)
                vblk = vb[rs, cs]
                mixed = _dot(ws_ref[g], vblk) + bsb_ref[g]
                ya_s[rs, cs] = u[rs, cs] * mixed
                du_s[rs, cs] = dya[rs, cs] * mixed
                dmx = dya[rs, cs] * u[rs, cs]
                dbs_acc[g] += dmx
                dmxb = dmx.astype(BF16)
                dws_ref[g] += _dot_nt(dmxb, vblk)
                dv_s[rs, cs] = _dot(wst_ref[g], dmxb)
        dz_ref[:, 4 * D:5 * D] = (dm * ya_s[...] * (sa * (1.0 - sa))).astype(BF16)
        dv = dv_s[...]
        _add_rows128(dlng_ref, jnp.sum(dv * xh, axis=0, keepdims=True))
        _add_rows128(dlnb_ref, jnp.sum(dv, axis=0, keepdims=True))
        dxh = dv * lng_ref[...]
        dgv = rstd * (dxh - jnp.mean(dxh, axis=-1, keepdims=True)
                      - xh * jnp.mean(dxh * xh, axis=-1, keepdims=True))
        dz_ref[:, 0:D] = (du_s[...] * _gelu_grad(zu, tu)).astype(BF16)
        dz_ref[:, D:2 * D] = (dgv * _gelu_grad(zv, tv)).astype(BF16)

        @pl.when(i == nt - 1)
        def _():
            for g in range(HEADS):
                dbs_ref[g:g + 1, :] = jnp.sum(dbs_acc[g].T, axis=0, keepdims=True)

    tile = pl.BlockSpec((tm, D), lambda i: (i, 0))
    wspec = _full((HEADS, HD, HD))
    body, dep_specs, deps = _behind(after, body, 14)
    return pl.pallas_call(
        body, name=f"mixer_bwd_l{layer}", grid=(nt,),
        in_specs=[tile, _full((D, D)), tile, tile]
        + [pl.BlockSpec((tm, D), lambda i, c=c: (i, c)) for c in (0, 1, 3, 4, 5)]
        + [_full((1, D)), _full((1, D)), wspec, wspec, wspec] + dep_specs,
        out_specs=[pl.BlockSpec((tm, N_IN), lambda i: (i, 0)), tile, wspec, _full((HEADS, HD)),
                   _full((LANE_ROWS, HD)), _full((LANE_ROWS, HD))],
        out_shape=[SDS((s, N_IN), BF16), SDS((s, D), F32), SDS((HEADS, HD, HD), F32), SDS((HEADS, HD), F32),
                   SDS((LANE_ROWS, HD), F32), SDS((LANE_ROWS, HD), F32)],
        scratch_shapes=[pltpu.VMEM((tm, D), F32)] * 3 + [pltpu.VMEM((HEADS, HD, HD), F32)],
        compiler_params=_cp("arbitrary"))(dx1, wo, h0, h1, z, z, z, z, z, lng, lnb, ws, wst, bsb, *deps)


def _lru_gates_bwd(xcb, gates, h0, h1, g0, g1, wr, wi, lam, layer, tm, after=()):
    s = xcb.shape[0]
    nt = s // tm

    def body(xc_ref, r0_ref, i0_ref, r1_ref, i1_ref, h0p_ref, h0_ref, h1_ref, h1n_ref, g0_ref, g1_ref,
             wr_ref, wi_ref, lam_ref, dxc_ref, dwr_ref, dwi_ref, dbr_ref, dbi_ref, dlam_ref):
        i = pl.program_id(0)
        fp, fn = _halo_flags(nt)

        @pl.when(i == 0)
        def _():
            for r in (dwr_ref, dwi_ref, dbr_ref, dbi_ref, dlam_ref):
                r[...] = jnp.zeros_like(r)

        xb = xc_ref[...]
        xc = xb.astype(F32)
        zeros8 = jnp.zeros((8, D), F32)
        h_prev = _taps(h0p_ref[...] * fp, h0_ref[...], zeros8, tm)[1]
        h_next = _taps(zeros8, h1_ref[...], h1n_ref[...] * fn, tm)[3]
        dxc = jnp.zeros((tm, D), F32)
        for d, (g_ref, hsh, r_ref, i_ref) in enumerate(((g0_ref, h_prev, r0_ref, i0_ref),
                                                        (g1_ref, h_next, r1_ref, i1_ref))):
            sp = _softplus(-lam_ref[d:d + 1, :])
            r = r_ref[...].astype(F32)
            ig = i_ref[...].astype(F32)
            a, q = _lru_decay(r, sp)
            rmult = jnp.where(q > 0.0, lax.rsqrt(jnp.where(q > 0.0, q, 1.0)), 0.0)
            mult = q * rmult
            db = g_ref[...]
            da = db * hsh
            dmult = db * (ig * xc)
            di = db * (mult * xc)
            dxc = dxc + db * (mult * ig)
            dla = da * a - dmult * (a * a * rmult)
            dsp_dlam = -_sigmoid(-lam_ref[d:d + 1, :])
            _add_rows128(dlam_ref, jnp.sum(dla * r, axis=0, keepdims=True) * ((-LRU_C) * dsp_dlam), d * LANE_ROWS)
            dpr = dla * sp * (-LRU_C) * (r * (1.0 - r))
            dpi = di * (ig * (1.0 - ig))
            _add_rows128(dbr_ref, jnp.sum(dpr, axis=0, keepdims=True), d * LANE_ROWS)
            _add_rows128(dbi_ref, jnp.sum(dpi, axis=0, keepdims=True), d * LANE_ROWS)
            dprb = dpr.astype(BF16)
            dpib = dpi.astype(BF16)
            parts = []
            for h in range(HEADS):
                cs = slice(h * HD, (h + 1) * HD)
                dwr_ref[d, h] += _dot_tn(xb[:, cs], dprb[:, cs])
                dwi_ref[d, h] += _dot_tn(xb[:, cs], dpib[:, cs])
                parts.append(_dot_nt(dprb[:, cs], wr_ref[d, h]) + _dot_nt(dpib[:, cs], wi_ref[d, h]))
            dxc = dxc + jnp.concatenate(parts, axis=1)
        dxc_ref[...] = dxc.astype(BF16)

    tile = pl.BlockSpec((tm, D), lambda i: (i, 0))
    hp, hc, hn = _halo_specs(tm, s, 0)
    wspec = _full((2, HEADS, HD, HD))
    vspec = _full((2 * LANE_ROWS, HD))
    body, dep_specs, deps = _behind(after, body, 14)
    return pl.pallas_call(
        body, name=f"lru_gates_bwd_l{layer}", grid=(nt,),
        in_specs=[tile] * 5 + [hp, hc, hc, hn, tile, tile, wspec, wspec, _full((2, D))] + dep_specs,
        out_specs=[tile, wspec, wspec, vspec, vspec, vspec],
        out_shape=[SDS((s, D), BF16), SDS((2, HEADS, HD, HD), F32), SDS((2, HEADS, HD, HD), F32)]
        + [SDS((2 * LANE_ROWS, HD), F32)] * 3,
        compiler_params=_cp("arbitrary"))(xcb, *gates, h0, h0, h1, h1, g0, g1, wr, wi, lam, *deps)


def _conv_bwd(dz, dxc, z, cw, layer, tm):
    s = z.shape[0]
    nt = s // tm

    def body(dz_in, dp_ref, dc_ref, dn_ref, zp_ref, zc_ref, zn_ref, cw_ref, dz_ref, dcw_ref, dcb_ref):
        del dz_in
        fp, fn = _halo_flags(nt)

        @pl.when(pl.program_id(0) == 0)
        def _():
            dcw_ref[...] = jnp.zeros_like(dcw_ref)
            dcb_ref[...] = jnp.zeros_like(dcb_ref)

        dxc_halo = _halo_load(dp_ref, dc_ref, dn_ref, fp, fn)
        dxc = dxc_halo[1]
        dm2, dm1, _, dp1, _ = _taps(*dxc_halo, tm)
        dz_ref[...] = (cw_ref[0:1, :] * dp1 + cw_ref[1:2, :] * dxc + cw_ref[2:3, :] * dm1
                       + cw_ref[3:4, :] * dm2).astype(BF16)
        _, zm1, z0, zp1, zp2 = _taps(*_halo_load(zp_ref, zc_ref, zn_ref, fp, fn), tm)
        for k, zt in enumerate((zm1, z0, zp1, zp2)):
            _add_rows128(dcw_ref, jnp.sum(dxc * zt, axis=0, keepdims=True), k * LANE_ROWS)
        _add_rows128(dcb_ref, jnp.sum(dxc, axis=0, keepdims=True))

    return pl.pallas_call(
        body, name=f"conv_bwd_l{layer}", grid=(nt,),
        in_specs=[pl.BlockSpec(memory_space=pl.ANY), *_halo_specs(tm, s, 0, 16), *_halo_specs(tm, s, 2, 16),
                  _full((4, D))],
        out_specs=[pl.BlockSpec((tm, D), lambda i: (i, 2)), _full((4 * LANE_ROWS, HD)), _full((LANE_ROWS, HD))],
        out_shape=[SDS((s, N_IN), BF16), SDS((4 * LANE_ROWS, HD), F32), SDS((LANE_ROWS, HD), F32)],
        input_output_aliases={0: 0},
        compiler_params=_cp("arbitrary"))(dz, dxc, dxc, dxc, z, z, z, cw)


def _me():
    return lax.axis_index("x"), lax.axis_index("y"), lax.axis_index("c")


def _peer(m):
    x, y, c = _me()
    px = 1 - x if m & 4 else x
    py = 1 - y if m & 2 else y
    pc = 1 - c if m & 1 else c
    return (px, py, pc), 4 * px + 2 * py + pc


_ANY = pl.BlockSpec(memory_space=pl.ANY)
_EXCHANGE_SEMS = [pltpu.SemaphoreType.DMA((N_DEV - 1,)), pltpu.SemaphoreType.DMA((N_DEV - 1,)), pltpu.SemaphoreType.DMA(())]


def _all_gather(v, after, name):
    def body(v_ref, after_ref, o_ref, send_sems, recv_sems, local_sem):
        del after_ref
        x, y, c = _me()
        me = 4 * x + 2 * y + c
        local = pltpu.make_async_copy(v_ref, o_ref.at[me], local_sem)
        local.start()
        sends = []
        for m in range(1, N_DEV):
            dev, _ = _peer(m)
            cp = pltpu.make_async_remote_copy(v_ref, o_ref.at[me], send_sems.at[m - 1], recv_sems.at[m - 1],
                                              device_id=dev, device_id_type=pl.DeviceIdType.MESH)
            cp.start()
            sends.append(cp)
        for m in range(1, N_DEV):
            dev, blk = _peer(m)
            pltpu.make_async_remote_copy(v_ref, o_ref.at[blk], send_sems.at[m - 1], recv_sems.at[m - 1],
                                         device_id=dev, device_id_type=pl.DeviceIdType.MESH).wait_recv()
        for cp in sends:
            cp.wait_send()
        local.wait()

    return pl.pallas_call(
        body, name=name, in_specs=[_ANY, _ANY], out_specs=_ANY,
        out_shape=SDS((N_DEV,) + v.shape, v.dtype), scratch_shapes=_EXCHANGE_SEMS)(v, after)


_HBM = pl.BlockSpec(memory_space=pltpu.HBM)
_SEM = pl.BlockSpec(memory_space=pltpu.SEMAPHORE)
_EFFECT = pltpu.CompilerParams(has_side_effects=pltpu.SideEffectType.DATAFLOW_SIDE_EFFECTING)
_PEER_SEMS = pltpu.SemaphoreType.DMA((N_DEV - 1,))


def _in_hbm(a):
    return pltpu.with_memory_space_constraint(a, pltpu.HBM)


def _remote(src, dst, send_sems, recv_sems, m):
    dev, _ = _peer(m)
    return pltpu.make_async_remote_copy(src, dst, send_sems.at[m - 1], recv_sems.at[m - 1],
                                        device_id=dev, device_id_type=pl.DeviceIdType.MESH)


def _gather_start(lands, after, name):
    n = len(lands)

    def body(*refs):
        land = refs[:n]
        sems = refs[n + 1:3 * n + 1]
        token = refs[-1]
        x, y, c = _me()
        me = 4 * x + 2 * y + c
        for t in range(n):
            for m in range(1, N_DEV):
                _remote(land[t].at[me], land[t].at[me], sems[2 * t], sems[2 * t + 1], m).start()
        token[...] = jnp.zeros_like(token)

    res = pl.pallas_call(
        body, name=name, in_specs=[_HBM] * n + [_ANY],
        out_specs=[_SEM] * (2 * n) + [_HBM] * n + [pl.BlockSpec(memory_space=pltpu.VMEM)],
        out_shape=[_PEER_SEMS] * (2 * n) + [pltpu.HBM(a.shape, a.dtype) for a in lands] + [SDS((8, 128), F32)],
        input_output_aliases={t: 2 * n + t for t in range(n)},
        compiler_params=_EFFECT)(*[_in_hbm(a) for a in lands], after)
    return [(res[2 * t], res[2 * t + 1], res[2 * n + t]) for t in range(n)], res[-1]


def _gather_wait(handle, after, name):
    send_sems, recv_sems, land = handle

    def body(land_ref, ssem, rsem, after_ref, out_ref):
        del after_ref, out_ref
        x, y, c = _me()
        me = 4 * x + 2 * y + c
        for m in range(1, N_DEV):
            _, blk = _peer(m)
            cp = _remote(land_ref.at[me], land_ref.at[blk], ssem, rsem, m)
            cp.wait_send()
            cp.wait_recv()

    return pl.pallas_call(
        body, name=name, in_specs=[_HBM, _SEM, _SEM, _ANY], out_specs=_HBM,
        out_shape=pltpu.HBM(land.shape, land.dtype), input_output_aliases={0: 0},
        compiler_params=_EFFECT)(land, send_sems, recv_sems, after)


FIRST_STAGE = (1, 2, 4, 6)
RELAYED = (2, 4, 6)
OTHER_CORE = 1


def _stage_copy(src, dst, send_sems, recv_sems, k, m):
    dev, _ = _peer(m)
    return pltpu.make_async_remote_copy(src, dst, send_sems.at[k], recv_sems.at[k],
                                        device_id=dev, device_id_type=pl.DeviceIdType.MESH)


def _gather2_start(lands, after, name):
    n = len(lands)

    def body(*refs):
        land = refs[:n]
        sems = refs[n + 1:3 * n + 1]
        token = refs[-1]
        x, y, c = _me()
        me = 4 * x + 2 * y + c
        for t in range(n):
            for k, m in enumerate(FIRST_STAGE):
                _stage_copy(land[t].at[me], land[t].at[me], sems[2 * t], sems[2 * t + 1], k, m).start()
        token[...] = jnp.zeros_like(token)

    stage_sems = pltpu.SemaphoreType.DMA((len(FIRST_STAGE),))
    res = pl.pallas_call(
        body, name=name, in_specs=[_HBM] * n + [_ANY],
        out_specs=[_SEM] * (2 * n) + [_HBM] * n + [pl.BlockSpec(memory_space=pltpu.VMEM)],
        out_shape=[stage_sems] * (2 * n) + [pltpu.HBM(a.shape, a.dtype) for a in lands] + [SDS((8, 128), F32)],
        input_output_aliases={t: 2 * n + t for t in range(n)},
        compiler_params=_EFFECT)(*[_in_hbm(a) for a in lands], after)
    return [(res[2 * t], res[2 * t + 1], res[2 * n + t]) for t in range(n)], res[-1]


def _gather2_relay(handles, after, name):
    n = len(handles)

    def body(*refs):
        land, send1, recv1 = refs[:n], refs[n:2 * n], refs[2 * n:3 * n]
        sems = refs[3 * n + 1:5 * n + 1]
        token = refs[-1]
        x, y, c = _me()
        me = 4 * x + 2 * y + c
        for t in range(n):
            for j, m in enumerate(RELAYED):
                _, blk = _peer(m)
                _stage_copy(land[t].at[me], land[t].at[blk], send1[t], recv1[t], 1 + j, m).wait_recv()
                _stage_copy(land[t].at[blk], land[t].at[blk], sems[2 * t], sems[2 * t + 1], j, OTHER_CORE).start()
        token[...] = jnp.zeros_like(token)

    relay_sems = pltpu.SemaphoreType.DMA((len(RELAYED),))
    lands = [h[2] for h in handles]
    res = pl.pallas_call(
        body, name=name, in_specs=[_HBM] * n + [_SEM] * (2 * n) + [_ANY],
        out_specs=[_SEM] * (2 * n) + [_HBM] * n + [pl.BlockSpec(memory_space=pltpu.VMEM)],
        out_shape=[relay_sems] * (2 * n) + [pltpu.HBM(a.shape, a.dtype) for a in lands] + [SDS((8, 128), F32)],
        input_output_aliases={t: 2 * n + t for t in range(n)},
        compiler_params=_EFFECT)(*lands, *[h[0] for h in handles], *[h[1] for h in handles], after)
    return [(h[0], h[1], res[2 * t], res[2 * t + 1], res[2 * n + t]) for t, h in enumerate(handles)], res[-1]


def _gather2_wait(handle, after, name):
    send1, recv1, send2, recv2, land = handle

    def body(land_ref, s1, r1, s2, r2, after_ref, out_ref):
        del after_ref, out_ref
        x, y, c = _me()
        me = 4 * x + 2 * y + c
        _, other = _peer(OTHER_CORE)
        _stage_copy(land_ref.at[me], land_ref.at[other], s1, r1, 0, OTHER_CORE).wait_recv()
        for k, m in enumerate(FIRST_STAGE):
            _stage_copy(land_ref.at[me], land_ref.at[me], s1, r1, k, m).wait_send()
        for j, m in enumerate(RELAYED):
            _, mine = _peer(m)
            _, theirs = _peer(m ^ OTHER_CORE)
            _stage_copy(land_ref.at[mine], land_ref.at[mine], s2, r2, j, OTHER_CORE).wait_send()
            _stage_copy(land_ref.at[mine], land_ref.at[theirs], s2, r2, j, OTHER_CORE).wait_recv()

    return pl.pallas_call(
        body, name=name, in_specs=[_HBM] + [_SEM] * 4 + [_ANY], out_specs=_HBM,
        out_shape=pltpu.HBM(land.shape, land.dtype), input_output_aliases={0: 0},
        compiler_params=_EFFECT)(land, send1, recv1, send2, recv2, after)


def _exchange_start(ps, name):
    n = len(ps)

    def body(*refs):
        p = refs[:n]
        got = refs[n:2 * n]
        sems = refs[2 * n:5 * n]
        token = refs[-1]
        x, y, c = _me()
        me = 4 * x + 2 * y + c
        for t in range(n):
            pltpu.make_async_copy(p[t].at[me], got[t].at[me], sems[3 * t + 2]).start()
            for m in range(1, N_DEV):
                _, blk = _peer(m)
                _remote(p[t].at[blk], got[t].at[me], sems[3 * t], sems[3 * t + 1], m).start()
        token[...] = jnp.zeros_like(token)

    res = pl.pallas_call(
        body, name=name, in_specs=[_HBM] * (2 * n),
        out_specs=[_SEM] * (3 * n) + [_HBM] * (2 * n) + [pl.BlockSpec(memory_space=pltpu.VMEM)],
        out_shape=[_PEER_SEMS, _PEER_SEMS, pltpu.SemaphoreType.DMA(())] * n
        + [pltpu.HBM(a.shape, a.dtype) for a in ps] * 2 + [SDS((8, 128), F32)],
        input_output_aliases={t: 3 * n + t for t in range(2 * n)},
        compiler_params=_EFFECT)(*[_in_hbm(a) for a in ps], *[_in_hbm(lax.empty(a.shape, a.dtype)) for a in ps])
    return [(res[3 * t], res[3 * t + 1], res[3 * t + 2], res[3 * n + t], res[4 * n + t]) for t in range(n)], res[-1]


def _exchange_wait(handle, after, name):
    send_sems, recv_sems, local_sem, p, got = handle

    def body(p_ref, got_ref, ssem, rsem, lsem, after_ref, p_out, got_out):
        del after_ref, p_out, got_out
        x, y, c = _me()
        me = 4 * x + 2 * y + c
        pltpu.make_async_copy(p_ref.at[me], got_ref.at[me], lsem).wait()
        for m in range(1, N_DEV):
            _, blk = _peer(m)
            cp = _remote(p_ref.at[blk], got_ref.at[blk], ssem, rsem, m)
            cp.wait_send()
            cp.wait_recv()

    return pl.pallas_call(
        body, name=name, in_specs=[_HBM, _HBM, _SEM, _SEM, _SEM, _ANY], out_specs=[_HBM, _HBM],
        out_shape=[pltpu.HBM(p.shape, p.dtype), pltpu.HBM(got.shape, got.dtype)],
        input_output_aliases={0: 0, 1: 1}, compiler_params=_EFFECT)(p, got, send_sems, recv_sems, local_sem, after)[1]


CHIPS = (0, 2, 4, 6)


def _pairs_start(p, name):
    def body(p_ref, pair_ref, ssem, rsem, p_out, pair_out, token):
        del p_out, pair_out
        for k, chip in enumerate(CHIPS):
            _, blk = _peer(chip ^ OTHER_CORE)
            _stage_copy(p_ref.at[blk], pair_ref.at[k], ssem, rsem, k, OTHER_CORE).start()
        token[...] = jnp.zeros_like(token)

    sems = pltpu.SemaphoreType.DMA((len(CHIPS),))
    pair = lax.empty((len(CHIPS),) + p.shape[1:], p.dtype)
    res = pl.pallas_call(
        body, name=name, in_specs=[_HBM] * 2,
        out_specs=[_SEM] * 2 + [_HBM] * 2 + [pl.BlockSpec(memory_space=pltpu.VMEM)],
        out_shape=[sems, sems, pltpu.HBM(p.shape, p.dtype), pltpu.HBM(pair.shape, pair.dtype), SDS((8, 128), F32)],
        input_output_aliases={0: 2, 1: 3}, compiler_params=_EFFECT)(_in_hbm(p), _in_hbm(pair))
    return res[:4], res[4]


def _pairs_wait(handle, after, name):
    send_sems, recv_sems, p, pair = handle

    def body(p_ref, pair_ref, ssem, rsem, after_ref, p_out, pair_out):
        del after_ref, p_out, pair_out
        for k, chip in enumerate(CHIPS):
            _, blk = _peer(chip ^ OTHER_CORE)
            cp = _stage_copy(p_ref.at[blk], pair_ref.at[k], ssem, rsem, k, OTHER_CORE)
            cp.wait_send()
            cp.wait_recv()

    return pl.pallas_call(
        body, name=name, in_specs=[_HBM, _HBM, _SEM, _SEM, _ANY], out_specs=[_HBM, _HBM],
        out_shape=[pltpu.HBM(p.shape, p.dtype), pltpu.HBM(pair.shape, pair.dtype)],
        input_output_aliases={0: 0, 1: 1}, compiler_params=_EFFECT)(p, pair, send_sems, recv_sems, after)


def _sum_pairs(p, pair, me1, name):
    _, r, c = pair.shape
    tr = _row_tile(r)

    def body(me_ref, p_ref, pair_ref, o_ref):
        del me_ref
        o_ref[...] = (p_ref[...].astype(F32) + pair_ref[...].astype(F32)).astype(o_ref.dtype)

    def mine(k, i, me):
        chip = 2 * k
        return (jnp.bitwise_xor(me[0], chip), i, 0)

    assert CHIPS == tuple(2 * k for k in range(len(CHIPS)))
    blk = pl.BlockSpec((None, tr, c), lambda k, i, me: (k, i, 0))
    return pl.pallas_call(
        body, name=name,
        grid_spec=pltpu.PrefetchScalarGridSpec(
            num_scalar_prefetch=1, grid=(len(CHIPS), r // tr),
            in_specs=[pl.BlockSpec((None, tr, c), mine), blk], out_specs=blk),
        out_shape=SDS(pair.shape, pair.dtype), compiler_params=_cp("parallel", "parallel"))(me1, p, pair)


def _chips_start(q, name):
    def body(q_ref, got_ref, ssem, rsem, lsem, q_out, got_out, token):
        del q_out, got_out
        pltpu.make_async_copy(q_ref.at[0], got_ref.at[0], lsem).start()
        for k, chip in enumerate(CHIPS[1:]):
            _stage_copy(q_ref.at[k + 1], got_ref.at[k + 1], ssem, rsem, k, chip).start()
        token[...] = jnp.zeros_like(token)

    sems = pltpu.SemaphoreType.DMA((len(CHIPS) - 1,))
    res = pl.pallas_call(
        body, name=name, in_specs=[_HBM] * 2,
        out_specs=[_SEM] * 3 + [_HBM] * 2 + [pl.BlockSpec(memory_space=pltpu.VMEM)],
        out_shape=[sems, sems, pltpu.SemaphoreType.DMA(()), pltpu.HBM(q.shape, q.dtype), pltpu.HBM(q.shape, q.dtype),
                   SDS((8, 128), F32)],
        input_output_aliases={0: 3, 1: 4}, compiler_params=_EFFECT)(_in_hbm(q), _in_hbm(lax.empty(q.shape, q.dtype)))
    return res[:5], res[5]


def _chips_wait(handle, after, name):
    send_sems, recv_sems, local_sem, q, got = handle

    def body(q_ref, got_ref, ssem, rsem, lsem, after_ref, q_out, got_out):
        del after_ref, q_out, got_out
        pltpu.make_async_copy(q_ref.at[0], got_ref.at[0], lsem).wait()
        for k, chip in enumerate(CHIPS[1:]):
            cp = _stage_copy(q_ref.at[k + 1], got_ref.at[k + 1], ssem, rsem, k, chip)
            cp.wait_send()
            cp.wait_recv()

    return pl.pallas_call(
        body, name=name, in_specs=[_HBM, _HBM, _SEM, _SEM, _SEM, _ANY], out_specs=[_HBM, _HBM],
        out_shape=[pltpu.HBM(q.shape, q.dtype), pltpu.HBM(got.shape, got.dtype)],
        input_output_aliases={0: 0, 1: 1}, compiler_params=_EFFECT)(q, got, send_sems, recv_sems, local_sem, after)[1]


def _cast_into_slot(w, layer, me1, name):
    _, r, c = w.shape
    tr = next(t for t in (512, 352, r) if r % t == 0)

    def body(me_ref, w_ref, o_ref):
        del me_ref
        o_ref[...] = w_ref[...].astype(BF16)

    return pl.pallas_call(
        body, name=name,
        grid_spec=pltpu.PrefetchScalarGridSpec(
            num_scalar_prefetch=1, grid=(r // tr,),
            in_specs=[pl.BlockSpec((None, tr, c), lambda i, me: (layer, i, 0))],
            out_specs=pl.BlockSpec((None, tr, c), lambda i, me: (me[0], i, 0))),
        out_shape=SDS((N_DEV, r, c), BF16), compiler_params=_cp("arbitrary"))(me1, w)


def _cast_all_into_slots(ws, layers, me1, after, name):
    n = len(ws)

    def body(me_ref, *refs):
        del me_ref
        for w_ref, o_ref in zip(refs[:n], refs[n + 1:]):
            o_ref[...] = w_ref[...].astype(BF16)

    return pl.pallas_call(
        body, name=name,
        grid_spec=pltpu.PrefetchScalarGridSpec(
            num_scalar_prefetch=1, grid=(1,),
            in_specs=[pl.BlockSpec((None,) + a.shape[1:], lambda i, me, l=l: (l, 0, 0)) for a, l in zip(ws, layers)]
            + [_ANY],
            out_specs=[pl.BlockSpec((None,) + a.shape[1:], lambda i, me: (me[0], 0, 0)) for a in ws]),
        out_shape=[SDS((N_DEV,) + a.shape[1:], BF16) for a in ws],
        compiler_params=_cp("arbitrary"))(me1, *ws, after)


def _sum8_into_slot(p, me1, name):
    _, r, c = p.shape

    def body(me_ref, p_ref, o_ref):
        del me_ref
        acc = p_ref[0]
        for k in range(1, N_DEV):
            acc = acc + p_ref[k]
        o_ref[...] = acc

    return pl.pallas_call(
        body, name=name,
        grid_spec=pltpu.PrefetchScalarGridSpec(
            num_scalar_prefetch=1, grid=(1,),
            in_specs=[pl.BlockSpec(p.shape, lambda i, me: (0, 0, 0))],
            out_specs=pl.BlockSpec((None, r, c), lambda i, me: (me[0], 0, 0))),
        out_shape=SDS(p.shape, F32), compiler_params=_cp("arbitrary"))(me1, p)


def _adamw(w, g, m, v):
    m = ADAM_B1 * m + (1.0 - ADAM_B1) * g
    v = ADAM_B2 * v + (1.0 - ADAM_B2) * (g * g)
    m_hat = m / (1.0 - ADAM_B1 ** ADAM_STEP)
    v_hat = v / (1.0 - ADAM_B2 ** ADAM_STEP)
    delta = -ADAM_LR * (m_hat / (jnp.sqrt(v_hat) + ADAM_EPS) + ADAM_WD * w)
    return delta, m, v


def _adam_tile(p_ref, w_ref, m_ref, v_ref, g_ref, d_ref, nm_ref, nv_ref):
    g = p_ref[0].astype(F32)
    for k in range(1, p_ref.shape[0]):
        g = g + p_ref[k].astype(F32)
    delta, nm, nv = _adamw(w_ref[...], g, m_ref[...], v_ref[...])
    g_ref[...] = g
    d_ref[...] = delta
    nm_ref[...] = nm
    nv_ref[...] = nv


class _AdamJob:
    def __init__(self, parts, w, m, v, layer, prev):
        self.args = [parts, w, m, v] + list(prev or ())
        self.layer, self.results = layer, None


def _carry(jobs, steps, body, n_in, n_out):
    in_specs, args, out_specs, out_shapes, aliases, n_prevs = [], [], [], [], {}, []
    for j, job in enumerate(jobs):
        _, r, c = job.args[0].shape
        nr = next(n for n in range(steps, 0, -1) if steps % n == 0 and r % (16 * n) == 0)
        nc = steps // nr
        assert c % (128 * nc) == 0
        tile = (r // nr, c // nc)
        blk = pl.BlockSpec((None,) + tile, lambda i, layer=job.layer, nc=nc: (layer, i // nc, i % nc))
        n_prev = len(job.args) - 4
        aliases.update({n_in + len(args) + 4 + k: n_out + 4 * j + k for k in range(n_prev)})
        in_specs += [pl.BlockSpec(job.args[0].shape[:1] + tile, lambda i, nc=nc: (0, i // nc, i % nc)), blk, blk, blk]
        in_specs += [_ANY] * n_prev
        args += job.args
        out_specs += [blk] * 4
        out_shapes += [SDS(job.args[1].shape, F32)] * 4
        n_prevs.append(n_prev)

    def carrying(*refs):
        ins, outs = refs[:n_in + len(args)], refs[n_in + len(args):]
        body(*ins[:n_in], *outs[:n_out])
        k = n_in
        for j, n_prev in enumerate(n_prevs):
            _adam_tile(*ins[k:k + 4], *outs[n_out + 4 * j:n_out + 4 * j + 4])
            k += 4 + n_prev

    return carrying, in_specs, args, out_specs, out_shapes, aliases


def _adam_shard(parts, w, m, v, layer, prev, name):
    n, r, c = parts.shape
    tr = next(t for t in (512, 352, r) if r % t == 0)
    n_prev = 0 if prev is None else 4

    def body(*refs):
        _adam_tile(*refs[:4], *refs[4 + n_prev:])

    blk = pl.BlockSpec((None, tr, c), lambda i: (layer, i, 0))
    return pl.pallas_call(
        body, name=name, grid=(r // tr,),
        in_specs=[pl.BlockSpec((n, tr, c), lambda i: (0, i, 0)), blk, blk, blk] + [_ANY] * n_prev,
        out_specs=[blk] * 4, out_shape=[SDS(w.shape, F32)] * 4,
        input_output_aliases={4 + k: k for k in range(n_prev)},
        compiler_params=_cp("parallel"))(parts, w, m, v, *(prev or ()))


SMALL_MATRICES = [("lru_w_r", 2048), ("lru_w_i", 2048), ("gmlp_w_s", 1024)]
SMALL_VECTORS = [("norm1_g", 8), ("gmlp_ln_g", 8), ("gmlp_ln_b", 8), ("gmlp_b_s", 8), ("conv_w", 32), ("conv_b", 8),
                 ("lru_b_r", 16), ("lru_b_i", 16), ("lru_lambda", 16), ("norm2_g", 8), ("final_g", 8)]
SMALL_VECTOR_ROW0 = sum(n for _, n in SMALL_MATRICES)
SMALL_VECTOR_BLOCK = 256
SMALL_ROWS = SMALL_VECTOR_ROW0 + SMALL_VECTOR_BLOCK
LOSS_ROW = SMALL_VECTOR_ROW0 + sum(n for _, n in SMALL_VECTORS)
assert LOSS_ROW + LANE_ROWS <= SMALL_ROWS


def _pack_small(small):
    parts = [small[k] for k, _ in SMALL_MATRICES]
    parts += [small[k] if k in small else jnp.zeros((n, HD), F32) for k, n in SMALL_VECTORS]
    parts += [small["loss"]] if "loss" in small else []
    flat = jnp.concatenate(parts)
    return jnp.pad(flat, ((0, SMALL_ROWS - flat.shape[0]), (0, 0))).reshape(N_DEV, SMALL_ROWS // N_DEV, HD)


def _adam_matrix(g0, g1, w, m, v, row0, name):
    _, rows, _ = w.shape
    tr = 512

    def body(g0_ref, g1_ref, w_ref, m_ref, v_ref, g_ref, d_ref, nm_ref, nv_ref):
        for l, src in enumerate((g0_ref, g1_ref)):
            g = src[...]
            delta, nm, nv = _adamw(w_ref[l], g, m_ref[l], v_ref[l])
            g_ref[l] = g
            d_ref[l] = delta
            nm_ref[l] = nm
            nv_ref[l] = nv

    gspec = pl.BlockSpec((tr, HD), lambda i: (row0 // tr + i, 0))
    blk = pl.BlockSpec((2, tr, HD), lambda i: (0, i, 0))
    return pl.pallas_call(body, name=name, grid=(rows // tr,), in_specs=[gspec, gspec] + [blk] * 3,
                          out_specs=[blk] * 4, out_shape=[SDS(w.shape, F32)] * 4,
                          compiler_params=_cp("parallel"))(g0, g1, w, m, v)


def _adam_vectors(g0, g1, dg1_parts, me1, ws, ms, vs):
    names = [k for k, _ in SMALL_VECTORS]
    n = len(names)

    def lanes(rows8):
        return jnp.concatenate([rows8[k:k + 1, :] for k in range(LANE_ROWS)], axis=1)

    def body(me_ref, g0_ref, g1_ref, dg1_ref, *refs):
        w_refs, m_refs, v_refs = refs[:n], refs[n:2 * n], refs[2 * n:3 * n]
        outs = refs[3 * n:]
        me = me_ref[0]
        g_refs = (g0_ref, g1_ref)

        def emit(i, idx, g):
            delta, nm, nv = _adamw(w_refs[i][idx], g, m_refs[i][idx], v_refs[i][idx])
            for j, val in enumerate((g, delta, nm, nv)):
                outs[4 * i + j][idx] = val

        off = 0
        for i, (name, rows) in enumerate(SMALL_VECTORS):
            for l in range(2):
                row = (slice(l, l + 1), slice(None))
                if name == "final_g":
                    if l == 1:
                        emit(i, (slice(0, 1), slice(None)), lanes(g1_ref[off:off + rows, :]))
                elif name == "norm1_g":
                    if l == 1:
                        emit(i, row, lanes(g0_ref[off:off + rows, :]))
                    else:
                        total = dg1_ref[0]
                        for k in range(1, N_DEV):
                            total = total + dg1_ref[k]
                        emit(i, row, lanes(total))
                elif name == "gmlp_b_s":
                    emit(i, (l,), g_refs[l][off:off + rows, :])
                elif rows == LANE_ROWS:
                    emit(i, row, lanes(g_refs[l][off:off + rows, :]))
                else:
                    for r in range(rows // LANE_ROWS):
                        emit(i, (l, slice(r, r + 1), slice(None)), g_refs[l][pl.ds(off + r * LANE_ROWS + me, 1), :])
            off += rows

    args = [ws[k] for k in names] + [ms[k] for k in names] + [vs[k] for k in names]
    gspec = pl.BlockSpec((SMALL_VECTOR_BLOCK, HD), lambda i, me: (SMALL_VECTOR_ROW0 // SMALL_VECTOR_BLOCK, 0))
    res = pl.pallas_call(
        body, name="adam_vectors",
        grid_spec=pltpu.PrefetchScalarGridSpec(
            num_scalar_prefetch=1, grid=(1,),
            in_specs=[gspec, gspec, _full(dg1_parts.shape)] + [_full(a.shape) for a in args],
            out_specs=[_full(ws[k].shape) for k in names for _ in range(4)]),
        out_shape=[SDS(ws[k].shape, F32) for k in names for _ in range(4)],
        compiler_params=_cp("arbitrary"))(me1, g0, g1, dg1_parts, *args)
    return {k: list(res[4 * i:4 * i + 4]) for i, k in enumerate(names)}


def _local_step(x, tgt, p, get_w, hook=lambda stage, layer, payload: None):
    s = x.shape[0]
    tm = _row_tile(s)
    wsb = p["gmlp_w_s"].astype(BF16)
    wstb = jnp.swapaxes(p["gmlp_w_s"], -1, -2).astype(BF16)
    bsb = jnp.broadcast_to(p["gmlp_b_s"][..., None], p["gmlp_w_s"].shape)
    wrb = p["lru_w_r"].astype(BF16)
    wib = p["lru_w_i"].astype(BF16)
    saved = []
    for l in range(2):
        win = get_w("w_in", l, x)
        z, h1 = _norm_inproj(x, p["norm1_g"][l][None], win, l, tm, after=[hook("pre_inproj", l, win)])
        a0, b0, a1, b1, xcb, *gates = _lru_gates_fwd(z, p["conv_w"][l], p["conv_b"][l][None], wrb[l], wib[l],
                                                     p["lru_b_r"][l], p["lru_b_i"][l], p["lru_lambda"][l], l, tm)
        h0, hr = _lru_scan(a0, b0, a1, b1, False, l)
        token = hook("pre_gmlp", l, h0)
        wout = get_w("w_out", l, h0 if token is None else token)
        x1, mg = _mixer_fwd(x, h0, hr, z, p["gmlp_ln_g"][l][None], p["gmlp_ln_b"][l][None], wsb[l], bsb[l], wout, l, tm)
        wfi = get_w("w_ffn_in", l, x1)
        wfo = get_w("w_ffn_out", l, x1)
        if l == 0:
            x2, ff, dff, h2 = _ffn_fwd(x1, p["norm2_g"][l][None], wfi, wfo, l, tm)
        else:
            dx, loss, dfg, ff, dff, h2 = _ffn_fwd(x1, p["norm2_g"][l][None], wfi, wfo, l, tm,
                                                  head=(p["final_g"][None], tgt))
        saved.append((x, z, h1, a0, a1, h0, hr, x1, mg, ff, dff, h2, win, wout, wfi, wfo, xcb, gates))
        x = x2
    for l in (1, 0):
        x0, z, h1, a0, a1, h0, hr, x1, mg, ff, dff, h2, win, wout, wfi, wfo, xcb, gates = saved[l]
        dgu, dx1, dg2 = _ffn_bwd(dx, wfo, dff, wfi.reshape(N_DEV, FF_BLK, D), x1, p["norm2_g"][l][None], l, tm)
        d_wfo = _mm_tn(ff, pl.BlockSpec((None, s, FF_BLK), lambda j: (j, 0, 0)), dx, _resident((s, D)),
                       4, (4, FF_BLK, D), pl.BlockSpec((None, FF_BLK, D), lambda j: (j, 0, 0)),
                       f"dw_ffn_out_l{l}", a_is_transposed=False)
        dgu8 = dgu.reshape(N_DEV, s, FF_BLK)
        d_wfi = _mm_tn(dgu8, pl.BlockSpec((None, s, FF_BLK), lambda j: (j, 0, 0)), h2, _resident((s, D)),
                       N_DEV, (N_DEV, FF_BLK, D), pl.BlockSpec((None, FF_BLK, D), lambda j: (j, 0, 0)),
                       f"dw_ffn_in_l{l}", a_is_transposed=False)
        d_wout = _mm_tn(mg, _resident((D, s)), dx1, pl.BlockSpec((s, D // 2), lambda j: (0, j)),
                        2, (D, D), pl.BlockSpec((D, D // 2), lambda j: (0, j)), f"dw_out_l{l}")
        token = hook("ffn_partials", l, dict(w_ffn_out=d_wfo.reshape(N_DEV, D_FF // N_DEV, D), w_ffn_in=d_wfi,
                                             w_out=d_wout.reshape(N_DEV, D // N_DEV, D)))
        pending = hook("mid_backward", l, dx1)
        dz, dh, dws, dbs, dlng, dlnb = _mixer_bwd(dx1, wout, h0, hr, z, p["gmlp_ln_g"][l][None], p["gmlp_ln_b"][l][None],
                                                  wsb[l], wstb[l], bsb[l], l, tm, after=[token])
        g1, g0 = _lru_scan(a1, dh, a0, dh, True, l)
        dxc, dwr, dwi, dbr, dbi, dlam = _lru_gates_bwd(
            xcb, gates, h0, hr, g0, g1, wrb[l], wib[l], p["lru_lambda"][l], l, tm, after=[pending])
        dz, dcw, dcb = _conv_bwd(dz, dxc, z, p["conv_w"][l], l, tm)
        small = dict(lru_w_r=dwr.reshape(-1, HD), lru_w_i=dwi.reshape(-1, HD), gmlp_w_s=dws.reshape(-1, HD),
                     gmlp_ln_g=dlng, gmlp_ln_b=dlnb, gmlp_b_s=dbs, conv_w=dcw, conv_b=dcb, lru_b_r=dbr,
                     lru_b_i=dbi, lru_lambda=dlam, norm2_g=dg2)
        if l == 1:
            small["final_g"] = dfg
            small["loss"] = jnp.broadcast_to(loss, (LANE_ROWS, HD))
        else:
            small["norm1_g"] = dg1
        started = hook("small_grads", l, small)
        d_win = _mm_tn(h1, _resident((D, s)), dz, pl.BlockSpec((s, IN_BLK), lambda j: (0, j)),
                       N_DEV, (N_DEV, D, IN_BLK), pl.BlockSpec((None, D, IN_BLK), lambda j: (j, 0, 0)),
                       f"dw_in_l{l}", after=started, jobs=hook("dw_in", l, started) or ())
        token = hook("mixer_partials", l, dict(w_in=d_win))
        dx, dg1 = _mm_nt_rms_bwd(
            dz, pl.BlockSpec((tm, N_IN), lambda i: (i, 0)),
            lambda r: [r[:, k * IN_BLK:(k + 1) * IN_BLK] for k in range(N_DEV)],
            win, False, x0, p["norm1_g"][l][None], dx1, f"inproj_bwd_dx_l{l}", tm,
            after=[token], jobs=hook("inproj_bwd_dx", l, token) or ())
    return loss, dx, dg1


_REPL = ["norm1_g", "gmlp_ln_g", "gmlp_ln_b", "gmlp_w_s", "gmlp_b_s", "conv_b", "lru_w_r", "lru_w_i", "norm2_g", "final_g"]
_LANE_SHARDED = ["conv_w", "lru_b_r", "lru_b_i", "lru_lambda"]
_BIG = ["w_in", "w_out", "w_ffn_in", "w_ffn_out"]
_ORDER = ["norm1_g", "w_in", "gmlp_ln_g", "gmlp_ln_b", "gmlp_w_s", "gmlp_b_s", "conv_w", "conv_b", "lru_w_r", "lru_b_r",
          "lru_w_i", "lru_b_i", "lru_lambda", "w_out", "norm2_g", "w_ffn_in", "w_ffn_out", "final_g"]


def kernel(x, norm1_g, w_in, gmlp_ln_g, gmlp_ln_b, gmlp_w_s, gmlp_b_s, conv_w, conv_b, lru_w_r, lru_b_r, lru_w_i, lru_b_i, lru_lambda, w_out, norm2_g, w_ffn_in, w_ffn_out, final_g, loss_target, m_norm1_g, m_w_in, m_gmlp_ln_g, m_gmlp_ln_b, m_gmlp_w_s, m_gmlp_b_s, m_conv_w, m_conv_b, m_lru_w_r, m_lru_b_r, m_lru_w_i, m_lru_b_i, m_lru_lambda, m_w_out, m_norm2_g, m_w_ffn_in, m_w_ffn_out, m_final_g, v_norm1_g, v_w_in, v_gmlp_ln_g, v_gmlp_ln_b, v_gmlp_w_s, v_gmlp_b_s, v_conv_w, v_conv_b, v_lru_w_r, v_lru_b_r, v_lru_w_i, v_lru_b_i, v_lru_lambda, v_w_out, v_norm2_g, v_w_ffn_in, v_w_ffn_out, v_final_g):
    w = dict(norm1_g=norm1_g, w_in=w_in, gmlp_ln_g=gmlp_ln_g, gmlp_ln_b=gmlp_ln_b, gmlp_w_s=gmlp_w_s, gmlp_b_s=gmlp_b_s,
             conv_w=conv_w, conv_b=conv_b, lru_w_r=lru_w_r, lru_b_r=lru_b_r, lru_w_i=lru_w_i, lru_b_i=lru_b_i,
             lru_lambda=lru_lambda, w_out=w_out, norm2_g=norm2_g, w_ffn_in=w_ffn_in, w_ffn_out=w_ffn_out, final_g=final_g)
    mom = dict(norm1_g=m_norm1_g, w_in=m_w_in, gmlp_ln_g=m_gmlp_ln_g, gmlp_ln_b=m_gmlp_ln_b, gmlp_w_s=m_gmlp_w_s,
               gmlp_b_s=m_gmlp_b_s, conv_w=m_conv_w, conv_b=m_conv_b, lru_w_r=m_lru_w_r, lru_b_r=m_lru_b_r,
               lru_w_i=m_lru_w_i, lru_b_i=m_lru_b_i, lru_lambda=m_lru_lambda, w_out=m_w_out, norm2_g=m_norm2_g,
               w_ffn_in=m_w_ffn_in, w_ffn_out=m_w_ffn_out, final_g=m_final_g)
    var = dict(norm1_g=v_norm1_g, w_in=v_w_in, gmlp_ln_g=v_gmlp_ln_g, gmlp_ln_b=v_gmlp_ln_b, gmlp_w_s=v_gmlp_w_s,
               gmlp_b_s=v_gmlp_b_s, conv_w=v_conv_w, conv_b=v_conv_b, lru_w_r=v_lru_w_r, lru_b_r=v_lru_b_r,
               lru_w_i=v_lru_w_i, lru_b_i=v_lru_b_i, lru_lambda=v_lru_lambda, w_out=v_w_out, norm2_g=v_norm2_g,
               w_ffn_in=v_w_ffn_in, w_ffn_out=v_w_ffn_out, final_g=v_final_g)
    for src in (w, mom, var):
        src["w_ffn_in"] = jnp.swapaxes(src["w_ffn_in"], 1, 2)
    xi, yi, ci = _me()
    me = 4 * xi + 2 * yi + ci

    lane_shapes = [w[k].shape for k in _LANE_SHARDED]
    lane_rows = sum(a[0] * a[1] for a in lane_shapes)
    packed = jnp.concatenate([w[k].reshape(-1, HD) for k in _LANE_SHARDED])
    packed = jnp.pad(packed, ((0, -lane_rows % 8), (0, 0)))

    me1 = jnp.reshape(me, (1,)).astype(jnp.int32)
    gathers = {}
    exchanges = {}
    views = dict(w_in=(N_DEV, D, IN_BLK), w_out=(D, D), w_ffn_in=(2, 4, FF_BLK, D), w_ffn_out=(4, FF_BLK, D))
    small_ex = {}
    small_ag = {}

    casts = {}

    def start_gather(names, l, after):
        lands = [casts[(k, l)] if (k, l) in casts else _cast_into_slot(w[k], l, me1, f"cast_{k}_l{l}") for k in names]
        started, tok = _gather2_start(lands, after, f"gather_start_{'_'.join(names)}_l{l}")
        gathers.update({(k, l): h for k, h in zip(names, started)})
        return tok

    def relay_gather(names, l, after):
        relayed, tok = _gather2_relay([gathers[(k, l)] for k in names], after, f"gather_relay_{'_'.join(names)}_l{l}")
        gathers.update({(k, l): h for k, h in zip(names, relayed)})
        return tok

    def get_w(k, l, after):
        return _gather2_wait(gathers[(k, l)], after, f"gather_wait_{k}_l{l}").reshape(views[k])

    carried = {("inproj_bwd_dx", 1): [("w_ffn_out", 1), ("w_ffn_in", 1), ("w_out", 1)], ("dw_in", 0): [("w_in", 1)],
               ("inproj_bwd_dx", 0): [("w_ffn_out", 0), ("w_ffn_in", 0), ("w_out", 0)]}
    adam = {}

    def adam_jobs(shards, after):
        for k, l in shards:
            got = _exchange_wait(exchanges[(k, l)], after, f"exchange_wait_{k}_l{l}")
            adam[k] = _AdamJob(got, w[k], mom[k], var[k], l, adam[k].results if k in adam else None)
        return [adam[k] for k, _ in shards]

    def hook(stage, l, payload):
        if stage in ("dw_in", "inproj_bwd_dx"):
            return adam_jobs(carried.get((stage, l), []), payload)
        if stage == "pre_inproj":
            return start_gather(_BIG[1:], l, payload) if l == 1 else None
        if stage == "pre_gmlp":
            tok = relay_gather(_BIG[1:], l, payload)
            return relay_gather(_BIG[:1], l + 1, tok) if l == 0 else tok
        if stage == "small_grads":
            (small_ex[l],), tok = _exchange_start([_pack_small(payload)], f"exchange_start_small_l{l}")
            return tok
        if stage == "mid_backward":
            return reduce_small(l + 1, payload) if l == 0 else None
        if (stage, l) == ("mixer_partials", 0):
            pairs, tok = _pairs_start(payload["w_in"], "pairs_start_w_in_l0")
            p, pair = _pairs_wait(pairs, reduce_small(0, tok), "pairs_wait_w_in_l0")
            sums = _sum_pairs(p, pair, me1, "sum_pairs_w_in_l0")
            exchanges[("w_in", 0)], tok = _chips_start(sums, "chips_start_w_in_l0")
            return tok
        started, tok = _exchange_start(list(payload.values()), f"exchange_start_{'_'.join(payload)}_l{l}")
        exchanges.update({(k, l): h for k, h in zip(payload, started)})
        return tok

    def reduce_small(l, after):
        got = _exchange_wait(small_ex[l], after, f"exchange_wait_small_l{l}")
        mine = _sum8_into_slot(got, me1, f"sum_small_l{l}")
        (small_ag[l],), tok = _gather_start([mine], got, f"gather_start_small_l{l}")
        return tok

    land = lax.dynamic_update_slice(jnp.zeros((N_DEV,) + packed.shape, F32), packed[None], (me, 0, 0))
    token = start_gather(_BIG[:1], 0, x)
    (lanes_handle,), token = _gather_start([land], token, "gather_start_lanes")
    later = [(k, l) for l in range(2) for k in _BIG if (k, l) != ("w_in", 0)]
    casts.update(zip(later, _cast_all_into_slots([w[k] for k, _ in later], [l for _, l in later], me1, token,
                                                 "cast_later_weights")))
    token = start_gather(_BIG[:1], 1, start_gather(_BIG[1:], 0, token))
    token = relay_gather(_BIG[:1], 0, token)
    lanes = _gather_wait(lanes_handle, token, "gather_wait_lanes")
    params = {k: w[k] for k in _REPL}
    off = 0
    for k, shp in zip(_LANE_SHARDED, lane_shapes):
        n = shp[0] * shp[1]
        params[k] = jnp.swapaxes(lanes[:, off:off + n], 0, 1).reshape(shp[0], shp[1], D)
        off += n
    _, dx, dg1 = _local_step(x[0], loss_target[0], params, get_w, hook)

    out = {k: job.results for k, job in adam.items()}
    after = dx
    g_small = [_gather_wait(small_ag[l], after, f"gather_wait_small_l{l}").reshape(SMALL_ROWS, HD) for l in (0, 1)]
    row0 = 0
    for k, rows in SMALL_MATRICES:
        res = _adam_matrix(*g_small, *[src[k].reshape(2, rows, HD) for src in (w, mom, var)], row0, f"adam_{k}")
        out[k] = [a.reshape(w[k].shape) for a in res]
        after = res[3]
        row0 += rows
    got = _chips_wait(exchanges[("w_in", 0)], after, "chips_wait_w_in_l0")
    out["w_in"] = _adam_shard(got, w["w_in"], mom["w_in"], var["w_in"], 0, out["w_in"], "adam_w_in_l0")
    out["w_ffn_in"] = [jnp.swapaxes(a, 1, 2) for a in out["w_ffn_in"]]
    as_rows = lambda a: a.reshape(1, D) if a.ndim == 1 else a
    vec = _adam_vectors(*g_small, _all_gather(dg1, out["w_in"][3], "gather_norm1_grad"), me1,
                        *[{k: as_rows(src[k]) for k, _ in SMALL_VECTORS} for src in (w, mom, var)])
    out.update({k: [a.reshape(w[k].shape) for a in res] for k, res in vec.items()})

    return (g_small[1][LOSS_ROW, 0], dx[None], *[out[k][0] for k in _ORDER], *[out[k][1] for k in _ORDER],
            *[out[k][2] for k in _ORDER], *[out[k][3] for k in _ORDER])
```

```python
import jax
import jax.numpy as jnp
from jax import lax
from jax.experimental import pallas as pl
from jax.experimental.pallas import tpu as pltpu

F32 = jnp.float32
BF16 = jnp.bfloat16
SDS = jax.ShapeDtypeStruct

D = 1024
N_IN = 6 * D
D_FF = 2816
N_DEV = 8
IN_BLK = N_IN // N_DEV
FF_BLK = 2 * D_FF // N_DEV
HEADS = 8
HD = 128
EPS = 1e-6
LRU_C = 8.0

ADAM_LR = 0.001
ADAM_B1 = 0.9
ADAM_B2 = 0.999
ADAM_EPS = 1e-08
ADAM_WD = 0.01
ADAM_STEP = 10

VMEM_LIMIT = 60 * 2**20


def _cp(*sem, **kw):
    return pltpu.CompilerParams(dimension_semantics=sem, vmem_limit_bytes=VMEM_LIMIT, **kw)


def _row_tile(s):
    return 512 if s >= 1024 else s // 2


_GELU_C = 0.7978845608028654


def _gelu(x):
    t = jnp.tanh(_GELU_C * (x + 0.044715 * (x * x * x)))
    return 0.5 * x * (1.0 + t), t


def _gelu_grad(x, t):
    return 0.5 * (1.0 + t) + 0.5 * x * (1.0 - t * t) * (_GELU_C * (1.0 + 0.134145 * (x * x)))


def _sigmoid(x):
    return 0.5 + 0.5 * jnp.tanh(0.5 * x)


def _softplus(x):
    e = jnp.exp(-jnp.abs(x))
    w = 1.0 + e
    l1p = jnp.where(w == 1.0, e, jnp.log(w) * e / jnp.where(w == 1.0, 1.0, w - 1.0))
    return jnp.maximum(x, 0.0) + l1p


def _rms_fwd(x, g):
    r = lax.rsqrt(jnp.mean(x * x, axis=-1, keepdims=True) + EPS)
    return x * r * g


def _rms_bwd(x, g, dh):
    r = lax.rsqrt(jnp.mean(x * x, axis=-1, keepdims=True) + EPS)
    xh = x * r
    dxh = dh * g
    dx = r * (dxh - xh * jnp.mean(dxh * xh, axis=-1, keepdims=True))
    dg = jnp.sum(dh * xh, axis=0, keepdims=True)
    return dx, dg


LANE_ROWS = D // HD


def _add_rows128(ref, vec, row0=0):
    for i in range(vec.shape[0]):
        for k in range(LANE_ROWS):
            j = row0 + i * LANE_ROWS + k
            ref[j:j + 1, :] += vec[i:i + 1, k * HD:(k + 1) * HD]


def _dot(a, b):
    return jnp.dot(a, b, preferred_element_type=F32)


def _dot_nt(a, b):
    return lax.dot_general(a, b, (((1,), (1,)), ((), ())), preferred_element_type=F32)


def _dot_tn(a, b):
    return lax.dot_general(a, b, (((0,), (0,)), ((), ())), preferred_element_type=F32)


def _taps(prev, cur, nxt, tm):
    hr = prev.shape[0]
    ext = jnp.concatenate([prev, cur, nxt], axis=0)
    n = tm + 2 * hr
    sl = slice(hr, hr + tm)
    return (pltpu.roll(ext, 2, 0)[sl], pltpu.roll(ext, 1, 0)[sl], cur,
            pltpu.roll(ext, n - 1, 0)[sl], pltpu.roll(ext, n - 2, 0)[sl])


def _halo_specs(tm, s, col, rows=8):
    nb = s // rows
    r = tm // rows
    return (pl.BlockSpec((rows, D), lambda i: (jnp.maximum(i * r - 1, 0), col)),
            pl.BlockSpec((tm, D), lambda i: (i, col)),
            pl.BlockSpec((rows, D), lambda i: (jnp.minimum((i + 1) * r, nb - 1), col)))


def _halo_load(prev_ref, cur_ref, next_ref, fp, fn):
    return prev_ref[...].astype(F32) * fp, cur_ref[...].astype(F32), next_ref[...].astype(F32) * fn


def _halo_flags(nt):
    i = pl.program_id(0)
    return (i > 0).astype(F32), (i < nt - 1).astype(F32)


def _full(shape):
    nd = len(shape)
    return pl.BlockSpec(shape, lambda *_: (0,) * nd)


def _resident(shape):
    nd = len(shape)
    return pl.BlockSpec(shape, lambda *_: (0,) * nd, pipeline_mode=pl.Buffered(1))


def _behind(tokens, body, n_in):
    deps = [t for t in tokens if t is not None]

    def ordered(*refs):
        body(*refs[:n_in], *refs[n_in + len(deps):])

    return ordered, [_ANY] * len(deps), deps


def _norm_inproj(x, g, w, layer, tm, after=()):
    s = x.shape[0]

    def body(x_ref, g_ref, w_ref, z_ref, ht_ref):
        h32 = _rms_fwd(x_ref[...], g_ref[...])
        ht_ref[...] = h32.T.astype(BF16)
        h = h32.astype(BF16)
        for j in range(N_DEV):
            z_ref[:, j * IN_BLK:(j + 1) * IN_BLK] = _dot(h, w_ref[j]).astype(BF16)

    body, dep_specs, deps = _behind(after, body, 3)
    return pl.pallas_call(
        body, name=f"norm_inproj_l{layer}", grid=(s // tm,),
        in_specs=[pl.BlockSpec((tm, D), lambda i: (i, 0)), _full((1, D)), _resident((N_DEV, D, IN_BLK))] + dep_specs,
        out_specs=[pl.BlockSpec((tm, N_IN), lambda i: (i, 0)), pl.BlockSpec((D, tm), lambda i: (0, i))],
        out_shape=[SDS((s, N_IN), BF16), SDS((D, s), BF16)],
        compiler_params=_cp("parallel"))(x, g, w, *deps)


def _gmlp_values(zu_ref, zv_ref, lng_ref, lnb_ref):
    zu = zu_ref[...].astype(F32)
    zv = zv_ref[...].astype(F32)
    u, tu = _gelu(zu)
    gv, tv = _gelu(zv)
    xc = gv - jnp.mean(gv, axis=-1, keepdims=True)
    rstd = lax.rsqrt(jnp.mean(xc * xc, axis=-1, keepdims=True) + EPS)
    xh = xc * rstd
    vb = (xh * lng_ref[...] + lnb_ref[...]).astype(BF16)
    return zu, zv, u, tu, tv, xh, rstd, vb


def _mixer_fwd(x, h0, h1, z, lng, lnb, ws, bsb, wo, layer, tm):
    s = x.shape[0]

    def body(x_ref, h0_ref, h1_ref, zu_ref, zv_ref, zg_ref, za_ref, zb_ref, lng_ref, lnb_ref, ws_ref, bsb_ref,
             wo_ref, x1_ref, mg_ref, ya_s):
        _, _, u, _, _, _, _, vb = _gmlp_values(zu_ref, zv_ref, lng_ref, lnb_ref)
        for c in range(tm // HD):
            rs = slice(c * HD, (c + 1) * HD)
            for g in range(HEADS):
                cs = slice(g * HD, (g + 1) * HD)
                ya_s[rs, cs] = u[rs, cs] * (_dot(ws_ref[g], vb[rs, cs]) + bsb_ref[g])
        gg, _ = _gelu(zg_ref[...].astype(F32))
        yb = (h0_ref[...] + h1_ref[...]) * gg
        m32 = _sigmoid(za_ref[...].astype(F32)) * ya_s[...] + _sigmoid(zb_ref[...].astype(F32)) * yb
        mg_ref[...] = m32.T.astype(BF16)
        x1_ref[...] = x_ref[...] + _dot(m32.astype(BF16), wo_ref[...])

    tile = pl.BlockSpec((tm, D), lambda i: (i, 0))
    wspec = _full((HEADS, HD, HD))
    return pl.pallas_call(
        body, name=f"mixer_fwd_l{layer}", grid=(s // tm,),
        in_specs=[tile, tile, tile] + [pl.BlockSpec((tm, D), lambda i, c=c: (i, c)) for c in (0, 1, 3, 4, 5)]
        + [_full((1, D)), _full((1, D)), wspec, wspec, _full((D, D))],
        out_specs=[tile, pl.BlockSpec((D, tm), lambda i: (0, i))], out_shape=[SDS((s, D), F32), SDS((D, s), BF16)],
        scratch_shapes=[pltpu.VMEM((tm, D), F32)],
        compiler_params=_cp("parallel"))(x, h0, h1, z, z, z, z, z, lng, lnb, ws, bsb, wo)


def _conv(taps, cw_ref, cb_ref):
    _, m1, c0, p1, p2 = taps
    return cb_ref[...] + m1 * cw_ref[0:1, :] + c0 * cw_ref[1:2, :] + p1 * cw_ref[2:3, :] + p2 * cw_ref[3:4, :]


def _heads_dot(xb, w_ref, d):
    return jnp.concatenate([_dot(xb[:, h * HD:(h + 1) * HD], w_ref[d, h]) for h in range(HEADS)], axis=1)


def _lru_decay(r, sp):
    la = (-LRU_C) * r * sp
    a = jnp.exp(la)
    return a, jnp.tanh(-la) * (a * a + 1.0)


def _lru_gates_fwd(z, cw, cb, wr, wi, br, bi, lam, layer, tm):
    s = z.shape[0]
    nt = s // tm

    def body(zp_ref, zc_ref, zn_ref, cw_ref, cb_ref, wr_ref, wi_ref, br_ref, bi_ref, lam_ref,
             a0_ref, b0_ref, a1_ref, b1_ref, xc_ref, r0_ref, i0_ref, r1_ref, i1_ref):
        fp, fn = _halo_flags(nt)
        xc = _conv(_taps(*_halo_load(zp_ref, zc_ref, zn_ref, fp, fn), tm), cw_ref, cb_ref)
        xb = xc.astype(BF16)
        xc_ref[...] = xb
        for d, (a_ref, b_ref, r_ref, i_ref) in enumerate(((a0_ref, b0_ref, r0_ref, i0_ref),
                                                          (a1_ref, b1_ref, r1_ref, i1_ref))):
            r = _sigmoid(_heads_dot(xb, wr_ref, d) + br_ref[d:d + 1, :])
            ig = _sigmoid(_heads_dot(xb, wi_ref, d) + bi_ref[d:d + 1, :])
            a, q = _lru_decay(r, _softplus(-lam_ref[d:d + 1, :]))
            a_ref[...] = a
            b_ref[...] = jnp.sqrt(q) * (ig * xc)
            r_ref[...] = r.astype(BF16)
            i_ref[...] = ig.astype(BF16)

    tile = pl.BlockSpec((tm, D), lambda i: (i, 0))
    return pl.pallas_call(
        body, name=f"lru_gates_fwd_l{layer}", grid=(nt,),
        in_specs=[*_halo_specs(tm, s, 2, 16), _full((4, D)), _full((1, D)),
                  _full((2, HEADS, HD, HD)), _full((2, HEADS, HD, HD)), _full((2, D)), _full((2, D)), _full((2, D))],
        out_specs=[tile] * 9, out_shape=[SDS((s, D), F32)] * 4 + [SDS((s, D), BF16)] * 5,
        compiler_params=_cp("parallel"))(z, z, z, cw, cb, wr, wi, br, bi, lam)


def _scan_group(a, x, c, reverse, bwd):
    row = lax.broadcasted_iota(jnp.int32, a.shape, 0)
    b = a * x if bwd else x
    for d in (1, 2, 4):
        keep = (row < 8 - d) if reverse else (row >= d)
        sh = 8 - d if reverse else d
        a_s = jnp.where(keep, pltpu.roll(a, sh, 0), 1.0)
        b_s = jnp.where(keep, pltpu.roll(b, sh, 0), 0.0)
        b = a * b_s + b
        a = a * a_s
    h = b + a * c
    new_c = h[0:1, :] if reverse else h[7:8, :]
    if not bwd:
        return h, new_c
    if reverse:
        prev = jnp.where(row < 7, pltpu.roll(h, 7, 0), c)
    else:
        prev = jnp.where(row >= 1, pltpu.roll(h, 1, 0), c)
    return x + prev, new_c


def _lru_scan(a_f, x_f, a_r, x_r, bwd, layer):
    s = a_f.shape[0]
    ts = min(1024, s // 2)
    cb = 512
    nt = s // ts
    ng = ts // 8

    ring = 3
    n_steps = (D // cb) * nt

    def body(af_hbm, xf_hbm, ar_hbm, xr_hbm, of_ref, or_ref, cf, cr, bufs, sems):
        now = pl.program_id(0) * nt + pl.program_id(1)

        def copies(j):
            j = jnp.asarray(j, jnp.int32)
            col = pl.ds(pl.multiple_of((j // nt) * cb, cb), cb)
            fwd_rows = pl.ds(pl.multiple_of((j % nt) * ts, ts), ts)
            rev_rows = pl.ds(pl.multiple_of((nt - 1 - j % nt) * ts, ts), ts)
            srcs = (af_hbm.at[fwd_rows, col], xf_hbm.at[fwd_rows, col], ar_hbm.at[rev_rows, col], xr_hbm.at[rev_rows, col])
            return [pltpu.make_async_copy(src, bufs.at[k, j % ring], sems.at[k, j % ring]) for k, src in enumerate(srcs)]

        @pl.when(now == 0)
        def _():
            for j in range(ring - 1):
                for cp in copies(j):
                    cp.start()

        @pl.when(now + ring - 1 < n_steps)
        def _():
            for cp in copies(now + ring - 1):
                cp.start()

        for cp in copies(now):
            cp.wait()
        af_ref, xf_ref, ar_ref, xr_ref = (bufs.at[k, now % ring] for k in range(4))

        @pl.when(pl.program_id(1) == 0)
        def _():
            cf[...] = jnp.zeros_like(cf)
            cr[...] = jnp.zeros_like(cr)

        def step(j, carry):
            c_f, c_r = carry
            rf = pl.multiple_of(j * 8, 8)
            rr = pl.multiple_of((ng - 1 - j) * 8, 8)
            o, c_f = _scan_group(af_ref[pl.ds(rf, 8), :], xf_ref[pl.ds(rf, 8), :], c_f, False, bwd)
            of_ref[pl.ds(rf, 8), :] = o
            o, c_r = _scan_group(ar_ref[pl.ds(rr, 8), :], xr_ref[pl.ds(rr, 8), :], c_r, True, bwd)
            or_ref[pl.ds(rr, 8), :] = o
            return c_f, c_r

        c_f, c_r = lax.fori_loop(0, ng, step, (cf[0:1, :], cr[0:1, :]), unroll=2)
        cf[...] = jnp.broadcast_to(c_f, cf.shape)
        cr[...] = jnp.broadcast_to(c_r, cr.shape)

    fwd = pl.BlockSpec((ts, cb), lambda c, t: (t, c))
    rev = pl.BlockSpec((ts, cb), lambda c, t: (nt - 1 - t, c))
    return pl.pallas_call(
        body, name=f"lru_scan_{'bwd' if bwd else 'fwd'}_l{layer}", grid=(D // cb, nt),
        in_specs=[_ANY] * 4, out_specs=[fwd, rev],
        out_shape=[SDS((s, D), F32)] * 2,
        scratch_shapes=[pltpu.VMEM((8, cb), F32), pltpu.VMEM((8, cb), F32), pltpu.VMEM((4, ring, ts, cb), F32),
                        pltpu.SemaphoreType.DMA((4, ring))],
        compiler_params=_cp("arbitrary", "arbitrary"))(a_f, x_f, a_r, x_r)


def _ffn_fwd(x1, g, wfi, wfo, layer, tm, head=None):
    s = x1.shape[0]

    def ffn(x_ref, g_ref, wi_ref, wo_ref, ff_ref, dff_ref, h_ref):
        x = x_ref[...]
        h = _rms_fwd(x, g_ref[...]).astype(BF16)
        h_ref[...] = h
        acc = x
        for k in range(4):
            gate = _dot_nt(h, wi_ref[0, k])
            up = _dot_nt(h, wi_ref[1, k])
            sg = _sigmoid(gate)
            silu = gate * sg
            ff = (silu * up).astype(BF16)
            ff_ref[k] = ff
            dff_ref[0, k] = (up * (sg * (1.0 + gate * (1.0 - sg)))).astype(BF16)
            dff_ref[1, k] = silu.astype(BF16)
            acc = acc + _dot(ff, wo_ref[k])
        return acc

    def body(x_ref, g_ref, wi_ref, wo_ref, x2_ref, ff_ref, dff_ref, h_ref):
        x2_ref[...] = ffn(x_ref, g_ref, wi_ref, wo_ref, ff_ref, dff_ref, h_ref)

    def body_with_head(x_ref, g_ref, wi_ref, wo_ref, fg_ref, t_ref, dx_ref, loss_ref, dfg_ref, ff_ref, dff_ref, h_ref):
        @pl.when(pl.program_id(0) == 0)
        def _():
            loss_ref[...] = jnp.zeros_like(loss_ref)
            dfg_ref[...] = jnp.zeros_like(dfg_ref)

        x2 = ffn(x_ref, g_ref, wi_ref, wo_ref, ff_ref, dff_ref, h_ref)
        fg = fg_ref[...]
        e = _rms_fwd(x2, fg) - t_ref[...]
        rows = jnp.sum(e * e, axis=-1, keepdims=True)
        loss_ref[...] += (0.5 / D) * jnp.sum(rows, axis=0, keepdims=True)
        dx, dg = _rms_bwd(x2, fg, e * (1.0 / D))
        dx_ref[...] = dx
        _add_rows128(dfg_ref, dg)

    tile = pl.BlockSpec((tm, D), lambda i: (i, 0))
    weights = [_resident((2, 4, FF_BLK, D)), _resident((4, FF_BLK, D))]
    kept_specs = [pl.BlockSpec((4, tm, FF_BLK), lambda i: (0, i, 0)),
                  pl.BlockSpec((2, 4, tm, FF_BLK), lambda i: (0, 0, i, 0)), tile]
    kept_shapes = [SDS((4, s, FF_BLK), BF16), SDS((2, 4, s, FF_BLK), BF16), SDS((s, D), BF16)]
    if head is None:
        return pl.pallas_call(
            body, name=f"ffn_fwd_l{layer}", grid=(s // tm,),
            in_specs=[tile, _full((1, D))] + weights, out_specs=[tile] + kept_specs,
            out_shape=[SDS((s, D), F32)] + kept_shapes, compiler_params=_cp("parallel"))(x1, g, wfi, wfo)
    final_g, tgt = head
    return pl.pallas_call(
        body_with_head, name=f"ffn_fwd_loss_l{layer}", grid=(s // tm,),
        in_specs=[tile, _full((1, D))] + weights + [_full((1, D)), tile],
        out_specs=[tile, _full((1, 1)), _full((LANE_ROWS, HD))] + kept_specs,
        out_shape=[SDS((s, D), F32), SDS((1, 1), F32), SDS((LANE_ROWS, HD), F32)] + kept_shapes,
        compiler_params=_cp("arbitrary"))(x1, g, wfi, wfo, final_g, tgt)


def _ffn_bwd(dx2, wfo, factors, wfi, x1, g, layer, tm):
    s = dx2.shape[0]

    def body(dx_ref, wo_ref, f_ref, wi_ref, x_ref, g_ref, dgu_ref, dx1_ref, dg_ref):
        @pl.when(pl.program_id(0) == 0)
        def _():
            dg_ref[...] = jnp.zeros_like(dg_ref)

        dx = dx_ref[...]
        dxb = dx.astype(BF16)
        dh = None
        for k in range(4):
            dff = _dot_nt(dxb, wo_ref[k])
            d_gate = (dff * f_ref[0, k].astype(F32)).astype(BF16)
            d_up = (dff * f_ref[1, k].astype(F32)).astype(BF16)
            dgu_ref[0, k] = d_gate
            dgu_ref[1, k] = d_up
            part = _dot(d_gate, wi_ref[k]) + _dot(d_up, wi_ref[4 + k])
            dh = part if dh is None else dh + part
        dxn, dg = _rms_bwd(x_ref[...], g_ref[...], dh)
        dx1_ref[...] = dx + dxn
        _add_rows128(dg_ref, dg)

    tile = pl.BlockSpec((tm, D), lambda i: (i, 0))
    blk = pl.BlockSpec((2, 4, tm, FF_BLK), lambda i: (0, 0, i, 0))
    return pl.pallas_call(
        body, name=f"ffn_bwd_l{layer}", grid=(s // tm,),
        in_specs=[tile, _resident((4, FF_BLK, D)), blk, _resident((N_DEV, FF_BLK, D)), tile, _full((1, D))],
        out_specs=[blk, tile, _full((LANE_ROWS, HD))],
        out_shape=[SDS((2, 4, s, FF_BLK), BF16), SDS((s, D), F32), SDS((LANE_ROWS, HD), F32)],
        compiler_params=_cp("arbitrary"))(dx2, wfo, factors, wfi, x1, g)


def _mm_nt_rms_bwd(a, a_spec, a_blocks, w, w_is_transposed, x, g, dres, name, tm, after=(), jobs=()):
    s = x.shape[0]

    def body(a_ref, w_ref, x_ref, g_ref, dres_ref, dx_ref, dg_ref):
        @pl.when(pl.program_id(0) == 0)
        def _():
            dg_ref[...] = jnp.zeros_like(dg_ref)

        dh = None
        for k, blk in enumerate(a_blocks(a_ref)):
            part = _dot(blk, w_ref[k]) if w_is_transposed else _dot_nt(blk, w_ref[k])
            dh = part if dh is None else dh + part
        dx, dg = _rms_bwd(x_ref[...], g_ref[...], dh)
        dx_ref[...] = dres_ref[...] + dx
        _add_rows128(dg_ref, dg)

    tile = pl.BlockSpec((tm, D), lambda i: (i, 0))
    body, dep_specs, deps = _behind(after, body, 5)
    body, job_in, job_args, job_out, job_shapes, aliases = _carry(jobs, s // tm, body, 5 + len(deps), 2)
    res = pl.pallas_call(
        body, name=name, grid=(s // tm,),
        in_specs=[a_spec, _resident(w.shape), tile, _full((1, D)), tile] + dep_specs + job_in,
        out_specs=[tile, _full((LANE_ROWS, HD))] + job_out,
        out_shape=[SDS((s, D), F32), SDS((LANE_ROWS, HD), F32)] + job_shapes,
        input_output_aliases=aliases, compiler_params=_cp("arbitrary"))(a, w, x, g, dres, *deps, *job_args)
    for j, job in enumerate(jobs):
        job.results = res[2 + 4 * j:6 + 4 * j]
    return res[0], res[1]


def _mm_tn(a, a_spec, b, b_spec, nb, out_shape, out_spec, name, a_is_transposed=True, after=None, jobs=()):
    def body(a_ref, b_ref, *rest):
        o_ref = rest[-1]
        bb = b_ref[...].astype(BF16)
        o_ref[...] = (_dot(a_ref[...], bb) if a_is_transposed else _dot_tn(a_ref[...], bb)).astype(BF16)

    deps = [] if after is None else [after]
    body, job_in, job_args, job_out, job_shapes, aliases = _carry(jobs, nb, body, 2 + len(deps), 1)
    res = pl.pallas_call(
        body, name=name, grid=(nb,), in_specs=[a_spec, b_spec] + [_ANY] * len(deps) + job_in,
        out_specs=[out_spec] + job_out, out_shape=[SDS(out_shape, BF16)] + job_shapes,
        input_output_aliases=aliases, compiler_params=_cp("parallel"))(a, b, *deps, *job_args)
    for j, job in enumerate(jobs):
        job.results = res[1 + 4 * j:5 + 4 * j]
    return res[0]


def _mixer_bwd(dx1, wo, h0, h1, z, lng, lnb, ws, wst, bsb, layer, tm, after=()):
    s = dx1.shape[0]
    nt = s // tm

    def body(dx_ref, wo_ref, h0_ref, h1_ref, zu_ref, zv_ref, zg_ref, za_ref, zb_ref, lng_ref, lnb_ref,
             ws_ref, wst_ref, bsb_ref, dz_ref, dh_ref, dws_ref, dbs_ref, dlng_ref, dlnb_ref,
             du_s, dv_s, ya_s, dbs_acc):
        i = pl.program_id(0)

        @pl.when(i == 0)
        def _():
            for r in (dws_ref, dlng_ref, dlnb_ref, dbs_acc):
                r[...] = jnp.zeros_like(r)

        dm = _dot_nt(dx_ref[...].astype(BF16), wo_ref[...])
        sa = _sigmoid(za_ref[...].astype(F32))
        sb = _sigmoid(zb_ref[...].astype(F32))
        zg = zg_ref[...].astype(F32)
        gg, tg = _gelu(zg)
        hs = h0_ref[...] + h1_ref[...]
        dyb = dm * sb
        dya = dm * sa
        dh_ref[...] = dyb * gg
        dz_ref[:, 2 * D:3 * D] = jnp.zeros((tm, D), BF16)
        dz_ref[:, 3 * D:4 * D] = (dyb * hs * _gelu_grad(zg, tg)).astype(BF16)
        dz_ref[:, 5 * D:6 * D] = (dm * (hs * gg) * (sb * (1.0 - sb))).astype(BF16)

        zu, zv, u, tu, tv, xh, rstd, vb = _gmlp_values(zu_ref, zv_ref, lng_ref, lnb_ref)
        for c in range(tm // HD):
            rs = slice(c * HD, (c + 1) * HD)
            for g in range(HEADS):
                cs = slice(g * HD, (g + 1) * HD)
                vblk = vb[rs, cs]
                mixed = _dot(ws_ref[g], vblk) + bsb_ref[g]
                ya_s[rs, cs] = u[rs, cs] * mixed
                du_s[rs, cs] = dya[rs, cs] * mixed
                dmx = dya[rs, cs] * u[rs, cs]
                dbs_acc[g] += dmx
                dmxb = dmx.astype(BF16)
                dws_ref[g] += _dot_nt(dmxb, vblk)
                dv_s[rs, cs] = _dot(wst_ref[g], dmxb)
        dz_ref[:, 4 * D:5 * D] = (dm * ya_s[...] * (sa * (1.0 - sa))).astype(BF16)
        dv = dv_s[...]
        _add_rows128(dlng_ref, jnp.sum(dv * xh, axis=0, keepdims=True))
        _add_rows128(dlnb_ref, jnp.sum(dv, axis=0, keepdims=True))
        dxh = dv * lng_ref[...]
        dgv = rstd * (dxh - jnp.mean(dxh, axis=-1, keepdims=True)
                      - xh * jnp.mean(dxh * xh, axis=-1, keepdims=True))
        dz_ref[:, 0:D] = (du_s[...] * _gelu_grad(zu, tu)).astype(BF16)
        dz_ref[:, D:2 * D] = (dgv * _gelu_grad(zv, tv)).astype(BF16)

        @pl.when(i == nt - 1)
        def _():
            for g in range(HEADS):
                dbs_ref[g:g + 1, :] = jnp.sum(dbs_acc[g].T, axis=0, keepdims=True)

    tile = pl.BlockSpec((tm, D), lambda i: (i, 0))
    wspec = _full((HEADS, HD, HD))
    body, dep_specs, deps = _behind(after, body, 14)
    return pl.pallas_call(
        body, name=f"mixer_bwd_l{layer}", grid=(nt,),
        in_specs=[tile, _full((D, D)), tile, tile]
        + [pl.BlockSpec((tm, D), lambda i, c=c: (i, c)) for c in (0, 1, 3, 4, 5)]
        + [_full((1, D)), _full((1, D)), wspec, wspec, wspec] + dep_specs,
        out_specs=[pl.BlockSpec((tm, N_IN), lambda i: (i, 0)), tile, wspec, _full((HEADS, HD)),
                   _full((LANE_ROWS, HD)), _full((LANE_ROWS, HD))],
        out_shape=[SDS((s, N_IN), BF16), SDS((s, D), F32), SDS((HEADS, HD, HD), F32), SDS((HEADS, HD), F32),
                   SDS((LANE_ROWS, HD), F32), SDS((LANE_ROWS, HD), F32)],
        scratch_shapes=[pltpu.VMEM((tm, D), F32)] * 3 + [pltpu.VMEM((HEADS, HD, HD), F32)],
        compiler_params=_cp("arbitrary"))(dx1, wo, h0, h1, z, z, z, z, z, lng, lnb, ws, wst, bsb, *deps)


def _lru_gates_bwd(xcb, gates, h0, h1, g0, g1, wr, wi, lam, layer, tm, after=()):
    s = xcb.shape[0]
    nt = s // tm

    def body(xc_ref, r0_ref, i0_ref, r1_ref, i1_ref, h0p_ref, h0_ref, h1_ref, h1n_ref, g0_ref, g1_ref,
             wr_ref, wi_ref, lam_ref, dxc_ref, dwr_ref, dwi_ref, dbr_ref, dbi_ref, dlam_ref):
        i = pl.program_id(0)
        fp, fn = _halo_flags(nt)

        @pl.when(i == 0)
        def _():
            for r in (dwr_ref, dwi_ref, dbr_ref, dbi_ref, dlam_ref):
                r[...] = jnp.zeros_like(r)

        xb = xc_ref[...]
        xc = xb.astype(F32)
        zeros8 = jnp.zeros((8, D), F32)
        h_prev = _taps(h0p_ref[...] * fp, h0_ref[...], zeros8, tm)[1]
        h_next = _taps(zeros8, h1_ref[...], h1n_ref[...] * fn, tm)[3]
        dxc = jnp.zeros((tm, D), F32)
        for d, (g_ref, hsh, r_ref, i_ref) in enumerate(((g0_ref, h_prev, r0_ref, i0_ref),
                                                        (g1_ref, h_next, r1_ref, i1_ref))):
            sp = _softplus(-lam_ref[d:d + 1, :])
            r = r_ref[...].astype(F32)
            ig = i_ref[...].astype(F32)
            a, q = _lru_decay(r, sp)
            rmult = jnp.where(q > 0.0, lax.rsqrt(jnp.where(q > 0.0, q, 1.0)), 0.0)
            mult = q * rmult
            db = g_ref[...]
            da = db * hsh
            dmult = db * (ig * xc)
            di = db * (mult * xc)
            dxc = dxc + db * (mult * ig)
            dla = da * a - dmult * (a * a * rmult)
            dsp_dlam = -_sigmoid(-lam_ref[d:d + 1, :])
            _add_rows128(dlam_ref, jnp.sum(dla * r, axis=0, keepdims=True) * ((-LRU_C) * dsp_dlam), d * LANE_ROWS)
            dpr = dla * sp * (-LRU_C) * (r * (1.0 - r))
            dpi = di * (ig * (1.0 - ig))
            _add_rows128(dbr_ref, jnp.sum(dpr, axis=0, keepdims=True), d * LANE_ROWS)
            _add_rows128(dbi_ref, jnp.sum(dpi, axis=0, keepdims=True), d * LANE_ROWS)
            dprb = dpr.astype(BF16)
            dpib = dpi.astype(BF16)
            parts = []
            for h in range(HEADS):
                cs = slice(h * HD, (h + 1) * HD)
                dwr_ref[d, h] += _dot_tn(xb[:, cs], dprb[:, cs])
                dwi_ref[d, h] += _dot_tn(xb[:, cs], dpib[:, cs])
                parts.append(_dot_nt(dprb[:, cs], wr_ref[d, h]) + _dot_nt(dpib[:, cs], wi_ref[d, h]))
            dxc = dxc + jnp.concatenate(parts, axis=1)
        dxc_ref[...] = dxc.astype(BF16)

    tile = pl.BlockSpec((tm, D), lambda i: (i, 0))
    hp, hc, hn = _halo_specs(tm, s, 0)
    wspec = _full((2, HEADS, HD, HD))
    vspec = _full((2 * LANE_ROWS, HD))
    body, dep_specs, deps = _behind(after, body, 14)
    return pl.pallas_call(
        body, name=f"lru_gates_bwd_l{layer}", grid=(nt,),
        in_specs=[tile] * 5 + [hp, hc, hc, hn, tile, tile, wspec, wspec, _full((2, D))] + dep_specs,
        out_specs=[tile, wspec, wspec, vspec, vspec, vspec],
        out_shape=[SDS((s, D), BF16), SDS((2, HEADS, HD, HD), F32), SDS((2, HEADS, HD, HD), F32)]
        + [SDS((2 * LANE_ROWS, HD), F32)] * 3,
        compiler_params=_cp("arbitrary"))(xcb, *gates, h0, h0, h1, h1, g0, g1, wr, wi, lam, *deps)


def _conv_bwd(dz, dxc, z, cw, layer, tm):
    s = z.shape[0]
    nt = s // tm

    def body(dz_in, dp_ref, dc_ref, dn_ref, zp_ref, zc_ref, zn_ref, cw_ref, dz_ref, dcw_ref, dcb_ref):
        del dz_in
        fp, fn = _halo_flags(nt)

        @pl.when(pl.program_id(0) == 0)
        def _():
            dcw_ref[...] = jnp.zeros_like(dcw_ref)
            dcb_ref[...] = jnp.zeros_like(dcb_ref)

        dxc_halo = _halo_load(dp_ref, dc_ref, dn_ref, fp, fn)
        dxc = dxc_halo[1]
        dm2, dm1, _, dp1, _ = _taps(*dxc_halo, tm)
        dz_ref[...] = (cw_ref[0:1, :] * dp1 + cw_ref[1:2, :] * dxc + cw_ref[2:3, :] * dm1
                       + cw_ref[3:4, :] * dm2).astype(BF16)
        _, zm1, z0, zp1, zp2 = _taps(*_halo_load(zp_ref, zc_ref, zn_ref, fp, fn), tm)
        for k, zt in enumerate((zm1, z0, zp1, zp2)):
            _add_rows128(dcw_ref, jnp.sum(dxc * zt, axis=0, keepdims=True), k * LANE_ROWS)
        _add_rows128(dcb_ref, jnp.sum(dxc, axis=0, keepdims=True))

    return pl.pallas_call(
        body, name=f"conv_bwd_l{layer}", grid=(nt,),
        in_specs=[pl.BlockSpec(memory_space=pl.ANY), *_halo_specs(tm, s, 0, 16), *_halo_specs(tm, s, 2, 16),
                  _full((4, D))],
        out_specs=[pl.BlockSpec((tm, D), lambda i: (i, 2)), _full((4 * LANE_ROWS, HD)), _full((LANE_ROWS, HD))],
        out_shape=[SDS((s, N_IN), BF16), SDS((4 * LANE_ROWS, HD), F32), SDS((LANE_ROWS, HD), F32)],
        input_output_aliases={0: 0},
        compiler_params=_cp("arbitrary"))(dz, dxc, dxc, dxc, z, z, z, cw)


def _me():
    return lax.axis_index("x"), lax.axis_index("y"), lax.axis_index("c")


def _peer(m):
    x, y, c = _me()
    px = 1 - x if m & 4 else x
    py = 1 - y if m & 2 else y
    pc = 1 - c if m & 1 else c
    return (px, py, pc), 4 * px + 2 * py + pc


_ANY = pl.BlockSpec(memory_space=pl.ANY)
_EXCHANGE_SEMS = [pltpu.SemaphoreType.DMA((N_DEV - 1,)), pltpu.SemaphoreType.DMA((N_DEV - 1,)), pltpu.SemaphoreType.DMA(())]


def _all_gather(v, after, name):
    def body(v_ref, after_ref, o_ref, send_sems, recv_sems, local_sem):
        del after_ref
        x, y, c = _me()
        me = 4 * x + 2 * y + c
        local = pltpu.make_async_copy(v_ref, o_ref.at[me], local_sem)
        local.start()
        sends = []
        for m in range(1, N_DEV):
            dev, _ = _peer(m)
            cp = pltpu.make_async_remote_copy(v_ref, o_ref.at[me], send_sems.at[m - 1], recv_sems.at[m - 1],
                                              device_id=dev, device_id_type=pl.DeviceIdType.MESH)
            cp.start()
            sends.append(cp)
        for m in range(1, N_DEV):
            dev, blk = _peer(m)
            pltpu.make_async_remote_copy(v_ref, o_ref.at[blk], send_sems.at[m - 1], recv_sems.at[m - 1],
                                         device_id=dev, device_id_type=pl.DeviceIdType.MESH).wait_recv()
        for cp in sends:
            cp.wait_send()
        local.wait()

    return pl.pallas_call(
        body, name=name, in_specs=[_ANY, _ANY], out_specs=_ANY,
        out_shape=SDS((N_DEV,) + v.shape, v.dtype), scratch_shapes=_EXCHANGE_SEMS)(v, after)


_HBM = pl.BlockSpec(memory_space=pltpu.HBM)
_SEM = pl.BlockSpec(memory_space=pltpu.SEMAPHORE)
_EFFECT = pltpu.CompilerParams(has_side_effects=pltpu.SideEffectType.DATAFLOW_SIDE_EFFECTING)
_PEER_SEMS = pltpu.SemaphoreType.DMA((N_DEV - 1,))


def _in_hbm(a):
    return pltpu.with_memory_space_constraint(a, pltpu.HBM)


def _remote(src, dst, send_sems, recv_sems, m):
    dev, _ = _peer(m)
    return pltpu.make_async_remote_copy(src, dst, send_sems.at[m - 1], recv_sems.at[m - 1],
                                        device_id=dev, device_id_type=pl.DeviceIdType.MESH)


def _gather_start(lands, after, name):
    n = len(lands)

    def body(*refs):
        land = refs[:n]
        sems = refs[n + 1:3 * n + 1]
        token = refs[-1]
        x, y, c = _me()
        me = 4 * x + 2 * y + c
        for t in range(n):
            for m in range(1, N_DEV):
                _remote(land[t].at[me], land[t].at[me], sems[2 * t], sems[2 * t + 1], m).start()
        token[...] = jnp.zeros_like(token)

    res = pl.pallas_call(
        body, name=name, in_specs=[_HBM] * n + [_ANY],
        out_specs=[_SEM] * (2 * n) + [_HBM] * n + [pl.BlockSpec(memory_space=pltpu.VMEM)],
        out_shape=[_PEER_SEMS] * (2 * n) + [pltpu.HBM(a.shape, a.dtype) for a in lands] + [SDS((8, 128), F32)],
        input_output_aliases={t: 2 * n + t for t in range(n)},
        compiler_params=_EFFECT)(*[_in_hbm(a) for a in lands], after)
    return [(res[2 * t], res[2 * t + 1], res[2 * n + t]) for t in range(n)], res[-1]


def _gather_wait(handle, after, name):
    send_sems, recv_sems, land = handle

    def body(land_ref, ssem, rsem, after_ref, out_ref):
        del after_ref, out_ref
        x, y, c = _me()
        me = 4 * x + 2 * y + c
        for m in range(1, N_DEV):
            _, blk = _peer(m)
            cp = _remote(land_ref.at[me], land_ref.at[blk], ssem, rsem, m)
            cp.wait_send()
            cp.wait_recv()

    return pl.pallas_call(
        body, name=name, in_specs=[_HBM, _SEM, _SEM, _ANY], out_specs=_HBM,
        out_shape=pltpu.HBM(land.shape, land.dtype), input_output_aliases={0: 0},
        compiler_params=_EFFECT)(land, send_sems, recv_sems, after)


FIRST_STAGE = (1, 2, 4, 6)
RELAYED = (2, 4, 6)
OTHER_CORE = 1


def _stage_copy(src, dst, send_sems, recv_sems, k, m):
    dev, _ = _peer(m)
    return pltpu.make_async_remote_copy(src, dst, send_sems.at[k], recv_sems.at[k],
                                        device_id=dev, device_id_type=pl.DeviceIdType.MESH)


def _gather2_start(lands, after, name):
    n = len(lands)

    def body(*refs):
        land = refs[:n]
        sems = refs[n + 1:3 * n + 1]
        token = refs[-1]
        x, y, c = _me()
        me = 4 * x + 2 * y + c
        for t in range(n):
            for k, m in enumerate(FIRST_STAGE):
                _stage_copy(land[t].at[me], land[t].at[me], sems[2 * t], sems[2 * t + 1], k, m).start()
        token[...] = jnp.zeros_like(token)

    stage_sems = pltpu.SemaphoreType.DMA((len(FIRST_STAGE),))
    res = pl.pallas_call(
        body, name=name, in_specs=[_HBM] * n + [_ANY],
        out_specs=[_SEM] * (2 * n) + [_HBM] * n + [pl.BlockSpec(memory_space=pltpu.VMEM)],
        out_shape=[stage_sems] * (2 * n) + [pltpu.HBM(a.shape, a.dtype) for a in lands] + [SDS((8, 128), F32)],
        input_output_aliases={t: 2 * n + t for t in range(n)},
        compiler_params=_EFFECT)(*[_in_hbm(a) for a in lands], after)
    return [(res[2 * t], res[2 * t + 1], res[2 * n + t]) for t in range(n)], res[-1]


def _gather2_relay(handles, after, name):
    n = len(handles)

    def body(*refs):
        land, send1, recv1 = refs[:n], refs[n:2 * n], refs[2 * n:3 * n]
        sems = refs[3 * n + 1:5 * n + 1]
        token = refs[-1]
        x, y, c = _me()
        me = 4 * x + 2 * y + c
        for t in range(n):
            for j, m in enumerate(RELAYED):
                _, blk = _peer(m)
                _stage_copy(land[t].at[me], land[t].at[blk], send1[t], recv1[t], 1 + j, m).wait_recv()
                _stage_copy(land[t].at[blk], land[t].at[blk], sems[2 * t], sems[2 * t + 1], j, OTHER_CORE).start()
        token[...] = jnp.zeros_like(token)

    relay_sems = pltpu.SemaphoreType.DMA((len(RELAYED),))
    lands = [h[2] for h in handles]
    res = pl.pallas_call(
        body, name=name, in_specs=[_HBM] * n + [_SEM] * (2 * n) + [_ANY],
        out_specs=[_SEM] * (2 * n) + [_HBM] * n + [pl.BlockSpec(memory_space=pltpu.VMEM)],
        out_shape=[relay_sems] * (2 * n) + [pltpu.HBM(a.shape, a.dtype) for a in lands] + [SDS((8, 128), F32)],
        input_output_aliases={t: 2 * n + t for t in range(n)},
        compiler_params=_EFFECT)(*lands, *[h[0] for h in handles], *[h[1] for h in handles], after)
    return [(h[0], h[1], res[2 * t], res[2 * t + 1], res[2 * n + t]) for t, h in enumerate(handles)], res[-1]


def _gather2_wait(handle, after, name):
    send1, recv1, send2, recv2, land = handle

    def body(land_ref, s1, r1, s2, r2, after_ref, out_ref):
        del after_ref, out_ref
        x, y, c = _me()
        me = 4 * x + 2 * y + c
        _, other = _peer(OTHER_CORE)
        _stage_copy(land_ref.at[me], land_ref.at[other], s1, r1, 0, OTHER_CORE).wait_recv()
        for k, m in enumerate(FIRST_STAGE):
            _stage_copy(land_ref.at[me], land_ref.at[me], s1, r1, k, m).wait_send()
        for j, m in enumerate(RELAYED):
            _, mine = _peer(m)
            _, theirs = _peer(m ^ OTHER_CORE)
            _stage_copy(land_ref.at[mine], land_ref.at[mine], s2, r2, j, OTHER_CORE).wait_send()
            _stage_copy(land_ref.at[mine], land_ref.at[theirs], s2, r2, j, OTHER_CORE).wait_recv()

    return pl.pallas_call(
        body, name=name, in_specs=[_HBM] + [_SEM] * 4 + [_ANY], out_specs=_HBM,
        out_shape=pltpu.HBM(land.shape, land.dtype), input_output_aliases={0: 0},
        compiler_params=_EFFECT)(land, send1, recv1, send2, recv2, after)


def _exchange_start(ps, name):
    n = len(ps)

    def body(*refs):
        p = refs[:n]
        got = refs[n:2 * n]
        sems = refs[2 * n:5 * n]
        token = refs[-1]
        x, y, c = _me()
        me = 4 * x + 2 * y + c
        for t in range(n):
            pltpu.make_async_copy(p[t].at[me], got[t].at[me], sems[3 * t + 2]).start()
            for m in range(1, N_DEV):
                _, blk = _peer(m)
                _remote(p[t].at[blk], got[t].at[me], sems[3 * t], sems[3 * t + 1], m).start()
        token[...] = jnp.zeros_like(token)

    res = pl.pallas_call(
        body, name=name, in_specs=[_HBM] * (2 * n),
        out_specs=[_SEM] * (3 * n) + [_HBM] * (2 * n) + [pl.BlockSpec(memory_space=pltpu.VMEM)],
        out_shape=[_PEER_SEMS, _PEER_SEMS, pltpu.SemaphoreType.DMA(())] * n
        + [pltpu.HBM(a.shape, a.dtype) for a in ps] * 2 + [SDS((8, 128), F32)],
        input_output_aliases={t: 3 * n + t for t in range(2 * n)},
        compiler_params=_EFFECT)(*[_in_hbm(a) for a in ps], *[_in_hbm(lax.empty(a.shape, a.dtype)) for a in ps])
    return [(res[3 * t], res[3 * t + 1], res[3 * t + 2], res[3 * n + t], res[4 * n + t]) for t in range(n)], res[-1]


def _exchange_wait(handle, after, name):
    send_sems, recv_sems, local_sem, p, got = handle

    def body(p_ref, got_ref, ssem, rsem, lsem, after_ref, p_out, got_out):
        del after_ref, p_out, got_out
        x, y, c = _me()
        me = 4 * x + 2 * y + c
        pltpu.make_async_copy(p_ref.at[me], got_ref.at[me], lsem).wait()
        for m in range(1, N_DEV):
            _, blk = _peer(m)
            cp = _remote(p_ref.at[blk], got_ref.at[blk], ssem, rsem, m)
            cp.wait_send()
            cp.wait_recv()

    return pl.pallas_call(
        body, name=name, in_specs=[_HBM, _HBM, _SEM, _SEM, _SEM, _ANY], out_specs=[_HBM, _HBM],
        out_shape=[pltpu.HBM(p.shape, p.dtype), pltpu.HBM(got.shape, got.dtype)],
        input_output_aliases={0: 0, 1: 1}, compiler_params=_EFFECT)(p, got, send_sems, recv_sems, local_sem, after)[1]


CHIPS = (0, 2, 4, 6)


def _pairs_start(p, name):
    def body(p_ref, pair_ref, ssem, rsem, p_out, pair_out, token):
        del p_out, pair_out
        for k, chip in enumerate(CHIPS):
            _, blk = _peer(chip ^ OTHER_CORE)
            _stage_copy(p_ref.at[blk], pair_ref.at[k], ssem, rsem, k, OTHER_CORE).start()
        token[...] = jnp.zeros_like(token)

    sems = pltpu.SemaphoreType.DMA((len(CHIPS),))
    pair = lax.empty((len(CHIPS),) + p.shape[1:], p.dtype)
    res = pl.pallas_call(
        body, name=name, in_specs=[_HBM] * 2,
        out_specs=[_SEM] * 2 + [_HBM] * 2 + [pl.BlockSpec(memory_space=pltpu.VMEM)],
        out_shape=[sems, sems, pltpu.HBM(p.shape, p.dtype), pltpu.HBM(pair.shape, pair.dtype), SDS((8, 128), F32)],
        input_output_aliases={0: 2, 1: 3}, compiler_params=_EFFECT)(_in_hbm(p), _in_hbm(pair))
    return res[:4], res[4]


def _pairs_wait(handle, after, name):
    send_sems, recv_sems, p, pair = handle

    def body(p_ref, pair_ref, ssem, rsem, after_ref, p_out, pair_out):
        del after_ref, p_out, pair_out
        for k, chip in enumerate(CHIPS):
            _, blk = _peer(chip ^ OTHER_CORE)
            cp = _stage_copy(p_ref.at[blk], pair_ref.at[k], ssem, rsem, k, OTHER_CORE)
            cp.wait_send()
            cp.wait_recv()

    return pl.pallas_call(
        body, name=name, in_specs=[_HBM, _HBM, _SEM, _SEM, _ANY], out_specs=[_HBM, _HBM],
        out_shape=[pltpu.HBM(p.shape, p.dtype), pltpu.HBM(pair.shape, pair.dtype)],
        input_output_aliases={0: 0, 1: 1}, compiler_params=_EFFECT)(p, pair, send_sems, recv_sems, after)


def _sum_pairs(p, pair, me1, name):
    _, r, c = pair.shape
    tr = _row_tile(r)

    def body(me_ref, p_ref, pair_ref, o_ref):
        del me_ref
        o_ref[...] = (p_ref[...].astype(F32) + pair_ref[...].astype(F32)).astype(o_ref.dtype)

    def mine(k, i, me):
        chip = 2 * k
        return (jnp.bitwise_xor(me[0], chip), i, 0)

    assert CHIPS == tuple(2 * k for k in range(len(CHIPS)))
    blk = pl.BlockSpec((None, tr, c), lambda k, i, me: (k, i, 0))
    return pl.pallas_call(
        body, name=name,
        grid_spec=pltpu.PrefetchScalarGridSpec(
            num_scalar_prefetch=1, grid=(len(CHIPS), r // tr),
            in_specs=[pl.BlockSpec((None, tr, c), mine), blk], out_specs=blk),
        out_shape=SDS(pair.shape, pair.dtype), compiler_params=_cp("parallel", "parallel"))(me1, p, pair)


def _chips_start(q, name):
    def body(q_ref, got_ref, ssem, rsem, lsem, q_out, got_out, token):
        del q_out, got_out
        pltpu.make_async_copy(q_ref.at[0], got_ref.at[0], lsem).start()
        for k, chip in enumerate(CHIPS[1:]):
            _stage_copy(q_ref.at[k + 1], got_ref.at[k + 1], ssem, rsem, k, chip).start()
        token[...] = jnp.zeros_like(token)

    sems = pltpu.SemaphoreType.DMA((len(CHIPS) - 1,))
    res = pl.pallas_call(
        body, name=name, in_specs=[_HBM] * 2,
        out_specs=[_SEM] * 3 + [_HBM] * 2 + [pl.BlockSpec(memory_space=pltpu.VMEM)],
        out_shape=[sems, sems, pltpu.SemaphoreType.DMA(()), pltpu.HBM(q.shape, q.dtype), pltpu.HBM(q.shape, q.dtype),
                   SDS((8, 128), F32)],
        input_output_aliases={0: 3, 1: 4}, compiler_params=_EFFECT)(_in_hbm(q), _in_hbm(lax.empty(q.shape, q.dtype)))
    return res[:5], res[5]


def _chips_wait(handle, after, name):
    send_sems, recv_sems, local_sem, q, got = handle

    def body(q_ref, got_ref, ssem, rsem, lsem, after_ref, q_out, got_out):
        del after_ref, q_out, got_out
        pltpu.make_async_copy(q_ref.at[0], got_ref.at[0], lsem).wait()
        for k, chip in enumerate(CHIPS[1:]):
            cp = _stage_copy(q_ref.at[k + 1], got_ref.at[k + 1], ssem, rsem, k, chip)
            cp.wait_send()
            cp.wait_recv()

    return pl.pallas_call(
        body, name=name, in_specs=[_HBM, _HBM, _SEM, _SEM, _SEM, _ANY], out_specs=[_HBM, _HBM],
        out_shape=[pltpu.HBM(q.shape, q.dtype), pltpu.HBM(got.shape, got.dtype)],
        input_output_aliases={0: 0, 1: 1}, compiler_params=_EFFECT)(q, got, send_sems, recv_sems, local_sem, after)[1]


def _cast_into_slot(w, layer, me1, name):
    _, r, c = w.shape
    tr = next(t for t in (512, 352, r) if r % t == 0)

    def body(me_ref, w_ref, o_ref):
        del me_ref
        o_ref[...] = w_ref[...].astype(BF16)

    return pl.pallas_call(
        body, name=name,
        grid_spec=pltpu.PrefetchScalarGridSpec(
            num_scalar_prefetch=1, grid=(r // tr,),
            in_specs=[pl.BlockSpec((None, tr, c), lambda i, me: (layer, i, 0))],
            out_specs=pl.BlockSpec((None, tr, c), lambda i, me: (me[0], i, 0))),
        out_shape=SDS((N_DEV, r, c), BF16), compiler_params=_cp("arbitrary"))(me1, w)


def _cast_all_into_slots(ws, layers, me1, after, name):
    n = len(ws)

    def body(me_ref, *refs):
        del me_ref
        for w_ref, o_ref in zip(refs[:n], refs[n + 1:]):
            o_ref[...] = w_ref[...].astype(BF16)

    return pl.pallas_call(
        body, name=name,
        grid_spec=pltpu.PrefetchScalarGridSpec(
            num_scalar_prefetch=1, grid=(1,),
            in_specs=[pl.BlockSpec((None,) + a.shape[1:], lambda i, me, l=l: (l, 0, 0)) for a, l in zip(ws, layers)]
            + [_ANY],
            out_specs=[pl.BlockSpec((None,) + a.shape[1:], lambda i, me: (me[0], 0, 0)) for a in ws]),
        out_shape=[SDS((N_DEV,) + a.shape[1:], BF16) for a in ws],
        compiler_params=_cp("arbitrary"))(me1, *ws, after)


def _sum8_into_slot(p, me1, name):
    _, r, c = p.shape

    def body(me_ref, p_ref, o_ref):
        del me_ref
        acc = p_ref[0]
        for k in range(1, N_DEV):
            acc = acc + p_ref[k]
        o_ref[...] = acc

    return pl.pallas_call(
        body, name=name,
        grid_spec=pltpu.PrefetchScalarGridSpec(
            num_scalar_prefetch=1, grid=(1,),
            in_specs=[pl.BlockSpec(p.shape, lambda i, me: (0, 0, 0))],
            out_specs=pl.BlockSpec((None, r, c), lambda i, me: (me[0], 0, 0))),
        out_shape=SDS(p.shape, F32), compiler_params=_cp("arbitrary"))(me1, p)


def _adamw(w, g, m, v):
    m = ADAM_B1 * m + (1.0 - ADAM_B1) * g
    v = ADAM_B2 * v + (1.0 - ADAM_B2) * (g * g)
    m_hat = m / (1.0 - ADAM_B1 ** ADAM_STEP)
    v_hat = v / (1.0 - ADAM_B2 ** ADAM_STEP)
    delta = -ADAM_LR * (m_hat / (jnp.sqrt(v_hat) + ADAM_EPS) + ADAM_WD * w)
    return delta, m, v


def _adam_tile(p_ref, w_ref, m_ref, v_ref, g_ref, d_ref, nm_ref, nv_ref):
    g = p_ref[0].astype(F32)
    for k in range(1, p_ref.shape[0]):
        g = g + p_ref[k].astype(F32)
    delta, nm, nv = _adamw(w_ref[...], g, m_ref[...], v_ref[...])
    g_ref[...] = g
    d_ref[...] = delta
    nm_ref[...] = nm
    nv_ref[...] = nv


class _AdamJob:
    def __init__(self, parts, w, m, v, layer, prev):
        self.args = [parts, w, m, v] + list(prev or ())
        self.layer, self.results = layer, None


def _carry(jobs, steps, body, n_in, n_out):
    in_specs, args, out_specs, out_shapes, aliases, n_prevs = [], [], [], [], {}, []
    for j, job in enumerate(jobs):
        _, r, c = job.args[0].shape
        nr = next(n for n in range(steps, 0, -1) if steps % n == 0 and r % (16 * n) == 0)
        nc = steps // nr
        assert c % (128 * nc) == 0
        tile = (r // nr, c // nc)
        blk = pl.BlockSpec((None,) + tile, lambda i, layer=job.layer, nc=nc: (layer, i // nc, i % nc))
        n_prev = len(job.args) - 4
        aliases.update({n_in + len(args) + 4 + k: n_out + 4 * j + k for k in range(n_prev)})
        in_specs += [pl.BlockSpec(job.args[0].shape[:1] + tile, lambda i, nc=nc: (0, i // nc, i % nc)), blk, blk, blk]
        in_specs += [_ANY] * n_prev
        args += job.args
        out_specs += [blk] * 4
        out_shapes += [SDS(job.args[1].shape, F32)] * 4
        n_prevs.append(n_prev)

    def carrying(*refs):
        ins, outs = refs[:n_in + len(args)], refs[n_in + len(args):]
        body(*ins[:n_in], *outs[:n_out])
        k = n_in
        for j, n_prev in enumerate(n_prevs):
            _adam_tile(*ins[k:k + 4], *outs[n_out + 4 * j:n_out + 4 * j + 4])
            k += 4 + n_prev

    return carrying, in_specs, args, out_specs, out_shapes, aliases


def _adam_shard(parts, w, m, v, layer, prev, name):
    n, r, c = parts.shape
    tr = next(t for t in (512, 352, r) if r % t == 0)
    n_prev = 0 if prev is None else 4

    def body(*refs):
        _adam_tile(*refs[:4], *refs[4 + n_prev:])

    blk = pl.BlockSpec((None, tr, c), lambda i: (layer, i, 0))
    return pl.pallas_call(
        body, name=name, grid=(r // tr,),
        in_specs=[pl.BlockSpec((n, tr, c), lambda i: (0, i, 0)), blk, blk, blk] + [_ANY] * n_prev,
        out_specs=[blk] * 4, out_shape=[SDS(w.shape, F32)] * 4,
        input_output_aliases={4 + k: k for k in range(n_prev)},
        compiler_params=_cp("parallel"))(parts, w, m, v, *(prev or ()))


SMALL_MATRICES = [("lru_w_r", 2048), ("lru_w_i", 2048), ("gmlp_w_s", 1024)]
SMALL_VECTORS = [("norm1_g", 8), ("gmlp_ln_g", 8), ("gmlp_ln_b", 8), ("gmlp_b_s", 8), ("conv_w", 32), ("conv_b", 8),
                 ("lru_b_r", 16), ("lru_b_i", 16), ("lru_lambda", 16), ("norm2_g", 8), ("final_g", 8)]
SMALL_VECTOR_ROW0 = sum(n for _, n in SMALL_MATRICES)
SMALL_VECTOR_BLOCK = 256
SMALL_ROWS = SMALL_VECTOR_ROW0 + SMALL_VECTOR_BLOCK
LOSS_ROW = SMALL_VECTOR_ROW0 + sum(n for _, n in SMALL_VECTORS)
assert LOSS_ROW + LANE_ROWS <= SMALL_ROWS


def _pack_small(small):
    parts = [small[k] for k, _ in SMALL_MATRICES]
    parts += [small[k] if k in small else jnp.zeros((n, HD), F32) for k, n in SMALL_VECTORS]
    parts += [small["loss"]] if "loss" in small else []
    flat = jnp.concatenate(parts)
    return jnp.pad(flat, ((0, SMALL_ROWS - flat.shape[0]), (0, 0))).reshape(N_DEV, SMALL_ROWS // N_DEV, HD)


def _adam_matrix(g0, g1, w, m, v, row0, name):
    _, rows, _ = w.shape
    tr = 512

    def body(g0_ref, g1_ref, w_ref, m_ref, v_ref, g_ref, d_ref, nm_ref, nv_ref):
        for l, src in enumerate((g0_ref, g1_ref)):
            g = src[...]
            delta, nm, nv = _adamw(w_ref[l], g, m_ref[l], v_ref[l])
            g_ref[l] = g
            d_ref[l] = delta
            nm_ref[l] = nm
            nv_ref[l] = nv

    gspec = pl.BlockSpec((tr, HD), lambda i: (row0 // tr + i, 0))
    blk = pl.BlockSpec((2, tr, HD), lambda i: (0, i, 0))
    return pl.pallas_call(body, name=name, grid=(rows // tr,), in_specs=[gspec, gspec] + [blk] * 3,
                          out_specs=[blk] * 4, out_shape=[SDS(w.shape, F32)] * 4,
                          compiler_params=_cp("parallel"))(g0, g1, w, m, v)


def _adam_vectors(g0, g1, dg1_parts, me1, ws, ms, vs):
    names = [k for k, _ in SMALL_VECTORS]
    n = len(names)

    def lanes(rows8):
        return jnp.concatenate([rows8[k:k + 1, :] for k in range(LANE_ROWS)], axis=1)

    def body(me_ref, g0_ref, g1_ref, dg1_ref, *refs):
        w_refs, m_refs, v_refs = refs[:n], refs[n:2 * n], refs[2 * n:3 * n]
        outs = refs[3 * n:]
        me = me_ref[0]
        g_refs = (g0_ref, g1_ref)

        def emit(i, idx, g):
            delta, nm, nv = _adamw(w_refs[i][idx], g, m_refs[i][idx], v_refs[i][idx])
            for j, val in enumerate((g, delta, nm, nv)):
                outs[4 * i + j][idx] = val

        off = 0
        for i, (name, rows) in enumerate(SMALL_VECTORS):
            for l in range(2):
                row = (slice(l, l + 1), slice(None))
                if name == "final_g":
                    if l == 1:
                        emit(i, (slice(0, 1), slice(None)), lanes(g1_ref[off:off + rows, :]))
                elif name == "norm1_g":
                    if l == 1:
                        emit(i, row, lanes(g0_ref[off:off + rows, :]))
                    else:
                        total = dg1_ref[0]
                        for k in range(1, N_DEV):
                            total = total + dg1_ref[k]
                        emit(i, row, lanes(total))
                elif name == "gmlp_b_s":
                    emit(i, (l,), g_refs[l][off:off + rows, :])
                elif rows == LANE_ROWS:
                    emit(i, row, lanes(g_refs[l][off:off + rows, :]))
                else:
                    for r in range(rows // LANE_ROWS):
                        emit(i, (l, slice(r, r + 1), slice(None)), g_refs[l][pl.ds(off + r * LANE_ROWS + me, 1), :])
            off += rows

    args = [ws[k] for k in names] + [ms[k] for k in names] + [vs[k] for k in names]
    gspec = pl.BlockSpec((SMALL_VECTOR_BLOCK, HD), lambda i, me: (SMALL_VECTOR_ROW0 // SMALL_VECTOR_BLOCK, 0))
    res = pl.pallas_call(
        body, name="adam_vectors",
        grid_spec=pltpu.PrefetchScalarGridSpec(
            num_scalar_prefetch=1, grid=(1,),
            in_specs=[gspec, gspec, _full(dg1_parts.shape)] + [_full(a.shape) for a in args],
            out_specs=[_full(ws[k].shape) for k in names for _ in range(4)]),
        out_shape=[SDS(ws[k].shape, F32) for k in names for _ in range(4)],
        compiler_params=_cp("arbitrary"))(me1, g0, g1, dg1_parts, *args)
    return {k: list(res[4 * i:4 * i + 4]) for i, k in enumerate(names)}


def _local_step(x, tgt, p, get_w, hook=lambda stage, layer, payload: None):
    s = x.shape[0]
    tm = _row_tile(s)
    wsb = p["gmlp_w_s"].astype(BF16)
    wstb = jnp.swapaxes(p["gmlp_w_s"], -1, -2).astype(BF16)
    bsb = jnp.broadcast_to(p["gmlp_b_s"][..., None], p["gmlp_w_s"].shape)
    wrb = p["lru_w_r"].astype(BF16)
    wib = p["lru_w_i"].astype(BF16)
    saved = []
    for l in range(2):
        win = get_w("w_in", l, x)
        z, h1 = _norm_inproj(x, p["norm1_g"][l][None], win, l, tm, after=[hook("pre_inproj", l, win)])
        a0, b0, a1, b1, xcb, *gates = _lru_gates_fwd(z, p["conv_w"][l], p["conv_b"][l][None], wrb[l], wib[l],
                                                     p["lru_b_r"][l], p["lru_b_i"][l], p["lru_lambda"][l], l, tm)
        h0, hr = _lru_scan(a0, b0, a1, b1, False, l)
        token = hook("pre_gmlp", l, h0)
        wout = get_w("w_out", l, h0 if token is None else token)
        x1, mg = _mixer_fwd(x, h0, hr, z, p["gmlp_ln_g"][l][None], p["gmlp_ln_b"][l][None], wsb[l], bsb[l], wout, l, tm)
        wfi = get_w("w_ffn_in", l, x1)
        wfo = get_w("w_ffn_out", l, x1)
        if l == 0:
            x2, ff, dff, h2 = _ffn_fwd(x1, p["norm2_g"][l][None], wfi, wfo, l, tm)
        else:
            dx, loss, dfg, ff, dff, h2 = _ffn_fwd(x1, p["norm2_g"][l][None], wfi, wfo, l, tm,
                                                  head=(p["final_g"][None], tgt))
        saved.append((x, z, h1, a0, a1, h0, hr, x1, mg, ff, dff, h2, win, wout, wfi, wfo, xcb, gates))
        x = x2
    for l in (1, 0):
        x0, z, h1, a0, a1, h0, hr, x1, mg, ff, dff, h2, win, wout, wfi, wfo, xcb, gates = saved[l]
        dgu, dx1, dg2 = _ffn_bwd(dx, wfo, dff, wfi.reshape(N_DEV, FF_BLK, D), x1, p["norm2_g"][l][None], l, tm)
        d_wfo = _mm_tn(ff, pl.BlockSpec((None, s, FF_BLK), lambda j: (j, 0, 0)), dx, _resident((s, D)),
                       4, (4, FF_BLK, D), pl.BlockSpec((None, FF_BLK, D), lambda j: (j, 0, 0)),
                       f"dw_ffn_out_l{l}", a_is_transposed=False)
        dgu8 = dgu.reshape(N_DEV, s, FF_BLK)
        d_wfi = _mm_tn(dgu8, pl.BlockSpec((None, s, FF_BLK), lambda j: (j, 0, 0)), h2, _resident((s, D)),
                       N_DEV, (N_DEV, FF_BLK, D), pl.BlockSpec((None, FF_BLK, D), lambda j: (j, 0, 0)),
                       f"dw_ffn_in_l{l}", a_is_transposed=False)
        d_wout = _mm_tn(mg, _resident((D, s)), dx1, pl.BlockSpec((s, D // 2), lambda j: (0, j)),
                        2, (D, D), pl.BlockSpec((D, D // 2), lambda j: (0, j)), f"dw_out_l{l}")
        token = hook("ffn_partials", l, dict(w_ffn_out=d_wfo.reshape(N_DEV, D_FF // N_DEV, D), w_ffn_in=d_wfi,
                                             w_out=d_wout.reshape(N_DEV, D // N_DEV, D)))
        pending = hook("mid_backward", l, dx1)
        dz, dh, dws, dbs, dlng, dlnb = _mixer_bwd(dx1, wout, h0, hr, z, p["gmlp_ln_g"][l][None], p["gmlp_ln_b"][l][None],
                                                  wsb[l], wstb[l], bsb[l], l, tm, after=[token])
        g1, g0 = _lru_scan(a1, dh, a0, dh, True, l)
        dxc, dwr, dwi, dbr, dbi, dlam = _lru_gates_bwd(
            xcb, gates, h0, hr, g0, g1, wrb[l], wib[l], p["lru_lambda"][l], l, tm, after=[pending])
        dz, dcw, dcb = _conv_bwd(dz, dxc, z, p["conv_w"][l], l, tm)
        small = dict(lru_w_r=dwr.reshape(-1, HD), lru_w_i=dwi.reshape(-1, HD), gmlp_w_s=dws.reshape(-1, HD),
                     gmlp_ln_g=dlng, gmlp_ln_b=dlnb, gmlp_b_s=dbs, conv_w=dcw, conv_b=dcb, lru_b_r=dbr,
                     lru_b_i=dbi, lru_lambda=dlam, norm2_g=dg2)
        if l == 1:
            small["final_g"] = dfg
            small["loss"] = jnp.broadcast_to(loss, (LANE_ROWS, HD))
        else:
            small["norm1_g"] = dg1
        started = hook("small_grads", l, small)
        d_win = _mm_tn(h1, _resident((D, s)), dz, pl.BlockSpec((s, IN_BLK), lambda j: (0, j)),
                       N_DEV, (N_DEV, D, IN_BLK), pl.BlockSpec((None, D, IN_BLK), lambda j: (j, 0, 0)),
                       f"dw_in_l{l}", after=started, jobs=hook("dw_in", l, started) or ())
        token = hook("mixer_partials", l, dict(w_in=d_win))
        dx, dg1 = _mm_nt_rms_bwd(
            dz, pl.BlockSpec((tm, N_IN), lambda i: (i, 0)),
            lambda r: [r[:, k * IN_BLK:(k + 1) * IN_BLK] for k in range(N_DEV)],
            win, False, x0, p["norm1_g"][l][None], dx1, f"inproj_bwd_dx_l{l}", tm,
            after=[token], jobs=hook("inproj_bwd_dx", l, token) or ())
    return loss, dx, dg1


_REPL = ["norm1_g", "gmlp_ln_g", "gmlp_ln_b", "gmlp_w_s", "gmlp_b_s", "conv_b", "lru_w_r", "lru_w_i", "norm2_g", "final_g"]
_LANE_SHARDED = ["conv_w", "lru_b_r", "lru_b_i", "lru_lambda"]
_BIG = ["w_in", "w_out", "w_ffn_in", "w_ffn_out"]
_ORDER = ["norm1_g", "w_in", "gmlp_ln_g", "gmlp_ln_b", "gmlp_w_s", "gmlp_b_s", "conv_w", "conv_b", "lru_w_r", "lru_b_r",
          "lru_w_i", "lru_b_i", "lru_lambda", "w_out", "norm2_g", "w_ffn_in", "w_ffn_out", "final_g"]


def kernel(x, norm1_g, w_in, gmlp_ln_g, gmlp_ln_b, gmlp_w_s, gmlp_b_s, conv_w, conv_b, lru_w_r, lru_b_r, lru_w_i, lru_b_i, lru_lambda, w_out, norm2_g, w_ffn_in, w_ffn_out, final_g, loss_target, m_norm1_g, m_w_in, m_gmlp_ln_g, m_gmlp_ln_b, m_gmlp_w_s, m_gmlp_b_s, m_conv_w, m_conv_b, m_lru_w_r, m_lru_b_r, m_lru_w_i, m_lru_b_i, m_lru_lambda, m_w_out, m_norm2_g, m_w_ffn_in, m_w_ffn_out, m_final_g, v_norm1_g, v_w_in, v_gmlp_ln_g, v_gmlp_ln_b, v_gmlp_w_s, v_gmlp_b_s, v_conv_w, v_conv_b, v_lru_w_r, v_lru_b_r, v_lru_w_i, v_lru_b_i, v_lru_lambda, v_w_out, v_norm2_g, v_w_ffn_in, v_w_ffn_out, v_final_g):
    w = dict(norm1_g=norm1_g, w_in=w_in, gmlp_ln_g=gmlp_ln_g, gmlp_ln_b=gmlp_ln_b, gmlp_w_s=gmlp_w_s, gmlp_b_s=gmlp_b_s,
             conv_w=conv_w, conv_b=conv_b, lru_w_r=lru_w_r, lru_b_r=lru_b_r, lru_w_i=lru_w_i, lru_b_i=lru_b_i,
             lru_lambda=lru_lambda, w_out=w_out, norm2_g=norm2_g, w_ffn_in=w_ffn_in, w_ffn_out=w_ffn_out, final_g=final_g)
    mom = dict(norm1_g=m_norm1_g, w_in=m_w_in, gmlp_ln_g=m_gmlp_ln_g, gmlp_ln_b=m_gmlp_ln_b, gmlp_w_s=m_gmlp_w_s,
               gmlp_b_s=m_gmlp_b_s, conv_w=m_conv_w, conv_b=m_conv_b, lru_w_r=m_lru_w_r, lru_b_r=m_lru_b_r,
               lru_w_i=m_lru_w_i, lru_b_i=m_lru_b_i, lru_lambda=m_lru_lambda, w_out=m_w_out, norm2_g=m_norm2_g,
               w_ffn_in=m_w_ffn_in, w_ffn_out=m_w_ffn_out, final_g=m_final_g)
    var = dict(norm1_g=v_norm1_g, w_in=v_w_in, gmlp_ln_g=v_gmlp_ln_g, gmlp_ln_b=v_gmlp_ln_b, gmlp_w_s=v_gmlp_w_s,
               gmlp_b_s=v_gmlp_b_s, conv_w=v_conv_w, conv_b=v_conv_b, lru_w_r=v_lru_w_r, lru_b_r=v_lru_b_r,
               lru_w_i=v_lru_w_i, lru_b_i=v_lru_b_i, lru_lambda=v_lru_lambda, w_out=v_w_out, norm2_g=v_norm2_g,
               w_ffn_in=v_w_ffn_in, w_ffn_out=v_w_ffn_out, final_g=v_final_g)
    for src in (w, mom, var):
        src["w_ffn_in"] = jnp.swapaxes(src["w_ffn_in"], 1, 2)
    xi, yi, ci = _me()
    me = 4 * xi + 2 * yi + ci

    lane_shapes = [w[k].shape for k in _LANE_SHARDED]
    lane_rows = sum(a[0] * a[1] for a in lane_shapes)
    packed = jnp.concatenate([w[k].reshape(-1, HD) for k in _LANE_SHARDED])
    packed = jnp.pad(packed, ((0, -lane_rows % 8), (0, 0)))

    me1 = jnp.reshape(me, (1,)).astype(jnp.int32)
    gathers = {}
    exchanges = {}
    views = dict(w_in=(N_DEV, D, IN_BLK), w_out=(D, D), w_ffn_in=(2, 4, FF_BLK, D), w_ffn_out=(4, FF_BLK, D))
    small_ex = {}
    small_ag = {}

    casts = {}

    def start_gather(names, l, after):
        lands = [casts[(k, l)] if (k, l) in casts else _cast_into_slot(w[k], l, me1, f"cast_{k}_l{l}") for k in names]
        started, tok = _gather2_start(lands, after, f"gather_start_{'_'.join(names)}_l{l}")
        gathers.update({(k, l): h for k, h in zip(names, started)})
        return tok

    def relay_gather(names, l, after):
        relayed, tok = _gather2_relay([gathers[(k, l)] for k in names], after, f"gather_relay_{'_'.join(names)}_l{l}")
        gathers.update({(k, l): h for k, h in zip(names, relayed)})
        return tok

    def get_w(k, l, after):
        return _gather2_wait(gathers[(k, l)], after, f"gather_wait_{k}_l{l}").reshape(views[k])

    carried = {("inproj_bwd_dx", 1): [("w_ffn_out", 1), ("w_ffn_in", 1), ("w_out", 1)], ("dw_in", 0): [("w_in", 1)],
               ("inproj_bwd_dx", 0): [("w_ffn_out", 0), ("w_ffn_in", 0), ("w_out", 0)]}
    adam = {}

    def adam_jobs(shards, after):
        for k, l in shards:
            got = _exchange_wait(exchanges[(k, l)], after, f"exchange_wait_{k}_l{l}")
            adam[k] = _AdamJob(got, w[k], mom[k], var[k], l, adam[k].results if k in adam else None)
        return [adam[k] for k, _ in shards]

    def hook(stage, l, payload):
        if stage in ("dw_in", "inproj_bwd_dx"):
            return adam_jobs(carried.get((stage, l), []), payload)
        if stage == "pre_inproj":
            return start_gather(_BIG[1:], l, payload) if l == 1 else None
        if stage == "pre_gmlp":
            tok = relay_gather(_BIG[1:], l, payload)
            return relay_gather(_BIG[:1], l + 1, tok) if l == 0 else tok
        if stage == "small_grads":
            (small_ex[l],), tok = _exchange_start([_pack_small(payload)], f"exchange_start_small_l{l}")
            return tok
        if stage == "mid_backward":
            return reduce_small(l + 1, payload) if l == 0 else None
        if (stage, l) == ("mixer_partials", 0):
            pairs, tok = _pairs_start(payload["w_in"], "pairs_start_w_in_l0")
            p, pair = _pairs_wait(pairs, reduce_small(0, tok), "pairs_wait_w_in_l0")
            sums = _sum_pairs(p, pair, me1, "sum_pairs_w_in_l0")
            exchanges[("w_in", 0)], tok = _chips_start(sums, "chips_start_w_in_l0")
            return tok
        started, tok = _exchange_start(list(payload.values()), f"exchange_start_{'_'.join(payload)}_l{l}")
        exchanges.update({(k, l): h for k, h in zip(payload, started)})
        return tok

    def reduce_small(l, after):
        got = _exchange_wait(small_ex[l], after, f"exchange_wait_small_l{l}")
        mine = _sum8_into_slot(got, me1, f"sum_small_l{l}")
        (small_ag[l],), tok = _gather_start([mine], got, f"gather_start_small_l{l}")
        return tok

    land = lax.dynamic_update_slice(jnp.zeros((N_DEV,) + packed.shape, F32), packed[None], (me, 0, 0))
    token = start_gather(_BIG[:1], 0, x)
    (lanes_handle,), token = _gather_start([land], token, "gather_start_lanes")
    later = [(k, l) for l in range(2) for k in _BIG if (k, l) != ("w_in", 0)]
    casts.update(zip(later, _cast_all_into_slots([w[k] for k, _ in later], [l for _, l in later], me1, token,
                                                 "cast_later_weights")))
    token = start_gather(_BIG[:1], 1, start_gather(_BIG[1:], 0, casts[later[0]]))
    token = relay_gather(_BIG[:1], 0, token)
    lanes = _gather_wait(lanes_handle, token, "gather_wait_lanes")
    params = {k: w[k] for k in _REPL}
    off = 0
    for k, shp in zip(_LANE_SHARDED, lane_shapes):
        n = shp[0] * shp[1]
        params[k] = jnp.swapaxes(lanes[:, off:off + n], 0, 1).reshape(shp[0], shp[1], D)
        off += n
    _, dx, dg1 = _local_step(x[0], loss_target[0], params, get_w, hook)

    out = {k: job.results for k, job in adam.items()}
    after = dx
    g_small = [_gather_wait(small_ag[l], after, f"gather_wait_small_l{l}").reshape(SMALL_ROWS, HD) for l in (0, 1)]
    row0 = 0
    for k, rows in SMALL_MATRICES:
        res = _adam_matrix(*g_small, *[src[k].reshape(2, rows, HD) for src in (w, mom, var)], row0, f"adam_{k}")
        out[k] = [a.reshape(w[k].shape) for a in res]
        after = res[3]
        row0 += rows
    got = _chips_wait(exchanges[("w_in", 0)], after, "chips_wait_w_in_l0")
    out["w_in"] = _adam_shard(got, w["w_in"], mom["w_in"], var["w_in"], 0, out["w_in"], "adam_w_in_l0")
    out["w_ffn_in"] = [jnp.swapaxes(a, 1, 2) for a in out["w_ffn_in"]]
    as_rows = lambda a: a.reshape(1, D) if a.ndim == 1 else a
    vec = _adam_vectors(*g_small, _all_gather(dg1, out["w_in"][3], "gather_norm1_grad"), me1,
                        *[{k: as_rows(src[k]) for k, _ in SMALL_VECTORS} for src in (w, mom, var)])
    out.update({k: [a.reshape(w[k].shape) for a in res] for k, res in vec.items()})

    return (g_small[1][LOSS_ROW, 0], dx[None], *[out[k][0] for k in _ORDER], *[out[k][1] for k in _ORDER],
            *[out[k][2] for k in _ORDER], *[out[k][3] for k in _ORDER])
```
